```python
import jax, jax.numpy as jnp
from jax import lax
import numpy as np

D_MODEL = 1024
BATCH = 8
SEQ = 4096
DEPTH = 4

N_Q_A = 8
N_KV_A = 2
HEAD_DIM_A = 64
W_A = N_Q_A * HEAD_DIM_A
W_KV_A = N_KV_A * HEAD_DIM_A
WINDOW = 128
BLOCK = 128
N_HEADS_B = 4
HEAD_DIM_B = 128
W_B = N_HEADS_B * HEAD_DIM_B
N_HEADS_C = 4
DK_C = 128
DV_C = 256
WK_C = N_HEADS_C * DK_C
WV_C = N_HEADS_C * DV_C
GATE_RANK = 16
GATE_TEMP = 16.0
N_MEM = 256
N_HEADS_M = 4
HEAD_DIM_M = 128
W_M = N_HEADS_M * HEAD_DIM_M

CHUNK = 16
EPS = 1e-6
MASK_VALUE = -1e30
MIN_GATE = 1e-30
N_EVEN = (DEPTH + 1) // 2
N_ODD = DEPTH // 2
EVEN_SIZES = (W_A, W_KV_A, W_KV_A, W_A, W_B, W_B, W_B, W_B, W_B, W_M, W_M)
ODD_SIZES = (WK_C, WK_C, WV_C, WV_C, GATE_RANK, GATE_RANK, W_M, W_M)
EVEN_IN = sum(EVEN_SIZES)
ODD_IN = sum(ODD_SIZES)
MIX_EVEN = W_A + W_B + W_M
MIX_ODD = WV_C + W_M

kernel_name = "hybrid_bidir_swa_hgrn2_gla_mem"


def rmsnorm(x, g):
    xf = x.astype(jnp.float32)
    y = xf * lax.rsqrt(jnp.mean(xf * xf, axis=-1, keepdims=True) + EPS)
    return (y * g.astype(jnp.float32)).astype(x.dtype)


def split_cols(t, sizes):
    return jnp.split(t, [int(s) for s in np.cumsum(sizes)[:-1]], axis=-1)


def split_heads(t, n_heads):
    B, T, W = t.shape
    return t.reshape(B, T, n_heads, W // n_heads).transpose(0, 2, 1, 3)


def merge_heads(t):
    B, H, T, d = t.shape
    return t.transpose(0, 2, 1, 3).reshape(B, T, H * d)


def group_rmsnorm(o, g, n_heads):
    B, T, W = o.shape
    y = rmsnorm(o.reshape(B, T, n_heads, W // n_heads), g.reshape(n_heads, W // n_heads))
    return y.reshape(B, T, W)


def alibi_slopes(n):
    return 2.0 ** (-8.0 * jnp.arange(1, n + 1, dtype=jnp.float32) / n)


def window_attention(q, k, v, sink):
    f32 = jnp.float32
    B, Hq, T, d = q.shape
    Hkv = k.shape[1]
    G = Hq // Hkv
    nb = T // BLOCK

    def key_blocks(t):
        tp = jnp.pad(t.astype(f32), ((0, 0), (0, 0), (BLOCK, BLOCK), (0, 0)))
        tp = tp.reshape(B, Hkv, nb + 2, BLOCK, d)
        return jnp.concatenate([tp[:, :, :-2], tp[:, :, 1:-1], tp[:, :, 2:]], axis=3)

    kb, vb = key_blocks(k), key_blocks(v)
    qb = q.astype(f32).reshape(B, Hkv, G, nb, BLOCK, d)
    s = jnp.einsum('bngcid,bncjd->bngcij', qb, kb) * (d ** -0.5)
    i = jnp.arange(BLOCK)[:, None]
    j = jnp.arange(3 * BLOCK)[None, :]
    dist = jnp.abs(i - j + BLOCK).astype(f32)
    kpos = (jnp.arange(nb)[:, None, None] - 1) * BLOCK + j[None]
    valid = (dist <= WINDOW)[None] & (kpos >= 0) & (kpos < T)
    slopes = alibi_slopes(Hq).reshape(Hkv, G, 1, 1, 1)
    s = jnp.where(valid, s - slopes * dist, MASK_VALUE)
    sk = sink.astype(f32).reshape(Hkv, G, 1, 1, 1)
    m = jnp.maximum(jnp.max(s, axis=-1, keepdims=True), sk)
    p = jnp.where(valid, jnp.exp(s - m), 0.0)
    denom = jnp.sum(p, axis=-1, keepdims=True) + jnp.exp(sk - m)
    o = jnp.einsum('bngcij,bncjd->bngcid', p, vb) / denom
    return o.reshape(B, Hq, T, d)


def chunked_gated_scan(q, k, v, log_f):
    f32 = jnp.float32
    B, H, T, dk = q.shape
    dv = v.shape[-1]
    n = T // CHUNK

    def to_chunks(t):
        return t.astype(f32).reshape(B, H, n, CHUNK, t.shape[-1]).transpose(2, 0, 1, 3, 4)

    qc, kc, vc, gc = to_chunks(q), to_chunks(k), to_chunks(v), to_chunks(log_f)
    lower = jnp.tril(jnp.ones((CHUNK, CHUNK), dtype=bool))[:, :, None]

    def step(S, inp):
        qi, ki, vi, gi = inp
        b = jnp.cumsum(gi, axis=-2)
        b_last = b[:, :, -1:, :]
        o_inter = jnp.einsum('bhtk,bhkv->bhtv', qi * jnp.exp(b), S)
        diff = b[:, :, :, None, :] - b[:, :, None, :, :]
        decay = jnp.where(lower, jnp.exp(jnp.where(lower, diff, 0.0)), 0.0)
        A = jnp.einsum('bhtk,bhtsk,bhsk->bhts', qi, decay, ki)
        o_intra = jnp.einsum('bhts,bhsv->bhtv', A, vi)
        S_new = jnp.exp(b_last)[:, :, 0, :, None] * S + jnp.einsum(
            'bhsk,bhsv->bhkv', ki * jnp.exp(b_last - b), vi)
        return S_new, o_inter + o_intra

    S0 = jnp.zeros((B, H, dk, dv), f32)
    _, oc = lax.scan(step, S0, (qc, kc, vc, gc))
    return oc.transpose(1, 2, 0, 3, 4).reshape(B, H, T, dv)


def bidir_scan(q, k_fwd, k_bwd, v, lf_fwd, lf_bwd):
    flip = lambda t: jnp.flip(t, axis=2)
    fwd = chunked_gated_scan(q, k_fwd, v, lf_fwd)
    bwd = flip(chunked_gated_scan(flip(q), flip(k_bwd), flip(v), flip(lf_bwd)))
    return fwd + bwd


def hgrn_forget(z, lb):
    zf = z.astype(jnp.float32)
    f = lb + (1.0 - lb) * jax.nn.sigmoid(zf)
    log_f = jnp.log(jnp.maximum(f, MIN_GATE))
    k = (1.0 - lb) * jax.nn.sigmoid(-zf)
    return log_f, k


def memory_attention(q, mem_n, w_kv):
    f32 = jnp.float32
    k, v = jnp.split(mem_n @ w_kv, 2, axis=-1)
    qh = split_heads(q, N_HEADS_M).astype(f32)
    kh = split_heads(k, N_HEADS_M).astype(f32)
    vh = split_heads(v, N_HEADS_M).astype(f32)
    p = jax.nn.softmax(jnp.einsum('bhtd,bhsd->bhts', qh, kh) * (HEAD_DIM_M ** -0.5), axis=-1)
    return merge_heads(jnp.einsum('bhts,bhsd->bhtd', p, vh)).astype(q.dtype)


def even_layer(x, g_norm, w_in, sink, lb, hgrn_g, w_out, mem_n, w_kv):
    h = rmsnorm(x, g_norm)
    qA, kA, vA, gA, qB, zBf, zBb, iB, gB, qM, gM = split_cols(h @ w_in, EVEN_SIZES)
    a = window_attention(split_heads(qA, N_Q_A), split_heads(kA, N_KV_A),
                         split_heads(vA, N_KV_A), sink)
    a = merge_heads(a).astype(x.dtype) * jax.nn.silu(gA)
    lf_f, k_f = hgrn_forget(zBf, lb[0])
    lf_b, k_b = hgrn_forget(zBb, lb[1])
    sh = lambda t: split_heads(t, N_HEADS_B)
    o = bidir_scan(sh(jax.nn.silu(qB)), sh(k_f), sh(k_b), sh(iB), sh(lf_f), sh(lf_b))
    o = group_rmsnorm(merge_heads(o).astype(x.dtype), hgrn_g, N_HEADS_B) * jax.nn.silu(gB)
    mo = memory_attention(qM, mem_n, w_kv) * jax.nn.silu(gM)
    return jnp.concatenate([a, o, mo], axis=-1) @ w_out


def odd_layer(x, g_norm, w_in, w_up, b_gate, gla_g, w_out, mem_n, w_kv):
    h = rmsnorm(x, g_norm)
    qC, kC, vC, gC, rf, rb, qM, gM = split_cols(h @ w_in, ODD_SIZES)
    lf_f = jax.nn.log_sigmoid((rf @ w_up[0] + b_gate[0]).astype(jnp.float32)) / GATE_TEMP
    lf_b = jax.nn.log_sigmoid((rb @ w_up[1] + b_gate[1]).astype(jnp.float32)) / GATE_TEMP
    sh = lambda t: split_heads(t, N_HEADS_C)
    kh = sh(kC)
    o = bidir_scan(sh(qC * (DK_C ** -0.5)), kh, kh, sh(vC), sh(lf_f), sh(lf_b))
    o = group_rmsnorm(merge_heads(o).astype(x.dtype), gla_g, N_HEADS_C) * jax.nn.silu(gC)
    mo = memory_attention(qM, mem_n, w_kv) * jax.nn.silu(gM)
    return jnp.concatenate([o, mo], axis=-1) @ w_out


def _fwd_setup_inputs(seed: int = 0) -> dict:
    key = jax.random.key(seed)
    ks = jax.random.split(key, 18)
    nrm = lambda k, shape, scale: jax.random.normal(k, shape, jnp.float32) * scale
    return {
        "x": nrm(ks[0], (BATCH, SEQ, D_MODEL), 1.0),
        "mem": nrm(ks[1], (BATCH, N_MEM, D_MODEL), 1.0),
        "norm_even": 1.0 + nrm(ks[2], (N_EVEN, D_MODEL), 0.02),
        "w_in_even": nrm(ks[3], (N_EVEN, D_MODEL, EVEN_IN), D_MODEL ** -0.5),
        "sink": nrm(ks[4], (N_EVEN, N_Q_A), 0.5),
        "lb_param": nrm(ks[5], (N_EVEN, 2, W_B), 0.5),
        "hgrn_norm": 1.0 + nrm(ks[6], (N_EVEN, W_B), 0.02),
        "w_out_even": nrm(ks[7], (N_EVEN, MIX_EVEN, D_MODEL), MIX_EVEN ** -0.5),
        "norm_odd": 1.0 + nrm(ks[8], (N_ODD, D_MODEL), 0.02),
        "w_in_odd": nrm(ks[9], (N_ODD, D_MODEL, ODD_IN), D_MODEL ** -0.5),
        "w_gate_up": nrm(ks[10], (N_ODD, 2, GATE_RANK, WK_C), GATE_RANK ** -0.5),
        "b_gate": nrm(ks[11], (N_ODD, 2, WK_C), 0.1),
        "gla_norm": 1.0 + nrm(ks[12], (N_ODD, WV_C), 0.02),
        "w_out_odd": nrm(ks[13], (N_ODD, MIX_ODD, D_MODEL), MIX_ODD ** -0.5),
        "mem_norm": 1.0 + nrm(ks[14], (D_MODEL,), 0.02),
        "w_mem_kv": nrm(ks[15], (DEPTH, D_MODEL, 2 * W_M), D_MODEL ** -0.5),
        "final_norm": 1.0 + nrm(ks[16], (D_MODEL,), 0.02),
    }


def _fwd_reference(x, mem, norm_even, w_in_even, sink, lb_param, hgrn_norm, w_out_even,
              norm_odd, w_in_odd, w_gate_up, b_gate, gla_norm, w_out_odd,
              mem_norm, w_mem_kv, final_norm):
    mem_n = rmsnorm(mem, mem_norm)
    lbs = jax.nn.softmax(lb_param.astype(jnp.float32), axis=0)
    lower = jnp.cumsum(lbs, axis=0) - lbs[0]
    for l in range(DEPTH):
        i = l // 2
        if l % 2 == 0:
            x = x + even_layer(x, norm_even[i], w_in_even[i], sink[i], lower[i],
                               hgrn_norm[i], w_out_even[i], mem_n, w_mem_kv[l])
        else:
            x = x + odd_layer(x, norm_odd[i], w_in_odd[i], w_gate_up[i], b_gate[i],
                              gla_norm[i], w_out_odd[i], mem_n, w_mem_kv[l])
    return rmsnorm(x, final_norm)


import jax as _jax
import jax.numpy as _jnp

TWIN_FORMAT = 'train_step'
FWD_PARAMS = ['x', 'mem', 'norm_even', 'w_in_even', 'sink', 'lb_param', 'hgrn_norm', 'w_out_even', 'norm_odd', 'w_in_odd', 'w_gate_up', 'b_gate', 'gla_norm', 'w_out_odd', 'mem_norm', 'w_mem_kv', 'final_norm']
TWIN_WEIGHTS = ['norm_even', 'w_in_even', 'sink', 'lb_param', 'hgrn_norm', 'w_out_even', 'norm_odd', 'w_in_odd', 'w_gate_up', 'b_gate', 'gla_norm', 'w_out_odd', 'mem_norm', 'w_mem_kv', 'final_norm']
TWIN_DIFF_INPUT = 'x'
TWIN_INPUTS = ['x', 'mem', 'norm_even', 'w_in_even', 'sink', 'lb_param', 'hgrn_norm', 'w_out_even', 'norm_odd', 'w_in_odd', 'w_gate_up', 'b_gate', 'gla_norm', 'w_out_odd', 'mem_norm', 'w_mem_kv', 'final_norm', 'loss_target', 'm_norm_even', 'm_w_in_even', 'm_sink', 'm_lb_param', 'm_hgrn_norm', 'm_w_out_even', 'm_norm_odd', 'm_w_in_odd', 'm_w_gate_up', 'm_b_gate', 'm_gla_norm', 'm_w_out_odd', 'm_mem_norm', 'm_w_mem_kv', 'm_final_norm', 'v_norm_even', 'v_w_in_even', 'v_sink', 'v_lb_param', 'v_hgrn_norm', 'v_w_out_even', 'v_norm_odd', 'v_w_in_odd', 'v_w_gate_up', 'v_b_gate', 'v_gla_norm', 'v_w_out_odd', 'v_mem_norm', 'v_w_mem_kv', 'v_final_norm']
TWIN_OUTPUTS = ['loss', 'grad_x', 'grad_norm_even', 'grad_w_in_even', 'grad_sink', 'grad_lb_param', 'grad_hgrn_norm', 'grad_w_out_even', 'grad_norm_odd', 'grad_w_in_odd', 'grad_w_gate_up', 'grad_b_gate', 'grad_gla_norm', 'grad_w_out_odd', 'grad_mem_norm', 'grad_w_mem_kv', 'grad_final_norm', 'delta_norm_even', 'delta_w_in_even', 'delta_sink', 'delta_lb_param', 'delta_hgrn_norm', 'delta_w_out_even', 'delta_norm_odd', 'delta_w_in_odd', 'delta_w_gate_up', 'delta_b_gate', 'delta_gla_norm', 'delta_w_out_odd', 'delta_mem_norm', 'delta_w_mem_kv', 'delta_final_norm', 'new_m_norm_even', 'new_m_w_in_even', 'new_m_sink', 'new_m_lb_param', 'new_m_hgrn_norm', 'new_m_w_out_even', 'new_m_norm_odd', 'new_m_w_in_odd', 'new_m_w_gate_up', 'new_m_b_gate', 'new_m_gla_norm', 'new_m_w_out_odd', 'new_m_mem_norm', 'new_m_w_mem_kv', 'new_m_final_norm', 'new_v_norm_even', 'new_v_w_in_even', 'new_v_sink', 'new_v_lb_param', 'new_v_hgrn_norm', 'new_v_w_out_even', 'new_v_norm_odd', 'new_v_w_in_odd', 'new_v_w_gate_up', 'new_v_b_gate', 'new_v_gla_norm', 'new_v_w_out_odd', 'new_v_mem_norm', 'new_v_w_mem_kv', 'new_v_final_norm']
TWIN_LEAF_KINDS = {'loss': 'loss', 'grad_x': 'grad_x', 'grad_norm_even': 'grad_w', 'grad_w_in_even': 'grad_w', 'grad_sink': 'grad_w', 'grad_lb_param': 'grad_w', 'grad_hgrn_norm': 'grad_w', 'grad_w_out_even': 'grad_w', 'grad_norm_odd': 'grad_w', 'grad_w_in_odd': 'grad_w', 'grad_w_gate_up': 'grad_w', 'grad_b_gate': 'grad_w', 'grad_gla_norm': 'grad_w', 'grad_w_out_odd': 'grad_w', 'grad_mem_norm': 'grad_w', 'grad_w_mem_kv': 'grad_w', 'grad_final_norm': 'grad_w', 'delta_norm_even': 'delta_w', 'delta_w_in_even': 'delta_w', 'delta_sink': 'delta_w', 'delta_lb_param': 'delta_w', 'delta_hgrn_norm': 'delta_w', 'delta_w_out_even': 'delta_w', 'delta_norm_odd': 'delta_w', 'delta_w_in_odd': 'delta_w', 'delta_w_gate_up': 'delta_w', 'delta_b_gate': 'delta_w', 'delta_gla_norm': 'delta_w', 'delta_w_out_odd': 'delta_w', 'delta_mem_norm': 'delta_w', 'delta_w_mem_kv': 'delta_w', 'delta_final_norm': 'delta_w', 'new_m_norm_even': 'new_m', 'new_m_w_in_even': 'new_m', 'new_m_sink': 'new_m', 'new_m_lb_param': 'new_m', 'new_m_hgrn_norm': 'new_m', 'new_m_w_out_even': 'new_m', 'new_m_norm_odd': 'new_m', 'new_m_w_in_odd': 'new_m', 'new_m_w_gate_up': 'new_m', 'new_m_b_gate': 'new_m', 'new_m_gla_norm': 'new_m', 'new_m_w_out_odd': 'new_m', 'new_m_mem_norm': 'new_m', 'new_m_w_mem_kv': 'new_m', 'new_m_final_norm': 'new_m', 'new_v_norm_even': 'new_v', 'new_v_w_in_even': 'new_v', 'new_v_sink': 'new_v', 'new_v_lb_param': 'new_v', 'new_v_hgrn_norm': 'new_v', 'new_v_w_out_even': 'new_v', 'new_v_norm_odd': 'new_v', 'new_v_w_in_odd': 'new_v', 'new_v_w_gate_up': 'new_v', 'new_v_b_gate': 'new_v', 'new_v_gla_norm': 'new_v', 'new_v_w_out_odd': 'new_v', 'new_v_mem_norm': 'new_v', 'new_v_w_mem_kv': 'new_v', 'new_v_final_norm': 'new_v'}


def _forward(args):
    return _fwd_reference(*[args[k] for k in FWD_PARAMS])


def _output_shape():
    def fwd():
        inp = _fwd_setup_inputs(0)
        return _fwd_reference(*[inp[k] for k in FWD_PARAMS])
    out = _jax.eval_shape(fwd)
    return out.shape, out.dtype

N_MICROBATCH = 1
ADAM_LR = 0.001
ADAM_B1 = 0.9
ADAM_B2 = 0.999
ADAM_EPS = 1e-08
ADAM_WD = 0.01
ADAM_STEP = 10
PER_EXAMPLE_BATCH_AXIS = {'x': 0, 'mem': 0, 'loss_target': 0}
SHARED_INPUTS = []
_WEIGHT_DTYPES = {'norm_even': _jnp.float32, 'w_in_even': _jnp.float32, 'sink': _jnp.float32, 'lb_param': _jnp.float32, 'hgrn_norm': _jnp.float32, 'w_out_even': _jnp.float32, 'norm_odd': _jnp.float32, 'w_in_odd': _jnp.float32, 'w_gate_up': _jnp.float32, 'b_gate': _jnp.float32, 'gla_norm': _jnp.float32, 'w_out_odd': _jnp.float32, 'mem_norm': _jnp.float32, 'w_mem_kv': _jnp.float32, 'final_norm': _jnp.float32}
MOMENT_SCALE = {'norm_even': 1.137913e-01, 'w_in_even': 5.295531e-02, 'sink': 2.534047e-02, 'lb_param': 4.245684e-03, 'hgrn_norm': 1.097983e-01, 'w_out_even': 7.880425e-02, 'norm_odd': 1.660133e-01, 'w_in_odd': 8.165957e-02, 'w_gate_up': 8.953775e-03, 'b_gate': 3.794371e-02, 'gla_norm': 7.668914e-02, 'w_out_odd': 8.033089e-02, 'mem_norm': 2.259224e-02, 'w_mem_kv': 1.002105e-02, 'final_norm': 3.202486e+01}


def _to_microbatches(a, axis):
    t = _jnp.moveaxis(a, axis, 0)
    t = t.reshape((N_MICROBATCH, t.shape[0] // N_MICROBATCH) + t.shape[1:])
    return _jnp.moveaxis(t, 1, axis + 1)


def setup_inputs(seed: int = 0) -> dict:
    inp = _fwd_setup_inputs(seed)
    key = _jax.random.fold_in(_jax.random.key(seed), 7919)
    shape, _ = _output_shape()
    out = dict(inp)
    out["loss_target"] = _jax.random.normal(_jax.random.fold_in(key, 0), shape, _jnp.float32)
    for i, name in enumerate(TWIN_WEIGHTS):
        w = inp[name].astype(_jnp.float32)
        if MOMENT_SCALE is None:
            s = _jnp.sqrt(_jnp.mean(_jnp.square(w)) + 1e-30)
        else:
            s = MOMENT_SCALE[name]
        km, kv = _jax.random.split(_jax.random.fold_in(key, i + 1))
        out[name] = w
        out["m_" + name] = s * _jax.random.normal(km, w.shape, _jnp.float32)
        out["v_" + name] = (s * s) * _jax.random.uniform(kv, w.shape, _jnp.float32, 0.5, 1.5)
    if N_MICROBATCH > 1:
        for name, axis in PER_EXAMPLE_BATCH_AXIS.items():
            out[name] = _to_microbatches(out[name], axis)
    return {'x': out['x'], 'mem': out['mem'], 'norm_even': out['norm_even'], 'w_in_even': out['w_in_even'], 'sink': out['sink'], 'lb_param': out['lb_param'], 'hgrn_norm': out['hgrn_norm'], 'w_out_even': out['w_out_even'], 'norm_odd': out['norm_odd'], 'w_in_odd': out['w_in_odd'], 'w_gate_up': out['w_gate_up'], 'b_gate': out['b_gate'], 'gla_norm': out['gla_norm'], 'w_out_odd': out['w_out_odd'], 'mem_norm': out['mem_norm'], 'w_mem_kv': out['w_mem_kv'], 'final_norm': out['final_norm'], 'loss_target': out['loss_target'], 'm_norm_even': out['m_norm_even'], 'm_w_in_even': out['m_w_in_even'], 'm_sink': out['m_sink'], 'm_lb_param': out['m_lb_param'], 'm_hgrn_norm': out['m_hgrn_norm'], 'm_w_out_even': out['m_w_out_even'], 'm_norm_odd': out['m_norm_odd'], 'm_w_in_odd': out['m_w_in_odd'], 'm_w_gate_up': out['m_w_gate_up'], 'm_b_gate': out['m_b_gate'], 'm_gla_norm': out['m_gla_norm'], 'm_w_out_odd': out['m_w_out_odd'], 'm_mem_norm': out['m_mem_norm'], 'm_w_mem_kv': out['m_w_mem_kv'], 'm_final_norm': out['m_final_norm'], 'v_norm_even': out['v_norm_even'], 'v_w_in_even': out['v_w_in_even'], 'v_sink': out['v_sink'], 'v_lb_param': out['v_lb_param'], 'v_hgrn_norm': out['v_hgrn_norm'], 'v_w_out_even': out['v_w_out_even'], 'v_norm_odd': out['v_norm_odd'], 'v_w_in_odd': out['v_w_in_odd'], 'v_w_gate_up': out['v_w_gate_up'], 'v_b_gate': out['v_b_gate'], 'v_gla_norm': out['v_gla_norm'], 'v_w_out_odd': out['v_w_out_odd'], 'v_mem_norm': out['v_mem_norm'], 'v_w_mem_kv': out['v_w_mem_kv'], 'v_final_norm': out['v_final_norm']}


def _loss(weights, diff, rest, loss_target):
    with _jax.named_scope("forward"):
        args = {**rest, TWIN_DIFF_INPUT: diff, **{k: w.astype(_WEIGHT_DTYPES[k]) for k, w in weights.items()}}
        y = _forward(args)
    with _jax.named_scope("loss_head"):
        err = _jnp.square(y.astype(_jnp.float32) - loss_target)
        return 0.5 * _jnp.sum(_jnp.mean(err, axis=-1)) if err.ndim else 0.5 * err


def _adamw(w, g, m, v):
    m = ADAM_B1 * m + (1.0 - ADAM_B1) * g
    v = ADAM_B2 * v + (1.0 - ADAM_B2) * _jnp.square(g)
    m_hat = m / (1.0 - ADAM_B1 ** ADAM_STEP)
    v_hat = v / (1.0 - ADAM_B2 ** ADAM_STEP)
    delta = -ADAM_LR * (m_hat / (_jnp.sqrt(v_hat) + ADAM_EPS) + ADAM_WD * w)
    return delta, m, v


def reference(x, mem, norm_even, w_in_even, sink, lb_param, hgrn_norm, w_out_even, norm_odd, w_in_odd, w_gate_up, b_gate, gla_norm, w_out_odd, mem_norm, w_mem_kv, final_norm, loss_target, m_norm_even, m_w_in_even, m_sink, m_lb_param, m_hgrn_norm, m_w_out_even, m_norm_odd, m_w_in_odd, m_w_gate_up, m_b_gate, m_gla_norm, m_w_out_odd, m_mem_norm, m_w_mem_kv, m_final_norm, v_norm_even, v_w_in_even, v_sink, v_lb_param, v_hgrn_norm, v_w_out_even, v_norm_odd, v_w_in_odd, v_w_gate_up, v_b_gate, v_gla_norm, v_w_out_odd, v_mem_norm, v_w_mem_kv, v_final_norm):
    given = dict(x=x, mem=mem, norm_even=norm_even, w_in_even=w_in_even, sink=sink, lb_param=lb_param, hgrn_norm=hgrn_norm, w_out_even=w_out_even, norm_odd=norm_odd, w_in_odd=w_in_odd, w_gate_up=w_gate_up, b_gate=b_gate, gla_norm=gla_norm, w_out_odd=w_out_odd, mem_norm=mem_norm, w_mem_kv=w_mem_kv, final_norm=final_norm, loss_target=loss_target, m_norm_even=m_norm_even, m_w_in_even=m_w_in_even, m_sink=m_sink, m_lb_param=m_lb_param, m_hgrn_norm=m_hgrn_norm, m_w_out_even=m_w_out_even, m_norm_odd=m_norm_odd, m_w_in_odd=m_w_in_odd, m_w_gate_up=m_w_gate_up, m_b_gate=m_b_gate, m_gla_norm=m_gla_norm, m_w_out_odd=m_w_out_odd, m_mem_norm=m_mem_norm, m_w_mem_kv=m_w_mem_kv, m_final_norm=m_final_norm, v_norm_even=v_norm_even, v_w_in_even=v_w_in_even, v_sink=v_sink, v_lb_param=v_lb_param, v_hgrn_norm=v_hgrn_norm, v_w_out_even=v_w_out_even, v_norm_odd=v_norm_odd, v_w_in_odd=v_w_in_odd, v_w_gate_up=v_w_gate_up, v_b_gate=v_b_gate, v_gla_norm=v_gla_norm, v_w_out_odd=v_w_out_odd, v_mem_norm=v_mem_norm, v_w_mem_kv=v_w_mem_kv, v_final_norm=v_final_norm)
    weights = {n: given[n] for n in TWIN_WEIGHTS}
    shared = {n: given[n] for n in SHARED_INPUTS}
    per_example = {n: given[n] for n in ['x', 'mem']}
    grad_fn = _jax.value_and_grad(_loss, argnums=(0, 1))

    def one_microbatch(ex, loss_target):
        ex = dict(ex)
        diff = ex.pop(TWIN_DIFF_INPUT)
        return grad_fn(weights, diff, {**shared, **ex}, loss_target)

    if N_MICROBATCH == 1:
        loss, (grad_w, grad_x) = one_microbatch(per_example, given["loss_target"])
    else:
        def body(carry, xs):
            loss_sum, grad_sum = carry
            l_k, (gw_k, gx_k) = one_microbatch(xs[0], xs[1])
            with _jax.named_scope("update"):
                return (loss_sum + l_k, _jax.tree.map(_jnp.add, grad_sum, gw_k)), gx_k

        init = (_jnp.zeros((), _jnp.float32), _jax.tree.map(_jnp.zeros_like, weights))
        (loss, grad_w), grad_x = _jax.lax.scan(body, init, (per_example, given["loss_target"]))
    with _jax.named_scope("update"):
        delta_w, new_m, new_v = {}, {}, {}
        for n in TWIN_WEIGHTS:
            delta_w[n], new_m[n], new_v[n] = _adamw(weights[n], grad_w[n], given["m_" + n], given["v_" + n])
    return (loss, grad_x, *[grad_w[n] for n in TWIN_WEIGHTS], *[delta_w[n] for n in TWIN_WEIGHTS],
            *[new_m[n] for n in TWIN_WEIGHTS], *[new_v[n] for n in TWIN_WEIGHTS])
```

```python
import functools

import numpy as np
import jax
import jax.numpy as jnp
from jax import lax
from jax.experimental import pallas as pl
from jax.experimental.pallas import tpu as pltpu

F32 = jnp.float32
BF16 = jnp.bfloat16

D_MODEL = 1024
DEPTH = 4
N_Q_A, N_KV_A, HEAD_DIM_A = 8, 2, 64
W_A, W_KV_A = 512, 128
WINDOW = 128
BLOCK = 128
N_HEADS_B, HEAD_DIM_B, W_B = 4, 128, 512
N_HEADS_C, DK_C, DV_C, WK_C, WV_C = 4, 128, 256, 512, 1024
GATE_RANK = 16
GATE_TEMP = 16.0
N_MEM, N_HEADS_M, HEAD_DIM_M, W_M = 256, 4, 128, 512
EPS = 1e-6
MASK_VALUE = -1e30
MIN_GATE = 1e-30
EVEN_IN, ODD_IN = 4864, 4128
ODD_PAD = 4224
MIX = 1536
ADAM_LR, ADAM_B1, ADAM_B2, ADAM_EPS, ADAM_WD, ADAM_STEP = 0.001, 0.9, 0.999, 1e-08, 0.01, 10

SCAN_CHUNK = 128
SCAN_LEVELS = 7
VMEM_LIMIT = 56 * 1024 * 1024

EVEN_REF_OFF = dict(qA=0, kA=512, vA=640, gA=768, qB=1280, zf=1792, zb=2304, iB=2816, gB=3328, qM=3840, gM=4352)
EVEN_W = dict(qA=512, kA=128, vA=128, gA=512, qB=512, zf=512, zb=512, iB=512, gB=512, qM=512, gM=512)
EVEN_ORDER = ("qA", "gA", "qB", "zf", "zb", "iB", "gB", "qM", "gM", "kA", "vA")
ODD_REF_OFF = dict(qC=0, kC=512, vC=1024, gC=2048, rr=3072, qM=3104, gM=3616)
ODD_W = dict(qC=512, kC=512, vC=1024, gC=1024, rr=32, qM=512, gM=512)
ODD_ORDER = ("qC", "kC", "vC", "gC", "qM", "gM", "rr")


def _offsets(order, widths):
    off, o = {}, 0
    for n in order:
        off[n] = o
        o += widths[n]
    return off


EVEN_OFF = _offsets(EVEN_ORDER, EVEN_W)
ODD_OFF = _offsets(ODD_ORDER, ODD_W)


def _dg(a, b, ca, cb):
    return lax.dot_general(a.astype(BF16), b.astype(BF16), (((ca,), (cb,)), ((), ())),
                           preferred_element_type=F32)


def dot_nn(a, b):
    return _dg(a, b, 1, 0)


def dot_nt(a, b):
    return _dg(a, b, 1, 1)


def dot_tn(a, b):
    return _dg(a, b, 0, 0)


@jax.custom_vjp
def bdot(a, b):
    return dot_nn(a, b)


bdot.defvjp(lambda a, b: (dot_nn(a, b), (a, b)),
            lambda r, g: (dot_nt(g, r[1]), dot_tn(r[0], g)))


@jax.custom_vjp
def bdot_t(a, b):
    return dot_nt(a, b)


bdot_t.defvjp(lambda a, b: (dot_nt(a, b), (a, b)),
              lambda r, g: (dot_nn(g, r[1]), dot_tn(g, r[0])))


@jax.custom_vjp
def bdot_tn(a, b):
    return dot_tn(a, b)


bdot_tn.defvjp(lambda a, b: (dot_tn(a, b), (a, b)),
               lambda r, g: (dot_nt(r[1], g), dot_nn(r[0], g)))


def _split_mm(h, x):
    hi = x.astype(BF16)
    lo = (x - hi.astype(F32)).astype(BF16)
    return (lax.dot_general(h, hi, (((1,), (0,)), ((), ())), preferred_element_type=F32)
            + lax.dot_general(h, lo, (((1,), (0,)), ((), ())), preferred_element_type=F32))


@jax.custom_vjp
def hdot(h, ht, x):
    return _split_mm(h, x)


hdot.defvjp(lambda h, ht, x: (_split_mm(h, x), (h, ht)),
            lambda r, g: (jnp.zeros_like(r[0]), jnp.zeros_like(r[1]), _split_mm(r[1], g)))


def _sigmoid(z):
    return 1.0 / (1.0 + jnp.exp(-z))


def _silu(z):
    return z * _sigmoid(z)


def _log_sigmoid(z):
    return jnp.minimum(z, 0.0) - jnp.log(1.0 + jnp.exp(-jnp.abs(z)))


def _rms(x, g):
    return x * lax.rsqrt(jnp.mean(x * x, axis=-1, keepdims=True) + EPS) * g


def rms_tile(x, g):
    return (_rms(x, g),)


@functools.partial(jax.custom_vjp, nondiff_argnums=(1, 2))
def split(x, n, axis):
    w = x.shape[axis] // n
    return tuple(lax.slice_in_dim(x, h * w, (h + 1) * w, axis=axis) for h in range(n))


split.defvjp(lambda x, n, axis: (split(x, n, axis), None),
             lambda n, axis, _, cts: (jnp.concatenate(cts, axis=axis),))


def _group_rms(o, g, heads):
    return jnp.concatenate([_rms(oh, gh) for oh, gh in zip(split(o, heads, 1), split(g, heads, 1))], axis=-1)


def even_post_tile(a, o2f, o2b, mo, gA, gB, gM, hg):
    y = _group_rms(o2f + o2b, hg, N_HEADS_B)
    return (jnp.concatenate([a * _silu(gA), y * _silu(gB), mo * _silu(gM)], axis=-1),)


def odd_post_tile(o2f, o2b, mo, gC, gM, gg):
    y = _group_rms(o2f + o2b, gg, N_HEADS_C)
    return (jnp.concatenate([y * _silu(gC), mo * _silu(gM)], axis=-1),)


def hgrn_prep_tile(qB, zf, zb, low_f, low_b):
    ks, gs = [], []
    for z, lb in ((zf, low_f), (zb, low_b)):
        f = lb + (1.0 - lb) * _sigmoid(z)
        gs.append(jnp.log(jnp.maximum(f, MIN_GATE)))
        ks.append((1.0 - lb) * _sigmoid(-z))
    return (_silu(qB), ks[0], ks[1], gs[0], gs[1])


def gla_prep_tile(qC, r128, wup_f, wup_b, bg_f, bg_b):
    gs = [_log_sigmoid(bdot(r128, wup) + bg) / GATE_TEMP for wup, bg in ((wup_f, bg_f), (wup_b, bg_b))]
    return (qC * (DK_C ** -0.5), gs[0], gs[1])


def mem_tile(q, k, v):
    s = bdot_t(q, k) * (HEAD_DIM_M ** -0.5)
    m = lax.stop_gradient(jnp.max(s, axis=-1, keepdims=True))
    p = jnp.exp(s - m)
    p = p / jnp.sum(p, axis=-1, keepdims=True)
    return (bdot(p, v),)


def attn_block(qs, ks, vs, sinks, slopes, c, seq):
    i = lax.broadcasted_iota(jnp.int32, (BLOCK, 3 * BLOCK), 0)
    j = lax.broadcasted_iota(jnp.int32, (BLOCK, 3 * BLOCK), 1)
    dist = jnp.abs(i - j + BLOCK).astype(F32)
    kpos = (c - 1) * BLOCK + j
    valid = (dist <= WINDOW) & (kpos >= 0) & (kpos < seq)
    outs = []
    for q, sk, slope in zip(qs, sinks, slopes):
        s = bdot_t(q, ks) * (HEAD_DIM_A ** -0.5)
        s = jnp.where(valid, s - slope * dist, MASK_VALUE)
        m = lax.stop_gradient(jnp.maximum(jnp.max(s, axis=-1, keepdims=True), sk))
        p = jnp.where(valid, jnp.exp(s - m), 0.0)
        denom = jnp.sum(p, axis=-1, keepdims=True) + jnp.exp(sk - m)
        outs.append(bdot(p, vs) / denom)
    return tuple(outs)


def scan_chunk(q, k, v, g, st, h, ht, qm, km, bm):
    C = SCAN_CHUNK
    e = split(hdot(h, ht, g), 2 + SCAN_LEVELS, 0)
    qe = q * jnp.exp(e[0])
    kd = k * jnp.exp(e[1])
    tot = jnp.sum(g, axis=0, keepdims=True)
    r = lax.broadcasted_iota(jnp.int32, (C, C), 0)
    s = lax.broadcasted_iota(jnp.int32, (C, C), 1)
    a = jnp.where(r == s, jnp.sum(q * k, axis=-1, keepdims=True), 0.0)
    for l in range(SCAN_LEVELS):
        el = jnp.exp(e[2 + l])
        qs = q * el * qm[l * C:(l + 1) * C]
        ks = k * el * km[l * C:(l + 1) * C]
        a = a + bdot_t(qs, ks) * bm[l * C:(l + 1) * C]
    o = bdot_t(qe, st) + bdot(a, v)
    st_new = st * jnp.exp(tot) + bdot_tn(v, kd)
    return o, st_new


def _scan_consts():
    C, L = SCAN_CHUNK, SCAN_LEVELS
    t = np.arange(C)[:, None]
    r = np.arange(C)[None, :]
    blocks = [(r <= t), (r > t)]
    qms, kms, bms = [], [], []
    for l in range(1, L + 1):
        m = C >> l
        upper_t = (t % (2 * m)) >= m
        same_half = (t // m) == (r // m)
        blocks.append(same_half & np.where(upper_t, r <= t, r > t))
        qms.append(np.broadcast_to(upper_t, (C, C)))
        kms.append(np.broadcast_to(~upper_t, (C, C)))
        bms.append((t // (2 * m)) == (r // (2 * m)))
    hf = np.concatenate(blocks, axis=0).astype(np.float32)
    flip = lambda mat: mat.reshape(-1, C, C)[:, ::-1, ::-1].reshape(-1, C)
    hb = flip(hf)
    qmf = np.concatenate(qms, axis=0).astype(np.float32)
    kmf = np.concatenate(kms, axis=0).astype(np.float32)
    bm = np.concatenate(bms, axis=0).astype(np.float32)
    h = np.stack([hf, hb])
    ht = np.stack([hf.T, hb.T])
    qm = np.stack([qmf, kmf])
    km = np.stack([kmf, qmf])
    return h, ht, qm, km, bm


def _cparams(sem):
    return pltpu.CompilerParams(dimension_semantics=sem, vmem_limit_bytes=VMEM_LIMIT)


def _row_tile(T):
    return min(T, 256)


def _in_spec(spec, tr):
    kind = spec[0]
    if kind == "row":
        _, arr, off, w = spec
        assert off % w == 0
        return arr, pl.BlockSpec((tr, w), functools.partial(lambda i, b: (i, b), b=off // w))
    if kind == "row3":
        _, arr, d, off, w = spec
        assert off % w == 0
        return arr, pl.BlockSpec((None, tr, w), functools.partial(lambda i, d, b: (d, i, b), d=d, b=off // w))
    _, arr = spec
    return arr, pl.BlockSpec(arr.shape, functools.partial(lambda i, n: (0,) * n, n=arr.ndim))


def rows_call(name, tile_fn, T, ins, out_widths, out_dtypes=None):
    tr = _row_tile(T)
    n_in = len(ins)
    out_dtypes = out_dtypes or [F32] * len(out_widths)

    def body(*refs):
        vals = [r[...] for r in refs[:n_in]]
        outs = tile_fn(*vals)
        for r, o in zip(refs[n_in:], outs):
            r[...] = o.astype(r.dtype)

    in_specs, args = [], []
    for spec in ins:
        arr, bs = _in_spec(spec, tr)
        args.append(arr)
        in_specs.append(bs)
    out_specs = [pl.BlockSpec((tr, w), lambda i: (i, 0)) for w in out_widths]
    out_shape = [jax.ShapeDtypeStruct((T, w), dt) for w, dt in zip(out_widths, out_dtypes)]
    return pl.pallas_call(body, out_shape=out_shape, grid=(T // tr,), in_specs=in_specs, out_specs=out_specs,
                          name=name, compiler_params=_cparams(("arbitrary",)))(*args)


def rows_vjp_call(name, tile_fn, T, ins, cts, skip=()):
    tr = _row_tile(T)
    n_in = len(ins)
    n_ct = [len(c) for c in cts]
    want = [k for k in range(n_in) if k not in skip]

    def body(*refs):
        i = pl.program_id(0)
        vals = [r[...] for r in refs[:n_in]]
        ct, pos = [], n_in
        for n in n_ct:
            acc = refs[pos][...]
            for r in refs[pos + 1:pos + n]:
                acc = acc + r[...]
            ct.append(acc)
            pos += n
        _, vjp = jax.vjp(tile_fn, *vals)
        grads = vjp(tuple(ct))
        for r, k in zip(refs[pos:], want):
            if ins[k][0] == "full":
                @pl.when(i == 0)
                def _():
                    r[...] = jnp.zeros_like(r)
                r[...] += grads[k]
            else:
                r[...] = grads[k]

    in_specs, args = [], []
    for spec in list(ins) + [s for c in cts for s in c]:
        arr, bs = _in_spec(spec, tr)
        args.append(arr)
        in_specs.append(bs)
    out_specs, out_shape = [], []
    for k in want:
        if ins[k][0] == "full":
            arr = ins[k][1]
            out_specs.append(pl.BlockSpec(arr.shape, functools.partial(lambda i, n: (0,) * n, n=arr.ndim)))
            out_shape.append(jax.ShapeDtypeStruct(arr.shape, F32))
        else:
            w = ins[k][-1]
            out_specs.append(pl.BlockSpec((tr, w), lambda i: (i, 0)))
            out_shape.append(jax.ShapeDtypeStruct((T, w), F32))
    return pl.pallas_call(body, out_shape=out_shape, grid=(T // tr,), in_specs=in_specs, out_specs=out_specs,
                          name=name, compiler_params=_cparams(("arbitrary",)))(*args)


def matmul(name, a, b, mode, add=None, out_dtype=F32):
    if mode == "tn":
        K, M = a.shape
        N = b.shape[1]
        tm = M if M <= 1024 else 512
        tn = N if N <= 1280 else (N // 2 if (N // 2) % 128 == 0 else N)
        tk = min(K, 512)
        grid = (M // tm, N // tn, K // tk)

        def body(a_ref, b_ref, o_ref):
            @pl.when(pl.program_id(2) == 0)
            def _():
                o_ref[...] = jnp.zeros_like(o_ref)
            o_ref[...] += dot_tn(a_ref[...], b_ref[...])

        return pl.pallas_call(
            body, out_shape=jax.ShapeDtypeStruct((M, N), F32), grid=grid,
            in_specs=[pl.BlockSpec((tk, tm), lambda i, j, k: (k, i)), pl.BlockSpec((tk, tn), lambda i, j, k: (k, j))],
            out_specs=pl.BlockSpec((tm, tn), lambda i, j, k: (i, j)), name=name,
            compiler_params=_cparams(("arbitrary", "arbitrary", "arbitrary")))(a, b)

    M, K = a.shape
    N = b.shape[1] if mode == "nn" else b.shape[0]
    tm = min(M, 256)
    tn = N if N <= 1536 else (N // 2 if (N // 2) % 128 == 0 else (N // 3 if (N // 3) % 128 == 0 else N))
    grid = (N // tn, M // tm)
    n_in = 2 + (add is not None)

    def body(*refs):
        a_ref, b_ref = refs[0], refs[1]
        o_ref = refs[n_in]
        acc = dot_nn(a_ref[...], b_ref[...]) if mode == "nn" else dot_nt(a_ref[...], b_ref[...])
        if add is not None:
            acc = acc + refs[2][...]
        o_ref[...] = acc.astype(o_ref.dtype)

    in_specs = [pl.BlockSpec((tm, K), lambda j, i: (i, 0)),
                pl.BlockSpec((K, tn), lambda j, i: (0, j)) if mode == "nn" else pl.BlockSpec((tn, K), lambda j, i: (j, 0))]
    args = [a, b]
    if add is not None:
        in_specs.append(pl.BlockSpec((tm, tn), lambda j, i: (i, j)))
        args.append(add)
    return pl.pallas_call(
        body, out_shape=jax.ShapeDtypeStruct((M, N), out_dtype), grid=grid, in_specs=in_specs,
        out_specs=pl.BlockSpec((tm, tn), lambda j, i: (i, j)), name=name,
        compiler_params=_cparams(("arbitrary", "arbitrary")))(*args)


def attn_fwd(q8, k2p, v2p, sink, slopes, T):
    nb = T // BLOCK

    def body(q_ref, k_ref, v_ref, sink_ref, slope_ref, o_ref):
        c = pl.program_id(1)
        start = pl.multiple_of(c * BLOCK, BLOCK)
        ks = k_ref[pl.ds(start, 3 * BLOCK), :]
        vs = v_ref[pl.ds(start, 3 * BLOCK), :]
        outs = attn_block([q_ref[g] for g in range(4)], ks, vs, [sink_ref[g] for g in range(4)],
                          [slope_ref[g] for g in range(4)], c, T)
        for g in range(4):
            o_ref[g] = outs[g]

    return pl.pallas_call(
        body, out_shape=jax.ShapeDtypeStruct((N_Q_A, T, HEAD_DIM_A), F32), grid=(N_KV_A, nb),
        in_specs=[pl.BlockSpec((4, BLOCK, HEAD_DIM_A), lambda n, c: (n, c, 0)),
                  pl.BlockSpec((None, T + 2 * BLOCK, HEAD_DIM_A), lambda n, c: (n, 0, 0)),
                  pl.BlockSpec((None, T + 2 * BLOCK, HEAD_DIM_A), lambda n, c: (n, 0, 0)),
                  pl.BlockSpec((4, 1, 1), lambda n, c: (n, 0, 0)),
                  pl.BlockSpec((4, 1, 1), lambda n, c: (n, 0, 0))],
        out_specs=pl.BlockSpec((4, BLOCK, HEAD_DIM_A), lambda n, c: (n, c, 0)),
        name="attn_fwd", compiler_params=_cparams(("arbitrary", "arbitrary")))(q8, k2p, v2p, sink, slopes)


def attn_bwd(q8, k2p, v2p, sink, slopes, do8, T):
    nb = T // BLOCK

    def body(q_ref, k_ref, v_ref, sink_ref, slope_ref, do_ref, dq_ref, dk_ref, dv_ref, dsink_ref):
        c = pl.program_id(1)
        start = pl.multiple_of(c * BLOCK, BLOCK)
        ks = k_ref[pl.ds(start, 3 * BLOCK), :]
        vs = v_ref[pl.ds(start, 3 * BLOCK), :]
        slopes = [slope_ref[g] for g in range(4)]
        _, vjp = jax.vjp(lambda qs, kk, vv, sks: attn_block(qs, kk, vv, sks, slopes, c, T),
                         [q_ref[g] for g in range(4)], ks, vs, [sink_ref[g] for g in range(4)])
        dqs, dks, dvs, dsks = vjp(tuple(do_ref[g] for g in range(4)))

        @pl.when(c == 0)
        def _():
            dk_ref[...] = jnp.zeros_like(dk_ref)
            dv_ref[...] = jnp.zeros_like(dv_ref)
            dsink_ref[...] = jnp.zeros_like(dsink_ref)

        dk_ref[pl.ds(start, 3 * BLOCK), :] += dks
        dv_ref[pl.ds(start, 3 * BLOCK), :] += dvs
        for g in range(4):
            dq_ref[g] = dqs[g]
            dsink_ref[g] += dsks[g]

    qspec = pl.BlockSpec((4, BLOCK, HEAD_DIM_A), lambda n, c: (n, c, 0))
    kspec = pl.BlockSpec((None, T + 2 * BLOCK, HEAD_DIM_A), lambda n, c: (n, 0, 0))
    sspec = pl.BlockSpec((4, 1, 1), lambda n, c: (n, 0, 0))
    return pl.pallas_call(
        body,
        out_shape=[jax.ShapeDtypeStruct((N_Q_A, T, HEAD_DIM_A), F32),
                   jax.ShapeDtypeStruct((N_KV_A, T + 2 * BLOCK, HEAD_DIM_A), F32),
                   jax.ShapeDtypeStruct((N_KV_A, T + 2 * BLOCK, HEAD_DIM_A), F32),
                   jax.ShapeDtypeStruct((N_Q_A, 1, 1), F32)],
        grid=(N_KV_A, nb), in_specs=[qspec, kspec, kspec, sspec, sspec, qspec],
        out_specs=[qspec, kspec, kspec, sspec],
        name="attn_bwd", compiler_params=_cparams(("arbitrary", "arbitrary")))(q8, k2p, v2p, sink, slopes, do8)


def mem_fwd(p, q_off, kv, T):
    tr = _row_tile(T)
    qb = q_off // HEAD_DIM_M

    def body(q_ref, k_ref, v_ref, o_ref):
        (o,) = mem_tile(q_ref[...], k_ref[...], v_ref[...])
        o_ref[...] = o

    return pl.pallas_call(
        body, out_shape=jax.ShapeDtypeStruct((T, W_M), F32), grid=(N_HEADS_M, T // tr),
        in_specs=[pl.BlockSpec((tr, HEAD_DIM_M), lambda h, i: (i, qb + h)),
                  pl.BlockSpec((N_MEM, HEAD_DIM_M), lambda h, i: (0, h)),
                  pl.BlockSpec((N_MEM, HEAD_DIM_M), lambda h, i: (0, N_HEADS_M + h))],
        out_specs=pl.BlockSpec((tr, HEAD_DIM_M), lambda h, i: (i, h)),
        name="mem_fwd", compiler_params=_cparams(("arbitrary", "arbitrary")))(p, kv, kv)


def mem_bwd(p, q_off, kv, do, T):
    tr = _row_tile(T)
    qb = q_off // HEAD_DIM_M

    def body(q_ref, k_ref, v_ref, do_ref, dq_ref, dk_ref, dv_ref):
        i = pl.program_id(1)
        _, vjp = jax.vjp(mem_tile, q_ref[...], k_ref[...], v_ref[...])
        dq, dk, dv = vjp((do_ref[...],))
        dq_ref[...] = dq

        @pl.when(i == 0)
        def _():
            dk_ref[...] = jnp.zeros_like(dk_ref)
            dv_ref[...] = jnp.zeros_like(dv_ref)

        dk_ref[...] += dk
        dv_ref[...] += dv

    kspec = pl.BlockSpec((N_MEM, HEAD_DIM_M), lambda h, i: (0, h))
    dq, dk, dv = pl.pallas_call(
        body,
        out_shape=[jax.ShapeDtypeStruct((T, W_M), F32), jax.ShapeDtypeStruct((N_MEM, W_M), F32),
                   jax.ShapeDtypeStruct((N_MEM, W_M), F32)],
        grid=(N_HEADS_M, T // tr),
        in_specs=[pl.BlockSpec((tr, HEAD_DIM_M), lambda h, i: (i, qb + h)), kspec,
                  pl.BlockSpec((N_MEM, HEAD_DIM_M), lambda h, i: (0, N_HEADS_M + h)),
                  pl.BlockSpec((tr, HEAD_DIM_M), lambda h, i: (i, h))],
        out_specs=[pl.BlockSpec((tr, HEAD_DIM_M), lambda h, i: (i, h)), kspec, kspec],
        name="mem_bwd", compiler_params=_cparams(("arbitrary", "arbitrary")))(p, kv, kv, do)
    return dq, jnp.concatenate([dk, dv], axis=-1)


def _scan_const_specs(dk):
    C, L = SCAN_CHUNK, SCAN_LEVELS
    return [pl.BlockSpec((None, (2 + L) * C, C), lambda d, h, n: (d, 0, 0)),
            pl.BlockSpec((None, C, (2 + L) * C), lambda d, h, n: (d, 0, 0)),
            pl.BlockSpec((None, L * C, dk), lambda d, h, n: (d, 0, 0)),
            pl.BlockSpec((None, L * C, dk), lambda d, h, n: (d, 0, 0)),
            pl.BlockSpec((L * C, C), lambda d, h, n: (0, 0))]


def _scan_const_args():
    h, ht, qm, km, bm = _scan_consts()
    return [jnp.asarray(h, BF16), jnp.asarray(ht, BF16), jnp.asarray(qm, F32), jnp.asarray(km, F32), jnp.asarray(bm, F32)]


def scan_fwd(name, q, q_off, k2, g2, v, v_off, heads, dk, dv, T):
    C = SCAN_CHUNK
    N = T // C
    kd = k2.shape[0]
    assert dk == C
    qb, vb = q_off // dk, v_off // dv

    def chunk(d, n):
        return n + d * (N - 1 - 2 * n)

    def body(q_ref, k_ref, g_ref, v_ref, h_ref, ht_ref, qm_ref, km_ref, bm_ref, o_ref, ss_ref, st_ref):
        n = pl.program_id(2)

        @pl.when(n == 0)
        def _():
            st_ref[...] = jnp.zeros_like(st_ref)

        st = st_ref[...]
        ss_ref[...] = st
        o, st_new = scan_chunk(q_ref[...], k_ref[...], v_ref[...], g_ref[...], st, h_ref[...], ht_ref[...],
                               qm_ref[...], km_ref[...], bm_ref[...])
        o_ref[...] = o
        st_ref[...] = st_new

    return pl.pallas_call(
        body,
        out_shape=[jax.ShapeDtypeStruct((2, T, heads * dv), F32), jax.ShapeDtypeStruct((2, heads, N, dv, dk), F32)],
        grid=(2, heads, N),
        in_specs=[pl.BlockSpec((C, dk), lambda d, h, n: (chunk(d, n), qb + h)),
                  pl.BlockSpec((None, C, dk), lambda d, h, n: (d * (kd - 1), chunk(d, n), h)),
                  pl.BlockSpec((None, C, dk), lambda d, h, n: (d, chunk(d, n), h)),
                  pl.BlockSpec((C, dv), lambda d, h, n: (chunk(d, n), vb + h))] + _scan_const_specs(dk),
        out_specs=[pl.BlockSpec((None, C, dv), lambda d, h, n: (d, chunk(d, n), h)),
                   pl.BlockSpec((None, None, None, dv, dk), lambda d, h, n: (d, h, chunk(d, n), 0, 0))],
        scratch_shapes=[pltpu.VMEM((dv, dk), F32)],
        name=name, compiler_params=_cparams(("arbitrary", "arbitrary", "arbitrary")))(q, k2, g2, v, *_scan_const_args())


def scan_bwd(name, q, q_off, k2, g2, v, v_off, ss, do, heads, dk, dv, T):
    C = SCAN_CHUNK
    N = T // C
    kd = k2.shape[0]
    qb, vb = q_off // dk, v_off // dv

    def chunk(d, n):
        return (N - 1 - n) + d * (2 * n - N + 1)

    def body(q_ref, k_ref, g_ref, v_ref, ss_ref, do_ref, h_ref, ht_ref, qm_ref, km_ref, bm_ref,
             dq_ref, dk_ref, dg_ref, dv_ref, dst_ref):
        n = pl.program_id(2)

        @pl.when(n == 0)
        def _():
            dst_ref[...] = jnp.zeros_like(dst_ref)

        consts = (h_ref[...], ht_ref[...], qm_ref[...], km_ref[...], bm_ref[...])
        _, vjp = jax.vjp(lambda q_, k_, v_, g_, st_: scan_chunk(q_, k_, v_, g_, st_, *consts),
                         q_ref[...], k_ref[...], v_ref[...], g_ref[...], ss_ref[...])
        dq, dk_, dv_, dg, dst = vjp((do_ref[...], dst_ref[...]))
        dq_ref[...] = dq
        dk_ref[...] = dk_
        dg_ref[...] = dg
        dv_ref[...] = dv_
        dst_ref[...] = dst

    kspec = pl.BlockSpec((None, C, dk), lambda d, h, n: (d, chunk(d, n), h))
    vspec = pl.BlockSpec((None, C, dv), lambda d, h, n: (d, chunk(d, n), h))
    return pl.pallas_call(
        body,
        out_shape=[jax.ShapeDtypeStruct((2, T, heads * dk), F32)] * 3 + [jax.ShapeDtypeStruct((2, T, heads * dv), F32)],
        grid=(2, heads, N),
        in_specs=[pl.BlockSpec((C, dk), lambda d, h, n: (chunk(d, n), qb + h)),
                  pl.BlockSpec((None, C, dk), lambda d, h, n: (d * (kd - 1), chunk(d, n), h)),
                  kspec,
                  pl.BlockSpec((C, dv), lambda d, h, n: (chunk(d, n), vb + h)),
                  pl.BlockSpec((None, None, None, dv, dk), lambda d, h, n: (d, h, chunk(d, n), 0, 0)),
                  pl.BlockSpec((C, dv), lambda d, h, n: (chunk(d, n), h))] + _scan_const_specs(dk),
        out_specs=[kspec, kspec, kspec, vspec],
        scratch_shapes=[pltpu.VMEM((dv, dk), F32)],
        name=name, compiler_params=_cparams(("arbitrary", "arbitrary", "arbitrary")))(
            q, k2, g2, v, ss, do, *_scan_const_args())


def final_call(x, g, target, T):
    tr = _row_tile(T)

    def tile(xv, gv, tv):
        y = _rms(xv, gv)
        err = (y - tv) ** 2
        return jnp.sum(jnp.sum(err, axis=-1, keepdims=True), axis=0, keepdims=True) * (0.5 / D_MODEL)

    def body(x_ref, g_ref, t_ref, loss_ref, dx_ref, dg_ref):
        i = pl.program_id(0)
        tv = t_ref[...]
        lv, vjp = jax.vjp(lambda a, b: tile(a, b, tv), x_ref[...], g_ref[...])
        dx, dg = vjp(jnp.ones((1, 1), F32))
        dx_ref[...] = dx

        @pl.when(i == 0)
        def _():
            loss_ref[...] = jnp.zeros_like(loss_ref)
            dg_ref[...] = jnp.zeros_like(dg_ref)

        loss_ref[...] += jnp.broadcast_to(lv, loss_ref.shape)
        dg_ref[...] += dg

    return pl.pallas_call(
        body,
        out_shape=[jax.ShapeDtypeStruct((8, 128), F32), jax.ShapeDtypeStruct((T, D_MODEL), F32),
                   jax.ShapeDtypeStruct((1, D_MODEL), F32)],
        grid=(T // tr,),
        in_specs=[pl.BlockSpec((tr, D_MODEL), lambda i: (i, 0)), pl.BlockSpec((1, D_MODEL), lambda i: (0, 0)),
                  pl.BlockSpec((tr, D_MODEL), lambda i: (i, 0))],
        out_specs=[pl.BlockSpec((8, 128), lambda i: (0, 0)), pl.BlockSpec((tr, D_MODEL), lambda i: (i, 0)),
                   pl.BlockSpec((1, D_MODEL), lambda i: (0, 0))],
        name="final_loss", compiler_params=_cparams(("arbitrary",)))(x, g, target)


def adamw_call(w, g, m, v):
    shape = w.shape
    c = shape[-1]
    r = int(np.prod(shape[:-1])) if len(shape) > 1 else 1
    tr = r if r <= 256 else 256
    assert r % tr == 0

    def body(w_ref, g_ref, m_ref, v_ref, d_ref, nm_ref, nv_ref):
        gv = g_ref[...]
        nm = ADAM_B1 * m_ref[...] + (1.0 - ADAM_B1) * gv
        nv = ADAM_B2 * v_ref[...] + (1.0 - ADAM_B2) * jnp.square(gv)
        m_hat = nm / (1.0 - ADAM_B1 ** ADAM_STEP)
        v_hat = nv / (1.0 - ADAM_B2 ** ADAM_STEP)
        d_ref[...] = -ADAM_LR * (m_hat / (jnp.sqrt(v_hat) + ADAM_EPS) + ADAM_WD * w_ref[...])
        nm_ref[...] = nm
        nv_ref[...] = nv

    spec = pl.BlockSpec((tr, c), lambda i: (i, 0))
    outs = pl.pallas_call(body, out_shape=[jax.ShapeDtypeStruct((r, c), F32)] * 3, grid=(r // tr,),
                          in_specs=[spec] * 4, out_specs=[spec] * 3, name="adamw",
                          compiler_params=_cparams(("arbitrary",)))(*(t.reshape(r, c) for t in (w, g, m, v)))
    return tuple(o.reshape(shape) for o in outs)


def sum_devices(g64):
    def body(x_ref, o_ref):
        acc = x_ref[0:8, :]
        for d in range(1, 8):
            acc = acc + x_ref[8 * d:8 * d + 8, :]
        o_ref[...] = acc

    return pl.pallas_call(body, out_shape=jax.ShapeDtypeStruct((8, D_MODEL), F32), name="sum_devices")(g64)


def _half_tile(rh):
    return rh if rh <= 512 else 256


def add_sibling(g, recv, c):
    _, R, C = g.shape
    rh = R // 2
    tr = _half_tile(rh)
    nblk = rh // tr

    def body(c_ref, g_ref, r_ref, o_ref):
        o_ref[...] = g_ref[...] + r_ref[...]

    grid_spec = pltpu.PrefetchScalarGridSpec(
        num_scalar_prefetch=1, grid=(4, nblk),
        in_specs=[pl.BlockSpec((None, tr, C), lambda j, i, c_ref: (j, i + c_ref[0] * nblk, 0)),
                  pl.BlockSpec((None, tr, C), lambda j, i, c_ref: (j, i, 0))],
        out_specs=pl.BlockSpec((None, tr, C), lambda j, i, c_ref: (j, i, 0)))
    return pl.pallas_call(body, out_shape=jax.ShapeDtypeStruct((4, rh, C), F32), grid_spec=grid_spec,
                          name="rs_add_sibling", compiler_params=_cparams(("arbitrary", "arbitrary")))(c, g, recv)


def add_chips(s1, r3, chip):
    _, rh, C = s1.shape
    tr = _half_tile(rh)

    def body(j_ref, s_ref, a_ref, b_ref, c_ref, o_ref):
        o_ref[...] = ((s_ref[...] + a_ref[...]) + b_ref[...]) + c_ref[...]

    grid_spec = pltpu.PrefetchScalarGridSpec(
        num_scalar_prefetch=1, grid=(rh // tr,),
        in_specs=[pl.BlockSpec((None, tr, C), lambda i, j_ref: (j_ref[0], i, 0))]
        + [pl.BlockSpec((None, tr, C), functools.partial(lambda i, j_ref, k: (k, i, 0), k=k)) for k in range(3)],
        out_specs=pl.BlockSpec((tr, C), lambda i, j_ref: (i, 0)))
    return pl.pallas_call(body, out_shape=jax.ShapeDtypeStruct((rh, C), F32), grid_spec=grid_spec,
                          name="rs_add_chips", compiler_params=_cparams(("arbitrary",)))(chip, s1, r3, r3, r3)


def _remote(src, dst, ssem, rsem, dev):
    return pltpu.make_async_remote_copy(src_ref=src, dst_ref=dst, send_sem=ssem, recv_sem=rsem,
                                        device_id=dev, device_id_type=pl.DeviceIdType.MESH)


def _mesh_places():
    x, y, c = lax.axis_index("x"), lax.axis_index("y"), lax.axis_index("c")
    chips = [(1 - x, y), (x, 1 - y), (1 - x, 1 - y)]
    return x, y, c, (x, y, 1 - c), chips


def _hbm_specs(n):
    return [pl.BlockSpec(memory_space=pltpu.HBM) for _ in range(n)]


def gather_weights(shards, small):
    nb = len(shards)

    def body(*refs):
        ins, outs = refs[:nb + 1], refs[nb + 1:2 * nb + 2]
        send_sems, recv_sems, local_sems = refs[2 * nb + 2:]
        x, y, c, sibling, chips = _mesh_places()
        mine = 2 * x + y
        local = [pltpu.make_async_copy(ins[a], outs[a].at[mine], local_sems.at[a]) for a in range(nb + 1)]
        for cp in local:
            cp.start()

        def half(a, chip_idx, which):
            rh = ins[a].shape[0] // 2
            return outs[a].at[chip_idx, pl.ds(which * rh, rh), :]

        sent = []
        for a in range(nb):
            rh = ins[a].shape[0] // 2
            src = ins[a].at[pl.ds(c * rh, rh), :]
            for k, chip in enumerate(chips):
                sent.append(_remote(src, half(a, mine, c), send_sems.at[a, k], recv_sems.at[a, k], (*chip, c)))
        for k, chip in enumerate(chips):
            sent.append(_remote(ins[nb], outs[nb].at[mine], send_sems.at[nb, k], recv_sems.at[nb, k], (*chip, c)))
        for cp in sent:
            cp.start()
        for a in range(nb):
            for k, chip in enumerate(chips):
                region = half(a, 2 * chip[0] + chip[1], c)
                _remote(region, region, send_sems.at[a, k], recv_sems.at[a, k], (*chip, c)).wait_recv()
                fwd = _remote(region, region, send_sems.at[a, 3 + k], recv_sems.at[a, 3 + k], sibling)
                fwd.start()
                sent.append(fwd)
        for k, chip in enumerate(chips):
            region = outs[nb].at[2 * chip[0] + chip[1]]
            _remote(region, region, send_sems.at[nb, k], recv_sems.at[nb, k], (*chip, c)).wait_recv()
        for a in range(nb):
            for k, chip in enumerate(chips):
                region = half(a, 2 * chip[0] + chip[1], 1 - c)
                _remote(region, region, send_sems.at[a, 3 + k], recv_sems.at[a, 3 + k], sibling).wait_recv()
        for cp in sent:
            cp.wait_send()
        for cp in local:
            cp.wait()

    arrs = list(shards) + [small]
    return pl.pallas_call(
        body, out_shape=[jax.ShapeDtypeStruct((4,) + a.shape, a.dtype) for a in arrs],
        in_specs=_hbm_specs(nb + 1), out_specs=_hbm_specs(nb + 1),
        scratch_shapes=[pltpu.SemaphoreType.DMA((nb + 1, 6)), pltpu.SemaphoreType.DMA((nb + 1, 6)),
                        pltpu.SemaphoreType.DMA((nb + 1,))],
        name="gather_weights")(*arrs)


def rs_exchange_siblings(gs):
    n = len(gs)

    def body(*refs):
        ins, outs = refs[:n], refs[n:2 * n]
        send_sems, recv_sems = refs[2 * n:]
        x, y, c, sibling, chips = _mesh_places()
        cps = []
        for a in range(n):
            rh = ins[a].shape[1] // 2
            cps.append(_remote(ins[a].at[:, pl.ds((1 - c) * rh, rh), :], outs[a], send_sems.at[a], recv_sems.at[a], sibling))
        for cp in cps:
            cp.start()
        for cp in cps:
            cp.wait()

    return pl.pallas_call(
        body, out_shape=[jax.ShapeDtypeStruct((4, g.shape[1] // 2, g.shape[2]), g.dtype) for g in gs],
        in_specs=_hbm_specs(n), out_specs=_hbm_specs(n),
        scratch_shapes=[pltpu.SemaphoreType.DMA((n,)), pltpu.SemaphoreType.DMA((n,))],
        name="rs_exchange_siblings")(*gs)


def rs_exchange_chips(s1s):
    n = len(s1s)

    def body(*refs):
        ins, outs = refs[:n], refs[n:2 * n]
        send_sems, recv_sems = refs[2 * n:]
        x, y, c, sibling, chips = _mesh_places()
        cps = []
        for a in range(n):
            for k, chip in enumerate(chips):
                cps.append(_remote(ins[a].at[2 * chip[0] + chip[1]], outs[a].at[k], send_sems.at[a, k],
                                   recv_sems.at[a, k], (*chip, c)))
        for cp in cps:
            cp.start()
        for cp in cps:
            cp.wait()

    return pl.pallas_call(
        body, out_shape=[jax.ShapeDtypeStruct((3,) + s.shape[1:], s.dtype) for s in s1s],
        in_specs=_hbm_specs(n), out_specs=_hbm_specs(n),
        scratch_shapes=[pltpu.SemaphoreType.DMA((n, 3)), pltpu.SemaphoreType.DMA((n, 3))],
        name="rs_exchange_chips")(*s1s)


def rs_share_final(fs):
    n = len(fs)

    def body(*refs):
        ins, outs = refs[:n], refs[n:2 * n]
        send_sems, recv_sems, local_sems = refs[2 * n:]
        x, y, c, sibling, chips = _mesh_places()
        cps, local = [], []
        for a in range(n):
            rh = ins[a].shape[0]
            dst = outs[a].at[pl.ds(c * rh, rh), :]
            local.append(pltpu.make_async_copy(ins[a], dst, local_sems.at[a]))
            cps.append(_remote(ins[a], dst, send_sems.at[a], recv_sems.at[a], sibling))
        for cp in local + cps:
            cp.start()
        for a in range(n):
            rh = ins[a].shape[0]
            other = outs[a].at[pl.ds((1 - c) * rh, rh), :]
            _remote(other, other, send_sems.at[a], recv_sems.at[a], sibling).wait_recv()
        for cp in cps:
            cp.wait_send()
        for cp in local:
            cp.wait()

    return pl.pallas_call(
        body, out_shape=[jax.ShapeDtypeStruct((2 * f.shape[0], f.shape[1]), f.dtype) for f in fs],
        in_specs=_hbm_specs(n), out_specs=_hbm_specs(n),
        scratch_shapes=[pltpu.SemaphoreType.DMA((n,)), pltpu.SemaphoreType.DMA((n,)), pltpu.SemaphoreType.DMA((n,))],
        name="rs_share_final")(*fs)


def allgather_small(v):
    m_per = v.shape[0]

    def body(x_ref, out_ref, send_sems, recv_sems, local_sem):
        x, y, c, sibling, chips = _mesh_places()
        me = (x, y, c)

        def rows(px, py, pc):
            return out_ref.at[pl.ds((4 * px + 2 * py + pc) * m_per, m_per), :]

        def copy(k, block, to, src=None):
            return _remote(rows(*block) if src is None else src, rows(*block), send_sems.at[k], recv_sems.at[k], to)

        mine = pltpu.make_async_copy(x_ref, rows(*me), local_sem)
        mine.start()
        first = [copy(0, me, sibling, src=x_ref)]
        first += [copy(1 + j, me, (*chip, c), src=x_ref) for j, chip in enumerate(chips)]
        for cp in first:
            cp.start()
        passed = [copy(4 + j, (*chip, c), sibling) for j, chip in enumerate(chips)]
        for j, chip in enumerate(chips):
            copy(1 + j, (*chip, c), me).wait_recv()
            passed[j].start()
        copy(0, sibling, me).wait_recv()
        for j, chip in enumerate(chips):
            copy(4 + j, (*chip, 1 - c), me).wait_recv()
        for cp in first + passed:
            cp.wait_send()
        mine.wait()

    return pl.pallas_call(
        body, out_shape=jax.ShapeDtypeStruct((8 * m_per, v.shape[1]), v.dtype),
        in_specs=[pl.BlockSpec(memory_space=pltpu.VMEM)], out_specs=pl.BlockSpec(memory_space=pltpu.VMEM),
        scratch_shapes=[pltpu.SemaphoreType.DMA((7,)), pltpu.SemaphoreType.DMA((7,)), pltpu.SemaphoreType.DMA],
        name="allgather_small")(v)


def rms_res_tile(x, g):
    return (_rms(x, g), x)


def _lower_bounds(lb_param):
    lbs = jax.nn.softmax(lb_param.astype(F32), axis=0)
    return jnp.cumsum(lbs, axis=0) - lbs[0]


def _heads_major(t, n):
    return t.reshape(t.shape[0], n, HEAD_DIM_A).transpose(1, 0, 2)


def _heads_minor(t):
    return t.transpose(1, 0, 2).reshape(t.shape[1], t.shape[0] * t.shape[2])


def _even_fwd(x, i, W, lower, kv, slopes, T):
    O = EVEN_OFF
    g = W["norm_even"][i].reshape(1, D_MODEL)
    (h,) = rows_call("rms_fwd", rms_tile, T, [("row", x, 0, D_MODEL), ("full", g)], [D_MODEL], [BF16])
    p = matmul("mm_in_e", h, W["w_in_e"][i], "nn")
    q8 = _heads_major(p[:, O["qA"]:O["qA"] + W_A], N_Q_A)
    pad = lambda t: jnp.pad(_heads_major(t, N_KV_A), ((0, 0), (BLOCK, BLOCK), (0, 0)))
    k2p = pad(p[:, O["kA"]:O["kA"] + W_KV_A])
    v2p = pad(p[:, O["vA"]:O["vA"] + W_KV_A])
    sink = W["sink"][i].reshape(N_Q_A, 1, 1)
    a = _heads_minor(attn_fwd(q8, k2p, v2p, sink, slopes, T))
    prep_ins = [("row", p, O["qB"], W_B), ("row", p, O["zf"], W_B), ("row", p, O["zb"], W_B),
                ("full", lower[i][0:1]), ("full", lower[i][1:2])]
    qh, kf, kb, gf, gb = rows_call("hgrn_prep_fwd", hgrn_prep_tile, T, prep_ins, [W_B] * 5)
    k2, g2 = jnp.stack([kf, kb]), jnp.stack([gf, gb])
    o2, ss = scan_fwd("scan_fwd_h", qh, 0, k2, g2, p, O["iB"], N_HEADS_B, HEAD_DIM_B, HEAD_DIM_B, T)
    mo = mem_fwd(p, O["qM"], kv, T)
    hg = W["hgrn_norm"][i].reshape(1, W_B)
    post_ins = [("row", a, 0, W_A), ("row3", o2, 0, 0, W_B), ("row3", o2, 1, 0, W_B), ("row", mo, 0, W_M),
                ("row", p, O["gA"], W_A), ("row", p, O["gB"], W_B), ("row", p, O["gM"], W_M), ("full", hg)]
    (mix,) = rows_call("even_post_fwd", even_post_tile, T, post_ins, [MIX], [BF16])
    x_new = matmul("mm_out", mix, W["w_out_e"][i], "nn", add=x)
    return x_new, dict(x=x, g=g, h=h, p=p, q8=q8, k2p=k2p, v2p=v2p, sink=sink, prep_ins=prep_ins, qh=qh, k2=k2,
                       g2=g2, ss=ss, post_ins=post_ins, mix=mix)


def _assemble_even(dqA, dgA, dqB, dzf, dzb, dv0, dv1, dgB, dqM, dgM, dkA, dvA):
    return (jnp.concatenate([dqA, dgA, dqB, dzf, dzb, dv0 + dv1, dgB, dqM, dgM, dkA, dvA], axis=-1),)


def _even_bwd(dxo, sv, i, W, kv, slopes, T):
    O = EVEN_OFF
    p = sv["p"]
    dmix = matmul("mm_dmix", dxo, W["w_out_e"][i], "nt")
    dwo = matmul("mm_dwo", sv["mix"], dxo, "tn")
    da, dof, dmo, dgA, dgB, dgM, dhg = rows_vjp_call("even_post_bwd", even_post_tile, T, sv["post_ins"],
                                                      [[("row", dmix, 0, MIX)]], skip=(2,))
    dq8, dk2p, dv2p, dsink = attn_bwd(sv["q8"], sv["k2p"], sv["v2p"], sv["sink"], slopes, _heads_major(da, N_Q_A), T)
    dqA = _heads_minor(dq8)
    dkA = _heads_minor(dk2p[:, BLOCK:-BLOCK])
    dvA = _heads_minor(dv2p[:, BLOCK:-BLOCK])
    dq2, dk2, dg2, dv2 = scan_bwd("scan_bwd_h", sv["qh"], 0, sv["k2"], sv["g2"], p, O["iB"], sv["ss"], dof,
                                  N_HEADS_B, HEAD_DIM_B, HEAD_DIM_B, T)
    r3 = lambda arr, d: ("row3", arr, d, 0, W_B)
    dqB, dzf, dzb, dlow_f, dlow_b = rows_vjp_call(
        "hgrn_prep_bwd", hgrn_prep_tile, T, sv["prep_ins"],
        [[r3(dq2, 0), r3(dq2, 1)], [r3(dk2, 0)], [r3(dk2, 1)], [r3(dg2, 0)], [r3(dg2, 1)]])
    dlow = jnp.concatenate([dlow_f, dlow_b], axis=0)
    dqM, dkv = mem_bwd(p, O["qM"], kv, dmo, T)
    row = lambda arr, w: ("row", arr, 0, w)
    (dp,) = rows_call("even_dp", _assemble_even, T,
                      [row(dqA, W_A), row(dgA, W_A), row(dqB, W_B), row(dzf, W_B), row(dzb, W_B), r3(dv2, 0), r3(dv2, 1),
                       row(dgB, W_B), row(dqM, W_M), row(dgM, W_M), row(dkA, W_KV_A), row(dvA, W_KV_A)],
                      [EVEN_IN], [BF16])
    dh = matmul("mm_dh_e", dp, W["w_in_e"][i], "nt")
    dwi = matmul("mm_dwi_e", sv["h"], dp, "tn")
    dx, dg = rows_vjp_call("rms_res_bwd", rms_res_tile, T, [("row", sv["x"], 0, D_MODEL), ("full", sv["g"])],
                           [[("row", dh, 0, D_MODEL)], [("row", dxo, 0, D_MODEL)]])
    return dx, dict(w_in=dwi, w_out=dwo, norm=dg[0], sink=dsink.reshape(N_Q_A), low=dlow, hg=dhg[0], kv=dkv)


def _pad_gate_up(w_up):
    z = jnp.zeros((2, 128, WK_C), F32)
    z = z.at[0, 0:GATE_RANK].set(w_up[0])
    return z.at[1, GATE_RANK:2 * GATE_RANK].set(w_up[1])


def _odd_fwd(x, i, W, kv, T):
    O = ODD_OFF
    g = W["norm_odd"][i].reshape(1, D_MODEL)
    (h,) = rows_call("rms_fwd", rms_tile, T, [("row", x, 0, D_MODEL), ("full", g)], [D_MODEL], [BF16])
    p = matmul("mm_in_o", h, W["w_in_o"][i], "nn")
    wup = _pad_gate_up(W["w_gate_up"][i])
    prep_ins = [("row", p, O["qC"], WK_C), ("row", p, O["rr"], 128), ("full", wup[0]), ("full", wup[1]),
                ("full", W["b_gate"][i][0:1]), ("full", W["b_gate"][i][1:2])]
    qg, gf, gb = rows_call("gla_prep_fwd", gla_prep_tile, T, prep_ins, [WK_C] * 3)
    g2 = jnp.stack([gf, gb])
    k2 = p[None, :, O["kC"]:O["kC"] + WK_C]
    o2, ss = scan_fwd("scan_fwd_g", qg, 0, k2, g2, p, O["vC"], N_HEADS_C, DK_C, DV_C, T)
    mo = mem_fwd(p, O["qM"], kv, T)
    gg = W["gla_norm"][i].reshape(1, WV_C)
    post_ins = [("row3", o2, 0, 0, WV_C), ("row3", o2, 1, 0, WV_C), ("row", mo, 0, W_M),
                ("row", p, O["gC"], WV_C), ("row", p, O["gM"], W_M), ("full", gg)]
    (mix,) = rows_call("odd_post_fwd", odd_post_tile, T, post_ins, [MIX], [BF16])
    x_new = matmul("mm_out", mix, W["w_out_o"][i], "nn", add=x)
    return x_new, dict(x=x, g=g, h=h, p=p, prep_ins=prep_ins, qg=qg, k2=k2, g2=g2, ss=ss, post_ins=post_ins, mix=mix)


def _assemble_odd(dqC, dk0, dk1, dv0, dv1, dgC, dqM, dgM, dr):
    return (jnp.concatenate([dqC, dk0 + dk1, dv0 + dv1, dgC, dqM, dgM, dr], axis=-1),)


def _odd_bwd(dxo, sv, i, W, kv, T):
    O = ODD_OFF
    p = sv["p"]
    dmix = matmul("mm_dmix", dxo, W["w_out_o"][i], "nt")
    dwo = matmul("mm_dwo", sv["mix"], dxo, "tn")
    dof, dmo, dgC, dgM, dgg = rows_vjp_call("odd_post_bwd", odd_post_tile, T, sv["post_ins"],
                                            [[("row", dmix, 0, MIX)]], skip=(1,))
    dq2, dk2, dg2, dv2 = scan_bwd("scan_bwd_g", sv["qg"], 0, sv["k2"], sv["g2"], p, O["vC"], sv["ss"], dof,
                                  N_HEADS_C, DK_C, DV_C, T)
    r3 = lambda arr, d, w: ("row3", arr, d, 0, w)
    dqC, dr, dwup_f, dwup_b, dbg_f, dbg_b = rows_vjp_call(
        "gla_prep_bwd", gla_prep_tile, T, sv["prep_ins"],
        [[r3(dq2, 0, WK_C), r3(dq2, 1, WK_C)], [r3(dg2, 0, WK_C)], [r3(dg2, 1, WK_C)]])
    dqM, dkv = mem_bwd(p, O["qM"], kv, dmo, T)
    row = lambda arr, w: ("row", arr, 0, w)
    (dp,) = rows_call("odd_dp", _assemble_odd, T,
                      [row(dqC, WK_C), r3(dk2, 0, WK_C), r3(dk2, 1, WK_C), r3(dv2, 0, WV_C), r3(dv2, 1, WV_C),
                       row(dgC, WV_C), row(dqM, W_M), row(dgM, W_M), row(dr, 128)],
                      [ODD_PAD], [BF16])
    dh = matmul("mm_dh_o", dp, W["w_in_o"][i], "nt")
    dwi = matmul("mm_dwi_o", sv["h"], dp, "tn")
    dx, dg = rows_vjp_call("rms_res_bwd", rms_res_tile, T, [("row", sv["x"], 0, D_MODEL), ("full", sv["g"])],
                           [[("row", dh, 0, D_MODEL)], [("row", dxo, 0, D_MODEL)]])
    dw_up = jnp.stack([dwup_f[0:GATE_RANK], dwup_b[GATE_RANK:2 * GATE_RANK]])
    dbg = jnp.concatenate([dbg_f, dbg_b], axis=0)
    return dx, dict(w_in=dwi, w_out=dwo, norm=dg[0], w_up=dw_up, b_gate=dbg, gg=dgg[0], kv=dkv)


def local_step(x, mem, target, W):
    T = x.shape[0]
    slopes = (2.0 ** (-8.0 * jnp.arange(1, N_Q_A + 1, dtype=F32) / N_Q_A)).reshape(N_Q_A, 1, 1)
    lower, lower_vjp = jax.vjp(_lower_bounds, W["lb_param"])
    mem_g = W["mem_norm"].reshape(1, D_MODEL)
    (mem_n,) = rows_call("mem_rms_fwd", rms_tile, N_MEM, [("row", mem, 0, D_MODEL), ("full", mem_g)], [D_MODEL], [BF16])
    kvs = [matmul("mm_kv", mem_n, W["w_kv"][l], "nn") for l in range(DEPTH)]
    saved = []
    for l in range(DEPTH):
        if l % 2 == 0:
            x, sv = _even_fwd(x, l // 2, W, lower, kvs[l], slopes, T)
        else:
            x, sv = _odd_fwd(x, l // 2, W, kvs[l], T)
        saved.append(sv)
    loss, dx, dgf = final_call(x, W["final_norm"].reshape(1, D_MODEL), target, T)
    per = [None] * DEPTH
    for l in reversed(range(DEPTH)):
        if l % 2 == 0:
            dx, per[l] = _even_bwd(dx, saved[l], l // 2, W, kvs[l], slopes, T)
        else:
            dx, per[l] = _odd_bwd(dx, saved[l], l // 2, W, kvs[l], T)
    dmem_n, dw_kv = None, []
    for l in range(DEPTH):
        dw_kv.append(matmul("mm_dwkv", mem_n, per[l]["kv"], "tn"))
        dmem_n = matmul("mm_dmem", per[l]["kv"], W["w_kv"][l], "nt", add=dmem_n)
    (dmem_norm,) = rows_vjp_call("mem_rms_bwd", rms_tile, N_MEM, [("row", mem, 0, D_MODEL), ("full", mem_g)],
                                 [[("row", dmem_n, 0, D_MODEL)]], skip=(0,))
    ev, od = (per[0], per[2]), (per[1], per[3])
    (d_lb,) = lower_vjp(jnp.stack([e["low"] for e in ev]))
    grads = dict(
        w_in_e=jnp.stack([e["w_in"] for e in ev]), w_in_o=jnp.stack([o["w_in"] for o in od]),
        w_out_e=jnp.stack([e["w_out"] for e in ev]), w_out_o=jnp.stack([o["w_out"] for o in od]),
        w_kv=jnp.stack(dw_kv), norm_even=jnp.stack([e["norm"] for e in ev]), sink=jnp.stack([e["sink"] for e in ev]),
        lb_param=d_lb, hgrn_norm=jnp.stack([e["hg"] for e in ev]), norm_odd=jnp.stack([o["norm"] for o in od]),
        w_gate_up=jnp.stack([o["w_up"] for o in od]), b_gate=jnp.stack([o["b_gate"] for o in od]),
        gla_norm=jnp.stack([o["gg"] for o in od]), mem_norm=dmem_norm[0], final_norm=dgf[0])
    return loss, dx, grads


SMALL_SPECS = (("lb_param", (2, 2, 128)), ("norm_odd", (2, 256)), ("w_gate_up", (2, 2, 16, 128)),
               ("b_gate", (2, 2, 128)), ("gla_norm", (2, 256)))
SMALL_ROWS = 80


def _pack_small_local(d):
    return jnp.concatenate([d[n].reshape(-1) for n, _ in SMALL_SPECS]).reshape(SMALL_ROWS, 128)


def _unpack_small_local(b):
    flat, out, o = b.reshape(-1), {}, 0
    for n, shp in SMALL_SPECS:
        sz = int(np.prod(shp))
        out[n] = flat[o:o + sz].reshape(shp)
        o += sz
    return out


def _unpack_small_full(g4):
    per = [_unpack_small_local(g4[j]) for j in range(4)]
    return {n: jnp.concatenate([per[j][n] for j in range(4)], axis=-1) for n, _ in SMALL_SPECS}


def _pack_small_blocks(full):
    blocks = []
    for j in range(4):
        blocks.append(_pack_small_local({n: full[n][..., j * shp[-1]:(j + 1) * shp[-1]] for n, shp in SMALL_SPECS}))
    return jnp.stack(blocks)


def _cols(t, order, off, widths):
    return [t[..., off[n]:off[n] + widths[n]] for n in order]


EVEN_REF_ORDER = ("qA", "kA", "vA", "gA", "qB", "zf", "zb", "iB", "gB", "qM", "gM")
ODD_REF_ORDER = ("qC", "kC", "vC", "gC", "rr", "qM", "gM")


def _full_weights(gathered, gsmall, rep):
    g_in_e, g_in_o, g_out_e, g_out_o, g_kv = gathered
    t = g_in_e.reshape(4, 2, D_MODEL, EVEN_IN // 4).transpose(1, 2, 0, 3).reshape(2, D_MODEL, EVEN_IN)
    w_in_e = jnp.concatenate(_cols(t, EVEN_ORDER, EVEN_REF_OFF, EVEN_W), axis=-1)
    t = g_in_o.reshape(4, 2, D_MODEL, ODD_IN // 4).transpose(1, 2, 0, 3).reshape(2, D_MODEL, ODD_IN)
    w_in_o = jnp.concatenate(_cols(t, ODD_ORDER, ODD_REF_OFF, ODD_W) + [jnp.zeros((2, D_MODEL, ODD_PAD - ODD_IN), BF16)],
                             axis=-1)
    blocks_to_rows = lambda g, n: g.reshape(4, n, g.shape[1] // n, g.shape[2]).transpose(1, 0, 2, 3).reshape(
        n, 4 * (g.shape[1] // n), g.shape[2])
    W = dict(w_in_e=w_in_e, w_in_o=w_in_o, w_out_e=blocks_to_rows(g_out_e, 2), w_out_o=blocks_to_rows(g_out_o, 2),
             w_kv=blocks_to_rows(g_kv, DEPTH))
    W.update(_unpack_small_full(gsmall))
    W.update(rep)
    return W


def _grad_blocks(grads):
    t = jnp.concatenate(_cols(grads["w_in_e"], EVEN_REF_ORDER, EVEN_OFF, EVEN_W), axis=-1)
    b_in_e = t.reshape(2, D_MODEL, 4, EVEN_IN // 4).transpose(2, 0, 1, 3).reshape(4, 2 * D_MODEL, EVEN_IN // 4)
    t = jnp.concatenate(_cols(grads["w_in_o"], ODD_REF_ORDER, ODD_OFF, ODD_W), axis=-1)
    b_in_o = t.reshape(2, D_MODEL, 4, ODD_IN // 4).transpose(2, 0, 1, 3).reshape(4, 2 * D_MODEL, ODD_IN // 4)
    rows_to_blocks = lambda g: g.reshape(g.shape[0], 4, g.shape[1] // 4, g.shape[2]).transpose(1, 0, 2, 3).reshape(
        4, g.shape[0] * (g.shape[1] // 4), g.shape[2])
    return [b_in_e, b_in_o, rows_to_blocks(grads["w_out_e"]), rows_to_blocks(grads["w_out_o"]),
            rows_to_blocks(grads["w_kv"]), _pack_small_blocks(grads)]


WEIGHT_NAMES = ("norm_even", "w_in_even", "sink", "lb_param", "hgrn_norm", "w_out_even", "norm_odd", "w_in_odd",
                "w_gate_up", "b_gate", "gla_norm", "w_out_odd", "mem_norm", "w_mem_kv", "final_norm")


def kernel(x, mem, norm_even, w_in_even, sink, lb_param, hgrn_norm, w_out_even, norm_odd, w_in_odd, w_gate_up, b_gate, gla_norm, w_out_odd, mem_norm, w_mem_kv, final_norm, loss_target, m_norm_even, m_w_in_even, m_sink, m_lb_param, m_hgrn_norm, m_w_out_even, m_norm_odd, m_w_in_odd, m_w_gate_up, m_b_gate, m_gla_norm, m_w_out_odd, m_mem_norm, m_w_mem_kv, m_final_norm, v_norm_even, v_w_in_even, v_sink, v_lb_param, v_hgrn_norm, v_w_out_even, v_norm_odd, v_w_in_odd, v_w_gate_up, v_b_gate, v_gla_norm, v_w_out_odd, v_mem_norm, v_w_mem_kv, v_final_norm):
    w = dict(zip(WEIGHT_NAMES, (norm_even, w_in_even, sink, lb_param, hgrn_norm, w_out_even, norm_odd, w_in_odd,
                                w_gate_up, b_gate, gla_norm, w_out_odd, mem_norm, w_mem_kv, final_norm)))
    m = dict(zip(WEIGHT_NAMES, (m_norm_even, m_w_in_even, m_sink, m_lb_param, m_hgrn_norm, m_w_out_even, m_norm_odd,
                                m_w_in_odd, m_w_gate_up, m_b_gate, m_gla_norm, m_w_out_odd, m_mem_norm, m_w_mem_kv,
                                m_final_norm)))
    v = dict(zip(WEIGHT_NAMES, (v_norm_even, v_w_in_even, v_sink, v_lb_param, v_hgrn_norm, v_w_out_even, v_norm_odd,
                                v_w_in_odd, v_w_gate_up, v_b_gate, v_gla_norm, v_w_out_odd, v_mem_norm, v_w_mem_kv,
                                v_final_norm)))
    ci = lax.axis_index("c").astype(jnp.int32).reshape(1)
    chip = (2 * lax.axis_index("x") + lax.axis_index("y")).astype(jnp.int32).reshape(1)

    flat2 = lambda t: t.reshape(-1, t.shape[-1])
    shards = [flat2(w[n]).astype(BF16) for n in ("w_in_even", "w_in_odd", "w_out_even", "w_out_odd", "w_mem_kv")]
    *gathered, gsmall = gather_weights(shards, _pack_small_local(w))
    rep = {n: w[n] for n in ("norm_even", "sink", "hgrn_norm", "mem_norm", "final_norm")}
    W = _full_weights(gathered, gsmall, rep)

    loss_tile, dx, grads = local_step(x[0], mem[0], loss_target[0], W)

    blocks = _grad_blocks(grads)
    recv = rs_exchange_siblings(blocks)
    chip_sums = [add_sibling(g, r, ci) for g, r in zip(blocks, recv)]
    recv3 = rs_exchange_chips(chip_sums)
    halves = [add_chips(s, r, chip) for s, r in zip(chip_sums, recv3)]
    g_in_e, g_in_o, g_out_e, g_out_o, g_kv, g_small = rs_share_final(halves)
    gl = _unpack_small_local(g_small)
    gl.update(w_in_even=g_in_e.reshape(w_in_even.shape), w_in_odd=g_in_o.reshape(w_in_odd.shape),
              w_out_even=g_out_e.reshape(w_out_even.shape), w_out_odd=g_out_o.reshape(w_out_odd.shape),
              w_mem_kv=g_kv.reshape(w_mem_kv.shape))

    pack = jnp.zeros((8, D_MODEL), F32)
    pack = pack.at[0:2].set(grads["norm_even"]).at[2].set(grads["hgrn_norm"].reshape(-1))
    pack = pack.at[3].set(grads["mem_norm"]).at[4].set(grads["final_norm"])
    pack = pack.at[5, 0:16].set(grads["sink"].reshape(-1)).at[5, 16].set(loss_tile[0, 0])
    tot = sum_devices(allgather_small(pack))
    gl.update(norm_even=tot[0:2], hgrn_norm=tot[2].reshape(2, W_B), mem_norm=tot[3], final_norm=tot[4],
              sink=tot[5, 0:16].reshape(2, N_Q_A))
    loss = tot[5, 16]

    upd = {n: adamw_call(w[n], gl[n], m[n], v[n]) for n in WEIGHT_NAMES}
    return (loss, dx[None], *[gl[n] for n in WEIGHT_NAMES], *[upd[n][0] for n in WEIGHT_NAMES],
            *[upd[n][1] for n in WEIGHT_NAMES], *[upd[n][2] for n in WEIGHT_NAMES])
```

```python
import functools

import numpy as np
import jax
import jax.numpy as jnp
from jax import lax
from jax.experimental import pallas as pl
from jax.experimental.pallas import tpu as pltpu

F32 = jnp.float32
BF16 = jnp.bfloat16

D_MODEL = 1024
DEPTH = 4
N_Q_A, N_KV_A, HEAD_DIM_A = 8, 2, 64
W_A, W_KV_A = 512, 128
WINDOW = 128
BLOCK = 128
N_HEADS_B, HEAD_DIM_B, W_B = 4, 128, 512
N_HEADS_C, DK_C, DV_C, WK_C, WV_C = 4, 128, 256, 512, 1024
GATE_RANK = 16
GATE_TEMP = 16.0
N_MEM, N_HEADS_M, HEAD_DIM_M, W_M = 256, 4, 128, 512
EPS = 1e-6
MASK_VALUE = -1e30
MIN_GATE = 1e-30
EVEN_IN, ODD_IN = 4864, 4128
ODD_PAD = 4224
MIX = 1536
ADAM_LR, ADAM_B1, ADAM_B2, ADAM_EPS, ADAM_WD, ADAM_STEP = 0.001, 0.9, 0.999, 1e-08, 0.01, 10

SCAN_CHUNK = 128
SCAN_LEVELS = 7
VMEM_LIMIT = 56 * 1024 * 1024

EVEN_REF_OFF = dict(qA=0, kA=512, vA=640, gA=768, qB=1280, zf=1792, zb=2304, iB=2816, gB=3328, qM=3840, gM=4352)
EVEN_W = dict(qA=512, kA=128, vA=128, gA=512, qB=512, zf=512, zb=512, iB=512, gB=512, qM=512, gM=512)
EVEN_ORDER = ("qA", "gA", "qB", "zf", "zb", "iB", "gB", "qM", "gM", "kA", "vA")
ODD_REF_OFF = dict(qC=0, kC=512, vC=1024, gC=2048, rr=3072, qM=3104, gM=3616)
ODD_W = dict(qC=512, kC=512, vC=1024, gC=1024, rr=32, qM=512, gM=512)
ODD_ORDER = ("qC", "kC", "vC", "gC", "qM", "gM", "rr")


def _offsets(order, widths):
    off, o = {}, 0
    for n in order:
        off[n] = o
        o += widths[n]
    return off


EVEN_OFF = _offsets(EVEN_ORDER, EVEN_W)
ODD_OFF = _offsets(ODD_ORDER, ODD_W)


def _dg(a, b, ca, cb):
    return lax.dot_general(a.astype(BF16), b.astype(BF16), (((ca,), (cb,)), ((), ())),
                           preferred_element_type=F32)


def dot_nn(a, b):
    return _dg(a, b, 1, 0)


def dot_nt(a, b):
    return _dg(a, b, 1, 1)


def dot_tn(a, b):
    return _dg(a, b, 0, 0)


@jax.custom_vjp
def bdot(a, b):
    return dot_nn(a, b)


bdot.defvjp(lambda a, b: (dot_nn(a, b), (a, b)),
            lambda r, g: (dot_nt(g, r[1]), dot_tn(r[0], g)))


@jax.custom_vjp
def bdot_t(a, b):
    return dot_nt(a, b)


bdot_t.defvjp(lambda a, b: (dot_nt(a, b), (a, b)),
              lambda r, g: (dot_nn(g, r[1]), dot_tn(g, r[0])))


@jax.custom_vjp
def bdot_tn(a, b):
    return dot_tn(a, b)


bdot_tn.defvjp(lambda a, b: (dot_tn(a, b), (a, b)),
               lambda r, g: (dot_nt(r[1], g), dot_nn(r[0], g)))


def _split_mm(h, x):
    hi = x.astype(BF16)
    lo = (x - hi.astype(F32)).astype(BF16)
    return (lax.dot_general(h, hi, (((1,), (0,)), ((), ())), preferred_element_type=F32)
            + lax.dot_general(h, lo, (((1,), (0,)), ((), ())), preferred_element_type=F32))


@jax.custom_vjp
def hdot(h, ht, x):
    return _split_mm(h, x)


hdot.defvjp(lambda h, ht, x: (_split_mm(h, x), (h, ht)),
            lambda r, g: (jnp.zeros_like(r[0]), jnp.zeros_like(r[1]), _split_mm(r[1], g)))


def _sigmoid(z):
    return 1.0 / (1.0 + jnp.exp(-z))


def _silu(z):
    return z * _sigmoid(z)


def _log_sigmoid(z):
    return jnp.minimum(z, 0.0) - jnp.log(1.0 + jnp.exp(-jnp.abs(z)))


def _rms(x, g):
    return x * lax.rsqrt(jnp.mean(x * x, axis=-1, keepdims=True) + EPS) * g


def rms_tile(x, g):
    return (_rms(x, g),)


@functools.partial(jax.custom_vjp, nondiff_argnums=(1, 2))
def split(x, n, axis):
    w = x.shape[axis] // n
    return tuple(lax.slice_in_dim(x, h * w, (h + 1) * w, axis=axis) for h in range(n))


split.defvjp(lambda x, n, axis: (split(x, n, axis), None),
             lambda n, axis, _, cts: (jnp.concatenate(cts, axis=axis),))


def _group_rms(o, g, heads):
    return jnp.concatenate([_rms(oh, gh) for oh, gh in zip(split(o, heads, 1), split(g, heads, 1))], axis=-1)


def even_post_tile(a, o2f, o2b, mo, gA, gB, gM, hg):
    y = _group_rms(o2f + o2b, hg, N_HEADS_B)
    return (jnp.concatenate([a * _silu(gA), y * _silu(gB), mo * _silu(gM)], axis=-1),)


def odd_post_tile(o2f, o2b, mo, gC, gM, gg):
    y = _group_rms(o2f + o2b, gg, N_HEADS_C)
    return (jnp.concatenate([y * _silu(gC), mo * _silu(gM)], axis=-1),)


def hgrn_prep_tile(qB, zf, zb, low_f, low_b):
    ks, gs = [], []
    for z, lb in ((zf, low_f), (zb, low_b)):
        f = lb + (1.0 - lb) * _sigmoid(z)
        gs.append(jnp.log(jnp.maximum(f, MIN_GATE)))
        ks.append((1.0 - lb) * _sigmoid(-z))
    return (_silu(qB), ks[0], ks[1], gs[0], gs[1])


def gla_prep_tile(qC, r128, wup_f, wup_b, bg_f, bg_b):
    gs = [_log_sigmoid(bdot(r128, wup) + bg) / GATE_TEMP for wup, bg in ((wup_f, bg_f), (wup_b, bg_b))]
    return (qC * (DK_C ** -0.5), gs[0], gs[1])


def mem_tile(q, k, v):
    s = bdot_t(q, k) * (HEAD_DIM_M ** -0.5)
    m = lax.stop_gradient(jnp.max(s, axis=-1, keepdims=True))
    p = jnp.exp(s - m)
    p = p / jnp.sum(p, axis=-1, keepdims=True)
    return (bdot(p, v),)


def attn_block(qs, ks, vs, sinks, slopes, c, seq):
    i = lax.broadcasted_iota(jnp.int32, (BLOCK, 3 * BLOCK), 0)
    j = lax.broadcasted_iota(jnp.int32, (BLOCK, 3 * BLOCK), 1)
    dist = jnp.abs(i - j + BLOCK).astype(F32)
    kpos = (c - 1) * BLOCK + j
    valid = (dist <= WINDOW) & (kpos >= 0) & (kpos < seq)
    outs = []
    for q, sk, slope in zip(qs, sinks, slopes):
        s = bdot_t(q, ks) * (HEAD_DIM_A ** -0.5)
        s = jnp.where(valid, s - slope * dist, MASK_VALUE)
        m = lax.stop_gradient(jnp.maximum(jnp.max(s, axis=-1, keepdims=True), sk))
        p = jnp.where(valid, jnp.exp(s - m), 0.0)
        denom = jnp.sum(p, axis=-1, keepdims=True) + jnp.exp(sk - m)
        outs.append(bdot(p, vs) / denom)
    return tuple(outs)


def scan_chunk(q, k, v, g, st, h, ht, qm, km, bm):
    C = SCAN_CHUNK
    e = split(hdot(h, ht, g), 2 + SCAN_LEVELS, 0)
    qe = q * jnp.exp(e[0])
    kd = k * jnp.exp(e[1])
    tot = jnp.sum(g, axis=0, keepdims=True)
    r = lax.broadcasted_iota(jnp.int32, (C, C), 0)
    s = lax.broadcasted_iota(jnp.int32, (C, C), 1)
    a = jnp.where(r == s, jnp.sum(q * k, axis=-1, keepdims=True), 0.0)
    for l in range(SCAN_LEVELS):
        el = jnp.exp(e[2 + l])
        qs = q * el * qm[l * C:(l + 1) * C]
        ks = k * el * km[l * C:(l + 1) * C]
        a = a + bdot_t(qs, ks) * bm[l * C:(l + 1) * C]
    o = bdot_t(qe, st) + bdot(a, v)
    st_new = st * jnp.exp(tot) + bdot_tn(v, kd)
    return o, st_new


def _scan_consts():
    C, L = SCAN_CHUNK, SCAN_LEVELS
    t = np.arange(C)[:, None]
    r = np.arange(C)[None, :]
    blocks = [(r <= t), (r > t)]
    qms, kms, bms = [], [], []
    for l in range(1, L + 1):
        m = C >> l
        upper_t = (t % (2 * m)) >= m
        same_half = (t // m) == (r // m)
        blocks.append(same_half & np.where(upper_t, r <= t, r > t))
        qms.append(np.broadcast_to(upper_t, (C, C)))
        kms.append(np.broadcast_to(~upper_t, (C, C)))
        bms.append((t // (2 * m)) == (r // (2 * m)))
    hf = np.concatenate(blocks, axis=0).astype(np.float32)
    flip = lambda mat: mat.reshape(-1, C, C)[:, ::-1, ::-1].reshape(-1, C)
    hb = flip(hf)
    qmf = np.concatenate(qms, axis=0).astype(np.float32)
    kmf = np.concatenate(kms, axis=0).astype(np.float32)
    bm = np.concatenate(bms, axis=0).astype(np.float32)
    h = np.stack([hf, hb])
    ht = np.stack([hf.T, hb.T])
    qm = np.stack([qmf, kmf])
    km = np.stack([kmf, qmf])
    return h, ht, qm, km, bm


def _cparams(sem):
    return pltpu.CompilerParams(dimension_semantics=sem, vmem_limit_bytes=VMEM_LIMIT)


def _row_tile(T):
    return min(T, 256)


def _in_spec(spec, tr):
    kind = spec[0]
    if kind == "row":
        _, arr, off, w = spec
        assert off % w == 0
        return arr, pl.BlockSpec((tr, w), functools.partial(lambda i, b: (i, b), b=off // w))
    if kind == "row3":
        _, arr, d, off, w = spec
        assert off % w == 0
        return arr, pl.BlockSpec((None, tr, w), functools.partial(lambda i, d, b: (d, i, b), d=d, b=off // w))
    _, arr = spec
    return arr, pl.BlockSpec(arr.shape, functools.partial(lambda i, n: (0,) * n, n=arr.ndim))


def rows_call(name, tile_fn, T, ins, out_widths, out_dtypes=None, stacks=None):
    tr = _row_tile(T)
    n_in = len(ins)
    out_dtypes = out_dtypes or [F32] * len(out_widths)
    stacks = stacks or [(k,) for k in range(len(out_widths))]

    def body(*refs):
        vals = [r[...] for r in refs[:n_in]]
        outs = tile_fn(*vals)
        for r, members in zip(refs[n_in:], stacks):
            if len(members) == 1:
                r[...] = outs[members[0]].astype(r.dtype)
            else:
                for d, k in enumerate(members):
                    r[d] = outs[k].astype(r.dtype)

    in_specs, args = [], []
    for spec in ins:
        arr, bs = _in_spec(spec, tr)
        args.append(arr)
        in_specs.append(bs)
    out_specs, out_shape = [], []
    for w, dt, members in zip(out_widths, out_dtypes, stacks):
        n = len(members)
        if n == 1:
            out_specs.append(pl.BlockSpec((tr, w), lambda i: (i, 0)))
            out_shape.append(jax.ShapeDtypeStruct((T, w), dt))
        else:
            out_specs.append(pl.BlockSpec((n, tr, w), lambda i: (0, i, 0)))
            out_shape.append(jax.ShapeDtypeStruct((n, T, w), dt))
    return pl.pallas_call(body, out_shape=out_shape, grid=(T // tr,), in_specs=in_specs, out_specs=out_specs,
                          name=name, compiler_params=_cparams(("arbitrary",)))(*args)


def rows_vjp_call(name, tile_fn, T, ins, cts, skip=()):
    tr = _row_tile(T)
    n_in = len(ins)
    n_ct = [len(c) for c in cts]
    want = [k for k in range(n_in) if k not in skip]

    def body(*refs):
        i = pl.program_id(0)
        vals = [r[...] for r in refs[:n_in]]
        ct, pos = [], n_in
        for n in n_ct:
            acc = refs[pos][...]
            for r in refs[pos + 1:pos + n]:
                acc = acc + r[...]
            ct.append(acc)
            pos += n
        _, vjp = jax.vjp(tile_fn, *vals)
        grads = vjp(tuple(ct))
        for r, k in zip(refs[pos:], want):
            if ins[k][0] == "full":
                @pl.when(i == 0)
                def _():
                    r[...] = jnp.zeros_like(r)
                r[...] += grads[k]
            else:
                r[...] = grads[k]

    in_specs, args = [], []
    for spec in list(ins) + [s for c in cts for s in c]:
        arr, bs = _in_spec(spec, tr)
        args.append(arr)
        in_specs.append(bs)
    out_specs, out_shape = [], []
    for k in want:
        if ins[k][0] == "full":
            arr = ins[k][1]
            out_specs.append(pl.BlockSpec(arr.shape, functools.partial(lambda i, n: (0,) * n, n=arr.ndim)))
            out_shape.append(jax.ShapeDtypeStruct(arr.shape, F32))
        else:
            w = ins[k][-1]
            out_specs.append(pl.BlockSpec((tr, w), lambda i: (i, 0)))
            out_shape.append(jax.ShapeDtypeStruct((T, w), F32))
    return pl.pallas_call(body, out_shape=out_shape, grid=(T // tr,), in_specs=in_specs, out_specs=out_specs,
                          name=name, compiler_params=_cparams(("arbitrary",)))(*args)


def matmul(name, a, b, mode, add=None, out_dtype=F32):
    if mode == "tn":
        K, M = a.shape
        N = b.shape[1]
        tm = M if M <= 1024 else 512
        tn = N if N <= 1280 else (N // 2 if (N // 2) % 128 == 0 else N)
        tk = min(K, 512)
        grid = (M // tm, N // tn, K // tk)

        def body(a_ref, b_ref, o_ref):
            @pl.when(pl.program_id(2) == 0)
            def _():
                o_ref[...] = jnp.zeros_like(o_ref)
            o_ref[...] += dot_tn(a_ref[...], b_ref[...])

        return pl.pallas_call(
            body, out_shape=jax.ShapeDtypeStruct((M, N), F32), grid=grid,
            in_specs=[pl.BlockSpec((tk, tm), lambda i, j, k: (k, i)), pl.BlockSpec((tk, tn), lambda i, j, k: (k, j))],
            out_specs=pl.BlockSpec((tm, tn), lambda i, j, k: (i, j)), name=name,
            compiler_params=_cparams(("arbitrary", "arbitrary", "arbitrary")))(a, b)

    M, K = a.shape
    N = b.shape[1] if mode == "nn" else b.shape[0]
    tm = min(M, 256)
    tn = N if N <= 1536 else (N // 2 if (N // 2) % 128 == 0 else (N // 3 if (N // 3) % 128 == 0 else N))
    grid = (N // tn, M // tm)
    n_in = 2 + (add is not None)

    def body(*refs):
        a_ref, b_ref = refs[0], refs[1]
        o_ref = refs[n_in]
        acc = dot_nn(a_ref[...], b_ref[...]) if mode == "nn" else dot_nt(a_ref[...], b_ref[...])
        if add is not None:
            acc = acc + refs[2][...]
        o_ref[...] = acc.astype(o_ref.dtype)

    in_specs = [pl.BlockSpec((tm, K), lambda j, i: (i, 0)),
                pl.BlockSpec((K, tn), lambda j, i: (0, j)) if mode == "nn" else pl.BlockSpec((tn, K), lambda j, i: (j, 0))]
    args = [a, b]
    if add is not None:
        in_specs.append(pl.BlockSpec((tm, tn), lambda j, i: (i, j)))
        args.append(add)
    return pl.pallas_call(
        body, out_shape=jax.ShapeDtypeStruct((M, N), out_dtype), grid=grid, in_specs=in_specs,
        out_specs=pl.BlockSpec((tm, tn), lambda j, i: (i, j)), name=name,
        compiler_params=_cparams(("arbitrary", "arbitrary")))(*args)


def attn_fwd(q8, k2p, v2p, sink, slopes, T):
    nb = T // BLOCK

    def body(q_ref, k_ref, v_ref, sink_ref, slope_ref, o_ref):
        c = pl.program_id(1)
        start = pl.multiple_of(c * BLOCK, BLOCK)
        ks = k_ref[pl.ds(start, 3 * BLOCK), :]
        vs = v_ref[pl.ds(start, 3 * BLOCK), :]
        outs = attn_block([q_ref[g] for g in range(4)], ks, vs, [sink_ref[g] for g in range(4)],
                          [slope_ref[g] for g in range(4)], c, T)
        for g in range(4):
            o_ref[g] = outs[g]

    return pl.pallas_call(
        body, out_shape=jax.ShapeDtypeStruct((N_Q_A, T, HEAD_DIM_A), F32), grid=(N_KV_A, nb),
        in_specs=[pl.BlockSpec((4, BLOCK, HEAD_DIM_A), lambda n, c: (n, c, 0)),
                  pl.BlockSpec((None, T + 2 * BLOCK, HEAD_DIM_A), lambda n, c: (n, 0, 0)),
                  pl.BlockSpec((None, T + 2 * BLOCK, HEAD_DIM_A), lambda n, c: (n, 0, 0)),
                  pl.BlockSpec((4, 1, 1), lambda n, c: (n, 0, 0)),
                  pl.BlockSpec((4, 1, 1), lambda n, c: (n, 0, 0))],
        out_specs=pl.BlockSpec((4, BLOCK, HEAD_DIM_A), lambda n, c: (n, c, 0)),
        name="attn_fwd", compiler_params=_cparams(("arbitrary", "arbitrary")))(q8, k2p, v2p, sink, slopes)


def attn_bwd(q8, k2p, v2p, sink, slopes, do8, T):
    nb = T // BLOCK

    def body(q_ref, k_ref, v_ref, sink_ref, slope_ref, do_ref, dq_ref, dk_ref, dv_ref, dsink_ref):
        c = pl.program_id(1)
        start = pl.multiple_of(c * BLOCK, BLOCK)
        ks = k_ref[pl.ds(start, 3 * BLOCK), :]
        vs = v_ref[pl.ds(start, 3 * BLOCK), :]
        slopes = [slope_ref[g] for g in range(4)]
        _, vjp = jax.vjp(lambda qs, kk, vv, sks: attn_block(qs, kk, vv, sks, slopes, c, T),
                         [q_ref[g] for g in range(4)], ks, vs, [sink_ref[g] for g in range(4)])
        dqs, dks, dvs, dsks = vjp(tuple(do_ref[g] for g in range(4)))

        @pl.when(c == 0)
        def _():
            dk_ref[...] = jnp.zeros_like(dk_ref)
            dv_ref[...] = jnp.zeros_like(dv_ref)
            dsink_ref[...] = jnp.zeros_like(dsink_ref)

        dk_ref[pl.ds(start, 3 * BLOCK), :] += dks
        dv_ref[pl.ds(start, 3 * BLOCK), :] += dvs
        for g in range(4):
            dq_ref[g] = dqs[g]
            dsink_ref[g] += dsks[g]

    qspec = pl.BlockSpec((4, BLOCK, HEAD_DIM_A), lambda n, c: (n, c, 0))
    kspec = pl.BlockSpec((None, T + 2 * BLOCK, HEAD_DIM_A), lambda n, c: (n, 0, 0))
    sspec = pl.BlockSpec((4, 1, 1), lambda n, c: (n, 0, 0))
    return pl.pallas_call(
        body,
        out_shape=[jax.ShapeDtypeStruct((N_Q_A, T, HEAD_DIM_A), F32),
                   jax.ShapeDtypeStruct((N_KV_A, T + 2 * BLOCK, HEAD_DIM_A), F32),
                   jax.ShapeDtypeStruct((N_KV_A, T + 2 * BLOCK, HEAD_DIM_A), F32),
                   jax.ShapeDtypeStruct((N_Q_A, 1, 1), F32)],
        grid=(N_KV_A, nb), in_specs=[qspec, kspec, kspec, sspec, sspec, qspec],
        out_specs=[qspec, kspec, kspec, sspec],
        name="attn_bwd", compiler_params=_cparams(("arbitrary", "arbitrary")))(q8, k2p, v2p, sink, slopes, do8)


def mem_fwd(p, q_off, kv, T):
    tr = _row_tile(T)
    qb = q_off // HEAD_DIM_M

    def body(q_ref, k_ref, v_ref, o_ref):
        (o,) = mem_tile(q_ref[...], k_ref[...], v_ref[...])
        o_ref[...] = o

    return pl.pallas_call(
        body, out_shape=jax.ShapeDtypeStruct((T, W_M), F32), grid=(N_HEADS_M, T // tr),
        in_specs=[pl.BlockSpec((tr, HEAD_DIM_M), lambda h, i: (i, qb + h)),
                  pl.BlockSpec((N_MEM, HEAD_DIM_M), lambda h, i: (0, h)),
                  pl.BlockSpec((N_MEM, HEAD_DIM_M), lambda h, i: (0, N_HEADS_M + h))],
        out_specs=pl.BlockSpec((tr, HEAD_DIM_M), lambda h, i: (i, h)),
        name="mem_fwd", compiler_params=_cparams(("arbitrary", "arbitrary")))(p, kv, kv)


def mem_bwd(p, q_off, kv, do, T):
    tr = _row_tile(T)
    qb = q_off // HEAD_DIM_M

    def body(q_ref, k_ref, v_ref, do_ref, dq_ref, dk_ref, dv_ref):
        i = pl.program_id(1)
        _, vjp = jax.vjp(mem_tile, q_ref[...], k_ref[...], v_ref[...])
        dq, dk, dv = vjp((do_ref[...],))
        dq_ref[...] = dq

        @pl.when(i == 0)
        def _():
            dk_ref[...] = jnp.zeros_like(dk_ref)
            dv_ref[...] = jnp.zeros_like(dv_ref)

        dk_ref[...] += dk
        dv_ref[...] += dv

    kspec = pl.BlockSpec((N_MEM, HEAD_DIM_M), lambda h, i: (0, h))
    dq, dk, dv = pl.pallas_call(
        body,
        out_shape=[jax.ShapeDtypeStruct((T, W_M), F32), jax.ShapeDtypeStruct((N_MEM, W_M), F32),
                   jax.ShapeDtypeStruct((N_MEM, W_M), F32)],
        grid=(N_HEADS_M, T // tr),
        in_specs=[pl.BlockSpec((tr, HEAD_DIM_M), lambda h, i: (i, qb + h)), kspec,
                  pl.BlockSpec((N_MEM, HEAD_DIM_M), lambda h, i: (0, N_HEADS_M + h)),
                  pl.BlockSpec((tr, HEAD_DIM_M), lambda h, i: (i, h))],
        out_specs=[pl.BlockSpec((tr, HEAD_DIM_M), lambda h, i: (i, h)), kspec, kspec],
        name="mem_bwd", compiler_params=_cparams(("arbitrary", "arbitrary")))(p, kv, kv, do)
    return dq, jnp.concatenate([dk, dv], axis=-1)


def _scan_const_specs(dk):
    C, L = SCAN_CHUNK, SCAN_LEVELS
    return [pl.BlockSpec((None, (2 + L) * C, C), lambda d, h, n: (d, 0, 0)),
            pl.BlockSpec((None, C, (2 + L) * C), lambda d, h, n: (d, 0, 0)),
            pl.BlockSpec((None, L * C, dk), lambda d, h, n: (d, 0, 0)),
            pl.BlockSpec((None, L * C, dk), lambda d, h, n: (d, 0, 0)),
            pl.BlockSpec((L * C, C), lambda d, h, n: (0, 0))]


def _scan_const_args():
    h, ht, qm, km, bm = _scan_consts()
    return [jnp.asarray(h, BF16), jnp.asarray(ht, BF16), jnp.asarray(qm, F32), jnp.asarray(km, F32), jnp.asarray(bm, F32)]


def scan_fwd(name, q, q_off, k2, g2, v, v_off, heads, dk, dv, T):
    C = SCAN_CHUNK
    N = T // C
    kd = k2.shape[0]
    assert dk == C
    qb, vb = q_off // dk, v_off // dv

    def chunk(d, n):
        return n + d * (N - 1 - 2 * n)

    def body(q_ref, k_ref, g_ref, v_ref, h_ref, ht_ref, qm_ref, km_ref, bm_ref, o_ref, ss_ref, st_ref):
        n = pl.program_id(2)

        @pl.when(n == 0)
        def _():
            st_ref[...] = jnp.zeros_like(st_ref)

        st = st_ref[...]
        ss_ref[...] = st
        o, st_new = scan_chunk(q_ref[...], k_ref[...], v_ref[...], g_ref[...], st, h_ref[...], ht_ref[...],
                               qm_ref[...], km_ref[...], bm_ref[...])
        o_ref[...] = o
        st_ref[...] = st_new

    return pl.pallas_call(
        body,
        out_shape=[jax.ShapeDtypeStruct((2, T, heads * dv), F32), jax.ShapeDtypeStruct((2, heads, N, dv, dk), F32)],
        grid=(2, heads, N),
        in_specs=[pl.BlockSpec((C, dk), lambda d, h, n: (chunk(d, n), qb + h)),
                  pl.BlockSpec((None, C, dk), lambda d, h, n: (d * (kd - 1), chunk(d, n), h)),
                  pl.BlockSpec((None, C, dk), lambda d, h, n: (d, chunk(d, n), h)),
                  pl.BlockSpec((C, dv), lambda d, h, n: (chunk(d, n), vb + h))] + _scan_const_specs(dk),
        out_specs=[pl.BlockSpec((None, C, dv), lambda d, h, n: (d, chunk(d, n), h)),
                   pl.BlockSpec((None, None, None, dv, dk), lambda d, h, n: (d, h, chunk(d, n), 0, 0))],
        scratch_shapes=[pltpu.VMEM((dv, dk), F32)],
        name=name, compiler_params=_cparams(("arbitrary", "arbitrary", "arbitrary")))(q, k2, g2, v, *_scan_const_args())


def scan_bwd(name, q, q_off, k2, g2, v, v_off, ss, do, heads, dk, dv, T):
    C = SCAN_CHUNK
    N = T // C
    kd = k2.shape[0]
    qb, vb = q_off // dk, v_off // dv

    def chunk(d, n):
        return (N - 1 - n) + d * (2 * n - N + 1)

    def body(q_ref, k_ref, g_ref, v_ref, ss_ref, do_ref, h_ref, ht_ref, qm_ref, km_ref, bm_ref,
             dq_ref, dk_ref, dg_ref, dv_ref, dst_ref):
        n = pl.program_id(2)

        @pl.when(n == 0)
        def _():
            dst_ref[...] = jnp.zeros_like(dst_ref)

        consts = (h_ref[...], ht_ref[...], qm_ref[...], km_ref[...], bm_ref[...])
        _, vjp = jax.vjp(lambda q_, k_, v_, g_, st_: scan_chunk(q_, k_, v_, g_, st_, *consts),
                         q_ref[...], k_ref[...], v_ref[...], g_ref[...], ss_ref[...])
        dq, dk_, dv_, dg, dst = vjp((do_ref[...], dst_ref[...]))
        dq_ref[...] = dq
        dk_ref[...] = dk_
        dg_ref[...] = dg
        dv_ref[...] = dv_
        dst_ref[...] = dst

    kspec = pl.BlockSpec((None, C, dk), lambda d, h, n: (d, chunk(d, n), h))
    vspec = pl.BlockSpec((None, C, dv), lambda d, h, n: (d, chunk(d, n), h))
    return pl.pallas_call(
        body,
        out_shape=[jax.ShapeDtypeStruct((2, T, heads * dk), F32)] * 3 + [jax.ShapeDtypeStruct((2, T, heads * dv), F32)],
        grid=(2, heads, N),
        in_specs=[pl.BlockSpec((C, dk), lambda d, h, n: (chunk(d, n), qb + h)),
                  pl.BlockSpec((None, C, dk), lambda d, h, n: (d * (kd - 1), chunk(d, n), h)),
                  kspec,
                  pl.BlockSpec((C, dv), lambda d, h, n: (chunk(d, n), vb + h)),
                  pl.BlockSpec((None, None, None, dv, dk), lambda d, h, n: (d, h, chunk(d, n), 0, 0)),
                  pl.BlockSpec((C, dv), lambda d, h, n: (chunk(d, n), h))] + _scan_const_specs(dk),
        out_specs=[kspec, kspec, kspec, vspec],
        scratch_shapes=[pltpu.VMEM((dv, dk), F32)],
        name=name, compiler_params=_cparams(("arbitrary", "arbitrary", "arbitrary")))(
            q, k2, g2, v, ss, do, *_scan_const_args())


def final_call(x, g, target, T):
    tr = _row_tile(T)

    def tile(xv, gv, tv):
        y = _rms(xv, gv)
        err = (y - tv) ** 2
        return jnp.sum(jnp.sum(err, axis=-1, keepdims=True), axis=0, keepdims=True) * (0.5 / D_MODEL)

    def body(x_ref, g_ref, t_ref, loss_ref, dx_ref, dg_ref):
        i = pl.program_id(0)
        tv = t_ref[...]
        lv, vjp = jax.vjp(lambda a, b: tile(a, b, tv), x_ref[...], g_ref[...])
        dx, dg = vjp(jnp.ones((1, 1), F32))
        dx_ref[...] = dx

        @pl.when(i == 0)
        def _():
            loss_ref[...] = jnp.zeros_like(loss_ref)
            dg_ref[...] = jnp.zeros_like(dg_ref)

        loss_ref[...] += jnp.broadcast_to(lv, loss_ref.shape)
        dg_ref[...] += dg

    return pl.pallas_call(
        body,
        out_shape=[jax.ShapeDtypeStruct((8, 128), F32), jax.ShapeDtypeStruct((T, D_MODEL), F32),
                   jax.ShapeDtypeStruct((1, D_MODEL), F32)],
        grid=(T // tr,),
        in_specs=[pl.BlockSpec((tr, D_MODEL), lambda i: (i, 0)), pl.BlockSpec((1, D_MODEL), lambda i: (0, 0)),
                  pl.BlockSpec((tr, D_MODEL), lambda i: (i, 0))],
        out_specs=[pl.BlockSpec((8, 128), lambda i: (0, 0)), pl.BlockSpec((tr, D_MODEL), lambda i: (i, 0)),
                   pl.BlockSpec((1, D_MODEL), lambda i: (0, 0))],
        name="final_loss", compiler_params=_cparams(("arbitrary",)))(x, g, target)


def adamw_call(w, g, m, v):
    shape = w.shape
    c = shape[-1]
    r = int(np.prod(shape[:-1])) if len(shape) > 1 else 1
    tr = r if r <= 256 else 256
    assert r % tr == 0

    def body(w_ref, g_ref, m_ref, v_ref, d_ref, nm_ref, nv_ref):
        gv = g_ref[...]
        nm = ADAM_B1 * m_ref[...] + (1.0 - ADAM_B1) * gv
        nv = ADAM_B2 * v_ref[...] + (1.0 - ADAM_B2) * jnp.square(gv)
        m_hat = nm / (1.0 - ADAM_B1 ** ADAM_STEP)
        v_hat = nv / (1.0 - ADAM_B2 ** ADAM_STEP)
        d_ref[...] = -ADAM_LR * (m_hat / (jnp.sqrt(v_hat) + ADAM_EPS) + ADAM_WD * w_ref[...])
        nm_ref[...] = nm
        nv_ref[...] = nv

    spec = pl.BlockSpec((tr, c), lambda i: (i, 0))
    outs = pl.pallas_call(body, out_shape=[jax.ShapeDtypeStruct((r, c), F32)] * 3, grid=(r // tr,),
                          in_specs=[spec] * 4, out_specs=[spec] * 3, name="adamw",
                          compiler_params=_cparams(("arbitrary",)))(*(t.reshape(r, c) for t in (w, g, m, v)))
    return tuple(o.reshape(shape) for o in outs)


def sum_devices(g64):
    def body(x_ref, o_ref):
        acc = x_ref[0:8, :]
        for d in range(1, 8):
            acc = acc + x_ref[8 * d:8 * d + 8, :]
        o_ref[...] = acc

    return pl.pallas_call(body, out_shape=jax.ShapeDtypeStruct((8, D_MODEL), F32), name="sum_devices")(g64)


def _half_tile(rh):
    return rh if rh <= 512 else 256


def add_sibling(g, recv, c, out_dtype):
    _, R, C = g.shape
    rh = R // 2
    tr = _half_tile(rh)
    nblk = rh // tr

    def body(c_ref, g_ref, r_ref, o_ref):
        o_ref[...] = (g_ref[...] + r_ref[...]).astype(o_ref.dtype)

    grid_spec = pltpu.PrefetchScalarGridSpec(
        num_scalar_prefetch=1, grid=(4, nblk),
        in_specs=[pl.BlockSpec((None, tr, C), lambda j, i, c_ref: (j, i + c_ref[0] * nblk, 0)),
                  pl.BlockSpec((None, tr, C), lambda j, i, c_ref: (j, i, 0))],
        out_specs=pl.BlockSpec((None, tr, C), lambda j, i, c_ref: (j, i, 0)))
    return pl.pallas_call(body, out_shape=jax.ShapeDtypeStruct((4, rh, C), out_dtype), grid_spec=grid_spec,
                          name="rs_add_sibling", compiler_params=_cparams(("arbitrary", "arbitrary")))(c, g, recv)


def add_chips(g, recv, r3, place):
    _, R, C = g.shape
    rh = R // 2
    tr = _half_tile(rh)
    nblk = rh // tr

    def body(p_ref, g_ref, s_ref, a_ref, b_ref, c_ref, o_ref):
        up = lambda r: r[...].astype(F32)
        o_ref[...] = (((g_ref[...] + up(s_ref)) + up(a_ref)) + up(b_ref)) + up(c_ref)

    grid_spec = pltpu.PrefetchScalarGridSpec(
        num_scalar_prefetch=1, grid=(nblk,),
        in_specs=[pl.BlockSpec((None, tr, C), lambda i, p_ref: (p_ref[0], i + p_ref[1] * nblk, 0)),
                  pl.BlockSpec((None, tr, C), lambda i, p_ref: (p_ref[0], i, 0))]
        + [pl.BlockSpec((None, tr, C), functools.partial(lambda i, p_ref, k: (k, i, 0), k=k)) for k in range(3)],
        out_specs=pl.BlockSpec((tr, C), lambda i, p_ref: (i + p_ref[1] * nblk, 0)))
    return pl.pallas_call(body, out_shape=jax.ShapeDtypeStruct((R, C), F32), grid_spec=grid_spec,
                          name="rs_add_chips", compiler_params=_cparams(("arbitrary",)))(place, g, recv, r3, r3, r3)


def _remote(src, dst, ssem, rsem, dev):
    return pltpu.make_async_remote_copy(src_ref=src, dst_ref=dst, send_sem=ssem, recv_sem=rsem,
                                        device_id=dev, device_id_type=pl.DeviceIdType.MESH)


def _mesh_places():
    x, y, c = lax.axis_index("x"), lax.axis_index("y"), lax.axis_index("c")
    chips = [(1 - x, y), (x, 1 - y), (1 - x, 1 - y)]
    return x, y, c, (x, y, 1 - c), chips


def _hbm_specs(n):
    return [pl.BlockSpec(memory_space=pltpu.HBM) for _ in range(n)]


def gather_weights(shards, small):
    nb = len(shards)

    def body(*refs):
        ins, outs = refs[:nb + 1], refs[nb + 1:2 * nb + 2]
        send_sems, recv_sems = refs[2 * nb + 2:]
        x, y, c, sibling, chips = _mesh_places()
        mine = 2 * x + y

        def half(a, chip_idx, which):
            rh = ins[a].shape[0] // 2
            return outs[a].at[chip_idx, pl.ds(which * rh, rh), :]

        sent = []
        for a in range(nb):
            rh = ins[a].shape[0] // 2
            src = ins[a].at[pl.ds(c * rh, rh), :]
            for k, chip in enumerate(chips):
                sent.append(_remote(src, half(a, mine, c), send_sems.at[a, k], recv_sems.at[a, k], (*chip, c)))
        for k, chip in enumerate(chips):
            sent.append(_remote(ins[nb], outs[nb].at[mine], send_sems.at[nb, k], recv_sems.at[nb, k], (*chip, c)))
        for cp in sent:
            cp.start()
        for a in range(nb):
            for k, chip in enumerate(chips):
                region = half(a, 2 * chip[0] + chip[1], c)
                _remote(region, region, send_sems.at[a, k], recv_sems.at[a, k], (*chip, c)).wait_recv()
                fwd = _remote(region, region, send_sems.at[a, 3 + k], recv_sems.at[a, 3 + k], sibling)
                fwd.start()
                sent.append(fwd)
        for k, chip in enumerate(chips):
            region = outs[nb].at[2 * chip[0] + chip[1]]
            _remote(region, region, send_sems.at[nb, k], recv_sems.at[nb, k], (*chip, c)).wait_recv()
        for a in range(nb):
            for k, chip in enumerate(chips):
                region = half(a, 2 * chip[0] + chip[1], 1 - c)
                _remote(region, region, send_sems.at[a, 3 + k], recv_sems.at[a, 3 + k], sibling).wait_recv()
        for cp in sent:
            cp.wait_send()

    arrs = list(shards) + [small]
    return pl.pallas_call(
        body, out_shape=[jax.ShapeDtypeStruct((4,) + a.shape, a.dtype) for a in arrs],
        in_specs=_hbm_specs(nb + 1), out_specs=_hbm_specs(nb + 1),
        scratch_shapes=[pltpu.SemaphoreType.DMA((nb + 1, 6)), pltpu.SemaphoreType.DMA((nb + 1, 6))],
        name="gather_weights")(*arrs)


def rs_exchange_siblings(gs):
    n = len(gs)

    def body(*refs):
        ins, outs = refs[:n], refs[n:2 * n]
        send_sems, recv_sems = refs[2 * n:]
        x, y, c, sibling, chips = _mesh_places()
        cps = []
        for a in range(n):
            rh = ins[a].shape[1] // 2
            cps.append(_remote(ins[a].at[:, pl.ds((1 - c) * rh, rh), :], outs[a], send_sems.at[a], recv_sems.at[a], sibling))
        for cp in cps:
            cp.start()
        for cp in cps:
            cp.wait()

    return pl.pallas_call(
        body, out_shape=[jax.ShapeDtypeStruct((4, g.shape[1] // 2, g.shape[2]), g.dtype) for g in gs],
        in_specs=_hbm_specs(n), out_specs=_hbm_specs(n),
        scratch_shapes=[pltpu.SemaphoreType.DMA((n,)), pltpu.SemaphoreType.DMA((n,))],
        name="rs_exchange_siblings")(*gs)


def rs_exchange_chips(s1s):
    n = len(s1s)

    def body(*refs):
        ins, outs = refs[:n], refs[n:2 * n]
        send_sems, recv_sems = refs[2 * n:]
        x, y, c, sibling, chips = _mesh_places()
        cps = []
        for a in range(n):
            for k, chip in enumerate(chips):
                cps.append(_remote(ins[a].at[2 * chip[0] + chip[1]], outs[a].at[k], send_sems.at[a, k],
                                   recv_sems.at[a, k], (*chip, c)))
        for cp in cps:
            cp.start()
        for cp in cps:
            cp.wait()

    return pl.pallas_call(
        body, out_shape=[jax.ShapeDtypeStruct((3,) + s.shape[1:], s.dtype) for s in s1s],
        in_specs=_hbm_specs(n), out_specs=_hbm_specs(n),
        scratch_shapes=[pltpu.SemaphoreType.DMA((n, 3)), pltpu.SemaphoreType.DMA((n, 3))],
        name="rs_exchange_chips")(*s1s)


def rs_share_final(fs):
    n = len(fs)

    def body(*refs):
        bufs = refs[n:2 * n]
        send_sems, recv_sems = refs[2 * n:]
        x, y, c, sibling, chips = _mesh_places()
        cps = []
        for a in range(n):
            rh = bufs[a].shape[0] // 2
            mine = bufs[a].at[pl.ds(c * rh, rh), :]
            cps.append(_remote(mine, mine, send_sems.at[a], recv_sems.at[a], sibling))
        for cp in cps:
            cp.start()
        for a in range(n):
            rh = bufs[a].shape[0] // 2
            other = bufs[a].at[pl.ds((1 - c) * rh, rh), :]
            _remote(other, other, send_sems.at[a], recv_sems.at[a], sibling).wait_recv()
        for cp in cps:
            cp.wait_send()

    return pl.pallas_call(
        body, out_shape=[jax.ShapeDtypeStruct(f.shape, f.dtype) for f in fs],
        in_specs=_hbm_specs(n), out_specs=_hbm_specs(n), input_output_aliases={a: a for a in range(n)},
        scratch_shapes=[pltpu.SemaphoreType.DMA((n,)), pltpu.SemaphoreType.DMA((n,))],
        name="rs_share_final")(*fs)


def allgather_small(v):
    m_per = v.shape[0]

    def body(x_ref, out_ref, send_sems, recv_sems, local_sem):
        x, y, c, sibling, chips = _mesh_places()
        me = (x, y, c)

        def rows(px, py, pc):
            return out_ref.at[pl.ds((4 * px + 2 * py + pc) * m_per, m_per), :]

        def copy(k, block, to, src=None):
            return _remote(rows(*block) if src is None else src, rows(*block), send_sems.at[k], recv_sems.at[k], to)

        mine = pltpu.make_async_copy(x_ref, rows(*me), local_sem)
        mine.start()
        first = [copy(0, me, sibling, src=x_ref)]
        first += [copy(1 + j, me, (*chip, c), src=x_ref) for j, chip in enumerate(chips)]
        for cp in first:
            cp.start()
        passed = [copy(4 + j, (*chip, c), sibling) for j, chip in enumerate(chips)]
        for j, chip in enumerate(chips):
            copy(1 + j, (*chip, c), me).wait_recv()
            passed[j].start()
        copy(0, sibling, me).wait_recv()
        for j, chip in enumerate(chips):
            copy(4 + j, (*chip, 1 - c), me).wait_recv()
        for cp in first + passed:
            cp.wait_send()
        mine.wait()

    return pl.pallas_call(
        body, out_shape=jax.ShapeDtypeStruct((8 * m_per, v.shape[1]), v.dtype),
        in_specs=[pl.BlockSpec(memory_space=pltpu.VMEM)], out_specs=pl.BlockSpec(memory_space=pltpu.VMEM),
        scratch_shapes=[pltpu.SemaphoreType.DMA((7,)), pltpu.SemaphoreType.DMA((7,)), pltpu.SemaphoreType.DMA],
        name="allgather_small")(v)


def rms_res_tile(x, g):
    return (_rms(x, g), x)


def _lower_bounds(lb_param):
    lbs = jax.nn.softmax(lb_param.astype(F32), axis=0)
    return jnp.cumsum(lbs, axis=0) - lbs[0]


def _heads_major(t, n):
    return t.reshape(t.shape[0], n, HEAD_DIM_A).transpose(1, 0, 2)


def _heads_minor(t):
    return t.transpose(1, 0, 2).reshape(t.shape[1], t.shape[0] * t.shape[2])


def _even_fwd(x, i, W, lower, kv, slopes, T):
    O = EVEN_OFF
    g = W["norm_even"][i].reshape(1, D_MODEL)
    (h,) = rows_call("rms_fwd", rms_tile, T, [("row", x, 0, D_MODEL), ("full", g)], [D_MODEL], [BF16])
    p = matmul("mm_in_e", h, W["w_in_e"][i], "nn")
    q8 = _heads_major(p[:, O["qA"]:O["qA"] + W_A], N_Q_A)
    pad = lambda t: jnp.pad(_heads_major(t, N_KV_A), ((0, 0), (BLOCK, BLOCK), (0, 0)))
    k2p = pad(p[:, O["kA"]:O["kA"] + W_KV_A])
    v2p = pad(p[:, O["vA"]:O["vA"] + W_KV_A])
    sink = W["sink"][i].reshape(N_Q_A, 1, 1)
    a = _heads_minor(attn_fwd(q8, k2p, v2p, sink, slopes, T))
    prep_ins = [("row", p, O["qB"], W_B), ("row", p, O["zf"], W_B), ("row", p, O["zb"], W_B),
                ("full", lower[i][0:1]), ("full", lower[i][1:2])]
    qh, k2, g2 = rows_call("hgrn_prep_fwd", hgrn_prep_tile, T, prep_ins, [W_B] * 3, stacks=[(0,), (1, 2), (3, 4)])
    o2, ss = scan_fwd("scan_fwd_h", qh, 0, k2, g2, p, O["iB"], N_HEADS_B, HEAD_DIM_B, HEAD_DIM_B, T)
    mo = mem_fwd(p, O["qM"], kv, T)
    hg = W["hgrn_norm"][i].reshape(1, W_B)
    post_ins = [("row", a, 0, W_A), ("row3", o2, 0, 0, W_B), ("row3", o2, 1, 0, W_B), ("row", mo, 0, W_M),
                ("row", p, O["gA"], W_A), ("row", p, O["gB"], W_B), ("row", p, O["gM"], W_M), ("full", hg)]
    (mix,) = rows_call("even_post_fwd", even_post_tile, T, post_ins, [MIX], [BF16])
    x_new = matmul("mm_out", mix, W["w_out_e"][i], "nn", add=x)
    return x_new, dict(x=x, g=g, h=h, p=p, q8=q8, k2p=k2p, v2p=v2p, sink=sink, prep_ins=prep_ins, qh=qh, k2=k2,
                       g2=g2, ss=ss, post_ins=post_ins, mix=mix)


def _assemble_even(dqA, dgA, dqB, dzf, dzb, dv0, dv1, dgB, dqM, dgM, dkA, dvA):
    return (jnp.concatenate([dqA, dgA, dqB, dzf, dzb, dv0 + dv1, dgB, dqM, dgM, dkA, dvA], axis=-1),)


def _even_bwd(dxo, sv, i, W, kv, slopes, T):
    O = EVEN_OFF
    p = sv["p"]
    dmix = matmul("mm_dmix", dxo, W["w_out_e"][i], "nt")
    dwo = matmul("mm_dwo", sv["mix"], dxo, "tn")
    da, dof, dmo, dgA, dgB, dgM, dhg = rows_vjp_call("even_post_bwd", even_post_tile, T, sv["post_ins"],
                                                      [[("row", dmix, 0, MIX)]], skip=(2,))
    dq8, dk2p, dv2p, dsink = attn_bwd(sv["q8"], sv["k2p"], sv["v2p"], sv["sink"], slopes, _heads_major(da, N_Q_A), T)
    dqA = _heads_minor(dq8)
    dkA = _heads_minor(dk2p[:, BLOCK:-BLOCK])
    dvA = _heads_minor(dv2p[:, BLOCK:-BLOCK])
    dq2, dk2, dg2, dv2 = scan_bwd("scan_bwd_h", sv["qh"], 0, sv["k2"], sv["g2"], p, O["iB"], sv["ss"], dof,
                                  N_HEADS_B, HEAD_DIM_B, HEAD_DIM_B, T)
    r3 = lambda arr, d: ("row3", arr, d, 0, W_B)
    dqB, dzf, dzb, dlow_f, dlow_b = rows_vjp_call(
        "hgrn_prep_bwd", hgrn_prep_tile, T, sv["prep_ins"],
        [[r3(dq2, 0), r3(dq2, 1)], [r3(dk2, 0)], [r3(dk2, 1)], [r3(dg2, 0)], [r3(dg2, 1)]])
    dlow = jnp.concatenate([dlow_f, dlow_b], axis=0)
    dqM, dkv = mem_bwd(p, O["qM"], kv, dmo, T)
    row = lambda arr, w: ("row", arr, 0, w)
    (dp,) = rows_call("even_dp", _assemble_even, T,
                      [row(dqA, W_A), row(dgA, W_A), row(dqB, W_B), row(dzf, W_B), row(dzb, W_B), r3(dv2, 0), r3(dv2, 1),
                       row(dgB, W_B), row(dqM, W_M), row(dgM, W_M), row(dkA, W_KV_A), row(dvA, W_KV_A)],
                      [EVEN_IN], [BF16])
    dh = matmul("mm_dh_e", dp, W["w_in_e"][i], "nt")
    dwi = matmul("mm_dwi_e", sv["h"], dp, "tn")
    dx, dg = rows_vjp_call("rms_res_bwd", rms_res_tile, T, [("row", sv["x"], 0, D_MODEL), ("full", sv["g"])],
                           [[("row", dh, 0, D_MODEL)], [("row", dxo, 0, D_MODEL)]])
    return dx, dict(w_in=dwi, w_out=dwo, norm=dg[0], sink=dsink.reshape(N_Q_A), low=dlow, hg=dhg[0], kv=dkv)


def _pad_gate_up(w_up):
    z = jnp.zeros((2, 128, WK_C), F32)
    z = z.at[0, 0:GATE_RANK].set(w_up[0])
    return z.at[1, GATE_RANK:2 * GATE_RANK].set(w_up[1])


def _odd_fwd(x, i, W, kv, T):
    O = ODD_OFF
    g = W["norm_odd"][i].reshape(1, D_MODEL)
    (h,) = rows_call("rms_fwd", rms_tile, T, [("row", x, 0, D_MODEL), ("full", g)], [D_MODEL], [BF16])
    p = matmul("mm_in_o", h, W["w_in_o"][i], "nn")
    wup = _pad_gate_up(W["w_gate_up"][i])
    prep_ins = [("row", p, O["qC"], WK_C), ("row", p, O["rr"], 128), ("full", wup[0]), ("full", wup[1]),
                ("full", W["b_gate"][i][0:1]), ("full", W["b_gate"][i][1:2])]
    qg, g2 = rows_call("gla_prep_fwd", gla_prep_tile, T, prep_ins, [WK_C] * 2, stacks=[(0,), (1, 2)])
    k2 = p[None, :, O["kC"]:O["kC"] + WK_C]
    o2, ss = scan_fwd("scan_fwd_g", qg, 0, k2, g2, p, O["vC"], N_HEADS_C, DK_C, DV_C, T)
    mo = mem_fwd(p, O["qM"], kv, T)
    gg = W["gla_norm"][i].reshape(1, WV_C)
    post_ins = [("row3", o2, 0, 0, WV_C), ("row3", o2, 1, 0, WV_C), ("row", mo, 0, W_M),
                ("row", p, O["gC"], WV_C), ("row", p, O["gM"], W_M), ("full", gg)]
    (mix,) = rows_call("odd_post_fwd", odd_post_tile, T, post_ins, [MIX], [BF16])
    x_new = matmul("mm_out", mix, W["w_out_o"][i], "nn", add=x)
    return x_new, dict(x=x, g=g, h=h, p=p, prep_ins=prep_ins, qg=qg, k2=k2, g2=g2, ss=ss, post_ins=post_ins, mix=mix)


def _assemble_odd(dqC, dk0, dk1, dv0, dv1, dgC, dqM, dgM, dr):
    return (jnp.concatenate([dqC, dk0 + dk1, dv0 + dv1, dgC, dqM, dgM, dr], axis=-1),)


def _odd_bwd(dxo, sv, i, W, kv, T):
    O = ODD_OFF
    p = sv["p"]
    dmix = matmul("mm_dmix", dxo, W["w_out_o"][i], "nt")
    dwo = matmul("mm_dwo", sv["mix"], dxo, "tn")
    dof, dmo, dgC, dgM, dgg = rows_vjp_call("odd_post_bwd", odd_post_tile, T, sv["post_ins"],
                                            [[("row", dmix, 0, MIX)]], skip=(1,))
    dq2, dk2, dg2, dv2 = scan_bwd("scan_bwd_g", sv["qg"], 0, sv["k2"], sv["g2"], p, O["vC"], sv["ss"], dof,
                                  N_HEADS_C, DK_C, DV_C, T)
    r3 = lambda arr, d, w: ("row3", arr, d, 0, w)
    dqC, dr, dwup_f, dwup_b, dbg_f, dbg_b = rows_vjp_call(
        "gla_prep_bwd", gla_prep_tile, T, sv["prep_ins"],
        [[r3(dq2, 0, WK_C), r3(dq2, 1, WK_C)], [r3(dg2, 0, WK_C)], [r3(dg2, 1, WK_C)]])
    dqM, dkv = mem_bwd(p, O["qM"], kv, dmo, T)
    row = lambda arr, w: ("row", arr, 0, w)
    (dp,) = rows_call("odd_dp", _assemble_odd, T,
                      [row(dqC, WK_C), r3(dk2, 0, WK_C), r3(dk2, 1, WK_C), r3(dv2, 0, WV_C), r3(dv2, 1, WV_C),
                       row(dgC, WV_C), row(dqM, W_M), row(dgM, W_M), row(dr, 128)],
                      [ODD_PAD], [BF16])
    dh = matmul("mm_dh_o", dp, W["w_in_o"][i], "nt")
    dwi = matmul("mm_dwi_o", sv["h"], dp, "tn")
    dx, dg = rows_vjp_call("rms_res_bwd", rms_res_tile, T, [("row", sv["x"], 0, D_MODEL), ("full", sv["g"])],
                           [[("row", dh, 0, D_MODEL)], [("row", dxo, 0, D_MODEL)]])
    dw_up = jnp.stack([dwup_f[0:GATE_RANK], dwup_b[GATE_RANK:2 * GATE_RANK]])
    dbg = jnp.concatenate([dbg_f, dbg_b], axis=0)
    return dx, dict(w_in=dwi, w_out=dwo, norm=dg[0], w_up=dw_up, b_gate=dbg, gg=dgg[0], kv=dkv)


def local_step(x, mem, target, W):
    T = x.shape[0]
    slopes = (2.0 ** (-8.0 * jnp.arange(1, N_Q_A + 1, dtype=F32) / N_Q_A)).reshape(N_Q_A, 1, 1)
    lower, lower_vjp = jax.vjp(_lower_bounds, W["lb_param"])
    mem_g = W["mem_norm"].reshape(1, D_MODEL)
    (mem_n,) = rows_call("mem_rms_fwd", rms_tile, N_MEM, [("row", mem, 0, D_MODEL), ("full", mem_g)], [D_MODEL], [BF16])
    kvs = [matmul("mm_kv", mem_n, W["w_kv"][l], "nn") for l in range(DEPTH)]
    saved = []
    for l in range(DEPTH):
        if l % 2 == 0:
            x, sv = _even_fwd(x, l // 2, W, lower, kvs[l], slopes, T)
        else:
            x, sv = _odd_fwd(x, l // 2, W, kvs[l], T)
        saved.append(sv)
    loss, dx, dgf = final_call(x, W["final_norm"].reshape(1, D_MODEL), target, T)
    per = [None] * DEPTH
    for l in reversed(range(DEPTH)):
        if l % 2 == 0:
            dx, per[l] = _even_bwd(dx, saved[l], l // 2, W, kvs[l], slopes, T)
        else:
            dx, per[l] = _odd_bwd(dx, saved[l], l // 2, W, kvs[l], T)
    dmem_n, dw_kv = None, []
    for l in range(DEPTH):
        dw_kv.append(matmul("mm_dwkv", mem_n, per[l]["kv"], "tn"))
        dmem_n = matmul("mm_dmem", per[l]["kv"], W["w_kv"][l], "nt", add=dmem_n)
    (dmem_norm,) = rows_vjp_call("mem_rms_bwd", rms_tile, N_MEM, [("row", mem, 0, D_MODEL), ("full", mem_g)],
                                 [[("row", dmem_n, 0, D_MODEL)]], skip=(0,))
    ev, od = (per[0], per[2]), (per[1], per[3])
    (d_lb,) = lower_vjp(jnp.stack([e["low"] for e in ev]))
    grads = dict(
        w_in_e=jnp.stack([e["w_in"] for e in ev]), w_in_o=jnp.stack([o["w_in"] for o in od]),
        w_out_e=jnp.stack([e["w_out"] for e in ev]), w_out_o=jnp.stack([o["w_out"] for o in od]),
        w_kv=jnp.stack(dw_kv), norm_even=jnp.stack([e["norm"] for e in ev]), sink=jnp.stack([e["sink"] for e in ev]),
        lb_param=d_lb, hgrn_norm=jnp.stack([e["hg"] for e in ev]), norm_odd=jnp.stack([o["norm"] for o in od]),
        w_gate_up=jnp.stack([o["w_up"] for o in od]), b_gate=jnp.stack([o["b_gate"] for o in od]),
        gla_norm=jnp.stack([o["gg"] for o in od]), mem_norm=dmem_norm[0], final_norm=dgf[0])
    return loss, dx, grads


SMALL_SPECS = (("lb_param", (2, 2, 128)), ("norm_odd", (2, 256)), ("w_gate_up", (2, 2, 16, 128)),
               ("b_gate", (2, 2, 128)), ("gla_norm", (2, 256)))
SMALL_ROWS = 80


def _pack_small_local(d):
    return jnp.concatenate([d[n].reshape(-1) for n, _ in SMALL_SPECS]).reshape(SMALL_ROWS, 128)


def _unpack_small_local(b):
    flat, out, o = b.reshape(-1), {}, 0
    for n, shp in SMALL_SPECS:
        sz = int(np.prod(shp))
        out[n] = flat[o:o + sz].reshape(shp)
        o += sz
    return out


def _unpack_small_full(g4):
    per = [_unpack_small_local(g4[j]) for j in range(4)]
    return {n: jnp.concatenate([per[j][n] for j in range(4)], axis=-1) for n, _ in SMALL_SPECS}


def _pack_small_blocks(full):
    blocks = []
    for j in range(4):
        blocks.append(_pack_small_local({n: full[n][..., j * shp[-1]:(j + 1) * shp[-1]] for n, shp in SMALL_SPECS}))
    return jnp.stack(blocks)


def _cols(t, order, off, widths):
    return [t[..., off[n]:off[n] + widths[n]] for n in order]


EVEN_REF_ORDER = ("qA", "kA", "vA", "gA", "qB", "zf", "zb", "iB", "gB", "qM", "gM")
ODD_REF_ORDER = ("qC", "kC", "vC", "gC", "rr", "qM", "gM")


def _full_weights(gathered, gsmall, rep):
    g_in_e, g_in_o, g_out_e, g_out_o, g_kv = gathered
    t = g_in_e.reshape(4, 2, D_MODEL, EVEN_IN // 4).transpose(1, 2, 0, 3).reshape(2, D_MODEL, EVEN_IN)
    w_in_e = jnp.concatenate(_cols(t, EVEN_ORDER, EVEN_REF_OFF, EVEN_W), axis=-1)
    t = g_in_o.reshape(4, 2, D_MODEL, ODD_IN // 4).transpose(1, 2, 0, 3).reshape(2, D_MODEL, ODD_IN)
    w_in_o = jnp.concatenate(_cols(t, ODD_ORDER, ODD_REF_OFF, ODD_W) + [jnp.zeros((2, D_MODEL, ODD_PAD - ODD_IN), BF16)],
                             axis=-1)
    blocks_to_rows = lambda g, n: g.reshape(4, n, g.shape[1] // n, g.shape[2]).transpose(1, 0, 2, 3).reshape(
        n, 4 * (g.shape[1] // n), g.shape[2])
    W = dict(w_in_e=w_in_e, w_in_o=w_in_o, w_out_e=blocks_to_rows(g_out_e, 2), w_out_o=blocks_to_rows(g_out_o, 2),
             w_kv=blocks_to_rows(g_kv, DEPTH))
    W.update(_unpack_small_full(gsmall))
    W.update(rep)
    return W


def _grad_blocks(grads):
    t = jnp.concatenate(_cols(grads["w_in_e"], EVEN_REF_ORDER, EVEN_OFF, EVEN_W), axis=-1)
    b_in_e = t.reshape(2, D_MODEL, 4, EVEN_IN // 4).transpose(2, 0, 1, 3).reshape(4, 2 * D_MODEL, EVEN_IN // 4)
    t = jnp.concatenate(_cols(grads["w_in_o"], ODD_REF_ORDER, ODD_OFF, ODD_W), axis=-1)
    b_in_o = t.reshape(2, D_MODEL, 4, ODD_IN // 4).transpose(2, 0, 1, 3).reshape(4, 2 * D_MODEL, ODD_IN // 4)
    rows_to_blocks = lambda g: g.reshape(g.shape[0], 4, g.shape[1] // 4, g.shape[2]).transpose(1, 0, 2, 3).reshape(
        4, g.shape[0] * (g.shape[1] // 4), g.shape[2])
    return [b_in_e, b_in_o, rows_to_blocks(grads["w_out_e"]), rows_to_blocks(grads["w_out_o"]),
            rows_to_blocks(grads["w_kv"]), _pack_small_blocks(grads)]


WEIGHT_NAMES = ("norm_even", "w_in_even", "sink", "lb_param", "hgrn_norm", "w_out_even", "norm_odd", "w_in_odd",
                "w_gate_up", "b_gate", "gla_norm", "w_out_odd", "mem_norm", "w_mem_kv", "final_norm")


def kernel(x, mem, norm_even, w_in_even, sink, lb_param, hgrn_norm, w_out_even, norm_odd, w_in_odd, w_gate_up, b_gate, gla_norm, w_out_odd, mem_norm, w_mem_kv, final_norm, loss_target, m_norm_even, m_w_in_even, m_sink, m_lb_param, m_hgrn_norm, m_w_out_even, m_norm_odd, m_w_in_odd, m_w_gate_up, m_b_gate, m_gla_norm, m_w_out_odd, m_mem_norm, m_w_mem_kv, m_final_norm, v_norm_even, v_w_in_even, v_sink, v_lb_param, v_hgrn_norm, v_w_out_even, v_norm_odd, v_w_in_odd, v_w_gate_up, v_b_gate, v_gla_norm, v_w_out_odd, v_mem_norm, v_w_mem_kv, v_final_norm):
    w = dict(zip(WEIGHT_NAMES, (norm_even, w_in_even, sink, lb_param, hgrn_norm, w_out_even, norm_odd, w_in_odd,
                                w_gate_up, b_gate, gla_norm, w_out_odd, mem_norm, w_mem_kv, final_norm)))
    m = dict(zip(WEIGHT_NAMES, (m_norm_even, m_w_in_even, m_sink, m_lb_param, m_hgrn_norm, m_w_out_even, m_norm_odd,
                                m_w_in_odd, m_w_gate_up, m_b_gate, m_gla_norm, m_w_out_odd, m_mem_norm, m_w_mem_kv,
                                m_final_norm)))
    v = dict(zip(WEIGHT_NAMES, (v_norm_even, v_w_in_even, v_sink, v_lb_param, v_hgrn_norm, v_w_out_even, v_norm_odd,
                                v_w_in_odd, v_w_gate_up, v_b_gate, v_gla_norm, v_w_out_odd, v_mem_norm, v_w_mem_kv,
                                v_final_norm)))
    ci = lax.axis_index("c").astype(jnp.int32).reshape(1)
    chip = (2 * lax.axis_index("x") + lax.axis_index("y")).astype(jnp.int32).reshape(1)

    flat2 = lambda t: t.reshape(-1, t.shape[-1])
    shards = [flat2(w[n]).astype(BF16) for n in ("w_in_even", "w_in_odd", "w_out_even", "w_out_odd", "w_mem_kv")]
    small = _pack_small_local(w)
    remote = gather_weights(shards, small)
    own = lambda g, s: lax.dynamic_update_slice(g, s[None], (chip[0], 0, 0))
    *gathered, gsmall = [own(g, s) for g, s in zip(remote, shards + [small])]
    rep = {n: w[n] for n in ("norm_even", "sink", "hgrn_norm", "mem_norm", "final_norm")}
    W = _full_weights(gathered, gsmall, rep)

    loss_tile, dx, grads = local_step(x[0], mem[0], loss_target[0], W)

    blocks = _grad_blocks(grads)
    recv = rs_exchange_siblings(blocks)
    wire = [BF16] * (len(blocks) - 1) + [F32]
    chip_sums = [add_sibling(g, r, ci, dt) for g, r, dt in zip(blocks, recv, wire)]
    recv3 = rs_exchange_chips(chip_sums)
    place = jnp.concatenate([chip, ci])
    halves = [add_chips(g, r, r3, place) for g, r, r3 in zip(blocks, recv, recv3)]
    g_in_e, g_in_o, g_out_e, g_out_o, g_kv, g_small = rs_share_final(halves)
    gl = _unpack_small_local(g_small)
    gl.update(w_in_even=g_in_e.reshape(w_in_even.shape), w_in_odd=g_in_o.reshape(w_in_odd.shape),
              w_out_even=g_out_e.reshape(w_out_even.shape), w_out_odd=g_out_o.reshape(w_out_odd.shape),
              w_mem_kv=g_kv.reshape(w_mem_kv.shape))

    pack = jnp.zeros((8, D_MODEL), F32)
    pack = pack.at[0:2].set(grads["norm_even"]).at[2].set(grads["hgrn_norm"].reshape(-1))
    pack = pack.at[3].set(grads["mem_norm"]).at[4].set(grads["final_norm"])
    pack = pack.at[5, 0:16].set(grads["sink"].reshape(-1)).at[5, 16].set(loss_tile[0, 0])
    tot = sum_devices(allgather_small(pack))
    gl.update(norm_even=tot[0:2], hgrn_norm=tot[2].reshape(2, W_B), mem_norm=tot[3], final_norm=tot[4],
              sink=tot[5, 0:16].reshape(2, N_Q_A))
    loss = tot[5, 16]

    upd = {n: adamw_call(w[n], gl[n], m[n], v[n]) for n in WEIGHT_NAMES}
    return (loss, dx[None], *[gl[n] for n in WEIGHT_NAMES], *[upd[n][0] for n in WEIGHT_NAMES],
            *[upd[n][1] for n in WEIGHT_NAMES], *[upd[n][2] for n in WEIGHT_NAMES])
```

```python
import functools

import numpy as np
import jax
import jax.numpy as jnp
from jax import lax
from jax.experimental import pallas as pl
from jax.experimental.pallas import tpu as pltpu

F32 = jnp.float32
BF16 = jnp.bfloat16

D_MODEL = 1024
DEPTH = 4
N_Q_A, N_KV_A, HEAD_DIM_A = 8, 2, 64
W_A, W_KV_A = 512, 128
WINDOW = 128
BLOCK = 128
N_HEADS_B, HEAD_DIM_B, W_B = 4, 128, 512
N_HEADS_C, DK_C, DV_C, WK_C, WV_C = 4, 128, 256, 512, 1024
GATE_RANK = 16
GATE_TEMP = 16.0
N_MEM, N_HEADS_M, HEAD_DIM_M, W_M = 256, 4, 128, 512
EPS = 1e-6
MASK_VALUE = -1e30
MIN_GATE = 1e-30
EVEN_IN, ODD_IN = 4864, 4128
ODD_PAD = 4224
MIX = 1536
ADAM_LR, ADAM_B1, ADAM_B2, ADAM_EPS, ADAM_WD, ADAM_STEP = 0.001, 0.9, 0.999, 1e-08, 0.01, 10

SCAN_CHUNK = 128
SCAN_LEVELS = 7
VMEM_LIMIT = 56 * 1024 * 1024

EVEN_REF_OFF = dict(qA=0, kA=512, vA=640, gA=768, qB=1280, zf=1792, zb=2304, iB=2816, gB=3328, qM=3840, gM=4352)
EVEN_W = dict(qA=512, kA=128, vA=128, gA=512, qB=512, zf=512, zb=512, iB=512, gB=512, qM=512, gM=512)
EVEN_ORDER = ("qA", "gA", "qB", "zf", "zb", "iB", "gB", "qM", "gM", "kA", "vA")
ODD_REF_OFF = dict(qC=0, kC=512, vC=1024, gC=2048, rr=3072, qM=3104, gM=3616)
ODD_W = dict(qC=512, kC=512, vC=1024, gC=1024, rr=32, qM=512, gM=512)
ODD_ORDER = ("qC", "kC", "vC", "gC", "qM", "gM", "rr")


def _offsets(order, widths):
    off, o = {}, 0
    for n in order:
        off[n] = o
        o += widths[n]
    return off


EVEN_OFF = _offsets(EVEN_ORDER, EVEN_W)
ODD_OFF = _offsets(ODD_ORDER, ODD_W)


def _dg(a, b, ca, cb):
    return lax.dot_general(a.astype(BF16), b.astype(BF16), (((ca,), (cb,)), ((), ())),
                           preferred_element_type=F32)


def dot_nn(a, b):
    return _dg(a, b, 1, 0)


def dot_nt(a, b):
    return _dg(a, b, 1, 1)


def dot_tn(a, b):
    return _dg(a, b, 0, 0)


@jax.custom_vjp
def bdot(a, b):
    return dot_nn(a, b)


bdot.defvjp(lambda a, b: (dot_nn(a, b), (a, b)),
            lambda r, g: (dot_nt(g, r[1]), dot_tn(r[0], g)))


@jax.custom_vjp
def bdot_t(a, b):
    return dot_nt(a, b)


bdot_t.defvjp(lambda a, b: (dot_nt(a, b), (a, b)),
              lambda r, g: (dot_nn(g, r[1]), dot_tn(g, r[0])))


@jax.custom_vjp
def bdot_tn(a, b):
    return dot_tn(a, b)


bdot_tn.defvjp(lambda a, b: (dot_tn(a, b), (a, b)),
               lambda r, g: (dot_nt(r[1], g), dot_nn(r[0], g)))


def _split_mm(h, x):
    hi = x.astype(BF16)
    lo = (x - hi.astype(F32)).astype(BF16)
    return (lax.dot_general(h, hi, (((1,), (0,)), ((), ())), preferred_element_type=F32)
            + lax.dot_general(h, lo, (((1,), (0,)), ((), ())), preferred_element_type=F32))


@jax.custom_vjp
def hdot(h, ht, x):
    return _split_mm(h, x)


hdot.defvjp(lambda h, ht, x: (_split_mm(h, x), (h, ht)),
            lambda r, g: (jnp.zeros_like(r[0]), jnp.zeros_like(r[1]), _split_mm(r[1], g)))


def _sigmoid(z):
    return 1.0 / (1.0 + jnp.exp(-z))


def _silu(z):
    return z * _sigmoid(z)


def _log_sigmoid(z):
    return jnp.minimum(z, 0.0) - jnp.log(1.0 + jnp.exp(-jnp.abs(z)))


def _rms(x, g):
    return x * lax.rsqrt(jnp.mean(x * x, axis=-1, keepdims=True) + EPS) * g


def rms_tile(x, g):
    return (_rms(x, g),)


@functools.partial(jax.custom_vjp, nondiff_argnums=(1, 2))
def split(x, n, axis):
    w = x.shape[axis] // n
    return tuple(lax.slice_in_dim(x, h * w, (h + 1) * w, axis=axis) for h in range(n))


split.defvjp(lambda x, n, axis: (split(x, n, axis), None),
             lambda n, axis, _, cts: (jnp.concatenate(cts, axis=axis),))


def _group_rms(o, g, heads):
    return jnp.concatenate([_rms(oh, gh) for oh, gh in zip(split(o, heads, 1), split(g, heads, 1))], axis=-1)


def even_post_tile(a, o2f, o2b, mo, gA, gB, gM, hg):
    y = _group_rms(o2f + o2b, hg, N_HEADS_B)
    return (jnp.concatenate([a * _silu(gA), y * _silu(gB), mo * _silu(gM)], axis=-1),)


def odd_post_tile(o2f, o2b, mo, gC, gM, gg):
    y = _group_rms(o2f + o2b, gg, N_HEADS_C)
    return (jnp.concatenate([y * _silu(gC), mo * _silu(gM)], axis=-1),)


def hgrn_prep_tile(qB, zf, zb, low_f, low_b):
    ks, gs = [], []
    for z, lb in ((zf, low_f), (zb, low_b)):
        f = lb + (1.0 - lb) * _sigmoid(z)
        gs.append(jnp.log(jnp.maximum(f, MIN_GATE)))
        ks.append((1.0 - lb) * _sigmoid(-z))
    return (_silu(qB), ks[0], ks[1], gs[0], gs[1])


def gla_prep_tile(qC, r128, wup_f, wup_b, bg_f, bg_b):
    gs = [_log_sigmoid(bdot(r128, wup) + bg) / GATE_TEMP for wup, bg in ((wup_f, bg_f), (wup_b, bg_b))]
    return (qC * (DK_C ** -0.5), gs[0], gs[1])


def mem_tile(q, k, v):
    s = bdot_t(q, k) * (HEAD_DIM_M ** -0.5)
    m = lax.stop_gradient(jnp.max(s, axis=-1, keepdims=True))
    p = jnp.exp(s - m)
    p = p / jnp.sum(p, axis=-1, keepdims=True)
    return (bdot(p, v),)


def attn_block(qs, ks, vs, sinks, slopes, c, seq):
    i = lax.broadcasted_iota(jnp.int32, (BLOCK, 3 * BLOCK), 0)
    j = lax.broadcasted_iota(jnp.int32, (BLOCK, 3 * BLOCK), 1)
    dist = jnp.abs(i - j + BLOCK).astype(F32)
    kpos = (c - 1) * BLOCK + j
    valid = (dist <= WINDOW) & (kpos >= 0) & (kpos < seq)
    outs = []
    for q, sk, slope in zip(qs, sinks, slopes):
        s = bdot_t(q, ks) * (HEAD_DIM_A ** -0.5)
        s = jnp.where(valid, s - slope * dist, MASK_VALUE)
        m = lax.stop_gradient(jnp.maximum(jnp.max(s, axis=-1, keepdims=True), sk))
        p = jnp.where(valid, jnp.exp(s - m), 0.0)
        denom = jnp.sum(p, axis=-1, keepdims=True) + jnp.exp(sk - m)
        outs.append(bdot(p, vs) / denom)
    return tuple(outs)


def scan_chunk(q, k, v, g, st, h, ht, qm, km, bm):
    C = SCAN_CHUNK
    e = split(hdot(h, ht, g), 2 + SCAN_LEVELS, 0)
    qe = q * jnp.exp(e[0])
    kd = k * jnp.exp(e[1])
    tot = jnp.sum(g, axis=0, keepdims=True)
    r = lax.broadcasted_iota(jnp.int32, (C, C), 0)
    s = lax.broadcasted_iota(jnp.int32, (C, C), 1)
    a = jnp.where(r == s, jnp.sum(q * k, axis=-1, keepdims=True), 0.0)
    for l in range(SCAN_LEVELS):
        el = jnp.exp(e[2 + l])
        qs = q * el * qm[l * C:(l + 1) * C]
        ks = k * el * km[l * C:(l + 1) * C]
        a = a + bdot_t(qs, ks) * bm[l * C:(l + 1) * C]
    o = bdot_t(qe, st) + bdot(a, v)
    st_new = st * jnp.exp(tot) + bdot_tn(v, kd)
    return o, st_new


def _scan_consts():
    C, L = SCAN_CHUNK, SCAN_LEVELS
    t = np.arange(C)[:, None]
    r = np.arange(C)[None, :]
    blocks = [(r <= t), (r > t)]
    qms, kms, bms = [], [], []
    for l in range(1, L + 1):
        m = C >> l
        upper_t = (t % (2 * m)) >= m
        same_half = (t // m) == (r // m)
        blocks.append(same_half & np.where(upper_t, r <= t, r > t))
        qms.append(np.broadcast_to(upper_t, (C, C)))
        kms.append(np.broadcast_to(~upper_t, (C, C)))
        bms.append((t // (2 * m)) == (r // (2 * m)))
    hf = np.concatenate(blocks, axis=0).astype(np.float32)
    flip = lambda mat: mat.reshape(-1, C, C)[:, ::-1, ::-1].reshape(-1, C)
    hb = flip(hf)
    qmf = np.concatenate(qms, axis=0).astype(np.float32)
    kmf = np.concatenate(kms, axis=0).astype(np.float32)
    bm = np.concatenate(bms, axis=0).astype(np.float32)
    h = np.stack([hf, hb])
    ht = np.stack([hf.T, hb.T])
    qm = np.stack([qmf, kmf])
    km = np.stack([kmf, qmf])
    return h, ht, qm, km, bm


def _cparams(sem):
    return pltpu.CompilerParams(dimension_semantics=sem, vmem_limit_bytes=VMEM_LIMIT)


def _row_tile(T):
    return min(T, 256)


def _in_spec(spec, tr):
    kind = spec[0]
    if kind == "row":
        _, arr, off, w = spec
        assert off % w == 0
        return arr, pl.BlockSpec((tr, w), functools.partial(lambda i, b: (i, b), b=off // w))
    if kind == "row3":
        _, arr, d, off, w = spec
        assert off % w == 0
        return arr, pl.BlockSpec((None, tr, w), functools.partial(lambda i, d, b: (d, i, b), d=d, b=off // w))
    _, arr = spec
    return arr, pl.BlockSpec(arr.shape, functools.partial(lambda i, n: (0,) * n, n=arr.ndim))


def rows_call(name, tile_fn, T, ins, out_widths, out_dtypes=None, stacks=None):
    tr = _row_tile(T)
    n_in = len(ins)
    out_dtypes = out_dtypes or [F32] * len(out_widths)
    stacks = stacks or [(k,) for k in range(len(out_widths))]

    def body(*refs):
        vals = [r[...] for r in refs[:n_in]]
        outs = tile_fn(*vals)
        for r, members in zip(refs[n_in:], stacks):
            if len(members) == 1:
                r[...] = outs[members[0]].astype(r.dtype)
            else:
                for d, k in enumerate(members):
                    r[d] = outs[k].astype(r.dtype)

    in_specs, args = [], []
    for spec in ins:
        arr, bs = _in_spec(spec, tr)
        args.append(arr)
        in_specs.append(bs)
    out_specs, out_shape = [], []
    for w, dt, members in zip(out_widths, out_dtypes, stacks):
        n = len(members)
        if n == 1:
            out_specs.append(pl.BlockSpec((tr, w), lambda i: (i, 0)))
            out_shape.append(jax.ShapeDtypeStruct((T, w), dt))
        else:
            out_specs.append(pl.BlockSpec((n, tr, w), lambda i: (0, i, 0)))
            out_shape.append(jax.ShapeDtypeStruct((n, T, w), dt))
    return pl.pallas_call(body, out_shape=out_shape, grid=(T // tr,), in_specs=in_specs, out_specs=out_specs,
                          name=name, compiler_params=_cparams(("arbitrary",)))(*args)


def rows_vjp_call(name, tile_fn, T, ins, cts, skip=()):
    tr = _row_tile(T)
    n_in = len(ins)
    n_ct = [len(c) for c in cts]
    want = [k for k in range(n_in) if k not in skip]

    def body(*refs):
        i = pl.program_id(0)
        vals = [r[...] for r in refs[:n_in]]
        ct, pos = [], n_in
        for n in n_ct:
            acc = refs[pos][...]
            for r in refs[pos + 1:pos + n]:
                acc = acc + r[...]
            ct.append(acc)
            pos += n
        _, vjp = jax.vjp(tile_fn, *vals)
        grads = vjp(tuple(ct))
        for r, k in zip(refs[pos:], want):
            if ins[k][0] == "full":
                @pl.when(i == 0)
                def _():
                    r[...] = jnp.zeros_like(r)
                r[...] += grads[k]
            else:
                r[...] = grads[k]

    in_specs, args = [], []
    for spec in list(ins) + [s for c in cts for s in c]:
        arr, bs = _in_spec(spec, tr)
        args.append(arr)
        in_specs.append(bs)
    out_specs, out_shape = [], []
    for k in want:
        if ins[k][0] == "full":
            arr = ins[k][1]
            out_specs.append(pl.BlockSpec(arr.shape, functools.partial(lambda i, n: (0,) * n, n=arr.ndim)))
            out_shape.append(jax.ShapeDtypeStruct(arr.shape, F32))
        else:
            w = ins[k][-1]
            out_specs.append(pl.BlockSpec((tr, w), lambda i: (i, 0)))
            out_shape.append(jax.ShapeDtypeStruct((T, w), F32))
    return pl.pallas_call(body, out_shape=out_shape, grid=(T // tr,), in_specs=in_specs, out_specs=out_specs,
                          name=name, compiler_params=_cparams(("arbitrary",)))(*args)


def matmul(name, a, b, mode, add=None, out_dtype=F32):
    if mode == "tn":
        K, M = a.shape
        N = b.shape[1]
        tm = M if M <= 1024 else 512
        tn = N if N <= 1280 else (N // 2 if (N // 2) % 128 == 0 else N)
        tk = min(K, 512)
        grid = (M // tm, N // tn, K // tk)

        def body(a_ref, b_ref, o_ref):
            @pl.when(pl.program_id(2) == 0)
            def _():
                o_ref[...] = jnp.zeros_like(o_ref)
            o_ref[...] += dot_tn(a_ref[...], b_ref[...])

        return pl.pallas_call(
            body, out_shape=jax.ShapeDtypeStruct((M, N), F32), grid=grid,
            in_specs=[pl.BlockSpec((tk, tm), lambda i, j, k: (k, i)), pl.BlockSpec((tk, tn), lambda i, j, k: (k, j))],
            out_specs=pl.BlockSpec((tm, tn), lambda i, j, k: (i, j)), name=name,
            compiler_params=_cparams(("arbitrary", "arbitrary", "arbitrary")))(a, b)

    M, K = a.shape
    N = b.shape[1] if mode == "nn" else b.shape[0]
    tm = min(M, 256)
    tn = N if N <= 1536 else (N // 2 if (N // 2) % 128 == 0 else (N // 3 if (N // 3) % 128 == 0 else N))
    grid = (N // tn, M // tm)
    n_in = 2 + (add is not None)

    def body(*refs):
        a_ref, b_ref = refs[0], refs[1]
        o_ref = refs[n_in]
        acc = dot_nn(a_ref[...], b_ref[...]) if mode == "nn" else dot_nt(a_ref[...], b_ref[...])
        if add is not None:
            acc = acc + refs[2][...]
        o_ref[...] = acc.astype(o_ref.dtype)

    in_specs = [pl.BlockSpec((tm, K), lambda j, i: (i, 0)),
                pl.BlockSpec((K, tn), lambda j, i: (0, j)) if mode == "nn" else pl.BlockSpec((tn, K), lambda j, i: (j, 0))]
    args = [a, b]
    if add is not None:
        in_specs.append(pl.BlockSpec((tm, tn), lambda j, i: (i, j)))
        args.append(add)
    return pl.pallas_call(
        body, out_shape=jax.ShapeDtypeStruct((M, N), out_dtype), grid=grid, in_specs=in_specs,
        out_specs=pl.BlockSpec((tm, tn), lambda j, i: (i, j)), name=name,
        compiler_params=_cparams(("arbitrary", "arbitrary")))(*args)


def attn_fwd(q8, k2p, v2p, sink, slopes, T):
    nb = T // BLOCK

    def body(q_ref, k_ref, v_ref, sink_ref, slope_ref, o_ref):
        c = pl.program_id(1)
        start = pl.multiple_of(c * BLOCK, BLOCK)
        ks = k_ref[pl.ds(start, 3 * BLOCK), :]
        vs = v_ref[pl.ds(start, 3 * BLOCK), :]
        outs = attn_block([q_ref[g] for g in range(4)], ks, vs, [sink_ref[g] for g in range(4)],
                          [slope_ref[g] for g in range(4)], c, T)
        for g in range(4):
            o_ref[g] = outs[g]

    return pl.pallas_call(
        body, out_shape=jax.ShapeDtypeStruct((N_Q_A, T, HEAD_DIM_A), F32), grid=(N_KV_A, nb),
        in_specs=[pl.BlockSpec((4, BLOCK, HEAD_DIM_A), lambda n, c: (n, c, 0)),
                  pl.BlockSpec((None, T + 2 * BLOCK, HEAD_DIM_A), lambda n, c: (n, 0, 0)),
                  pl.BlockSpec((None, T + 2 * BLOCK, HEAD_DIM_A), lambda n, c: (n, 0, 0)),
                  pl.BlockSpec((4, 1, 1), lambda n, c: (n, 0, 0)),
                  pl.BlockSpec((4, 1, 1), lambda n, c: (n, 0, 0))],
        out_specs=pl.BlockSpec((4, BLOCK, HEAD_DIM_A), lambda n, c: (n, c, 0)),
        name="attn_fwd", compiler_params=_cparams(("arbitrary", "arbitrary")))(q8, k2p, v2p, sink, slopes)


def attn_bwd(q8, k2p, v2p, sink, slopes, do8, T):
    nb = T // BLOCK

    def body(q_ref, k_ref, v_ref, sink_ref, slope_ref, do_ref, dq_ref, dk_ref, dv_ref, dsink_ref):
        c = pl.program_id(1)
        start = pl.multiple_of(c * BLOCK, BLOCK)
        ks = k_ref[pl.ds(start, 3 * BLOCK), :]
        vs = v_ref[pl.ds(start, 3 * BLOCK), :]
        slopes = [slope_ref[g] for g in range(4)]
        _, vjp = jax.vjp(lambda qs, kk, vv, sks: attn_block(qs, kk, vv, sks, slopes, c, T),
                         [q_ref[g] for g in range(4)], ks, vs, [sink_ref[g] for g in range(4)])
        dqs, dks, dvs, dsks = vjp(tuple(do_ref[g] for g in range(4)))

        @pl.when(c == 0)
        def _():
            dk_ref[...] = jnp.zeros_like(dk_ref)
            dv_ref[...] = jnp.zeros_like(dv_ref)
            dsink_ref[...] = jnp.zeros_like(dsink_ref)

        dk_ref[pl.ds(start, 3 * BLOCK), :] += dks
        dv_ref[pl.ds(start, 3 * BLOCK), :] += dvs
        for g in range(4):
            dq_ref[g] = dqs[g]
            dsink_ref[g] += dsks[g]

    qspec = pl.BlockSpec((4, BLOCK, HEAD_DIM_A), lambda n, c: (n, c, 0))
    kspec = pl.BlockSpec((None, T + 2 * BLOCK, HEAD_DIM_A), lambda n, c: (n, 0, 0))
    sspec = pl.BlockSpec((4, 1, 1), lambda n, c: (n, 0, 0))
    return pl.pallas_call(
        body,
        out_shape=[jax.ShapeDtypeStruct((N_Q_A, T, HEAD_DIM_A), F32),
                   jax.ShapeDtypeStruct((N_KV_A, T + 2 * BLOCK, HEAD_DIM_A), F32),
                   jax.ShapeDtypeStruct((N_KV_A, T + 2 * BLOCK, HEAD_DIM_A), F32),
                   jax.ShapeDtypeStruct((N_Q_A, 1, 1), F32)],
        grid=(N_KV_A, nb), in_specs=[qspec, kspec, kspec, sspec, sspec, qspec],
        out_specs=[qspec, kspec, kspec, sspec],
        name="attn_bwd", compiler_params=_cparams(("arbitrary", "arbitrary")))(q8, k2p, v2p, sink, slopes, do8)


def mem_fwd(p, q_off, kv, T):
    tr = _row_tile(T)
    assert q_off % W_M == 0

    def body(q_ref, kv_ref, o_ref):
        for h in range(N_HEADS_M):
            hs = pl.ds(h * HEAD_DIM_M, HEAD_DIM_M)
            (o,) = mem_tile(q_ref[:, hs], kv_ref[:, hs], kv_ref[:, pl.ds(W_M + h * HEAD_DIM_M, HEAD_DIM_M)])
            o_ref[:, hs] = o

    return pl.pallas_call(
        body, out_shape=jax.ShapeDtypeStruct((T, W_M), F32), grid=(T // tr,),
        in_specs=[pl.BlockSpec((tr, W_M), lambda i: (i, q_off // W_M)), pl.BlockSpec((N_MEM, 2 * W_M), lambda i: (0, 0))],
        out_specs=pl.BlockSpec((tr, W_M), lambda i: (i, 0)),
        name="mem_fwd", compiler_params=_cparams(("arbitrary",)))(p, kv)


def mem_bwd(p, q_off, kv, do, T):
    tr = _row_tile(T)

    def body(q_ref, kv_ref, do_ref, dq_ref, dkv_ref):
        @pl.when(pl.program_id(0) == 0)
        def _():
            dkv_ref[...] = jnp.zeros_like(dkv_ref)

        for h in range(N_HEADS_M):
            hs = pl.ds(h * HEAD_DIM_M, HEAD_DIM_M)
            vs = pl.ds(W_M + h * HEAD_DIM_M, HEAD_DIM_M)
            _, vjp = jax.vjp(mem_tile, q_ref[:, hs], kv_ref[:, hs], kv_ref[:, vs])
            dq, dk, dv = vjp((do_ref[:, hs],))
            dq_ref[:, hs] = dq
            dkv_ref[:, hs] += dk
            dkv_ref[:, vs] += dv

    kvspec = pl.BlockSpec((N_MEM, 2 * W_M), lambda i: (0, 0))
    return pl.pallas_call(
        body,
        out_shape=[jax.ShapeDtypeStruct((T, W_M), F32), jax.ShapeDtypeStruct((N_MEM, 2 * W_M), F32)],
        grid=(T // tr,),
        in_specs=[pl.BlockSpec((tr, W_M), lambda i: (i, q_off // W_M)), kvspec, pl.BlockSpec((tr, W_M), lambda i: (i, 0))],
        out_specs=[pl.BlockSpec((tr, W_M), lambda i: (i, 0)), kvspec],
        name="mem_bwd", compiler_params=_cparams(("arbitrary",)))(p, kv, do)


def _scan_const_specs(dk):
    C, L = SCAN_CHUNK, SCAN_LEVELS
    return [pl.BlockSpec((2, (2 + L) * C, C), lambda n: (0, 0, 0)),
            pl.BlockSpec((2, C, (2 + L) * C), lambda n: (0, 0, 0)),
            pl.BlockSpec((2, L * C, dk), lambda n: (0, 0, 0)),
            pl.BlockSpec((2, L * C, dk), lambda n: (0, 0, 0)),
            pl.BlockSpec((L * C, C), lambda n: (0, 0))]


def _chunk_spec(src, width, chunk_of):
    arr, sel = src
    if arr.ndim == 2:
        assert sel % width == 0
        return pl.BlockSpec((SCAN_CHUNK, width), functools.partial(lambda n, b: (chunk_of(n), b), b=sel // width))
    return pl.BlockSpec((None, SCAN_CHUNK, width), functools.partial(lambda n, d: (d, chunk_of(n), 0), d=sel))


def _scan_const_args():
    h, ht, qm, km, bm = _scan_consts()
    return [jnp.asarray(h, BF16), jnp.asarray(ht, BF16), jnp.asarray(qm, F32), jnp.asarray(km, F32), jnp.asarray(bm, F32)]


def scan_fwd(name, q, kf, kb, gf, gb, v, heads, dk, dv, T):
    C = SCAN_CHUNK
    N = T // C
    assert dk == C
    W, Wv = heads * dk, heads * dv
    fwd = lambda n: n
    rev = lambda n: N - 1 - n

    def body(qf_ref, qb_ref, kf_ref, kb_ref, gf_ref, gb_ref, vf_ref, vb_ref, h_ref, ht_ref, qm_ref, km_ref, bm_ref,
             of_ref, ob_ref, ssf_ref, ssb_ref, st_ref):
        @pl.when(pl.program_id(0) == 0)
        def _():
            st_ref[...] = jnp.zeros_like(st_ref)

        bm = bm_ref[...]
        dirs = ((qf_ref, kf_ref, gf_ref, vf_ref, of_ref, ssf_ref), (qb_ref, kb_ref, gb_ref, vb_ref, ob_ref, ssb_ref))
        for d, (q_r, k_r, g_r, v_r, o_r, ss_r) in enumerate(dirs):
            consts = (h_ref[d], ht_ref[d], qm_ref[d], km_ref[d], bm)
            for h in range(heads):
                ks, vs = pl.ds(h * dk, dk), pl.ds(h * dv, dv)
                st = st_ref[d, h]
                ss_r[h] = st
                o, st_new = scan_chunk(q_r[:, ks], k_r[:, ks], v_r[:, vs], g_r[:, ks], st, *consts)
                o_r[:, vs] = o
                st_ref[d, h] = st_new

    srcs = [(q, fwd, W), (q, rev, W), (kf, fwd, W), (kb, rev, W), (gf, fwd, W), (gb, rev, W), (v, fwd, Wv), (v, rev, Wv)]
    ss_spec = lambda order: pl.BlockSpec((heads, None, dv, dk), lambda n: (0, order(n), 0, 0))
    return pl.pallas_call(
        body,
        out_shape=[jax.ShapeDtypeStruct((T, Wv), F32)] * 2 + [jax.ShapeDtypeStruct((heads, N, dv, dk), F32)] * 2,
        grid=(N,),
        in_specs=[_chunk_spec(s, w, order) for s, order, w in srcs] + _scan_const_specs(dk),
        out_specs=[pl.BlockSpec((C, Wv), lambda n: (fwd(n), 0)), pl.BlockSpec((C, Wv), lambda n: (rev(n), 0)),
                   ss_spec(fwd), ss_spec(rev)],
        scratch_shapes=[pltpu.VMEM((2, heads, dv, dk), F32)],
        name=name, compiler_params=_cparams(("arbitrary",)))(*[s[0] for s, _, _ in srcs], *_scan_const_args())


def scan_bwd(name, q, kf, kb, gf, gb, v, ss_f, ss_b, do, heads, dk, dv, T):
    C = SCAN_CHUNK
    N = T // C
    W, Wv = heads * dk, heads * dv
    fwd = lambda n: N - 1 - n
    rev = lambda n: n

    def body(qf_ref, qb_ref, kf_ref, kb_ref, gf_ref, gb_ref, vf_ref, vb_ref, ssf_ref, ssb_ref, dof_ref, dob_ref,
             h_ref, ht_ref, qm_ref, km_ref, bm_ref,
             dqf_ref, dkf_ref, dgf_ref, dvf_ref, dqb_ref, dkb_ref, dgb_ref, dvb_ref, dst_ref):
        @pl.when(pl.program_id(0) == 0)
        def _():
            dst_ref[...] = jnp.zeros_like(dst_ref)

        bm = bm_ref[...]
        dirs = ((qf_ref, kf_ref, gf_ref, vf_ref, ssf_ref, dof_ref, dqf_ref, dkf_ref, dgf_ref, dvf_ref),
                (qb_ref, kb_ref, gb_ref, vb_ref, ssb_ref, dob_ref, dqb_ref, dkb_ref, dgb_ref, dvb_ref))
        for d, (q_r, k_r, g_r, v_r, ss_r, do_r, dq_r, dk_r, dg_r, dv_r) in enumerate(dirs):
            consts = (h_ref[d], ht_ref[d], qm_ref[d], km_ref[d], bm)
            for h in range(heads):
                ks, vs = pl.ds(h * dk, dk), pl.ds(h * dv, dv)
                _, vjp = jax.vjp(lambda q_, k_, v_, g_, st_: scan_chunk(q_, k_, v_, g_, st_, *consts),
                                 q_r[:, ks], k_r[:, ks], v_r[:, vs], g_r[:, ks], ss_r[h])
                dq, dk_, dv_, dg, dst = vjp((do_r[:, vs], dst_ref[d, h]))
                dq_r[:, ks] = dq
                dk_r[:, ks] = dk_
                dg_r[:, ks] = dg
                dv_r[:, vs] = dv_
                dst_ref[d, h] = dst

    srcs = [(q, fwd, W), (q, rev, W), (kf, fwd, W), (kb, rev, W), (gf, fwd, W), (gb, rev, W), (v, fwd, Wv), (v, rev, Wv)]
    ss_spec = lambda order: pl.BlockSpec((heads, None, dv, dk), lambda n: (0, order(n), 0, 0))
    kspec = lambda order: pl.BlockSpec((C, W), lambda n: (order(n), 0))
    vspec = lambda order: pl.BlockSpec((C, Wv), lambda n: (order(n), 0))
    return pl.pallas_call(
        body,
        out_shape=([jax.ShapeDtypeStruct((T, W), F32)] * 3 + [jax.ShapeDtypeStruct((T, Wv), F32)]) * 2,
        grid=(N,),
        in_specs=[_chunk_spec(s, w, order) for s, order, w in srcs]
        + [ss_spec(fwd), ss_spec(rev), _chunk_spec(do, Wv, fwd), _chunk_spec(do, Wv, rev)] + _scan_const_specs(dk),
        out_specs=[kspec(fwd)] * 3 + [vspec(fwd)] + [kspec(rev)] * 3 + [vspec(rev)],
        scratch_shapes=[pltpu.VMEM((2, heads, dv, dk), F32)],
        name=name, compiler_params=_cparams(("arbitrary",)))(
            *[s[0] for s, _, _ in srcs], ss_f, ss_b, do[0], do[0], *_scan_const_args())


def final_call(x, g, target, T):
    tr = _row_tile(T)

    def tile(xv, gv, tv):
        y = _rms(xv, gv)
        err = (y - tv) ** 2
        return jnp.sum(jnp.sum(err, axis=-1, keepdims=True), axis=0, keepdims=True) * (0.5 / D_MODEL)

    def body(x_ref, g_ref, t_ref, loss_ref, dx_ref, dg_ref):
        i = pl.program_id(0)
        tv = t_ref[...]
        lv, vjp = jax.vjp(lambda a, b: tile(a, b, tv), x_ref[...], g_ref[...])
        dx, dg = vjp(jnp.ones((1, 1), F32))
        dx_ref[...] = dx

        @pl.when(i == 0)
        def _():
            loss_ref[...] = jnp.zeros_like(loss_ref)
            dg_ref[...] = jnp.zeros_like(dg_ref)

        loss_ref[...] += jnp.broadcast_to(lv, loss_ref.shape)
        dg_ref[...] += dg

    return pl.pallas_call(
        body,
        out_shape=[jax.ShapeDtypeStruct((8, 128), F32), jax.ShapeDtypeStruct((T, D_MODEL), F32),
                   jax.ShapeDtypeStruct((1, D_MODEL), F32)],
        grid=(T // tr,),
        in_specs=[pl.BlockSpec((tr, D_MODEL), lambda i: (i, 0)), pl.BlockSpec((1, D_MODEL), lambda i: (0, 0)),
                  pl.BlockSpec((tr, D_MODEL), lambda i: (i, 0))],
        out_specs=[pl.BlockSpec((8, 128), lambda i: (0, 0)), pl.BlockSpec((tr, D_MODEL), lambda i: (i, 0)),
                   pl.BlockSpec((1, D_MODEL), lambda i: (0, 0))],
        name="final_loss", compiler_params=_cparams(("arbitrary",)))(x, g, target)


def adamw_call(w, g, m, v):
    shape = w.shape
    c = shape[-1]
    r = int(np.prod(shape[:-1])) if len(shape) > 1 else 1
    tr = r if r <= 256 else 256
    assert r % tr == 0

    def body(w_ref, g_ref, m_ref, v_ref, d_ref, nm_ref, nv_ref):
        gv = g_ref[...]
        nm = ADAM_B1 * m_ref[...] + (1.0 - ADAM_B1) * gv
        nv = ADAM_B2 * v_ref[...] + (1.0 - ADAM_B2) * jnp.square(gv)
        m_hat = nm / (1.0 - ADAM_B1 ** ADAM_STEP)
        v_hat = nv / (1.0 - ADAM_B2 ** ADAM_STEP)
        d_ref[...] = -ADAM_LR * (m_hat / (jnp.sqrt(v_hat) + ADAM_EPS) + ADAM_WD * w_ref[...])
        nm_ref[...] = nm
        nv_ref[...] = nv

    spec = pl.BlockSpec((tr, c), lambda i: (i, 0))
    outs = pl.pallas_call(body, out_shape=[jax.ShapeDtypeStruct((r, c), F32)] * 3, grid=(r // tr,),
                          in_specs=[spec] * 4, out_specs=[spec] * 3, name="adamw",
                          compiler_params=_cparams(("arbitrary",)))(*(t.reshape(r, c) for t in (w, g, m, v)))
    return tuple(o.reshape(shape) for o in outs)


def sum_devices(g64):
    def body(x_ref, o_ref):
        acc = x_ref[0:8, :]
        for d in range(1, 8):
            acc = acc + x_ref[8 * d:8 * d + 8, :]
        o_ref[...] = acc

    return pl.pallas_call(body, out_shape=jax.ShapeDtypeStruct((8, D_MODEL), F32), name="sum_devices")(g64)


def _half_tile(rh):
    return rh if rh <= 512 else 256


def add_sibling(g, recv, c, out_dtype):
    _, R, C = g.shape
    rh = R // 2
    tr = _half_tile(rh)
    nblk = rh // tr

    def body(c_ref, g_ref, r_ref, o_ref):
        o_ref[...] = (g_ref[...] + r_ref[...]).astype(o_ref.dtype)

    grid_spec = pltpu.PrefetchScalarGridSpec(
        num_scalar_prefetch=1, grid=(4, nblk),
        in_specs=[pl.BlockSpec((None, tr, C), lambda j, i, c_ref: (j, i + c_ref[0] * nblk, 0)),
                  pl.BlockSpec((None, tr, C), lambda j, i, c_ref: (j, i, 0))],
        out_specs=pl.BlockSpec((None, tr, C), lambda j, i, c_ref: (j, i, 0)))
    return pl.pallas_call(body, out_shape=jax.ShapeDtypeStruct((4, rh, C), out_dtype), grid_spec=grid_spec,
                          name="rs_add_sibling", compiler_params=_cparams(("arbitrary", "arbitrary")))(c, g, recv)


def add_chips(g, recv, r3, place):
    _, R, C = g.shape
    rh = R // 2
    tr = _half_tile(rh)
    nblk = rh // tr

    def body(p_ref, g_ref, s_ref, a_ref, b_ref, c_ref, o_ref):
        up = lambda r: r[...].astype(F32)
        o_ref[...] = (((g_ref[...] + up(s_ref)) + up(a_ref)) + up(b_ref)) + up(c_ref)

    grid_spec = pltpu.PrefetchScalarGridSpec(
        num_scalar_prefetch=1, grid=(nblk,),
        in_specs=[pl.BlockSpec((None, tr, C), lambda i, p_ref: (p_ref[0], i + p_ref[1] * nblk, 0)),
                  pl.BlockSpec((None, tr, C), lambda i, p_ref: (p_ref[0], i, 0))]
        + [pl.BlockSpec((None, tr, C), functools.partial(lambda i, p_ref, k: (k, i, 0), k=k)) for k in range(3)],
        out_specs=pl.BlockSpec((tr, C), lambda i, p_ref: (i + p_ref[1] * nblk, 0)))
    return pl.pallas_call(body, out_shape=jax.ShapeDtypeStruct((R, C), F32), grid_spec=grid_spec,
                          name="rs_add_chips", compiler_params=_cparams(("arbitrary",)))(place, g, recv, r3, r3, r3)


def _remote(src, dst, ssem, rsem, dev):
    return pltpu.make_async_remote_copy(src_ref=src, dst_ref=dst, send_sem=ssem, recv_sem=rsem,
                                        device_id=dev, device_id_type=pl.DeviceIdType.MESH)


def _mesh_places():
    x, y, c = lax.axis_index("x"), lax.axis_index("y"), lax.axis_index("c")
    chips = [(1 - x, y), (x, 1 - y), (1 - x, 1 - y)]
    return x, y, c, (x, y, 1 - c), chips


def _hbm_specs(n):
    return [pl.BlockSpec(memory_space=pltpu.HBM) for _ in range(n)]


def gather_weights(shards, small):
    nb = len(shards)

    def body(*refs):
        ins, outs = refs[:nb + 1], refs[nb + 1:2 * nb + 2]
        send_sems, recv_sems = refs[2 * nb + 2:]
        x, y, c, sibling, chips = _mesh_places()
        mine = 2 * x + y

        def half(a, chip_idx, which):
            rh = ins[a].shape[0] // 2
            return outs[a].at[chip_idx, pl.ds(which * rh, rh), :]

        sent = []
        for a in range(nb):
            rh = ins[a].shape[0] // 2
            src = ins[a].at[pl.ds(c * rh, rh), :]
            for k, chip in enumerate(chips):
                sent.append(_remote(src, half(a, mine, c), send_sems.at[a, k], recv_sems.at[a, k], (*chip, c)))
        for k, chip in enumerate(chips):
            sent.append(_remote(ins[nb], outs[nb].at[mine], send_sems.at[nb, k], recv_sems.at[nb, k], (*chip, c)))
        for cp in sent:
            cp.start()
        for a in range(nb):
            for k, chip in enumerate(chips):
                region = half(a, 2 * chip[0] + chip[1], c)
                _remote(region, region, send_sems.at[a, k], recv_sems.at[a, k], (*chip, c)).wait_recv()
                fwd = _remote(region, region, send_sems.at[a, 3 + k], recv_sems.at[a, 3 + k], sibling)
                fwd.start()
                sent.append(fwd)
        for k, chip in enumerate(chips):
            region = outs[nb].at[2 * chip[0] + chip[1]]
            _remote(region, region, send_sems.at[nb, k], recv_sems.at[nb, k], (*chip, c)).wait_recv()
        for a in range(nb):
            for k, chip in enumerate(chips):
                region = half(a, 2 * chip[0] + chip[1], 1 - c)
                _remote(region, region, send_sems.at[a, 3 + k], recv_sems.at[a, 3 + k], sibling).wait_recv()
        for cp in sent:
            cp.wait_send()

    arrs = list(shards) + [small]
    return pl.pallas_call(
        body, out_shape=[jax.ShapeDtypeStruct((4,) + a.shape, a.dtype) for a in arrs],
        in_specs=_hbm_specs(nb + 1), out_specs=_hbm_specs(nb + 1),
        scratch_shapes=[pltpu.SemaphoreType.DMA((nb + 1, 6)), pltpu.SemaphoreType.DMA((nb + 1, 6))],
        name="gather_weights")(*arrs)


def rs_exchange_siblings(gs):
    n = len(gs)

    def body(*refs):
        ins, outs = refs[:n], refs[n:2 * n]
        send_sems, recv_sems = refs[2 * n:]
        x, y, c, sibling, chips = _mesh_places()
        cps = []
        for a in range(n):
            rh = ins[a].shape[1] // 2
            cps.append(_remote(ins[a].at[:, pl.ds((1 - c) * rh, rh), :], outs[a], send_sems.at[a], recv_sems.at[a], sibling))
        for cp in cps:
            cp.start()
        for cp in cps:
            cp.wait()

    return pl.pallas_call(
        body, out_shape=[jax.ShapeDtypeStruct((4, g.shape[1] // 2, g.shape[2]), g.dtype) for g in gs],
        in_specs=_hbm_specs(n), out_specs=_hbm_specs(n),
        scratch_shapes=[pltpu.SemaphoreType.DMA((n,)), pltpu.SemaphoreType.DMA((n,))],
        name="rs_exchange_siblings")(*gs)


def rs_exchange_chips(s1s):
    n = len(s1s)

    def body(*refs):
        ins, outs = refs[:n], refs[n:2 * n]
        send_sems, recv_sems = refs[2 * n:]
        x, y, c, sibling, chips = _mesh_places()
        cps = []
        for a in range(n):
            for k, chip in enumerate(chips):
                cps.append(_remote(ins[a].at[2 * chip[0] + chip[1]], outs[a].at[k], send_sems.at[a, k],
                                   recv_sems.at[a, k], (*chip, c)))
        for cp in cps:
            cp.start()
        for cp in cps:
            cp.wait()

    return pl.pallas_call(
        body, out_shape=[jax.ShapeDtypeStruct((3,) + s.shape[1:], s.dtype) for s in s1s],
        in_specs=_hbm_specs(n), out_specs=_hbm_specs(n),
        scratch_shapes=[pltpu.SemaphoreType.DMA((n, 3)), pltpu.SemaphoreType.DMA((n, 3))],
        name="rs_exchange_chips")(*s1s)


def rs_share_final(fs):
    n = len(fs)

    def body(*refs):
        bufs = refs[n:2 * n]
        send_sems, recv_sems = refs[2 * n:]
        x, y, c, sibling, chips = _mesh_places()
        cps = []
        for a in range(n):
            rh = bufs[a].shape[0] // 2
            mine = bufs[a].at[pl.ds(c * rh, rh), :]
            cps.append(_remote(mine, mine, send_sems.at[a], recv_sems.at[a], sibling))
        for cp in cps:
            cp.start()
        for a in range(n):
            rh = bufs[a].shape[0] // 2
            other = bufs[a].at[pl.ds((1 - c) * rh, rh), :]
            _remote(other, other, send_sems.at[a], recv_sems.at[a], sibling).wait_recv()
        for cp in cps:
            cp.wait_send()

    return pl.pallas_call(
        body, out_shape=[jax.ShapeDtypeStruct(f.shape, f.dtype) for f in fs],
        in_specs=_hbm_specs(n), out_specs=_hbm_specs(n), input_output_aliases={a: a for a in range(n)},
        scratch_shapes=[pltpu.SemaphoreType.DMA((n,)), pltpu.SemaphoreType.DMA((n,))],
        name="rs_share_final")(*fs)


def allgather_small(v):
    m_per = v.shape[0]

    def body(x_ref, out_ref, send_sems, recv_sems, local_sem):
        x, y, c, sibling, chips = _mesh_places()
        me = (x, y, c)

        def rows(px, py, pc):
            return out_ref.at[pl.ds((4 * px + 2 * py + pc) * m_per, m_per), :]

        def copy(k, block, to, src=None):
            return _remote(rows(*block) if src is None else src, rows(*block), send_sems.at[k], recv_sems.at[k], to)

        mine = pltpu.make_async_copy(x_ref, rows(*me), local_sem)
        mine.start()
        first = [copy(0, me, sibling, src=x_ref)]
        first += [copy(1 + j, me, (*chip, c), src=x_ref) for j, chip in enumerate(chips)]
        for cp in first:
            cp.start()
        passed = [copy(4 + j, (*chip, c), sibling) for j, chip in enumerate(chips)]
        for j, chip in enumerate(chips):
            copy(1 + j, (*chip, c), me).wait_recv()
            passed[j].start()
        copy(0, sibling, me).wait_recv()
        for j, chip in enumerate(chips):
            copy(4 + j, (*chip, 1 - c), me).wait_recv()
        for cp in first + passed:
            cp.wait_send()
        mine.wait()

    return pl.pallas_call(
        body, out_shape=jax.ShapeDtypeStruct((8 * m_per, v.shape[1]), v.dtype),
        in_specs=[pl.BlockSpec(memory_space=pltpu.VMEM)], out_specs=pl.BlockSpec(memory_space=pltpu.VMEM),
        scratch_shapes=[pltpu.SemaphoreType.DMA((7,)), pltpu.SemaphoreType.DMA((7,)), pltpu.SemaphoreType.DMA],
        name="allgather_small")(v)


def rms_res_tile(x, g):
    return (_rms(x, g), x)


def _lower_bounds(lb_param):
    lbs = jax.nn.softmax(lb_param.astype(F32), axis=0)
    return jnp.cumsum(lbs, axis=0) - lbs[0]


def _heads_major(t, n):
    return t.reshape(t.shape[0], n, HEAD_DIM_A).transpose(1, 0, 2)


def _heads_minor(t):
    return t.transpose(1, 0, 2).reshape(t.shape[1], t.shape[0] * t.shape[2])


def _even_fwd(x, i, W, lower, kv, slopes, T):
    O = EVEN_OFF
    g = W["norm_even"][i].reshape(1, D_MODEL)
    (h,) = rows_call("rms_fwd", rms_tile, T, [("row", x, 0, D_MODEL), ("full", g)], [D_MODEL], [BF16])
    p = matmul("mm_in_e", h, W["w_in_e"][i], "nn")
    q8 = _heads_major(p[:, O["qA"]:O["qA"] + W_A], N_Q_A)
    pad = lambda t: jnp.pad(_heads_major(t, N_KV_A), ((0, 0), (BLOCK, BLOCK), (0, 0)))
    k2p = pad(p[:, O["kA"]:O["kA"] + W_KV_A])
    v2p = pad(p[:, O["vA"]:O["vA"] + W_KV_A])
    sink = W["sink"][i].reshape(N_Q_A, 1, 1)
    a = _heads_minor(attn_fwd(q8, k2p, v2p, sink, slopes, T))
    prep_ins = [("row", p, O["qB"], W_B), ("row", p, O["zf"], W_B), ("row", p, O["zb"], W_B),
                ("full", lower[i][0:1]), ("full", lower[i][1:2])]
    qh, k2, g2 = rows_call("hgrn_prep_fwd", hgrn_prep_tile, T, prep_ins, [W_B] * 3, stacks=[(0,), (1, 2), (3, 4)])
    scan_srcs = [(qh, 0), (k2, 0), (k2, 1), (g2, 0), (g2, 1), (p, O["iB"])]
    o_f, o_b, ss_f, ss_b = scan_fwd("scan_fwd_h", *scan_srcs, N_HEADS_B, HEAD_DIM_B, HEAD_DIM_B, T)
    mo = mem_fwd(p, O["qM"], kv, T)
    hg = W["hgrn_norm"][i].reshape(1, W_B)
    post_ins = [("row", a, 0, W_A), ("row", o_f, 0, W_B), ("row", o_b, 0, W_B), ("row", mo, 0, W_M),
                ("row", p, O["gA"], W_A), ("row", p, O["gB"], W_B), ("row", p, O["gM"], W_M), ("full", hg)]
    (mix,) = rows_call("even_post_fwd", even_post_tile, T, post_ins, [MIX], [BF16])
    x_new = matmul("mm_out", mix, W["w_out_e"][i], "nn", add=x)
    return x_new, dict(x=x, g=g, h=h, p=p, q8=q8, k2p=k2p, v2p=v2p, sink=sink, prep_ins=prep_ins,
                       scan_srcs=scan_srcs, ss_f=ss_f, ss_b=ss_b, post_ins=post_ins, mix=mix)


def _assemble_even(dqA, dgA, dqB, dzf, dzb, dv0, dv1, dgB, dqM, dgM, dkA, dvA):
    return (jnp.concatenate([dqA, dgA, dqB, dzf, dzb, dv0 + dv1, dgB, dqM, dgM, dkA, dvA], axis=-1),)


def _even_bwd(dxo, sv, i, W, kv, slopes, T):
    O = EVEN_OFF
    p = sv["p"]
    dmix = matmul("mm_dmix", dxo, W["w_out_e"][i], "nt")
    dwo = matmul("mm_dwo", sv["mix"], dxo, "tn")
    da, dof, dmo, dgA, dgB, dgM, dhg = rows_vjp_call("even_post_bwd", even_post_tile, T, sv["post_ins"],
                                                      [[("row", dmix, 0, MIX)]], skip=(2,))
    dq8, dk2p, dv2p, dsink = attn_bwd(sv["q8"], sv["k2p"], sv["v2p"], sv["sink"], slopes, _heads_major(da, N_Q_A), T)
    dqA = _heads_minor(dq8)
    dkA = _heads_minor(dk2p[:, BLOCK:-BLOCK])
    dvA = _heads_minor(dv2p[:, BLOCK:-BLOCK])
    dqf, dkf, dgf, dvf, dqb, dkb, dgb, dvb = scan_bwd("scan_bwd_h", *sv["scan_srcs"], sv["ss_f"], sv["ss_b"], (dof, 0),
                                                      N_HEADS_B, HEAD_DIM_B, HEAD_DIM_B, T)
    row = lambda arr, w: ("row", arr, 0, w)
    dqB, dzf, dzb, dlow_f, dlow_b = rows_vjp_call(
        "hgrn_prep_bwd", hgrn_prep_tile, T, sv["prep_ins"],
        [[row(dqf, W_B), row(dqb, W_B)], [row(dkf, W_B)], [row(dkb, W_B)], [row(dgf, W_B)], [row(dgb, W_B)]])
    dlow = jnp.concatenate([dlow_f, dlow_b], axis=0)
    dqM, dkv = mem_bwd(p, O["qM"], kv, dmo, T)
    (dp,) = rows_call("even_dp", _assemble_even, T,
                      [row(dqA, W_A), row(dgA, W_A), row(dqB, W_B), row(dzf, W_B), row(dzb, W_B), row(dvf, W_B), row(dvb, W_B),
                       row(dgB, W_B), row(dqM, W_M), row(dgM, W_M), row(dkA, W_KV_A), row(dvA, W_KV_A)],
                      [EVEN_IN], [BF16])
    dh = matmul("mm_dh_e", dp, W["w_in_e"][i], "nt")
    dwi = matmul("mm_dwi_e", sv["h"], dp, "tn")
    dx, dg = rows_vjp_call("rms_res_bwd", rms_res_tile, T, [("row", sv["x"], 0, D_MODEL), ("full", sv["g"])],
                           [[("row", dh, 0, D_MODEL)], [("row", dxo, 0, D_MODEL)]])
    return dx, dict(w_in=dwi, w_out=dwo, norm=dg[0], sink=dsink.reshape(N_Q_A), low=dlow, hg=dhg[0], kv=dkv)


def _pad_gate_up(w_up):
    z = jnp.zeros((2, 128, WK_C), F32)
    z = z.at[0, 0:GATE_RANK].set(w_up[0])
    return z.at[1, GATE_RANK:2 * GATE_RANK].set(w_up[1])


def _odd_fwd(x, i, W, kv, T):
    O = ODD_OFF
    g = W["norm_odd"][i].reshape(1, D_MODEL)
    (h,) = rows_call("rms_fwd", rms_tile, T, [("row", x, 0, D_MODEL), ("full", g)], [D_MODEL], [BF16])
    p = matmul("mm_in_o", h, W["w_in_o"][i], "nn")
    wup = _pad_gate_up(W["w_gate_up"][i])
    prep_ins = [("row", p, O["qC"], WK_C), ("row", p, O["rr"], 128), ("full", wup[0]), ("full", wup[1]),
                ("full", W["b_gate"][i][0:1]), ("full", W["b_gate"][i][1:2])]
    qg, g2 = rows_call("gla_prep_fwd", gla_prep_tile, T, prep_ins, [WK_C] * 2, stacks=[(0,), (1, 2)])
    scan_srcs = [(qg, 0), (p, O["kC"]), (p, O["kC"]), (g2, 0), (g2, 1), (p, O["vC"])]
    o_f, o_b, ss_f, ss_b = scan_fwd("scan_fwd_g", *scan_srcs, N_HEADS_C, DK_C, DV_C, T)
    mo = mem_fwd(p, O["qM"], kv, T)
    gg = W["gla_norm"][i].reshape(1, WV_C)
    post_ins = [("row", o_f, 0, WV_C), ("row", o_b, 0, WV_C), ("row", mo, 0, W_M),
                ("row", p, O["gC"], WV_C), ("row", p, O["gM"], W_M), ("full", gg)]
    (mix,) = rows_call("odd_post_fwd", odd_post_tile, T, post_ins, [MIX], [BF16])
    x_new = matmul("mm_out", mix, W["w_out_o"][i], "nn", add=x)
    return x_new, dict(x=x, g=g, h=h, p=p, prep_ins=prep_ins, scan_srcs=scan_srcs, ss_f=ss_f, ss_b=ss_b,
                       post_ins=post_ins, mix=mix)


def _assemble_odd(dqC, dk0, dk1, dv0, dv1, dgC, dqM, dgM, dr):
    return (jnp.concatenate([dqC, dk0 + dk1, dv0 + dv1, dgC, dqM, dgM, dr], axis=-1),)


def _odd_bwd(dxo, sv, i, W, kv, T):
    O = ODD_OFF
    p = sv["p"]
    dmix = matmul("mm_dmix", dxo, W["w_out_o"][i], "nt")
    dwo = matmul("mm_dwo", sv["mix"], dxo, "tn")
    dof, dmo, dgC, dgM, dgg = rows_vjp_call("odd_post_bwd", odd_post_tile, T, sv["post_ins"],
                                            [[("row", dmix, 0, MIX)]], skip=(1,))
    dqf, dkf, dgf, dvf, dqb, dkb, dgb, dvb = scan_bwd("scan_bwd_g", *sv["scan_srcs"], sv["ss_f"], sv["ss_b"], (dof, 0),
                                                      N_HEADS_C, DK_C, DV_C, T)
    row = lambda arr, w: ("row", arr, 0, w)
    dqC, dr, dwup_f, dwup_b, dbg_f, dbg_b = rows_vjp_call(
        "gla_prep_bwd", gla_prep_tile, T, sv["prep_ins"],
        [[row(dqf, WK_C), row(dqb, WK_C)], [row(dgf, WK_C)], [row(dgb, WK_C)]])
    dqM, dkv = mem_bwd(p, O["qM"], kv, dmo, T)
    (dp,) = rows_call("odd_dp", _assemble_odd, T,
                      [row(dqC, WK_C), row(dkf, WK_C), row(dkb, WK_C), row(dvf, WV_C), row(dvb, WV_C),
                       row(dgC, WV_C), row(dqM, W_M), row(dgM, W_M), row(dr, 128)],
                      [ODD_PAD], [BF16])
    dh = matmul("mm_dh_o", dp, W["w_in_o"][i], "nt")
    dwi = matmul("mm_dwi_o", sv["h"], dp, "tn")
    dx, dg = rows_vjp_call("rms_res_bwd", rms_res_tile, T, [("row", sv["x"], 0, D_MODEL), ("full", sv["g"])],
                           [[("row", dh, 0, D_MODEL)], [("row", dxo, 0, D_MODEL)]])
    dw_up = jnp.stack([dwup_f[0:GATE_RANK], dwup_b[GATE_RANK:2 * GATE_RANK]])
    dbg = jnp.concatenate([dbg_f, dbg_b], axis=0)
    return dx, dict(w_in=dwi, w_out=dwo, norm=dg[0], w_up=dw_up, b_gate=dbg, gg=dgg[0], kv=dkv)


def local_step(x, mem, target, W):
    T = x.shape[0]
    slopes = (2.0 ** (-8.0 * jnp.arange(1, N_Q_A + 1, dtype=F32) / N_Q_A)).reshape(N_Q_A, 1, 1)
    lower, lower_vjp = jax.vjp(_lower_bounds, W["lb_param"])
    mem_g = W["mem_norm"].reshape(1, D_MODEL)
    (mem_n,) = rows_call("mem_rms_fwd", rms_tile, N_MEM, [("row", mem, 0, D_MODEL), ("full", mem_g)], [D_MODEL], [BF16])
    kvs = [matmul("mm_kv", mem_n, W["w_kv"][l], "nn") for l in range(DEPTH)]
    saved = []
    for l in range(DEPTH):
        if l % 2 == 0:
            x, sv = _even_fwd(x, l // 2, W, lower, kvs[l], slopes, T)
        else:
            x, sv = _odd_fwd(x, l // 2, W, kvs[l], T)
        saved.append(sv)
    loss, dx, dgf = final_call(x, W["final_norm"].reshape(1, D_MODEL), target, T)
    per = [None] * DEPTH
    for l in reversed(range(DEPTH)):
        if l % 2 == 0:
            dx, per[l] = _even_bwd(dx, saved[l], l // 2, W, kvs[l], slopes, T)
        else:
            dx, per[l] = _odd_bwd(dx, saved[l], l // 2, W, kvs[l], T)
    dmem_n, dw_kv = None, []
    for l in range(DEPTH):
        dw_kv.append(matmul("mm_dwkv", mem_n, per[l]["kv"], "tn"))
        dmem_n = matmul("mm_dmem", per[l]["kv"], W["w_kv"][l], "nt", add=dmem_n)
    (dmem_norm,) = rows_vjp_call("mem_rms_bwd", rms_tile, N_MEM, [("row", mem, 0, D_MODEL), ("full", mem_g)],
                                 [[("row", dmem_n, 0, D_MODEL)]], skip=(0,))
    ev, od = (per[0], per[2]), (per[1], per[3])
    (d_lb,) = lower_vjp(jnp.stack([e["low"] for e in ev]))
    grads = dict(
        w_in_e=jnp.stack([e["w_in"] for e in ev]), w_in_o=jnp.stack([o["w_in"] for o in od]),
        w_out_e=jnp.stack([e["w_out"] for e in ev]), w_out_o=jnp.stack([o["w_out"] for o in od]),
        w_kv=jnp.stack(dw_kv), norm_even=jnp.stack([e["norm"] for e in ev]), sink=jnp.stack([e["sink"] for e in ev]),
        lb_param=d_lb, hgrn_norm=jnp.stack([e["hg"] for e in ev]), norm_odd=jnp.stack([o["norm"] for o in od]),
        w_gate_up=jnp.stack([o["w_up"] for o in od]), b_gate=jnp.stack([o["b_gate"] for o in od]),
        gla_norm=jnp.stack([o["gg"] for o in od]), mem_norm=dmem_norm[0], final_norm=dgf[0])
    return loss, dx, grads


SMALL_SPECS = (("lb_param", (2, 2, 128)), ("norm_odd", (2, 256)), ("w_gate_up", (2, 2, 16, 128)),
               ("b_gate", (2, 2, 128)), ("gla_norm", (2, 256)))
SMALL_ROWS = 80


def _pack_small_local(d):
    return jnp.concatenate([d[n].reshape(-1) for n, _ in SMALL_SPECS]).reshape(SMALL_ROWS, 128)


def _unpack_small_local(b):
    flat, out, o = b.reshape(-1), {}, 0
    for n, shp in SMALL_SPECS:
        sz = int(np.prod(shp))
        out[n] = flat[o:o + sz].reshape(shp)
        o += sz
    return out


def _unpack_small_full(g4):
    per = [_unpack_small_local(g4[j]) for j in range(4)]
    return {n: jnp.concatenate([per[j][n] for j in range(4)], axis=-1) for n, _ in SMALL_SPECS}


def _pack_small_blocks(full):
    blocks = []
    for j in range(4):
        blocks.append(_pack_small_local({n: full[n][..., j * shp[-1]:(j + 1) * shp[-1]] for n, shp in SMALL_SPECS}))
    return jnp.stack(blocks)


def _cols(t, order, off, widths):
    return [t[..., off[n]:off[n] + widths[n]] for n in order]


EVEN_REF_ORDER = ("qA", "kA", "vA", "gA", "qB", "zf", "zb", "iB", "gB", "qM", "gM")
ODD_REF_ORDER = ("qC", "kC", "vC", "gC", "rr", "qM", "gM")


def _full_weights(gathered, gsmall, rep):
    g_in_e, g_in_o, g_out_e, g_out_o, g_kv = gathered
    t = g_in_e.reshape(4, 2, D_MODEL, EVEN_IN // 4).transpose(1, 2, 0, 3).reshape(2, D_MODEL, EVEN_IN)
    w_in_e = jnp.concatenate(_cols(t, EVEN_ORDER, EVEN_REF_OFF, EVEN_W), axis=-1)
    t = g_in_o.reshape(4, 2, D_MODEL, ODD_IN // 4).transpose(1, 2, 0, 3).reshape(2, D_MODEL, ODD_IN)
    w_in_o = jnp.concatenate(_cols(t, ODD_ORDER, ODD_REF_OFF, ODD_W) + [jnp.zeros((2, D_MODEL, ODD_PAD - ODD_IN), BF16)],
                             axis=-1)
    blocks_to_rows = lambda g, n: g.reshape(4, n, g.shape[1] // n, g.shape[2]).transpose(1, 0, 2, 3).reshape(
        n, 4 * (g.shape[1] // n), g.shape[2])
    W = dict(w_in_e=w_in_e, w_in_o=w_in_o, w_out_e=blocks_to_rows(g_out_e, 2), w_out_o=blocks_to_rows(g_out_o, 2),
             w_kv=blocks_to_rows(g_kv, DEPTH))
    W.update(_unpack_small_full(gsmall))
    W.update(rep)
    return W


def _grad_blocks(grads):
    t = jnp.concatenate(_cols(grads["w_in_e"], EVEN_REF_ORDER, EVEN_OFF, EVEN_W), axis=-1)
    b_in_e = t.reshape(2, D_MODEL, 4, EVEN_IN // 4).transpose(2, 0, 1, 3).reshape(4, 2 * D_MODEL, EVEN_IN // 4)
    t = jnp.concatenate(_cols(grads["w_in_o"], ODD_REF_ORDER, ODD_OFF, ODD_W), axis=-1)
    b_in_o = t.reshape(2, D_MODEL, 4, ODD_IN // 4).transpose(2, 0, 1, 3).reshape(4, 2 * D_MODEL, ODD_IN // 4)
    rows_to_blocks = lambda g: g.reshape(g.shape[0], 4, g.shape[1] // 4, g.shape[2]).transpose(1, 0, 2, 3).reshape(
        4, g.shape[0] * (g.shape[1] // 4), g.shape[2])
    return [b_in_e, b_in_o, rows_to_blocks(grads["w_out_e"]), rows_to_blocks(grads["w_out_o"]),
            rows_to_blocks(grads["w_kv"]), _pack_small_blocks(grads)]


WEIGHT_NAMES = ("norm_even", "w_in_even", "sink", "lb_param", "hgrn_norm", "w_out_even", "norm_odd", "w_in_odd",
                "w_gate_up", "b_gate", "gla_norm", "w_out_odd", "mem_norm", "w_mem_kv", "final_norm")


def kernel(x, mem, norm_even, w_in_even, sink, lb_param, hgrn_norm, w_out_even, norm_odd, w_in_odd, w_gate_up, b_gate, gla_norm, w_out_odd, mem_norm, w_mem_kv, final_norm, loss_target, m_norm_even, m_w_in_even, m_sink, m_lb_param, m_hgrn_norm, m_w_out_even, m_norm_odd, m_w_in_odd, m_w_gate_up, m_b_gate, m_gla_norm, m_w_out_odd, m_mem_norm, m_w_mem_kv, m_final_norm, v_norm_even, v_w_in_even, v_sink, v_lb_param, v_hgrn_norm, v_w_out_even, v_norm_odd, v_w_in_odd, v_w_gate_up, v_b_gate, v_gla_norm, v_w_out_odd, v_mem_norm, v_w_mem_kv, v_final_norm):
    w = dict(zip(WEIGHT_NAMES, (norm_even, w_in_even, sink, lb_param, hgrn_norm, w_out_even, norm_odd, w_in_odd,
                                w_gate_up, b_gate, gla_norm, w_out_odd, mem_norm, w_mem_kv, final_norm)))
    m = dict(zip(WEIGHT_NAMES, (m_norm_even, m_w_in_even, m_sink, m_lb_param, m_hgrn_norm, m_w_out_even, m_norm_odd,
                                m_w_in_odd, m_w_gate_up, m_b_gate, m_gla_norm, m_w_out_odd, m_mem_norm, m_w_mem_kv,
                                m_final_norm)))
    v = dict(zip(WEIGHT_NAMES, (v_norm_even, v_w_in_even, v_sink, v_lb_param, v_hgrn_norm, v_w_out_even, v_norm_odd,
                                v_w_in_odd, v_w_gate_up, v_b_gate, v_gla_norm, v_w_out_odd, v_mem_norm, v_w_mem_kv,
                                v_final_norm)))
    ci = lax.axis_index("c").astype(jnp.int32).reshape(1)
    chip = (2 * lax.axis_index("x") + lax.axis_index("y")).astype(jnp.int32).reshape(1)

    flat2 = lambda t: t.reshape(-1, t.shape[-1])
    shards = [flat2(w[n]).astype(BF16) for n in ("w_in_even", "w_in_odd", "w_out_even", "w_out_odd", "w_mem_kv")]
    small = _pack_small_local(w)
    remote = gather_weights(shards, small)
    own = lambda g, s: lax.dynamic_update_slice(g, s[None], (chip[0], 0, 0))
    *gathered, gsmall = [own(g, s) for g, s in zip(remote, shards + [small])]
    rep = {n: w[n] for n in ("norm_even", "sink", "hgrn_norm", "mem_norm", "final_norm")}
    W = _full_weights(gathered, gsmall, rep)

    loss_tile, dx, grads = local_step(x[0], mem[0], loss_target[0], W)

    blocks = _grad_blocks(grads)
    recv = rs_exchange_siblings(blocks)
    wire = [BF16] * (len(blocks) - 1) + [F32]
    chip_sums = [add_sibling(g, r, ci, dt) for g, r, dt in zip(blocks, recv, wire)]
    recv3 = rs_exchange_chips(chip_sums)
    place = jnp.concatenate([chip, ci])
    halves = [add_chips(g, r, r3, place) for g, r, r3 in zip(blocks, recv, recv3)]
    g_in_e, g_in_o, g_out_e, g_out_o, g_kv, g_small = rs_share_final(halves)
    gl = _unpack_small_local(g_small)
    gl.update(w_in_even=g_in_e.reshape(w_in_even.shape), w_in_odd=g_in_o.reshape(w_in_odd.shape),
              w_out_even=g_out_e.reshape(w_out_even.shape), w_out_odd=g_out_o.reshape(w_out_odd.shape),
              w_mem_kv=g_kv.reshape(w_mem_kv.shape))

    pack = jnp.zeros((8, D_MODEL), F32)
    pack = pack.at[0:2].set(grads["norm_even"]).at[2].set(grads["hgrn_norm"].reshape(-1))
    pack = pack.at[3].set(grads["mem_norm"]).at[4].set(grads["final_norm"])
    pack = pack.at[5, 0:16].set(grads["sink"].reshape(-1)).at[5, 16].set(loss_tile[0, 0])
    tot = sum_devices(allgather_small(pack))
    gl.update(norm_even=tot[0:2], hgrn_norm=tot[2].reshape(2, W_B), mem_norm=tot[3], final_norm=tot[4],
              sink=tot[5, 0:16].reshape(2, N_Q_A))
    loss = tot[5, 16]

    upd = {n: adamw_call(w[n], gl[n], m[n], v[n]) for n in WEIGHT_NAMES}
    return (loss, dx[None], *[gl[n] for n in WEIGHT_NAMES], *[upd[n][0] for n in WEIGHT_NAMES],
            *[upd[n][1] for n in WEIGHT_NAMES], *[upd[n][2] for n in WEIGHT_NAMES])
```

```python
import functools

import numpy as np
import jax
import jax.numpy as jnp
from jax import lax
from jax.experimental import pallas as pl
from jax.experimental.pallas import tpu as pltpu

F32 = jnp.float32
BF16 = jnp.bfloat16

D_MODEL = 1024
DEPTH = 4
N_Q_A, N_KV_A, HEAD_DIM_A = 8, 2, 64
W_A, W_KV_A = 512, 128
WINDOW = 128
BLOCK = 128
N_HEADS_B, HEAD_DIM_B, W_B = 4, 128, 512
N_HEADS_C, DK_C, DV_C, WK_C, WV_C = 4, 128, 256, 512, 1024
GATE_RANK = 16
GATE_TEMP = 16.0
N_MEM, N_HEADS_M, HEAD_DIM_M, W_M = 256, 4, 128, 512
EPS = 1e-6
MASK_VALUE = -1e30
MIN_GATE = 1e-30
EVEN_IN, ODD_IN = 4864, 4128
ODD_PAD = 4224
MIX = 1536
ADAM_LR, ADAM_B1, ADAM_B2, ADAM_EPS, ADAM_WD, ADAM_STEP = 0.001, 0.9, 0.999, 1e-08, 0.01, 10

SCAN_CHUNK = 128
SCAN_LEVELS = 7
VMEM_LIMIT = 56 * 1024 * 1024

EVEN_REF_OFF = dict(qA=0, kA=512, vA=640, gA=768, qB=1280, zf=1792, zb=2304, iB=2816, gB=3328, qM=3840, gM=4352)
EVEN_W = dict(qA=512, kA=128, vA=128, gA=512, qB=512, zf=512, zb=512, iB=512, gB=512, qM=512, gM=512)
EVEN_ORDER = ("qA", "gA", "qB", "zf", "zb", "iB", "gB", "qM", "gM", "kA", "vA")
ODD_REF_OFF = dict(qC=0, kC=512, vC=1024, gC=2048, rr=3072, qM=3104, gM=3616)
ODD_W = dict(qC=512, kC=512, vC=1024, gC=1024, rr=32, qM=512, gM=512)
ODD_ORDER = ("qC", "kC", "vC", "gC", "qM", "gM", "rr")


def _offsets(order, widths):
    off, o = {}, 0
    for n in order:
        off[n] = o
        o += widths[n]
    return off


EVEN_OFF = _offsets(EVEN_ORDER, EVEN_W)
ODD_OFF = _offsets(ODD_ORDER, ODD_W)


def _dg(a, b, ca, cb):
    return lax.dot_general(a.astype(BF16), b.astype(BF16), (((ca,), (cb,)), ((), ())),
                           preferred_element_type=F32)


def dot_nn(a, b):
    return _dg(a, b, 1, 0)


def dot_nt(a, b):
    return _dg(a, b, 1, 1)


def dot_tn(a, b):
    return _dg(a, b, 0, 0)


@jax.custom_vjp
def bdot(a, b):
    return dot_nn(a, b)


bdot.defvjp(lambda a, b: (dot_nn(a, b), (a, b)),
            lambda r, g: (dot_nt(g, r[1]), dot_tn(r[0], g)))


@jax.custom_vjp
def bdot_t(a, b):
    return dot_nt(a, b)


bdot_t.defvjp(lambda a, b: (dot_nt(a, b), (a, b)),
              lambda r, g: (dot_nn(g, r[1]), dot_tn(g, r[0])))


@jax.custom_vjp
def bdot_tn(a, b):
    return dot_tn(a, b)


bdot_tn.defvjp(lambda a, b: (dot_tn(a, b), (a, b)),
               lambda r, g: (dot_nt(r[1], g), dot_nn(r[0], g)))


def _split_mm(h, x):
    hi = x.astype(BF16)
    lo = (x - hi.astype(F32)).astype(BF16)
    return (lax.dot_general(h, hi, (((1,), (0,)), ((), ())), preferred_element_type=F32)
            + lax.dot_general(h, lo, (((1,), (0,)), ((), ())), preferred_element_type=F32))


@jax.custom_vjp
def hdot(h, ht, x):
    return _split_mm(h, x)


hdot.defvjp(lambda h, ht, x: (_split_mm(h, x), (h, ht)),
            lambda r, g: (jnp.zeros_like(r[0]), jnp.zeros_like(r[1]), _split_mm(r[1], g)))


def _sigmoid(z):
    return 1.0 / (1.0 + jnp.exp(-z))


def _silu(z):
    return z * _sigmoid(z)


def _log_sigmoid(z):
    return jnp.minimum(z, 0.0) - jnp.log(1.0 + jnp.exp(-jnp.abs(z)))


def _rms(x, g):
    return x * lax.rsqrt(jnp.mean(x * x, axis=-1, keepdims=True) + EPS) * g


def rms_tile(x, g):
    return (_rms(x, g),)


@functools.partial(jax.custom_vjp, nondiff_argnums=(1, 2))
def split(x, n, axis):
    w = x.shape[axis] // n
    return tuple(lax.slice_in_dim(x, h * w, (h + 1) * w, axis=axis) for h in range(n))


split.defvjp(lambda x, n, axis: (split(x, n, axis), None),
             lambda n, axis, _, cts: (jnp.concatenate(cts, axis=axis),))


def _group_rms(o, g, heads):
    return jnp.concatenate([_rms(oh, gh) for oh, gh in zip(split(o, heads, 1), split(g, heads, 1))], axis=-1)


def even_post_tile(a, o2f, o2b, mo, gA, gB, gM, hg):
    y = _group_rms(o2f + o2b, hg, N_HEADS_B)
    return (jnp.concatenate([a * _silu(gA), y * _silu(gB), mo * _silu(gM)], axis=-1),)


def odd_post_tile(o2f, o2b, mo, gC, gM, gg):
    y = _group_rms(o2f + o2b, gg, N_HEADS_C)
    return (jnp.concatenate([y * _silu(gC), mo * _silu(gM)], axis=-1),)


def hgrn_prep_tile(qB, zf, zb, low_f, low_b):
    ks, gs = [], []
    for z, lb in ((zf, low_f), (zb, low_b)):
        f = lb + (1.0 - lb) * _sigmoid(z)
        gs.append(jnp.log(jnp.maximum(f, MIN_GATE)))
        ks.append((1.0 - lb) * _sigmoid(-z))
    return (_silu(qB), ks[0], ks[1], gs[0], gs[1])


def gla_prep_tile(qC, r128, wup_f, wup_b, bg_f, bg_b):
    gs = [_log_sigmoid(bdot(r128, wup) + bg) / GATE_TEMP for wup, bg in ((wup_f, bg_f), (wup_b, bg_b))]
    return (qC * (DK_C ** -0.5), gs[0], gs[1])


def mem_tile(q, k, v):
    s = bdot_t(q, k) * (HEAD_DIM_M ** -0.5)
    m = lax.stop_gradient(jnp.max(s, axis=-1, keepdims=True))
    p = jnp.exp(s - m)
    p = p / jnp.sum(p, axis=-1, keepdims=True)
    return (bdot(p, v),)


def attn_block(qs, ks, vs, sinks, slopes, c, seq):
    i = lax.broadcasted_iota(jnp.int32, (BLOCK, 3 * BLOCK), 0)
    j = lax.broadcasted_iota(jnp.int32, (BLOCK, 3 * BLOCK), 1)
    dist = jnp.abs(i - j + BLOCK).astype(F32)
    kpos = (c - 1) * BLOCK + j
    valid = (dist <= WINDOW) & (kpos >= 0) & (kpos < seq)
    outs = []
    for q, sk, slope in zip(qs, sinks, slopes):
        s = bdot_t(q, ks) * (HEAD_DIM_A ** -0.5)
        s = jnp.where(valid, s - slope * dist, MASK_VALUE)
        m = lax.stop_gradient(jnp.maximum(jnp.max(s, axis=-1, keepdims=True), sk))
        p = jnp.where(valid, jnp.exp(s - m), 0.0)
        denom = jnp.sum(p, axis=-1, keepdims=True) + jnp.exp(sk - m)
        outs.append(bdot(p, vs) / denom)
    return tuple(outs)


def scan_chunk(q, k, v, g, st, h, ht, qm, km, bm):
    C = SCAN_CHUNK
    e = split(hdot(h, ht, g), 2 + SCAN_LEVELS, 0)
    qe = q * jnp.exp(e[0])
    kd = k * jnp.exp(e[1])
    tot = jnp.sum(g, axis=0, keepdims=True)
    r = lax.broadcasted_iota(jnp.int32, (C, C), 0)
    s = lax.broadcasted_iota(jnp.int32, (C, C), 1)
    a = jnp.where(r == s, jnp.sum(q * k, axis=-1, keepdims=True), 0.0)
    for l in range(SCAN_LEVELS):
        el = jnp.exp(e[2 + l])
        qs = q * el * qm[l * C:(l + 1) * C]
        ks = k * el * km[l * C:(l + 1) * C]
        a = a + bdot_t(qs, ks) * bm[l * C:(l + 1) * C]
    o = bdot_t(qe, st) + bdot(a, v)
    st_new = st * jnp.exp(tot) + bdot_tn(v, kd)
    return o, st_new


def _scan_consts():
    C, L = SCAN_CHUNK, SCAN_LEVELS
    t = np.arange(C)[:, None]
    r = np.arange(C)[None, :]
    blocks = [(r <= t), (r > t)]
    qms, kms, bms = [], [], []
    for l in range(1, L + 1):
        m = C >> l
        upper_t = (t % (2 * m)) >= m
        same_half = (t // m) == (r // m)
        blocks.append(same_half & np.where(upper_t, r <= t, r > t))
        qms.append(np.broadcast_to(upper_t, (C, C)))
        kms.append(np.broadcast_to(~upper_t, (C, C)))
        bms.append((t // (2 * m)) == (r // (2 * m)))
    hf = np.concatenate(blocks, axis=0).astype(np.float32)
    flip = lambda mat: mat.reshape(-1, C, C)[:, ::-1, ::-1].reshape(-1, C)
    hb = flip(hf)
    qmf = np.concatenate(qms, axis=0).astype(np.float32)
    kmf = np.concatenate(kms, axis=0).astype(np.float32)
    bm = np.concatenate(bms, axis=0).astype(np.float32)
    h = np.stack([hf, hb])
    ht = np.stack([hf.T, hb.T])
    qm = np.stack([qmf, kmf])
    km = np.stack([kmf, qmf])
    return h, ht, qm, km, bm


def _cparams(sem):
    return pltpu.CompilerParams(dimension_semantics=sem, vmem_limit_bytes=VMEM_LIMIT)


def _row_tile(T):
    return min(T, 256)


def _in_spec(spec, tr):
    kind = spec[0]
    if kind == "row":
        _, arr, off, w = spec
        assert off % w == 0
        return arr, pl.BlockSpec((tr, w), functools.partial(lambda i, b: (i, b), b=off // w))
    if kind == "row3":
        _, arr, d, off, w = spec
        assert off % w == 0
        return arr, pl.BlockSpec((None, tr, w), functools.partial(lambda i, d, b: (d, i, b), d=d, b=off // w))
    _, arr = spec
    return arr, pl.BlockSpec(arr.shape, functools.partial(lambda i, n: (0,) * n, n=arr.ndim))


def rows_call(name, tile_fn, T, ins, out_widths, out_dtypes=None, stacks=None):
    tr = _row_tile(T)
    n_in = len(ins)
    out_dtypes = out_dtypes or [F32] * len(out_widths)
    stacks = stacks or [(k,) for k in range(len(out_widths))]

    def body(*refs):
        vals = [r[...] for r in refs[:n_in]]
        outs = tile_fn(*vals)
        for r, members in zip(refs[n_in:], stacks):
            if len(members) == 1:
                r[...] = outs[members[0]].astype(r.dtype)
            else:
                for d, k in enumerate(members):
                    r[d] = outs[k].astype(r.dtype)

    in_specs, args = [], []
    for spec in ins:
        arr, bs = _in_spec(spec, tr)
        args.append(arr)
        in_specs.append(bs)
    out_specs, out_shape = [], []
    for w, dt, members in zip(out_widths, out_dtypes, stacks):
        n = len(members)
        if n == 1:
            out_specs.append(pl.BlockSpec((tr, w), lambda i: (i, 0)))
            out_shape.append(jax.ShapeDtypeStruct((T, w), dt))
        else:
            out_specs.append(pl.BlockSpec((n, tr, w), lambda i: (0, i, 0)))
            out_shape.append(jax.ShapeDtypeStruct((n, T, w), dt))
    return pl.pallas_call(body, out_shape=out_shape, grid=(T // tr,), in_specs=in_specs, out_specs=out_specs,
                          name=name, compiler_params=_cparams(("arbitrary",)))(*args)


def rows_vjp_call(name, tile_fn, T, ins, cts, skip=()):
    tr = _row_tile(T)
    n_in = len(ins)
    n_ct = [len(c) for c in cts]
    want = [k for k in range(n_in) if k not in skip]

    def body(*refs):
        i = pl.program_id(0)
        vals = [r[...] for r in refs[:n_in]]
        ct, pos = [], n_in
        for n in n_ct:
            acc = refs[pos][...]
            for r in refs[pos + 1:pos + n]:
                acc = acc + r[...]
            ct.append(acc)
            pos += n
        _, vjp = jax.vjp(tile_fn, *vals)
        grads = vjp(tuple(ct))
        for r, k in zip(refs[pos:], want):
            if ins[k][0] == "full":
                @pl.when(i == 0)
                def _():
                    r[...] = jnp.zeros_like(r)
                r[...] += grads[k]
            else:
                r[...] = grads[k]

    in_specs, args = [], []
    for spec in list(ins) + [s for c in cts for s in c]:
        arr, bs = _in_spec(spec, tr)
        args.append(arr)
        in_specs.append(bs)
    out_specs, out_shape = [], []
    for k in want:
        if ins[k][0] == "full":
            arr = ins[k][1]
            out_specs.append(pl.BlockSpec(arr.shape, functools.partial(lambda i, n: (0,) * n, n=arr.ndim)))
            out_shape.append(jax.ShapeDtypeStruct(arr.shape, F32))
        else:
            w = ins[k][-1]
            out_specs.append(pl.BlockSpec((tr, w), lambda i: (i, 0)))
            out_shape.append(jax.ShapeDtypeStruct((T, w), F32))
    return pl.pallas_call(body, out_shape=out_shape, grid=(T // tr,), in_specs=in_specs, out_specs=out_specs,
                          name=name, compiler_params=_cparams(("arbitrary",)))(*args)


def matmul(name, a, b, mode, add=None, out_dtype=F32):
    if mode == "tn":
        K, M = a.shape
        N = b.shape[1]
        tm = M if M <= 1536 else 512
        tn = N if N <= 1280 else (N // 2 if (N // 2) % 128 == 0 else N)
        tk = min(K, 512)
        grid = (M // tm, N // tn, K // tk)

        def body(a_ref, b_ref, o_ref):
            @pl.when(pl.program_id(2) == 0)
            def _():
                o_ref[...] = jnp.zeros_like(o_ref)
            o_ref[...] += dot_tn(a_ref[...], b_ref[...])

        return pl.pallas_call(
            body, out_shape=jax.ShapeDtypeStruct((M, N), F32), grid=grid,
            in_specs=[pl.BlockSpec((tk, tm), lambda i, j, k: (k, i)), pl.BlockSpec((tk, tn), lambda i, j, k: (k, j))],
            out_specs=pl.BlockSpec((tm, tn), lambda i, j, k: (i, j)), name=name,
            compiler_params=_cparams(("arbitrary", "arbitrary", "arbitrary")))(a, b)

    M, K = a.shape
    N = b.shape[1] if mode == "nn" else b.shape[0]
    tm = min(M, 256)
    tn = N if N <= 1536 else (N // 2 if (N // 2) % 128 == 0 else (N // 3 if (N // 3) % 128 == 0 else N))
    grid = (N // tn, M // tm)
    n_in = 2 + (add is not None)

    def body(*refs):
        a_ref, b_ref = refs[0], refs[1]
        o_ref = refs[n_in]
        acc = dot_nn(a_ref[...], b_ref[...]) if mode == "nn" else dot_nt(a_ref[...], b_ref[...])
        if add is not None:
            acc = acc + refs[2][...]
        o_ref[...] = acc.astype(o_ref.dtype)

    in_specs = [pl.BlockSpec((tm, K), lambda j, i: (i, 0)),
                pl.BlockSpec((K, tn), lambda j, i: (0, j)) if mode == "nn" else pl.BlockSpec((tn, K), lambda j, i: (j, 0))]
    args = [a, b]
    if add is not None:
        in_specs.append(pl.BlockSpec((tm, tn), lambda j, i: (i, j)))
        args.append(add)
    return pl.pallas_call(
        body, out_shape=jax.ShapeDtypeStruct((M, N), out_dtype), grid=grid, in_specs=in_specs,
        out_specs=pl.BlockSpec((tm, tn), lambda j, i: (i, j)), name=name,
        compiler_params=_cparams(("arbitrary", "arbitrary")))(*args)


def _attn_heads(n):
    G = N_Q_A // N_KV_A
    k_sl = pl.ds(n * HEAD_DIM_A, HEAD_DIM_A)
    v_sl = pl.ds(W_KV_A + n * HEAD_DIM_A, HEAD_DIM_A)
    q_sl = [pl.ds((n * G + g) * HEAD_DIM_A, HEAD_DIM_A) for g in range(G)]
    return k_sl, v_sl, q_sl, range(n * G, (n + 1) * G)


def attn_fwd(p, q_off, kvp, sink, slopes, T):
    nb = T // BLOCK
    assert q_off % W_A == 0

    def body(q_ref, kv_ref, sink_ref, slope_ref, o_ref):
        c = pl.program_id(0)
        rows = pl.ds(pl.multiple_of(c * BLOCK, BLOCK), 3 * BLOCK)
        for n in range(N_KV_A):
            k_sl, v_sl, q_sl, heads = _attn_heads(n)
            outs = attn_block([q_ref[:, s] for s in q_sl], kv_ref[rows, k_sl], kv_ref[rows, v_sl],
                              [sink_ref[h] for h in heads], [slope_ref[h] for h in heads], c, T)
            for s, o in zip(q_sl, outs):
                o_ref[:, s] = o

    full = lambda a: pl.BlockSpec(a.shape, functools.partial(lambda c, nd: (0,) * nd, nd=a.ndim))
    return pl.pallas_call(
        body, out_shape=jax.ShapeDtypeStruct((T, W_A), F32), grid=(nb,),
        in_specs=[pl.BlockSpec((BLOCK, W_A), lambda c: (c, q_off // W_A)), full(kvp), full(sink), full(slopes)],
        out_specs=pl.BlockSpec((BLOCK, W_A), lambda c: (c, 0)),
        name="attn_fwd", compiler_params=_cparams(("arbitrary",)))(p, kvp, sink, slopes)


def attn_bwd(p, q_off, kvp, sink, slopes, do, T):
    nb = T // BLOCK

    def body(q_ref, kv_ref, sink_ref, slope_ref, do_ref, dq_ref, dkv_ref, dsink_ref):
        c = pl.program_id(0)
        rows = pl.ds(pl.multiple_of(c * BLOCK, BLOCK), 3 * BLOCK)

        @pl.when(c == 0)
        def _():
            dkv_ref[...] = jnp.zeros_like(dkv_ref)
            dsink_ref[...] = jnp.zeros_like(dsink_ref)

        for n in range(N_KV_A):
            k_sl, v_sl, q_sl, heads = _attn_heads(n)
            slopes_n = [slope_ref[h] for h in heads]
            _, vjp = jax.vjp(lambda qs, kk, vv, sks: attn_block(qs, kk, vv, sks, slopes_n, c, T),
                             [q_ref[:, s] for s in q_sl], kv_ref[rows, k_sl], kv_ref[rows, v_sl],
                             [sink_ref[h] for h in heads])
            dqs, dks, dvs, dsks = vjp(tuple(do_ref[:, s] for s in q_sl))
            dkv_ref[rows, k_sl] += dks
            dkv_ref[rows, v_sl] += dvs
            for s, h, dq, dsk in zip(q_sl, heads, dqs, dsks):
                dq_ref[:, s] = dq
                dsink_ref[h] += dsk

    full = lambda a: pl.BlockSpec(a.shape, functools.partial(lambda c, nd: (0,) * nd, nd=a.ndim))
    qspec = pl.BlockSpec((BLOCK, W_A), lambda c: (c, 0))
    return pl.pallas_call(
        body,
        out_shape=[jax.ShapeDtypeStruct((T, W_A), F32), jax.ShapeDtypeStruct(kvp.shape, F32),
                   jax.ShapeDtypeStruct((N_Q_A, 1, 1), F32)],
        grid=(nb,),
        in_specs=[pl.BlockSpec((BLOCK, W_A), lambda c: (c, q_off // W_A)), full(kvp), full(sink), full(slopes), qspec],
        out_specs=[qspec, full(kvp), full(sink)],
        name="attn_bwd", compiler_params=_cparams(("arbitrary",)))(p, kvp, sink, slopes, do)


def mem_fwd(p, q_off, kv, T):
    tr = _row_tile(T)
    assert q_off % W_M == 0

    def body(q_ref, kv_ref, o_ref):
        for h in range(N_HEADS_M):
            hs = pl.ds(h * HEAD_DIM_M, HEAD_DIM_M)
            (o,) = mem_tile(q_ref[:, hs], kv_ref[:, hs], kv_ref[:, pl.ds(W_M + h * HEAD_DIM_M, HEAD_DIM_M)])
            o_ref[:, hs] = o

    return pl.pallas_call(
        body, out_shape=jax.ShapeDtypeStruct((T, W_M), F32), grid=(T // tr,),
        in_specs=[pl.BlockSpec((tr, W_M), lambda i: (i, q_off // W_M)), pl.BlockSpec((N_MEM, 2 * W_M), lambda i: (0, 0))],
        out_specs=pl.BlockSpec((tr, W_M), lambda i: (i, 0)),
        name="mem_fwd", compiler_params=_cparams(("arbitrary",)))(p, kv)


def mem_bwd(p, q_off, kv, do, T):
    tr = _row_tile(T)

    def body(q_ref, kv_ref, do_ref, dq_ref, dkv_ref):
        @pl.when(pl.program_id(0) == 0)
        def _():
            dkv_ref[...] = jnp.zeros_like(dkv_ref)

        for h in range(N_HEADS_M):
            hs = pl.ds(h * HEAD_DIM_M, HEAD_DIM_M)
            vs = pl.ds(W_M + h * HEAD_DIM_M, HEAD_DIM_M)
            _, vjp = jax.vjp(mem_tile, q_ref[:, hs], kv_ref[:, hs], kv_ref[:, vs])
            dq, dk, dv = vjp((do_ref[:, hs],))
            dq_ref[:, hs] = dq
            dkv_ref[:, hs] += dk
            dkv_ref[:, vs] += dv

    kvspec = pl.BlockSpec((N_MEM, 2 * W_M), lambda i: (0, 0))
    return pl.pallas_call(
        body,
        out_shape=[jax.ShapeDtypeStruct((T, W_M), F32), jax.ShapeDtypeStruct((N_MEM, 2 * W_M), F32)],
        grid=(T // tr,),
        in_specs=[pl.BlockSpec((tr, W_M), lambda i: (i, q_off // W_M)), kvspec, pl.BlockSpec((tr, W_M), lambda i: (i, 0))],
        out_specs=[pl.BlockSpec((tr, W_M), lambda i: (i, 0)), kvspec],
        name="mem_bwd", compiler_params=_cparams(("arbitrary",)))(p, kv, do)


def _scan_const_specs(dk):
    C, L = SCAN_CHUNK, SCAN_LEVELS
    return [pl.BlockSpec((2, (2 + L) * C, C), lambda n: (0, 0, 0)),
            pl.BlockSpec((2, C, (2 + L) * C), lambda n: (0, 0, 0)),
            pl.BlockSpec((2, L * C, dk), lambda n: (0, 0, 0)),
            pl.BlockSpec((2, L * C, dk), lambda n: (0, 0, 0)),
            pl.BlockSpec((L * C, C), lambda n: (0, 0))]


def _chunk_spec(src, width, chunk_of):
    arr, sel = src
    if arr.ndim == 2:
        assert sel % width == 0
        return pl.BlockSpec((SCAN_CHUNK, width), functools.partial(lambda n, b: (chunk_of(n), b), b=sel // width))
    return pl.BlockSpec((None, SCAN_CHUNK, width), functools.partial(lambda n, d: (d, chunk_of(n), 0), d=sel))


def _scan_const_args():
    h, ht, qm, km, bm = _scan_consts()
    return [jnp.asarray(h, BF16), jnp.asarray(ht, BF16), jnp.asarray(qm, F32), jnp.asarray(km, F32), jnp.asarray(bm, F32)]


def scan_fwd(name, q, kf, kb, gf, gb, v, heads, dk, dv, T):
    C = SCAN_CHUNK
    N = T // C
    assert dk == C
    W, Wv = heads * dk, heads * dv
    fwd = lambda n: n
    rev = lambda n: N - 1 - n

    def body(qf_ref, qb_ref, kf_ref, kb_ref, gf_ref, gb_ref, vf_ref, vb_ref, h_ref, ht_ref, qm_ref, km_ref, bm_ref,
             of_ref, ob_ref, ssf_ref, ssb_ref, st_ref):
        @pl.when(pl.program_id(0) == 0)
        def _():
            st_ref[...] = jnp.zeros_like(st_ref)

        bm = bm_ref[...]
        dirs = ((qf_ref, kf_ref, gf_ref, vf_ref, of_ref, ssf_ref), (qb_ref, kb_ref, gb_ref, vb_ref, ob_ref, ssb_ref))
        for d, (q_r, k_r, g_r, v_r, o_r, ss_r) in enumerate(dirs):
            consts = (h_ref[d], ht_ref[d], qm_ref[d], km_ref[d], bm)
            for h in range(heads):
                ks, vs = pl.ds(h * dk, dk), pl.ds(h * dv, dv)
                st = st_ref[d, h]
                ss_r[h] = st
                o, st_new = scan_chunk(q_r[:, ks], k_r[:, ks], v_r[:, vs], g_r[:, ks], st, *consts)
                o_r[:, vs] = o
                st_ref[d, h] = st_new

    srcs = [(q, fwd, W), (q, rev, W), (kf, fwd, W), (kb, rev, W), (gf, fwd, W), (gb, rev, W), (v, fwd, Wv), (v, rev, Wv)]
    ss_spec = lambda order: pl.BlockSpec((heads, None, dv, dk), lambda n: (0, order(n), 0, 0))
    return pl.pallas_call(
        body,
        out_shape=[jax.ShapeDtypeStruct((T, Wv), F32)] * 2 + [jax.ShapeDtypeStruct((heads, N, dv, dk), F32)] * 2,
        grid=(N,),
        in_specs=[_chunk_spec(s, w, order) for s, order, w in srcs] + _scan_const_specs(dk),
        out_specs=[pl.BlockSpec((C, Wv), lambda n: (fwd(n), 0)), pl.BlockSpec((C, Wv), lambda n: (rev(n), 0)),
                   ss_spec(fwd), ss_spec(rev)],
        scratch_shapes=[pltpu.VMEM((2, heads, dv, dk), F32)],
        name=name, compiler_params=_cparams(("arbitrary",)))(*[s[0] for s, _, _ in srcs], *_scan_const_args())


def scan_bwd(name, q, kf, kb, gf, gb, v, ss_f, ss_b, do, heads, dk, dv, T):
    C = SCAN_CHUNK
    N = T // C
    W, Wv = heads * dk, heads * dv
    fwd = lambda n: N - 1 - n
    rev = lambda n: n

    def body(qf_ref, qb_ref, kf_ref, kb_ref, gf_ref, gb_ref, vf_ref, vb_ref, ssf_ref, ssb_ref, dof_ref, dob_ref,
             h_ref, ht_ref, qm_ref, km_ref, bm_ref,
             dqf_ref, dkf_ref, dgf_ref, dvf_ref, dqb_ref, dkb_ref, dgb_ref, dvb_ref, dst_ref):
        @pl.when(pl.program_id(0) == 0)
        def _():
            dst_ref[...] = jnp.zeros_like(dst_ref)

        bm = bm_ref[...]
        dirs = ((qf_ref, kf_ref, gf_ref, vf_ref, ssf_ref, dof_ref, dqf_ref, dkf_ref, dgf_ref, dvf_ref),
                (qb_ref, kb_ref, gb_ref, vb_ref, ssb_ref, dob_ref, dqb_ref, dkb_ref, dgb_ref, dvb_ref))
        for d, (q_r, k_r, g_r, v_r, ss_r, do_r, dq_r, dk_r, dg_r, dv_r) in enumerate(dirs):
            consts = (h_ref[d], ht_ref[d], qm_ref[d], km_ref[d], bm)
            for h in range(heads):
                ks, vs = pl.ds(h * dk, dk), pl.ds(h * dv, dv)
                _, vjp = jax.vjp(lambda q_, k_, v_, g_, st_: scan_chunk(q_, k_, v_, g_, st_, *consts),
                                 q_r[:, ks], k_r[:, ks], v_r[:, vs], g_r[:, ks], ss_r[h])
                dq, dk_, dv_, dg, dst = vjp((do_r[:, vs], dst_ref[d, h]))
                dq_r[:, ks] = dq
                dk_r[:, ks] = dk_
                dg_r[:, ks] = dg
                dv_r[:, vs] = dv_
                dst_ref[d, h] = dst

    srcs = [(q, fwd, W), (q, rev, W), (kf, fwd, W), (kb, rev, W), (gf, fwd, W), (gb, rev, W), (v, fwd, Wv), (v, rev, Wv)]
    ss_spec = lambda order: pl.BlockSpec((heads, None, dv, dk), lambda n: (0, order(n), 0, 0))
    kspec = lambda order: pl.BlockSpec((C, W), lambda n: (order(n), 0))
    vspec = lambda order: pl.BlockSpec((C, Wv), lambda n: (order(n), 0))
    return pl.pallas_call(
        body,
        out_shape=([jax.ShapeDtypeStruct((T, W), F32)] * 3 + [jax.ShapeDtypeStruct((T, Wv), F32)]) * 2,
        grid=(N,),
        in_specs=[_chunk_spec(s, w, order) for s, order, w in srcs]
        + [ss_spec(fwd), ss_spec(rev), _chunk_spec(do, Wv, fwd), _chunk_spec(do, Wv, rev)] + _scan_const_specs(dk),
        out_specs=[kspec(fwd)] * 3 + [vspec(fwd)] + [kspec(rev)] * 3 + [vspec(rev)],
        scratch_shapes=[pltpu.VMEM((2, heads, dv, dk), F32)],
        name=name, compiler_params=_cparams(("arbitrary",)))(
            *[s[0] for s, _, _ in srcs], ss_f, ss_b, do[0], do[0], *_scan_const_args())


def final_call(x, g, target, T):
    tr = _row_tile(T)

    def tile(xv, gv, tv):
        y = _rms(xv, gv)
        err = (y - tv) ** 2
        return jnp.sum(jnp.sum(err, axis=-1, keepdims=True), axis=0, keepdims=True) * (0.5 / D_MODEL)

    def body(x_ref, g_ref, t_ref, loss_ref, dx_ref, dg_ref):
        i = pl.program_id(0)
        tv = t_ref[...]
        lv, vjp = jax.vjp(lambda a, b: tile(a, b, tv), x_ref[...], g_ref[...])
        dx, dg = vjp(jnp.ones((1, 1), F32))
        dx_ref[...] = dx

        @pl.when(i == 0)
        def _():
            loss_ref[...] = jnp.zeros_like(loss_ref)
            dg_ref[...] = jnp.zeros_like(dg_ref)

        loss_ref[...] += jnp.broadcast_to(lv, loss_ref.shape)
        dg_ref[...] += dg

    return pl.pallas_call(
        body,
        out_shape=[jax.ShapeDtypeStruct((8, 128), F32), jax.ShapeDtypeStruct((T, D_MODEL), F32),
                   jax.ShapeDtypeStruct((1, D_MODEL), F32)],
        grid=(T // tr,),
        in_specs=[pl.BlockSpec((tr, D_MODEL), lambda i: (i, 0)), pl.BlockSpec((1, D_MODEL), lambda i: (0, 0)),
                  pl.BlockSpec((tr, D_MODEL), lambda i: (i, 0))],
        out_specs=[pl.BlockSpec((8, 128), lambda i: (0, 0)), pl.BlockSpec((tr, D_MODEL), lambda i: (i, 0)),
                   pl.BlockSpec((1, D_MODEL), lambda i: (0, 0))],
        name="final_loss", compiler_params=_cparams(("arbitrary",)))(x, g, target)


def adamw_call(w, g, m, v):
    shape = w.shape
    c = shape[-1]
    r = int(np.prod(shape[:-1])) if len(shape) > 1 else 1
    tr = r if r <= 256 else 256
    assert r % tr == 0

    def body(w_ref, g_ref, m_ref, v_ref, d_ref, nm_ref, nv_ref):
        gv = g_ref[...]
        nm = ADAM_B1 * m_ref[...] + (1.0 - ADAM_B1) * gv
        nv = ADAM_B2 * v_ref[...] + (1.0 - ADAM_B2) * jnp.square(gv)
        m_hat = nm / (1.0 - ADAM_B1 ** ADAM_STEP)
        v_hat = nv / (1.0 - ADAM_B2 ** ADAM_STEP)
        d_ref[...] = -ADAM_LR * (m_hat / (jnp.sqrt(v_hat) + ADAM_EPS) + ADAM_WD * w_ref[...])
        nm_ref[...] = nm
        nv_ref[...] = nv

    spec = pl.BlockSpec((tr, c), lambda i: (i, 0))
    outs = pl.pallas_call(body, out_shape=[jax.ShapeDtypeStruct((r, c), F32)] * 3, grid=(r // tr,),
                          in_specs=[spec] * 4, out_specs=[spec] * 3, name="adamw",
                          compiler_params=_cparams(("arbitrary",)))(*(t.reshape(r, c) for t in (w, g, m, v)))
    return tuple(o.reshape(shape) for o in outs)


def sum_devices(g64):
    def body(x_ref, o_ref):
        acc = x_ref[0:8, :]
        for d in range(1, 8):
            acc = acc + x_ref[8 * d:8 * d + 8, :]
        o_ref[...] = acc

    return pl.pallas_call(body, out_shape=jax.ShapeDtypeStruct((8, D_MODEL), F32), name="sum_devices")(g64)


def _half_tile(rh):
    return rh if rh <= 512 else 256


def add_sibling(g, recv, c, out_dtype):
    _, R, C = g.shape
    rh = R // 2
    tr = _half_tile(rh)
    nblk = rh // tr

    def body(c_ref, g_ref, r_ref, o_ref):
        o_ref[...] = (g_ref[...] + r_ref[...]).astype(o_ref.dtype)

    grid_spec = pltpu.PrefetchScalarGridSpec(
        num_scalar_prefetch=1, grid=(4, nblk),
        in_specs=[pl.BlockSpec((None, tr, C), lambda j, i, c_ref: (j, i + c_ref[0] * nblk, 0)),
                  pl.BlockSpec((None, tr, C), lambda j, i, c_ref: (j, i, 0))],
        out_specs=pl.BlockSpec((None, tr, C), lambda j, i, c_ref: (j, i, 0)))
    return pl.pallas_call(body, out_shape=jax.ShapeDtypeStruct((4, rh, C), out_dtype), grid_spec=grid_spec,
                          name="rs_add_sibling", compiler_params=_cparams(("arbitrary", "arbitrary")))(c, g, recv)


def add_chips(g, recv, r3, place):
    _, R, C = g.shape
    rh = R // 2
    tr = _half_tile(rh)
    nblk = rh // tr

    def body(p_ref, g_ref, s_ref, a_ref, b_ref, c_ref, o_ref):
        up = lambda r: r[...].astype(F32)
        o_ref[...] = (((g_ref[...] + up(s_ref)) + up(a_ref)) + up(b_ref)) + up(c_ref)

    grid_spec = pltpu.PrefetchScalarGridSpec(
        num_scalar_prefetch=1, grid=(nblk,),
        in_specs=[pl.BlockSpec((None, tr, C), lambda i, p_ref: (p_ref[0], i + p_ref[1] * nblk, 0)),
                  pl.BlockSpec((None, tr, C), lambda i, p_ref: (p_ref[0], i, 0))]
        + [pl.BlockSpec((None, tr, C), functools.partial(lambda i, p_ref, k: (k, i, 0), k=k)) for k in range(3)],
        out_specs=pl.BlockSpec((tr, C), lambda i, p_ref: (i + p_ref[1] * nblk, 0)))
    return pl.pallas_call(body, out_shape=jax.ShapeDtypeStruct((R, C), F32), grid_spec=grid_spec,
                          name="rs_add_chips", compiler_params=_cparams(("arbitrary",)))(place, g, recv, r3, r3, r3)


def _remote(src, dst, ssem, rsem, dev):
    return pltpu.make_async_remote_copy(src_ref=src, dst_ref=dst, send_sem=ssem, recv_sem=rsem,
                                        device_id=dev, device_id_type=pl.DeviceIdType.MESH)


def _mesh_places():
    x, y, c = lax.axis_index("x"), lax.axis_index("y"), lax.axis_index("c")
    chips = [(1 - x, y), (x, 1 - y), (1 - x, 1 - y)]
    return x, y, c, (x, y, 1 - c), chips


def _hbm_specs(n):
    return [pl.BlockSpec(memory_space=pltpu.HBM) for _ in range(n)]


def gather_weights(shards, small):
    nb = len(shards)

    def body(*refs):
        ins, outs = refs[:nb + 1], refs[nb + 1:2 * nb + 2]
        send_sems, recv_sems = refs[2 * nb + 2:]
        x, y, c, sibling, chips = _mesh_places()
        mine = 2 * x + y

        def half(a, chip_idx, which):
            rh = ins[a].shape[0] // 2
            return outs[a].at[chip_idx, pl.ds(which * rh, rh), :]

        sent = []
        for a in range(nb):
            rh = ins[a].shape[0] // 2
            src = ins[a].at[pl.ds(c * rh, rh), :]
            for k, chip in enumerate(chips):
                sent.append(_remote(src, half(a, mine, c), send_sems.at[a, k], recv_sems.at[a, k], (*chip, c)))
        for k, chip in enumerate(chips):
            sent.append(_remote(ins[nb], outs[nb].at[mine], send_sems.at[nb, k], recv_sems.at[nb, k], (*chip, c)))
        for cp in sent:
            cp.start()
        for a in range(nb):
            for k, chip in enumerate(chips):
                region = half(a, 2 * chip[0] + chip[1], c)
                _remote(region, region, send_sems.at[a, k], recv_sems.at[a, k], (*chip, c)).wait_recv()
                fwd = _remote(region, region, send_sems.at[a, 3 + k], recv_sems.at[a, 3 + k], sibling)
                fwd.start()
                sent.append(fwd)
        for k, chip in enumerate(chips):
            region = outs[nb].at[2 * chip[0] + chip[1]]
            _remote(region, region, send_sems.at[nb, k], recv_sems.at[nb, k], (*chip, c)).wait_recv()
        for a in range(nb):
            for k, chip in enumerate(chips):
                region = half(a, 2 * chip[0] + chip[1], 1 - c)
                _remote(region, region, send_sems.at[a, 3 + k], recv_sems.at[a, 3 + k], sibling).wait_recv()
        for cp in sent:
            cp.wait_send()

    arrs = list(shards) + [small]
    return pl.pallas_call(
        body, out_shape=[jax.ShapeDtypeStruct((4,) + a.shape, a.dtype) for a in arrs],
        in_specs=_hbm_specs(nb + 1), out_specs=_hbm_specs(nb + 1),
        scratch_shapes=[pltpu.SemaphoreType.DMA((nb + 1, 6)), pltpu.SemaphoreType.DMA((nb + 1, 6))],
        name="gather_weights")(*arrs)


def rs_exchange_siblings(gs):
    n = len(gs)

    def body(*refs):
        ins, outs = refs[:n], refs[n:2 * n]
        send_sems, recv_sems = refs[2 * n:]
        x, y, c, sibling, chips = _mesh_places()
        cps = []
        for a in range(n):
            rh = ins[a].shape[1] // 2
            cps.append(_remote(ins[a].at[:, pl.ds((1 - c) * rh, rh), :], outs[a], send_sems.at[a], recv_sems.at[a], sibling))
        for cp in cps:
            cp.start()
        for cp in cps:
            cp.wait()

    return pl.pallas_call(
        body, out_shape=[jax.ShapeDtypeStruct((4, g.shape[1] // 2, g.shape[2]), g.dtype) for g in gs],
        in_specs=_hbm_specs(n), out_specs=_hbm_specs(n),
        scratch_shapes=[pltpu.SemaphoreType.DMA((n,)), pltpu.SemaphoreType.DMA((n,))],
        name="rs_exchange_siblings")(*gs)


def rs_exchange_chips(s1s):
    n = len(s1s)

    def body(*refs):
        ins, outs = refs[:n], refs[n:2 * n]
        send_sems, recv_sems = refs[2 * n:]
        x, y, c, sibling, chips = _mesh_places()
        cps = []
        for a in range(n):
            for k, chip in enumerate(chips):
                cps.append(_remote(ins[a].at[2 * chip[0] + chip[1]], outs[a].at[k], send_sems.at[a, k],
                                   recv_sems.at[a, k], (*chip, c)))
        for cp in cps:
            cp.start()
        for cp in cps:
            cp.wait()

    return pl.pallas_call(
        body, out_shape=[jax.ShapeDtypeStruct((3,) + s.shape[1:], s.dtype) for s in s1s],
        in_specs=_hbm_specs(n), out_specs=_hbm_specs(n),
        scratch_shapes=[pltpu.SemaphoreType.DMA((n, 3)), pltpu.SemaphoreType.DMA((n, 3))],
        name="rs_exchange_chips")(*s1s)


def rs_share_final(fs):
    n = len(fs)

    def body(*refs):
        bufs = refs[n:2 * n]
        send_sems, recv_sems = refs[2 * n:]
        x, y, c, sibling, chips = _mesh_places()
        cps = []
        for a in range(n):
            rh = bufs[a].shape[0] // 2
            mine = bufs[a].at[pl.ds(c * rh, rh), :]
            cps.append(_remote(mine, mine, send_sems.at[a], recv_sems.at[a], sibling))
        for cp in cps:
            cp.start()
        for a in range(n):
            rh = bufs[a].shape[0] // 2
            other = bufs[a].at[pl.ds((1 - c) * rh, rh), :]
            _remote(other, other, send_sems.at[a], recv_sems.at[a], sibling).wait_recv()
        for cp in cps:
            cp.wait_send()

    return pl.pallas_call(
        body, out_shape=[jax.ShapeDtypeStruct(f.shape, f.dtype) for f in fs],
        in_specs=_hbm_specs(n), out_specs=_hbm_specs(n), input_output_aliases={a: a for a in range(n)},
        scratch_shapes=[pltpu.SemaphoreType.DMA((n,)), pltpu.SemaphoreType.DMA((n,))],
        name="rs_share_final")(*fs)


def allgather_small(v):
    m_per = v.shape[0]

    def body(x_ref, out_ref, send_sems, recv_sems, local_sem):
        x, y, c, sibling, chips = _mesh_places()
        me = (x, y, c)

        def rows(px, py, pc):
            return out_ref.at[pl.ds((4 * px + 2 * py + pc) * m_per, m_per), :]

        def copy(k, block, to, src=None):
            return _remote(rows(*block) if src is None else src, rows(*block), send_sems.at[k], recv_sems.at[k], to)

        mine = pltpu.make_async_copy(x_ref, rows(*me), local_sem)
        mine.start()
        first = [copy(0, me, sibling, src=x_ref)]
        first += [copy(1 + j, me, (*chip, c), src=x_ref) for j, chip in enumerate(chips)]
        for cp in first:
            cp.start()
        passed = [copy(4 + j, (*chip, c), sibling) for j, chip in enumerate(chips)]
        for j, chip in enumerate(chips):
            copy(1 + j, (*chip, c), me).wait_recv()
            passed[j].start()
        copy(0, sibling, me).wait_recv()
        for j, chip in enumerate(chips):
            copy(4 + j, (*chip, 1 - c), me).wait_recv()
        for cp in first + passed:
            cp.wait_send()
        mine.wait()

    return pl.pallas_call(
        body, out_shape=jax.ShapeDtypeStruct((8 * m_per, v.shape[1]), v.dtype),
        in_specs=[pl.BlockSpec(memory_space=pltpu.VMEM)], out_specs=pl.BlockSpec(memory_space=pltpu.VMEM),
        scratch_shapes=[pltpu.SemaphoreType.DMA((7,)), pltpu.SemaphoreType.DMA((7,)), pltpu.SemaphoreType.DMA],
        name="allgather_small")(v)


def rms_res_tile(x, g):
    return (_rms(x, g), x)


def _lower_bounds(lb_param):
    lbs = jax.nn.softmax(lb_param.astype(F32), axis=0)
    return jnp.cumsum(lbs, axis=0) - lbs[0]


def _heads_major(t, n):
    return t.reshape(t.shape[0], n, HEAD_DIM_A).transpose(1, 0, 2)


def _heads_minor(t):
    return t.transpose(1, 0, 2).reshape(t.shape[1], t.shape[0] * t.shape[2])


def _even_fwd(x, i, W, lower, kv, slopes, T):
    O = EVEN_OFF
    g = W["norm_even"][i].reshape(1, D_MODEL)
    (h,) = rows_call("rms_fwd", rms_tile, T, [("row", x, 0, D_MODEL), ("full", g)], [D_MODEL], [BF16])
    p = matmul("mm_in_e", h, W["w_in_e"][i], "nn")
    kvp = jnp.pad(p[:, O["kA"]:O["kA"] + 2 * W_KV_A], ((BLOCK, BLOCK), (0, 0)))
    sink = W["sink"][i].reshape(N_Q_A, 1, 1)
    a = attn_fwd(p, O["qA"], kvp, sink, slopes, T)
    prep_ins = [("row", p, O["qB"], W_B), ("row", p, O["zf"], W_B), ("row", p, O["zb"], W_B),
                ("full", lower[i][0:1]), ("full", lower[i][1:2])]
    qh, k2, g2 = rows_call("hgrn_prep_fwd", hgrn_prep_tile, T, prep_ins, [W_B] * 3, stacks=[(0,), (1, 2), (3, 4)])
    scan_srcs = [(qh, 0), (k2, 0), (k2, 1), (g2, 0), (g2, 1), (p, O["iB"])]
    o_f, o_b, ss_f, ss_b = scan_fwd("scan_fwd_h", *scan_srcs, N_HEADS_B, HEAD_DIM_B, HEAD_DIM_B, T)
    mo = mem_fwd(p, O["qM"], kv, T)
    hg = W["hgrn_norm"][i].reshape(1, W_B)
    post_ins = [("row", a, 0, W_A), ("row", o_f, 0, W_B), ("row", o_b, 0, W_B), ("row", mo, 0, W_M),
                ("row", p, O["gA"], W_A), ("row", p, O["gB"], W_B), ("row", p, O["gM"], W_M), ("full", hg)]
    (mix,) = rows_call("even_post_fwd", even_post_tile, T, post_ins, [MIX], [BF16])
    x_new = matmul("mm_out", mix, W["w_out_e"][i], "nn", add=x)
    return x_new, dict(x=x, g=g, h=h, p=p, kvp=kvp, sink=sink, prep_ins=prep_ins,
                       scan_srcs=scan_srcs, ss_f=ss_f, ss_b=ss_b, post_ins=post_ins, mix=mix)


def _assemble_even(dqA, dgA, dqB, dzf, dzb, dv0, dv1, dgB, dqM, dgM, dkvA):
    return (jnp.concatenate([dqA, dgA, dqB, dzf, dzb, dv0 + dv1, dgB, dqM, dgM, dkvA], axis=-1),)


def _even_bwd(dxo, sv, i, W, kv, slopes, T):
    O = EVEN_OFF
    p = sv["p"]
    dmix = matmul("mm_dmix", dxo, W["w_out_e"][i], "nt")
    dwo = matmul("mm_dwo", sv["mix"], dxo, "tn")
    da, dof, dmo, dgA, dgB, dgM, dhg = rows_vjp_call("even_post_bwd", even_post_tile, T, sv["post_ins"],
                                                      [[("row", dmix, 0, MIX)]], skip=(2,))
    dqA, dkvp, dsink = attn_bwd(p, O["qA"], sv["kvp"], sv["sink"], slopes, da, T)
    dkvA = dkvp[BLOCK:-BLOCK]
    dqf, dkf, dgf, dvf, dqb, dkb, dgb, dvb = scan_bwd("scan_bwd_h", *sv["scan_srcs"], sv["ss_f"], sv["ss_b"], (dof, 0),
                                                      N_HEADS_B, HEAD_DIM_B, HEAD_DIM_B, T)
    row = lambda arr, w: ("row", arr, 0, w)
    dqB, dzf, dzb, dlow_f, dlow_b = rows_vjp_call(
        "hgrn_prep_bwd", hgrn_prep_tile, T, sv["prep_ins"],
        [[row(dqf, W_B), row(dqb, W_B)], [row(dkf, W_B)], [row(dkb, W_B)], [row(dgf, W_B)], [row(dgb, W_B)]])
    dlow = jnp.concatenate([dlow_f, dlow_b], axis=0)
    dqM, dkv = mem_bwd(p, O["qM"], kv, dmo, T)
    (dp,) = rows_call("even_dp", _assemble_even, T,
                      [row(dqA, W_A), row(dgA, W_A), row(dqB, W_B), row(dzf, W_B), row(dzb, W_B), row(dvf, W_B), row(dvb, W_B),
                       row(dgB, W_B), row(dqM, W_M), row(dgM, W_M), row(dkvA, 2 * W_KV_A)],
                      [EVEN_IN], [BF16])
    dh = matmul("mm_dh_e", dp, W["w_in_e"][i], "nt")
    dwi = matmul("mm_dwi_e", sv["h"], dp, "tn")
    dx, dg = rows_vjp_call("rms_res_bwd", rms_res_tile, T, [("row", sv["x"], 0, D_MODEL), ("full", sv["g"])],
                           [[("row", dh, 0, D_MODEL)], [("row", dxo, 0, D_MODEL)]])
    return dx, dict(w_in=dwi, w_out=dwo, norm=dg[0], sink=dsink.reshape(N_Q_A), low=dlow, hg=dhg[0], kv=dkv)


def _pad_gate_up(w_up):
    z = jnp.zeros((2, 128, WK_C), F32)
    z = z.at[0, 0:GATE_RANK].set(w_up[0])
    return z.at[1, GATE_RANK:2 * GATE_RANK].set(w_up[1])


def _odd_fwd(x, i, W, kv, T):
    O = ODD_OFF
    g = W["norm_odd"][i].reshape(1, D_MODEL)
    (h,) = rows_call("rms_fwd", rms_tile, T, [("row", x, 0, D_MODEL), ("full", g)], [D_MODEL], [BF16])
    p = matmul("mm_in_o", h, W["w_in_o"][i], "nn")
    wup = _pad_gate_up(W["w_gate_up"][i])
    prep_ins = [("row", p, O["qC"], WK_C), ("row", p, O["rr"], 128), ("full", wup[0]), ("full", wup[1]),
                ("full", W["b_gate"][i][0:1]), ("full", W["b_gate"][i][1:2])]
    qg, g2 = rows_call("gla_prep_fwd", gla_prep_tile, T, prep_ins, [WK_C] * 2, stacks=[(0,), (1, 2)])
    scan_srcs = [(qg, 0), (p, O["kC"]), (p, O["kC"]), (g2, 0), (g2, 1), (p, O["vC"])]
    o_f, o_b, ss_f, ss_b = scan_fwd("scan_fwd_g", *scan_srcs, N_HEADS_C, DK_C, DV_C, T)
    mo = mem_fwd(p, O["qM"], kv, T)
    gg = W["gla_norm"][i].reshape(1, WV_C)
    post_ins = [("row", o_f, 0, WV_C), ("row", o_b, 0, WV_C), ("row", mo, 0, W_M),
                ("row", p, O["gC"], WV_C), ("row", p, O["gM"], W_M), ("full", gg)]
    (mix,) = rows_call("odd_post_fwd", odd_post_tile, T, post_ins, [MIX], [BF16])
    x_new = matmul("mm_out", mix, W["w_out_o"][i], "nn", add=x)
    return x_new, dict(x=x, g=g, h=h, p=p, prep_ins=prep_ins, scan_srcs=scan_srcs, ss_f=ss_f, ss_b=ss_b,
                       post_ins=post_ins, mix=mix)


def _assemble_odd(dqC, dk0, dk1, dv0, dv1, dgC, dqM, dgM, dr):
    return (jnp.concatenate([dqC, dk0 + dk1, dv0 + dv1, dgC, dqM, dgM, dr], axis=-1),)


def _odd_bwd(dxo, sv, i, W, kv, T):
    O = ODD_OFF
    p = sv["p"]
    dmix = matmul("mm_dmix", dxo, W["w_out_o"][i], "nt")
    dwo = matmul("mm_dwo", sv["mix"], dxo, "tn")
    dof, dmo, dgC, dgM, dgg = rows_vjp_call("odd_post_bwd", odd_post_tile, T, sv["post_ins"],
                                            [[("row", dmix, 0, MIX)]], skip=(1,))
    dqf, dkf, dgf, dvf, dqb, dkb, dgb, dvb = scan_bwd("scan_bwd_g", *sv["scan_srcs"], sv["ss_f"], sv["ss_b"], (dof, 0),
                                                      N_HEADS_C, DK_C, DV_C, T)
    row = lambda arr, w: ("row", arr, 0, w)
    dqC, dr, dwup_f, dwup_b, dbg_f, dbg_b = rows_vjp_call(
        "gla_prep_bwd", gla_prep_tile, T, sv["prep_ins"],
        [[row(dqf, WK_C), row(dqb, WK_C)], [row(dgf, WK_C)], [row(dgb, WK_C)]])
    dqM, dkv = mem_bwd(p, O["qM"], kv, dmo, T)
    (dp,) = rows_call("odd_dp", _assemble_odd, T,
                      [row(dqC, WK_C), row(dkf, WK_C), row(dkb, WK_C), row(dvf, WV_C), row(dvb, WV_C),
                       row(dgC, WV_C), row(dqM, W_M), row(dgM, W_M), row(dr, 128)],
                      [ODD_PAD], [BF16])
    dh = matmul("mm_dh_o", dp, W["w_in_o"][i], "nt")
    dwi = matmul("mm_dwi_o", sv["h"], dp, "tn")
    dx, dg = rows_vjp_call("rms_res_bwd", rms_res_tile, T, [("row", sv["x"], 0, D_MODEL), ("full", sv["g"])],
                           [[("row", dh, 0, D_MODEL)], [("row", dxo, 0, D_MODEL)]])
    dw_up = jnp.stack([dwup_f[0:GATE_RANK], dwup_b[GATE_RANK:2 * GATE_RANK]])
    dbg = jnp.concatenate([dbg_f, dbg_b], axis=0)
    return dx, dict(w_in=dwi, w_out=dwo, norm=dg[0], w_up=dw_up, b_gate=dbg, gg=dgg[0], kv=dkv)


def local_step(x, mem, target, W):
    T = x.shape[0]
    slopes = (2.0 ** (-8.0 * jnp.arange(1, N_Q_A + 1, dtype=F32) / N_Q_A)).reshape(N_Q_A, 1, 1)
    lower, lower_vjp = jax.vjp(_lower_bounds, W["lb_param"])
    mem_g = W["mem_norm"].reshape(1, D_MODEL)
    (mem_n,) = rows_call("mem_rms_fwd", rms_tile, N_MEM, [("row", mem, 0, D_MODEL), ("full", mem_g)], [D_MODEL], [BF16])
    kvs = [matmul("mm_kv", mem_n, W["w_kv"][l], "nn") for l in range(DEPTH)]
    saved = []
    for l in range(DEPTH):
        if l % 2 == 0:
            x, sv = _even_fwd(x, l // 2, W, lower, kvs[l], slopes, T)
        else:
            x, sv = _odd_fwd(x, l // 2, W, kvs[l], T)
        saved.append(sv)
    loss, dx, dgf = final_call(x, W["final_norm"].reshape(1, D_MODEL), target, T)
    per = [None] * DEPTH
    for l in reversed(range(DEPTH)):
        if l % 2 == 0:
            dx, per[l] = _even_bwd(dx, saved[l], l // 2, W, kvs[l], slopes, T)
        else:
            dx, per[l] = _odd_bwd(dx, saved[l], l // 2, W, kvs[l], T)
    dmem_n, dw_kv = None, []
    for l in range(DEPTH):
        dw_kv.append(matmul("mm_dwkv", mem_n, per[l]["kv"], "tn"))
        dmem_n = matmul("mm_dmem", per[l]["kv"], W["w_kv"][l], "nt", add=dmem_n)
    (dmem_norm,) = rows_vjp_call("mem_rms_bwd", rms_tile, N_MEM, [("row", mem, 0, D_MODEL), ("full", mem_g)],
                                 [[("row", dmem_n, 0, D_MODEL)]], skip=(0,))
    ev, od = (per[0], per[2]), (per[1], per[3])
    (d_lb,) = lower_vjp(jnp.stack([e["low"] for e in ev]))
    grads = dict(
        w_in_e=jnp.stack([e["w_in"] for e in ev]), w_in_o=jnp.stack([o["w_in"] for o in od]),
        w_out_e=jnp.stack([e["w_out"] for e in ev]), w_out_o=jnp.stack([o["w_out"] for o in od]),
        w_kv=jnp.stack(dw_kv), norm_even=jnp.stack([e["norm"] for e in ev]), sink=jnp.stack([e["sink"] for e in ev]),
        lb_param=d_lb, hgrn_norm=jnp.stack([e["hg"] for e in ev]), norm_odd=jnp.stack([o["norm"] for o in od]),
        w_gate_up=jnp.stack([o["w_up"] for o in od]), b_gate=jnp.stack([o["b_gate"] for o in od]),
        gla_norm=jnp.stack([o["gg"] for o in od]), mem_norm=dmem_norm[0], final_norm=dgf[0])
    return loss, dx, grads


SMALL_SPECS = (("lb_param", (2, 2, 128)), ("norm_odd", (2, 256)), ("w_gate_up", (2, 2, 16, 128)),
               ("b_gate", (2, 2, 128)), ("gla_norm", (2, 256)))
SMALL_ROWS = 80


def _pack_small_local(d):
    return jnp.concatenate([d[n].reshape(-1) for n, _ in SMALL_SPECS]).reshape(SMALL_ROWS, 128)


def _unpack_small_local(b):
    flat, out, o = b.reshape(-1), {}, 0
    for n, shp in SMALL_SPECS:
        sz = int(np.prod(shp))
        out[n] = flat[o:o + sz].reshape(shp)
        o += sz
    return out


def _unpack_small_full(g4):
    per = [_unpack_small_local(g4[j]) for j in range(4)]
    return {n: jnp.concatenate([per[j][n] for j in range(4)], axis=-1) for n, _ in SMALL_SPECS}


def _pack_small_blocks(full):
    blocks = []
    for j in range(4):
        blocks.append(_pack_small_local({n: full[n][..., j * shp[-1]:(j + 1) * shp[-1]] for n, shp in SMALL_SPECS}))
    return jnp.stack(blocks)


def _cols(t, order, off, widths):
    return [t[..., off[n]:off[n] + widths[n]] for n in order]


EVEN_REF_ORDER = ("qA", "kA", "vA", "gA", "qB", "zf", "zb", "iB", "gB", "qM", "gM")
ODD_REF_ORDER = ("qC", "kC", "vC", "gC", "rr", "qM", "gM")


def _full_weights(gathered, gsmall, rep):
    g_in_e, g_in_o, g_out_e, g_out_o, g_kv = gathered
    t = g_in_e.reshape(4, 2, D_MODEL, EVEN_IN // 4).transpose(1, 2, 0, 3).reshape(2, D_MODEL, EVEN_IN)
    w_in_e = jnp.concatenate(_cols(t, EVEN_ORDER, EVEN_REF_OFF, EVEN_W), axis=-1)
    t = g_in_o.reshape(4, 2, D_MODEL, ODD_IN // 4).transpose(1, 2, 0, 3).reshape(2, D_MODEL, ODD_IN)
    w_in_o = jnp.concatenate(_cols(t, ODD_ORDER, ODD_REF_OFF, ODD_W) + [jnp.zeros((2, D_MODEL, ODD_PAD - ODD_IN), BF16)],
                             axis=-1)
    blocks_to_rows = lambda g, n: g.reshape(4, n, g.shape[1] // n, g.shape[2]).transpose(1, 0, 2, 3).reshape(
        n, 4 * (g.shape[1] // n), g.shape[2])
    W = dict(w_in_e=w_in_e, w_in_o=w_in_o, w_out_e=blocks_to_rows(g_out_e, 2), w_out_o=blocks_to_rows(g_out_o, 2),
             w_kv=blocks_to_rows(g_kv, DEPTH))
    W.update(_unpack_small_full(gsmall))
    W.update(rep)
    return W


def _grad_blocks(grads):
    t = jnp.concatenate(_cols(grads["w_in_e"], EVEN_REF_ORDER, EVEN_OFF, EVEN_W), axis=-1)
    b_in_e = t.reshape(2, D_MODEL, 4, EVEN_IN // 4).transpose(2, 0, 1, 3).reshape(4, 2 * D_MODEL, EVEN_IN // 4)
    t = jnp.concatenate(_cols(grads["w_in_o"], ODD_REF_ORDER, ODD_OFF, ODD_W), axis=-1)
    b_in_o = t.reshape(2, D_MODEL, 4, ODD_IN // 4).transpose(2, 0, 1, 3).reshape(4, 2 * D_MODEL, ODD_IN // 4)
    rows_to_blocks = lambda g: g.reshape(g.shape[0], 4, g.shape[1] // 4, g.shape[2]).transpose(1, 0, 2, 3).reshape(
        4, g.shape[0] * (g.shape[1] // 4), g.shape[2])
    return [b_in_e, b_in_o, rows_to_blocks(grads["w_out_e"]), rows_to_blocks(grads["w_out_o"]),
            rows_to_blocks(grads["w_kv"]), _pack_small_blocks(grads)]


WEIGHT_NAMES = ("norm_even", "w_in_even", "sink", "lb_param", "hgrn_norm", "w_out_even", "norm_odd", "w_in_odd",
                "w_gate_up", "b_gate", "gla_norm", "w_out_odd", "mem_norm", "w_mem_kv", "final_norm")


def kernel(x, mem, norm_even, w_in_even, sink, lb_param, hgrn_norm, w_out_even, norm_odd, w_in_odd, w_gate_up, b_gate, gla_norm, w_out_odd, mem_norm, w_mem_kv, final_norm, loss_target, m_norm_even, m_w_in_even, m_sink, m_lb_param, m_hgrn_norm, m_w_out_even, m_norm_odd, m_w_in_odd, m_w_gate_up, m_b_gate, m_gla_norm, m_w_out_odd, m_mem_norm, m_w_mem_kv, m_final_norm, v_norm_even, v_w_in_even, v_sink, v_lb_param, v_hgrn_norm, v_w_out_even, v_norm_odd, v_w_in_odd, v_w_gate_up, v_b_gate, v_gla_norm, v_w_out_odd, v_mem_norm, v_w_mem_kv, v_final_norm):
    w = dict(zip(WEIGHT_NAMES, (norm_even, w_in_even, sink, lb_param, hgrn_norm, w_out_even, norm_odd, w_in_odd,
                                w_gate_up, b_gate, gla_norm, w_out_odd, mem_norm, w_mem_kv, final_norm)))
    m = dict(zip(WEIGHT_NAMES, (m_norm_even, m_w_in_even, m_sink, m_lb_param, m_hgrn_norm, m_w_out_even, m_norm_odd,
                                m_w_in_odd, m_w_gate_up, m_b_gate, m_gla_norm, m_w_out_odd, m_mem_norm, m_w_mem_kv,
                                m_final_norm)))
    v = dict(zip(WEIGHT_NAMES, (v_norm_even, v_w_in_even, v_sink, v_lb_param, v_hgrn_norm, v_w_out_even, v_norm_odd,
                                v_w_in_odd, v_w_gate_up, v_b_gate, v_gla_norm, v_w_out_odd, v_mem_norm, v_w_mem_kv,
                                v_final_norm)))
    ci = lax.axis_index("c").astype(jnp.int32).reshape(1)
    chip = (2 * lax.axis_index("x") + lax.axis_index("y")).astype(jnp.int32).reshape(1)

    flat2 = lambda t: t.reshape(-1, t.shape[-1])
    shards = [flat2(w[n]).astype(BF16) for n in ("w_in_even", "w_in_odd", "w_out_even", "w_out_odd", "w_mem_kv")]
    small = _pack_small_local(w)
    remote = gather_weights(shards, small)
    own = lambda g, s: lax.dynamic_update_slice(g, s[None], (chip[0], 0, 0))
    *gathered, gsmall = [own(g, s) for g, s in zip(remote, shards + [small])]
    rep = {n: w[n] for n in ("norm_even", "sink", "hgrn_norm", "mem_norm", "final_norm")}
    W = _full_weights(gathered, gsmall, rep)

    loss_tile, dx, grads = local_step(x[0], mem[0], loss_target[0], W)

    blocks = _grad_blocks(grads)
    recv = rs_exchange_siblings(blocks)
    wire = [BF16] * (len(blocks) - 1) + [F32]
    chip_sums = [add_sibling(g, r, ci, dt) for g, r, dt in zip(blocks, recv, wire)]
    recv3 = rs_exchange_chips(chip_sums)
    place = jnp.concatenate([chip, ci])
    halves = [add_chips(g, r, r3, place) for g, r, r3 in zip(blocks, recv, recv3)]
    g_in_e, g_in_o, g_out_e, g_out_o, g_kv, g_small = rs_share_final(halves)
    gl = _unpack_small_local(g_small)
    gl.update(w_in_even=g_in_e.reshape(w_in_even.shape), w_in_odd=g_in_o.reshape(w_in_odd.shape),
              w_out_even=g_out_e.reshape(w_out_even.shape), w_out_odd=g_out_o.reshape(w_out_odd.shape),
              w_mem_kv=g_kv.reshape(w_mem_kv.shape))

    pack = jnp.zeros((8, D_MODEL), F32)
    pack = pack.at[0:2].set(grads["norm_even"]).at[2].set(grads["hgrn_norm"].reshape(-1))
    pack = pack.at[3].set(grads["mem_norm"]).at[4].set(grads["final_norm"])
    pack = pack.at[5, 0:16].set(grads["sink"].reshape(-1)).at[5, 16].set(loss_tile[0, 0])
    tot = sum_devices(allgather_small(pack))
    gl.update(norm_even=tot[0:2], hgrn_norm=tot[2].reshape(2, W_B), mem_norm=tot[3], final_norm=tot[4],
              sink=tot[5, 0:16].reshape(2, N_Q_A))
    loss = tot[5, 16]

    upd = {n: adamw_call(w[n], gl[n], m[n], v[n]) for n in WEIGHT_NAMES}
    return (loss, dx[None], *[gl[n] for n in WEIGHT_NAMES], *[upd[n][0] for n in WEIGHT_NAMES],
            *[upd[n][1] for n in WEIGHT_NAMES], *[upd[n][2] for n in WEIGHT_NAMES])
```

```python
import functools

import numpy as np
import jax
import jax.numpy as jnp
from jax import lax
from jax.experimental import pallas as pl
from jax.experimental.pallas import tpu as pltpu
from jax.experimental.pallas import tpu_sc as plsc

F32 = jnp.float32
BF16 = jnp.bfloat16

D_MODEL = 1024
DEPTH = 4
N_Q_A, N_KV_A, HEAD_DIM_A = 8, 2, 64
W_A, W_KV_A = 512, 128
WINDOW = 128
BLOCK = 128
N_HEADS_B, HEAD_DIM_B, W_B = 4, 128, 512
N_HEADS_C, DK_C, DV_C, WK_C, WV_C = 4, 128, 256, 512, 1024
GATE_RANK = 16
GATE_TEMP = 16.0
N_MEM, N_HEADS_M, HEAD_DIM_M, W_M = 256, 4, 128, 512
EPS = 1e-6
MASK_VALUE = -1e30
MIN_GATE = 1e-30
EVEN_IN, ODD_IN = 4864, 4128
ODD_PAD = 4224
MIX = 1536
ADAM_LR, ADAM_B1, ADAM_B2, ADAM_EPS, ADAM_WD, ADAM_STEP = 0.001, 0.9, 0.999, 1e-08, 0.01, 10

SCAN_CHUNK = 128
SCAN_LEVELS = 7
VMEM_LIMIT = 56 * 1024 * 1024

EVEN_REF_OFF = dict(qA=0, kA=512, vA=640, gA=768, qB=1280, zf=1792, zb=2304, iB=2816, gB=3328, qM=3840, gM=4352)
EVEN_W = dict(qA=512, kA=128, vA=128, gA=512, qB=512, zf=512, zb=512, iB=512, gB=512, qM=512, gM=512)
EVEN_ORDER = ("qA", "gA", "qB", "zf", "zb", "iB", "gB", "qM", "gM", "kA", "vA")
ODD_REF_OFF = dict(qC=0, kC=512, vC=1024, gC=2048, rr=3072, qM=3104, gM=3616)
ODD_W = dict(qC=512, kC=512, vC=1024, gC=1024, rr=32, qM=512, gM=512)
ODD_ORDER = ("qC", "kC", "vC", "gC", "qM", "gM", "rr")


def _offsets(order, widths):
    off, o = {}, 0
    for n in order:
        off[n] = o
        o += widths[n]
    return off


EVEN_OFF = _offsets(EVEN_ORDER, EVEN_W)
ODD_OFF = _offsets(ODD_ORDER, ODD_W)


def _dg(a, b, ca, cb):
    return lax.dot_general(a.astype(BF16), b.astype(BF16), (((ca,), (cb,)), ((), ())),
                           preferred_element_type=F32)


def dot_nn(a, b):
    return _dg(a, b, 1, 0)


def dot_nt(a, b):
    return _dg(a, b, 1, 1)


def dot_tn(a, b):
    return _dg(a, b, 0, 0)


@jax.custom_vjp
def bdot(a, b):
    return dot_nn(a, b)


bdot.defvjp(lambda a, b: (dot_nn(a, b), (a, b)),
            lambda r, g: (dot_nt(g, r[1]), dot_tn(r[0], g)))


@jax.custom_vjp
def bdot_t(a, b):
    return dot_nt(a, b)


bdot_t.defvjp(lambda a, b: (dot_nt(a, b), (a, b)),
              lambda r, g: (dot_nn(g, r[1]), dot_tn(g, r[0])))


@jax.custom_vjp
def bdot_tn(a, b):
    return dot_tn(a, b)


bdot_tn.defvjp(lambda a, b: (dot_tn(a, b), (a, b)),
               lambda r, g: (dot_nt(r[1], g), dot_nn(r[0], g)))


def _split_mm(h, x):
    hi = x.astype(BF16)
    lo = (x - hi.astype(F32)).astype(BF16)
    return (lax.dot_general(h, hi, (((1,), (0,)), ((), ())), preferred_element_type=F32)
            + lax.dot_general(h, lo, (((1,), (0,)), ((), ())), preferred_element_type=F32))


@jax.custom_vjp
def hdot(h, ht, x):
    return _split_mm(h, x)


hdot.defvjp(lambda h, ht, x: (_split_mm(h, x), (h, ht)),
            lambda r, g: (jnp.zeros_like(r[0]), jnp.zeros_like(r[1]), _split_mm(r[1], g)))


def _sigmoid(z):
    return 1.0 / (1.0 + jnp.exp(-z))


def _silu(z):
    return z * _sigmoid(z)


def _log_sigmoid(z):
    return jnp.minimum(z, 0.0) - jnp.log(1.0 + jnp.exp(-jnp.abs(z)))


def _rms(x, g):
    return x * lax.rsqrt(jnp.mean(x * x, axis=-1, keepdims=True) + EPS) * g


def rms_tile(x, g):
    return (_rms(x, g),)


@functools.partial(jax.custom_vjp, nondiff_argnums=(1, 2))
def split(x, n, axis):
    w = x.shape[axis] // n
    return tuple(lax.slice_in_dim(x, h * w, (h + 1) * w, axis=axis) for h in range(n))


split.defvjp(lambda x, n, axis: (split(x, n, axis), None),
             lambda n, axis, _, cts: (jnp.concatenate(cts, axis=axis),))


def _group_rms(o, g, heads):
    return jnp.concatenate([_rms(oh, gh) for oh, gh in zip(split(o, heads, 1), split(g, heads, 1))], axis=-1)


def even_post_tile(a, o2f, o2b, mo, gA, gB, gM, hg):
    y = _group_rms(o2f + o2b, hg, N_HEADS_B)
    return (jnp.concatenate([a * _silu(gA), y * _silu(gB), mo * _silu(gM)], axis=-1),)


def odd_post_tile(o2f, o2b, mo, gC, gM, gg):
    y = _group_rms(o2f + o2b, gg, N_HEADS_C)
    return (jnp.concatenate([y * _silu(gC), mo * _silu(gM)], axis=-1),)


def hgrn_prep_tile(qB, zf, zb, low_f, low_b):
    ks, gs = [], []
    for z, lb in ((zf, low_f), (zb, low_b)):
        f = lb + (1.0 - lb) * _sigmoid(z)
        gs.append(jnp.log(jnp.maximum(f, MIN_GATE)))
        ks.append((1.0 - lb) * _sigmoid(-z))
    return (_silu(qB), ks[0], ks[1], gs[0], gs[1])


def gla_prep_tile(qC, r128, wup_f, wup_b, bg_f, bg_b):
    gs = [_log_sigmoid(bdot(r128, wup) + bg) / GATE_TEMP for wup, bg in ((wup_f, bg_f), (wup_b, bg_b))]
    return (qC * (DK_C ** -0.5), gs[0], gs[1])


def mem_tile(q, k, v):
    s = bdot_t(q, k) * (HEAD_DIM_M ** -0.5)
    m = lax.stop_gradient(jnp.max(s, axis=-1, keepdims=True))
    p = jnp.exp(s - m)
    p = p / jnp.sum(p, axis=-1, keepdims=True)
    return (bdot(p, v),)


def attn_block(qs, ks, vs, sinks, slopes, c, seq):
    i = lax.broadcasted_iota(jnp.int32, (BLOCK, 3 * BLOCK), 0)
    j = lax.broadcasted_iota(jnp.int32, (BLOCK, 3 * BLOCK), 1)
    dist = jnp.abs(i - j + BLOCK).astype(F32)
    kpos = (c - 1) * BLOCK + j
    valid = (dist <= WINDOW) & (kpos >= 0) & (kpos < seq)
    outs = []
    for q, sk, slope in zip(qs, sinks, slopes):
        s = bdot_t(q, ks) * (HEAD_DIM_A ** -0.5)
        s = jnp.where(valid, s - slope * dist, MASK_VALUE)
        m = lax.stop_gradient(jnp.maximum(jnp.max(s, axis=-1, keepdims=True), sk))
        p = jnp.where(valid, jnp.exp(s - m), 0.0)
        denom = jnp.sum(p, axis=-1, keepdims=True) + jnp.exp(sk - m)
        outs.append(bdot(p, vs) / denom)
    return tuple(outs)


def scan_chunk(q, k, v, g, st, h, ht, qm, km, bm):
    C = SCAN_CHUNK
    e = split(hdot(h, ht, g), 2 + SCAN_LEVELS, 0)
    qe = q * jnp.exp(e[0])
    kd = k * jnp.exp(e[1])
    tot = jnp.sum(g, axis=0, keepdims=True)
    r = lax.broadcasted_iota(jnp.int32, (C, C), 0)
    s = lax.broadcasted_iota(jnp.int32, (C, C), 1)
    a = jnp.where(r == s, jnp.sum(q * k, axis=-1, keepdims=True), 0.0)
    for l in range(SCAN_LEVELS):
        el = jnp.exp(e[2 + l])
        qs = q * el * qm[l * C:(l + 1) * C]
        ks = k * el * km[l * C:(l + 1) * C]
        a = a + bdot_t(qs, ks) * bm[l * C:(l + 1) * C]
    o = bdot_t(qe, st) + bdot(a, v)
    st_new = st * jnp.exp(tot) + bdot_tn(v, kd)
    return o, st_new


def _scan_consts():
    C, L = SCAN_CHUNK, SCAN_LEVELS
    t = np.arange(C)[:, None]
    r = np.arange(C)[None, :]
    blocks = [(r <= t), (r > t)]
    qms, kms, bms = [], [], []
    for l in range(1, L + 1):
        m = C >> l
        upper_t = (t % (2 * m)) >= m
        same_half = (t // m) == (r // m)
        blocks.append(same_half & np.where(upper_t, r <= t, r > t))
        qms.append(np.broadcast_to(upper_t, (C, C)))
        kms.append(np.broadcast_to(~upper_t, (C, C)))
        bms.append((t // (2 * m)) == (r // (2 * m)))
    hf = np.concatenate(blocks, axis=0).astype(np.float32)
    flip = lambda mat: mat.reshape(-1, C, C)[:, ::-1, ::-1].reshape(-1, C)
    hb = flip(hf)
    qmf = np.concatenate(qms, axis=0).astype(np.float32)
    kmf = np.concatenate(kms, axis=0).astype(np.float32)
    bm = np.concatenate(bms, axis=0).astype(np.float32)
    h = np.stack([hf, hb])
    ht = np.stack([hf.T, hb.T])
    qm = np.stack([qmf, kmf])
    km = np.stack([kmf, qmf])
    return h, ht, qm, km, bm


def _cparams(sem):
    return pltpu.CompilerParams(dimension_semantics=sem, vmem_limit_bytes=VMEM_LIMIT)


def _row_tile(T):
    return min(T, 256)


def _in_spec(spec, tr):
    kind = spec[0]
    if kind == "row":
        _, arr, off, w = spec
        assert off % w == 0
        return arr, pl.BlockSpec((tr, w), functools.partial(lambda i, b: (i, b), b=off // w))
    if kind == "row3":
        _, arr, d, off, w = spec
        assert off % w == 0
        return arr, pl.BlockSpec((None, tr, w), functools.partial(lambda i, d, b: (d, i, b), d=d, b=off // w))
    _, arr = spec
    return arr, pl.BlockSpec(arr.shape, functools.partial(lambda i, n: (0,) * n, n=arr.ndim))


def rows_call(name, tile_fn, T, ins, out_widths, out_dtypes=None, stacks=None):
    tr = _row_tile(T)
    n_in = len(ins)
    out_dtypes = out_dtypes or [F32] * len(out_widths)
    stacks = stacks or [(k,) for k in range(len(out_widths))]

    def body(*refs):
        vals = [r[...] for r in refs[:n_in]]
        outs = tile_fn(*vals)
        for r, members in zip(refs[n_in:], stacks):
            if len(members) == 1:
                r[...] = outs[members[0]].astype(r.dtype)
            else:
                for d, k in enumerate(members):
                    r[d] = outs[k].astype(r.dtype)

    in_specs, args = [], []
    for spec in ins:
        arr, bs = _in_spec(spec, tr)
        args.append(arr)
        in_specs.append(bs)
    out_specs, out_shape = [], []
    for w, dt, members in zip(out_widths, out_dtypes, stacks):
        n = len(members)
        if n == 1:
            out_specs.append(pl.BlockSpec((tr, w), lambda i: (i, 0)))
            out_shape.append(jax.ShapeDtypeStruct((T, w), dt))
        else:
            out_specs.append(pl.BlockSpec((n, tr, w), lambda i: (0, i, 0)))
            out_shape.append(jax.ShapeDtypeStruct((n, T, w), dt))
    return pl.pallas_call(body, out_shape=out_shape, grid=(T // tr,), in_specs=in_specs, out_specs=out_specs,
                          name=name, compiler_params=_cparams(("arbitrary",)))(*args)


def rows_vjp_call(name, tile_fn, T, ins, cts, skip=()):
    tr = _row_tile(T)
    n_in = len(ins)
    n_ct = [len(c) for c in cts]
    want = [k for k in range(n_in) if k not in skip]

    def body(*refs):
        i = pl.program_id(0)
        vals = [r[...] for r in refs[:n_in]]
        ct, pos = [], n_in
        for n in n_ct:
            acc = refs[pos][...]
            for r in refs[pos + 1:pos + n]:
                acc = acc + r[...]
            ct.append(acc)
            pos += n
        _, vjp = jax.vjp(tile_fn, *vals)
        grads = vjp(tuple(ct))
        for r, k in zip(refs[pos:], want):
            if ins[k][0] == "full":
                @pl.when(i == 0)
                def _():
                    r[...] = jnp.zeros_like(r)
                r[...] += grads[k]
            else:
                r[...] = grads[k]

    in_specs, args = [], []
    for spec in list(ins) + [s for c in cts for s in c]:
        arr, bs = _in_spec(spec, tr)
        args.append(arr)
        in_specs.append(bs)
    out_specs, out_shape = [], []
    for k in want:
        if ins[k][0] == "full":
            arr = ins[k][1]
            out_specs.append(pl.BlockSpec(arr.shape, functools.partial(lambda i, n: (0,) * n, n=arr.ndim)))
            out_shape.append(jax.ShapeDtypeStruct(arr.shape, F32))
        else:
            w = ins[k][-1]
            out_specs.append(pl.BlockSpec((tr, w), lambda i: (i, 0)))
            out_shape.append(jax.ShapeDtypeStruct((T, w), F32))
    return pl.pallas_call(body, out_shape=out_shape, grid=(T // tr,), in_specs=in_specs, out_specs=out_specs,
                          name=name, compiler_params=_cparams(("arbitrary",)))(*args)


def matmul(name, a, b, mode, add=None, out_dtype=F32):
    if mode == "tn":
        K, M = a.shape
        N = b.shape[1]
        tm = M if M <= 1536 else 512
        tn = N if N <= 1280 else (N // 2 if (N // 2) % 128 == 0 else N)
        tk = min(K, 512)
        grid = (M // tm, N // tn, K // tk)

        def body(a_ref, b_ref, o_ref):
            @pl.when(pl.program_id(2) == 0)
            def _():
                o_ref[...] = jnp.zeros_like(o_ref)
            o_ref[...] += dot_tn(a_ref[...], b_ref[...])

        return pl.pallas_call(
            body, out_shape=jax.ShapeDtypeStruct((M, N), F32), grid=grid,
            in_specs=[pl.BlockSpec((tk, tm), lambda i, j, k: (k, i)), pl.BlockSpec((tk, tn), lambda i, j, k: (k, j))],
            out_specs=pl.BlockSpec((tm, tn), lambda i, j, k: (i, j)), name=name,
            compiler_params=_cparams(("arbitrary", "arbitrary", "arbitrary")))(a, b)

    M, K = a.shape
    N = b.shape[1] if mode == "nn" else b.shape[0]
    tm = min(M, 256)
    tn = N if N <= 1536 else (N // 2 if (N // 2) % 128 == 0 else (N // 3 if (N // 3) % 128 == 0 else N))
    grid = (N // tn, M // tm)
    n_in = 2 + (add is not None)

    def body(*refs):
        a_ref, b_ref = refs[0], refs[1]
        o_ref = refs[n_in]
        acc = dot_nn(a_ref[...], b_ref[...]) if mode == "nn" else dot_nt(a_ref[...], b_ref[...])
        if add is not None:
            acc = acc + refs[2][...]
        o_ref[...] = acc.astype(o_ref.dtype)

    in_specs = [pl.BlockSpec((tm, K), lambda j, i: (i, 0)),
                pl.BlockSpec((K, tn), lambda j, i: (0, j)) if mode == "nn" else pl.BlockSpec((tn, K), lambda j, i: (j, 0))]
    args = [a, b]
    if add is not None:
        in_specs.append(pl.BlockSpec((tm, tn), lambda j, i: (i, j)))
        args.append(add)
    return pl.pallas_call(
        body, out_shape=jax.ShapeDtypeStruct((M, N), out_dtype), grid=grid, in_specs=in_specs,
        out_specs=pl.BlockSpec((tm, tn), lambda j, i: (i, j)), name=name,
        compiler_params=_cparams(("arbitrary", "arbitrary")))(*args)


def _attn_heads(n):
    G = N_Q_A // N_KV_A
    k_sl = pl.ds(n * HEAD_DIM_A, HEAD_DIM_A)
    v_sl = pl.ds(W_KV_A + n * HEAD_DIM_A, HEAD_DIM_A)
    q_sl = [pl.ds((n * G + g) * HEAD_DIM_A, HEAD_DIM_A) for g in range(G)]
    return k_sl, v_sl, q_sl, range(n * G, (n + 1) * G)


def attn_fwd(p, q_off, kvp, sink, slopes, T):
    nb = T // BLOCK
    assert q_off % W_A == 0

    def body(q_ref, kv_ref, sink_ref, slope_ref, o_ref):
        c = pl.program_id(0)
        rows = pl.ds(pl.multiple_of(c * BLOCK, BLOCK), 3 * BLOCK)
        for n in range(N_KV_A):
            k_sl, v_sl, q_sl, heads = _attn_heads(n)
            outs = attn_block([q_ref[:, s] for s in q_sl], kv_ref[rows, k_sl], kv_ref[rows, v_sl],
                              [sink_ref[h] for h in heads], [slope_ref[h] for h in heads], c, T)
            for s, o in zip(q_sl, outs):
                o_ref[:, s] = o

    full = lambda a: pl.BlockSpec(a.shape, functools.partial(lambda c, nd: (0,) * nd, nd=a.ndim))
    return pl.pallas_call(
        body, out_shape=jax.ShapeDtypeStruct((T, W_A), F32), grid=(nb,),
        in_specs=[pl.BlockSpec((BLOCK, W_A), lambda c: (c, q_off // W_A)), full(kvp), full(sink), full(slopes)],
        out_specs=pl.BlockSpec((BLOCK, W_A), lambda c: (c, 0)),
        name="attn_fwd", compiler_params=_cparams(("arbitrary",)))(p, kvp, sink, slopes)


def attn_bwd(p, q_off, kvp, sink, slopes, do, T):
    nb = T // BLOCK

    def body(q_ref, kv_ref, sink_ref, slope_ref, do_ref, dq_ref, dkv_ref, dsink_ref):
        c = pl.program_id(0)
        rows = pl.ds(pl.multiple_of(c * BLOCK, BLOCK), 3 * BLOCK)

        @pl.when(c == 0)
        def _():
            dkv_ref[...] = jnp.zeros_like(dkv_ref)
            dsink_ref[...] = jnp.zeros_like(dsink_ref)

        for n in range(N_KV_A):
            k_sl, v_sl, q_sl, heads = _attn_heads(n)
            slopes_n = [slope_ref[h] for h in heads]
            _, vjp = jax.vjp(lambda qs, kk, vv, sks: attn_block(qs, kk, vv, sks, slopes_n, c, T),
                             [q_ref[:, s] for s in q_sl], kv_ref[rows, k_sl], kv_ref[rows, v_sl],
                             [sink_ref[h] for h in heads])
            dqs, dks, dvs, dsks = vjp(tuple(do_ref[:, s] for s in q_sl))
            dkv_ref[rows, k_sl] += dks
            dkv_ref[rows, v_sl] += dvs
            for s, h, dq, dsk in zip(q_sl, heads, dqs, dsks):
                dq_ref[:, s] = dq
                dsink_ref[h] += dsk

    full = lambda a: pl.BlockSpec(a.shape, functools.partial(lambda c, nd: (0,) * nd, nd=a.ndim))
    qspec = pl.BlockSpec((BLOCK, W_A), lambda c: (c, 0))
    return pl.pallas_call(
        body,
        out_shape=[jax.ShapeDtypeStruct((T, W_A), F32), jax.ShapeDtypeStruct(kvp.shape, F32),
                   jax.ShapeDtypeStruct((N_Q_A, 1, 1), F32)],
        grid=(nb,),
        in_specs=[pl.BlockSpec((BLOCK, W_A), lambda c: (c, q_off // W_A)), full(kvp), full(sink), full(slopes), qspec],
        out_specs=[qspec, full(kvp), full(sink)],
        name="attn_bwd", compiler_params=_cparams(("arbitrary",)))(p, kvp, sink, slopes, do)


def mem_fwd(p, q_off, kv, T):
    tr = _row_tile(T)
    assert q_off % W_M == 0

    def body(q_ref, kv_ref, o_ref):
        for h in range(N_HEADS_M):
            hs = pl.ds(h * HEAD_DIM_M, HEAD_DIM_M)
            (o,) = mem_tile(q_ref[:, hs], kv_ref[:, hs], kv_ref[:, pl.ds(W_M + h * HEAD_DIM_M, HEAD_DIM_M)])
            o_ref[:, hs] = o

    return pl.pallas_call(
        body, out_shape=jax.ShapeDtypeStruct((T, W_M), F32), grid=(T // tr,),
        in_specs=[pl.BlockSpec((tr, W_M), lambda i: (i, q_off // W_M)), pl.BlockSpec((N_MEM, 2 * W_M), lambda i: (0, 0))],
        out_specs=pl.BlockSpec((tr, W_M), lambda i: (i, 0)),
        name="mem_fwd", compiler_params=_cparams(("arbitrary",)))(p, kv)


def mem_bwd(p, q_off, kv, do, T):
    tr = _row_tile(T)

    def body(q_ref, kv_ref, do_ref, dq_ref, dkv_ref):
        @pl.when(pl.program_id(0) == 0)
        def _():
            dkv_ref[...] = jnp.zeros_like(dkv_ref)

        for h in range(N_HEADS_M):
            hs = pl.ds(h * HEAD_DIM_M, HEAD_DIM_M)
            vs = pl.ds(W_M + h * HEAD_DIM_M, HEAD_DIM_M)
            _, vjp = jax.vjp(mem_tile, q_ref[:, hs], kv_ref[:, hs], kv_ref[:, vs])
            dq, dk, dv = vjp((do_ref[:, hs],))
            dq_ref[:, hs] = dq
            dkv_ref[:, hs] += dk
            dkv_ref[:, vs] += dv

    kvspec = pl.BlockSpec((N_MEM, 2 * W_M), lambda i: (0, 0))
    return pl.pallas_call(
        body,
        out_shape=[jax.ShapeDtypeStruct((T, W_M), F32), jax.ShapeDtypeStruct((N_MEM, 2 * W_M), F32)],
        grid=(T // tr,),
        in_specs=[pl.BlockSpec((tr, W_M), lambda i: (i, q_off // W_M)), kvspec, pl.BlockSpec((tr, W_M), lambda i: (i, 0))],
        out_specs=[pl.BlockSpec((tr, W_M), lambda i: (i, 0)), kvspec],
        name="mem_bwd", compiler_params=_cparams(("arbitrary",)))(p, kv, do)


def _scan_const_specs(dk):
    C, L = SCAN_CHUNK, SCAN_LEVELS
    return [pl.BlockSpec((2, (2 + L) * C, C), lambda n: (0, 0, 0)),
            pl.BlockSpec((2, C, (2 + L) * C), lambda n: (0, 0, 0)),
            pl.BlockSpec((2, L * C, dk), lambda n: (0, 0, 0)),
            pl.BlockSpec((2, L * C, dk), lambda n: (0, 0, 0)),
            pl.BlockSpec((L * C, C), lambda n: (0, 0))]


def _chunk_spec(src, width, chunk_of):
    arr, sel = src
    if arr.ndim == 2:
        assert sel % width == 0
        return pl.BlockSpec((SCAN_CHUNK, width), functools.partial(lambda n, b: (chunk_of(n), b), b=sel // width))
    return pl.BlockSpec((None, SCAN_CHUNK, width), functools.partial(lambda n, d: (d, chunk_of(n), 0), d=sel))


def _scan_const_args():
    h, ht, qm, km, bm = _scan_consts()
    return [jnp.asarray(h, BF16), jnp.asarray(ht, BF16), jnp.asarray(qm, F32), jnp.asarray(km, F32), jnp.asarray(bm, F32)]


def scan_fwd(name, q, kf, kb, gf, gb, v, heads, dk, dv, T):
    C = SCAN_CHUNK
    N = T // C
    assert dk == C
    W, Wv = heads * dk, heads * dv
    fwd = lambda n: n
    rev = lambda n: N - 1 - n

    def body(qf_ref, qb_ref, kf_ref, kb_ref, gf_ref, gb_ref, vf_ref, vb_ref, h_ref, ht_ref, qm_ref, km_ref, bm_ref,
             of_ref, ob_ref, ssf_ref, ssb_ref, st_ref):
        @pl.when(pl.program_id(0) == 0)
        def _():
            st_ref[...] = jnp.zeros_like(st_ref)

        bm = bm_ref[...]
        dirs = ((qf_ref, kf_ref, gf_ref, vf_ref, of_ref, ssf_ref), (qb_ref, kb_ref, gb_ref, vb_ref, ob_ref, ssb_ref))
        for d, (q_r, k_r, g_r, v_r, o_r, ss_r) in enumerate(dirs):
            consts = (h_ref[d], ht_ref[d], qm_ref[d], km_ref[d], bm)
            for h in range(heads):
                ks, vs = pl.ds(h * dk, dk), pl.ds(h * dv, dv)
                st = st_ref[d, h]
                ss_r[h] = st
                o, st_new = scan_chunk(q_r[:, ks], k_r[:, ks], v_r[:, vs], g_r[:, ks], st, *consts)
                o_r[:, vs] = o
                st_ref[d, h] = st_new

    srcs = [(q, fwd, W), (q, rev, W), (kf, fwd, W), (kb, rev, W), (gf, fwd, W), (gb, rev, W), (v, fwd, Wv), (v, rev, Wv)]
    ss_spec = lambda order: pl.BlockSpec((heads, None, dv, dk), lambda n: (0, order(n), 0, 0))
    return pl.pallas_call(
        body,
        out_shape=[jax.ShapeDtypeStruct((T, Wv), F32)] * 2 + [jax.ShapeDtypeStruct((heads, N, dv, dk), F32)] * 2,
        grid=(N,),
        in_specs=[_chunk_spec(s, w, order) for s, order, w in srcs] + _scan_const_specs(dk),
        out_specs=[pl.BlockSpec((C, Wv), lambda n: (fwd(n), 0)), pl.BlockSpec((C, Wv), lambda n: (rev(n), 0)),
                   ss_spec(fwd), ss_spec(rev)],
        scratch_shapes=[pltpu.VMEM((2, heads, dv, dk), F32)],
        name=name, compiler_params=_cparams(("arbitrary",)))(*[s[0] for s, _, _ in srcs], *_scan_const_args())


def scan_bwd(name, q, kf, kb, gf, gb, v, ss_f, ss_b, do, heads, dk, dv, T):
    C = SCAN_CHUNK
    N = T // C
    W, Wv = heads * dk, heads * dv
    fwd = lambda n: N - 1 - n
    rev = lambda n: n

    def body(qf_ref, qb_ref, kf_ref, kb_ref, gf_ref, gb_ref, vf_ref, vb_ref, ssf_ref, ssb_ref, dof_ref, dob_ref,
             h_ref, ht_ref, qm_ref, km_ref, bm_ref,
             dqf_ref, dkf_ref, dgf_ref, dvf_ref, dqb_ref, dkb_ref, dgb_ref, dvb_ref, dst_ref):
        @pl.when(pl.program_id(0) == 0)
        def _():
            dst_ref[...] = jnp.zeros_like(dst_ref)

        bm = bm_ref[...]
        dirs = ((qf_ref, kf_ref, gf_ref, vf_ref, ssf_ref, dof_ref, dqf_ref, dkf_ref, dgf_ref, dvf_ref),
                (qb_ref, kb_ref, gb_ref, vb_ref, ssb_ref, dob_ref, dqb_ref, dkb_ref, dgb_ref, dvb_ref))
        for d, (q_r, k_r, g_r, v_r, ss_r, do_r, dq_r, dk_r, dg_r, dv_r) in enumerate(dirs):
            consts = (h_ref[d], ht_ref[d], qm_ref[d], km_ref[d], bm)
            for h in range(heads):
                ks, vs = pl.ds(h * dk, dk), pl.ds(h * dv, dv)
                _, vjp = jax.vjp(lambda q_, k_, v_, g_, st_: scan_chunk(q_, k_, v_, g_, st_, *consts),
                                 q_r[:, ks], k_r[:, ks], v_r[:, vs], g_r[:, ks], ss_r[h])
                dq, dk_, dv_, dg, dst = vjp((do_r[:, vs], dst_ref[d, h]))
                dq_r[:, ks] = dq
                dk_r[:, ks] = dk_
                dg_r[:, ks] = dg
                dv_r[:, vs] = dv_
                dst_ref[d, h] = dst

    srcs = [(q, fwd, W), (q, rev, W), (kf, fwd, W), (kb, rev, W), (gf, fwd, W), (gb, rev, W), (v, fwd, Wv), (v, rev, Wv)]
    ss_spec = lambda order: pl.BlockSpec((heads, None, dv, dk), lambda n: (0, order(n), 0, 0))
    kspec = lambda order: pl.BlockSpec((C, W), lambda n: (order(n), 0))
    vspec = lambda order: pl.BlockSpec((C, Wv), lambda n: (order(n), 0))
    return pl.pallas_call(
        body,
        out_shape=([jax.ShapeDtypeStruct((T, W), F32)] * 3 + [jax.ShapeDtypeStruct((T, Wv), F32)]) * 2,
        grid=(N,),
        in_specs=[_chunk_spec(s, w, order) for s, order, w in srcs]
        + [ss_spec(fwd), ss_spec(rev), _chunk_spec(do, Wv, fwd), _chunk_spec(do, Wv, rev)] + _scan_const_specs(dk),
        out_specs=[kspec(fwd)] * 3 + [vspec(fwd)] + [kspec(rev)] * 3 + [vspec(rev)],
        scratch_shapes=[pltpu.VMEM((2, heads, dv, dk), F32)],
        name=name, compiler_params=_cparams(("arbitrary",)))(
            *[s[0] for s, _, _ in srcs], ss_f, ss_b, do[0], do[0], *_scan_const_args())


def final_call(x, g, target, T):
    tr = _row_tile(T)

    def tile(xv, gv, tv):
        y = _rms(xv, gv)
        err = (y - tv) ** 2
        return jnp.sum(jnp.sum(err, axis=-1, keepdims=True), axis=0, keepdims=True) * (0.5 / D_MODEL)

    def body(x_ref, g_ref, t_ref, loss_ref, dx_ref, dg_ref):
        i = pl.program_id(0)
        tv = t_ref[...]
        lv, vjp = jax.vjp(lambda a, b: tile(a, b, tv), x_ref[...], g_ref[...])
        dx, dg = vjp(jnp.ones((1, 1), F32))
        dx_ref[...] = dx

        @pl.when(i == 0)
        def _():
            loss_ref[...] = jnp.zeros_like(loss_ref)
            dg_ref[...] = jnp.zeros_like(dg_ref)

        loss_ref[...] += jnp.broadcast_to(lv, loss_ref.shape)
        dg_ref[...] += dg

    return pl.pallas_call(
        body,
        out_shape=[jax.ShapeDtypeStruct((8, 128), F32), jax.ShapeDtypeStruct((T, D_MODEL), F32),
                   jax.ShapeDtypeStruct((1, D_MODEL), F32)],
        grid=(T // tr,),
        in_specs=[pl.BlockSpec((tr, D_MODEL), lambda i: (i, 0)), pl.BlockSpec((1, D_MODEL), lambda i: (0, 0)),
                  pl.BlockSpec((tr, D_MODEL), lambda i: (i, 0))],
        out_specs=[pl.BlockSpec((8, 128), lambda i: (0, 0)), pl.BlockSpec((tr, D_MODEL), lambda i: (i, 0)),
                   pl.BlockSpec((1, D_MODEL), lambda i: (0, 0))],
        name="final_loss", compiler_params=_cparams(("arbitrary",)))(x, g, target)


def adamw_call(w, g, m, v):
    shape = w.shape
    c = shape[-1]
    r = int(np.prod(shape[:-1])) if len(shape) > 1 else 1
    tr = r if r <= 256 else 256
    assert r % tr == 0

    def body(w_ref, g_ref, m_ref, v_ref, d_ref, nm_ref, nv_ref):
        gv = g_ref[...]
        nm = ADAM_B1 * m_ref[...] + (1.0 - ADAM_B1) * gv
        nv = ADAM_B2 * v_ref[...] + (1.0 - ADAM_B2) * jnp.square(gv)
        m_hat = nm / (1.0 - ADAM_B1 ** ADAM_STEP)
        v_hat = nv / (1.0 - ADAM_B2 ** ADAM_STEP)
        d_ref[...] = -ADAM_LR * (m_hat / (jnp.sqrt(v_hat) + ADAM_EPS) + ADAM_WD * w_ref[...])
        nm_ref[...] = nm
        nv_ref[...] = nv

    spec = pl.BlockSpec((tr, c), lambda i: (i, 0))
    outs = pl.pallas_call(body, out_shape=[jax.ShapeDtypeStruct((r, c), F32)] * 3, grid=(r // tr,),
                          in_specs=[spec] * 4, out_specs=[spec] * 3, name="adamw",
                          compiler_params=_cparams(("arbitrary",)))(*(t.reshape(r, c) for t in (w, g, m, v)))
    return tuple(o.reshape(shape) for o in outs)


def sum_devices(g64):
    def body(x_ref, o_ref):
        acc = x_ref[0:8, :]
        for d in range(1, 8):
            acc = acc + x_ref[8 * d:8 * d + 8, :]
        o_ref[...] = acc

    return pl.pallas_call(body, out_shape=jax.ShapeDtypeStruct((8, D_MODEL), F32), name="sum_devices")(g64)


def _half_tile(rh):
    return rh if rh <= 512 else 256


def add_sibling(g, recv, c, out_dtype):
    _, R, C = g.shape
    rh = R // 2
    tr = _half_tile(rh)
    nblk = rh // tr

    def body(c_ref, g_ref, r_ref, o_ref):
        o_ref[...] = (g_ref[...] + r_ref[...]).astype(o_ref.dtype)

    grid_spec = pltpu.PrefetchScalarGridSpec(
        num_scalar_prefetch=1, grid=(4, nblk),
        in_specs=[pl.BlockSpec((None, tr, C), lambda j, i, c_ref: (j, i + c_ref[0] * nblk, 0)),
                  pl.BlockSpec((None, tr, C), lambda j, i, c_ref: (j, i, 0))],
        out_specs=pl.BlockSpec((None, tr, C), lambda j, i, c_ref: (j, i, 0)))
    return pl.pallas_call(body, out_shape=jax.ShapeDtypeStruct((4, rh, C), out_dtype), grid_spec=grid_spec,
                          name="rs_add_sibling", compiler_params=_cparams(("arbitrary", "arbitrary")))(c, g, recv)


def add_chips(g, recv, r3, place):
    _, R, C = g.shape
    rh = R // 2
    tr = _half_tile(rh)
    nblk = rh // tr

    def body(p_ref, g_ref, s_ref, a_ref, b_ref, c_ref, o_ref):
        up = lambda r: r[...].astype(F32)
        o_ref[...] = (((g_ref[...] + up(s_ref)) + up(a_ref)) + up(b_ref)) + up(c_ref)

    grid_spec = pltpu.PrefetchScalarGridSpec(
        num_scalar_prefetch=1, grid=(nblk,),
        in_specs=[pl.BlockSpec((None, tr, C), lambda i, p_ref: (p_ref[0], i + p_ref[1] * nblk, 0)),
                  pl.BlockSpec((None, tr, C), lambda i, p_ref: (p_ref[0], i, 0))]
        + [pl.BlockSpec((None, tr, C), functools.partial(lambda i, p_ref, k: (k, i, 0), k=k)) for k in range(3)],
        out_specs=pl.BlockSpec((tr, C), lambda i, p_ref: (i + p_ref[1] * nblk, 0)))
    return pl.pallas_call(body, out_shape=jax.ShapeDtypeStruct((R, C), F32), grid_spec=grid_spec,
                          name="rs_add_chips", compiler_params=_cparams(("arbitrary",)))(place, g, recv, r3, r3, r3)


def _remote(src, dst, ssem, rsem, dev):
    return pltpu.make_async_remote_copy(src_ref=src, dst_ref=dst, send_sem=ssem, recv_sem=rsem,
                                        device_id=dev, device_id_type=pl.DeviceIdType.MESH)


def _mesh_places():
    x, y, c = lax.axis_index("x"), lax.axis_index("y"), lax.axis_index("c")
    chips = [(1 - x, y), (x, 1 - y), (1 - x, 1 - y)]
    return x, y, c, (x, y, 1 - c), chips


def _hbm_specs(n):
    return [pl.BlockSpec(memory_space=pltpu.HBM) for _ in range(n)]


def _gather_body(ins, outs, n_split, send_sems, recv_sems, handshake):
    x, y, c, sibling, chips = _mesh_places()
    mine = 2 * x + y
    if handshake:
        barrier = pltpu.get_barrier_semaphore()
        peers = [sibling] + [(*chip, c) for chip in chips]
        for peer in peers:
            pl.semaphore_signal(barrier, inc=1, device_id=peer, device_id_type=pl.DeviceIdType.MESH)
        pl.semaphore_wait(barrier, len(peers))

    def half(a, chip_idx, which):
        rh = ins[a].shape[0] // 2
        return outs[a].at[chip_idx, pl.ds(which * rh, rh), :]

    sent = []
    for a in range(len(ins)):
        for k, chip in enumerate(chips):
            if a < n_split:
                rh = ins[a].shape[0] // 2
                src, dst = ins[a].at[pl.ds(c * rh, rh), :], half(a, mine, c)
            else:
                src, dst = ins[a], outs[a].at[mine]
            sent.append(_remote(src, dst, send_sems.at[a, k], recv_sems.at[a, k], (*chip, c)))
    for cp in sent:
        cp.start()
    for a in range(len(ins)):
        for k, chip in enumerate(chips):
            j = 2 * chip[0] + chip[1]
            region = half(a, j, c) if a < n_split else outs[a].at[j]
            _remote(region, region, send_sems.at[a, k], recv_sems.at[a, k], (*chip, c)).wait_recv()
            if a < n_split:
                fwd = _remote(region, region, send_sems.at[a, 3 + k], recv_sems.at[a, 3 + k], sibling)
                fwd.start()
                sent.append(fwd)
    for a in range(n_split):
        for k, chip in enumerate(chips):
            region = half(a, 2 * chip[0] + chip[1], 1 - c)
            _remote(region, region, send_sems.at[a, 3 + k], recv_sems.at[a, 3 + k], sibling).wait_recv()
    for cp in sent:
        cp.wait_send()


def gather_weights(shards, small):
    arrs = list(shards) + [small]
    n = len(arrs)

    def body(*refs):
        _gather_body(refs[:n], refs[n:2 * n], n - 1, refs[2 * n], refs[2 * n + 1], handshake=False)

    return pl.pallas_call(
        body, out_shape=[jax.ShapeDtypeStruct((4,) + a.shape, a.dtype) for a in arrs],
        in_specs=_hbm_specs(n), out_specs=_hbm_specs(n),
        scratch_shapes=[pltpu.SemaphoreType.DMA((n, 6)), pltpu.SemaphoreType.DMA((n, 6))],
        name="gather_weights")(*arrs)


def gather_weights_async(shards):
    n = len(shards)

    def body(*refs):
        _gather_body(refs[:n], refs[n:2 * n], n, refs[2 * n], refs[2 * n + 1], handshake=True)

    return pl.kernel(
        body, out_type=[jax.ShapeDtypeStruct((4,) + a.shape, a.dtype) for a in shards],
        mesh=plsc.ScalarSubcoreMesh(axis_name="seq", num_cores=1),
        scratch_types=[pltpu.SemaphoreType.DMA((n, 6)), pltpu.SemaphoreType.DMA((n, 6))],
        compiler_params=pltpu.CompilerParams(collective_id=1), name="gather_weights_async")(*shards)


def rs_exchange_siblings(gs):
    n = len(gs)

    def body(*refs):
        ins, outs = refs[:n], refs[n:2 * n]
        send_sems, recv_sems = refs[2 * n:]
        x, y, c, sibling, chips = _mesh_places()
        cps = []
        for a in range(n):
            rh = ins[a].shape[1] // 2
            cps.append(_remote(ins[a].at[:, pl.ds((1 - c) * rh, rh), :], outs[a], send_sems.at[a], recv_sems.at[a], sibling))
        for cp in cps:
            cp.start()
        for cp in cps:
            cp.wait()

    return pl.pallas_call(
        body, out_shape=[jax.ShapeDtypeStruct((4, g.shape[1] // 2, g.shape[2]), g.dtype) for g in gs],
        in_specs=_hbm_specs(n), out_specs=_hbm_specs(n),
        scratch_shapes=[pltpu.SemaphoreType.DMA((n,)), pltpu.SemaphoreType.DMA((n,))],
        name="rs_exchange_siblings")(*gs)


def rs_exchange_chips(s1s):
    n = len(s1s)

    def body(*refs):
        ins, outs = refs[:n], refs[n:2 * n]
        send_sems, recv_sems = refs[2 * n:]
        x, y, c, sibling, chips = _mesh_places()
        cps = []
        for a in range(n):
            for k, chip in enumerate(chips):
                cps.append(_remote(ins[a].at[2 * chip[0] + chip[1]], outs[a].at[k], send_sems.at[a, k],
                                   recv_sems.at[a, k], (*chip, c)))
        for cp in cps:
            cp.start()
        for cp in cps:
            cp.wait()

    return pl.pallas_call(
        body, out_shape=[jax.ShapeDtypeStruct((3,) + s.shape[1:], s.dtype) for s in s1s],
        in_specs=_hbm_specs(n), out_specs=_hbm_specs(n),
        scratch_shapes=[pltpu.SemaphoreType.DMA((n, 3)), pltpu.SemaphoreType.DMA((n, 3))],
        name="rs_exchange_chips")(*s1s)


def rs_share_final(fs):
    n = len(fs)

    def body(*refs):
        bufs = refs[n:2 * n]
        send_sems, recv_sems = refs[2 * n:]
        x, y, c, sibling, chips = _mesh_places()
        cps = []
        for a in range(n):
            rh = bufs[a].shape[0] // 2
            mine = bufs[a].at[pl.ds(c * rh, rh), :]
            cps.append(_remote(mine, mine, send_sems.at[a], recv_sems.at[a], sibling))
        for cp in cps:
            cp.start()
        for a in range(n):
            rh = bufs[a].shape[0] // 2
            other = bufs[a].at[pl.ds((1 - c) * rh, rh), :]
            _remote(other, other, send_sems.at[a], recv_sems.at[a], sibling).wait_recv()
        for cp in cps:
            cp.wait_send()

    return pl.pallas_call(
        body, out_shape=[jax.ShapeDtypeStruct(f.shape, f.dtype) for f in fs],
        in_specs=_hbm_specs(n), out_specs=_hbm_specs(n), input_output_aliases={a: a for a in range(n)},
        scratch_shapes=[pltpu.SemaphoreType.DMA((n,)), pltpu.SemaphoreType.DMA((n,))],
        name="rs_share_final")(*fs)


def allgather_small(v):
    m_per = v.shape[0]

    def body(x_ref, out_ref, send_sems, recv_sems, local_sem):
        x, y, c, sibling, chips = _mesh_places()
        me = (x, y, c)

        def rows(px, py, pc):
            return out_ref.at[pl.ds((4 * px + 2 * py + pc) * m_per, m_per), :]

        def copy(k, block, to, src=None):
            return _remote(rows(*block) if src is None else src, rows(*block), send_sems.at[k], recv_sems.at[k], to)

        mine = pltpu.make_async_copy(x_ref, rows(*me), local_sem)
        mine.start()
        first = [copy(0, me, sibling, src=x_ref)]
        first += [copy(1 + j, me, (*chip, c), src=x_ref) for j, chip in enumerate(chips)]
        for cp in first:
            cp.start()
        passed = [copy(4 + j, (*chip, c), sibling) for j, chip in enumerate(chips)]
        for j, chip in enumerate(chips):
            copy(1 + j, (*chip, c), me).wait_recv()
            passed[j].start()
        copy(0, sibling, me).wait_recv()
        for j, chip in enumerate(chips):
            copy(4 + j, (*chip, 1 - c), me).wait_recv()
        for cp in first + passed:
            cp.wait_send()
        mine.wait()

    return pl.pallas_call(
        body, out_shape=jax.ShapeDtypeStruct((8 * m_per, v.shape[1]), v.dtype),
        in_specs=[pl.BlockSpec(memory_space=pltpu.VMEM)], out_specs=pl.BlockSpec(memory_space=pltpu.VMEM),
        scratch_shapes=[pltpu.SemaphoreType.DMA((7,)), pltpu.SemaphoreType.DMA((7,)), pltpu.SemaphoreType.DMA],
        name="allgather_small")(v)


def rms_res_tile(x, g):
    return (_rms(x, g), x)


def _lower_bounds(lb_param):
    lbs = jax.nn.softmax(lb_param.astype(F32), axis=0)
    return jnp.cumsum(lbs, axis=0) - lbs[0]


def _heads_major(t, n):
    return t.reshape(t.shape[0], n, HEAD_DIM_A).transpose(1, 0, 2)


def _heads_minor(t):
    return t.transpose(1, 0, 2).reshape(t.shape[1], t.shape[0] * t.shape[2])


def _even_fwd(x, i, W, lower, kv, slopes, T):
    O = EVEN_OFF
    g = W["norm_even"][i].reshape(1, D_MODEL)
    (h,) = rows_call("rms_fwd", rms_tile, T, [("row", x, 0, D_MODEL), ("full", g)], [D_MODEL], [BF16])
    p = matmul("mm_in_e", h, W["w_in_e"][i], "nn")
    kvp = jnp.pad(p[:, O["kA"]:O["kA"] + 2 * W_KV_A], ((BLOCK, BLOCK), (0, 0)))
    sink = W["sink"][i].reshape(N_Q_A, 1, 1)
    a = attn_fwd(p, O["qA"], kvp, sink, slopes, T)
    prep_ins = [("row", p, O["qB"], W_B), ("row", p, O["zf"], W_B), ("row", p, O["zb"], W_B),
                ("full", lower[i][0:1]), ("full", lower[i][1:2])]
    qh, k2, g2 = rows_call("hgrn_prep_fwd", hgrn_prep_tile, T, prep_ins, [W_B] * 3, stacks=[(0,), (1, 2), (3, 4)])
    scan_srcs = [(qh, 0), (k2, 0), (k2, 1), (g2, 0), (g2, 1), (p, O["iB"])]
    o_f, o_b, ss_f, ss_b = scan_fwd("scan_fwd_h", *scan_srcs, N_HEADS_B, HEAD_DIM_B, HEAD_DIM_B, T)
    mo = mem_fwd(p, O["qM"], kv, T)
    hg = W["hgrn_norm"][i].reshape(1, W_B)
    post_ins = [("row", a, 0, W_A), ("row", o_f, 0, W_B), ("row", o_b, 0, W_B), ("row", mo, 0, W_M),
                ("row", p, O["gA"], W_A), ("row", p, O["gB"], W_B), ("row", p, O["gM"], W_M), ("full", hg)]
    (mix,) = rows_call("even_post_fwd", even_post_tile, T, post_ins, [MIX], [BF16])
    x_new = matmul("mm_out", mix, W["w_out_e"][i], "nn", add=x)
    return x_new, dict(x=x, g=g, h=h, p=p, kvp=kvp, sink=sink, prep_ins=prep_ins,
                       scan_srcs=scan_srcs, ss_f=ss_f, ss_b=ss_b, post_ins=post_ins, mix=mix)


def _assemble_even(dqA, dgA, dqB, dzf, dzb, dv0, dv1, dgB, dqM, dgM, dkvA):
    return (jnp.concatenate([dqA, dgA, dqB, dzf, dzb, dv0 + dv1, dgB, dqM, dgM, dkvA], axis=-1),)


def _even_bwd(dxo, sv, i, W, kv, slopes, T):
    O = EVEN_OFF
    p = sv["p"]
    dmix = matmul("mm_dmix", dxo, W["w_out_e"][i], "nt")
    dwo = matmul("mm_dwo", sv["mix"], dxo, "tn")
    da, dof, dmo, dgA, dgB, dgM, dhg = rows_vjp_call("even_post_bwd", even_post_tile, T, sv["post_ins"],
                                                      [[("row", dmix, 0, MIX)]], skip=(2,))
    dqA, dkvp, dsink = attn_bwd(p, O["qA"], sv["kvp"], sv["sink"], slopes, da, T)
    dkvA = dkvp[BLOCK:-BLOCK]
    dqf, dkf, dgf, dvf, dqb, dkb, dgb, dvb = scan_bwd("scan_bwd_h", *sv["scan_srcs"], sv["ss_f"], sv["ss_b"], (dof, 0),
                                                      N_HEADS_B, HEAD_DIM_B, HEAD_DIM_B, T)
    row = lambda arr, w: ("row", arr, 0, w)
    dqB, dzf, dzb, dlow_f, dlow_b = rows_vjp_call(
        "hgrn_prep_bwd", hgrn_prep_tile, T, sv["prep_ins"],
        [[row(dqf, W_B), row(dqb, W_B)], [row(dkf, W_B)], [row(dkb, W_B)], [row(dgf, W_B)], [row(dgb, W_B)]])
    dlow = jnp.concatenate([dlow_f, dlow_b], axis=0)
    dqM, dkv = mem_bwd(p, O["qM"], kv, dmo, T)
    (dp,) = rows_call("even_dp", _assemble_even, T,
                      [row(dqA, W_A), row(dgA, W_A), row(dqB, W_B), row(dzf, W_B), row(dzb, W_B), row(dvf, W_B), row(dvb, W_B),
                       row(dgB, W_B), row(dqM, W_M), row(dgM, W_M), row(dkvA, 2 * W_KV_A)],
                      [EVEN_IN], [BF16])
    dh = matmul("mm_dh_e", dp, W["w_in_e"][i], "nt")
    dwi = matmul("mm_dwi_e", sv["h"], dp, "tn")
    dx, dg = rows_vjp_call("rms_res_bwd", rms_res_tile, T, [("row", sv["x"], 0, D_MODEL), ("full", sv["g"])],
                           [[("row", dh, 0, D_MODEL)], [("row", dxo, 0, D_MODEL)]])
    return dx, dict(w_in=dwi, w_out=dwo, norm=dg[0], sink=dsink.reshape(N_Q_A), low=dlow, hg=dhg[0], kv=dkv)


def _pad_gate_up(w_up):
    z = jnp.zeros((2, 128, WK_C), F32)
    z = z.at[0, 0:GATE_RANK].set(w_up[0])
    return z.at[1, GATE_RANK:2 * GATE_RANK].set(w_up[1])


def _odd_fwd(x, i, W, kv, T):
    O = ODD_OFF
    g = W["norm_odd"][i].reshape(1, D_MODEL)
    (h,) = rows_call("rms_fwd", rms_tile, T, [("row", x, 0, D_MODEL), ("full", g)], [D_MODEL], [BF16])
    p = matmul("mm_in_o", h, W["w_in_o"][i], "nn")
    wup = _pad_gate_up(W["w_gate_up"][i])
    prep_ins = [("row", p, O["qC"], WK_C), ("row", p, O["rr"], 128), ("full", wup[0]), ("full", wup[1]),
                ("full", W["b_gate"][i][0:1]), ("full", W["b_gate"][i][1:2])]
    qg, g2 = rows_call("gla_prep_fwd", gla_prep_tile, T, prep_ins, [WK_C] * 2, stacks=[(0,), (1, 2)])
    scan_srcs = [(qg, 0), (p, O["kC"]), (p, O["kC"]), (g2, 0), (g2, 1), (p, O["vC"])]
    o_f, o_b, ss_f, ss_b = scan_fwd("scan_fwd_g", *scan_srcs, N_HEADS_C, DK_C, DV_C, T)
    mo = mem_fwd(p, O["qM"], kv, T)
    gg = W["gla_norm"][i].reshape(1, WV_C)
    post_ins = [("row", o_f, 0, WV_C), ("row", o_b, 0, WV_C), ("row", mo, 0, W_M),
                ("row", p, O["gC"], WV_C), ("row", p, O["gM"], W_M), ("full", gg)]
    (mix,) = rows_call("odd_post_fwd", odd_post_tile, T, post_ins, [MIX], [BF16])
    x_new = matmul("mm_out", mix, W["w_out_o"][i], "nn", add=x)
    return x_new, dict(x=x, g=g, h=h, p=p, prep_ins=prep_ins, scan_srcs=scan_srcs, ss_f=ss_f, ss_b=ss_b,
                       post_ins=post_ins, mix=mix)


def _assemble_odd(dqC, dk0, dk1, dv0, dv1, dgC, dqM, dgM, dr):
    return (jnp.concatenate([dqC, dk0 + dk1, dv0 + dv1, dgC, dqM, dgM, dr], axis=-1),)


def _odd_bwd(dxo, sv, i, W, kv, T):
    O = ODD_OFF
    p = sv["p"]
    dmix = matmul("mm_dmix", dxo, W["w_out_o"][i], "nt")
    dwo = matmul("mm_dwo", sv["mix"], dxo, "tn")
    dof, dmo, dgC, dgM, dgg = rows_vjp_call("odd_post_bwd", odd_post_tile, T, sv["post_ins"],
                                            [[("row", dmix, 0, MIX)]], skip=(1,))
    dqf, dkf, dgf, dvf, dqb, dkb, dgb, dvb = scan_bwd("scan_bwd_g", *sv["scan_srcs"], sv["ss_f"], sv["ss_b"], (dof, 0),
                                                      N_HEADS_C, DK_C, DV_C, T)
    row = lambda arr, w: ("row", arr, 0, w)
    dqC, dr, dwup_f, dwup_b, dbg_f, dbg_b = rows_vjp_call(
        "gla_prep_bwd", gla_prep_tile, T, sv["prep_ins"],
        [[row(dqf, WK_C), row(dqb, WK_C)], [row(dgf, WK_C)], [row(dgb, WK_C)]])
    dqM, dkv = mem_bwd(p, O["qM"], kv, dmo, T)
    (dp,) = rows_call("odd_dp", _assemble_odd, T,
                      [row(dqC, WK_C), row(dkf, WK_C), row(dkb, WK_C), row(dvf, WV_C), row(dvb, WV_C),
                       row(dgC, WV_C), row(dqM, W_M), row(dgM, W_M), row(dr, 128)],
                      [ODD_PAD], [BF16])
    dh = matmul("mm_dh_o", dp, W["w_in_o"][i], "nt")
    dwi = matmul("mm_dwi_o", sv["h"], dp, "tn")
    dx, dg = rows_vjp_call("rms_res_bwd", rms_res_tile, T, [("row", sv["x"], 0, D_MODEL), ("full", sv["g"])],
                           [[("row", dh, 0, D_MODEL)], [("row", dxo, 0, D_MODEL)]])
    dw_up = jnp.stack([dwup_f[0:GATE_RANK], dwup_b[GATE_RANK:2 * GATE_RANK]])
    dbg = jnp.concatenate([dbg_f, dbg_b], axis=0)
    return dx, dict(w_in=dwi, w_out=dwo, norm=dg[0], w_up=dw_up, b_gate=dbg, gg=dgg[0], kv=dkv)


def local_step(x, mem, target, W, later=None):
    T = x.shape[0]
    slopes = (2.0 ** (-8.0 * jnp.arange(1, N_Q_A + 1, dtype=F32) / N_Q_A)).reshape(N_Q_A, 1, 1)
    lower, lower_vjp = jax.vjp(_lower_bounds, W["lb_param"])
    mem_g = W["mem_norm"].reshape(1, D_MODEL)
    (mem_n,) = rows_call("mem_rms_fwd", rms_tile, N_MEM, [("row", mem, 0, D_MODEL), ("full", mem_g)], [D_MODEL], [BF16])
    kvs, saved = [], []
    for l in range(DEPTH):
        if l == 1 and later is not None:
            x, W = later(x, W)
        kvs.append(matmul("mm_kv", mem_n, W["w_kv"][l], "nn"))
        if l % 2 == 0:
            x, sv = _even_fwd(x, l // 2, W, lower, kvs[l], slopes, T)
        else:
            x, sv = _odd_fwd(x, l // 2, W, kvs[l], T)
        saved.append(sv)
    loss, dx, dgf = final_call(x, W["final_norm"].reshape(1, D_MODEL), target, T)
    per = [None] * DEPTH
    for l in reversed(range(DEPTH)):
        if l % 2 == 0:
            dx, per[l] = _even_bwd(dx, saved[l], l // 2, W, kvs[l], slopes, T)
        else:
            dx, per[l] = _odd_bwd(dx, saved[l], l // 2, W, kvs[l], T)
    dmem_n, dw_kv = None, []
    for l in range(DEPTH):
        dw_kv.append(matmul("mm_dwkv", mem_n, per[l]["kv"], "tn"))
        dmem_n = matmul("mm_dmem", per[l]["kv"], W["w_kv"][l], "nt", add=dmem_n)
    (dmem_norm,) = rows_vjp_call("mem_rms_bwd", rms_tile, N_MEM, [("row", mem, 0, D_MODEL), ("full", mem_g)],
                                 [[("row", dmem_n, 0, D_MODEL)]], skip=(0,))
    ev, od = (per[0], per[2]), (per[1], per[3])
    (d_lb,) = lower_vjp(jnp.stack([e["low"] for e in ev]))
    grads = dict(
        w_in_e=jnp.stack([e["w_in"] for e in ev]), w_in_o=jnp.stack([o["w_in"] for o in od]),
        w_out_e=jnp.stack([e["w_out"] for e in ev]), w_out_o=jnp.stack([o["w_out"] for o in od]),
        w_kv=jnp.stack(dw_kv), norm_even=jnp.stack([e["norm"] for e in ev]), sink=jnp.stack([e["sink"] for e in ev]),
        lb_param=d_lb, hgrn_norm=jnp.stack([e["hg"] for e in ev]), norm_odd=jnp.stack([o["norm"] for o in od]),
        w_gate_up=jnp.stack([o["w_up"] for o in od]), b_gate=jnp.stack([o["b_gate"] for o in od]),
        gla_norm=jnp.stack([o["gg"] for o in od]), mem_norm=dmem_norm[0], final_norm=dgf[0])
    return loss, dx, grads


SMALL_SPECS = (("lb_param", (2, 2, 128)), ("norm_odd", (2, 256)), ("w_gate_up", (2, 2, 16, 128)),
               ("b_gate", (2, 2, 128)), ("gla_norm", (2, 256)))
SMALL_ROWS = 80


def _pack_small_local(d):
    return jnp.concatenate([d[n].reshape(-1) for n, _ in SMALL_SPECS]).reshape(SMALL_ROWS, 128)


def _unpack_small_local(b):
    flat, out, o = b.reshape(-1), {}, 0
    for n, shp in SMALL_SPECS:
        sz = int(np.prod(shp))
        out[n] = flat[o:o + sz].reshape(shp)
        o += sz
    return out


def _unpack_small_full(g4):
    per = [_unpack_small_local(g4[j]) for j in range(4)]
    return {n: jnp.concatenate([per[j][n] for j in range(4)], axis=-1) for n, _ in SMALL_SPECS}


def _pack_small_blocks(full):
    blocks = []
    for j in range(4):
        blocks.append(_pack_small_local({n: full[n][..., j * shp[-1]:(j + 1) * shp[-1]] for n, shp in SMALL_SPECS}))
    return jnp.stack(blocks)


def _cols(t, order, off, widths):
    return [t[..., off[n]:off[n] + widths[n]] for n in order]


EVEN_REF_ORDER = ("qA", "kA", "vA", "gA", "qB", "zf", "zb", "iB", "gB", "qM", "gM")
ODD_REF_ORDER = ("qC", "kC", "vC", "gC", "rr", "qM", "gM")


def _layer_weights(l, g_in, g_out, g_kv):
    t = g_in.transpose(1, 0, 2).reshape(D_MODEL, -1)
    if l % 2 == 0:
        w_in = jnp.concatenate(_cols(t, EVEN_ORDER, EVEN_REF_OFF, EVEN_W), axis=-1)
    else:
        w_in = jnp.concatenate(_cols(t, ODD_ORDER, ODD_REF_OFF, ODD_W) + [jnp.zeros((D_MODEL, ODD_PAD - ODD_IN), BF16)],
                               axis=-1)
    return w_in, g_out.reshape(MIX, D_MODEL), g_kv.reshape(D_MODEL, 2 * W_M)


def _grad_blocks(grads):
    t = jnp.concatenate(_cols(grads["w_in_e"], EVEN_REF_ORDER, EVEN_OFF, EVEN_W), axis=-1)
    b_in_e = t.reshape(2, D_MODEL, 4, EVEN_IN // 4).transpose(2, 0, 1, 3).reshape(4, 2 * D_MODEL, EVEN_IN // 4)
    t = jnp.concatenate(_cols(grads["w_in_o"], ODD_REF_ORDER, ODD_OFF, ODD_W), axis=-1)
    b_in_o = t.reshape(2, D_MODEL, 4, ODD_IN // 4).transpose(2, 0, 1, 3).reshape(4, 2 * D_MODEL, ODD_IN // 4)
    rows_to_blocks = lambda g: g.reshape(g.shape[0], 4, g.shape[1] // 4, g.shape[2]).transpose(1, 0, 2, 3).reshape(
        4, g.shape[0] * (g.shape[1] // 4), g.shape[2])
    return [b_in_e, b_in_o, rows_to_blocks(grads["w_out_e"]), rows_to_blocks(grads["w_out_o"]),
            rows_to_blocks(grads["w_kv"]), _pack_small_blocks(grads)]


WEIGHT_NAMES = ("norm_even", "w_in_even", "sink", "lb_param", "hgrn_norm", "w_out_even", "norm_odd", "w_in_odd",
                "w_gate_up", "b_gate", "gla_norm", "w_out_odd", "mem_norm", "w_mem_kv", "final_norm")


def kernel(x, mem, norm_even, w_in_even, sink, lb_param, hgrn_norm, w_out_even, norm_odd, w_in_odd, w_gate_up, b_gate, gla_norm, w_out_odd, mem_norm, w_mem_kv, final_norm, loss_target, m_norm_even, m_w_in_even, m_sink, m_lb_param, m_hgrn_norm, m_w_out_even, m_norm_odd, m_w_in_odd, m_w_gate_up, m_b_gate, m_gla_norm, m_w_out_odd, m_mem_norm, m_w_mem_kv, m_final_norm, v_norm_even, v_w_in_even, v_sink, v_lb_param, v_hgrn_norm, v_w_out_even, v_norm_odd, v_w_in_odd, v_w_gate_up, v_b_gate, v_gla_norm, v_w_out_odd, v_mem_norm, v_w_mem_kv, v_final_norm):
    w = dict(zip(WEIGHT_NAMES, (norm_even, w_in_even, sink, lb_param, hgrn_norm, w_out_even, norm_odd, w_in_odd,
                                w_gate_up, b_gate, gla_norm, w_out_odd, mem_norm, w_mem_kv, final_norm)))
    m = dict(zip(WEIGHT_NAMES, (m_norm_even, m_w_in_even, m_sink, m_lb_param, m_hgrn_norm, m_w_out_even, m_norm_odd,
                                m_w_in_odd, m_w_gate_up, m_b_gate, m_gla_norm, m_w_out_odd, m_mem_norm, m_w_mem_kv,
                                m_final_norm)))
    v = dict(zip(WEIGHT_NAMES, (v_norm_even, v_w_in_even, v_sink, v_lb_param, v_hgrn_norm, v_w_out_even, v_norm_odd,
                                v_w_in_odd, v_w_gate_up, v_b_gate, v_gla_norm, v_w_out_odd, v_mem_norm, v_w_mem_kv,
                                v_final_norm)))
    ci = lax.axis_index("c").astype(jnp.int32).reshape(1)
    chip = (2 * lax.axis_index("x") + lax.axis_index("y")).astype(jnp.int32).reshape(1)

    shards = []
    for l in range(DEPTH):
        names = ("w_in_even", "w_out_even") if l % 2 == 0 else ("w_in_odd", "w_out_odd")
        shards.append([w[names[0]][l // 2].astype(BF16), w[names[1]][l // 2].astype(BF16), w_mem_kv[l].astype(BF16)])
    small = _pack_small_local(w)
    own = lambda g, s: lax.dynamic_update_slice(g, s[None], (chip[0], 0, 0))
    first = [own(g, s) for g, s in zip(gather_weights(shards[0], small), shards[0] + [small])]
    later_shards = shards[1] + shards[2] + shards[3]
    later_raw = gather_weights_async(later_shards)
    w0 = _layer_weights(0, *first[0:3])
    W = dict(w_in_e=[w0[0]], w_out_e=[w0[1]], w_kv=[w0[2]])
    W.update(_unpack_small_full(first[3]))
    W.update({n: w[n] for n in ("norm_even", "sink", "hgrn_norm", "mem_norm", "final_norm")})

    def later(x1, W):
        x1, raw = lax.optimization_barrier((x1, list(later_raw)))
        g = [own(a, s) for a, s in zip(raw, later_shards)]
        w1, w2, w3 = (_layer_weights(l, *g[3 * (l - 1):3 * l]) for l in (1, 2, 3))
        W = dict(W)
        W.update(w_in_e=[w0[0], w2[0]], w_in_o=[w1[0], w3[0]], w_out_e=[w0[1], w2[1]], w_out_o=[w1[1], w3[1]],
                 w_kv=[w0[2], w1[2], w2[2], w3[2]])
        return x1, W

    loss_tile, dx, grads = local_step(x[0], mem[0], loss_target[0], W, later)

    blocks = _grad_blocks(grads)
    recv = rs_exchange_siblings(blocks)
    wire = [BF16] * (len(blocks) - 1) + [F32]
    chip_sums = [add_sibling(g, r, ci, dt) for g, r, dt in zip(blocks, recv, wire)]
    recv3 = rs_exchange_chips(chip_sums)
    place = jnp.concatenate([chip, ci])
    halves = [add_chips(g, r, r3, place) for g, r, r3 in zip(blocks, recv, recv3)]
    g_in_e, g_in_o, g_out_e, g_out_o, g_kv, g_small = rs_share_final(halves)
    gl = _unpack_small_local(g_small)
    gl.update(w_in_even=g_in_e.reshape(w_in_even.shape), w_in_odd=g_in_o.reshape(w_in_odd.shape),
              w_out_even=g_out_e.reshape(w_out_even.shape), w_out_odd=g_out_o.reshape(w_out_odd.shape),
              w_mem_kv=g_kv.reshape(w_mem_kv.shape))

    pack = jnp.zeros((8, D_MODEL), F32)
    pack = pack.at[0:2].set(grads["norm_even"]).at[2].set(grads["hgrn_norm"].reshape(-1))
    pack = pack.at[3].set(grads["mem_norm"]).at[4].set(grads["final_norm"])
    pack = pack.at[5, 0:16].set(grads["sink"].reshape(-1)).at[5, 16].set(loss_tile[0, 0])
    tot = sum_devices(allgather_small(pack))
    gl.update(norm_even=tot[0:2], hgrn_norm=tot[2].reshape(2, W_B), mem_norm=tot[3], final_norm=tot[4],
              sink=tot[5, 0:16].reshape(2, N_Q_A))
    loss = tot[5, 16]

    upd = {n: adamw_call(w[n], gl[n], m[n], v[n]) for n in WEIGHT_NAMES}
    return (loss, dx[None], *[gl[n] for n in WEIGHT_NAMES], *[upd[n][0] for n in WEIGHT_NAMES],
            *[upd[n][1] for n in WEIGHT_NAMES], *[upd[n][2] for n in WEIGHT_NAMES])
```

```python
import functools

import numpy as np
import jax
import jax.numpy as jnp
from jax import lax
from jax.experimental import pallas as pl
from jax.experimental.pallas import tpu as pltpu
from jax.experimental.pallas import tpu_sc as plsc

F32 = jnp.float32
BF16 = jnp.bfloat16

D_MODEL = 1024
DEPTH = 4
N_Q_A, N_KV_A, HEAD_DIM_A = 8, 2, 64
W_A, W_KV_A = 512, 128
WINDOW = 128
BLOCK = 128
N_HEADS_B, HEAD_DIM_B, W_B = 4, 128, 512
N_HEADS_C, DK_C, DV_C, WK_C, WV_C = 4, 128, 256, 512, 1024
GATE_RANK = 16
GATE_TEMP = 16.0
N_MEM, N_HEADS_M, HEAD_DIM_M, W_M = 256, 4, 128, 512
EPS = 1e-6
MASK_VALUE = -1e30
MIN_GATE = 1e-30
EVEN_IN, ODD_IN = 4864, 4128
ODD_PAD = 4224
MIX = 1536
ADAM_LR, ADAM_B1, ADAM_B2, ADAM_EPS, ADAM_WD, ADAM_STEP = 0.001, 0.9, 0.999, 1e-08, 0.01, 10

SCAN_CHUNK = 128
SCAN_LEVELS = 7
VMEM_LIMIT = 56 * 1024 * 1024

EVEN_REF_OFF = dict(qA=0, kA=512, vA=640, gA=768, qB=1280, zf=1792, zb=2304, iB=2816, gB=3328, qM=3840, gM=4352)
EVEN_W = dict(qA=512, kA=128, vA=128, gA=512, qB=512, zf=512, zb=512, iB=512, gB=512, qM=512, gM=512)
EVEN_ORDER = ("qA", "gA", "qB", "zf", "zb", "iB", "gB", "qM", "gM", "kA", "vA")
ODD_REF_OFF = dict(qC=0, kC=512, vC=1024, gC=2048, rr=3072, qM=3104, gM=3616)
ODD_W = dict(qC=512, kC=512, vC=1024, gC=1024, rr=32, qM=512, gM=512)
ODD_ORDER = ("qC", "kC", "vC", "gC", "qM", "gM", "rr")


def _offsets(order, widths):
    off, o = {}, 0
    for n in order:
        off[n] = o
        o += widths[n]
    return off


EVEN_OFF = _offsets(EVEN_ORDER, EVEN_W)
ODD_OFF = _offsets(ODD_ORDER, ODD_W)


def _dg(a, b, ca, cb):
    return lax.dot_general(a.astype(BF16), b.astype(BF16), (((ca,), (cb,)), ((), ())),
                           preferred_element_type=F32)


def dot_nn(a, b):
    return _dg(a, b, 1, 0)


def dot_nt(a, b):
    return _dg(a, b, 1, 1)


def dot_tn(a, b):
    return _dg(a, b, 0, 0)


@jax.custom_vjp
def bdot(a, b):
    return dot_nn(a, b)


bdot.defvjp(lambda a, b: (dot_nn(a, b), (a, b)),
            lambda r, g: (dot_nt(g, r[1]), dot_tn(r[0], g)))


@jax.custom_vjp
def bdot_t(a, b):
    return dot_nt(a, b)


bdot_t.defvjp(lambda a, b: (dot_nt(a, b), (a, b)),
              lambda r, g: (dot_nn(g, r[1]), dot_tn(g, r[0])))


@jax.custom_vjp
def bdot_tn(a, b):
    return dot_tn(a, b)


bdot_tn.defvjp(lambda a, b: (dot_tn(a, b), (a, b)),
               lambda r, g: (dot_nt(r[1], g), dot_nn(r[0], g)))


def _split_mm(h, x):
    hi = x.astype(BF16)
    lo = (x - hi.astype(F32)).astype(BF16)
    return (lax.dot_general(h, hi, (((1,), (0,)), ((), ())), preferred_element_type=F32)
            + lax.dot_general(h, lo, (((1,), (0,)), ((), ())), preferred_element_type=F32))


@jax.custom_vjp
def hdot(h, ht, x):
    return _split_mm(h, x)


hdot.defvjp(lambda h, ht, x: (_split_mm(h, x), (h, ht)),
            lambda r, g: (jnp.zeros_like(r[0]), jnp.zeros_like(r[1]), _split_mm(r[1], g)))


def _sigmoid(z):
    return 1.0 / (1.0 + jnp.exp(-z))


def _silu(z):
    return z * _sigmoid(z)


def _log_sigmoid(z):
    return jnp.minimum(z, 0.0) - jnp.log(1.0 + jnp.exp(-jnp.abs(z)))


def _rms(x, g):
    return x * lax.rsqrt(jnp.mean(x * x, axis=-1, keepdims=True) + EPS) * g


def rms_tile(x, g):
    return (_rms(x, g),)


@functools.partial(jax.custom_vjp, nondiff_argnums=(1, 2))
def split(x, n, axis):
    w = x.shape[axis] // n
    return tuple(lax.slice_in_dim(x, h * w, (h + 1) * w, axis=axis) for h in range(n))


split.defvjp(lambda x, n, axis: (split(x, n, axis), None),
             lambda n, axis, _, cts: (jnp.concatenate(cts, axis=axis),))


def _group_rms(o, g, heads):
    return jnp.concatenate([_rms(oh, gh) for oh, gh in zip(split(o, heads, 1), split(g, heads, 1))], axis=-1)


def even_post_tile(a, o2f, o2b, mo, gA, gB, gM, hg):
    y = _group_rms(o2f + o2b, hg, N_HEADS_B)
    return (jnp.concatenate([a * _silu(gA), y * _silu(gB), mo * _silu(gM)], axis=-1),)


def odd_post_tile(o2f, o2b, mo, gC, gM, gg):
    y = _group_rms(o2f + o2b, gg, N_HEADS_C)
    return (jnp.concatenate([y * _silu(gC), mo * _silu(gM)], axis=-1),)


def hgrn_prep_tile(qB, zf, zb, low_f, low_b):
    ks, gs = [], []
    for z, lb in ((zf, low_f), (zb, low_b)):
        f = lb + (1.0 - lb) * _sigmoid(z)
        gs.append(jnp.log(jnp.maximum(f, MIN_GATE)))
        ks.append((1.0 - lb) * _sigmoid(-z))
    return (_silu(qB), ks[0], ks[1], gs[0], gs[1])


def gla_prep_tile(qC, r128, wup_f, wup_b, bg_f, bg_b):
    gs = [_log_sigmoid(bdot(r128, wup) + bg) / GATE_TEMP for wup, bg in ((wup_f, bg_f), (wup_b, bg_b))]
    return (qC * (DK_C ** -0.5), gs[0], gs[1])


def mem_tile(q, k, v):
    s = bdot_t(q, k) * (HEAD_DIM_M ** -0.5)
    m = lax.stop_gradient(jnp.max(s, axis=-1, keepdims=True))
    p = jnp.exp(s - m)
    p = p / jnp.sum(p, axis=-1, keepdims=True)
    return (bdot(p, v),)


def attn_block(qs, ks, vs, sinks, slopes, c, seq):
    i = lax.broadcasted_iota(jnp.int32, (BLOCK, 3 * BLOCK), 0)
    j = lax.broadcasted_iota(jnp.int32, (BLOCK, 3 * BLOCK), 1)
    dist = jnp.abs(i - j + BLOCK).astype(F32)
    kpos = (c - 1) * BLOCK + j
    valid = (dist <= WINDOW) & (kpos >= 0) & (kpos < seq)
    outs = []
    for q, sk, slope in zip(qs, sinks, slopes):
        s = bdot_t(q, ks) * (HEAD_DIM_A ** -0.5)
        s = jnp.where(valid, s - slope * dist, MASK_VALUE)
        m = lax.stop_gradient(jnp.maximum(jnp.max(s, axis=-1, keepdims=True), sk))
        p = jnp.where(valid, jnp.exp(s - m), 0.0)
        denom = jnp.sum(p, axis=-1, keepdims=True) + jnp.exp(sk - m)
        outs.append(bdot(p, vs) / denom)
    return tuple(outs)


def scan_chunk(q, k, v, g, st, h, ht, qm, km, bm):
    C = SCAN_CHUNK
    e = split(hdot(h, ht, g), 2 + SCAN_LEVELS, 0)
    qe = q * jnp.exp(e[0])
    kd = k * jnp.exp(e[1])
    tot = jnp.sum(g, axis=0, keepdims=True)
    r = lax.broadcasted_iota(jnp.int32, (C, C), 0)
    s = lax.broadcasted_iota(jnp.int32, (C, C), 1)
    a = jnp.where(r == s, jnp.sum(q * k, axis=-1, keepdims=True), 0.0)
    for l in range(SCAN_LEVELS):
        el = jnp.exp(e[2 + l])
        qs = q * el * qm[l * C:(l + 1) * C]
        ks = k * el * km[l * C:(l + 1) * C]
        a = a + bdot_t(qs, ks) * bm[l * C:(l + 1) * C]
    o = bdot_t(qe, st) + bdot(a, v)
    st_new = st * jnp.exp(tot) + bdot_tn(v, kd)
    return o, st_new


def _scan_consts():
    C, L = SCAN_CHUNK, SCAN_LEVELS
    t = np.arange(C)[:, None]
    r = np.arange(C)[None, :]
    blocks = [(r <= t), (r > t)]
    qms, kms, bms = [], [], []
    for l in range(1, L + 1):
        m = C >> l
        upper_t = (t % (2 * m)) >= m
        same_half = (t // m) == (r // m)
        blocks.append(same_half & np.where(upper_t, r <= t, r > t))
        qms.append(np.broadcast_to(upper_t, (C, C)))
        kms.append(np.broadcast_to(~upper_t, (C, C)))
        bms.append((t // (2 * m)) == (r // (2 * m)))
    hf = np.concatenate(blocks, axis=0).astype(np.float32)
    flip = lambda mat: mat.reshape(-1, C, C)[:, ::-1, ::-1].reshape(-1, C)
    hb = flip(hf)
    qmf = np.concatenate(qms, axis=0).astype(np.float32)
    kmf = np.concatenate(kms, axis=0).astype(np.float32)
    bm = np.concatenate(bms, axis=0).astype(np.float32)
    h = np.stack([hf, hb])
    ht = np.stack([hf.T, hb.T])
    qm = np.stack([qmf, kmf])
    km = np.stack([kmf, qmf])
    return h, ht, qm, km, bm


def _cparams(sem):
    return pltpu.CompilerParams(dimension_semantics=sem, vmem_limit_bytes=VMEM_LIMIT)


def _row_tile(T):
    return min(T, 256)


def _in_spec(spec, tr):
    kind = spec[0]
    if kind == "row":
        _, arr, off, w = spec
        assert off % w == 0
        return arr, pl.BlockSpec((tr, w), functools.partial(lambda i, b: (i, b), b=off // w))
    if kind == "row3":
        _, arr, d, off, w = spec
        assert off % w == 0
        return arr, pl.BlockSpec((None, tr, w), functools.partial(lambda i, d, b: (d, i, b), d=d, b=off // w))
    _, arr = spec
    return arr, pl.BlockSpec(arr.shape, functools.partial(lambda i, n: (0,) * n, n=arr.ndim))


def rows_call(name, tile_fn, T, ins, out_widths, out_dtypes=None, stacks=None):
    tr = _row_tile(T)
    n_in = len(ins)
    out_dtypes = out_dtypes or [F32] * len(out_widths)
    stacks = stacks or [(k,) for k in range(len(out_widths))]

    def body(*refs):
        vals = [r[...] for r in refs[:n_in]]
        outs = tile_fn(*vals)
        for r, members in zip(refs[n_in:], stacks):
            if len(members) == 1:
                r[...] = outs[members[0]].astype(r.dtype)
            else:
                for d, k in enumerate(members):
                    r[d] = outs[k].astype(r.dtype)

    in_specs, args = [], []
    for spec in ins:
        arr, bs = _in_spec(spec, tr)
        args.append(arr)
        in_specs.append(bs)
    out_specs, out_shape = [], []
    for w, dt, members in zip(out_widths, out_dtypes, stacks):
        n = len(members)
        if n == 1:
            out_specs.append(pl.BlockSpec((tr, w), lambda i: (i, 0)))
            out_shape.append(jax.ShapeDtypeStruct((T, w), dt))
        else:
            out_specs.append(pl.BlockSpec((n, tr, w), lambda i: (0, i, 0)))
            out_shape.append(jax.ShapeDtypeStruct((n, T, w), dt))
    return pl.pallas_call(body, out_shape=out_shape, grid=(T // tr,), in_specs=in_specs, out_specs=out_specs,
                          name=name, compiler_params=_cparams(("arbitrary",)))(*args)


def rows_vjp_call(name, tile_fn, T, ins, cts, skip=()):
    tr = _row_tile(T)
    n_in = len(ins)
    n_ct = [len(c) for c in cts]
    want = [k for k in range(n_in) if k not in skip]

    def body(*refs):
        i = pl.program_id(0)
        vals = [r[...] for r in refs[:n_in]]
        ct, pos = [], n_in
        for n in n_ct:
            acc = refs[pos][...]
            for r in refs[pos + 1:pos + n]:
                acc = acc + r[...]
            ct.append(acc)
            pos += n
        _, vjp = jax.vjp(tile_fn, *vals)
        grads = vjp(tuple(ct))
        for r, k in zip(refs[pos:], want):
            if ins[k][0] == "full":
                @pl.when(i == 0)
                def _():
                    r[...] = jnp.zeros_like(r)
                r[...] += grads[k]
            else:
                r[...] = grads[k]

    in_specs, args = [], []
    for spec in list(ins) + [s for c in cts for s in c]:
        arr, bs = _in_spec(spec, tr)
        args.append(arr)
        in_specs.append(bs)
    out_specs, out_shape = [], []
    for k in want:
        if ins[k][0] == "full":
            arr = ins[k][1]
            out_specs.append(pl.BlockSpec(arr.shape, functools.partial(lambda i, n: (0,) * n, n=arr.ndim)))
            out_shape.append(jax.ShapeDtypeStruct(arr.shape, F32))
        else:
            w = ins[k][-1]
            out_specs.append(pl.BlockSpec((tr, w), lambda i: (i, 0)))
            out_shape.append(jax.ShapeDtypeStruct((T, w), F32))
    return pl.pallas_call(body, out_shape=out_shape, grid=(T // tr,), in_specs=in_specs, out_specs=out_specs,
                          name=name, compiler_params=_cparams(("arbitrary",)))(*args)


def matmul(name, a, b, mode, add=None, out_dtype=F32):
    if mode == "tn":
        K, M = a.shape
        N = b.shape[1]
        tm = M if M <= 1536 else 512
        tn = N if N <= 1280 else (N // 2 if (N // 2) % 128 == 0 else N)
        tk = min(K, 512)
        grid = (M // tm, N // tn, K // tk)

        def body(a_ref, b_ref, o_ref):
            @pl.when(pl.program_id(2) == 0)
            def _():
                o_ref[...] = jnp.zeros_like(o_ref)
            o_ref[...] += dot_tn(a_ref[...], b_ref[...])

        return pl.pallas_call(
            body, out_shape=jax.ShapeDtypeStruct((M, N), F32), grid=grid,
            in_specs=[pl.BlockSpec((tk, tm), lambda i, j, k: (k, i)), pl.BlockSpec((tk, tn), lambda i, j, k: (k, j))],
            out_specs=pl.BlockSpec((tm, tn), lambda i, j, k: (i, j)), name=name,
            compiler_params=_cparams(("arbitrary", "arbitrary", "arbitrary")))(a, b)

    M, K = a.shape
    N = b.shape[1] if mode == "nn" else b.shape[0]
    tm = min(M, 256)
    tn = N if N <= 1536 else (N // 2 if (N // 2) % 128 == 0 else (N // 3 if (N // 3) % 128 == 0 else N))
    grid = (N // tn, M // tm)
    n_in = 2 + (add is not None)

    def body(*refs):
        a_ref, b_ref = refs[0], refs[1]
        o_ref = refs[n_in]
        acc = dot_nn(a_ref[...], b_ref[...]) if mode == "nn" else dot_nt(a_ref[...], b_ref[...])
        if add is not None:
            acc = acc + refs[2][...]
        o_ref[...] = acc.astype(o_ref.dtype)

    in_specs = [pl.BlockSpec((tm, K), lambda j, i: (i, 0)),
                pl.BlockSpec((K, tn), lambda j, i: (0, j)) if mode == "nn" else pl.BlockSpec((tn, K), lambda j, i: (j, 0))]
    args = [a, b]
    if add is not None:
        in_specs.append(pl.BlockSpec((tm, tn), lambda j, i: (i, j)))
        args.append(add)
    return pl.pallas_call(
        body, out_shape=jax.ShapeDtypeStruct((M, N), out_dtype), grid=grid, in_specs=in_specs,
        out_specs=pl.BlockSpec((tm, tn), lambda j, i: (i, j)), name=name,
        compiler_params=_cparams(("arbitrary", "arbitrary")))(*args)


def _attn_heads(n):
    G = N_Q_A // N_KV_A
    k_sl = pl.ds(n * HEAD_DIM_A, HEAD_DIM_A)
    v_sl = pl.ds(W_KV_A + n * HEAD_DIM_A, HEAD_DIM_A)
    q_sl = [pl.ds((n * G + g) * HEAD_DIM_A, HEAD_DIM_A) for g in range(G)]
    return k_sl, v_sl, q_sl, range(n * G, (n + 1) * G)


def attn_fwd(p, q_off, kvp, sink, slopes, T):
    nb = T // BLOCK
    assert q_off % W_A == 0

    def body(q_ref, kv_ref, sink_ref, slope_ref, o_ref):
        c = pl.program_id(0)
        rows = pl.ds(pl.multiple_of(c * BLOCK, BLOCK), 3 * BLOCK)
        for n in range(N_KV_A):
            k_sl, v_sl, q_sl, heads = _attn_heads(n)
            outs = attn_block([q_ref[:, s] for s in q_sl], kv_ref[rows, k_sl], kv_ref[rows, v_sl],
                              [sink_ref[h] for h in heads], [slope_ref[h] for h in heads], c, T)
            for s, o in zip(q_sl, outs):
                o_ref[:, s] = o

    full = lambda a: pl.BlockSpec(a.shape, functools.partial(lambda c, nd: (0,) * nd, nd=a.ndim))
    return pl.pallas_call(
        body, out_shape=jax.ShapeDtypeStruct((T, W_A), F32), grid=(nb,),
        in_specs=[pl.BlockSpec((BLOCK, W_A), lambda c: (c, q_off // W_A)), full(kvp), full(sink), full(slopes)],
        out_specs=pl.BlockSpec((BLOCK, W_A), lambda c: (c, 0)),
        name="attn_fwd", compiler_params=_cparams(("arbitrary",)))(p, kvp, sink, slopes)


def attn_bwd(p, q_off, kvp, sink, slopes, do, T):
    nb = T // BLOCK

    def body(q_ref, kv_ref, sink_ref, slope_ref, do_ref, dq_ref, dkv_ref, dsink_ref):
        c = pl.program_id(0)
        rows = pl.ds(pl.multiple_of(c * BLOCK, BLOCK), 3 * BLOCK)

        @pl.when(c == 0)
        def _():
            dkv_ref[...] = jnp.zeros_like(dkv_ref)
            dsink_ref[...] = jnp.zeros_like(dsink_ref)

        for n in range(N_KV_A):
            k_sl, v_sl, q_sl, heads = _attn_heads(n)
            slopes_n = [slope_ref[h] for h in heads]
            _, vjp = jax.vjp(lambda qs, kk, vv, sks: attn_block(qs, kk, vv, sks, slopes_n, c, T),
                             [q_ref[:, s] for s in q_sl], kv_ref[rows, k_sl], kv_ref[rows, v_sl],
                             [sink_ref[h] for h in heads])
            dqs, dks, dvs, dsks = vjp(tuple(do_ref[:, s] for s in q_sl))
            dkv_ref[rows, k_sl] += dks
            dkv_ref[rows, v_sl] += dvs
            for s, h, dq, dsk in zip(q_sl, heads, dqs, dsks):
                dq_ref[:, s] = dq
                dsink_ref[h] += dsk

    full = lambda a: pl.BlockSpec(a.shape, functools.partial(lambda c, nd: (0,) * nd, nd=a.ndim))
    qspec = pl.BlockSpec((BLOCK, W_A), lambda c: (c, 0))
    return pl.pallas_call(
        body,
        out_shape=[jax.ShapeDtypeStruct((T, W_A), F32), jax.ShapeDtypeStruct(kvp.shape, F32),
                   jax.ShapeDtypeStruct((N_Q_A, 1, 1), F32)],
        grid=(nb,),
        in_specs=[pl.BlockSpec((BLOCK, W_A), lambda c: (c, q_off // W_A)), full(kvp), full(sink), full(slopes), qspec],
        out_specs=[qspec, full(kvp), full(sink)],
        name="attn_bwd", compiler_params=_cparams(("arbitrary",)))(p, kvp, sink, slopes, do)


def mem_fwd(p, q_off, kv, T):
    tr = _row_tile(T)
    assert q_off % W_M == 0

    def body(q_ref, kv_ref, o_ref):
        for h in range(N_HEADS_M):
            hs = pl.ds(h * HEAD_DIM_M, HEAD_DIM_M)
            (o,) = mem_tile(q_ref[:, hs], kv_ref[:, hs], kv_ref[:, pl.ds(W_M + h * HEAD_DIM_M, HEAD_DIM_M)])
            o_ref[:, hs] = o

    return pl.pallas_call(
        body, out_shape=jax.ShapeDtypeStruct((T, W_M), F32), grid=(T // tr,),
        in_specs=[pl.BlockSpec((tr, W_M), lambda i: (i, q_off // W_M)), pl.BlockSpec((N_MEM, 2 * W_M), lambda i: (0, 0))],
        out_specs=pl.BlockSpec((tr, W_M), lambda i: (i, 0)),
        name="mem_fwd", compiler_params=_cparams(("arbitrary",)))(p, kv)


def mem_bwd(p, q_off, kv, do, T):
    tr = _row_tile(T)

    def body(q_ref, kv_ref, do_ref, dq_ref, dkv_ref):
        @pl.when(pl.program_id(0) == 0)
        def _():
            dkv_ref[...] = jnp.zeros_like(dkv_ref)

        for h in range(N_HEADS_M):
            hs = pl.ds(h * HEAD_DIM_M, HEAD_DIM_M)
            vs = pl.ds(W_M + h * HEAD_DIM_M, HEAD_DIM_M)
            _, vjp = jax.vjp(mem_tile, q_ref[:, hs], kv_ref[:, hs], kv_ref[:, vs])
            dq, dk, dv = vjp((do_ref[:, hs],))
            dq_ref[:, hs] = dq
            dkv_ref[:, hs] += dk
            dkv_ref[:, vs] += dv

    kvspec = pl.BlockSpec((N_MEM, 2 * W_M), lambda i: (0, 0))
    return pl.pallas_call(
        body,
        out_shape=[jax.ShapeDtypeStruct((T, W_M), F32), jax.ShapeDtypeStruct((N_MEM, 2 * W_M), F32)],
        grid=(T // tr,),
        in_specs=[pl.BlockSpec((tr, W_M), lambda i: (i, q_off // W_M)), kvspec, pl.BlockSpec((tr, W_M), lambda i: (i, 0))],
        out_specs=[pl.BlockSpec((tr, W_M), lambda i: (i, 0)), kvspec],
        name="mem_bwd", compiler_params=_cparams(("arbitrary",)))(p, kv, do)


def _scan_const_specs(dk):
    C, L = SCAN_CHUNK, SCAN_LEVELS
    return [pl.BlockSpec((2, (2 + L) * C, C), lambda n: (0, 0, 0)),
            pl.BlockSpec((2, C, (2 + L) * C), lambda n: (0, 0, 0)),
            pl.BlockSpec((2, L * C, dk), lambda n: (0, 0, 0)),
            pl.BlockSpec((2, L * C, dk), lambda n: (0, 0, 0)),
            pl.BlockSpec((L * C, C), lambda n: (0, 0))]


def _chunk_spec(src, width, chunk_of):
    arr, sel = src
    if arr.ndim == 2:
        assert sel % width == 0
        return pl.BlockSpec((SCAN_CHUNK, width), functools.partial(lambda n, b: (chunk_of(n), b), b=sel // width))
    return pl.BlockSpec((None, SCAN_CHUNK, width), functools.partial(lambda n, d: (d, chunk_of(n), 0), d=sel))


def _scan_const_args():
    h, ht, qm, km, bm = _scan_consts()
    return [jnp.asarray(h, BF16), jnp.asarray(ht, BF16), jnp.asarray(qm, F32), jnp.asarray(km, F32), jnp.asarray(bm, F32)]


def scan_fwd(name, q, kf, kb, gf, gb, v, heads, dk, dv, T):
    C = SCAN_CHUNK
    N = T // C
    assert dk == C
    W, Wv = heads * dk, heads * dv
    fwd = lambda n: n
    rev = lambda n: N - 1 - n

    def body(qf_ref, qb_ref, kf_ref, kb_ref, gf_ref, gb_ref, vf_ref, vb_ref, h_ref, ht_ref, qm_ref, km_ref, bm_ref,
             of_ref, ob_ref, ssf_ref, ssb_ref, st_ref):
        @pl.when(pl.program_id(0) == 0)
        def _():
            st_ref[...] = jnp.zeros_like(st_ref)

        bm = bm_ref[...]
        dirs = ((qf_ref, kf_ref, gf_ref, vf_ref, of_ref, ssf_ref), (qb_ref, kb_ref, gb_ref, vb_ref, ob_ref, ssb_ref))
        for d, (q_r, k_r, g_r, v_r, o_r, ss_r) in enumerate(dirs):
            consts = (h_ref[d], ht_ref[d], qm_ref[d], km_ref[d], bm)
            for h in range(heads):
                ks, vs = pl.ds(h * dk, dk), pl.ds(h * dv, dv)
                st = st_ref[d, h]
                ss_r[h] = st
                o, st_new = scan_chunk(q_r[:, ks], k_r[:, ks], v_r[:, vs], g_r[:, ks], st, *consts)
                o_r[:, vs] = o
                st_ref[d, h] = st_new

    srcs = [(q, fwd, W), (q, rev, W), (kf, fwd, W), (kb, rev, W), (gf, fwd, W), (gb, rev, W), (v, fwd, Wv), (v, rev, Wv)]
    ss_spec = lambda order: pl.BlockSpec((heads, None, dv, dk), lambda n: (0, order(n), 0, 0))
    return pl.pallas_call(
        body,
        out_shape=[jax.ShapeDtypeStruct((T, Wv), F32)] * 2 + [jax.ShapeDtypeStruct((heads, N, dv, dk), F32)] * 2,
        grid=(N,),
        in_specs=[_chunk_spec(s, w, order) for s, order, w in srcs] + _scan_const_specs(dk),
        out_specs=[pl.BlockSpec((C, Wv), lambda n: (fwd(n), 0)), pl.BlockSpec((C, Wv), lambda n: (rev(n), 0)),
                   ss_spec(fwd), ss_spec(rev)],
        scratch_shapes=[pltpu.VMEM((2, heads, dv, dk), F32)],
        name=name, compiler_params=_cparams(("arbitrary",)))(*[s[0] for s, _, _ in srcs], *_scan_const_args())


def scan_bwd(name, q, kf, kb, gf, gb, v, ss_f, ss_b, do, heads, dk, dv, T):
    C = SCAN_CHUNK
    N = T // C
    W, Wv = heads * dk, heads * dv
    fwd = lambda n: N - 1 - n
    rev = lambda n: n

    def body(qf_ref, qb_ref, kf_ref, kb_ref, gf_ref, gb_ref, vf_ref, vb_ref, ssf_ref, ssb_ref, dof_ref, dob_ref,
             h_ref, ht_ref, qm_ref, km_ref, bm_ref,
             dqf_ref, dkf_ref, dgf_ref, dvf_ref, dqb_ref, dkb_ref, dgb_ref, dvb_ref, dst_ref):
        @pl.when(pl.program_id(0) == 0)
        def _():
            dst_ref[...] = jnp.zeros_like(dst_ref)

        bm = bm_ref[...]
        dirs = ((qf_ref, kf_ref, gf_ref, vf_ref, ssf_ref, dof_ref, dqf_ref, dkf_ref, dgf_ref, dvf_ref),
                (qb_ref, kb_ref, gb_ref, vb_ref, ssb_ref, dob_ref, dqb_ref, dkb_ref, dgb_ref, dvb_ref))
        for d, (q_r, k_r, g_r, v_r, ss_r, do_r, dq_r, dk_r, dg_r, dv_r) in enumerate(dirs):
            consts = (h_ref[d], ht_ref[d], qm_ref[d], km_ref[d], bm)
            for h in range(heads):
                ks, vs = pl.ds(h * dk, dk), pl.ds(h * dv, dv)
                _, vjp = jax.vjp(lambda q_, k_, v_, g_, st_: scan_chunk(q_, k_, v_, g_, st_, *consts),
                                 q_r[:, ks], k_r[:, ks], v_r[:, vs], g_r[:, ks], ss_r[h])
                dq, dk_, dv_, dg, dst = vjp((do_r[:, vs], dst_ref[d, h]))
                dq_r[:, ks] = dq
                dk_r[:, ks] = dk_
                dg_r[:, ks] = dg
                dv_r[:, vs] = dv_
                dst_ref[d, h] = dst

    srcs = [(q, fwd, W), (q, rev, W), (kf, fwd, W), (kb, rev, W), (gf, fwd, W), (gb, rev, W), (v, fwd, Wv), (v, rev, Wv)]
    ss_spec = lambda order: pl.BlockSpec((heads, None, dv, dk), lambda n: (0, order(n), 0, 0))
    kspec = lambda order: pl.BlockSpec((C, W), lambda n: (order(n), 0))
    vspec = lambda order: pl.BlockSpec((C, Wv), lambda n: (order(n), 0))
    return pl.pallas_call(
        body,
        out_shape=([jax.ShapeDtypeStruct((T, W), F32)] * 3 + [jax.ShapeDtypeStruct((T, Wv), F32)]) * 2,
        grid=(N,),
        in_specs=[_chunk_spec(s, w, order) for s, order, w in srcs]
        + [ss_spec(fwd), ss_spec(rev), _chunk_spec(do, Wv, fwd), _chunk_spec(do, Wv, rev)] + _scan_const_specs(dk),
        out_specs=[kspec(fwd)] * 3 + [vspec(fwd)] + [kspec(rev)] * 3 + [vspec(rev)],
        scratch_shapes=[pltpu.VMEM((2, heads, dv, dk), F32)],
        name=name, compiler_params=_cparams(("arbitrary",)))(
            *[s[0] for s, _, _ in srcs], ss_f, ss_b, do[0], do[0], *_scan_const_args())


def final_call(x, g, target, T):
    tr = _row_tile(T)

    def tile(xv, gv, tv):
        y = _rms(xv, gv)
        err = (y - tv) ** 2
        return jnp.sum(jnp.sum(err, axis=-1, keepdims=True), axis=0, keepdims=True) * (0.5 / D_MODEL)

    def body(x_ref, g_ref, t_ref, loss_ref, dx_ref, dg_ref):
        i = pl.program_id(0)
        tv = t_ref[...]
        lv, vjp = jax.vjp(lambda a, b: tile(a, b, tv), x_ref[...], g_ref[...])
        dx, dg = vjp(jnp.ones((1, 1), F32))
        dx_ref[...] = dx

        @pl.when(i == 0)
        def _():
            loss_ref[...] = jnp.zeros_like(loss_ref)
            dg_ref[...] = jnp.zeros_like(dg_ref)

        loss_ref[...] += jnp.broadcast_to(lv, loss_ref.shape)
        dg_ref[...] += dg

    return pl.pallas_call(
        body,
        out_shape=[jax.ShapeDtypeStruct((8, 128), F32), jax.ShapeDtypeStruct((T, D_MODEL), F32),
                   jax.ShapeDtypeStruct((1, D_MODEL), F32)],
        grid=(T // tr,),
        in_specs=[pl.BlockSpec((tr, D_MODEL), lambda i: (i, 0)), pl.BlockSpec((1, D_MODEL), lambda i: (0, 0)),
                  pl.BlockSpec((tr, D_MODEL), lambda i: (i, 0))],
        out_specs=[pl.BlockSpec((8, 128), lambda i: (0, 0)), pl.BlockSpec((tr, D_MODEL), lambda i: (i, 0)),
                   pl.BlockSpec((1, D_MODEL), lambda i: (0, 0))],
        name="final_loss", compiler_params=_cparams(("arbitrary",)))(x, g, target)


def adamw_call(w, g, m, v):
    shape = w.shape
    c = shape[-1]
    r = int(np.prod(shape[:-1])) if len(shape) > 1 else 1
    tr = r if r <= 256 else 256
    assert r % tr == 0

    def body(w_ref, g_ref, m_ref, v_ref, d_ref, nm_ref, nv_ref):
        gv = g_ref[...]
        nm = ADAM_B1 * m_ref[...] + (1.0 - ADAM_B1) * gv
        nv = ADAM_B2 * v_ref[...] + (1.0 - ADAM_B2) * jnp.square(gv)
        m_hat = nm / (1.0 - ADAM_B1 ** ADAM_STEP)
        v_hat = nv / (1.0 - ADAM_B2 ** ADAM_STEP)
        d_ref[...] = -ADAM_LR * (m_hat / (jnp.sqrt(v_hat) + ADAM_EPS) + ADAM_WD * w_ref[...])
        nm_ref[...] = nm
        nv_ref[...] = nv

    spec = pl.BlockSpec((tr, c), lambda i: (i, 0))
    outs = pl.pallas_call(body, out_shape=[jax.ShapeDtypeStruct((r, c), F32)] * 3, grid=(r // tr,),
                          in_specs=[spec] * 4, out_specs=[spec] * 3, name="adamw",
                          compiler_params=_cparams(("arbitrary",)))(*(t.reshape(r, c) for t in (w, g, m, v)))
    return tuple(o.reshape(shape) for o in outs)


def adamw_halves(w, mine, other, m, v, c):
    L, R, C = w.shape
    rh = R // 2
    tr = rh if rh <= 256 else 256
    nbh = rh // tr

    def body(c_ref, w_ref, a_ref, b_ref, m_ref, v_ref, g_ref, d_ref, nm_ref, nv_ref):
        is_mine = (pl.program_id(1) // nbh) == c_ref[0]
        gv = jnp.where(is_mine, a_ref[...], b_ref[...])
        nm = ADAM_B1 * m_ref[...] + (1.0 - ADAM_B1) * gv
        nv = ADAM_B2 * v_ref[...] + (1.0 - ADAM_B2) * jnp.square(gv)
        m_hat = nm / (1.0 - ADAM_B1 ** ADAM_STEP)
        v_hat = nv / (1.0 - ADAM_B2 ** ADAM_STEP)
        g_ref[...] = gv
        d_ref[...] = -ADAM_LR * (m_hat / (jnp.sqrt(v_hat) + ADAM_EPS) + ADAM_WD * w_ref[...])
        nm_ref[...] = nm
        nv_ref[...] = nv

    full = pl.BlockSpec((None, tr, C), lambda l, i, c_ref: (l, i, 0))
    half = pl.BlockSpec((None, tr, C), lambda l, i, c_ref: (l, i % nbh, 0))
    grid_spec = pltpu.PrefetchScalarGridSpec(num_scalar_prefetch=1, grid=(L, R // tr),
                                             in_specs=[full, half, half, full, full], out_specs=[full] * 4)
    return pl.pallas_call(body, out_shape=[jax.ShapeDtypeStruct(w.shape, F32)] * 4, grid_spec=grid_spec,
                          name="adamw_halves", compiler_params=_cparams(("arbitrary", "arbitrary")))(c, w, mine, other, m, v)


def sum_devices(g64):
    def body(x_ref, o_ref):
        acc = x_ref[0:8, :]
        for d in range(1, 8):
            acc = acc + x_ref[8 * d:8 * d + 8, :]
        o_ref[...] = acc

    return pl.pallas_call(body, out_shape=jax.ShapeDtypeStruct((8, D_MODEL), F32), name="sum_devices")(g64)


def _half_tile(rh):
    return rh if rh <= 512 else 256


def add_sibling(g, recv, c, out_dtype):
    _, R, C = g.shape
    rh = R // 2
    tr = _half_tile(rh)
    nblk = rh // tr

    def body(c_ref, g_ref, r_ref, o_ref):
        o_ref[...] = (g_ref[...] + r_ref[...]).astype(o_ref.dtype)

    grid_spec = pltpu.PrefetchScalarGridSpec(
        num_scalar_prefetch=1, grid=(4, nblk),
        in_specs=[pl.BlockSpec((None, tr, C), lambda j, i, c_ref: (j, i + c_ref[0] * nblk, 0)),
                  pl.BlockSpec((None, tr, C), lambda j, i, c_ref: (j, i, 0))],
        out_specs=pl.BlockSpec((None, tr, C), lambda j, i, c_ref: (j, i, 0)))
    return pl.pallas_call(body, out_shape=jax.ShapeDtypeStruct((4, rh, C), out_dtype), grid_spec=grid_spec,
                          name="rs_add_sibling", compiler_params=_cparams(("arbitrary", "arbitrary")))(c, g, recv)


def add_chips(g, recv, r3, place):
    _, R, C = g.shape
    rh = R // 2
    tr = _half_tile(rh)
    nblk = rh // tr

    def body(p_ref, g_ref, s_ref, a_ref, b_ref, c_ref, o_ref):
        up = lambda r: r[...].astype(F32)
        o_ref[...] = (((g_ref[...] + up(s_ref)) + up(a_ref)) + up(b_ref)) + up(c_ref)

    grid_spec = pltpu.PrefetchScalarGridSpec(
        num_scalar_prefetch=1, grid=(nblk,),
        in_specs=[pl.BlockSpec((None, tr, C), lambda i, p_ref: (p_ref[0], i + p_ref[1] * nblk, 0)),
                  pl.BlockSpec((None, tr, C), lambda i, p_ref: (p_ref[0], i, 0))]
        + [pl.BlockSpec((None, tr, C), functools.partial(lambda i, p_ref, k: (k, i, 0), k=k)) for k in range(3)],
        out_specs=pl.BlockSpec((tr, C), lambda i, p_ref: (i, 0)))
    return pl.pallas_call(body, out_shape=jax.ShapeDtypeStruct((rh, C), F32), grid_spec=grid_spec,
                          name="rs_add_chips", compiler_params=_cparams(("arbitrary",)))(place, g, recv, r3, r3, r3)


def _remote(src, dst, ssem, rsem, dev):
    return pltpu.make_async_remote_copy(src_ref=src, dst_ref=dst, send_sem=ssem, recv_sem=rsem,
                                        device_id=dev, device_id_type=pl.DeviceIdType.MESH)


def _mesh_places():
    x, y, c = lax.axis_index("x"), lax.axis_index("y"), lax.axis_index("c")
    chips = [(1 - x, y), (x, 1 - y), (1 - x, 1 - y)]
    return x, y, c, (x, y, 1 - c), chips


def _hbm_specs(n):
    return [pl.BlockSpec(memory_space=pltpu.HBM) for _ in range(n)]


def _gather_body(ins, outs, n_split, send_sems, recv_sems, handshake):
    x, y, c, sibling, chips = _mesh_places()
    mine = 2 * x + y
    if handshake:
        barrier = pltpu.get_barrier_semaphore()
        peers = [sibling] + [(*chip, c) for chip in chips]
        for peer in peers:
            pl.semaphore_signal(barrier, inc=1, device_id=peer, device_id_type=pl.DeviceIdType.MESH)
        pl.semaphore_wait(barrier, len(peers))

    def half(a, chip_idx, which):
        rh = ins[a].shape[0] // 2
        return outs[a].at[chip_idx, pl.ds(which * rh, rh), :]

    sent = []
    for a in range(len(ins)):
        for k, chip in enumerate(chips):
            if a < n_split:
                rh = ins[a].shape[0] // 2
                src, dst = ins[a].at[pl.ds(c * rh, rh), :], half(a, mine, c)
            else:
                src, dst = ins[a], outs[a].at[mine]
            sent.append(_remote(src, dst, send_sems.at[a, k], recv_sems.at[a, k], (*chip, c)))
    for cp in sent:
        cp.start()
    for a in range(len(ins)):
        for k, chip in enumerate(chips):
            j = 2 * chip[0] + chip[1]
            region = half(a, j, c) if a < n_split else outs[a].at[j]
            _remote(region, region, send_sems.at[a, k], recv_sems.at[a, k], (*chip, c)).wait_recv()
            if a < n_split:
                fwd = _remote(region, region, send_sems.at[a, 3 + k], recv_sems.at[a, 3 + k], sibling)
                fwd.start()
                sent.append(fwd)
    for a in range(n_split):
        for k, chip in enumerate(chips):
            region = half(a, 2 * chip[0] + chip[1], 1 - c)
            _remote(region, region, send_sems.at[a, 3 + k], recv_sems.at[a, 3 + k], sibling).wait_recv()
    for cp in sent:
        cp.wait_send()


def gather_weights(shards, small):
    arrs = list(shards) + [small]
    n = len(arrs)

    def body(*refs):
        _gather_body(refs[:n], refs[n:2 * n], n - 1, refs[2 * n], refs[2 * n + 1], handshake=False)

    return pl.pallas_call(
        body, out_shape=[jax.ShapeDtypeStruct((4,) + a.shape, a.dtype) for a in arrs],
        in_specs=_hbm_specs(n), out_specs=_hbm_specs(n),
        scratch_shapes=[pltpu.SemaphoreType.DMA((n, 6)), pltpu.SemaphoreType.DMA((n, 6))],
        name="gather_weights")(*arrs)


def gather_weights_async(shards):
    n = len(shards)

    def body(*refs):
        _gather_body(refs[:n], refs[n:2 * n], n, refs[2 * n], refs[2 * n + 1], handshake=True)

    return pl.kernel(
        body, out_type=[jax.ShapeDtypeStruct((4,) + a.shape, a.dtype) for a in shards],
        mesh=plsc.ScalarSubcoreMesh(axis_name="seq", num_cores=1),
        scratch_types=[pltpu.SemaphoreType.DMA((n, 6)), pltpu.SemaphoreType.DMA((n, 6))],
        compiler_params=pltpu.CompilerParams(collective_id=1), name="gather_weights_async")(*shards)


def _sequencer_call(name, body, out_type, sem_shape, collective_id, args):
    return pl.kernel(
        body, out_type=out_type, mesh=plsc.ScalarSubcoreMesh(axis_name="seq", num_cores=1),
        scratch_types=[pltpu.SemaphoreType.DMA(sem_shape), pltpu.SemaphoreType.DMA(sem_shape)],
        compiler_params=pltpu.CompilerParams(collective_id=collective_id), name=name)(*args)


def _handshake(peers):
    barrier = pltpu.get_barrier_semaphore()
    for peer in peers:
        pl.semaphore_signal(barrier, inc=1, device_id=peer, device_id_type=pl.DeviceIdType.MESH)
    pl.semaphore_wait(barrier, len(peers))


def exchange_siblings(name, srcs, halves, collective_id):
    n = len(srcs)

    def body(*refs):
        ins, outs = refs[:n], refs[n:2 * n]
        send_sems, recv_sems = refs[2 * n:]
        x, y, c, sibling, chips = _mesh_places()
        _handshake([sibling])
        cps = []
        for a in range(n):
            src = ins[a]
            if halves:
                rh = src.shape[1] // 2
                src = src.at[:, pl.ds((1 - c) * rh, rh), :]
            cps.append(_remote(src, outs[a], send_sems.at[a], recv_sems.at[a], sibling))
        for cp in cps:
            cp.start()
        for cp in cps:
            cp.wait()

    shape = lambda g: (4, g.shape[1] // 2, g.shape[2]) if halves else g.shape
    return _sequencer_call(name, body, [jax.ShapeDtypeStruct(shape(g), g.dtype) for g in srcs], (n,), collective_id, srcs)


def exchange_chips(name, s1s, collective_id):
    n = len(s1s)

    def body(*refs):
        ins, outs = refs[:n], refs[n:2 * n]
        send_sems, recv_sems = refs[2 * n:]
        x, y, c, sibling, chips = _mesh_places()
        _handshake([(*chip, c) for chip in chips])
        cps = []
        for a in range(n):
            for k, chip in enumerate(chips):
                cps.append(_remote(ins[a].at[2 * chip[0] + chip[1]], outs[a].at[k], send_sems.at[a, k],
                                   recv_sems.at[a, k], (*chip, c)))
        for cp in cps:
            cp.start()
        for cp in cps:
            cp.wait()

    return _sequencer_call(name, body, [jax.ShapeDtypeStruct((3,) + s.shape[1:], s.dtype) for s in s1s], (n, 3),
                           collective_id, s1s)


def allgather_small(v):
    m_per = v.shape[0]

    def body(x_ref, out_ref, send_sems, recv_sems, local_sem):
        x, y, c, sibling, chips = _mesh_places()
        me = (x, y, c)

        def rows(px, py, pc):
            return out_ref.at[pl.ds((4 * px + 2 * py + pc) * m_per, m_per), :]

        def copy(k, block, to, src=None):
            return _remote(rows(*block) if src is None else src, rows(*block), send_sems.at[k], recv_sems.at[k], to)

        mine = pltpu.make_async_copy(x_ref, rows(*me), local_sem)
        mine.start()
        first = [copy(0, me, sibling, src=x_ref)]
        first += [copy(1 + j, me, (*chip, c), src=x_ref) for j, chip in enumerate(chips)]
        for cp in first:
            cp.start()
        passed = [copy(4 + j, (*chip, c), sibling) for j, chip in enumerate(chips)]
        for j, chip in enumerate(chips):
            copy(1 + j, (*chip, c), me).wait_recv()
            passed[j].start()
        copy(0, sibling, me).wait_recv()
        for j, chip in enumerate(chips):
            copy(4 + j, (*chip, 1 - c), me).wait_recv()
        for cp in first + passed:
            cp.wait_send()
        mine.wait()

    return pl.pallas_call(
        body, out_shape=jax.ShapeDtypeStruct((8 * m_per, v.shape[1]), v.dtype),
        in_specs=[pl.BlockSpec(memory_space=pltpu.VMEM)], out_specs=pl.BlockSpec(memory_space=pltpu.VMEM),
        scratch_shapes=[pltpu.SemaphoreType.DMA((7,)), pltpu.SemaphoreType.DMA((7,)), pltpu.SemaphoreType.DMA],
        name="allgather_small")(v)


def rms_res_tile(x, g):
    return (_rms(x, g), x)


def _lower_bounds(lb_param):
    lbs = jax.nn.softmax(lb_param.astype(F32), axis=0)
    return jnp.cumsum(lbs, axis=0) - lbs[0]


def _heads_major(t, n):
    return t.reshape(t.shape[0], n, HEAD_DIM_A).transpose(1, 0, 2)


def _heads_minor(t):
    return t.transpose(1, 0, 2).reshape(t.shape[1], t.shape[0] * t.shape[2])


def _even_fwd(x, i, W, lower, kv, slopes, T):
    O = EVEN_OFF
    g = W["norm_even"][i].reshape(1, D_MODEL)
    (h,) = rows_call("rms_fwd", rms_tile, T, [("row", x, 0, D_MODEL), ("full", g)], [D_MODEL], [BF16])
    p = matmul("mm_in_e", h, W["w_in_e"][i], "nn")
    kvp = jnp.pad(p[:, O["kA"]:O["kA"] + 2 * W_KV_A], ((BLOCK, BLOCK), (0, 0)))
    sink = W["sink"][i].reshape(N_Q_A, 1, 1)
    a = attn_fwd(p, O["qA"], kvp, sink, slopes, T)
    prep_ins = [("row", p, O["qB"], W_B), ("row", p, O["zf"], W_B), ("row", p, O["zb"], W_B),
                ("full", lower[i][0:1]), ("full", lower[i][1:2])]
    qh, k2, g2 = rows_call("hgrn_prep_fwd", hgrn_prep_tile, T, prep_ins, [W_B] * 3, stacks=[(0,), (1, 2), (3, 4)])
    scan_srcs = [(qh, 0), (k2, 0), (k2, 1), (g2, 0), (g2, 1), (p, O["iB"])]
    o_f, o_b, ss_f, ss_b = scan_fwd("scan_fwd_h", *scan_srcs, N_HEADS_B, HEAD_DIM_B, HEAD_DIM_B, T)
    mo = mem_fwd(p, O["qM"], kv, T)
    hg = W["hgrn_norm"][i].reshape(1, W_B)
    post_ins = [("row", a, 0, W_A), ("row", o_f, 0, W_B), ("row", o_b, 0, W_B), ("row", mo, 0, W_M),
                ("row", p, O["gA"], W_A), ("row", p, O["gB"], W_B), ("row", p, O["gM"], W_M), ("full", hg)]
    (mix,) = rows_call("even_post_fwd", even_post_tile, T, post_ins, [MIX], [BF16])
    x_new = matmul("mm_out", mix, W["w_out_e"][i], "nn", add=x)
    return x_new, dict(x=x, g=g, h=h, p=p, kvp=kvp, sink=sink, prep_ins=prep_ins,
                       scan_srcs=scan_srcs, ss_f=ss_f, ss_b=ss_b, post_ins=post_ins, mix=mix)


def _assemble_even(dqA, dgA, dqB, dzf, dzb, dv0, dv1, dgB, dqM, dgM, dkvA):
    return (jnp.concatenate([dqA, dgA, dqB, dzf, dzb, dv0 + dv1, dgB, dqM, dgM, dkvA], axis=-1),)


def _even_bwd(dxo, sv, i, W, kv, slopes, T):
    O = EVEN_OFF
    p = sv["p"]
    dmix = matmul("mm_dmix", dxo, W["w_out_e"][i], "nt")
    dwo = matmul("mm_dwo", sv["mix"], dxo, "tn")
    da, dof, dmo, dgA, dgB, dgM, dhg = rows_vjp_call("even_post_bwd", even_post_tile, T, sv["post_ins"],
                                                      [[("row", dmix, 0, MIX)]], skip=(2,))
    dqA, dkvp, dsink = attn_bwd(p, O["qA"], sv["kvp"], sv["sink"], slopes, da, T)
    dkvA = dkvp[BLOCK:-BLOCK]
    dqf, dkf, dgf, dvf, dqb, dkb, dgb, dvb = scan_bwd("scan_bwd_h", *sv["scan_srcs"], sv["ss_f"], sv["ss_b"], (dof, 0),
                                                      N_HEADS_B, HEAD_DIM_B, HEAD_DIM_B, T)
    row = lambda arr, w: ("row", arr, 0, w)
    dqB, dzf, dzb, dlow_f, dlow_b = rows_vjp_call(
        "hgrn_prep_bwd", hgrn_prep_tile, T, sv["prep_ins"],
        [[row(dqf, W_B), row(dqb, W_B)], [row(dkf, W_B)], [row(dkb, W_B)], [row(dgf, W_B)], [row(dgb, W_B)]])
    dlow = jnp.concatenate([dlow_f, dlow_b], axis=0)
    dqM, dkv = mem_bwd(p, O["qM"], kv, dmo, T)
    (dp,) = rows_call("even_dp", _assemble_even, T,
                      [row(dqA, W_A), row(dgA, W_A), row(dqB, W_B), row(dzf, W_B), row(dzb, W_B), row(dvf, W_B), row(dvb, W_B),
                       row(dgB, W_B), row(dqM, W_M), row(dgM, W_M), row(dkvA, 2 * W_KV_A)],
                      [EVEN_IN], [BF16])
    dh = matmul("mm_dh_e", dp, W["w_in_e"][i], "nt")
    dwi = matmul("mm_dwi_e", sv["h"], dp, "tn")
    dx, dg = rows_vjp_call("rms_res_bwd", rms_res_tile, T, [("row", sv["x"], 0, D_MODEL), ("full", sv["g"])],
                           [[("row", dh, 0, D_MODEL)], [("row", dxo, 0, D_MODEL)]])
    return dx, dict(w_in=dwi, w_out=dwo, norm=dg[0], sink=dsink.reshape(N_Q_A), low=dlow, hg=dhg[0], kv=dkv)


def _pad_gate_up(w_up):
    z = jnp.zeros((2, 128, WK_C), F32)
    z = z.at[0, 0:GATE_RANK].set(w_up[0])
    return z.at[1, GATE_RANK:2 * GATE_RANK].set(w_up[1])


def _odd_fwd(x, i, W, kv, T):
    O = ODD_OFF
    g = W["norm_odd"][i].reshape(1, D_MODEL)
    (h,) = rows_call("rms_fwd", rms_tile, T, [("row", x, 0, D_MODEL), ("full", g)], [D_MODEL], [BF16])
    p = matmul("mm_in_o", h, W["w_in_o"][i], "nn")
    wup = _pad_gate_up(W["w_gate_up"][i])
    prep_ins = [("row", p, O["qC"], WK_C), ("row", p, O["rr"], 128), ("full", wup[0]), ("full", wup[1]),
                ("full", W["b_gate"][i][0:1]), ("full", W["b_gate"][i][1:2])]
    qg, g2 = rows_call("gla_prep_fwd", gla_prep_tile, T, prep_ins, [WK_C] * 2, stacks=[(0,), (1, 2)])
    scan_srcs = [(qg, 0), (p, O["kC"]), (p, O["kC"]), (g2, 0), (g2, 1), (p, O["vC"])]
    o_f, o_b, ss_f, ss_b = scan_fwd("scan_fwd_g", *scan_srcs, N_HEADS_C, DK_C, DV_C, T)
    mo = mem_fwd(p, O["qM"], kv, T)
    gg = W["gla_norm"][i].reshape(1, WV_C)
    post_ins = [("row", o_f, 0, WV_C), ("row", o_b, 0, WV_C), ("row", mo, 0, W_M),
                ("row", p, O["gC"], WV_C), ("row", p, O["gM"], W_M), ("full", gg)]
    (mix,) = rows_call("odd_post_fwd", odd_post_tile, T, post_ins, [MIX], [BF16])
    x_new = matmul("mm_out", mix, W["w_out_o"][i], "nn", add=x)
    return x_new, dict(x=x, g=g, h=h, p=p, prep_ins=prep_ins, scan_srcs=scan_srcs, ss_f=ss_f, ss_b=ss_b,
                       post_ins=post_ins, mix=mix)


def _assemble_odd(dqC, dk0, dk1, dv0, dv1, dgC, dqM, dgM, dr):
    return (jnp.concatenate([dqC, dk0 + dk1, dv0 + dv1, dgC, dqM, dgM, dr], axis=-1),)


def _odd_bwd(dxo, sv, i, W, kv, T):
    O = ODD_OFF
    p = sv["p"]
    dmix = matmul("mm_dmix", dxo, W["w_out_o"][i], "nt")
    dwo = matmul("mm_dwo", sv["mix"], dxo, "tn")
    dof, dmo, dgC, dgM, dgg = rows_vjp_call("odd_post_bwd", odd_post_tile, T, sv["post_ins"],
                                            [[("row", dmix, 0, MIX)]], skip=(1,))
    dqf, dkf, dgf, dvf, dqb, dkb, dgb, dvb = scan_bwd("scan_bwd_g", *sv["scan_srcs"], sv["ss_f"], sv["ss_b"], (dof, 0),
                                                      N_HEADS_C, DK_C, DV_C, T)
    row = lambda arr, w: ("row", arr, 0, w)
    dqC, dr, dwup_f, dwup_b, dbg_f, dbg_b = rows_vjp_call(
        "gla_prep_bwd", gla_prep_tile, T, sv["prep_ins"],
        [[row(dqf, WK_C), row(dqb, WK_C)], [row(dgf, WK_C)], [row(dgb, WK_C)]])
    dqM, dkv = mem_bwd(p, O["qM"], kv, dmo, T)
    (dp,) = rows_call("odd_dp", _assemble_odd, T,
                      [row(dqC, WK_C), row(dkf, WK_C), row(dkb, WK_C), row(dvf, WV_C), row(dvb, WV_C),
                       row(dgC, WV_C), row(dqM, W_M), row(dgM, W_M), row(dr, 128)],
                      [ODD_PAD], [BF16])
    dh = matmul("mm_dh_o", dp, W["w_in_o"][i], "nt")
    dwi = matmul("mm_dwi_o", sv["h"], dp, "tn")
    dx, dg = rows_vjp_call("rms_res_bwd", rms_res_tile, T, [("row", sv["x"], 0, D_MODEL), ("full", sv["g"])],
                           [[("row", dh, 0, D_MODEL)], [("row", dxo, 0, D_MODEL)]])
    dw_up = jnp.stack([dwup_f[0:GATE_RANK], dwup_b[GATE_RANK:2 * GATE_RANK]])
    dbg = jnp.concatenate([dbg_f, dbg_b], axis=0)
    return dx, dict(w_in=dwi, w_out=dwo, norm=dg[0], w_up=dw_up, b_gate=dbg, gg=dgg[0], kv=dkv)


def local_step(x, mem, target, W, later=None, on_layer_grads=None):
    T = x.shape[0]
    slopes = (2.0 ** (-8.0 * jnp.arange(1, N_Q_A + 1, dtype=F32) / N_Q_A)).reshape(N_Q_A, 1, 1)
    lower, lower_vjp = jax.vjp(_lower_bounds, W["lb_param"])
    mem_g = W["mem_norm"].reshape(1, D_MODEL)
    (mem_n,) = rows_call("mem_rms_fwd", rms_tile, N_MEM, [("row", mem, 0, D_MODEL), ("full", mem_g)], [D_MODEL], [BF16])
    kvs, saved = [], []
    for l in range(DEPTH):
        if l == 1 and later is not None:
            x, W = later(x, W)
        kvs.append(matmul("mm_kv", mem_n, W["w_kv"][l], "nn"))
        if l % 2 == 0:
            x, sv = _even_fwd(x, l // 2, W, lower, kvs[l], slopes, T)
        else:
            x, sv = _odd_fwd(x, l // 2, W, kvs[l], T)
        saved.append(sv)
    loss, dx, dgf = final_call(x, W["final_norm"].reshape(1, D_MODEL), target, T)
    per = [None] * DEPTH
    dmem_n = None
    for l in reversed(range(DEPTH)):
        if l % 2 == 0:
            dx, per[l] = _even_bwd(dx, saved[l], l // 2, W, kvs[l], slopes, T)
        else:
            dx, per[l] = _odd_bwd(dx, saved[l], l // 2, W, kvs[l], T)
        per[l]["w_kv"] = matmul("mm_dwkv", mem_n, per[l]["kv"], "tn")
        dmem_n = matmul("mm_dmem", per[l]["kv"], W["w_kv"][l], "nt", add=dmem_n)
        if on_layer_grads is not None:
            dx = on_layer_grads(l, dx, per[l])
    dw_kv = [per[l]["w_kv"] for l in range(DEPTH)]
    (dmem_norm,) = rows_vjp_call("mem_rms_bwd", rms_tile, N_MEM, [("row", mem, 0, D_MODEL), ("full", mem_g)],
                                 [[("row", dmem_n, 0, D_MODEL)]], skip=(0,))
    ev, od = (per[0], per[2]), (per[1], per[3])
    (d_lb,) = lower_vjp(jnp.stack([e["low"] for e in ev]))
    grads = dict(
        w_in_e=jnp.stack([e["w_in"] for e in ev]), w_in_o=jnp.stack([o["w_in"] for o in od]),
        w_out_e=jnp.stack([e["w_out"] for e in ev]), w_out_o=jnp.stack([o["w_out"] for o in od]),
        w_kv=jnp.stack(dw_kv), norm_even=jnp.stack([e["norm"] for e in ev]), sink=jnp.stack([e["sink"] for e in ev]),
        lb_param=d_lb, hgrn_norm=jnp.stack([e["hg"] for e in ev]), norm_odd=jnp.stack([o["norm"] for o in od]),
        w_gate_up=jnp.stack([o["w_up"] for o in od]), b_gate=jnp.stack([o["b_gate"] for o in od]),
        gla_norm=jnp.stack([o["gg"] for o in od]), mem_norm=dmem_norm[0], final_norm=dgf[0])
    return loss, dx, grads


SMALL_SPECS = (("lb_param", (2, 2, 128)), ("norm_odd", (2, 256)), ("w_gate_up", (2, 2, 16, 128)),
               ("b_gate", (2, 2, 128)), ("gla_norm", (2, 256)))
SMALL_ROWS = 80


def _pack_small_local(d):
    return jnp.concatenate([d[n].reshape(-1) for n, _ in SMALL_SPECS]).reshape(SMALL_ROWS, 128)


def _unpack_small_local(b):
    flat, out, o = b.reshape(-1), {}, 0
    for n, shp in SMALL_SPECS:
        sz = int(np.prod(shp))
        out[n] = flat[o:o + sz].reshape(shp)
        o += sz
    return out


def _unpack_small_full(g4):
    per = [_unpack_small_local(g4[j]) for j in range(4)]
    return {n: jnp.concatenate([per[j][n] for j in range(4)], axis=-1) for n, _ in SMALL_SPECS}


def _pack_small_blocks(full):
    blocks = []
    for j in range(4):
        blocks.append(_pack_small_local({n: full[n][..., j * shp[-1]:(j + 1) * shp[-1]] for n, shp in SMALL_SPECS}))
    return jnp.stack(blocks)


def _cols(t, order, off, widths):
    return [t[..., off[n]:off[n] + widths[n]] for n in order]


EVEN_REF_ORDER = ("qA", "kA", "vA", "gA", "qB", "zf", "zb", "iB", "gB", "qM", "gM")
ODD_REF_ORDER = ("qC", "kC", "vC", "gC", "rr", "qM", "gM")


def _layer_weights(l, g_in, g_out, g_kv):
    t = g_in.transpose(1, 0, 2).reshape(D_MODEL, -1)
    if l % 2 == 0:
        w_in = jnp.concatenate(_cols(t, EVEN_ORDER, EVEN_REF_OFF, EVEN_W), axis=-1)
    else:
        w_in = jnp.concatenate(_cols(t, ODD_ORDER, ODD_REF_OFF, ODD_W) + [jnp.zeros((D_MODEL, ODD_PAD - ODD_IN), BF16)],
                               axis=-1)
    return w_in, g_out.reshape(MIX, D_MODEL), g_kv.reshape(D_MODEL, 2 * W_M)


def _layer_grad_blocks(l, gl):
    if l % 2 == 0:
        t = jnp.concatenate(_cols(gl["w_in"], EVEN_REF_ORDER, EVEN_OFF, EVEN_W), axis=-1)
    else:
        t = jnp.concatenate(_cols(gl["w_in"], ODD_REF_ORDER, ODD_OFF, ODD_W), axis=-1)
    b_in = t.reshape(D_MODEL, 4, -1).transpose(1, 0, 2)
    return [b_in, gl["w_out"].reshape(4, MIX // 4, D_MODEL), gl["w_kv"].reshape(4, D_MODEL // 4, 2 * W_M)]


WEIGHT_NAMES = ("norm_even", "w_in_even", "sink", "lb_param", "hgrn_norm", "w_out_even", "norm_odd", "w_in_odd",
                "w_gate_up", "b_gate", "gla_norm", "w_out_odd", "mem_norm", "w_mem_kv", "final_norm")


def kernel(x, mem, norm_even, w_in_even, sink, lb_param, hgrn_norm, w_out_even, norm_odd, w_in_odd, w_gate_up, b_gate, gla_norm, w_out_odd, mem_norm, w_mem_kv, final_norm, loss_target, m_norm_even, m_w_in_even, m_sink, m_lb_param, m_hgrn_norm, m_w_out_even, m_norm_odd, m_w_in_odd, m_w_gate_up, m_b_gate, m_gla_norm, m_w_out_odd, m_mem_norm, m_w_mem_kv, m_final_norm, v_norm_even, v_w_in_even, v_sink, v_lb_param, v_hgrn_norm, v_w_out_even, v_norm_odd, v_w_in_odd, v_w_gate_up, v_b_gate, v_gla_norm, v_w_out_odd, v_mem_norm, v_w_mem_kv, v_final_norm):
    w = dict(zip(WEIGHT_NAMES, (norm_even, w_in_even, sink, lb_param, hgrn_norm, w_out_even, norm_odd, w_in_odd,
                                w_gate_up, b_gate, gla_norm, w_out_odd, mem_norm, w_mem_kv, final_norm)))
    m = dict(zip(WEIGHT_NAMES, (m_norm_even, m_w_in_even, m_sink, m_lb_param, m_hgrn_norm, m_w_out_even, m_norm_odd,
                                m_w_in_odd, m_w_gate_up, m_b_gate, m_gla_norm, m_w_out_odd, m_mem_norm, m_w_mem_kv,
                                m_final_norm)))
    v = dict(zip(WEIGHT_NAMES, (v_norm_even, v_w_in_even, v_sink, v_lb_param, v_hgrn_norm, v_w_out_even, v_norm_odd,
                                v_w_in_odd, v_w_gate_up, v_b_gate, v_gla_norm, v_w_out_odd, v_mem_norm, v_w_mem_kv,
                                v_final_norm)))
    ci = lax.axis_index("c").astype(jnp.int32).reshape(1)
    chip = (2 * lax.axis_index("x") + lax.axis_index("y")).astype(jnp.int32).reshape(1)

    shards = []
    for l in range(DEPTH):
        names = ("w_in_even", "w_out_even") if l % 2 == 0 else ("w_in_odd", "w_out_odd")
        shards.append([w[names[0]][l // 2].astype(BF16), w[names[1]][l // 2].astype(BF16), w_mem_kv[l].astype(BF16)])
    small = _pack_small_local(w)
    own = lambda g, s: lax.dynamic_update_slice(g, s[None], (chip[0], 0, 0))
    first = [own(g, s) for g, s in zip(gather_weights(shards[0], small), shards[0] + [small])]
    later_shards = shards[1] + shards[2] + shards[3]
    later_raw = gather_weights_async(later_shards)
    w0 = _layer_weights(0, *first[0:3])
    W = dict(w_in_e=[w0[0]], w_out_e=[w0[1]], w_kv=[w0[2]])
    W.update(_unpack_small_full(first[3]))
    W.update({n: w[n] for n in ("norm_even", "sink", "hgrn_norm", "mem_norm", "final_norm")})

    def later(x1, W):
        x1, raw = lax.optimization_barrier((x1, list(later_raw)))
        g = [own(a, s) for a, s in zip(raw, later_shards)]
        w1, w2, w3 = (_layer_weights(l, *g[3 * (l - 1):3 * l]) for l in (1, 2, 3))
        W = dict(W)
        W.update(w_in_e=[w0[0], w2[0]], w_in_o=[w1[0], w3[0]], w_out_e=[w0[1], w2[1]], w_out_o=[w1[1], w3[1]],
                 w_kv=[w0[2], w1[2], w2[2], w3[2]])
        return x1, W

    place = jnp.concatenate([chip, ci])

    def start(tag, blocks, wire):
        return dict(tag=tag, blocks=blocks, wire=wire, step=0,
                    recv=exchange_siblings(f"rs_siblings_{tag}", blocks, True, 2))

    def advance(p):
        if p["step"] == 0:
            sums = [add_sibling(g, r, ci, dt) for g, r, dt in zip(p["blocks"], p["recv"], p["wire"])]
            p["recv3"] = exchange_chips(f"rs_chips_{p['tag']}", sums, 3)
        else:
            p["mine"] = [add_chips(g, r, r3, place) for g, r, r3 in zip(p["blocks"], p["recv"], p["recv3"])]
            p["other"] = exchange_siblings(f"rs_final_{p['tag']}", p["mine"], False, 4)
        p["step"] += 1

    pipes, first_layer = [], {}

    def on_layer_grads(l, dx, gl):
        for p in pipes:
            if p["step"] < 2:
                key = "recv" if p["step"] == 0 else "recv3"
                dx, arrived = lax.optimization_barrier((dx, list(p[key])))
                p[key] = arrived
                advance(p)
        if l == 0:
            first_layer.update(gl)
        else:
            pipes.append(start(f"l{l}", _layer_grad_blocks(l, gl), [BF16] * 3))
        return dx

    loss_tile, dx, grads = local_step(x[0], mem[0], loss_target[0], W, later, on_layer_grads)
    pipes.append(start("l0", _layer_grad_blocks(0, first_layer) + [_pack_small_blocks(grads)], [BF16] * 3 + [F32]))
    while any(p["step"] < 2 for p in pipes):
        for p in pipes:
            if p["step"] < 2:
                advance(p)
    by_layer = {int(p["tag"][1:]): p for p in pipes}
    halves = lambda layers, k: (jnp.stack([by_layer[l]["mine"][k] for l in layers]),
                                jnp.stack([by_layer[l]["other"][k] for l in layers]))
    big = dict(w_in_even=halves((0, 2), 0), w_in_odd=halves((1, 3), 0), w_out_even=halves((0, 2), 1),
               w_out_odd=halves((1, 3), 1), w_mem_kv=halves((0, 1, 2, 3), 2))
    s_mine, s_other = by_layer[0]["mine"][3], by_layer[0]["other"][3]
    g_small = jnp.where(ci[0] == 0, jnp.concatenate([s_mine, s_other]), jnp.concatenate([s_other, s_mine]))
    gl = _unpack_small_local(g_small)

    pack = jnp.zeros((8, D_MODEL), F32)
    pack = pack.at[0:2].set(grads["norm_even"]).at[2].set(grads["hgrn_norm"].reshape(-1))
    pack = pack.at[3].set(grads["mem_norm"]).at[4].set(grads["final_norm"])
    pack = pack.at[5, 0:16].set(grads["sink"].reshape(-1)).at[5, 16].set(loss_tile[0, 0])
    tot = sum_devices(allgather_small(pack))
    gl.update(norm_even=tot[0:2], hgrn_norm=tot[2].reshape(2, W_B), mem_norm=tot[3], final_norm=tot[4],
              sink=tot[5, 0:16].reshape(2, N_Q_A))
    loss = tot[5, 16]

    upd = {}
    for n in WEIGHT_NAMES:
        if n in big:
            gl[n], *upd[n] = adamw_halves(w[n], *big[n], m[n], v[n], ci)
        else:
            upd[n] = adamw_call(w[n], gl[n], m[n], v[n])
    return (loss, dx[None], *[gl[n] for n in WEIGHT_NAMES], *[upd[n][0] for n in WEIGHT_NAMES],
            *[upd[n][1] for n in WEIGHT_NAMES], *[upd[n][2] for n in WEIGHT_NAMES])
```

```python
import functools

import numpy as np
import jax
import jax.numpy as jnp
from jax import lax
from jax.experimental import pallas as pl
from jax.experimental.pallas import tpu as pltpu
from jax.experimental.pallas import tpu_sc as plsc

F32 = jnp.float32
BF16 = jnp.bfloat16

D_MODEL = 1024
DEPTH = 4
N_Q_A, N_KV_A, HEAD_DIM_A = 8, 2, 64
W_A, W_KV_A = 512, 128
WINDOW = 128
BLOCK = 128
N_HEADS_B, HEAD_DIM_B, W_B = 4, 128, 512
N_HEADS_C, DK_C, DV_C, WK_C, WV_C = 4, 128, 256, 512, 1024
GATE_RANK = 16
GATE_TEMP = 16.0
N_MEM, N_HEADS_M, HEAD_DIM_M, W_M = 256, 4, 128, 512
EPS = 1e-6
MASK_VALUE = -1e30
MIN_GATE = 1e-30
EVEN_IN, ODD_IN = 4864, 4128
ODD_PAD = 4224
MIX = 1536
ADAM_LR, ADAM_B1, ADAM_B2, ADAM_EPS, ADAM_WD, ADAM_STEP = 0.001, 0.9, 0.999, 1e-08, 0.01, 10

SCAN_CHUNK = 128
SCAN_LEVELS = 7
VMEM_LIMIT = 56 * 1024 * 1024

EVEN_REF_OFF = dict(qA=0, kA=512, vA=640, gA=768, qB=1280, zf=1792, zb=2304, iB=2816, gB=3328, qM=3840, gM=4352)
EVEN_W = dict(qA=512, kA=128, vA=128, gA=512, qB=512, zf=512, zb=512, iB=512, gB=512, qM=512, gM=512)
EVEN_ORDER = ("qA", "gA", "qB", "zf", "zb", "iB", "gB", "qM", "gM", "kA", "vA")
ODD_REF_OFF = dict(qC=0, kC=512, vC=1024, gC=2048, rr=3072, qM=3104, gM=3616)
ODD_W = dict(qC=512, kC=512, vC=1024, gC=1024, rr=32, qM=512, gM=512)
ODD_ORDER = ("qC", "kC", "vC", "gC", "qM", "gM", "rr")


def _offsets(order, widths):
    off, o = {}, 0
    for n in order:
        off[n] = o
        o += widths[n]
    return off


EVEN_OFF = _offsets(EVEN_ORDER, EVEN_W)
ODD_OFF = _offsets(ODD_ORDER, ODD_W)


def _dg(a, b, ca, cb):
    return lax.dot_general(a.astype(BF16), b.astype(BF16), (((ca,), (cb,)), ((), ())),
                           preferred_element_type=F32)


def dot_nn(a, b):
    return _dg(a, b, 1, 0)


def dot_nt(a, b):
    return _dg(a, b, 1, 1)


def dot_tn(a, b):
    return _dg(a, b, 0, 0)


@jax.custom_vjp
def bdot(a, b):
    return dot_nn(a, b)


bdot.defvjp(lambda a, b: (dot_nn(a, b), (a, b)),
            lambda r, g: (dot_nt(g, r[1]), dot_tn(r[0], g)))


@jax.custom_vjp
def bdot_t(a, b):
    return dot_nt(a, b)


bdot_t.defvjp(lambda a, b: (dot_nt(a, b), (a, b)),
              lambda r, g: (dot_nn(g, r[1]), dot_tn(g, r[0])))


@jax.custom_vjp
def bdot_tn(a, b):
    return dot_tn(a, b)


bdot_tn.defvjp(lambda a, b: (dot_tn(a, b), (a, b)),
               lambda r, g: (dot_nt(r[1], g), dot_nn(r[0], g)))


def _split_mm(h, x):
    hi = x.astype(BF16)
    lo = (x - hi.astype(F32)).astype(BF16)
    return (lax.dot_general(h, hi, (((1,), (0,)), ((), ())), preferred_element_type=F32)
            + lax.dot_general(h, lo, (((1,), (0,)), ((), ())), preferred_element_type=F32))


@jax.custom_vjp
def hdot(h, ht, x):
    return _split_mm(h, x)


hdot.defvjp(lambda h, ht, x: (_split_mm(h, x), (h, ht)),
            lambda r, g: (jnp.zeros_like(r[0]), jnp.zeros_like(r[1]), _split_mm(r[1], g)))


def _sigmoid(z):
    return 1.0 / (1.0 + jnp.exp(-z))


def _silu(z):
    return z * _sigmoid(z)


def _log_sigmoid(z):
    return jnp.minimum(z, 0.0) - jnp.log(1.0 + jnp.exp(-jnp.abs(z)))


def _rms(x, g):
    return x * lax.rsqrt(jnp.mean(x * x, axis=-1, keepdims=True) + EPS) * g


def rms_tile(x, g):
    return (_rms(x, g),)


@functools.partial(jax.custom_vjp, nondiff_argnums=(1, 2))
def split(x, n, axis):
    w = x.shape[axis] // n
    return tuple(lax.slice_in_dim(x, h * w, (h + 1) * w, axis=axis) for h in range(n))


split.defvjp(lambda x, n, axis: (split(x, n, axis), None),
             lambda n, axis, _, cts: (jnp.concatenate(cts, axis=axis),))


def _group_rms(o, g, heads):
    return jnp.concatenate([_rms(oh, gh) for oh, gh in zip(split(o, heads, 1), split(g, heads, 1))], axis=-1)


def even_post_tile(a, o2f, o2b, mo, gA, gB, gM, hg):
    y = _group_rms(o2f + o2b, hg, N_HEADS_B)
    return (jnp.concatenate([a * _silu(gA), y * _silu(gB), mo * _silu(gM)], axis=-1),)


def odd_post_tile(o2f, o2b, mo, gC, gM, gg):
    y = _group_rms(o2f + o2b, gg, N_HEADS_C)
    return (jnp.concatenate([y * _silu(gC), mo * _silu(gM)], axis=-1),)


def hgrn_prep_tile(qB, zf, zb, low_f, low_b):
    ks, gs = [], []
    for z, lb in ((zf, low_f), (zb, low_b)):
        f = lb + (1.0 - lb) * _sigmoid(z)
        gs.append(jnp.log(jnp.maximum(f, MIN_GATE)))
        ks.append((1.0 - lb) * _sigmoid(-z))
    return (_silu(qB), ks[0], ks[1], gs[0], gs[1])


def gla_prep_tile(qC, r128, wup_f, wup_b, bg_f, bg_b):
    gs = [_log_sigmoid(bdot(r128, wup) + bg) / GATE_TEMP for wup, bg in ((wup_f, bg_f), (wup_b, bg_b))]
    return (qC * (DK_C ** -0.5), gs[0], gs[1])


def mem_tile(q, k, v):
    s = bdot_t(q, k) * (HEAD_DIM_M ** -0.5)
    m = lax.stop_gradient(jnp.max(s, axis=-1, keepdims=True))
    p = jnp.exp(s - m)
    p = p / jnp.sum(p, axis=-1, keepdims=True)
    return (bdot(p, v),)


def attn_block(qs, ks, vs, sinks, slopes, c, seq):
    i = lax.broadcasted_iota(jnp.int32, (BLOCK, 3 * BLOCK), 0)
    j = lax.broadcasted_iota(jnp.int32, (BLOCK, 3 * BLOCK), 1)
    dist = jnp.abs(i - j + BLOCK).astype(F32)
    kpos = (c - 1) * BLOCK + j
    valid = (dist <= WINDOW) & (kpos >= 0) & (kpos < seq)
    outs = []
    for q, sk, slope in zip(qs, sinks, slopes):
        s = bdot_t(q, ks) * (HEAD_DIM_A ** -0.5)
        s = jnp.where(valid, s - slope * dist, MASK_VALUE)
        m = lax.stop_gradient(jnp.maximum(jnp.max(s, axis=-1, keepdims=True), sk))
        p = jnp.where(valid, jnp.exp(s - m), 0.0)
        denom = jnp.sum(p, axis=-1, keepdims=True) + jnp.exp(sk - m)
        outs.append(bdot(p, vs) / denom)
    return tuple(outs)


def scan_chunk(q, k, v, e, tot, st, qm, km, bm):
    C = SCAN_CHUNK
    e = split(e, 2 + SCAN_LEVELS, 0)
    qe = q * jnp.exp(e[0])
    kd = k * jnp.exp(e[1])
    r = lax.broadcasted_iota(jnp.int32, (C, C), 0)
    s = lax.broadcasted_iota(jnp.int32, (C, C), 1)
    a = jnp.where(r == s, jnp.sum(q * k, axis=-1, keepdims=True), 0.0)
    for l in range(SCAN_LEVELS):
        el = jnp.exp(e[2 + l])
        qs = q * el * qm[l * C:(l + 1) * C]
        ks = k * el * km[l * C:(l + 1) * C]
        a = a + bdot_t(qs, ks) * bm[l * C:(l + 1) * C]
    o = bdot_t(qe, st) + bdot(a, v)
    st_new = st * jnp.exp(tot) + bdot_tn(v, kd)
    return o, st_new


def _scan_consts():
    C, L = SCAN_CHUNK, SCAN_LEVELS
    t = np.arange(C)[:, None]
    r = np.arange(C)[None, :]
    blocks = [(r <= t), (r > t)]
    qms, kms, bms = [], [], []
    for l in range(1, L + 1):
        m = C >> l
        upper_t = (t % (2 * m)) >= m
        same_half = (t // m) == (r // m)
        blocks.append(same_half & np.where(upper_t, r <= t, r > t))
        qms.append(np.broadcast_to(upper_t, (C, C)))
        kms.append(np.broadcast_to(~upper_t, (C, C)))
        bms.append((t // (2 * m)) == (r // (2 * m)))
    hf = np.concatenate(blocks, axis=0).astype(np.float32)
    flip = lambda mat: mat.reshape(-1, C, C)[:, ::-1, ::-1].reshape(-1, C)
    hb = flip(hf)
    qmf = np.concatenate(qms, axis=0).astype(np.float32)
    kmf = np.concatenate(kms, axis=0).astype(np.float32)
    bm = np.concatenate(bms, axis=0).astype(np.float32)
    h = np.stack([hf, hb])
    ht = np.stack([hf.T, hb.T])
    qm = np.stack([qmf, kmf])
    km = np.stack([kmf, qmf])
    return h, ht, qm, km, bm


def _cparams(sem):
    return pltpu.CompilerParams(dimension_semantics=sem, vmem_limit_bytes=VMEM_LIMIT)


def _row_tile(T):
    return min(T, 256)


def _in_spec(spec, tr):
    kind = spec[0]
    if kind == "row":
        _, arr, off, w = spec
        assert off % w == 0
        return arr, pl.BlockSpec((tr, w), functools.partial(lambda i, b: (i, b), b=off // w))
    if kind == "row3":
        _, arr, d, off, w = spec
        assert off % w == 0
        return arr, pl.BlockSpec((None, tr, w), functools.partial(lambda i, d, b: (d, i, b), d=d, b=off // w))
    _, arr = spec
    return arr, pl.BlockSpec(arr.shape, functools.partial(lambda i, n: (0,) * n, n=arr.ndim))


def rows_call(name, tile_fn, T, ins, out_widths, out_dtypes=None, stacks=None):
    tr = _row_tile(T)
    n_in = len(ins)
    out_dtypes = out_dtypes or [F32] * len(out_widths)
    stacks = stacks or [(k,) for k in range(len(out_widths))]

    def body(*refs):
        vals = [r[...] for r in refs[:n_in]]
        outs = tile_fn(*vals)
        for r, members in zip(refs[n_in:], stacks):
            if len(members) == 1:
                r[...] = outs[members[0]].astype(r.dtype)
            else:
                for d, k in enumerate(members):
                    r[d] = outs[k].astype(r.dtype)

    in_specs, args = [], []
    for spec in ins:
        arr, bs = _in_spec(spec, tr)
        args.append(arr)
        in_specs.append(bs)
    out_specs, out_shape = [], []
    for w, dt, members in zip(out_widths, out_dtypes, stacks):
        n = len(members)
        if n == 1:
            out_specs.append(pl.BlockSpec((tr, w), lambda i: (i, 0)))
            out_shape.append(jax.ShapeDtypeStruct((T, w), dt))
        else:
            out_specs.append(pl.BlockSpec((n, tr, w), lambda i: (0, i, 0)))
            out_shape.append(jax.ShapeDtypeStruct((n, T, w), dt))
    return pl.pallas_call(body, out_shape=out_shape, grid=(T // tr,), in_specs=in_specs, out_specs=out_specs,
                          name=name, compiler_params=_cparams(("arbitrary",)))(*args)


def rows_vjp_call(name, tile_fn, T, ins, cts, skip=()):
    tr = _row_tile(T)
    n_in = len(ins)
    n_ct = [len(c) for c in cts]
    want = [k for k in range(n_in) if k not in skip]

    def body(*refs):
        i = pl.program_id(0)
        vals = [r[...] for r in refs[:n_in]]
        ct, pos = [], n_in
        for n in n_ct:
            acc = refs[pos][...]
            for r in refs[pos + 1:pos + n]:
                acc = acc + r[...]
            ct.append(acc)
            pos += n
        _, vjp = jax.vjp(tile_fn, *vals)
        grads = vjp(tuple(ct))
        for r, k in zip(refs[pos:], want):
            if ins[k][0] == "full":
                @pl.when(i == 0)
                def _():
                    r[...] = jnp.zeros_like(r)
                r[...] += grads[k]
            else:
                r[...] = grads[k]

    in_specs, args = [], []
    for spec in list(ins) + [s for c in cts for s in c]:
        arr, bs = _in_spec(spec, tr)
        args.append(arr)
        in_specs.append(bs)
    out_specs, out_shape = [], []
    for k in want:
        if ins[k][0] == "full":
            arr = ins[k][1]
            out_specs.append(pl.BlockSpec(arr.shape, functools.partial(lambda i, n: (0,) * n, n=arr.ndim)))
            out_shape.append(jax.ShapeDtypeStruct(arr.shape, F32))
        else:
            w = ins[k][-1]
            out_specs.append(pl.BlockSpec((tr, w), lambda i: (i, 0)))
            out_shape.append(jax.ShapeDtypeStruct((T, w), F32))
    return pl.pallas_call(body, out_shape=out_shape, grid=(T // tr,), in_specs=in_specs, out_specs=out_specs,
                          name=name, compiler_params=_cparams(("arbitrary",)))(*args)


def matmul(name, a, b, mode, add=None, out_dtype=F32):
    if mode == "tn":
        K, M = a.shape
        N = b.shape[1]
        tm = M if M <= 1536 else 512
        tn = N if N <= 1280 else (N // 2 if (N // 2) % 128 == 0 else N)
        tk = min(K, 512)
        grid = (M // tm, N // tn, K // tk)

        def body(a_ref, b_ref, o_ref):
            @pl.when(pl.program_id(2) == 0)
            def _():
                o_ref[...] = jnp.zeros_like(o_ref)
            o_ref[...] += dot_tn(a_ref[...], b_ref[...])

        return pl.pallas_call(
            body, out_shape=jax.ShapeDtypeStruct((M, N), F32), grid=grid,
            in_specs=[pl.BlockSpec((tk, tm), lambda i, j, k: (k, i)), pl.BlockSpec((tk, tn), lambda i, j, k: (k, j))],
            out_specs=pl.BlockSpec((tm, tn), lambda i, j, k: (i, j)), name=name,
            compiler_params=_cparams(("arbitrary", "arbitrary", "arbitrary")))(a, b)

    M, K = a.shape
    N = b.shape[1] if mode == "nn" else b.shape[0]
    tm = min(M, 512)
    tn = N if N <= 1536 else (N // 2 if (N // 2) % 128 == 0 else (N // 3 if (N // 3) % 128 == 0 else N))
    grid = (N // tn, M // tm)
    n_in = 2 + (add is not None)

    def body(*refs):
        a_ref, b_ref = refs[0], refs[1]
        o_ref = refs[n_in]
        acc = dot_nn(a_ref[...], b_ref[...]) if mode == "nn" else dot_nt(a_ref[...], b_ref[...])
        if add is not None:
            acc = acc + refs[2][...]
        o_ref[...] = acc.astype(o_ref.dtype)

    in_specs = [pl.BlockSpec((tm, K), lambda j, i: (i, 0)),
                pl.BlockSpec((K, tn), lambda j, i: (0, j)) if mode == "nn" else pl.BlockSpec((tn, K), lambda j, i: (j, 0))]
    args = [a, b]
    if add is not None:
        in_specs.append(pl.BlockSpec((tm, tn), lambda j, i: (i, j)))
        args.append(add)
    return pl.pallas_call(
        body, out_shape=jax.ShapeDtypeStruct((M, N), out_dtype), grid=grid, in_specs=in_specs,
        out_specs=pl.BlockSpec((tm, tn), lambda j, i: (i, j)), name=name,
        compiler_params=_cparams(("arbitrary", "arbitrary")))(*args)


def _attn_heads(n):
    G = N_Q_A // N_KV_A
    k_sl = pl.ds(n * HEAD_DIM_A, HEAD_DIM_A)
    v_sl = pl.ds(W_KV_A + n * HEAD_DIM_A, HEAD_DIM_A)
    q_sl = [pl.ds((n * G + g) * HEAD_DIM_A, HEAD_DIM_A) for g in range(G)]
    return k_sl, v_sl, q_sl, range(n * G, (n + 1) * G)


def attn_fwd(p, q_off, kvp, sink, slopes, T):
    nb = T // BLOCK
    assert q_off % W_A == 0

    def body(q_ref, kv_ref, sink_ref, slope_ref, o_ref):
        c = pl.program_id(0)
        rows = pl.ds(pl.multiple_of(c * BLOCK, BLOCK), 3 * BLOCK)
        for n in range(N_KV_A):
            k_sl, v_sl, q_sl, heads = _attn_heads(n)
            outs = attn_block([q_ref[:, s] for s in q_sl], kv_ref[rows, k_sl], kv_ref[rows, v_sl],
                              [sink_ref[h] for h in heads], [slope_ref[h] for h in heads], c, T)
            for s, o in zip(q_sl, outs):
                o_ref[:, s] = o

    full = lambda a: pl.BlockSpec(a.shape, functools.partial(lambda c, nd: (0,) * nd, nd=a.ndim))
    return pl.pallas_call(
        body, out_shape=jax.ShapeDtypeStruct((T, W_A), F32), grid=(nb,),
        in_specs=[pl.BlockSpec((BLOCK, W_A), lambda c: (c, q_off // W_A)), full(kvp), full(sink), full(slopes)],
        out_specs=pl.BlockSpec((BLOCK, W_A), lambda c: (c, 0)),
        name="attn_fwd", compiler_params=_cparams(("arbitrary",)))(p, kvp, sink, slopes)


def attn_bwd(p, q_off, kvp, sink, slopes, do, T):
    nb = T // BLOCK

    def body(q_ref, kv_ref, sink_ref, slope_ref, do_ref, dq_ref, dkv_ref, dsink_ref):
        c = pl.program_id(0)
        rows = pl.ds(pl.multiple_of(c * BLOCK, BLOCK), 3 * BLOCK)

        @pl.when(c == 0)
        def _():
            dkv_ref[...] = jnp.zeros_like(dkv_ref)
            dsink_ref[...] = jnp.zeros_like(dsink_ref)

        for n in range(N_KV_A):
            k_sl, v_sl, q_sl, heads = _attn_heads(n)
            slopes_n = [slope_ref[h] for h in heads]
            _, vjp = jax.vjp(lambda qs, kk, vv, sks: attn_block(qs, kk, vv, sks, slopes_n, c, T),
                             [q_ref[:, s] for s in q_sl], kv_ref[rows, k_sl], kv_ref[rows, v_sl],
                             [sink_ref[h] for h in heads])
            dqs, dks, dvs, dsks = vjp(tuple(do_ref[:, s] for s in q_sl))
            dkv_ref[rows, k_sl] += dks
            dkv_ref[rows, v_sl] += dvs
            for s, h, dq, dsk in zip(q_sl, heads, dqs, dsks):
                dq_ref[:, s] = dq
                dsink_ref[h] += dsk

    full = lambda a: pl.BlockSpec(a.shape, functools.partial(lambda c, nd: (0,) * nd, nd=a.ndim))
    qspec = pl.BlockSpec((BLOCK, W_A), lambda c: (c, 0))
    return pl.pallas_call(
        body,
        out_shape=[jax.ShapeDtypeStruct((T, W_A), F32), jax.ShapeDtypeStruct(kvp.shape, F32),
                   jax.ShapeDtypeStruct((N_Q_A, 1, 1), F32)],
        grid=(nb,),
        in_specs=[pl.BlockSpec((BLOCK, W_A), lambda c: (c, q_off // W_A)), full(kvp), full(sink), full(slopes), qspec],
        out_specs=[qspec, full(kvp), full(sink)],
        name="attn_bwd", compiler_params=_cparams(("arbitrary",)))(p, kvp, sink, slopes, do)


def mem_fwd(p, q_off, kv, T):
    tr = _row_tile(T)
    assert q_off % W_M == 0

    def body(q_ref, kv_ref, o_ref):
        for h in range(N_HEADS_M):
            hs = pl.ds(h * HEAD_DIM_M, HEAD_DIM_M)
            (o,) = mem_tile(q_ref[:, hs], kv_ref[:, hs], kv_ref[:, pl.ds(W_M + h * HEAD_DIM_M, HEAD_DIM_M)])
            o_ref[:, hs] = o

    return pl.pallas_call(
        body, out_shape=jax.ShapeDtypeStruct((T, W_M), F32), grid=(T // tr,),
        in_specs=[pl.BlockSpec((tr, W_M), lambda i: (i, q_off // W_M)), pl.BlockSpec((N_MEM, 2 * W_M), lambda i: (0, 0))],
        out_specs=pl.BlockSpec((tr, W_M), lambda i: (i, 0)),
        name="mem_fwd", compiler_params=_cparams(("arbitrary",)))(p, kv)


def mem_bwd(p, q_off, kv, do, T):
    tr = _row_tile(T)

    def body(q_ref, kv_ref, do_ref, dq_ref, dkv_ref):
        @pl.when(pl.program_id(0) == 0)
        def _():
            dkv_ref[...] = jnp.zeros_like(dkv_ref)

        for h in range(N_HEADS_M):
            hs = pl.ds(h * HEAD_DIM_M, HEAD_DIM_M)
            vs = pl.ds(W_M + h * HEAD_DIM_M, HEAD_DIM_M)
            _, vjp = jax.vjp(mem_tile, q_ref[:, hs], kv_ref[:, hs], kv_ref[:, vs])
            dq, dk, dv = vjp((do_ref[:, hs],))
            dq_ref[:, hs] = dq
            dkv_ref[:, hs] += dk
            dkv_ref[:, vs] += dv

    kvspec = pl.BlockSpec((N_MEM, 2 * W_M), lambda i: (0, 0))
    return pl.pallas_call(
        body,
        out_shape=[jax.ShapeDtypeStruct((T, W_M), F32), jax.ShapeDtypeStruct((N_MEM, 2 * W_M), F32)],
        grid=(T // tr,),
        in_specs=[pl.BlockSpec((tr, W_M), lambda i: (i, q_off // W_M)), kvspec, pl.BlockSpec((tr, W_M), lambda i: (i, 0))],
        out_specs=[pl.BlockSpec((tr, W_M), lambda i: (i, 0)), kvspec],
        name="mem_bwd", compiler_params=_cparams(("arbitrary",)))(p, kv, do)


def _scan_const_specs(dk):
    C, L = SCAN_CHUNK, SCAN_LEVELS
    return [pl.BlockSpec((2, (2 + L) * C, C), lambda n: (0, 0, 0)),
            pl.BlockSpec((2, C, (2 + L) * C), lambda n: (0, 0, 0)),
            pl.BlockSpec((2, L * C, dk), lambda n: (0, 0, 0)),
            pl.BlockSpec((2, L * C, dk), lambda n: (0, 0, 0)),
            pl.BlockSpec((L * C, C), lambda n: (0, 0))]


def _chunk_spec(src, width, chunk_of):
    arr, sel = src
    if arr.ndim == 2:
        assert sel % width == 0
        return pl.BlockSpec((SCAN_CHUNK, width), functools.partial(lambda n, b: (chunk_of(n), b), b=sel // width))
    return pl.BlockSpec((None, SCAN_CHUNK, width), functools.partial(lambda n, d: (d, chunk_of(n), 0), d=sel))


def _scan_const_args():
    h, ht, qm, km, bm = _scan_consts()
    return [jnp.asarray(h, BF16), jnp.asarray(ht, BF16), jnp.asarray(qm, F32), jnp.asarray(km, F32), jnp.asarray(bm, F32)]


def scan_fwd(name, q, kf, kb, gf, gb, v, heads, dk, dv, T):
    C = SCAN_CHUNK
    N = T // C
    assert dk == C
    W, Wv = heads * dk, heads * dv
    fwd = lambda n: n
    rev = lambda n: N - 1 - n

    def body(qf_ref, qb_ref, kf_ref, kb_ref, gf_ref, gb_ref, vf_ref, vb_ref, h_ref, ht_ref, qm_ref, km_ref, bm_ref,
             of_ref, ob_ref, ssf_ref, ssb_ref, st_ref):
        @pl.when(pl.program_id(0) == 0)
        def _():
            st_ref[...] = jnp.zeros_like(st_ref)

        bm = bm_ref[...]
        dirs = ((qf_ref, kf_ref, gf_ref, vf_ref, of_ref, ssf_ref), (qb_ref, kb_ref, gb_ref, vb_ref, ob_ref, ssb_ref))
        for d, (q_r, k_r, g_r, v_r, o_r, ss_r) in enumerate(dirs):
            consts = (qm_ref[d], km_ref[d], bm)
            g = g_r[...]
            e = _split_mm(h_ref[d], g)
            tot = jnp.sum(g, axis=0, keepdims=True)
            for h in range(heads):
                ks, vs = slice(h * dk, (h + 1) * dk), slice(h * dv, (h + 1) * dv)
                st = st_ref[d, h]
                ss_r[h] = st
                o, st_new = scan_chunk(q_r[:, ks], k_r[:, ks], v_r[:, vs], e[:, ks], tot[:, ks], st, *consts)
                o_r[:, vs] = o
                st_ref[d, h] = st_new

    srcs = [(q, fwd, W), (q, rev, W), (kf, fwd, W), (kb, rev, W), (gf, fwd, W), (gb, rev, W), (v, fwd, Wv), (v, rev, Wv)]
    ss_spec = lambda order: pl.BlockSpec((heads, None, dv, dk), lambda n: (0, order(n), 0, 0))
    return pl.pallas_call(
        body,
        out_shape=[jax.ShapeDtypeStruct((T, Wv), F32)] * 2 + [jax.ShapeDtypeStruct((heads, N, dv, dk), F32)] * 2,
        grid=(N,),
        in_specs=[_chunk_spec(s, w, order) for s, order, w in srcs] + _scan_const_specs(dk),
        out_specs=[pl.BlockSpec((C, Wv), lambda n: (fwd(n), 0)), pl.BlockSpec((C, Wv), lambda n: (rev(n), 0)),
                   ss_spec(fwd), ss_spec(rev)],
        scratch_shapes=[pltpu.VMEM((2, heads, dv, dk), F32)],
        name=name, compiler_params=_cparams(("arbitrary",)))(*[s[0] for s, _, _ in srcs], *_scan_const_args())


def scan_bwd(name, q, kf, kb, gf, gb, v, ss_f, ss_b, do, heads, dk, dv, T):
    C = SCAN_CHUNK
    N = T // C
    W, Wv = heads * dk, heads * dv
    fwd = lambda n: N - 1 - n
    rev = lambda n: n

    def body(qf_ref, qb_ref, kf_ref, kb_ref, gf_ref, gb_ref, vf_ref, vb_ref, ssf_ref, ssb_ref, dof_ref, dob_ref,
             h_ref, ht_ref, qm_ref, km_ref, bm_ref,
             dqf_ref, dkf_ref, dgf_ref, dvf_ref, dqb_ref, dkb_ref, dgb_ref, dvb_ref, dst_ref):
        @pl.when(pl.program_id(0) == 0)
        def _():
            dst_ref[...] = jnp.zeros_like(dst_ref)

        bm = bm_ref[...]
        dirs = ((qf_ref, kf_ref, gf_ref, vf_ref, ssf_ref, dof_ref, dqf_ref, dkf_ref, dgf_ref, dvf_ref),
                (qb_ref, kb_ref, gb_ref, vb_ref, ssb_ref, dob_ref, dqb_ref, dkb_ref, dgb_ref, dvb_ref))
        for d, (q_r, k_r, g_r, v_r, ss_r, do_r, dq_r, dk_r, dg_r, dv_r) in enumerate(dirs):
            consts = (qm_ref[d], km_ref[d], bm)
            g = g_r[...]
            e = _split_mm(h_ref[d], g)
            tot = jnp.sum(g, axis=0, keepdims=True)
            des, dtots = [], []
            for h in range(heads):
                ks, vs = slice(h * dk, (h + 1) * dk), slice(h * dv, (h + 1) * dv)
                _, vjp = jax.vjp(lambda q_, k_, v_, e_, t_, st_: scan_chunk(q_, k_, v_, e_, t_, st_, *consts),
                                 q_r[:, ks], k_r[:, ks], v_r[:, vs], e[:, ks], tot[:, ks], ss_r[h])
                dq, dk_, dv_, de, dtot, dst = vjp((do_r[:, vs], dst_ref[d, h]))
                dq_r[:, ks] = dq
                dk_r[:, ks] = dk_
                dv_r[:, vs] = dv_
                dst_ref[d, h] = dst
                des.append(de)
                dtots.append(dtot)
            dg_r[...] = _split_mm(ht_ref[d], jnp.concatenate(des, axis=-1)) + jnp.concatenate(dtots, axis=-1)

    srcs = [(q, fwd, W), (q, rev, W), (kf, fwd, W), (kb, rev, W), (gf, fwd, W), (gb, rev, W), (v, fwd, Wv), (v, rev, Wv)]
    ss_spec = lambda order: pl.BlockSpec((heads, None, dv, dk), lambda n: (0, order(n), 0, 0))
    kspec = lambda order: pl.BlockSpec((C, W), lambda n: (order(n), 0))
    vspec = lambda order: pl.BlockSpec((C, Wv), lambda n: (order(n), 0))
    return pl.pallas_call(
        body,
        out_shape=([jax.ShapeDtypeStruct((T, W), F32)] * 3 + [jax.ShapeDtypeStruct((T, Wv), F32)]) * 2,
        grid=(N,),
        in_specs=[_chunk_spec(s, w, order) for s, order, w in srcs]
        + [ss_spec(fwd), ss_spec(rev), _chunk_spec(do, Wv, fwd), _chunk_spec(do, Wv, rev)] + _scan_const_specs(dk),
        out_specs=[kspec(fwd)] * 3 + [vspec(fwd)] + [kspec(rev)] * 3 + [vspec(rev)],
        scratch_shapes=[pltpu.VMEM((2, heads, dv, dk), F32)],
        name=name, compiler_params=_cparams(("arbitrary",)))(
            *[s[0] for s, _, _ in srcs], ss_f, ss_b, do[0], do[0], *_scan_const_args())


def final_call(x, g, target, T):
    tr = _row_tile(T)

    def tile(xv, gv, tv):
        y = _rms(xv, gv)
        err = (y - tv) ** 2
        return jnp.sum(jnp.sum(err, axis=-1, keepdims=True), axis=0, keepdims=True) * (0.5 / D_MODEL)

    def body(x_ref, g_ref, t_ref, loss_ref, dx_ref, dg_ref):
        i = pl.program_id(0)
        tv = t_ref[...]
        lv, vjp = jax.vjp(lambda a, b: tile(a, b, tv), x_ref[...], g_ref[...])
        dx, dg = vjp(jnp.ones((1, 1), F32))
        dx_ref[...] = dx

        @pl.when(i == 0)
        def _():
            loss_ref[...] = jnp.zeros_like(loss_ref)
            dg_ref[...] = jnp.zeros_like(dg_ref)

        loss_ref[...] += jnp.broadcast_to(lv, loss_ref.shape)
        dg_ref[...] += dg

    return pl.pallas_call(
        body,
        out_shape=[jax.ShapeDtypeStruct((8, 128), F32), jax.ShapeDtypeStruct((T, D_MODEL), F32),
                   jax.ShapeDtypeStruct((1, D_MODEL), F32)],
        grid=(T // tr,),
        in_specs=[pl.BlockSpec((tr, D_MODEL), lambda i: (i, 0)), pl.BlockSpec((1, D_MODEL), lambda i: (0, 0)),
                  pl.BlockSpec((tr, D_MODEL), lambda i: (i, 0))],
        out_specs=[pl.BlockSpec((8, 128), lambda i: (0, 0)), pl.BlockSpec((tr, D_MODEL), lambda i: (i, 0)),
                   pl.BlockSpec((1, D_MODEL), lambda i: (0, 0))],
        name="final_loss", compiler_params=_cparams(("arbitrary",)))(x, g, target)


def adamw_call(w, g, m, v):
    shape = w.shape
    c = shape[-1]
    r = int(np.prod(shape[:-1])) if len(shape) > 1 else 1
    tr = r if r <= 256 else 256
    assert r % tr == 0

    def body(w_ref, g_ref, m_ref, v_ref, d_ref, nm_ref, nv_ref):
        gv = g_ref[...]
        nm = ADAM_B1 * m_ref[...] + (1.0 - ADAM_B1) * gv
        nv = ADAM_B2 * v_ref[...] + (1.0 - ADAM_B2) * jnp.square(gv)
        m_hat = nm / (1.0 - ADAM_B1 ** ADAM_STEP)
        v_hat = nv / (1.0 - ADAM_B2 ** ADAM_STEP)
        d_ref[...] = -ADAM_LR * (m_hat / (jnp.sqrt(v_hat) + ADAM_EPS) + ADAM_WD * w_ref[...])
        nm_ref[...] = nm
        nv_ref[...] = nv

    spec = pl.BlockSpec((tr, c), lambda i: (i, 0))
    outs = pl.pallas_call(body, out_shape=[jax.ShapeDtypeStruct((r, c), F32)] * 3, grid=(r // tr,),
                          in_specs=[spec] * 4, out_specs=[spec] * 3, name="adamw",
                          compiler_params=_cparams(("arbitrary",)))(*(t.reshape(r, c) for t in (w, g, m, v)))
    return tuple(o.reshape(shape) for o in outs)


def adamw_halves(w, mine, other, m, v, c):
    L, R, C = w.shape
    rh = R // 2
    tr = rh if rh <= 256 else 256
    nbh = rh // tr

    def body(c_ref, w_ref, a_ref, b_ref, m_ref, v_ref, g_ref, d_ref, nm_ref, nv_ref):
        is_mine = (pl.program_id(1) // nbh) == c_ref[0]
        gv = jnp.where(is_mine, a_ref[...], b_ref[...])
        nm = ADAM_B1 * m_ref[...] + (1.0 - ADAM_B1) * gv
        nv = ADAM_B2 * v_ref[...] + (1.0 - ADAM_B2) * jnp.square(gv)
        m_hat = nm / (1.0 - ADAM_B1 ** ADAM_STEP)
        v_hat = nv / (1.0 - ADAM_B2 ** ADAM_STEP)
        g_ref[...] = gv
        d_ref[...] = -ADAM_LR * (m_hat / (jnp.sqrt(v_hat) + ADAM_EPS) + ADAM_WD * w_ref[...])
        nm_ref[...] = nm
        nv_ref[...] = nv

    full = pl.BlockSpec((None, tr, C), lambda l, i, c_ref: (l, i, 0))
    half = pl.BlockSpec((None, tr, C), lambda l, i, c_ref: (l, i % nbh, 0))
    grid_spec = pltpu.PrefetchScalarGridSpec(num_scalar_prefetch=1, grid=(L, R // tr),
                                             in_specs=[full, half, half, full, full], out_specs=[full] * 4)
    return pl.pallas_call(body, out_shape=[jax.ShapeDtypeStruct(w.shape, F32)] * 4, grid_spec=grid_spec,
                          name="adamw_halves", compiler_params=_cparams(("arbitrary", "arbitrary")))(c, w, mine, other, m, v)


def sum_devices(g64):
    def body(x_ref, o_ref):
        acc = x_ref[0:8, :]
        for d in range(1, 8):
            acc = acc + x_ref[8 * d:8 * d + 8, :]
        o_ref[...] = acc

    return pl.pallas_call(body, out_shape=jax.ShapeDtypeStruct((8, D_MODEL), F32), name="sum_devices")(g64)


def _half_tile(rh):
    return rh if rh <= 512 else 256


def add_sibling(g, recv, c, out_dtype):
    _, R, C = g.shape
    rh = R // 2
    tr = _half_tile(rh)
    nblk = rh // tr

    def body(c_ref, g_ref, r_ref, o_ref):
        o_ref[...] = (g_ref[...] + r_ref[...]).astype(o_ref.dtype)

    grid_spec = pltpu.PrefetchScalarGridSpec(
        num_scalar_prefetch=1, grid=(4, nblk),
        in_specs=[pl.BlockSpec((None, tr, C), lambda j, i, c_ref: (j, i + c_ref[0] * nblk, 0)),
                  pl.BlockSpec((None, tr, C), lambda j, i, c_ref: (j, i, 0))],
        out_specs=pl.BlockSpec((None, tr, C), lambda j, i, c_ref: (j, i, 0)))
    return pl.pallas_call(body, out_shape=jax.ShapeDtypeStruct((4, rh, C), out_dtype), grid_spec=grid_spec,
                          name="rs_add_sibling", compiler_params=_cparams(("arbitrary", "arbitrary")))(c, g, recv)


def add_chips(g, recv, r3, place):
    _, R, C = g.shape
    rh = R // 2
    tr = _half_tile(rh)
    nblk = rh // tr

    def body(p_ref, g_ref, s_ref, a_ref, b_ref, c_ref, o_ref):
        up = lambda r: r[...].astype(F32)
        o_ref[...] = (((g_ref[...] + up(s_ref)) + up(a_ref)) + up(b_ref)) + up(c_ref)

    grid_spec = pltpu.PrefetchScalarGridSpec(
        num_scalar_prefetch=1, grid=(nblk,),
        in_specs=[pl.BlockSpec((None, tr, C), lambda i, p_ref: (p_ref[0], i + p_ref[1] * nblk, 0)),
                  pl.BlockSpec((None, tr, C), lambda i, p_ref: (p_ref[0], i, 0))]
        + [pl.BlockSpec((None, tr, C), functools.partial(lambda i, p_ref, k: (k, i, 0), k=k)) for k in range(3)],
        out_specs=pl.BlockSpec((tr, C), lambda i, p_ref: (i, 0)))
    return pl.pallas_call(body, out_shape=jax.ShapeDtypeStruct((rh, C), F32), grid_spec=grid_spec,
                          name="rs_add_chips", compiler_params=_cparams(("arbitrary",)))(place, g, recv, r3, r3, r3)


def _remote(src, dst, ssem, rsem, dev):
    return pltpu.make_async_remote_copy(src_ref=src, dst_ref=dst, send_sem=ssem, recv_sem=rsem,
                                        device_id=dev, device_id_type=pl.DeviceIdType.MESH)


def _mesh_places():
    x, y, c = lax.axis_index("x"), lax.axis_index("y"), lax.axis_index("c")
    chips = [(1 - x, y), (x, 1 - y), (1 - x, 1 - y)]
    return x, y, c, (x, y, 1 - c), chips


def _hbm_specs(n):
    return [pl.BlockSpec(memory_space=pltpu.HBM) for _ in range(n)]


def _gather_body(ins, outs, n_split, send_sems, recv_sems, handshake):
    x, y, c, sibling, chips = _mesh_places()
    mine = 2 * x + y
    if handshake:
        barrier = pltpu.get_barrier_semaphore()
        peers = [sibling] + [(*chip, c) for chip in chips]
        for peer in peers:
            pl.semaphore_signal(barrier, inc=1, device_id=peer, device_id_type=pl.DeviceIdType.MESH)
        pl.semaphore_wait(barrier, len(peers))

    def half(a, chip_idx, which):
        rh = ins[a].shape[0] // 2
        return outs[a].at[chip_idx, pl.ds(which * rh, rh), :]

    sent = []
    for a in range(len(ins)):
        for k, chip in enumerate(chips):
            if a < n_split:
                rh = ins[a].shape[0] // 2
                src, dst = ins[a].at[pl.ds(c * rh, rh), :], half(a, mine, c)
            else:
                src, dst = ins[a], outs[a].at[mine]
            sent.append(_remote(src, dst, send_sems.at[a, k], recv_sems.at[a, k], (*chip, c)))
    for cp in sent:
        cp.start()
    for a in range(len(ins)):
        for k, chip in enumerate(chips):
            j = 2 * chip[0] + chip[1]
            region = half(a, j, c) if a < n_split else outs[a].at[j]
            _remote(region, region, send_sems.at[a, k], recv_sems.at[a, k], (*chip, c)).wait_recv()
            if a < n_split:
                fwd = _remote(region, region, send_sems.at[a, 3 + k], recv_sems.at[a, 3 + k], sibling)
                fwd.start()
                sent.append(fwd)
    for a in range(n_split):
        for k, chip in enumerate(chips):
            region = half(a, 2 * chip[0] + chip[1], 1 - c)
            _remote(region, region, send_sems.at[a, 3 + k], recv_sems.at[a, 3 + k], sibling).wait_recv()
    for cp in sent:
        cp.wait_send()


def gather_weights(shards, small):
    arrs = list(shards) + [small]
    n = len(arrs)

    def body(*refs):
        _gather_body(refs[:n], refs[n:2 * n], n - 1, refs[2 * n], refs[2 * n + 1], handshake=False)

    return pl.pallas_call(
        body, out_shape=[jax.ShapeDtypeStruct((4,) + a.shape, a.dtype) for a in arrs],
        in_specs=_hbm_specs(n), out_specs=_hbm_specs(n),
        scratch_shapes=[pltpu.SemaphoreType.DMA((n, 6)), pltpu.SemaphoreType.DMA((n, 6))],
        name="gather_weights")(*arrs)


def gather_weights_async(shards):
    n = len(shards)

    def body(*refs):
        _gather_body(refs[:n], refs[n:2 * n], n, refs[2 * n], refs[2 * n + 1], handshake=True)

    return pl.kernel(
        body, out_type=[jax.ShapeDtypeStruct((4,) + a.shape, a.dtype) for a in shards],
        mesh=plsc.ScalarSubcoreMesh(axis_name="seq", num_cores=1),
        scratch_types=[pltpu.SemaphoreType.DMA((n, 6)), pltpu.SemaphoreType.DMA((n, 6))],
        compiler_params=pltpu.CompilerParams(collective_id=1), name="gather_weights_async")(*shards)


def _sequencer_call(name, body, out_type, sem_shape, collective_id, args):
    return pl.kernel(
        body, out_type=out_type, mesh=plsc.ScalarSubcoreMesh(axis_name="seq", num_cores=1),
        scratch_types=[pltpu.SemaphoreType.DMA(sem_shape), pltpu.SemaphoreType.DMA(sem_shape)],
        compiler_params=pltpu.CompilerParams(collective_id=collective_id), name=name)(*args)


def _handshake(peers):
    barrier = pltpu.get_barrier_semaphore()
    for peer in peers:
        pl.semaphore_signal(barrier, inc=1, device_id=peer, device_id_type=pl.DeviceIdType.MESH)
    pl.semaphore_wait(barrier, len(peers))


def exchange_siblings(name, srcs, halves, collective_id):
    n = len(srcs)

    def body(*refs):
        ins, outs = refs[:n], refs[n:2 * n]
        send_sems, recv_sems = refs[2 * n:]
        x, y, c, sibling, chips = _mesh_places()
        _handshake([sibling])
        cps = []
        for a in range(n):
            src = ins[a]
            if halves:
                rh = src.shape[1] // 2
                src = src.at[:, pl.ds((1 - c) * rh, rh), :]
            cps.append(_remote(src, outs[a], send_sems.at[a], recv_sems.at[a], sibling))
        for cp in cps:
            cp.start()
        for cp in cps:
            cp.wait()

    shape = lambda g: (4, g.shape[1] // 2, g.shape[2]) if halves else g.shape
    return _sequencer_call(name, body, [jax.ShapeDtypeStruct(shape(g), g.dtype) for g in srcs], (n,), collective_id, srcs)


def exchange_chips(name, s1s, collective_id):
    n = len(s1s)

    def body(*refs):
        ins, outs = refs[:n], refs[n:2 * n]
        send_sems, recv_sems = refs[2 * n:]
        x, y, c, sibling, chips = _mesh_places()
        _handshake([(*chip, c) for chip in chips])
        cps = []
        for a in range(n):
            for k, chip in enumerate(chips):
                cps.append(_remote(ins[a].at[2 * chip[0] + chip[1]], outs[a].at[k], send_sems.at[a, k],
                                   recv_sems.at[a, k], (*chip, c)))
        for cp in cps:
            cp.start()
        for cp in cps:
            cp.wait()

    return _sequencer_call(name, body, [jax.ShapeDtypeStruct((3,) + s.shape[1:], s.dtype) for s in s1s], (n, 3),
                           collective_id, s1s)


def allgather_small(v):
    m_per = v.shape[0]

    def body(x_ref, out_ref, send_sems, recv_sems, local_sem):
        x, y, c, sibling, chips = _mesh_places()
        me = (x, y, c)

        def rows(px, py, pc):
            return out_ref.at[pl.ds((4 * px + 2 * py + pc) * m_per, m_per), :]

        def copy(k, block, to, src=None):
            return _remote(rows(*block) if src is None else src, rows(*block), send_sems.at[k], recv_sems.at[k], to)

        mine = pltpu.make_async_copy(x_ref, rows(*me), local_sem)
        mine.start()
        first = [copy(0, me, sibling, src=x_ref)]
        first += [copy(1 + j, me, (*chip, c), src=x_ref) for j, chip in enumerate(chips)]
        for cp in first:
            cp.start()
        passed = [copy(4 + j, (*chip, c), sibling) for j, chip in enumerate(chips)]
        for j, chip in enumerate(chips):
            copy(1 + j, (*chip, c), me).wait_recv()
            passed[j].start()
        copy(0, sibling, me).wait_recv()
        for j, chip in enumerate(chips):
            copy(4 + j, (*chip, 1 - c), me).wait_recv()
        for cp in first + passed:
            cp.wait_send()
        mine.wait()

    return pl.pallas_call(
        body, out_shape=jax.ShapeDtypeStruct((8 * m_per, v.shape[1]), v.dtype),
        in_specs=[pl.BlockSpec(memory_space=pltpu.VMEM)], out_specs=pl.BlockSpec(memory_space=pltpu.VMEM),
        scratch_shapes=[pltpu.SemaphoreType.DMA((7,)), pltpu.SemaphoreType.DMA((7,)), pltpu.SemaphoreType.DMA],
        name="allgather_small")(v)


def rms_res_tile(x, g):
    return (_rms(x, g), x)


def _lower_bounds(lb_param):
    lbs = jax.nn.softmax(lb_param.astype(F32), axis=0)
    return jnp.cumsum(lbs, axis=0) - lbs[0]


def _heads_major(t, n):
    return t.reshape(t.shape[0], n, HEAD_DIM_A).transpose(1, 0, 2)


def _heads_minor(t):
    return t.transpose(1, 0, 2).reshape(t.shape[1], t.shape[0] * t.shape[2])


def _even_fwd(x, i, W, lower, kv, slopes, T):
    O = EVEN_OFF
    g = W["norm_even"][i].reshape(1, D_MODEL)
    (h,) = rows_call("rms_fwd", rms_tile, T, [("row", x, 0, D_MODEL), ("full", g)], [D_MODEL], [BF16])
    p = matmul("mm_in_e", h, W["w_in_e"][i], "nn")
    kvp = jnp.pad(p[:, O["kA"]:O["kA"] + 2 * W_KV_A], ((BLOCK, BLOCK), (0, 0)))
    sink = W["sink"][i].reshape(N_Q_A, 1, 1)
    a = attn_fwd(p, O["qA"], kvp, sink, slopes, T)
    prep_ins = [("row", p, O["qB"], W_B), ("row", p, O["zf"], W_B), ("row", p, O["zb"], W_B),
                ("full", lower[i][0:1]), ("full", lower[i][1:2])]
    qh, k2, g2 = rows_call("hgrn_prep_fwd", hgrn_prep_tile, T, prep_ins, [W_B] * 3, stacks=[(0,), (1, 2), (3, 4)])
    scan_srcs = [(qh, 0), (k2, 0), (k2, 1), (g2, 0), (g2, 1), (p, O["iB"])]
    o_f, o_b, ss_f, ss_b = scan_fwd("scan_fwd_h", *scan_srcs, N_HEADS_B, HEAD_DIM_B, HEAD_DIM_B, T)
    mo = mem_fwd(p, O["qM"], kv, T)
    hg = W["hgrn_norm"][i].reshape(1, W_B)
    post_ins = [("row", a, 0, W_A), ("row", o_f, 0, W_B), ("row", o_b, 0, W_B), ("row", mo, 0, W_M),
                ("row", p, O["gA"], W_A), ("row", p, O["gB"], W_B), ("row", p, O["gM"], W_M), ("full", hg)]
    (mix,) = rows_call("even_post_fwd", even_post_tile, T, post_ins, [MIX], [BF16])
    x_new = matmul("mm_out", mix, W["w_out_e"][i], "nn", add=x)
    return x_new, dict(x=x, g=g, h=h, p=p, kvp=kvp, sink=sink, prep_ins=prep_ins,
                       scan_srcs=scan_srcs, ss_f=ss_f, ss_b=ss_b, post_ins=post_ins, mix=mix)


def _assemble_even(dqA, dgA, dqB, dzf, dzb, dv0, dv1, dgB, dqM, dgM, dkvA):
    return (jnp.concatenate([dqA, dgA, dqB, dzf, dzb, dv0 + dv1, dgB, dqM, dgM, dkvA], axis=-1),)


def _even_bwd(dxo, sv, i, W, kv, slopes, T, sync):
    O = EVEN_OFF
    p = sv["p"]
    dmix = matmul("mm_dmix", dxo, W["w_out_e"][i], "nt")
    dwo = matmul("mm_dwo", sv["mix"], dxo, "tn")
    da, dof, dmo, dgA, dgB, dgM, dhg = rows_vjp_call("even_post_bwd", even_post_tile, T, sv["post_ins"],
                                                      [[("row", dmix, 0, MIX)]], skip=(2,))
    dqA, dkvp, dsink = attn_bwd(p, O["qA"], sv["kvp"], sv["sink"], slopes, da, T)
    dkvA = dkvp[BLOCK:-BLOCK]
    dqf, dkf, dgf, dvf, dqb, dkb, dgb, dvb = scan_bwd("scan_bwd_h", *sv["scan_srcs"], sv["ss_f"], sv["ss_b"], (dof, 0),
                                                      N_HEADS_B, HEAD_DIM_B, HEAD_DIM_B, T)
    dqf = sync(dqf)
    row = lambda arr, w: ("row", arr, 0, w)
    dqB, dzf, dzb, dlow_f, dlow_b = rows_vjp_call(
        "hgrn_prep_bwd", hgrn_prep_tile, T, sv["prep_ins"],
        [[row(dqf, W_B), row(dqb, W_B)], [row(dkf, W_B)], [row(dkb, W_B)], [row(dgf, W_B)], [row(dgb, W_B)]])
    dlow = jnp.concatenate([dlow_f, dlow_b], axis=0)
    dqM, dkv = mem_bwd(p, O["qM"], kv, dmo, T)
    (dp,) = rows_call("even_dp", _assemble_even, T,
                      [row(dqA, W_A), row(dgA, W_A), row(dqB, W_B), row(dzf, W_B), row(dzb, W_B), row(dvf, W_B), row(dvb, W_B),
                       row(dgB, W_B), row(dqM, W_M), row(dgM, W_M), row(dkvA, 2 * W_KV_A)],
                      [EVEN_IN], [BF16])
    dh = matmul("mm_dh_e", dp, W["w_in_e"][i], "nt")
    dwi = matmul("mm_dwi_e", sv["h"], dp, "tn")
    dx, dg = rows_vjp_call("rms_res_bwd", rms_res_tile, T, [("row", sv["x"], 0, D_MODEL), ("full", sv["g"])],
                           [[("row", dh, 0, D_MODEL)], [("row", dxo, 0, D_MODEL)]])
    return dx, dict(w_in=dwi, w_out=dwo, norm=dg[0], sink=dsink.reshape(N_Q_A), low=dlow, hg=dhg[0], kv=dkv)


def _pad_gate_up(w_up):
    z = jnp.zeros((2, 128, WK_C), F32)
    z = z.at[0, 0:GATE_RANK].set(w_up[0])
    return z.at[1, GATE_RANK:2 * GATE_RANK].set(w_up[1])


def _odd_fwd(x, i, W, kv, T):
    O = ODD_OFF
    g = W["norm_odd"][i].reshape(1, D_MODEL)
    (h,) = rows_call("rms_fwd", rms_tile, T, [("row", x, 0, D_MODEL), ("full", g)], [D_MODEL], [BF16])
    p = matmul("mm_in_o", h, W["w_in_o"][i], "nn")
    wup = _pad_gate_up(W["w_gate_up"][i])
    prep_ins = [("row", p, O["qC"], WK_C), ("row", p, O["rr"], 128), ("full", wup[0]), ("full", wup[1]),
                ("full", W["b_gate"][i][0:1]), ("full", W["b_gate"][i][1:2])]
    qg, g2 = rows_call("gla_prep_fwd", gla_prep_tile, T, prep_ins, [WK_C] * 2, stacks=[(0,), (1, 2)])
    scan_srcs = [(qg, 0), (p, O["kC"]), (p, O["kC"]), (g2, 0), (g2, 1), (p, O["vC"])]
    o_f, o_b, ss_f, ss_b = scan_fwd("scan_fwd_g", *scan_srcs, N_HEADS_C, DK_C, DV_C, T)
    mo = mem_fwd(p, O["qM"], kv, T)
    gg = W["gla_norm"][i].reshape(1, WV_C)
    post_ins = [("row", o_f, 0, WV_C), ("row", o_b, 0, WV_C), ("row", mo, 0, W_M),
                ("row", p, O["gC"], WV_C), ("row", p, O["gM"], W_M), ("full", gg)]
    (mix,) = rows_call("odd_post_fwd", odd_post_tile, T, post_ins, [MIX], [BF16])
    x_new = matmul("mm_out", mix, W["w_out_o"][i], "nn", add=x)
    return x_new, dict(x=x, g=g, h=h, p=p, prep_ins=prep_ins, scan_srcs=scan_srcs, ss_f=ss_f, ss_b=ss_b,
                       post_ins=post_ins, mix=mix)


def _assemble_odd(dqC, dk0, dk1, dv0, dv1, dgC, dqM, dgM, dr):
    return (jnp.concatenate([dqC, dk0 + dk1, dv0 + dv1, dgC, dqM, dgM, dr], axis=-1),)


def _odd_bwd(dxo, sv, i, W, kv, T, sync):
    O = ODD_OFF
    p = sv["p"]
    dmix = matmul("mm_dmix", dxo, W["w_out_o"][i], "nt")
    dwo = matmul("mm_dwo", sv["mix"], dxo, "tn")
    dof, dmo, dgC, dgM, dgg = rows_vjp_call("odd_post_bwd", odd_post_tile, T, sv["post_ins"],
                                            [[("row", dmix, 0, MIX)]], skip=(1,))
    dqf, dkf, dgf, dvf, dqb, dkb, dgb, dvb = scan_bwd("scan_bwd_g", *sv["scan_srcs"], sv["ss_f"], sv["ss_b"], (dof, 0),
                                                      N_HEADS_C, DK_C, DV_C, T)
    dqf = sync(dqf)
    row = lambda arr, w: ("row", arr, 0, w)
    dqC, dr, dwup_f, dwup_b, dbg_f, dbg_b = rows_vjp_call(
        "gla_prep_bwd", gla_prep_tile, T, sv["prep_ins"],
        [[row(dqf, WK_C), row(dqb, WK_C)], [row(dgf, WK_C)], [row(dgb, WK_C)]])
    dqM, dkv = mem_bwd(p, O["qM"], kv, dmo, T)
    (dp,) = rows_call("odd_dp", _assemble_odd, T,
                      [row(dqC, WK_C), row(dkf, WK_C), row(dkb, WK_C), row(dvf, WV_C), row(dvb, WV_C),
                       row(dgC, WV_C), row(dqM, W_M), row(dgM, W_M), row(dr, 128)],
                      [ODD_PAD], [BF16])
    dh = matmul("mm_dh_o", dp, W["w_in_o"][i], "nt")
    dwi = matmul("mm_dwi_o", sv["h"], dp, "tn")
    dx, dg = rows_vjp_call("rms_res_bwd", rms_res_tile, T, [("row", sv["x"], 0, D_MODEL), ("full", sv["g"])],
                           [[("row", dh, 0, D_MODEL)], [("row", dxo, 0, D_MODEL)]])
    dw_up = jnp.stack([dwup_f[0:GATE_RANK], dwup_b[GATE_RANK:2 * GATE_RANK]])
    dbg = jnp.concatenate([dbg_f, dbg_b], axis=0)
    return dx, dict(w_in=dwi, w_out=dwo, norm=dg[0], w_up=dw_up, b_gate=dbg, gg=dgg[0], kv=dkv)


def local_step(x, mem, target, W, later=None, on_layer_grads=None, sync=lambda a: a):
    T = x.shape[0]
    slopes = (2.0 ** (-8.0 * jnp.arange(1, N_Q_A + 1, dtype=F32) / N_Q_A)).reshape(N_Q_A, 1, 1)
    lower, lower_vjp = jax.vjp(_lower_bounds, W["lb_param"])
    mem_g = W["mem_norm"].reshape(1, D_MODEL)
    (mem_n,) = rows_call("mem_rms_fwd", rms_tile, N_MEM, [("row", mem, 0, D_MODEL), ("full", mem_g)], [D_MODEL], [BF16])
    kvs, saved = [], []
    for l in range(DEPTH):
        if l == 1 and later is not None:
            x, W = later(x, W)
        kvs.append(matmul("mm_kv", mem_n, W["w_kv"][l], "nn"))
        if l % 2 == 0:
            x, sv = _even_fwd(x, l // 2, W, lower, kvs[l], slopes, T)
        else:
            x, sv = _odd_fwd(x, l // 2, W, kvs[l], T)
        saved.append(sv)
    loss, dx, dgf = final_call(x, W["final_norm"].reshape(1, D_MODEL), target, T)
    per = [None] * DEPTH
    dmem_n = None
    for l in reversed(range(DEPTH)):
        if l % 2 == 0:
            dx, per[l] = _even_bwd(dx, saved[l], l // 2, W, kvs[l], slopes, T, sync)
        else:
            dx, per[l] = _odd_bwd(dx, saved[l], l // 2, W, kvs[l], T, sync)
        per[l]["w_kv"] = matmul("mm_dwkv", mem_n, per[l]["kv"], "tn")
        dmem_n = matmul("mm_dmem", per[l]["kv"], W["w_kv"][l], "nt", add=dmem_n)
        if on_layer_grads is not None:
            dx = on_layer_grads(l, dx, per[l])
    dw_kv = [per[l]["w_kv"] for l in range(DEPTH)]
    (dmem_norm,) = rows_vjp_call("mem_rms_bwd", rms_tile, N_MEM, [("row", mem, 0, D_MODEL), ("full", mem_g)],
                                 [[("row", dmem_n, 0, D_MODEL)]], skip=(0,))
    ev, od = (per[0], per[2]), (per[1], per[3])
    (d_lb,) = lower_vjp(jnp.stack([e["low"] for e in ev]))
    grads = dict(
        w_in_e=jnp.stack([e["w_in"] for e in ev]), w_in_o=jnp.stack([o["w_in"] for o in od]),
        w_out_e=jnp.stack([e["w_out"] for e in ev]), w_out_o=jnp.stack([o["w_out"] for o in od]),
        w_kv=jnp.stack(dw_kv), norm_even=jnp.stack([e["norm"] for e in ev]), sink=jnp.stack([e["sink"] for e in ev]),
        lb_param=d_lb, hgrn_norm=jnp.stack([e["hg"] for e in ev]), norm_odd=jnp.stack([o["norm"] for o in od]),
        w_gate_up=jnp.stack([o["w_up"] for o in od]), b_gate=jnp.stack([o["b_gate"] for o in od]),
        gla_norm=jnp.stack([o["gg"] for o in od]), mem_norm=dmem_norm[0], final_norm=dgf[0])
    return loss, dx, grads


SMALL_SPECS = (("lb_param", (2, 2, 128)), ("norm_odd", (2, 256)), ("w_gate_up", (2, 2, 16, 128)),
               ("b_gate", (2, 2, 128)), ("gla_norm", (2, 256)))
SMALL_ROWS = 80


def _pack_small_local(d):
    return jnp.concatenate([d[n].reshape(-1) for n, _ in SMALL_SPECS]).reshape(SMALL_ROWS, 128)


def _unpack_small_local(b):
    flat, out, o = b.reshape(-1), {}, 0
    for n, shp in SMALL_SPECS:
        sz = int(np.prod(shp))
        out[n] = flat[o:o + sz].reshape(shp)
        o += sz
    return out


def _unpack_small_full(g4):
    per = [_unpack_small_local(g4[j]) for j in range(4)]
    return {n: jnp.concatenate([per[j][n] for j in range(4)], axis=-1) for n, _ in SMALL_SPECS}


def _pack_small_blocks(full):
    blocks = []
    for j in range(4):
        blocks.append(_pack_small_local({n: full[n][..., j * shp[-1]:(j + 1) * shp[-1]] for n, shp in SMALL_SPECS}))
    return jnp.stack(blocks)


def _cols(t, order, off, widths):
    return [t[..., off[n]:off[n] + widths[n]] for n in order]


EVEN_REF_ORDER = ("qA", "kA", "vA", "gA", "qB", "zf", "zb", "iB", "gB", "qM", "gM")
ODD_REF_ORDER = ("qC", "kC", "vC", "gC", "rr", "qM", "gM")


def _layer_weights(l, g_in, g_out, g_kv):
    t = g_in.transpose(1, 0, 2).reshape(D_MODEL, -1)
    if l % 2 == 0:
        w_in = jnp.concatenate(_cols(t, EVEN_ORDER, EVEN_REF_OFF, EVEN_W), axis=-1)
    else:
        w_in = jnp.concatenate(_cols(t, ODD_ORDER, ODD_REF_OFF, ODD_W) + [jnp.zeros((D_MODEL, ODD_PAD - ODD_IN), BF16)],
                               axis=-1)
    return w_in, g_out.reshape(MIX, D_MODEL), g_kv.reshape(D_MODEL, 2 * W_M)


def _layer_grad_blocks(l, gl):
    if l % 2 == 0:
        t = jnp.concatenate(_cols(gl["w_in"], EVEN_REF_ORDER, EVEN_OFF, EVEN_W), axis=-1)
    else:
        t = jnp.concatenate(_cols(gl["w_in"], ODD_REF_ORDER, ODD_OFF, ODD_W), axis=-1)
    b_in = t.reshape(D_MODEL, 4, -1).transpose(1, 0, 2)
    return [b_in, gl["w_out"].reshape(4, MIX // 4, D_MODEL), gl["w_kv"].reshape(4, D_MODEL // 4, 2 * W_M)]


WEIGHT_NAMES = ("norm_even", "w_in_even", "sink", "lb_param", "hgrn_norm", "w_out_even", "norm_odd", "w_in_odd",
                "w_gate_up", "b_gate", "gla_norm", "w_out_odd", "mem_norm", "w_mem_kv", "final_norm")


def kernel(x, mem, norm_even, w_in_even, sink, lb_param, hgrn_norm, w_out_even, norm_odd, w_in_odd, w_gate_up, b_gate, gla_norm, w_out_odd, mem_norm, w_mem_kv, final_norm, loss_target, m_norm_even, m_w_in_even, m_sink, m_lb_param, m_hgrn_norm, m_w_out_even, m_norm_odd, m_w_in_odd, m_w_gate_up, m_b_gate, m_gla_norm, m_w_out_odd, m_mem_norm, m_w_mem_kv, m_final_norm, v_norm_even, v_w_in_even, v_sink, v_lb_param, v_hgrn_norm, v_w_out_even, v_norm_odd, v_w_in_odd, v_w_gate_up, v_b_gate, v_gla_norm, v_w_out_odd, v_mem_norm, v_w_mem_kv, v_final_norm):
    w = dict(zip(WEIGHT_NAMES, (norm_even, w_in_even, sink, lb_param, hgrn_norm, w_out_even, norm_odd, w_in_odd,
                                w_gate_up, b_gate, gla_norm, w_out_odd, mem_norm, w_mem_kv, final_norm)))
    m = dict(zip(WEIGHT_NAMES, (m_norm_even, m_w_in_even, m_sink, m_lb_param, m_hgrn_norm, m_w_out_even, m_norm_odd,
                                m_w_in_odd, m_w_gate_up, m_b_gate, m_gla_norm, m_w_out_odd, m_mem_norm, m_w_mem_kv,
                                m_final_norm)))
    v = dict(zip(WEIGHT_NAMES, (v_norm_even, v_w_in_even, v_sink, v_lb_param, v_hgrn_norm, v_w_out_even, v_norm_odd,
                                v_w_in_odd, v_w_gate_up, v_b_gate, v_gla_norm, v_w_out_odd, v_mem_norm, v_w_mem_kv,
                                v_final_norm)))
    ci = lax.axis_index("c").astype(jnp.int32).reshape(1)
    chip = (2 * lax.axis_index("x") + lax.axis_index("y")).astype(jnp.int32).reshape(1)

    shards = []
    for l in range(DEPTH):
        names = ("w_in_even", "w_out_even") if l % 2 == 0 else ("w_in_odd", "w_out_odd")
        shards.append([w[names[0]][l // 2].astype(BF16), w[names[1]][l // 2].astype(BF16), w_mem_kv[l].astype(BF16)])
    small = _pack_small_local(w)
    own = lambda g, s: lax.dynamic_update_slice(g, s[None], (chip[0], 0, 0))
    first = [own(g, s) for g, s in zip(gather_weights(shards[0], small), shards[0] + [small])]
    later_shards = shards[1] + shards[2] + shards[3]
    later_raw = gather_weights_async(later_shards)
    w0 = _layer_weights(0, *first[0:3])
    W = dict(w_in_e=[w0[0]], w_out_e=[w0[1]], w_kv=[w0[2]])
    W.update(_unpack_small_full(first[3]))
    W.update({n: w[n] for n in ("norm_even", "sink", "hgrn_norm", "mem_norm", "final_norm")})

    def later(x1, W):
        x1, raw = lax.optimization_barrier((x1, list(later_raw)))
        g = [own(a, s) for a, s in zip(raw, later_shards)]
        w1, w2, w3 = (_layer_weights(l, *g[3 * (l - 1):3 * l]) for l in (1, 2, 3))
        W = dict(W)
        W.update(w_in_e=[w0[0], w2[0]], w_in_o=[w1[0], w3[0]], w_out_e=[w0[1], w2[1]], w_out_o=[w1[1], w3[1]],
                 w_kv=[w0[2], w1[2], w2[2], w3[2]])
        return x1, W

    place = jnp.concatenate([chip, ci])

    def start(tag, blocks, wire):
        return dict(tag=tag, blocks=blocks, wire=wire, step=0,
                    recv=exchange_siblings(f"rs_siblings_{tag}", blocks, True, 2))

    def advance(p):
        if p["step"] == 0:
            sums = [add_sibling(g, r, ci, dt) for g, r, dt in zip(p["blocks"], p["recv"], p["wire"])]
            p["recv3"] = exchange_chips(f"rs_chips_{p['tag']}", sums, 3)
        else:
            p["mine"] = [add_chips(g, r, r3, place) for g, r, r3 in zip(p["blocks"], p["recv"], p["recv3"])]
            p["other"] = exchange_siblings(f"rs_final_{p['tag']}", p["mine"], False, 4)
        p["step"] += 1

    pipes, first_layer = [], {}

    def sync(a):
        for p in pipes:
            if p["step"] < 3:
                key = ("recv", "recv3", "other")[p["step"]]
                a, arrived = lax.optimization_barrier((a, list(p[key])))
                p[key] = arrived
                if p["step"] < 2:
                    advance(p)
                else:
                    p["step"] = 3
        return a

    def on_layer_grads(l, dx, gl):
        dx = sync(dx)
        if l == 0:
            first_layer.update(gl)
        else:
            pipes.append(start(f"l{l}", _layer_grad_blocks(l, gl), [BF16] * 3))
        return dx

    loss_tile, dx, grads = local_step(x[0], mem[0], loss_target[0], W, later, on_layer_grads, sync)
    pipes.append(start("l0", _layer_grad_blocks(0, first_layer) + [_pack_small_blocks(grads)], [BF16] * 3 + [F32]))
    while any(p["step"] < 2 for p in pipes):
        for p in pipes:
            if p["step"] < 2:
                advance(p)
    by_layer = {int(p["tag"][1:]): p for p in pipes}
    halves = lambda layers, k: (jnp.stack([by_layer[l]["mine"][k] for l in layers]),
                                jnp.stack([by_layer[l]["other"][k] for l in layers]))
    big = dict(w_in_even=halves((0, 2), 0), w_in_odd=halves((1, 3), 0), w_out_even=halves((0, 2), 1),
               w_out_odd=halves((1, 3), 1), w_mem_kv=halves((0, 1, 2, 3), 2))
    s_mine, s_other = by_layer[0]["mine"][3], by_layer[0]["other"][3]
    g_small = jnp.where(ci[0] == 0, jnp.concatenate([s_mine, s_other]), jnp.concatenate([s_other, s_mine]))
    gl = _unpack_small_local(g_small)

    pack = jnp.zeros((8, D_MODEL), F32)
    pack = pack.at[0:2].set(grads["norm_even"]).at[2].set(grads["hgrn_norm"].reshape(-1))
    pack = pack.at[3].set(grads["mem_norm"]).at[4].set(grads["final_norm"])
    pack = pack.at[5, 0:16].set(grads["sink"].reshape(-1)).at[5, 16].set(loss_tile[0, 0])
    tot = sum_devices(allgather_small(pack))
    gl.update(norm_even=tot[0:2], hgrn_norm=tot[2].reshape(2, W_B), mem_norm=tot[3], final_norm=tot[4],
              sink=tot[5, 0:16].reshape(2, N_Q_A))
    loss = tot[5, 16]

    upd = {}
    for n in WEIGHT_NAMES:
        if n in big:
            gl[n], *upd[n] = adamw_halves(w[n], *big[n], m[n], v[n], ci)
        else:
            upd[n] = adamw_call(w[n], gl[n], m[n], v[n])
    return (loss, dx[None], *[gl[n] for n in WEIGHT_NAMES], *[upd[n][0] for n in WEIGHT_NAMES],
            *[upd[n][1] for n in WEIGHT_NAMES], *[upd[n][2] for n in WEIGHT_NAMES])
```

```python
import functools

import numpy as np
import jax
import jax.numpy as jnp
from jax import lax
from jax.experimental import pallas as pl
from jax.experimental.pallas import tpu as pltpu
from jax.experimental.pallas import tpu_sc as plsc

F32 = jnp.float32
BF16 = jnp.bfloat16

D_MODEL = 1024
DEPTH = 4
N_Q_A, N_KV_A, HEAD_DIM_A = 8, 2, 64
W_A, W_KV_A = 512, 128
WINDOW = 128
BLOCK = 128
N_HEADS_B, HEAD_DIM_B, W_B = 4, 128, 512
N_HEADS_C, DK_C, DV_C, WK_C, WV_C = 4, 128, 256, 512, 1024
GATE_RANK = 16
GATE_TEMP = 16.0
N_MEM, N_HEADS_M, HEAD_DIM_M, W_M = 256, 4, 128, 512
EPS = 1e-6
MASK_VALUE = -1e30
MIN_GATE = 1e-30
EVEN_IN, ODD_IN = 4864, 4128
ODD_PAD = 4224
MIX = 1536
ADAM_LR, ADAM_B1, ADAM_B2, ADAM_EPS, ADAM_WD, ADAM_STEP = 0.001, 0.9, 0.999, 1e-08, 0.01, 10

SCAN_CHUNK = 128
SCAN_LEVELS = 7
VMEM_LIMIT = 56 * 1024 * 1024

EVEN_REF_OFF = dict(qA=0, kA=512, vA=640, gA=768, qB=1280, zf=1792, zb=2304, iB=2816, gB=3328, qM=3840, gM=4352)
EVEN_W = dict(qA=512, kA=128, vA=128, gA=512, qB=512, zf=512, zb=512, iB=512, gB=512, qM=512, gM=512)
EVEN_ORDER = ("qA", "gA", "qB", "zf", "zb", "iB", "gB", "qM", "gM", "kA", "vA")
ODD_REF_OFF = dict(qC=0, kC=512, vC=1024, gC=2048, rr=3072, qM=3104, gM=3616)
ODD_W = dict(qC=512, kC=512, vC=1024, gC=1024, rr=32, qM=512, gM=512)
ODD_ORDER = ("qC", "kC", "vC", "gC", "qM", "gM", "rr")


def _offsets(order, widths):
    off, o = {}, 0
    for n in order:
        off[n] = o
        o += widths[n]
    return off


EVEN_OFF = _offsets(EVEN_ORDER, EVEN_W)
ODD_OFF = _offsets(ODD_ORDER, ODD_W)


def _dg(a, b, ca, cb):
    return lax.dot_general(a.astype(BF16), b.astype(BF16), (((ca,), (cb,)), ((), ())),
                           preferred_element_type=F32)


def dot_nn(a, b):
    return _dg(a, b, 1, 0)


def dot_nt(a, b):
    return _dg(a, b, 1, 1)


def dot_tn(a, b):
    return _dg(a, b, 0, 0)


@jax.custom_vjp
def bdot(a, b):
    return dot_nn(a, b)


bdot.defvjp(lambda a, b: (dot_nn(a, b), (a, b)),
            lambda r, g: (dot_nt(g, r[1]), dot_tn(r[0], g)))


@jax.custom_vjp
def bdot_t(a, b):
    return dot_nt(a, b)


bdot_t.defvjp(lambda a, b: (dot_nt(a, b), (a, b)),
              lambda r, g: (dot_nn(g, r[1]), dot_tn(g, r[0])))


@jax.custom_vjp
def bdot_tn(a, b):
    return dot_tn(a, b)


bdot_tn.defvjp(lambda a, b: (dot_tn(a, b), (a, b)),
               lambda r, g: (dot_nt(r[1], g), dot_nn(r[0], g)))


def _split_mm(h, x):
    hi = x.astype(BF16)
    lo = (x - hi.astype(F32)).astype(BF16)
    return (lax.dot_general(h, hi, (((1,), (0,)), ((), ())), preferred_element_type=F32)
            + lax.dot_general(h, lo, (((1,), (0,)), ((), ())), preferred_element_type=F32))


def _sigmoid(z):
    return 1.0 / (1.0 + jnp.exp(-z))


def _silu(z):
    return z * _sigmoid(z)


def _log_sigmoid(z):
    return jnp.minimum(z, 0.0) - jnp.log(1.0 + jnp.exp(-jnp.abs(z)))


def _rms(x, g):
    return x * lax.rsqrt(jnp.mean(x * x, axis=-1, keepdims=True) + EPS) * g


def rms_tile(x, g):
    return (_rms(x, g),)


@functools.partial(jax.custom_vjp, nondiff_argnums=(1, 2))
def split(x, n, axis):
    w = x.shape[axis] // n
    return tuple(lax.slice_in_dim(x, h * w, (h + 1) * w, axis=axis) for h in range(n))


split.defvjp(lambda x, n, axis: (split(x, n, axis), None),
             lambda n, axis, _, cts: (jnp.concatenate(cts, axis=axis),))


def _group_rms(o, g, heads):
    return jnp.concatenate([_rms(oh, gh) for oh, gh in zip(split(o, heads, 1), split(g, heads, 1))], axis=-1)


def even_post_tile(a, o2f, o2b, mo, gA, gB, gM, hg):
    y = _group_rms(o2f + o2b, hg, N_HEADS_B)
    return (jnp.concatenate([a * _silu(gA), y * _silu(gB), mo * _silu(gM)], axis=-1),)


def odd_post_tile(o2f, o2b, mo, gC, gM, gg):
    y = _group_rms(o2f + o2b, gg, N_HEADS_C)
    return (jnp.concatenate([y * _silu(gC), mo * _silu(gM)], axis=-1),)


def hgrn_prep(raw, par):
    qB, z, iB = raw
    (lb,) = par
    f = lb + (1.0 - lb) * _sigmoid(z)
    return _silu(qB), (1.0 - lb) * _sigmoid(-z), iB, jnp.log(jnp.maximum(f, MIN_GATE))


def gla_prep(raw, par):
    qC, kC, vC, r128 = raw
    wup, bg = par
    return qC * (DK_C ** -0.5), kC, vC, _log_sigmoid(bdot(r128, wup) + bg) / GATE_TEMP


def mem_tile(q, k, v):
    s = bdot_t(q, k) * (HEAD_DIM_M ** -0.5)
    m = lax.stop_gradient(jnp.max(s, axis=-1, keepdims=True))
    p = jnp.exp(s - m)
    p = p / jnp.sum(p, axis=-1, keepdims=True)
    return (bdot(p, v),)


def attn_block(qs, ks, vs, sinks, slopes, c, seq):
    i = lax.broadcasted_iota(jnp.int32, (BLOCK, 3 * BLOCK), 0)
    j = lax.broadcasted_iota(jnp.int32, (BLOCK, 3 * BLOCK), 1)
    dist = jnp.abs(i - j + BLOCK).astype(F32)
    kpos = (c - 1) * BLOCK + j
    valid = (dist <= WINDOW) & (kpos >= 0) & (kpos < seq)
    outs = []
    for q, sk, slope in zip(qs, sinks, slopes):
        s = bdot_t(q, ks) * (HEAD_DIM_A ** -0.5)
        s = jnp.where(valid, s - slope * dist, MASK_VALUE)
        m = lax.stop_gradient(jnp.maximum(jnp.max(s, axis=-1, keepdims=True), sk))
        p = jnp.where(valid, jnp.exp(s - m), 0.0)
        denom = jnp.sum(p, axis=-1, keepdims=True) + jnp.exp(sk - m)
        outs.append(bdot(p, vs) / denom)
    return tuple(outs)


def scan_chunk(q, k, v, e, tot, st, qm, km, bm):
    C = SCAN_CHUNK
    e = split(e, 2 + SCAN_LEVELS, 0)
    qe = q * jnp.exp(e[0])
    kd = k * jnp.exp(e[1])
    r = lax.broadcasted_iota(jnp.int32, (C, C), 0)
    s = lax.broadcasted_iota(jnp.int32, (C, C), 1)
    a = jnp.where(r == s, jnp.sum(q * k, axis=-1, keepdims=True), 0.0)
    for l in range(SCAN_LEVELS):
        el = jnp.exp(e[2 + l])
        qs = q * el * qm[l * C:(l + 1) * C]
        ks = k * el * km[l * C:(l + 1) * C]
        a = a + bdot_t(qs, ks) * bm[l * C:(l + 1) * C]
    o = bdot_t(qe, st) + bdot(a, v)
    st_new = st * jnp.exp(tot) + bdot_tn(v, kd)
    return o, st_new


def _scan_consts():
    C, L = SCAN_CHUNK, SCAN_LEVELS
    t = np.arange(C)[:, None]
    r = np.arange(C)[None, :]
    blocks = [(r <= t), (r > t)]
    qms, kms, bms = [], [], []
    for l in range(1, L + 1):
        m = C >> l
        upper_t = (t % (2 * m)) >= m
        same_half = (t // m) == (r // m)
        blocks.append(same_half & np.where(upper_t, r <= t, r > t))
        qms.append(np.broadcast_to(upper_t, (C, C)))
        kms.append(np.broadcast_to(~upper_t, (C, C)))
        bms.append((t // (2 * m)) == (r // (2 * m)))
    hf = np.concatenate(blocks, axis=0).astype(np.float32)
    flip = lambda mat: mat.reshape(-1, C, C)[:, ::-1, ::-1].reshape(-1, C)
    hb = flip(hf)
    qmf = np.concatenate(qms, axis=0).astype(np.float32)
    kmf = np.concatenate(kms, axis=0).astype(np.float32)
    bm = np.concatenate(bms, axis=0).astype(np.float32)
    h = np.stack([hf, hb])
    ht = np.stack([hf.T, hb.T])
    qm = np.stack([qmf, kmf])
    km = np.stack([kmf, qmf])
    return h, ht, qm, km, bm


def _cparams(sem):
    return pltpu.CompilerParams(dimension_semantics=sem, vmem_limit_bytes=VMEM_LIMIT)


def _row_tile(T):
    return min(T, 256)


def _in_spec(spec, tr):
    kind = spec[0]
    if kind == "row":
        _, arr, off, w = spec
        assert off % w == 0
        return arr, pl.BlockSpec((tr, w), functools.partial(lambda i, b: (i, b), b=off // w))
    if kind == "row3":
        _, arr, d, off, w = spec
        assert off % w == 0
        return arr, pl.BlockSpec((None, tr, w), functools.partial(lambda i, d, b: (d, i, b), d=d, b=off // w))
    _, arr = spec
    return arr, pl.BlockSpec(arr.shape, functools.partial(lambda i, n: (0,) * n, n=arr.ndim))


def rows_call(name, tile_fn, T, ins, out_widths, out_dtypes=None, stacks=None):
    tr = _row_tile(T)
    n_in = len(ins)
    out_dtypes = out_dtypes or [F32] * len(out_widths)
    stacks = stacks or [(k,) for k in range(len(out_widths))]

    def body(*refs):
        vals = [r[...] for r in refs[:n_in]]
        outs = tile_fn(*vals)
        for r, members in zip(refs[n_in:], stacks):
            if len(members) == 1:
                r[...] = outs[members[0]].astype(r.dtype)
            else:
                for d, k in enumerate(members):
                    r[d] = outs[k].astype(r.dtype)

    in_specs, args = [], []
    for spec in ins:
        arr, bs = _in_spec(spec, tr)
        args.append(arr)
        in_specs.append(bs)
    out_specs, out_shape = [], []
    for w, dt, members in zip(out_widths, out_dtypes, stacks):
        n = len(members)
        if n == 1:
            out_specs.append(pl.BlockSpec((tr, w), lambda i: (i, 0)))
            out_shape.append(jax.ShapeDtypeStruct((T, w), dt))
        else:
            out_specs.append(pl.BlockSpec((n, tr, w), lambda i: (0, i, 0)))
            out_shape.append(jax.ShapeDtypeStruct((n, T, w), dt))
    return pl.pallas_call(body, out_shape=out_shape, grid=(T // tr,), in_specs=in_specs, out_specs=out_specs,
                          name=name, compiler_params=_cparams(("arbitrary",)))(*args)


def rows_vjp_call(name, tile_fn, T, ins, cts, skip=()):
    tr = _row_tile(T)
    n_in = len(ins)
    n_ct = [len(c) for c in cts]
    want = [k for k in range(n_in) if k not in skip]

    def body(*refs):
        i = pl.program_id(0)
        vals = [r[...] for r in refs[:n_in]]
        ct, pos = [], n_in
        for n in n_ct:
            acc = refs[pos][...]
            for r in refs[pos + 1:pos + n]:
                acc = acc + r[...]
            ct.append(acc)
            pos += n
        _, vjp = jax.vjp(tile_fn, *vals)
        grads = vjp(tuple(ct))
        for r, k in zip(refs[pos:], want):
            if ins[k][0] == "full":
                @pl.when(i == 0)
                def _():
                    r[...] = jnp.zeros_like(r)
                r[...] += grads[k]
            else:
                r[...] = grads[k]

    in_specs, args = [], []
    for spec in list(ins) + [s for c in cts for s in c]:
        arr, bs = _in_spec(spec, tr)
        args.append(arr)
        in_specs.append(bs)
    out_specs, out_shape = [], []
    for k in want:
        if ins[k][0] == "full":
            arr = ins[k][1]
            out_specs.append(pl.BlockSpec(arr.shape, functools.partial(lambda i, n: (0,) * n, n=arr.ndim)))
            out_shape.append(jax.ShapeDtypeStruct(arr.shape, F32))
        else:
            w = ins[k][-1]
            out_specs.append(pl.BlockSpec((tr, w), lambda i: (i, 0)))
            out_shape.append(jax.ShapeDtypeStruct((T, w), F32))
    return pl.pallas_call(body, out_shape=out_shape, grid=(T // tr,), in_specs=in_specs, out_specs=out_specs,
                          name=name, compiler_params=_cparams(("arbitrary",)))(*args)


def matmul(name, a, b, mode, add=None, out_dtype=F32):
    if mode == "tn":
        K, M = a.shape
        N = b.shape[1]
        tm = M if M <= 1536 else 512
        tn = N if N <= 1280 else (N // 2 if (N // 2) % 128 == 0 else N)
        tk = min(K, 512)
        grid = (M // tm, N // tn, K // tk)

        def body(a_ref, b_ref, o_ref):
            @pl.when(pl.program_id(2) == 0)
            def _():
                o_ref[...] = jnp.zeros_like(o_ref)
            o_ref[...] += dot_tn(a_ref[...], b_ref[...])

        return pl.pallas_call(
            body, out_shape=jax.ShapeDtypeStruct((M, N), F32), grid=grid,
            in_specs=[pl.BlockSpec((tk, tm), lambda i, j, k: (k, i)), pl.BlockSpec((tk, tn), lambda i, j, k: (k, j))],
            out_specs=pl.BlockSpec((tm, tn), lambda i, j, k: (i, j)), name=name,
            compiler_params=_cparams(("arbitrary", "arbitrary", "arbitrary")))(a, b)

    M, K = a.shape
    N = b.shape[1] if mode == "nn" else b.shape[0]
    tm = min(M, 512)
    tn = N if N <= 1536 else (N // 2 if (N // 2) % 128 == 0 else (N // 3 if (N // 3) % 128 == 0 else N))
    grid = (N // tn, M // tm)
    n_in = 2 + (add is not None)

    def body(*refs):
        a_ref, b_ref = refs[0], refs[1]
        o_ref = refs[n_in]
        acc = dot_nn(a_ref[...], b_ref[...]) if mode == "nn" else dot_nt(a_ref[...], b_ref[...])
        if add is not None:
            acc = acc + refs[2][...]
        o_ref[...] = acc.astype(o_ref.dtype)

    in_specs = [pl.BlockSpec((tm, K), lambda j, i: (i, 0)),
                pl.BlockSpec((K, tn), lambda j, i: (0, j)) if mode == "nn" else pl.BlockSpec((tn, K), lambda j, i: (j, 0))]
    args = [a, b]
    if add is not None:
        in_specs.append(pl.BlockSpec((tm, tn), lambda j, i: (i, j)))
        args.append(add)
    return pl.pallas_call(
        body, out_shape=jax.ShapeDtypeStruct((M, N), out_dtype), grid=grid, in_specs=in_specs,
        out_specs=pl.BlockSpec((tm, tn), lambda j, i: (i, j)), name=name,
        compiler_params=_cparams(("arbitrary", "arbitrary")))(*args)


def _attn_heads(n):
    G = N_Q_A // N_KV_A
    k_sl = pl.ds(n * HEAD_DIM_A, HEAD_DIM_A)
    v_sl = pl.ds(W_KV_A + n * HEAD_DIM_A, HEAD_DIM_A)
    q_sl = [pl.ds((n * G + g) * HEAD_DIM_A, HEAD_DIM_A) for g in range(G)]
    return k_sl, v_sl, q_sl, range(n * G, (n + 1) * G)


def attn_fwd(p, q_off, kvp, sink, slopes, T):
    nb = T // BLOCK
    assert q_off % W_A == 0

    def body(q_ref, kv_ref, sink_ref, slope_ref, o_ref):
        c = pl.program_id(0)
        rows = pl.ds(pl.multiple_of(c * BLOCK, BLOCK), 3 * BLOCK)
        for n in range(N_KV_A):
            k_sl, v_sl, q_sl, heads = _attn_heads(n)
            outs = attn_block([q_ref[:, s] for s in q_sl], kv_ref[rows, k_sl], kv_ref[rows, v_sl],
                              [sink_ref[h] for h in heads], [slope_ref[h] for h in heads], c, T)
            for s, o in zip(q_sl, outs):
                o_ref[:, s] = o

    full = lambda a: pl.BlockSpec(a.shape, functools.partial(lambda c, nd: (0,) * nd, nd=a.ndim))
    return pl.pallas_call(
        body, out_shape=jax.ShapeDtypeStruct((T, W_A), F32), grid=(nb,),
        in_specs=[pl.BlockSpec((BLOCK, W_A), lambda c: (c, q_off // W_A)), full(kvp), full(sink), full(slopes)],
        out_specs=pl.BlockSpec((BLOCK, W_A), lambda c: (c, 0)),
        name="attn_fwd", compiler_params=_cparams(("arbitrary",)))(p, kvp, sink, slopes)


def attn_bwd(p, q_off, kvp, sink, slopes, do, T):
    nb = T // BLOCK

    def body(q_ref, kv_ref, sink_ref, slope_ref, do_ref, dq_ref, dkv_ref, dsink_ref):
        c = pl.program_id(0)
        rows = pl.ds(pl.multiple_of(c * BLOCK, BLOCK), 3 * BLOCK)

        @pl.when(c == 0)
        def _():
            dkv_ref[...] = jnp.zeros_like(dkv_ref)
            dsink_ref[...] = jnp.zeros_like(dsink_ref)

        for n in range(N_KV_A):
            k_sl, v_sl, q_sl, heads = _attn_heads(n)
            slopes_n = [slope_ref[h] for h in heads]
            _, vjp = jax.vjp(lambda qs, kk, vv, sks: attn_block(qs, kk, vv, sks, slopes_n, c, T),
                             [q_ref[:, s] for s in q_sl], kv_ref[rows, k_sl], kv_ref[rows, v_sl],
                             [sink_ref[h] for h in heads])
            dqs, dks, dvs, dsks = vjp(tuple(do_ref[:, s] for s in q_sl))
            dkv_ref[rows, k_sl] += dks
            dkv_ref[rows, v_sl] += dvs
            for s, h, dq, dsk in zip(q_sl, heads, dqs, dsks):
                dq_ref[:, s] = dq
                dsink_ref[h] += dsk

    full = lambda a: pl.BlockSpec(a.shape, functools.partial(lambda c, nd: (0,) * nd, nd=a.ndim))
    qspec = pl.BlockSpec((BLOCK, W_A), lambda c: (c, 0))
    return pl.pallas_call(
        body,
        out_shape=[jax.ShapeDtypeStruct((T, W_A), F32), jax.ShapeDtypeStruct(kvp.shape, F32),
                   jax.ShapeDtypeStruct((N_Q_A, 1, 1), F32)],
        grid=(nb,),
        in_specs=[pl.BlockSpec((BLOCK, W_A), lambda c: (c, q_off // W_A)), full(kvp), full(sink), full(slopes), qspec],
        out_specs=[qspec, full(kvp), full(sink)],
        name="attn_bwd", compiler_params=_cparams(("arbitrary",)))(p, kvp, sink, slopes, do)


def mem_fwd(p, q_off, kv, T):
    tr = _row_tile(T)
    assert q_off % W_M == 0

    def body(q_ref, kv_ref, o_ref):
        for h in range(N_HEADS_M):
            hs = pl.ds(h * HEAD_DIM_M, HEAD_DIM_M)
            (o,) = mem_tile(q_ref[:, hs], kv_ref[:, hs], kv_ref[:, pl.ds(W_M + h * HEAD_DIM_M, HEAD_DIM_M)])
            o_ref[:, hs] = o

    return pl.pallas_call(
        body, out_shape=jax.ShapeDtypeStruct((T, W_M), F32), grid=(T // tr,),
        in_specs=[pl.BlockSpec((tr, W_M), lambda i: (i, q_off // W_M)), pl.BlockSpec((N_MEM, 2 * W_M), lambda i: (0, 0))],
        out_specs=pl.BlockSpec((tr, W_M), lambda i: (i, 0)),
        name="mem_fwd", compiler_params=_cparams(("arbitrary",)))(p, kv)


def mem_bwd(p, q_off, kv, do, T):
    tr = _row_tile(T)

    def body(q_ref, kv_ref, do_ref, dq_ref, dkv_ref):
        @pl.when(pl.program_id(0) == 0)
        def _():
            dkv_ref[...] = jnp.zeros_like(dkv_ref)

        for h in range(N_HEADS_M):
            hs = pl.ds(h * HEAD_DIM_M, HEAD_DIM_M)
            vs = pl.ds(W_M + h * HEAD_DIM_M, HEAD_DIM_M)
            _, vjp = jax.vjp(mem_tile, q_ref[:, hs], kv_ref[:, hs], kv_ref[:, vs])
            dq, dk, dv = vjp((do_ref[:, hs],))
            dq_ref[:, hs] = dq
            dkv_ref[:, hs] += dk
            dkv_ref[:, vs] += dv

    kvspec = pl.BlockSpec((N_MEM, 2 * W_M), lambda i: (0, 0))
    return pl.pallas_call(
        body,
        out_shape=[jax.ShapeDtypeStruct((T, W_M), F32), jax.ShapeDtypeStruct((N_MEM, 2 * W_M), F32)],
        grid=(T // tr,),
        in_specs=[pl.BlockSpec((tr, W_M), lambda i: (i, q_off // W_M)), kvspec, pl.BlockSpec((tr, W_M), lambda i: (i, 0))],
        out_specs=[pl.BlockSpec((tr, W_M), lambda i: (i, 0)), kvspec],
        name="mem_bwd", compiler_params=_cparams(("arbitrary",)))(p, kv, do)


def _scan_const_specs(dk):
    C, L = SCAN_CHUNK, SCAN_LEVELS
    return [pl.BlockSpec((2, (2 + L) * C, C), lambda n: (0, 0, 0)),
            pl.BlockSpec((2, C, (2 + L) * C), lambda n: (0, 0, 0)),
            pl.BlockSpec((2, L * C, dk), lambda n: (0, 0, 0)),
            pl.BlockSpec((2, L * C, dk), lambda n: (0, 0, 0)),
            pl.BlockSpec((L * C, C), lambda n: (0, 0))]


def _chunk_spec(src, width, chunk_of):
    arr, sel = src
    if arr.ndim == 2:
        assert sel % width == 0
        return pl.BlockSpec((SCAN_CHUNK, width), functools.partial(lambda n, b: (chunk_of(n), b), b=sel // width))
    return pl.BlockSpec((None, SCAN_CHUNK, width), functools.partial(lambda n, d: (d, chunk_of(n), 0), d=sel))


def _scan_const_args():
    h, ht, qm, km, bm = _scan_consts()
    return [jnp.asarray(h, BF16), jnp.asarray(ht, BF16), jnp.asarray(qm, F32), jnp.asarray(km, F32), jnp.asarray(bm, F32)]


def _full_spec(a):
    return pl.BlockSpec(a.shape, functools.partial(lambda n, nd: (0,) * nd, nd=a.ndim))


def scan_fwd(name, prep, raws, params, heads, dk, dv, T):
    C = SCAN_CHUNK
    N = T // C
    assert dk == C
    Wv = heads * dv
    orders = (lambda n: n, lambda n: N - 1 - n)
    n_raw, n_par = [len(r) for r in raws], [len(p) for p in params]

    def body(*refs):
        pos, raw_refs, par_refs = 0, [], []
        for d in range(2):
            raw_refs.append(refs[pos:pos + n_raw[d]])
            pos += n_raw[d]
        for d in range(2):
            par_refs.append(refs[pos:pos + n_par[d]])
            pos += n_par[d]
        h_ref, ht_ref, qm_ref, km_ref, bm_ref = refs[pos:pos + 5]
        o_refs, ss_refs, st_ref = refs[pos + 5:pos + 7], refs[pos + 7:pos + 9], refs[pos + 9]

        @pl.when(pl.program_id(0) == 0)
        def _():
            st_ref[...] = jnp.zeros_like(st_ref)

        bm = bm_ref[...]
        for d in range(2):
            consts = (qm_ref[d], km_ref[d], bm)
            q, k, v, g = prep([r[...] for r in raw_refs[d]], [p[...] for p in par_refs[d]])
            e = _split_mm(h_ref[d], g)
            tot = jnp.sum(g, axis=0, keepdims=True)
            for h in range(heads):
                ks, vs = slice(h * dk, (h + 1) * dk), slice(h * dv, (h + 1) * dv)
                st = st_ref[d, h]
                ss_refs[d][h] = st
                o, st_new = scan_chunk(q[:, ks], k[:, ks], v[:, vs], e[:, ks], tot[:, ks], st, *consts)
                o_refs[d][:, vs] = o
                st_ref[d, h] = st_new

    ss_spec = lambda order: pl.BlockSpec((heads, None, dv, dk), lambda n: (0, order(n), 0, 0))
    return pl.pallas_call(
        body,
        out_shape=[jax.ShapeDtypeStruct((T, Wv), F32)] * 2 + [jax.ShapeDtypeStruct((heads, N, dv, dk), F32)] * 2,
        grid=(N,),
        in_specs=[_chunk_spec(s, w, orders[d]) for d in range(2) for s, w in raws[d]]
        + [_full_spec(p) for d in range(2) for p in params[d]] + _scan_const_specs(dk),
        out_specs=[pl.BlockSpec((C, Wv), lambda n: (orders[0](n), 0)), pl.BlockSpec((C, Wv), lambda n: (orders[1](n), 0)),
                   ss_spec(orders[0]), ss_spec(orders[1])],
        scratch_shapes=[pltpu.VMEM((2, heads, dv, dk), F32)],
        name=name, compiler_params=_cparams(("arbitrary",)))(
            *[s[0] for d in range(2) for s, _ in raws[d]], *[p for d in range(2) for p in params[d]], *_scan_const_args())


def scan_bwd(name, prep, raws, params, ss, do, heads, dk, dv, T):
    C = SCAN_CHUNK
    N = T // C
    Wv = heads * dv
    orders = (lambda n: N - 1 - n, lambda n: n)
    n_raw, n_par = [len(r) for r in raws], [len(p) for p in params]

    def body(*refs):
        pos, raw_refs, par_refs, draw_refs, dpar_refs = 0, [], [], [], []
        for group, counts in ((raw_refs, n_raw), (par_refs, n_par)):
            for d in range(2):
                group.append(refs[pos:pos + counts[d]])
                pos += counts[d]
        ss_refs, do_refs = refs[pos:pos + 2], refs[pos + 2:pos + 4]
        h_ref, ht_ref, qm_ref, km_ref, bm_ref = refs[pos + 4:pos + 9]
        pos += 9
        for group, counts in ((draw_refs, n_raw), (dpar_refs, n_par)):
            for d in range(2):
                group.append(refs[pos:pos + counts[d]])
                pos += counts[d]
        dst_ref = refs[pos]

        @pl.when(pl.program_id(0) == 0)
        def _():
            dst_ref[...] = jnp.zeros_like(dst_ref)
            for d in range(2):
                for r in dpar_refs[d]:
                    r[...] = jnp.zeros_like(r)

        bm = bm_ref[...]
        for d in range(2):
            consts = (qm_ref[d], km_ref[d], bm)
            (q, k, v, g), prep_vjp = jax.vjp(prep, [r[...] for r in raw_refs[d]], [p[...] for p in par_refs[d]])
            e = _split_mm(h_ref[d], g)
            tot = jnp.sum(g, axis=0, keepdims=True)
            dqs, dks, dvs, des, dtots = [], [], [], [], []
            for h in range(heads):
                ks, vs = slice(h * dk, (h + 1) * dk), slice(h * dv, (h + 1) * dv)
                _, vjp = jax.vjp(lambda q_, k_, v_, e_, t_, st_: scan_chunk(q_, k_, v_, e_, t_, st_, *consts),
                                 q[:, ks], k[:, ks], v[:, vs], e[:, ks], tot[:, ks], ss_refs[d][h])
                dq, dk_, dv_, de, dtot, dst = vjp((do_refs[d][:, vs], dst_ref[d, h]))
                dst_ref[d, h] = dst
                for group, val in ((dqs, dq), (dks, dk_), (dvs, dv_), (des, de), (dtots, dtot)):
                    group.append(val)
            cat = lambda parts: jnp.concatenate(parts, axis=-1)
            dg = _split_mm(ht_ref[d], cat(des)) + cat(dtots)
            draws, dpars = prep_vjp((cat(dqs), cat(dks), cat(dvs), dg))
            for r, val in zip(draw_refs[d], draws):
                r[...] = val
            for r, val in zip(dpar_refs[d], dpars):
                r[...] += val

    ss_spec = lambda order: pl.BlockSpec((heads, None, dv, dk), lambda n: (0, order(n), 0, 0))
    row_out = lambda w, order: pl.BlockSpec((C, w), lambda n: (order(n), 0))
    return pl.pallas_call(
        body,
        out_shape=[jax.ShapeDtypeStruct((T, w), F32) for d in range(2) for _, w in raws[d]]
        + [jax.ShapeDtypeStruct(p.shape, F32) for d in range(2) for p in params[d]],
        grid=(N,),
        in_specs=[_chunk_spec(s, w, orders[d]) for d in range(2) for s, w in raws[d]]
        + [_full_spec(p) for d in range(2) for p in params[d]]
        + [ss_spec(orders[0]), ss_spec(orders[1]), _chunk_spec(do, Wv, orders[0]), _chunk_spec(do, Wv, orders[1])]
        + _scan_const_specs(dk),
        out_specs=[row_out(w, orders[d]) for d in range(2) for _, w in raws[d]]
        + [_full_spec(p) for d in range(2) for p in params[d]],
        scratch_shapes=[pltpu.VMEM((2, heads, dv, dk), F32)],
        name=name, compiler_params=_cparams(("arbitrary",)))(
            *[s[0] for d in range(2) for s, _ in raws[d]], *[p for d in range(2) for p in params[d]],
            ss[0], ss[1], do[0], do[0], *_scan_const_args())


def final_call(x, g, target, T):
    tr = _row_tile(T)

    def tile(xv, gv, tv):
        y = _rms(xv, gv)
        err = (y - tv) ** 2
        return jnp.sum(jnp.sum(err, axis=-1, keepdims=True), axis=0, keepdims=True) * (0.5 / D_MODEL)

    def body(x_ref, g_ref, t_ref, loss_ref, dx_ref, dg_ref):
        i = pl.program_id(0)
        tv = t_ref[...]
        lv, vjp = jax.vjp(lambda a, b: tile(a, b, tv), x_ref[...], g_ref[...])
        dx, dg = vjp(jnp.ones((1, 1), F32))
        dx_ref[...] = dx

        @pl.when(i == 0)
        def _():
            loss_ref[...] = jnp.zeros_like(loss_ref)
            dg_ref[...] = jnp.zeros_like(dg_ref)

        loss_ref[...] += jnp.broadcast_to(lv, loss_ref.shape)
        dg_ref[...] += dg

    return pl.pallas_call(
        body,
        out_shape=[jax.ShapeDtypeStruct((8, 128), F32), jax.ShapeDtypeStruct((T, D_MODEL), F32),
                   jax.ShapeDtypeStruct((1, D_MODEL), F32)],
        grid=(T // tr,),
        in_specs=[pl.BlockSpec((tr, D_MODEL), lambda i: (i, 0)), pl.BlockSpec((1, D_MODEL), lambda i: (0, 0)),
                  pl.BlockSpec((tr, D_MODEL), lambda i: (i, 0))],
        out_specs=[pl.BlockSpec((8, 128), lambda i: (0, 0)), pl.BlockSpec((tr, D_MODEL), lambda i: (i, 0)),
                   pl.BlockSpec((1, D_MODEL), lambda i: (0, 0))],
        name="final_loss", compiler_params=_cparams(("arbitrary",)))(x, g, target)


def adamw_call(w, g, m, v):
    shape = w.shape
    c = shape[-1]
    r = int(np.prod(shape[:-1])) if len(shape) > 1 else 1
    tr = r if r <= 256 else 256
    assert r % tr == 0

    def body(w_ref, g_ref, m_ref, v_ref, d_ref, nm_ref, nv_ref):
        gv = g_ref[...]
        nm = ADAM_B1 * m_ref[...] + (1.0 - ADAM_B1) * gv
        nv = ADAM_B2 * v_ref[...] + (1.0 - ADAM_B2) * jnp.square(gv)
        m_hat = nm / (1.0 - ADAM_B1 ** ADAM_STEP)
        v_hat = nv / (1.0 - ADAM_B2 ** ADAM_STEP)
        d_ref[...] = -ADAM_LR * (m_hat / (jnp.sqrt(v_hat) + ADAM_EPS) + ADAM_WD * w_ref[...])
        nm_ref[...] = nm
        nv_ref[...] = nv

    spec = pl.BlockSpec((tr, c), lambda i: (i, 0))
    outs = pl.pallas_call(body, out_shape=[jax.ShapeDtypeStruct((r, c), F32)] * 3, grid=(r // tr,),
                          in_specs=[spec] * 4, out_specs=[spec] * 3, name="adamw",
                          compiler_params=_cparams(("arbitrary",)))(*(t.reshape(r, c) for t in (w, g, m, v)))
    return tuple(o.reshape(shape) for o in outs)


def adamw_halves(w, mine, other, m, v, c):
    L, R, C = w.shape
    rh = R // 2
    tr = rh if rh <= 256 else 256
    nbh = rh // tr

    def body(c_ref, w_ref, a_ref, b_ref, m_ref, v_ref, g_ref, d_ref, nm_ref, nv_ref):
        is_mine = (pl.program_id(1) // nbh) == c_ref[0]
        gv = jnp.where(is_mine, a_ref[...], b_ref[...])
        nm = ADAM_B1 * m_ref[...] + (1.0 - ADAM_B1) * gv
        nv = ADAM_B2 * v_ref[...] + (1.0 - ADAM_B2) * jnp.square(gv)
        m_hat = nm / (1.0 - ADAM_B1 ** ADAM_STEP)
        v_hat = nv / (1.0 - ADAM_B2 ** ADAM_STEP)
        g_ref[...] = gv
        d_ref[...] = -ADAM_LR * (m_hat / (jnp.sqrt(v_hat) + ADAM_EPS) + ADAM_WD * w_ref[...])
        nm_ref[...] = nm
        nv_ref[...] = nv

    full = pl.BlockSpec((None, tr, C), lambda l, i, c_ref: (l, i, 0))
    half = pl.BlockSpec((None, tr, C), lambda l, i, c_ref: (l, i % nbh, 0))
    grid_spec = pltpu.PrefetchScalarGridSpec(num_scalar_prefetch=1, grid=(L, R // tr),
                                             in_specs=[full, half, half, full, full], out_specs=[full] * 4)
    return pl.pallas_call(body, out_shape=[jax.ShapeDtypeStruct(w.shape, F32)] * 4, grid_spec=grid_spec,
                          name="adamw_halves", compiler_params=_cparams(("arbitrary", "arbitrary")))(c, w, mine, other, m, v)


def sum_devices(g64):
    def body(x_ref, o_ref):
        acc = x_ref[0:8, :]
        for d in range(1, 8):
            acc = acc + x_ref[8 * d:8 * d + 8, :]
        o_ref[...] = acc

    return pl.pallas_call(body, out_shape=jax.ShapeDtypeStruct((8, D_MODEL), F32), name="sum_devices")(g64)


def _half_tile(rh):
    return rh if rh <= 512 else 256


def add_sibling(g, recv, c, out_dtype):
    _, R, C = g.shape
    rh = R // 2
    tr = _half_tile(rh)
    nblk = rh // tr

    def body(c_ref, g_ref, r_ref, o_ref):
        o_ref[...] = (g_ref[...] + r_ref[...]).astype(o_ref.dtype)

    grid_spec = pltpu.PrefetchScalarGridSpec(
        num_scalar_prefetch=1, grid=(4, nblk),
        in_specs=[pl.BlockSpec((None, tr, C), lambda j, i, c_ref: (j, i + c_ref[0] * nblk, 0)),
                  pl.BlockSpec((None, tr, C), lambda j, i, c_ref: (j, i, 0))],
        out_specs=pl.BlockSpec((None, tr, C), lambda j, i, c_ref: (j, i, 0)))
    return pl.pallas_call(body, out_shape=jax.ShapeDtypeStruct((4, rh, C), out_dtype), grid_spec=grid_spec,
                          name="rs_add_sibling", compiler_params=_cparams(("arbitrary", "arbitrary")))(c, g, recv)


def add_chips(g, recv, r3, place):
    _, R, C = g.shape
    rh = R // 2
    tr = _half_tile(rh)
    nblk = rh // tr

    def body(p_ref, g_ref, s_ref, a_ref, b_ref, c_ref, o_ref):
        up = lambda r: r[...].astype(F32)
        o_ref[...] = (((g_ref[...] + up(s_ref)) + up(a_ref)) + up(b_ref)) + up(c_ref)

    grid_spec = pltpu.PrefetchScalarGridSpec(
        num_scalar_prefetch=1, grid=(nblk,),
        in_specs=[pl.BlockSpec((None, tr, C), lambda i, p_ref: (p_ref[0], i + p_ref[1] * nblk, 0)),
                  pl.BlockSpec((None, tr, C), lambda i, p_ref: (p_ref[0], i, 0))]
        + [pl.BlockSpec((None, tr, C), functools.partial(lambda i, p_ref, k: (k, i, 0), k=k)) for k in range(3)],
        out_specs=pl.BlockSpec((tr, C), lambda i, p_ref: (i, 0)))
    return pl.pallas_call(body, out_shape=jax.ShapeDtypeStruct((rh, C), F32), grid_spec=grid_spec,
                          name="rs_add_chips", compiler_params=_cparams(("arbitrary",)))(place, g, recv, r3, r3, r3)


def _remote(src, dst, ssem, rsem, dev):
    return pltpu.make_async_remote_copy(src_ref=src, dst_ref=dst, send_sem=ssem, recv_sem=rsem,
                                        device_id=dev, device_id_type=pl.DeviceIdType.MESH)


def _mesh_places():
    x, y, c = lax.axis_index("x"), lax.axis_index("y"), lax.axis_index("c")
    chips = [(1 - x, y), (x, 1 - y), (1 - x, 1 - y)]
    return x, y, c, (x, y, 1 - c), chips


def _hbm_specs(n):
    return [pl.BlockSpec(memory_space=pltpu.HBM) for _ in range(n)]


def _gather_body(ins, outs, n_split, send_sems, recv_sems, handshake):
    x, y, c, sibling, chips = _mesh_places()
    mine = 2 * x + y
    if handshake:
        barrier = pltpu.get_barrier_semaphore()
        peers = [sibling] + [(*chip, c) for chip in chips]
        for peer in peers:
            pl.semaphore_signal(barrier, inc=1, device_id=peer, device_id_type=pl.DeviceIdType.MESH)
        pl.semaphore_wait(barrier, len(peers))

    def half(a, chip_idx, which):
        rh = ins[a].shape[0] // 2
        return outs[a].at[chip_idx, pl.ds(which * rh, rh), :]

    sent = []
    for a in range(len(ins)):
        for k, chip in enumerate(chips):
            if a < n_split:
                rh = ins[a].shape[0] // 2
                src, dst = ins[a].at[pl.ds(c * rh, rh), :], half(a, mine, c)
            else:
                src, dst = ins[a], outs[a].at[mine]
            sent.append(_remote(src, dst, send_sems.at[a, k], recv_sems.at[a, k], (*chip, c)))
    for cp in sent:
        cp.start()
    for a in range(len(ins)):
        for k, chip in enumerate(chips):
            j = 2 * chip[0] + chip[1]
            region = half(a, j, c) if a < n_split else outs[a].at[j]
            _remote(region, region, send_sems.at[a, k], recv_sems.at[a, k], (*chip, c)).wait_recv()
            if a < n_split:
                fwd = _remote(region, region, send_sems.at[a, 3 + k], recv_sems.at[a, 3 + k], sibling)
                fwd.start()
                sent.append(fwd)
    for a in range(n_split):
        for k, chip in enumerate(chips):
            region = half(a, 2 * chip[0] + chip[1], 1 - c)
            _remote(region, region, send_sems.at[a, 3 + k], recv_sems.at[a, 3 + k], sibling).wait_recv()
    for cp in sent:
        cp.wait_send()


def gather_weights(shards, small):
    arrs = list(shards) + [small]
    n = len(arrs)

    def body(*refs):
        _gather_body(refs[:n], refs[n:2 * n], n - 1, refs[2 * n], refs[2 * n + 1], handshake=False)

    return pl.pallas_call(
        body, out_shape=[jax.ShapeDtypeStruct((4,) + a.shape, a.dtype) for a in arrs],
        in_specs=_hbm_specs(n), out_specs=_hbm_specs(n),
        scratch_shapes=[pltpu.SemaphoreType.DMA((n, 6)), pltpu.SemaphoreType.DMA((n, 6))],
        name="gather_weights")(*arrs)


def gather_weights_async(shards):
    n = len(shards)

    def body(*refs):
        _gather_body(refs[:n], refs[n:2 * n], n, refs[2 * n], refs[2 * n + 1], handshake=True)

    return pl.kernel(
        body, out_type=[jax.ShapeDtypeStruct((4,) + a.shape, a.dtype) for a in shards],
        mesh=plsc.ScalarSubcoreMesh(axis_name="seq", num_cores=1),
        scratch_types=[pltpu.SemaphoreType.DMA((n, 6)), pltpu.SemaphoreType.DMA((n, 6))],
        compiler_params=pltpu.CompilerParams(collective_id=1), name="gather_weights_async")(*shards)


def _sequencer_call(name, body, out_type, sem_shape, collective_id, args):
    return pl.kernel(
        body, out_type=out_type, mesh=plsc.ScalarSubcoreMesh(axis_name="seq", num_cores=1),
        scratch_types=[pltpu.SemaphoreType.DMA(sem_shape), pltpu.SemaphoreType.DMA(sem_shape)],
        compiler_params=pltpu.CompilerParams(collective_id=collective_id), name=name)(*args)


def _handshake(peers):
    barrier = pltpu.get_barrier_semaphore()
    for peer in peers:
        pl.semaphore_signal(barrier, inc=1, device_id=peer, device_id_type=pl.DeviceIdType.MESH)
    pl.semaphore_wait(barrier, len(peers))


def exchange_siblings(name, srcs, halves, collective_id):
    n = len(srcs)

    def body(*refs):
        ins, outs = refs[:n], refs[n:2 * n]
        send_sems, recv_sems = refs[2 * n:]
        x, y, c, sibling, chips = _mesh_places()
        _handshake([sibling])
        cps = []
        for a in range(n):
            src = ins[a]
            if halves:
                rh = src.shape[1] // 2
                src = src.at[:, pl.ds((1 - c) * rh, rh), :]
            cps.append(_remote(src, outs[a], send_sems.at[a], recv_sems.at[a], sibling))
        for cp in cps:
            cp.start()
        for cp in cps:
            cp.wait()

    shape = lambda g: (4, g.shape[1] // 2, g.shape[2]) if halves else g.shape
    return _sequencer_call(name, body, [jax.ShapeDtypeStruct(shape(g), g.dtype) for g in srcs], (n,), collective_id, srcs)


def exchange_chips(name, s1s, collective_id):
    n = len(s1s)

    def body(*refs):
        ins, outs = refs[:n], refs[n:2 * n]
        send_sems, recv_sems = refs[2 * n:]
        x, y, c, sibling, chips = _mesh_places()
        _handshake([(*chip, c) for chip in chips])
        cps = []
        for a in range(n):
            for k, chip in enumerate(chips):
                cps.append(_remote(ins[a].at[2 * chip[0] + chip[1]], outs[a].at[k], send_sems.at[a, k],
                                   recv_sems.at[a, k], (*chip, c)))
        for cp in cps:
            cp.start()
        for cp in cps:
            cp.wait()

    return _sequencer_call(name, body, [jax.ShapeDtypeStruct((3,) + s.shape[1:], s.dtype) for s in s1s], (n, 3),
                           collective_id, s1s)


def allgather_small(v):
    m_per = v.shape[0]

    def body(x_ref, out_ref, send_sems, recv_sems, local_sem):
        x, y, c, sibling, chips = _mesh_places()
        me = (x, y, c)

        def rows(px, py, pc):
            return out_ref.at[pl.ds((4 * px + 2 * py + pc) * m_per, m_per), :]

        def copy(k, block, to, src=None):
            return _remote(rows(*block) if src is None else src, rows(*block), send_sems.at[k], recv_sems.at[k], to)

        mine = pltpu.make_async_copy(x_ref, rows(*me), local_sem)
        mine.start()
        first = [copy(0, me, sibling, src=x_ref)]
        first += [copy(1 + j, me, (*chip, c), src=x_ref) for j, chip in enumerate(chips)]
        for cp in first:
            cp.start()
        passed = [copy(4 + j, (*chip, c), sibling) for j, chip in enumerate(chips)]
        for j, chip in enumerate(chips):
            copy(1 + j, (*chip, c), me).wait_recv()
            passed[j].start()
        copy(0, sibling, me).wait_recv()
        for j, chip in enumerate(chips):
            copy(4 + j, (*chip, 1 - c), me).wait_recv()
        for cp in first + passed:
            cp.wait_send()
        mine.wait()

    return pl.pallas_call(
        body, out_shape=jax.ShapeDtypeStruct((8 * m_per, v.shape[1]), v.dtype),
        in_specs=[pl.BlockSpec(memory_space=pltpu.VMEM)], out_specs=pl.BlockSpec(memory_space=pltpu.VMEM),
        scratch_shapes=[pltpu.SemaphoreType.DMA((7,)), pltpu.SemaphoreType.DMA((7,)), pltpu.SemaphoreType.DMA],
        name="allgather_small")(v)


def rms_res_tile(x, g):
    return (_rms(x, g), x)


def _lower_bounds(lb_param):
    lbs = jax.nn.softmax(lb_param.astype(F32), axis=0)
    return jnp.cumsum(lbs, axis=0) - lbs[0]


def _even_fwd(x, i, W, lower, kv, slopes, T):
    O = EVEN_OFF
    g = W["norm_even"][i].reshape(1, D_MODEL)
    (h,) = rows_call("rms_fwd", rms_tile, T, [("row", x, 0, D_MODEL), ("full", g)], [D_MODEL], [BF16])
    p = matmul("mm_in_e", h, W["w_in_e"][i], "nn")
    kvp = jnp.pad(p[:, O["kA"]:O["kA"] + 2 * W_KV_A], ((BLOCK, BLOCK), (0, 0)))
    sink = W["sink"][i].reshape(N_Q_A, 1, 1)
    a = attn_fwd(p, O["qA"], kvp, sink, slopes, T)
    scan_raws = [[((p, O["qB"]), W_B), ((p, O[z]), W_B), ((p, O["iB"]), W_B)] for z in ("zf", "zb")]
    scan_pars = [[lower[i][0:1]], [lower[i][1:2]]]
    o_f, o_b, ss_f, ss_b = scan_fwd("scan_fwd_h", hgrn_prep, scan_raws, scan_pars, N_HEADS_B, HEAD_DIM_B, HEAD_DIM_B, T)
    mo = mem_fwd(p, O["qM"], kv, T)
    hg = W["hgrn_norm"][i].reshape(1, W_B)
    post_ins = [("row", a, 0, W_A), ("row", o_f, 0, W_B), ("row", o_b, 0, W_B), ("row", mo, 0, W_M),
                ("row", p, O["gA"], W_A), ("row", p, O["gB"], W_B), ("row", p, O["gM"], W_M), ("full", hg)]
    (mix,) = rows_call("even_post_fwd", even_post_tile, T, post_ins, [MIX], [BF16])
    x_new = matmul("mm_out", mix, W["w_out_e"][i], "nn", add=x)
    return x_new, dict(x=x, g=g, h=h, p=p, kvp=kvp, sink=sink, scan_raws=scan_raws, scan_pars=scan_pars,
                       ss=(ss_f, ss_b), post_ins=post_ins, mix=mix)


def _assemble_even(dqA, dgA, dqB_f, dqB_b, dzf, dzb, diB_f, diB_b, dgB, dqM, dgM, dkvA):
    return (jnp.concatenate([dqA, dgA, dqB_f + dqB_b, dzf, dzb, diB_f + diB_b, dgB, dqM, dgM, dkvA], axis=-1),)


def _even_bwd(dxo, sv, i, W, kv, slopes, T, sync):
    O = EVEN_OFF
    p = sv["p"]
    dmix = matmul("mm_dmix", dxo, W["w_out_e"][i], "nt")
    dwo = matmul("mm_dwo", sv["mix"], dxo, "tn")
    da, dof, dmo, dgA, dgB, dgM, dhg = rows_vjp_call("even_post_bwd", even_post_tile, T, sv["post_ins"],
                                                      [[("row", dmix, 0, MIX)]], skip=(2,))
    dqA, dkvp, dsink = attn_bwd(p, O["qA"], sv["kvp"], sv["sink"], slopes, da, T)
    dkvA = dkvp[BLOCK:-BLOCK]
    dqB_f, dzf, diB_f, dqB_b, dzb, diB_b, dlow_f, dlow_b = scan_bwd(
        "scan_bwd_h", hgrn_prep, sv["scan_raws"], sv["scan_pars"], sv["ss"], (dof, 0), N_HEADS_B, HEAD_DIM_B, HEAD_DIM_B, T)
    dqB_f = sync(dqB_f)
    row = lambda arr, w: ("row", arr, 0, w)
    dlow = jnp.concatenate([dlow_f, dlow_b], axis=0)
    dqM, dkv = mem_bwd(p, O["qM"], kv, dmo, T)
    (dp,) = rows_call("even_dp", _assemble_even, T,
                      [row(dqA, W_A), row(dgA, W_A), row(dqB_f, W_B), row(dqB_b, W_B), row(dzf, W_B), row(dzb, W_B),
                       row(diB_f, W_B), row(diB_b, W_B), row(dgB, W_B), row(dqM, W_M), row(dgM, W_M),
                       row(dkvA, 2 * W_KV_A)],
                      [EVEN_IN], [BF16])
    dh = matmul("mm_dh_e", dp, W["w_in_e"][i], "nt")
    dwi = matmul("mm_dwi_e", sv["h"], dp, "tn")
    dx, dg = rows_vjp_call("rms_res_bwd", rms_res_tile, T, [("row", sv["x"], 0, D_MODEL), ("full", sv["g"])],
                           [[("row", dh, 0, D_MODEL)], [("row", dxo, 0, D_MODEL)]])
    return dx, dict(w_in=dwi, w_out=dwo, norm=dg[0], sink=dsink.reshape(N_Q_A), low=dlow, hg=dhg[0], kv=dkv)


def _pad_gate_up(w_up):
    z = jnp.zeros((2, 128, WK_C), F32)
    z = z.at[0, 0:GATE_RANK].set(w_up[0])
    return z.at[1, GATE_RANK:2 * GATE_RANK].set(w_up[1])


def _odd_fwd(x, i, W, kv, T):
    O = ODD_OFF
    g = W["norm_odd"][i].reshape(1, D_MODEL)
    (h,) = rows_call("rms_fwd", rms_tile, T, [("row", x, 0, D_MODEL), ("full", g)], [D_MODEL], [BF16])
    p = matmul("mm_in_o", h, W["w_in_o"][i], "nn")
    wup = _pad_gate_up(W["w_gate_up"][i])
    one_dir = [((p, O["qC"]), WK_C), ((p, O["kC"]), WK_C), ((p, O["vC"]), WV_C), ((p, O["rr"]), 128)]
    scan_raws = [one_dir, one_dir]
    scan_pars = [[wup[d], W["b_gate"][i][d:d + 1]] for d in range(2)]
    o_f, o_b, ss_f, ss_b = scan_fwd("scan_fwd_g", gla_prep, scan_raws, scan_pars, N_HEADS_C, DK_C, DV_C, T)
    mo = mem_fwd(p, O["qM"], kv, T)
    gg = W["gla_norm"][i].reshape(1, WV_C)
    post_ins = [("row", o_f, 0, WV_C), ("row", o_b, 0, WV_C), ("row", mo, 0, W_M),
                ("row", p, O["gC"], WV_C), ("row", p, O["gM"], W_M), ("full", gg)]
    (mix,) = rows_call("odd_post_fwd", odd_post_tile, T, post_ins, [MIX], [BF16])
    x_new = matmul("mm_out", mix, W["w_out_o"][i], "nn", add=x)
    return x_new, dict(x=x, g=g, h=h, p=p, scan_raws=scan_raws, scan_pars=scan_pars, ss=(ss_f, ss_b),
                       post_ins=post_ins, mix=mix)


def _assemble_odd(dq0, dq1, dk0, dk1, dv0, dv1, dgC, dqM, dgM, dr0, dr1):
    return (jnp.concatenate([dq0 + dq1, dk0 + dk1, dv0 + dv1, dgC, dqM, dgM, dr0 + dr1], axis=-1),)


def _odd_bwd(dxo, sv, i, W, kv, T, sync):
    O = ODD_OFF
    p = sv["p"]
    dmix = matmul("mm_dmix", dxo, W["w_out_o"][i], "nt")
    dwo = matmul("mm_dwo", sv["mix"], dxo, "tn")
    dof, dmo, dgC, dgM, dgg = rows_vjp_call("odd_post_bwd", odd_post_tile, T, sv["post_ins"],
                                            [[("row", dmix, 0, MIX)]], skip=(1,))
    dqf, dkf, dvf, dr_f, dqb, dkb, dvb, dr_b, dwup_f, dbg_f, dwup_b, dbg_b = scan_bwd(
        "scan_bwd_g", gla_prep, sv["scan_raws"], sv["scan_pars"], sv["ss"], (dof, 0), N_HEADS_C, DK_C, DV_C, T)
    dqf = sync(dqf)
    row = lambda arr, w: ("row", arr, 0, w)
    dqM, dkv = mem_bwd(p, O["qM"], kv, dmo, T)
    (dp,) = rows_call("odd_dp", _assemble_odd, T,
                      [row(dqf, WK_C), row(dqb, WK_C), row(dkf, WK_C), row(dkb, WK_C), row(dvf, WV_C), row(dvb, WV_C),
                       row(dgC, WV_C), row(dqM, W_M), row(dgM, W_M), row(dr_f, 128), row(dr_b, 128)],
                      [ODD_PAD], [BF16])
    dh = matmul("mm_dh_o", dp, W["w_in_o"][i], "nt")
    dwi = matmul("mm_dwi_o", sv["h"], dp, "tn")
    dx, dg = rows_vjp_call("rms_res_bwd", rms_res_tile, T, [("row", sv["x"], 0, D_MODEL), ("full", sv["g"])],
                           [[("row", dh, 0, D_MODEL)], [("row", dxo, 0, D_MODEL)]])
    dw_up = jnp.stack([dwup_f[0:GATE_RANK], dwup_b[GATE_RANK:2 * GATE_RANK]])
    dbg = jnp.concatenate([dbg_f, dbg_b], axis=0)
    return dx, dict(w_in=dwi, w_out=dwo, norm=dg[0], w_up=dw_up, b_gate=dbg, gg=dgg[0], kv=dkv)


def local_step(x, mem, target, W, later=None, on_layer_grads=None, sync=lambda a: a):
    T = x.shape[0]
    slopes = (2.0 ** (-8.0 * jnp.arange(1, N_Q_A + 1, dtype=F32) / N_Q_A)).reshape(N_Q_A, 1, 1)
    lower, lower_vjp = jax.vjp(_lower_bounds, W["lb_param"])
    mem_g = W["mem_norm"].reshape(1, D_MODEL)
    (mem_n,) = rows_call("mem_rms_fwd", rms_tile, N_MEM, [("row", mem, 0, D_MODEL), ("full", mem_g)], [D_MODEL], [BF16])
    kvs, saved = [], []
    for l in range(DEPTH):
        if l == 1 and later is not None:
            x, W = later(x, W)
        kvs.append(matmul("mm_kv", mem_n, W["w_kv"][l], "nn"))
        if l % 2 == 0:
            x, sv = _even_fwd(x, l // 2, W, lower, kvs[l], slopes, T)
        else:
            x, sv = _odd_fwd(x, l // 2, W, kvs[l], T)
        saved.append(sv)
    loss, dx, dgf = final_call(x, W["final_norm"].reshape(1, D_MODEL), target, T)
    per = [None] * DEPTH
    dmem_n = None
    for l in reversed(range(DEPTH)):
        if l % 2 == 0:
            dx, per[l] = _even_bwd(dx, saved[l], l // 2, W, kvs[l], slopes, T, sync)
        else:
            dx, per[l] = _odd_bwd(dx, saved[l], l // 2, W, kvs[l], T, sync)
        per[l]["w_kv"] = matmul("mm_dwkv", mem_n, per[l]["kv"], "tn")
        dmem_n = matmul("mm_dmem", per[l]["kv"], W["w_kv"][l], "nt", add=dmem_n)
        if on_layer_grads is not None:
            dx = on_layer_grads(l, dx, per[l])
    dw_kv = [per[l]["w_kv"] for l in range(DEPTH)]
    (dmem_norm,) = rows_vjp_call("mem_rms_bwd", rms_tile, N_MEM, [("row", mem, 0, D_MODEL), ("full", mem_g)],
                                 [[("row", dmem_n, 0, D_MODEL)]], skip=(0,))
    ev, od = (per[0], per[2]), (per[1], per[3])
    (d_lb,) = lower_vjp(jnp.stack([e["low"] for e in ev]))
    grads = dict(
        w_in_e=jnp.stack([e["w_in"] for e in ev]), w_in_o=jnp.stack([o["w_in"] for o in od]),
        w_out_e=jnp.stack([e["w_out"] for e in ev]), w_out_o=jnp.stack([o["w_out"] for o in od]),
        w_kv=jnp.stack(dw_kv), norm_even=jnp.stack([e["norm"] for e in ev]), sink=jnp.stack([e["sink"] for e in ev]),
        lb_param=d_lb, hgrn_norm=jnp.stack([e["hg"] for e in ev]), norm_odd=jnp.stack([o["norm"] for o in od]),
        w_gate_up=jnp.stack([o["w_up"] for o in od]), b_gate=jnp.stack([o["b_gate"] for o in od]),
        gla_norm=jnp.stack([o["gg"] for o in od]), mem_norm=dmem_norm[0], final_norm=dgf[0])
    return loss, dx, grads


SMALL_SPECS = (("lb_param", (2, 2, 128)), ("norm_odd", (2, 256)), ("w_gate_up", (2, 2, 16, 128)),
               ("b_gate", (2, 2, 128)), ("gla_norm", (2, 256)))
SMALL_ROWS = 80


def _pack_small_local(d):
    return jnp.concatenate([d[n].reshape(-1) for n, _ in SMALL_SPECS]).reshape(SMALL_ROWS, 128)


def _unpack_small_local(b):
    flat, out, o = b.reshape(-1), {}, 0
    for n, shp in SMALL_SPECS:
        sz = int(np.prod(shp))
        out[n] = flat[o:o + sz].reshape(shp)
        o += sz
    return out


def _unpack_small_full(g4):
    per = [_unpack_small_local(g4[j]) for j in range(4)]
    return {n: jnp.concatenate([per[j][n] for j in range(4)], axis=-1) for n, _ in SMALL_SPECS}


def _pack_small_blocks(full):
    blocks = []
    for j in range(4):
        blocks.append(_pack_small_local({n: full[n][..., j * shp[-1]:(j + 1) * shp[-1]] for n, shp in SMALL_SPECS}))
    return jnp.stack(blocks)


def _cols(t, order, off, widths):
    return [t[..., off[n]:off[n] + widths[n]] for n in order]


EVEN_REF_ORDER = ("qA", "kA", "vA", "gA", "qB", "zf", "zb", "iB", "gB", "qM", "gM")
ODD_REF_ORDER = ("qC", "kC", "vC", "gC", "rr", "qM", "gM")


def _layer_weights(l, g_in, g_out, g_kv):
    t = g_in.transpose(1, 0, 2).reshape(D_MODEL, -1)
    if l % 2 == 0:
        w_in = jnp.concatenate(_cols(t, EVEN_ORDER, EVEN_REF_OFF, EVEN_W), axis=-1)
    else:
        w_in = jnp.concatenate(_cols(t, ODD_ORDER, ODD_REF_OFF, ODD_W) + [jnp.zeros((D_MODEL, ODD_PAD - ODD_IN), BF16)],
                               axis=-1)
    return w_in, g_out.reshape(MIX, D_MODEL), g_kv.reshape(D_MODEL, 2 * W_M)


def _layer_grad_blocks(l, gl):
    if l % 2 == 0:
        t = jnp.concatenate(_cols(gl["w_in"], EVEN_REF_ORDER, EVEN_OFF, EVEN_W), axis=-1)
    else:
        t = jnp.concatenate(_cols(gl["w_in"], ODD_REF_ORDER, ODD_OFF, ODD_W), axis=-1)
    b_in = t.reshape(D_MODEL, 4, -1).transpose(1, 0, 2)
    return [b_in, gl["w_out"].reshape(4, MIX // 4, D_MODEL), gl["w_kv"].reshape(4, D_MODEL // 4, 2 * W_M)]


WEIGHT_NAMES = ("norm_even", "w_in_even", "sink", "lb_param", "hgrn_norm", "w_out_even", "norm_odd", "w_in_odd",
                "w_gate_up", "b_gate", "gla_norm", "w_out_odd", "mem_norm", "w_mem_kv", "final_norm")


def kernel(x, mem, norm_even, w_in_even, sink, lb_param, hgrn_norm, w_out_even, norm_odd, w_in_odd, w_gate_up, b_gate, gla_norm, w_out_odd, mem_norm, w_mem_kv, final_norm, loss_target, m_norm_even, m_w_in_even, m_sink, m_lb_param, m_hgrn_norm, m_w_out_even, m_norm_odd, m_w_in_odd, m_w_gate_up, m_b_gate, m_gla_norm, m_w_out_odd, m_mem_norm, m_w_mem_kv, m_final_norm, v_norm_even, v_w_in_even, v_sink, v_lb_param, v_hgrn_norm, v_w_out_even, v_norm_odd, v_w_in_odd, v_w_gate_up, v_b_gate, v_gla_norm, v_w_out_odd, v_mem_norm, v_w_mem_kv, v_final_norm):
    w = dict(zip(WEIGHT_NAMES, (norm_even, w_in_even, sink, lb_param, hgrn_norm, w_out_even, norm_odd, w_in_odd,
                                w_gate_up, b_gate, gla_norm, w_out_odd, mem_norm, w_mem_kv, final_norm)))
    m = dict(zip(WEIGHT_NAMES, (m_norm_even, m_w_in_even, m_sink, m_lb_param, m_hgrn_norm, m_w_out_even, m_norm_odd,
                                m_w_in_odd, m_w_gate_up, m_b_gate, m_gla_norm, m_w_out_odd, m_mem_norm, m_w_mem_kv,
                                m_final_norm)))
    v = dict(zip(WEIGHT_NAMES, (v_norm_even, v_w_in_even, v_sink, v_lb_param, v_hgrn_norm, v_w_out_even, v_norm_odd,
                                v_w_in_odd, v_w_gate_up, v_b_gate, v_gla_norm, v_w_out_odd, v_mem_norm, v_w_mem_kv,
                                v_final_norm)))
    ci = lax.axis_index("c").astype(jnp.int32).reshape(1)
    chip = (2 * lax.axis_index("x") + lax.axis_index("y")).astype(jnp.int32).reshape(1)

    shards = []
    for l in range(DEPTH):
        names = ("w_in_even", "w_out_even") if l % 2 == 0 else ("w_in_odd", "w_out_odd")
        shards.append([w[names[0]][l // 2].astype(BF16), w[names[1]][l // 2].astype(BF16), w_mem_kv[l].astype(BF16)])
    small = _pack_small_local(w)
    own = lambda g, s: lax.dynamic_update_slice(g, s[None], (chip[0], 0, 0))
    first = [own(g, s) for g, s in zip(gather_weights(shards[0], small), shards[0] + [small])]
    later_shards = shards[1] + shards[2] + shards[3]
    later_raw = gather_weights_async(later_shards)
    w0 = _layer_weights(0, *first[0:3])
    W = dict(w_in_e=[w0[0]], w_out_e=[w0[1]], w_kv=[w0[2]])
    W.update(_unpack_small_full(first[3]))
    W.update({n: w[n] for n in ("norm_even", "sink", "hgrn_norm", "mem_norm", "final_norm")})

    def later(x1, W):
        x1, raw = lax.optimization_barrier((x1, list(later_raw)))
        g = [own(a, s) for a, s in zip(raw, later_shards)]
        w1, w2, w3 = (_layer_weights(l, *g[3 * (l - 1):3 * l]) for l in (1, 2, 3))
        W = dict(W)
        W.update(w_in_e=[w0[0], w2[0]], w_in_o=[w1[0], w3[0]], w_out_e=[w0[1], w2[1]], w_out_o=[w1[1], w3[1]],
                 w_kv=[w0[2], w1[2], w2[2], w3[2]])
        return x1, W

    place = jnp.concatenate([chip, ci])

    def start(tag, blocks, wire):
        return dict(tag=tag, blocks=blocks, wire=wire, step=0,
                    recv=exchange_siblings(f"rs_siblings_{tag}", blocks, True, 2))

    def advance(p):
        if p["step"] == 0:
            sums = [add_sibling(g, r, ci, dt) for g, r, dt in zip(p["blocks"], p["recv"], p["wire"])]
            p["recv3"] = exchange_chips(f"rs_chips_{p['tag']}", sums, 3)
        else:
            p["mine"] = [add_chips(g, r, r3, place) for g, r, r3 in zip(p["blocks"], p["recv"], p["recv3"])]
            p["other"] = exchange_siblings(f"rs_final_{p['tag']}", p["mine"], False, 4)
        p["step"] += 1

    pipes, first_layer = [], {}

    def sync(a):
        for p in pipes:
            if p["step"] < 3:
                key = ("recv", "recv3", "other")[p["step"]]
                a, arrived = lax.optimization_barrier((a, list(p[key])))
                p[key] = arrived
                if p["step"] < 2:
                    advance(p)
                else:
                    p["step"] = 3
        return a

    def on_layer_grads(l, dx, gl):
        dx = sync(dx)
        if l == 0:
            first_layer.update(gl)
        else:
            pipes.append(start(f"l{l}", _layer_grad_blocks(l, gl), [BF16] * 3))
        return dx

    loss_tile, dx, grads = local_step(x[0], mem[0], loss_target[0], W, later, on_layer_grads, sync)
    pipes.append(start("l0", _layer_grad_blocks(0, first_layer) + [_pack_small_blocks(grads)], [BF16] * 3 + [F32]))
    while any(p["step"] < 2 for p in pipes):
        for p in pipes:
            if p["step"] < 2:
                advance(p)
    by_layer = {int(p["tag"][1:]): p for p in pipes}
    halves = lambda layers, k: (jnp.stack([by_layer[l]["mine"][k] for l in layers]),
                                jnp.stack([by_layer[l]["other"][k] for l in layers]))
    big = dict(w_in_even=halves((0, 2), 0), w_in_odd=halves((1, 3), 0), w_out_even=halves((0, 2), 1),
               w_out_odd=halves((1, 3), 1), w_mem_kv=halves((0, 1, 2, 3), 2))
    s_mine, s_other = by_layer[0]["mine"][3], by_layer[0]["other"][3]
    g_small = jnp.where(ci[0] == 0, jnp.concatenate([s_mine, s_other]), jnp.concatenate([s_other, s_mine]))
    gl = _unpack_small_local(g_small)

    pack = jnp.zeros((8, D_MODEL), F32)
    pack = pack.at[0:2].set(grads["norm_even"]).at[2].set(grads["hgrn_norm"].reshape(-1))
    pack = pack.at[3].set(grads["mem_norm"]).at[4].set(grads["final_norm"])
    pack = pack.at[5, 0:16].set(grads["sink"].reshape(-1)).at[5, 16].set(loss_tile[0, 0])
    tot = sum_devices(allgather_small(pack))
    gl.update(norm_even=tot[0:2], hgrn_norm=tot[2].reshape(2, W_B), mem_norm=tot[3], final_norm=tot[4],
              sink=tot[5, 0:16].reshape(2, N_Q_A))
    loss = tot[5, 16]

    upd = {}
    for n in WEIGHT_NAMES:
        if n in big:
            gl[n], *upd[n] = adamw_halves(w[n], *big[n], m[n], v[n], ci)
        else:
            upd[n] = adamw_call(w[n], gl[n], m[n], v[n])
    return (loss, dx[None], *[gl[n] for n in WEIGHT_NAMES], *[upd[n][0] for n in WEIGHT_NAMES],
            *[upd[n][1] for n in WEIGHT_NAMES], *[upd[n][2] for n in WEIGHT_NAMES])
```

```python
import functools

import numpy as np
import jax
import jax.numpy as jnp
from jax import lax
from jax.experimental import pallas as pl
from jax.experimental.pallas import tpu as pltpu
from jax.experimental.pallas import tpu_sc as plsc

F32 = jnp.float32
BF16 = jnp.bfloat16

D_MODEL = 1024
DEPTH = 4
N_Q_A, N_KV_A, HEAD_DIM_A = 8, 2, 64
W_A, W_KV_A = 512, 128
WINDOW = 128
BLOCK = 128
N_HEADS_B, HEAD_DIM_B, W_B = 4, 128, 512
N_HEADS_C, DK_C, DV_C, WK_C, WV_C = 4, 128, 256, 512, 1024
GATE_RANK = 16
GATE_TEMP = 16.0
N_MEM, N_HEADS_M, HEAD_DIM_M, W_M = 256, 4, 128, 512
EPS = 1e-6
MASK_VALUE = -1e30
MIN_GATE = 1e-30
EVEN_IN, ODD_IN = 4864, 4128
ODD_PAD = 4224
MIX = 1536
ADAM_LR, ADAM_B1, ADAM_B2, ADAM_EPS, ADAM_WD, ADAM_STEP = 0.001, 0.9, 0.999, 1e-08, 0.01, 10

SCAN_CHUNK = 128
SCAN_LEVELS = 7
VMEM_LIMIT = 56 * 1024 * 1024

EVEN_REF_OFF = dict(qA=0, kA=512, vA=640, gA=768, qB=1280, zf=1792, zb=2304, iB=2816, gB=3328, qM=3840, gM=4352)
EVEN_W = dict(qA=512, kA=128, vA=128, gA=512, qB=512, zf=512, zb=512, iB=512, gB=512, qM=512, gM=512)
EVEN_ORDER = ("qA", "gA", "qB", "zf", "zb", "iB", "gB", "qM", "gM", "kA", "vA")
ODD_REF_OFF = dict(qC=0, kC=512, vC=1024, gC=2048, rr=3072, qM=3104, gM=3616)
ODD_W = dict(qC=512, kC=512, vC=1024, gC=1024, rr=32, qM=512, gM=512)
ODD_ORDER = ("qC", "kC", "vC", "gC", "qM", "gM", "rr")


def _offsets(order, widths):
    off, o = {}, 0
    for n in order:
        off[n] = o
        o += widths[n]
    return off


EVEN_OFF = _offsets(EVEN_ORDER, EVEN_W)
ODD_OFF = _offsets(ODD_ORDER, ODD_W)


def _dg(a, b, ca, cb):
    return lax.dot_general(a.astype(BF16), b.astype(BF16), (((ca,), (cb,)), ((), ())),
                           preferred_element_type=F32)


def dot_nn(a, b):
    return _dg(a, b, 1, 0)


def dot_nt(a, b):
    return _dg(a, b, 1, 1)


def dot_tn(a, b):
    return _dg(a, b, 0, 0)


@jax.custom_vjp
def bdot(a, b):
    return dot_nn(a, b)


bdot.defvjp(lambda a, b: (dot_nn(a, b), (a, b)),
            lambda r, g: (dot_nt(g, r[1]), dot_tn(r[0], g)))


@jax.custom_vjp
def bdot_t(a, b):
    return dot_nt(a, b)


bdot_t.defvjp(lambda a, b: (dot_nt(a, b), (a, b)),
              lambda r, g: (dot_nn(g, r[1]), dot_tn(g, r[0])))


@jax.custom_vjp
def bdot_tn(a, b):
    return dot_tn(a, b)


bdot_tn.defvjp(lambda a, b: (dot_tn(a, b), (a, b)),
               lambda r, g: (dot_nt(r[1], g), dot_nn(r[0], g)))


def _split_mm(h, x):
    hi = x.astype(BF16)
    lo = (x - hi.astype(F32)).astype(BF16)
    return (lax.dot_general(h, hi, (((1,), (0,)), ((), ())), preferred_element_type=F32)
            + lax.dot_general(h, lo, (((1,), (0,)), ((), ())), preferred_element_type=F32))


def _sigmoid(z):
    return 1.0 / (1.0 + jnp.exp(-z))


def _silu(z):
    return z * _sigmoid(z)


def _log_sigmoid(z):
    return jnp.minimum(z, 0.0) - jnp.log(1.0 + jnp.exp(-jnp.abs(z)))


def _rms(x, g):
    return x * lax.rsqrt(jnp.mean(x * x, axis=-1, keepdims=True) + EPS) * g


def rms_tile(x, g):
    return (_rms(x, g),)


@functools.partial(jax.custom_vjp, nondiff_argnums=(1, 2))
def split(x, n, axis):
    w = x.shape[axis] // n
    return tuple(lax.slice_in_dim(x, h * w, (h + 1) * w, axis=axis) for h in range(n))


split.defvjp(lambda x, n, axis: (split(x, n, axis), None),
             lambda n, axis, _, cts: (jnp.concatenate(cts, axis=axis),))


def _group_rms(o, g, heads):
    return jnp.concatenate([_rms(oh, gh) for oh, gh in zip(split(o, heads, 1), split(g, heads, 1))], axis=-1)


def even_post_tile(a, o2f, o2b, mo, gA, gB, gM, hg):
    y = _group_rms(o2f + o2b, hg, N_HEADS_B)
    return (jnp.concatenate([a * _silu(gA), y * _silu(gB), mo * _silu(gM)], axis=-1),)


def odd_post_tile(o2f, o2b, mo, gC, gM, gg):
    y = _group_rms(o2f + o2b, gg, N_HEADS_C)
    return (jnp.concatenate([y * _silu(gC), mo * _silu(gM)], axis=-1),)


def hgrn_prep(raw, par):
    qB, z, iB = raw
    (lb,) = par
    f = lb + (1.0 - lb) * _sigmoid(z)
    return _silu(qB), (1.0 - lb) * _sigmoid(-z), iB, jnp.log(jnp.maximum(f, MIN_GATE))


def gla_prep(raw, par):
    qC, kC, vC, r128 = raw
    wup, bg = par
    return qC * (DK_C ** -0.5), kC, vC, _log_sigmoid(bdot(r128, wup) + bg) / GATE_TEMP


def mem_tile(q, k, v):
    s = bdot_t(q, k) * (HEAD_DIM_M ** -0.5)
    m = lax.stop_gradient(jnp.max(s, axis=-1, keepdims=True))
    p = jnp.exp(s - m)
    p = p / jnp.sum(p, axis=-1, keepdims=True)
    return (bdot(p, v),)


def attn_block(qs, ks, vs, sinks, slopes, c, seq):
    i = lax.broadcasted_iota(jnp.int32, (BLOCK, 3 * BLOCK), 0)
    j = lax.broadcasted_iota(jnp.int32, (BLOCK, 3 * BLOCK), 1)
    dist = jnp.abs(i - j + BLOCK).astype(F32)
    kpos = (c - 1) * BLOCK + j
    valid = (dist <= WINDOW) & (kpos >= 0) & (kpos < seq)
    outs = []
    for q, sk, slope in zip(qs, sinks, slopes):
        s = bdot_t(q, ks) * (HEAD_DIM_A ** -0.5)
        s = jnp.where(valid, s - slope * dist, MASK_VALUE)
        m = lax.stop_gradient(jnp.maximum(jnp.max(s, axis=-1, keepdims=True), sk))
        p = jnp.where(valid, jnp.exp(s - m), 0.0)
        denom = jnp.sum(p, axis=-1, keepdims=True) + jnp.exp(sk - m)
        outs.append(bdot(p, vs) / denom)
    return tuple(outs)


def scan_chunk(q, k, v, e, tot, st, qm, pm):
    C = SCAN_CHUNK
    e = split(e, 2 + SCAN_LEVELS, 0)
    qe = q * jnp.exp(e[0])
    kd = k * jnp.exp(e[1])
    r = lax.broadcasted_iota(jnp.int32, (C, C), 0)
    s = lax.broadcasted_iota(jnp.int32, (C, C), 1)
    a = jnp.where(r == s, jnp.sum(q * k, axis=-1, keepdims=True), 0.0)
    for l in range(SCAN_LEVELS):
        u = jnp.where(qm[l * C:(l + 1) * C] != 0.0, q, k) * jnp.exp(e[2 + l])
        a = a + bdot_t(u, u) * pm[l * C:(l + 1) * C]
    o = bdot_t(qe, st) + bdot(a, v)
    st_new = st * jnp.exp(tot) + bdot_tn(v, kd)
    return o, st_new


def _scan_consts():
    C, L = SCAN_CHUNK, SCAN_LEVELS
    t = np.arange(C)[:, None]
    r = np.arange(C)[None, :]
    blocks = [(r <= t), (r > t)]
    qms, pms = [], []
    for l in range(1, L + 1):
        m = C >> l
        upper_t = (t % (2 * m)) >= m
        upper_r = (r % (2 * m)) >= m
        same_half = (t // m) == (r // m)
        blocks.append(same_half & np.where(upper_t, r <= t, r > t))
        qms.append(np.broadcast_to(upper_t, (C, C)))
        pms.append(((t // (2 * m)) == (r // (2 * m))) & upper_t & ~upper_r)
    hf = np.concatenate(blocks, axis=0).astype(np.float32)
    flip = lambda mat: mat.reshape(-1, C, C)[:, ::-1, ::-1].reshape(-1, C)
    qmf = np.concatenate(qms, axis=0).astype(np.float32)
    pmf = np.concatenate(pms, axis=0).astype(np.float32)
    h = np.stack([hf, flip(hf)])
    ht = np.stack([h[0].T, h[1].T])
    qm = np.stack([qmf, 1.0 - qmf])
    pm = np.stack([pmf, flip(pmf)])
    return h, ht, qm, pm


def _cparams(sem):
    return pltpu.CompilerParams(dimension_semantics=sem, vmem_limit_bytes=VMEM_LIMIT)


def _row_tile(T):
    return min(T, 256)


def _in_spec(spec, tr):
    kind = spec[0]
    if kind == "row":
        _, arr, off, w = spec
        assert off % w == 0
        return arr, pl.BlockSpec((tr, w), functools.partial(lambda i, b: (i, b), b=off // w))
    if kind == "row3":
        _, arr, d, off, w = spec
        assert off % w == 0
        return arr, pl.BlockSpec((None, tr, w), functools.partial(lambda i, d, b: (d, i, b), d=d, b=off // w))
    _, arr = spec
    return arr, pl.BlockSpec(arr.shape, functools.partial(lambda i, n: (0,) * n, n=arr.ndim))


def rows_call(name, tile_fn, T, ins, out_widths, out_dtypes=None, stacks=None):
    tr = _row_tile(T)
    n_in = len(ins)
    out_dtypes = out_dtypes or [F32] * len(out_widths)
    stacks = stacks or [(k,) for k in range(len(out_widths))]

    def body(*refs):
        vals = [r[...] for r in refs[:n_in]]
        outs = tile_fn(*vals)
        for r, members in zip(refs[n_in:], stacks):
            if len(members) == 1:
                r[...] = outs[members[0]].astype(r.dtype)
            else:
                for d, k in enumerate(members):
                    r[d] = outs[k].astype(r.dtype)

    in_specs, args = [], []
    for spec in ins:
        arr, bs = _in_spec(spec, tr)
        args.append(arr)
        in_specs.append(bs)
    out_specs, out_shape = [], []
    for w, dt, members in zip(out_widths, out_dtypes, stacks):
        n = len(members)
        if n == 1:
            out_specs.append(pl.BlockSpec((tr, w), lambda i: (i, 0)))
            out_shape.append(jax.ShapeDtypeStruct((T, w), dt))
        else:
            out_specs.append(pl.BlockSpec((n, tr, w), lambda i: (0, i, 0)))
            out_shape.append(jax.ShapeDtypeStruct((n, T, w), dt))
    return pl.pallas_call(body, out_shape=out_shape, grid=(T // tr,), in_specs=in_specs, out_specs=out_specs,
                          name=name, compiler_params=_cparams(("arbitrary",)))(*args)


def rows_vjp_call(name, tile_fn, T, ins, cts, skip=(), narrow=()):
    tr = _row_tile(T)
    n_in = len(ins)
    n_ct = [len(c) for c in cts]
    want = [k for k in range(n_in) if k not in skip]

    def body(*refs):
        i = pl.program_id(0)
        vals = [r[...] for r in refs[:n_in]]
        ct, pos = [], n_in
        for n in n_ct:
            acc = refs[pos][...]
            for r in refs[pos + 1:pos + n]:
                acc = acc + r[...]
            ct.append(acc)
            pos += n
        _, vjp = jax.vjp(tile_fn, *vals)
        grads = vjp(tuple(ct))
        for r, k in zip(refs[pos:], want):
            if ins[k][0] == "full":
                @pl.when(i == 0)
                def _():
                    r[...] = jnp.zeros_like(r)
                r[...] += grads[k]
            else:
                r[...] = grads[k].astype(r.dtype)

    in_specs, args = [], []
    for spec in list(ins) + [s for c in cts for s in c]:
        arr, bs = _in_spec(spec, tr)
        args.append(arr)
        in_specs.append(bs)
    out_specs, out_shape = [], []
    for k in want:
        if ins[k][0] == "full":
            arr = ins[k][1]
            out_specs.append(pl.BlockSpec(arr.shape, functools.partial(lambda i, n: (0,) * n, n=arr.ndim)))
            out_shape.append(jax.ShapeDtypeStruct(arr.shape, F32))
        else:
            w = ins[k][-1]
            out_specs.append(pl.BlockSpec((tr, w), lambda i: (i, 0)))
            out_shape.append(jax.ShapeDtypeStruct((T, w), BF16 if k in narrow else F32))
    return pl.pallas_call(body, out_shape=out_shape, grid=(T // tr,), in_specs=in_specs, out_specs=out_specs,
                          name=name, compiler_params=_cparams(("arbitrary",)))(*args)


def matmul(name, a, b, mode, add=None, out_dtype=F32):
    if mode == "tn":
        K, M = a.shape
        N = b.shape[1]
        tm = M if M <= 1536 else 512
        tn = N if N <= 1280 else (N // 2 if (N // 2) % 128 == 0 else N)
        tk = min(K, 512)
        grid = (M // tm, N // tn, K // tk)

        def body(a_ref, b_ref, o_ref):
            @pl.when(pl.program_id(2) == 0)
            def _():
                o_ref[...] = jnp.zeros_like(o_ref)
            o_ref[...] += dot_tn(a_ref[...], b_ref[...])

        return pl.pallas_call(
            body, out_shape=jax.ShapeDtypeStruct((M, N), F32), grid=grid,
            in_specs=[pl.BlockSpec((tk, tm), lambda i, j, k: (k, i)), pl.BlockSpec((tk, tn), lambda i, j, k: (k, j))],
            out_specs=pl.BlockSpec((tm, tn), lambda i, j, k: (i, j)), name=name,
            compiler_params=_cparams(("arbitrary", "arbitrary", "arbitrary")))(a, b)

    M, K = a.shape
    N = b.shape[1] if mode == "nn" else b.shape[0]
    tm = min(M, 512)
    tn = N if N <= 1536 else (N // 2 if (N // 2) % 128 == 0 else (N // 3 if (N // 3) % 128 == 0 else N))
    grid = (N // tn, M // tm)
    n_in = 2 + (add is not None)

    def body(*refs):
        a_ref, b_ref = refs[0], refs[1]
        o_ref = refs[n_in]
        acc = dot_nn(a_ref[...], b_ref[...]) if mode == "nn" else dot_nt(a_ref[...], b_ref[...])
        if add is not None:
            acc = acc + refs[2][...]
        o_ref[...] = acc.astype(o_ref.dtype)

    in_specs = [pl.BlockSpec((tm, K), lambda j, i: (i, 0)),
                pl.BlockSpec((K, tn), lambda j, i: (0, j)) if mode == "nn" else pl.BlockSpec((tn, K), lambda j, i: (j, 0))]
    args = [a, b]
    if add is not None:
        in_specs.append(pl.BlockSpec((tm, tn), lambda j, i: (i, j)))
        args.append(add)
    return pl.pallas_call(
        body, out_shape=jax.ShapeDtypeStruct((M, N), out_dtype), grid=grid, in_specs=in_specs,
        out_specs=pl.BlockSpec((tm, tn), lambda j, i: (i, j)), name=name,
        compiler_params=_cparams(("arbitrary", "arbitrary")))(*args)


def _attn_heads(n):
    G = N_Q_A // N_KV_A
    k_sl = pl.ds(n * HEAD_DIM_A, HEAD_DIM_A)
    v_sl = pl.ds(W_KV_A + n * HEAD_DIM_A, HEAD_DIM_A)
    q_sl = [pl.ds((n * G + g) * HEAD_DIM_A, HEAD_DIM_A) for g in range(G)]
    return k_sl, v_sl, q_sl, range(n * G, (n + 1) * G)


def attn_fwd(p, q_off, kvp, sink, slopes, T):
    nb = T // BLOCK
    assert q_off % W_A == 0

    def body(q_ref, kv_ref, sink_ref, slope_ref, o_ref):
        c = pl.program_id(0)
        rows = pl.ds(pl.multiple_of(c * BLOCK, BLOCK), 3 * BLOCK)
        for n in range(N_KV_A):
            k_sl, v_sl, q_sl, heads = _attn_heads(n)
            outs = attn_block([q_ref[:, s] for s in q_sl], kv_ref[rows, k_sl], kv_ref[rows, v_sl],
                              [sink_ref[h] for h in heads], [slope_ref[h] for h in heads], c, T)
            for s, o in zip(q_sl, outs):
                o_ref[:, s] = o

    full = lambda a: pl.BlockSpec(a.shape, functools.partial(lambda c, nd: (0,) * nd, nd=a.ndim))
    return pl.pallas_call(
        body, out_shape=jax.ShapeDtypeStruct((T, W_A), F32), grid=(nb,),
        in_specs=[pl.BlockSpec((BLOCK, W_A), lambda c: (c, q_off // W_A)), full(kvp), full(sink), full(slopes)],
        out_specs=pl.BlockSpec((BLOCK, W_A), lambda c: (c, 0)),
        name="attn_fwd", compiler_params=_cparams(("arbitrary",)))(p, kvp, sink, slopes)


def attn_bwd(p, q_off, kvp, sink, slopes, do, T):
    nb = T // BLOCK

    def body(q_ref, kv_ref, sink_ref, slope_ref, do_ref, dq_ref, dkv_ref, dsink_ref):
        c = pl.program_id(0)
        rows = pl.ds(pl.multiple_of(c * BLOCK, BLOCK), 3 * BLOCK)

        @pl.when(c == 0)
        def _():
            dkv_ref[...] = jnp.zeros_like(dkv_ref)
            dsink_ref[...] = jnp.zeros_like(dsink_ref)

        for n in range(N_KV_A):
            k_sl, v_sl, q_sl, heads = _attn_heads(n)
            slopes_n = [slope_ref[h] for h in heads]
            _, vjp = jax.vjp(lambda qs, kk, vv, sks: attn_block(qs, kk, vv, sks, slopes_n, c, T),
                             [q_ref[:, s] for s in q_sl], kv_ref[rows, k_sl], kv_ref[rows, v_sl],
                             [sink_ref[h] for h in heads])
            dqs, dks, dvs, dsks = vjp(tuple(do_ref[:, s] for s in q_sl))
            dkv_ref[rows, k_sl] += dks
            dkv_ref[rows, v_sl] += dvs
            for s, h, dq, dsk in zip(q_sl, heads, dqs, dsks):
                dq_ref[:, s] = dq.astype(dq_ref.dtype)
                dsink_ref[h] += dsk

    full = lambda a: pl.BlockSpec(a.shape, functools.partial(lambda c, nd: (0,) * nd, nd=a.ndim))
    qspec = pl.BlockSpec((BLOCK, W_A), lambda c: (c, 0))
    return pl.pallas_call(
        body,
        out_shape=[jax.ShapeDtypeStruct((T, W_A), BF16), jax.ShapeDtypeStruct(kvp.shape, F32),
                   jax.ShapeDtypeStruct((N_Q_A, 1, 1), F32)],
        grid=(nb,),
        in_specs=[pl.BlockSpec((BLOCK, W_A), lambda c: (c, q_off // W_A)), full(kvp), full(sink), full(slopes), qspec],
        out_specs=[qspec, full(kvp), full(sink)],
        name="attn_bwd", compiler_params=_cparams(("arbitrary",)))(p, kvp, sink, slopes, do)


def mem_fwd(p, q_off, kv, T):
    tr = _row_tile(T)
    assert q_off % W_M == 0

    def body(q_ref, kv_ref, o_ref):
        for h in range(N_HEADS_M):
            hs = pl.ds(h * HEAD_DIM_M, HEAD_DIM_M)
            (o,) = mem_tile(q_ref[:, hs], kv_ref[:, hs], kv_ref[:, pl.ds(W_M + h * HEAD_DIM_M, HEAD_DIM_M)])
            o_ref[:, hs] = o

    return pl.pallas_call(
        body, out_shape=jax.ShapeDtypeStruct((T, W_M), F32), grid=(T // tr,),
        in_specs=[pl.BlockSpec((tr, W_M), lambda i: (i, q_off // W_M)), pl.BlockSpec((N_MEM, 2 * W_M), lambda i: (0, 0))],
        out_specs=pl.BlockSpec((tr, W_M), lambda i: (i, 0)),
        name="mem_fwd", compiler_params=_cparams(("arbitrary",)))(p, kv)


def mem_bwd(p, q_off, kv, do, T):
    tr = _row_tile(T)

    def body(q_ref, kv_ref, do_ref, dq_ref, dkv_ref):
        @pl.when(pl.program_id(0) == 0)
        def _():
            dkv_ref[...] = jnp.zeros_like(dkv_ref)

        for h in range(N_HEADS_M):
            hs = pl.ds(h * HEAD_DIM_M, HEAD_DIM_M)
            vs = pl.ds(W_M + h * HEAD_DIM_M, HEAD_DIM_M)
            _, vjp = jax.vjp(mem_tile, q_ref[:, hs], kv_ref[:, hs], kv_ref[:, vs])
            dq, dk, dv = vjp((do_ref[:, hs],))
            dq_ref[:, hs] = dq.astype(dq_ref.dtype)
            dkv_ref[:, hs] += dk
            dkv_ref[:, vs] += dv

    kvspec = pl.BlockSpec((N_MEM, 2 * W_M), lambda i: (0, 0))
    return pl.pallas_call(
        body,
        out_shape=[jax.ShapeDtypeStruct((T, W_M), BF16), jax.ShapeDtypeStruct((N_MEM, 2 * W_M), F32)],
        grid=(T // tr,),
        in_specs=[pl.BlockSpec((tr, W_M), lambda i: (i, q_off // W_M)), kvspec, pl.BlockSpec((tr, W_M), lambda i: (i, 0))],
        out_specs=[pl.BlockSpec((tr, W_M), lambda i: (i, 0)), kvspec],
        name="mem_bwd", compiler_params=_cparams(("arbitrary",)))(p, kv, do)


def _scan_const_specs(dk):
    C, L = SCAN_CHUNK, SCAN_LEVELS
    return [pl.BlockSpec((2, (2 + L) * C, C), lambda n: (0, 0, 0)),
            pl.BlockSpec((2, C, (2 + L) * C), lambda n: (0, 0, 0)),
            pl.BlockSpec((2, L * C, dk), lambda n: (0, 0, 0)),
            pl.BlockSpec((2, L * C, C), lambda n: (0, 0, 0))]


def _chunk_spec(src, width, chunk_of):
    arr, sel = src
    if arr.ndim == 2:
        assert sel % width == 0
        return pl.BlockSpec((SCAN_CHUNK, width), functools.partial(lambda n, b: (chunk_of(n), b), b=sel // width))
    return pl.BlockSpec((None, SCAN_CHUNK, width), functools.partial(lambda n, d: (d, chunk_of(n), 0), d=sel))


def _scan_const_args():
    h, ht, qm, pm = _scan_consts()
    return [jnp.asarray(h, BF16), jnp.asarray(ht, BF16), jnp.asarray(qm, F32), jnp.asarray(pm, F32)]


def _full_spec(a):
    return pl.BlockSpec(a.shape, functools.partial(lambda n, nd: (0,) * nd, nd=a.ndim))


def scan_fwd(name, prep, raws, params, heads, dk, dv, T):
    C = SCAN_CHUNK
    N = T // C
    assert dk == C
    Wv = heads * dv
    orders = (lambda n: n, lambda n: N - 1 - n)
    n_raw, n_par = [len(r) for r in raws], [len(p) for p in params]

    def body(*refs):
        pos, raw_refs, par_refs = 0, [], []
        for d in range(2):
            raw_refs.append(refs[pos:pos + n_raw[d]])
            pos += n_raw[d]
        for d in range(2):
            par_refs.append(refs[pos:pos + n_par[d]])
            pos += n_par[d]
        h_ref, ht_ref, qm_ref, pm_ref = refs[pos:pos + 4]
        o_refs, ss_refs, st_ref = refs[pos + 4:pos + 6], refs[pos + 6:pos + 8], refs[pos + 8]

        @pl.when(pl.program_id(0) == 0)
        def _():
            st_ref[...] = jnp.zeros_like(st_ref)

        for d in range(2):
            consts = (qm_ref[d], pm_ref[d])
            q, k, v, g = prep([r[...] for r in raw_refs[d]], [p[...] for p in par_refs[d]])
            e = _split_mm(h_ref[d], g)
            tot = jnp.sum(g, axis=0, keepdims=True)
            for h in range(heads):
                ks, vs = slice(h * dk, (h + 1) * dk), slice(h * dv, (h + 1) * dv)
                st = st_ref[d, h]
                ss_refs[d][h] = st
                o, st_new = scan_chunk(q[:, ks], k[:, ks], v[:, vs], e[:, ks], tot[:, ks], st, *consts)
                o_refs[d][:, vs] = o
                st_ref[d, h] = st_new

    ss_spec = lambda order: pl.BlockSpec((heads, None, dv, dk), lambda n: (0, order(n), 0, 0))
    return pl.pallas_call(
        body,
        out_shape=[jax.ShapeDtypeStruct((T, Wv), F32)] * 2 + [jax.ShapeDtypeStruct((heads, N, dv, dk), F32)] * 2,
        grid=(N,),
        in_specs=[_chunk_spec(s, w, orders[d]) for d in range(2) for s, w in raws[d]]
        + [_full_spec(p) for d in range(2) for p in params[d]] + _scan_const_specs(dk),
        out_specs=[pl.BlockSpec((C, Wv), lambda n: (orders[0](n), 0)), pl.BlockSpec((C, Wv), lambda n: (orders[1](n), 0)),
                   ss_spec(orders[0]), ss_spec(orders[1])],
        scratch_shapes=[pltpu.VMEM((2, heads, dv, dk), F32)],
        name=name, compiler_params=_cparams(("arbitrary",)))(
            *[s[0] for d in range(2) for s, _ in raws[d]], *[p for d in range(2) for p in params[d]], *_scan_const_args())


def scan_bwd(name, prep, raws, params, ss, do, heads, dk, dv, T):
    C = SCAN_CHUNK
    N = T // C
    Wv = heads * dv
    orders = (lambda n: N - 1 - n, lambda n: n)
    n_raw, n_par = [len(r) for r in raws], [len(p) for p in params]

    def body(*refs):
        pos, raw_refs, par_refs, draw_refs, dpar_refs = 0, [], [], [], []
        for group, counts in ((raw_refs, n_raw), (par_refs, n_par)):
            for d in range(2):
                group.append(refs[pos:pos + counts[d]])
                pos += counts[d]
        ss_refs, do_refs = refs[pos:pos + 2], refs[pos + 2:pos + 4]
        h_ref, ht_ref, qm_ref, pm_ref = refs[pos + 4:pos + 8]
        pos += 8
        for group, counts in ((draw_refs, n_raw), (dpar_refs, n_par)):
            for d in range(2):
                group.append(refs[pos:pos + counts[d]])
                pos += counts[d]
        dst_ref = refs[pos]

        @pl.when(pl.program_id(0) == 0)
        def _():
            dst_ref[...] = jnp.zeros_like(dst_ref)
            for d in range(2):
                for r in dpar_refs[d]:
                    r[...] = jnp.zeros_like(r)

        for d in range(2):
            consts = (qm_ref[d], pm_ref[d])
            (q, k, v, g), prep_vjp = jax.vjp(prep, [r[...] for r in raw_refs[d]], [p[...] for p in par_refs[d]])
            e = _split_mm(h_ref[d], g)
            tot = jnp.sum(g, axis=0, keepdims=True)
            dqs, dks, dvs, des, dtots = [], [], [], [], []
            for h in range(heads):
                ks, vs = slice(h * dk, (h + 1) * dk), slice(h * dv, (h + 1) * dv)
                _, vjp = jax.vjp(lambda q_, k_, v_, e_, t_, st_: scan_chunk(q_, k_, v_, e_, t_, st_, *consts),
                                 q[:, ks], k[:, ks], v[:, vs], e[:, ks], tot[:, ks], ss_refs[d][h])
                dq, dk_, dv_, de, dtot, dst = vjp((do_refs[d][:, vs], dst_ref[d, h]))
                dst_ref[d, h] = dst
                for group, val in ((dqs, dq), (dks, dk_), (dvs, dv_), (des, de), (dtots, dtot)):
                    group.append(val)
            cat = lambda parts: jnp.concatenate(parts, axis=-1)
            dg = _split_mm(ht_ref[d], cat(des)) + cat(dtots)
            draws, dpars = prep_vjp((cat(dqs), cat(dks), cat(dvs), dg))
            for r, val in zip(draw_refs[d], draws):
                r[...] = val.astype(r.dtype)
            for r, val in zip(dpar_refs[d], dpars):
                r[...] += val

    ss_spec = lambda order: pl.BlockSpec((heads, None, dv, dk), lambda n: (0, order(n), 0, 0))
    row_out = lambda w, order: pl.BlockSpec((C, w), lambda n: (order(n), 0))
    return pl.pallas_call(
        body,
        out_shape=[jax.ShapeDtypeStruct((T, w), BF16) for d in range(2) for _, w in raws[d]]
        + [jax.ShapeDtypeStruct(p.shape, F32) for d in range(2) for p in params[d]],
        grid=(N,),
        in_specs=[_chunk_spec(s, w, orders[d]) for d in range(2) for s, w in raws[d]]
        + [_full_spec(p) for d in range(2) for p in params[d]]
        + [ss_spec(orders[0]), ss_spec(orders[1]), _chunk_spec(do, Wv, orders[0]), _chunk_spec(do, Wv, orders[1])]
        + _scan_const_specs(dk),
        out_specs=[row_out(w, orders[d]) for d in range(2) for _, w in raws[d]]
        + [_full_spec(p) for d in range(2) for p in params[d]],
        scratch_shapes=[pltpu.VMEM((2, heads, dv, dk), F32)],
        name=name, compiler_params=_cparams(("arbitrary",)))(
            *[s[0] for d in range(2) for s, _ in raws[d]], *[p for d in range(2) for p in params[d]],
            ss[0], ss[1], do[0], do[0], *_scan_const_args())


def final_call(x, g, target, T):
    tr = _row_tile(T)

    def tile(xv, gv, tv):
        y = _rms(xv, gv)
        err = (y - tv) ** 2
        return jnp.sum(jnp.sum(err, axis=-1, keepdims=True), axis=0, keepdims=True) * (0.5 / D_MODEL)

    def body(x_ref, g_ref, t_ref, loss_ref, dx_ref, dg_ref):
        i = pl.program_id(0)
        tv = t_ref[...]
        lv, vjp = jax.vjp(lambda a, b: tile(a, b, tv), x_ref[...], g_ref[...])
        dx, dg = vjp(jnp.ones((1, 1), F32))
        dx_ref[...] = dx

        @pl.when(i == 0)
        def _():
            loss_ref[...] = jnp.zeros_like(loss_ref)
            dg_ref[...] = jnp.zeros_like(dg_ref)

        loss_ref[...] += jnp.broadcast_to(lv, loss_ref.shape)
        dg_ref[...] += dg

    return pl.pallas_call(
        body,
        out_shape=[jax.ShapeDtypeStruct((8, 128), F32), jax.ShapeDtypeStruct((T, D_MODEL), F32),
                   jax.ShapeDtypeStruct((1, D_MODEL), F32)],
        grid=(T // tr,),
        in_specs=[pl.BlockSpec((tr, D_MODEL), lambda i: (i, 0)), pl.BlockSpec((1, D_MODEL), lambda i: (0, 0)),
                  pl.BlockSpec((tr, D_MODEL), lambda i: (i, 0))],
        out_specs=[pl.BlockSpec((8, 128), lambda i: (0, 0)), pl.BlockSpec((tr, D_MODEL), lambda i: (i, 0)),
                   pl.BlockSpec((1, D_MODEL), lambda i: (0, 0))],
        name="final_loss", compiler_params=_cparams(("arbitrary",)))(x, g, target)


def adamw_call(w, g, m, v):
    shape = w.shape
    c = shape[-1]
    r = int(np.prod(shape[:-1])) if len(shape) > 1 else 1
    tr = r if r <= 256 else 256
    assert r % tr == 0

    def body(w_ref, g_ref, m_ref, v_ref, d_ref, nm_ref, nv_ref):
        gv = g_ref[...]
        nm = ADAM_B1 * m_ref[...] + (1.0 - ADAM_B1) * gv
        nv = ADAM_B2 * v_ref[...] + (1.0 - ADAM_B2) * jnp.square(gv)
        m_hat = nm / (1.0 - ADAM_B1 ** ADAM_STEP)
        v_hat = nv / (1.0 - ADAM_B2 ** ADAM_STEP)
        d_ref[...] = -ADAM_LR * (m_hat / (jnp.sqrt(v_hat) + ADAM_EPS) + ADAM_WD * w_ref[...])
        nm_ref[...] = nm
        nv_ref[...] = nv

    spec = pl.BlockSpec((tr, c), lambda i: (i, 0))
    outs = pl.pallas_call(body, out_shape=[jax.ShapeDtypeStruct((r, c), F32)] * 3, grid=(r // tr,),
                          in_specs=[spec] * 4, out_specs=[spec] * 3, name="adamw",
                          compiler_params=_cparams(("arbitrary",)))(*(t.reshape(r, c) for t in (w, g, m, v)))
    return tuple(o.reshape(shape) for o in outs)


def adamw_halves(w, mine, other, m, v, c):
    L, R, C = w.shape
    rh = R // 2
    tr = rh if rh <= 256 else 256
    nbh = rh // tr

    def body(c_ref, w_ref, a_ref, b_ref, m_ref, v_ref, g_ref, d_ref, nm_ref, nv_ref):
        is_mine = (pl.program_id(1) // nbh) == c_ref[0]
        gv = jnp.where(is_mine, a_ref[...], b_ref[...])
        nm = ADAM_B1 * m_ref[...] + (1.0 - ADAM_B1) * gv
        nv = ADAM_B2 * v_ref[...] + (1.0 - ADAM_B2) * jnp.square(gv)
        m_hat = nm / (1.0 - ADAM_B1 ** ADAM_STEP)
        v_hat = nv / (1.0 - ADAM_B2 ** ADAM_STEP)
        g_ref[...] = gv
        d_ref[...] = -ADAM_LR * (m_hat / (jnp.sqrt(v_hat) + ADAM_EPS) + ADAM_WD * w_ref[...])
        nm_ref[...] = nm
        nv_ref[...] = nv

    full = pl.BlockSpec((None, tr, C), lambda l, i, c_ref: (l, i, 0))
    half = pl.BlockSpec((None, tr, C), lambda l, i, c_ref: (l, i % nbh, 0))
    grid_spec = pltpu.PrefetchScalarGridSpec(num_scalar_prefetch=1, grid=(L, R // tr),
                                             in_specs=[full, half, half, full, full], out_specs=[full] * 4)
    return pl.pallas_call(body, out_shape=[jax.ShapeDtypeStruct(w.shape, F32)] * 4, grid_spec=grid_spec,
                          name="adamw_halves", compiler_params=_cparams(("arbitrary", "arbitrary")))(c, w, mine, other, m, v)


def sum_devices(g64):
    def body(x_ref, o_ref):
        acc = x_ref[0:8, :]
        for d in range(1, 8):
            acc = acc + x_ref[8 * d:8 * d + 8, :]
        o_ref[...] = acc

    return pl.pallas_call(body, out_shape=jax.ShapeDtypeStruct((8, D_MODEL), F32), name="sum_devices")(g64)


def _half_tile(rh):
    return rh if rh <= 512 else 256


def add_sibling(g, recv, c, out_dtype):
    _, R, C = g.shape
    rh = R // 2
    tr = _half_tile(rh)
    nblk = rh // tr

    def body(c_ref, g_ref, r_ref, o_ref):
        o_ref[...] = (g_ref[...] + r_ref[...]).astype(o_ref.dtype)

    grid_spec = pltpu.PrefetchScalarGridSpec(
        num_scalar_prefetch=1, grid=(4, nblk),
        in_specs=[pl.BlockSpec((None, tr, C), lambda j, i, c_ref: (j, i + c_ref[0] * nblk, 0)),
                  pl.BlockSpec((None, tr, C), lambda j, i, c_ref: (j, i, 0))],
        out_specs=pl.BlockSpec((None, tr, C), lambda j, i, c_ref: (j, i, 0)))
    return pl.pallas_call(body, out_shape=jax.ShapeDtypeStruct((4, rh, C), out_dtype), grid_spec=grid_spec,
                          name="rs_add_sibling", compiler_params=_cparams(("arbitrary", "arbitrary")))(c, g, recv)


def add_chips(g, recv, r3, place):
    _, R, C = g.shape
    rh = R // 2
    tr = _half_tile(rh)
    nblk = rh // tr

    def body(p_ref, g_ref, s_ref, a_ref, b_ref, c_ref, o_ref):
        up = lambda r: r[...].astype(F32)
        o_ref[...] = (((g_ref[...] + up(s_ref)) + up(a_ref)) + up(b_ref)) + up(c_ref)

    grid_spec = pltpu.PrefetchScalarGridSpec(
        num_scalar_prefetch=1, grid=(nblk,),
        in_specs=[pl.BlockSpec((None, tr, C), lambda i, p_ref: (p_ref[0], i + p_ref[1] * nblk, 0)),
                  pl.BlockSpec((None, tr, C), lambda i, p_ref: (p_ref[0], i, 0))]
        + [pl.BlockSpec((None, tr, C), functools.partial(lambda i, p_ref, k: (k, i, 0), k=k)) for k in range(3)],
        out_specs=pl.BlockSpec((tr, C), lambda i, p_ref: (i, 0)))
    return pl.pallas_call(body, out_shape=jax.ShapeDtypeStruct((rh, C), F32), grid_spec=grid_spec,
                          name="rs_add_chips", compiler_params=_cparams(("arbitrary",)))(place, g, recv, r3, r3, r3)


def _remote(src, dst, ssem, rsem, dev):
    return pltpu.make_async_remote_copy(src_ref=src, dst_ref=dst, send_sem=ssem, recv_sem=rsem,
                                        device_id=dev, device_id_type=pl.DeviceIdType.MESH)


def _mesh_places():
    x, y, c = lax.axis_index("x"), lax.axis_index("y"), lax.axis_index("c")
    chips = [(1 - x, y), (x, 1 - y), (1 - x, 1 - y)]
    return x, y, c, (x, y, 1 - c), chips


def _hbm_specs(n):
    return [pl.BlockSpec(memory_space=pltpu.HBM) for _ in range(n)]


def _gather_body(ins, outs, n_split, send_sems, recv_sems, handshake):
    x, y, c, sibling, chips = _mesh_places()
    mine = 2 * x + y
    if handshake:
        barrier = pltpu.get_barrier_semaphore()
        peers = [sibling] + [(*chip, c) for chip in chips]
        for peer in peers:
            pl.semaphore_signal(barrier, inc=1, device_id=peer, device_id_type=pl.DeviceIdType.MESH)
        pl.semaphore_wait(barrier, len(peers))

    def half(a, chip_idx, which):
        rh = ins[a].shape[0] // 2
        return outs[a].at[chip_idx, pl.ds(which * rh, rh), :]

    sent = []
    for a in range(len(ins)):
        for k, chip in enumerate(chips):
            if a < n_split:
                rh = ins[a].shape[0] // 2
                src, dst = ins[a].at[pl.ds(c * rh, rh), :], half(a, mine, c)
            else:
                src, dst = ins[a], outs[a].at[mine]
            sent.append(_remote(src, dst, send_sems.at[a, k], recv_sems.at[a, k], (*chip, c)))
    for cp in sent:
        cp.start()
    for a in range(len(ins)):
        for k, chip in enumerate(chips):
            j = 2 * chip[0] + chip[1]
            region = half(a, j, c) if a < n_split else outs[a].at[j]
            _remote(region, region, send_sems.at[a, k], recv_sems.at[a, k], (*chip, c)).wait_recv()
            if a < n_split:
                fwd = _remote(region, region, send_sems.at[a, 3 + k], recv_sems.at[a, 3 + k], sibling)
                fwd.start()
                sent.append(fwd)
    for a in range(n_split):
        for k, chip in enumerate(chips):
            region = half(a, 2 * chip[0] + chip[1], 1 - c)
            _remote(region, region, send_sems.at[a, 3 + k], recv_sems.at[a, 3 + k], sibling).wait_recv()
    for cp in sent:
        cp.wait_send()


def gather_weights(shards, small):
    arrs = list(shards) + [small]
    n = len(arrs)

    def body(*refs):
        _gather_body(refs[:n], refs[n:2 * n], n - 1, refs[2 * n], refs[2 * n + 1], handshake=False)

    return pl.pallas_call(
        body, out_shape=[jax.ShapeDtypeStruct((4,) + a.shape, a.dtype) for a in arrs],
        in_specs=_hbm_specs(n), out_specs=_hbm_specs(n),
        scratch_shapes=[pltpu.SemaphoreType.DMA((n, 6)), pltpu.SemaphoreType.DMA((n, 6))],
        name="gather_weights")(*arrs)


def gather_weights_async(shards):
    n = len(shards)

    def body(*refs):
        _gather_body(refs[:n], refs[n:2 * n], n, refs[2 * n], refs[2 * n + 1], handshake=True)

    return pl.kernel(
        body, out_type=[jax.ShapeDtypeStruct((4,) + a.shape, a.dtype) for a in shards],
        mesh=plsc.ScalarSubcoreMesh(axis_name="seq", num_cores=1),
        scratch_types=[pltpu.SemaphoreType.DMA((n, 6)), pltpu.SemaphoreType.DMA((n, 6))],
        compiler_params=pltpu.CompilerParams(collective_id=1), name="gather_weights_async")(*shards)


def _sequencer_call(name, body, out_type, sem_shape, collective_id, args):
    return pl.kernel(
        body, out_type=out_type, mesh=plsc.ScalarSubcoreMesh(axis_name="seq", num_cores=1),
        scratch_types=[pltpu.SemaphoreType.DMA(sem_shape), pltpu.SemaphoreType.DMA(sem_shape)],
        compiler_params=pltpu.CompilerParams(collective_id=collective_id), name=name)(*args)


def _handshake(peers):
    barrier = pltpu.get_barrier_semaphore()
    for peer in peers:
        pl.semaphore_signal(barrier, inc=1, device_id=peer, device_id_type=pl.DeviceIdType.MESH)
    pl.semaphore_wait(barrier, len(peers))


def exchange_siblings(name, srcs, halves, collective_id):
    n = len(srcs)

    def body(*refs):
        ins, outs = refs[:n], refs[n:2 * n]
        send_sems, recv_sems = refs[2 * n:]
        x, y, c, sibling, chips = _mesh_places()
        _handshake([sibling])
        cps = []
        for a in range(n):
            src = ins[a]
            if halves:
                rh = src.shape[1] // 2
                src = src.at[:, pl.ds((1 - c) * rh, rh), :]
            cps.append(_remote(src, outs[a], send_sems.at[a], recv_sems.at[a], sibling))
        for cp in cps:
            cp.start()
        for cp in cps:
            cp.wait()

    shape = lambda g: (4, g.shape[1] // 2, g.shape[2]) if halves else g.shape
    return _sequencer_call(name, body, [jax.ShapeDtypeStruct(shape(g), g.dtype) for g in srcs], (n,), collective_id, srcs)


def exchange_chips(name, s1s, collective_id):
    n = len(s1s)

    def body(*refs):
        ins, outs = refs[:n], refs[n:2 * n]
        send_sems, recv_sems = refs[2 * n:]
        x, y, c, sibling, chips = _mesh_places()
        _handshake([(*chip, c) for chip in chips])
        cps = []
        for a in range(n):
            for k, chip in enumerate(chips):
                cps.append(_remote(ins[a].at[2 * chip[0] + chip[1]], outs[a].at[k], send_sems.at[a, k],
                                   recv_sems.at[a, k], (*chip, c)))
        for cp in cps:
            cp.start()
        for cp in cps:
            cp.wait()

    return _sequencer_call(name, body, [jax.ShapeDtypeStruct((3,) + s.shape[1:], s.dtype) for s in s1s], (n, 3),
                           collective_id, s1s)


def allgather_small(v):
    m_per = v.shape[0]

    def body(x_ref, out_ref, send_sems, recv_sems, local_sem):
        x, y, c, sibling, chips = _mesh_places()
        me = (x, y, c)

        def rows(px, py, pc):
            return out_ref.at[pl.ds((4 * px + 2 * py + pc) * m_per, m_per), :]

        def copy(k, block, to, src=None):
            return _remote(rows(*block) if src is None else src, rows(*block), send_sems.at[k], recv_sems.at[k], to)

        mine = pltpu.make_async_copy(x_ref, rows(*me), local_sem)
        mine.start()
        first = [copy(0, me, sibling, src=x_ref)]
        first += [copy(1 + j, me, (*chip, c), src=x_ref) for j, chip in enumerate(chips)]
        for cp in first:
            cp.start()
        passed = [copy(4 + j, (*chip, c), sibling) for j, chip in enumerate(chips)]
        for j, chip in enumerate(chips):
            copy(1 + j, (*chip, c), me).wait_recv()
            passed[j].start()
        copy(0, sibling, me).wait_recv()
        for j, chip in enumerate(chips):
            copy(4 + j, (*chip, 1 - c), me).wait_recv()
        for cp in first + passed:
            cp.wait_send()
        mine.wait()

    return pl.pallas_call(
        body, out_shape=jax.ShapeDtypeStruct((8 * m_per, v.shape[1]), v.dtype),
        in_specs=[pl.BlockSpec(memory_space=pltpu.VMEM)], out_specs=pl.BlockSpec(memory_space=pltpu.VMEM),
        scratch_shapes=[pltpu.SemaphoreType.DMA((7,)), pltpu.SemaphoreType.DMA((7,)), pltpu.SemaphoreType.DMA],
        name="allgather_small")(v)


def rms_res_tile(x, g):
    return (_rms(x, g), x)


def _lower_bounds(lb_param):
    lbs = jax.nn.softmax(lb_param.astype(F32), axis=0)
    return jnp.cumsum(lbs, axis=0) - lbs[0]


def _even_fwd(x, i, W, lower, kv, slopes, T):
    O = EVEN_OFF
    g = W["norm_even"][i].reshape(1, D_MODEL)
    (h,) = rows_call("rms_fwd", rms_tile, T, [("row", x, 0, D_MODEL), ("full", g)], [D_MODEL], [BF16])
    p = matmul("mm_in_e", h, W["w_in_e"][i], "nn")
    kvp = jnp.pad(p[:, O["kA"]:O["kA"] + 2 * W_KV_A], ((BLOCK, BLOCK), (0, 0)))
    sink = W["sink"][i].reshape(N_Q_A, 1, 1)
    a = attn_fwd(p, O["qA"], kvp, sink, slopes, T)
    scan_raws = [[((p, O["qB"]), W_B), ((p, O[z]), W_B), ((p, O["iB"]), W_B)] for z in ("zf", "zb")]
    scan_pars = [[lower[i][0:1]], [lower[i][1:2]]]
    o_f, o_b, ss_f, ss_b = scan_fwd("scan_fwd_h", hgrn_prep, scan_raws, scan_pars, N_HEADS_B, HEAD_DIM_B, HEAD_DIM_B, T)
    mo = mem_fwd(p, O["qM"], kv, T)
    hg = W["hgrn_norm"][i].reshape(1, W_B)
    post_ins = [("row", a, 0, W_A), ("row", o_f, 0, W_B), ("row", o_b, 0, W_B), ("row", mo, 0, W_M),
                ("row", p, O["gA"], W_A), ("row", p, O["gB"], W_B), ("row", p, O["gM"], W_M), ("full", hg)]
    (mix,) = rows_call("even_post_fwd", even_post_tile, T, post_ins, [MIX], [BF16])
    x_new = matmul("mm_out", mix, W["w_out_e"][i], "nn", add=x)
    return x_new, dict(x=x, g=g, h=h, p=p, kvp=kvp, sink=sink, scan_raws=scan_raws, scan_pars=scan_pars,
                       ss=(ss_f, ss_b), post_ins=post_ins, mix=mix)


def _add2(a, b):
    return a.astype(F32) + b.astype(F32)


def _assemble_even(dqA, dgA, dqB_f, dqB_b, dzf, dzb, diB_f, diB_b, dgB, dqM, dgM, dkvA):
    parts = [dqA, dgA, _add2(dqB_f, dqB_b), dzf, dzb, _add2(diB_f, diB_b), dgB, dqM, dgM, dkvA]
    return (jnp.concatenate([t.astype(BF16) for t in parts], axis=-1),)


def _even_bwd(dxo, sv, i, W, kv, slopes, T, sync):
    O = EVEN_OFF
    p = sv["p"]
    dmix = matmul("mm_dmix", dxo, W["w_out_e"][i], "nt")
    dwo = matmul("mm_dwo", sv["mix"], dxo, "tn")
    da, dof, dmo, dgA, dgB, dgM, dhg = rows_vjp_call("even_post_bwd", even_post_tile, T, sv["post_ins"],
                                                      [[("row", dmix, 0, MIX)]], skip=(2,), narrow=(4, 5, 6))
    dqA, dkvp, dsink = attn_bwd(p, O["qA"], sv["kvp"], sv["sink"], slopes, da, T)
    dkvA = dkvp[BLOCK:-BLOCK]
    dqB_f, dzf, diB_f, dqB_b, dzb, diB_b, dlow_f, dlow_b = scan_bwd(
        "scan_bwd_h", hgrn_prep, sv["scan_raws"], sv["scan_pars"], sv["ss"], (dof, 0), N_HEADS_B, HEAD_DIM_B, HEAD_DIM_B, T)
    dqB_f = sync(dqB_f)
    row = lambda arr, w: ("row", arr, 0, w)
    dlow = jnp.concatenate([dlow_f, dlow_b], axis=0)
    dqM, dkv = mem_bwd(p, O["qM"], kv, dmo, T)
    (dp,) = rows_call("even_dp", _assemble_even, T,
                      [row(dqA, W_A), row(dgA, W_A), row(dqB_f, W_B), row(dqB_b, W_B), row(dzf, W_B), row(dzb, W_B),
                       row(diB_f, W_B), row(diB_b, W_B), row(dgB, W_B), row(dqM, W_M), row(dgM, W_M),
                       row(dkvA, 2 * W_KV_A)],
                      [EVEN_IN], [BF16])
    dh = matmul("mm_dh_e", dp, W["w_in_e"][i], "nt")
    dwi = matmul("mm_dwi_e", sv["h"], dp, "tn")
    dx, dg = rows_vjp_call("rms_res_bwd", rms_res_tile, T, [("row", sv["x"], 0, D_MODEL), ("full", sv["g"])],
                           [[("row", dh, 0, D_MODEL)], [("row", dxo, 0, D_MODEL)]])
    return dx, dict(w_in=dwi, w_out=dwo, norm=dg[0], sink=dsink.reshape(N_Q_A), low=dlow, hg=dhg[0], kv=dkv)


def _pad_gate_up(w_up):
    z = jnp.zeros((2, 128, WK_C), F32)
    z = z.at[0, 0:GATE_RANK].set(w_up[0])
    return z.at[1, GATE_RANK:2 * GATE_RANK].set(w_up[1])


def _odd_fwd(x, i, W, kv, T):
    O = ODD_OFF
    g = W["norm_odd"][i].reshape(1, D_MODEL)
    (h,) = rows_call("rms_fwd", rms_tile, T, [("row", x, 0, D_MODEL), ("full", g)], [D_MODEL], [BF16])
    p = matmul("mm_in_o", h, W["w_in_o"][i], "nn")
    wup = _pad_gate_up(W["w_gate_up"][i])
    one_dir = [((p, O["qC"]), WK_C), ((p, O["kC"]), WK_C), ((p, O["vC"]), WV_C), ((p, O["rr"]), 128)]
    scan_raws = [one_dir, one_dir]
    scan_pars = [[wup[d], W["b_gate"][i][d:d + 1]] for d in range(2)]
    o_f, o_b, ss_f, ss_b = scan_fwd("scan_fwd_g", gla_prep, scan_raws, scan_pars, N_HEADS_C, DK_C, DV_C, T)
    mo = mem_fwd(p, O["qM"], kv, T)
    gg = W["gla_norm"][i].reshape(1, WV_C)
    post_ins = [("row", o_f, 0, WV_C), ("row", o_b, 0, WV_C), ("row", mo, 0, W_M),
                ("row", p, O["gC"], WV_C), ("row", p, O["gM"], W_M), ("full", gg)]
    (mix,) = rows_call("odd_post_fwd", odd_post_tile, T, post_ins, [MIX], [BF16])
    x_new = matmul("mm_out", mix, W["w_out_o"][i], "nn", add=x)
    return x_new, dict(x=x, g=g, h=h, p=p, scan_raws=scan_raws, scan_pars=scan_pars, ss=(ss_f, ss_b),
                       post_ins=post_ins, mix=mix)


def _assemble_odd(dq0, dq1, dk0, dk1, dv0, dv1, dgC, dqM, dgM, dr0, dr1):
    parts = [_add2(dq0, dq1), _add2(dk0, dk1), _add2(dv0, dv1), dgC, dqM, dgM, _add2(dr0, dr1)]
    return (jnp.concatenate([t.astype(BF16) for t in parts], axis=-1),)


def _odd_bwd(dxo, sv, i, W, kv, T, sync):
    O = ODD_OFF
    p = sv["p"]
    dmix = matmul("mm_dmix", dxo, W["w_out_o"][i], "nt")
    dwo = matmul("mm_dwo", sv["mix"], dxo, "tn")
    dof, dmo, dgC, dgM, dgg = rows_vjp_call("odd_post_bwd", odd_post_tile, T, sv["post_ins"],
                                            [[("row", dmix, 0, MIX)]], skip=(1,), narrow=(3, 4))
    dqf, dkf, dvf, dr_f, dqb, dkb, dvb, dr_b, dwup_f, dbg_f, dwup_b, dbg_b = scan_bwd(
        "scan_bwd_g", gla_prep, sv["scan_raws"], sv["scan_pars"], sv["ss"], (dof, 0), N_HEADS_C, DK_C, DV_C, T)
    dqf = sync(dqf)
    row = lambda arr, w: ("row", arr, 0, w)
    dqM, dkv = mem_bwd(p, O["qM"], kv, dmo, T)
    (dp,) = rows_call("odd_dp", _assemble_odd, T,
                      [row(dqf, WK_C), row(dqb, WK_C), row(dkf, WK_C), row(dkb, WK_C), row(dvf, WV_C), row(dvb, WV_C),
                       row(dgC, WV_C), row(dqM, W_M), row(dgM, W_M), row(dr_f, 128), row(dr_b, 128)],
                      [ODD_PAD], [BF16])
    dh = matmul("mm_dh_o", dp, W["w_in_o"][i], "nt")
    dwi = matmul("mm_dwi_o", sv["h"], dp, "tn")
    dx, dg = rows_vjp_call("rms_res_bwd", rms_res_tile, T, [("row", sv["x"], 0, D_MODEL), ("full", sv["g"])],
                           [[("row", dh, 0, D_MODEL)], [("row", dxo, 0, D_MODEL)]])
    dw_up = jnp.stack([dwup_f[0:GATE_RANK], dwup_b[GATE_RANK:2 * GATE_RANK]])
    dbg = jnp.concatenate([dbg_f, dbg_b], axis=0)
    return dx, dict(w_in=dwi, w_out=dwo, norm=dg[0], w_up=dw_up, b_gate=dbg, gg=dgg[0], kv=dkv)


def local_step(x, mem, target, W, later=None, on_layer_grads=None, sync=lambda a: a):
    T = x.shape[0]
    slopes = (2.0 ** (-8.0 * jnp.arange(1, N_Q_A + 1, dtype=F32) / N_Q_A)).reshape(N_Q_A, 1, 1)
    lower, lower_vjp = jax.vjp(_lower_bounds, W["lb_param"])
    mem_g = W["mem_norm"].reshape(1, D_MODEL)
    (mem_n,) = rows_call("mem_rms_fwd", rms_tile, N_MEM, [("row", mem, 0, D_MODEL), ("full", mem_g)], [D_MODEL], [BF16])
    kvs, saved = [], []
    for l in range(DEPTH):
        if l == 1 and later is not None:
            x, W = later(x, W)
        kvs.append(matmul("mm_kv", mem_n, W["w_kv"][l], "nn"))
        if l % 2 == 0:
            x, sv = _even_fwd(x, l // 2, W, lower, kvs[l], slopes, T)
        else:
            x, sv = _odd_fwd(x, l // 2, W, kvs[l], T)
        saved.append(sv)
    loss, dx, dgf = final_call(x, W["final_norm"].reshape(1, D_MODEL), target, T)
    per = [None] * DEPTH
    dmem_n = None
    for l in reversed(range(DEPTH)):
        if l % 2 == 0:
            dx, per[l] = _even_bwd(dx, saved[l], l // 2, W, kvs[l], slopes, T, sync)
        else:
            dx, per[l] = _odd_bwd(dx, saved[l], l // 2, W, kvs[l], T, sync)
        per[l]["w_kv"] = matmul("mm_dwkv", mem_n, per[l]["kv"], "tn")
        dmem_n = matmul("mm_dmem", per[l]["kv"], W["w_kv"][l], "nt", add=dmem_n)
        if on_layer_grads is not None:
            dx = on_layer_grads(l, dx, per[l])
    dw_kv = [per[l]["w_kv"] for l in range(DEPTH)]
    (dmem_norm,) = rows_vjp_call("mem_rms_bwd", rms_tile, N_MEM, [("row", mem, 0, D_MODEL), ("full", mem_g)],
                                 [[("row", dmem_n, 0, D_MODEL)]], skip=(0,))
    ev, od = (per[0], per[2]), (per[1], per[3])
    (d_lb,) = lower_vjp(jnp.stack([e["low"] for e in ev]))
    grads = dict(
        w_in_e=jnp.stack([e["w_in"] for e in ev]), w_in_o=jnp.stack([o["w_in"] for o in od]),
        w_out_e=jnp.stack([e["w_out"] for e in ev]), w_out_o=jnp.stack([o["w_out"] for o in od]),
        w_kv=jnp.stack(dw_kv), norm_even=jnp.stack([e["norm"] for e in ev]), sink=jnp.stack([e["sink"] for e in ev]),
        lb_param=d_lb, hgrn_norm=jnp.stack([e["hg"] for e in ev]), norm_odd=jnp.stack([o["norm"] for o in od]),
        w_gate_up=jnp.stack([o["w_up"] for o in od]), b_gate=jnp.stack([o["b_gate"] for o in od]),
        gla_norm=jnp.stack([o["gg"] for o in od]), mem_norm=dmem_norm[0], final_norm=dgf[0])
    return loss, dx, grads


SMALL_SPECS = (("lb_param", (2, 2, 128)), ("norm_odd", (2, 256)), ("w_gate_up", (2, 2, 16, 128)),
               ("b_gate", (2, 2, 128)), ("gla_norm", (2, 256)))
SMALL_ROWS = 80


def _pack_small_local(d):
    return jnp.concatenate([d[n].reshape(-1) for n, _ in SMALL_SPECS]).reshape(SMALL_ROWS, 128)


def _unpack_small_local(b):
    flat, out, o = b.reshape(-1), {}, 0
    for n, shp in SMALL_SPECS:
        sz = int(np.prod(shp))
        out[n] = flat[o:o + sz].reshape(shp)
        o += sz
    return out


def _unpack_small_full(g4):
    per = [_unpack_small_local(g4[j]) for j in range(4)]
    return {n: jnp.concatenate([per[j][n] for j in range(4)], axis=-1) for n, _ in SMALL_SPECS}


def _pack_small_blocks(full):
    blocks = []
    for j in range(4):
        blocks.append(_pack_small_local({n: full[n][..., j * shp[-1]:(j + 1) * shp[-1]] for n, shp in SMALL_SPECS}))
    return jnp.stack(blocks)


def _cols(t, order, off, widths):
    return [t[..., off[n]:off[n] + widths[n]] for n in order]


EVEN_REF_ORDER = ("qA", "kA", "vA", "gA", "qB", "zf", "zb", "iB", "gB", "qM", "gM")
ODD_REF_ORDER = ("qC", "kC", "vC", "gC", "rr", "qM", "gM")


def _layer_weights(l, g_in, g_out, g_kv):
    t = g_in.transpose(1, 0, 2).reshape(D_MODEL, -1)
    if l % 2 == 0:
        w_in = jnp.concatenate(_cols(t, EVEN_ORDER, EVEN_REF_OFF, EVEN_W), axis=-1)
    else:
        w_in = jnp.concatenate(_cols(t, ODD_ORDER, ODD_REF_OFF, ODD_W) + [jnp.zeros((D_MODEL, ODD_PAD - ODD_IN), BF16)],
                               axis=-1)
    return w_in, g_out.reshape(MIX, D_MODEL), g_kv.reshape(D_MODEL, 2 * W_M)


def _layer_grad_blocks(l, gl):
    if l % 2 == 0:
        t = jnp.concatenate(_cols(gl["w_in"], EVEN_REF_ORDER, EVEN_OFF, EVEN_W), axis=-1)
    else:
        t = jnp.concatenate(_cols(gl["w_in"], ODD_REF_ORDER, ODD_OFF, ODD_W), axis=-1)
    b_in = t.reshape(D_MODEL, 4, -1).transpose(1, 0, 2)
    return [b_in, gl["w_out"].reshape(4, MIX // 4, D_MODEL), gl["w_kv"].reshape(4, D_MODEL // 4, 2 * W_M)]


WEIGHT_NAMES = ("norm_even", "w_in_even", "sink", "lb_param", "hgrn_norm", "w_out_even", "norm_odd", "w_in_odd",
                "w_gate_up", "b_gate", "gla_norm", "w_out_odd", "mem_norm", "w_mem_kv", "final_norm")


def kernel(x, mem, norm_even, w_in_even, sink, lb_param, hgrn_norm, w_out_even, norm_odd, w_in_odd, w_gate_up, b_gate, gla_norm, w_out_odd, mem_norm, w_mem_kv, final_norm, loss_target, m_norm_even, m_w_in_even, m_sink, m_lb_param, m_hgrn_norm, m_w_out_even, m_norm_odd, m_w_in_odd, m_w_gate_up, m_b_gate, m_gla_norm, m_w_out_odd, m_mem_norm, m_w_mem_kv, m_final_norm, v_norm_even, v_w_in_even, v_sink, v_lb_param, v_hgrn_norm, v_w_out_even, v_norm_odd, v_w_in_odd, v_w_gate_up, v_b_gate, v_gla_norm, v_w_out_odd, v_mem_norm, v_w_mem_kv, v_final_norm):
    w = dict(zip(WEIGHT_NAMES, (norm_even, w_in_even, sink, lb_param, hgrn_norm, w_out_even, norm_odd, w_in_odd,
                                w_gate_up, b_gate, gla_norm, w_out_odd, mem_norm, w_mem_kv, final_norm)))
    m = dict(zip(WEIGHT_NAMES, (m_norm_even, m_w_in_even, m_sink, m_lb_param, m_hgrn_norm, m_w_out_even, m_norm_odd,
                                m_w_in_odd, m_w_gate_up, m_b_gate, m_gla_norm, m_w_out_odd, m_mem_norm, m_w_mem_kv,
                                m_final_norm)))
    v = dict(zip(WEIGHT_NAMES, (v_norm_even, v_w_in_even, v_sink, v_lb_param, v_hgrn_norm, v_w_out_even, v_norm_odd,
                                v_w_in_odd, v_w_gate_up, v_b_gate, v_gla_norm, v_w_out_odd, v_mem_norm, v_w_mem_kv,
                                v_final_norm)))
    ci = lax.axis_index("c").astype(jnp.int32).reshape(1)
    chip = (2 * lax.axis_index("x") + lax.axis_index("y")).astype(jnp.int32).reshape(1)

    shards = []
    for l in range(DEPTH):
        names = ("w_in_even", "w_out_even") if l % 2 == 0 else ("w_in_odd", "w_out_odd")
        shards.append([w[names[0]][l // 2].astype(BF16), w[names[1]][l // 2].astype(BF16), w_mem_kv[l].astype(BF16)])
    small = _pack_small_local(w)
    own = lambda g, s: lax.dynamic_update_slice(g, s[None], (chip[0], 0, 0))
    first = [own(g, s) for g, s in zip(gather_weights(shards[0], small), shards[0] + [small])]
    later_shards = shards[1] + shards[2] + shards[3]
    later_raw = gather_weights_async(later_shards)
    w0 = _layer_weights(0, *first[0:3])
    W = dict(w_in_e=[w0[0]], w_out_e=[w0[1]], w_kv=[w0[2]])
    W.update(_unpack_small_full(first[3]))
    W.update({n: w[n] for n in ("norm_even", "sink", "hgrn_norm", "mem_norm", "final_norm")})

    def later(x1, W):
        x1, raw = lax.optimization_barrier((x1, list(later_raw)))
        g = [own(a, s) for a, s in zip(raw, later_shards)]
        w1, w2, w3 = (_layer_weights(l, *g[3 * (l - 1):3 * l]) for l in (1, 2, 3))
        W = dict(W)
        W.update(w_in_e=[w0[0], w2[0]], w_in_o=[w1[0], w3[0]], w_out_e=[w0[1], w2[1]], w_out_o=[w1[1], w3[1]],
                 w_kv=[w0[2], w1[2], w2[2], w3[2]])
        return x1, W

    place = jnp.concatenate([chip, ci])

    def start(tag, blocks, wire):
        return dict(tag=tag, blocks=blocks, wire=wire, step=0,
                    recv=exchange_siblings(f"rs_siblings_{tag}", blocks, True, 2))

    def advance(p):
        if p["step"] == 0:
            sums = [add_sibling(g, r, ci, dt) for g, r, dt in zip(p["blocks"], p["recv"], p["wire"])]
            p["recv3"] = exchange_chips(f"rs_chips_{p['tag']}", sums, 3)
        else:
            p["mine"] = [add_chips(g, r, r3, place) for g, r, r3 in zip(p["blocks"], p["recv"], p["recv3"])]
            p["other"] = exchange_siblings(f"rs_final_{p['tag']}", p["mine"], False, 4)
        p["step"] += 1

    pipes, first_layer = [], {}

    def sync(a):
        for p in pipes:
            if p["step"] < 3:
                key = ("recv", "recv3", "other")[p["step"]]
                a, arrived = lax.optimization_barrier((a, list(p[key])))
                p[key] = arrived
                if p["step"] < 2:
                    advance(p)
                else:
                    p["step"] = 3
        return a

    def on_layer_grads(l, dx, gl):
        dx = sync(dx)
        if l == 0:
            first_layer.update(gl)
        else:
            pipes.append(start(f"l{l}", _layer_grad_blocks(l, gl), [BF16] * 3))
        return dx

    loss_tile, dx, grads = local_step(x[0], mem[0], loss_target[0], W, later, on_layer_grads, sync)
    pipes.append(start("l0", _layer_grad_blocks(0, first_layer) + [_pack_small_blocks(grads)], [BF16] * 3 + [F32]))
    while any(p["step"] < 2 for p in pipes):
        for p in pipes:
            if p["step"] < 2:
                advance(p)
    by_layer = {int(p["tag"][1:]): p for p in pipes}
    halves = lambda layers, k: (jnp.stack([by_layer[l]["mine"][k] for l in layers]),
                                jnp.stack([by_layer[l]["other"][k] for l in layers]))
    big = dict(w_in_even=halves((0, 2), 0), w_in_odd=halves((1, 3), 0), w_out_even=halves((0, 2), 1),
               w_out_odd=halves((1, 3), 1), w_mem_kv=halves((0, 1, 2, 3), 2))
    s_mine, s_other = by_layer[0]["mine"][3], by_layer[0]["other"][3]
    g_small = jnp.where(ci[0] == 0, jnp.concatenate([s_mine, s_other]), jnp.concatenate([s_other, s_mine]))
    gl = _unpack_small_local(g_small)

    pack = jnp.zeros((8, D_MODEL), F32)
    pack = pack.at[0:2].set(grads["norm_even"]).at[2].set(grads["hgrn_norm"].reshape(-1))
    pack = pack.at[3].set(grads["mem_norm"]).at[4].set(grads["final_norm"])
    pack = pack.at[5, 0:16].set(grads["sink"].reshape(-1)).at[5, 16].set(loss_tile[0, 0])
    tot = sum_devices(allgather_small(pack))
    gl.update(norm_even=tot[0:2], hgrn_norm=tot[2].reshape(2, W_B), mem_norm=tot[3], final_norm=tot[4],
              sink=tot[5, 0:16].reshape(2, N_Q_A))
    loss = tot[5, 16]

    upd = {}
    for n in WEIGHT_NAMES:
        if n in big:
            gl[n], *upd[n] = adamw_halves(w[n], *big[n], m[n], v[n], ci)
        else:
            upd[n] = adamw_call(w[n], gl[n], m[n], v[n])
    return (loss, dx[None], *[gl[n] for n in WEIGHT_NAMES], *[upd[n][0] for n in WEIGHT_NAMES],
            *[upd[n][1] for n in WEIGHT_NAMES], *[upd[n][2] for n in WEIGHT_NAMES])
```

```python
import functools

import numpy as np
import jax
import jax.numpy as jnp
from jax import lax
from jax.experimental import pallas as pl
from jax.experimental.pallas import tpu as pltpu
from jax.experimental.pallas import tpu_sc as plsc

F32 = jnp.float32
BF16 = jnp.bfloat16

D_MODEL = 1024
DEPTH = 4
N_Q_A, N_KV_A, HEAD_DIM_A = 8, 2, 64
W_A, W_KV_A = 512, 128
WINDOW = 128
BLOCK = 128
N_HEADS_B, HEAD_DIM_B, W_B = 4, 128, 512
N_HEADS_C, DK_C, DV_C, WK_C, WV_C = 4, 128, 256, 512, 1024
GATE_RANK = 16
GATE_TEMP = 16.0
N_MEM, N_HEADS_M, HEAD_DIM_M, W_M = 256, 4, 128, 512
EPS = 1e-6
MASK_VALUE = -1e30
MIN_GATE = 1e-30
EVEN_IN, ODD_IN = 4864, 4128
ODD_PAD = 4224
MIX = 1536
ADAM_LR, ADAM_B1, ADAM_B2, ADAM_EPS, ADAM_WD, ADAM_STEP = 0.001, 0.9, 0.999, 1e-08, 0.01, 10

SCAN_CHUNK = 128
SCAN_LEVELS = 7
VMEM_LIMIT = 56 * 1024 * 1024

EVEN_REF_OFF = dict(qA=0, kA=512, vA=640, gA=768, qB=1280, zf=1792, zb=2304, iB=2816, gB=3328, qM=3840, gM=4352)
EVEN_W = dict(qA=512, kA=128, vA=128, gA=512, qB=512, zf=512, zb=512, iB=512, gB=512, qM=512, gM=512)
EVEN_ORDER = ("qA", "gA", "qB", "zf", "zb", "iB", "gB", "qM", "gM", "kA", "vA")
ODD_REF_OFF = dict(qC=0, kC=512, vC=1024, gC=2048, rr=3072, qM=3104, gM=3616)
ODD_W = dict(qC=512, kC=512, vC=1024, gC=1024, rr=32, qM=512, gM=512)
ODD_ORDER = ("qC", "kC", "vC", "gC", "qM", "gM", "rr")


def _offsets(order, widths):
    off, o = {}, 0
    for n in order:
        off[n] = o
        o += widths[n]
    return off


EVEN_OFF = _offsets(EVEN_ORDER, EVEN_W)
ODD_OFF = _offsets(ODD_ORDER, ODD_W)


def _dg(a, b, ca, cb):
    return lax.dot_general(a.astype(BF16), b.astype(BF16), (((ca,), (cb,)), ((), ())),
                           preferred_element_type=F32)


def dot_nn(a, b):
    return _dg(a, b, 1, 0)


def dot_nt(a, b):
    return _dg(a, b, 1, 1)


def dot_tn(a, b):
    return _dg(a, b, 0, 0)


@jax.custom_vjp
def bdot(a, b):
    return dot_nn(a, b)


bdot.defvjp(lambda a, b: (dot_nn(a, b), (a, b)),
            lambda r, g: (dot_nt(g, r[1]), dot_tn(r[0], g)))


@jax.custom_vjp
def bdot_t(a, b):
    return dot_nt(a, b)


bdot_t.defvjp(lambda a, b: (dot_nt(a, b), (a, b)),
              lambda r, g: (dot_nn(g, r[1]), dot_tn(g, r[0])))


@jax.custom_vjp
def bdot_tn(a, b):
    return dot_tn(a, b)


bdot_tn.defvjp(lambda a, b: (dot_tn(a, b), (a, b)),
               lambda r, g: (dot_nt(r[1], g), dot_nn(r[0], g)))


def _split_mm(h, x):
    hi = x.astype(BF16)
    lo = (x - hi.astype(F32)).astype(BF16)
    return (lax.dot_general(h, hi, (((1,), (0,)), ((), ())), preferred_element_type=F32)
            + lax.dot_general(h, lo, (((1,), (0,)), ((), ())), preferred_element_type=F32))


def _sigmoid(z):
    return 1.0 / (1.0 + jnp.exp(-z))


def _silu(z):
    return z * _sigmoid(z)


def _log_sigmoid(z):
    return jnp.minimum(z, 0.0) - jnp.log(1.0 + jnp.exp(-jnp.abs(z)))


def _rms(x, g):
    return x * lax.rsqrt(jnp.mean(x * x, axis=-1, keepdims=True) + EPS) * g


def rms_tile(x, g):
    return (_rms(x, g),)


@functools.partial(jax.custom_vjp, nondiff_argnums=(1, 2))
def split(x, n, axis):
    w = x.shape[axis] // n
    return tuple(lax.slice_in_dim(x, h * w, (h + 1) * w, axis=axis) for h in range(n))


split.defvjp(lambda x, n, axis: (split(x, n, axis), None),
             lambda n, axis, _, cts: (jnp.concatenate(cts, axis=axis),))


def _group_rms(o, g, heads):
    return jnp.concatenate([_rms(oh, gh) for oh, gh in zip(split(o, heads, 1), split(g, heads, 1))], axis=-1)


def even_post_tile(a, o2f, o2b, mo, gA, gB, gM, hg):
    y = _group_rms(o2f + o2b, hg, N_HEADS_B)
    return (jnp.concatenate([a * _silu(gA), y * _silu(gB), mo * _silu(gM)], axis=-1),)


def odd_post_tile(o2f, o2b, mo, gC, gM, gg):
    y = _group_rms(o2f + o2b, gg, N_HEADS_C)
    return (jnp.concatenate([y * _silu(gC), mo * _silu(gM)], axis=-1),)


def hgrn_prep(raw, par):
    qB, z, iB = raw
    (lb,) = par
    f = lb + (1.0 - lb) * _sigmoid(z)
    return _silu(qB), (1.0 - lb) * _sigmoid(-z), iB, jnp.log(jnp.maximum(f, MIN_GATE))


def gla_prep(raw, par):
    qC, kC, vC, r128 = raw
    wup, bg = par
    return qC * (DK_C ** -0.5), kC, vC, _log_sigmoid(bdot(r128, wup) + bg) / GATE_TEMP


def mem_tile(q, k, v):
    s = bdot_t(q, k) * (HEAD_DIM_M ** -0.5)
    m = lax.stop_gradient(jnp.max(s, axis=-1, keepdims=True))
    p = jnp.exp(s - m)
    p = p / jnp.sum(p, axis=-1, keepdims=True)
    return (bdot(p, v),)


ATTN_GROUP = N_Q_A // N_KV_A


def attn_block(q, ks, vs, sink, slope, c, seq):
    rows = ATTN_GROUP * BLOCK
    i = lax.broadcasted_iota(jnp.int32, (rows, 3 * BLOCK), 0) % BLOCK
    j = lax.broadcasted_iota(jnp.int32, (rows, 3 * BLOCK), 1)
    dist = jnp.abs(i - j + BLOCK).astype(F32)
    kpos = (c - 1) * BLOCK + j
    valid = (dist <= WINDOW) & (kpos >= 0) & (kpos < seq)
    s = bdot_t(q, ks) * (HEAD_DIM_A ** -0.5)
    s = jnp.where(valid, s - slope * dist, MASK_VALUE)
    m = lax.stop_gradient(jnp.maximum(jnp.max(s, axis=-1, keepdims=True), sink))
    p = jnp.where(valid, jnp.exp(s - m), 0.0)
    denom = jnp.sum(p, axis=-1, keepdims=True) + jnp.exp(sink - m)
    return bdot(p, vs) / denom


def scan_chunk(q, k, v, e, tot, st, qm, pm):
    C = SCAN_CHUNK
    e = split(e, 2 + SCAN_LEVELS, 0)
    qe = q * jnp.exp(e[0])
    kd = k * jnp.exp(e[1])
    r = lax.broadcasted_iota(jnp.int32, (C, C), 0)
    s = lax.broadcasted_iota(jnp.int32, (C, C), 1)
    a = jnp.where(r == s, jnp.sum(q * k, axis=-1, keepdims=True), 0.0)
    for l in range(SCAN_LEVELS):
        u = jnp.where(qm[l * C:(l + 1) * C] != 0.0, q, k) * jnp.exp(e[2 + l])
        a = a + bdot_t(u, u) * pm[l * C:(l + 1) * C]
    o = bdot_t(qe, st) + bdot(a, v)
    st_new = st * jnp.exp(tot) + bdot_tn(v, kd)
    return o, st_new


def _scan_consts():
    C, L = SCAN_CHUNK, SCAN_LEVELS
    t = np.arange(C)[:, None]
    r = np.arange(C)[None, :]
    blocks = [(r <= t), (r > t)]
    qms, pms = [], []
    for l in range(1, L + 1):
        m = C >> l
        upper_t = (t % (2 * m)) >= m
        upper_r = (r % (2 * m)) >= m
        same_half = (t // m) == (r // m)
        blocks.append(same_half & np.where(upper_t, r <= t, r > t))
        qms.append(np.broadcast_to(upper_t, (C, C)))
        pms.append(((t // (2 * m)) == (r // (2 * m))) & upper_t & ~upper_r)
    hf = np.concatenate(blocks, axis=0).astype(np.float32)
    flip = lambda mat: mat.reshape(-1, C, C)[:, ::-1, ::-1].reshape(-1, C)
    qmf = np.concatenate(qms, axis=0).astype(np.float32)
    pmf = np.concatenate(pms, axis=0).astype(np.float32)
    h = np.stack([hf, flip(hf)])
    ht = np.stack([h[0].T, h[1].T])
    qm = np.stack([qmf, 1.0 - qmf])
    pm = np.stack([pmf, flip(pmf)])
    return h, ht, qm, pm


def _cparams(sem):
    return pltpu.CompilerParams(dimension_semantics=sem, vmem_limit_bytes=VMEM_LIMIT)


def _row_tile(T):
    return min(T, 512)


def _in_spec(spec, tr):
    kind = spec[0]
    if kind == "row":
        _, arr, off, w = spec
        assert off % w == 0
        return arr, pl.BlockSpec((tr, w), functools.partial(lambda i, b: (i, b), b=off // w))
    if kind == "row3":
        _, arr, d, off, w = spec
        assert off % w == 0
        return arr, pl.BlockSpec((None, tr, w), functools.partial(lambda i, d, b: (d, i, b), d=d, b=off // w))
    _, arr = spec
    return arr, pl.BlockSpec(arr.shape, functools.partial(lambda i, n: (0,) * n, n=arr.ndim))


def rows_call(name, tile_fn, T, ins, out_widths, out_dtypes=None, stacks=None):
    tr = _row_tile(T)
    n_in = len(ins)
    out_dtypes = out_dtypes or [F32] * len(out_widths)
    stacks = stacks or [(k,) for k in range(len(out_widths))]

    def body(*refs):
        vals = [r[...] for r in refs[:n_in]]
        outs = tile_fn(*vals)
        for r, members in zip(refs[n_in:], stacks):
            if len(members) == 1:
                r[...] = outs[members[0]].astype(r.dtype)
            else:
                for d, k in enumerate(members):
                    r[d] = outs[k].astype(r.dtype)

    in_specs, args = [], []
    for spec in ins:
        arr, bs = _in_spec(spec, tr)
        args.append(arr)
        in_specs.append(bs)
    out_specs, out_shape = [], []
    for w, dt, members in zip(out_widths, out_dtypes, stacks):
        n = len(members)
        if n == 1:
            out_specs.append(pl.BlockSpec((tr, w), lambda i: (i, 0)))
            out_shape.append(jax.ShapeDtypeStruct((T, w), dt))
        else:
            out_specs.append(pl.BlockSpec((n, tr, w), lambda i: (0, i, 0)))
            out_shape.append(jax.ShapeDtypeStruct((n, T, w), dt))
    return pl.pallas_call(body, out_shape=out_shape, grid=(T // tr,), in_specs=in_specs, out_specs=out_specs,
                          name=name, compiler_params=_cparams(("arbitrary",)))(*args)


def rows_vjp_call(name, tile_fn, T, ins, cts, skip=(), narrow=()):
    tr = _row_tile(T)
    n_in = len(ins)
    n_ct = [len(c) for c in cts]
    want = [k for k in range(n_in) if k not in skip]

    def body(*refs):
        i = pl.program_id(0)
        vals = [r[...] for r in refs[:n_in]]
        ct, pos = [], n_in
        for n in n_ct:
            acc = refs[pos][...]
            for r in refs[pos + 1:pos + n]:
                acc = acc + r[...]
            ct.append(acc)
            pos += n
        _, vjp = jax.vjp(tile_fn, *vals)
        grads = vjp(tuple(ct))
        for r, k in zip(refs[pos:], want):
            if ins[k][0] == "full":
                @pl.when(i == 0)
                def _():
                    r[...] = jnp.zeros_like(r)
                r[...] += grads[k]
            else:
                r[...] = grads[k].astype(r.dtype)

    in_specs, args = [], []
    for spec in list(ins) + [s for c in cts for s in c]:
        arr, bs = _in_spec(spec, tr)
        args.append(arr)
        in_specs.append(bs)
    out_specs, out_shape = [], []
    for k in want:
        if ins[k][0] == "full":
            arr = ins[k][1]
            out_specs.append(pl.BlockSpec(arr.shape, functools.partial(lambda i, n: (0,) * n, n=arr.ndim)))
            out_shape.append(jax.ShapeDtypeStruct(arr.shape, F32))
        else:
            w = ins[k][-1]
            out_specs.append(pl.BlockSpec((tr, w), lambda i: (i, 0)))
            out_shape.append(jax.ShapeDtypeStruct((T, w), BF16 if k in narrow else F32))
    return pl.pallas_call(body, out_shape=out_shape, grid=(T // tr,), in_specs=in_specs, out_specs=out_specs,
                          name=name, compiler_params=_cparams(("arbitrary",)))(*args)


def matmul(name, a, b, mode, add=None, out_dtype=F32):
    if mode == "tn":
        K, M = a.shape
        N = b.shape[1]
        tm = M if M <= 1536 else 512
        tn = N if N <= 1280 else (N // 2 if (N // 2) % 128 == 0 else N)
        tk = min(K, 512)
        grid = (M // tm, N // tn, K // tk)

        def body(a_ref, b_ref, o_ref):
            @pl.when(pl.program_id(2) == 0)
            def _():
                o_ref[...] = jnp.zeros_like(o_ref)
            o_ref[...] += dot_tn(a_ref[...], b_ref[...])

        return pl.pallas_call(
            body, out_shape=jax.ShapeDtypeStruct((M, N), F32), grid=grid,
            in_specs=[pl.BlockSpec((tk, tm), lambda i, j, k: (k, i)), pl.BlockSpec((tk, tn), lambda i, j, k: (k, j))],
            out_specs=pl.BlockSpec((tm, tn), lambda i, j, k: (i, j)), name=name,
            compiler_params=_cparams(("arbitrary", "arbitrary", "arbitrary")))(a, b)

    M, K = a.shape
    N = b.shape[1] if mode == "nn" else b.shape[0]
    tm = min(M, 512)
    tn = N if N <= 1536 else (N // 2 if (N // 2) % 128 == 0 else (N // 3 if (N // 3) % 128 == 0 else N))
    grid = (N // tn, M // tm)
    n_in = 2 + (add is not None)

    def body(*refs):
        a_ref, b_ref = refs[0], refs[1]
        o_ref = refs[n_in]
        acc = dot_nn(a_ref[...], b_ref[...]) if mode == "nn" else dot_nt(a_ref[...], b_ref[...])
        if add is not None:
            acc = acc + refs[2][...]
        o_ref[...] = acc.astype(o_ref.dtype)

    in_specs = [pl.BlockSpec((tm, K), lambda j, i: (i, 0)),
                pl.BlockSpec((K, tn), lambda j, i: (0, j)) if mode == "nn" else pl.BlockSpec((tn, K), lambda j, i: (j, 0))]
    args = [a, b]
    if add is not None:
        in_specs.append(pl.BlockSpec((tm, tn), lambda j, i: (i, j)))
        args.append(add)
    return pl.pallas_call(
        body, out_shape=jax.ShapeDtypeStruct((M, N), out_dtype), grid=grid, in_specs=in_specs,
        out_specs=pl.BlockSpec((tm, tn), lambda j, i: (i, j)), name=name,
        compiler_params=_cparams(("arbitrary", "arbitrary")))(*args)


def _attn_heads(n):
    G = N_Q_A // N_KV_A
    k_sl = pl.ds(n * HEAD_DIM_A, HEAD_DIM_A)
    v_sl = pl.ds(W_KV_A + n * HEAD_DIM_A, HEAD_DIM_A)
    q_sl = [pl.ds((n * G + g) * HEAD_DIM_A, HEAD_DIM_A) for g in range(G)]
    return k_sl, v_sl, q_sl, range(n * G, (n + 1) * G)


def attn_fwd(p, q_off, kvp, sink, slopes, T):
    nb = T // BLOCK
    assert q_off % W_A == 0

    def body(q_ref, kv_ref, sink_ref, slope_ref, o_ref):
        c = pl.program_id(0)
        rows = pl.ds(pl.multiple_of(c * BLOCK, BLOCK), 3 * BLOCK)
        for n in range(N_KV_A):
            k_sl, v_sl, q_sl, heads = _attn_heads(n)
            group = pl.ds(n * ATTN_GROUP * BLOCK, ATTN_GROUP * BLOCK)
            q = jnp.concatenate([q_ref[:, s] for s in q_sl], axis=0)
            o = attn_block(q, kv_ref[rows, k_sl], kv_ref[rows, v_sl], sink_ref[group, :], slope_ref[group, :], c, T)
            for g, s in enumerate(q_sl):
                o_ref[:, s] = o[g * BLOCK:(g + 1) * BLOCK]

    full = lambda a: pl.BlockSpec(a.shape, functools.partial(lambda c, nd: (0,) * nd, nd=a.ndim))
    return pl.pallas_call(
        body, out_shape=jax.ShapeDtypeStruct((T, W_A), F32), grid=(nb,),
        in_specs=[pl.BlockSpec((BLOCK, W_A), lambda c: (c, q_off // W_A)), full(kvp), full(sink), full(slopes)],
        out_specs=pl.BlockSpec((BLOCK, W_A), lambda c: (c, 0)),
        name="attn_fwd", compiler_params=_cparams(("arbitrary",)))(p, kvp, sink, slopes)


def attn_bwd(p, q_off, kvp, sink, slopes, do, T):
    nb = T // BLOCK

    def body(q_ref, kv_ref, sink_ref, slope_ref, do_ref, dq_ref, dkv_ref, dsink_ref):
        c = pl.program_id(0)

        @pl.when(c == 0)
        def _():
            dkv_ref[...] = jnp.zeros_like(dkv_ref)
            dsink_ref[...] = jnp.zeros_like(dsink_ref)

        rows = pl.ds(pl.multiple_of(c * BLOCK, BLOCK), 3 * BLOCK)
        for n in range(N_KV_A):
            k_sl, v_sl, q_sl, heads = _attn_heads(n)
            group = pl.ds(n * ATTN_GROUP * BLOCK, ATTN_GROUP * BLOCK)
            slope = slope_ref[group, :]
            q = jnp.concatenate([q_ref[:, s] for s in q_sl], axis=0)
            do = jnp.concatenate([do_ref[:, s] for s in q_sl], axis=0)
            _, vjp = jax.vjp(lambda q_, kk, vv, sk: attn_block(q_, kk, vv, sk, slope, c, T),
                             q, kv_ref[rows, k_sl], kv_ref[rows, v_sl], sink_ref[group, :])
            dq, dks, dvs, dsk = vjp(do)
            dkv_ref[rows, k_sl] += dks
            dkv_ref[rows, v_sl] += dvs
            for g, (s, h) in enumerate(zip(q_sl, heads)):
                seg = slice(g * BLOCK, (g + 1) * BLOCK)
                dq_ref[:, s] = dq[seg].astype(dq_ref.dtype)
                dsink_ref[h] += jnp.sum(dsk[seg], axis=0, keepdims=True)

    full = lambda a: pl.BlockSpec(a.shape, functools.partial(lambda c, nd: (0,) * nd, nd=a.ndim))
    qspec = pl.BlockSpec((BLOCK, W_A), lambda c: (c, 0))
    return pl.pallas_call(
        body,
        out_shape=[jax.ShapeDtypeStruct((T, W_A), BF16), jax.ShapeDtypeStruct(kvp.shape, F32),
                   jax.ShapeDtypeStruct((N_Q_A, 1, 1), F32)],
        grid=(nb,),
        in_specs=[pl.BlockSpec((BLOCK, W_A), lambda c: (c, q_off // W_A)), full(kvp), full(sink), full(slopes), qspec],
        out_specs=[qspec, full(kvp), pl.BlockSpec((N_Q_A, 1, 1), lambda c: (0, 0, 0))],
        name="attn_bwd", compiler_params=_cparams(("arbitrary",)))(p, kvp, sink, slopes, do)


def mem_fwd(p, q_off, kv, T):
    tr = min(T, 2 * _row_tile(T))
    assert q_off % W_M == 0

    def body(q_ref, kv_ref, o_ref):
        for h in range(N_HEADS_M):
            hs = pl.ds(h * HEAD_DIM_M, HEAD_DIM_M)
            (o,) = mem_tile(q_ref[:, hs], kv_ref[:, hs], kv_ref[:, pl.ds(W_M + h * HEAD_DIM_M, HEAD_DIM_M)])
            o_ref[:, hs] = o

    return pl.pallas_call(
        body, out_shape=jax.ShapeDtypeStruct((T, W_M), F32), grid=(T // tr,),
        in_specs=[pl.BlockSpec((tr, W_M), lambda i: (i, q_off // W_M)), pl.BlockSpec((N_MEM, 2 * W_M), lambda i: (0, 0))],
        out_specs=pl.BlockSpec((tr, W_M), lambda i: (i, 0)),
        name="mem_fwd", compiler_params=_cparams(("arbitrary",)))(p, kv)


def mem_bwd(p, q_off, kv, do, T):
    tr = min(T, 2 * _row_tile(T))

    def body(q_ref, kv_ref, do_ref, dq_ref, dkv_ref):
        @pl.when(pl.program_id(0) == 0)
        def _():
            dkv_ref[...] = jnp.zeros_like(dkv_ref)

        for h in range(N_HEADS_M):
            hs = pl.ds(h * HEAD_DIM_M, HEAD_DIM_M)
            vs = pl.ds(W_M + h * HEAD_DIM_M, HEAD_DIM_M)
            _, vjp = jax.vjp(mem_tile, q_ref[:, hs], kv_ref[:, hs], kv_ref[:, vs])
            dq, dk, dv = vjp((do_ref[:, hs],))
            dq_ref[:, hs] = dq.astype(dq_ref.dtype)
            dkv_ref[:, hs] += dk
            dkv_ref[:, vs] += dv

    kvspec = pl.BlockSpec((N_MEM, 2 * W_M), lambda i: (0, 0))
    return pl.pallas_call(
        body,
        out_shape=[jax.ShapeDtypeStruct((T, W_M), BF16), jax.ShapeDtypeStruct((N_MEM, 2 * W_M), F32)],
        grid=(T // tr,),
        in_specs=[pl.BlockSpec((tr, W_M), lambda i: (i, q_off // W_M)), kvspec, pl.BlockSpec((tr, W_M), lambda i: (i, 0))],
        out_specs=[pl.BlockSpec((tr, W_M), lambda i: (i, 0)), kvspec],
        name="mem_bwd", compiler_params=_cparams(("arbitrary",)))(p, kv, do)


def _scan_const_specs(dk):
    C, L = SCAN_CHUNK, SCAN_LEVELS
    return [pl.BlockSpec((2, (2 + L) * C, C), lambda n: (0, 0, 0)),
            pl.BlockSpec((2, C, (2 + L) * C), lambda n: (0, 0, 0)),
            pl.BlockSpec((2, L * C, dk), lambda n: (0, 0, 0)),
            pl.BlockSpec((2, L * C, C), lambda n: (0, 0, 0))]


def _chunk_spec(src, width, chunk_of):
    arr, sel = src
    if arr.ndim == 2:
        assert sel % width == 0
        return pl.BlockSpec((SCAN_CHUNK, width), functools.partial(lambda n, b: (chunk_of(n), b), b=sel // width))
    return pl.BlockSpec((None, SCAN_CHUNK, width), functools.partial(lambda n, d: (d, chunk_of(n), 0), d=sel))


def _scan_const_args():
    h, ht, qm, pm = _scan_consts()
    return [jnp.asarray(h, BF16), jnp.asarray(ht, BF16), jnp.asarray(qm, F32), jnp.asarray(pm, F32)]


def _full_spec(a):
    return pl.BlockSpec(a.shape, functools.partial(lambda n, nd: (0,) * nd, nd=a.ndim))


def scan_fwd(name, prep, raws, params, heads, dk, dv, T):
    C = SCAN_CHUNK
    N = T // C
    assert dk == C
    Wv = heads * dv
    orders = (lambda n: n, lambda n: N - 1 - n)
    n_raw, n_par = [len(r) for r in raws], [len(p) for p in params]

    def body(*refs):
        pos, raw_refs, par_refs = 0, [], []
        for d in range(2):
            raw_refs.append(refs[pos:pos + n_raw[d]])
            pos += n_raw[d]
        for d in range(2):
            par_refs.append(refs[pos:pos + n_par[d]])
            pos += n_par[d]
        h_ref, ht_ref, qm_ref, pm_ref = refs[pos:pos + 4]
        o_refs, ss_refs, st_ref = refs[pos + 4:pos + 6], refs[pos + 6:pos + 8], refs[pos + 8]

        @pl.when(pl.program_id(0) == 0)
        def _():
            st_ref[...] = jnp.zeros_like(st_ref)

        for d in range(2):
            consts = (qm_ref[d], pm_ref[d])
            q, k, v, g = prep([r[...] for r in raw_refs[d]], [p[...] for p in par_refs[d]])
            e = _split_mm(h_ref[d], g)
            tot = jnp.sum(g, axis=0, keepdims=True)
            for h in range(heads):
                ks, vs = slice(h * dk, (h + 1) * dk), slice(h * dv, (h + 1) * dv)
                st = st_ref[d, h]
                ss_refs[d][h] = st
                o, st_new = scan_chunk(q[:, ks], k[:, ks], v[:, vs], e[:, ks], tot[:, ks], st, *consts)
                o_refs[d][:, vs] = o
                st_ref[d, h] = st_new

    ss_spec = lambda order: pl.BlockSpec((heads, None, dv, dk), lambda n: (0, order(n), 0, 0))
    return pl.pallas_call(
        body,
        out_shape=[jax.ShapeDtypeStruct((T, Wv), F32)] * 2 + [jax.ShapeDtypeStruct((heads, N, dv, dk), F32)] * 2,
        grid=(N,),
        in_specs=[_chunk_spec(s, w, orders[d]) for d in range(2) for s, w in raws[d]]
        + [_full_spec(p) for d in range(2) for p in params[d]] + _scan_const_specs(dk),
        out_specs=[pl.BlockSpec((C, Wv), lambda n: (orders[0](n), 0)), pl.BlockSpec((C, Wv), lambda n: (orders[1](n), 0)),
                   ss_spec(orders[0]), ss_spec(orders[1])],
        scratch_shapes=[pltpu.VMEM((2, heads, dv, dk), F32)],
        name=name, compiler_params=_cparams(("arbitrary",)))(
            *[s[0] for d in range(2) for s, _ in raws[d]], *[p for d in range(2) for p in params[d]], *_scan_const_args())


def scan_bwd(name, prep, raws, params, ss, do, heads, dk, dv, T):
    C = SCAN_CHUNK
    N = T // C
    Wv = heads * dv
    orders = (lambda n: N - 1 - n, lambda n: n)
    n_raw, n_par = [len(r) for r in raws], [len(p) for p in params]

    def body(*refs):
        pos, raw_refs, par_refs, draw_refs, dpar_refs = 0, [], [], [], []
        for group, counts in ((raw_refs, n_raw), (par_refs, n_par)):
            for d in range(2):
                group.append(refs[pos:pos + counts[d]])
                pos += counts[d]
        ss_refs, do_refs = refs[pos:pos + 2], refs[pos + 2:pos + 4]
        h_ref, ht_ref, qm_ref, pm_ref = refs[pos + 4:pos + 8]
        pos += 8
        for group, counts in ((draw_refs, n_raw), (dpar_refs, n_par)):
            for d in range(2):
                group.append(refs[pos:pos + counts[d]])
                pos += counts[d]
        dst_ref = refs[pos]

        @pl.when(pl.program_id(0) == 0)
        def _():
            dst_ref[...] = jnp.zeros_like(dst_ref)
            for d in range(2):
                for r in dpar_refs[d]:
                    r[...] = jnp.zeros_like(r)

        for d in range(2):
            consts = (qm_ref[d], pm_ref[d])
            (q, k, v, g), prep_vjp = jax.vjp(prep, [r[...] for r in raw_refs[d]], [p[...] for p in par_refs[d]])
            e = _split_mm(h_ref[d], g)
            tot = jnp.sum(g, axis=0, keepdims=True)
            dqs, dks, dvs, des, dtots = [], [], [], [], []
            for h in range(heads):
                ks, vs = slice(h * dk, (h + 1) * dk), slice(h * dv, (h + 1) * dv)
                _, vjp = jax.vjp(lambda q_, k_, v_, e_, t_, st_: scan_chunk(q_, k_, v_, e_, t_, st_, *consts),
                                 q[:, ks], k[:, ks], v[:, vs], e[:, ks], tot[:, ks], ss_refs[d][h])
                dq, dk_, dv_, de, dtot, dst = vjp((do_refs[d][:, vs], dst_ref[d, h]))
                dst_ref[d, h] = dst
                for group, val in ((dqs, dq), (dks, dk_), (dvs, dv_), (des, de), (dtots, dtot)):
                    group.append(val)
            cat = lambda parts: jnp.concatenate(parts, axis=-1)
            dg = _split_mm(ht_ref[d], cat(des)) + cat(dtots)
            draws, dpars = prep_vjp((cat(dqs), cat(dks), cat(dvs), dg))
            for r, val in zip(draw_refs[d], draws):
                r[...] = val.astype(r.dtype)
            for r, val in zip(dpar_refs[d], dpars):
                r[...] += val

    ss_spec = lambda order: pl.BlockSpec((heads, None, dv, dk), lambda n: (0, order(n), 0, 0))
    row_out = lambda w, order: pl.BlockSpec((C, w), lambda n: (order(n), 0))
    return pl.pallas_call(
        body,
        out_shape=[jax.ShapeDtypeStruct((T, w), BF16) for d in range(2) for _, w in raws[d]]
        + [jax.ShapeDtypeStruct(p.shape, F32) for d in range(2) for p in params[d]],
        grid=(N,),
        in_specs=[_chunk_spec(s, w, orders[d]) for d in range(2) for s, w in raws[d]]
        + [_full_spec(p) for d in range(2) for p in params[d]]
        + [ss_spec(orders[0]), ss_spec(orders[1]), _chunk_spec(do, Wv, orders[0]), _chunk_spec(do, Wv, orders[1])]
        + _scan_const_specs(dk),
        out_specs=[row_out(w, orders[d]) for d in range(2) for _, w in raws[d]]
        + [_full_spec(p) for d in range(2) for p in params[d]],
        scratch_shapes=[pltpu.VMEM((2, heads, dv, dk), F32)],
        name=name, compiler_params=_cparams(("arbitrary",)))(
            *[s[0] for d in range(2) for s, _ in raws[d]], *[p for d in range(2) for p in params[d]],
            ss[0], ss[1], do[0], do[0], *_scan_const_args())


def final_call(x, g, target, T):
    tr = _row_tile(T)

    def tile(xv, gv, tv):
        y = _rms(xv, gv)
        err = (y - tv) ** 2
        return jnp.sum(jnp.sum(err, axis=-1, keepdims=True), axis=0, keepdims=True) * (0.5 / D_MODEL)

    def body(x_ref, g_ref, t_ref, loss_ref, dx_ref, dg_ref):
        i = pl.program_id(0)
        tv = t_ref[...]
        lv, vjp = jax.vjp(lambda a, b: tile(a, b, tv), x_ref[...], g_ref[...])
        dx, dg = vjp(jnp.ones((1, 1), F32))
        dx_ref[...] = dx

        @pl.when(i == 0)
        def _():
            loss_ref[...] = jnp.zeros_like(loss_ref)
            dg_ref[...] = jnp.zeros_like(dg_ref)

        loss_ref[...] += jnp.broadcast_to(lv, loss_ref.shape)
        dg_ref[...] += dg

    return pl.pallas_call(
        body,
        out_shape=[jax.ShapeDtypeStruct((8, 128), F32), jax.ShapeDtypeStruct((T, D_MODEL), F32),
                   jax.ShapeDtypeStruct((1, D_MODEL), F32)],
        grid=(T // tr,),
        in_specs=[pl.BlockSpec((tr, D_MODEL), lambda i: (i, 0)), pl.BlockSpec((1, D_MODEL), lambda i: (0, 0)),
                  pl.BlockSpec((tr, D_MODEL), lambda i: (i, 0))],
        out_specs=[pl.BlockSpec((8, 128), lambda i: (0, 0)), pl.BlockSpec((tr, D_MODEL), lambda i: (i, 0)),
                   pl.BlockSpec((1, D_MODEL), lambda i: (0, 0))],
        name="final_loss", compiler_params=_cparams(("arbitrary",)))(x, g, target)


def adamw_call(w, g, m, v):
    shape = w.shape
    c = shape[-1]
    r = int(np.prod(shape[:-1])) if len(shape) > 1 else 1
    tr = r if r <= 256 else 256
    assert r % tr == 0

    def body(w_ref, g_ref, m_ref, v_ref, d_ref, nm_ref, nv_ref):
        gv = g_ref[...]
        nm = ADAM_B1 * m_ref[...] + (1.0 - ADAM_B1) * gv
        nv = ADAM_B2 * v_ref[...] + (1.0 - ADAM_B2) * jnp.square(gv)
        m_hat = nm / (1.0 - ADAM_B1 ** ADAM_STEP)
        v_hat = nv / (1.0 - ADAM_B2 ** ADAM_STEP)
        d_ref[...] = -ADAM_LR * (m_hat / (jnp.sqrt(v_hat) + ADAM_EPS) + ADAM_WD * w_ref[...])
        nm_ref[...] = nm
        nv_ref[...] = nv

    spec = pl.BlockSpec((tr, c), lambda i: (i, 0))
    outs = pl.pallas_call(body, out_shape=[jax.ShapeDtypeStruct((r, c), F32)] * 3, grid=(r // tr,),
                          in_specs=[spec] * 4, out_specs=[spec] * 3, name="adamw",
                          compiler_params=_cparams(("arbitrary",)))(*(t.reshape(r, c) for t in (w, g, m, v)))
    return tuple(o.reshape(shape) for o in outs)


def adamw_halves(w, mine, other, m, v, c):
    L, R, C = w.shape
    rh = R // 2
    tr = rh if rh <= 256 else 256
    nbh = rh // tr

    def body(c_ref, w_ref, a_ref, b_ref, m_ref, v_ref, g_ref, d_ref, nm_ref, nv_ref):
        is_mine = (pl.program_id(1) // nbh) == c_ref[0]
        gv = jnp.where(is_mine, a_ref[...], b_ref[...])
        nm = ADAM_B1 * m_ref[...] + (1.0 - ADAM_B1) * gv
        nv = ADAM_B2 * v_ref[...] + (1.0 - ADAM_B2) * jnp.square(gv)
        m_hat = nm / (1.0 - ADAM_B1 ** ADAM_STEP)
        v_hat = nv / (1.0 - ADAM_B2 ** ADAM_STEP)
        g_ref[...] = gv
        d_ref[...] = -ADAM_LR * (m_hat / (jnp.sqrt(v_hat) + ADAM_EPS) + ADAM_WD * w_ref[...])
        nm_ref[...] = nm
        nv_ref[...] = nv

    full = pl.BlockSpec((None, tr, C), lambda l, i, c_ref: (l, i, 0))
    half = pl.BlockSpec((None, tr, C), lambda l, i, c_ref: (l, i % nbh, 0))
    grid_spec = pltpu.PrefetchScalarGridSpec(num_scalar_prefetch=1, grid=(L, R // tr),
                                             in_specs=[full, half, half, full, full], out_specs=[full] * 4)
    return pl.pallas_call(body, out_shape=[jax.ShapeDtypeStruct(w.shape, F32)] * 4, grid_spec=grid_spec,
                          name="adamw_halves", compiler_params=_cparams(("arbitrary", "arbitrary")))(c, w, mine, other, m, v)


def sum_devices(g64):
    def body(x_ref, o_ref):
        acc = x_ref[0:8, :]
        for d in range(1, 8):
            acc = acc + x_ref[8 * d:8 * d + 8, :]
        o_ref[...] = acc

    return pl.pallas_call(body, out_shape=jax.ShapeDtypeStruct((8, D_MODEL), F32), name="sum_devices")(g64)


def _half_tile(rh):
    return rh if rh <= 512 else 256


def add_sibling(g, recv, c, out_dtype):
    _, R, C = g.shape
    rh = R // 2
    tr = _half_tile(rh)
    nblk = rh // tr

    def body(c_ref, g_ref, r_ref, o_ref):
        o_ref[...] = (g_ref[...] + r_ref[...]).astype(o_ref.dtype)

    grid_spec = pltpu.PrefetchScalarGridSpec(
        num_scalar_prefetch=1, grid=(4, nblk),
        in_specs=[pl.BlockSpec((None, tr, C), lambda j, i, c_ref: (j, i + c_ref[0] * nblk, 0)),
                  pl.BlockSpec((None, tr, C), lambda j, i, c_ref: (j, i, 0))],
        out_specs=pl.BlockSpec((None, tr, C), lambda j, i, c_ref: (j, i, 0)))
    return pl.pallas_call(body, out_shape=jax.ShapeDtypeStruct((4, rh, C), out_dtype), grid_spec=grid_spec,
                          name="rs_add_sibling", compiler_params=_cparams(("arbitrary", "arbitrary")))(c, g, recv)


def add_chips(g, recv, r3, place):
    _, R, C = g.shape
    rh = R // 2
    tr = _half_tile(rh)
    nblk = rh // tr

    def body(p_ref, g_ref, s_ref, a_ref, b_ref, c_ref, o_ref):
        up = lambda r: r[...].astype(F32)
        o_ref[...] = (((g_ref[...] + up(s_ref)) + up(a_ref)) + up(b_ref)) + up(c_ref)

    grid_spec = pltpu.PrefetchScalarGridSpec(
        num_scalar_prefetch=1, grid=(nblk,),
        in_specs=[pl.BlockSpec((None, tr, C), lambda i, p_ref: (p_ref[0], i + p_ref[1] * nblk, 0)),
                  pl.BlockSpec((None, tr, C), lambda i, p_ref: (p_ref[0], i, 0))]
        + [pl.BlockSpec((None, tr, C), functools.partial(lambda i, p_ref, k: (k, i, 0), k=k)) for k in range(3)],
        out_specs=pl.BlockSpec((tr, C), lambda i, p_ref: (i, 0)))
    return pl.pallas_call(body, out_shape=jax.ShapeDtypeStruct((rh, C), F32), grid_spec=grid_spec,
                          name="rs_add_chips", compiler_params=_cparams(("arbitrary",)))(place, g, recv, r3, r3, r3)


def _remote(src, dst, ssem, rsem, dev):
    return pltpu.make_async_remote_copy(src_ref=src, dst_ref=dst, send_sem=ssem, recv_sem=rsem,
                                        device_id=dev, device_id_type=pl.DeviceIdType.MESH)


def _mesh_places():
    x, y, c = lax.axis_index("x"), lax.axis_index("y"), lax.axis_index("c")
    chips = [(1 - x, y), (x, 1 - y), (1 - x, 1 - y)]
    return x, y, c, (x, y, 1 - c), chips


def _hbm_specs(n):
    return [pl.BlockSpec(memory_space=pltpu.HBM) for _ in range(n)]


def _gather_body(ins, outs, n_split, send_sems, recv_sems, handshake):
    x, y, c, sibling, chips = _mesh_places()
    mine = 2 * x + y
    if handshake:
        barrier = pltpu.get_barrier_semaphore()
        peers = [sibling] + [(*chip, c) for chip in chips]
        for peer in peers:
            pl.semaphore_signal(barrier, inc=1, device_id=peer, device_id_type=pl.DeviceIdType.MESH)
        pl.semaphore_wait(barrier, len(peers))

    def half(a, chip_idx, which):
        rh = ins[a].shape[0] // 2
        return outs[a].at[chip_idx, pl.ds(which * rh, rh), :]

    sent = []
    for a in range(len(ins)):
        for k, chip in enumerate(chips):
            if a < n_split:
                rh = ins[a].shape[0] // 2
                src, dst = ins[a].at[pl.ds(c * rh, rh), :], half(a, mine, c)
            else:
                src, dst = ins[a], outs[a].at[mine]
            sent.append(_remote(src, dst, send_sems.at[a, k], recv_sems.at[a, k], (*chip, c)))
    for cp in sent:
        cp.start()
    for a in range(len(ins)):
        for k, chip in enumerate(chips):
            j = 2 * chip[0] + chip[1]
            region = half(a, j, c) if a < n_split else outs[a].at[j]
            _remote(region, region, send_sems.at[a, k], recv_sems.at[a, k], (*chip, c)).wait_recv()
            if a < n_split:
                fwd = _remote(region, region, send_sems.at[a, 3 + k], recv_sems.at[a, 3 + k], sibling)
                fwd.start()
                sent.append(fwd)
    for a in range(n_split):
        for k, chip in enumerate(chips):
            region = half(a, 2 * chip[0] + chip[1], 1 - c)
            _remote(region, region, send_sems.at[a, 3 + k], recv_sems.at[a, 3 + k], sibling).wait_recv()
    for cp in sent:
        cp.wait_send()


def gather_weights(shards, small):
    arrs = list(shards) + [small]
    n = len(arrs)

    def body(*refs):
        _gather_body(refs[:n], refs[n:2 * n], n - 1, refs[2 * n], refs[2 * n + 1], handshake=False)

    return pl.pallas_call(
        body, out_shape=[jax.ShapeDtypeStruct((4,) + a.shape, a.dtype) for a in arrs],
        in_specs=_hbm_specs(n), out_specs=_hbm_specs(n),
        scratch_shapes=[pltpu.SemaphoreType.DMA((n, 6)), pltpu.SemaphoreType.DMA((n, 6))],
        name="gather_weights")(*arrs)


def gather_weights_async(shards):
    n = len(shards)

    def body(*refs):
        _gather_body(refs[:n], refs[n:2 * n], n, refs[2 * n], refs[2 * n + 1], handshake=True)

    return pl.kernel(
        body, out_type=[jax.ShapeDtypeStruct((4,) + a.shape, a.dtype) for a in shards],
        mesh=plsc.ScalarSubcoreMesh(axis_name="seq", num_cores=1),
        scratch_types=[pltpu.SemaphoreType.DMA((n, 6)), pltpu.SemaphoreType.DMA((n, 6))],
        compiler_params=pltpu.CompilerParams(collective_id=1), name="gather_weights_async")(*shards)


def _sequencer_call(name, body, out_type, sem_shape, collective_id, args):
    return pl.kernel(
        body, out_type=out_type, mesh=plsc.ScalarSubcoreMesh(axis_name="seq", num_cores=1),
        scratch_types=[pltpu.SemaphoreType.DMA(sem_shape), pltpu.SemaphoreType.DMA(sem_shape)],
        compiler_params=pltpu.CompilerParams(collective_id=collective_id), name=name)(*args)


def _handshake(peers):
    barrier = pltpu.get_barrier_semaphore()
    for peer in peers:
        pl.semaphore_signal(barrier, inc=1, device_id=peer, device_id_type=pl.DeviceIdType.MESH)
    pl.semaphore_wait(barrier, len(peers))


def exchange_siblings(name, srcs, halves, collective_id):
    n = len(srcs)

    def body(*refs):
        ins, outs = refs[:n], refs[n:2 * n]
        send_sems, recv_sems = refs[2 * n:]
        x, y, c, sibling, chips = _mesh_places()
        _handshake([sibling])
        cps = []
        for a in range(n):
            src = ins[a]
            if halves:
                rh = src.shape[1] // 2
                src = src.at[:, pl.ds((1 - c) * rh, rh), :]
            cps.append(_remote(src, outs[a], send_sems.at[a], recv_sems.at[a], sibling))
        for cp in cps:
            cp.start()
        for cp in cps:
            cp.wait()

    shape = lambda g: (4, g.shape[1] // 2, g.shape[2]) if halves else g.shape
    return _sequencer_call(name, body, [jax.ShapeDtypeStruct(shape(g), g.dtype) for g in srcs], (n,), collective_id, srcs)


def exchange_chips(name, s1s, collective_id):
    n = len(s1s)

    def body(*refs):
        ins, outs = refs[:n], refs[n:2 * n]
        send_sems, recv_sems = refs[2 * n:]
        x, y, c, sibling, chips = _mesh_places()
        _handshake([(*chip, c) for chip in chips])
        cps = []
        for a in range(n):
            for k, chip in enumerate(chips):
                cps.append(_remote(ins[a].at[2 * chip[0] + chip[1]], outs[a].at[k], send_sems.at[a, k],
                                   recv_sems.at[a, k], (*chip, c)))
        for cp in cps:
            cp.start()
        for cp in cps:
            cp.wait()

    return _sequencer_call(name, body, [jax.ShapeDtypeStruct((3,) + s.shape[1:], s.dtype) for s in s1s], (n, 3),
                           collective_id, s1s)


def allgather_small(v):
    m_per = v.shape[0]

    def body(x_ref, out_ref, send_sems, recv_sems, local_sem):
        x, y, c, sibling, chips = _mesh_places()
        me = (x, y, c)

        def rows(px, py, pc):
            return out_ref.at[pl.ds((4 * px + 2 * py + pc) * m_per, m_per), :]

        def copy(k, block, to, src=None):
            return _remote(rows(*block) if src is None else src, rows(*block), send_sems.at[k], recv_sems.at[k], to)

        mine = pltpu.make_async_copy(x_ref, rows(*me), local_sem)
        mine.start()
        first = [copy(0, me, sibling, src=x_ref)]
        first += [copy(1 + j, me, (*chip, c), src=x_ref) for j, chip in enumerate(chips)]
        for cp in first:
            cp.start()
        passed = [copy(4 + j, (*chip, c), sibling) for j, chip in enumerate(chips)]
        for j, chip in enumerate(chips):
            copy(1 + j, (*chip, c), me).wait_recv()
            passed[j].start()
        copy(0, sibling, me).wait_recv()
        for j, chip in enumerate(chips):
            copy(4 + j, (*chip, 1 - c), me).wait_recv()
        for cp in first + passed:
            cp.wait_send()
        mine.wait()

    return pl.pallas_call(
        body, out_shape=jax.ShapeDtypeStruct((8 * m_per, v.shape[1]), v.dtype),
        in_specs=[pl.BlockSpec(memory_space=pltpu.VMEM)], out_specs=pl.BlockSpec(memory_space=pltpu.VMEM),
        scratch_shapes=[pltpu.SemaphoreType.DMA((7,)), pltpu.SemaphoreType.DMA((7,)), pltpu.SemaphoreType.DMA],
        name="allgather_small")(v)


def rms_res_tile(x, g):
    return (_rms(x, g), x)


def _lower_bounds(lb_param):
    lbs = jax.nn.softmax(lb_param.astype(F32), axis=0)
    return jnp.cumsum(lbs, axis=0) - lbs[0]


def _even_fwd(x, i, W, lower, kv, slopes, T):
    O = EVEN_OFF
    g = W["norm_even"][i].reshape(1, D_MODEL)
    (h,) = rows_call("rms_fwd", rms_tile, T, [("row", x, 0, D_MODEL), ("full", g)], [D_MODEL], [BF16])
    p = matmul("mm_in_e", h, W["w_in_e"][i], "nn")
    kvp = jnp.pad(p[:, O["kA"]:O["kA"] + 2 * W_KV_A], ((BLOCK, BLOCK), (0, 0)))
    sink = jnp.repeat(W["sink"][i], BLOCK).reshape(N_Q_A * BLOCK, 1)
    a = attn_fwd(p, O["qA"], kvp, sink, slopes, T)
    scan_raws = [[((p, O["qB"]), W_B), ((p, O[z]), W_B), ((p, O["iB"]), W_B)] for z in ("zf", "zb")]
    scan_pars = [[lower[i][0:1]], [lower[i][1:2]]]
    o_f, o_b, ss_f, ss_b = scan_fwd("scan_fwd_h", hgrn_prep, scan_raws, scan_pars, N_HEADS_B, HEAD_DIM_B, HEAD_DIM_B, T)
    mo = mem_fwd(p, O["qM"], kv, T)
    hg = W["hgrn_norm"][i].reshape(1, W_B)
    post_ins = [("row", a, 0, W_A), ("row", o_f, 0, W_B), ("row", o_b, 0, W_B), ("row", mo, 0, W_M),
                ("row", p, O["gA"], W_A), ("row", p, O["gB"], W_B), ("row", p, O["gM"], W_M), ("full", hg)]
    (mix,) = rows_call("even_post_fwd", even_post_tile, T, post_ins, [MIX], [BF16])
    x_new = matmul("mm_out", mix, W["w_out_e"][i], "nn", add=x)
    return x_new, dict(x=x, g=g, h=h, p=p, kvp=kvp, sink=sink, scan_raws=scan_raws, scan_pars=scan_pars,
                       ss=(ss_f, ss_b), post_ins=post_ins, mix=mix)


def _add2(a, b):
    return a.astype(F32) + b.astype(F32)


def _assemble_even(dqA, dgA, dqB_f, dqB_b, dzf, dzb, diB_f, diB_b, dgB, dqM, dgM, dkvA):
    parts = [dqA, dgA, _add2(dqB_f, dqB_b), dzf, dzb, _add2(diB_f, diB_b), dgB, dqM, dgM, dkvA]
    return (jnp.concatenate([t.astype(BF16) for t in parts], axis=-1),)


def _even_bwd(dxo, sv, i, W, kv, slopes, T, sync):
    O = EVEN_OFF
    p = sv["p"]
    dmix = matmul("mm_dmix", dxo, W["w_out_e"][i], "nt")
    dwo = matmul("mm_dwo", sv["mix"], dxo, "tn")
    da, dof, dmo, dgA, dgB, dgM, dhg = rows_vjp_call("even_post_bwd", even_post_tile, T, sv["post_ins"],
                                                      [[("row", dmix, 0, MIX)]], skip=(2,), narrow=(4, 5, 6))
    dqA, dkvp, dsink = attn_bwd(p, O["qA"], sv["kvp"], sv["sink"], slopes, da, T)
    dkvA = dkvp[BLOCK:-BLOCK]
    dqB_f, dzf, diB_f, dqB_b, dzb, diB_b, dlow_f, dlow_b = scan_bwd(
        "scan_bwd_h", hgrn_prep, sv["scan_raws"], sv["scan_pars"], sv["ss"], (dof, 0), N_HEADS_B, HEAD_DIM_B, HEAD_DIM_B, T)
    dqB_f = sync(dqB_f)
    row = lambda arr, w: ("row", arr, 0, w)
    dlow = jnp.concatenate([dlow_f, dlow_b], axis=0)
    dqM, dkv = mem_bwd(p, O["qM"], kv, dmo, T)
    (dp,) = rows_call("even_dp", _assemble_even, T,
                      [row(dqA, W_A), row(dgA, W_A), row(dqB_f, W_B), row(dqB_b, W_B), row(dzf, W_B), row(dzb, W_B),
                       row(diB_f, W_B), row(diB_b, W_B), row(dgB, W_B), row(dqM, W_M), row(dgM, W_M),
                       row(dkvA, 2 * W_KV_A)],
                      [EVEN_IN], [BF16])
    dh = matmul("mm_dh_e", dp, W["w_in_e"][i], "nt")
    dwi = matmul("mm_dwi_e", sv["h"], dp, "tn")
    dx, dg = rows_vjp_call("rms_res_bwd", rms_res_tile, T, [("row", sv["x"], 0, D_MODEL), ("full", sv["g"])],
                           [[("row", dh, 0, D_MODEL)], [("row", dxo, 0, D_MODEL)]])
    return dx, dict(w_in=dwi, w_out=dwo, norm=dg[0], sink=dsink.reshape(N_Q_A), low=dlow, hg=dhg[0], kv=dkv)


def _pad_gate_up(w_up):
    z = jnp.zeros((2, 128, WK_C), F32)
    z = z.at[0, 0:GATE_RANK].set(w_up[0])
    return z.at[1, GATE_RANK:2 * GATE_RANK].set(w_up[1])


def _odd_fwd(x, i, W, kv, T):
    O = ODD_OFF
    g = W["norm_odd"][i].reshape(1, D_MODEL)
    (h,) = rows_call("rms_fwd", rms_tile, T, [("row", x, 0, D_MODEL), ("full", g)], [D_MODEL], [BF16])
    p = matmul("mm_in_o", h, W["w_in_o"][i], "nn")
    wup = _pad_gate_up(W["w_gate_up"][i])
    one_dir = [((p, O["qC"]), WK_C), ((p, O["kC"]), WK_C), ((p, O["vC"]), WV_C), ((p, O["rr"]), 128)]
    scan_raws = [one_dir, one_dir]
    scan_pars = [[wup[d], W["b_gate"][i][d:d + 1]] for d in range(2)]
    o_f, o_b, ss_f, ss_b = scan_fwd("scan_fwd_g", gla_prep, scan_raws, scan_pars, N_HEADS_C, DK_C, DV_C, T)
    mo = mem_fwd(p, O["qM"], kv, T)
    gg = W["gla_norm"][i].reshape(1, WV_C)
    post_ins = [("row", o_f, 0, WV_C), ("row", o_b, 0, WV_C), ("row", mo, 0, W_M),
                ("row", p, O["gC"], WV_C), ("row", p, O["gM"], W_M), ("full", gg)]
    (mix,) = rows_call("odd_post_fwd", odd_post_tile, T, post_ins, [MIX], [BF16])
    x_new = matmul("mm_out", mix, W["w_out_o"][i], "nn", add=x)
    return x_new, dict(x=x, g=g, h=h, p=p, scan_raws=scan_raws, scan_pars=scan_pars, ss=(ss_f, ss_b),
                       post_ins=post_ins, mix=mix)


def _assemble_odd(dq0, dq1, dk0, dk1, dv0, dv1, dgC, dqM, dgM, dr0, dr1):
    parts = [_add2(dq0, dq1), _add2(dk0, dk1), _add2(dv0, dv1), dgC, dqM, dgM, _add2(dr0, dr1)]
    return (jnp.concatenate([t.astype(BF16) for t in parts], axis=-1),)


def _odd_bwd(dxo, sv, i, W, kv, T, sync):
    O = ODD_OFF
    p = sv["p"]
    dmix = matmul("mm_dmix", dxo, W["w_out_o"][i], "nt")
    dwo = matmul("mm_dwo", sv["mix"], dxo, "tn")
    dof, dmo, dgC, dgM, dgg = rows_vjp_call("odd_post_bwd", odd_post_tile, T, sv["post_ins"],
                                            [[("row", dmix, 0, MIX)]], skip=(1,), narrow=(3, 4))
    dqf, dkf, dvf, dr_f, dqb, dkb, dvb, dr_b, dwup_f, dbg_f, dwup_b, dbg_b = scan_bwd(
        "scan_bwd_g", gla_prep, sv["scan_raws"], sv["scan_pars"], sv["ss"], (dof, 0), N_HEADS_C, DK_C, DV_C, T)
    dqf = sync(dqf)
    row = lambda arr, w: ("row", arr, 0, w)
    dqM, dkv = mem_bwd(p, O["qM"], kv, dmo, T)
    (dp,) = rows_call("odd_dp", _assemble_odd, T,
                      [row(dqf, WK_C), row(dqb, WK_C), row(dkf, WK_C), row(dkb, WK_C), row(dvf, WV_C), row(dvb, WV_C),
                       row(dgC, WV_C), row(dqM, W_M), row(dgM, W_M), row(dr_f, 128), row(dr_b, 128)],
                      [ODD_PAD], [BF16])
    dh = matmul("mm_dh_o", dp, W["w_in_o"][i], "nt")
    dwi = matmul("mm_dwi_o", sv["h"], dp, "tn")
    dx, dg = rows_vjp_call("rms_res_bwd", rms_res_tile, T, [("row", sv["x"], 0, D_MODEL), ("full", sv["g"])],
                           [[("row", dh, 0, D_MODEL)], [("row", dxo, 0, D_MODEL)]])
    dw_up = jnp.stack([dwup_f[0:GATE_RANK], dwup_b[GATE_RANK:2 * GATE_RANK]])
    dbg = jnp.concatenate([dbg_f, dbg_b], axis=0)
    return dx, dict(w_in=dwi, w_out=dwo, norm=dg[0], w_up=dw_up, b_gate=dbg, gg=dgg[0], kv=dkv)


def local_step(x, mem, target, W, later=None, on_layer_grads=None, sync=lambda a: a):
    T = x.shape[0]
    slopes = jnp.repeat(2.0 ** (-8.0 * jnp.arange(1, N_Q_A + 1, dtype=F32) / N_Q_A), BLOCK).reshape(N_Q_A * BLOCK, 1)
    lower, lower_vjp = jax.vjp(_lower_bounds, W["lb_param"])
    mem_g = W["mem_norm"].reshape(1, D_MODEL)
    (mem_n,) = rows_call("mem_rms_fwd", rms_tile, N_MEM, [("row", mem, 0, D_MODEL), ("full", mem_g)], [D_MODEL], [BF16])
    kvs, saved = [], []
    for l in range(DEPTH):
        if l == 1 and later is not None:
            x, W = later(x, W)
        kvs.append(matmul("mm_kv", mem_n, W["w_kv"][l], "nn"))
        if l % 2 == 0:
            x, sv = _even_fwd(x, l // 2, W, lower, kvs[l], slopes, T)
        else:
            x, sv = _odd_fwd(x, l // 2, W, kvs[l], T)
        saved.append(sv)
    loss, dx, dgf = final_call(x, W["final_norm"].reshape(1, D_MODEL), target, T)
    per = [None] * DEPTH
    dmem_n = None
    for l in reversed(range(DEPTH)):
        if l % 2 == 0:
            dx, per[l] = _even_bwd(dx, saved[l], l // 2, W, kvs[l], slopes, T, sync)
        else:
            dx, per[l] = _odd_bwd(dx, saved[l], l // 2, W, kvs[l], T, sync)
        per[l]["w_kv"] = matmul("mm_dwkv", mem_n, per[l]["kv"], "tn")
        dmem_n = matmul("mm_dmem", per[l]["kv"], W["w_kv"][l], "nt", add=dmem_n)
        if on_layer_grads is not None:
            dx = on_layer_grads(l, dx, per[l])
    dw_kv = [per[l]["w_kv"] for l in range(DEPTH)]
    (dmem_norm,) = rows_vjp_call("mem_rms_bwd", rms_tile, N_MEM, [("row", mem, 0, D_MODEL), ("full", mem_g)],
                                 [[("row", dmem_n, 0, D_MODEL)]], skip=(0,))
    ev, od = (per[0], per[2]), (per[1], per[3])
    (d_lb,) = lower_vjp(jnp.stack([e["low"] for e in ev]))
    grads = dict(
        w_in_e=jnp.stack([e["w_in"] for e in ev]), w_in_o=jnp.stack([o["w_in"] for o in od]),
        w_out_e=jnp.stack([e["w_out"] for e in ev]), w_out_o=jnp.stack([o["w_out"] for o in od]),
        w_kv=jnp.stack(dw_kv), norm_even=jnp.stack([e["norm"] for e in ev]), sink=jnp.stack([e["sink"] for e in ev]),
        lb_param=d_lb, hgrn_norm=jnp.stack([e["hg"] for e in ev]), norm_odd=jnp.stack([o["norm"] for o in od]),
        w_gate_up=jnp.stack([o["w_up"] for o in od]), b_gate=jnp.stack([o["b_gate"] for o in od]),
        gla_norm=jnp.stack([o["gg"] for o in od]), mem_norm=dmem_norm[0], final_norm=dgf[0])
    return loss, dx, grads


SMALL_SPECS = (("lb_param", (2, 2, 128)), ("norm_odd", (2, 256)), ("w_gate_up", (2, 2, 16, 128)),
               ("b_gate", (2, 2, 128)), ("gla_norm", (2, 256)))
SMALL_ROWS = 80


def _pack_small_local(d):
    return jnp.concatenate([d[n].reshape(-1) for n, _ in SMALL_SPECS]).reshape(SMALL_ROWS, 128)


def _unpack_small_local(b):
    flat, out, o = b.reshape(-1), {}, 0
    for n, shp in SMALL_SPECS:
        sz = int(np.prod(shp))
        out[n] = flat[o:o + sz].reshape(shp)
        o += sz
    return out


def _unpack_small_full(g4):
    per = [_unpack_small_local(g4[j]) for j in range(4)]
    return {n: jnp.concatenate([per[j][n] for j in range(4)], axis=-1) for n, _ in SMALL_SPECS}


def _pack_small_blocks(full):
    blocks = []
    for j in range(4):
        blocks.append(_pack_small_local({n: full[n][..., j * shp[-1]:(j + 1) * shp[-1]] for n, shp in SMALL_SPECS}))
    return jnp.stack(blocks)


def _cols(t, order, off, widths):
    return [t[..., off[n]:off[n] + widths[n]] for n in order]


EVEN_REF_ORDER = ("qA", "kA", "vA", "gA", "qB", "zf", "zb", "iB", "gB", "qM", "gM")
ODD_REF_ORDER = ("qC", "kC", "vC", "gC", "rr", "qM", "gM")


def _layer_weights(l, g_in, g_out, g_kv):
    t = g_in.transpose(1, 0, 2).reshape(D_MODEL, -1)
    if l % 2 == 0:
        w_in = jnp.concatenate(_cols(t, EVEN_ORDER, EVEN_REF_OFF, EVEN_W), axis=-1)
    else:
        w_in = jnp.concatenate(_cols(t, ODD_ORDER, ODD_REF_OFF, ODD_W) + [jnp.zeros((D_MODEL, ODD_PAD - ODD_IN), BF16)],
                               axis=-1)
    return w_in, g_out.reshape(MIX, D_MODEL), g_kv.reshape(D_MODEL, 2 * W_M)


def _layer_grad_blocks(l, gl):
    if l % 2 == 0:
        t = jnp.concatenate(_cols(gl["w_in"], EVEN_REF_ORDER, EVEN_OFF, EVEN_W), axis=-1)
    else:
        t = jnp.concatenate(_cols(gl["w_in"], ODD_REF_ORDER, ODD_OFF, ODD_W), axis=-1)
    b_in = t.reshape(D_MODEL, 4, -1).transpose(1, 0, 2)
    return [b_in, gl["w_out"].reshape(4, MIX // 4, D_MODEL), gl["w_kv"].reshape(4, D_MODEL // 4, 2 * W_M)]


WEIGHT_NAMES = ("norm_even", "w_in_even", "sink", "lb_param", "hgrn_norm", "w_out_even", "norm_odd", "w_in_odd",
                "w_gate_up", "b_gate", "gla_norm", "w_out_odd", "mem_norm", "w_mem_kv", "final_norm")


def kernel(x, mem, norm_even, w_in_even, sink, lb_param, hgrn_norm, w_out_even, norm_odd, w_in_odd, w_gate_up, b_gate, gla_norm, w_out_odd, mem_norm, w_mem_kv, final_norm, loss_target, m_norm_even, m_w_in_even, m_sink, m_lb_param, m_hgrn_norm, m_w_out_even, m_norm_odd, m_w_in_odd, m_w_gate_up, m_b_gate, m_gla_norm, m_w_out_odd, m_mem_norm, m_w_mem_kv, m_final_norm, v_norm_even, v_w_in_even, v_sink, v_lb_param, v_hgrn_norm, v_w_out_even, v_norm_odd, v_w_in_odd, v_w_gate_up, v_b_gate, v_gla_norm, v_w_out_odd, v_mem_norm, v_w_mem_kv, v_final_norm):
    w = dict(zip(WEIGHT_NAMES, (norm_even, w_in_even, sink, lb_param, hgrn_norm, w_out_even, norm_odd, w_in_odd,
                                w_gate_up, b_gate, gla_norm, w_out_odd, mem_norm, w_mem_kv, final_norm)))
    m = dict(zip(WEIGHT_NAMES, (m_norm_even, m_w_in_even, m_sink, m_lb_param, m_hgrn_norm, m_w_out_even, m_norm_odd,
                                m_w_in_odd, m_w_gate_up, m_b_gate, m_gla_norm, m_w_out_odd, m_mem_norm, m_w_mem_kv,
                                m_final_norm)))
    v = dict(zip(WEIGHT_NAMES, (v_norm_even, v_w_in_even, v_sink, v_lb_param, v_hgrn_norm, v_w_out_even, v_norm_odd,
                                v_w_in_odd, v_w_gate_up, v_b_gate, v_gla_norm, v_w_out_odd, v_mem_norm, v_w_mem_kv,
                                v_final_norm)))
    ci = lax.axis_index("c").astype(jnp.int32).reshape(1)
    chip = (2 * lax.axis_index("x") + lax.axis_index("y")).astype(jnp.int32).reshape(1)

    shards = []
    for l in range(DEPTH):
        names = ("w_in_even", "w_out_even") if l % 2 == 0 else ("w_in_odd", "w_out_odd")
        shards.append([w[names[0]][l // 2].astype(BF16), w[names[1]][l // 2].astype(BF16), w_mem_kv[l].astype(BF16)])
    small = _pack_small_local(w)
    own = lambda g, s: lax.dynamic_update_slice(g, s[None], (chip[0], 0, 0))
    first = [own(g, s) for g, s in zip(gather_weights(shards[0], small), shards[0] + [small])]
    later_shards = shards[1] + shards[2] + shards[3]
    later_raw = gather_weights_async(later_shards)
    w0 = _layer_weights(0, *first[0:3])
    W = dict(w_in_e=[w0[0]], w_out_e=[w0[1]], w_kv=[w0[2]])
    W.update(_unpack_small_full(first[3]))
    W.update({n: w[n] for n in ("norm_even", "sink", "hgrn_norm", "mem_norm", "final_norm")})

    def later(x1, W):
        x1, raw = lax.optimization_barrier((x1, list(later_raw)))
        g = [own(a, s) for a, s in zip(raw, later_shards)]
        w1, w2, w3 = (_layer_weights(l, *g[3 * (l - 1):3 * l]) for l in (1, 2, 3))
        W = dict(W)
        W.update(w_in_e=[w0[0], w2[0]], w_in_o=[w1[0], w3[0]], w_out_e=[w0[1], w2[1]], w_out_o=[w1[1], w3[1]],
                 w_kv=[w0[2], w1[2], w2[2], w3[2]])
        return x1, W

    place = jnp.concatenate([chip, ci])

    def start(tag, blocks, wire):
        return dict(tag=tag, blocks=blocks, wire=wire, step=0,
                    recv=exchange_siblings(f"rs_siblings_{tag}", blocks, True, 2))

    def advance(p):
        if p["step"] == 0:
            sums = [add_sibling(g, r, ci, dt) for g, r, dt in zip(p["blocks"], p["recv"], p["wire"])]
            p["recv3"] = exchange_chips(f"rs_chips_{p['tag']}", sums, 3)
        else:
            p["mine"] = [add_chips(g, r, r3, place) for g, r, r3 in zip(p["blocks"], p["recv"], p["recv3"])]
            p["other"] = exchange_siblings(f"rs_final_{p['tag']}", p["mine"], False, 4)
        p["step"] += 1

    pipes, first_layer = [], {}

    def sync(a):
        for p in pipes:
            if p["step"] < 3:
                key = ("recv", "recv3", "other")[p["step"]]
                a, arrived = lax.optimization_barrier((a, list(p[key])))
                p[key] = arrived
                if p["step"] < 2:
                    advance(p)
                else:
                    p["step"] = 3
        return a

    def on_layer_grads(l, dx, gl):
        dx = sync(dx)
        if l == 0:
            first_layer.update(gl)
        else:
            pipes.append(start(f"l{l}", _layer_grad_blocks(l, gl), [BF16] * 3))
        return dx

    loss_tile, dx, grads = local_step(x[0], mem[0], loss_target[0], W, later, on_layer_grads, sync)
    pipes.append(start("l0", _layer_grad_blocks(0, first_layer) + [_pack_small_blocks(grads)], [BF16] * 3 + [F32]))
    while any(p["step"] < 2 for p in pipes):
        for p in pipes:
            if p["step"] < 2:
                advance(p)
    by_layer = {int(p["tag"][1:]): p for p in pipes}
    halves = lambda layers, k: (jnp.stack([by_layer[l]["mine"][k] for l in layers]),
                                jnp.stack([by_layer[l]["other"][k] for l in layers]))
    big = dict(w_in_even=halves((0, 2), 0), w_in_odd=halves((1, 3), 0), w_out_even=halves((0, 2), 1),
               w_out_odd=halves((1, 3), 1), w_mem_kv=halves((0, 1, 2, 3), 2))
    s_mine, s_other = by_layer[0]["mine"][3], by_layer[0]["other"][3]
    g_small = jnp.where(ci[0] == 0, jnp.concatenate([s_mine, s_other]), jnp.concatenate([s_other, s_mine]))
    gl = _unpack_small_local(g_small)

    pack = jnp.zeros((8, D_MODEL), F32)
    pack = pack.at[0:2].set(grads["norm_even"]).at[2].set(grads["hgrn_norm"].reshape(-1))
    pack = pack.at[3].set(grads["mem_norm"]).at[4].set(grads["final_norm"])
    pack = pack.at[5, 0:16].set(grads["sink"].reshape(-1)).at[5, 16].set(loss_tile[0, 0])
    tot = sum_devices(allgather_small(pack))
    gl.update(norm_even=tot[0:2], hgrn_norm=tot[2].reshape(2, W_B), mem_norm=tot[3], final_norm=tot[4],
              sink=tot[5, 0:16].reshape(2, N_Q_A))
    loss = tot[5, 16]

    upd = {}
    for n in WEIGHT_NAMES:
        if n in big:
            gl[n], *upd[n] = adamw_halves(w[n], *big[n], m[n], v[n], ci)
        else:
            upd[n] = adamw_call(w[n], gl[n], m[n], v[n])
    return (loss, dx[None], *[gl[n] for n in WEIGHT_NAMES], *[upd[n][0] for n in WEIGHT_NAMES],
            *[upd[n][1] for n in WEIGHT_NAMES], *[upd[n][2] for n in WEIGHT_NAMES])
```

```python
import functools

import numpy as np
import jax
import jax.numpy as jnp
from jax import lax
from jax.experimental import pallas as pl
from jax.experimental.pallas import tpu as pltpu
from jax.experimental.pallas import tpu_sc as plsc

F32 = jnp.float32
BF16 = jnp.bfloat16

D_MODEL = 1024
DEPTH = 4
N_Q_A, N_KV_A, HEAD_DIM_A = 8, 2, 64
W_A, W_KV_A = 512, 128
WINDOW = 128
BLOCK = 128
N_HEADS_B, HEAD_DIM_B, W_B = 4, 128, 512
N_HEADS_C, DK_C, DV_C, WK_C, WV_C = 4, 128, 256, 512, 1024
GATE_RANK = 16
GATE_TEMP = 16.0
N_MEM, N_HEADS_M, HEAD_DIM_M, W_M = 256, 4, 128, 512
EPS = 1e-6
MASK_VALUE = -1e30
MIN_GATE = 1e-30
EVEN_IN, ODD_IN = 4864, 4128
ODD_PAD = 4224
MIX = 1536
ADAM_LR, ADAM_B1, ADAM_B2, ADAM_EPS, ADAM_WD, ADAM_STEP = 0.001, 0.9, 0.999, 1e-08, 0.01, 10

SCAN_CHUNK = 128
SCAN_LEVELS = 7
VMEM_LIMIT = 56 * 1024 * 1024

EVEN_REF_OFF = dict(qA=0, kA=512, vA=640, gA=768, qB=1280, zf=1792, zb=2304, iB=2816, gB=3328, qM=3840, gM=4352)
EVEN_W = dict(qA=512, kA=128, vA=128, gA=512, qB=512, zf=512, zb=512, iB=512, gB=512, qM=512, gM=512)
EVEN_ORDER = ("qA", "gA", "qB", "zf", "zb", "iB", "gB", "qM", "gM", "kA", "vA")
ODD_REF_OFF = dict(qC=0, kC=512, vC=1024, gC=2048, rr=3072, qM=3104, gM=3616)
ODD_W = dict(qC=512, kC=512, vC=1024, gC=1024, rr=32, qM=512, gM=512)
ODD_ORDER = ("qC", "kC", "vC", "gC", "qM", "gM", "rr")


def _offsets(order, widths):
    off, o = {}, 0
    for n in order:
        off[n] = o
        o += widths[n]
    return off


EVEN_OFF = _offsets(EVEN_ORDER, EVEN_W)
ODD_OFF = _offsets(ODD_ORDER, ODD_W)


def _dg(a, b, ca, cb):
    return lax.dot_general(a.astype(BF16), b.astype(BF16), (((ca,), (cb,)), ((), ())),
                           preferred_element_type=F32)


def dot_nn(a, b):
    return _dg(a, b, 1, 0)


def dot_nt(a, b):
    return _dg(a, b, 1, 1)


def dot_tn(a, b):
    return _dg(a, b, 0, 0)


@jax.custom_vjp
def bdot(a, b):
    return dot_nn(a, b)


bdot.defvjp(lambda a, b: (dot_nn(a, b), (a, b)),
            lambda r, g: (dot_nt(g, r[1]), dot_tn(r[0], g)))


@jax.custom_vjp
def bdot_t(a, b):
    return dot_nt(a, b)


bdot_t.defvjp(lambda a, b: (dot_nt(a, b), (a, b)),
              lambda r, g: (dot_nn(g, r[1]), dot_tn(g, r[0])))


@jax.custom_vjp
def bdot_tn(a, b):
    return dot_tn(a, b)


bdot_tn.defvjp(lambda a, b: (dot_tn(a, b), (a, b)),
               lambda r, g: (dot_nt(r[1], g), dot_nn(r[0], g)))


def _split_mm(h, x):
    hi = x.astype(BF16)
    lo = (x - hi.astype(F32)).astype(BF16)
    return (lax.dot_general(h, hi, (((1,), (0,)), ((), ())), preferred_element_type=F32)
            + lax.dot_general(h, lo, (((1,), (0,)), ((), ())), preferred_element_type=F32))


def _sigmoid(z):
    return 1.0 / (1.0 + jnp.exp(-z))


def _silu(z):
    return z * _sigmoid(z)


def _log_sigmoid(z):
    return jnp.minimum(z, 0.0) - jnp.log(1.0 + jnp.exp(-jnp.abs(z)))


def _rms(x, g):
    return x * lax.rsqrt(jnp.mean(x * x, axis=-1, keepdims=True) + EPS) * g


def rms_tile(x, g):
    return (_rms(x, g),)


@functools.partial(jax.custom_vjp, nondiff_argnums=(1, 2))
def split(x, n, axis):
    w = x.shape[axis] // n
    return tuple(lax.slice_in_dim(x, h * w, (h + 1) * w, axis=axis) for h in range(n))


split.defvjp(lambda x, n, axis: (split(x, n, axis), None),
             lambda n, axis, _, cts: (jnp.concatenate(cts, axis=axis),))


def _group_rms(o, g, heads):
    return jnp.concatenate([_rms(oh, gh) for oh, gh in zip(split(o, heads, 1), split(g, heads, 1))], axis=-1)


def even_post_tile(a, o2f, o2b, mo, gA, gB, gM, hg):
    y = _group_rms(o2f + o2b, hg, N_HEADS_B)
    return (jnp.concatenate([a * _silu(gA), y * _silu(gB), mo * _silu(gM)], axis=-1),)


def odd_post_tile(o2f, o2b, mo, gC, gM, gg):
    y = _group_rms(o2f + o2b, gg, N_HEADS_C)
    return (jnp.concatenate([y * _silu(gC), mo * _silu(gM)], axis=-1),)


def hgrn_prep(raw, par):
    qB, z, iB = raw
    (lb,) = par
    f = lb + (1.0 - lb) * _sigmoid(z)
    return _silu(qB), (1.0 - lb) * _sigmoid(-z), iB, jnp.log(jnp.maximum(f, MIN_GATE))


def gla_prep(raw, par):
    qC, kC, vC, r128 = raw
    wup, bg = par
    return qC * (DK_C ** -0.5), kC, vC, _log_sigmoid(bdot(r128, wup) + bg) / GATE_TEMP


def mem_tile(q, k, v):
    s = bdot_t(q, k) * (HEAD_DIM_M ** -0.5)
    m = lax.stop_gradient(jnp.max(s, axis=-1, keepdims=True))
    p = jnp.exp(s - m)
    p = p / jnp.sum(p, axis=-1, keepdims=True)
    return (bdot(p, v),)


ATTN_GROUP = N_Q_A // N_KV_A


def attn_block(q, ks, vs, sink, slope, c, seq):
    rows = ATTN_GROUP * BLOCK
    i = lax.broadcasted_iota(jnp.int32, (rows, 3 * BLOCK), 0) % BLOCK
    j = lax.broadcasted_iota(jnp.int32, (rows, 3 * BLOCK), 1)
    dist = jnp.abs(i - j + BLOCK).astype(F32)
    kpos = (c - 1) * BLOCK + j
    valid = (dist <= WINDOW) & (kpos >= 0) & (kpos < seq)
    s = bdot_t(q, ks) * (HEAD_DIM_A ** -0.5)
    s = jnp.where(valid, s - slope * dist, MASK_VALUE)
    m = lax.stop_gradient(jnp.maximum(jnp.max(s, axis=-1, keepdims=True), sink))
    p = jnp.where(valid, jnp.exp(s - m), 0.0)
    denom = jnp.sum(p, axis=-1, keepdims=True) + jnp.exp(sink - m)
    return bdot(p, vs) / denom


def scan_chunk(q, k, v, e, tot, st, qm, pm):
    C = SCAN_CHUNK
    e = split(e, 2 + SCAN_LEVELS, 0)
    qe = q * jnp.exp(e[0])
    kd = k * jnp.exp(e[1])
    r = lax.broadcasted_iota(jnp.int32, (C, C), 0)
    s = lax.broadcasted_iota(jnp.int32, (C, C), 1)
    a = jnp.where(r == s, jnp.sum(q * k, axis=-1, keepdims=True), 0.0)
    for l in range(SCAN_LEVELS):
        u = jnp.where(qm[l * C:(l + 1) * C] != 0.0, q, k) * jnp.exp(e[2 + l])
        a = a + bdot_t(u, u) * pm[l * C:(l + 1) * C]
    o = bdot_t(qe, st) + bdot(a, v)
    st_new = st * jnp.exp(tot) + bdot_tn(v, kd)
    return o, st_new


def _scan_consts():
    C, L = SCAN_CHUNK, SCAN_LEVELS
    t = np.arange(C)[:, None]
    r = np.arange(C)[None, :]
    blocks = [(r <= t), (r > t)]
    qms, pms = [], []
    for l in range(1, L + 1):
        m = C >> l
        upper_t = (t % (2 * m)) >= m
        upper_r = (r % (2 * m)) >= m
        same_half = (t // m) == (r // m)
        blocks.append(same_half & np.where(upper_t, r <= t, r > t))
        qms.append(np.broadcast_to(upper_t, (C, C)))
        pms.append(((t // (2 * m)) == (r // (2 * m))) & upper_t & ~upper_r)
    hf = np.concatenate(blocks, axis=0).astype(np.float32)
    flip = lambda mat: mat.reshape(-1, C, C)[:, ::-1, ::-1].reshape(-1, C)
    qmf = np.concatenate(qms, axis=0).astype(np.float32)
    pmf = np.concatenate(pms, axis=0).astype(np.float32)
    h = np.stack([hf, flip(hf)])
    ht = np.stack([h[0].T, h[1].T])
    qm = np.stack([qmf, 1.0 - qmf])
    pm = np.stack([pmf, flip(pmf)])
    return h, ht, qm, pm


def _cparams(sem):
    return pltpu.CompilerParams(dimension_semantics=sem, vmem_limit_bytes=VMEM_LIMIT)


def _row_tile(T):
    return min(T, 512)


def _in_spec(spec, tr):
    kind = spec[0]
    if kind == "row":
        _, arr, off, w = spec
        assert off % w == 0
        return arr, pl.BlockSpec((tr, w), functools.partial(lambda i, b: (i, b), b=off // w))
    if kind == "row3":
        _, arr, d, off, w = spec
        assert off % w == 0
        return arr, pl.BlockSpec((None, tr, w), functools.partial(lambda i, d, b: (d, i, b), d=d, b=off // w))
    _, arr = spec
    return arr, pl.BlockSpec(arr.shape, functools.partial(lambda i, n: (0,) * n, n=arr.ndim))


def rows_call(name, tile_fn, T, ins, out_widths, out_dtypes=None, stacks=None):
    tr = _row_tile(T)
    n_in = len(ins)
    out_dtypes = out_dtypes or [F32] * len(out_widths)
    stacks = stacks or [(k,) for k in range(len(out_widths))]

    def body(*refs):
        vals = [r[...] for r in refs[:n_in]]
        outs = tile_fn(*vals)
        for r, members in zip(refs[n_in:], stacks):
            if len(members) == 1:
                r[...] = outs[members[0]].astype(r.dtype)
            else:
                for d, k in enumerate(members):
                    r[d] = outs[k].astype(r.dtype)

    in_specs, args = [], []
    for spec in ins:
        arr, bs = _in_spec(spec, tr)
        args.append(arr)
        in_specs.append(bs)
    out_specs, out_shape = [], []
    for w, dt, members in zip(out_widths, out_dtypes, stacks):
        n = len(members)
        if n == 1:
            out_specs.append(pl.BlockSpec((tr, w), lambda i: (i, 0)))
            out_shape.append(jax.ShapeDtypeStruct((T, w), dt))
        else:
            out_specs.append(pl.BlockSpec((n, tr, w), lambda i: (0, i, 0)))
            out_shape.append(jax.ShapeDtypeStruct((n, T, w), dt))
    return pl.pallas_call(body, out_shape=out_shape, grid=(T // tr,), in_specs=in_specs, out_specs=out_specs,
                          name=name, compiler_params=_cparams(("arbitrary",)))(*args)


def rows_vjp_call(name, tile_fn, T, ins, cts, skip=(), narrow=()):
    tr = _row_tile(T)
    n_in = len(ins)
    n_ct = [len(c) for c in cts]
    want = [k for k in range(n_in) if k not in skip]

    def body(*refs):
        i = pl.program_id(0)
        vals = [r[...] for r in refs[:n_in]]
        ct, pos = [], n_in
        for n in n_ct:
            acc = refs[pos][...]
            for r in refs[pos + 1:pos + n]:
                acc = acc + r[...]
            ct.append(acc)
            pos += n
        _, vjp = jax.vjp(tile_fn, *vals)
        grads = vjp(tuple(ct))
        for r, k in zip(refs[pos:], want):
            if ins[k][0] == "full":
                @pl.when(i == 0)
                def _():
                    r[...] = jnp.zeros_like(r)
                r[...] += grads[k]
            else:
                r[...] = grads[k].astype(r.dtype)

    in_specs, args = [], []
    for spec in list(ins) + [s for c in cts for s in c]:
        arr, bs = _in_spec(spec, tr)
        args.append(arr)
        in_specs.append(bs)
    out_specs, out_shape = [], []
    for k in want:
        if ins[k][0] == "full":
            arr = ins[k][1]
            out_specs.append(pl.BlockSpec(arr.shape, functools.partial(lambda i, n: (0,) * n, n=arr.ndim)))
            out_shape.append(jax.ShapeDtypeStruct(arr.shape, F32))
        else:
            w = ins[k][-1]
            out_specs.append(pl.BlockSpec((tr, w), lambda i: (i, 0)))
            out_shape.append(jax.ShapeDtypeStruct((T, w), BF16 if k in narrow else F32))
    return pl.pallas_call(body, out_shape=out_shape, grid=(T // tr,), in_specs=in_specs, out_specs=out_specs,
                          name=name, compiler_params=_cparams(("arbitrary",)))(*args)


def matmul(name, a, b, mode, add=None, out_dtype=F32):
    if mode == "tn":
        K, M = a.shape
        N = b.shape[1]
        tm = M if M <= 1536 else 512
        tn = N if N <= 1280 else (N // 2 if (N // 2) % 128 == 0 else N)
        tk = min(K, 512)
        grid = (M // tm, N // tn, K // tk)

        def body(a_ref, b_ref, o_ref):
            @pl.when(pl.program_id(2) == 0)
            def _():
                o_ref[...] = jnp.zeros_like(o_ref)
            o_ref[...] += dot_tn(a_ref[...], b_ref[...])

        return pl.pallas_call(
            body, out_shape=jax.ShapeDtypeStruct((M, N), F32), grid=grid,
            in_specs=[pl.BlockSpec((tk, tm), lambda i, j, k: (k, i)), pl.BlockSpec((tk, tn), lambda i, j, k: (k, j))],
            out_specs=pl.BlockSpec((tm, tn), lambda i, j, k: (i, j)), name=name,
            compiler_params=_cparams(("arbitrary", "arbitrary", "arbitrary")))(a, b)

    M, K = a.shape
    N = b.shape[1] if mode == "nn" else b.shape[0]
    tm = min(M, 512)
    tn = N if N <= 1536 else (N // 2 if (N // 2) % 128 == 0 else (N // 3 if (N // 3) % 128 == 0 else N))
    grid = (N // tn, M // tm)
    n_in = 2 + (add is not None)

    def body(*refs):
        a_ref, b_ref = refs[0], refs[1]
        o_ref = refs[n_in]
        acc = dot_nn(a_ref[...], b_ref[...]) if mode == "nn" else dot_nt(a_ref[...], b_ref[...])
        if add is not None:
            acc = acc + refs[2][...]
        o_ref[...] = acc.astype(o_ref.dtype)

    in_specs = [pl.BlockSpec((tm, K), lambda j, i: (i, 0)),
                pl.BlockSpec((K, tn), lambda j, i: (0, j)) if mode == "nn" else pl.BlockSpec((tn, K), lambda j, i: (j, 0))]
    args = [a, b]
    if add is not None:
        in_specs.append(pl.BlockSpec((tm, tn), lambda j, i: (i, j)))
        args.append(add)
    return pl.pallas_call(
        body, out_shape=jax.ShapeDtypeStruct((M, N), out_dtype), grid=grid, in_specs=in_specs,
        out_specs=pl.BlockSpec((tm, tn), lambda j, i: (i, j)), name=name,
        compiler_params=_cparams(("arbitrary", "arbitrary")))(*args)


def _attn_heads(n):
    G = N_Q_A // N_KV_A
    k_sl = pl.ds(n * HEAD_DIM_A, HEAD_DIM_A)
    v_sl = pl.ds(W_KV_A + n * HEAD_DIM_A, HEAD_DIM_A)
    q_sl = [pl.ds((n * G + g) * HEAD_DIM_A, HEAD_DIM_A) for g in range(G)]
    return k_sl, v_sl, q_sl, range(n * G, (n + 1) * G)


def attn_fwd(p, q_off, kvp, sink, slopes, T):
    nb = T // BLOCK
    assert q_off % W_A == 0

    def body(q_ref, kv_ref, sink_ref, slope_ref, o_ref):
        c = pl.program_id(0)
        rows = pl.ds(pl.multiple_of(c * BLOCK, BLOCK), 3 * BLOCK)
        for n in range(N_KV_A):
            k_sl, v_sl, q_sl, heads = _attn_heads(n)
            group = pl.ds(n * ATTN_GROUP * BLOCK, ATTN_GROUP * BLOCK)
            q = jnp.concatenate([q_ref[:, s] for s in q_sl], axis=0)
            o = attn_block(q, kv_ref[rows, k_sl], kv_ref[rows, v_sl], sink_ref[group, :], slope_ref[group, :], c, T)
            for g, s in enumerate(q_sl):
                o_ref[:, s] = o[g * BLOCK:(g + 1) * BLOCK]

    full = lambda a: pl.BlockSpec(a.shape, functools.partial(lambda c, nd: (0,) * nd, nd=a.ndim))
    return pl.pallas_call(
        body, out_shape=jax.ShapeDtypeStruct((T, W_A), F32), grid=(nb,),
        in_specs=[pl.BlockSpec((BLOCK, W_A), lambda c: (c, q_off // W_A)), full(kvp), full(sink), full(slopes)],
        out_specs=pl.BlockSpec((BLOCK, W_A), lambda c: (c, 0)),
        name="attn_fwd", compiler_params=_cparams(("arbitrary",)))(p, kvp, sink, slopes)


def attn_bwd(p, q_off, kvp, sink, slopes, do, T):
    nb = T // BLOCK

    def body(q_ref, kv_ref, sink_ref, slope_ref, do_ref, dq_ref, dkv_ref, dsink_ref):
        c = pl.program_id(0)

        @pl.when(c == 0)
        def _():
            dkv_ref[...] = jnp.zeros_like(dkv_ref)
            dsink_ref[...] = jnp.zeros_like(dsink_ref)

        rows = pl.ds(pl.multiple_of(c * BLOCK, BLOCK), 3 * BLOCK)
        for n in range(N_KV_A):
            k_sl, v_sl, q_sl, heads = _attn_heads(n)
            group = pl.ds(n * ATTN_GROUP * BLOCK, ATTN_GROUP * BLOCK)
            slope = slope_ref[group, :]
            q = jnp.concatenate([q_ref[:, s] for s in q_sl], axis=0)
            do = jnp.concatenate([do_ref[:, s] for s in q_sl], axis=0)
            _, vjp = jax.vjp(lambda q_, kk, vv, sk: attn_block(q_, kk, vv, sk, slope, c, T),
                             q, kv_ref[rows, k_sl], kv_ref[rows, v_sl], sink_ref[group, :])
            dq, dks, dvs, dsk = vjp(do)
            dkv_ref[rows, k_sl] += dks
            dkv_ref[rows, v_sl] += dvs
            for g, (s, h) in enumerate(zip(q_sl, heads)):
                seg = slice(g * BLOCK, (g + 1) * BLOCK)
                dq_ref[:, s] = dq[seg].astype(dq_ref.dtype)
                dsink_ref[h] += jnp.sum(dsk[seg], axis=0, keepdims=True)

    full = lambda a: pl.BlockSpec(a.shape, functools.partial(lambda c, nd: (0,) * nd, nd=a.ndim))
    qspec = pl.BlockSpec((BLOCK, W_A), lambda c: (c, 0))
    return pl.pallas_call(
        body,
        out_shape=[jax.ShapeDtypeStruct((T, W_A), BF16), jax.ShapeDtypeStruct(kvp.shape, F32),
                   jax.ShapeDtypeStruct((N_Q_A, 1, 1), F32)],
        grid=(nb,),
        in_specs=[pl.BlockSpec((BLOCK, W_A), lambda c: (c, q_off // W_A)), full(kvp), full(sink), full(slopes), qspec],
        out_specs=[qspec, full(kvp), pl.BlockSpec((N_Q_A, 1, 1), lambda c: (0, 0, 0))],
        name="attn_bwd", compiler_params=_cparams(("arbitrary",)))(p, kvp, sink, slopes, do)


def mem_fwd(p, q_off, kv, T):
    tr = min(T, 2 * _row_tile(T))
    assert q_off % W_M == 0

    def body(q_ref, kv_ref, o_ref):
        for h in range(N_HEADS_M):
            hs = pl.ds(h * HEAD_DIM_M, HEAD_DIM_M)
            (o,) = mem_tile(q_ref[:, hs], kv_ref[:, hs], kv_ref[:, pl.ds(W_M + h * HEAD_DIM_M, HEAD_DIM_M)])
            o_ref[:, hs] = o

    return pl.pallas_call(
        body, out_shape=jax.ShapeDtypeStruct((T, W_M), F32), grid=(T // tr,),
        in_specs=[pl.BlockSpec((tr, W_M), lambda i: (i, q_off // W_M)), pl.BlockSpec((N_MEM, 2 * W_M), lambda i: (0, 0))],
        out_specs=pl.BlockSpec((tr, W_M), lambda i: (i, 0)),
        name="mem_fwd", compiler_params=_cparams(("arbitrary",)))(p, kv)


def mem_bwd(p, q_off, kv, do, T):
    tr = min(T, 2 * _row_tile(T))

    def body(q_ref, kv_ref, do_ref, dq_ref, dkv_ref):
        @pl.when(pl.program_id(0) == 0)
        def _():
            dkv_ref[...] = jnp.zeros_like(dkv_ref)

        for h in range(N_HEADS_M):
            hs = pl.ds(h * HEAD_DIM_M, HEAD_DIM_M)
            vs = pl.ds(W_M + h * HEAD_DIM_M, HEAD_DIM_M)
            _, vjp = jax.vjp(mem_tile, q_ref[:, hs], kv_ref[:, hs], kv_ref[:, vs])
            dq, dk, dv = vjp((do_ref[:, hs],))
            dq_ref[:, hs] = dq.astype(dq_ref.dtype)
            dkv_ref[:, hs] += dk
            dkv_ref[:, vs] += dv

    kvspec = pl.BlockSpec((N_MEM, 2 * W_M), lambda i: (0, 0))
    return pl.pallas_call(
        body,
        out_shape=[jax.ShapeDtypeStruct((T, W_M), BF16), jax.ShapeDtypeStruct((N_MEM, 2 * W_M), F32)],
        grid=(T // tr,),
        in_specs=[pl.BlockSpec((tr, W_M), lambda i: (i, q_off // W_M)), kvspec, pl.BlockSpec((tr, W_M), lambda i: (i, 0))],
        out_specs=[pl.BlockSpec((tr, W_M), lambda i: (i, 0)), kvspec],
        name="mem_bwd", compiler_params=_cparams(("arbitrary",)))(p, kv, do)


def _scan_const_specs(dk):
    C, L = SCAN_CHUNK, SCAN_LEVELS
    return [pl.BlockSpec((2, (2 + L) * C, C), lambda n: (0, 0, 0)),
            pl.BlockSpec((2, C, (2 + L) * C), lambda n: (0, 0, 0)),
            pl.BlockSpec((2, L * C, dk), lambda n: (0, 0, 0)),
            pl.BlockSpec((2, L * C, C), lambda n: (0, 0, 0))]


def _chunk_spec(src, width, chunk_of):
    arr, sel = src
    if arr.ndim == 2:
        assert sel % width == 0
        return pl.BlockSpec((SCAN_CHUNK, width), functools.partial(lambda n, b: (chunk_of(n), b), b=sel // width))
    return pl.BlockSpec((None, SCAN_CHUNK, width), functools.partial(lambda n, d: (d, chunk_of(n), 0), d=sel))


def _scan_const_args():
    h, ht, qm, pm = _scan_consts()
    return [jnp.asarray(h, BF16), jnp.asarray(ht, BF16), jnp.asarray(qm, F32), jnp.asarray(pm, F32)]


def _full_spec(a):
    return pl.BlockSpec(a.shape, functools.partial(lambda n, nd: (0,) * nd, nd=a.ndim))


def scan_fwd(name, prep, raws, params, heads, dk, dv, T):
    C = SCAN_CHUNK
    N = T // C
    assert dk == C
    Wv = heads * dv
    orders = (lambda n: n, lambda n: N - 1 - n)
    n_raw, n_par = [len(r) for r in raws], [len(p) for p in params]

    def body(*refs):
        pos, raw_refs, par_refs = 0, [], []
        for d in range(2):
            raw_refs.append(refs[pos:pos + n_raw[d]])
            pos += n_raw[d]
        for d in range(2):
            par_refs.append(refs[pos:pos + n_par[d]])
            pos += n_par[d]
        h_ref, ht_ref, qm_ref, pm_ref = refs[pos:pos + 4]
        o_refs, ss_refs, st_ref = refs[pos + 4:pos + 6], refs[pos + 6:pos + 8], refs[pos + 8]

        @pl.when(pl.program_id(0) == 0)
        def _():
            st_ref[...] = jnp.zeros_like(st_ref)

        for d in range(2):
            consts = (qm_ref[d], pm_ref[d])
            q, k, v, g = prep([r[...] for r in raw_refs[d]], [p[...] for p in par_refs[d]])
            e = _split_mm(h_ref[d], g)
            tot = jnp.sum(g, axis=0, keepdims=True)
            for h in range(heads):
                ks, vs = slice(h * dk, (h + 1) * dk), slice(h * dv, (h + 1) * dv)
                st = st_ref[d, h]
                ss_refs[d][h] = st
                o, st_new = scan_chunk(q[:, ks], k[:, ks], v[:, vs], e[:, ks], tot[:, ks], st, *consts)
                o_refs[d][:, vs] = o
                st_ref[d, h] = st_new

    ss_spec = lambda order: pl.BlockSpec((heads, None, dv, dk), lambda n: (0, order(n), 0, 0))
    return pl.pallas_call(
        body,
        out_shape=[jax.ShapeDtypeStruct((T, Wv), F32)] * 2 + [jax.ShapeDtypeStruct((heads, N, dv, dk), F32)] * 2,
        grid=(N,),
        in_specs=[_chunk_spec(s, w, orders[d]) for d in range(2) for s, w in raws[d]]
        + [_full_spec(p) for d in range(2) for p in params[d]] + _scan_const_specs(dk),
        out_specs=[pl.BlockSpec((C, Wv), lambda n: (orders[0](n), 0)), pl.BlockSpec((C, Wv), lambda n: (orders[1](n), 0)),
                   ss_spec(orders[0]), ss_spec(orders[1])],
        scratch_shapes=[pltpu.VMEM((2, heads, dv, dk), F32)],
        name=name, compiler_params=_cparams(("arbitrary",)))(
            *[s[0] for d in range(2) for s, _ in raws[d]], *[p for d in range(2) for p in params[d]], *_scan_const_args())


def scan_bwd(name, prep, raws, params, ss, do, heads, dk, dv, T):
    C = SCAN_CHUNK
    N = T // C
    Wv = heads * dv
    orders = (lambda n: N - 1 - n, lambda n: n)
    n_raw, n_par = [len(r) for r in raws], [len(p) for p in params]

    def body(*refs):
        pos, raw_refs, par_refs, draw_refs, dpar_refs = 0, [], [], [], []
        for group, counts in ((raw_refs, n_raw), (par_refs, n_par)):
            for d in range(2):
                group.append(refs[pos:pos + counts[d]])
                pos += counts[d]
        ss_refs, do_refs = refs[pos:pos + 2], refs[pos + 2:pos + 4]
        h_ref, ht_ref, qm_ref, pm_ref = refs[pos + 4:pos + 8]
        pos += 8
        for group, counts in ((draw_refs, n_raw), (dpar_refs, n_par)):
            for d in range(2):
                group.append(refs[pos:pos + counts[d]])
                pos += counts[d]
        dst_ref = refs[pos]

        @pl.when(pl.program_id(0) == 0)
        def _():
            dst_ref[...] = jnp.zeros_like(dst_ref)
            for d in range(2):
                for r in dpar_refs[d]:
                    r[...] = jnp.zeros_like(r)

        for d in range(2):
            consts = (qm_ref[d], pm_ref[d])
            (q, k, v, g), prep_vjp = jax.vjp(prep, [r[...] for r in raw_refs[d]], [p[...] for p in par_refs[d]])
            e = _split_mm(h_ref[d], g)
            tot = jnp.sum(g, axis=0, keepdims=True)
            dqs, dks, dvs, des, dtots = [], [], [], [], []
            for h in range(heads):
                ks, vs = slice(h * dk, (h + 1) * dk), slice(h * dv, (h + 1) * dv)
                _, vjp = jax.vjp(lambda q_, k_, v_, e_, t_, st_: scan_chunk(q_, k_, v_, e_, t_, st_, *consts),
                                 q[:, ks], k[:, ks], v[:, vs], e[:, ks], tot[:, ks], ss_refs[d][h])
                dq, dk_, dv_, de, dtot, dst = vjp((do_refs[d][:, vs], dst_ref[d, h]))
                dst_ref[d, h] = dst
                for group, val in ((dqs, dq), (dks, dk_), (dvs, dv_), (des, de), (dtots, dtot)):
                    group.append(val)
            cat = lambda parts: jnp.concatenate(parts, axis=-1)
            dg = _split_mm(ht_ref[d], cat(des)) + cat(dtots)
            draws, dpars = prep_vjp((cat(dqs), cat(dks), cat(dvs), dg))
            for r, val in zip(draw_refs[d], draws):
                r[...] = val.astype(r.dtype)
            for r, val in zip(dpar_refs[d], dpars):
                r[...] += val

    ss_spec = lambda order: pl.BlockSpec((heads, None, dv, dk), lambda n: (0, order(n), 0, 0))
    row_out = lambda w, order: pl.BlockSpec((C, w), lambda n: (order(n), 0))
    return pl.pallas_call(
        body,
        out_shape=[jax.ShapeDtypeStruct((T, w), BF16) for d in range(2) for _, w in raws[d]]
        + [jax.ShapeDtypeStruct(p.shape, F32) for d in range(2) for p in params[d]],
        grid=(N,),
        in_specs=[_chunk_spec(s, w, orders[d]) for d in range(2) for s, w in raws[d]]
        + [_full_spec(p) for d in range(2) for p in params[d]]
        + [ss_spec(orders[0]), ss_spec(orders[1]), _chunk_spec(do, Wv, orders[0]), _chunk_spec(do, Wv, orders[1])]
        + _scan_const_specs(dk),
        out_specs=[row_out(w, orders[d]) for d in range(2) for _, w in raws[d]]
        + [_full_spec(p) for d in range(2) for p in params[d]],
        scratch_shapes=[pltpu.VMEM((2, heads, dv, dk), F32)],
        name=name, compiler_params=_cparams(("arbitrary",)))(
            *[s[0] for d in range(2) for s, _ in raws[d]], *[p for d in range(2) for p in params[d]],
            ss[0], ss[1], do[0], do[0], *_scan_const_args())


def final_call(x, g, target, T):
    tr = _row_tile(T)

    def tile(xv, gv, tv):
        y = _rms(xv, gv)
        err = (y - tv) ** 2
        return jnp.sum(jnp.sum(err, axis=-1, keepdims=True), axis=0, keepdims=True) * (0.5 / D_MODEL)

    def body(x_ref, g_ref, t_ref, loss_ref, dx_ref, dg_ref):
        i = pl.program_id(0)
        tv = t_ref[...]
        lv, vjp = jax.vjp(lambda a, b: tile(a, b, tv), x_ref[...], g_ref[...])
        dx, dg = vjp(jnp.ones((1, 1), F32))
        dx_ref[...] = dx

        @pl.when(i == 0)
        def _():
            loss_ref[...] = jnp.zeros_like(loss_ref)
            dg_ref[...] = jnp.zeros_like(dg_ref)

        loss_ref[...] += jnp.broadcast_to(lv, loss_ref.shape)
        dg_ref[...] += dg

    return pl.pallas_call(
        body,
        out_shape=[jax.ShapeDtypeStruct((8, 128), F32), jax.ShapeDtypeStruct((T, D_MODEL), F32),
                   jax.ShapeDtypeStruct((1, D_MODEL), F32)],
        grid=(T // tr,),
        in_specs=[pl.BlockSpec((tr, D_MODEL), lambda i: (i, 0)), pl.BlockSpec((1, D_MODEL), lambda i: (0, 0)),
                  pl.BlockSpec((tr, D_MODEL), lambda i: (i, 0))],
        out_specs=[pl.BlockSpec((8, 128), lambda i: (0, 0)), pl.BlockSpec((tr, D_MODEL), lambda i: (i, 0)),
                   pl.BlockSpec((1, D_MODEL), lambda i: (0, 0))],
        name="final_loss", compiler_params=_cparams(("arbitrary",)))(x, g, target)


def adamw_call(w, g, m, v):
    shape = w.shape
    c = shape[-1]
    r = int(np.prod(shape[:-1])) if len(shape) > 1 else 1
    tr = r if r <= 256 else 256
    assert r % tr == 0

    def body(w_ref, g_ref, m_ref, v_ref, d_ref, nm_ref, nv_ref):
        gv = g_ref[...]
        nm = ADAM_B1 * m_ref[...] + (1.0 - ADAM_B1) * gv
        nv = ADAM_B2 * v_ref[...] + (1.0 - ADAM_B2) * jnp.square(gv)
        m_hat = nm / (1.0 - ADAM_B1 ** ADAM_STEP)
        v_hat = nv / (1.0 - ADAM_B2 ** ADAM_STEP)
        d_ref[...] = -ADAM_LR * (m_hat / (jnp.sqrt(v_hat) + ADAM_EPS) + ADAM_WD * w_ref[...])
        nm_ref[...] = nm
        nv_ref[...] = nv

    spec = pl.BlockSpec((tr, c), lambda i: (i, 0))
    outs = pl.pallas_call(body, out_shape=[jax.ShapeDtypeStruct((r, c), F32)] * 3, grid=(r // tr,),
                          in_specs=[spec] * 4, out_specs=[spec] * 3, name="adamw",
                          compiler_params=_cparams(("arbitrary",)))(*(t.reshape(r, c) for t in (w, g, m, v)))
    return tuple(o.reshape(shape) for o in outs)


def adamw_halves(w, mine, other, m, v, c):
    L, R, C = w.shape
    rh = R // 2
    tr = rh if rh <= 256 else rh // 2
    assert tr % 8 == 0
    nbh = rh // tr

    def body(c_ref, w_ref, a_ref, b_ref, m_ref, v_ref, g_ref, d_ref, nm_ref, nv_ref):
        is_mine = (pl.program_id(1) // nbh) == c_ref[0]
        gv = jnp.where(is_mine, a_ref[...], b_ref[...])
        nm = ADAM_B1 * m_ref[...] + (1.0 - ADAM_B1) * gv
        nv = ADAM_B2 * v_ref[...] + (1.0 - ADAM_B2) * jnp.square(gv)
        m_hat = nm / (1.0 - ADAM_B1 ** ADAM_STEP)
        v_hat = nv / (1.0 - ADAM_B2 ** ADAM_STEP)
        g_ref[...] = gv
        d_ref[...] = -ADAM_LR * (m_hat / (jnp.sqrt(v_hat) + ADAM_EPS) + ADAM_WD * w_ref[...])
        nm_ref[...] = nm
        nv_ref[...] = nv

    full = pl.BlockSpec((None, tr, C), lambda l, i, c_ref: (l, i, 0))
    half = pl.BlockSpec((None, tr, C), lambda l, i, c_ref: (l, i % nbh, 0))
    grid_spec = pltpu.PrefetchScalarGridSpec(num_scalar_prefetch=1, grid=(L, R // tr),
                                             in_specs=[full, half, half, full, full], out_specs=[full] * 4)
    return pl.pallas_call(body, out_shape=[jax.ShapeDtypeStruct(w.shape, F32)] * 4, grid_spec=grid_spec,
                          name="adamw_halves", compiler_params=_cparams(("arbitrary", "arbitrary")))(c, w, mine, other, m, v)


def sum_devices(g64):
    def body(x_ref, o_ref):
        acc = x_ref[0:8, :]
        for d in range(1, 8):
            acc = acc + x_ref[8 * d:8 * d + 8, :]
        o_ref[...] = acc

    return pl.pallas_call(body, out_shape=jax.ShapeDtypeStruct((8, D_MODEL), F32), name="sum_devices")(g64)


def _half_tile(rh):
    if rh <= 512:
        return rh
    return next(rh // d for d in range(2, rh) if rh % d == 0 and (rh // d) % 16 == 0 and rh // d <= 512)


def add_sibling(g, recv, c, out_dtype):
    _, R, C = g.shape
    rh = R // 2
    tr = _half_tile(rh)
    nblk = rh // tr

    def body(c_ref, g_ref, r_ref, o_ref):
        o_ref[...] = (g_ref[...] + r_ref[...]).astype(o_ref.dtype)

    grid_spec = pltpu.PrefetchScalarGridSpec(
        num_scalar_prefetch=1, grid=(4, nblk),
        in_specs=[pl.BlockSpec((None, tr, C), lambda j, i, c_ref: (j, i + c_ref[0] * nblk, 0)),
                  pl.BlockSpec((None, tr, C), lambda j, i, c_ref: (j, i, 0))],
        out_specs=pl.BlockSpec((None, tr, C), lambda j, i, c_ref: (j, i, 0)))
    return pl.pallas_call(body, out_shape=jax.ShapeDtypeStruct((4, rh, C), out_dtype), grid_spec=grid_spec,
                          name="rs_add_sibling", compiler_params=_cparams(("arbitrary", "arbitrary")))(c, g, recv)


def add_chips(g, recv, r3, place):
    _, R, C = g.shape
    rh = R // 2
    tr = _half_tile(rh)
    nblk = rh // tr

    def body(p_ref, g_ref, s_ref, a_ref, b_ref, c_ref, o_ref):
        up = lambda r: r[...].astype(F32)
        o_ref[...] = (((g_ref[...] + up(s_ref)) + up(a_ref)) + up(b_ref)) + up(c_ref)

    grid_spec = pltpu.PrefetchScalarGridSpec(
        num_scalar_prefetch=1, grid=(nblk,),
        in_specs=[pl.BlockSpec((None, tr, C), lambda i, p_ref: (p_ref[0], i + p_ref[1] * nblk, 0)),
                  pl.BlockSpec((None, tr, C), lambda i, p_ref: (p_ref[0], i, 0))]
        + [pl.BlockSpec((None, tr, C), functools.partial(lambda i, p_ref, k: (k, i, 0), k=k)) for k in range(3)],
        out_specs=pl.BlockSpec((tr, C), lambda i, p_ref: (i, 0)))
    return pl.pallas_call(body, out_shape=jax.ShapeDtypeStruct((rh, C), F32), grid_spec=grid_spec,
                          name="rs_add_chips", compiler_params=_cparams(("arbitrary",)))(place, g, recv, r3, r3, r3)


def _remote(src, dst, ssem, rsem, dev):
    return pltpu.make_async_remote_copy(src_ref=src, dst_ref=dst, send_sem=ssem, recv_sem=rsem,
                                        device_id=dev, device_id_type=pl.DeviceIdType.MESH)


def _mesh_places():
    x, y, c = lax.axis_index("x"), lax.axis_index("y"), lax.axis_index("c")
    chips = [(1 - x, y), (x, 1 - y), (1 - x, 1 - y)]
    return x, y, c, (x, y, 1 - c), chips


def _hbm_specs(n):
    return [pl.BlockSpec(memory_space=pltpu.HBM) for _ in range(n)]


def _gather_body(ins, outs, n_split, send_sems, recv_sems, handshake):
    x, y, c, sibling, chips = _mesh_places()
    mine = 2 * x + y
    if handshake:
        barrier = pltpu.get_barrier_semaphore()
        peers = [sibling] + [(*chip, c) for chip in chips]
        for peer in peers:
            pl.semaphore_signal(barrier, inc=1, device_id=peer, device_id_type=pl.DeviceIdType.MESH)
        pl.semaphore_wait(barrier, len(peers))

    def half(a, chip_idx, which):
        rh = ins[a].shape[0] // 2
        return outs[a].at[chip_idx, pl.ds(which * rh, rh), :]

    sent = []
    for a in range(len(ins)):
        for k, chip in enumerate(chips):
            if a < n_split:
                rh = ins[a].shape[0] // 2
                src, dst = ins[a].at[pl.ds(c * rh, rh), :], half(a, mine, c)
            else:
                src, dst = ins[a], outs[a].at[mine]
            sent.append(_remote(src, dst, send_sems.at[a, k], recv_sems.at[a, k], (*chip, c)))
    for cp in sent:
        cp.start()
    for a in range(len(ins)):
        for k, chip in enumerate(chips):
            j = 2 * chip[0] + chip[1]
            region = half(a, j, c) if a < n_split else outs[a].at[j]
            _remote(region, region, send_sems.at[a, k], recv_sems.at[a, k], (*chip, c)).wait_recv()
            if a < n_split:
                fwd = _remote(region, region, send_sems.at[a, 3 + k], recv_sems.at[a, 3 + k], sibling)
                fwd.start()
                sent.append(fwd)
    for a in range(n_split):
        for k, chip in enumerate(chips):
            region = half(a, 2 * chip[0] + chip[1], 1 - c)
            _remote(region, region, send_sems.at[a, 3 + k], recv_sems.at[a, 3 + k], sibling).wait_recv()
    for cp in sent:
        cp.wait_send()


def gather_weights(shards, small):
    arrs = list(shards) + [small]
    n = len(arrs)

    def body(*refs):
        _gather_body(refs[:n], refs[n:2 * n], n - 1, refs[2 * n], refs[2 * n + 1], handshake=False)

    return pl.pallas_call(
        body, out_shape=[jax.ShapeDtypeStruct((4,) + a.shape, a.dtype) for a in arrs],
        in_specs=_hbm_specs(n), out_specs=_hbm_specs(n),
        scratch_shapes=[pltpu.SemaphoreType.DMA((n, 6)), pltpu.SemaphoreType.DMA((n, 6))],
        name="gather_weights")(*arrs)


def gather_weights_async(shards):
    n = len(shards)

    def body(*refs):
        _gather_body(refs[:n], refs[n:2 * n], n, refs[2 * n], refs[2 * n + 1], handshake=True)

    return pl.kernel(
        body, out_type=[jax.ShapeDtypeStruct((4,) + a.shape, a.dtype) for a in shards],
        mesh=plsc.ScalarSubcoreMesh(axis_name="seq", num_cores=1),
        scratch_types=[pltpu.SemaphoreType.DMA((n, 6)), pltpu.SemaphoreType.DMA((n, 6))],
        compiler_params=pltpu.CompilerParams(collective_id=1), name="gather_weights_async")(*shards)


def _sequencer_call(name, body, out_type, sem_shape, collective_id, args):
    return pl.kernel(
        body, out_type=out_type, mesh=plsc.ScalarSubcoreMesh(axis_name="seq", num_cores=1),
        scratch_types=[pltpu.SemaphoreType.DMA(sem_shape), pltpu.SemaphoreType.DMA(sem_shape)],
        compiler_params=pltpu.CompilerParams(collective_id=collective_id), name=name)(*args)


def _handshake(peers):
    barrier = pltpu.get_barrier_semaphore()
    for peer in peers:
        pl.semaphore_signal(barrier, inc=1, device_id=peer, device_id_type=pl.DeviceIdType.MESH)
    pl.semaphore_wait(barrier, len(peers))


def exchange_siblings(name, srcs, halves, collective_id):
    n = len(srcs)

    def body(*refs):
        ins, outs = refs[:n], refs[n:2 * n]
        send_sems, recv_sems = refs[2 * n:]
        x, y, c, sibling, chips = _mesh_places()
        _handshake([sibling])
        cps = []
        for a in range(n):
            src = ins[a]
            if halves:
                rh = src.shape[1] // 2
                src = src.at[:, pl.ds((1 - c) * rh, rh), :]
            cps.append(_remote(src, outs[a], send_sems.at[a], recv_sems.at[a], sibling))
        for cp in cps:
            cp.start()
        for cp in cps:
            cp.wait()

    shape = lambda g: (4, g.shape[1] // 2, g.shape[2]) if halves else g.shape
    return _sequencer_call(name, body, [jax.ShapeDtypeStruct(shape(g), g.dtype) for g in srcs], (n,), collective_id, srcs)


def exchange_chips(name, s1s, collective_id):
    n = len(s1s)

    def body(*refs):
        ins, outs = refs[:n], refs[n:2 * n]
        send_sems, recv_sems = refs[2 * n:]
        x, y, c, sibling, chips = _mesh_places()
        _handshake([(*chip, c) for chip in chips])
        cps = []
        for a in range(n):
            for k, chip in enumerate(chips):
                cps.append(_remote(ins[a].at[2 * chip[0] + chip[1]], outs[a].at[k], send_sems.at[a, k],
                                   recv_sems.at[a, k], (*chip, c)))
        for cp in cps:
            cp.start()
        for cp in cps:
            cp.wait()

    return _sequencer_call(name, body, [jax.ShapeDtypeStruct((3,) + s.shape[1:], s.dtype) for s in s1s], (n, 3),
                           collective_id, s1s)


def allgather_small(v):
    m_per = v.shape[0]

    def body(x_ref, out_ref, send_sems, recv_sems, local_sem):
        x, y, c, sibling, chips = _mesh_places()
        me = (x, y, c)

        def rows(px, py, pc):
            return out_ref.at[pl.ds((4 * px + 2 * py + pc) * m_per, m_per), :]

        def copy(k, block, to, src=None):
            return _remote(rows(*block) if src is None else src, rows(*block), send_sems.at[k], recv_sems.at[k], to)

        mine = pltpu.make_async_copy(x_ref, rows(*me), local_sem)
        mine.start()
        first = [copy(0, me, sibling, src=x_ref)]
        first += [copy(1 + j, me, (*chip, c), src=x_ref) for j, chip in enumerate(chips)]
        for cp in first:
            cp.start()
        passed = [copy(4 + j, (*chip, c), sibling) for j, chip in enumerate(chips)]
        for j, chip in enumerate(chips):
            copy(1 + j, (*chip, c), me).wait_recv()
            passed[j].start()
        copy(0, sibling, me).wait_recv()
        for j, chip in enumerate(chips):
            copy(4 + j, (*chip, 1 - c), me).wait_recv()
        for cp in first + passed:
            cp.wait_send()
        mine.wait()

    return pl.pallas_call(
        body, out_shape=jax.ShapeDtypeStruct((8 * m_per, v.shape[1]), v.dtype),
        in_specs=[pl.BlockSpec(memory_space=pltpu.VMEM)], out_specs=pl.BlockSpec(memory_space=pltpu.VMEM),
        scratch_shapes=[pltpu.SemaphoreType.DMA((7,)), pltpu.SemaphoreType.DMA((7,)), pltpu.SemaphoreType.DMA],
        name="allgather_small")(v)


def rms_res_tile(x, g):
    return (_rms(x, g), x)


def _lower_bounds(lb_param):
    lbs = jax.nn.softmax(lb_param.astype(F32), axis=0)
    return jnp.cumsum(lbs, axis=0) - lbs[0]


def _even_fwd(x, i, W, lower, kv, slopes, T):
    O = EVEN_OFF
    g = W["norm_even"][i].reshape(1, D_MODEL)
    (h,) = rows_call("rms_fwd", rms_tile, T, [("row", x, 0, D_MODEL), ("full", g)], [D_MODEL], [BF16])
    p = matmul("mm_in_e", h, W["w_in_e"][i], "nn")
    kvp = jnp.pad(p[:, O["kA"]:O["kA"] + 2 * W_KV_A], ((BLOCK, BLOCK), (0, 0)))
    sink = jnp.repeat(W["sink"][i], BLOCK).reshape(N_Q_A * BLOCK, 1)
    a = attn_fwd(p, O["qA"], kvp, sink, slopes, T)
    scan_raws = [[((p, O["qB"]), W_B), ((p, O[z]), W_B), ((p, O["iB"]), W_B)] for z in ("zf", "zb")]
    scan_pars = [[lower[i][0:1]], [lower[i][1:2]]]
    o_f, o_b, ss_f, ss_b = scan_fwd("scan_fwd_h", hgrn_prep, scan_raws, scan_pars, N_HEADS_B, HEAD_DIM_B, HEAD_DIM_B, T)
    mo = mem_fwd(p, O["qM"], kv, T)
    hg = W["hgrn_norm"][i].reshape(1, W_B)
    post_ins = [("row", a, 0, W_A), ("row", o_f, 0, W_B), ("row", o_b, 0, W_B), ("row", mo, 0, W_M),
                ("row", p, O["gA"], W_A), ("row", p, O["gB"], W_B), ("row", p, O["gM"], W_M), ("full", hg)]
    (mix,) = rows_call("even_post_fwd", even_post_tile, T, post_ins, [MIX], [BF16])
    x_new = matmul("mm_out", mix, W["w_out_e"][i], "nn", add=x)
    return x_new, dict(x=x, g=g, h=h, p=p, kvp=kvp, sink=sink, scan_raws=scan_raws, scan_pars=scan_pars,
                       ss=(ss_f, ss_b), post_ins=post_ins, mix=mix)


def _add2(a, b):
    return a.astype(F32) + b.astype(F32)


def _assemble_even(dqA, dgA, dqB_f, dqB_b, dzf, dzb, diB_f, diB_b, dgB, dqM, dgM, dkvA):
    parts = [dqA, dgA, _add2(dqB_f, dqB_b), dzf, dzb, _add2(diB_f, diB_b), dgB, dqM, dgM, dkvA]
    return (jnp.concatenate([t.astype(BF16) for t in parts], axis=-1),)


def _even_bwd(dxo, sv, i, W, kv, slopes, T, sync):
    O = EVEN_OFF
    p = sv["p"]
    dmix = matmul("mm_dmix", dxo, W["w_out_e"][i], "nt")
    dwo = matmul("mm_dwo", sv["mix"], dxo, "tn")
    da, dof, dmo, dgA, dgB, dgM, dhg = rows_vjp_call("even_post_bwd", even_post_tile, T, sv["post_ins"],
                                                      [[("row", dmix, 0, MIX)]], skip=(2,), narrow=(4, 5, 6))
    dqA, dkvp, dsink = attn_bwd(p, O["qA"], sv["kvp"], sv["sink"], slopes, da, T)
    dkvA = dkvp[BLOCK:-BLOCK]
    dqB_f, dzf, diB_f, dqB_b, dzb, diB_b, dlow_f, dlow_b = scan_bwd(
        "scan_bwd_h", hgrn_prep, sv["scan_raws"], sv["scan_pars"], sv["ss"], (dof, 0), N_HEADS_B, HEAD_DIM_B, HEAD_DIM_B, T)
    dqB_f = sync(dqB_f)
    row = lambda arr, w: ("row", arr, 0, w)
    dlow = jnp.concatenate([dlow_f, dlow_b], axis=0)
    dqM, dkv = mem_bwd(p, O["qM"], kv, dmo, T)
    (dp,) = rows_call("even_dp", _assemble_even, T,
                      [row(dqA, W_A), row(dgA, W_A), row(dqB_f, W_B), row(dqB_b, W_B), row(dzf, W_B), row(dzb, W_B),
                       row(diB_f, W_B), row(diB_b, W_B), row(dgB, W_B), row(dqM, W_M), row(dgM, W_M),
                       row(dkvA, 2 * W_KV_A)],
                      [EVEN_IN], [BF16])
    dh = matmul("mm_dh_e", dp, W["w_in_e"][i], "nt")
    dwi = matmul("mm_dwi_e", sv["h"], dp, "tn")
    dx, dg = rows_vjp_call("rms_res_bwd", rms_res_tile, T, [("row", sv["x"], 0, D_MODEL), ("full", sv["g"])],
                           [[("row", dh, 0, D_MODEL)], [("row", dxo, 0, D_MODEL)]])
    return dx, dict(w_in=dwi, w_out=dwo, norm=dg[0], sink=dsink.reshape(N_Q_A), low=dlow, hg=dhg[0], kv=dkv)


def _pad_gate_up(w_up):
    z = jnp.zeros((2, 128, WK_C), F32)
    z = z.at[0, 0:GATE_RANK].set(w_up[0])
    return z.at[1, GATE_RANK:2 * GATE_RANK].set(w_up[1])


def _odd_fwd(x, i, W, kv, T):
    O = ODD_OFF
    g = W["norm_odd"][i].reshape(1, D_MODEL)
    (h,) = rows_call("rms_fwd", rms_tile, T, [("row", x, 0, D_MODEL), ("full", g)], [D_MODEL], [BF16])
    p = matmul("mm_in_o", h, W["w_in_o"][i], "nn")
    wup = _pad_gate_up(W["w_gate_up"][i])
    one_dir = [((p, O["qC"]), WK_C), ((p, O["kC"]), WK_C), ((p, O["vC"]), WV_C), ((p, O["rr"]), 128)]
    scan_raws = [one_dir, one_dir]
    scan_pars = [[wup[d], W["b_gate"][i][d:d + 1]] for d in range(2)]
    o_f, o_b, ss_f, ss_b = scan_fwd("scan_fwd_g", gla_prep, scan_raws, scan_pars, N_HEADS_C, DK_C, DV_C, T)
    mo = mem_fwd(p, O["qM"], kv, T)
    gg = W["gla_norm"][i].reshape(1, WV_C)
    post_ins = [("row", o_f, 0, WV_C), ("row", o_b, 0, WV_C), ("row", mo, 0, W_M),
                ("row", p, O["gC"], WV_C), ("row", p, O["gM"], W_M), ("full", gg)]
    (mix,) = rows_call("odd_post_fwd", odd_post_tile, T, post_ins, [MIX], [BF16])
    x_new = matmul("mm_out", mix, W["w_out_o"][i], "nn", add=x)
    return x_new, dict(x=x, g=g, h=h, p=p, scan_raws=scan_raws, scan_pars=scan_pars, ss=(ss_f, ss_b),
                       post_ins=post_ins, mix=mix)


def _assemble_odd(dq0, dq1, dk0, dk1, dv0, dv1, dgC, dqM, dgM, dr0, dr1):
    parts = [_add2(dq0, dq1), _add2(dk0, dk1), _add2(dv0, dv1), dgC, dqM, dgM, _add2(dr0, dr1)]
    return (jnp.concatenate([t.astype(BF16) for t in parts], axis=-1),)


def _odd_bwd(dxo, sv, i, W, kv, T, sync):
    O = ODD_OFF
    p = sv["p"]
    dmix = matmul("mm_dmix", dxo, W["w_out_o"][i], "nt")
    dwo = matmul("mm_dwo", sv["mix"], dxo, "tn")
    dof, dmo, dgC, dgM, dgg = rows_vjp_call("odd_post_bwd", odd_post_tile, T, sv["post_ins"],
                                            [[("row", dmix, 0, MIX)]], skip=(1,), narrow=(3, 4))
    dqf, dkf, dvf, dr_f, dqb, dkb, dvb, dr_b, dwup_f, dbg_f, dwup_b, dbg_b = scan_bwd(
        "scan_bwd_g", gla_prep, sv["scan_raws"], sv["scan_pars"], sv["ss"], (dof, 0), N_HEADS_C, DK_C, DV_C, T)
    dqf = sync(dqf)
    row = lambda arr, w: ("row", arr, 0, w)
    dqM, dkv = mem_bwd(p, O["qM"], kv, dmo, T)
    (dp,) = rows_call("odd_dp", _assemble_odd, T,
                      [row(dqf, WK_C), row(dqb, WK_C), row(dkf, WK_C), row(dkb, WK_C), row(dvf, WV_C), row(dvb, WV_C),
                       row(dgC, WV_C), row(dqM, W_M), row(dgM, W_M), row(dr_f, 128), row(dr_b, 128)],
                      [ODD_PAD], [BF16])
    dh = matmul("mm_dh_o", dp, W["w_in_o"][i], "nt")
    dwi = matmul("mm_dwi_o", sv["h"], dp, "tn")
    dx, dg = rows_vjp_call("rms_res_bwd", rms_res_tile, T, [("row", sv["x"], 0, D_MODEL), ("full", sv["g"])],
                           [[("row", dh, 0, D_MODEL)], [("row", dxo, 0, D_MODEL)]])
    dw_up = jnp.stack([dwup_f[0:GATE_RANK], dwup_b[GATE_RANK:2 * GATE_RANK]])
    dbg = jnp.concatenate([dbg_f, dbg_b], axis=0)
    return dx, dict(w_in=dwi, w_out=dwo, norm=dg[0], w_up=dw_up, b_gate=dbg, gg=dgg[0], kv=dkv)


def local_step(x, mem, target, W, later=None, on_layer_grads=None, sync=lambda a: a):
    T = x.shape[0]
    slopes = jnp.repeat(2.0 ** (-8.0 * jnp.arange(1, N_Q_A + 1, dtype=F32) / N_Q_A), BLOCK).reshape(N_Q_A * BLOCK, 1)
    lower, lower_vjp = jax.vjp(_lower_bounds, W["lb_param"])
    mem_g = W["mem_norm"].reshape(1, D_MODEL)
    (mem_n,) = rows_call("mem_rms_fwd", rms_tile, N_MEM, [("row", mem, 0, D_MODEL), ("full", mem_g)], [D_MODEL], [BF16])
    kvs, saved = [], []
    for l in range(DEPTH):
        if l == 1 and later is not None:
            x, W = later(x, W)
        kvs.append(matmul("mm_kv", mem_n, W["w_kv"][l], "nn"))
        if l % 2 == 0:
            x, sv = _even_fwd(x, l // 2, W, lower, kvs[l], slopes, T)
        else:
            x, sv = _odd_fwd(x, l // 2, W, kvs[l], T)
        saved.append(sv)
    loss, dx, dgf = final_call(x, W["final_norm"].reshape(1, D_MODEL), target, T)
    per = [None] * DEPTH
    dmem_n = None
    for l in reversed(range(DEPTH)):
        if l % 2 == 0:
            dx, per[l] = _even_bwd(dx, saved[l], l // 2, W, kvs[l], slopes, T, sync)
        else:
            dx, per[l] = _odd_bwd(dx, saved[l], l // 2, W, kvs[l], T, sync)
        per[l]["w_kv"] = matmul("mm_dwkv", mem_n, per[l]["kv"], "tn")
        dmem_n = matmul("mm_dmem", per[l]["kv"], W["w_kv"][l], "nt", add=dmem_n)
        if on_layer_grads is not None:
            dx = on_layer_grads(l, dx, per[l])
    dw_kv = [per[l]["w_kv"] for l in range(DEPTH)]
    (dmem_norm,) = rows_vjp_call("mem_rms_bwd", rms_tile, N_MEM, [("row", mem, 0, D_MODEL), ("full", mem_g)],
                                 [[("row", dmem_n, 0, D_MODEL)]], skip=(0,))
    ev, od = (per[0], per[2]), (per[1], per[3])
    (d_lb,) = lower_vjp(jnp.stack([e["low"] for e in ev]))
    grads = dict(
        w_in_e=jnp.stack([e["w_in"] for e in ev]), w_in_o=jnp.stack([o["w_in"] for o in od]),
        w_out_e=jnp.stack([e["w_out"] for e in ev]), w_out_o=jnp.stack([o["w_out"] for o in od]),
        w_kv=jnp.stack(dw_kv), norm_even=jnp.stack([e["norm"] for e in ev]), sink=jnp.stack([e["sink"] for e in ev]),
        lb_param=d_lb, hgrn_norm=jnp.stack([e["hg"] for e in ev]), norm_odd=jnp.stack([o["norm"] for o in od]),
        w_gate_up=jnp.stack([o["w_up"] for o in od]), b_gate=jnp.stack([o["b_gate"] for o in od]),
        gla_norm=jnp.stack([o["gg"] for o in od]), mem_norm=dmem_norm[0], final_norm=dgf[0])
    return loss, dx, grads


SMALL_SPECS = (("lb_param", (2, 2, 128)), ("norm_odd", (2, 256)), ("w_gate_up", (2, 2, 16, 128)),
               ("b_gate", (2, 2, 128)), ("gla_norm", (2, 256)))
SMALL_ROWS = 80


def _pack_small_local(d):
    return jnp.concatenate([d[n].reshape(-1) for n, _ in SMALL_SPECS]).reshape(SMALL_ROWS, 128)


def _unpack_small_local(b):
    flat, out, o = b.reshape(-1), {}, 0
    for n, shp in SMALL_SPECS:
        sz = int(np.prod(shp))
        out[n] = flat[o:o + sz].reshape(shp)
        o += sz
    return out


def _unpack_small_full(g4):
    per = [_unpack_small_local(g4[j]) for j in range(4)]
    return {n: jnp.concatenate([per[j][n] for j in range(4)], axis=-1) for n, _ in SMALL_SPECS}


def _pack_small_blocks(full):
    blocks = []
    for j in range(4):
        blocks.append(_pack_small_local({n: full[n][..., j * shp[-1]:(j + 1) * shp[-1]] for n, shp in SMALL_SPECS}))
    return jnp.stack(blocks)


def _cols(t, order, off, widths):
    return [t[..., off[n]:off[n] + widths[n]] for n in order]


EVEN_REF_ORDER = ("qA", "kA", "vA", "gA", "qB", "zf", "zb", "iB", "gB", "qM", "gM")
ODD_REF_ORDER = ("qC", "kC", "vC", "gC", "rr", "qM", "gM")


def _layer_weights(l, g_in, g_out, g_kv):
    t = g_in.transpose(1, 0, 2).reshape(D_MODEL, -1)
    if l % 2 == 0:
        w_in = jnp.concatenate(_cols(t, EVEN_ORDER, EVEN_REF_OFF, EVEN_W), axis=-1)
    else:
        w_in = jnp.concatenate(_cols(t, ODD_ORDER, ODD_REF_OFF, ODD_W) + [jnp.zeros((D_MODEL, ODD_PAD - ODD_IN), BF16)],
                               axis=-1)
    return w_in, g_out.reshape(MIX, D_MODEL), g_kv.reshape(D_MODEL, 2 * W_M)


def _layer_grad_blocks(l, gl):
    if l % 2 == 0:
        t = jnp.concatenate(_cols(gl["w_in"], EVEN_REF_ORDER, EVEN_OFF, EVEN_W), axis=-1)
        b_in = t.reshape(D_MODEL, 4, -1).transpose(1, 2, 0)
    else:
        t = jnp.concatenate(_cols(gl["w_in"], ODD_REF_ORDER, ODD_OFF, ODD_W), axis=-1)
        b_in = t.reshape(D_MODEL, 4, -1).transpose(1, 0, 2)
    return [b_in, gl["w_out"].reshape(4, MIX // 4, D_MODEL), gl["w_kv"].reshape(4, D_MODEL // 4, 2 * W_M)]


WEIGHT_NAMES = ("norm_even", "w_in_even", "sink", "lb_param", "hgrn_norm", "w_out_even", "norm_odd", "w_in_odd",
                "w_gate_up", "b_gate", "gla_norm", "w_out_odd", "mem_norm", "w_mem_kv", "final_norm")


def kernel(x, mem, norm_even, w_in_even, sink, lb_param, hgrn_norm, w_out_even, norm_odd, w_in_odd, w_gate_up, b_gate, gla_norm, w_out_odd, mem_norm, w_mem_kv, final_norm, loss_target, m_norm_even, m_w_in_even, m_sink, m_lb_param, m_hgrn_norm, m_w_out_even, m_norm_odd, m_w_in_odd, m_w_gate_up, m_b_gate, m_gla_norm, m_w_out_odd, m_mem_norm, m_w_mem_kv, m_final_norm, v_norm_even, v_w_in_even, v_sink, v_lb_param, v_hgrn_norm, v_w_out_even, v_norm_odd, v_w_in_odd, v_w_gate_up, v_b_gate, v_gla_norm, v_w_out_odd, v_mem_norm, v_w_mem_kv, v_final_norm):
    w = dict(zip(WEIGHT_NAMES, (norm_even, w_in_even, sink, lb_param, hgrn_norm, w_out_even, norm_odd, w_in_odd,
                                w_gate_up, b_gate, gla_norm, w_out_odd, mem_norm, w_mem_kv, final_norm)))
    m = dict(zip(WEIGHT_NAMES, (m_norm_even, m_w_in_even, m_sink, m_lb_param, m_hgrn_norm, m_w_out_even, m_norm_odd,
                                m_w_in_odd, m_w_gate_up, m_b_gate, m_gla_norm, m_w_out_odd, m_mem_norm, m_w_mem_kv,
                                m_final_norm)))
    v = dict(zip(WEIGHT_NAMES, (v_norm_even, v_w_in_even, v_sink, v_lb_param, v_hgrn_norm, v_w_out_even, v_norm_odd,
                                v_w_in_odd, v_w_gate_up, v_b_gate, v_gla_norm, v_w_out_odd, v_mem_norm, v_w_mem_kv,
                                v_final_norm)))
    ci = lax.axis_index("c").astype(jnp.int32).reshape(1)
    chip = (2 * lax.axis_index("x") + lax.axis_index("y")).astype(jnp.int32).reshape(1)

    shards = []
    for l in range(DEPTH):
        names = ("w_in_even", "w_out_even") if l % 2 == 0 else ("w_in_odd", "w_out_odd")
        shards.append([w[names[0]][l // 2].astype(BF16), w[names[1]][l // 2].astype(BF16), w_mem_kv[l].astype(BF16)])
    small = _pack_small_local(w)
    own = lambda g, s: lax.dynamic_update_slice(g, s[None], (chip[0], 0, 0))
    first = [own(g, s) for g, s in zip(gather_weights(shards[0], small), shards[0] + [small])]
    later_shards = shards[1] + shards[2] + shards[3]
    later_raw = gather_weights_async(later_shards)
    w0 = _layer_weights(0, *first[0:3])
    W = dict(w_in_e=[w0[0]], w_out_e=[w0[1]], w_kv=[w0[2]])
    W.update(_unpack_small_full(first[3]))
    W.update({n: w[n] for n in ("norm_even", "sink", "hgrn_norm", "mem_norm", "final_norm")})

    def later(x1, W):
        x1, raw = lax.optimization_barrier((x1, list(later_raw)))
        g = [own(a, s) for a, s in zip(raw, later_shards)]
        w1, w2, w3 = (_layer_weights(l, *g[3 * (l - 1):3 * l]) for l in (1, 2, 3))
        W = dict(W)
        W.update(w_in_e=[w0[0], w2[0]], w_in_o=[w1[0], w3[0]], w_out_e=[w0[1], w2[1]], w_out_o=[w1[1], w3[1]],
                 w_kv=[w0[2], w1[2], w2[2], w3[2]])
        return x1, W

    place = jnp.concatenate([chip, ci])

    def start(tag, blocks, wire):
        return dict(tag=tag, blocks=blocks, wire=wire, step=0,
                    recv=exchange_siblings(f"rs_siblings_{tag}", blocks, True, 2))

    def advance(p):
        if p["step"] == 0:
            sums = [add_sibling(g, r, ci, dt) for g, r, dt in zip(p["blocks"], p["recv"], p["wire"])]
            p["recv3"] = exchange_chips(f"rs_chips_{p['tag']}", sums, 3)
        else:
            p["mine"] = [add_chips(g, r, r3, place) for g, r, r3 in zip(p["blocks"], p["recv"], p["recv3"])]
            p["other"] = exchange_siblings(f"rs_final_{p['tag']}", p["mine"], False, 4)
        p["step"] += 1

    pipes, first_layer = [], {}

    def sync(a):
        for p in pipes:
            if p["step"] < 3:
                key = ("recv", "recv3", "other")[p["step"]]
                a, arrived = lax.optimization_barrier((a, list(p[key])))
                p[key] = arrived
                if p["step"] < 2:
                    advance(p)
                else:
                    p["step"] = 3
        return a

    def on_layer_grads(l, dx, gl):
        dx = sync(dx)
        if l == 0:
            first_layer.update(gl)
        else:
            pipes.append(start(f"l{l}", _layer_grad_blocks(l, gl), [BF16] * 3))
        return dx

    loss_tile, dx, grads = local_step(x[0], mem[0], loss_target[0], W, later, on_layer_grads, sync)
    pipes.append(start("l0", _layer_grad_blocks(0, first_layer) + [_pack_small_blocks(grads)], [BF16] * 3 + [F32]))
    while any(p["step"] < 2 for p in pipes):
        for p in pipes:
            if p["step"] < 2:
                advance(p)
    by_layer = {int(p["tag"][1:]): p for p in pipes}
    halves = lambda layers, k: (jnp.stack([by_layer[l]["mine"][k] for l in layers]),
                                jnp.stack([by_layer[l]["other"][k] for l in layers]))
    big = dict(w_in_even=halves((0, 2), 0), w_in_odd=halves((1, 3), 0), w_out_even=halves((0, 2), 1),
               w_out_odd=halves((1, 3), 1), w_mem_kv=halves((0, 1, 2, 3), 2))
    s_mine, s_other = by_layer[0]["mine"][3], by_layer[0]["other"][3]
    g_small = jnp.where(ci[0] == 0, jnp.concatenate([s_mine, s_other]), jnp.concatenate([s_other, s_mine]))
    gl = _unpack_small_local(g_small)

    pack = jnp.zeros((8, D_MODEL), F32)
    pack = pack.at[0:2].set(grads["norm_even"]).at[2].set(grads["hgrn_norm"].reshape(-1))
    pack = pack.at[3].set(grads["mem_norm"]).at[4].set(grads["final_norm"])
    pack = pack.at[5, 0:16].set(grads["sink"].reshape(-1)).at[5, 16].set(loss_tile[0, 0])
    tot = sum_devices(allgather_small(pack))
    gl.update(norm_even=tot[0:2], hgrn_norm=tot[2].reshape(2, W_B), mem_norm=tot[3], final_norm=tot[4],
              sink=tot[5, 0:16].reshape(2, N_Q_A))
    loss = tot[5, 16]

    upd = {}
    for n in WEIGHT_NAMES:
        if n == "w_in_even":
            tr_ = lambda a: jnp.swapaxes(a, 1, 2)
            gl[n], *upd[n] = [tr_(o) for o in adamw_halves(tr_(w[n]), *big[n], tr_(m[n]), tr_(v[n]), ci)]
        elif n in big:
            gl[n], *upd[n] = adamw_halves(w[n], *big[n], m[n], v[n], ci)
        else:
            upd[n] = adamw_call(w[n], gl[n], m[n], v[n])
    return (loss, dx[None], *[gl[n] for n in WEIGHT_NAMES], *[upd[n][0] for n in WEIGHT_NAMES],
            *[upd[n][1] for n in WEIGHT_NAMES], *[upd[n][2] for n in WEIGHT_NAMES])
```

```python
import functools

import numpy as np
import jax
import jax.numpy as jnp
from jax import lax
from jax.experimental import pallas as pl
from jax.experimental.pallas import tpu as pltpu
from jax.experimental.pallas import tpu_sc as plsc

F32 = jnp.float32
BF16 = jnp.bfloat16

D_MODEL = 1024
DEPTH = 4
N_Q_A, N_KV_A, HEAD_DIM_A = 8, 2, 64
W_A, W_KV_A = 512, 128
WINDOW = 128
BLOCK = 128
N_HEADS_B, HEAD_DIM_B, W_B = 4, 128, 512
N_HEADS_C, DK_C, DV_C, WK_C, WV_C = 4, 128, 256, 512, 1024
GATE_RANK = 16
GATE_TEMP = 16.0
N_MEM, N_HEADS_M, HEAD_DIM_M, W_M = 256, 4, 128, 512
EPS = 1e-6
MASK_VALUE = -1e30
MIN_GATE = 1e-30
EVEN_IN, ODD_IN = 4864, 4128
ODD_PAD = 4224
MIX = 1536
ADAM_LR, ADAM_B1, ADAM_B2, ADAM_EPS, ADAM_WD, ADAM_STEP = 0.001, 0.9, 0.999, 1e-08, 0.01, 10

SCAN_CHUNK = 128
SCAN_LEVELS = 7
VMEM_LIMIT = 56 * 1024 * 1024

EVEN_REF_OFF = dict(qA=0, kA=512, vA=640, gA=768, qB=1280, zf=1792, zb=2304, iB=2816, gB=3328, qM=3840, gM=4352)
EVEN_W = dict(qA=512, kA=128, vA=128, gA=512, qB=512, zf=512, zb=512, iB=512, gB=512, qM=512, gM=512)
EVEN_ORDER = ("qA", "gA", "qB", "zf", "zb", "iB", "gB", "qM", "gM", "kA", "vA")
ODD_REF_OFF = dict(qC=0, kC=512, vC=1024, gC=2048, rr=3072, qM=3104, gM=3616)
ODD_W = dict(qC=512, kC=512, vC=1024, gC=1024, rr=32, qM=512, gM=512)
ODD_ORDER = ("qC", "kC", "vC", "gC", "qM", "gM", "rr")


def _offsets(order, widths):
    off, o = {}, 0
    for n in order:
        off[n] = o
        o += widths[n]
    return off


EVEN_OFF = _offsets(EVEN_ORDER, EVEN_W)
ODD_OFF = _offsets(ODD_ORDER, ODD_W)


def _dg(a, b, ca, cb):
    return lax.dot_general(a.astype(BF16), b.astype(BF16), (((ca,), (cb,)), ((), ())),
                           preferred_element_type=F32)


def dot_nn(a, b):
    return _dg(a, b, 1, 0)


def dot_nt(a, b):
    return _dg(a, b, 1, 1)


def dot_tn(a, b):
    return _dg(a, b, 0, 0)


@jax.custom_vjp
def bdot(a, b):
    return dot_nn(a, b)


bdot.defvjp(lambda a, b: (dot_nn(a, b), (a, b)),
            lambda r, g: (dot_nt(g, r[1]), dot_tn(r[0], g)))


@jax.custom_vjp
def bdot_t(a, b):
    return dot_nt(a, b)


bdot_t.defvjp(lambda a, b: (dot_nt(a, b), (a, b)),
              lambda r, g: (dot_nn(g, r[1]), dot_tn(g, r[0])))


@jax.custom_vjp
def bdot_tn(a, b):
    return dot_tn(a, b)


bdot_tn.defvjp(lambda a, b: (dot_tn(a, b), (a, b)),
               lambda r, g: (dot_nt(r[1], g), dot_nn(r[0], g)))


def _split_mm(h, x):
    hi = x.astype(BF16)
    lo = (x - hi.astype(F32)).astype(BF16)
    return (lax.dot_general(h, hi, (((1,), (0,)), ((), ())), preferred_element_type=F32)
            + lax.dot_general(h, lo, (((1,), (0,)), ((), ())), preferred_element_type=F32))


def _sigmoid(z):
    return 1.0 / (1.0 + jnp.exp(-z))


def _silu(z):
    return z * _sigmoid(z)


def _log_sigmoid(z):
    return jnp.minimum(z, 0.0) - jnp.log(1.0 + jnp.exp(-jnp.abs(z)))


def _rms(x, g):
    return x * lax.rsqrt(jnp.mean(x * x, axis=-1, keepdims=True) + EPS) * g


def rms_tile(x, g):
    return (_rms(x, g),)


@functools.partial(jax.custom_vjp, nondiff_argnums=(1, 2))
def split(x, n, axis):
    w = x.shape[axis] // n
    return tuple(lax.slice_in_dim(x, h * w, (h + 1) * w, axis=axis) for h in range(n))


split.defvjp(lambda x, n, axis: (split(x, n, axis), None),
             lambda n, axis, _, cts: (jnp.concatenate(cts, axis=axis),))


def _group_rms(o, g, heads):
    return jnp.concatenate([_rms(oh, gh) for oh, gh in zip(split(o, heads, 1), split(g, heads, 1))], axis=-1)


def even_post_tile(a, o2f, o2b, mo, gA, gB, gM, hg):
    y = _group_rms(o2f + o2b, hg, N_HEADS_B)
    return (jnp.concatenate([a * _silu(gA), y * _silu(gB), mo * _silu(gM)], axis=-1),)


def odd_post_tile(o2f, o2b, mo, gC, gM, gg):
    y = _group_rms(o2f + o2b, gg, N_HEADS_C)
    return (jnp.concatenate([y * _silu(gC), mo * _silu(gM)], axis=-1),)


def hgrn_prep(raw, par):
    qB, z, iB = raw
    (lb,) = par
    f = lb + (1.0 - lb) * _sigmoid(z)
    return _silu(qB), (1.0 - lb) * _sigmoid(-z), iB, jnp.log(jnp.maximum(f, MIN_GATE))


def gla_prep(raw, par):
    qC, kC, vC, r128 = raw
    wup, bg = par
    return qC * (DK_C ** -0.5), kC, vC, _log_sigmoid(bdot(r128, wup) + bg) / GATE_TEMP


def mem_tile(q, k, v):
    s = bdot_t(q, k) * (HEAD_DIM_M ** -0.5)
    m = lax.stop_gradient(jnp.max(s, axis=-1, keepdims=True))
    p = jnp.exp(s - m)
    p = p / jnp.sum(p, axis=-1, keepdims=True)
    return (bdot(p, v),)


ATTN_GROUP = N_Q_A // N_KV_A


def attn_block(q, ks, vs, sink, slope, c, seq):
    rows = ATTN_GROUP * BLOCK
    i = lax.broadcasted_iota(jnp.int32, (rows, 3 * BLOCK), 0) % BLOCK
    j = lax.broadcasted_iota(jnp.int32, (rows, 3 * BLOCK), 1)
    dist = jnp.abs(i - j + BLOCK).astype(F32)
    kpos = (c - 1) * BLOCK + j
    valid = (dist <= WINDOW) & (kpos >= 0) & (kpos < seq)
    s = bdot_t(q, ks) * (HEAD_DIM_A ** -0.5)
    s = jnp.where(valid, s - slope * dist, MASK_VALUE)
    m = lax.stop_gradient(jnp.maximum(jnp.max(s, axis=-1, keepdims=True), sink))
    p = jnp.where(valid, jnp.exp(s - m), 0.0)
    denom = jnp.sum(p, axis=-1, keepdims=True) + jnp.exp(sink - m)
    return bdot(p, vs) / denom


def scan_chunk(q, k, v, e, tot, st, qm, pm):
    C = SCAN_CHUNK
    e = split(e, 2 + SCAN_LEVELS, 0)
    qe = q * jnp.exp(e[0])
    kd = k * jnp.exp(e[1])
    r = lax.broadcasted_iota(jnp.int32, (C, C), 0)
    s = lax.broadcasted_iota(jnp.int32, (C, C), 1)
    a = jnp.where(r == s, jnp.sum(q * k, axis=-1, keepdims=True), 0.0)
    for l in range(SCAN_LEVELS):
        u = jnp.where(qm[l * C:(l + 1) * C] != 0.0, q, k) * jnp.exp(e[2 + l])
        a = a + bdot_t(u, u) * pm[l * C:(l + 1) * C]
    o = bdot_t(qe, st) + bdot(a, v)
    st_new = st * jnp.exp(tot) + bdot_tn(v, kd)
    return o, st_new


def _scan_consts():
    C, L = SCAN_CHUNK, SCAN_LEVELS
    t = np.arange(C)[:, None]
    r = np.arange(C)[None, :]
    blocks = [(r <= t), (r > t)]
    qms, pms = [], []
    for l in range(1, L + 1):
        m = C >> l
        upper_t = (t % (2 * m)) >= m
        upper_r = (r % (2 * m)) >= m
        same_half = (t // m) == (r // m)
        blocks.append(same_half & np.where(upper_t, r <= t, r > t))
        qms.append(np.broadcast_to(upper_t, (C, C)))
        pms.append(((t // (2 * m)) == (r // (2 * m))) & upper_t & ~upper_r)
    hf = np.concatenate(blocks, axis=0).astype(np.float32)
    flip = lambda mat: mat.reshape(-1, C, C)[:, ::-1, ::-1].reshape(-1, C)
    qmf = np.concatenate(qms, axis=0).astype(np.float32)
    pmf = np.concatenate(pms, axis=0).astype(np.float32)
    h = np.stack([hf, flip(hf)])
    ht = np.stack([h[0].T, h[1].T])
    qm = np.stack([qmf, 1.0 - qmf])
    pm = np.stack([pmf, flip(pmf)])
    return h, ht, qm, pm


def _cparams(sem):
    return pltpu.CompilerParams(dimension_semantics=sem, vmem_limit_bytes=VMEM_LIMIT)


def _row_tile(T):
    return min(T, 512)


def _in_spec(spec, tr):
    kind = spec[0]
    if kind == "row":
        _, arr, off, w = spec
        assert off % w == 0
        return arr, pl.BlockSpec((tr, w), functools.partial(lambda i, b: (i, b), b=off // w))
    if kind == "row3":
        _, arr, d, off, w = spec
        assert off % w == 0
        return arr, pl.BlockSpec((None, tr, w), functools.partial(lambda i, d, b: (d, i, b), d=d, b=off // w))
    _, arr = spec
    return arr, pl.BlockSpec(arr.shape, functools.partial(lambda i, n: (0,) * n, n=arr.ndim))


def rows_call(name, tile_fn, T, ins, out_widths, out_dtypes=None, stacks=None):
    tr = _row_tile(T)
    n_in = len(ins)
    out_dtypes = out_dtypes or [F32] * len(out_widths)
    stacks = stacks or [(k,) for k in range(len(out_widths))]

    def body(*refs):
        vals = [r[...] for r in refs[:n_in]]
        outs = tile_fn(*vals)
        for r, members in zip(refs[n_in:], stacks):
            if len(members) == 1:
                r[...] = outs[members[0]].astype(r.dtype)
            else:
                for d, k in enumerate(members):
                    r[d] = outs[k].astype(r.dtype)

    in_specs, args = [], []
    for spec in ins:
        arr, bs = _in_spec(spec, tr)
        args.append(arr)
        in_specs.append(bs)
    out_specs, out_shape = [], []
    for w, dt, members in zip(out_widths, out_dtypes, stacks):
        n = len(members)
        if n == 1:
            out_specs.append(pl.BlockSpec((tr, w), lambda i: (i, 0)))
            out_shape.append(jax.ShapeDtypeStruct((T, w), dt))
        else:
            out_specs.append(pl.BlockSpec((n, tr, w), lambda i: (0, i, 0)))
            out_shape.append(jax.ShapeDtypeStruct((n, T, w), dt))
    return pl.pallas_call(body, out_shape=out_shape, grid=(T // tr,), in_specs=in_specs, out_specs=out_specs,
                          name=name, compiler_params=_cparams(("arbitrary",)))(*args)


def rows_vjp_call(name, tile_fn, T, ins, cts, skip=(), narrow=()):
    tr = _row_tile(T)
    n_in = len(ins)
    n_ct = [len(c) for c in cts]
    want = [k for k in range(n_in) if k not in skip]

    def body(*refs):
        i = pl.program_id(0)
        vals = [r[...] for r in refs[:n_in]]
        ct, pos = [], n_in
        for n in n_ct:
            acc = refs[pos][...]
            for r in refs[pos + 1:pos + n]:
                acc = acc + r[...]
            ct.append(acc)
            pos += n
        _, vjp = jax.vjp(tile_fn, *vals)
        grads = vjp(tuple(ct))
        for r, k in zip(refs[pos:], want):
            if ins[k][0] == "full":
                @pl.when(i == 0)
                def _():
                    r[...] = jnp.zeros_like(r)
                r[...] += grads[k]
            else:
                r[...] = grads[k].astype(r.dtype)

    in_specs, args = [], []
    for spec in list(ins) + [s for c in cts for s in c]:
        arr, bs = _in_spec(spec, tr)
        args.append(arr)
        in_specs.append(bs)
    out_specs, out_shape = [], []
    for k in want:
        if ins[k][0] == "full":
            arr = ins[k][1]
            out_specs.append(pl.BlockSpec(arr.shape, functools.partial(lambda i, n: (0,) * n, n=arr.ndim)))
            out_shape.append(jax.ShapeDtypeStruct(arr.shape, F32))
        else:
            w = ins[k][-1]
            out_specs.append(pl.BlockSpec((tr, w), lambda i: (i, 0)))
            out_shape.append(jax.ShapeDtypeStruct((T, w), BF16 if k in narrow else F32))
    return pl.pallas_call(body, out_shape=out_shape, grid=(T // tr,), in_specs=in_specs, out_specs=out_specs,
                          name=name, compiler_params=_cparams(("arbitrary",)))(*args)


def matmul(name, a, b, mode, add=None, out_dtype=F32):
    if mode == "tn":
        K, M = a.shape
        N = b.shape[1]
        tm = M if M <= 1536 else 512
        tn = N if N <= 1280 else (N // 2 if (N // 2) % 128 == 0 else N)
        tk = min(K, 512)
        grid = (M // tm, N // tn, K // tk)

        def body(a_ref, b_ref, o_ref):
            @pl.when(pl.program_id(2) == 0)
            def _():
                o_ref[...] = jnp.zeros_like(o_ref)
            o_ref[...] += dot_tn(a_ref[...], b_ref[...])

        return pl.pallas_call(
            body, out_shape=jax.ShapeDtypeStruct((M, N), F32), grid=grid,
            in_specs=[pl.BlockSpec((tk, tm), lambda i, j, k: (k, i)), pl.BlockSpec((tk, tn), lambda i, j, k: (k, j))],
            out_specs=pl.BlockSpec((tm, tn), lambda i, j, k: (i, j)), name=name,
            compiler_params=_cparams(("arbitrary", "arbitrary", "arbitrary")))(a, b)

    M, K = a.shape
    N = b.shape[1] if mode == "nn" else b.shape[0]
    tm = min(M, 512)
    tn = N if N <= 1536 else (N // 2 if (N // 2) % 128 == 0 else (N // 3 if (N // 3) % 128 == 0 else N))
    grid = (N // tn, M // tm)
    n_in = 2 + (add is not None)

    def body(*refs):
        a_ref, b_ref = refs[0], refs[1]
        o_ref = refs[n_in]
        acc = dot_nn(a_ref[...], b_ref[...]) if mode == "nn" else dot_nt(a_ref[...], b_ref[...])
        if add is not None:
            acc = acc + refs[2][...]
        o_ref[...] = acc.astype(o_ref.dtype)

    in_specs = [pl.BlockSpec((tm, K), lambda j, i: (i, 0)),
                pl.BlockSpec((K, tn), lambda j, i: (0, j)) if mode == "nn" else pl.BlockSpec((tn, K), lambda j, i: (j, 0))]
    args = [a, b]
    if add is not None:
        in_specs.append(pl.BlockSpec((tm, tn), lambda j, i: (i, j)))
        args.append(add)
    return pl.pallas_call(
        body, out_shape=jax.ShapeDtypeStruct((M, N), out_dtype), grid=grid, in_specs=in_specs,
        out_specs=pl.BlockSpec((tm, tn), lambda j, i: (i, j)), name=name,
        compiler_params=_cparams(("arbitrary", "arbitrary")))(*args)


def _attn_heads(n):
    G = N_Q_A // N_KV_A
    k_sl = pl.ds(n * HEAD_DIM_A, HEAD_DIM_A)
    v_sl = pl.ds(W_KV_A + n * HEAD_DIM_A, HEAD_DIM_A)
    q_sl = [pl.ds((n * G + g) * HEAD_DIM_A, HEAD_DIM_A) for g in range(G)]
    return k_sl, v_sl, q_sl, range(n * G, (n + 1) * G)


def attn_fwd(p, q_off, kvp, sink, slopes, T):
    nb = T // BLOCK
    assert q_off % W_A == 0

    def body(q_ref, kv_ref, sink_ref, slope_ref, o_ref):
        c = pl.program_id(0)
        rows = pl.ds(pl.multiple_of(c * BLOCK, BLOCK), 3 * BLOCK)
        for n in range(N_KV_A):
            k_sl, v_sl, q_sl, heads = _attn_heads(n)
            group = pl.ds(n * ATTN_GROUP * BLOCK, ATTN_GROUP * BLOCK)
            q = jnp.concatenate([q_ref[:, s] for s in q_sl], axis=0)
            o = attn_block(q, kv_ref[rows, k_sl], kv_ref[rows, v_sl], sink_ref[group, :], slope_ref[group, :], c, T)
            for g, s in enumerate(q_sl):
                o_ref[:, s] = o[g * BLOCK:(g + 1) * BLOCK]

    full = lambda a: pl.BlockSpec(a.shape, functools.partial(lambda c, nd: (0,) * nd, nd=a.ndim))
    return pl.pallas_call(
        body, out_shape=jax.ShapeDtypeStruct((T, W_A), F32), grid=(nb,),
        in_specs=[pl.BlockSpec((BLOCK, W_A), lambda c: (c, q_off // W_A)), full(kvp), full(sink), full(slopes)],
        out_specs=pl.BlockSpec((BLOCK, W_A), lambda c: (c, 0)),
        name="attn_fwd", compiler_params=_cparams(("arbitrary",)))(p, kvp, sink, slopes)


def attn_bwd(p, q_off, kvp, sink, slopes, do, T):
    nb = T // BLOCK

    def body(q_ref, kv_ref, sink_ref, slope_ref, do_ref, dq_ref, dkv_ref, dsink_ref):
        c = pl.program_id(0)

        @pl.when(c == 0)
        def _():
            dkv_ref[...] = jnp.zeros_like(dkv_ref)
            dsink_ref[...] = jnp.zeros_like(dsink_ref)

        rows = pl.ds(pl.multiple_of(c * BLOCK, BLOCK), 3 * BLOCK)
        for n in range(N_KV_A):
            k_sl, v_sl, q_sl, heads = _attn_heads(n)
            group = pl.ds(n * ATTN_GROUP * BLOCK, ATTN_GROUP * BLOCK)
            slope = slope_ref[group, :]
            q = jnp.concatenate([q_ref[:, s] for s in q_sl], axis=0)
            do = jnp.concatenate([do_ref[:, s] for s in q_sl], axis=0)
            _, vjp = jax.vjp(lambda q_, kk, vv, sk: attn_block(q_, kk, vv, sk, slope, c, T),
                             q, kv_ref[rows, k_sl], kv_ref[rows, v_sl], sink_ref[group, :])
            dq, dks, dvs, dsk = vjp(do)
            dkv_ref[rows, k_sl] += dks
            dkv_ref[rows, v_sl] += dvs
            for g, (s, h) in enumerate(zip(q_sl, heads)):
                seg = slice(g * BLOCK, (g + 1) * BLOCK)
                dq_ref[:, s] = dq[seg].astype(dq_ref.dtype)
                dsink_ref[h] += jnp.sum(dsk[seg], axis=0, keepdims=True)

    full = lambda a: pl.BlockSpec(a.shape, functools.partial(lambda c, nd: (0,) * nd, nd=a.ndim))
    qspec = pl.BlockSpec((BLOCK, W_A), lambda c: (c, 0))
    return pl.pallas_call(
        body,
        out_shape=[jax.ShapeDtypeStruct((T, W_A), BF16), jax.ShapeDtypeStruct(kvp.shape, F32),
                   jax.ShapeDtypeStruct((N_Q_A, 1, 1), F32)],
        grid=(nb,),
        in_specs=[pl.BlockSpec((BLOCK, W_A), lambda c: (c, q_off // W_A)), full(kvp), full(sink), full(slopes), qspec],
        out_specs=[qspec, full(kvp), pl.BlockSpec((N_Q_A, 1, 1), lambda c: (0, 0, 0))],
        name="attn_bwd", compiler_params=_cparams(("arbitrary",)))(p, kvp, sink, slopes, do)


def mem_fwd(p, q_off, kv, T):
    tr = min(T, 2 * _row_tile(T))
    assert q_off % W_M == 0

    def body(q_ref, kv_ref, o_ref):
        for h in range(N_HEADS_M):
            hs = pl.ds(h * HEAD_DIM_M, HEAD_DIM_M)
            (o,) = mem_tile(q_ref[:, hs], kv_ref[:, hs], kv_ref[:, pl.ds(W_M + h * HEAD_DIM_M, HEAD_DIM_M)])
            o_ref[:, hs] = o

    return pl.pallas_call(
        body, out_shape=jax.ShapeDtypeStruct((T, W_M), F32), grid=(T // tr,),
        in_specs=[pl.BlockSpec((tr, W_M), lambda i: (i, q_off // W_M)), pl.BlockSpec((N_MEM, 2 * W_M), lambda i: (0, 0))],
        out_specs=pl.BlockSpec((tr, W_M), lambda i: (i, 0)),
        name="mem_fwd", compiler_params=_cparams(("arbitrary",)))(p, kv)


def mem_bwd(p, q_off, kv, do, T):
    tr = min(T, 2 * _row_tile(T))

    def body(q_ref, kv_ref, do_ref, dq_ref, dkv_ref):
        @pl.when(pl.program_id(0) == 0)
        def _():
            dkv_ref[...] = jnp.zeros_like(dkv_ref)

        for h in range(N_HEADS_M):
            hs = pl.ds(h * HEAD_DIM_M, HEAD_DIM_M)
            vs = pl.ds(W_M + h * HEAD_DIM_M, HEAD_DIM_M)
            _, vjp = jax.vjp(mem_tile, q_ref[:, hs], kv_ref[:, hs], kv_ref[:, vs])
            dq, dk, dv = vjp((do_ref[:, hs],))
            dq_ref[:, hs] = dq.astype(dq_ref.dtype)
            dkv_ref[:, hs] += dk
            dkv_ref[:, vs] += dv

    kvspec = pl.BlockSpec((N_MEM, 2 * W_M), lambda i: (0, 0))
    return pl.pallas_call(
        body,
        out_shape=[jax.ShapeDtypeStruct((T, W_M), BF16), jax.ShapeDtypeStruct((N_MEM, 2 * W_M), F32)],
        grid=(T // tr,),
        in_specs=[pl.BlockSpec((tr, W_M), lambda i: (i, q_off // W_M)), kvspec, pl.BlockSpec((tr, W_M), lambda i: (i, 0))],
        out_specs=[pl.BlockSpec((tr, W_M), lambda i: (i, 0)), kvspec],
        name="mem_bwd", compiler_params=_cparams(("arbitrary",)))(p, kv, do)


def _scan_const_specs(dk):
    C, L = SCAN_CHUNK, SCAN_LEVELS
    return [pl.BlockSpec((2, (2 + L) * C, C), lambda n: (0, 0, 0)),
            pl.BlockSpec((2, C, (2 + L) * C), lambda n: (0, 0, 0)),
            pl.BlockSpec((2, L * C, dk), lambda n: (0, 0, 0)),
            pl.BlockSpec((2, L * C, C), lambda n: (0, 0, 0))]


def _chunk_spec(src, width, chunk_of):
    arr, sel = src
    if arr.ndim == 2:
        assert sel % width == 0
        return pl.BlockSpec((SCAN_CHUNK, width), functools.partial(lambda n, b: (chunk_of(n), b), b=sel // width))
    return pl.BlockSpec((None, SCAN_CHUNK, width), functools.partial(lambda n, d: (d, chunk_of(n), 0), d=sel))


def _scan_const_args():
    h, ht, qm, pm = _scan_consts()
    return [jnp.asarray(h, BF16), jnp.asarray(ht, BF16), jnp.asarray(qm, F32), jnp.asarray(pm, F32)]


def _full_spec(a):
    return pl.BlockSpec(a.shape, functools.partial(lambda n, nd: (0,) * nd, nd=a.ndim))


def scan_fwd(name, prep, raws, params, heads, dk, dv, T):
    C = SCAN_CHUNK
    N = T // C
    assert dk == C
    Wv = heads * dv
    orders = (lambda n: n, lambda n: N - 1 - n)
    n_raw, n_par = [len(r) for r in raws], [len(p) for p in params]

    def body(*refs):
        pos, raw_refs, par_refs = 0, [], []
        for d in range(2):
            raw_refs.append(refs[pos:pos + n_raw[d]])
            pos += n_raw[d]
        for d in range(2):
            par_refs.append(refs[pos:pos + n_par[d]])
            pos += n_par[d]
        h_ref, ht_ref, qm_ref, pm_ref = refs[pos:pos + 4]
        o_refs, ss_refs, st_ref = refs[pos + 4:pos + 6], refs[pos + 6:pos + 8], refs[pos + 8]

        @pl.when(pl.program_id(0) == 0)
        def _():
            st_ref[...] = jnp.zeros_like(st_ref)

        for d in range(2):
            consts = (qm_ref[d], pm_ref[d])
            q, k, v, g = prep([r[...] for r in raw_refs[d]], [p[...] for p in par_refs[d]])
            e = _split_mm(h_ref[d], g)
            tot = jnp.sum(g, axis=0, keepdims=True)
            for h in range(heads):
                ks, vs = slice(h * dk, (h + 1) * dk), slice(h * dv, (h + 1) * dv)
                st = st_ref[d, h]
                ss_refs[d][h] = st
                o, st_new = scan_chunk(q[:, ks], k[:, ks], v[:, vs], e[:, ks], tot[:, ks], st, *consts)
                o_refs[d][:, vs] = o
                st_ref[d, h] = st_new

    ss_spec = lambda order: pl.BlockSpec((heads, None, dv, dk), lambda n: (0, order(n), 0, 0))
    return pl.pallas_call(
        body,
        out_shape=[jax.ShapeDtypeStruct((T, Wv), F32)] * 2 + [jax.ShapeDtypeStruct((heads, N, dv, dk), F32)] * 2,
        grid=(N,),
        in_specs=[_chunk_spec(s, w, orders[d]) for d in range(2) for s, w in raws[d]]
        + [_full_spec(p) for d in range(2) for p in params[d]] + _scan_const_specs(dk),
        out_specs=[pl.BlockSpec((C, Wv), lambda n: (orders[0](n), 0)), pl.BlockSpec((C, Wv), lambda n: (orders[1](n), 0)),
                   ss_spec(orders[0]), ss_spec(orders[1])],
        scratch_shapes=[pltpu.VMEM((2, heads, dv, dk), F32)],
        name=name, compiler_params=_cparams(("arbitrary",)))(
            *[s[0] for d in range(2) for s, _ in raws[d]], *[p for d in range(2) for p in params[d]], *_scan_const_args())


def scan_bwd(name, prep, raws, params, ss, do, heads, dk, dv, T):
    C = SCAN_CHUNK
    N = T // C
    Wv = heads * dv
    orders = (lambda n: N - 1 - n, lambda n: n)
    n_raw, n_par = [len(r) for r in raws], [len(p) for p in params]

    def body(*refs):
        pos, raw_refs, par_refs, draw_refs, dpar_refs = 0, [], [], [], []
        for group, counts in ((raw_refs, n_raw), (par_refs, n_par)):
            for d in range(2):
                group.append(refs[pos:pos + counts[d]])
                pos += counts[d]
        ss_refs, do_refs = refs[pos:pos + 2], refs[pos + 2:pos + 4]
        h_ref, ht_ref, qm_ref, pm_ref = refs[pos + 4:pos + 8]
        pos += 8
        for group, counts in ((draw_refs, n_raw), (dpar_refs, n_par)):
            for d in range(2):
                group.append(refs[pos:pos + counts[d]])
                pos += counts[d]
        dst_ref = refs[pos]

        @pl.when(pl.program_id(0) == 0)
        def _():
            dst_ref[...] = jnp.zeros_like(dst_ref)
            for d in range(2):
                for r in dpar_refs[d]:
                    r[...] = jnp.zeros_like(r)

        for d in range(2):
            consts = (qm_ref[d], pm_ref[d])
            (q, k, v, g), prep_vjp = jax.vjp(prep, [r[...] for r in raw_refs[d]], [p[...] for p in par_refs[d]])
            e = _split_mm(h_ref[d], g)
            tot = jnp.sum(g, axis=0, keepdims=True)
            dqs, dks, dvs, des, dtots = [], [], [], [], []
            for h in range(heads):
                ks, vs = slice(h * dk, (h + 1) * dk), slice(h * dv, (h + 1) * dv)
                _, vjp = jax.vjp(lambda q_, k_, v_, e_, t_, st_: scan_chunk(q_, k_, v_, e_, t_, st_, *consts),
                                 q[:, ks], k[:, ks], v[:, vs], e[:, ks], tot[:, ks], ss_refs[d][h])
                dq, dk_, dv_, de, dtot, dst = vjp((do_refs[d][:, vs], dst_ref[d, h]))
                dst_ref[d, h] = dst
                for group, val in ((dqs, dq), (dks, dk_), (dvs, dv_), (des, de), (dtots, dtot)):
                    group.append(val)
            cat = lambda parts: jnp.concatenate(parts, axis=-1)
            dg = _split_mm(ht_ref[d], cat(des)) + cat(dtots)
            draws, dpars = prep_vjp((cat(dqs), cat(dks), cat(dvs), dg))
            for r, val in zip(draw_refs[d], draws):
                r[...] = val.astype(r.dtype)
            for r, val in zip(dpar_refs[d], dpars):
                r[...] += val

    ss_spec = lambda order: pl.BlockSpec((heads, None, dv, dk), lambda n: (0, order(n), 0, 0))
    row_out = lambda w, order: pl.BlockSpec((C, w), lambda n: (order(n), 0))
    return pl.pallas_call(
        body,
        out_shape=[jax.ShapeDtypeStruct((T, w), BF16) for d in range(2) for _, w in raws[d]]
        + [jax.ShapeDtypeStruct(p.shape, F32) for d in range(2) for p in params[d]],
        grid=(N,),
        in_specs=[_chunk_spec(s, w, orders[d]) for d in range(2) for s, w in raws[d]]
        + [_full_spec(p) for d in range(2) for p in params[d]]
        + [ss_spec(orders[0]), ss_spec(orders[1]), _chunk_spec(do, Wv, orders[0]), _chunk_spec(do, Wv, orders[1])]
        + _scan_const_specs(dk),
        out_specs=[row_out(w, orders[d]) for d in range(2) for _, w in raws[d]]
        + [_full_spec(p) for d in range(2) for p in params[d]],
        scratch_shapes=[pltpu.VMEM((2, heads, dv, dk), F32)],
        name=name, compiler_params=_cparams(("arbitrary",)))(
            *[s[0] for d in range(2) for s, _ in raws[d]], *[p for d in range(2) for p in params[d]],
            ss[0], ss[1], do[0], do[0], *_scan_const_args())


def final_call(x, g, target, T):
    tr = _row_tile(T)

    def tile(xv, gv, tv):
        y = _rms(xv, gv)
        err = (y - tv) ** 2
        return jnp.sum(jnp.sum(err, axis=-1, keepdims=True), axis=0, keepdims=True) * (0.5 / D_MODEL)

    def body(x_ref, g_ref, t_ref, loss_ref, dx_ref, dg_ref):
        i = pl.program_id(0)
        tv = t_ref[...]
        lv, vjp = jax.vjp(lambda a, b: tile(a, b, tv), x_ref[...], g_ref[...])
        dx, dg = vjp(jnp.ones((1, 1), F32))
        dx_ref[...] = dx

        @pl.when(i == 0)
        def _():
            loss_ref[...] = jnp.zeros_like(loss_ref)
            dg_ref[...] = jnp.zeros_like(dg_ref)

        loss_ref[...] += jnp.broadcast_to(lv, loss_ref.shape)
        dg_ref[...] += dg

    return pl.pallas_call(
        body,
        out_shape=[jax.ShapeDtypeStruct((8, 128), F32), jax.ShapeDtypeStruct((T, D_MODEL), F32),
                   jax.ShapeDtypeStruct((1, D_MODEL), F32)],
        grid=(T // tr,),
        in_specs=[pl.BlockSpec((tr, D_MODEL), lambda i: (i, 0)), pl.BlockSpec((1, D_MODEL), lambda i: (0, 0)),
                  pl.BlockSpec((tr, D_MODEL), lambda i: (i, 0))],
        out_specs=[pl.BlockSpec((8, 128), lambda i: (0, 0)), pl.BlockSpec((tr, D_MODEL), lambda i: (i, 0)),
                   pl.BlockSpec((1, D_MODEL), lambda i: (0, 0))],
        name="final_loss", compiler_params=_cparams(("arbitrary",)))(x, g, target)


def adamw_call(w, g, m, v):
    shape = w.shape
    c = shape[-1]
    r = int(np.prod(shape[:-1])) if len(shape) > 1 else 1
    tr = r if r <= 256 else 256
    assert r % tr == 0

    def body(w_ref, g_ref, m_ref, v_ref, d_ref, nm_ref, nv_ref):
        gv = g_ref[...]
        nm = ADAM_B1 * m_ref[...] + (1.0 - ADAM_B1) * gv
        nv = ADAM_B2 * v_ref[...] + (1.0 - ADAM_B2) * jnp.square(gv)
        m_hat = nm / (1.0 - ADAM_B1 ** ADAM_STEP)
        v_hat = nv / (1.0 - ADAM_B2 ** ADAM_STEP)
        d_ref[...] = -ADAM_LR * (m_hat / (jnp.sqrt(v_hat) + ADAM_EPS) + ADAM_WD * w_ref[...])
        nm_ref[...] = nm
        nv_ref[...] = nv

    spec = pl.BlockSpec((tr, c), lambda i: (i, 0))
    outs = pl.pallas_call(body, out_shape=[jax.ShapeDtypeStruct((r, c), F32)] * 3, grid=(r // tr,),
                          in_specs=[spec] * 4, out_specs=[spec] * 3, name="adamw",
                          compiler_params=_cparams(("arbitrary",)))(*(t.reshape(r, c) for t in (w, g, m, v)))
    return tuple(o.reshape(shape) for o in outs)


def adamw_halves(w, mine, other, m, v, c):
    L, R, C = w.shape
    rh = R // 2
    tr = rh if rh <= 256 else rh // 2
    assert tr % 8 == 0
    nbh = rh // tr

    def body(c_ref, w_ref, a_ref, b_ref, m_ref, v_ref, g_ref, d_ref, nm_ref, nv_ref):
        is_mine = (pl.program_id(1) // nbh) == c_ref[0]
        gv = jnp.where(is_mine, a_ref[...], b_ref[...])
        nm = ADAM_B1 * m_ref[...] + (1.0 - ADAM_B1) * gv
        nv = ADAM_B2 * v_ref[...] + (1.0 - ADAM_B2) * jnp.square(gv)
        m_hat = nm / (1.0 - ADAM_B1 ** ADAM_STEP)
        v_hat = nv / (1.0 - ADAM_B2 ** ADAM_STEP)
        g_ref[...] = gv
        d_ref[...] = -ADAM_LR * (m_hat / (jnp.sqrt(v_hat) + ADAM_EPS) + ADAM_WD * w_ref[...])
        nm_ref[...] = nm
        nv_ref[...] = nv

    full = pl.BlockSpec((None, tr, C), lambda l, i, c_ref: (l, i, 0))
    half = pl.BlockSpec((None, tr, C), lambda l, i, c_ref: (l, i % nbh, 0))
    grid_spec = pltpu.PrefetchScalarGridSpec(num_scalar_prefetch=1, grid=(L, R // tr),
                                             in_specs=[full, half, half, full, full], out_specs=[full] * 4)
    return pl.pallas_call(body, out_shape=[jax.ShapeDtypeStruct(w.shape, F32)] * 4, grid_spec=grid_spec,
                          name="adamw_halves", compiler_params=_cparams(("arbitrary", "arbitrary")))(c, w, mine, other, m, v)


def sum_devices(g64):
    def body(x_ref, o_ref):
        acc = x_ref[0:8, :]
        for d in range(1, 8):
            acc = acc + x_ref[8 * d:8 * d + 8, :]
        o_ref[...] = acc

    return pl.pallas_call(body, out_shape=jax.ShapeDtypeStruct((8, D_MODEL), F32), name="sum_devices")(g64)


def _half_tile(rh):
    if rh <= 512:
        return rh
    return next(rh // d for d in range(2, rh) if rh % d == 0 and (rh // d) % 16 == 0 and rh // d <= 512)


def add_sibling(g, recv, c, out_dtype):
    _, R, C = g.shape
    rh = R // 2
    tr = _half_tile(rh)
    nblk = rh // tr

    def body(c_ref, g_ref, r_ref, o_ref):
        o_ref[...] = (g_ref[...] + r_ref[...]).astype(o_ref.dtype)

    grid_spec = pltpu.PrefetchScalarGridSpec(
        num_scalar_prefetch=1, grid=(4, nblk),
        in_specs=[pl.BlockSpec((None, tr, C), lambda j, i, c_ref: (j, i + c_ref[0] * nblk, 0)),
                  pl.BlockSpec((None, tr, C), lambda j, i, c_ref: (j, i, 0))],
        out_specs=pl.BlockSpec((None, tr, C), lambda j, i, c_ref: (j, i, 0)))
    return pl.pallas_call(body, out_shape=jax.ShapeDtypeStruct((4, rh, C), out_dtype), grid_spec=grid_spec,
                          name="rs_add_sibling", compiler_params=_cparams(("arbitrary", "arbitrary")))(c, g, recv)


def add_chips(g, recv, r3, place):
    _, R, C = g.shape
    rh = R // 2
    tr = _half_tile(rh)
    nblk = rh // tr

    def body(p_ref, g_ref, s_ref, a_ref, b_ref, c_ref, o_ref):
        up = lambda r: r[...].astype(F32)
        o_ref[...] = (((g_ref[...] + up(s_ref)) + up(a_ref)) + up(b_ref)) + up(c_ref)

    grid_spec = pltpu.PrefetchScalarGridSpec(
        num_scalar_prefetch=1, grid=(nblk,),
        in_specs=[pl.BlockSpec((None, tr, C), lambda i, p_ref: (p_ref[0], i + p_ref[1] * nblk, 0)),
                  pl.BlockSpec((None, tr, C), lambda i, p_ref: (p_ref[0], i, 0))]
        + [pl.BlockSpec((None, tr, C), functools.partial(lambda i, p_ref, k: (k, i, 0), k=k)) for k in range(3)],
        out_specs=pl.BlockSpec((tr, C), lambda i, p_ref: (i, 0)))
    return pl.pallas_call(body, out_shape=jax.ShapeDtypeStruct((rh, C), F32), grid_spec=grid_spec,
                          name="rs_add_chips", compiler_params=_cparams(("arbitrary",)))(place, g, recv, r3, r3, r3)


def _remote(src, dst, ssem, rsem, dev):
    return pltpu.make_async_remote_copy(src_ref=src, dst_ref=dst, send_sem=ssem, recv_sem=rsem,
                                        device_id=dev, device_id_type=pl.DeviceIdType.MESH)


def _mesh_places():
    x, y, c = lax.axis_index("x"), lax.axis_index("y"), lax.axis_index("c")
    chips = [(1 - x, y), (x, 1 - y), (1 - x, 1 - y)]
    return x, y, c, (x, y, 1 - c), chips


def _hbm_specs(n):
    return [pl.BlockSpec(memory_space=pltpu.HBM) for _ in range(n)]


def _gather_body(ins, outs, n_split, send_sems, recv_sems, handshake):
    x, y, c, sibling, chips = _mesh_places()
    mine = 2 * x + y
    if handshake:
        barrier = pltpu.get_barrier_semaphore()
        peers = [sibling] + [(*chip, c) for chip in chips]
        for peer in peers:
            pl.semaphore_signal(barrier, inc=1, device_id=peer, device_id_type=pl.DeviceIdType.MESH)
        pl.semaphore_wait(barrier, len(peers))

    def half(a, chip_idx, which):
        rh = ins[a].shape[0] // 2
        return outs[a].at[chip_idx, pl.ds(which * rh, rh), :]

    sent = []
    for a in range(len(ins)):
        for k, chip in enumerate(chips):
            if a < n_split:
                rh = ins[a].shape[0] // 2
                src, dst = ins[a].at[pl.ds(c * rh, rh), :], half(a, mine, c)
            else:
                src, dst = ins[a], outs[a].at[mine]
            sent.append(_remote(src, dst, send_sems.at[a, k], recv_sems.at[a, k], (*chip, c)))
    for cp in sent:
        cp.start()
    for a in range(len(ins)):
        for k, chip in enumerate(chips):
            j = 2 * chip[0] + chip[1]
            region = half(a, j, c) if a < n_split else outs[a].at[j]
            _remote(region, region, send_sems.at[a, k], recv_sems.at[a, k], (*chip, c)).wait_recv()
            if a < n_split:
                fwd = _remote(region, region, send_sems.at[a, 3 + k], recv_sems.at[a, 3 + k], sibling)
                fwd.start()
                sent.append(fwd)
    for a in range(n_split):
        for k, chip in enumerate(chips):
            region = half(a, 2 * chip[0] + chip[1], 1 - c)
            _remote(region, region, send_sems.at[a, 3 + k], recv_sems.at[a, 3 + k], sibling).wait_recv()
    for cp in sent:
        cp.wait_send()


def gather_weights(shards, small):
    arrs = list(shards) + [small]
    n = len(arrs)

    def body(*refs):
        _gather_body(refs[:n], refs[n:2 * n], n - 1, refs[2 * n], refs[2 * n + 1], handshake=False)

    return pl.pallas_call(
        body, out_shape=[jax.ShapeDtypeStruct((4,) + a.shape, a.dtype) for a in arrs],
        in_specs=_hbm_specs(n), out_specs=_hbm_specs(n),
        scratch_shapes=[pltpu.SemaphoreType.DMA((n, 6)), pltpu.SemaphoreType.DMA((n, 6))],
        name="gather_weights")(*arrs)


def gather_weights_async(shards):
    n = len(shards)

    def body(*refs):
        _gather_body(refs[:n], refs[n:2 * n], n, refs[2 * n], refs[2 * n + 1], handshake=True)

    return pl.kernel(
        body, out_type=[jax.ShapeDtypeStruct((4,) + a.shape, a.dtype) for a in shards],
        mesh=plsc.ScalarSubcoreMesh(axis_name="seq", num_cores=1),
        scratch_types=[pltpu.SemaphoreType.DMA((n, 6)), pltpu.SemaphoreType.DMA((n, 6))],
        compiler_params=pltpu.CompilerParams(collective_id=1), name="gather_weights_async")(*shards)


def _sequencer_call(name, body, out_type, sem_shape, collective_id, args):
    return pl.kernel(
        body, out_type=out_type, mesh=plsc.ScalarSubcoreMesh(axis_name="seq", num_cores=1),
        scratch_types=[pltpu.SemaphoreType.DMA(sem_shape), pltpu.SemaphoreType.DMA(sem_shape)],
        compiler_params=pltpu.CompilerParams(collective_id=collective_id), name=name)(*args)


def _handshake(peers):
    barrier = pltpu.get_barrier_semaphore()
    for peer in peers:
        pl.semaphore_signal(barrier, inc=1, device_id=peer, device_id_type=pl.DeviceIdType.MESH)
    pl.semaphore_wait(barrier, len(peers))


def exchange_siblings(name, srcs, halves, collective_id):
    n = len(srcs)

    def body(*refs):
        ins, outs = refs[:n], refs[n:2 * n]
        send_sems, recv_sems = refs[2 * n:]
        x, y, c, sibling, chips = _mesh_places()
        _handshake([sibling])
        cps = []
        for a in range(n):
            src = ins[a]
            if halves:
                rh = src.shape[1] // 2
                src = src.at[:, pl.ds((1 - c) * rh, rh), :]
            cps.append(_remote(src, outs[a], send_sems.at[a], recv_sems.at[a], sibling))
        for cp in cps:
            cp.start()
        for cp in cps:
            cp.wait()

    shape = lambda g: (4, g.shape[1] // 2, g.shape[2]) if halves else g.shape
    return _sequencer_call(name, body, [jax.ShapeDtypeStruct(shape(g), g.dtype) for g in srcs], (n,), collective_id, srcs)


def exchange_chips(name, s1s, collective_id):
    n = len(s1s)

    def body(*refs):
        ins, outs = refs[:n], refs[n:2 * n]
        send_sems, recv_sems = refs[2 * n:]
        x, y, c, sibling, chips = _mesh_places()
        _handshake([(*chip, c) for chip in chips])
        cps = []
        for a in range(n):
            for k, chip in enumerate(chips):
                cps.append(_remote(ins[a].at[2 * chip[0] + chip[1]], outs[a].at[k], send_sems.at[a, k],
                                   recv_sems.at[a, k], (*chip, c)))
        for cp in cps:
            cp.start()
        for cp in cps:
            cp.wait()

    return _sequencer_call(name, body, [jax.ShapeDtypeStruct((3,) + s.shape[1:], s.dtype) for s in s1s], (n, 3),
                           collective_id, s1s)


def allgather_small(v):
    m_per = v.shape[0]

    def body(x_ref, out_ref, send_sems, recv_sems, local_sem):
        x, y, c, sibling, chips = _mesh_places()
        me = (x, y, c)

        def rows(px, py, pc):
            return out_ref.at[pl.ds((4 * px + 2 * py + pc) * m_per, m_per), :]

        def copy(k, block, to, src=None):
            return _remote(rows(*block) if src is None else src, rows(*block), send_sems.at[k], recv_sems.at[k], to)

        mine = pltpu.make_async_copy(x_ref, rows(*me), local_sem)
        mine.start()
        first = [copy(0, me, sibling, src=x_ref)]
        first += [copy(1 + j, me, (*chip, c), src=x_ref) for j, chip in enumerate(chips)]
        for cp in first:
            cp.start()
        passed = [copy(4 + j, (*chip, c), sibling) for j, chip in enumerate(chips)]
        for j, chip in enumerate(chips):
            copy(1 + j, (*chip, c), me).wait_recv()
            passed[j].start()
        copy(0, sibling, me).wait_recv()
        for j, chip in enumerate(chips):
            copy(4 + j, (*chip, 1 - c), me).wait_recv()
        for cp in first + passed:
            cp.wait_send()
        mine.wait()

    return pl.pallas_call(
        body, out_shape=jax.ShapeDtypeStruct((8 * m_per, v.shape[1]), v.dtype),
        in_specs=[pl.BlockSpec(memory_space=pltpu.VMEM)], out_specs=pl.BlockSpec(memory_space=pltpu.VMEM),
        scratch_shapes=[pltpu.SemaphoreType.DMA((7,)), pltpu.SemaphoreType.DMA((7,)), pltpu.SemaphoreType.DMA],
        name="allgather_small")(v)


def rms_res_tile(x, g):
    return (_rms(x, g), x)


def _lower_bounds(lb_param):
    lbs = jax.nn.softmax(lb_param.astype(F32), axis=0)
    return jnp.cumsum(lbs, axis=0) - lbs[0]


def _even_fwd(x, i, W, lower, kv, slopes, T):
    O = EVEN_OFF
    g = W["norm_even"][i].reshape(1, D_MODEL)
    (h,) = rows_call("rms_fwd", rms_tile, T, [("row", x, 0, D_MODEL), ("full", g)], [D_MODEL], [BF16])
    p = matmul("mm_in_e", h, W["w_in_e"][i], "nn")
    kvp = jnp.pad(p[:, O["kA"]:O["kA"] + 2 * W_KV_A], ((BLOCK, BLOCK), (0, 0)))
    sink = jnp.repeat(W["sink"][i], BLOCK).reshape(N_Q_A * BLOCK, 1)
    a = attn_fwd(p, O["qA"], kvp, sink, slopes, T)
    scan_raws = [[((p, O["qB"]), W_B), ((p, O[z]), W_B), ((p, O["iB"]), W_B)] for z in ("zf", "zb")]
    scan_pars = [[lower[i][0:1]], [lower[i][1:2]]]
    o_f, o_b, ss_f, ss_b = scan_fwd("scan_fwd_h", hgrn_prep, scan_raws, scan_pars, N_HEADS_B, HEAD_DIM_B, HEAD_DIM_B, T)
    mo = mem_fwd(p, O["qM"], kv, T)
    hg = W["hgrn_norm"][i].reshape(1, W_B)
    post_ins = [("row", a, 0, W_A), ("row", o_f, 0, W_B), ("row", o_b, 0, W_B), ("row", mo, 0, W_M),
                ("row", p, O["gA"], W_A), ("row", p, O["gB"], W_B), ("row", p, O["gM"], W_M), ("full", hg)]
    (mix,) = rows_call("even_post_fwd", even_post_tile, T, post_ins, [MIX], [BF16])
    x_new = matmul("mm_out", mix, W["w_out_e"][i], "nn", add=x)
    return x_new, dict(x=x, g=g, h=h, p=p, kvp=kvp, sink=sink, scan_raws=scan_raws, scan_pars=scan_pars,
                       ss=(ss_f, ss_b), post_ins=post_ins, mix=mix)


def _add2(a, b):
    return a.astype(F32) + b.astype(F32)


def _assemble_even(dqA, dgA, dqB_f, dqB_b, dzf, dzb, diB_f, diB_b, dgB, dqM, dgM, dkvA):
    parts = [dqA, dgA, _add2(dqB_f, dqB_b), dzf, dzb, _add2(diB_f, diB_b), dgB, dqM, dgM, dkvA]
    return (jnp.concatenate([t.astype(BF16) for t in parts], axis=-1),)


def _even_bwd(dxo, sv, i, W, kv, slopes, T, sync):
    O = EVEN_OFF
    p = sv["p"]
    dmix = matmul("mm_dmix", dxo, W["w_out_e"][i], "nt")
    dwo = matmul("mm_dwo", sv["mix"], dxo, "tn")
    da, dof, dmo, dgA, dgB, dgM, dhg = rows_vjp_call("even_post_bwd", even_post_tile, T, sv["post_ins"],
                                                      [[("row", dmix, 0, MIX)]], skip=(2,), narrow=(4, 5, 6))
    dqA, dkvp, dsink = attn_bwd(p, O["qA"], sv["kvp"], sv["sink"], slopes, da, T)
    dkvA = dkvp[BLOCK:-BLOCK]
    dqB_f, dzf, diB_f, dqB_b, dzb, diB_b, dlow_f, dlow_b = scan_bwd(
        "scan_bwd_h", hgrn_prep, sv["scan_raws"], sv["scan_pars"], sv["ss"], (dof, 0), N_HEADS_B, HEAD_DIM_B, HEAD_DIM_B, T)
    dqB_f = sync(dqB_f)
    row = lambda arr, w: ("row", arr, 0, w)
    dlow = jnp.concatenate([dlow_f, dlow_b], axis=0)
    dqM, dkv = mem_bwd(p, O["qM"], kv, dmo, T)
    (dp,) = rows_call("even_dp", _assemble_even, T,
                      [row(dqA, W_A), row(dgA, W_A), row(dqB_f, W_B), row(dqB_b, W_B), row(dzf, W_B), row(dzb, W_B),
                       row(diB_f, W_B), row(diB_b, W_B), row(dgB, W_B), row(dqM, W_M), row(dgM, W_M),
                       row(dkvA, 2 * W_KV_A)],
                      [EVEN_IN], [BF16])
    dh = matmul("mm_dh_e", dp, W["w_in_e"][i], "nt")
    dwi = matmul("mm_dwi_e", sv["h"], dp, "tn")
    dx, dg = rows_vjp_call("rms_res_bwd", rms_res_tile, T, [("row", sv["x"], 0, D_MODEL), ("full", sv["g"])],
                           [[("row", dh, 0, D_MODEL)], [("row", dxo, 0, D_MODEL)]])
    return dx, dict(w_in=dwi, w_out=dwo, norm=dg[0], sink=dsink.reshape(N_Q_A), low=dlow, hg=dhg[0], kv=dkv)


def _pad_gate_up(w_up):
    z = jnp.zeros((2, 128, WK_C), F32)
    z = z.at[0, 0:GATE_RANK].set(w_up[0])
    return z.at[1, GATE_RANK:2 * GATE_RANK].set(w_up[1])


def _odd_fwd(x, i, W, kv, T):
    O = ODD_OFF
    g = W["norm_odd"][i].reshape(1, D_MODEL)
    (h,) = rows_call("rms_fwd", rms_tile, T, [("row", x, 0, D_MODEL), ("full", g)], [D_MODEL], [BF16])
    p = matmul("mm_in_o", h, W["w_in_o"][i], "nn")
    wup = _pad_gate_up(W["w_gate_up"][i])
    one_dir = [((p, O["qC"]), WK_C), ((p, O["kC"]), WK_C), ((p, O["vC"]), WV_C), ((p, O["rr"]), 128)]
    scan_raws = [one_dir, one_dir]
    scan_pars = [[wup[d], W["b_gate"][i][d:d + 1]] for d in range(2)]
    o_f, o_b, ss_f, ss_b = scan_fwd("scan_fwd_g", gla_prep, scan_raws, scan_pars, N_HEADS_C, DK_C, DV_C, T)
    mo = mem_fwd(p, O["qM"], kv, T)
    gg = W["gla_norm"][i].reshape(1, WV_C)
    post_ins = [("row", o_f, 0, WV_C), ("row", o_b, 0, WV_C), ("row", mo, 0, W_M),
                ("row", p, O["gC"], WV_C), ("row", p, O["gM"], W_M), ("full", gg)]
    (mix,) = rows_call("odd_post_fwd", odd_post_tile, T, post_ins, [MIX], [BF16])
    x_new = matmul("mm_out", mix, W["w_out_o"][i], "nn", add=x)
    return x_new, dict(x=x, g=g, h=h, p=p, scan_raws=scan_raws, scan_pars=scan_pars, ss=(ss_f, ss_b),
                       post_ins=post_ins, mix=mix)


def _assemble_odd(dq0, dq1, dk0, dk1, dv0, dv1, dgC, dqM, dgM, dr0, dr1):
    parts = [_add2(dq0, dq1), _add2(dk0, dk1), _add2(dv0, dv1), dgC, dqM, dgM, _add2(dr0, dr1)]
    return (jnp.concatenate([t.astype(BF16) for t in parts], axis=-1),)


def _odd_bwd(dxo, sv, i, W, kv, T, sync):
    O = ODD_OFF
    p = sv["p"]
    dmix = matmul("mm_dmix", dxo, W["w_out_o"][i], "nt")
    dwo = matmul("mm_dwo", sv["mix"], dxo, "tn")
    dof, dmo, dgC, dgM, dgg = rows_vjp_call("odd_post_bwd", odd_post_tile, T, sv["post_ins"],
                                            [[("row", dmix, 0, MIX)]], skip=(1,), narrow=(3, 4))
    dqf, dkf, dvf, dr_f, dqb, dkb, dvb, dr_b, dwup_f, dbg_f, dwup_b, dbg_b = scan_bwd(
        "scan_bwd_g", gla_prep, sv["scan_raws"], sv["scan_pars"], sv["ss"], (dof, 0), N_HEADS_C, DK_C, DV_C, T)
    dqf = sync(dqf)
    row = lambda arr, w: ("row", arr, 0, w)
    dqM, dkv = mem_bwd(p, O["qM"], kv, dmo, T)
    (dp,) = rows_call("odd_dp", _assemble_odd, T,
                      [row(dqf, WK_C), row(dqb, WK_C), row(dkf, WK_C), row(dkb, WK_C), row(dvf, WV_C), row(dvb, WV_C),
                       row(dgC, WV_C), row(dqM, W_M), row(dgM, W_M), row(dr_f, 128), row(dr_b, 128)],
                      [ODD_PAD], [BF16])
    dh = matmul("mm_dh_o", dp, W["w_in_o"][i], "nt")
    dwi = matmul("mm_dwi_o", sv["h"], dp, "tn")
    dx, dg = rows_vjp_call("rms_res_bwd", rms_res_tile, T, [("row", sv["x"], 0, D_MODEL), ("full", sv["g"])],
                           [[("row", dh, 0, D_MODEL)], [("row", dxo, 0, D_MODEL)]])
    dw_up = jnp.stack([dwup_f[0:GATE_RANK], dwup_b[GATE_RANK:2 * GATE_RANK]])
    dbg = jnp.concatenate([dbg_f, dbg_b], axis=0)
    return dx, dict(w_in=dwi, w_out=dwo, norm=dg[0], w_up=dw_up, b_gate=dbg, gg=dgg[0], kv=dkv)


def local_step(x, mem, target, W, later=None, on_layer_grads=None, sync=lambda a: a):
    T = x.shape[0]
    slopes = jnp.repeat(2.0 ** (-8.0 * jnp.arange(1, N_Q_A + 1, dtype=F32) / N_Q_A), BLOCK).reshape(N_Q_A * BLOCK, 1)
    lower, lower_vjp = jax.vjp(_lower_bounds, W["lb_param"])
    mem_g = W["mem_norm"].reshape(1, D_MODEL)
    (mem_n,) = rows_call("mem_rms_fwd", rms_tile, N_MEM, [("row", mem, 0, D_MODEL), ("full", mem_g)], [D_MODEL], [BF16])
    kvs, saved = [], []
    for l in range(DEPTH):
        if l == 1 and later is not None:
            x, W = later(x, W)
        kvs.append(matmul("mm_kv", mem_n, W["w_kv"][l], "nn"))
        if l % 2 == 0:
            x, sv = _even_fwd(x, l // 2, W, lower, kvs[l], slopes, T)
        else:
            x, sv = _odd_fwd(x, l // 2, W, kvs[l], T)
        saved.append(sv)
    loss, dx, dgf = final_call(x, W["final_norm"].reshape(1, D_MODEL), target, T)
    per = [None] * DEPTH
    dmem_n = None
    for l in reversed(range(DEPTH)):
        if l % 2 == 0:
            dx, per[l] = _even_bwd(dx, saved[l], l // 2, W, kvs[l], slopes, T, sync)
        else:
            dx, per[l] = _odd_bwd(dx, saved[l], l // 2, W, kvs[l], T, sync)
        per[l]["w_kv"] = matmul("mm_dwkv", mem_n, per[l]["kv"], "tn")
        dmem_n = matmul("mm_dmem", per[l]["kv"], W["w_kv"][l], "nt", add=dmem_n)
        if on_layer_grads is not None:
            dx = on_layer_grads(l, dx, per[l])
    dw_kv = [per[l]["w_kv"] for l in range(DEPTH)]
    (dmem_norm,) = rows_vjp_call("mem_rms_bwd", rms_tile, N_MEM, [("row", mem, 0, D_MODEL), ("full", mem_g)],
                                 [[("row", dmem_n, 0, D_MODEL)]], skip=(0,))
    ev, od = (per[0], per[2]), (per[1], per[3])
    (d_lb,) = lower_vjp(jnp.stack([e["low"] for e in ev]))
    grads = dict(
        w_in_e=jnp.stack([e["w_in"] for e in ev]), w_in_o=jnp.stack([o["w_in"] for o in od]),
        w_out_e=jnp.stack([e["w_out"] for e in ev]), w_out_o=jnp.stack([o["w_out"] for o in od]),
        w_kv=jnp.stack(dw_kv), norm_even=jnp.stack([e["norm"] for e in ev]), sink=jnp.stack([e["sink"] for e in ev]),
        lb_param=d_lb, hgrn_norm=jnp.stack([e["hg"] for e in ev]), norm_odd=jnp.stack([o["norm"] for o in od]),
        w_gate_up=jnp.stack([o["w_up"] for o in od]), b_gate=jnp.stack([o["b_gate"] for o in od]),
        gla_norm=jnp.stack([o["gg"] for o in od]), mem_norm=dmem_norm[0], final_norm=dgf[0])
    return loss, dx, grads


SMALL_SPECS = (("lb_param", (2, 2, 128)), ("norm_odd", (2, 256)), ("w_gate_up", (2, 2, 16, 128)),
               ("b_gate", (2, 2, 128)), ("gla_norm", (2, 256)))
SMALL_ROWS = 80


def _pack_small_local(d):
    return jnp.concatenate([d[n].reshape(-1) for n, _ in SMALL_SPECS]).reshape(SMALL_ROWS, 128)


def _unpack_small_local(b):
    flat, out, o = b.reshape(-1), {}, 0
    for n, shp in SMALL_SPECS:
        sz = int(np.prod(shp))
        out[n] = flat[o:o + sz].reshape(shp)
        o += sz
    return out


def _unpack_small_full(g4):
    per = [_unpack_small_local(g4[j]) for j in range(4)]
    return {n: jnp.concatenate([per[j][n] for j in range(4)], axis=-1) for n, _ in SMALL_SPECS}


def _pack_small_blocks(full):
    blocks = []
    for j in range(4):
        blocks.append(_pack_small_local({n: full[n][..., j * shp[-1]:(j + 1) * shp[-1]] for n, shp in SMALL_SPECS}))
    return jnp.stack(blocks)


def _cols(t, order, off, widths):
    return [t[..., off[n]:off[n] + widths[n]] for n in order]


EVEN_REF_ORDER = ("qA", "kA", "vA", "gA", "qB", "zf", "zb", "iB", "gB", "qM", "gM")
ODD_REF_ORDER = ("qC", "kC", "vC", "gC", "rr", "qM", "gM")


def _layer_weights(l, g_in, g_out, g_kv):
    t = g_in.transpose(1, 0, 2).reshape(D_MODEL, -1)
    if l % 2 == 0:
        w_in = jnp.concatenate(_cols(t, EVEN_ORDER, EVEN_REF_OFF, EVEN_W), axis=-1)
    else:
        w_in = jnp.concatenate(_cols(t, ODD_ORDER, ODD_REF_OFF, ODD_W) + [jnp.zeros((D_MODEL, ODD_PAD - ODD_IN), BF16)],
                               axis=-1)
    return w_in, g_out.reshape(MIX, D_MODEL), g_kv.reshape(D_MODEL, 2 * W_M)


def _layer_grad_blocks(l, gl):
    if l % 2 == 0:
        t = jnp.concatenate(_cols(gl["w_in"], EVEN_REF_ORDER, EVEN_OFF, EVEN_W), axis=-1)
        b_in = t.reshape(D_MODEL, 4, -1).transpose(1, 2, 0)
    else:
        t = jnp.concatenate(_cols(gl["w_in"], ODD_REF_ORDER, ODD_OFF, ODD_W), axis=-1)
        b_in = t.reshape(D_MODEL, 4, -1).transpose(1, 0, 2)
    return [b_in, gl["w_out"].reshape(4, MIX // 4, D_MODEL), gl["w_kv"].reshape(4, D_MODEL // 4, 2 * W_M)]


WEIGHT_NAMES = ("norm_even", "w_in_even", "sink", "lb_param", "hgrn_norm", "w_out_even", "norm_odd", "w_in_odd",
                "w_gate_up", "b_gate", "gla_norm", "w_out_odd", "mem_norm", "w_mem_kv", "final_norm")


def kernel(x, mem, norm_even, w_in_even, sink, lb_param, hgrn_norm, w_out_even, norm_odd, w_in_odd, w_gate_up, b_gate, gla_norm, w_out_odd, mem_norm, w_mem_kv, final_norm, loss_target, m_norm_even, m_w_in_even, m_sink, m_lb_param, m_hgrn_norm, m_w_out_even, m_norm_odd, m_w_in_odd, m_w_gate_up, m_b_gate, m_gla_norm, m_w_out_odd, m_mem_norm, m_w_mem_kv, m_final_norm, v_norm_even, v_w_in_even, v_sink, v_lb_param, v_hgrn_norm, v_w_out_even, v_norm_odd, v_w_in_odd, v_w_gate_up, v_b_gate, v_gla_norm, v_w_out_odd, v_mem_norm, v_w_mem_kv, v_final_norm):
    w = dict(zip(WEIGHT_NAMES, (norm_even, w_in_even, sink, lb_param, hgrn_norm, w_out_even, norm_odd, w_in_odd,
                                w_gate_up, b_gate, gla_norm, w_out_odd, mem_norm, w_mem_kv, final_norm)))
    m = dict(zip(WEIGHT_NAMES, (m_norm_even, m_w_in_even, m_sink, m_lb_param, m_hgrn_norm, m_w_out_even, m_norm_odd,
                                m_w_in_odd, m_w_gate_up, m_b_gate, m_gla_norm, m_w_out_odd, m_mem_norm, m_w_mem_kv,
                                m_final_norm)))
    v = dict(zip(WEIGHT_NAMES, (v_norm_even, v_w_in_even, v_sink, v_lb_param, v_hgrn_norm, v_w_out_even, v_norm_odd,
                                v_w_in_odd, v_w_gate_up, v_b_gate, v_gla_norm, v_w_out_odd, v_mem_norm, v_w_mem_kv,
                                v_final_norm)))
    ci = lax.axis_index("c").astype(jnp.int32).reshape(1)
    chip = (2 * lax.axis_index("x") + lax.axis_index("y")).astype(jnp.int32).reshape(1)

    shards = []
    for l in range(DEPTH):
        names = ("w_in_even", "w_out_even") if l % 2 == 0 else ("w_in_odd", "w_out_odd")
        shards.append([w[names[0]][l // 2].astype(BF16), w[names[1]][l // 2].astype(BF16), w_mem_kv[l].astype(BF16)])
    small = _pack_small_local(w)
    own = lambda g, s: lax.dynamic_update_slice(g, s[None], (chip[0], 0, 0))
    first = [own(g, s) for g, s in zip(gather_weights(shards[0], small), shards[0] + [small])]
    later_shards = shards[1] + shards[2] + shards[3]
    later_raw = gather_weights_async(later_shards)
    w0 = _layer_weights(0, *first[0:3])
    W = dict(w_in_e=[w0[0]], w_out_e=[w0[1]], w_kv=[w0[2]])
    W.update(_unpack_small_full(first[3]))
    W.update({n: w[n] for n in ("norm_even", "sink", "hgrn_norm", "mem_norm", "final_norm")})

    def later(x1, W):
        x1, raw = lax.optimization_barrier((x1, list(later_raw)))
        g = [own(a, s) for a, s in zip(raw, later_shards)]
        w1, w2, w3 = (_layer_weights(l, *g[3 * (l - 1):3 * l]) for l in (1, 2, 3))
        W = dict(W)
        W.update(w_in_e=[w0[0], w2[0]], w_in_o=[w1[0], w3[0]], w_out_e=[w0[1], w2[1]], w_out_o=[w1[1], w3[1]],
                 w_kv=[w0[2], w1[2], w2[2], w3[2]])
        return x1, W

    place = jnp.concatenate([chip, ci])

    def start(tag, blocks, wire):
        return dict(tag=tag, blocks=blocks, wire=wire, step=0,
                    recv=exchange_siblings(f"rs_siblings_{tag}", blocks, True, 2))

    def advance(p):
        if p["step"] == 0:
            sums = [add_sibling(g, r, ci, dt) for g, r, dt in zip(p["blocks"], p["recv"], p["wire"])]
            p["recv3"] = exchange_chips(f"rs_chips_{p['tag']}", sums, 3)
        else:
            p["mine"] = [add_chips(g, r, r3, place) for g, r, r3 in zip(p["blocks"], p["recv"], p["recv3"])]
            p["other"] = exchange_siblings(f"rs_final_{p['tag']}", p["mine"], False, 4)
        p["step"] += 1

    pipes, first_layer = [], {}

    def sync(a):
        for p in pipes:
            if p["step"] < 3:
                key = ("recv", "recv3", "other")[p["step"]]
                a, arrived = lax.optimization_barrier((a, list(p[key])))
                p[key] = arrived
                if p["step"] < 2:
                    advance(p)
                else:
                    p["step"] = 3
        return a

    def on_layer_grads(l, dx, gl):
        dx = sync(dx)
        if l == 0:
            first_layer.update(gl)
        else:
            pipes.append(start(f"l{l}", _layer_grad_blocks(l, gl), [BF16] * 3))
        return dx

    loss_tile, dx, grads = local_step(x[0], mem[0], loss_target[0], W, later, on_layer_grads, sync)
    last = start("l0", _layer_grad_blocks(0, first_layer) + [_pack_small_blocks(grads)], [BF16] * 3 + [F32])
    for p in pipes + [last]:
        while p["step"] < (1 if p is last else 2):
            advance(p)
    by_layer = {int(p["tag"][1:]): p for p in pipes + [last]}
    halves = lambda layers, k: (jnp.stack([by_layer[l]["mine"][k] for l in layers]),
                                jnp.stack([by_layer[l]["other"][k] for l in layers]))
    gl, upd = {}, {}

    pack = jnp.zeros((8, D_MODEL), F32)
    pack = pack.at[0:2].set(grads["norm_even"]).at[2].set(grads["hgrn_norm"].reshape(-1))
    pack = pack.at[3].set(grads["mem_norm"]).at[4].set(grads["final_norm"])
    pack = pack.at[5, 0:16].set(grads["sink"].reshape(-1)).at[5, 16].set(loss_tile[0, 0])
    tot = sum_devices(allgather_small(pack))
    gl.update(norm_even=tot[0:2], hgrn_norm=tot[2].reshape(2, W_B), mem_norm=tot[3], final_norm=tot[4],
              sink=tot[5, 0:16].reshape(2, N_Q_A))
    loss = tot[5, 16]
    for n in ("norm_even", "hgrn_norm", "mem_norm", "final_norm", "sink"):
        upd[n] = adamw_call(w[n], gl[n], m[n], v[n])
    for n, k in (("w_in_odd", 0), ("w_out_odd", 1)):
        gl[n], *upd[n] = adamw_halves(w[n], *halves((1, 3), k), m[n], v[n], ci)
    early = [upd[n] for n in sorted(upd)] + [gl["w_in_odd"], gl["w_out_odd"]]
    last["recv3"], early = lax.optimization_barrier((list(last["recv3"]), early))
    for n, res in zip(sorted(upd), early):
        upd[n] = res
    gl["w_in_odd"], gl["w_out_odd"] = early[-2:]
    advance(last)

    big = dict(w_in_even=halves((0, 2), 0), w_out_even=halves((0, 2), 1), w_mem_kv=halves((0, 1, 2, 3), 2))
    s_mine, s_other = last["mine"][3], last["other"][3]
    g_small = jnp.where(ci[0] == 0, jnp.concatenate([s_mine, s_other]), jnp.concatenate([s_other, s_mine]))
    gl.update(_unpack_small_local(g_small))
    for n in WEIGHT_NAMES:
        if n == "w_in_even":
            tr_ = lambda a: jnp.swapaxes(a, 1, 2)
            gl[n], *upd[n] = [tr_(o) for o in adamw_halves(tr_(w[n]), *big[n], tr_(m[n]), tr_(v[n]), ci)]
        elif n in big:
            gl[n], *upd[n] = adamw_halves(w[n], *big[n], m[n], v[n], ci)
        elif n not in upd:
            upd[n] = adamw_call(w[n], gl[n], m[n], v[n])
    return (loss, dx[None], *[gl[n] for n in WEIGHT_NAMES], *[upd[n][0] for n in WEIGHT_NAMES],
            *[upd[n][1] for n in WEIGHT_NAMES], *[upd[n][2] for n in WEIGHT_NAMES])
```

```python
import functools

import numpy as np
import jax
import jax.numpy as jnp
from jax import lax
from jax.experimental import pallas as pl
from jax.experimental.pallas import tpu as pltpu
from jax.experimental.pallas import tpu_sc as plsc

F32 = jnp.float32
BF16 = jnp.bfloat16

D_MODEL = 1024
DEPTH = 4
N_Q_A, N_KV_A, HEAD_DIM_A = 8, 2, 64
W_A, W_KV_A = 512, 128
WINDOW = 128
BLOCK = 128
N_HEADS_B, HEAD_DIM_B, W_B = 4, 128, 512
N_HEADS_C, DK_C, DV_C, WK_C, WV_C = 4, 128, 256, 512, 1024
GATE_RANK = 16
GATE_TEMP = 16.0
N_MEM, N_HEADS_M, HEAD_DIM_M, W_M = 256, 4, 128, 512
EPS = 1e-6
MASK_VALUE = -1e30
MIN_GATE = 1e-30
EVEN_IN, ODD_IN = 4864, 4128
ODD_PAD = 4224
MIX = 1536
ADAM_LR, ADAM_B1, ADAM_B2, ADAM_EPS, ADAM_WD, ADAM_STEP = 0.001, 0.9, 0.999, 1e-08, 0.01, 10

SCAN_CHUNK = 128
SCAN_LEVELS = 7
VMEM_LIMIT = 56 * 1024 * 1024

EVEN_REF_OFF = dict(qA=0, kA=512, vA=640, gA=768, qB=1280, zf=1792, zb=2304, iB=2816, gB=3328, qM=3840, gM=4352)
EVEN_W = dict(qA=512, kA=128, vA=128, gA=512, qB=512, zf=512, zb=512, iB=512, gB=512, qM=512, gM=512)
EVEN_ORDER = ("qA", "gA", "qB", "zf", "zb", "iB", "gB", "qM", "gM", "kA", "vA")
ODD_REF_OFF = dict(qC=0, kC=512, vC=1024, gC=2048, rr=3072, qM=3104, gM=3616)
ODD_W = dict(qC=512, kC=512, vC=1024, gC=1024, rr=32, qM=512, gM=512)
ODD_ORDER = ("qC", "kC", "vC", "gC", "qM", "gM", "rr")


def _offsets(order, widths):
    off, o = {}, 0
    for n in order:
        off[n] = o
        o += widths[n]
    return off


EVEN_OFF = _offsets(EVEN_ORDER, EVEN_W)
ODD_OFF = _offsets(ODD_ORDER, ODD_W)


def _dg(a, b, ca, cb):
    return lax.dot_general(a.astype(BF16), b.astype(BF16), (((ca,), (cb,)), ((), ())),
                           preferred_element_type=F32)


def dot_nn(a, b):
    return _dg(a, b, 1, 0)


def dot_nt(a, b):
    return _dg(a, b, 1, 1)


def dot_tn(a, b):
    return _dg(a, b, 0, 0)


@jax.custom_vjp
def bdot(a, b):
    return dot_nn(a, b)


bdot.defvjp(lambda a, b: (dot_nn(a, b), (a, b)),
            lambda r, g: (dot_nt(g, r[1]), dot_tn(r[0], g)))


@jax.custom_vjp
def bdot_t(a, b):
    return dot_nt(a, b)


bdot_t.defvjp(lambda a, b: (dot_nt(a, b), (a, b)),
              lambda r, g: (dot_nn(g, r[1]), dot_tn(g, r[0])))


@jax.custom_vjp
def bdot_tn(a, b):
    return dot_tn(a, b)


bdot_tn.defvjp(lambda a, b: (dot_tn(a, b), (a, b)),
               lambda r, g: (dot_nt(r[1], g), dot_nn(r[0], g)))


def _split_mm(h, x):
    hi = x.astype(BF16)
    lo = (x - hi.astype(F32)).astype(BF16)
    return (lax.dot_general(h, hi, (((1,), (0,)), ((), ())), preferred_element_type=F32)
            + lax.dot_general(h, lo, (((1,), (0,)), ((), ())), preferred_element_type=F32))


def _sigmoid(z):
    return 1.0 / (1.0 + jnp.exp(-z))


def _silu(z):
    return z * _sigmoid(z)


def _log_sigmoid(z):
    return jnp.minimum(z, 0.0) - jnp.log(1.0 + jnp.exp(-jnp.abs(z)))


def _rms(x, g):
    return x * lax.rsqrt(jnp.mean(x * x, axis=-1, keepdims=True) + EPS) * g


def rms_tile(x, g):
    return (_rms(x, g),)


@functools.partial(jax.custom_vjp, nondiff_argnums=(1, 2))
def split(x, n, axis):
    w = x.shape[axis] // n
    return tuple(lax.slice_in_dim(x, h * w, (h + 1) * w, axis=axis) for h in range(n))


split.defvjp(lambda x, n, axis: (split(x, n, axis), None),
             lambda n, axis, _, cts: (jnp.concatenate(cts, axis=axis),))


def _group_rms(o, g, heads):
    return jnp.concatenate([_rms(oh, gh) for oh, gh in zip(split(o, heads, 1), split(g, heads, 1))], axis=-1)


def even_post_tile(a, o2f, o2b, mo, gA, gB, gM, hg):
    y = _group_rms(o2f + o2b, hg, N_HEADS_B)
    return (jnp.concatenate([a * _silu(gA), y * _silu(gB), mo * _silu(gM)], axis=-1),)


def odd_post_tile(o2f, o2b, mo, gC, gM, gg):
    y = _group_rms(o2f + o2b, gg, N_HEADS_C)
    return (jnp.concatenate([y * _silu(gC), mo * _silu(gM)], axis=-1),)


def hgrn_prep(raw, par):
    qB, z, iB = raw
    (lb,) = par
    f = lb + (1.0 - lb) * _sigmoid(z)
    return _silu(qB), (1.0 - lb) * _sigmoid(-z), iB, jnp.log(jnp.maximum(f, MIN_GATE))


def gla_prep(raw, par):
    qC, kC, vC, r128 = raw
    wup, bg = par
    return qC * (DK_C ** -0.5), kC, vC, _log_sigmoid(bdot(r128, wup) + bg) / GATE_TEMP


def mem_tile(q, k, v):
    s = bdot_t(q, k) * (HEAD_DIM_M ** -0.5)
    m = lax.stop_gradient(jnp.max(s, axis=-1, keepdims=True))
    p = jnp.exp(s - m)
    p = p / jnp.sum(p, axis=-1, keepdims=True)
    return (bdot(p, v),)


ATTN_GROUP = N_Q_A // N_KV_A


def attn_block(q, ks, vs, sink, slope, c, seq):
    rows = ATTN_GROUP * BLOCK
    i = lax.broadcasted_iota(jnp.int32, (rows, 3 * BLOCK), 0) % BLOCK
    j = lax.broadcasted_iota(jnp.int32, (rows, 3 * BLOCK), 1)
    dist = jnp.abs(i - j + BLOCK).astype(F32)
    kpos = (c - 1) * BLOCK + j
    valid = (dist <= WINDOW) & (kpos >= 0) & (kpos < seq)
    s = bdot_t(q, ks) * (HEAD_DIM_A ** -0.5)
    s = jnp.where(valid, s - slope * dist, MASK_VALUE)
    m = lax.stop_gradient(jnp.maximum(jnp.max(s, axis=-1, keepdims=True), sink))
    p = jnp.where(valid, jnp.exp(s - m), 0.0)
    denom = jnp.sum(p, axis=-1, keepdims=True) + jnp.exp(sink - m)
    return bdot(p, vs) / denom


def scan_chunk(q, k, v, e, tot, st, qm, pm):
    C = SCAN_CHUNK
    e = split(e, 2 + SCAN_LEVELS, 0)
    qe = q * jnp.exp(e[0])
    kd = k * jnp.exp(e[1])
    r = lax.broadcasted_iota(jnp.int32, (C, C), 0)
    s = lax.broadcasted_iota(jnp.int32, (C, C), 1)
    a = jnp.where(r == s, jnp.sum(q * k, axis=-1, keepdims=True), 0.0)
    for l in range(SCAN_LEVELS):
        u = jnp.where(qm[l * C:(l + 1) * C] != 0.0, q, k) * jnp.exp(e[2 + l])
        a = a + bdot_t(u, u) * pm[l * C:(l + 1) * C]
    o = bdot_t(qe, st) + bdot(a, v)
    st_new = st * jnp.exp(tot) + bdot_tn(v, kd)
    return o, st_new


def _scan_consts():
    C, L = SCAN_CHUNK, SCAN_LEVELS
    t = np.arange(C)[:, None]
    r = np.arange(C)[None, :]
    blocks = [(r <= t), (r > t)]
    qms, pms = [], []
    for l in range(1, L + 1):
        m = C >> l
        upper_t = (t % (2 * m)) >= m
        upper_r = (r % (2 * m)) >= m
        same_half = (t // m) == (r // m)
        blocks.append(same_half & np.where(upper_t, r <= t, r > t))
        qms.append(np.broadcast_to(upper_t, (C, C)))
        pms.append(((t // (2 * m)) == (r // (2 * m))) & upper_t & ~upper_r)
    hf = np.concatenate(blocks, axis=0).astype(np.float32)
    flip = lambda mat: mat.reshape(-1, C, C)[:, ::-1, ::-1].reshape(-1, C)
    qmf = np.concatenate(qms, axis=0).astype(np.float32)
    pmf = np.concatenate(pms, axis=0).astype(np.float32)
    h = np.stack([hf, flip(hf)])
    ht = np.stack([h[0].T, h[1].T])
    qm = np.stack([qmf, 1.0 - qmf])
    pm = np.stack([pmf, flip(pmf)])
    return h, ht, qm, pm


def _cparams(sem):
    return pltpu.CompilerParams(dimension_semantics=sem, vmem_limit_bytes=VMEM_LIMIT)


def _row_tile(T):
    return min(T, 512)


def _in_spec(spec, tr):
    kind = spec[0]
    if kind == "row":
        _, arr, off, w = spec
        assert off % w == 0
        return arr, pl.BlockSpec((tr, w), functools.partial(lambda i, b: (i, b), b=off // w))
    if kind == "row3":
        _, arr, d, off, w = spec
        assert off % w == 0
        return arr, pl.BlockSpec((None, tr, w), functools.partial(lambda i, d, b: (d, i, b), d=d, b=off // w))
    _, arr = spec
    return arr, pl.BlockSpec(arr.shape, functools.partial(lambda i, n: (0,) * n, n=arr.ndim))


def rows_call(name, tile_fn, T, ins, out_widths, out_dtypes=None, stacks=None):
    tr = _row_tile(T)
    n_in = len(ins)
    out_dtypes = out_dtypes or [F32] * len(out_widths)
    stacks = stacks or [(k,) for k in range(len(out_widths))]

    def body(*refs):
        vals = [r[...] for r in refs[:n_in]]
        outs = tile_fn(*vals)
        for r, members in zip(refs[n_in:], stacks):
            if len(members) == 1:
                r[...] = outs[members[0]].astype(r.dtype)
            else:
                for d, k in enumerate(members):
                    r[d] = outs[k].astype(r.dtype)

    in_specs, args = [], []
    for spec in ins:
        arr, bs = _in_spec(spec, tr)
        args.append(arr)
        in_specs.append(bs)
    out_specs, out_shape = [], []
    for w, dt, members in zip(out_widths, out_dtypes, stacks):
        n = len(members)
        if n == 1:
            out_specs.append(pl.BlockSpec((tr, w), lambda i: (i, 0)))
            out_shape.append(jax.ShapeDtypeStruct((T, w), dt))
        else:
            out_specs.append(pl.BlockSpec((n, tr, w), lambda i: (0, i, 0)))
            out_shape.append(jax.ShapeDtypeStruct((n, T, w), dt))
    return pl.pallas_call(body, out_shape=out_shape, grid=(T // tr,), in_specs=in_specs, out_specs=out_specs,
                          name=name, compiler_params=_cparams(("arbitrary",)))(*args)


def rows_vjp_call(name, tile_fn, T, ins, cts, skip=(), narrow=()):
    tr = _row_tile(T)
    n_in = len(ins)
    n_ct = [len(c) for c in cts]
    want = [k for k in range(n_in) if k not in skip]

    def body(*refs):
        i = pl.program_id(0)
        vals = [r[...] for r in refs[:n_in]]
        ct, pos = [], n_in
        for n in n_ct:
            acc = refs[pos][...]
            for r in refs[pos + 1:pos + n]:
                acc = acc + r[...]
            ct.append(acc)
            pos += n
        _, vjp = jax.vjp(tile_fn, *vals)
        grads = vjp(tuple(ct))
        for r, k in zip(refs[pos:], want):
            if ins[k][0] == "full":
                @pl.when(i == 0)
                def _():
                    r[...] = jnp.zeros_like(r)
                r[...] += grads[k]
            else:
                r[...] = grads[k].astype(r.dtype)

    in_specs, args = [], []
    for spec in list(ins) + [s for c in cts for s in c]:
        arr, bs = _in_spec(spec, tr)
        args.append(arr)
        in_specs.append(bs)
    out_specs, out_shape = [], []
    for k in want:
        if ins[k][0] == "full":
            arr = ins[k][1]
            out_specs.append(pl.BlockSpec(arr.shape, functools.partial(lambda i, n: (0,) * n, n=arr.ndim)))
            out_shape.append(jax.ShapeDtypeStruct(arr.shape, F32))
        else:
            w = ins[k][-1]
            out_specs.append(pl.BlockSpec((tr, w), lambda i: (i, 0)))
            out_shape.append(jax.ShapeDtypeStruct((T, w), BF16 if k in narrow else F32))
    return pl.pallas_call(body, out_shape=out_shape, grid=(T // tr,), in_specs=in_specs, out_specs=out_specs,
                          name=name, compiler_params=_cparams(("arbitrary",)))(*args)


def matmul(name, a, b, mode, add=None, out_dtype=F32):
    if mode == "tn":
        K, M = a.shape
        N = b.shape[1]
        tm = M if M <= 1536 else 512
        tn = N if N <= 1280 else (N // 2 if (N // 2) % 128 == 0 else N)
        tk = min(K, 512)
        grid = (M // tm, N // tn, K // tk)

        def body(a_ref, b_ref, o_ref):
            @pl.when(pl.program_id(2) == 0)
            def _():
                o_ref[...] = jnp.zeros_like(o_ref)
            o_ref[...] += dot_tn(a_ref[...], b_ref[...])

        return pl.pallas_call(
            body, out_shape=jax.ShapeDtypeStruct((M, N), F32), grid=grid,
            in_specs=[pl.BlockSpec((tk, tm), lambda i, j, k: (k, i)), pl.BlockSpec((tk, tn), lambda i, j, k: (k, j))],
            out_specs=pl.BlockSpec((tm, tn), lambda i, j, k: (i, j)), name=name,
            compiler_params=_cparams(("arbitrary", "arbitrary", "arbitrary")))(a, b)

    M, K = a.shape
    N = b.shape[1] if mode == "nn" else b.shape[0]
    tm = min(M, 512)
    tn = N if N <= 1536 else (N // 2 if (N // 2) % 128 == 0 else (N // 3 if (N // 3) % 128 == 0 else N))
    grid = (N // tn, M // tm)
    n_in = 2 + (add is not None)

    def body(*refs):
        a_ref, b_ref = refs[0], refs[1]
        o_ref = refs[n_in]
        acc = dot_nn(a_ref[...], b_ref[...]) if mode == "nn" else dot_nt(a_ref[...], b_ref[...])
        if add is not None:
            acc = acc + refs[2][...]
        o_ref[...] = acc.astype(o_ref.dtype)

    in_specs = [pl.BlockSpec((tm, K), lambda j, i: (i, 0)),
                pl.BlockSpec((K, tn), lambda j, i: (0, j)) if mode == "nn" else pl.BlockSpec((tn, K), lambda j, i: (j, 0))]
    args = [a, b]
    if add is not None:
        in_specs.append(pl.BlockSpec((tm, tn), lambda j, i: (i, j)))
        args.append(add)
    return pl.pallas_call(
        body, out_shape=jax.ShapeDtypeStruct((M, N), out_dtype), grid=grid, in_specs=in_specs,
        out_specs=pl.BlockSpec((tm, tn), lambda j, i: (i, j)), name=name,
        compiler_params=_cparams(("arbitrary", "arbitrary")))(*args)


def _attn_heads(n):
    G = N_Q_A // N_KV_A
    k_sl = pl.ds(n * HEAD_DIM_A, HEAD_DIM_A)
    v_sl = pl.ds(W_KV_A + n * HEAD_DIM_A, HEAD_DIM_A)
    q_sl = [pl.ds((n * G + g) * HEAD_DIM_A, HEAD_DIM_A) for g in range(G)]
    return k_sl, v_sl, q_sl, range(n * G, (n + 1) * G)


def attn_fwd(p, q_off, kvp, sink, slopes, T):
    nb = T // BLOCK
    assert q_off % W_A == 0

    def body(q_ref, kv_ref, sink_ref, slope_ref, o_ref):
        c = pl.program_id(0)
        rows = pl.ds(pl.multiple_of(c * BLOCK, BLOCK), 3 * BLOCK)
        for n in range(N_KV_A):
            k_sl, v_sl, q_sl, heads = _attn_heads(n)
            group = pl.ds(n * ATTN_GROUP * BLOCK, ATTN_GROUP * BLOCK)
            q = jnp.concatenate([q_ref[:, s] for s in q_sl], axis=0)
            o = attn_block(q, kv_ref[rows, k_sl], kv_ref[rows, v_sl], sink_ref[group, :], slope_ref[group, :], c, T)
            for g, s in enumerate(q_sl):
                o_ref[:, s] = o[g * BLOCK:(g + 1) * BLOCK]

    full = lambda a: pl.BlockSpec(a.shape, functools.partial(lambda c, nd: (0,) * nd, nd=a.ndim))
    return pl.pallas_call(
        body, out_shape=jax.ShapeDtypeStruct((T, W_A), F32), grid=(nb,),
        in_specs=[pl.BlockSpec((BLOCK, W_A), lambda c: (c, q_off // W_A)), full(kvp), full(sink), full(slopes)],
        out_specs=pl.BlockSpec((BLOCK, W_A), lambda c: (c, 0)),
        name="attn_fwd", compiler_params=_cparams(("arbitrary",)))(p, kvp, sink, slopes)


def attn_bwd(p, q_off, kvp, sink, slopes, do, T):
    nb = T // BLOCK

    def body(q_ref, kv_ref, sink_ref, slope_ref, do_ref, dq_ref, dkv_ref, dsink_ref):
        c = pl.program_id(0)

        @pl.when(c == 0)
        def _():
            dkv_ref[...] = jnp.zeros_like(dkv_ref)
            dsink_ref[...] = jnp.zeros_like(dsink_ref)

        rows = pl.ds(pl.multiple_of(c * BLOCK, BLOCK), 3 * BLOCK)
        for n in range(N_KV_A):
            k_sl, v_sl, q_sl, heads = _attn_heads(n)
            group = pl.ds(n * ATTN_GROUP * BLOCK, ATTN_GROUP * BLOCK)
            slope = slope_ref[group, :]
            q = jnp.concatenate([q_ref[:, s] for s in q_sl], axis=0)
            do = jnp.concatenate([do_ref[:, s] for s in q_sl], axis=0)
            _, vjp = jax.vjp(lambda q_, kk, vv, sk: attn_block(q_, kk, vv, sk, slope, c, T),
                             q, kv_ref[rows, k_sl], kv_ref[rows, v_sl], sink_ref[group, :])
            dq, dks, dvs, dsk = vjp(do)
            dkv_ref[rows, k_sl] += dks
            dkv_ref[rows, v_sl] += dvs
            for g, (s, h) in enumerate(zip(q_sl, heads)):
                seg = slice(g * BLOCK, (g + 1) * BLOCK)
                dq_ref[:, s] = dq[seg].astype(dq_ref.dtype)
                dsink_ref[h] += jnp.sum(dsk[seg], axis=0, keepdims=True)

    full = lambda a: pl.BlockSpec(a.shape, functools.partial(lambda c, nd: (0,) * nd, nd=a.ndim))
    qspec = pl.BlockSpec((BLOCK, W_A), lambda c: (c, 0))
    return pl.pallas_call(
        body,
        out_shape=[jax.ShapeDtypeStruct((T, W_A), BF16), jax.ShapeDtypeStruct(kvp.shape, F32),
                   jax.ShapeDtypeStruct((N_Q_A, 1, 1), F32)],
        grid=(nb,),
        in_specs=[pl.BlockSpec((BLOCK, W_A), lambda c: (c, q_off // W_A)), full(kvp), full(sink), full(slopes), qspec],
        out_specs=[qspec, full(kvp), pl.BlockSpec((N_Q_A, 1, 1), lambda c: (0, 0, 0))],
        name="attn_bwd", compiler_params=_cparams(("arbitrary",)))(p, kvp, sink, slopes, do)


def mem_fwd(p, q_off, kv, T):
    tr = min(T, 2 * _row_tile(T))
    assert q_off % W_M == 0

    def body(q_ref, kv_ref, o_ref):
        for h in range(N_HEADS_M):
            hs = pl.ds(h * HEAD_DIM_M, HEAD_DIM_M)
            (o,) = mem_tile(q_ref[:, hs], kv_ref[:, hs], kv_ref[:, pl.ds(W_M + h * HEAD_DIM_M, HEAD_DIM_M)])
            o_ref[:, hs] = o

    return pl.pallas_call(
        body, out_shape=jax.ShapeDtypeStruct((T, W_M), F32), grid=(T // tr,),
        in_specs=[pl.BlockSpec((tr, W_M), lambda i: (i, q_off // W_M)), pl.BlockSpec((N_MEM, 2 * W_M), lambda i: (0, 0))],
        out_specs=pl.BlockSpec((tr, W_M), lambda i: (i, 0)),
        name="mem_fwd", compiler_params=_cparams(("arbitrary",)))(p, kv)


def mem_bwd(p, q_off, kv, do, T):
    tr = min(T, 2 * _row_tile(T))

    def body(q_ref, kv_ref, do_ref, dq_ref, dkv_ref):
        @pl.when(pl.program_id(0) == 0)
        def _():
            dkv_ref[...] = jnp.zeros_like(dkv_ref)

        for h in range(N_HEADS_M):
            hs = pl.ds(h * HEAD_DIM_M, HEAD_DIM_M)
            vs = pl.ds(W_M + h * HEAD_DIM_M, HEAD_DIM_M)
            _, vjp = jax.vjp(mem_tile, q_ref[:, hs], kv_ref[:, hs], kv_ref[:, vs])
            dq, dk, dv = vjp((do_ref[:, hs],))
            dq_ref[:, hs] = dq.astype(dq_ref.dtype)
            dkv_ref[:, hs] += dk
            dkv_ref[:, vs] += dv

    kvspec = pl.BlockSpec((N_MEM, 2 * W_M), lambda i: (0, 0))
    return pl.pallas_call(
        body,
        out_shape=[jax.ShapeDtypeStruct((T, W_M), BF16), jax.ShapeDtypeStruct((N_MEM, 2 * W_M), F32)],
        grid=(T // tr,),
        in_specs=[pl.BlockSpec((tr, W_M), lambda i: (i, q_off // W_M)), kvspec, pl.BlockSpec((tr, W_M), lambda i: (i, 0))],
        out_specs=[pl.BlockSpec((tr, W_M), lambda i: (i, 0)), kvspec],
        name="mem_bwd", compiler_params=_cparams(("arbitrary",)))(p, kv, do)


def _scan_const_specs(dk):
    C, L = SCAN_CHUNK, SCAN_LEVELS
    return [pl.BlockSpec((2, (2 + L) * C, C), lambda n: (0, 0, 0)),
            pl.BlockSpec((2, C, (2 + L) * C), lambda n: (0, 0, 0)),
            pl.BlockSpec((2, L * C, dk), lambda n: (0, 0, 0)),
            pl.BlockSpec((2, L * C, C), lambda n: (0, 0, 0))]


def _chunk_spec(src, width, chunk_of):
    arr, sel = src
    if arr.ndim == 2:
        assert sel % width == 0
        return pl.BlockSpec((SCAN_CHUNK, width), functools.partial(lambda n, b: (chunk_of(n), b), b=sel // width))
    return pl.BlockSpec((None, SCAN_CHUNK, width), functools.partial(lambda n, d: (d, chunk_of(n), 0), d=sel))


def _scan_const_args():
    h, ht, qm, pm = _scan_consts()
    return [jnp.asarray(h, BF16), jnp.asarray(ht, BF16), jnp.asarray(qm, F32), jnp.asarray(pm, F32)]


def _full_spec(a):
    return pl.BlockSpec(a.shape, functools.partial(lambda n, nd: (0,) * nd, nd=a.ndim))


def scan_fwd(name, prep, raws, params, heads, dk, dv, T):
    C = SCAN_CHUNK
    N = T // C
    assert dk == C
    Wv = heads * dv
    orders = (lambda n: n, lambda n: N - 1 - n)
    n_raw, n_par = [len(r) for r in raws], [len(p) for p in params]

    def body(*refs):
        pos, raw_refs, par_refs = 0, [], []
        for d in range(2):
            raw_refs.append(refs[pos:pos + n_raw[d]])
            pos += n_raw[d]
        for d in range(2):
            par_refs.append(refs[pos:pos + n_par[d]])
            pos += n_par[d]
        h_ref, ht_ref, qm_ref, pm_ref = refs[pos:pos + 4]
        o_refs, ss_refs, st_ref = refs[pos + 4:pos + 6], refs[pos + 6:pos + 8], refs[pos + 8]

        @pl.when(pl.program_id(0) == 0)
        def _():
            st_ref[...] = jnp.zeros_like(st_ref)

        for d in range(2):
            consts = (qm_ref[d], pm_ref[d])
            q, k, v, g = prep([r[...] for r in raw_refs[d]], [p[...] for p in par_refs[d]])
            e = _split_mm(h_ref[d], g)
            tot = jnp.sum(g, axis=0, keepdims=True)
            for h in range(heads):
                ks, vs = slice(h * dk, (h + 1) * dk), slice(h * dv, (h + 1) * dv)
                st = st_ref[d, h]
                ss_refs[d][h] = st
                o, st_new = scan_chunk(q[:, ks], k[:, ks], v[:, vs], e[:, ks], tot[:, ks], st, *consts)
                o_refs[d][:, vs] = o
                st_ref[d, h] = st_new

    ss_spec = lambda order: pl.BlockSpec((heads, None, dv, dk), lambda n: (0, order(n), 0, 0))
    return pl.pallas_call(
        body,
        out_shape=[jax.ShapeDtypeStruct((T, Wv), F32)] * 2 + [jax.ShapeDtypeStruct((heads, N, dv, dk), F32)] * 2,
        grid=(N,),
        in_specs=[_chunk_spec(s, w, orders[d]) for d in range(2) for s, w in raws[d]]
        + [_full_spec(p) for d in range(2) for p in params[d]] + _scan_const_specs(dk),
        out_specs=[pl.BlockSpec((C, Wv), lambda n: (orders[0](n), 0)), pl.BlockSpec((C, Wv), lambda n: (orders[1](n), 0)),
                   ss_spec(orders[0]), ss_spec(orders[1])],
        scratch_shapes=[pltpu.VMEM((2, heads, dv, dk), F32)],
        name=name, compiler_params=_cparams(("arbitrary",)))(
            *[s[0] for d in range(2) for s, _ in raws[d]], *[p for d in range(2) for p in params[d]], *_scan_const_args())


def scan_bwd(name, prep, raws, params, ss, do, heads, dk, dv, T):
    C = SCAN_CHUNK
    N = T // C
    Wv = heads * dv
    orders = (lambda n: N - 1 - n, lambda n: n)
    n_raw, n_par = [len(r) for r in raws], [len(p) for p in params]

    def body(*refs):
        pos, raw_refs, par_refs, draw_refs, dpar_refs = 0, [], [], [], []
        for group, counts in ((raw_refs, n_raw), (par_refs, n_par)):
            for d in range(2):
                group.append(refs[pos:pos + counts[d]])
                pos += counts[d]
        ss_refs, do_refs = refs[pos:pos + 2], refs[pos + 2:pos + 4]
        h_ref, ht_ref, qm_ref, pm_ref = refs[pos + 4:pos + 8]
        pos += 8
        for group, counts in ((draw_refs, n_raw), (dpar_refs, n_par)):
            for d in range(2):
                group.append(refs[pos:pos + counts[d]])
                pos += counts[d]
        dst_ref = refs[pos]

        @pl.when(pl.program_id(0) == 0)
        def _():
            dst_ref[...] = jnp.zeros_like(dst_ref)
            for d in range(2):
                for r in dpar_refs[d]:
                    r[...] = jnp.zeros_like(r)

        for d in range(2):
            consts = (qm_ref[d], pm_ref[d])
            (q, k, v, g), prep_vjp = jax.vjp(prep, [r[...] for r in raw_refs[d]], [p[...] for p in par_refs[d]])
            e = _split_mm(h_ref[d], g)
            tot = jnp.sum(g, axis=0, keepdims=True)
            dqs, dks, dvs, des, dtots = [], [], [], [], []
            for h in range(heads):
                ks, vs = slice(h * dk, (h + 1) * dk), slice(h * dv, (h + 1) * dv)
                _, vjp = jax.vjp(lambda q_, k_, v_, e_, t_, st_: scan_chunk(q_, k_, v_, e_, t_, st_, *consts),
                                 q[:, ks], k[:, ks], v[:, vs], e[:, ks], tot[:, ks], ss_refs[d][h])
                dq, dk_, dv_, de, dtot, dst = vjp((do_refs[d][:, vs], dst_ref[d, h]))
                dst_ref[d, h] = dst
                for group, val in ((dqs, dq), (dks, dk_), (dvs, dv_), (des, de), (dtots, dtot)):
                    group.append(val)
            cat = lambda parts: jnp.concatenate(parts, axis=-1)
            dg = _split_mm(ht_ref[d], cat(des)) + cat(dtots)
            draws, dpars = prep_vjp((cat(dqs), cat(dks), cat(dvs), dg))
            for r, val in zip(draw_refs[d], draws):
                r[...] = val.astype(r.dtype)
            for r, val in zip(dpar_refs[d], dpars):
                r[...] += val

    ss_spec = lambda order: pl.BlockSpec((heads, None, dv, dk), lambda n: (0, order(n), 0, 0))
    row_out = lambda w, order: pl.BlockSpec((C, w), lambda n: (order(n), 0))
    return pl.pallas_call(
        body,
        out_shape=[jax.ShapeDtypeStruct((T, w), BF16) for d in range(2) for _, w in raws[d]]
        + [jax.ShapeDtypeStruct(p.shape, F32) for d in range(2) for p in params[d]],
        grid=(N,),
        in_specs=[_chunk_spec(s, w, orders[d]) for d in range(2) for s, w in raws[d]]
        + [_full_spec(p) for d in range(2) for p in params[d]]
        + [ss_spec(orders[0]), ss_spec(orders[1]), _chunk_spec(do, Wv, orders[0]), _chunk_spec(do, Wv, orders[1])]
        + _scan_const_specs(dk),
        out_specs=[row_out(w, orders[d]) for d in range(2) for _, w in raws[d]]
        + [_full_spec(p) for d in range(2) for p in params[d]],
        scratch_shapes=[pltpu.VMEM((2, heads, dv, dk), F32)],
        name=name, compiler_params=_cparams(("arbitrary",)))(
            *[s[0] for d in range(2) for s, _ in raws[d]], *[p for d in range(2) for p in params[d]],
            ss[0], ss[1], do[0], do[0], *_scan_const_args())


def final_call(x, g, target, T):
    tr = _row_tile(T)

    def tile(xv, gv, tv):
        y = _rms(xv, gv)
        err = (y - tv) ** 2
        return jnp.sum(jnp.sum(err, axis=-1, keepdims=True), axis=0, keepdims=True) * (0.5 / D_MODEL)

    def body(x_ref, g_ref, t_ref, loss_ref, dx_ref, dg_ref):
        i = pl.program_id(0)
        tv = t_ref[...]
        lv, vjp = jax.vjp(lambda a, b: tile(a, b, tv), x_ref[...], g_ref[...])
        dx, dg = vjp(jnp.ones((1, 1), F32))
        dx_ref[...] = dx

        @pl.when(i == 0)
        def _():
            loss_ref[...] = jnp.zeros_like(loss_ref)
            dg_ref[...] = jnp.zeros_like(dg_ref)

        loss_ref[...] += jnp.broadcast_to(lv, loss_ref.shape)
        dg_ref[...] += dg

    return pl.pallas_call(
        body,
        out_shape=[jax.ShapeDtypeStruct((8, 128), F32), jax.ShapeDtypeStruct((T, D_MODEL), F32),
                   jax.ShapeDtypeStruct((1, D_MODEL), F32)],
        grid=(T // tr,),
        in_specs=[pl.BlockSpec((tr, D_MODEL), lambda i: (i, 0)), pl.BlockSpec((1, D_MODEL), lambda i: (0, 0)),
                  pl.BlockSpec((tr, D_MODEL), lambda i: (i, 0))],
        out_specs=[pl.BlockSpec((8, 128), lambda i: (0, 0)), pl.BlockSpec((tr, D_MODEL), lambda i: (i, 0)),
                   pl.BlockSpec((1, D_MODEL), lambda i: (0, 0))],
        name="final_loss", compiler_params=_cparams(("arbitrary",)))(x, g, target)


def adamw_call(w, g, m, v):
    shape = w.shape
    c = shape[-1]
    r = int(np.prod(shape[:-1])) if len(shape) > 1 else 1
    tr = r if r <= 256 else 256
    assert r % tr == 0

    def body(w_ref, g_ref, m_ref, v_ref, d_ref, nm_ref, nv_ref):
        gv = g_ref[...]
        nm = ADAM_B1 * m_ref[...] + (1.0 - ADAM_B1) * gv
        nv = ADAM_B2 * v_ref[...] + (1.0 - ADAM_B2) * jnp.square(gv)
        m_hat = nm / (1.0 - ADAM_B1 ** ADAM_STEP)
        v_hat = nv / (1.0 - ADAM_B2 ** ADAM_STEP)
        d_ref[...] = -ADAM_LR * (m_hat / (jnp.sqrt(v_hat) + ADAM_EPS) + ADAM_WD * w_ref[...])
        nm_ref[...] = nm
        nv_ref[...] = nv

    spec = pl.BlockSpec((tr, c), lambda i: (i, 0))
    outs = pl.pallas_call(body, out_shape=[jax.ShapeDtypeStruct((r, c), F32)] * 3, grid=(r // tr,),
                          in_specs=[spec] * 4, out_specs=[spec] * 3, name="adamw",
                          compiler_params=_cparams(("arbitrary",)))(*(t.reshape(r, c) for t in (w, g, m, v)))
    return tuple(o.reshape(shape) for o in outs)


def adamw_halves(w, mine, other, m, v, c):
    L, R, C = w.shape
    by_cols = mine.shape[-1] != C
    if by_cols:
        tile, nbh = (R, C // 2), 1
        full_idx = lambda l, i: (l, 0, i)
    else:
        rh = R // 2
        tr = rh if rh <= 256 else rh // 2
        assert tr % 8 == 0
        tile, nbh = (tr, C), rh // tr
        full_idx = lambda l, i: (l, i, 0)

    def body(c_ref, w_ref, a_ref, b_ref, m_ref, v_ref, g_ref, d_ref, nm_ref, nv_ref):
        is_mine = (pl.program_id(1) // nbh) == c_ref[0]
        gv = jnp.where(is_mine, a_ref[...], b_ref[...])
        nm = ADAM_B1 * m_ref[...] + (1.0 - ADAM_B1) * gv
        nv = ADAM_B2 * v_ref[...] + (1.0 - ADAM_B2) * jnp.square(gv)
        m_hat = nm / (1.0 - ADAM_B1 ** ADAM_STEP)
        v_hat = nv / (1.0 - ADAM_B2 ** ADAM_STEP)
        g_ref[...] = gv
        d_ref[...] = -ADAM_LR * (m_hat / (jnp.sqrt(v_hat) + ADAM_EPS) + ADAM_WD * w_ref[...])
        nm_ref[...] = nm
        nv_ref[...] = nv

    full = pl.BlockSpec((None,) + tile, lambda l, i, c_ref: full_idx(l, i))
    half = pl.BlockSpec((None,) + tile, lambda l, i, c_ref: (l, i % nbh, 0))
    grid_spec = pltpu.PrefetchScalarGridSpec(num_scalar_prefetch=1, grid=(L, 2 * nbh),
                                             in_specs=[full, half, half, full, full], out_specs=[full] * 4)
    return pl.pallas_call(body, out_shape=[jax.ShapeDtypeStruct(w.shape, F32)] * 4, grid_spec=grid_spec,
                          name="adamw_halves", compiler_params=_cparams(("arbitrary", "arbitrary")))(c, w, mine, other, m, v)


def sum_devices(g64):
    def body(x_ref, o_ref):
        acc = x_ref[0:8, :]
        for d in range(1, 8):
            acc = acc + x_ref[8 * d:8 * d + 8, :]
        o_ref[...] = acc

    return pl.pallas_call(body, out_shape=jax.ShapeDtypeStruct((8, D_MODEL), F32), name="sum_devices")(g64)


def _half_tile(rh):
    if rh <= 512:
        return rh
    return next(rh // d for d in range(2, rh) if rh % d == 0 and (rh // d) % 16 == 0 and rh // d <= 512)


def _half_geometry(full_shape, half_shape):
    R, C = full_shape[-2:]
    if half_shape[-1] != C:
        return (R, C // 2), 1, lambda i, c: (0, c)
    tr = _half_tile(R // 2)
    nblk = (R // 2) // tr
    return (tr, C), nblk, lambda i, c: (i + c * nblk, 0)


def add_sibling(g, recv, c, out_dtype):
    tile, nblk, own = _half_geometry(g.shape, recv.shape)

    def body(c_ref, g_ref, r_ref, o_ref):
        o_ref[...] = (g_ref[...] + r_ref[...]).astype(o_ref.dtype)

    half = pl.BlockSpec((None,) + tile, lambda j, i, c_ref: (j, i, 0))
    grid_spec = pltpu.PrefetchScalarGridSpec(
        num_scalar_prefetch=1, grid=(4, nblk),
        in_specs=[pl.BlockSpec((None,) + tile, lambda j, i, c_ref: (j,) + own(i, c_ref[0])), half], out_specs=half)
    return pl.pallas_call(body, out_shape=jax.ShapeDtypeStruct(recv.shape, out_dtype), grid_spec=grid_spec,
                          name="rs_add_sibling", compiler_params=_cparams(("arbitrary", "arbitrary")))(c, g, recv)


def add_chips(g, recv, r3, place):
    tile, nblk, own = _half_geometry(g.shape, recv.shape)

    def body(p_ref, g_ref, s_ref, a_ref, b_ref, c_ref, o_ref):
        up = lambda r: r[...].astype(F32)
        o_ref[...] = (((g_ref[...] + up(s_ref)) + up(a_ref)) + up(b_ref)) + up(c_ref)

    grid_spec = pltpu.PrefetchScalarGridSpec(
        num_scalar_prefetch=1, grid=(nblk,),
        in_specs=[pl.BlockSpec((None,) + tile, lambda i, p_ref: (p_ref[0],) + own(i, p_ref[1])),
                  pl.BlockSpec((None,) + tile, lambda i, p_ref: (p_ref[0], i, 0))]
        + [pl.BlockSpec((None,) + tile, functools.partial(lambda i, p_ref, k: (k, i, 0), k=k)) for k in range(3)],
        out_specs=pl.BlockSpec(tile, lambda i, p_ref: (i, 0)))
    return pl.pallas_call(body, out_shape=jax.ShapeDtypeStruct(recv.shape[1:], F32), grid_spec=grid_spec,
                          name="rs_add_chips", compiler_params=_cparams(("arbitrary",)))(place, g, recv, r3, r3, r3)


def _remote(src, dst, ssem, rsem, dev):
    return pltpu.make_async_remote_copy(src_ref=src, dst_ref=dst, send_sem=ssem, recv_sem=rsem,
                                        device_id=dev, device_id_type=pl.DeviceIdType.MESH)


def _mesh_places():
    x, y, c = lax.axis_index("x"), lax.axis_index("y"), lax.axis_index("c")
    chips = [(1 - x, y), (x, 1 - y), (1 - x, 1 - y)]
    return x, y, c, (x, y, 1 - c), chips


def _hbm_specs(n):
    return [pl.BlockSpec(memory_space=pltpu.HBM) for _ in range(n)]


def _gather_body(ins, outs, n_split, send_sems, recv_sems, handshake):
    x, y, c, sibling, chips = _mesh_places()
    mine = 2 * x + y
    if handshake:
        barrier = pltpu.get_barrier_semaphore()
        peers = [sibling] + [(*chip, c) for chip in chips]
        for peer in peers:
            pl.semaphore_signal(barrier, inc=1, device_id=peer, device_id_type=pl.DeviceIdType.MESH)
        pl.semaphore_wait(barrier, len(peers))

    def half(a, chip_idx, which):
        rh = ins[a].shape[0] // 2
        return outs[a].at[chip_idx, pl.ds(which * rh, rh), :]

    sent = []
    for a in range(len(ins)):
        for k, chip in enumerate(chips):
            if a < n_split:
                rh = ins[a].shape[0] // 2
                src, dst = ins[a].at[pl.ds(c * rh, rh), :], half(a, mine, c)
            else:
                src, dst = ins[a], outs[a].at[mine]
            sent.append(_remote(src, dst, send_sems.at[a, k], recv_sems.at[a, k], (*chip, c)))
    for cp in sent:
        cp.start()
    for a in range(len(ins)):
        for k, chip in enumerate(chips):
            j = 2 * chip[0] + chip[1]
            region = half(a, j, c) if a < n_split else outs[a].at[j]
            _remote(region, region, send_sems.at[a, k], recv_sems.at[a, k], (*chip, c)).wait_recv()
            if a < n_split:
                fwd = _remote(region, region, send_sems.at[a, 3 + k], recv_sems.at[a, 3 + k], sibling)
                fwd.start()
                sent.append(fwd)
    for a in range(n_split):
        for k, chip in enumerate(chips):
            region = half(a, 2 * chip[0] + chip[1], 1 - c)
            _remote(region, region, send_sems.at[a, 3 + k], recv_sems.at[a, 3 + k], sibling).wait_recv()
    for cp in sent:
        cp.wait_send()


def gather_weights(shards, small):
    arrs = list(shards) + [small]
    n = len(arrs)

    def body(*refs):
        _gather_body(refs[:n], refs[n:2 * n], n - 1, refs[2 * n], refs[2 * n + 1], handshake=False)

    return pl.pallas_call(
        body, out_shape=[jax.ShapeDtypeStruct((4,) + a.shape, a.dtype) for a in arrs],
        in_specs=_hbm_specs(n), out_specs=_hbm_specs(n),
        scratch_shapes=[pltpu.SemaphoreType.DMA((n, 6)), pltpu.SemaphoreType.DMA((n, 6))],
        name="gather_weights")(*arrs)


def gather_weights_async(shards):
    n = len(shards)

    def body(*refs):
        _gather_body(refs[:n], refs[n:2 * n], n, refs[2 * n], refs[2 * n + 1], handshake=True)

    return pl.kernel(
        body, out_type=[jax.ShapeDtypeStruct((4,) + a.shape, a.dtype) for a in shards],
        mesh=plsc.ScalarSubcoreMesh(axis_name="seq", num_cores=1),
        scratch_types=[pltpu.SemaphoreType.DMA((n, 6)), pltpu.SemaphoreType.DMA((n, 6))],
        compiler_params=pltpu.CompilerParams(collective_id=1), name="gather_weights_async")(*shards)


def _sequencer_call(name, body, out_type, sem_shape, collective_id, args):
    return pl.kernel(
        body, out_type=out_type, mesh=plsc.ScalarSubcoreMesh(axis_name="seq", num_cores=1),
        scratch_types=[pltpu.SemaphoreType.DMA(sem_shape), pltpu.SemaphoreType.DMA(sem_shape)],
        compiler_params=pltpu.CompilerParams(collective_id=collective_id), name=name)(*args)


def _handshake(peers):
    barrier = pltpu.get_barrier_semaphore()
    for peer in peers:
        pl.semaphore_signal(barrier, inc=1, device_id=peer, device_id_type=pl.DeviceIdType.MESH)
    pl.semaphore_wait(barrier, len(peers))


def exchange_siblings(name, srcs, axes, collective_id):
    n = len(srcs)

    def body(*refs):
        ins, outs = refs[:n], refs[n:2 * n]
        send_sems, recv_sems = refs[2 * n:]
        x, y, c, sibling, chips = _mesh_places()
        _handshake([sibling])
        cps = []
        for a in range(n):
            src = ins[a]
            if axes[a] is not None:
                half = src.shape[axes[a]] // 2
                theirs = pl.ds((1 - c) * half, half)
                src = src.at[:, theirs, :] if axes[a] == 1 else src.at[:, :, theirs]
            cps.append(_remote(src, outs[a], send_sems.at[a], recv_sems.at[a], sibling))
        for cp in cps:
            cp.start()
        for cp in cps:
            cp.wait()

    def shape(g, axis):
        return g.shape if axis is None else tuple(d // 2 if k == axis else d for k, d in enumerate(g.shape))

    return _sequencer_call(name, body, [jax.ShapeDtypeStruct(shape(g, ax), g.dtype) for g, ax in zip(srcs, axes)],
                           (n,), collective_id, srcs)


def exchange_chips(name, s1s, collective_id):
    n = len(s1s)

    def body(*refs):
        ins, outs = refs[:n], refs[n:2 * n]
        send_sems, recv_sems = refs[2 * n:]
        x, y, c, sibling, chips = _mesh_places()
        _handshake([(*chip, c) for chip in chips])
        cps = []
        for a in range(n):
            for k, chip in enumerate(chips):
                cps.append(_remote(ins[a].at[2 * chip[0] + chip[1]], outs[a].at[k], send_sems.at[a, k],
                                   recv_sems.at[a, k], (*chip, c)))
        for cp in cps:
            cp.start()
        for cp in cps:
            cp.wait()

    return _sequencer_call(name, body, [jax.ShapeDtypeStruct((3,) + s.shape[1:], s.dtype) for s in s1s], (n, 3),
                           collective_id, s1s)


def allgather_small(v):
    m_per = v.shape[0]

    def body(x_ref, out_ref, send_sems, recv_sems, local_sem):
        x, y, c, sibling, chips = _mesh_places()
        me = (x, y, c)

        def rows(px, py, pc):
            return out_ref.at[pl.ds((4 * px + 2 * py + pc) * m_per, m_per), :]

        def copy(k, block, to, src=None):
            return _remote(rows(*block) if src is None else src, rows(*block), send_sems.at[k], recv_sems.at[k], to)

        mine = pltpu.make_async_copy(x_ref, rows(*me), local_sem)
        mine.start()
        first = [copy(0, me, sibling, src=x_ref)]
        first += [copy(1 + j, me, (*chip, c), src=x_ref) for j, chip in enumerate(chips)]
        for cp in first:
            cp.start()
        passed = [copy(4 + j, (*chip, c), sibling) for j, chip in enumerate(chips)]
        for j, chip in enumerate(chips):
            copy(1 + j, (*chip, c), me).wait_recv()
            passed[j].start()
        copy(0, sibling, me).wait_recv()
        for j, chip in enumerate(chips):
            copy(4 + j, (*chip, 1 - c), me).wait_recv()
        for cp in first + passed:
            cp.wait_send()
        mine.wait()

    return pl.pallas_call(
        body, out_shape=jax.ShapeDtypeStruct((8 * m_per, v.shape[1]), v.dtype),
        in_specs=[pl.BlockSpec(memory_space=pltpu.VMEM)], out_specs=pl.BlockSpec(memory_space=pltpu.VMEM),
        scratch_shapes=[pltpu.SemaphoreType.DMA((7,)), pltpu.SemaphoreType.DMA((7,)), pltpu.SemaphoreType.DMA],
        name="allgather_small")(v)


def rms_res_tile(x, g):
    return (_rms(x, g), x)


def _lower_bounds(lb_param):
    lbs = jax.nn.softmax(lb_param.astype(F32), axis=0)
    return jnp.cumsum(lbs, axis=0) - lbs[0]


def _even_fwd(x, i, W, lower, kv, slopes, T):
    O = EVEN_OFF
    g = W["norm_even"][i].reshape(1, D_MODEL)
    (h,) = rows_call("rms_fwd", rms_tile, T, [("row", x, 0, D_MODEL), ("full", g)], [D_MODEL], [BF16])
    p = matmul("mm_in_e", h, W["w_in_e"][i], "nn")
    kvp = jnp.pad(p[:, O["kA"]:O["kA"] + 2 * W_KV_A], ((BLOCK, BLOCK), (0, 0)))
    sink = jnp.repeat(W["sink"][i], BLOCK).reshape(N_Q_A * BLOCK, 1)
    a = attn_fwd(p, O["qA"], kvp, sink, slopes, T)
    scan_raws = [[((p, O["qB"]), W_B), ((p, O[z]), W_B), ((p, O["iB"]), W_B)] for z in ("zf", "zb")]
    scan_pars = [[lower[i][0:1]], [lower[i][1:2]]]
    o_f, o_b, ss_f, ss_b = scan_fwd("scan_fwd_h", hgrn_prep, scan_raws, scan_pars, N_HEADS_B, HEAD_DIM_B, HEAD_DIM_B, T)
    mo = mem_fwd(p, O["qM"], kv, T)
    hg = W["hgrn_norm"][i].reshape(1, W_B)
    post_ins = [("row", a, 0, W_A), ("row", o_f, 0, W_B), ("row", o_b, 0, W_B), ("row", mo, 0, W_M),
                ("row", p, O["gA"], W_A), ("row", p, O["gB"], W_B), ("row", p, O["gM"], W_M), ("full", hg)]
    (mix,) = rows_call("even_post_fwd", even_post_tile, T, post_ins, [MIX], [BF16])
    x_new = matmul("mm_out", mix, W["w_out_e"][i], "nn", add=x)
    return x_new, dict(x=x, g=g, h=h, p=p, kvp=kvp, sink=sink, scan_raws=scan_raws, scan_pars=scan_pars,
                       ss=(ss_f, ss_b), post_ins=post_ins, mix=mix)


def _add2(a, b):
    return a.astype(F32) + b.astype(F32)


def _assemble_even(dqA, dgA, dqB_f, dqB_b, dzf, dzb, diB_f, diB_b, dgB, dqM, dgM, dkvA):
    parts = [dqA, dgA, _add2(dqB_f, dqB_b), dzf, dzb, _add2(diB_f, diB_b), dgB, dqM, dgM, dkvA]
    return (jnp.concatenate([t.astype(BF16) for t in parts], axis=-1),)


def _even_bwd(dxo, sv, i, W, kv, slopes, T, sync):
    O = EVEN_OFF
    p = sv["p"]
    dmix = matmul("mm_dmix", dxo, W["w_out_e"][i], "nt")
    dwo = matmul("mm_dwo", sv["mix"], dxo, "tn")
    da, dof, dmo, dgA, dgB, dgM, dhg = rows_vjp_call("even_post_bwd", even_post_tile, T, sv["post_ins"],
                                                      [[("row", dmix, 0, MIX)]], skip=(2,), narrow=(4, 5, 6))
    dqA, dkvp, dsink = attn_bwd(p, O["qA"], sv["kvp"], sv["sink"], slopes, da, T)
    dkvA = dkvp[BLOCK:-BLOCK]
    dqB_f, dzf, diB_f, dqB_b, dzb, diB_b, dlow_f, dlow_b = scan_bwd(
        "scan_bwd_h", hgrn_prep, sv["scan_raws"], sv["scan_pars"], sv["ss"], (dof, 0), N_HEADS_B, HEAD_DIM_B, HEAD_DIM_B, T)
    dqB_f = sync(dqB_f)
    row = lambda arr, w: ("row", arr, 0, w)
    dlow = jnp.concatenate([dlow_f, dlow_b], axis=0)
    dqM, dkv = mem_bwd(p, O["qM"], kv, dmo, T)
    (dp,) = rows_call("even_dp", _assemble_even, T,
                      [row(dqA, W_A), row(dgA, W_A), row(dqB_f, W_B), row(dqB_b, W_B), row(dzf, W_B), row(dzb, W_B),
                       row(diB_f, W_B), row(diB_b, W_B), row(dgB, W_B), row(dqM, W_M), row(dgM, W_M),
                       row(dkvA, 2 * W_KV_A)],
                      [EVEN_IN], [BF16])
    dh = matmul("mm_dh_e", dp, W["w_in_e"][i], "nt")
    dwi = matmul("mm_dwi_e", sv["h"], dp, "tn")
    dx, dg = rows_vjp_call("rms_res_bwd", rms_res_tile, T, [("row", sv["x"], 0, D_MODEL), ("full", sv["g"])],
                           [[("row", dh, 0, D_MODEL)], [("row", dxo, 0, D_MODEL)]])
    return dx, dict(w_in=dwi, w_out=dwo, norm=dg[0], sink=dsink.reshape(N_Q_A), low=dlow, hg=dhg[0], kv=dkv)


def _pad_gate_up(w_up):
    z = jnp.zeros((2, 128, WK_C), F32)
    z = z.at[0, 0:GATE_RANK].set(w_up[0])
    return z.at[1, GATE_RANK:2 * GATE_RANK].set(w_up[1])


def _odd_fwd(x, i, W, kv, T):
    O = ODD_OFF
    g = W["norm_odd"][i].reshape(1, D_MODEL)
    (h,) = rows_call("rms_fwd", rms_tile, T, [("row", x, 0, D_MODEL), ("full", g)], [D_MODEL], [BF16])
    p = matmul("mm_in_o", h, W["w_in_o"][i], "nn")
    wup = _pad_gate_up(W["w_gate_up"][i])
    one_dir = [((p, O["qC"]), WK_C), ((p, O["kC"]), WK_C), ((p, O["vC"]), WV_C), ((p, O["rr"]), 128)]
    scan_raws = [one_dir, one_dir]
    scan_pars = [[wup[d], W["b_gate"][i][d:d + 1]] for d in range(2)]
    o_f, o_b, ss_f, ss_b = scan_fwd("scan_fwd_g", gla_prep, scan_raws, scan_pars, N_HEADS_C, DK_C, DV_C, T)
    mo = mem_fwd(p, O["qM"], kv, T)
    gg = W["gla_norm"][i].reshape(1, WV_C)
    post_ins = [("row", o_f, 0, WV_C), ("row", o_b, 0, WV_C), ("row", mo, 0, W_M),
                ("row", p, O["gC"], WV_C), ("row", p, O["gM"], W_M), ("full", gg)]
    (mix,) = rows_call("odd_post_fwd", odd_post_tile, T, post_ins, [MIX], [BF16])
    x_new = matmul("mm_out", mix, W["w_out_o"][i], "nn", add=x)
    return x_new, dict(x=x, g=g, h=h, p=p, scan_raws=scan_raws, scan_pars=scan_pars, ss=(ss_f, ss_b),
                       post_ins=post_ins, mix=mix)


def _assemble_odd(dq0, dq1, dk0, dk1, dv0, dv1, dgC, dqM, dgM, dr0, dr1):
    parts = [_add2(dq0, dq1), _add2(dk0, dk1), _add2(dv0, dv1), dgC, dqM, dgM, _add2(dr0, dr1)]
    return (jnp.concatenate([t.astype(BF16) for t in parts], axis=-1),)


def _odd_bwd(dxo, sv, i, W, kv, T, sync):
    O = ODD_OFF
    p = sv["p"]
    dmix = matmul("mm_dmix", dxo, W["w_out_o"][i], "nt")
    dwo = matmul("mm_dwo", sv["mix"], dxo, "tn")
    dof, dmo, dgC, dgM, dgg = rows_vjp_call("odd_post_bwd", odd_post_tile, T, sv["post_ins"],
                                            [[("row", dmix, 0, MIX)]], skip=(1,), narrow=(3, 4))
    dqf, dkf, dvf, dr_f, dqb, dkb, dvb, dr_b, dwup_f, dbg_f, dwup_b, dbg_b = scan_bwd(
        "scan_bwd_g", gla_prep, sv["scan_raws"], sv["scan_pars"], sv["ss"], (dof, 0), N_HEADS_C, DK_C, DV_C, T)
    dqf = sync(dqf)
    row = lambda arr, w: ("row", arr, 0, w)
    dqM, dkv = mem_bwd(p, O["qM"], kv, dmo, T)
    (dp,) = rows_call("odd_dp", _assemble_odd, T,
                      [row(dqf, WK_C), row(dqb, WK_C), row(dkf, WK_C), row(dkb, WK_C), row(dvf, WV_C), row(dvb, WV_C),
                       row(dgC, WV_C), row(dqM, W_M), row(dgM, W_M), row(dr_f, 128), row(dr_b, 128)],
                      [ODD_PAD], [BF16])
    dh = matmul("mm_dh_o", dp, W["w_in_o"][i], "nt")
    dwi = matmul("mm_dwi_o", sv["h"], dp, "tn")
    dx, dg = rows_vjp_call("rms_res_bwd", rms_res_tile, T, [("row", sv["x"], 0, D_MODEL), ("full", sv["g"])],
                           [[("row", dh, 0, D_MODEL)], [("row", dxo, 0, D_MODEL)]])
    dw_up = jnp.stack([dwup_f[0:GATE_RANK], dwup_b[GATE_RANK:2 * GATE_RANK]])
    dbg = jnp.concatenate([dbg_f, dbg_b], axis=0)
    return dx, dict(w_in=dwi, w_out=dwo, norm=dg[0], w_up=dw_up, b_gate=dbg, gg=dgg[0], kv=dkv)


def local_step(x, mem, target, W, later=None, on_layer_grads=None, sync=lambda a: a):
    T = x.shape[0]
    slopes = jnp.repeat(2.0 ** (-8.0 * jnp.arange(1, N_Q_A + 1, dtype=F32) / N_Q_A), BLOCK).reshape(N_Q_A * BLOCK, 1)
    lower, lower_vjp = jax.vjp(_lower_bounds, W["lb_param"])
    mem_g = W["mem_norm"].reshape(1, D_MODEL)
    (mem_n,) = rows_call("mem_rms_fwd", rms_tile, N_MEM, [("row", mem, 0, D_MODEL), ("full", mem_g)], [D_MODEL], [BF16])
    kvs, saved = [], []
    for l in range(DEPTH):
        if l == 1 and later is not None:
            x, W = later(x, W)
        kvs.append(matmul("mm_kv", mem_n, W["w_kv"][l], "nn"))
        if l % 2 == 0:
            x, sv = _even_fwd(x, l // 2, W, lower, kvs[l], slopes, T)
        else:
            x, sv = _odd_fwd(x, l // 2, W, kvs[l], T)
        saved.append(sv)
    loss, dx, dgf = final_call(x, W["final_norm"].reshape(1, D_MODEL), target, T)
    per = [None] * DEPTH
    dmem_n = None
    for l in reversed(range(DEPTH)):
        if l % 2 == 0:
            dx, per[l] = _even_bwd(dx, saved[l], l // 2, W, kvs[l], slopes, T, sync)
        else:
            dx, per[l] = _odd_bwd(dx, saved[l], l // 2, W, kvs[l], T, sync)
        per[l]["w_kv"] = matmul("mm_dwkv", mem_n, per[l]["kv"], "tn")
        dmem_n = matmul("mm_dmem", per[l]["kv"], W["w_kv"][l], "nt", add=dmem_n)
        if on_layer_grads is not None:
            dx = on_layer_grads(l, dx, per[l])
    dw_kv = [per[l]["w_kv"] for l in range(DEPTH)]
    (dmem_norm,) = rows_vjp_call("mem_rms_bwd", rms_tile, N_MEM, [("row", mem, 0, D_MODEL), ("full", mem_g)],
                                 [[("row", dmem_n, 0, D_MODEL)]], skip=(0,))
    ev, od = (per[0], per[2]), (per[1], per[3])
    (d_lb,) = lower_vjp(jnp.stack([e["low"] for e in ev]))
    grads = dict(
        w_in_e=jnp.stack([e["w_in"] for e in ev]), w_in_o=jnp.stack([o["w_in"] for o in od]),
        w_out_e=jnp.stack([e["w_out"] for e in ev]), w_out_o=jnp.stack([o["w_out"] for o in od]),
        w_kv=jnp.stack(dw_kv), norm_even=jnp.stack([e["norm"] for e in ev]), sink=jnp.stack([e["sink"] for e in ev]),
        lb_param=d_lb, hgrn_norm=jnp.stack([e["hg"] for e in ev]), norm_odd=jnp.stack([o["norm"] for o in od]),
        w_gate_up=jnp.stack([o["w_up"] for o in od]), b_gate=jnp.stack([o["b_gate"] for o in od]),
        gla_norm=jnp.stack([o["gg"] for o in od]), mem_norm=dmem_norm[0], final_norm=dgf[0])
    return loss, dx, grads


SMALL_SPECS = (("lb_param", (2, 2, 128)), ("norm_odd", (2, 256)), ("w_gate_up", (2, 2, 16, 128)),
               ("b_gate", (2, 2, 128)), ("gla_norm", (2, 256)))
SMALL_ROWS = 80


def _pack_small_local(d):
    return jnp.concatenate([d[n].reshape(-1) for n, _ in SMALL_SPECS]).reshape(SMALL_ROWS, 128)


def _unpack_small_local(b):
    flat, out, o = b.reshape(-1), {}, 0
    for n, shp in SMALL_SPECS:
        sz = int(np.prod(shp))
        out[n] = flat[o:o + sz].reshape(shp)
        o += sz
    return out


def _unpack_small_full(g4):
    per = [_unpack_small_local(g4[j]) for j in range(4)]
    return {n: jnp.concatenate([per[j][n] for j in range(4)], axis=-1) for n, _ in SMALL_SPECS}


def _pack_small_blocks(full):
    blocks = []
    for j in range(4):
        blocks.append(_pack_small_local({n: full[n][..., j * shp[-1]:(j + 1) * shp[-1]] for n, shp in SMALL_SPECS}))
    return jnp.stack(blocks)


def _cols(t, order, off, widths):
    return [t[..., off[n]:off[n] + widths[n]] for n in order]


EVEN_REF_ORDER = ("qA", "kA", "vA", "gA", "qB", "zf", "zb", "iB", "gB", "qM", "gM")
ODD_REF_ORDER = ("qC", "kC", "vC", "gC", "rr", "qM", "gM")


def _layer_weights(l, g_in, g_out, g_kv):
    t = g_in.transpose(1, 0, 2).reshape(D_MODEL, -1)
    if l % 2 == 0:
        w_in = jnp.concatenate(_cols(t, EVEN_ORDER, EVEN_REF_OFF, EVEN_W), axis=-1)
    else:
        w_in = jnp.concatenate(_cols(t, ODD_ORDER, ODD_REF_OFF, ODD_W) + [jnp.zeros((D_MODEL, ODD_PAD - ODD_IN), BF16)],
                               axis=-1)
    return w_in, g_out.reshape(MIX, D_MODEL), g_kv.reshape(D_MODEL, 2 * W_M)


def _layer_grad_blocks(l, gl):
    if l % 2 == 0:
        t = jnp.concatenate(_cols(gl["w_in"], EVEN_REF_ORDER, EVEN_OFF, EVEN_W), axis=-1)
    else:
        t = jnp.concatenate(_cols(gl["w_in"], ODD_REF_ORDER, ODD_OFF, ODD_W), axis=-1)
    b_in = t.reshape(D_MODEL, 4, -1).transpose(1, 2, 0)
    return [b_in, gl["w_out"].reshape(4, MIX // 4, D_MODEL), gl["w_kv"].reshape(4, D_MODEL // 4, 2 * W_M)]


WEIGHT_NAMES = ("norm_even", "w_in_even", "sink", "lb_param", "hgrn_norm", "w_out_even", "norm_odd", "w_in_odd",
                "w_gate_up", "b_gate", "gla_norm", "w_out_odd", "mem_norm", "w_mem_kv", "final_norm")


def kernel(x, mem, norm_even, w_in_even, sink, lb_param, hgrn_norm, w_out_even, norm_odd, w_in_odd, w_gate_up, b_gate, gla_norm, w_out_odd, mem_norm, w_mem_kv, final_norm, loss_target, m_norm_even, m_w_in_even, m_sink, m_lb_param, m_hgrn_norm, m_w_out_even, m_norm_odd, m_w_in_odd, m_w_gate_up, m_b_gate, m_gla_norm, m_w_out_odd, m_mem_norm, m_w_mem_kv, m_final_norm, v_norm_even, v_w_in_even, v_sink, v_lb_param, v_hgrn_norm, v_w_out_even, v_norm_odd, v_w_in_odd, v_w_gate_up, v_b_gate, v_gla_norm, v_w_out_odd, v_mem_norm, v_w_mem_kv, v_final_norm):
    w = dict(zip(WEIGHT_NAMES, (norm_even, w_in_even, sink, lb_param, hgrn_norm, w_out_even, norm_odd, w_in_odd,
                                w_gate_up, b_gate, gla_norm, w_out_odd, mem_norm, w_mem_kv, final_norm)))
    m = dict(zip(WEIGHT_NAMES, (m_norm_even, m_w_in_even, m_sink, m_lb_param, m_hgrn_norm, m_w_out_even, m_norm_odd,
                                m_w_in_odd, m_w_gate_up, m_b_gate, m_gla_norm, m_w_out_odd, m_mem_norm, m_w_mem_kv,
                                m_final_norm)))
    v = dict(zip(WEIGHT_NAMES, (v_norm_even, v_w_in_even, v_sink, v_lb_param, v_hgrn_norm, v_w_out_even, v_norm_odd,
                                v_w_in_odd, v_w_gate_up, v_b_gate, v_gla_norm, v_w_out_odd, v_mem_norm, v_w_mem_kv,
                                v_final_norm)))
    ci = lax.axis_index("c").astype(jnp.int32).reshape(1)
    chip = (2 * lax.axis_index("x") + lax.axis_index("y")).astype(jnp.int32).reshape(1)

    shards = []
    for l in range(DEPTH):
        names = ("w_in_even", "w_out_even") if l % 2 == 0 else ("w_in_odd", "w_out_odd")
        shards.append([w[names[0]][l // 2].astype(BF16), w[names[1]][l // 2].astype(BF16), w_mem_kv[l].astype(BF16)])
    small = _pack_small_local(w)
    own = lambda g, s: lax.dynamic_update_slice(g, s[None], (chip[0], 0, 0))
    first = [own(g, s) for g, s in zip(gather_weights(shards[0], small), shards[0] + [small])]
    later_shards = shards[1] + shards[2] + shards[3]
    later_raw = gather_weights_async(later_shards)
    w0 = _layer_weights(0, *first[0:3])
    W = dict(w_in_e=[w0[0]], w_out_e=[w0[1]], w_kv=[w0[2]])
    W.update(_unpack_small_full(first[3]))
    W.update({n: w[n] for n in ("norm_even", "sink", "hgrn_norm", "mem_norm", "final_norm")})

    def later(x1, W):
        x1, raw = lax.optimization_barrier((x1, list(later_raw)))
        g = [own(a, s) for a, s in zip(raw, later_shards)]
        w1, w2, w3 = (_layer_weights(l, *g[3 * (l - 1):3 * l]) for l in (1, 2, 3))
        W = dict(W)
        W.update(w_in_e=[w0[0], w2[0]], w_in_o=[w1[0], w3[0]], w_out_e=[w0[1], w2[1]], w_out_o=[w1[1], w3[1]],
                 w_kv=[w0[2], w1[2], w2[2], w3[2]])
        return x1, W

    place = jnp.concatenate([chip, ci])

    def start(tag, blocks, wire):
        axes = [2 if b.shape[1] == ODD_IN // 4 else 1 for b in blocks]
        return dict(tag=tag, blocks=blocks, wire=wire, step=0,
                    recv=exchange_siblings(f"rs_siblings_{tag}", blocks, axes, 2))

    def advance(p):
        if p["step"] == 0:
            sums = [add_sibling(g, r, ci, dt) for g, r, dt in zip(p["blocks"], p["recv"], p["wire"])]
            p["recv3"] = exchange_chips(f"rs_chips_{p['tag']}", sums, 3)
        else:
            p["mine"] = [add_chips(g, r, r3, place) for g, r, r3 in zip(p["blocks"], p["recv"], p["recv3"])]
            p["other"] = exchange_siblings(f"rs_final_{p['tag']}", p["mine"], [None] * len(p["mine"]), 4)
        p["step"] += 1

    pipes, first_layer = [], {}

    def sync(a):
        for p in pipes:
            if p["step"] < 3:
                key = ("recv", "recv3", "other")[p["step"]]
                a, arrived = lax.optimization_barrier((a, list(p[key])))
                p[key] = arrived
                if p["step"] < 2:
                    advance(p)
                else:
                    p["step"] = 3
        return a

    def on_layer_grads(l, dx, gl):
        dx = sync(dx)
        if l == 0:
            first_layer.update(gl)
        else:
            pipes.append(start(f"l{l}", _layer_grad_blocks(l, gl), [BF16] * 3))
        return dx

    loss_tile, dx, grads = local_step(x[0], mem[0], loss_target[0], W, later, on_layer_grads, sync)
    last = start("l0", _layer_grad_blocks(0, first_layer) + [_pack_small_blocks(grads)], [BF16] * 3 + [F32])
    for p in pipes + [last]:
        while p["step"] < (1 if p is last else 2):
            advance(p)
    by_layer = {int(p["tag"][1:]): p for p in pipes + [last]}
    halves = lambda layers, k: (jnp.stack([by_layer[l]["mine"][k] for l in layers]),
                                jnp.stack([by_layer[l]["other"][k] for l in layers]))
    gl, upd = {}, {}

    pack = jnp.zeros((8, D_MODEL), F32)
    pack = pack.at[0:2].set(grads["norm_even"]).at[2].set(grads["hgrn_norm"].reshape(-1))
    pack = pack.at[3].set(grads["mem_norm"]).at[4].set(grads["final_norm"])
    pack = pack.at[5, 0:16].set(grads["sink"].reshape(-1)).at[5, 16].set(loss_tile[0, 0])
    tot = sum_devices(allgather_small(pack))
    gl.update(norm_even=tot[0:2], hgrn_norm=tot[2].reshape(2, W_B), mem_norm=tot[3], final_norm=tot[4],
              sink=tot[5, 0:16].reshape(2, N_Q_A))
    loss = tot[5, 16]
    for n in ("norm_even", "hgrn_norm", "mem_norm", "final_norm", "sink"):
        upd[n] = adamw_call(w[n], gl[n], m[n], v[n])
    tr_ = lambda a: jnp.swapaxes(a, 1, 2)
    gl["w_in_odd"], *upd["w_in_odd"] = [tr_(o) for o in adamw_halves(
        tr_(w["w_in_odd"]), *halves((1, 3), 0), tr_(m["w_in_odd"]), tr_(v["w_in_odd"]), ci)]
    gl["w_out_odd"], *upd["w_out_odd"] = adamw_halves(w["w_out_odd"], *halves((1, 3), 1), m["w_out_odd"],
                                                      v["w_out_odd"], ci)
    early = [upd[n] for n in sorted(upd)] + [gl["w_in_odd"], gl["w_out_odd"]]
    last["recv3"], early = lax.optimization_barrier((list(last["recv3"]), early))
    for n, res in zip(sorted(upd), early):
        upd[n] = res
    gl["w_in_odd"], gl["w_out_odd"] = early[-2:]
    advance(last)

    big = dict(w_in_even=halves((0, 2), 0), w_out_even=halves((0, 2), 1), w_mem_kv=halves((0, 1, 2, 3), 2))
    s_mine, s_other = last["mine"][3], last["other"][3]
    g_small = jnp.where(ci[0] == 0, jnp.concatenate([s_mine, s_other]), jnp.concatenate([s_other, s_mine]))
    gl.update(_unpack_small_local(g_small))
    for n in WEIGHT_NAMES:
        if n == "w_in_even":
            gl[n], *upd[n] = [tr_(o) for o in adamw_halves(tr_(w[n]), *big[n], tr_(m[n]), tr_(v[n]), ci)]
        elif n in big:
            gl[n], *upd[n] = adamw_halves(w[n], *big[n], m[n], v[n], ci)
        elif n not in upd:
            upd[n] = adamw_call(w[n], gl[n], m[n], v[n])
    return (loss, dx[None], *[gl[n] for n in WEIGHT_NAMES], *[upd[n][0] for n in WEIGHT_NAMES],
            *[upd[n][1] for n in WEIGHT_NAMES], *[upd[n][2] for n in WEIGHT_NAMES])
```

```python
import functools

import numpy as np
import jax
import jax.numpy as jnp
from jax import lax
from jax.experimental import pallas as pl
from jax.experimental.pallas import tpu as pltpu
from jax.experimental.pallas import tpu_sc as plsc

F32 = jnp.float32
BF16 = jnp.bfloat16

D_MODEL = 1024
DEPTH = 4
N_Q_A, N_KV_A, HEAD_DIM_A = 8, 2, 64
W_A, W_KV_A = 512, 128
WINDOW = 128
BLOCK = 128
N_HEADS_B, HEAD_DIM_B, W_B = 4, 128, 512
N_HEADS_C, DK_C, DV_C, WK_C, WV_C = 4, 128, 256, 512, 1024
GATE_RANK = 16
GATE_TEMP = 16.0
N_MEM, N_HEADS_M, HEAD_DIM_M, W_M = 256, 4, 128, 512
EPS = 1e-6
MASK_VALUE = -1e30
MIN_GATE = 1e-30
EVEN_IN, ODD_IN = 4864, 4128
ODD_PAD = 4224
MIX = 1536
ADAM_LR, ADAM_B1, ADAM_B2, ADAM_EPS, ADAM_WD, ADAM_STEP = 0.001, 0.9, 0.999, 1e-08, 0.01, 10

SCAN_CHUNK = 128
SCAN_SUB = 2
SCAN_LEVELS = 7
VMEM_LIMIT = 56 * 1024 * 1024

EVEN_REF_OFF = dict(qA=0, kA=512, vA=640, gA=768, qB=1280, zf=1792, zb=2304, iB=2816, gB=3328, qM=3840, gM=4352)
EVEN_W = dict(qA=512, kA=128, vA=128, gA=512, qB=512, zf=512, zb=512, iB=512, gB=512, qM=512, gM=512)
EVEN_ORDER = ("qA", "gA", "qB", "zf", "zb", "iB", "gB", "qM", "gM", "kA", "vA")
ODD_REF_OFF = dict(qC=0, kC=512, vC=1024, gC=2048, rr=3072, qM=3104, gM=3616)
ODD_W = dict(qC=512, kC=512, vC=1024, gC=1024, rr=32, qM=512, gM=512)
ODD_ORDER = ("qC", "kC", "vC", "gC", "qM", "gM", "rr")


def _offsets(order, widths):
    off, o = {}, 0
    for n in order:
        off[n] = o
        o += widths[n]
    return off


EVEN_OFF = _offsets(EVEN_ORDER, EVEN_W)
ODD_OFF = _offsets(ODD_ORDER, ODD_W)


def _dg(a, b, ca, cb):
    return lax.dot_general(a.astype(BF16), b.astype(BF16), (((ca,), (cb,)), ((), ())),
                           preferred_element_type=F32)


def dot_nn(a, b):
    return _dg(a, b, 1, 0)


def dot_nt(a, b):
    return _dg(a, b, 1, 1)


def dot_tn(a, b):
    return _dg(a, b, 0, 0)


@jax.custom_vjp
def bdot(a, b):
    return dot_nn(a, b)


bdot.defvjp(lambda a, b: (dot_nn(a, b), (a, b)),
            lambda r, g: (dot_nt(g, r[1]), dot_tn(r[0], g)))


@jax.custom_vjp
def bdot_t(a, b):
    return dot_nt(a, b)


bdot_t.defvjp(lambda a, b: (dot_nt(a, b), (a, b)),
              lambda r, g: (dot_nn(g, r[1]), dot_tn(g, r[0])))


@jax.custom_vjp
def bdot_tn(a, b):
    return dot_tn(a, b)


bdot_tn.defvjp(lambda a, b: (dot_tn(a, b), (a, b)),
               lambda r, g: (dot_nt(r[1], g), dot_nn(r[0], g)))


def _split_mm(h, x):
    hi = x.astype(BF16)
    lo = (x - hi.astype(F32)).astype(BF16)
    return (lax.dot_general(h, hi, (((1,), (0,)), ((), ())), preferred_element_type=F32)
            + lax.dot_general(h, lo, (((1,), (0,)), ((), ())), preferred_element_type=F32))


def _sigmoid(z):
    return 1.0 / (1.0 + jnp.exp(-z))


def _silu(z):
    return z * _sigmoid(z)


def _log_sigmoid(z):
    return jnp.minimum(z, 0.0) - jnp.log(1.0 + jnp.exp(-jnp.abs(z)))


def _rms(x, g):
    return x * lax.rsqrt(jnp.mean(x * x, axis=-1, keepdims=True) + EPS) * g


def rms_tile(x, g):
    return (_rms(x, g),)


@functools.partial(jax.custom_vjp, nondiff_argnums=(1, 2))
def split(x, n, axis):
    w = x.shape[axis] // n
    return tuple(lax.slice_in_dim(x, h * w, (h + 1) * w, axis=axis) for h in range(n))


split.defvjp(lambda x, n, axis: (split(x, n, axis), None),
             lambda n, axis, _, cts: (jnp.concatenate(cts, axis=axis),))


def _group_rms(o, g, heads):
    return jnp.concatenate([_rms(oh, gh) for oh, gh in zip(split(o, heads, 1), split(g, heads, 1))], axis=-1)


def even_post_tile(a, o2f, o2b, mo, gA, gB, gM, hg):
    y = _group_rms(o2f + o2b, hg, N_HEADS_B)
    return (jnp.concatenate([a * _silu(gA), y * _silu(gB), mo * _silu(gM)], axis=-1),)


def odd_post_tile(o2f, o2b, mo, gC, gM, gg):
    y = _group_rms(o2f + o2b, gg, N_HEADS_C)
    return (jnp.concatenate([y * _silu(gC), mo * _silu(gM)], axis=-1),)


def hgrn_prep(raw, par):
    qB, z, iB = raw
    (lb,) = par
    f = lb + (1.0 - lb) * _sigmoid(z)
    return _silu(qB), (1.0 - lb) * _sigmoid(-z), iB, jnp.log(jnp.maximum(f, MIN_GATE))


def gla_prep(raw, par):
    qC, kC, vC, r128 = raw
    wup, bg = par
    return qC * (DK_C ** -0.5), kC, vC, _log_sigmoid(bdot(r128, wup) + bg) / GATE_TEMP


def mem_tile(q, k, v):
    s = bdot_t(q, k) * (HEAD_DIM_M ** -0.5)
    m = lax.stop_gradient(jnp.max(s, axis=-1, keepdims=True))
    p = jnp.exp(s - m)
    p = p / jnp.sum(p, axis=-1, keepdims=True)
    return (bdot(p, v),)


ATTN_GROUP = N_Q_A // N_KV_A


def attn_block(q, ks, vs, sink, slope, c, seq):
    rows = ATTN_GROUP * BLOCK
    i = lax.broadcasted_iota(jnp.int32, (rows, 3 * BLOCK), 0) % BLOCK
    j = lax.broadcasted_iota(jnp.int32, (rows, 3 * BLOCK), 1)
    dist = jnp.abs(i - j + BLOCK).astype(F32)
    kpos = (c - 1) * BLOCK + j
    valid = (dist <= WINDOW) & (kpos >= 0) & (kpos < seq)
    s = bdot_t(q, ks) * (HEAD_DIM_A ** -0.5)
    s = jnp.where(valid, s - slope * dist, MASK_VALUE)
    m = lax.stop_gradient(jnp.maximum(jnp.max(s, axis=-1, keepdims=True), sink))
    p = jnp.where(valid, jnp.exp(s - m), 0.0)
    denom = jnp.sum(p, axis=-1, keepdims=True) + jnp.exp(sink - m)
    return bdot(p, vs) / denom


def scan_chunk(q, k, v, e, tot, st, qm, pm):
    C = SCAN_CHUNK
    e = split(e, 2 + SCAN_LEVELS, 0)
    qe = q * jnp.exp(e[0])
    kd = k * jnp.exp(e[1])
    r = lax.broadcasted_iota(jnp.int32, (C, C), 0)
    s = lax.broadcasted_iota(jnp.int32, (C, C), 1)
    a = jnp.where(r == s, jnp.sum(q * k, axis=-1, keepdims=True), 0.0)
    for l in range(SCAN_LEVELS):
        u = jnp.where(qm[l * C:(l + 1) * C] != 0.0, q, k) * jnp.exp(e[2 + l])
        a = a + bdot_t(u, u) * pm[l * C:(l + 1) * C]
    o = bdot_t(qe, st) + bdot(a, v)
    st_new = st * jnp.exp(tot) + bdot_tn(v, kd)
    return o, st_new


def _scan_consts():
    C, L = SCAN_CHUNK, SCAN_LEVELS
    t = np.arange(C)[:, None]
    r = np.arange(C)[None, :]
    blocks = [(r <= t), (r > t)]
    qms, pms = [], []
    for l in range(1, L + 1):
        m = C >> l
        upper_t = (t % (2 * m)) >= m
        upper_r = (r % (2 * m)) >= m
        same_half = (t // m) == (r // m)
        blocks.append(same_half & np.where(upper_t, r <= t, r > t))
        qms.append(np.broadcast_to(upper_t, (C, C)))
        pms.append(((t // (2 * m)) == (r // (2 * m))) & upper_t & ~upper_r)
    hf = np.concatenate(blocks, axis=0).astype(np.float32)
    flip = lambda mat: mat.reshape(-1, C, C)[:, ::-1, ::-1].reshape(-1, C)
    qmf = np.concatenate(qms, axis=0).astype(np.float32)
    pmf = np.concatenate(pms, axis=0).astype(np.float32)
    h = np.stack([hf, flip(hf)])
    ht = np.stack([h[0].T, h[1].T])
    qm = np.stack([qmf, 1.0 - qmf])
    pm = np.stack([pmf, flip(pmf)])
    return h, ht, qm, pm


def _cparams(sem):
    return pltpu.CompilerParams(dimension_semantics=sem, vmem_limit_bytes=VMEM_LIMIT)


def _row_tile(T):
    return min(T, 512)


def _in_spec(spec, tr):
    kind = spec[0]
    if kind == "row":
        _, arr, off, w = spec
        assert off % w == 0
        return arr, pl.BlockSpec((tr, w), functools.partial(lambda i, b: (i, b), b=off // w))
    if kind == "row3":
        _, arr, d, off, w = spec
        assert off % w == 0
        return arr, pl.BlockSpec((None, tr, w), functools.partial(lambda i, d, b: (d, i, b), d=d, b=off // w))
    _, arr = spec
    return arr, pl.BlockSpec(arr.shape, functools.partial(lambda i, n: (0,) * n, n=arr.ndim))


def rows_call(name, tile_fn, T, ins, out_widths, out_dtypes=None, stacks=None):
    tr = _row_tile(T)
    n_in = len(ins)
    out_dtypes = out_dtypes or [F32] * len(out_widths)
    stacks = stacks or [(k,) for k in range(len(out_widths))]

    def body(*refs):
        vals = [r[...] for r in refs[:n_in]]
        outs = tile_fn(*vals)
        for r, members in zip(refs[n_in:], stacks):
            if len(members) == 1:
                r[...] = outs[members[0]].astype(r.dtype)
            else:
                for d, k in enumerate(members):
                    r[d] = outs[k].astype(r.dtype)

    in_specs, args = [], []
    for spec in ins:
        arr, bs = _in_spec(spec, tr)
        args.append(arr)
        in_specs.append(bs)
    out_specs, out_shape = [], []
    for w, dt, members in zip(out_widths, out_dtypes, stacks):
        n = len(members)
        if n == 1:
            out_specs.append(pl.BlockSpec((tr, w), lambda i: (i, 0)))
            out_shape.append(jax.ShapeDtypeStruct((T, w), dt))
        else:
            out_specs.append(pl.BlockSpec((n, tr, w), lambda i: (0, i, 0)))
            out_shape.append(jax.ShapeDtypeStruct((n, T, w), dt))
    return pl.pallas_call(body, out_shape=out_shape, grid=(T // tr,), in_specs=in_specs, out_specs=out_specs,
                          name=name, compiler_params=_cparams(("arbitrary",)))(*args)


def rows_vjp_call(name, tile_fn, T, ins, cts, skip=(), narrow=()):
    tr = _row_tile(T)
    n_in = len(ins)
    n_ct = [len(c) for c in cts]
    want = [k for k in range(n_in) if k not in skip]

    def body(*refs):
        i = pl.program_id(0)
        vals = [r[...] for r in refs[:n_in]]
        ct, pos = [], n_in
        for n in n_ct:
            acc = refs[pos][...]
            for r in refs[pos + 1:pos + n]:
                acc = acc + r[...]
            ct.append(acc)
            pos += n
        _, vjp = jax.vjp(tile_fn, *vals)
        grads = vjp(tuple(ct))
        for r, k in zip(refs[pos:], want):
            if ins[k][0] == "full":
                @pl.when(i == 0)
                def _():
                    r[...] = jnp.zeros_like(r)
                r[...] += grads[k]
            else:
                r[...] = grads[k].astype(r.dtype)

    in_specs, args = [], []
    for spec in list(ins) + [s for c in cts for s in c]:
        arr, bs = _in_spec(spec, tr)
        args.append(arr)
        in_specs.append(bs)
    out_specs, out_shape = [], []
    for k in want:
        if ins[k][0] == "full":
            arr = ins[k][1]
            out_specs.append(pl.BlockSpec(arr.shape, functools.partial(lambda i, n: (0,) * n, n=arr.ndim)))
            out_shape.append(jax.ShapeDtypeStruct(arr.shape, F32))
        else:
            w = ins[k][-1]
            out_specs.append(pl.BlockSpec((tr, w), lambda i: (i, 0)))
            out_shape.append(jax.ShapeDtypeStruct((T, w), BF16 if k in narrow else F32))
    return pl.pallas_call(body, out_shape=out_shape, grid=(T // tr,), in_specs=in_specs, out_specs=out_specs,
                          name=name, compiler_params=_cparams(("arbitrary",)))(*args)


def matmul(name, a, b, mode, add=None, out_dtype=F32):
    if mode == "tn":
        K, M = a.shape
        N = b.shape[1]
        tm = M if M <= 1536 else 512
        tn = N if N <= 1280 else (N // 2 if (N // 2) % 128 == 0 else N)
        tk = min(K, 512)
        grid = (M // tm, N // tn, K // tk)

        def body(a_ref, b_ref, o_ref):
            @pl.when(pl.program_id(2) == 0)
            def _():
                o_ref[...] = jnp.zeros_like(o_ref)
            o_ref[...] += dot_tn(a_ref[...], b_ref[...])

        return pl.pallas_call(
            body, out_shape=jax.ShapeDtypeStruct((M, N), F32), grid=grid,
            in_specs=[pl.BlockSpec((tk, tm), lambda i, j, k: (k, i)), pl.BlockSpec((tk, tn), lambda i, j, k: (k, j))],
            out_specs=pl.BlockSpec((tm, tn), lambda i, j, k: (i, j)), name=name,
            compiler_params=_cparams(("arbitrary", "arbitrary", "arbitrary")))(a, b)

    M, K = a.shape
    N = b.shape[1] if mode == "nn" else b.shape[0]
    tm = min(M, 512)
    tn = N if N <= 1536 else (N // 2 if (N // 2) % 128 == 0 else (N // 3 if (N // 3) % 128 == 0 else N))
    grid = (N // tn, M // tm)
    n_in = 2 + (add is not None)

    def body(*refs):
        a_ref, b_ref = refs[0], refs[1]
        o_ref = refs[n_in]
        acc = dot_nn(a_ref[...], b_ref[...]) if mode == "nn" else dot_nt(a_ref[...], b_ref[...])
        if add is not None:
            acc = acc + refs[2][...]
        o_ref[...] = acc.astype(o_ref.dtype)

    in_specs = [pl.BlockSpec((tm, K), lambda j, i: (i, 0)),
                pl.BlockSpec((K, tn), lambda j, i: (0, j)) if mode == "nn" else pl.BlockSpec((tn, K), lambda j, i: (j, 0))]
    args = [a, b]
    if add is not None:
        in_specs.append(pl.BlockSpec((tm, tn), lambda j, i: (i, j)))
        args.append(add)
    return pl.pallas_call(
        body, out_shape=jax.ShapeDtypeStruct((M, N), out_dtype), grid=grid, in_specs=in_specs,
        out_specs=pl.BlockSpec((tm, tn), lambda j, i: (i, j)), name=name,
        compiler_params=_cparams(("arbitrary", "arbitrary")))(*args)


def _attn_heads(n):
    G = N_Q_A // N_KV_A
    k_sl = pl.ds(n * HEAD_DIM_A, HEAD_DIM_A)
    v_sl = pl.ds(W_KV_A + n * HEAD_DIM_A, HEAD_DIM_A)
    q_sl = [pl.ds((n * G + g) * HEAD_DIM_A, HEAD_DIM_A) for g in range(G)]
    return k_sl, v_sl, q_sl, range(n * G, (n + 1) * G)


def attn_fwd(p, q_off, kvp, sink, slopes, T):
    nb = T // BLOCK
    assert q_off % W_A == 0

    def body(q_ref, kv_ref, sink_ref, slope_ref, o_ref):
        c = pl.program_id(0)
        rows = pl.ds(pl.multiple_of(c * BLOCK, BLOCK), 3 * BLOCK)
        for n in range(N_KV_A):
            k_sl, v_sl, q_sl, heads = _attn_heads(n)
            group = pl.ds(n * ATTN_GROUP * BLOCK, ATTN_GROUP * BLOCK)
            q = jnp.concatenate([q_ref[:, s] for s in q_sl], axis=0)
            o = attn_block(q, kv_ref[rows, k_sl], kv_ref[rows, v_sl], sink_ref[group, :], slope_ref[group, :], c, T)
            for g, s in enumerate(q_sl):
                o_ref[:, s] = o[g * BLOCK:(g + 1) * BLOCK]

    full = lambda a: pl.BlockSpec(a.shape, functools.partial(lambda c, nd: (0,) * nd, nd=a.ndim))
    return pl.pallas_call(
        body, out_shape=jax.ShapeDtypeStruct((T, W_A), F32), grid=(nb,),
        in_specs=[pl.BlockSpec((BLOCK, W_A), lambda c: (c, q_off // W_A)), full(kvp), full(sink), full(slopes)],
        out_specs=pl.BlockSpec((BLOCK, W_A), lambda c: (c, 0)),
        name="attn_fwd", compiler_params=_cparams(("arbitrary",)))(p, kvp, sink, slopes)


def attn_bwd(p, q_off, kvp, sink, slopes, do, T):
    nb = T // BLOCK

    def body(q_ref, kv_ref, sink_ref, slope_ref, do_ref, dq_ref, dkv_ref, dsink_ref):
        c = pl.program_id(0)

        @pl.when(c == 0)
        def _():
            dkv_ref[...] = jnp.zeros_like(dkv_ref)
            dsink_ref[...] = jnp.zeros_like(dsink_ref)

        rows = pl.ds(pl.multiple_of(c * BLOCK, BLOCK), 3 * BLOCK)
        for n in range(N_KV_A):
            k_sl, v_sl, q_sl, heads = _attn_heads(n)
            group = pl.ds(n * ATTN_GROUP * BLOCK, ATTN_GROUP * BLOCK)
            slope = slope_ref[group, :]
            q = jnp.concatenate([q_ref[:, s] for s in q_sl], axis=0)
            do = jnp.concatenate([do_ref[:, s] for s in q_sl], axis=0)
            _, vjp = jax.vjp(lambda q_, kk, vv, sk: attn_block(q_, kk, vv, sk, slope, c, T),
                             q, kv_ref[rows, k_sl], kv_ref[rows, v_sl], sink_ref[group, :])
            dq, dks, dvs, dsk = vjp(do)
            dkv_ref[rows, k_sl] += dks
            dkv_ref[rows, v_sl] += dvs
            for g, (s, h) in enumerate(zip(q_sl, heads)):
                seg = slice(g * BLOCK, (g + 1) * BLOCK)
                dq_ref[:, s] = dq[seg].astype(dq_ref.dtype)
                dsink_ref[h] += jnp.sum(dsk[seg], axis=0, keepdims=True)

    full = lambda a: pl.BlockSpec(a.shape, functools.partial(lambda c, nd: (0,) * nd, nd=a.ndim))
    qspec = pl.BlockSpec((BLOCK, W_A), lambda c: (c, 0))
    return pl.pallas_call(
        body,
        out_shape=[jax.ShapeDtypeStruct((T, W_A), BF16), jax.ShapeDtypeStruct(kvp.shape, F32),
                   jax.ShapeDtypeStruct((N_Q_A, 1, 1), F32)],
        grid=(nb,),
        in_specs=[pl.BlockSpec((BLOCK, W_A), lambda c: (c, q_off // W_A)), full(kvp), full(sink), full(slopes), qspec],
        out_specs=[qspec, full(kvp), pl.BlockSpec((N_Q_A, 1, 1), lambda c: (0, 0, 0))],
        name="attn_bwd", compiler_params=_cparams(("arbitrary",)))(p, kvp, sink, slopes, do)


def mem_fwd(p, q_off, kv, T):
    tr = min(T, 2 * _row_tile(T))
    assert q_off % W_M == 0

    def body(q_ref, kv_ref, o_ref):
        for h in range(N_HEADS_M):
            hs = pl.ds(h * HEAD_DIM_M, HEAD_DIM_M)
            (o,) = mem_tile(q_ref[:, hs], kv_ref[:, hs], kv_ref[:, pl.ds(W_M + h * HEAD_DIM_M, HEAD_DIM_M)])
            o_ref[:, hs] = o

    return pl.pallas_call(
        body, out_shape=jax.ShapeDtypeStruct((T, W_M), F32), grid=(T // tr,),
        in_specs=[pl.BlockSpec((tr, W_M), lambda i: (i, q_off // W_M)), pl.BlockSpec((N_MEM, 2 * W_M), lambda i: (0, 0))],
        out_specs=pl.BlockSpec((tr, W_M), lambda i: (i, 0)),
        name="mem_fwd", compiler_params=_cparams(("arbitrary",)))(p, kv)


def mem_bwd(p, q_off, kv, do, T):
    tr = min(T, 2 * _row_tile(T))

    def body(q_ref, kv_ref, do_ref, dq_ref, dkv_ref):
        @pl.when(pl.program_id(0) == 0)
        def _():
            dkv_ref[...] = jnp.zeros_like(dkv_ref)

        for h in range(N_HEADS_M):
            hs = pl.ds(h * HEAD_DIM_M, HEAD_DIM_M)
            vs = pl.ds(W_M + h * HEAD_DIM_M, HEAD_DIM_M)
            _, vjp = jax.vjp(mem_tile, q_ref[:, hs], kv_ref[:, hs], kv_ref[:, vs])
            dq, dk, dv = vjp((do_ref[:, hs],))
            dq_ref[:, hs] = dq.astype(dq_ref.dtype)
            dkv_ref[:, hs] += dk
            dkv_ref[:, vs] += dv

    kvspec = pl.BlockSpec((N_MEM, 2 * W_M), lambda i: (0, 0))
    return pl.pallas_call(
        body,
        out_shape=[jax.ShapeDtypeStruct((T, W_M), BF16), jax.ShapeDtypeStruct((N_MEM, 2 * W_M), F32)],
        grid=(T // tr,),
        in_specs=[pl.BlockSpec((tr, W_M), lambda i: (i, q_off // W_M)), kvspec, pl.BlockSpec((tr, W_M), lambda i: (i, 0))],
        out_specs=[pl.BlockSpec((tr, W_M), lambda i: (i, 0)), kvspec],
        name="mem_bwd", compiler_params=_cparams(("arbitrary",)))(p, kv, do)


def _scan_const_specs(dk):
    C, L = SCAN_CHUNK, SCAN_LEVELS
    return [pl.BlockSpec((2, (2 + L) * C, C), lambda n: (0, 0, 0)),
            pl.BlockSpec((2, C, (2 + L) * C), lambda n: (0, 0, 0)),
            pl.BlockSpec((2, L * C, dk), lambda n: (0, 0, 0)),
            pl.BlockSpec((2, L * C, C), lambda n: (0, 0, 0))]


def _chunk_spec(src, width, chunk_of):
    arr, sel = src
    if arr.ndim == 2:
        assert sel % width == 0
        return pl.BlockSpec((SCAN_CHUNK * SCAN_SUB, width),
                            functools.partial(lambda n, b: (chunk_of(n), b), b=sel // width))
    return pl.BlockSpec((None, SCAN_CHUNK * SCAN_SUB, width), functools.partial(lambda n, d: (d, chunk_of(n), 0), d=sel))


def _scan_const_args():
    h, ht, qm, pm = _scan_consts()
    return [jnp.asarray(h, BF16), jnp.asarray(ht, BF16), jnp.asarray(qm, F32), jnp.asarray(pm, F32)]


def _full_spec(a):
    return pl.BlockSpec(a.shape, functools.partial(lambda n, nd: (0,) * nd, nd=a.ndim))


def scan_fwd(name, prep, raws, params, heads, dk, dv, T):
    C, S = SCAN_CHUNK, SCAN_SUB
    N = T // (C * S)
    assert dk == C
    Wv = heads * dv
    orders = (lambda n: n, lambda n: N - 1 - n)
    n_raw, n_par = [len(r) for r in raws], [len(p) for p in params]

    def body(*refs):
        pos, raw_refs, par_refs = 0, [], []
        for d in range(2):
            raw_refs.append(refs[pos:pos + n_raw[d]])
            pos += n_raw[d]
        for d in range(2):
            par_refs.append(refs[pos:pos + n_par[d]])
            pos += n_par[d]
        h_ref, ht_ref, qm_ref, pm_ref = refs[pos:pos + 4]
        o_refs, ss_refs, st_ref = refs[pos + 4:pos + 6], refs[pos + 6:pos + 8], refs[pos + 8]

        @pl.when(pl.program_id(0) == 0)
        def _():
            st_ref[...] = jnp.zeros_like(st_ref)

        for d in range(2):
            consts = (qm_ref[d], pm_ref[d])
            pars = [p[...] for p in par_refs[d]]
            for sub in (range(S) if d == 0 else reversed(range(S))):
                rows = pl.ds(sub * C, C)
                q, k, v, g = prep([r[rows, :] for r in raw_refs[d]], pars)
                e = _split_mm(h_ref[d], g)
                tot = jnp.sum(g, axis=0, keepdims=True)
                for h in range(heads):
                    ks, vs = slice(h * dk, (h + 1) * dk), slice(h * dv, (h + 1) * dv)
                    st = st_ref[d, h]
                    ss_refs[d][h, sub] = st
                    o, st_new = scan_chunk(q[:, ks], k[:, ks], v[:, vs], e[:, ks], tot[:, ks], st, *consts)
                    o_refs[d][rows, vs] = o
                    st_ref[d, h] = st_new

    ss_spec = lambda order: pl.BlockSpec((heads, S, dv, dk), lambda n: (0, order(n), 0, 0))
    return pl.pallas_call(
        body,
        out_shape=[jax.ShapeDtypeStruct((T, Wv), F32)] * 2 + [jax.ShapeDtypeStruct((heads, T // C, dv, dk), F32)] * 2,
        grid=(N,),
        in_specs=[_chunk_spec(s, w, orders[d]) for d in range(2) for s, w in raws[d]]
        + [_full_spec(p) for d in range(2) for p in params[d]] + _scan_const_specs(dk),
        out_specs=[pl.BlockSpec((C * S, Wv), lambda n: (orders[0](n), 0)),
                   pl.BlockSpec((C * S, Wv), lambda n: (orders[1](n), 0)), ss_spec(orders[0]), ss_spec(orders[1])],
        scratch_shapes=[pltpu.VMEM((2, heads, dv, dk), F32)],
        name=name, compiler_params=_cparams(("arbitrary",)))(
            *[s[0] for d in range(2) for s, _ in raws[d]], *[p for d in range(2) for p in params[d]], *_scan_const_args())


def scan_bwd(name, prep, raws, params, ss, do, heads, dk, dv, T):
    C, S = SCAN_CHUNK, SCAN_SUB
    N = T // (C * S)
    Wv = heads * dv
    orders = (lambda n: N - 1 - n, lambda n: n)
    n_raw, n_par = [len(r) for r in raws], [len(p) for p in params]

    def body(*refs):
        pos, raw_refs, par_refs, draw_refs, dpar_refs = 0, [], [], [], []
        for group, counts in ((raw_refs, n_raw), (par_refs, n_par)):
            for d in range(2):
                group.append(refs[pos:pos + counts[d]])
                pos += counts[d]
        ss_refs, do_refs = refs[pos:pos + 2], refs[pos + 2:pos + 4]
        h_ref, ht_ref, qm_ref, pm_ref = refs[pos + 4:pos + 8]
        pos += 8
        for group, counts in ((draw_refs, n_raw), (dpar_refs, n_par)):
            for d in range(2):
                group.append(refs[pos:pos + counts[d]])
                pos += counts[d]
        dst_ref = refs[pos]

        @pl.when(pl.program_id(0) == 0)
        def _():
            dst_ref[...] = jnp.zeros_like(dst_ref)
            for d in range(2):
                for r in dpar_refs[d]:
                    r[...] = jnp.zeros_like(r)

        for d in range(2):
            consts = (qm_ref[d], pm_ref[d])
            pars = [p[...] for p in par_refs[d]]
            for sub in (reversed(range(S)) if d == 0 else range(S)):
                rows = pl.ds(sub * C, C)
                (q, k, v, g), prep_vjp = jax.vjp(prep, [r[rows, :] for r in raw_refs[d]], pars)
                e = _split_mm(h_ref[d], g)
                tot = jnp.sum(g, axis=0, keepdims=True)
                dqs, dks, dvs, des, dtots = [], [], [], [], []
                for h in range(heads):
                    ks, vs = slice(h * dk, (h + 1) * dk), slice(h * dv, (h + 1) * dv)
                    _, vjp = jax.vjp(lambda q_, k_, v_, e_, t_, st_: scan_chunk(q_, k_, v_, e_, t_, st_, *consts),
                                     q[:, ks], k[:, ks], v[:, vs], e[:, ks], tot[:, ks], ss_refs[d][h, sub])
                    dq, dk_, dv_, de, dtot, dst = vjp((do_refs[d][rows, vs], dst_ref[d, h]))
                    dst_ref[d, h] = dst
                    for group, val in ((dqs, dq), (dks, dk_), (dvs, dv_), (des, de), (dtots, dtot)):
                        group.append(val)
                cat = lambda parts: jnp.concatenate(parts, axis=-1)
                dg = _split_mm(ht_ref[d], cat(des)) + cat(dtots)
                draws, dpars = prep_vjp((cat(dqs), cat(dks), cat(dvs), dg))
                for r, val in zip(draw_refs[d], draws):
                    r[rows, :] = val.astype(r.dtype)
                for r, val in zip(dpar_refs[d], dpars):
                    r[...] += val

    ss_spec = lambda order: pl.BlockSpec((heads, S, dv, dk), lambda n: (0, order(n), 0, 0))
    row_out = lambda w, order: pl.BlockSpec((C * S, w), lambda n: (order(n), 0))
    return pl.pallas_call(
        body,
        out_shape=[jax.ShapeDtypeStruct((T, w), BF16) for d in range(2) for _, w in raws[d]]
        + [jax.ShapeDtypeStruct(p.shape, F32) for d in range(2) for p in params[d]],
        grid=(N,),
        in_specs=[_chunk_spec(s, w, orders[d]) for d in range(2) for s, w in raws[d]]
        + [_full_spec(p) for d in range(2) for p in params[d]]
        + [ss_spec(orders[0]), ss_spec(orders[1]), _chunk_spec(do, Wv, orders[0]), _chunk_spec(do, Wv, orders[1])]
        + _scan_const_specs(dk),
        out_specs=[row_out(w, orders[d]) for d in range(2) for _, w in raws[d]]
        + [_full_spec(p) for d in range(2) for p in params[d]],
        scratch_shapes=[pltpu.VMEM((2, heads, dv, dk), F32)],
        name=name, compiler_params=_cparams(("arbitrary",)))(
            *[s[0] for d in range(2) for s, _ in raws[d]], *[p for d in range(2) for p in params[d]],
            ss[0], ss[1], do[0], do[0], *_scan_const_args())


def final_call(x, g, target, T):
    tr = _row_tile(T)

    def tile(xv, gv, tv):
        y = _rms(xv, gv)
        err = (y - tv) ** 2
        return jnp.sum(jnp.sum(err, axis=-1, keepdims=True), axis=0, keepdims=True) * (0.5 / D_MODEL)

    def body(x_ref, g_ref, t_ref, loss_ref, dx_ref, dg_ref):
        i = pl.program_id(0)
        tv = t_ref[...]
        lv, vjp = jax.vjp(lambda a, b: tile(a, b, tv), x_ref[...], g_ref[...])
        dx, dg = vjp(jnp.ones((1, 1), F32))
        dx_ref[...] = dx

        @pl.when(i == 0)
        def _():
            loss_ref[...] = jnp.zeros_like(loss_ref)
            dg_ref[...] = jnp.zeros_like(dg_ref)

        loss_ref[...] += jnp.broadcast_to(lv, loss_ref.shape)
        dg_ref[...] += dg

    return pl.pallas_call(
        body,
        out_shape=[jax.ShapeDtypeStruct((8, 128), F32), jax.ShapeDtypeStruct((T, D_MODEL), F32),
                   jax.ShapeDtypeStruct((1, D_MODEL), F32)],
        grid=(T // tr,),
        in_specs=[pl.BlockSpec((tr, D_MODEL), lambda i: (i, 0)), pl.BlockSpec((1, D_MODEL), lambda i: (0, 0)),
                  pl.BlockSpec((tr, D_MODEL), lambda i: (i, 0))],
        out_specs=[pl.BlockSpec((8, 128), lambda i: (0, 0)), pl.BlockSpec((tr, D_MODEL), lambda i: (i, 0)),
                   pl.BlockSpec((1, D_MODEL), lambda i: (0, 0))],
        name="final_loss", compiler_params=_cparams(("arbitrary",)))(x, g, target)


def adamw_call(w, g, m, v):
    shape = w.shape
    c = shape[-1]
    r = int(np.prod(shape[:-1])) if len(shape) > 1 else 1
    tr = r if r <= 256 else 256
    assert r % tr == 0

    def body(w_ref, g_ref, m_ref, v_ref, d_ref, nm_ref, nv_ref):
        gv = g_ref[...]
        nm = ADAM_B1 * m_ref[...] + (1.0 - ADAM_B1) * gv
        nv = ADAM_B2 * v_ref[...] + (1.0 - ADAM_B2) * jnp.square(gv)
        m_hat = nm / (1.0 - ADAM_B1 ** ADAM_STEP)
        v_hat = nv / (1.0 - ADAM_B2 ** ADAM_STEP)
        d_ref[...] = -ADAM_LR * (m_hat / (jnp.sqrt(v_hat) + ADAM_EPS) + ADAM_WD * w_ref[...])
        nm_ref[...] = nm
        nv_ref[...] = nv

    spec = pl.BlockSpec((tr, c), lambda i: (i, 0))
    outs = pl.pallas_call(body, out_shape=[jax.ShapeDtypeStruct((r, c), F32)] * 3, grid=(r // tr,),
                          in_specs=[spec] * 4, out_specs=[spec] * 3, name="adamw",
                          compiler_params=_cparams(("arbitrary",)))(*(t.reshape(r, c) for t in (w, g, m, v)))
    return tuple(o.reshape(shape) for o in outs)


def adamw_halves(w, mine, other, m, v, c):
    L, R, C = w.shape
    by_cols = mine.shape[-1] != C
    if by_cols:
        tile, nbh = (R, C // 2), 1
        full_idx = lambda l, i: (l, 0, i)
    else:
        rh = R // 2
        tr = rh if rh <= 256 else rh // 2
        assert tr % 8 == 0
        tile, nbh = (tr, C), rh // tr
        full_idx = lambda l, i: (l, i, 0)

    def body(c_ref, w_ref, a_ref, b_ref, m_ref, v_ref, g_ref, d_ref, nm_ref, nv_ref):
        is_mine = (pl.program_id(1) // nbh) == c_ref[0]
        gv = jnp.where(is_mine, a_ref[...], b_ref[...])
        nm = ADAM_B1 * m_ref[...] + (1.0 - ADAM_B1) * gv
        nv = ADAM_B2 * v_ref[...] + (1.0 - ADAM_B2) * jnp.square(gv)
        m_hat = nm / (1.0 - ADAM_B1 ** ADAM_STEP)
        v_hat = nv / (1.0 - ADAM_B2 ** ADAM_STEP)
        g_ref[...] = gv
        d_ref[...] = -ADAM_LR * (m_hat / (jnp.sqrt(v_hat) + ADAM_EPS) + ADAM_WD * w_ref[...])
        nm_ref[...] = nm
        nv_ref[...] = nv

    full = pl.BlockSpec((None,) + tile, lambda l, i, c_ref: full_idx(l, i))
    half = pl.BlockSpec((None,) + tile, lambda l, i, c_ref: (l, i % nbh, 0))
    grid_spec = pltpu.PrefetchScalarGridSpec(num_scalar_prefetch=1, grid=(L, 2 * nbh),
                                             in_specs=[full, half, half, full, full], out_specs=[full] * 4)
    return pl.pallas_call(body, out_shape=[jax.ShapeDtypeStruct(w.shape, F32)] * 4, grid_spec=grid_spec,
                          name="adamw_halves", compiler_params=_cparams(("arbitrary", "arbitrary")))(c, w, mine, other, m, v)


def sum_devices(g64):
    def body(x_ref, o_ref):
        acc = x_ref[0:8, :]
        for d in range(1, 8):
            acc = acc + x_ref[8 * d:8 * d + 8, :]
        o_ref[...] = acc

    return pl.pallas_call(body, out_shape=jax.ShapeDtypeStruct((8, D_MODEL), F32), name="sum_devices")(g64)


def _half_tile(rh):
    if rh <= 512:
        return rh
    return next(rh // d for d in range(2, rh) if rh % d == 0 and (rh // d) % 16 == 0 and rh // d <= 512)


def _half_geometry(full_shape, half_shape):
    R, C = full_shape[-2:]
    if half_shape[-1] != C:
        return (R, C // 2), 1, lambda i, c: (0, c)
    tr = _half_tile(R // 2)
    nblk = (R // 2) // tr
    return (tr, C), nblk, lambda i, c: (i + c * nblk, 0)


def add_sibling(g, recv, c, out_dtype):
    tile, nblk, own = _half_geometry(g.shape, recv.shape)

    def body(c_ref, g_ref, r_ref, o_ref):
        o_ref[...] = (g_ref[...] + r_ref[...]).astype(o_ref.dtype)

    half = pl.BlockSpec((None,) + tile, lambda j, i, c_ref: (j, i, 0))
    grid_spec = pltpu.PrefetchScalarGridSpec(
        num_scalar_prefetch=1, grid=(4, nblk),
        in_specs=[pl.BlockSpec((None,) + tile, lambda j, i, c_ref: (j,) + own(i, c_ref[0])), half], out_specs=half)
    return pl.pallas_call(body, out_shape=jax.ShapeDtypeStruct(recv.shape, out_dtype), grid_spec=grid_spec,
                          name="rs_add_sibling", compiler_params=_cparams(("arbitrary", "arbitrary")))(c, g, recv)


def add_chips(g, recv, r3, place):
    tile, nblk, own = _half_geometry(g.shape, recv.shape)

    def body(p_ref, g_ref, s_ref, a_ref, b_ref, c_ref, o_ref):
        up = lambda r: r[...].astype(F32)
        o_ref[...] = (((g_ref[...] + up(s_ref)) + up(a_ref)) + up(b_ref)) + up(c_ref)

    grid_spec = pltpu.PrefetchScalarGridSpec(
        num_scalar_prefetch=1, grid=(nblk,),
        in_specs=[pl.BlockSpec((None,) + tile, lambda i, p_ref: (p_ref[0],) + own(i, p_ref[1])),
                  pl.BlockSpec((None,) + tile, lambda i, p_ref: (p_ref[0], i, 0))]
        + [pl.BlockSpec((None,) + tile, functools.partial(lambda i, p_ref, k: (k, i, 0), k=k)) for k in range(3)],
        out_specs=pl.BlockSpec(tile, lambda i, p_ref: (i, 0)))
    return pl.pallas_call(body, out_shape=jax.ShapeDtypeStruct(recv.shape[1:], F32), grid_spec=grid_spec,
                          name="rs_add_chips", compiler_params=_cparams(("arbitrary",)))(place, g, recv, r3, r3, r3)


def _remote(src, dst, ssem, rsem, dev):
    return pltpu.make_async_remote_copy(src_ref=src, dst_ref=dst, send_sem=ssem, recv_sem=rsem,
                                        device_id=dev, device_id_type=pl.DeviceIdType.MESH)


def _mesh_places():
    x, y, c = lax.axis_index("x"), lax.axis_index("y"), lax.axis_index("c")
    chips = [(1 - x, y), (x, 1 - y), (1 - x, 1 - y)]
    return x, y, c, (x, y, 1 - c), chips


def _hbm_specs(n):
    return [pl.BlockSpec(memory_space=pltpu.HBM) for _ in range(n)]


def _gather_body(ins, outs, n_split, send_sems, recv_sems, handshake):
    x, y, c, sibling, chips = _mesh_places()
    mine = 2 * x + y
    if handshake:
        barrier = pltpu.get_barrier_semaphore()
        peers = [sibling] + [(*chip, c) for chip in chips]
        for peer in peers:
            pl.semaphore_signal(barrier, inc=1, device_id=peer, device_id_type=pl.DeviceIdType.MESH)
        pl.semaphore_wait(barrier, len(peers))

    def half(a, chip_idx, which):
        rh = ins[a].shape[0] // 2
        return outs[a].at[chip_idx, pl.ds(which * rh, rh), :]

    sent = []
    for a in range(len(ins)):
        for k, chip in enumerate(chips):
            if a < n_split:
                rh = ins[a].shape[0] // 2
                src, dst = ins[a].at[pl.ds(c * rh, rh), :], half(a, mine, c)
            else:
                src, dst = ins[a], outs[a].at[mine]
            sent.append(_remote(src, dst, send_sems.at[a, k], recv_sems.at[a, k], (*chip, c)))
    for cp in sent:
        cp.start()
    for a in range(len(ins)):
        for k, chip in enumerate(chips):
            j = 2 * chip[0] + chip[1]
            region = half(a, j, c) if a < n_split else outs[a].at[j]
            _remote(region, region, send_sems.at[a, k], recv_sems.at[a, k], (*chip, c)).wait_recv()
            if a < n_split:
                fwd = _remote(region, region, send_sems.at[a, 3 + k], recv_sems.at[a, 3 + k], sibling)
                fwd.start()
                sent.append(fwd)
    for a in range(n_split):
        for k, chip in enumerate(chips):
            region = half(a, 2 * chip[0] + chip[1], 1 - c)
            _remote(region, region, send_sems.at[a, 3 + k], recv_sems.at[a, 3 + k], sibling).wait_recv()
    for cp in sent:
        cp.wait_send()


def gather_weights(shards, small):
    arrs = list(shards) + [small]
    n = len(arrs)

    def body(*refs):
        _gather_body(refs[:n], refs[n:2 * n], n - 1, refs[2 * n], refs[2 * n + 1], handshake=False)

    return pl.pallas_call(
        body, out_shape=[jax.ShapeDtypeStruct((4,) + a.shape, a.dtype) for a in arrs],
        in_specs=_hbm_specs(n), out_specs=_hbm_specs(n),
        scratch_shapes=[pltpu.SemaphoreType.DMA((n, 6)), pltpu.SemaphoreType.DMA((n, 6))],
        name="gather_weights")(*arrs)


def gather_weights_async(shards):
    n = len(shards)

    def body(*refs):
        _gather_body(refs[:n], refs[n:2 * n], n, refs[2 * n], refs[2 * n + 1], handshake=True)

    return pl.kernel(
        body, out_type=[jax.ShapeDtypeStruct((4,) + a.shape, a.dtype) for a in shards],
        mesh=plsc.ScalarSubcoreMesh(axis_name="seq", num_cores=1),
        scratch_types=[pltpu.SemaphoreType.DMA((n, 6)), pltpu.SemaphoreType.DMA((n, 6))],
        compiler_params=pltpu.CompilerParams(collective_id=1), name="gather_weights_async")(*shards)


def _sequencer_call(name, body, out_type, sem_shape, collective_id, args):
    return pl.kernel(
        body, out_type=out_type, mesh=plsc.ScalarSubcoreMesh(axis_name="seq", num_cores=1),
        scratch_types=[pltpu.SemaphoreType.DMA(sem_shape), pltpu.SemaphoreType.DMA(sem_shape)],
        compiler_params=pltpu.CompilerParams(collective_id=collective_id), name=name)(*args)


def _handshake(peers):
    barrier = pltpu.get_barrier_semaphore()
    for peer in peers:
        pl.semaphore_signal(barrier, inc=1, device_id=peer, device_id_type=pl.DeviceIdType.MESH)
    pl.semaphore_wait(barrier, len(peers))


def exchange_siblings(name, srcs, axes, collective_id):
    n = len(srcs)

    def body(*refs):
        ins, outs = refs[:n], refs[n:2 * n]
        send_sems, recv_sems = refs[2 * n:]
        x, y, c, sibling, chips = _mesh_places()
        _handshake([sibling])
        cps = []
        for a in range(n):
            src = ins[a]
            if axes[a] is not None:
                half = src.shape[axes[a]] // 2
                theirs = pl.ds((1 - c) * half, half)
                src = src.at[:, theirs, :] if axes[a] == 1 else src.at[:, :, theirs]
            cps.append(_remote(src, outs[a], send_sems.at[a], recv_sems.at[a], sibling))
        for cp in cps:
            cp.start()
        for cp in cps:
            cp.wait()

    def shape(g, axis):
        return g.shape if axis is None else tuple(d // 2 if k == axis else d for k, d in enumerate(g.shape))

    return _sequencer_call(name, body, [jax.ShapeDtypeStruct(shape(g, ax), g.dtype) for g, ax in zip(srcs, axes)],
                           (n,), collective_id, srcs)


def exchange_chips(name, s1s, collective_id):
    n = len(s1s)

    def body(*refs):
        ins, outs = refs[:n], refs[n:2 * n]
        send_sems, recv_sems = refs[2 * n:]
        x, y, c, sibling, chips = _mesh_places()
        _handshake([(*chip, c) for chip in chips])
        cps = []
        for a in range(n):
            for k, chip in enumerate(chips):
                cps.append(_remote(ins[a].at[2 * chip[0] + chip[1]], outs[a].at[k], send_sems.at[a, k],
                                   recv_sems.at[a, k], (*chip, c)))
        for cp in cps:
            cp.start()
        for cp in cps:
            cp.wait()

    return _sequencer_call(name, body, [jax.ShapeDtypeStruct((3,) + s.shape[1:], s.dtype) for s in s1s], (n, 3),
                           collective_id, s1s)


def allgather_small(v):
    m_per = v.shape[0]

    def body(x_ref, out_ref, send_sems, recv_sems, local_sem):
        x, y, c, sibling, chips = _mesh_places()
        me = (x, y, c)

        def rows(px, py, pc):
            return out_ref.at[pl.ds((4 * px + 2 * py + pc) * m_per, m_per), :]

        def copy(k, block, to, src=None):
            return _remote(rows(*block) if src is None else src, rows(*block), send_sems.at[k], recv_sems.at[k], to)

        mine = pltpu.make_async_copy(x_ref, rows(*me), local_sem)
        mine.start()
        first = [copy(0, me, sibling, src=x_ref)]
        first += [copy(1 + j, me, (*chip, c), src=x_ref) for j, chip in enumerate(chips)]
        for cp in first:
            cp.start()
        passed = [copy(4 + j, (*chip, c), sibling) for j, chip in enumerate(chips)]
        for j, chip in enumerate(chips):
            copy(1 + j, (*chip, c), me).wait_recv()
            passed[j].start()
        copy(0, sibling, me).wait_recv()
        for j, chip in enumerate(chips):
            copy(4 + j, (*chip, 1 - c), me).wait_recv()
        for cp in first + passed:
            cp.wait_send()
        mine.wait()

    return pl.pallas_call(
        body, out_shape=jax.ShapeDtypeStruct((8 * m_per, v.shape[1]), v.dtype),
        in_specs=[pl.BlockSpec(memory_space=pltpu.VMEM)], out_specs=pl.BlockSpec(memory_space=pltpu.VMEM),
        scratch_shapes=[pltpu.SemaphoreType.DMA((7,)), pltpu.SemaphoreType.DMA((7,)), pltpu.SemaphoreType.DMA],
        name="allgather_small")(v)


def rms_res_tile(x, g):
    return (_rms(x, g), x)


def _lower_bounds(lb_param):
    lbs = jax.nn.softmax(lb_param.astype(F32), axis=0)
    return jnp.cumsum(lbs, axis=0) - lbs[0]


def _even_fwd(x, i, W, lower, kv, slopes, T):
    O = EVEN_OFF
    g = W["norm_even"][i].reshape(1, D_MODEL)
    (h,) = rows_call("rms_fwd", rms_tile, T, [("row", x, 0, D_MODEL), ("full", g)], [D_MODEL], [BF16])
    p = matmul("mm_in_e", h, W["w_in_e"][i], "nn")
    kvp = jnp.pad(p[:, O["kA"]:O["kA"] + 2 * W_KV_A], ((BLOCK, BLOCK), (0, 0)))
    sink = jnp.repeat(W["sink"][i], BLOCK).reshape(N_Q_A * BLOCK, 1)
    a = attn_fwd(p, O["qA"], kvp, sink, slopes, T)
    scan_raws = [[((p, O["qB"]), W_B), ((p, O[z]), W_B), ((p, O["iB"]), W_B)] for z in ("zf", "zb")]
    scan_pars = [[lower[i][0:1]], [lower[i][1:2]]]
    o_f, o_b, ss_f, ss_b = scan_fwd("scan_fwd_h", hgrn_prep, scan_raws, scan_pars, N_HEADS_B, HEAD_DIM_B, HEAD_DIM_B, T)
    mo = mem_fwd(p, O["qM"], kv, T)
    hg = W["hgrn_norm"][i].reshape(1, W_B)
    post_ins = [("row", a, 0, W_A), ("row", o_f, 0, W_B), ("row", o_b, 0, W_B), ("row", mo, 0, W_M),
                ("row", p, O["gA"], W_A), ("row", p, O["gB"], W_B), ("row", p, O["gM"], W_M), ("full", hg)]
    (mix,) = rows_call("even_post_fwd", even_post_tile, T, post_ins, [MIX], [BF16])
    x_new = matmul("mm_out", mix, W["w_out_e"][i], "nn", add=x)
    return x_new, dict(x=x, g=g, h=h, p=p, kvp=kvp, sink=sink, scan_raws=scan_raws, scan_pars=scan_pars,
                       ss=(ss_f, ss_b), post_ins=post_ins, mix=mix)


def _add2(a, b):
    return a.astype(F32) + b.astype(F32)


def _assemble_even(dqA, dgA, dqB_f, dqB_b, dzf, dzb, diB_f, diB_b, dgB, dqM, dgM, dkvA):
    parts = [dqA, dgA, _add2(dqB_f, dqB_b), dzf, dzb, _add2(diB_f, diB_b), dgB, dqM, dgM, dkvA]
    return (jnp.concatenate([t.astype(BF16) for t in parts], axis=-1),)


def _even_bwd(dxo, sv, i, W, kv, slopes, T, sync):
    O = EVEN_OFF
    p = sv["p"]
    dmix = matmul("mm_dmix", dxo, W["w_out_e"][i], "nt")
    dwo = matmul("mm_dwo", sv["mix"], dxo, "tn")
    da, dof, dmo, dgA, dgB, dgM, dhg = rows_vjp_call("even_post_bwd", even_post_tile, T, sv["post_ins"],
                                                      [[("row", dmix, 0, MIX)]], skip=(2,), narrow=(4, 5, 6))
    dqA, dkvp, dsink = attn_bwd(p, O["qA"], sv["kvp"], sv["sink"], slopes, da, T)
    dkvA = dkvp[BLOCK:-BLOCK]
    dqB_f, dzf, diB_f, dqB_b, dzb, diB_b, dlow_f, dlow_b = scan_bwd(
        "scan_bwd_h", hgrn_prep, sv["scan_raws"], sv["scan_pars"], sv["ss"], (dof, 0), N_HEADS_B, HEAD_DIM_B, HEAD_DIM_B, T)
    dqB_f = sync(dqB_f)
    row = lambda arr, w: ("row", arr, 0, w)
    dlow = jnp.concatenate([dlow_f, dlow_b], axis=0)
    dqM, dkv = mem_bwd(p, O["qM"], kv, dmo, T)
    (dp,) = rows_call("even_dp", _assemble_even, T,
                      [row(dqA, W_A), row(dgA, W_A), row(dqB_f, W_B), row(dqB_b, W_B), row(dzf, W_B), row(dzb, W_B),
                       row(diB_f, W_B), row(diB_b, W_B), row(dgB, W_B), row(dqM, W_M), row(dgM, W_M),
                       row(dkvA, 2 * W_KV_A)],
                      [EVEN_IN], [BF16])
    dh = matmul("mm_dh_e", dp, W["w_in_e"][i], "nt")
    dwi = matmul("mm_dwi_e", sv["h"], dp, "tn")
    dx, dg = rows_vjp_call("rms_res_bwd", rms_res_tile, T, [("row", sv["x"], 0, D_MODEL), ("full", sv["g"])],
                           [[("row", dh, 0, D_MODEL)], [("row", dxo, 0, D_MODEL)]])
    return dx, dict(w_in=dwi, w_out=dwo, norm=dg[0], sink=dsink.reshape(N_Q_A), low=dlow, hg=dhg[0], kv=dkv)


def _pad_gate_up(w_up):
    z = jnp.zeros((2, 128, WK_C), F32)
    z = z.at[0, 0:GATE_RANK].set(w_up[0])
    return z.at[1, GATE_RANK:2 * GATE_RANK].set(w_up[1])


def _odd_fwd(x, i, W, kv, T):
    O = ODD_OFF
    g = W["norm_odd"][i].reshape(1, D_MODEL)
    (h,) = rows_call("rms_fwd", rms_tile, T, [("row", x, 0, D_MODEL), ("full", g)], [D_MODEL], [BF16])
    p = matmul("mm_in_o", h, W["w_in_o"][i], "nn")
    wup = _pad_gate_up(W["w_gate_up"][i])
    one_dir = [((p, O["qC"]), WK_C), ((p, O["kC"]), WK_C), ((p, O["vC"]), WV_C), ((p, O["rr"]), 128)]
    scan_raws = [one_dir, one_dir]
    scan_pars = [[wup[d], W["b_gate"][i][d:d + 1]] for d in range(2)]
    o_f, o_b, ss_f, ss_b = scan_fwd("scan_fwd_g", gla_prep, scan_raws, scan_pars, N_HEADS_C, DK_C, DV_C, T)
    mo = mem_fwd(p, O["qM"], kv, T)
    gg = W["gla_norm"][i].reshape(1, WV_C)
    post_ins = [("row", o_f, 0, WV_C), ("row", o_b, 0, WV_C), ("row", mo, 0, W_M),
                ("row", p, O["gC"], WV_C), ("row", p, O["gM"], W_M), ("full", gg)]
    (mix,) = rows_call("odd_post_fwd", odd_post_tile, T, post_ins, [MIX], [BF16])
    x_new = matmul("mm_out", mix, W["w_out_o"][i], "nn", add=x)
    return x_new, dict(x=x, g=g, h=h, p=p, scan_raws=scan_raws, scan_pars=scan_pars, ss=(ss_f, ss_b),
                       post_ins=post_ins, mix=mix)


def _assemble_odd(dq0, dq1, dk0, dk1, dv0, dv1, dgC, dqM, dgM, dr0, dr1):
    parts = [_add2(dq0, dq1), _add2(dk0, dk1), _add2(dv0, dv1), dgC, dqM, dgM, _add2(dr0, dr1)]
    return (jnp.concatenate([t.astype(BF16) for t in parts], axis=-1),)


def _odd_bwd(dxo, sv, i, W, kv, T, sync):
    O = ODD_OFF
    p = sv["p"]
    dmix = matmul("mm_dmix", dxo, W["w_out_o"][i], "nt")
    dwo = matmul("mm_dwo", sv["mix"], dxo, "tn")
    dof, dmo, dgC, dgM, dgg = rows_vjp_call("odd_post_bwd", odd_post_tile, T, sv["post_ins"],
                                            [[("row", dmix, 0, MIX)]], skip=(1,), narrow=(3, 4))
    dqf, dkf, dvf, dr_f, dqb, dkb, dvb, dr_b, dwup_f, dbg_f, dwup_b, dbg_b = scan_bwd(
        "scan_bwd_g", gla_prep, sv["scan_raws"], sv["scan_pars"], sv["ss"], (dof, 0), N_HEADS_C, DK_C, DV_C, T)
    dqf = sync(dqf)
    row = lambda arr, w: ("row", arr, 0, w)
    dqM, dkv = mem_bwd(p, O["qM"], kv, dmo, T)
    (dp,) = rows_call("odd_dp", _assemble_odd, T,
                      [row(dqf, WK_C), row(dqb, WK_C), row(dkf, WK_C), row(dkb, WK_C), row(dvf, WV_C), row(dvb, WV_C),
                       row(dgC, WV_C), row(dqM, W_M), row(dgM, W_M), row(dr_f, 128), row(dr_b, 128)],
                      [ODD_PAD], [BF16])
    dh = matmul("mm_dh_o", dp, W["w_in_o"][i], "nt")
    dwi = matmul("mm_dwi_o", sv["h"], dp, "tn")
    dx, dg = rows_vjp_call("rms_res_bwd", rms_res_tile, T, [("row", sv["x"], 0, D_MODEL), ("full", sv["g"])],
                           [[("row", dh, 0, D_MODEL)], [("row", dxo, 0, D_MODEL)]])
    dw_up = jnp.stack([dwup_f[0:GATE_RANK], dwup_b[GATE_RANK:2 * GATE_RANK]])
    dbg = jnp.concatenate([dbg_f, dbg_b], axis=0)
    return dx, dict(w_in=dwi, w_out=dwo, norm=dg[0], w_up=dw_up, b_gate=dbg, gg=dgg[0], kv=dkv)


def local_step(x, mem, target, W, later=None, on_layer_grads=None, sync=lambda a: a):
    T = x.shape[0]
    slopes = jnp.repeat(2.0 ** (-8.0 * jnp.arange(1, N_Q_A + 1, dtype=F32) / N_Q_A), BLOCK).reshape(N_Q_A * BLOCK, 1)
    lower, lower_vjp = jax.vjp(_lower_bounds, W["lb_param"])
    mem_g = W["mem_norm"].reshape(1, D_MODEL)
    (mem_n,) = rows_call("mem_rms_fwd", rms_tile, N_MEM, [("row", mem, 0, D_MODEL), ("full", mem_g)], [D_MODEL], [BF16])
    kvs, saved = [], []
    for l in range(DEPTH):
        if l == 1 and later is not None:
            x, W = later(x, W)
        kvs.append(matmul("mm_kv", mem_n, W["w_kv"][l], "nn"))
        if l % 2 == 0:
            x, sv = _even_fwd(x, l // 2, W, lower, kvs[l], slopes, T)
        else:
            x, sv = _odd_fwd(x, l // 2, W, kvs[l], T)
        saved.append(sv)
    loss, dx, dgf = final_call(x, W["final_norm"].reshape(1, D_MODEL), target, T)
    per = [None] * DEPTH
    dmem_n = None
    for l in reversed(range(DEPTH)):
        if l % 2 == 0:
            dx, per[l] = _even_bwd(dx, saved[l], l // 2, W, kvs[l], slopes, T, sync)
        else:
            dx, per[l] = _odd_bwd(dx, saved[l], l // 2, W, kvs[l], T, sync)
        per[l]["w_kv"] = matmul("mm_dwkv", mem_n, per[l]["kv"], "tn")
        dmem_n = matmul("mm_dmem", per[l]["kv"], W["w_kv"][l], "nt", add=dmem_n)
        if on_layer_grads is not None:
            dx = on_layer_grads(l, dx, per[l])
    dw_kv = [per[l]["w_kv"] for l in range(DEPTH)]
    (dmem_norm,) = rows_vjp_call("mem_rms_bwd", rms_tile, N_MEM, [("row", mem, 0, D_MODEL), ("full", mem_g)],
                                 [[("row", dmem_n, 0, D_MODEL)]], skip=(0,))
    ev, od = (per[0], per[2]), (per[1], per[3])
    (d_lb,) = lower_vjp(jnp.stack([e["low"] for e in ev]))
    grads = dict(
        w_in_e=jnp.stack([e["w_in"] for e in ev]), w_in_o=jnp.stack([o["w_in"] for o in od]),
        w_out_e=jnp.stack([e["w_out"] for e in ev]), w_out_o=jnp.stack([o["w_out"] for o in od]),
        w_kv=jnp.stack(dw_kv), norm_even=jnp.stack([e["norm"] for e in ev]), sink=jnp.stack([e["sink"] for e in ev]),
        lb_param=d_lb, hgrn_norm=jnp.stack([e["hg"] for e in ev]), norm_odd=jnp.stack([o["norm"] for o in od]),
        w_gate_up=jnp.stack([o["w_up"] for o in od]), b_gate=jnp.stack([o["b_gate"] for o in od]),
        gla_norm=jnp.stack([o["gg"] for o in od]), mem_norm=dmem_norm[0], final_norm=dgf[0])
    return loss, dx, grads


SMALL_SPECS = (("lb_param", (2, 2, 128)), ("norm_odd", (2, 256)), ("w_gate_up", (2, 2, 16, 128)),
               ("b_gate", (2, 2, 128)), ("gla_norm", (2, 256)))
SMALL_ROWS = 80


def _pack_small_local(d):
    return jnp.concatenate([d[n].reshape(-1) for n, _ in SMALL_SPECS]).reshape(SMALL_ROWS, 128)


def _unpack_small_local(b):
    flat, out, o = b.reshape(-1), {}, 0
    for n, shp in SMALL_SPECS:
        sz = int(np.prod(shp))
        out[n] = flat[o:o + sz].reshape(shp)
        o += sz
    return out


def _unpack_small_full(g4):
    per = [_unpack_small_local(g4[j]) for j in range(4)]
    return {n: jnp.concatenate([per[j][n] for j in range(4)], axis=-1) for n, _ in SMALL_SPECS}


def _pack_small_blocks(full):
    blocks = []
    for j in range(4):
        blocks.append(_pack_small_local({n: full[n][..., j * shp[-1]:(j + 1) * shp[-1]] for n, shp in SMALL_SPECS}))
    return jnp.stack(blocks)


def _cols(t, order, off, widths):
    return [t[..., off[n]:off[n] + widths[n]] for n in order]


EVEN_REF_ORDER = ("qA", "kA", "vA", "gA", "qB", "zf", "zb", "iB", "gB", "qM", "gM")
ODD_REF_ORDER = ("qC", "kC", "vC", "gC", "rr", "qM", "gM")


def _layer_weights(l, g_in, g_out, g_kv):
    t = g_in.transpose(1, 0, 2).reshape(D_MODEL, -1)
    if l % 2 == 0:
        w_in = jnp.concatenate(_cols(t, EVEN_ORDER, EVEN_REF_OFF, EVEN_W), axis=-1)
    else:
        w_in = jnp.concatenate(_cols(t, ODD_ORDER, ODD_REF_OFF, ODD_W) + [jnp.zeros((D_MODEL, ODD_PAD - ODD_IN), BF16)],
                               axis=-1)
    return w_in, g_out.reshape(MIX, D_MODEL), g_kv.reshape(D_MODEL, 2 * W_M)


def _layer_grad_blocks(l, gl):
    if l % 2 == 0:
        t = jnp.concatenate(_cols(gl["w_in"], EVEN_REF_ORDER, EVEN_OFF, EVEN_W), axis=-1)
    else:
        t = jnp.concatenate(_cols(gl["w_in"], ODD_REF_ORDER, ODD_OFF, ODD_W), axis=-1)
    b_in = t.reshape(D_MODEL, 4, -1).transpose(1, 2, 0)
    return [b_in, gl["w_out"].reshape(4, MIX // 4, D_MODEL), gl["w_kv"].reshape(4, D_MODEL // 4, 2 * W_M)]


WEIGHT_NAMES = ("norm_even", "w_in_even", "sink", "lb_param", "hgrn_norm", "w_out_even", "norm_odd", "w_in_odd",
                "w_gate_up", "b_gate", "gla_norm", "w_out_odd", "mem_norm", "w_mem_kv", "final_norm")


def kernel(x, mem, norm_even, w_in_even, sink, lb_param, hgrn_norm, w_out_even, norm_odd, w_in_odd, w_gate_up, b_gate, gla_norm, w_out_odd, mem_norm, w_mem_kv, final_norm, loss_target, m_norm_even, m_w_in_even, m_sink, m_lb_param, m_hgrn_norm, m_w_out_even, m_norm_odd, m_w_in_odd, m_w_gate_up, m_b_gate, m_gla_norm, m_w_out_odd, m_mem_norm, m_w_mem_kv, m_final_norm, v_norm_even, v_w_in_even, v_sink, v_lb_param, v_hgrn_norm, v_w_out_even, v_norm_odd, v_w_in_odd, v_w_gate_up, v_b_gate, v_gla_norm, v_w_out_odd, v_mem_norm, v_w_mem_kv, v_final_norm):
    w = dict(zip(WEIGHT_NAMES, (norm_even, w_in_even, sink, lb_param, hgrn_norm, w_out_even, norm_odd, w_in_odd,
                                w_gate_up, b_gate, gla_norm, w_out_odd, mem_norm, w_mem_kv, final_norm)))
    m = dict(zip(WEIGHT_NAMES, (m_norm_even, m_w_in_even, m_sink, m_lb_param, m_hgrn_norm, m_w_out_even, m_norm_odd,
                                m_w_in_odd, m_w_gate_up, m_b_gate, m_gla_norm, m_w_out_odd, m_mem_norm, m_w_mem_kv,
                                m_final_norm)))
    v = dict(zip(WEIGHT_NAMES, (v_norm_even, v_w_in_even, v_sink, v_lb_param, v_hgrn_norm, v_w_out_even, v_norm_odd,
                                v_w_in_odd, v_w_gate_up, v_b_gate, v_gla_norm, v_w_out_odd, v_mem_norm, v_w_mem_kv,
                                v_final_norm)))
    ci = lax.axis_index("c").astype(jnp.int32).reshape(1)
    chip = (2 * lax.axis_index("x") + lax.axis_index("y")).astype(jnp.int32).reshape(1)

    shards = []
    for l in range(DEPTH):
        names = ("w_in_even", "w_out_even") if l % 2 == 0 else ("w_in_odd", "w_out_odd")
        shards.append([w[names[0]][l // 2].astype(BF16), w[names[1]][l // 2].astype(BF16), w_mem_kv[l].astype(BF16)])
    small = _pack_small_local(w)
    own = lambda g, s: lax.dynamic_update_slice(g, s[None], (chip[0], 0, 0))
    first = [own(g, s) for g, s in zip(gather_weights(shards[0], small), shards[0] + [small])]
    later_shards = shards[1] + shards[2] + shards[3]
    later_raw = gather_weights_async(later_shards)
    w0 = _layer_weights(0, *first[0:3])
    W = dict(w_in_e=[w0[0]], w_out_e=[w0[1]], w_kv=[w0[2]])
    W.update(_unpack_small_full(first[3]))
    W.update({n: w[n] for n in ("norm_even", "sink", "hgrn_norm", "mem_norm", "final_norm")})

    def later(x1, W):
        x1, raw = lax.optimization_barrier((x1, list(later_raw)))
        g = [own(a, s) for a, s in zip(raw, later_shards)]
        w1, w2, w3 = (_layer_weights(l, *g[3 * (l - 1):3 * l]) for l in (1, 2, 3))
        W = dict(W)
        W.update(w_in_e=[w0[0], w2[0]], w_in_o=[w1[0], w3[0]], w_out_e=[w0[1], w2[1]], w_out_o=[w1[1], w3[1]],
                 w_kv=[w0[2], w1[2], w2[2], w3[2]])
        return x1, W

    place = jnp.concatenate([chip, ci])

    def start(tag, blocks, wire):
        axes = [2 if b.shape[1] == ODD_IN // 4 else 1 for b in blocks]
        return dict(tag=tag, blocks=blocks, wire=wire, step=0,
                    recv=exchange_siblings(f"rs_siblings_{tag}", blocks, axes, 2))

    def advance(p):
        if p["step"] == 0:
            sums = [add_sibling(g, r, ci, dt) for g, r, dt in zip(p["blocks"], p["recv"], p["wire"])]
            p["recv3"] = exchange_chips(f"rs_chips_{p['tag']}", sums, 3)
        else:
            p["mine"] = [add_chips(g, r, r3, place) for g, r, r3 in zip(p["blocks"], p["recv"], p["recv3"])]
            p["other"] = exchange_siblings(f"rs_final_{p['tag']}", p["mine"], [None] * len(p["mine"]), 4)
        p["step"] += 1

    pipes, first_layer = [], {}

    def sync(a):
        for p in pipes:
            if p["step"] < 3:
                key = ("recv", "recv3", "other")[p["step"]]
                a, arrived = lax.optimization_barrier((a, list(p[key])))
                p[key] = arrived
                if p["step"] < 2:
                    advance(p)
                else:
                    p["step"] = 3
        return a

    def on_layer_grads(l, dx, gl):
        dx = sync(dx)
        if l == 0:
            first_layer.update(gl)
        else:
            pipes.append(start(f"l{l}", _layer_grad_blocks(l, gl), [BF16] * 3))
        return dx

    loss_tile, dx, grads = local_step(x[0], mem[0], loss_target[0], W, later, on_layer_grads, sync)
    last = start("l0", _layer_grad_blocks(0, first_layer) + [_pack_small_blocks(grads)], [BF16] * 3 + [F32])
    for p in pipes + [last]:
        while p["step"] < (1 if p is last else 2):
            advance(p)
    by_layer = {int(p["tag"][1:]): p for p in pipes + [last]}
    halves = lambda layers, k: (jnp.stack([by_layer[l]["mine"][k] for l in layers]),
                                jnp.stack([by_layer[l]["other"][k] for l in layers]))
    gl, upd = {}, {}

    pack = jnp.zeros((8, D_MODEL), F32)
    pack = pack.at[0:2].set(grads["norm_even"]).at[2].set(grads["hgrn_norm"].reshape(-1))
    pack = pack.at[3].set(grads["mem_norm"]).at[4].set(grads["final_norm"])
    pack = pack.at[5, 0:16].set(grads["sink"].reshape(-1)).at[5, 16].set(loss_tile[0, 0])
    tot = sum_devices(allgather_small(pack))
    gl.update(norm_even=tot[0:2], hgrn_norm=tot[2].reshape(2, W_B), mem_norm=tot[3], final_norm=tot[4],
              sink=tot[5, 0:16].reshape(2, N_Q_A))
    loss = tot[5, 16]
    for n in ("norm_even", "hgrn_norm", "mem_norm", "final_norm", "sink"):
        upd[n] = adamw_call(w[n], gl[n], m[n], v[n])
    tr_ = lambda a: jnp.swapaxes(a, 1, 2)
    gl["w_in_odd"], *upd["w_in_odd"] = [tr_(o) for o in adamw_halves(
        tr_(w["w_in_odd"]), *halves((1, 3), 0), tr_(m["w_in_odd"]), tr_(v["w_in_odd"]), ci)]
    gl["w_out_odd"], *upd["w_out_odd"] = adamw_halves(w["w_out_odd"], *halves((1, 3), 1), m["w_out_odd"],
                                                      v["w_out_odd"], ci)
    early = [upd[n] for n in sorted(upd)] + [gl["w_in_odd"], gl["w_out_odd"]]
    last["recv3"], early = lax.optimization_barrier((list(last["recv3"]), early))
    for n, res in zip(sorted(upd), early):
        upd[n] = res
    gl["w_in_odd"], gl["w_out_odd"] = early[-2:]
    advance(last)

    big = dict(w_in_even=halves((0, 2), 0), w_out_even=halves((0, 2), 1), w_mem_kv=halves((0, 1, 2, 3), 2))
    s_mine, s_other = last["mine"][3], last["other"][3]
    g_small = jnp.where(ci[0] == 0, jnp.concatenate([s_mine, s_other]), jnp.concatenate([s_other, s_mine]))
    gl.update(_unpack_small_local(g_small))
    for n in WEIGHT_NAMES:
        if n == "w_in_even":
            gl[n], *upd[n] = [tr_(o) for o in adamw_halves(tr_(w[n]), *big[n], tr_(m[n]), tr_(v[n]), ci)]
        elif n in big:
            gl[n], *upd[n] = adamw_halves(w[n], *big[n], m[n], v[n], ci)
        elif n not in upd:
            upd[n] = adamw_call(w[n], gl[n], m[n], v[n])
    return (loss, dx[None], *[gl[n] for n in WEIGHT_NAMES], *[upd[n][0] for n in WEIGHT_NAMES],
            *[upd[n][1] for n in WEIGHT_NAMES], *[upd[n][2] for n in WEIGHT_NAMES])
```

```python
import functools

import numpy as np
import jax
import jax.numpy as jnp
from jax import lax
from jax.experimental import pallas as pl
from jax.experimental.pallas import tpu as pltpu
from jax.experimental.pallas import tpu_sc as plsc

F32 = jnp.float32
BF16 = jnp.bfloat16

D_MODEL = 1024
DEPTH = 4
N_Q_A, N_KV_A, HEAD_DIM_A = 8, 2, 64
W_A, W_KV_A = 512, 128
WINDOW = 128
BLOCK = 128
N_HEADS_B, HEAD_DIM_B, W_B = 4, 128, 512
N_HEADS_C, DK_C, DV_C, WK_C, WV_C = 4, 128, 256, 512, 1024
GATE_RANK = 16
GATE_TEMP = 16.0
N_MEM, N_HEADS_M, HEAD_DIM_M, W_M = 256, 4, 128, 512
EPS = 1e-6
MASK_VALUE = -1e30
MIN_GATE = 1e-30
EVEN_IN, ODD_IN = 4864, 4128
ODD_PAD = 4224
MIX = 1536
ADAM_LR, ADAM_B1, ADAM_B2, ADAM_EPS, ADAM_WD, ADAM_STEP = 0.001, 0.9, 0.999, 1e-08, 0.01, 10

SCAN_CHUNK = 128
SCAN_SUB = 2
SCAN_LEVELS = 7
VMEM_LIMIT = 56 * 1024 * 1024

EVEN_REF_OFF = dict(qA=0, kA=512, vA=640, gA=768, qB=1280, zf=1792, zb=2304, iB=2816, gB=3328, qM=3840, gM=4352)
EVEN_W = dict(qA=512, kA=128, vA=128, gA=512, qB=512, zf=512, zb=512, iB=512, gB=512, qM=512, gM=512)
EVEN_ORDER = ("qA", "gA", "qB", "zf", "zb", "iB", "gB", "qM", "gM", "kA", "vA")
ODD_REF_OFF = dict(qC=0, kC=512, vC=1024, gC=2048, rr=3072, qM=3104, gM=3616)
ODD_W = dict(qC=512, kC=512, vC=1024, gC=1024, rr=32, qM=512, gM=512)
ODD_ORDER = ("qC", "kC", "vC", "gC", "qM", "gM", "rr")


def _offsets(order, widths):
    off, o = {}, 0
    for n in order:
        off[n] = o
        o += widths[n]
    return off


EVEN_OFF = _offsets(EVEN_ORDER, EVEN_W)
ODD_OFF = _offsets(ODD_ORDER, ODD_W)


def _dg(a, b, ca, cb):
    return lax.dot_general(a.astype(BF16), b.astype(BF16), (((ca,), (cb,)), ((), ())),
                           preferred_element_type=F32)


def dot_nn(a, b):
    return _dg(a, b, 1, 0)


def dot_nt(a, b):
    return _dg(a, b, 1, 1)


def dot_tn(a, b):
    return _dg(a, b, 0, 0)


@jax.custom_vjp
def bdot(a, b):
    return dot_nn(a, b)


bdot.defvjp(lambda a, b: (dot_nn(a, b), (a, b)),
            lambda r, g: (dot_nt(g, r[1]), dot_tn(r[0], g)))


@jax.custom_vjp
def bdot_t(a, b):
    return dot_nt(a, b)


bdot_t.defvjp(lambda a, b: (dot_nt(a, b), (a, b)),
              lambda r, g: (dot_nn(g, r[1]), dot_tn(g, r[0])))


@jax.custom_vjp
def bdot_tn(a, b):
    return dot_tn(a, b)


bdot_tn.defvjp(lambda a, b: (dot_tn(a, b), (a, b)),
               lambda r, g: (dot_nt(r[1], g), dot_nn(r[0], g)))


def _split_mm(h, x):
    hi = x.astype(BF16)
    lo = (x - hi.astype(F32)).astype(BF16)
    return (lax.dot_general(h, hi, (((1,), (0,)), ((), ())), preferred_element_type=F32)
            + lax.dot_general(h, lo, (((1,), (0,)), ((), ())), preferred_element_type=F32))


def _sigmoid(z):
    return 1.0 / (1.0 + jnp.exp(-z))


def _silu(z):
    return z * _sigmoid(z)


def _log_sigmoid(z):
    return jnp.minimum(z, 0.0) - jnp.log(1.0 + jnp.exp(-jnp.abs(z)))


def _rms(x, g):
    return x * lax.rsqrt(jnp.mean(x * x, axis=-1, keepdims=True) + EPS) * g


def rms_tile(x, g):
    return (_rms(x, g),)


@functools.partial(jax.custom_vjp, nondiff_argnums=(1, 2))
def split(x, n, axis):
    w = x.shape[axis] // n
    return tuple(lax.slice_in_dim(x, h * w, (h + 1) * w, axis=axis) for h in range(n))


split.defvjp(lambda x, n, axis: (split(x, n, axis), None),
             lambda n, axis, _, cts: (jnp.concatenate(cts, axis=axis),))


def _group_rms(o, g, heads):
    return jnp.concatenate([_rms(oh, gh) for oh, gh in zip(split(o, heads, 1), split(g, heads, 1))], axis=-1)


def even_post_tile(a, o2f, o2b, mo, gA, gB, gM, hg):
    y = _group_rms(o2f + o2b, hg, N_HEADS_B)
    return (jnp.concatenate([a * _silu(gA), y * _silu(gB), mo * _silu(gM)], axis=-1),)


def odd_post_tile(o2f, o2b, mo, gC, gM, gg):
    y = _group_rms(o2f + o2b, gg, N_HEADS_C)
    return (jnp.concatenate([y * _silu(gC), mo * _silu(gM)], axis=-1),)


def hgrn_prep(raw, par):
    qB, z, iB = raw
    (lb,) = par
    f = lb + (1.0 - lb) * _sigmoid(z)
    return _silu(qB), (1.0 - lb) * _sigmoid(-z), iB, jnp.log(jnp.maximum(f, MIN_GATE))


def gla_prep(raw, par):
    qC, kC, vC, r128 = raw
    wup, bg = par
    return qC * (DK_C ** -0.5), kC, vC, _log_sigmoid(bdot(r128, wup) + bg) / GATE_TEMP


def mem_tile(q, k, v):
    s = bdot_t(q, k) * (HEAD_DIM_M ** -0.5)
    m = lax.stop_gradient(jnp.max(s, axis=-1, keepdims=True))
    p = jnp.exp(s - m)
    p = p / jnp.sum(p, axis=-1, keepdims=True)
    return (bdot(p, v),)


ATTN_GROUP = N_Q_A // N_KV_A


def attn_block(q, ks, vs, sink, slope, c, seq):
    rows = ATTN_GROUP * BLOCK
    i = lax.broadcasted_iota(jnp.int32, (rows, 3 * BLOCK), 0) % BLOCK
    j = lax.broadcasted_iota(jnp.int32, (rows, 3 * BLOCK), 1)
    dist = jnp.abs(i - j + BLOCK).astype(F32)
    kpos = (c - 1) * BLOCK + j
    valid = (dist <= WINDOW) & (kpos >= 0) & (kpos < seq)
    s = bdot_t(q, ks) * (HEAD_DIM_A ** -0.5)
    s = jnp.where(valid, s - slope * dist, MASK_VALUE)
    m = lax.stop_gradient(jnp.maximum(jnp.max(s, axis=-1, keepdims=True), sink))
    p = jnp.where(valid, jnp.exp(s - m), 0.0)
    denom = jnp.sum(p, axis=-1, keepdims=True) + jnp.exp(sink - m)
    return bdot(p, vs) / denom


def scan_chunk(q, k, v, e, tot, st, qm, pm):
    C = SCAN_CHUNK
    e = split(e, 2 + SCAN_LEVELS, 0)
    qe = q * jnp.exp(e[0])
    kd = k * jnp.exp(e[1])
    r = lax.broadcasted_iota(jnp.int32, (C, C), 0)
    s = lax.broadcasted_iota(jnp.int32, (C, C), 1)
    a = jnp.where(r == s, jnp.sum(q * k, axis=-1, keepdims=True), 0.0)
    for l in range(SCAN_LEVELS):
        u = jnp.where(qm[l * C:(l + 1) * C] != 0.0, q, k) * jnp.exp(e[2 + l])
        a = a + bdot_t(u, u) * pm[l * C:(l + 1) * C]
    o = bdot_t(qe, st) + bdot(a, v)
    st_new = st * jnp.exp(tot) + bdot_tn(v, kd)
    return o, st_new


def _scan_consts():
    C, L = SCAN_CHUNK, SCAN_LEVELS
    t = np.arange(C)[:, None]
    r = np.arange(C)[None, :]
    blocks = [(r <= t), (r > t)]
    qms, pms = [], []
    for l in range(1, L + 1):
        m = C >> l
        upper_t = (t % (2 * m)) >= m
        upper_r = (r % (2 * m)) >= m
        same_half = (t // m) == (r // m)
        blocks.append(same_half & np.where(upper_t, r <= t, r > t))
        qms.append(np.broadcast_to(upper_t, (C, C)))
        pms.append(((t // (2 * m)) == (r // (2 * m))) & upper_t & ~upper_r)
    hf = np.concatenate(blocks, axis=0).astype(np.float32)
    flip = lambda mat: mat.reshape(-1, C, C)[:, ::-1, ::-1].reshape(-1, C)
    qmf = np.concatenate(qms, axis=0).astype(np.float32)
    pmf = np.concatenate(pms, axis=0).astype(np.float32)
    h = np.stack([hf, flip(hf)])
    ht = np.stack([h[0].T, h[1].T])
    qm = np.stack([qmf, 1.0 - qmf])
    pm = np.stack([pmf, flip(pmf)])
    return h, ht, qm, pm


def _cparams(sem):
    return pltpu.CompilerParams(dimension_semantics=sem, vmem_limit_bytes=VMEM_LIMIT)


def _row_tile(T):
    return min(T, 512)


def _in_spec(spec, tr):
    kind = spec[0]
    if kind == "row":
        _, arr, off, w = spec
        assert off % w == 0
        return arr, pl.BlockSpec((tr, w), functools.partial(lambda i, b: (i, b), b=off // w))
    if kind == "row3":
        _, arr, d, off, w = spec
        assert off % w == 0
        return arr, pl.BlockSpec((None, tr, w), functools.partial(lambda i, d, b: (d, i, b), d=d, b=off // w))
    _, arr = spec
    return arr, pl.BlockSpec(arr.shape, functools.partial(lambda i, n: (0,) * n, n=arr.ndim))


def rows_call(name, tile_fn, T, ins, out_widths, out_dtypes=None, stacks=None):
    tr = _row_tile(T)
    n_in = len(ins)
    out_dtypes = out_dtypes or [F32] * len(out_widths)
    stacks = stacks or [(k,) for k in range(len(out_widths))]

    def body(*refs):
        vals = [r[...] for r in refs[:n_in]]
        outs = tile_fn(*vals)
        for r, members in zip(refs[n_in:], stacks):
            if len(members) == 1:
                r[...] = outs[members[0]].astype(r.dtype)
            else:
                for d, k in enumerate(members):
                    r[d] = outs[k].astype(r.dtype)

    in_specs, args = [], []
    for spec in ins:
        arr, bs = _in_spec(spec, tr)
        args.append(arr)
        in_specs.append(bs)
    out_specs, out_shape = [], []
    for w, dt, members in zip(out_widths, out_dtypes, stacks):
        n = len(members)
        if n == 1:
            out_specs.append(pl.BlockSpec((tr, w), lambda i: (i, 0)))
            out_shape.append(jax.ShapeDtypeStruct((T, w), dt))
        else:
            out_specs.append(pl.BlockSpec((n, tr, w), lambda i: (0, i, 0)))
            out_shape.append(jax.ShapeDtypeStruct((n, T, w), dt))
    return pl.pallas_call(body, out_shape=out_shape, grid=(T // tr,), in_specs=in_specs, out_specs=out_specs,
                          name=name, compiler_params=_cparams(("arbitrary",)))(*args)


def rows_vjp_call(name, tile_fn, T, ins, cts, skip=(), narrow=()):
    tr = _row_tile(T)
    n_in = len(ins)
    n_ct = [len(c) for c in cts]
    want = [k for k in range(n_in) if k not in skip]

    def body(*refs):
        i = pl.program_id(0)
        vals = [r[...] for r in refs[:n_in]]
        ct, pos = [], n_in
        for n in n_ct:
            acc = refs[pos][...]
            for r in refs[pos + 1:pos + n]:
                acc = acc + r[...]
            ct.append(acc)
            pos += n
        _, vjp = jax.vjp(tile_fn, *vals)
        grads = vjp(tuple(ct))
        for r, k in zip(refs[pos:], want):
            if ins[k][0] == "full":
                @pl.when(i == 0)
                def _():
                    r[...] = jnp.zeros_like(r)
                r[...] += grads[k]
            else:
                r[...] = grads[k].astype(r.dtype)

    in_specs, args = [], []
    for spec in list(ins) + [s for c in cts for s in c]:
        arr, bs = _in_spec(spec, tr)
        args.append(arr)
        in_specs.append(bs)
    out_specs, out_shape = [], []
    for k in want:
        if ins[k][0] == "full":
            arr = ins[k][1]
            out_specs.append(pl.BlockSpec(arr.shape, functools.partial(lambda i, n: (0,) * n, n=arr.ndim)))
            out_shape.append(jax.ShapeDtypeStruct(arr.shape, F32))
        else:
            w = ins[k][-1]
            out_specs.append(pl.BlockSpec((tr, w), lambda i: (i, 0)))
            out_shape.append(jax.ShapeDtypeStruct((T, w), BF16 if k in narrow else F32))
    return pl.pallas_call(body, out_shape=out_shape, grid=(T // tr,), in_specs=in_specs, out_specs=out_specs,
                          name=name, compiler_params=_cparams(("arbitrary",)))(*args)


def matmul(name, a, b, mode, add=None, out_dtype=F32):
    if mode == "tn":
        K, M = a.shape
        N = b.shape[1]
        tm = M if M <= 1536 else 512
        tn = N if N <= 1280 else (N // 2 if (N // 2) % 128 == 0 else N)
        tk = min(K, 512)
        grid = (M // tm, N // tn, K // tk)

        def body(a_ref, b_ref, o_ref):
            @pl.when(pl.program_id(2) == 0)
            def _():
                o_ref[...] = jnp.zeros_like(o_ref)
            o_ref[...] += dot_tn(a_ref[...], b_ref[...])

        return pl.pallas_call(
            body, out_shape=jax.ShapeDtypeStruct((M, N), F32), grid=grid,
            in_specs=[pl.BlockSpec((tk, tm), lambda i, j, k: (k, i)), pl.BlockSpec((tk, tn), lambda i, j, k: (k, j))],
            out_specs=pl.BlockSpec((tm, tn), lambda i, j, k: (i, j)), name=name,
            compiler_params=_cparams(("arbitrary", "arbitrary", "arbitrary")))(a, b)

    M, K = a.shape
    N = b.shape[1] if mode == "nn" else b.shape[0]
    tm = min(M, 512)
    tn = N if N <= 1536 else (N // 2 if (N // 2) % 128 == 0 else (N // 3 if (N // 3) % 128 == 0 else N))
    grid = (N // tn, M // tm)
    n_in = 2 + (add is not None)

    def body(*refs):
        a_ref, b_ref = refs[0], refs[1]
        o_ref = refs[n_in]
        acc = dot_nn(a_ref[...], b_ref[...]) if mode == "nn" else dot_nt(a_ref[...], b_ref[...])
        if add is not None:
            acc = acc + refs[2][...]
        o_ref[...] = acc.astype(o_ref.dtype)

    in_specs = [pl.BlockSpec((tm, K), lambda j, i: (i, 0)),
                pl.BlockSpec((K, tn), lambda j, i: (0, j)) if mode == "nn" else pl.BlockSpec((tn, K), lambda j, i: (j, 0))]
    args = [a, b]
    if add is not None:
        in_specs.append(pl.BlockSpec((tm, tn), lambda j, i: (i, j)))
        args.append(add)
    return pl.pallas_call(
        body, out_shape=jax.ShapeDtypeStruct((M, N), out_dtype), grid=grid, in_specs=in_specs,
        out_specs=pl.BlockSpec((tm, tn), lambda j, i: (i, j)), name=name,
        compiler_params=_cparams(("arbitrary", "arbitrary")))(*args)


def _attn_heads(n):
    G = N_Q_A // N_KV_A
    k_sl = pl.ds(n * HEAD_DIM_A, HEAD_DIM_A)
    v_sl = pl.ds(W_KV_A + n * HEAD_DIM_A, HEAD_DIM_A)
    q_sl = [pl.ds((n * G + g) * HEAD_DIM_A, HEAD_DIM_A) for g in range(G)]
    return k_sl, v_sl, q_sl, range(n * G, (n + 1) * G)


def attn_fwd(p, q_off, kvp, sink, slopes, T):
    nb = T // BLOCK
    assert q_off % W_A == 0

    def body(q_ref, kv_ref, sink_ref, slope_ref, o_ref):
        c = pl.program_id(0)
        rows = pl.ds(pl.multiple_of(c * BLOCK, BLOCK), 3 * BLOCK)
        for n in range(N_KV_A):
            k_sl, v_sl, q_sl, heads = _attn_heads(n)
            group = pl.ds(n * ATTN_GROUP * BLOCK, ATTN_GROUP * BLOCK)
            q = jnp.concatenate([q_ref[:, s] for s in q_sl], axis=0)
            o = attn_block(q, kv_ref[rows, k_sl], kv_ref[rows, v_sl], sink_ref[group, :], slope_ref[group, :], c, T)
            for g, s in enumerate(q_sl):
                o_ref[:, s] = o[g * BLOCK:(g + 1) * BLOCK]

    full = lambda a: pl.BlockSpec(a.shape, functools.partial(lambda c, nd: (0,) * nd, nd=a.ndim))
    return pl.pallas_call(
        body, out_shape=jax.ShapeDtypeStruct((T, W_A), F32), grid=(nb,),
        in_specs=[pl.BlockSpec((BLOCK, W_A), lambda c: (c, q_off // W_A)), full(kvp), full(sink), full(slopes)],
        out_specs=pl.BlockSpec((BLOCK, W_A), lambda c: (c, 0)),
        name="attn_fwd", compiler_params=_cparams(("arbitrary",)))(p, kvp, sink, slopes)


def attn_bwd(p, q_off, kvp, sink, slopes, do, T):
    nb = T // BLOCK

    def body(q_ref, kv_ref, sink_ref, slope_ref, do_ref, dq_ref, dkv_ref, dsink_ref):
        c = pl.program_id(0)

        @pl.when(c == 0)
        def _():
            dkv_ref[...] = jnp.zeros_like(dkv_ref)
            dsink_ref[...] = jnp.zeros_like(dsink_ref)

        rows = pl.ds(pl.multiple_of(c * BLOCK, BLOCK), 3 * BLOCK)
        for n in range(N_KV_A):
            k_sl, v_sl, q_sl, heads = _attn_heads(n)
            group = pl.ds(n * ATTN_GROUP * BLOCK, ATTN_GROUP * BLOCK)
            slope = slope_ref[group, :]
            q = jnp.concatenate([q_ref[:, s] for s in q_sl], axis=0)
            do = jnp.concatenate([do_ref[:, s] for s in q_sl], axis=0)
            _, vjp = jax.vjp(lambda q_, kk, vv, sk: attn_block(q_, kk, vv, sk, slope, c, T),
                             q, kv_ref[rows, k_sl], kv_ref[rows, v_sl], sink_ref[group, :])
            dq, dks, dvs, dsk = vjp(do)
            dkv_ref[rows, k_sl] += dks
            dkv_ref[rows, v_sl] += dvs
            for g, (s, h) in enumerate(zip(q_sl, heads)):
                seg = slice(g * BLOCK, (g + 1) * BLOCK)
                dq_ref[:, s] = dq[seg].astype(dq_ref.dtype)
                dsink_ref[h] += jnp.sum(dsk[seg], axis=0, keepdims=True)

    full = lambda a: pl.BlockSpec(a.shape, functools.partial(lambda c, nd: (0,) * nd, nd=a.ndim))
    qspec = pl.BlockSpec((BLOCK, W_A), lambda c: (c, 0))
    return pl.pallas_call(
        body,
        out_shape=[jax.ShapeDtypeStruct((T, W_A), BF16), jax.ShapeDtypeStruct(kvp.shape, F32),
                   jax.ShapeDtypeStruct((N_Q_A, 1, 1), F32)],
        grid=(nb,),
        in_specs=[pl.BlockSpec((BLOCK, W_A), lambda c: (c, q_off // W_A)), full(kvp), full(sink), full(slopes), qspec],
        out_specs=[qspec, full(kvp), pl.BlockSpec((N_Q_A, 1, 1), lambda c: (0, 0, 0))],
        name="attn_bwd", compiler_params=_cparams(("arbitrary",)))(p, kvp, sink, slopes, do)


def mem_fwd(p, q_off, kv, T):
    tr = min(T, 2 * _row_tile(T))
    assert q_off % W_M == 0

    def body(q_ref, kv_ref, o_ref):
        for h in range(N_HEADS_M):
            hs = pl.ds(h * HEAD_DIM_M, HEAD_DIM_M)
            (o,) = mem_tile(q_ref[:, hs], kv_ref[:, hs], kv_ref[:, pl.ds(W_M + h * HEAD_DIM_M, HEAD_DIM_M)])
            o_ref[:, hs] = o

    return pl.pallas_call(
        body, out_shape=jax.ShapeDtypeStruct((T, W_M), F32), grid=(T // tr,),
        in_specs=[pl.BlockSpec((tr, W_M), lambda i: (i, q_off // W_M)), pl.BlockSpec((N_MEM, 2 * W_M), lambda i: (0, 0))],
        out_specs=pl.BlockSpec((tr, W_M), lambda i: (i, 0)),
        name="mem_fwd", compiler_params=_cparams(("arbitrary",)))(p, kv)


def mem_bwd(p, q_off, kv, do, T):
    tr = min(T, 2 * _row_tile(T))

    def body(q_ref, kv_ref, do_ref, dq_ref, dkv_ref):
        @pl.when(pl.program_id(0) == 0)
        def _():
            dkv_ref[...] = jnp.zeros_like(dkv_ref)

        for h in range(N_HEADS_M):
            hs = pl.ds(h * HEAD_DIM_M, HEAD_DIM_M)
            vs = pl.ds(W_M + h * HEAD_DIM_M, HEAD_DIM_M)
            _, vjp = jax.vjp(mem_tile, q_ref[:, hs], kv_ref[:, hs], kv_ref[:, vs])
            dq, dk, dv = vjp((do_ref[:, hs],))
            dq_ref[:, hs] = dq.astype(dq_ref.dtype)
            dkv_ref[:, hs] += dk
            dkv_ref[:, vs] += dv

    kvspec = pl.BlockSpec((N_MEM, 2 * W_M), lambda i: (0, 0))
    return pl.pallas_call(
        body,
        out_shape=[jax.ShapeDtypeStruct((T, W_M), BF16), jax.ShapeDtypeStruct((N_MEM, 2 * W_M), F32)],
        grid=(T // tr,),
        in_specs=[pl.BlockSpec((tr, W_M), lambda i: (i, q_off // W_M)), kvspec, pl.BlockSpec((tr, W_M), lambda i: (i, 0))],
        out_specs=[pl.BlockSpec((tr, W_M), lambda i: (i, 0)), kvspec],
        name="mem_bwd", compiler_params=_cparams(("arbitrary",)))(p, kv, do)


def _scan_const_specs(dk):
    C, L = SCAN_CHUNK, SCAN_LEVELS
    return [pl.BlockSpec((2, (2 + L) * C, C), lambda n: (0, 0, 0)),
            pl.BlockSpec((2, C, (2 + L) * C), lambda n: (0, 0, 0)),
            pl.BlockSpec((2, L * C, dk), lambda n: (0, 0, 0)),
            pl.BlockSpec((2, L * C, C), lambda n: (0, 0, 0))]


def _chunk_spec(src, width, chunk_of):
    arr, sel = src
    if arr.ndim == 2:
        assert sel % width == 0
        return pl.BlockSpec((SCAN_CHUNK * SCAN_SUB, width),
                            functools.partial(lambda n, b: (chunk_of(n), b), b=sel // width))
    return pl.BlockSpec((None, SCAN_CHUNK * SCAN_SUB, width), functools.partial(lambda n, d: (d, chunk_of(n), 0), d=sel))


def _scan_const_args():
    h, ht, qm, pm = _scan_consts()
    return [jnp.asarray(h, BF16), jnp.asarray(ht, BF16), jnp.asarray(qm, F32), jnp.asarray(pm, F32)]


def _full_spec(a):
    return pl.BlockSpec(a.shape, functools.partial(lambda n, nd: (0,) * nd, nd=a.ndim))


def scan_fwd(name, prep, raws, params, heads, dk, dv, T):
    C, S = SCAN_CHUNK, SCAN_SUB
    N = T // (C * S)
    assert dk == C
    Wv = heads * dv
    orders = (lambda n: n, lambda n: N - 1 - n)
    n_raw, n_par = [len(r) for r in raws], [len(p) for p in params]

    def body(*refs):
        pos, raw_refs, par_refs = 0, [], []
        for d in range(2):
            raw_refs.append(refs[pos:pos + n_raw[d]])
            pos += n_raw[d]
        for d in range(2):
            par_refs.append(refs[pos:pos + n_par[d]])
            pos += n_par[d]
        h_ref, ht_ref, qm_ref, pm_ref = refs[pos:pos + 4]
        o_refs, ss_refs, st_ref = refs[pos + 4:pos + 6], refs[pos + 6:pos + 8], refs[pos + 8]

        @pl.when(pl.program_id(0) == 0)
        def _():
            st_ref[...] = jnp.zeros_like(st_ref)

        for d in range(2):
            consts = (qm_ref[d], pm_ref[d])
            pars = [p[...] for p in par_refs[d]]
            for sub in (range(S) if d == 0 else reversed(range(S))):
                rows = pl.ds(sub * C, C)
                q, k, v, g = prep([r[rows, :] for r in raw_refs[d]], pars)
                e = _split_mm(h_ref[d], g)
                tot = jnp.sum(g, axis=0, keepdims=True)
                for h in range(heads):
                    ks, vs = slice(h * dk, (h + 1) * dk), slice(h * dv, (h + 1) * dv)
                    st = st_ref[d, h]
                    ss_refs[d][h, sub] = st
                    o, st_new = scan_chunk(q[:, ks], k[:, ks], v[:, vs], e[:, ks], tot[:, ks], st, *consts)
                    o_refs[d][rows, vs] = o
                    st_ref[d, h] = st_new

    ss_spec = lambda order: pl.BlockSpec((heads, S, dv, dk), lambda n: (0, order(n), 0, 0))
    return pl.pallas_call(
        body,
        out_shape=[jax.ShapeDtypeStruct((T, Wv), F32)] * 2 + [jax.ShapeDtypeStruct((heads, T // C, dv, dk), F32)] * 2,
        grid=(N,),
        in_specs=[_chunk_spec(s, w, orders[d]) for d in range(2) for s, w in raws[d]]
        + [_full_spec(p) for d in range(2) for p in params[d]] + _scan_const_specs(dk),
        out_specs=[pl.BlockSpec((C * S, Wv), lambda n: (orders[0](n), 0)),
                   pl.BlockSpec((C * S, Wv), lambda n: (orders[1](n), 0)), ss_spec(orders[0]), ss_spec(orders[1])],
        scratch_shapes=[pltpu.VMEM((2, heads, dv, dk), F32)],
        name=name, compiler_params=_cparams(("arbitrary",)))(
            *[s[0] for d in range(2) for s, _ in raws[d]], *[p for d in range(2) for p in params[d]], *_scan_const_args())


def scan_bwd(name, prep, raws, params, ss, do, heads, dk, dv, T):
    C, S = SCAN_CHUNK, SCAN_SUB
    N = T // (C * S)
    Wv = heads * dv
    orders = (lambda n: N - 1 - n, lambda n: n)
    n_raw, n_par = [len(r) for r in raws], [len(p) for p in params]

    def body(*refs):
        pos, raw_refs, par_refs, draw_refs, dpar_refs = 0, [], [], [], []
        for group, counts in ((raw_refs, n_raw), (par_refs, n_par)):
            for d in range(2):
                group.append(refs[pos:pos + counts[d]])
                pos += counts[d]
        ss_refs, do_refs = refs[pos:pos + 2], refs[pos + 2:pos + 4]
        h_ref, ht_ref, qm_ref, pm_ref = refs[pos + 4:pos + 8]
        pos += 8
        for group, counts in ((draw_refs, n_raw), (dpar_refs, n_par)):
            for d in range(2):
                group.append(refs[pos:pos + counts[d]])
                pos += counts[d]
        dst_ref = refs[pos]

        @pl.when(pl.program_id(0) == 0)
        def _():
            dst_ref[...] = jnp.zeros_like(dst_ref)
            for d in range(2):
                for r in dpar_refs[d]:
                    r[...] = jnp.zeros_like(r)

        for d in range(2):
            consts = (qm_ref[d], pm_ref[d])
            pars = [p[...] for p in par_refs[d]]
            for sub in (reversed(range(S)) if d == 0 else range(S)):
                rows = pl.ds(sub * C, C)
                (q, k, v, g), prep_vjp = jax.vjp(prep, [r[rows, :] for r in raw_refs[d]], pars)
                e = _split_mm(h_ref[d], g)
                tot = jnp.sum(g, axis=0, keepdims=True)
                dqs, dks, dvs, des, dtots = [], [], [], [], []
                for h in range(heads):
                    ks, vs = slice(h * dk, (h + 1) * dk), slice(h * dv, (h + 1) * dv)
                    _, vjp = jax.vjp(lambda q_, k_, v_, e_, t_, st_: scan_chunk(q_, k_, v_, e_, t_, st_, *consts),
                                     q[:, ks], k[:, ks], v[:, vs], e[:, ks], tot[:, ks], ss_refs[d][h, sub])
                    dq, dk_, dv_, de, dtot, dst = vjp((do_refs[d][rows, vs], dst_ref[d, h]))
                    dst_ref[d, h] = dst
                    for group, val in ((dqs, dq), (dks, dk_), (dvs, dv_), (des, de), (dtots, dtot)):
                        group.append(val)
                cat = lambda parts: jnp.concatenate(parts, axis=-1)
                dg = _split_mm(ht_ref[d], cat(des)) + cat(dtots)
                draws, dpars = prep_vjp((cat(dqs), cat(dks), cat(dvs), dg))
                for r, val in zip(draw_refs[d], draws):
                    r[rows, :] = val.astype(r.dtype)
                for r, val in zip(dpar_refs[d], dpars):
                    r[...] += val

    ss_spec = lambda order: pl.BlockSpec((heads, S, dv, dk), lambda n: (0, order(n), 0, 0))
    row_out = lambda w, order: pl.BlockSpec((C * S, w), lambda n: (order(n), 0))
    return pl.pallas_call(
        body,
        out_shape=[jax.ShapeDtypeStruct((T, w), BF16) for d in range(2) for _, w in raws[d]]
        + [jax.ShapeDtypeStruct(p.shape, F32) for d in range(2) for p in params[d]],
        grid=(N,),
        in_specs=[_chunk_spec(s, w, orders[d]) for d in range(2) for s, w in raws[d]]
        + [_full_spec(p) for d in range(2) for p in params[d]]
        + [ss_spec(orders[0]), ss_spec(orders[1]), _chunk_spec(do, Wv, orders[0]), _chunk_spec(do, Wv, orders[1])]
        + _scan_const_specs(dk),
        out_specs=[row_out(w, orders[d]) for d in range(2) for _, w in raws[d]]
        + [_full_spec(p) for d in range(2) for p in params[d]],
        scratch_shapes=[pltpu.VMEM((2, heads, dv, dk), F32)],
        name=name, compiler_params=_cparams(("arbitrary",)))(
            *[s[0] for d in range(2) for s, _ in raws[d]], *[p for d in range(2) for p in params[d]],
            ss[0], ss[1], do[0], do[0], *_scan_const_args())


def final_call(x, g, target, T):
    tr = _row_tile(T)

    def tile(xv, gv, tv):
        y = _rms(xv, gv)
        err = (y - tv) ** 2
        return jnp.sum(jnp.sum(err, axis=-1, keepdims=True), axis=0, keepdims=True) * (0.5 / D_MODEL)

    def body(x_ref, g_ref, t_ref, loss_ref, dx_ref, dg_ref):
        i = pl.program_id(0)
        tv = t_ref[...]
        lv, vjp = jax.vjp(lambda a, b: tile(a, b, tv), x_ref[...], g_ref[...])
        dx, dg = vjp(jnp.ones((1, 1), F32))
        dx_ref[...] = dx

        @pl.when(i == 0)
        def _():
            loss_ref[...] = jnp.zeros_like(loss_ref)
            dg_ref[...] = jnp.zeros_like(dg_ref)

        loss_ref[...] += jnp.broadcast_to(lv, loss_ref.shape)
        dg_ref[...] += dg

    return pl.pallas_call(
        body,
        out_shape=[jax.ShapeDtypeStruct((8, 128), F32), jax.ShapeDtypeStruct((T, D_MODEL), F32),
                   jax.ShapeDtypeStruct((1, D_MODEL), F32)],
        grid=(T // tr,),
        in_specs=[pl.BlockSpec((tr, D_MODEL), lambda i: (i, 0)), pl.BlockSpec((1, D_MODEL), lambda i: (0, 0)),
                  pl.BlockSpec((tr, D_MODEL), lambda i: (i, 0))],
        out_specs=[pl.BlockSpec((8, 128), lambda i: (0, 0)), pl.BlockSpec((tr, D_MODEL), lambda i: (i, 0)),
                   pl.BlockSpec((1, D_MODEL), lambda i: (0, 0))],
        name="final_loss", compiler_params=_cparams(("arbitrary",)))(x, g, target)


def adamw_call(w, g, m, v):
    shape = w.shape
    c = shape[-1]
    r = int(np.prod(shape[:-1])) if len(shape) > 1 else 1
    tr = r if r <= 256 else 256
    assert r % tr == 0

    def body(w_ref, g_ref, m_ref, v_ref, d_ref, nm_ref, nv_ref):
        gv = g_ref[...]
        nm = ADAM_B1 * m_ref[...] + (1.0 - ADAM_B1) * gv
        nv = ADAM_B2 * v_ref[...] + (1.0 - ADAM_B2) * jnp.square(gv)
        m_hat = nm / (1.0 - ADAM_B1 ** ADAM_STEP)
        v_hat = nv / (1.0 - ADAM_B2 ** ADAM_STEP)
        d_ref[...] = -ADAM_LR * (m_hat / (jnp.sqrt(v_hat) + ADAM_EPS) + ADAM_WD * w_ref[...])
        nm_ref[...] = nm
        nv_ref[...] = nv

    spec = pl.BlockSpec((tr, c), lambda i: (i, 0))
    outs = pl.pallas_call(body, out_shape=[jax.ShapeDtypeStruct((r, c), F32)] * 3, grid=(r // tr,),
                          in_specs=[spec] * 4, out_specs=[spec] * 3, name="adamw",
                          compiler_params=_cparams(("arbitrary",)))(*(t.reshape(r, c) for t in (w, g, m, v)))
    return tuple(o.reshape(shape) for o in outs)


def adamw_halves(w, mine, other, m, v, c):
    L, R, C = w.shape
    by_cols = mine.shape[-1] != C
    if by_cols:
        tile, nbh = (R, C // 2), 1
        full_idx = lambda l, i: (l, 0, i)
    else:
        rh = R // 2
        tr = rh if rh <= 256 else rh // 2
        assert tr % 8 == 0
        tile, nbh = (tr, C), rh // tr
        full_idx = lambda l, i: (l, i, 0)

    def body(c_ref, w_ref, a_ref, b_ref, m_ref, v_ref, g_ref, d_ref, nm_ref, nv_ref):
        is_mine = (pl.program_id(1) // nbh) == c_ref[0]
        gv = jnp.where(is_mine, a_ref[...], b_ref[...])
        nm = ADAM_B1 * m_ref[...] + (1.0 - ADAM_B1) * gv
        nv = ADAM_B2 * v_ref[...] + (1.0 - ADAM_B2) * jnp.square(gv)
        m_hat = nm / (1.0 - ADAM_B1 ** ADAM_STEP)
        v_hat = nv / (1.0 - ADAM_B2 ** ADAM_STEP)
        g_ref[...] = gv
        d_ref[...] = -ADAM_LR * (m_hat / (jnp.sqrt(v_hat) + ADAM_EPS) + ADAM_WD * w_ref[...])
        nm_ref[...] = nm
        nv_ref[...] = nv

    full = pl.BlockSpec((None,) + tile, lambda l, i, c_ref: full_idx(l, i))
    half = pl.BlockSpec((None,) + tile, lambda l, i, c_ref: (l, i % nbh, 0))
    grid_spec = pltpu.PrefetchScalarGridSpec(num_scalar_prefetch=1, grid=(L, 2 * nbh),
                                             in_specs=[full, half, half, full, full], out_specs=[full] * 4)
    return pl.pallas_call(body, out_shape=[jax.ShapeDtypeStruct(w.shape, F32)] * 4, grid_spec=grid_spec,
                          name="adamw_halves", compiler_params=_cparams(("arbitrary", "arbitrary")))(c, w, mine, other, m, v)


def sum_devices(g64):
    def body(x_ref, o_ref):
        acc = x_ref[0:8, :]
        for d in range(1, 8):
            acc = acc + x_ref[8 * d:8 * d + 8, :]
        o_ref[...] = acc

    return pl.pallas_call(body, out_shape=jax.ShapeDtypeStruct((8, D_MODEL), F32), name="sum_devices")(g64)


def _half_tile(rh):
    if rh <= 512:
        return rh
    return next(rh // d for d in range(2, rh) if rh % d == 0 and (rh // d) % 16 == 0 and rh // d <= 512)


def _half_geometry(full_shape, half_shape):
    R, C = full_shape[-2:]
    if half_shape[-1] != C:
        return (R, C // 2), 1, lambda i, c: (0, c)
    tr = _half_tile(R // 2)
    nblk = (R // 2) // tr
    return (tr, C), nblk, lambda i, c: (i + c * nblk, 0)


def _work_items(counts):
    starts = [int(v) for v in np.cumsum([0] + list(counts[:-1]))]
    local = lambda a, s: jnp.clip(s - starts[a], 0, counts[a] - 1)
    return starts, int(sum(counts)), local


def add_sibling(gs, recvs, c, out_dtypes):
    n = len(gs)
    geo = [_half_geometry(g.shape, r.shape) for g, r in zip(gs, recvs)]
    counts = [4 * nblk for _, nblk, _ in geo]
    starts, total, local = _work_items(counts)

    def body(c_ref, *refs):
        s = pl.program_id(0)
        for a in range(n):
            g_ref, r_ref, o_ref = refs[a], refs[n + a], refs[2 * n + a]

            @pl.when((s >= starts[a]) & (s < starts[a] + counts[a]))
            def _():
                o_ref[...] = (g_ref[...] + r_ref[...]).astype(o_ref.dtype)

    def own_idx(s, c_ref, a):
        _, nblk, own = geo[a]
        k = local(a, s)
        return (k // nblk,) + own(k % nblk, c_ref[0])

    def half_idx(s, c_ref, a):
        k = local(a, s)
        return (k // geo[a][1], k % geo[a][1], 0)

    halves = [pl.BlockSpec((None,) + geo[a][0], functools.partial(half_idx, a=a)) for a in range(n)]
    grid_spec = pltpu.PrefetchScalarGridSpec(
        num_scalar_prefetch=1, grid=(total,),
        in_specs=[pl.BlockSpec((None,) + geo[a][0], functools.partial(own_idx, a=a)) for a in range(n)] + halves,
        out_specs=halves)
    return pl.pallas_call(body, out_shape=[jax.ShapeDtypeStruct(r.shape, dt) for r, dt in zip(recvs, out_dtypes)],
                          grid_spec=grid_spec, name="rs_add_sibling",
                          compiler_params=_cparams(("arbitrary",)))(c, *gs, *recvs)


def add_chips(gs, recvs, r3s, place):
    n = len(gs)
    geo = [_half_geometry(g.shape, r.shape) for g, r in zip(gs, recvs)]
    counts = [nblk for _, nblk, _ in geo]
    starts, total, local = _work_items(counts)

    def body(p_ref, *refs):
        s = pl.program_id(0)
        up = lambda r: r[...].astype(F32)
        for a in range(n):
            g_ref, s_ref, o_ref = refs[a], refs[n + a], refs[5 * n + a]
            a_ref, b_ref, c_ref = refs[2 * n + 3 * a:2 * n + 3 * a + 3]

            @pl.when((s >= starts[a]) & (s < starts[a] + counts[a]))
            def _():
                o_ref[...] = (((g_ref[...] + up(s_ref)) + up(a_ref)) + up(b_ref)) + up(c_ref)

    own_idx = lambda s, p_ref, a: (p_ref[0],) + geo[a][2](local(a, s), p_ref[1])
    sib_idx = lambda s, p_ref, a: (p_ref[0], local(a, s), 0)
    chip_idx = lambda s, p_ref, a, k: (k, local(a, s), 0)
    spec = lambda a, idx, **kw: pl.BlockSpec((None,) + geo[a][0], functools.partial(idx, a=a, **kw))
    grid_spec = pltpu.PrefetchScalarGridSpec(
        num_scalar_prefetch=1, grid=(total,),
        in_specs=[spec(a, own_idx) for a in range(n)] + [spec(a, sib_idx) for a in range(n)]
        + [spec(a, chip_idx, k=k) for a in range(n) for k in range(3)],
        out_specs=[pl.BlockSpec(geo[a][0], functools.partial(lambda s, p_ref, a: (local(a, s), 0), a=a))
                   for a in range(n)])
    return pl.pallas_call(body, out_shape=[jax.ShapeDtypeStruct(r.shape[1:], F32) for r in recvs],
                          grid_spec=grid_spec, name="rs_add_chips", compiler_params=_cparams(("arbitrary",)))(
                              place, *gs, *recvs, *[r for r3 in r3s for r in (r3, r3, r3)])


def _remote(src, dst, ssem, rsem, dev):
    return pltpu.make_async_remote_copy(src_ref=src, dst_ref=dst, send_sem=ssem, recv_sem=rsem,
                                        device_id=dev, device_id_type=pl.DeviceIdType.MESH)


def _mesh_places():
    x, y, c = lax.axis_index("x"), lax.axis_index("y"), lax.axis_index("c")
    chips = [(1 - x, y), (x, 1 - y), (1 - x, 1 - y)]
    return x, y, c, (x, y, 1 - c), chips


def _hbm_specs(n):
    return [pl.BlockSpec(memory_space=pltpu.HBM) for _ in range(n)]


def _gather_body(ins, outs, n_split, send_sems, recv_sems, handshake):
    x, y, c, sibling, chips = _mesh_places()
    mine = 2 * x + y
    if handshake:
        barrier = pltpu.get_barrier_semaphore()
        peers = [sibling] + [(*chip, c) for chip in chips]
        for peer in peers:
            pl.semaphore_signal(barrier, inc=1, device_id=peer, device_id_type=pl.DeviceIdType.MESH)
        pl.semaphore_wait(barrier, len(peers))

    def half(a, chip_idx, which):
        rh = ins[a].shape[0] // 2
        return outs[a].at[chip_idx, pl.ds(which * rh, rh), :]

    sent = []
    for a in range(len(ins)):
        for k, chip in enumerate(chips):
            if a < n_split:
                rh = ins[a].shape[0] // 2
                src, dst = ins[a].at[pl.ds(c * rh, rh), :], half(a, mine, c)
            else:
                src, dst = ins[a], outs[a].at[mine]
            sent.append(_remote(src, dst, send_sems.at[a, k], recv_sems.at[a, k], (*chip, c)))
    for cp in sent:
        cp.start()
    for a in range(len(ins)):
        for k, chip in enumerate(chips):
            j = 2 * chip[0] + chip[1]
            region = half(a, j, c) if a < n_split else outs[a].at[j]
            _remote(region, region, send_sems.at[a, k], recv_sems.at[a, k], (*chip, c)).wait_recv()
            if a < n_split:
                fwd = _remote(region, region, send_sems.at[a, 3 + k], recv_sems.at[a, 3 + k], sibling)
                fwd.start()
                sent.append(fwd)
    for a in range(n_split):
        for k, chip in enumerate(chips):
            region = half(a, 2 * chip[0] + chip[1], 1 - c)
            _remote(region, region, send_sems.at[a, 3 + k], recv_sems.at[a, 3 + k], sibling).wait_recv()
    for cp in sent:
        cp.wait_send()


def gather_weights(shards, small):
    arrs = list(shards) + [small]
    n = len(arrs)

    def body(*refs):
        _gather_body(refs[:n], refs[n:2 * n], n - 1, refs[2 * n], refs[2 * n + 1], handshake=False)

    return pl.pallas_call(
        body, out_shape=[jax.ShapeDtypeStruct((4,) + a.shape, a.dtype) for a in arrs],
        in_specs=_hbm_specs(n), out_specs=_hbm_specs(n),
        scratch_shapes=[pltpu.SemaphoreType.DMA((n, 6)), pltpu.SemaphoreType.DMA((n, 6))],
        name="gather_weights")(*arrs)


def gather_weights_async(shards):
    n = len(shards)

    def body(*refs):
        _gather_body(refs[:n], refs[n:2 * n], n, refs[2 * n], refs[2 * n + 1], handshake=True)

    return pl.kernel(
        body, out_type=[jax.ShapeDtypeStruct((4,) + a.shape, a.dtype) for a in shards],
        mesh=plsc.ScalarSubcoreMesh(axis_name="seq", num_cores=1),
        scratch_types=[pltpu.SemaphoreType.DMA((n, 6)), pltpu.SemaphoreType.DMA((n, 6))],
        compiler_params=pltpu.CompilerParams(collective_id=1), name="gather_weights_async")(*shards)


def _sequencer_call(name, body, out_type, sem_shape, collective_id, args):
    return pl.kernel(
        body, out_type=out_type, mesh=plsc.ScalarSubcoreMesh(axis_name="seq", num_cores=1),
        scratch_types=[pltpu.SemaphoreType.DMA(sem_shape), pltpu.SemaphoreType.DMA(sem_shape)],
        compiler_params=pltpu.CompilerParams(collective_id=collective_id), name=name)(*args)


def _handshake(peers):
    barrier = pltpu.get_barrier_semaphore()
    for peer in peers:
        pl.semaphore_signal(barrier, inc=1, device_id=peer, device_id_type=pl.DeviceIdType.MESH)
    pl.semaphore_wait(barrier, len(peers))


def exchange_siblings(name, srcs, axes, collective_id):
    n = len(srcs)

    def body(*refs):
        ins, outs = refs[:n], refs[n:2 * n]
        send_sems, recv_sems = refs[2 * n:]
        x, y, c, sibling, chips = _mesh_places()
        _handshake([sibling])
        cps = []
        for a in range(n):
            src = ins[a]
            if axes[a] is not None:
                half = src.shape[axes[a]] // 2
                theirs = pl.ds((1 - c) * half, half)
                src = src.at[:, theirs, :] if axes[a] == 1 else src.at[:, :, theirs]
            cps.append(_remote(src, outs[a], send_sems.at[a], recv_sems.at[a], sibling))
        for cp in cps:
            cp.start()
        for cp in cps:
            cp.wait()

    def shape(g, axis):
        return g.shape if axis is None else tuple(d // 2 if k == axis else d for k, d in enumerate(g.shape))

    return _sequencer_call(name, body, [jax.ShapeDtypeStruct(shape(g, ax), g.dtype) for g, ax in zip(srcs, axes)],
                           (n,), collective_id, srcs)


def exchange_chips(name, s1s, collective_id):
    n = len(s1s)

    def body(*refs):
        ins, outs = refs[:n], refs[n:2 * n]
        send_sems, recv_sems = refs[2 * n:]
        x, y, c, sibling, chips = _mesh_places()
        _handshake([(*chip, c) for chip in chips])
        cps = []
        for a in range(n):
            for k, chip in enumerate(chips):
                cps.append(_remote(ins[a].at[2 * chip[0] + chip[1]], outs[a].at[k], send_sems.at[a, k],
                                   recv_sems.at[a, k], (*chip, c)))
        for cp in cps:
            cp.start()
        for cp in cps:
            cp.wait()

    return _sequencer_call(name, body, [jax.ShapeDtypeStruct((3,) + s.shape[1:], s.dtype) for s in s1s], (n, 3),
                           collective_id, s1s)


def allgather_small(v):
    m_per = v.shape[0]

    def body(x_ref, out_ref, send_sems, recv_sems, local_sem):
        x, y, c, sibling, chips = _mesh_places()
        me = (x, y, c)

        def rows(px, py, pc):
            return out_ref.at[pl.ds((4 * px + 2 * py + pc) * m_per, m_per), :]

        def copy(k, block, to, src=None):
            return _remote(rows(*block) if src is None else src, rows(*block), send_sems.at[k], recv_sems.at[k], to)

        mine = pltpu.make_async_copy(x_ref, rows(*me), local_sem)
        mine.start()
        first = [copy(0, me, sibling, src=x_ref)]
        first += [copy(1 + j, me, (*chip, c), src=x_ref) for j, chip in enumerate(chips)]
        for cp in first:
            cp.start()
        passed = [copy(4 + j, (*chip, c), sibling) for j, chip in enumerate(chips)]
        for j, chip in enumerate(chips):
            copy(1 + j, (*chip, c), me).wait_recv()
            passed[j].start()
        copy(0, sibling, me).wait_recv()
        for j, chip in enumerate(chips):
            copy(4 + j, (*chip, 1 - c), me).wait_recv()
        for cp in first + passed:
            cp.wait_send()
        mine.wait()

    return pl.pallas_call(
        body, out_shape=jax.ShapeDtypeStruct((8 * m_per, v.shape[1]), v.dtype),
        in_specs=[pl.BlockSpec(memory_space=pltpu.VMEM)], out_specs=pl.BlockSpec(memory_space=pltpu.VMEM),
        scratch_shapes=[pltpu.SemaphoreType.DMA((7,)), pltpu.SemaphoreType.DMA((7,)), pltpu.SemaphoreType.DMA],
        name="allgather_small")(v)


def rms_res_tile(x, g):
    return (_rms(x, g), x)


def _lower_bounds(lb_param):
    lbs = jax.nn.softmax(lb_param.astype(F32), axis=0)
    return jnp.cumsum(lbs, axis=0) - lbs[0]


def _even_fwd(x, i, W, lower, kv, slopes, T):
    O = EVEN_OFF
    g = W["norm_even"][i].reshape(1, D_MODEL)
    (h,) = rows_call("rms_fwd", rms_tile, T, [("row", x, 0, D_MODEL), ("full", g)], [D_MODEL], [BF16])
    p = matmul("mm_in_e", h, W["w_in_e"][i], "nn")
    kvp = jnp.pad(p[:, O["kA"]:O["kA"] + 2 * W_KV_A], ((BLOCK, BLOCK), (0, 0)))
    sink = jnp.repeat(W["sink"][i], BLOCK).reshape(N_Q_A * BLOCK, 1)
    a = attn_fwd(p, O["qA"], kvp, sink, slopes, T)
    scan_raws = [[((p, O["qB"]), W_B), ((p, O[z]), W_B), ((p, O["iB"]), W_B)] for z in ("zf", "zb")]
    scan_pars = [[lower[i][0:1]], [lower[i][1:2]]]
    o_f, o_b, ss_f, ss_b = scan_fwd("scan_fwd_h", hgrn_prep, scan_raws, scan_pars, N_HEADS_B, HEAD_DIM_B, HEAD_DIM_B, T)
    mo = mem_fwd(p, O["qM"], kv, T)
    hg = W["hgrn_norm"][i].reshape(1, W_B)
    post_ins = [("row", a, 0, W_A), ("row", o_f, 0, W_B), ("row", o_b, 0, W_B), ("row", mo, 0, W_M),
                ("row", p, O["gA"], W_A), ("row", p, O["gB"], W_B), ("row", p, O["gM"], W_M), ("full", hg)]
    (mix,) = rows_call("even_post_fwd", even_post_tile, T, post_ins, [MIX], [BF16])
    x_new = matmul("mm_out", mix, W["w_out_e"][i], "nn", add=x)
    return x_new, dict(x=x, g=g, h=h, p=p, kvp=kvp, sink=sink, scan_raws=scan_raws, scan_pars=scan_pars,
                       ss=(ss_f, ss_b), post_ins=post_ins, mix=mix)


def _add2(a, b):
    return a.astype(F32) + b.astype(F32)


def _assemble_even(dqA, dgA, dqB_f, dqB_b, dzf, dzb, diB_f, diB_b, dgB, dqM, dgM, dkvA):
    parts = [dqA, dgA, _add2(dqB_f, dqB_b), dzf, dzb, _add2(diB_f, diB_b), dgB, dqM, dgM, dkvA]
    return (jnp.concatenate([t.astype(BF16) for t in parts], axis=-1),)


def _even_bwd(dxo, sv, i, W, kv, slopes, T, sync):
    O = EVEN_OFF
    p = sv["p"]
    dmix = matmul("mm_dmix", dxo, W["w_out_e"][i], "nt")
    dwo = matmul("mm_dwo", sv["mix"], dxo, "tn")
    da, dof, dmo, dgA, dgB, dgM, dhg = rows_vjp_call("even_post_bwd", even_post_tile, T, sv["post_ins"],
                                                      [[("row", dmix, 0, MIX)]], skip=(2,), narrow=(4, 5, 6))
    dqA, dkvp, dsink = attn_bwd(p, O["qA"], sv["kvp"], sv["sink"], slopes, da, T)
    dkvA = dkvp[BLOCK:-BLOCK]
    dqB_f, dzf, diB_f, dqB_b, dzb, diB_b, dlow_f, dlow_b = scan_bwd(
        "scan_bwd_h", hgrn_prep, sv["scan_raws"], sv["scan_pars"], sv["ss"], (dof, 0), N_HEADS_B, HEAD_DIM_B, HEAD_DIM_B, T)
    dqB_f = sync(dqB_f)
    row = lambda arr, w: ("row", arr, 0, w)
    dlow = jnp.concatenate([dlow_f, dlow_b], axis=0)
    dqM, dkv = mem_bwd(p, O["qM"], kv, dmo, T)
    (dp,) = rows_call("even_dp", _assemble_even, T,
                      [row(dqA, W_A), row(dgA, W_A), row(dqB_f, W_B), row(dqB_b, W_B), row(dzf, W_B), row(dzb, W_B),
                       row(diB_f, W_B), row(diB_b, W_B), row(dgB, W_B), row(dqM, W_M), row(dgM, W_M),
                       row(dkvA, 2 * W_KV_A)],
                      [EVEN_IN], [BF16])
    dh = matmul("mm_dh_e", dp, W["w_in_e"][i], "nt")
    dwi = matmul("mm_dwi_e", sv["h"], dp, "tn")
    dx, dg = rows_vjp_call("rms_res_bwd", rms_res_tile, T, [("row", sv["x"], 0, D_MODEL), ("full", sv["g"])],
                           [[("row", dh, 0, D_MODEL)], [("row", dxo, 0, D_MODEL)]])
    return dx, dict(w_in=dwi, w_out=dwo, norm=dg[0], sink=dsink.reshape(N_Q_A), low=dlow, hg=dhg[0], kv=dkv)


def _pad_gate_up(w_up):
    z = jnp.zeros((2, 128, WK_C), F32)
    z = z.at[0, 0:GATE_RANK].set(w_up[0])
    return z.at[1, GATE_RANK:2 * GATE_RANK].set(w_up[1])


def _odd_fwd(x, i, W, kv, T):
    O = ODD_OFF
    g = W["norm_odd"][i].reshape(1, D_MODEL)
    (h,) = rows_call("rms_fwd", rms_tile, T, [("row", x, 0, D_MODEL), ("full", g)], [D_MODEL], [BF16])
    p = matmul("mm_in_o", h, W["w_in_o"][i], "nn")
    wup = _pad_gate_up(W["w_gate_up"][i])
    one_dir = [((p, O["qC"]), WK_C), ((p, O["kC"]), WK_C), ((p, O["vC"]), WV_C), ((p, O["rr"]), 128)]
    scan_raws = [one_dir, one_dir]
    scan_pars = [[wup[d], W["b_gate"][i][d:d + 1]] for d in range(2)]
    o_f, o_b, ss_f, ss_b = scan_fwd("scan_fwd_g", gla_prep, scan_raws, scan_pars, N_HEADS_C, DK_C, DV_C, T)
    mo = mem_fwd(p, O["qM"], kv, T)
    gg = W["gla_norm"][i].reshape(1, WV_C)
    post_ins = [("row", o_f, 0, WV_C), ("row", o_b, 0, WV_C), ("row", mo, 0, W_M),
                ("row", p, O["gC"], WV_C), ("row", p, O["gM"], W_M), ("full", gg)]
    (mix,) = rows_call("odd_post_fwd", odd_post_tile, T, post_ins, [MIX], [BF16])
    x_new = matmul("mm_out", mix, W["w_out_o"][i], "nn", add=x)
    return x_new, dict(x=x, g=g, h=h, p=p, scan_raws=scan_raws, scan_pars=scan_pars, ss=(ss_f, ss_b),
                       post_ins=post_ins, mix=mix)


def _assemble_odd(dq0, dq1, dk0, dk1, dv0, dv1, dgC, dqM, dgM, dr0, dr1):
    parts = [_add2(dq0, dq1), _add2(dk0, dk1), _add2(dv0, dv1), dgC, dqM, dgM, _add2(dr0, dr1)]
    return (jnp.concatenate([t.astype(BF16) for t in parts], axis=-1),)


def _odd_bwd(dxo, sv, i, W, kv, T, sync):
    O = ODD_OFF
    p = sv["p"]
    dmix = matmul("mm_dmix", dxo, W["w_out_o"][i], "nt")
    dwo = matmul("mm_dwo", sv["mix"], dxo, "tn")
    dof, dmo, dgC, dgM, dgg = rows_vjp_call("odd_post_bwd", odd_post_tile, T, sv["post_ins"],
                                            [[("row", dmix, 0, MIX)]], skip=(1,), narrow=(3, 4))
    dqf, dkf, dvf, dr_f, dqb, dkb, dvb, dr_b, dwup_f, dbg_f, dwup_b, dbg_b = scan_bwd(
        "scan_bwd_g", gla_prep, sv["scan_raws"], sv["scan_pars"], sv["ss"], (dof, 0), N_HEADS_C, DK_C, DV_C, T)
    dqf = sync(dqf)
    row = lambda arr, w: ("row", arr, 0, w)
    dqM, dkv = mem_bwd(p, O["qM"], kv, dmo, T)
    (dp,) = rows_call("odd_dp", _assemble_odd, T,
                      [row(dqf, WK_C), row(dqb, WK_C), row(dkf, WK_C), row(dkb, WK_C), row(dvf, WV_C), row(dvb, WV_C),
                       row(dgC, WV_C), row(dqM, W_M), row(dgM, W_M), row(dr_f, 128), row(dr_b, 128)],
                      [ODD_PAD], [BF16])
    dh = matmul("mm_dh_o", dp, W["w_in_o"][i], "nt")
    dwi = matmul("mm_dwi_o", sv["h"], dp, "tn")
    dx, dg = rows_vjp_call("rms_res_bwd", rms_res_tile, T, [("row", sv["x"], 0, D_MODEL), ("full", sv["g"])],
                           [[("row", dh, 0, D_MODEL)], [("row", dxo, 0, D_MODEL)]])
    dw_up = jnp.stack([dwup_f[0:GATE_RANK], dwup_b[GATE_RANK:2 * GATE_RANK]])
    dbg = jnp.concatenate([dbg_f, dbg_b], axis=0)
    return dx, dict(w_in=dwi, w_out=dwo, norm=dg[0], w_up=dw_up, b_gate=dbg, gg=dgg[0], kv=dkv)


def local_step(x, mem, target, W, later=None, on_layer_grads=None, sync=lambda a: a):
    T = x.shape[0]
    slopes = jnp.repeat(2.0 ** (-8.0 * jnp.arange(1, N_Q_A + 1, dtype=F32) / N_Q_A), BLOCK).reshape(N_Q_A * BLOCK, 1)
    lower, lower_vjp = jax.vjp(_lower_bounds, W["lb_param"])
    mem_g = W["mem_norm"].reshape(1, D_MODEL)
    (mem_n,) = rows_call("mem_rms_fwd", rms_tile, N_MEM, [("row", mem, 0, D_MODEL), ("full", mem_g)], [D_MODEL], [BF16])
    kvs, saved = [], []
    for l in range(DEPTH):
        if l == 1 and later is not None:
            x, W = later(x, W)
        kvs.append(matmul("mm_kv", mem_n, W["w_kv"][l], "nn"))
        if l % 2 == 0:
            x, sv = _even_fwd(x, l // 2, W, lower, kvs[l], slopes, T)
        else:
            x, sv = _odd_fwd(x, l // 2, W, kvs[l], T)
        saved.append(sv)
    loss, dx, dgf = final_call(x, W["final_norm"].reshape(1, D_MODEL), target, T)
    per = [None] * DEPTH
    dmem_n = None
    for l in reversed(range(DEPTH)):
        if l % 2 == 0:
            dx, per[l] = _even_bwd(dx, saved[l], l // 2, W, kvs[l], slopes, T, sync)
        else:
            dx, per[l] = _odd_bwd(dx, saved[l], l // 2, W, kvs[l], T, sync)
        per[l]["w_kv"] = matmul("mm_dwkv", mem_n, per[l]["kv"], "tn")
        dmem_n = matmul("mm_dmem", per[l]["kv"], W["w_kv"][l], "nt", add=dmem_n)
        if on_layer_grads is not None:
            dx = on_layer_grads(l, dx, per[l])
    dw_kv = [per[l]["w_kv"] for l in range(DEPTH)]
    (dmem_norm,) = rows_vjp_call("mem_rms_bwd", rms_tile, N_MEM, [("row", mem, 0, D_MODEL), ("full", mem_g)],
                                 [[("row", dmem_n, 0, D_MODEL)]], skip=(0,))
    ev, od = (per[0], per[2]), (per[1], per[3])
    (d_lb,) = lower_vjp(jnp.stack([e["low"] for e in ev]))
    grads = dict(
        w_in_e=jnp.stack([e["w_in"] for e in ev]), w_in_o=jnp.stack([o["w_in"] for o in od]),
        w_out_e=jnp.stack([e["w_out"] for e in ev]), w_out_o=jnp.stack([o["w_out"] for o in od]),
        w_kv=jnp.stack(dw_kv), norm_even=jnp.stack([e["norm"] for e in ev]), sink=jnp.stack([e["sink"] for e in ev]),
        lb_param=d_lb, hgrn_norm=jnp.stack([e["hg"] for e in ev]), norm_odd=jnp.stack([o["norm"] for o in od]),
        w_gate_up=jnp.stack([o["w_up"] for o in od]), b_gate=jnp.stack([o["b_gate"] for o in od]),
        gla_norm=jnp.stack([o["gg"] for o in od]), mem_norm=dmem_norm[0], final_norm=dgf[0])
    return loss, dx, grads


SMALL_SPECS = (("lb_param", (2, 2, 128)), ("norm_odd", (2, 256)), ("w_gate_up", (2, 2, 16, 128)),
               ("b_gate", (2, 2, 128)), ("gla_norm", (2, 256)))
SMALL_ROWS = 80


def _pack_small_local(d):
    return jnp.concatenate([d[n].reshape(-1) for n, _ in SMALL_SPECS]).reshape(SMALL_ROWS, 128)


def _unpack_small_local(b):
    flat, out, o = b.reshape(-1), {}, 0
    for n, shp in SMALL_SPECS:
        sz = int(np.prod(shp))
        out[n] = flat[o:o + sz].reshape(shp)
        o += sz
    return out


def _unpack_small_full(g4):
    per = [_unpack_small_local(g4[j]) for j in range(4)]
    return {n: jnp.concatenate([per[j][n] for j in range(4)], axis=-1) for n, _ in SMALL_SPECS}


def _pack_small_blocks(full):
    blocks = []
    for j in range(4):
        blocks.append(_pack_small_local({n: full[n][..., j * shp[-1]:(j + 1) * shp[-1]] for n, shp in SMALL_SPECS}))
    return jnp.stack(blocks)


def _cols(t, order, off, widths):
    return [t[..., off[n]:off[n] + widths[n]] for n in order]


EVEN_REF_ORDER = ("qA", "kA", "vA", "gA", "qB", "zf", "zb", "iB", "gB", "qM", "gM")
ODD_REF_ORDER = ("qC", "kC", "vC", "gC", "rr", "qM", "gM")


def _layer_weights(l, g_in, g_out, g_kv):
    t = g_in.transpose(1, 0, 2).reshape(D_MODEL, -1)
    if l % 2 == 0:
        w_in = jnp.concatenate(_cols(t, EVEN_ORDER, EVEN_REF_OFF, EVEN_W), axis=-1)
    else:
        w_in = jnp.concatenate(_cols(t, ODD_ORDER, ODD_REF_OFF, ODD_W) + [jnp.zeros((D_MODEL, ODD_PAD - ODD_IN), BF16)],
                               axis=-1)
    return w_in, g_out.reshape(MIX, D_MODEL), g_kv.reshape(D_MODEL, 2 * W_M)


def _layer_grad_blocks(l, gl):
    if l % 2 == 0:
        t = jnp.concatenate(_cols(gl["w_in"], EVEN_REF_ORDER, EVEN_OFF, EVEN_W), axis=-1)
    else:
        t = jnp.concatenate(_cols(gl["w_in"], ODD_REF_ORDER, ODD_OFF, ODD_W), axis=-1)
    b_in = t.reshape(D_MODEL, 4, -1).transpose(1, 2, 0)
    return [b_in, gl["w_out"].reshape(4, MIX // 4, D_MODEL), gl["w_kv"].reshape(4, D_MODEL // 4, 2 * W_M)]


WEIGHT_NAMES = ("norm_even", "w_in_even", "sink", "lb_param", "hgrn_norm", "w_out_even", "norm_odd", "w_in_odd",
                "w_gate_up", "b_gate", "gla_norm", "w_out_odd", "mem_norm", "w_mem_kv", "final_norm")


def kernel(x, mem, norm_even, w_in_even, sink, lb_param, hgrn_norm, w_out_even, norm_odd, w_in_odd, w_gate_up, b_gate, gla_norm, w_out_odd, mem_norm, w_mem_kv, final_norm, loss_target, m_norm_even, m_w_in_even, m_sink, m_lb_param, m_hgrn_norm, m_w_out_even, m_norm_odd, m_w_in_odd, m_w_gate_up, m_b_gate, m_gla_norm, m_w_out_odd, m_mem_norm, m_w_mem_kv, m_final_norm, v_norm_even, v_w_in_even, v_sink, v_lb_param, v_hgrn_norm, v_w_out_even, v_norm_odd, v_w_in_odd, v_w_gate_up, v_b_gate, v_gla_norm, v_w_out_odd, v_mem_norm, v_w_mem_kv, v_final_norm):
    w = dict(zip(WEIGHT_NAMES, (norm_even, w_in_even, sink, lb_param, hgrn_norm, w_out_even, norm_odd, w_in_odd,
                                w_gate_up, b_gate, gla_norm, w_out_odd, mem_norm, w_mem_kv, final_norm)))
    m = dict(zip(WEIGHT_NAMES, (m_norm_even, m_w_in_even, m_sink, m_lb_param, m_hgrn_norm, m_w_out_even, m_norm_odd,
                                m_w_in_odd, m_w_gate_up, m_b_gate, m_gla_norm, m_w_out_odd, m_mem_norm, m_w_mem_kv,
                                m_final_norm)))
    v = dict(zip(WEIGHT_NAMES, (v_norm_even, v_w_in_even, v_sink, v_lb_param, v_hgrn_norm, v_w_out_even, v_norm_odd,
                                v_w_in_odd, v_w_gate_up, v_b_gate, v_gla_norm, v_w_out_odd, v_mem_norm, v_w_mem_kv,
                                v_final_norm)))
    ci = lax.axis_index("c").astype(jnp.int32).reshape(1)
    chip = (2 * lax.axis_index("x") + lax.axis_index("y")).astype(jnp.int32).reshape(1)

    shards = []
    for l in range(DEPTH):
        names = ("w_in_even", "w_out_even") if l % 2 == 0 else ("w_in_odd", "w_out_odd")
        shards.append([w[names[0]][l // 2].astype(BF16), w[names[1]][l // 2].astype(BF16), w_mem_kv[l].astype(BF16)])
    small = _pack_small_local(w)
    own = lambda g, s: lax.dynamic_update_slice(g, s[None], (chip[0], 0, 0))
    first = [own(g, s) for g, s in zip(gather_weights(shards[0], small), shards[0] + [small])]
    later_shards = shards[1] + shards[2] + shards[3]
    later_raw = gather_weights_async(later_shards)
    w0 = _layer_weights(0, *first[0:3])
    W = dict(w_in_e=[w0[0]], w_out_e=[w0[1]], w_kv=[w0[2]])
    W.update(_unpack_small_full(first[3]))
    W.update({n: w[n] for n in ("norm_even", "sink", "hgrn_norm", "mem_norm", "final_norm")})

    def later(x1, W):
        x1, raw = lax.optimization_barrier((x1, list(later_raw)))
        g = [own(a, s) for a, s in zip(raw, later_shards)]
        w1, w2, w3 = (_layer_weights(l, *g[3 * (l - 1):3 * l]) for l in (1, 2, 3))
        W = dict(W)
        W.update(w_in_e=[w0[0], w2[0]], w_in_o=[w1[0], w3[0]], w_out_e=[w0[1], w2[1]], w_out_o=[w1[1], w3[1]],
                 w_kv=[w0[2], w1[2], w2[2], w3[2]])
        return x1, W

    place = jnp.concatenate([chip, ci])

    def start(tag, blocks, wire):
        axes = [2 if b.shape[1] == ODD_IN // 4 else 1 for b in blocks]
        return dict(tag=tag, blocks=blocks, wire=wire, step=0,
                    recv=exchange_siblings(f"rs_siblings_{tag}", blocks, axes, 2))

    def advance(p):
        if p["step"] == 0:
            sums = add_sibling(p["blocks"], p["recv"], ci, p["wire"])
            p["recv3"] = exchange_chips(f"rs_chips_{p['tag']}", sums, 3)
        else:
            p["mine"] = add_chips(p["blocks"], p["recv"], p["recv3"], place)
            p["other"] = exchange_siblings(f"rs_final_{p['tag']}", p["mine"], [None] * len(p["mine"]), 4)
        p["step"] += 1

    pipes, first_layer = [], {}

    def sync(a):
        for p in pipes:
            if p["step"] < 3:
                key = ("recv", "recv3", "other")[p["step"]]
                a, arrived = lax.optimization_barrier((a, list(p[key])))
                p[key] = arrived
                if p["step"] < 2:
                    advance(p)
                else:
                    p["step"] = 3
        return a

    def on_layer_grads(l, dx, gl):
        dx = sync(dx)
        if l == 0:
            first_layer.update(gl)
        else:
            pipes.append(start(f"l{l}", _layer_grad_blocks(l, gl), [BF16] * 3))
        return dx

    loss_tile, dx, grads = local_step(x[0], mem[0], loss_target[0], W, later, on_layer_grads, sync)
    last = start("l0", _layer_grad_blocks(0, first_layer) + [_pack_small_blocks(grads)], [BF16] * 3 + [F32])
    for p in pipes + [last]:
        while p["step"] < (1 if p is last else 2):
            advance(p)
    by_layer = {int(p["tag"][1:]): p for p in pipes + [last]}
    halves = lambda layers, k: (jnp.stack([by_layer[l]["mine"][k] for l in layers]),
                                jnp.stack([by_layer[l]["other"][k] for l in layers]))
    gl, upd = {}, {}

    pack = jnp.zeros((8, D_MODEL), F32)
    pack = pack.at[0:2].set(grads["norm_even"]).at[2].set(grads["hgrn_norm"].reshape(-1))
    pack = pack.at[3].set(grads["mem_norm"]).at[4].set(grads["final_norm"])
    pack = pack.at[5, 0:16].set(grads["sink"].reshape(-1)).at[5, 16].set(loss_tile[0, 0])
    tot = sum_devices(allgather_small(pack))
    gl.update(norm_even=tot[0:2], hgrn_norm=tot[2].reshape(2, W_B), mem_norm=tot[3], final_norm=tot[4],
              sink=tot[5, 0:16].reshape(2, N_Q_A))
    loss = tot[5, 16]
    for n in ("norm_even", "hgrn_norm", "mem_norm", "final_norm", "sink"):
        upd[n] = adamw_call(w[n], gl[n], m[n], v[n])
    tr_ = lambda a: jnp.swapaxes(a, 1, 2)
    gl["w_in_odd"], *upd["w_in_odd"] = [tr_(o) for o in adamw_halves(
        tr_(w["w_in_odd"]), *halves((1, 3), 0), tr_(m["w_in_odd"]), tr_(v["w_in_odd"]), ci)]
    gl["w_out_odd"], *upd["w_out_odd"] = adamw_halves(w["w_out_odd"], *halves((1, 3), 1), m["w_out_odd"],
                                                      v["w_out_odd"], ci)
    early = [upd[n] for n in sorted(upd)] + [gl["w_in_odd"], gl["w_out_odd"]]
    last["recv3"], early = lax.optimization_barrier((list(last["recv3"]), early))
    for n, res in zip(sorted(upd), early):
        upd[n] = res
    gl["w_in_odd"], gl["w_out_odd"] = early[-2:]
    advance(last)

    big = dict(w_in_even=halves((0, 2), 0), w_out_even=halves((0, 2), 1), w_mem_kv=halves((0, 1, 2, 3), 2))
    s_mine, s_other = last["mine"][3], last["other"][3]
    g_small = jnp.where(ci[0] == 0, jnp.concatenate([s_mine, s_other]), jnp.concatenate([s_other, s_mine]))
    gl.update(_unpack_small_local(g_small))
    for n in WEIGHT_NAMES:
        if n == "w_in_even":
            gl[n], *upd[n] = [tr_(o) for o in adamw_halves(tr_(w[n]), *big[n], tr_(m[n]), tr_(v[n]), ci)]
        elif n in big:
            gl[n], *upd[n] = adamw_halves(w[n], *big[n], m[n], v[n], ci)
        elif n not in upd:
            upd[n] = adamw_call(w[n], gl[n], m[n], v[n])
    return (loss, dx[None], *[gl[n] for n in WEIGHT_NAMES], *[upd[n][0] for n in WEIGHT_NAMES],
            *[upd[n][1] for n in WEIGHT_NAMES], *[upd[n][2] for n in WEIGHT_NAMES])
```

```python
import functools

import numpy as np
import jax
import jax.numpy as jnp
from jax import lax
from jax.experimental import pallas as pl
from jax.experimental.pallas import tpu as pltpu
from jax.experimental.pallas import tpu_sc as plsc

F32 = jnp.float32
BF16 = jnp.bfloat16

D_MODEL = 1024
DEPTH = 4
N_Q_A, N_KV_A, HEAD_DIM_A = 8, 2, 64
W_A, W_KV_A = 512, 128
WINDOW = 128
BLOCK = 128
N_HEADS_B, HEAD_DIM_B, W_B = 4, 128, 512
N_HEADS_C, DK_C, DV_C, WK_C, WV_C = 4, 128, 256, 512, 1024
GATE_RANK = 16
GATE_TEMP = 16.0
N_MEM, N_HEADS_M, HEAD_DIM_M, W_M = 256, 4, 128, 512
EPS = 1e-6
MASK_VALUE = -1e30
MIN_GATE = 1e-30
EVEN_IN, ODD_IN = 4864, 4128
ODD_PAD = 4224
MIX = 1536
ADAM_LR, ADAM_B1, ADAM_B2, ADAM_EPS, ADAM_WD, ADAM_STEP = 0.001, 0.9, 0.999, 1e-08, 0.01, 10

SCAN_CHUNK = 128
SCAN_SUB = 2
SCAN_LEVELS = 7
VMEM_LIMIT = 56 * 1024 * 1024

EVEN_REF_OFF = dict(qA=0, kA=512, vA=640, gA=768, qB=1280, zf=1792, zb=2304, iB=2816, gB=3328, qM=3840, gM=4352)
EVEN_W = dict(qA=512, kA=128, vA=128, gA=512, qB=512, zf=512, zb=512, iB=512, gB=512, qM=512, gM=512)
EVEN_ORDER = ("qA", "gA", "qB", "zf", "zb", "iB", "gB", "qM", "gM", "kA", "vA")
ODD_REF_OFF = dict(qC=0, kC=512, vC=1024, gC=2048, rr=3072, qM=3104, gM=3616)
ODD_W = dict(qC=512, kC=512, vC=1024, gC=1024, rr=32, qM=512, gM=512)
ODD_ORDER = ("qC", "kC", "vC", "gC", "qM", "gM", "rr")


def _offsets(order, widths):
    off, o = {}, 0
    for n in order:
        off[n] = o
        o += widths[n]
    return off


EVEN_OFF = _offsets(EVEN_ORDER, EVEN_W)
ODD_OFF = _offsets(ODD_ORDER, ODD_W)


def _dg(a, b, ca, cb):
    return lax.dot_general(a.astype(BF16), b.astype(BF16), (((ca,), (cb,)), ((), ())),
                           preferred_element_type=F32)


def dot_nn(a, b):
    return _dg(a, b, 1, 0)


def dot_nt(a, b):
    return _dg(a, b, 1, 1)


def dot_tn(a, b):
    return _dg(a, b, 0, 0)


@jax.custom_vjp
def bdot(a, b):
    return dot_nn(a, b)


bdot.defvjp(lambda a, b: (dot_nn(a, b), (a, b)),
            lambda r, g: (dot_nt(g, r[1]), dot_tn(r[0], g)))


@jax.custom_vjp
def bdot_t(a, b):
    return dot_nt(a, b)


bdot_t.defvjp(lambda a, b: (dot_nt(a, b), (a, b)),
              lambda r, g: (dot_nn(g, r[1]), dot_tn(g, r[0])))


@jax.custom_vjp
def bdot_tn(a, b):
    return dot_tn(a, b)


bdot_tn.defvjp(lambda a, b: (dot_tn(a, b), (a, b)),
               lambda r, g: (dot_nt(r[1], g), dot_nn(r[0], g)))


def _split_mm(h, x):
    hi = x.astype(BF16)
    lo = (x - hi.astype(F32)).astype(BF16)
    return (lax.dot_general(h, hi, (((1,), (0,)), ((), ())), preferred_element_type=F32)
            + lax.dot_general(h, lo, (((1,), (0,)), ((), ())), preferred_element_type=F32))


def _sigmoid(z):
    return 1.0 / (1.0 + jnp.exp(-z))


def _silu(z):
    return z * _sigmoid(z)


def _log_sigmoid(z):
    return jnp.minimum(z, 0.0) - jnp.log(1.0 + jnp.exp(-jnp.abs(z)))


def _rms(x, g):
    return x * lax.rsqrt(jnp.mean(x * x, axis=-1, keepdims=True) + EPS) * g


def rms_tile(x, g):
    return (_rms(x, g),)


@functools.partial(jax.custom_vjp, nondiff_argnums=(1, 2))
def split(x, n, axis):
    w = x.shape[axis] // n
    return tuple(lax.slice_in_dim(x, h * w, (h + 1) * w, axis=axis) for h in range(n))


split.defvjp(lambda x, n, axis: (split(x, n, axis), None),
             lambda n, axis, _, cts: (jnp.concatenate(cts, axis=axis),))


def _group_rms(o, g, heads):
    return jnp.concatenate([_rms(oh, gh) for oh, gh in zip(split(o, heads, 1), split(g, heads, 1))], axis=-1)


def even_post_tile(a, o2f, o2b, mo, gA, gB, gM, hg):
    y = _group_rms(o2f + o2b, hg, N_HEADS_B)
    return (jnp.concatenate([a * _silu(gA), y * _silu(gB), mo * _silu(gM)], axis=-1),)


def odd_post_tile(o2f, o2b, mo, gC, gM, gg):
    y = _group_rms(o2f + o2b, gg, N_HEADS_C)
    return (jnp.concatenate([y * _silu(gC), mo * _silu(gM)], axis=-1),)


def hgrn_prep(raw, par):
    qB, z, iB = raw
    (lb,) = par
    f = lb + (1.0 - lb) * _sigmoid(z)
    return _silu(qB), (1.0 - lb) * _sigmoid(-z), iB, jnp.log(jnp.maximum(f, MIN_GATE))


def gla_prep(raw, par):
    qC, kC, vC, r128 = raw
    wup, bg = par
    return qC * (DK_C ** -0.5), kC, vC, _log_sigmoid(bdot(r128, wup) + bg) / GATE_TEMP


def mem_tile(q, k, v):
    s = bdot_t(q, k) * (HEAD_DIM_M ** -0.5)
    m = lax.stop_gradient(jnp.max(s, axis=-1, keepdims=True))
    p = jnp.exp(s - m)
    p = p / jnp.sum(p, axis=-1, keepdims=True)
    return (bdot(p, v),)


ATTN_GROUP = N_Q_A // N_KV_A


def attn_block(q, ks, vs, sink, slope, c, seq):
    rows = ATTN_GROUP * BLOCK
    i = lax.broadcasted_iota(jnp.int32, (rows, 3 * BLOCK), 0) % BLOCK
    j = lax.broadcasted_iota(jnp.int32, (rows, 3 * BLOCK), 1)
    dist = jnp.abs(i - j + BLOCK).astype(F32)
    kpos = (c - 1) * BLOCK + j
    valid = (dist <= WINDOW) & (kpos >= 0) & (kpos < seq)
    s = bdot_t(q, ks) * (HEAD_DIM_A ** -0.5)
    s = jnp.where(valid, s - slope * dist, MASK_VALUE)
    m = lax.stop_gradient(jnp.maximum(jnp.max(s, axis=-1, keepdims=True), sink))
    p = jnp.where(valid, jnp.exp(s - m), 0.0)
    denom = jnp.sum(p, axis=-1, keepdims=True) + jnp.exp(sink - m)
    return bdot(p, vs) / denom


def scan_chunk(q, k, v, e, tot, st, qm, pm):
    C = SCAN_CHUNK
    e = split(e, 2 + SCAN_LEVELS, 0)
    qe = q * jnp.exp(e[0])
    kd = k * jnp.exp(e[1])
    r = lax.broadcasted_iota(jnp.int32, (C, C), 0)
    s = lax.broadcasted_iota(jnp.int32, (C, C), 1)
    a = jnp.where(r == s, jnp.sum(q * k, axis=-1, keepdims=True), 0.0)
    for l in range(SCAN_LEVELS):
        u = jnp.where(qm[l * C:(l + 1) * C] != 0.0, q, k) * jnp.exp(e[2 + l])
        a = a + bdot_t(u, u) * pm[l * C:(l + 1) * C]
    o = bdot_t(qe, st) + bdot(a, v)
    st_new = st * jnp.exp(tot) + bdot_tn(v, kd)
    return o, st_new


def _scan_consts():
    C, L = SCAN_CHUNK, SCAN_LEVELS
    t = np.arange(C)[:, None]
    r = np.arange(C)[None, :]
    blocks = [(r <= t), (r > t)]
    qms, pms = [], []
    for l in range(1, L + 1):
        m = C >> l
        upper_t = (t % (2 * m)) >= m
        upper_r = (r % (2 * m)) >= m
        same_half = (t // m) == (r // m)
        blocks.append(same_half & np.where(upper_t, r <= t, r > t))
        qms.append(np.broadcast_to(upper_t, (C, C)))
        pms.append(((t // (2 * m)) == (r // (2 * m))) & upper_t & ~upper_r)
    hf = np.concatenate(blocks, axis=0).astype(np.float32)
    flip = lambda mat: mat.reshape(-1, C, C)[:, ::-1, ::-1].reshape(-1, C)
    qmf = np.concatenate(qms, axis=0).astype(np.float32)
    pmf = np.concatenate(pms, axis=0).astype(np.float32)
    h = np.stack([hf, flip(hf)])
    ht = np.stack([h[0].T, h[1].T])
    qm = np.stack([qmf, 1.0 - qmf])
    pm = np.stack([pmf, flip(pmf)])
    return h, ht, qm, pm


def _cparams(sem):
    return pltpu.CompilerParams(dimension_semantics=sem, vmem_limit_bytes=VMEM_LIMIT)


def _row_tile(T):
    return min(T, 512)


def _in_spec(spec, tr):
    kind = spec[0]
    if kind == "row":
        _, arr, off, w = spec
        assert off % w == 0
        return arr, pl.BlockSpec((tr, w), functools.partial(lambda i, b: (i, b), b=off // w))
    if kind == "row3":
        _, arr, d, off, w = spec
        assert off % w == 0
        return arr, pl.BlockSpec((None, tr, w), functools.partial(lambda i, d, b: (d, i, b), d=d, b=off // w))
    _, arr = spec
    return arr, pl.BlockSpec(arr.shape, functools.partial(lambda i, n: (0,) * n, n=arr.ndim))


def rows_call(name, tile_fn, T, ins, out_widths, out_dtypes=None, stacks=None):
    tr = _row_tile(T)
    n_in = len(ins)
    out_dtypes = out_dtypes or [F32] * len(out_widths)
    stacks = stacks or [(k,) for k in range(len(out_widths))]

    def body(*refs):
        vals = [r[...] for r in refs[:n_in]]
        outs = tile_fn(*vals)
        for r, members in zip(refs[n_in:], stacks):
            if len(members) == 1:
                r[...] = outs[members[0]].astype(r.dtype)
            else:
                for d, k in enumerate(members):
                    r[d] = outs[k].astype(r.dtype)

    in_specs, args = [], []
    for spec in ins:
        arr, bs = _in_spec(spec, tr)
        args.append(arr)
        in_specs.append(bs)
    out_specs, out_shape = [], []
    for w, dt, members in zip(out_widths, out_dtypes, stacks):
        n = len(members)
        if n == 1:
            out_specs.append(pl.BlockSpec((tr, w), lambda i: (i, 0)))
            out_shape.append(jax.ShapeDtypeStruct((T, w), dt))
        else:
            out_specs.append(pl.BlockSpec((n, tr, w), lambda i: (0, i, 0)))
            out_shape.append(jax.ShapeDtypeStruct((n, T, w), dt))
    return pl.pallas_call(body, out_shape=out_shape, grid=(T // tr,), in_specs=in_specs, out_specs=out_specs,
                          name=name, compiler_params=_cparams(("arbitrary",)))(*args)


def rows_vjp_call(name, tile_fn, T, ins, cts, skip=(), narrow=()):
    tr = _row_tile(T)
    n_in = len(ins)
    n_ct = [len(c) for c in cts]
    want = [k for k in range(n_in) if k not in skip]

    def body(*refs):
        i = pl.program_id(0)
        vals = [r[...] for r in refs[:n_in]]
        ct, pos = [], n_in
        for n in n_ct:
            acc = refs[pos][...]
            for r in refs[pos + 1:pos + n]:
                acc = acc + r[...]
            ct.append(acc)
            pos += n
        _, vjp = jax.vjp(tile_fn, *vals)
        grads = vjp(tuple(ct))
        for r, k in zip(refs[pos:], want):
            if ins[k][0] == "full":
                @pl.when(i == 0)
                def _():
                    r[...] = jnp.zeros_like(r)
                r[...] += grads[k]
            else:
                r[...] = grads[k].astype(r.dtype)

    in_specs, args = [], []
    for spec in list(ins) + [s for c in cts for s in c]:
        arr, bs = _in_spec(spec, tr)
        args.append(arr)
        in_specs.append(bs)
    out_specs, out_shape = [], []
    for k in want:
        if ins[k][0] == "full":
            arr = ins[k][1]
            out_specs.append(pl.BlockSpec(arr.shape, functools.partial(lambda i, n: (0,) * n, n=arr.ndim)))
            out_shape.append(jax.ShapeDtypeStruct(arr.shape, F32))
        else:
            w = ins[k][-1]
            out_specs.append(pl.BlockSpec((tr, w), lambda i: (i, 0)))
            out_shape.append(jax.ShapeDtypeStruct((T, w), BF16 if k in narrow else F32))
    return pl.pallas_call(body, out_shape=out_shape, grid=(T // tr,), in_specs=in_specs, out_specs=out_specs,
                          name=name, compiler_params=_cparams(("arbitrary",)))(*args)


def matmul(name, a, b, mode, add=None, out_dtype=F32):
    if mode == "tn":
        K, M = a.shape
        N = b.shape[1]
        tm = M if M <= 1536 else 512
        tn = N if N <= 1280 else (N // 2 if (N // 2) % 128 == 0 else N)
        tk = min(K, 512)
        grid = (M // tm, N // tn, K // tk)

        def body(a_ref, b_ref, o_ref):
            @pl.when(pl.program_id(2) == 0)
            def _():
                o_ref[...] = jnp.zeros_like(o_ref)
            o_ref[...] += dot_tn(a_ref[...], b_ref[...])

        return pl.pallas_call(
            body, out_shape=jax.ShapeDtypeStruct((M, N), F32), grid=grid,
            in_specs=[pl.BlockSpec((tk, tm), lambda i, j, k: (k, i)), pl.BlockSpec((tk, tn), lambda i, j, k: (k, j))],
            out_specs=pl.BlockSpec((tm, tn), lambda i, j, k: (i, j)), name=name,
            compiler_params=_cparams(("arbitrary", "arbitrary", "arbitrary")))(a, b)

    M, K = a.shape
    N = b.shape[1] if mode == "nn" else b.shape[0]
    tm = min(M, 512)
    tn = N if N <= 1536 else (N // 2 if (N // 2) % 128 == 0 else (N // 3 if (N // 3) % 128 == 0 else N))
    grid = (N // tn, M // tm)
    n_in = 2 + (add is not None)

    def body(*refs):
        a_ref, b_ref = refs[0], refs[1]
        o_ref = refs[n_in]
        acc = dot_nn(a_ref[...], b_ref[...]) if mode == "nn" else dot_nt(a_ref[...], b_ref[...])
        if add is not None:
            acc = acc + refs[2][...]
        o_ref[...] = acc.astype(o_ref.dtype)

    in_specs = [pl.BlockSpec((tm, K), lambda j, i: (i, 0)),
                pl.BlockSpec((K, tn), lambda j, i: (0, j)) if mode == "nn" else pl.BlockSpec((tn, K), lambda j, i: (j, 0))]
    args = [a, b]
    if add is not None:
        in_specs.append(pl.BlockSpec((tm, tn), lambda j, i: (i, j)))
        args.append(add)
    return pl.pallas_call(
        body, out_shape=jax.ShapeDtypeStruct((M, N), out_dtype), grid=grid, in_specs=in_specs,
        out_specs=pl.BlockSpec((tm, tn), lambda j, i: (i, j)), name=name,
        compiler_params=_cparams(("arbitrary", "arbitrary")))(*args)


def norm_project(name, x, g, w):
    T, D = x.shape
    N = w.shape[1]
    tm = min(T, 512)

    def body(x_ref, g_ref, w_ref, h_ref, p_ref):
        h = _rms(x_ref[...], g_ref[...]).astype(BF16)
        h_ref[...] = h
        p_ref[...] = dot_nn(h, w_ref[...])

    return pl.pallas_call(
        body, out_shape=[jax.ShapeDtypeStruct((T, D), BF16), jax.ShapeDtypeStruct((T, N), F32)], grid=(T // tm,),
        in_specs=[pl.BlockSpec((tm, D), lambda i: (i, 0)), pl.BlockSpec((1, D), lambda i: (0, 0)),
                  pl.BlockSpec((D, N), lambda i: (0, 0))],
        out_specs=[pl.BlockSpec((tm, D), lambda i: (i, 0)), pl.BlockSpec((tm, N), lambda i: (i, 0))],
        name=name, compiler_params=_cparams(("arbitrary",)))(x, g, w)


def out_project_bwd(name, dy, mix, w):
    T, D = dy.shape
    K = w.shape[0]
    tm = min(T, 512)

    def body(dy_ref, mix_ref, w_ref, dmix_ref, dw_ref):
        @pl.when(pl.program_id(0) == 0)
        def _():
            dw_ref[...] = jnp.zeros_like(dw_ref)

        d = dy_ref[...].astype(BF16)
        dmix_ref[...] = dot_nt(d, w_ref[...])
        dw_ref[...] += dot_tn(mix_ref[...], d)

    return pl.pallas_call(
        body, out_shape=[jax.ShapeDtypeStruct((T, K), F32), jax.ShapeDtypeStruct((K, D), F32)], grid=(T // tm,),
        in_specs=[pl.BlockSpec((tm, D), lambda i: (i, 0)), pl.BlockSpec((tm, K), lambda i: (i, 0)),
                  pl.BlockSpec((K, D), lambda i: (0, 0))],
        out_specs=[pl.BlockSpec((tm, K), lambda i: (i, 0)), pl.BlockSpec((K, D), lambda i: (0, 0))],
        name=name, compiler_params=_cparams(("arbitrary",)))(dy, mix, w)


def _attn_heads(n):
    G = N_Q_A // N_KV_A
    k_sl = pl.ds(n * HEAD_DIM_A, HEAD_DIM_A)
    v_sl = pl.ds(W_KV_A + n * HEAD_DIM_A, HEAD_DIM_A)
    q_sl = [pl.ds((n * G + g) * HEAD_DIM_A, HEAD_DIM_A) for g in range(G)]
    return k_sl, v_sl, q_sl, range(n * G, (n + 1) * G)


def attn_fwd(p, q_off, kvp, sink, slopes, T):
    nb = T // BLOCK
    assert q_off % W_A == 0

    def body(q_ref, kv_ref, sink_ref, slope_ref, o_ref):
        c = pl.program_id(0)
        rows = pl.ds(pl.multiple_of(c * BLOCK, BLOCK), 3 * BLOCK)
        for n in range(N_KV_A):
            k_sl, v_sl, q_sl, heads = _attn_heads(n)
            group = pl.ds(n * ATTN_GROUP * BLOCK, ATTN_GROUP * BLOCK)
            q = jnp.concatenate([q_ref[:, s] for s in q_sl], axis=0)
            o = attn_block(q, kv_ref[rows, k_sl], kv_ref[rows, v_sl], sink_ref[group, :], slope_ref[group, :], c, T)
            for g, s in enumerate(q_sl):
                o_ref[:, s] = o[g * BLOCK:(g + 1) * BLOCK]

    full = lambda a: pl.BlockSpec(a.shape, functools.partial(lambda c, nd: (0,) * nd, nd=a.ndim))
    return pl.pallas_call(
        body, out_shape=jax.ShapeDtypeStruct((T, W_A), F32), grid=(nb,),
        in_specs=[pl.BlockSpec((BLOCK, W_A), lambda c: (c, q_off // W_A)), full(kvp), full(sink), full(slopes)],
        out_specs=pl.BlockSpec((BLOCK, W_A), lambda c: (c, 0)),
        name="attn_fwd", compiler_params=_cparams(("arbitrary",)))(p, kvp, sink, slopes)


def attn_bwd(p, q_off, kvp, sink, slopes, do, T):
    nb = T // BLOCK

    def body(q_ref, kv_ref, sink_ref, slope_ref, do_ref, dq_ref, dkv_ref, dsink_ref):
        c = pl.program_id(0)

        @pl.when(c == 0)
        def _():
            dkv_ref[...] = jnp.zeros_like(dkv_ref)
            dsink_ref[...] = jnp.zeros_like(dsink_ref)

        rows = pl.ds(pl.multiple_of(c * BLOCK, BLOCK), 3 * BLOCK)
        for n in range(N_KV_A):
            k_sl, v_sl, q_sl, heads = _attn_heads(n)
            group = pl.ds(n * ATTN_GROUP * BLOCK, ATTN_GROUP * BLOCK)
            slope = slope_ref[group, :]
            q = jnp.concatenate([q_ref[:, s] for s in q_sl], axis=0)
            do = jnp.concatenate([do_ref[:, s] for s in q_sl], axis=0)
            _, vjp = jax.vjp(lambda q_, kk, vv, sk: attn_block(q_, kk, vv, sk, slope, c, T),
                             q, kv_ref[rows, k_sl], kv_ref[rows, v_sl], sink_ref[group, :])
            dq, dks, dvs, dsk = vjp(do)
            dkv_ref[rows, k_sl] += dks
            dkv_ref[rows, v_sl] += dvs
            for g, (s, h) in enumerate(zip(q_sl, heads)):
                seg = slice(g * BLOCK, (g + 1) * BLOCK)
                dq_ref[:, s] = dq[seg].astype(dq_ref.dtype)
                dsink_ref[h] += jnp.sum(dsk[seg], axis=0, keepdims=True)

    full = lambda a: pl.BlockSpec(a.shape, functools.partial(lambda c, nd: (0,) * nd, nd=a.ndim))
    qspec = pl.BlockSpec((BLOCK, W_A), lambda c: (c, 0))
    return pl.pallas_call(
        body,
        out_shape=[jax.ShapeDtypeStruct((T, W_A), BF16), jax.ShapeDtypeStruct(kvp.shape, F32),
                   jax.ShapeDtypeStruct((N_Q_A, 1, 1), F32)],
        grid=(nb,),
        in_specs=[pl.BlockSpec((BLOCK, W_A), lambda c: (c, q_off // W_A)), full(kvp), full(sink), full(slopes), qspec],
        out_specs=[qspec, full(kvp), pl.BlockSpec((N_Q_A, 1, 1), lambda c: (0, 0, 0))],
        name="attn_bwd", compiler_params=_cparams(("arbitrary",)))(p, kvp, sink, slopes, do)


def mem_fwd(p, q_off, kv, T):
    tr = min(T, 2 * _row_tile(T))
    assert q_off % W_M == 0

    def body(q_ref, kv_ref, o_ref):
        for h in range(N_HEADS_M):
            hs = pl.ds(h * HEAD_DIM_M, HEAD_DIM_M)
            (o,) = mem_tile(q_ref[:, hs], kv_ref[:, hs], kv_ref[:, pl.ds(W_M + h * HEAD_DIM_M, HEAD_DIM_M)])
            o_ref[:, hs] = o

    return pl.pallas_call(
        body, out_shape=jax.ShapeDtypeStruct((T, W_M), F32), grid=(T // tr,),
        in_specs=[pl.BlockSpec((tr, W_M), lambda i: (i, q_off // W_M)), pl.BlockSpec((N_MEM, 2 * W_M), lambda i: (0, 0))],
        out_specs=pl.BlockSpec((tr, W_M), lambda i: (i, 0)),
        name="mem_fwd", compiler_params=_cparams(("arbitrary",)))(p, kv)


def mem_bwd(p, q_off, kv, do, T):
    tr = min(T, 2 * _row_tile(T))

    def body(q_ref, kv_ref, do_ref, dq_ref, dkv_ref):
        @pl.when(pl.program_id(0) == 0)
        def _():
            dkv_ref[...] = jnp.zeros_like(dkv_ref)

        for h in range(N_HEADS_M):
            hs = pl.ds(h * HEAD_DIM_M, HEAD_DIM_M)
            vs = pl.ds(W_M + h * HEAD_DIM_M, HEAD_DIM_M)
            _, vjp = jax.vjp(mem_tile, q_ref[:, hs], kv_ref[:, hs], kv_ref[:, vs])
            dq, dk, dv = vjp((do_ref[:, hs],))
            dq_ref[:, hs] = dq.astype(dq_ref.dtype)
            dkv_ref[:, hs] += dk
            dkv_ref[:, vs] += dv

    kvspec = pl.BlockSpec((N_MEM, 2 * W_M), lambda i: (0, 0))
    return pl.pallas_call(
        body,
        out_shape=[jax.ShapeDtypeStruct((T, W_M), BF16), jax.ShapeDtypeStruct((N_MEM, 2 * W_M), F32)],
        grid=(T // tr,),
        in_specs=[pl.BlockSpec((tr, W_M), lambda i: (i, q_off // W_M)), kvspec, pl.BlockSpec((tr, W_M), lambda i: (i, 0))],
        out_specs=[pl.BlockSpec((tr, W_M), lambda i: (i, 0)), kvspec],
        name="mem_bwd", compiler_params=_cparams(("arbitrary",)))(p, kv, do)


def _scan_const_specs(dk):
    C, L = SCAN_CHUNK, SCAN_LEVELS
    return [pl.BlockSpec((2, (2 + L) * C, C), lambda n: (0, 0, 0)),
            pl.BlockSpec((2, C, (2 + L) * C), lambda n: (0, 0, 0)),
            pl.BlockSpec((2, L * C, dk), lambda n: (0, 0, 0)),
            pl.BlockSpec((2, L * C, C), lambda n: (0, 0, 0))]


def _chunk_spec(src, width, chunk_of):
    arr, sel = src
    if arr.ndim == 2:
        assert sel % width == 0
        return pl.BlockSpec((SCAN_CHUNK * SCAN_SUB, width),
                            functools.partial(lambda n, b: (chunk_of(n), b), b=sel // width))
    return pl.BlockSpec((None, SCAN_CHUNK * SCAN_SUB, width), functools.partial(lambda n, d: (d, chunk_of(n), 0), d=sel))


def _scan_const_args():
    h, ht, qm, pm = _scan_consts()
    return [jnp.asarray(h, BF16), jnp.asarray(ht, BF16), jnp.asarray(qm, F32), jnp.asarray(pm, F32)]


def _full_spec(a):
    return pl.BlockSpec(a.shape, functools.partial(lambda n, nd: (0,) * nd, nd=a.ndim))


def scan_fwd(name, prep, raws, params, heads, dk, dv, T):
    C, S = SCAN_CHUNK, SCAN_SUB
    N = T // (C * S)
    assert dk == C
    Wv = heads * dv
    orders = (lambda n: n, lambda n: N - 1 - n)
    n_raw, n_par = [len(r) for r in raws], [len(p) for p in params]

    def body(*refs):
        pos, raw_refs, par_refs = 0, [], []
        for d in range(2):
            raw_refs.append(refs[pos:pos + n_raw[d]])
            pos += n_raw[d]
        for d in range(2):
            par_refs.append(refs[pos:pos + n_par[d]])
            pos += n_par[d]
        h_ref, ht_ref, qm_ref, pm_ref = refs[pos:pos + 4]
        o_refs, ss_refs, st_ref = refs[pos + 4:pos + 6], refs[pos + 6:pos + 8], refs[pos + 8]

        @pl.when(pl.program_id(0) == 0)
        def _():
            st_ref[...] = jnp.zeros_like(st_ref)

        for d in range(2):
            consts = (qm_ref[d], pm_ref[d])
            pars = [p[...] for p in par_refs[d]]
            for sub in (range(S) if d == 0 else reversed(range(S))):
                rows = pl.ds(sub * C, C)
                q, k, v, g = prep([r[rows, :] for r in raw_refs[d]], pars)
                e = _split_mm(h_ref[d], g)
                tot = jnp.sum(g, axis=0, keepdims=True)
                for h in range(heads):
                    ks, vs = slice(h * dk, (h + 1) * dk), slice(h * dv, (h + 1) * dv)
                    st = st_ref[d, h]
                    ss_refs[d][h, sub] = st
                    o, st_new = scan_chunk(q[:, ks], k[:, ks], v[:, vs], e[:, ks], tot[:, ks], st, *consts)
                    o_refs[d][rows, vs] = o
                    st_ref[d, h] = st_new

    ss_spec = lambda order: pl.BlockSpec((heads, S, dv, dk), lambda n: (0, order(n), 0, 0))
    return pl.pallas_call(
        body,
        out_shape=[jax.ShapeDtypeStruct((T, Wv), F32)] * 2 + [jax.ShapeDtypeStruct((heads, T // C, dv, dk), F32)] * 2,
        grid=(N,),
        in_specs=[_chunk_spec(s, w, orders[d]) for d in range(2) for s, w in raws[d]]
        + [_full_spec(p) for d in range(2) for p in params[d]] + _scan_const_specs(dk),
        out_specs=[pl.BlockSpec((C * S, Wv), lambda n: (orders[0](n), 0)),
                   pl.BlockSpec((C * S, Wv), lambda n: (orders[1](n), 0)), ss_spec(orders[0]), ss_spec(orders[1])],
        scratch_shapes=[pltpu.VMEM((2, heads, dv, dk), F32)],
        name=name, compiler_params=_cparams(("arbitrary",)))(
            *[s[0] for d in range(2) for s, _ in raws[d]], *[p for d in range(2) for p in params[d]], *_scan_const_args())


def scan_bwd(name, prep, raws, params, ss, do, heads, dk, dv, T):
    C, S = SCAN_CHUNK, SCAN_SUB
    N = T // (C * S)
    Wv = heads * dv
    orders = (lambda n: N - 1 - n, lambda n: n)
    n_raw, n_par = [len(r) for r in raws], [len(p) for p in params]

    def body(*refs):
        pos, raw_refs, par_refs, draw_refs, dpar_refs = 0, [], [], [], []
        for group, counts in ((raw_refs, n_raw), (par_refs, n_par)):
            for d in range(2):
                group.append(refs[pos:pos + counts[d]])
                pos += counts[d]
        ss_refs, do_refs = refs[pos:pos + 2], refs[pos + 2:pos + 4]
        h_ref, ht_ref, qm_ref, pm_ref = refs[pos + 4:pos + 8]
        pos += 8
        for group, counts in ((draw_refs, n_raw), (dpar_refs, n_par)):
            for d in range(2):
                group.append(refs[pos:pos + counts[d]])
                pos += counts[d]
        dst_ref = refs[pos]

        @pl.when(pl.program_id(0) == 0)
        def _():
            dst_ref[...] = jnp.zeros_like(dst_ref)
            for d in range(2):
                for r in dpar_refs[d]:
                    r[...] = jnp.zeros_like(r)

        for d in range(2):
            consts = (qm_ref[d], pm_ref[d])
            pars = [p[...] for p in par_refs[d]]
            for sub in (reversed(range(S)) if d == 0 else range(S)):
                rows = pl.ds(sub * C, C)
                (q, k, v, g), prep_vjp = jax.vjp(prep, [r[rows, :] for r in raw_refs[d]], pars)
                e = _split_mm(h_ref[d], g)
                tot = jnp.sum(g, axis=0, keepdims=True)
                dqs, dks, dvs, des, dtots = [], [], [], [], []
                for h in range(heads):
                    ks, vs = slice(h * dk, (h + 1) * dk), slice(h * dv, (h + 1) * dv)
                    _, vjp = jax.vjp(lambda q_, k_, v_, e_, t_, st_: scan_chunk(q_, k_, v_, e_, t_, st_, *consts),
                                     q[:, ks], k[:, ks], v[:, vs], e[:, ks], tot[:, ks], ss_refs[d][h, sub])
                    dq, dk_, dv_, de, dtot, dst = vjp((do_refs[d][rows, vs], dst_ref[d, h]))
                    dst_ref[d, h] = dst
                    for group, val in ((dqs, dq), (dks, dk_), (dvs, dv_), (des, de), (dtots, dtot)):
                        group.append(val)
                cat = lambda parts: jnp.concatenate(parts, axis=-1)
                dg = _split_mm(ht_ref[d], cat(des)) + cat(dtots)
                draws, dpars = prep_vjp((cat(dqs), cat(dks), cat(dvs), dg))
                for r, val in zip(draw_refs[d], draws):
                    r[rows, :] = val.astype(r.dtype)
                for r, val in zip(dpar_refs[d], dpars):
                    r[...] += val

    ss_spec = lambda order: pl.BlockSpec((heads, S, dv, dk), lambda n: (0, order(n), 0, 0))
    row_out = lambda w, order: pl.BlockSpec((C * S, w), lambda n: (order(n), 0))
    return pl.pallas_call(
        body,
        out_shape=[jax.ShapeDtypeStruct((T, w), BF16) for d in range(2) for _, w in raws[d]]
        + [jax.ShapeDtypeStruct(p.shape, F32) for d in range(2) for p in params[d]],
        grid=(N,),
        in_specs=[_chunk_spec(s, w, orders[d]) for d in range(2) for s, w in raws[d]]
        + [_full_spec(p) for d in range(2) for p in params[d]]
        + [ss_spec(orders[0]), ss_spec(orders[1]), _chunk_spec(do, Wv, orders[0]), _chunk_spec(do, Wv, orders[1])]
        + _scan_const_specs(dk),
        out_specs=[row_out(w, orders[d]) for d in range(2) for _, w in raws[d]]
        + [_full_spec(p) for d in range(2) for p in params[d]],
        scratch_shapes=[pltpu.VMEM((2, heads, dv, dk), F32)],
        name=name, compiler_params=_cparams(("arbitrary",)))(
            *[s[0] for d in range(2) for s, _ in raws[d]], *[p for d in range(2) for p in params[d]],
            ss[0], ss[1], do[0], do[0], *_scan_const_args())


def final_call(x, g, target, T):
    tr = _row_tile(T)

    def tile(xv, gv, tv):
        y = _rms(xv, gv)
        err = (y - tv) ** 2
        return jnp.sum(jnp.sum(err, axis=-1, keepdims=True), axis=0, keepdims=True) * (0.5 / D_MODEL)

    def body(x_ref, g_ref, t_ref, loss_ref, dx_ref, dg_ref):
        i = pl.program_id(0)
        tv = t_ref[...]
        lv, vjp = jax.vjp(lambda a, b: tile(a, b, tv), x_ref[...], g_ref[...])
        dx, dg = vjp(jnp.ones((1, 1), F32))
        dx_ref[...] = dx

        @pl.when(i == 0)
        def _():
            loss_ref[...] = jnp.zeros_like(loss_ref)
            dg_ref[...] = jnp.zeros_like(dg_ref)

        loss_ref[...] += jnp.broadcast_to(lv, loss_ref.shape)
        dg_ref[...] += dg

    return pl.pallas_call(
        body,
        out_shape=[jax.ShapeDtypeStruct((8, 128), F32), jax.ShapeDtypeStruct((T, D_MODEL), F32),
                   jax.ShapeDtypeStruct((1, D_MODEL), F32)],
        grid=(T // tr,),
        in_specs=[pl.BlockSpec((tr, D_MODEL), lambda i: (i, 0)), pl.BlockSpec((1, D_MODEL), lambda i: (0, 0)),
                  pl.BlockSpec((tr, D_MODEL), lambda i: (i, 0))],
        out_specs=[pl.BlockSpec((8, 128), lambda i: (0, 0)), pl.BlockSpec((tr, D_MODEL), lambda i: (i, 0)),
                   pl.BlockSpec((1, D_MODEL), lambda i: (0, 0))],
        name="final_loss", compiler_params=_cparams(("arbitrary",)))(x, g, target)


def adamw_call(w, g, m, v):
    shape = w.shape
    c = shape[-1]
    r = int(np.prod(shape[:-1])) if len(shape) > 1 else 1
    tr = r if r <= 256 else 256
    assert r % tr == 0

    def body(w_ref, g_ref, m_ref, v_ref, d_ref, nm_ref, nv_ref):
        gv = g_ref[...]
        nm = ADAM_B1 * m_ref[...] + (1.0 - ADAM_B1) * gv
        nv = ADAM_B2 * v_ref[...] + (1.0 - ADAM_B2) * jnp.square(gv)
        m_hat = nm / (1.0 - ADAM_B1 ** ADAM_STEP)
        v_hat = nv / (1.0 - ADAM_B2 ** ADAM_STEP)
        d_ref[...] = -ADAM_LR * (m_hat / (jnp.sqrt(v_hat) + ADAM_EPS) + ADAM_WD * w_ref[...])
        nm_ref[...] = nm
        nv_ref[...] = nv

    spec = pl.BlockSpec((tr, c), lambda i: (i, 0))
    outs = pl.pallas_call(body, out_shape=[jax.ShapeDtypeStruct((r, c), F32)] * 3, grid=(r // tr,),
                          in_specs=[spec] * 4, out_specs=[spec] * 3, name="adamw",
                          compiler_params=_cparams(("arbitrary",)))(*(t.reshape(r, c) for t in (w, g, m, v)))
    return tuple(o.reshape(shape) for o in outs)


def adamw_halves(w, mine, other, m, v, c):
    L, R, C = w.shape
    by_cols = mine.shape[-1] != C
    if by_cols:
        tile, nbh = (R, C // 2), 1
        full_idx = lambda l, i: (l, 0, i)
    else:
        rh = R // 2
        tr = rh if rh <= 256 else rh // 2
        assert tr % 8 == 0
        tile, nbh = (tr, C), rh // tr
        full_idx = lambda l, i: (l, i, 0)

    def body(c_ref, w_ref, a_ref, b_ref, m_ref, v_ref, g_ref, d_ref, nm_ref, nv_ref):
        is_mine = (pl.program_id(1) // nbh) == c_ref[0]
        gv = jnp.where(is_mine, a_ref[...], b_ref[...])
        nm = ADAM_B1 * m_ref[...] + (1.0 - ADAM_B1) * gv
        nv = ADAM_B2 * v_ref[...] + (1.0 - ADAM_B2) * jnp.square(gv)
        m_hat = nm / (1.0 - ADAM_B1 ** ADAM_STEP)
        v_hat = nv / (1.0 - ADAM_B2 ** ADAM_STEP)
        g_ref[...] = gv
        d_ref[...] = -ADAM_LR * (m_hat / (jnp.sqrt(v_hat) + ADAM_EPS) + ADAM_WD * w_ref[...])
        nm_ref[...] = nm
        nv_ref[...] = nv

    full = pl.BlockSpec((None,) + tile, lambda l, i, c_ref: full_idx(l, i))
    half = pl.BlockSpec((None,) + tile, lambda l, i, c_ref: (l, i % nbh, 0))
    grid_spec = pltpu.PrefetchScalarGridSpec(num_scalar_prefetch=1, grid=(L, 2 * nbh),
                                             in_specs=[full, half, half, full, full], out_specs=[full] * 4)
    return pl.pallas_call(body, out_shape=[jax.ShapeDtypeStruct(w.shape, F32)] * 4, grid_spec=grid_spec,
                          name="adamw_halves", compiler_params=_cparams(("arbitrary", "arbitrary")))(c, w, mine, other, m, v)


def sum_devices(g64):
    def body(x_ref, o_ref):
        acc = x_ref[0:8, :]
        for d in range(1, 8):
            acc = acc + x_ref[8 * d:8 * d + 8, :]
        o_ref[...] = acc

    return pl.pallas_call(body, out_shape=jax.ShapeDtypeStruct((8, D_MODEL), F32), name="sum_devices")(g64)


def _half_tile(rh):
    if rh <= 512:
        return rh
    return next(rh // d for d in range(2, rh) if rh % d == 0 and (rh // d) % 16 == 0 and rh // d <= 512)


def _half_geometry(full_shape, half_shape):
    R, C = full_shape[-2:]
    if half_shape[-1] != C:
        return (R, C // 2), 1, lambda i, c: (0, c)
    tr = _half_tile(R // 2)
    nblk = (R // 2) // tr
    return (tr, C), nblk, lambda i, c: (i + c * nblk, 0)


def _work_items(counts):
    starts = [int(v) for v in np.cumsum([0] + list(counts[:-1]))]
    local = lambda a, s: jnp.clip(s - starts[a], 0, counts[a] - 1)
    return starts, int(sum(counts)), local


def add_sibling(gs, recvs, c, out_dtypes):
    n = len(gs)
    geo = [_half_geometry(g.shape, r.shape) for g, r in zip(gs, recvs)]
    counts = [4 * nblk for _, nblk, _ in geo]
    starts, total, local = _work_items(counts)

    def body(c_ref, *refs):
        s = pl.program_id(0)
        for a in range(n):
            g_ref, r_ref, o_ref = refs[a], refs[n + a], refs[2 * n + a]

            @pl.when((s >= starts[a]) & (s < starts[a] + counts[a]))
            def _():
                o_ref[...] = (g_ref[...] + r_ref[...]).astype(o_ref.dtype)

    def own_idx(s, c_ref, a):
        _, nblk, own = geo[a]
        k = local(a, s)
        return (k // nblk,) + own(k % nblk, c_ref[0])

    def half_idx(s, c_ref, a):
        k = local(a, s)
        return (k // geo[a][1], k % geo[a][1], 0)

    halves = [pl.BlockSpec((None,) + geo[a][0], functools.partial(half_idx, a=a)) for a in range(n)]
    grid_spec = pltpu.PrefetchScalarGridSpec(
        num_scalar_prefetch=1, grid=(total,),
        in_specs=[pl.BlockSpec((None,) + geo[a][0], functools.partial(own_idx, a=a)) for a in range(n)] + halves,
        out_specs=halves)
    return pl.pallas_call(body, out_shape=[jax.ShapeDtypeStruct(r.shape, dt) for r, dt in zip(recvs, out_dtypes)],
                          grid_spec=grid_spec, name="rs_add_sibling",
                          compiler_params=_cparams(("arbitrary",)))(c, *gs, *recvs)


def add_chips(gs, recvs, r3s, place):
    n = len(gs)
    geo = [_half_geometry(g.shape, r.shape) for g, r in zip(gs, recvs)]
    counts = [nblk for _, nblk, _ in geo]
    starts, total, local = _work_items(counts)

    def body(p_ref, *refs):
        s = pl.program_id(0)
        up = lambda r: r[...].astype(F32)
        for a in range(n):
            g_ref, s_ref, o_ref = refs[a], refs[n + a], refs[5 * n + a]
            a_ref, b_ref, c_ref = refs[2 * n + 3 * a:2 * n + 3 * a + 3]

            @pl.when((s >= starts[a]) & (s < starts[a] + counts[a]))
            def _():
                o_ref[...] = (((g_ref[...] + up(s_ref)) + up(a_ref)) + up(b_ref)) + up(c_ref)

    own_idx = lambda s, p_ref, a: (p_ref[0],) + geo[a][2](local(a, s), p_ref[1])
    sib_idx = lambda s, p_ref, a: (p_ref[0], local(a, s), 0)
    chip_idx = lambda s, p_ref, a, k: (k, local(a, s), 0)
    spec = lambda a, idx, **kw: pl.BlockSpec((None,) + geo[a][0], functools.partial(idx, a=a, **kw))
    grid_spec = pltpu.PrefetchScalarGridSpec(
        num_scalar_prefetch=1, grid=(total,),
        in_specs=[spec(a, own_idx) for a in range(n)] + [spec(a, sib_idx) for a in range(n)]
        + [spec(a, chip_idx, k=k) for a in range(n) for k in range(3)],
        out_specs=[pl.BlockSpec(geo[a][0], functools.partial(lambda s, p_ref, a: (local(a, s), 0), a=a))
                   for a in range(n)])
    return pl.pallas_call(body, out_shape=[jax.ShapeDtypeStruct(r.shape[1:], F32) for r in recvs],
                          grid_spec=grid_spec, name="rs_add_chips", compiler_params=_cparams(("arbitrary",)))(
                              place, *gs, *recvs, *[r for r3 in r3s for r in (r3, r3, r3)])


def _remote(src, dst, ssem, rsem, dev):
    return pltpu.make_async_remote_copy(src_ref=src, dst_ref=dst, send_sem=ssem, recv_sem=rsem,
                                        device_id=dev, device_id_type=pl.DeviceIdType.MESH)


def _mesh_places():
    x, y, c = lax.axis_index("x"), lax.axis_index("y"), lax.axis_index("c")
    chips = [(1 - x, y), (x, 1 - y), (1 - x, 1 - y)]
    return x, y, c, (x, y, 1 - c), chips


def _hbm_specs(n):
    return [pl.BlockSpec(memory_space=pltpu.HBM) for _ in range(n)]


def _gather_body(ins, outs, n_split, send_sems, recv_sems, handshake):
    x, y, c, sibling, chips = _mesh_places()
    mine = 2 * x + y
    if handshake:
        barrier = pltpu.get_barrier_semaphore()
        peers = [sibling] + [(*chip, c) for chip in chips]
        for peer in peers:
            pl.semaphore_signal(barrier, inc=1, device_id=peer, device_id_type=pl.DeviceIdType.MESH)
        pl.semaphore_wait(barrier, len(peers))

    def half(a, chip_idx, which):
        rh = ins[a].shape[0] // 2
        return outs[a].at[chip_idx, pl.ds(which * rh, rh), :]

    sent = []
    for a in range(len(ins)):
        for k, chip in enumerate(chips):
            if a < n_split:
                rh = ins[a].shape[0] // 2
                src, dst = ins[a].at[pl.ds(c * rh, rh), :], half(a, mine, c)
            else:
                src, dst = ins[a], outs[a].at[mine]
            sent.append(_remote(src, dst, send_sems.at[a, k], recv_sems.at[a, k], (*chip, c)))
    for cp in sent:
        cp.start()
    for a in range(len(ins)):
        for k, chip in enumerate(chips):
            j = 2 * chip[0] + chip[1]
            region = half(a, j, c) if a < n_split else outs[a].at[j]
            _remote(region, region, send_sems.at[a, k], recv_sems.at[a, k], (*chip, c)).wait_recv()
            if a < n_split:
                fwd = _remote(region, region, send_sems.at[a, 3 + k], recv_sems.at[a, 3 + k], sibling)
                fwd.start()
                sent.append(fwd)
    for a in range(n_split):
        for k, chip in enumerate(chips):
            region = half(a, 2 * chip[0] + chip[1], 1 - c)
            _remote(region, region, send_sems.at[a, 3 + k], recv_sems.at[a, 3 + k], sibling).wait_recv()
    for cp in sent:
        cp.wait_send()


def gather_weights(shards, small):
    arrs = list(shards) + [small]
    n = len(arrs)

    def body(*refs):
        _gather_body(refs[:n], refs[n:2 * n], n - 1, refs[2 * n], refs[2 * n + 1], handshake=False)

    return pl.pallas_call(
        body, out_shape=[jax.ShapeDtypeStruct((4,) + a.shape, a.dtype) for a in arrs],
        in_specs=_hbm_specs(n), out_specs=_hbm_specs(n),
        scratch_shapes=[pltpu.SemaphoreType.DMA((n, 6)), pltpu.SemaphoreType.DMA((n, 6))],
        name="gather_weights")(*arrs)


def gather_weights_async(shards):
    n = len(shards)

    def body(*refs):
        _gather_body(refs[:n], refs[n:2 * n], n, refs[2 * n], refs[2 * n + 1], handshake=True)

    return pl.kernel(
        body, out_type=[jax.ShapeDtypeStruct((4,) + a.shape, a.dtype) for a in shards],
        mesh=plsc.ScalarSubcoreMesh(axis_name="seq", num_cores=1),
        scratch_types=[pltpu.SemaphoreType.DMA((n, 6)), pltpu.SemaphoreType.DMA((n, 6))],
        compiler_params=pltpu.CompilerParams(collective_id=1), name="gather_weights_async")(*shards)


def _sequencer_call(name, body, out_type, sem_shape, collective_id, args):
    return pl.kernel(
        body, out_type=out_type, mesh=plsc.ScalarSubcoreMesh(axis_name="seq", num_cores=1),
        scratch_types=[pltpu.SemaphoreType.DMA(sem_shape), pltpu.SemaphoreType.DMA(sem_shape)],
        compiler_params=pltpu.CompilerParams(collective_id=collective_id), name=name)(*args)


def _handshake(peers):
    barrier = pltpu.get_barrier_semaphore()
    for peer in peers:
        pl.semaphore_signal(barrier, inc=1, device_id=peer, device_id_type=pl.DeviceIdType.MESH)
    pl.semaphore_wait(barrier, len(peers))


def exchange_siblings(name, srcs, axes, collective_id):
    n = len(srcs)

    def body(*refs):
        ins, outs = refs[:n], refs[n:2 * n]
        send_sems, recv_sems = refs[2 * n:]
        x, y, c, sibling, chips = _mesh_places()
        _handshake([sibling])
        cps = []
        for a in range(n):
            src = ins[a]
            if axes[a] is not None:
                half = src.shape[axes[a]] // 2
                theirs = pl.ds((1 - c) * half, half)
                src = src.at[:, theirs, :] if axes[a] == 1 else src.at[:, :, theirs]
            cps.append(_remote(src, outs[a], send_sems.at[a], recv_sems.at[a], sibling))
        for cp in cps:
            cp.start()
        for cp in cps:
            cp.wait()

    def shape(g, axis):
        return g.shape if axis is None else tuple(d // 2 if k == axis else d for k, d in enumerate(g.shape))

    return _sequencer_call(name, body, [jax.ShapeDtypeStruct(shape(g, ax), g.dtype) for g, ax in zip(srcs, axes)],
                           (n,), collective_id, srcs)


def exchange_chips(name, s1s, collective_id):
    n = len(s1s)

    def body(*refs):
        ins, outs = refs[:n], refs[n:2 * n]
        send_sems, recv_sems = refs[2 * n:]
        x, y, c, sibling, chips = _mesh_places()
        _handshake([(*chip, c) for chip in chips])
        cps = []
        for a in range(n):
            for k, chip in enumerate(chips):
                cps.append(_remote(ins[a].at[2 * chip[0] + chip[1]], outs[a].at[k], send_sems.at[a, k],
                                   recv_sems.at[a, k], (*chip, c)))
        for cp in cps:
            cp.start()
        for cp in cps:
            cp.wait()

    return _sequencer_call(name, body, [jax.ShapeDtypeStruct((3,) + s.shape[1:], s.dtype) for s in s1s], (n, 3),
                           collective_id, s1s)


def allgather_small(v):
    m_per = v.shape[0]

    def body(x_ref, out_ref, send_sems, recv_sems, local_sem):
        x, y, c, sibling, chips = _mesh_places()
        me = (x, y, c)

        def rows(px, py, pc):
            return out_ref.at[pl.ds((4 * px + 2 * py + pc) * m_per, m_per), :]

        def copy(k, block, to, src=None):
            return _remote(rows(*block) if src is None else src, rows(*block), send_sems.at[k], recv_sems.at[k], to)

        mine = pltpu.make_async_copy(x_ref, rows(*me), local_sem)
        mine.start()
        first = [copy(0, me, sibling, src=x_ref)]
        first += [copy(1 + j, me, (*chip, c), src=x_ref) for j, chip in enumerate(chips)]
        for cp in first:
            cp.start()
        passed = [copy(4 + j, (*chip, c), sibling) for j, chip in enumerate(chips)]
        for j, chip in enumerate(chips):
            copy(1 + j, (*chip, c), me).wait_recv()
            passed[j].start()
        copy(0, sibling, me).wait_recv()
        for j, chip in enumerate(chips):
            copy(4 + j, (*chip, 1 - c), me).wait_recv()
        for cp in first + passed:
            cp.wait_send()
        mine.wait()

    return pl.pallas_call(
        body, out_shape=jax.ShapeDtypeStruct((8 * m_per, v.shape[1]), v.dtype),
        in_specs=[pl.BlockSpec(memory_space=pltpu.VMEM)], out_specs=pl.BlockSpec(memory_space=pltpu.VMEM),
        scratch_shapes=[pltpu.SemaphoreType.DMA((7,)), pltpu.SemaphoreType.DMA((7,)), pltpu.SemaphoreType.DMA],
        name="allgather_small")(v)


def rms_res_tile(x, g):
    return (_rms(x, g), x)


def _lower_bounds(lb_param):
    lbs = jax.nn.softmax(lb_param.astype(F32), axis=0)
    return jnp.cumsum(lbs, axis=0) - lbs[0]


def _even_fwd(x, i, W, lower, kv, slopes, T):
    O = EVEN_OFF
    g = W["norm_even"][i].reshape(1, D_MODEL)
    h, p = norm_project("mm_in_e", x, g, W["w_in_e"][i])
    kvp = jnp.pad(p[:, O["kA"]:O["kA"] + 2 * W_KV_A], ((BLOCK, BLOCK), (0, 0)))
    sink = jnp.repeat(W["sink"][i], BLOCK).reshape(N_Q_A * BLOCK, 1)
    a = attn_fwd(p, O["qA"], kvp, sink, slopes, T)
    scan_raws = [[((p, O["qB"]), W_B), ((p, O[z]), W_B), ((p, O["iB"]), W_B)] for z in ("zf", "zb")]
    scan_pars = [[lower[i][0:1]], [lower[i][1:2]]]
    o_f, o_b, ss_f, ss_b = scan_fwd("scan_fwd_h", hgrn_prep, scan_raws, scan_pars, N_HEADS_B, HEAD_DIM_B, HEAD_DIM_B, T)
    mo = mem_fwd(p, O["qM"], kv, T)
    hg = W["hgrn_norm"][i].reshape(1, W_B)
    post_ins = [("row", a, 0, W_A), ("row", o_f, 0, W_B), ("row", o_b, 0, W_B), ("row", mo, 0, W_M),
                ("row", p, O["gA"], W_A), ("row", p, O["gB"], W_B), ("row", p, O["gM"], W_M), ("full", hg)]
    (mix,) = rows_call("even_post_fwd", even_post_tile, T, post_ins, [MIX], [BF16])
    x_new = matmul("mm_out", mix, W["w_out_e"][i], "nn", add=x)
    return x_new, dict(x=x, g=g, h=h, p=p, kvp=kvp, sink=sink, scan_raws=scan_raws, scan_pars=scan_pars,
                       ss=(ss_f, ss_b), post_ins=post_ins, mix=mix)


def _add2(a, b):
    return a.astype(F32) + b.astype(F32)


def _assemble_even(dqA, dgA, dqB_f, dqB_b, dzf, dzb, diB_f, diB_b, dgB, dqM, dgM, dkvA):
    parts = [dqA, dgA, _add2(dqB_f, dqB_b), dzf, dzb, _add2(diB_f, diB_b), dgB, dqM, dgM, dkvA]
    return (jnp.concatenate([t.astype(BF16) for t in parts], axis=-1),)


def _even_bwd(dxo, sv, i, W, kv, slopes, T, sync):
    O = EVEN_OFF
    p = sv["p"]
    dmix, dwo = out_project_bwd("mm_out_bwd", dxo, sv["mix"], W["w_out_e"][i])
    da, dof, dmo, dgA, dgB, dgM, dhg = rows_vjp_call("even_post_bwd", even_post_tile, T, sv["post_ins"],
                                                      [[("row", dmix, 0, MIX)]], skip=(2,), narrow=(4, 5, 6))
    dqA, dkvp, dsink = attn_bwd(p, O["qA"], sv["kvp"], sv["sink"], slopes, da, T)
    dkvA = dkvp[BLOCK:-BLOCK]
    dqB_f, dzf, diB_f, dqB_b, dzb, diB_b, dlow_f, dlow_b = scan_bwd(
        "scan_bwd_h", hgrn_prep, sv["scan_raws"], sv["scan_pars"], sv["ss"], (dof, 0), N_HEADS_B, HEAD_DIM_B, HEAD_DIM_B, T)
    dqB_f = sync(dqB_f)
    row = lambda arr, w: ("row", arr, 0, w)
    dlow = jnp.concatenate([dlow_f, dlow_b], axis=0)
    dqM, dkv = mem_bwd(p, O["qM"], kv, dmo, T)
    (dp,) = rows_call("even_dp", _assemble_even, T,
                      [row(dqA, W_A), row(dgA, W_A), row(dqB_f, W_B), row(dqB_b, W_B), row(dzf, W_B), row(dzb, W_B),
                       row(diB_f, W_B), row(diB_b, W_B), row(dgB, W_B), row(dqM, W_M), row(dgM, W_M),
                       row(dkvA, 2 * W_KV_A)],
                      [EVEN_IN], [BF16])
    dh = matmul("mm_dh_e", dp, W["w_in_e"][i], "nt")
    dwi = matmul("mm_dwi_e", sv["h"], dp, "tn")
    dx, dg = rows_vjp_call("rms_res_bwd", rms_res_tile, T, [("row", sv["x"], 0, D_MODEL), ("full", sv["g"])],
                           [[("row", dh, 0, D_MODEL)], [("row", dxo, 0, D_MODEL)]])
    return dx, dict(w_in=dwi, w_out=dwo, norm=dg[0], sink=dsink.reshape(N_Q_A), low=dlow, hg=dhg[0], kv=dkv)


def _pad_gate_up(w_up):
    z = jnp.zeros((2, 128, WK_C), F32)
    z = z.at[0, 0:GATE_RANK].set(w_up[0])
    return z.at[1, GATE_RANK:2 * GATE_RANK].set(w_up[1])


def _odd_fwd(x, i, W, kv, T):
    O = ODD_OFF
    g = W["norm_odd"][i].reshape(1, D_MODEL)
    h, p = norm_project("mm_in_o", x, g, W["w_in_o"][i])
    wup = _pad_gate_up(W["w_gate_up"][i])
    one_dir = [((p, O["qC"]), WK_C), ((p, O["kC"]), WK_C), ((p, O["vC"]), WV_C), ((p, O["rr"]), 128)]
    scan_raws = [one_dir, one_dir]
    scan_pars = [[wup[d], W["b_gate"][i][d:d + 1]] for d in range(2)]
    o_f, o_b, ss_f, ss_b = scan_fwd("scan_fwd_g", gla_prep, scan_raws, scan_pars, N_HEADS_C, DK_C, DV_C, T)
    mo = mem_fwd(p, O["qM"], kv, T)
    gg = W["gla_norm"][i].reshape(1, WV_C)
    post_ins = [("row", o_f, 0, WV_C), ("row", o_b, 0, WV_C), ("row", mo, 0, W_M),
                ("row", p, O["gC"], WV_C), ("row", p, O["gM"], W_M), ("full", gg)]
    (mix,) = rows_call("odd_post_fwd", odd_post_tile, T, post_ins, [MIX], [BF16])
    x_new = matmul("mm_out", mix, W["w_out_o"][i], "nn", add=x)
    return x_new, dict(x=x, g=g, h=h, p=p, scan_raws=scan_raws, scan_pars=scan_pars, ss=(ss_f, ss_b),
                       post_ins=post_ins, mix=mix)


def _assemble_odd(dq0, dq1, dk0, dk1, dv0, dv1, dgC, dqM, dgM, dr0, dr1):
    parts = [_add2(dq0, dq1), _add2(dk0, dk1), _add2(dv0, dv1), dgC, dqM, dgM, _add2(dr0, dr1)]
    return (jnp.concatenate([t.astype(BF16) for t in parts], axis=-1),)


def _odd_bwd(dxo, sv, i, W, kv, T, sync):
    O = ODD_OFF
    p = sv["p"]
    dmix, dwo = out_project_bwd("mm_out_bwd", dxo, sv["mix"], W["w_out_o"][i])
    dof, dmo, dgC, dgM, dgg = rows_vjp_call("odd_post_bwd", odd_post_tile, T, sv["post_ins"],
                                            [[("row", dmix, 0, MIX)]], skip=(1,), narrow=(3, 4))
    dqf, dkf, dvf, dr_f, dqb, dkb, dvb, dr_b, dwup_f, dbg_f, dwup_b, dbg_b = scan_bwd(
        "scan_bwd_g", gla_prep, sv["scan_raws"], sv["scan_pars"], sv["ss"], (dof, 0), N_HEADS_C, DK_C, DV_C, T)
    dqf = sync(dqf)
    row = lambda arr, w: ("row", arr, 0, w)
    dqM, dkv = mem_bwd(p, O["qM"], kv, dmo, T)
    (dp,) = rows_call("odd_dp", _assemble_odd, T,
                      [row(dqf, WK_C), row(dqb, WK_C), row(dkf, WK_C), row(dkb, WK_C), row(dvf, WV_C), row(dvb, WV_C),
                       row(dgC, WV_C), row(dqM, W_M), row(dgM, W_M), row(dr_f, 128), row(dr_b, 128)],
                      [ODD_PAD], [BF16])
    dh = matmul("mm_dh_o", dp, W["w_in_o"][i], "nt")
    dwi = matmul("mm_dwi_o", sv["h"], dp, "tn")
    dx, dg = rows_vjp_call("rms_res_bwd", rms_res_tile, T, [("row", sv["x"], 0, D_MODEL), ("full", sv["g"])],
                           [[("row", dh, 0, D_MODEL)], [("row", dxo, 0, D_MODEL)]])
    dw_up = jnp.stack([dwup_f[0:GATE_RANK], dwup_b[GATE_RANK:2 * GATE_RANK]])
    dbg = jnp.concatenate([dbg_f, dbg_b], axis=0)
    return dx, dict(w_in=dwi, w_out=dwo, norm=dg[0], w_up=dw_up, b_gate=dbg, gg=dgg[0], kv=dkv)


def local_step(x, mem, target, W, later=None, on_layer_grads=None, sync=lambda a: a):
    T = x.shape[0]
    slopes = jnp.repeat(2.0 ** (-8.0 * jnp.arange(1, N_Q_A + 1, dtype=F32) / N_Q_A), BLOCK).reshape(N_Q_A * BLOCK, 1)
    lower, lower_vjp = jax.vjp(_lower_bounds, W["lb_param"])
    mem_g = W["mem_norm"].reshape(1, D_MODEL)
    (mem_n,) = rows_call("mem_rms_fwd", rms_tile, N_MEM, [("row", mem, 0, D_MODEL), ("full", mem_g)], [D_MODEL], [BF16])
    kvs, saved = [], []
    for l in range(DEPTH):
        if l == 1 and later is not None:
            x, W = later(x, W)
        kvs.append(matmul("mm_kv", mem_n, W["w_kv"][l], "nn"))
        if l % 2 == 0:
            x, sv = _even_fwd(x, l // 2, W, lower, kvs[l], slopes, T)
        else:
            x, sv = _odd_fwd(x, l // 2, W, kvs[l], T)
        saved.append(sv)
    loss, dx, dgf = final_call(x, W["final_norm"].reshape(1, D_MODEL), target, T)
    per = [None] * DEPTH
    dmem_n = None
    for l in reversed(range(DEPTH)):
        if l % 2 == 0:
            dx, per[l] = _even_bwd(dx, saved[l], l // 2, W, kvs[l], slopes, T, sync)
        else:
            dx, per[l] = _odd_bwd(dx, saved[l], l // 2, W, kvs[l], T, sync)
        per[l]["w_kv"] = matmul("mm_dwkv", mem_n, per[l]["kv"], "tn")
        dmem_n = matmul("mm_dmem", per[l]["kv"], W["w_kv"][l], "nt", add=dmem_n)
        if on_layer_grads is not None:
            dx = on_layer_grads(l, dx, per[l])
    dw_kv = [per[l]["w_kv"] for l in range(DEPTH)]
    (dmem_norm,) = rows_vjp_call("mem_rms_bwd", rms_tile, N_MEM, [("row", mem, 0, D_MODEL), ("full", mem_g)],
                                 [[("row", dmem_n, 0, D_MODEL)]], skip=(0,))
    ev, od = (per[0], per[2]), (per[1], per[3])
    (d_lb,) = lower_vjp(jnp.stack([e["low"] for e in ev]))
    grads = dict(
        w_in_e=jnp.stack([e["w_in"] for e in ev]), w_in_o=jnp.stack([o["w_in"] for o in od]),
        w_out_e=jnp.stack([e["w_out"] for e in ev]), w_out_o=jnp.stack([o["w_out"] for o in od]),
        w_kv=jnp.stack(dw_kv), norm_even=jnp.stack([e["norm"] for e in ev]), sink=jnp.stack([e["sink"] for e in ev]),
        lb_param=d_lb, hgrn_norm=jnp.stack([e["hg"] for e in ev]), norm_odd=jnp.stack([o["norm"] for o in od]),
        w_gate_up=jnp.stack([o["w_up"] for o in od]), b_gate=jnp.stack([o["b_gate"] for o in od]),
        gla_norm=jnp.stack([o["gg"] for o in od]), mem_norm=dmem_norm[0], final_norm=dgf[0])
    return loss, dx, grads


SMALL_SPECS = (("lb_param", (2, 2, 128)), ("norm_odd", (2, 256)), ("w_gate_up", (2, 2, 16, 128)),
               ("b_gate", (2, 2, 128)), ("gla_norm", (2, 256)))
SMALL_ROWS = 80


def _pack_small_local(d):
    return jnp.concatenate([d[n].reshape(-1) for n, _ in SMALL_SPECS]).reshape(SMALL_ROWS, 128)


def _unpack_small_local(b):
    flat, out, o = b.reshape(-1), {}, 0
    for n, shp in SMALL_SPECS:
        sz = int(np.prod(shp))
        out[n] = flat[o:o + sz].reshape(shp)
        o += sz
    return out


def _unpack_small_full(g4):
    per = [_unpack_small_local(g4[j]) for j in range(4)]
    return {n: jnp.concatenate([per[j][n] for j in range(4)], axis=-1) for n, _ in SMALL_SPECS}


def _pack_small_blocks(full):
    blocks = []
    for j in range(4):
        blocks.append(_pack_small_local({n: full[n][..., j * shp[-1]:(j + 1) * shp[-1]] for n, shp in SMALL_SPECS}))
    return jnp.stack(blocks)


def _cols(t, order, off, widths):
    return [t[..., off[n]:off[n] + widths[n]] for n in order]


EVEN_REF_ORDER = ("qA", "kA", "vA", "gA", "qB", "zf", "zb", "iB", "gB", "qM", "gM")
ODD_REF_ORDER = ("qC", "kC", "vC", "gC", "rr", "qM", "gM")


def _layer_weights(l, g_in, g_out, g_kv):
    t = g_in.transpose(1, 0, 2).reshape(D_MODEL, -1)
    if l % 2 == 0:
        w_in = jnp.concatenate(_cols(t, EVEN_ORDER, EVEN_REF_OFF, EVEN_W), axis=-1)
    else:
        w_in = jnp.concatenate(_cols(t, ODD_ORDER, ODD_REF_OFF, ODD_W) + [jnp.zeros((D_MODEL, ODD_PAD - ODD_IN), BF16)],
                               axis=-1)
    return w_in, g_out.reshape(MIX, D_MODEL), g_kv.reshape(D_MODEL, 2 * W_M)


def _layer_grad_blocks(l, gl):
    if l % 2 == 0:
        t = jnp.concatenate(_cols(gl["w_in"], EVEN_REF_ORDER, EVEN_OFF, EVEN_W), axis=-1)
    else:
        t = jnp.concatenate(_cols(gl["w_in"], ODD_REF_ORDER, ODD_OFF, ODD_W), axis=-1)
    b_in = t.reshape(D_MODEL, 4, -1).transpose(1, 2, 0)
    return [b_in, gl["w_out"].reshape(4, MIX // 4, D_MODEL), gl["w_kv"].reshape(4, D_MODEL // 4, 2 * W_M)]


WEIGHT_NAMES = ("norm_even", "w_in_even", "sink", "lb_param", "hgrn_norm", "w_out_even", "norm_odd", "w_in_odd",
                "w_gate_up", "b_gate", "gla_norm", "w_out_odd", "mem_norm", "w_mem_kv", "final_norm")


def kernel(x, mem, norm_even, w_in_even, sink, lb_param, hgrn_norm, w_out_even, norm_odd, w_in_odd, w_gate_up, b_gate, gla_norm, w_out_odd, mem_norm, w_mem_kv, final_norm, loss_target, m_norm_even, m_w_in_even, m_sink, m_lb_param, m_hgrn_norm, m_w_out_even, m_norm_odd, m_w_in_odd, m_w_gate_up, m_b_gate, m_gla_norm, m_w_out_odd, m_mem_norm, m_w_mem_kv, m_final_norm, v_norm_even, v_w_in_even, v_sink, v_lb_param, v_hgrn_norm, v_w_out_even, v_norm_odd, v_w_in_odd, v_w_gate_up, v_b_gate, v_gla_norm, v_w_out_odd, v_mem_norm, v_w_mem_kv, v_final_norm):
    w = dict(zip(WEIGHT_NAMES, (norm_even, w_in_even, sink, lb_param, hgrn_norm, w_out_even, norm_odd, w_in_odd,
                                w_gate_up, b_gate, gla_norm, w_out_odd, mem_norm, w_mem_kv, final_norm)))
    m = dict(zip(WEIGHT_NAMES, (m_norm_even, m_w_in_even, m_sink, m_lb_param, m_hgrn_norm, m_w_out_even, m_norm_odd,
                                m_w_in_odd, m_w_gate_up, m_b_gate, m_gla_norm, m_w_out_odd, m_mem_norm, m_w_mem_kv,
                                m_final_norm)))
    v = dict(zip(WEIGHT_NAMES, (v_norm_even, v_w_in_even, v_sink, v_lb_param, v_hgrn_norm, v_w_out_even, v_norm_odd,
                                v_w_in_odd, v_w_gate_up, v_b_gate, v_gla_norm, v_w_out_odd, v_mem_norm, v_w_mem_kv,
                                v_final_norm)))
    ci = lax.axis_index("c").astype(jnp.int32).reshape(1)
    chip = (2 * lax.axis_index("x") + lax.axis_index("y")).astype(jnp.int32).reshape(1)

    shards = []
    for l in range(DEPTH):
        names = ("w_in_even", "w_out_even") if l % 2 == 0 else ("w_in_odd", "w_out_odd")
        shards.append([w[names[0]][l // 2].astype(BF16), w[names[1]][l // 2].astype(BF16), w_mem_kv[l].astype(BF16)])
    small = _pack_small_local(w)
    own = lambda g, s: lax.dynamic_update_slice(g, s[None], (chip[0], 0, 0))
    first = [own(g, s) for g, s in zip(gather_weights(shards[0], small), shards[0] + [small])]
    later_shards = shards[1] + shards[2] + shards[3]
    later_raw = gather_weights_async(later_shards)
    w0 = _layer_weights(0, *first[0:3])
    W = dict(w_in_e=[w0[0]], w_out_e=[w0[1]], w_kv=[w0[2]])
    W.update(_unpack_small_full(first[3]))
    W.update({n: w[n] for n in ("norm_even", "sink", "hgrn_norm", "mem_norm", "final_norm")})

    def later(x1, W):
        x1, raw = lax.optimization_barrier((x1, list(later_raw)))
        g = [own(a, s) for a, s in zip(raw, later_shards)]
        w1, w2, w3 = (_layer_weights(l, *g[3 * (l - 1):3 * l]) for l in (1, 2, 3))
        W = dict(W)
        W.update(w_in_e=[w0[0], w2[0]], w_in_o=[w1[0], w3[0]], w_out_e=[w0[1], w2[1]], w_out_o=[w1[1], w3[1]],
                 w_kv=[w0[2], w1[2], w2[2], w3[2]])
        return x1, W

    place = jnp.concatenate([chip, ci])

    def start(tag, blocks, wire):
        axes = [2 if b.shape[1] == ODD_IN // 4 else 1 for b in blocks]
        return dict(tag=tag, blocks=blocks, wire=wire, step=0,
                    recv=exchange_siblings(f"rs_siblings_{tag}", blocks, axes, 2))

    def advance(p):
        if p["step"] == 0:
            sums = add_sibling(p["blocks"], p["recv"], ci, p["wire"])
            p["recv3"] = exchange_chips(f"rs_chips_{p['tag']}", sums, 3)
        else:
            p["mine"] = add_chips(p["blocks"], p["recv"], p["recv3"], place)
            p["other"] = exchange_siblings(f"rs_final_{p['tag']}", p["mine"], [None] * len(p["mine"]), 4)
        p["step"] += 1

    pipes, first_layer = [], {}

    def sync(a):
        for p in pipes:
            if p["step"] < 3:
                key = ("recv", "recv3", "other")[p["step"]]
                a, arrived = lax.optimization_barrier((a, list(p[key])))
                p[key] = arrived
                if p["step"] < 2:
                    advance(p)
                else:
                    p["step"] = 3
        return a

    def on_layer_grads(l, dx, gl):
        dx = sync(dx)
        if l == 0:
            first_layer.update(gl)
        else:
            pipes.append(start(f"l{l}", _layer_grad_blocks(l, gl), [BF16] * 3))
        return dx

    loss_tile, dx, grads = local_step(x[0], mem[0], loss_target[0], W, later, on_layer_grads, sync)
    last = start("l0", _layer_grad_blocks(0, first_layer) + [_pack_small_blocks(grads)], [BF16] * 3 + [F32])
    for p in pipes + [last]:
        while p["step"] < (1 if p is last else 2):
            advance(p)
    by_layer = {int(p["tag"][1:]): p for p in pipes + [last]}
    halves = lambda layers, k: (jnp.stack([by_layer[l]["mine"][k] for l in layers]),
                                jnp.stack([by_layer[l]["other"][k] for l in layers]))
    gl, upd = {}, {}

    pack = jnp.zeros((8, D_MODEL), F32)
    pack = pack.at[0:2].set(grads["norm_even"]).at[2].set(grads["hgrn_norm"].reshape(-1))
    pack = pack.at[3].set(grads["mem_norm"]).at[4].set(grads["final_norm"])
    pack = pack.at[5, 0:16].set(grads["sink"].reshape(-1)).at[5, 16].set(loss_tile[0, 0])
    tot = sum_devices(allgather_small(pack))
    gl.update(norm_even=tot[0:2], hgrn_norm=tot[2].reshape(2, W_B), mem_norm=tot[3], final_norm=tot[4],
              sink=tot[5, 0:16].reshape(2, N_Q_A))
    loss = tot[5, 16]
    for n in ("norm_even", "hgrn_norm", "mem_norm", "final_norm", "sink"):
        upd[n] = adamw_call(w[n], gl[n], m[n], v[n])
    tr_ = lambda a: jnp.swapaxes(a, 1, 2)
    gl["w_in_odd"], *upd["w_in_odd"] = [tr_(o) for o in adamw_halves(
        tr_(w["w_in_odd"]), *halves((1, 3), 0), tr_(m["w_in_odd"]), tr_(v["w_in_odd"]), ci)]
    gl["w_out_odd"], *upd["w_out_odd"] = adamw_halves(w["w_out_odd"], *halves((1, 3), 1), m["w_out_odd"],
                                                      v["w_out_odd"], ci)
    early = [upd[n] for n in sorted(upd)] + [gl["w_in_odd"], gl["w_out_odd"]]
    last["recv3"], early = lax.optimization_barrier((list(last["recv3"]), early))
    for n, res in zip(sorted(upd), early):
        upd[n] = res
    gl["w_in_odd"], gl["w_out_odd"] = early[-2:]
    advance(last)

    big = dict(w_in_even=halves((0, 2), 0), w_out_even=halves((0, 2), 1), w_mem_kv=halves((0, 1, 2, 3), 2))
    s_mine, s_other = last["mine"][3], last["other"][3]
    g_small = jnp.where(ci[0] == 0, jnp.concatenate([s_mine, s_other]), jnp.concatenate([s_other, s_mine]))
    gl.update(_unpack_small_local(g_small))
    for n in WEIGHT_NAMES:
        if n == "w_in_even":
            gl[n], *upd[n] = [tr_(o) for o in adamw_halves(tr_(w[n]), *big[n], tr_(m[n]), tr_(v[n]), ci)]
        elif n in big:
            gl[n], *upd[n] = adamw_halves(w[n], *big[n], m[n], v[n], ci)
        elif n not in upd:
            upd[n] = adamw_call(w[n], gl[n], m[n], v[n])
    return (loss, dx[None], *[gl[n] for n in WEIGHT_NAMES], *[upd[n][0] for n in WEIGHT_NAMES],
            *[upd[n][1] for n in WEIGHT_NAMES], *[upd[n][2] for n in WEIGHT_NAMES])
```

```python
import functools

import numpy as np
import jax
import jax.numpy as jnp
from jax import lax
from jax.experimental import pallas as pl
from jax.experimental.pallas import tpu as pltpu
from jax.experimental.pallas import tpu_sc as plsc

F32 = jnp.float32
BF16 = jnp.bfloat16

D_MODEL = 1024
DEPTH = 4
N_Q_A, N_KV_A, HEAD_DIM_A = 8, 2, 64
W_A, W_KV_A = 512, 128
WINDOW = 128
BLOCK = 128
N_HEADS_B, HEAD_DIM_B, W_B = 4, 128, 512
N_HEADS_C, DK_C, DV_C, WK_C, WV_C = 4, 128, 256, 512, 1024
GATE_RANK = 16
GATE_TEMP = 16.0
N_MEM, N_HEADS_M, HEAD_DIM_M, W_M = 256, 4, 128, 512
EPS = 1e-6
MASK_VALUE = -1e30
MIN_GATE = 1e-30
EVEN_IN, ODD_IN = 4864, 4128
ODD_PAD = 4224
MIX = 1536
ADAM_LR, ADAM_B1, ADAM_B2, ADAM_EPS, ADAM_WD, ADAM_STEP = 0.001, 0.9, 0.999, 1e-08, 0.01, 10

SCAN_CHUNK = 128
SCAN_SUB = 2
SCAN_LEVELS = 7
VMEM_LIMIT = 56 * 1024 * 1024

EVEN_REF_OFF = dict(qA=0, kA=512, vA=640, gA=768, qB=1280, zf=1792, zb=2304, iB=2816, gB=3328, qM=3840, gM=4352)
EVEN_W = dict(qA=512, kA=128, vA=128, gA=512, qB=512, zf=512, zb=512, iB=512, gB=512, qM=512, gM=512)
EVEN_ORDER = ("qA", "gA", "qB", "zf", "zb", "iB", "gB", "qM", "gM", "kA", "vA")
ODD_REF_OFF = dict(qC=0, kC=512, vC=1024, gC=2048, rr=3072, qM=3104, gM=3616)
ODD_W = dict(qC=512, kC=512, vC=1024, gC=1024, rr=32, qM=512, gM=512)
ODD_ORDER = ("qC", "kC", "vC", "gC", "qM", "gM", "rr")


def _offsets(order, widths):
    off, o = {}, 0
    for n in order:
        off[n] = o
        o += widths[n]
    return off


EVEN_OFF = _offsets(EVEN_ORDER, EVEN_W)
ODD_OFF = _offsets(ODD_ORDER, ODD_W)


def _dg(a, b, ca, cb):
    return lax.dot_general(a.astype(BF16), b.astype(BF16), (((ca,), (cb,)), ((), ())),
                           preferred_element_type=F32)


def dot_nn(a, b):
    return _dg(a, b, 1, 0)


def dot_nt(a, b):
    return _dg(a, b, 1, 1)


def dot_tn(a, b):
    return _dg(a, b, 0, 0)


@jax.custom_vjp
def bdot(a, b):
    return dot_nn(a, b)


bdot.defvjp(lambda a, b: (dot_nn(a, b), (a, b)),
            lambda r, g: (dot_nt(g, r[1]), dot_tn(r[0], g)))


@jax.custom_vjp
def bdot_t(a, b):
    return dot_nt(a, b)


bdot_t.defvjp(lambda a, b: (dot_nt(a, b), (a, b)),
              lambda r, g: (dot_nn(g, r[1]), dot_tn(g, r[0])))


@jax.custom_vjp
def bdot_tn(a, b):
    return dot_tn(a, b)


bdot_tn.defvjp(lambda a, b: (dot_tn(a, b), (a, b)),
               lambda r, g: (dot_nt(r[1], g), dot_nn(r[0], g)))


def _split_mm(h, x):
    hi = x.astype(BF16)
    lo = (x - hi.astype(F32)).astype(BF16)
    return (lax.dot_general(h, hi, (((1,), (0,)), ((), ())), preferred_element_type=F32)
            + lax.dot_general(h, lo, (((1,), (0,)), ((), ())), preferred_element_type=F32))


def _sigmoid(z):
    return 1.0 / (1.0 + jnp.exp(-z))


def _silu(z):
    return z * _sigmoid(z)


def _log_sigmoid(z):
    return jnp.minimum(z, 0.0) - jnp.log(1.0 + jnp.exp(-jnp.abs(z)))


def _rms(x, g):
    return x * lax.rsqrt(jnp.mean(x * x, axis=-1, keepdims=True) + EPS) * g


def rms_tile(x, g):
    return (_rms(x, g),)


@functools.partial(jax.custom_vjp, nondiff_argnums=(1, 2))
def split(x, n, axis):
    w = x.shape[axis] // n
    return tuple(lax.slice_in_dim(x, h * w, (h + 1) * w, axis=axis) for h in range(n))


split.defvjp(lambda x, n, axis: (split(x, n, axis), None),
             lambda n, axis, _, cts: (jnp.concatenate(cts, axis=axis),))


def _group_rms(o, g, heads):
    return jnp.concatenate([_rms(oh, gh) for oh, gh in zip(split(o, heads, 1), split(g, heads, 1))], axis=-1)


def even_post_tile(a, o2f, o2b, mo, gA, gB, gM, hg):
    y = _group_rms(o2f + o2b, hg, N_HEADS_B)
    return (jnp.concatenate([a * _silu(gA), y * _silu(gB), mo * _silu(gM)], axis=-1),)


def odd_post_tile(o2f, o2b, mo, gC, gM, gg):
    y = _group_rms(o2f + o2b, gg, N_HEADS_C)
    return (jnp.concatenate([y * _silu(gC), mo * _silu(gM)], axis=-1),)


def hgrn_prep(raw, par):
    qB, z, iB = raw
    (lb,) = par
    f = lb + (1.0 - lb) * _sigmoid(z)
    return _silu(qB), (1.0 - lb) * _sigmoid(-z), iB, jnp.log(jnp.maximum(f, MIN_GATE))


def gla_prep(raw, par):
    qC, kC, vC, r128 = raw
    wup, bg = par
    return qC * (DK_C ** -0.5), kC, vC, _log_sigmoid(bdot(r128, wup) + bg) / GATE_TEMP


def mem_tile(q, k, v):
    s = bdot_t(q, k) * (HEAD_DIM_M ** -0.5)
    m = lax.stop_gradient(jnp.max(s, axis=-1, keepdims=True))
    p = jnp.exp(s - m)
    p = p / jnp.sum(p, axis=-1, keepdims=True)
    return (bdot(p, v),)


ATTN_GROUP = N_Q_A // N_KV_A


def attn_block(q, ks, vs, sink, slope, c, seq):
    rows = ATTN_GROUP * BLOCK
    i = lax.broadcasted_iota(jnp.int32, (rows, 3 * BLOCK), 0) % BLOCK
    j = lax.broadcasted_iota(jnp.int32, (rows, 3 * BLOCK), 1)
    dist = jnp.abs(i - j + BLOCK).astype(F32)
    kpos = (c - 1) * BLOCK + j
    valid = (dist <= WINDOW) & (kpos >= 0) & (kpos < seq)
    s = bdot_t(q, ks) * (HEAD_DIM_A ** -0.5)
    s = jnp.where(valid, s - slope * dist, MASK_VALUE)
    m = lax.stop_gradient(jnp.maximum(jnp.max(s, axis=-1, keepdims=True), sink))
    p = jnp.where(valid, jnp.exp(s - m), 0.0)
    denom = jnp.sum(p, axis=-1, keepdims=True) + jnp.exp(sink - m)
    return bdot(p, vs) / denom


def scan_chunk(q, k, v, e, tot, st, qm, pm):
    C = SCAN_CHUNK
    e = split(e, 2 + SCAN_LEVELS, 0)
    qe = q * jnp.exp(e[0])
    kd = k * jnp.exp(e[1])
    r = lax.broadcasted_iota(jnp.int32, (C, C), 0)
    s = lax.broadcasted_iota(jnp.int32, (C, C), 1)
    a = jnp.where(r == s, jnp.sum(q * k, axis=-1, keepdims=True), 0.0)
    for l in range(SCAN_LEVELS):
        u = jnp.where(qm[l * C:(l + 1) * C] != 0.0, q, k) * jnp.exp(e[2 + l])
        a = a + bdot_t(u, u) * pm[l * C:(l + 1) * C]
    o = bdot_t(qe, st) + bdot(a, v)
    st_new = st * jnp.exp(tot) + bdot_tn(v, kd)
    return o, st_new


def _scan_consts():
    C, L = SCAN_CHUNK, SCAN_LEVELS
    t = np.arange(C)[:, None]
    r = np.arange(C)[None, :]
    blocks = [(r <= t), (r > t)]
    qms, pms = [], []
    for l in range(1, L + 1):
        m = C >> l
        upper_t = (t % (2 * m)) >= m
        upper_r = (r % (2 * m)) >= m
        same_half = (t // m) == (r // m)
        blocks.append(same_half & np.where(upper_t, r <= t, r > t))
        qms.append(np.broadcast_to(upper_t, (C, C)))
        pms.append(((t // (2 * m)) == (r // (2 * m))) & upper_t & ~upper_r)
    hf = np.concatenate(blocks, axis=0).astype(np.float32)
    flip = lambda mat: mat.reshape(-1, C, C)[:, ::-1, ::-1].reshape(-1, C)
    qmf = np.concatenate(qms, axis=0).astype(np.float32)
    pmf = np.concatenate(pms, axis=0).astype(np.float32)
    h = np.stack([hf, flip(hf)])
    ht = np.stack([h[0].T, h[1].T])
    qm = np.stack([qmf, 1.0 - qmf])
    pm = np.stack([pmf, flip(pmf)])
    return h, ht, qm, pm


def _cparams(sem):
    return pltpu.CompilerParams(dimension_semantics=sem, vmem_limit_bytes=VMEM_LIMIT)


def _row_tile(T):
    return min(T, 512)


def _in_spec(spec, tr):
    kind = spec[0]
    if kind == "row":
        _, arr, off, w = spec
        assert off % w == 0
        return arr, pl.BlockSpec((tr, w), functools.partial(lambda i, b: (i, b), b=off // w))
    if kind == "row3":
        _, arr, d, off, w = spec
        assert off % w == 0
        return arr, pl.BlockSpec((None, tr, w), functools.partial(lambda i, d, b: (d, i, b), d=d, b=off // w))
    _, arr = spec
    return arr, pl.BlockSpec(arr.shape, functools.partial(lambda i, n: (0,) * n, n=arr.ndim))


def rows_call(name, tile_fn, T, ins, out_widths, out_dtypes=None, stacks=None):
    tr = _row_tile(T)
    n_in = len(ins)
    out_dtypes = out_dtypes or [F32] * len(out_widths)
    stacks = stacks or [(k,) for k in range(len(out_widths))]

    def body(*refs):
        vals = [r[...] for r in refs[:n_in]]
        outs = tile_fn(*vals)
        for r, members in zip(refs[n_in:], stacks):
            if len(members) == 1:
                r[...] = outs[members[0]].astype(r.dtype)
            else:
                for d, k in enumerate(members):
                    r[d] = outs[k].astype(r.dtype)

    in_specs, args = [], []
    for spec in ins:
        arr, bs = _in_spec(spec, tr)
        args.append(arr)
        in_specs.append(bs)
    out_specs, out_shape = [], []
    for w, dt, members in zip(out_widths, out_dtypes, stacks):
        n = len(members)
        if n == 1:
            out_specs.append(pl.BlockSpec((tr, w), lambda i: (i, 0)))
            out_shape.append(jax.ShapeDtypeStruct((T, w), dt))
        else:
            out_specs.append(pl.BlockSpec((n, tr, w), lambda i: (0, i, 0)))
            out_shape.append(jax.ShapeDtypeStruct((n, T, w), dt))
    return pl.pallas_call(body, out_shape=out_shape, grid=(T // tr,), in_specs=in_specs, out_specs=out_specs,
                          name=name, compiler_params=_cparams(("arbitrary",)))(*args)


def rows_vjp_call(name, tile_fn, T, ins, cts, skip=(), narrow=()):
    tr = _row_tile(T)
    n_in = len(ins)
    n_ct = [len(c) for c in cts]
    want = [k for k in range(n_in) if k not in skip]

    def body(*refs):
        i = pl.program_id(0)
        vals = [r[...] for r in refs[:n_in]]
        ct, pos = [], n_in
        for n in n_ct:
            acc = refs[pos][...]
            for r in refs[pos + 1:pos + n]:
                acc = acc + r[...]
            ct.append(acc)
            pos += n
        _, vjp = jax.vjp(tile_fn, *vals)
        grads = vjp(tuple(ct))
        for r, k in zip(refs[pos:], want):
            if ins[k][0] == "full":
                @pl.when(i == 0)
                def _():
                    r[...] = jnp.zeros_like(r)
                r[...] += grads[k]
            else:
                r[...] = grads[k].astype(r.dtype)

    in_specs, args = [], []
    for spec in list(ins) + [s for c in cts for s in c]:
        arr, bs = _in_spec(spec, tr)
        args.append(arr)
        in_specs.append(bs)
    out_specs, out_shape = [], []
    for k in want:
        if ins[k][0] == "full":
            arr = ins[k][1]
            out_specs.append(pl.BlockSpec(arr.shape, functools.partial(lambda i, n: (0,) * n, n=arr.ndim)))
            out_shape.append(jax.ShapeDtypeStruct(arr.shape, F32))
        else:
            w = ins[k][-1]
            out_specs.append(pl.BlockSpec((tr, w), lambda i: (i, 0)))
            out_shape.append(jax.ShapeDtypeStruct((T, w), BF16 if k in narrow else F32))
    return pl.pallas_call(body, out_shape=out_shape, grid=(T // tr,), in_specs=in_specs, out_specs=out_specs,
                          name=name, compiler_params=_cparams(("arbitrary",)))(*args)


def matmul(name, a, b, mode, add=None, out_dtype=F32):
    if mode == "tn":
        K, M = a.shape
        N = b.shape[1]
        tm = M if M <= 1536 else 512
        tn = N if N <= 1280 else (N // 2 if (N // 2) % 128 == 0 else N)
        tk = min(K, 512)
        grid = (M // tm, N // tn, K // tk)

        def body(a_ref, b_ref, o_ref):
            @pl.when(pl.program_id(2) == 0)
            def _():
                o_ref[...] = jnp.zeros_like(o_ref)
            o_ref[...] += dot_tn(a_ref[...], b_ref[...])

        return pl.pallas_call(
            body, out_shape=jax.ShapeDtypeStruct((M, N), F32), grid=grid,
            in_specs=[pl.BlockSpec((tk, tm), lambda i, j, k: (k, i)), pl.BlockSpec((tk, tn), lambda i, j, k: (k, j))],
            out_specs=pl.BlockSpec((tm, tn), lambda i, j, k: (i, j)), name=name,
            compiler_params=_cparams(("arbitrary", "arbitrary", "arbitrary")))(a, b)

    M, K = a.shape
    N = b.shape[1] if mode == "nn" else b.shape[0]
    tm = min(M, 512)
    tn = N if N <= 1536 else (N // 2 if (N // 2) % 128 == 0 else (N // 3 if (N // 3) % 128 == 0 else N))
    grid = (N // tn, M // tm)
    n_in = 2 + (add is not None)

    def body(*refs):
        a_ref, b_ref = refs[0], refs[1]
        o_ref = refs[n_in]
        acc = dot_nn(a_ref[...], b_ref[...]) if mode == "nn" else dot_nt(a_ref[...], b_ref[...])
        if add is not None:
            acc = acc + refs[2][...]
        o_ref[...] = acc.astype(o_ref.dtype)

    in_specs = [pl.BlockSpec((tm, K), lambda j, i: (i, 0)),
                pl.BlockSpec((K, tn), lambda j, i: (0, j)) if mode == "nn" else pl.BlockSpec((tn, K), lambda j, i: (j, 0))]
    args = [a, b]
    if add is not None:
        in_specs.append(pl.BlockSpec((tm, tn), lambda j, i: (i, j)))
        args.append(add)
    return pl.pallas_call(
        body, out_shape=jax.ShapeDtypeStruct((M, N), out_dtype), grid=grid, in_specs=in_specs,
        out_specs=pl.BlockSpec((tm, tn), lambda j, i: (i, j)), name=name,
        compiler_params=_cparams(("arbitrary", "arbitrary")))(*args)


def norm_project(name, x, g, w):
    T, D = x.shape
    N = w.shape[1]
    tm = min(T, 512)

    def body(x_ref, g_ref, w_ref, h_ref, p_ref):
        h = _rms(x_ref[...], g_ref[...]).astype(BF16)
        h_ref[...] = h
        p_ref[...] = dot_nn(h, w_ref[...])

    return pl.pallas_call(
        body, out_shape=[jax.ShapeDtypeStruct((T, D), BF16), jax.ShapeDtypeStruct((T, N), F32)], grid=(T // tm,),
        in_specs=[pl.BlockSpec((tm, D), lambda i: (i, 0)), pl.BlockSpec((1, D), lambda i: (0, 0)),
                  pl.BlockSpec((D, N), lambda i: (0, 0))],
        out_specs=[pl.BlockSpec((tm, D), lambda i: (i, 0)), pl.BlockSpec((tm, N), lambda i: (i, 0))],
        name=name, compiler_params=_cparams(("arbitrary",)))(x, g, w)


def norm_project_bwd(name, dp, w, x, g, dy):
    T, D = x.shape
    N = w.shape[1]
    tm = _row_tile(T)

    def body(dp_ref, w_ref, x_ref, g_ref, dy_ref, dx_ref, dg_ref):
        @pl.when(pl.program_id(0) == 0)
        def _():
            dg_ref[...] = jnp.zeros_like(dg_ref)

        _, vjp = jax.vjp(_rms, x_ref[...], g_ref[...])
        dx, dg = vjp(dot_nt(dp_ref[...], w_ref[...]))
        dx_ref[...] = dx + dy_ref[...]
        dg_ref[...] += dg

    row = pl.BlockSpec((tm, D), lambda i: (i, 0))
    vec = pl.BlockSpec((1, D), lambda i: (0, 0))
    return pl.pallas_call(
        body, out_shape=[jax.ShapeDtypeStruct((T, D), F32), jax.ShapeDtypeStruct((1, D), F32)], grid=(T // tm,),
        in_specs=[pl.BlockSpec((tm, N), lambda i: (i, 0)), pl.BlockSpec((D, N), lambda i: (0, 0)), row, vec, row],
        out_specs=[row, vec], name=name, compiler_params=_cparams(("arbitrary",)))(dp, w, x, g, dy)


def mix_project(name, tile_fn, T, ins, w, x):
    tr = _row_tile(T)
    n_in = len(ins)
    K, D = w.shape

    def body(*refs):
        w_ref, x_ref, y_ref = refs[n_in:]
        (mix,) = tile_fn(*[r[...] for r in refs[:n_in]])
        y_ref[...] = x_ref[...] + dot_nn(mix, w_ref[...])

    in_specs, args = [], []
    for spec in ins:
        arr, bs = _in_spec(spec, tr)
        args.append(arr)
        in_specs.append(bs)
    row = pl.BlockSpec((tr, D), lambda i: (i, 0))
    return pl.pallas_call(
        body, out_shape=jax.ShapeDtypeStruct((T, D), F32), grid=(T // tr,),
        in_specs=in_specs + [pl.BlockSpec((K, D), lambda i: (0, 0)), row], out_specs=row,
        name=name, compiler_params=_cparams(("arbitrary",)))(*args, w, x)


def mix_project_bwd(name, tile_fn, T, ins, w, dy, skip=(), narrow=()):
    tr = _row_tile(T)
    n_in = len(ins)
    K, D = w.shape
    want = [k for k in range(n_in) if k not in skip]

    def body(*refs):
        w_ref, dy_ref = refs[n_in:n_in + 2]
        outs, dw_ref = refs[n_in + 2:-1], refs[-1]
        first = pl.program_id(0) == 0
        (mix,), vjp = jax.vjp(tile_fn, *[r[...] for r in refs[:n_in]])
        d = dy_ref[...].astype(BF16)
        grads = vjp((dot_nt(d, w_ref[...]),))

        @pl.when(first)
        def _():
            dw_ref[...] = jnp.zeros_like(dw_ref)

        dw_ref[...] += dot_tn(mix, d)
        for r, k in zip(outs, want):
            if ins[k][0] == "full":
                @pl.when(first)
                def _():
                    r[...] = jnp.zeros_like(r)
                r[...] += grads[k]
            else:
                r[...] = grads[k].astype(r.dtype)

    in_specs, args = [], []
    for spec in ins:
        arr, bs = _in_spec(spec, tr)
        args.append(arr)
        in_specs.append(bs)
    out_specs, out_shape = [], []
    for k in want:
        if ins[k][0] == "full":
            arr = ins[k][1]
            out_specs.append(_full_spec(arr))
            out_shape.append(jax.ShapeDtypeStruct(arr.shape, F32))
        else:
            wd = ins[k][-1]
            out_specs.append(pl.BlockSpec((tr, wd), lambda i: (i, 0)))
            out_shape.append(jax.ShapeDtypeStruct((T, wd), BF16 if k in narrow else F32))
    wspec = pl.BlockSpec((K, D), lambda i: (0, 0))
    return pl.pallas_call(
        body, out_shape=out_shape + [jax.ShapeDtypeStruct((K, D), F32)], grid=(T // tr,),
        in_specs=in_specs + [wspec, pl.BlockSpec((tr, D), lambda i: (i, 0))], out_specs=out_specs + [wspec],
        name=name, compiler_params=_cparams(("arbitrary",)))(*args, w, dy)


def _attn_heads(n):
    G = N_Q_A // N_KV_A
    k_sl = pl.ds(n * HEAD_DIM_A, HEAD_DIM_A)
    v_sl = pl.ds(W_KV_A + n * HEAD_DIM_A, HEAD_DIM_A)
    q_sl = [pl.ds((n * G + g) * HEAD_DIM_A, HEAD_DIM_A) for g in range(G)]
    return k_sl, v_sl, q_sl, range(n * G, (n + 1) * G)


def attn_fwd(p, q_off, kvp, sink, slopes, T):
    nb = T // BLOCK
    assert q_off % W_A == 0

    def body(q_ref, kv_ref, sink_ref, slope_ref, o_ref):
        c = pl.program_id(0)
        rows = pl.ds(pl.multiple_of(c * BLOCK, BLOCK), 3 * BLOCK)
        for n in range(N_KV_A):
            k_sl, v_sl, q_sl, heads = _attn_heads(n)
            group = pl.ds(n * ATTN_GROUP * BLOCK, ATTN_GROUP * BLOCK)
            q = jnp.concatenate([q_ref[:, s] for s in q_sl], axis=0)
            o = attn_block(q, kv_ref[rows, k_sl], kv_ref[rows, v_sl], sink_ref[group, :], slope_ref[group, :], c, T)
            for g, s in enumerate(q_sl):
                o_ref[:, s] = o[g * BLOCK:(g + 1) * BLOCK]

    full = lambda a: pl.BlockSpec(a.shape, functools.partial(lambda c, nd: (0,) * nd, nd=a.ndim))
    return pl.pallas_call(
        body, out_shape=jax.ShapeDtypeStruct((T, W_A), F32), grid=(nb,),
        in_specs=[pl.BlockSpec((BLOCK, W_A), lambda c: (c, q_off // W_A)), full(kvp), full(sink), full(slopes)],
        out_specs=pl.BlockSpec((BLOCK, W_A), lambda c: (c, 0)),
        name="attn_fwd", compiler_params=_cparams(("arbitrary",)))(p, kvp, sink, slopes)


def attn_bwd(p, q_off, kvp, sink, slopes, do, T):
    nb = T // BLOCK

    def body(q_ref, kv_ref, sink_ref, slope_ref, do_ref, dq_ref, dkv_ref, dsink_ref):
        c = pl.program_id(0)

        @pl.when(c == 0)
        def _():
            dkv_ref[...] = jnp.zeros_like(dkv_ref)
            dsink_ref[...] = jnp.zeros_like(dsink_ref)

        rows = pl.ds(pl.multiple_of(c * BLOCK, BLOCK), 3 * BLOCK)
        for n in range(N_KV_A):
            k_sl, v_sl, q_sl, heads = _attn_heads(n)
            group = pl.ds(n * ATTN_GROUP * BLOCK, ATTN_GROUP * BLOCK)
            slope = slope_ref[group, :]
            q = jnp.concatenate([q_ref[:, s] for s in q_sl], axis=0)
            do = jnp.concatenate([do_ref[:, s] for s in q_sl], axis=0)
            _, vjp = jax.vjp(lambda q_, kk, vv, sk: attn_block(q_, kk, vv, sk, slope, c, T),
                             q, kv_ref[rows, k_sl], kv_ref[rows, v_sl], sink_ref[group, :])
            dq, dks, dvs, dsk = vjp(do)
            dkv_ref[rows, k_sl] += dks
            dkv_ref[rows, v_sl] += dvs
            for g, (s, h) in enumerate(zip(q_sl, heads)):
                seg = slice(g * BLOCK, (g + 1) * BLOCK)
                dq_ref[:, s] = dq[seg].astype(dq_ref.dtype)
                dsink_ref[h] += jnp.sum(dsk[seg], axis=0, keepdims=True)

    full = lambda a: pl.BlockSpec(a.shape, functools.partial(lambda c, nd: (0,) * nd, nd=a.ndim))
    qspec = pl.BlockSpec((BLOCK, W_A), lambda c: (c, 0))
    return pl.pallas_call(
        body,
        out_shape=[jax.ShapeDtypeStruct((T, W_A), BF16), jax.ShapeDtypeStruct(kvp.shape, F32),
                   jax.ShapeDtypeStruct((N_Q_A, 1, 1), F32)],
        grid=(nb,),
        in_specs=[pl.BlockSpec((BLOCK, W_A), lambda c: (c, q_off // W_A)), full(kvp), full(sink), full(slopes), qspec],
        out_specs=[qspec, full(kvp), pl.BlockSpec((N_Q_A, 1, 1), lambda c: (0, 0, 0))],
        name="attn_bwd", compiler_params=_cparams(("arbitrary",)))(p, kvp, sink, slopes, do)


def mem_fwd(p, q_off, kv, T):
    tr = min(T, 2 * _row_tile(T))
    assert q_off % W_M == 0

    def body(q_ref, kv_ref, o_ref):
        for h in range(N_HEADS_M):
            hs = pl.ds(h * HEAD_DIM_M, HEAD_DIM_M)
            (o,) = mem_tile(q_ref[:, hs], kv_ref[:, hs], kv_ref[:, pl.ds(W_M + h * HEAD_DIM_M, HEAD_DIM_M)])
            o_ref[:, hs] = o

    return pl.pallas_call(
        body, out_shape=jax.ShapeDtypeStruct((T, W_M), F32), grid=(T // tr,),
        in_specs=[pl.BlockSpec((tr, W_M), lambda i: (i, q_off // W_M)), pl.BlockSpec((N_MEM, 2 * W_M), lambda i: (0, 0))],
        out_specs=pl.BlockSpec((tr, W_M), lambda i: (i, 0)),
        name="mem_fwd", compiler_params=_cparams(("arbitrary",)))(p, kv)


def mem_bwd(p, q_off, kv, do, T):
    tr = min(T, 2 * _row_tile(T))

    def body(q_ref, kv_ref, do_ref, dq_ref, dkv_ref):
        @pl.when(pl.program_id(0) == 0)
        def _():
            dkv_ref[...] = jnp.zeros_like(dkv_ref)

        for h in range(N_HEADS_M):
            hs = pl.ds(h * HEAD_DIM_M, HEAD_DIM_M)
            vs = pl.ds(W_M + h * HEAD_DIM_M, HEAD_DIM_M)
            _, vjp = jax.vjp(mem_tile, q_ref[:, hs], kv_ref[:, hs], kv_ref[:, vs])
            dq, dk, dv = vjp((do_ref[:, hs],))
            dq_ref[:, hs] = dq.astype(dq_ref.dtype)
            dkv_ref[:, hs] += dk
            dkv_ref[:, vs] += dv

    kvspec = pl.BlockSpec((N_MEM, 2 * W_M), lambda i: (0, 0))
    return pl.pallas_call(
        body,
        out_shape=[jax.ShapeDtypeStruct((T, W_M), BF16), jax.ShapeDtypeStruct((N_MEM, 2 * W_M), F32)],
        grid=(T // tr,),
        in_specs=[pl.BlockSpec((tr, W_M), lambda i: (i, q_off // W_M)), kvspec, pl.BlockSpec((tr, W_M), lambda i: (i, 0))],
        out_specs=[pl.BlockSpec((tr, W_M), lambda i: (i, 0)), kvspec],
        name="mem_bwd", compiler_params=_cparams(("arbitrary",)))(p, kv, do)


def _scan_const_specs(dk):
    C, L = SCAN_CHUNK, SCAN_LEVELS
    return [pl.BlockSpec((2, (2 + L) * C, C), lambda n: (0, 0, 0)),
            pl.BlockSpec((2, C, (2 + L) * C), lambda n: (0, 0, 0)),
            pl.BlockSpec((2, L * C, dk), lambda n: (0, 0, 0)),
            pl.BlockSpec((2, L * C, C), lambda n: (0, 0, 0))]


def _chunk_spec(src, width, chunk_of):
    arr, sel = src
    if arr.ndim == 2:
        assert sel % width == 0
        return pl.BlockSpec((SCAN_CHUNK * SCAN_SUB, width),
                            functools.partial(lambda n, b: (chunk_of(n), b), b=sel // width))
    return pl.BlockSpec((None, SCAN_CHUNK * SCAN_SUB, width), functools.partial(lambda n, d: (d, chunk_of(n), 0), d=sel))


def _scan_const_args():
    h, ht, qm, pm = _scan_consts()
    return [jnp.asarray(h, BF16), jnp.asarray(ht, BF16), jnp.asarray(qm, F32), jnp.asarray(pm, F32)]


def _full_spec(a):
    return pl.BlockSpec(a.shape, functools.partial(lambda n, nd: (0,) * nd, nd=a.ndim))


def scan_fwd(name, prep, raws, params, heads, dk, dv, T):
    C, S = SCAN_CHUNK, SCAN_SUB
    N = T // (C * S)
    assert dk == C
    Wv = heads * dv
    orders = (lambda n: n, lambda n: N - 1 - n)
    n_raw, n_par = [len(r) for r in raws], [len(p) for p in params]

    def body(*refs):
        pos, raw_refs, par_refs = 0, [], []
        for d in range(2):
            raw_refs.append(refs[pos:pos + n_raw[d]])
            pos += n_raw[d]
        for d in range(2):
            par_refs.append(refs[pos:pos + n_par[d]])
            pos += n_par[d]
        h_ref, ht_ref, qm_ref, pm_ref = refs[pos:pos + 4]
        o_refs, ss_refs, st_ref = refs[pos + 4:pos + 6], refs[pos + 6:pos + 8], refs[pos + 8]

        @pl.when(pl.program_id(0) == 0)
        def _():
            st_ref[...] = jnp.zeros_like(st_ref)

        for d in range(2):
            consts = (qm_ref[d], pm_ref[d])
            pars = [p[...] for p in par_refs[d]]
            for sub in (range(S) if d == 0 else reversed(range(S))):
                rows = pl.ds(sub * C, C)
                q, k, v, g = prep([r[rows, :] for r in raw_refs[d]], pars)
                e = _split_mm(h_ref[d], g)
                tot = jnp.sum(g, axis=0, keepdims=True)
                for h in range(heads):
                    ks, vs = slice(h * dk, (h + 1) * dk), slice(h * dv, (h + 1) * dv)
                    st = st_ref[d, h]
                    ss_refs[d][h, sub] = st
                    o, st_new = scan_chunk(q[:, ks], k[:, ks], v[:, vs], e[:, ks], tot[:, ks], st, *consts)
                    o_refs[d][rows, vs] = o
                    st_ref[d, h] = st_new

    ss_spec = lambda order: pl.BlockSpec((heads, S, dv, dk), lambda n: (0, order(n), 0, 0))
    return pl.pallas_call(
        body,
        out_shape=[jax.ShapeDtypeStruct((T, Wv), F32)] * 2 + [jax.ShapeDtypeStruct((heads, T // C, dv, dk), F32)] * 2,
        grid=(N,),
        in_specs=[_chunk_spec(s, w, orders[d]) for d in range(2) for s, w in raws[d]]
        + [_full_spec(p) for d in range(2) for p in params[d]] + _scan_const_specs(dk),
        out_specs=[pl.BlockSpec((C * S, Wv), lambda n: (orders[0](n), 0)),
                   pl.BlockSpec((C * S, Wv), lambda n: (orders[1](n), 0)), ss_spec(orders[0]), ss_spec(orders[1])],
        scratch_shapes=[pltpu.VMEM((2, heads, dv, dk), F32)],
        name=name, compiler_params=_cparams(("arbitrary",)))(
            *[s[0] for d in range(2) for s, _ in raws[d]], *[p for d in range(2) for p in params[d]], *_scan_const_args())


def scan_bwd(name, prep, raws, params, ss, do, heads, dk, dv, T):
    C, S = SCAN_CHUNK, SCAN_SUB
    N = T // (C * S)
    Wv = heads * dv
    orders = (lambda n: N - 1 - n, lambda n: n)
    n_raw, n_par = [len(r) for r in raws], [len(p) for p in params]

    def body(*refs):
        pos, raw_refs, par_refs, draw_refs, dpar_refs = 0, [], [], [], []
        for group, counts in ((raw_refs, n_raw), (par_refs, n_par)):
            for d in range(2):
                group.append(refs[pos:pos + counts[d]])
                pos += counts[d]
        ss_refs, do_refs = refs[pos:pos + 2], refs[pos + 2:pos + 4]
        h_ref, ht_ref, qm_ref, pm_ref = refs[pos + 4:pos + 8]
        pos += 8
        for group, counts in ((draw_refs, n_raw), (dpar_refs, n_par)):
            for d in range(2):
                group.append(refs[pos:pos + counts[d]])
                pos += counts[d]
        dst_ref = refs[pos]

        @pl.when(pl.program_id(0) == 0)
        def _():
            dst_ref[...] = jnp.zeros_like(dst_ref)
            for d in range(2):
                for r in dpar_refs[d]:
                    r[...] = jnp.zeros_like(r)

        for d in range(2):
            consts = (qm_ref[d], pm_ref[d])
            pars = [p[...] for p in par_refs[d]]
            for sub in (reversed(range(S)) if d == 0 else range(S)):
                rows = pl.ds(sub * C, C)
                (q, k, v, g), prep_vjp = jax.vjp(prep, [r[rows, :] for r in raw_refs[d]], pars)
                e = _split_mm(h_ref[d], g)
                tot = jnp.sum(g, axis=0, keepdims=True)
                dqs, dks, dvs, des, dtots = [], [], [], [], []
                for h in range(heads):
                    ks, vs = slice(h * dk, (h + 1) * dk), slice(h * dv, (h + 1) * dv)
                    _, vjp = jax.vjp(lambda q_, k_, v_, e_, t_, st_: scan_chunk(q_, k_, v_, e_, t_, st_, *consts),
                                     q[:, ks], k[:, ks], v[:, vs], e[:, ks], tot[:, ks], ss_refs[d][h, sub])
                    dq, dk_, dv_, de, dtot, dst = vjp((do_refs[d][rows, vs], dst_ref[d, h]))
                    dst_ref[d, h] = dst
                    for group, val in ((dqs, dq), (dks, dk_), (dvs, dv_), (des, de), (dtots, dtot)):
                        group.append(val)
                cat = lambda parts: jnp.concatenate(parts, axis=-1)
                dg = _split_mm(ht_ref[d], cat(des)) + cat(dtots)
                draws, dpars = prep_vjp((cat(dqs), cat(dks), cat(dvs), dg))
                for r, val in zip(draw_refs[d], draws):
                    r[rows, :] = val.astype(r.dtype)
                for r, val in zip(dpar_refs[d], dpars):
                    r[...] += val

    ss_spec = lambda order: pl.BlockSpec((heads, S, dv, dk), lambda n: (0, order(n), 0, 0))
    row_out = lambda w, order: pl.BlockSpec((C * S, w), lambda n: (order(n), 0))
    return pl.pallas_call(
        body,
        out_shape=[jax.ShapeDtypeStruct((T, w), BF16) for d in range(2) for _, w in raws[d]]
        + [jax.ShapeDtypeStruct(p.shape, F32) for d in range(2) for p in params[d]],
        grid=(N,),
        in_specs=[_chunk_spec(s, w, orders[d]) for d in range(2) for s, w in raws[d]]
        + [_full_spec(p) for d in range(2) for p in params[d]]
        + [ss_spec(orders[0]), ss_spec(orders[1]), _chunk_spec(do, Wv, orders[0]), _chunk_spec(do, Wv, orders[1])]
        + _scan_const_specs(dk),
        out_specs=[row_out(w, orders[d]) for d in range(2) for _, w in raws[d]]
        + [_full_spec(p) for d in range(2) for p in params[d]],
        scratch_shapes=[pltpu.VMEM((2, heads, dv, dk), F32)],
        name=name, compiler_params=_cparams(("arbitrary",)))(
            *[s[0] for d in range(2) for s, _ in raws[d]], *[p for d in range(2) for p in params[d]],
            ss[0], ss[1], do[0], do[0], *_scan_const_args())


def final_call(x, g, target, T):
    tr = _row_tile(T)

    def tile(xv, gv, tv):
        y = _rms(xv, gv)
        err = (y - tv) ** 2
        return jnp.sum(jnp.sum(err, axis=-1, keepdims=True), axis=0, keepdims=True) * (0.5 / D_MODEL)

    def body(x_ref, g_ref, t_ref, loss_ref, dx_ref, dg_ref):
        i = pl.program_id(0)
        tv = t_ref[...]
        lv, vjp = jax.vjp(lambda a, b: tile(a, b, tv), x_ref[...], g_ref[...])
        dx, dg = vjp(jnp.ones((1, 1), F32))
        dx_ref[...] = dx

        @pl.when(i == 0)
        def _():
            loss_ref[...] = jnp.zeros_like(loss_ref)
            dg_ref[...] = jnp.zeros_like(dg_ref)

        loss_ref[...] += jnp.broadcast_to(lv, loss_ref.shape)
        dg_ref[...] += dg

    return pl.pallas_call(
        body,
        out_shape=[jax.ShapeDtypeStruct((8, 128), F32), jax.ShapeDtypeStruct((T, D_MODEL), F32),
                   jax.ShapeDtypeStruct((1, D_MODEL), F32)],
        grid=(T // tr,),
        in_specs=[pl.BlockSpec((tr, D_MODEL), lambda i: (i, 0)), pl.BlockSpec((1, D_MODEL), lambda i: (0, 0)),
                  pl.BlockSpec((tr, D_MODEL), lambda i: (i, 0))],
        out_specs=[pl.BlockSpec((8, 128), lambda i: (0, 0)), pl.BlockSpec((tr, D_MODEL), lambda i: (i, 0)),
                   pl.BlockSpec((1, D_MODEL), lambda i: (0, 0))],
        name="final_loss", compiler_params=_cparams(("arbitrary",)))(x, g, target)


def adamw_call(w, g, m, v):
    shape = w.shape
    c = shape[-1]
    r = int(np.prod(shape[:-1])) if len(shape) > 1 else 1
    tr = r if r <= 256 else 256
    assert r % tr == 0

    def body(w_ref, g_ref, m_ref, v_ref, d_ref, nm_ref, nv_ref):
        gv = g_ref[...]
        nm = ADAM_B1 * m_ref[...] + (1.0 - ADAM_B1) * gv
        nv = ADAM_B2 * v_ref[...] + (1.0 - ADAM_B2) * jnp.square(gv)
        m_hat = nm / (1.0 - ADAM_B1 ** ADAM_STEP)
        v_hat = nv / (1.0 - ADAM_B2 ** ADAM_STEP)
        d_ref[...] = -ADAM_LR * (m_hat / (jnp.sqrt(v_hat) + ADAM_EPS) + ADAM_WD * w_ref[...])
        nm_ref[...] = nm
        nv_ref[...] = nv

    spec = pl.BlockSpec((tr, c), lambda i: (i, 0))
    outs = pl.pallas_call(body, out_shape=[jax.ShapeDtypeStruct((r, c), F32)] * 3, grid=(r // tr,),
                          in_specs=[spec] * 4, out_specs=[spec] * 3, name="adamw",
                          compiler_params=_cparams(("arbitrary",)))(*(t.reshape(r, c) for t in (w, g, m, v)))
    return tuple(o.reshape(shape) for o in outs)


def adamw_halves(w, mine, other, m, v, c):
    L, R, C = w.shape
    by_cols = mine.shape[-1] != C
    if by_cols:
        tile, nbh = (R, C // 2), 1
        full_idx = lambda l, i: (l, 0, i)
    else:
        rh = R // 2
        tr = rh if rh <= 256 else rh // 2
        assert tr % 8 == 0
        tile, nbh = (tr, C), rh // tr
        full_idx = lambda l, i: (l, i, 0)

    def body(c_ref, w_ref, a_ref, b_ref, m_ref, v_ref, g_ref, d_ref, nm_ref, nv_ref):
        is_mine = (pl.program_id(1) // nbh) == c_ref[0]
        gv = jnp.where(is_mine, a_ref[...], b_ref[...])
        nm = ADAM_B1 * m_ref[...] + (1.0 - ADAM_B1) * gv
        nv = ADAM_B2 * v_ref[...] + (1.0 - ADAM_B2) * jnp.square(gv)
        m_hat = nm / (1.0 - ADAM_B1 ** ADAM_STEP)
        v_hat = nv / (1.0 - ADAM_B2 ** ADAM_STEP)
        g_ref[...] = gv
        d_ref[...] = -ADAM_LR * (m_hat / (jnp.sqrt(v_hat) + ADAM_EPS) + ADAM_WD * w_ref[...])
        nm_ref[...] = nm
        nv_ref[...] = nv

    full = pl.BlockSpec((None,) + tile, lambda l, i, c_ref: full_idx(l, i))
    half = pl.BlockSpec((None,) + tile, lambda l, i, c_ref: (l, i % nbh, 0))
    grid_spec = pltpu.PrefetchScalarGridSpec(num_scalar_prefetch=1, grid=(L, 2 * nbh),
                                             in_specs=[full, half, half, full, full], out_specs=[full] * 4)
    return pl.pallas_call(body, out_shape=[jax.ShapeDtypeStruct(w.shape, F32)] * 4, grid_spec=grid_spec,
                          name="adamw_halves", compiler_params=_cparams(("arbitrary", "arbitrary")))(c, w, mine, other, m, v)


def sum_devices(g64):
    def body(x_ref, o_ref):
        acc = x_ref[0:8, :]
        for d in range(1, 8):
            acc = acc + x_ref[8 * d:8 * d + 8, :]
        o_ref[...] = acc

    return pl.pallas_call(body, out_shape=jax.ShapeDtypeStruct((8, D_MODEL), F32), name="sum_devices")(g64)


def _half_tile(rh):
    if rh <= 512:
        return rh
    return next(rh // d for d in range(2, rh) if rh % d == 0 and (rh // d) % 16 == 0 and rh // d <= 512)


def _half_geometry(full_shape, half_shape):
    R, C = full_shape[-2:]
    if half_shape[-1] != C:
        return (R, C // 2), 1, lambda i, c: (0, c)
    tr = _half_tile(R // 2)
    nblk = (R // 2) // tr
    return (tr, C), nblk, lambda i, c: (i + c * nblk, 0)


def _work_items(counts):
    starts = [int(v) for v in np.cumsum([0] + list(counts[:-1]))]
    local = lambda a, s: jnp.clip(s - starts[a], 0, counts[a] - 1)
    return starts, int(sum(counts)), local


def add_sibling(gs, recvs, c, out_dtypes):
    n = len(gs)
    geo = [_half_geometry(g.shape, r.shape) for g, r in zip(gs, recvs)]
    counts = [4 * nblk for _, nblk, _ in geo]
    starts, total, local = _work_items(counts)

    def body(c_ref, *refs):
        s = pl.program_id(0)
        for a in range(n):
            g_ref, r_ref, o_ref = refs[a], refs[n + a], refs[2 * n + a]

            @pl.when((s >= starts[a]) & (s < starts[a] + counts[a]))
            def _():
                o_ref[...] = (g_ref[...] + r_ref[...]).astype(o_ref.dtype)

    def own_idx(s, c_ref, a):
        _, nblk, own = geo[a]
        k = local(a, s)
        return (k // nblk,) + own(k % nblk, c_ref[0])

    def half_idx(s, c_ref, a):
        k = local(a, s)
        return (k // geo[a][1], k % geo[a][1], 0)

    halves = [pl.BlockSpec((None,) + geo[a][0], functools.partial(half_idx, a=a)) for a in range(n)]
    grid_spec = pltpu.PrefetchScalarGridSpec(
        num_scalar_prefetch=1, grid=(total,),
        in_specs=[pl.BlockSpec((None,) + geo[a][0], functools.partial(own_idx, a=a)) for a in range(n)] + halves,
        out_specs=halves)
    return pl.pallas_call(body, out_shape=[jax.ShapeDtypeStruct(r.shape, dt) for r, dt in zip(recvs, out_dtypes)],
                          grid_spec=grid_spec, name="rs_add_sibling",
                          compiler_params=_cparams(("arbitrary",)))(c, *gs, *recvs)


def add_chips(gs, recvs, r3s, place):
    n = len(gs)
    geo = [_half_geometry(g.shape, r.shape) for g, r in zip(gs, recvs)]
    counts = [nblk for _, nblk, _ in geo]
    starts, total, local = _work_items(counts)

    def body(p_ref, *refs):
        s = pl.program_id(0)
        up = lambda r: r[...].astype(F32)
        for a in range(n):
            g_ref, s_ref, o_ref = refs[a], refs[n + a], refs[5 * n + a]
            a_ref, b_ref, c_ref = refs[2 * n + 3 * a:2 * n + 3 * a + 3]

            @pl.when((s >= starts[a]) & (s < starts[a] + counts[a]))
            def _():
                o_ref[...] = (((g_ref[...] + up(s_ref)) + up(a_ref)) + up(b_ref)) + up(c_ref)

    own_idx = lambda s, p_ref, a: (p_ref[0],) + geo[a][2](local(a, s), p_ref[1])
    sib_idx = lambda s, p_ref, a: (p_ref[0], local(a, s), 0)
    chip_idx = lambda s, p_ref, a, k: (k, local(a, s), 0)
    spec = lambda a, idx, **kw: pl.BlockSpec((None,) + geo[a][0], functools.partial(idx, a=a, **kw))
    grid_spec = pltpu.PrefetchScalarGridSpec(
        num_scalar_prefetch=1, grid=(total,),
        in_specs=[spec(a, own_idx) for a in range(n)] + [spec(a, sib_idx) for a in range(n)]
        + [spec(a, chip_idx, k=k) for a in range(n) for k in range(3)],
        out_specs=[pl.BlockSpec(geo[a][0], functools.partial(lambda s, p_ref, a: (local(a, s), 0), a=a))
                   for a in range(n)])
    return pl.pallas_call(body, out_shape=[jax.ShapeDtypeStruct(r.shape[1:], F32) for r in recvs],
                          grid_spec=grid_spec, name="rs_add_chips", compiler_params=_cparams(("arbitrary",)))(
                              place, *gs, *recvs, *[r for r3 in r3s for r in (r3, r3, r3)])


def _remote(src, dst, ssem, rsem, dev):
    return pltpu.make_async_remote_copy(src_ref=src, dst_ref=dst, send_sem=ssem, recv_sem=rsem,
                                        device_id=dev, device_id_type=pl.DeviceIdType.MESH)


def _mesh_places():
    x, y, c = lax.axis_index("x"), lax.axis_index("y"), lax.axis_index("c")
    chips = [(1 - x, y), (x, 1 - y), (1 - x, 1 - y)]
    return x, y, c, (x, y, 1 - c), chips


def _hbm_specs(n):
    return [pl.BlockSpec(memory_space=pltpu.HBM) for _ in range(n)]


def _gather_body(ins, outs, n_split, send_sems, recv_sems, handshake):
    x, y, c, sibling, chips = _mesh_places()
    mine = 2 * x + y
    if handshake:
        barrier = pltpu.get_barrier_semaphore()
        peers = [sibling] + [(*chip, c) for chip in chips]
        for peer in peers:
            pl.semaphore_signal(barrier, inc=1, device_id=peer, device_id_type=pl.DeviceIdType.MESH)
        pl.semaphore_wait(barrier, len(peers))

    def half(a, chip_idx, which):
        rh = ins[a].shape[0] // 2
        return outs[a].at[chip_idx, pl.ds(which * rh, rh), :]

    sent = []
    for a in range(len(ins)):
        for k, chip in enumerate(chips):
            if a < n_split:
                rh = ins[a].shape[0] // 2
                src, dst = ins[a].at[pl.ds(c * rh, rh), :], half(a, mine, c)
            else:
                src, dst = ins[a], outs[a].at[mine]
            sent.append(_remote(src, dst, send_sems.at[a, k], recv_sems.at[a, k], (*chip, c)))
    for cp in sent:
        cp.start()
    for a in range(len(ins)):
        for k, chip in enumerate(chips):
            j = 2 * chip[0] + chip[1]
            region = half(a, j, c) if a < n_split else outs[a].at[j]
            _remote(region, region, send_sems.at[a, k], recv_sems.at[a, k], (*chip, c)).wait_recv()
            if a < n_split:
                fwd = _remote(region, region, send_sems.at[a, 3 + k], recv_sems.at[a, 3 + k], sibling)
                fwd.start()
                sent.append(fwd)
    for a in range(n_split):
        for k, chip in enumerate(chips):
            region = half(a, 2 * chip[0] + chip[1], 1 - c)
            _remote(region, region, send_sems.at[a, 3 + k], recv_sems.at[a, 3 + k], sibling).wait_recv()
    for cp in sent:
        cp.wait_send()


def gather_weights(shards, small):
    arrs = list(shards) + [small]
    n = len(arrs)

    def body(*refs):
        _gather_body(refs[:n], refs[n:2 * n], n - 1, refs[2 * n], refs[2 * n + 1], handshake=False)

    return pl.pallas_call(
        body, out_shape=[jax.ShapeDtypeStruct((4,) + a.shape, a.dtype) for a in arrs],
        in_specs=_hbm_specs(n), out_specs=_hbm_specs(n),
        scratch_shapes=[pltpu.SemaphoreType.DMA((n, 6)), pltpu.SemaphoreType.DMA((n, 6))],
        name="gather_weights")(*arrs)


def gather_weights_async(shards):
    n = len(shards)

    def body(*refs):
        _gather_body(refs[:n], refs[n:2 * n], n, refs[2 * n], refs[2 * n + 1], handshake=True)

    return pl.kernel(
        body, out_type=[jax.ShapeDtypeStruct((4,) + a.shape, a.dtype) for a in shards],
        mesh=plsc.ScalarSubcoreMesh(axis_name="seq", num_cores=1),
        scratch_types=[pltpu.SemaphoreType.DMA((n, 6)), pltpu.SemaphoreType.DMA((n, 6))],
        compiler_params=pltpu.CompilerParams(collective_id=1), name="gather_weights_async")(*shards)


def _sequencer_call(name, body, out_type, sem_shape, collective_id, args):
    return pl.kernel(
        body, out_type=out_type, mesh=plsc.ScalarSubcoreMesh(axis_name="seq", num_cores=1),
        scratch_types=[pltpu.SemaphoreType.DMA(sem_shape), pltpu.SemaphoreType.DMA(sem_shape)],
        compiler_params=pltpu.CompilerParams(collective_id=collective_id), name=name)(*args)


def _handshake(peers):
    barrier = pltpu.get_barrier_semaphore()
    for peer in peers:
        pl.semaphore_signal(barrier, inc=1, device_id=peer, device_id_type=pl.DeviceIdType.MESH)
    pl.semaphore_wait(barrier, len(peers))


def exchange_siblings(name, srcs, axes, collective_id):
    n = len(srcs)

    def body(*refs):
        ins, outs = refs[:n], refs[n:2 * n]
        send_sems, recv_sems = refs[2 * n:]
        x, y, c, sibling, chips = _mesh_places()
        _handshake([sibling])
        cps = []
        for a in range(n):
            src = ins[a]
            if axes[a] is not None:
                half = src.shape[axes[a]] // 2
                theirs = pl.ds((1 - c) * half, half)
                src = src.at[:, theirs, :] if axes[a] == 1 else src.at[:, :, theirs]
            cps.append(_remote(src, outs[a], send_sems.at[a], recv_sems.at[a], sibling))
        for cp in cps:
            cp.start()
        for cp in cps:
            cp.wait()

    def shape(g, axis):
        return g.shape if axis is None else tuple(d // 2 if k == axis else d for k, d in enumerate(g.shape))

    return _sequencer_call(name, body, [jax.ShapeDtypeStruct(shape(g, ax), g.dtype) for g, ax in zip(srcs, axes)],
                           (n,), collective_id, srcs)


def exchange_chips(name, s1s, collective_id):
    n = len(s1s)

    def body(*refs):
        ins, outs = refs[:n], refs[n:2 * n]
        send_sems, recv_sems = refs[2 * n:]
        x, y, c, sibling, chips = _mesh_places()
        _handshake([(*chip, c) for chip in chips])
        cps = []
        for a in range(n):
            for k, chip in enumerate(chips):
                cps.append(_remote(ins[a].at[2 * chip[0] + chip[1]], outs[a].at[k], send_sems.at[a, k],
                                   recv_sems.at[a, k], (*chip, c)))
        for cp in cps:
            cp.start()
        for cp in cps:
            cp.wait()

    return _sequencer_call(name, body, [jax.ShapeDtypeStruct((3,) + s.shape[1:], s.dtype) for s in s1s], (n, 3),
                           collective_id, s1s)


def allgather_small(v):
    m_per = v.shape[0]

    def body(x_ref, out_ref, send_sems, recv_sems, local_sem):
        x, y, c, sibling, chips = _mesh_places()
        me = (x, y, c)

        def rows(px, py, pc):
            return out_ref.at[pl.ds((4 * px + 2 * py + pc) * m_per, m_per), :]

        def copy(k, block, to, src=None):
            return _remote(rows(*block) if src is None else src, rows(*block), send_sems.at[k], recv_sems.at[k], to)

        mine = pltpu.make_async_copy(x_ref, rows(*me), local_sem)
        mine.start()
        first = [copy(0, me, sibling, src=x_ref)]
        first += [copy(1 + j, me, (*chip, c), src=x_ref) for j, chip in enumerate(chips)]
        for cp in first:
            cp.start()
        passed = [copy(4 + j, (*chip, c), sibling) for j, chip in enumerate(chips)]
        for j, chip in enumerate(chips):
            copy(1 + j, (*chip, c), me).wait_recv()
            passed[j].start()
        copy(0, sibling, me).wait_recv()
        for j, chip in enumerate(chips):
            copy(4 + j, (*chip, 1 - c), me).wait_recv()
        for cp in first + passed:
            cp.wait_send()
        mine.wait()

    return pl.pallas_call(
        body, out_shape=jax.ShapeDtypeStruct((8 * m_per, v.shape[1]), v.dtype),
        in_specs=[pl.BlockSpec(memory_space=pltpu.VMEM)], out_specs=pl.BlockSpec(memory_space=pltpu.VMEM),
        scratch_shapes=[pltpu.SemaphoreType.DMA((7,)), pltpu.SemaphoreType.DMA((7,)), pltpu.SemaphoreType.DMA],
        name="allgather_small")(v)


def _lower_bounds(lb_param):
    lbs = jax.nn.softmax(lb_param.astype(F32), axis=0)
    return jnp.cumsum(lbs, axis=0) - lbs[0]


def _even_fwd(x, i, W, lower, kv, slopes, T):
    O = EVEN_OFF
    g = W["norm_even"][i].reshape(1, D_MODEL)
    h, p = norm_project("mm_in_e", x, g, W["w_in_e"][i])
    kvp = jnp.pad(p[:, O["kA"]:O["kA"] + 2 * W_KV_A], ((BLOCK, BLOCK), (0, 0)))
    sink = jnp.repeat(W["sink"][i], BLOCK).reshape(N_Q_A * BLOCK, 1)
    a = attn_fwd(p, O["qA"], kvp, sink, slopes, T)
    scan_raws = [[((p, O["qB"]), W_B), ((p, O[z]), W_B), ((p, O["iB"]), W_B)] for z in ("zf", "zb")]
    scan_pars = [[lower[i][0:1]], [lower[i][1:2]]]
    o_f, o_b, ss_f, ss_b = scan_fwd("scan_fwd_h", hgrn_prep, scan_raws, scan_pars, N_HEADS_B, HEAD_DIM_B, HEAD_DIM_B, T)
    mo = mem_fwd(p, O["qM"], kv, T)
    hg = W["hgrn_norm"][i].reshape(1, W_B)
    post_ins = [("row", a, 0, W_A), ("row", o_f, 0, W_B), ("row", o_b, 0, W_B), ("row", mo, 0, W_M),
                ("row", p, O["gA"], W_A), ("row", p, O["gB"], W_B), ("row", p, O["gM"], W_M), ("full", hg)]
    x_new = mix_project("even_out", even_post_tile, T, post_ins, W["w_out_e"][i], x)
    return x_new, dict(x=x, g=g, h=h, p=p, kvp=kvp, sink=sink, scan_raws=scan_raws, scan_pars=scan_pars,
                       ss=(ss_f, ss_b), post_ins=post_ins)


def _add2(a, b):
    return a.astype(F32) + b.astype(F32)


def _assemble_even(dqA, dgA, dqB_f, dqB_b, dzf, dzb, diB_f, diB_b, dgB, dqM, dgM, dkvA):
    parts = [dqA, dgA, _add2(dqB_f, dqB_b), dzf, dzb, _add2(diB_f, diB_b), dgB, dqM, dgM, dkvA]
    return (jnp.concatenate([t.astype(BF16) for t in parts], axis=-1),)


def _even_bwd(dxo, sv, i, W, kv, slopes, T, sync):
    O = EVEN_OFF
    p = sv["p"]
    da, dof, dmo, dgA, dgB, dgM, dhg, dwo = mix_project_bwd("even_out_bwd", even_post_tile, T, sv["post_ins"],
                                                            W["w_out_e"][i], dxo, skip=(2,), narrow=(4, 5, 6))
    dqA, dkvp, dsink = attn_bwd(p, O["qA"], sv["kvp"], sv["sink"], slopes, da, T)
    dkvA = dkvp[BLOCK:-BLOCK]
    dqB_f, dzf, diB_f, dqB_b, dzb, diB_b, dlow_f, dlow_b = scan_bwd(
        "scan_bwd_h", hgrn_prep, sv["scan_raws"], sv["scan_pars"], sv["ss"], (dof, 0), N_HEADS_B, HEAD_DIM_B, HEAD_DIM_B, T)
    dqB_f = sync(dqB_f)
    row = lambda arr, w: ("row", arr, 0, w)
    dlow = jnp.concatenate([dlow_f, dlow_b], axis=0)
    dqM, dkv = mem_bwd(p, O["qM"], kv, dmo, T)
    (dp,) = rows_call("even_dp", _assemble_even, T,
                      [row(dqA, W_A), row(dgA, W_A), row(dqB_f, W_B), row(dqB_b, W_B), row(dzf, W_B), row(dzb, W_B),
                       row(diB_f, W_B), row(diB_b, W_B), row(dgB, W_B), row(dqM, W_M), row(dgM, W_M),
                       row(dkvA, 2 * W_KV_A)],
                      [EVEN_IN], [BF16])
    dwi = matmul("mm_dwi_e", sv["h"], dp, "tn")
    dx, dg = norm_project_bwd("mm_in_e_bwd", dp, W["w_in_e"][i], sv["x"], sv["g"], dxo)
    return dx, dict(w_in=dwi, w_out=dwo, norm=dg[0], sink=dsink.reshape(N_Q_A), low=dlow, hg=dhg[0], kv=dkv)


def _pad_gate_up(w_up):
    z = jnp.zeros((2, 128, WK_C), F32)
    z = z.at[0, 0:GATE_RANK].set(w_up[0])
    return z.at[1, GATE_RANK:2 * GATE_RANK].set(w_up[1])


def _odd_fwd(x, i, W, kv, T):
    O = ODD_OFF
    g = W["norm_odd"][i].reshape(1, D_MODEL)
    h, p = norm_project("mm_in_o", x, g, W["w_in_o"][i])
    wup = _pad_gate_up(W["w_gate_up"][i])
    one_dir = [((p, O["qC"]), WK_C), ((p, O["kC"]), WK_C), ((p, O["vC"]), WV_C), ((p, O["rr"]), 128)]
    scan_raws = [one_dir, one_dir]
    scan_pars = [[wup[d], W["b_gate"][i][d:d + 1]] for d in range(2)]
    o_f, o_b, ss_f, ss_b = scan_fwd("scan_fwd_g", gla_prep, scan_raws, scan_pars, N_HEADS_C, DK_C, DV_C, T)
    mo = mem_fwd(p, O["qM"], kv, T)
    gg = W["gla_norm"][i].reshape(1, WV_C)
    post_ins = [("row", o_f, 0, WV_C), ("row", o_b, 0, WV_C), ("row", mo, 0, W_M),
                ("row", p, O["gC"], WV_C), ("row", p, O["gM"], W_M), ("full", gg)]
    x_new = mix_project("odd_out", odd_post_tile, T, post_ins, W["w_out_o"][i], x)
    return x_new, dict(x=x, g=g, h=h, p=p, scan_raws=scan_raws, scan_pars=scan_pars, ss=(ss_f, ss_b),
                       post_ins=post_ins)


def _assemble_odd(dq0, dq1, dk0, dk1, dv0, dv1, dgC, dqM, dgM, dr0, dr1):
    parts = [_add2(dq0, dq1), _add2(dk0, dk1), _add2(dv0, dv1), dgC, dqM, dgM, _add2(dr0, dr1)]
    return (jnp.concatenate([t.astype(BF16) for t in parts], axis=-1),)


def _odd_bwd(dxo, sv, i, W, kv, T, sync):
    O = ODD_OFF
    p = sv["p"]
    dof, dmo, dgC, dgM, dgg, dwo = mix_project_bwd("odd_out_bwd", odd_post_tile, T, sv["post_ins"], W["w_out_o"][i],
                                                   dxo, skip=(1,), narrow=(3, 4))
    dqf, dkf, dvf, dr_f, dqb, dkb, dvb, dr_b, dwup_f, dbg_f, dwup_b, dbg_b = scan_bwd(
        "scan_bwd_g", gla_prep, sv["scan_raws"], sv["scan_pars"], sv["ss"], (dof, 0), N_HEADS_C, DK_C, DV_C, T)
    dqf = sync(dqf)
    row = lambda arr, w: ("row", arr, 0, w)
    dqM, dkv = mem_bwd(p, O["qM"], kv, dmo, T)
    (dp,) = rows_call("odd_dp", _assemble_odd, T,
                      [row(dqf, WK_C), row(dqb, WK_C), row(dkf, WK_C), row(dkb, WK_C), row(dvf, WV_C), row(dvb, WV_C),
                       row(dgC, WV_C), row(dqM, W_M), row(dgM, W_M), row(dr_f, 128), row(dr_b, 128)],
                      [ODD_PAD], [BF16])
    dwi = matmul("mm_dwi_o", sv["h"], dp, "tn")
    dx, dg = norm_project_bwd("mm_in_o_bwd", dp, W["w_in_o"][i], sv["x"], sv["g"], dxo)
    dw_up = jnp.stack([dwup_f[0:GATE_RANK], dwup_b[GATE_RANK:2 * GATE_RANK]])
    dbg = jnp.concatenate([dbg_f, dbg_b], axis=0)
    return dx, dict(w_in=dwi, w_out=dwo, norm=dg[0], w_up=dw_up, b_gate=dbg, gg=dgg[0], kv=dkv)


def local_step(x, mem, target, W, later=None, on_layer_grads=None, sync=lambda a: a):
    T = x.shape[0]
    slopes = jnp.repeat(2.0 ** (-8.0 * jnp.arange(1, N_Q_A + 1, dtype=F32) / N_Q_A), BLOCK).reshape(N_Q_A * BLOCK, 1)
    lower, lower_vjp = jax.vjp(_lower_bounds, W["lb_param"])
    mem_g = W["mem_norm"].reshape(1, D_MODEL)
    (mem_n,) = rows_call("mem_rms_fwd", rms_tile, N_MEM, [("row", mem, 0, D_MODEL), ("full", mem_g)], [D_MODEL], [BF16])
    kvs, saved = [], []
    for l in range(DEPTH):
        if l == 1 and later is not None:
            x, W = later(x, W)
        kvs.append(matmul("mm_kv", mem_n, W["w_kv"][l], "nn"))
        if l % 2 == 0:
            x, sv = _even_fwd(x, l // 2, W, lower, kvs[l], slopes, T)
        else:
            x, sv = _odd_fwd(x, l // 2, W, kvs[l], T)
        saved.append(sv)
    loss, dx, dgf = final_call(x, W["final_norm"].reshape(1, D_MODEL), target, T)
    per = [None] * DEPTH
    dmem_n = None
    for l in reversed(range(DEPTH)):
        if l % 2 == 0:
            dx, per[l] = _even_bwd(dx, saved[l], l // 2, W, kvs[l], slopes, T, sync)
        else:
            dx, per[l] = _odd_bwd(dx, saved[l], l // 2, W, kvs[l], T, sync)
        per[l]["w_kv"] = matmul("mm_dwkv", mem_n, per[l]["kv"], "tn")
        dmem_n = matmul("mm_dmem", per[l]["kv"], W["w_kv"][l], "nt", add=dmem_n)
        if on_layer_grads is not None:
            dx = on_layer_grads(l, dx, per[l])
    dw_kv = [per[l]["w_kv"] for l in range(DEPTH)]
    (dmem_norm,) = rows_vjp_call("mem_rms_bwd", rms_tile, N_MEM, [("row", mem, 0, D_MODEL), ("full", mem_g)],
                                 [[("row", dmem_n, 0, D_MODEL)]], skip=(0,))
    ev, od = (per[0], per[2]), (per[1], per[3])
    (d_lb,) = lower_vjp(jnp.stack([e["low"] for e in ev]))
    grads = dict(
        w_in_e=jnp.stack([e["w_in"] for e in ev]), w_in_o=jnp.stack([o["w_in"] for o in od]),
        w_out_e=jnp.stack([e["w_out"] for e in ev]), w_out_o=jnp.stack([o["w_out"] for o in od]),
        w_kv=jnp.stack(dw_kv), norm_even=jnp.stack([e["norm"] for e in ev]), sink=jnp.stack([e["sink"] for e in ev]),
        lb_param=d_lb, hgrn_norm=jnp.stack([e["hg"] for e in ev]), norm_odd=jnp.stack([o["norm"] for o in od]),
        w_gate_up=jnp.stack([o["w_up"] for o in od]), b_gate=jnp.stack([o["b_gate"] for o in od]),
        gla_norm=jnp.stack([o["gg"] for o in od]), mem_norm=dmem_norm[0], final_norm=dgf[0])
    return loss, dx, grads


SMALL_SPECS = (("lb_param", (2, 2, 128)), ("norm_odd", (2, 256)), ("w_gate_up", (2, 2, 16, 128)),
               ("b_gate", (2, 2, 128)), ("gla_norm", (2, 256)))
SMALL_ROWS = 80


def _pack_small_local(d):
    return jnp.concatenate([d[n].reshape(-1) for n, _ in SMALL_SPECS]).reshape(SMALL_ROWS, 128)


def _unpack_small_local(b):
    flat, out, o = b.reshape(-1), {}, 0
    for n, shp in SMALL_SPECS:
        sz = int(np.prod(shp))
        out[n] = flat[o:o + sz].reshape(shp)
        o += sz
    return out


def _unpack_small_full(g4):
    per = [_unpack_small_local(g4[j]) for j in range(4)]
    return {n: jnp.concatenate([per[j][n] for j in range(4)], axis=-1) for n, _ in SMALL_SPECS}


def _pack_small_blocks(full):
    blocks = []
    for j in range(4):
        blocks.append(_pack_small_local({n: full[n][..., j * shp[-1]:(j + 1) * shp[-1]] for n, shp in SMALL_SPECS}))
    return jnp.stack(blocks)


def _cols(t, order, off, widths):
    return [t[..., off[n]:off[n] + widths[n]] for n in order]


EVEN_REF_ORDER = ("qA", "kA", "vA", "gA", "qB", "zf", "zb", "iB", "gB", "qM", "gM")
ODD_REF_ORDER = ("qC", "kC", "vC", "gC", "rr", "qM", "gM")


def _layer_weights(l, g_in, g_out, g_kv):
    t = g_in.transpose(1, 0, 2).reshape(D_MODEL, -1)
    if l % 2 == 0:
        w_in = jnp.concatenate(_cols(t, EVEN_ORDER, EVEN_REF_OFF, EVEN_W), axis=-1)
    else:
        w_in = jnp.concatenate(_cols(t, ODD_ORDER, ODD_REF_OFF, ODD_W) + [jnp.zeros((D_MODEL, ODD_PAD - ODD_IN), BF16)],
                               axis=-1)
    return w_in, g_out.reshape(MIX, D_MODEL), g_kv.reshape(D_MODEL, 2 * W_M)


def _layer_grad_blocks(l, gl):
    if l % 2 == 0:
        t = jnp.concatenate(_cols(gl["w_in"], EVEN_REF_ORDER, EVEN_OFF, EVEN_W), axis=-1)
    else:
        t = jnp.concatenate(_cols(gl["w_in"], ODD_REF_ORDER, ODD_OFF, ODD_W), axis=-1)
    b_in = t.reshape(D_MODEL, 4, -1).transpose(1, 2, 0)
    return [b_in, gl["w_out"].reshape(4, MIX // 4, D_MODEL), gl["w_kv"].reshape(4, D_MODEL // 4, 2 * W_M)]


WEIGHT_NAMES = ("norm_even", "w_in_even", "sink", "lb_param", "hgrn_norm", "w_out_even", "norm_odd", "w_in_odd",
                "w_gate_up", "b_gate", "gla_norm", "w_out_odd", "mem_norm", "w_mem_kv", "final_norm")


def kernel(x, mem, norm_even, w_in_even, sink, lb_param, hgrn_norm, w_out_even, norm_odd, w_in_odd, w_gate_up, b_gate, gla_norm, w_out_odd, mem_norm, w_mem_kv, final_norm, loss_target, m_norm_even, m_w_in_even, m_sink, m_lb_param, m_hgrn_norm, m_w_out_even, m_norm_odd, m_w_in_odd, m_w_gate_up, m_b_gate, m_gla_norm, m_w_out_odd, m_mem_norm, m_w_mem_kv, m_final_norm, v_norm_even, v_w_in_even, v_sink, v_lb_param, v_hgrn_norm, v_w_out_even, v_norm_odd, v_w_in_odd, v_w_gate_up, v_b_gate, v_gla_norm, v_w_out_odd, v_mem_norm, v_w_mem_kv, v_final_norm):
    w = dict(zip(WEIGHT_NAMES, (norm_even, w_in_even, sink, lb_param, hgrn_norm, w_out_even, norm_odd, w_in_odd,
                                w_gate_up, b_gate, gla_norm, w_out_odd, mem_norm, w_mem_kv, final_norm)))
    m = dict(zip(WEIGHT_NAMES, (m_norm_even, m_w_in_even, m_sink, m_lb_param, m_hgrn_norm, m_w_out_even, m_norm_odd,
                                m_w_in_odd, m_w_gate_up, m_b_gate, m_gla_norm, m_w_out_odd, m_mem_norm, m_w_mem_kv,
                                m_final_norm)))
    v = dict(zip(WEIGHT_NAMES, (v_norm_even, v_w_in_even, v_sink, v_lb_param, v_hgrn_norm, v_w_out_even, v_norm_odd,
                                v_w_in_odd, v_w_gate_up, v_b_gate, v_gla_norm, v_w_out_odd, v_mem_norm, v_w_mem_kv,
                                v_final_norm)))
    ci = lax.axis_index("c").astype(jnp.int32).reshape(1)
    chip = (2 * lax.axis_index("x") + lax.axis_index("y")).astype(jnp.int32).reshape(1)

    shards = []
    for l in range(DEPTH):
        names = ("w_in_even", "w_out_even") if l % 2 == 0 else ("w_in_odd", "w_out_odd")
        shards.append([w[names[0]][l // 2].astype(BF16), w[names[1]][l // 2].astype(BF16), w_mem_kv[l].astype(BF16)])
    small = _pack_small_local(w)
    own = lambda g, s: lax.dynamic_update_slice(g, s[None], (chip[0], 0, 0))
    first = [own(g, s) for g, s in zip(gather_weights(shards[0], small), shards[0] + [small])]
    later_shards = shards[1] + shards[2] + shards[3]
    later_raw = gather_weights_async(later_shards)
    w0 = _layer_weights(0, *first[0:3])
    W = dict(w_in_e=[w0[0]], w_out_e=[w0[1]], w_kv=[w0[2]])
    W.update(_unpack_small_full(first[3]))
    W.update({n: w[n] for n in ("norm_even", "sink", "hgrn_norm", "mem_norm", "final_norm")})

    def later(x1, W):
        x1, raw = lax.optimization_barrier((x1, list(later_raw)))
        g = [own(a, s) for a, s in zip(raw, later_shards)]
        w1, w2, w3 = (_layer_weights(l, *g[3 * (l - 1):3 * l]) for l in (1, 2, 3))
        W = dict(W)
        W.update(w_in_e=[w0[0], w2[0]], w_in_o=[w1[0], w3[0]], w_out_e=[w0[1], w2[1]], w_out_o=[w1[1], w3[1]],
                 w_kv=[w0[2], w1[2], w2[2], w3[2]])
        return x1, W

    place = jnp.concatenate([chip, ci])

    def start(tag, blocks, wire):
        axes = [2 if b.shape[1] == ODD_IN // 4 else 1 for b in blocks]
        return dict(tag=tag, blocks=blocks, wire=wire, step=0,
                    recv=exchange_siblings(f"rs_siblings_{tag}", blocks, axes, 2))

    def advance(p):
        if p["step"] == 0:
            sums = add_sibling(p["blocks"], p["recv"], ci, p["wire"])
            p["recv3"] = exchange_chips(f"rs_chips_{p['tag']}", sums, 3)
        else:
            p["mine"] = add_chips(p["blocks"], p["recv"], p["recv3"], place)
            p["other"] = exchange_siblings(f"rs_final_{p['tag']}", p["mine"], [None] * len(p["mine"]), 4)
        p["step"] += 1

    pipes, first_layer = [], {}

    def sync(a):
        for p in pipes:
            if p["step"] < 3:
                key = ("recv", "recv3", "other")[p["step"]]
                a, arrived = lax.optimization_barrier((a, list(p[key])))
                p[key] = arrived
                if p["step"] < 2:
                    advance(p)
                else:
                    p["step"] = 3
        return a

    def on_layer_grads(l, dx, gl):
        dx = sync(dx)
        if l == 0:
            first_layer.update(gl)
        else:
            pipes.append(start(f"l{l}", _layer_grad_blocks(l, gl), [BF16] * 3))
        return dx

    loss_tile, dx, grads = local_step(x[0], mem[0], loss_target[0], W, later, on_layer_grads, sync)
    last = start("l0", _layer_grad_blocks(0, first_layer) + [_pack_small_blocks(grads)], [BF16] * 3 + [F32])
    for p in pipes + [last]:
        while p["step"] < (1 if p is last else 2):
            advance(p)
    by_layer = {int(p["tag"][1:]): p for p in pipes + [last]}
    halves = lambda layers, k: (jnp.stack([by_layer[l]["mine"][k] for l in layers]),
                                jnp.stack([by_layer[l]["other"][k] for l in layers]))
    gl, upd = {}, {}

    pack = jnp.zeros((8, D_MODEL), F32)
    pack = pack.at[0:2].set(grads["norm_even"]).at[2].set(grads["hgrn_norm"].reshape(-1))
    pack = pack.at[3].set(grads["mem_norm"]).at[4].set(grads["final_norm"])
    pack = pack.at[5, 0:16].set(grads["sink"].reshape(-1)).at[5, 16].set(loss_tile[0, 0])
    tot = sum_devices(allgather_small(pack))
    gl.update(norm_even=tot[0:2], hgrn_norm=tot[2].reshape(2, W_B), mem_norm=tot[3], final_norm=tot[4],
              sink=tot[5, 0:16].reshape(2, N_Q_A))
    loss = tot[5, 16]
    for n in ("norm_even", "hgrn_norm", "mem_norm", "final_norm", "sink"):
        upd[n] = adamw_call(w[n], gl[n], m[n], v[n])
    tr_ = lambda a: jnp.swapaxes(a, 1, 2)
    gl["w_in_odd"], *upd["w_in_odd"] = [tr_(o) for o in adamw_halves(
        tr_(w["w_in_odd"]), *halves((1, 3), 0), tr_(m["w_in_odd"]), tr_(v["w_in_odd"]), ci)]
    gl["w_out_odd"], *upd["w_out_odd"] = adamw_halves(w["w_out_odd"], *halves((1, 3), 1), m["w_out_odd"],
                                                      v["w_out_odd"], ci)
    early = [upd[n] for n in sorted(upd)] + [gl["w_in_odd"], gl["w_out_odd"]]
    last["recv3"], early = lax.optimization_barrier((list(last["recv3"]), early))
    for n, res in zip(sorted(upd), early):
        upd[n] = res
    gl["w_in_odd"], gl["w_out_odd"] = early[-2:]
    advance(last)

    big = dict(w_in_even=halves((0, 2), 0), w_out_even=halves((0, 2), 1), w_mem_kv=halves((0, 1, 2, 3), 2))
    s_mine, s_other = last["mine"][3], last["other"][3]
    g_small = jnp.where(ci[0] == 0, jnp.concatenate([s_mine, s_other]), jnp.concatenate([s_other, s_mine]))
    gl.update(_unpack_small_local(g_small))
    for n in WEIGHT_NAMES:
        if n == "w_in_even":
            gl[n], *upd[n] = [tr_(o) for o in adamw_halves(tr_(w[n]), *big[n], tr_(m[n]), tr_(v[n]), ci)]
        elif n in big:
            gl[n], *upd[n] = adamw_halves(w[n], *big[n], m[n], v[n], ci)
        elif n not in upd:
            upd[n] = adamw_call(w[n], gl[n], m[n], v[n])
    return (loss, dx[None], *[gl[n] for n in WEIGHT_NAMES], *[upd[n][0] for n in WEIGHT_NAMES],
            *[upd[n][1] for n in WEIGHT_NAMES], *[upd[n][2] for n in WEIGHT_NAMES])
```

```python
import functools

import numpy as np
import jax
import jax.numpy as jnp
from jax import lax
from jax.experimental import pallas as pl
from jax.experimental.pallas import tpu as pltpu
from jax.experimental.pallas import tpu_sc as plsc

F32 = jnp.float32
BF16 = jnp.bfloat16

D_MODEL = 1024
DEPTH = 4
N_Q_A, N_KV_A, HEAD_DIM_A = 8, 2, 64
W_A, W_KV_A = 512, 128
WINDOW = 128
BLOCK = 128
N_HEADS_B, HEAD_DIM_B, W_B = 4, 128, 512
N_HEADS_C, DK_C, DV_C, WK_C, WV_C = 4, 128, 256, 512, 1024
GATE_RANK = 16
GATE_TEMP = 16.0
N_MEM, N_HEADS_M, HEAD_DIM_M, W_M = 256, 4, 128, 512
EPS = 1e-6
MASK_VALUE = -1e30
MIN_GATE = 1e-30
EVEN_IN, ODD_IN = 4864, 4128
ODD_PAD = 4224
MIX = 1536
ADAM_LR, ADAM_B1, ADAM_B2, ADAM_EPS, ADAM_WD, ADAM_STEP = 0.001, 0.9, 0.999, 1e-08, 0.01, 10

SCAN_CHUNK = 128
SCAN_SUB = 2
SCAN_LEVELS = 7
VMEM_LIMIT = 56 * 1024 * 1024

EVEN_REF_OFF = dict(qA=0, kA=512, vA=640, gA=768, qB=1280, zf=1792, zb=2304, iB=2816, gB=3328, qM=3840, gM=4352)
EVEN_W = dict(qA=512, kA=128, vA=128, gA=512, qB=512, zf=512, zb=512, iB=512, gB=512, qM=512, gM=512)
EVEN_ORDER = ("qA", "gA", "qB", "zf", "zb", "iB", "gB", "qM", "gM", "kA", "vA")
ODD_REF_OFF = dict(qC=0, kC=512, vC=1024, gC=2048, rr=3072, qM=3104, gM=3616)
ODD_W = dict(qC=512, kC=512, vC=1024, gC=1024, rr=32, qM=512, gM=512)
ODD_ORDER = ("qC", "kC", "vC", "gC", "qM", "gM", "rr")


def _offsets(order, widths):
    off, o = {}, 0
    for n in order:
        off[n] = o
        o += widths[n]
    return off


EVEN_OFF = _offsets(EVEN_ORDER, EVEN_W)
ODD_OFF = _offsets(ODD_ORDER, ODD_W)


def _dg(a, b, ca, cb):
    return lax.dot_general(a.astype(BF16), b.astype(BF16), (((ca,), (cb,)), ((), ())),
                           preferred_element_type=F32)


def dot_nn(a, b):
    return _dg(a, b, 1, 0)


def dot_nt(a, b):
    return _dg(a, b, 1, 1)


def dot_tn(a, b):
    return _dg(a, b, 0, 0)


@jax.custom_vjp
def bdot(a, b):
    return dot_nn(a, b)


bdot.defvjp(lambda a, b: (dot_nn(a, b), (a, b)),
            lambda r, g: (dot_nt(g, r[1]), dot_tn(r[0], g)))


@jax.custom_vjp
def bdot_t(a, b):
    return dot_nt(a, b)


bdot_t.defvjp(lambda a, b: (dot_nt(a, b), (a, b)),
              lambda r, g: (dot_nn(g, r[1]), dot_tn(g, r[0])))


@jax.custom_vjp
def bdot_tn(a, b):
    return dot_tn(a, b)


bdot_tn.defvjp(lambda a, b: (dot_tn(a, b), (a, b)),
               lambda r, g: (dot_nt(r[1], g), dot_nn(r[0], g)))


def _split_mm(h, x):
    hi = x.astype(BF16)
    lo = (x - hi.astype(F32)).astype(BF16)
    return (lax.dot_general(h, hi, (((1,), (0,)), ((), ())), preferred_element_type=F32)
            + lax.dot_general(h, lo, (((1,), (0,)), ((), ())), preferred_element_type=F32))


def _sigmoid(z):
    return 1.0 / (1.0 + jnp.exp(-z))


def _silu(z):
    return z * _sigmoid(z)


def _log_sigmoid(z):
    return jnp.minimum(z, 0.0) - jnp.log(1.0 + jnp.exp(-jnp.abs(z)))


def _rms(x, g):
    return x * lax.rsqrt(jnp.mean(x * x, axis=-1, keepdims=True) + EPS) * g


def rms_tile(x, g):
    return (_rms(x, g),)


@functools.partial(jax.custom_vjp, nondiff_argnums=(1, 2))
def split(x, n, axis):
    w = x.shape[axis] // n
    return tuple(lax.slice_in_dim(x, h * w, (h + 1) * w, axis=axis) for h in range(n))


split.defvjp(lambda x, n, axis: (split(x, n, axis), None),
             lambda n, axis, _, cts: (jnp.concatenate(cts, axis=axis),))


def _group_rms(o, g, heads):
    return jnp.concatenate([_rms(oh, gh) for oh, gh in zip(split(o, heads, 1), split(g, heads, 1))], axis=-1)


def even_post_tile(a, o2f, o2b, mo, gA, gB, gM, hg):
    y = _group_rms(o2f + o2b, hg, N_HEADS_B)
    return (jnp.concatenate([a * _silu(gA), y * _silu(gB), mo * _silu(gM)], axis=-1),)


def odd_post_tile(o2f, o2b, mo, gC, gM, gg):
    y = _group_rms(o2f + o2b, gg, N_HEADS_C)
    return (jnp.concatenate([y * _silu(gC), mo * _silu(gM)], axis=-1),)


def hgrn_prep(raw, par):
    qB, z, iB = raw
    (lb,) = par
    f = lb + (1.0 - lb) * _sigmoid(z)
    return _silu(qB), (1.0 - lb) * _sigmoid(-z), iB, jnp.log(jnp.maximum(f, MIN_GATE))


def gla_prep(raw, par):
    qC, kC, vC, r128 = raw
    wup, bg = par
    return qC * (DK_C ** -0.5), kC, vC, _log_sigmoid(bdot(r128, wup) + bg) / GATE_TEMP


def mem_tile(q, k, v):
    s = bdot_t(q, k) * (HEAD_DIM_M ** -0.5)
    m = lax.stop_gradient(jnp.max(s, axis=-1, keepdims=True))
    p = jnp.exp(s - m)
    p = p / jnp.sum(p, axis=-1, keepdims=True)
    return (bdot(p, v),)


ATTN_GROUP = N_Q_A // N_KV_A


def attn_block(q, ks, vs, sink, slope, c, seq):
    rows = ATTN_GROUP * BLOCK
    i = lax.broadcasted_iota(jnp.int32, (rows, 3 * BLOCK), 0) % BLOCK
    j = lax.broadcasted_iota(jnp.int32, (rows, 3 * BLOCK), 1)
    dist = jnp.abs(i - j + BLOCK).astype(F32)
    kpos = (c - 1) * BLOCK + j
    valid = (dist <= WINDOW) & (kpos >= 0) & (kpos < seq)
    s = bdot_t(q, ks) * (HEAD_DIM_A ** -0.5)
    s = jnp.where(valid, s - slope * dist, MASK_VALUE)
    m = lax.stop_gradient(jnp.maximum(jnp.max(s, axis=-1, keepdims=True), sink))
    p = jnp.where(valid, jnp.exp(s - m), 0.0)
    denom = jnp.sum(p, axis=-1, keepdims=True) + jnp.exp(sink - m)
    return bdot(p, vs) / denom


def scan_chunk(q, k, v, e, tot, st, qm, pm):
    C = SCAN_CHUNK
    e = split(e, 2 + SCAN_LEVELS, 0)
    qe = q * jnp.exp(e[0])
    kd = k * jnp.exp(e[1])
    r = lax.broadcasted_iota(jnp.int32, (C, C), 0)
    s = lax.broadcasted_iota(jnp.int32, (C, C), 1)
    a = jnp.where(r == s, jnp.sum(q * k, axis=-1, keepdims=True), 0.0)
    for l in range(SCAN_LEVELS):
        u = jnp.where(qm[l * C:(l + 1) * C] != 0.0, q, k) * jnp.exp(e[2 + l])
        a = a + bdot_t(u, u) * pm[l * C:(l + 1) * C]
    o = bdot_t(qe, st) + bdot(a, v)
    st_new = st * jnp.exp(tot) + bdot_tn(v, kd)
    return o, st_new


def _scan_consts():
    C, L = SCAN_CHUNK, SCAN_LEVELS
    t = np.arange(C)[:, None]
    r = np.arange(C)[None, :]
    blocks = [(r <= t), (r > t)]
    qms, pms = [], []
    for l in range(1, L + 1):
        m = C >> l
        upper_t = (t % (2 * m)) >= m
        upper_r = (r % (2 * m)) >= m
        same_half = (t // m) == (r // m)
        blocks.append(same_half & np.where(upper_t, r <= t, r > t))
        qms.append(np.broadcast_to(upper_t, (C, C)))
        pms.append(((t // (2 * m)) == (r // (2 * m))) & upper_t & ~upper_r)
    hf = np.concatenate(blocks, axis=0).astype(np.float32)
    flip = lambda mat: mat.reshape(-1, C, C)[:, ::-1, ::-1].reshape(-1, C)
    qmf = np.concatenate(qms, axis=0).astype(np.float32)
    pmf = np.concatenate(pms, axis=0).astype(np.float32)
    h = np.stack([hf, flip(hf)])
    ht = np.stack([h[0].T, h[1].T])
    qm = np.stack([qmf, 1.0 - qmf])
    pm = np.stack([pmf, flip(pmf)])
    return h, ht, qm, pm


def _cparams(sem):
    return pltpu.CompilerParams(dimension_semantics=sem, vmem_limit_bytes=VMEM_LIMIT)


def _row_tile(T):
    return min(T, 512)


def _in_spec(spec, tr):
    kind = spec[0]
    if kind == "row":
        _, arr, off, w = spec
        assert off % w == 0
        return arr, pl.BlockSpec((tr, w), functools.partial(lambda i, b: (i, b), b=off // w))
    if kind == "row3":
        _, arr, d, off, w = spec
        assert off % w == 0
        return arr, pl.BlockSpec((None, tr, w), functools.partial(lambda i, d, b: (d, i, b), d=d, b=off // w))
    _, arr = spec
    return arr, pl.BlockSpec(arr.shape, functools.partial(lambda i, n: (0,) * n, n=arr.ndim))


def rows_call(name, tile_fn, T, ins, out_widths, out_dtypes=None, stacks=None):
    tr = _row_tile(T)
    n_in = len(ins)
    out_dtypes = out_dtypes or [F32] * len(out_widths)
    stacks = stacks or [(k,) for k in range(len(out_widths))]

    def body(*refs):
        vals = [r[...] for r in refs[:n_in]]
        outs = tile_fn(*vals)
        for r, members in zip(refs[n_in:], stacks):
            if len(members) == 1:
                r[...] = outs[members[0]].astype(r.dtype)
            else:
                for d, k in enumerate(members):
                    r[d] = outs[k].astype(r.dtype)

    in_specs, args = [], []
    for spec in ins:
        arr, bs = _in_spec(spec, tr)
        args.append(arr)
        in_specs.append(bs)
    out_specs, out_shape = [], []
    for w, dt, members in zip(out_widths, out_dtypes, stacks):
        n = len(members)
        if n == 1:
            out_specs.append(pl.BlockSpec((tr, w), lambda i: (i, 0)))
            out_shape.append(jax.ShapeDtypeStruct((T, w), dt))
        else:
            out_specs.append(pl.BlockSpec((n, tr, w), lambda i: (0, i, 0)))
            out_shape.append(jax.ShapeDtypeStruct((n, T, w), dt))
    return pl.pallas_call(body, out_shape=out_shape, grid=(T // tr,), in_specs=in_specs, out_specs=out_specs,
                          name=name, compiler_params=_cparams(("arbitrary",)))(*args)


def rows_vjp_call(name, tile_fn, T, ins, cts, skip=(), narrow=()):
    tr = _row_tile(T)
    n_in = len(ins)
    n_ct = [len(c) for c in cts]
    want = [k for k in range(n_in) if k not in skip]

    def body(*refs):
        i = pl.program_id(0)
        vals = [r[...] for r in refs[:n_in]]
        ct, pos = [], n_in
        for n in n_ct:
            acc = refs[pos][...]
            for r in refs[pos + 1:pos + n]:
                acc = acc + r[...]
            ct.append(acc)
            pos += n
        _, vjp = jax.vjp(tile_fn, *vals)
        grads = vjp(tuple(ct))
        for r, k in zip(refs[pos:], want):
            if ins[k][0] == "full":
                @pl.when(i == 0)
                def _():
                    r[...] = jnp.zeros_like(r)
                r[...] += grads[k]
            else:
                r[...] = grads[k].astype(r.dtype)

    in_specs, args = [], []
    for spec in list(ins) + [s for c in cts for s in c]:
        arr, bs = _in_spec(spec, tr)
        args.append(arr)
        in_specs.append(bs)
    out_specs, out_shape = [], []
    for k in want:
        if ins[k][0] == "full":
            arr = ins[k][1]
            out_specs.append(pl.BlockSpec(arr.shape, functools.partial(lambda i, n: (0,) * n, n=arr.ndim)))
            out_shape.append(jax.ShapeDtypeStruct(arr.shape, F32))
        else:
            w = ins[k][-1]
            out_specs.append(pl.BlockSpec((tr, w), lambda i: (i, 0)))
            out_shape.append(jax.ShapeDtypeStruct((T, w), BF16 if k in narrow else F32))
    return pl.pallas_call(body, out_shape=out_shape, grid=(T // tr,), in_specs=in_specs, out_specs=out_specs,
                          name=name, compiler_params=_cparams(("arbitrary",)))(*args)


def matmul(name, a, b, mode, add=None, out_dtype=F32):
    if mode == "tn":
        K, M = a.shape
        N = b.shape[1]
        tm = M if M <= 1536 else 512
        tn = N if N <= 1280 else (N // 2 if (N // 2) % 128 == 0 else N)
        tk = min(K, 512)
        grid = (M // tm, N // tn, K // tk)

        def body(a_ref, b_ref, o_ref):
            @pl.when(pl.program_id(2) == 0)
            def _():
                o_ref[...] = jnp.zeros_like(o_ref)
            o_ref[...] += dot_tn(a_ref[...], b_ref[...])

        return pl.pallas_call(
            body, out_shape=jax.ShapeDtypeStruct((M, N), F32), grid=grid,
            in_specs=[pl.BlockSpec((tk, tm), lambda i, j, k: (k, i)), pl.BlockSpec((tk, tn), lambda i, j, k: (k, j))],
            out_specs=pl.BlockSpec((tm, tn), lambda i, j, k: (i, j)), name=name,
            compiler_params=_cparams(("arbitrary", "arbitrary", "arbitrary")))(a, b)

    M, K = a.shape
    N = b.shape[1] if mode == "nn" else b.shape[0]
    tm = min(M, 512)
    tn = N if N <= 1536 else (N // 2 if (N // 2) % 128 == 0 else (N // 3 if (N // 3) % 128 == 0 else N))
    grid = (N // tn, M // tm)
    n_in = 2 + (add is not None)

    def body(*refs):
        a_ref, b_ref = refs[0], refs[1]
        o_ref = refs[n_in]
        acc = dot_nn(a_ref[...], b_ref[...]) if mode == "nn" else dot_nt(a_ref[...], b_ref[...])
        if add is not None:
            acc = acc + refs[2][...]
        o_ref[...] = acc.astype(o_ref.dtype)

    in_specs = [pl.BlockSpec((tm, K), lambda j, i: (i, 0)),
                pl.BlockSpec((K, tn), lambda j, i: (0, j)) if mode == "nn" else pl.BlockSpec((tn, K), lambda j, i: (j, 0))]
    args = [a, b]
    if add is not None:
        in_specs.append(pl.BlockSpec((tm, tn), lambda j, i: (i, j)))
        args.append(add)
    return pl.pallas_call(
        body, out_shape=jax.ShapeDtypeStruct((M, N), out_dtype), grid=grid, in_specs=in_specs,
        out_specs=pl.BlockSpec((tm, tn), lambda j, i: (i, j)), name=name,
        compiler_params=_cparams(("arbitrary", "arbitrary")))(*args)


def norm_project(name, x, g, w):
    T, D = x.shape
    N = w.shape[1]
    tm = min(T, 512)

    def body(x_ref, g_ref, w_ref, h_ref, p_ref):
        h = _rms(x_ref[...], g_ref[...]).astype(BF16)
        h_ref[...] = h
        p_ref[...] = dot_nn(h, w_ref[...])

    return pl.pallas_call(
        body, out_shape=[jax.ShapeDtypeStruct((T, D), BF16), jax.ShapeDtypeStruct((T, N), F32)], grid=(T // tm,),
        in_specs=[pl.BlockSpec((tm, D), lambda i: (i, 0)), pl.BlockSpec((1, D), lambda i: (0, 0)),
                  pl.BlockSpec((D, N), lambda i: (0, 0))],
        out_specs=[pl.BlockSpec((tm, D), lambda i: (i, 0)), pl.BlockSpec((tm, N), lambda i: (i, 0))],
        name=name, compiler_params=_cparams(("arbitrary",)))(x, g, w)


def norm_project_bwd(name, assemble, pieces, w, x, g, dy):
    T, D = x.shape
    N = w.shape[1]
    tm = min(T, 256)
    n_in = len(pieces)

    def body(*refs):
        w_ref, x_ref, g_ref, dy_ref, dp_ref, dx_ref, dg_ref = refs[n_in:]

        @pl.when(pl.program_id(0) == 0)
        def _():
            dg_ref[...] = jnp.zeros_like(dg_ref)

        (dp,) = assemble(*[r[...] for r in refs[:n_in]])
        dp_ref[...] = dp
        _, vjp = jax.vjp(_rms, x_ref[...], g_ref[...])
        dx, dg = vjp(dot_nt(dp, w_ref[...]))
        dx_ref[...] = dx + dy_ref[...]
        dg_ref[...] += dg

    in_specs, args = [], []
    for spec in pieces:
        arr, bs = _in_spec(spec, tm)
        args.append(arr)
        in_specs.append(bs)
    row = pl.BlockSpec((tm, D), lambda i: (i, 0))
    vec = pl.BlockSpec((1, D), lambda i: (0, 0))
    wide = pl.BlockSpec((tm, N), lambda i: (i, 0))
    return pl.pallas_call(
        body,
        out_shape=[jax.ShapeDtypeStruct((T, N), BF16), jax.ShapeDtypeStruct((T, D), F32), jax.ShapeDtypeStruct((1, D), F32)],
        grid=(T // tm,), in_specs=in_specs + [pl.BlockSpec((D, N), lambda i: (0, 0)), row, vec, row],
        out_specs=[wide, row, vec], name=name, compiler_params=_cparams(("arbitrary",)))(*args, w, x, g, dy)


def mix_project(name, tile_fn, T, ins, w, x):
    tr = _row_tile(T)
    n_in = len(ins)
    K, D = w.shape

    def body(*refs):
        w_ref, x_ref, y_ref = refs[n_in:]
        (mix,) = tile_fn(*[r[...] for r in refs[:n_in]])
        y_ref[...] = x_ref[...] + dot_nn(mix, w_ref[...])

    in_specs, args = [], []
    for spec in ins:
        arr, bs = _in_spec(spec, tr)
        args.append(arr)
        in_specs.append(bs)
    row = pl.BlockSpec((tr, D), lambda i: (i, 0))
    return pl.pallas_call(
        body, out_shape=jax.ShapeDtypeStruct((T, D), F32), grid=(T // tr,),
        in_specs=in_specs + [pl.BlockSpec((K, D), lambda i: (0, 0)), row], out_specs=row,
        name=name, compiler_params=_cparams(("arbitrary",)))(*args, w, x)


def mix_project_bwd(name, tile_fn, T, ins, w, dy, skip=(), narrow=()):
    tr = _row_tile(T)
    n_in = len(ins)
    K, D = w.shape
    want = [k for k in range(n_in) if k not in skip]

    def body(*refs):
        w_ref, dy_ref = refs[n_in:n_in + 2]
        outs, dw_ref = refs[n_in + 2:-1], refs[-1]
        first = pl.program_id(0) == 0
        (mix,), vjp = jax.vjp(tile_fn, *[r[...] for r in refs[:n_in]])
        d = dy_ref[...].astype(BF16)
        grads = vjp((dot_nt(d, w_ref[...]),))

        @pl.when(first)
        def _():
            dw_ref[...] = jnp.zeros_like(dw_ref)

        dw_ref[...] += dot_tn(mix, d)
        for r, k in zip(outs, want):
            if ins[k][0] == "full":
                @pl.when(first)
                def _():
                    r[...] = jnp.zeros_like(r)
                r[...] += grads[k]
            else:
                r[...] = grads[k].astype(r.dtype)

    in_specs, args = [], []
    for spec in ins:
        arr, bs = _in_spec(spec, tr)
        args.append(arr)
        in_specs.append(bs)
    out_specs, out_shape = [], []
    for k in want:
        if ins[k][0] == "full":
            arr = ins[k][1]
            out_specs.append(_full_spec(arr))
            out_shape.append(jax.ShapeDtypeStruct(arr.shape, F32))
        else:
            wd = ins[k][-1]
            out_specs.append(pl.BlockSpec((tr, wd), lambda i: (i, 0)))
            out_shape.append(jax.ShapeDtypeStruct((T, wd), BF16 if k in narrow else F32))
    wspec = pl.BlockSpec((K, D), lambda i: (0, 0))
    return pl.pallas_call(
        body, out_shape=out_shape + [jax.ShapeDtypeStruct((K, D), F32)], grid=(T // tr,),
        in_specs=in_specs + [wspec, pl.BlockSpec((tr, D), lambda i: (i, 0))], out_specs=out_specs + [wspec],
        name=name, compiler_params=_cparams(("arbitrary",)))(*args, w, dy)


def _attn_heads(n):
    G = N_Q_A // N_KV_A
    k_sl = pl.ds(n * HEAD_DIM_A, HEAD_DIM_A)
    v_sl = pl.ds(W_KV_A + n * HEAD_DIM_A, HEAD_DIM_A)
    q_sl = [pl.ds((n * G + g) * HEAD_DIM_A, HEAD_DIM_A) for g in range(G)]
    return k_sl, v_sl, q_sl, range(n * G, (n + 1) * G)


def attn_fwd(p, q_off, kvp, sink, slopes, T):
    nb = T // BLOCK
    assert q_off % W_A == 0

    def body(q_ref, kv_ref, sink_ref, slope_ref, o_ref):
        c = pl.program_id(0)
        rows = pl.ds(pl.multiple_of(c * BLOCK, BLOCK), 3 * BLOCK)
        for n in range(N_KV_A):
            k_sl, v_sl, q_sl, heads = _attn_heads(n)
            group = pl.ds(n * ATTN_GROUP * BLOCK, ATTN_GROUP * BLOCK)
            q = jnp.concatenate([q_ref[:, s] for s in q_sl], axis=0)
            o = attn_block(q, kv_ref[rows, k_sl], kv_ref[rows, v_sl], sink_ref[group, :], slope_ref[group, :], c, T)
            for g, s in enumerate(q_sl):
                o_ref[:, s] = o[g * BLOCK:(g + 1) * BLOCK]

    full = lambda a: pl.BlockSpec(a.shape, functools.partial(lambda c, nd: (0,) * nd, nd=a.ndim))
    return pl.pallas_call(
        body, out_shape=jax.ShapeDtypeStruct((T, W_A), F32), grid=(nb,),
        in_specs=[pl.BlockSpec((BLOCK, W_A), lambda c: (c, q_off // W_A)), full(kvp), full(sink), full(slopes)],
        out_specs=pl.BlockSpec((BLOCK, W_A), lambda c: (c, 0)),
        name="attn_fwd", compiler_params=_cparams(("arbitrary",)))(p, kvp, sink, slopes)


def attn_bwd(p, q_off, kvp, sink, slopes, do, T):
    nb = T // BLOCK

    def body(q_ref, kv_ref, sink_ref, slope_ref, do_ref, dq_ref, dkv_ref, dsink_ref):
        c = pl.program_id(0)

        @pl.when(c == 0)
        def _():
            dkv_ref[...] = jnp.zeros_like(dkv_ref)
            dsink_ref[...] = jnp.zeros_like(dsink_ref)

        rows = pl.ds(pl.multiple_of(c * BLOCK, BLOCK), 3 * BLOCK)
        for n in range(N_KV_A):
            k_sl, v_sl, q_sl, heads = _attn_heads(n)
            group = pl.ds(n * ATTN_GROUP * BLOCK, ATTN_GROUP * BLOCK)
            slope = slope_ref[group, :]
            q = jnp.concatenate([q_ref[:, s] for s in q_sl], axis=0)
            do = jnp.concatenate([do_ref[:, s] for s in q_sl], axis=0)
            _, vjp = jax.vjp(lambda q_, kk, vv, sk: attn_block(q_, kk, vv, sk, slope, c, T),
                             q, kv_ref[rows, k_sl], kv_ref[rows, v_sl], sink_ref[group, :])
            dq, dks, dvs, dsk = vjp(do)
            dkv_ref[rows, k_sl] += dks
            dkv_ref[rows, v_sl] += dvs
            for g, (s, h) in enumerate(zip(q_sl, heads)):
                seg = slice(g * BLOCK, (g + 1) * BLOCK)
                dq_ref[:, s] = dq[seg].astype(dq_ref.dtype)
                dsink_ref[h] += jnp.sum(dsk[seg], axis=0, keepdims=True)

    full = lambda a: pl.BlockSpec(a.shape, functools.partial(lambda c, nd: (0,) * nd, nd=a.ndim))
    qspec = pl.BlockSpec((BLOCK, W_A), lambda c: (c, 0))
    return pl.pallas_call(
        body,
        out_shape=[jax.ShapeDtypeStruct((T, W_A), BF16), jax.ShapeDtypeStruct(kvp.shape, F32),
                   jax.ShapeDtypeStruct((N_Q_A, 1, 1), F32)],
        grid=(nb,),
        in_specs=[pl.BlockSpec((BLOCK, W_A), lambda c: (c, q_off // W_A)), full(kvp), full(sink), full(slopes), qspec],
        out_specs=[qspec, full(kvp), pl.BlockSpec((N_Q_A, 1, 1), lambda c: (0, 0, 0))],
        name="attn_bwd", compiler_params=_cparams(("arbitrary",)))(p, kvp, sink, slopes, do)


def mem_fwd(p, q_off, kv, T):
    tr = min(T, 2 * _row_tile(T))
    assert q_off % W_M == 0

    def body(q_ref, kv_ref, o_ref):
        for h in range(N_HEADS_M):
            hs = pl.ds(h * HEAD_DIM_M, HEAD_DIM_M)
            (o,) = mem_tile(q_ref[:, hs], kv_ref[:, hs], kv_ref[:, pl.ds(W_M + h * HEAD_DIM_M, HEAD_DIM_M)])
            o_ref[:, hs] = o

    return pl.pallas_call(
        body, out_shape=jax.ShapeDtypeStruct((T, W_M), F32), grid=(T // tr,),
        in_specs=[pl.BlockSpec((tr, W_M), lambda i: (i, q_off // W_M)), pl.BlockSpec((N_MEM, 2 * W_M), lambda i: (0, 0))],
        out_specs=pl.BlockSpec((tr, W_M), lambda i: (i, 0)),
        name="mem_fwd", compiler_params=_cparams(("arbitrary",)))(p, kv)


def mem_bwd(p, q_off, kv, do, T):
    tr = min(T, 2 * _row_tile(T))

    def body(q_ref, kv_ref, do_ref, dq_ref, dkv_ref):
        @pl.when(pl.program_id(0) == 0)
        def _():
            dkv_ref[...] = jnp.zeros_like(dkv_ref)

        for h in range(N_HEADS_M):
            hs = pl.ds(h * HEAD_DIM_M, HEAD_DIM_M)
            vs = pl.ds(W_M + h * HEAD_DIM_M, HEAD_DIM_M)
            _, vjp = jax.vjp(mem_tile, q_ref[:, hs], kv_ref[:, hs], kv_ref[:, vs])
            dq, dk, dv = vjp((do_ref[:, hs],))
            dq_ref[:, hs] = dq.astype(dq_ref.dtype)
            dkv_ref[:, hs] += dk
            dkv_ref[:, vs] += dv

    kvspec = pl.BlockSpec((N_MEM, 2 * W_M), lambda i: (0, 0))
    return pl.pallas_call(
        body,
        out_shape=[jax.ShapeDtypeStruct((T, W_M), BF16), jax.ShapeDtypeStruct((N_MEM, 2 * W_M), F32)],
        grid=(T // tr,),
        in_specs=[pl.BlockSpec((tr, W_M), lambda i: (i, q_off // W_M)), kvspec, pl.BlockSpec((tr, W_M), lambda i: (i, 0))],
        out_specs=[pl.BlockSpec((tr, W_M), lambda i: (i, 0)), kvspec],
        name="mem_bwd", compiler_params=_cparams(("arbitrary",)))(p, kv, do)


def _scan_const_specs(dk):
    C, L = SCAN_CHUNK, SCAN_LEVELS
    return [pl.BlockSpec((2, (2 + L) * C, C), lambda n: (0, 0, 0)),
            pl.BlockSpec((2, C, (2 + L) * C), lambda n: (0, 0, 0)),
            pl.BlockSpec((2, L * C, dk), lambda n: (0, 0, 0)),
            pl.BlockSpec((2, L * C, C), lambda n: (0, 0, 0))]


def _chunk_spec(src, width, chunk_of):
    arr, sel = src
    if arr.ndim == 2:
        assert sel % width == 0
        return pl.BlockSpec((SCAN_CHUNK * SCAN_SUB, width),
                            functools.partial(lambda n, b: (chunk_of(n), b), b=sel // width))
    return pl.BlockSpec((None, SCAN_CHUNK * SCAN_SUB, width), functools.partial(lambda n, d: (d, chunk_of(n), 0), d=sel))


def _scan_const_args():
    h, ht, qm, pm = _scan_consts()
    return [jnp.asarray(h, BF16), jnp.asarray(ht, BF16), jnp.asarray(qm, F32), jnp.asarray(pm, F32)]


def _full_spec(a):
    return pl.BlockSpec(a.shape, functools.partial(lambda n, nd: (0,) * nd, nd=a.ndim))


def scan_fwd(name, prep, raws, params, heads, dk, dv, T):
    C, S = SCAN_CHUNK, SCAN_SUB
    N = T // (C * S)
    assert dk == C
    Wv = heads * dv
    orders = (lambda n: n, lambda n: N - 1 - n)
    n_raw, n_par = [len(r) for r in raws], [len(p) for p in params]

    def body(*refs):
        pos, raw_refs, par_refs = 0, [], []
        for d in range(2):
            raw_refs.append(refs[pos:pos + n_raw[d]])
            pos += n_raw[d]
        for d in range(2):
            par_refs.append(refs[pos:pos + n_par[d]])
            pos += n_par[d]
        h_ref, ht_ref, qm_ref, pm_ref = refs[pos:pos + 4]
        o_refs, ss_refs, st_ref = refs[pos + 4:pos + 6], refs[pos + 6:pos + 8], refs[pos + 8]

        @pl.when(pl.program_id(0) == 0)
        def _():
            st_ref[...] = jnp.zeros_like(st_ref)

        for d in range(2):
            consts = (qm_ref[d], pm_ref[d])
            pars = [p[...] for p in par_refs[d]]
            for sub in (range(S) if d == 0 else reversed(range(S))):
                rows = pl.ds(sub * C, C)
                q, k, v, g = prep([r[rows, :] for r in raw_refs[d]], pars)
                e = _split_mm(h_ref[d], g)
                tot = jnp.sum(g, axis=0, keepdims=True)
                for h in range(heads):
                    ks, vs = slice(h * dk, (h + 1) * dk), slice(h * dv, (h + 1) * dv)
                    st = st_ref[d, h]
                    ss_refs[d][h, sub] = st
                    o, st_new = scan_chunk(q[:, ks], k[:, ks], v[:, vs], e[:, ks], tot[:, ks], st, *consts)
                    o_refs[d][rows, vs] = o
                    st_ref[d, h] = st_new

    ss_spec = lambda order: pl.BlockSpec((heads, S, dv, dk), lambda n: (0, order(n), 0, 0))
    return pl.pallas_call(
        body,
        out_shape=[jax.ShapeDtypeStruct((T, Wv), F32)] * 2 + [jax.ShapeDtypeStruct((heads, T // C, dv, dk), F32)] * 2,
        grid=(N,),
        in_specs=[_chunk_spec(s, w, orders[d]) for d in range(2) for s, w in raws[d]]
        + [_full_spec(p) for d in range(2) for p in params[d]] + _scan_const_specs(dk),
        out_specs=[pl.BlockSpec((C * S, Wv), lambda n: (orders[0](n), 0)),
                   pl.BlockSpec((C * S, Wv), lambda n: (orders[1](n), 0)), ss_spec(orders[0]), ss_spec(orders[1])],
        scratch_shapes=[pltpu.VMEM((2, heads, dv, dk), F32)],
        name=name, compiler_params=_cparams(("arbitrary",)))(
            *[s[0] for d in range(2) for s, _ in raws[d]], *[p for d in range(2) for p in params[d]], *_scan_const_args())


def scan_bwd(name, prep, raws, params, ss, do, heads, dk, dv, T):
    C, S = SCAN_CHUNK, SCAN_SUB
    N = T // (C * S)
    Wv = heads * dv
    orders = (lambda n: N - 1 - n, lambda n: n)
    n_raw, n_par = [len(r) for r in raws], [len(p) for p in params]

    def body(*refs):
        pos, raw_refs, par_refs, draw_refs, dpar_refs = 0, [], [], [], []
        for group, counts in ((raw_refs, n_raw), (par_refs, n_par)):
            for d in range(2):
                group.append(refs[pos:pos + counts[d]])
                pos += counts[d]
        ss_refs, do_refs = refs[pos:pos + 2], refs[pos + 2:pos + 4]
        h_ref, ht_ref, qm_ref, pm_ref = refs[pos + 4:pos + 8]
        pos += 8
        for group, counts in ((draw_refs, n_raw), (dpar_refs, n_par)):
            for d in range(2):
                group.append(refs[pos:pos + counts[d]])
                pos += counts[d]
        dst_ref = refs[pos]

        @pl.when(pl.program_id(0) == 0)
        def _():
            dst_ref[...] = jnp.zeros_like(dst_ref)
            for d in range(2):
                for r in dpar_refs[d]:
                    r[...] = jnp.zeros_like(r)

        for d in range(2):
            consts = (qm_ref[d], pm_ref[d])
            pars = [p[...] for p in par_refs[d]]
            for sub in (reversed(range(S)) if d == 0 else range(S)):
                rows = pl.ds(sub * C, C)
                (q, k, v, g), prep_vjp = jax.vjp(prep, [r[rows, :] for r in raw_refs[d]], pars)
                e = _split_mm(h_ref[d], g)
                tot = jnp.sum(g, axis=0, keepdims=True)
                dqs, dks, dvs, des, dtots = [], [], [], [], []
                for h in range(heads):
                    ks, vs = slice(h * dk, (h + 1) * dk), slice(h * dv, (h + 1) * dv)
                    _, vjp = jax.vjp(lambda q_, k_, v_, e_, t_, st_: scan_chunk(q_, k_, v_, e_, t_, st_, *consts),
                                     q[:, ks], k[:, ks], v[:, vs], e[:, ks], tot[:, ks], ss_refs[d][h, sub])
                    dq, dk_, dv_, de, dtot, dst = vjp((do_refs[d][rows, vs], dst_ref[d, h]))
                    dst_ref[d, h] = dst
                    for group, val in ((dqs, dq), (dks, dk_), (dvs, dv_), (des, de), (dtots, dtot)):
                        group.append(val)
                cat = lambda parts: jnp.concatenate(parts, axis=-1)
                dg = _split_mm(ht_ref[d], cat(des)) + cat(dtots)
                draws, dpars = prep_vjp((cat(dqs), cat(dks), cat(dvs), dg))
                for r, val in zip(draw_refs[d], draws):
                    r[rows, :] = val.astype(r.dtype)
                for r, val in zip(dpar_refs[d], dpars):
                    r[...] += val

    ss_spec = lambda order: pl.BlockSpec((heads, S, dv, dk), lambda n: (0, order(n), 0, 0))
    row_out = lambda w, order: pl.BlockSpec((C * S, w), lambda n: (order(n), 0))
    return pl.pallas_call(
        body,
        out_shape=[jax.ShapeDtypeStruct((T, w), BF16) for d in range(2) for _, w in raws[d]]
        + [jax.ShapeDtypeStruct(p.shape, F32) for d in range(2) for p in params[d]],
        grid=(N,),
        in_specs=[_chunk_spec(s, w, orders[d]) for d in range(2) for s, w in raws[d]]
        + [_full_spec(p) for d in range(2) for p in params[d]]
        + [ss_spec(orders[0]), ss_spec(orders[1]), _chunk_spec(do, Wv, orders[0]), _chunk_spec(do, Wv, orders[1])]
        + _scan_const_specs(dk),
        out_specs=[row_out(w, orders[d]) for d in range(2) for _, w in raws[d]]
        + [_full_spec(p) for d in range(2) for p in params[d]],
        scratch_shapes=[pltpu.VMEM((2, heads, dv, dk), F32)],
        name=name, compiler_params=_cparams(("arbitrary",)))(
            *[s[0] for d in range(2) for s, _ in raws[d]], *[p for d in range(2) for p in params[d]],
            ss[0], ss[1], do[0], do[0], *_scan_const_args())


def final_call(x, g, target, T):
    tr = _row_tile(T)

    def tile(xv, gv, tv):
        y = _rms(xv, gv)
        err = (y - tv) ** 2
        return jnp.sum(jnp.sum(err, axis=-1, keepdims=True), axis=0, keepdims=True) * (0.5 / D_MODEL)

    def body(x_ref, g_ref, t_ref, loss_ref, dx_ref, dg_ref):
        i = pl.program_id(0)
        tv = t_ref[...]
        lv, vjp = jax.vjp(lambda a, b: tile(a, b, tv), x_ref[...], g_ref[...])
        dx, dg = vjp(jnp.ones((1, 1), F32))
        dx_ref[...] = dx

        @pl.when(i == 0)
        def _():
            loss_ref[...] = jnp.zeros_like(loss_ref)
            dg_ref[...] = jnp.zeros_like(dg_ref)

        loss_ref[...] += jnp.broadcast_to(lv, loss_ref.shape)
        dg_ref[...] += dg

    return pl.pallas_call(
        body,
        out_shape=[jax.ShapeDtypeStruct((8, 128), F32), jax.ShapeDtypeStruct((T, D_MODEL), F32),
                   jax.ShapeDtypeStruct((1, D_MODEL), F32)],
        grid=(T // tr,),
        in_specs=[pl.BlockSpec((tr, D_MODEL), lambda i: (i, 0)), pl.BlockSpec((1, D_MODEL), lambda i: (0, 0)),
                  pl.BlockSpec((tr, D_MODEL), lambda i: (i, 0))],
        out_specs=[pl.BlockSpec((8, 128), lambda i: (0, 0)), pl.BlockSpec((tr, D_MODEL), lambda i: (i, 0)),
                   pl.BlockSpec((1, D_MODEL), lambda i: (0, 0))],
        name="final_loss", compiler_params=_cparams(("arbitrary",)))(x, g, target)


def adamw_call(w, g, m, v):
    shape = w.shape
    c = shape[-1]
    r = int(np.prod(shape[:-1])) if len(shape) > 1 else 1
    tr = r if r <= 256 else 256
    assert r % tr == 0

    def body(w_ref, g_ref, m_ref, v_ref, d_ref, nm_ref, nv_ref):
        gv = g_ref[...]
        nm = ADAM_B1 * m_ref[...] + (1.0 - ADAM_B1) * gv
        nv = ADAM_B2 * v_ref[...] + (1.0 - ADAM_B2) * jnp.square(gv)
        m_hat = nm / (1.0 - ADAM_B1 ** ADAM_STEP)
        v_hat = nv / (1.0 - ADAM_B2 ** ADAM_STEP)
        d_ref[...] = -ADAM_LR * (m_hat / (jnp.sqrt(v_hat) + ADAM_EPS) + ADAM_WD * w_ref[...])
        nm_ref[...] = nm
        nv_ref[...] = nv

    spec = pl.BlockSpec((tr, c), lambda i: (i, 0))
    outs = pl.pallas_call(body, out_shape=[jax.ShapeDtypeStruct((r, c), F32)] * 3, grid=(r // tr,),
                          in_specs=[spec] * 4, out_specs=[spec] * 3, name="adamw",
                          compiler_params=_cparams(("arbitrary",)))(*(t.reshape(r, c) for t in (w, g, m, v)))
    return tuple(o.reshape(shape) for o in outs)


def adamw_halves(w, mine, other, m, v, c):
    L, R, C = w.shape
    by_cols = mine.shape[-1] != C
    if by_cols:
        tile, nbh = (R, C // 2), 1
        full_idx = lambda l, i: (l, 0, i)
    else:
        rh = R // 2
        tr = rh if rh <= 256 else rh // 2
        assert tr % 8 == 0
        tile, nbh = (tr, C), rh // tr
        full_idx = lambda l, i: (l, i, 0)

    def body(c_ref, w_ref, a_ref, b_ref, m_ref, v_ref, g_ref, d_ref, nm_ref, nv_ref):
        is_mine = (pl.program_id(1) // nbh) == c_ref[0]
        gv = jnp.where(is_mine, a_ref[...], b_ref[...])
        nm = ADAM_B1 * m_ref[...] + (1.0 - ADAM_B1) * gv
        nv = ADAM_B2 * v_ref[...] + (1.0 - ADAM_B2) * jnp.square(gv)
        m_hat = nm / (1.0 - ADAM_B1 ** ADAM_STEP)
        v_hat = nv / (1.0 - ADAM_B2 ** ADAM_STEP)
        g_ref[...] = gv
        d_ref[...] = -ADAM_LR * (m_hat / (jnp.sqrt(v_hat) + ADAM_EPS) + ADAM_WD * w_ref[...])
        nm_ref[...] = nm
        nv_ref[...] = nv

    full = pl.BlockSpec((None,) + tile, lambda l, i, c_ref: full_idx(l, i))
    half = pl.BlockSpec((None,) + tile, lambda l, i, c_ref: (l, i % nbh, 0))
    grid_spec = pltpu.PrefetchScalarGridSpec(num_scalar_prefetch=1, grid=(L, 2 * nbh),
                                             in_specs=[full, half, half, full, full], out_specs=[full] * 4)
    return pl.pallas_call(body, out_shape=[jax.ShapeDtypeStruct(w.shape, F32)] * 4, grid_spec=grid_spec,
                          name="adamw_halves", compiler_params=_cparams(("arbitrary", "arbitrary")))(c, w, mine, other, m, v)


def sum_devices(g64):
    def body(x_ref, o_ref):
        acc = x_ref[0:8, :]
        for d in range(1, 8):
            acc = acc + x_ref[8 * d:8 * d + 8, :]
        o_ref[...] = acc

    return pl.pallas_call(body, out_shape=jax.ShapeDtypeStruct((8, D_MODEL), F32), name="sum_devices")(g64)


def _half_tile(rh):
    if rh <= 512:
        return rh
    return next(rh // d for d in range(2, rh) if rh % d == 0 and (rh // d) % 16 == 0 and rh // d <= 512)


def _half_geometry(full_shape, half_shape):
    R, C = full_shape[-2:]
    if half_shape[-1] != C:
        return (R, C // 2), 1, lambda i, c: (0, c)
    tr = _half_tile(R // 2)
    nblk = (R // 2) // tr
    return (tr, C), nblk, lambda i, c: (i + c * nblk, 0)


def _work_items(counts):
    starts = [int(v) for v in np.cumsum([0] + list(counts[:-1]))]
    local = lambda a, s: jnp.clip(s - starts[a], 0, counts[a] - 1)
    return starts, int(sum(counts)), local


def add_sibling(gs, recvs, c, out_dtypes):
    n = len(gs)
    geo = [_half_geometry(g.shape, r.shape) for g, r in zip(gs, recvs)]
    counts = [4 * nblk for _, nblk, _ in geo]
    starts, total, local = _work_items(counts)

    def body(c_ref, *refs):
        s = pl.program_id(0)
        for a in range(n):
            g_ref, r_ref, o_ref = refs[a], refs[n + a], refs[2 * n + a]

            @pl.when((s >= starts[a]) & (s < starts[a] + counts[a]))
            def _():
                o_ref[...] = (g_ref[...] + r_ref[...]).astype(o_ref.dtype)

    def own_idx(s, c_ref, a):
        _, nblk, own = geo[a]
        k = local(a, s)
        return (k // nblk,) + own(k % nblk, c_ref[0])

    def half_idx(s, c_ref, a):
        k = local(a, s)
        return (k // geo[a][1], k % geo[a][1], 0)

    halves = [pl.BlockSpec((None,) + geo[a][0], functools.partial(half_idx, a=a)) for a in range(n)]
    grid_spec = pltpu.PrefetchScalarGridSpec(
        num_scalar_prefetch=1, grid=(total,),
        in_specs=[pl.BlockSpec((None,) + geo[a][0], functools.partial(own_idx, a=a)) for a in range(n)] + halves,
        out_specs=halves)
    return pl.pallas_call(body, out_shape=[jax.ShapeDtypeStruct(r.shape, dt) for r, dt in zip(recvs, out_dtypes)],
                          grid_spec=grid_spec, name="rs_add_sibling",
                          compiler_params=_cparams(("arbitrary",)))(c, *gs, *recvs)


def add_chips(gs, recvs, r3s, place):
    n = len(gs)
    geo = [_half_geometry(g.shape, r.shape) for g, r in zip(gs, recvs)]
    counts = [nblk for _, nblk, _ in geo]
    starts, total, local = _work_items(counts)

    def body(p_ref, *refs):
        s = pl.program_id(0)
        up = lambda r: r[...].astype(F32)
        for a in range(n):
            g_ref, s_ref, o_ref = refs[a], refs[n + a], refs[5 * n + a]
            a_ref, b_ref, c_ref = refs[2 * n + 3 * a:2 * n + 3 * a + 3]

            @pl.when((s >= starts[a]) & (s < starts[a] + counts[a]))
            def _():
                o_ref[...] = (((g_ref[...] + up(s_ref)) + up(a_ref)) + up(b_ref)) + up(c_ref)

    own_idx = lambda s, p_ref, a: (p_ref[0],) + geo[a][2](local(a, s), p_ref[1])
    sib_idx = lambda s, p_ref, a: (p_ref[0], local(a, s), 0)
    chip_idx = lambda s, p_ref, a, k: (k, local(a, s), 0)
    spec = lambda a, idx, **kw: pl.BlockSpec((None,) + geo[a][0], functools.partial(idx, a=a, **kw))
    grid_spec = pltpu.PrefetchScalarGridSpec(
        num_scalar_prefetch=1, grid=(total,),
        in_specs=[spec(a, own_idx) for a in range(n)] + [spec(a, sib_idx) for a in range(n)]
        + [spec(a, chip_idx, k=k) for a in range(n) for k in range(3)],
        out_specs=[pl.BlockSpec(geo[a][0], functools.partial(lambda s, p_ref, a: (local(a, s), 0), a=a))
                   for a in range(n)])
    return pl.pallas_call(body, out_shape=[jax.ShapeDtypeStruct(r.shape[1:], F32) for r in recvs],
                          grid_spec=grid_spec, name="rs_add_chips", compiler_params=_cparams(("arbitrary",)))(
                              place, *gs, *recvs, *[r for r3 in r3s for r in (r3, r3, r3)])


def _remote(src, dst, ssem, rsem, dev):
    return pltpu.make_async_remote_copy(src_ref=src, dst_ref=dst, send_sem=ssem, recv_sem=rsem,
                                        device_id=dev, device_id_type=pl.DeviceIdType.MESH)


def _mesh_places():
    x, y, c = lax.axis_index("x"), lax.axis_index("y"), lax.axis_index("c")
    chips = [(1 - x, y), (x, 1 - y), (1 - x, 1 - y)]
    return x, y, c, (x, y, 1 - c), chips


def _hbm_specs(n):
    return [pl.BlockSpec(memory_space=pltpu.HBM) for _ in range(n)]


def _gather_body(ins, outs, n_split, send_sems, recv_sems, handshake):
    x, y, c, sibling, chips = _mesh_places()
    mine = 2 * x + y
    if handshake:
        barrier = pltpu.get_barrier_semaphore()
        peers = [sibling] + [(*chip, c) for chip in chips]
        for peer in peers:
            pl.semaphore_signal(barrier, inc=1, device_id=peer, device_id_type=pl.DeviceIdType.MESH)
        pl.semaphore_wait(barrier, len(peers))

    def half(a, chip_idx, which):
        rh = ins[a].shape[0] // 2
        return outs[a].at[chip_idx, pl.ds(which * rh, rh), :]

    sent = []
    for a in range(len(ins)):
        for k, chip in enumerate(chips):
            if a < n_split:
                rh = ins[a].shape[0] // 2
                src, dst = ins[a].at[pl.ds(c * rh, rh), :], half(a, mine, c)
            else:
                src, dst = ins[a], outs[a].at[mine]
            sent.append(_remote(src, dst, send_sems.at[a, k], recv_sems.at[a, k], (*chip, c)))
    for cp in sent:
        cp.start()
    for a in range(len(ins)):
        for k, chip in enumerate(chips):
            j = 2 * chip[0] + chip[1]
            region = half(a, j, c) if a < n_split else outs[a].at[j]
            _remote(region, region, send_sems.at[a, k], recv_sems.at[a, k], (*chip, c)).wait_recv()
            if a < n_split:
                fwd = _remote(region, region, send_sems.at[a, 3 + k], recv_sems.at[a, 3 + k], sibling)
                fwd.start()
                sent.append(fwd)
    for a in range(n_split):
        for k, chip in enumerate(chips):
            region = half(a, 2 * chip[0] + chip[1], 1 - c)
            _remote(region, region, send_sems.at[a, 3 + k], recv_sems.at[a, 3 + k], sibling).wait_recv()
    for cp in sent:
        cp.wait_send()


def gather_weights(shards, small):
    arrs = list(shards) + [small]
    n = len(arrs)

    def body(*refs):
        _gather_body(refs[:n], refs[n:2 * n], n - 1, refs[2 * n], refs[2 * n + 1], handshake=False)

    return pl.pallas_call(
        body, out_shape=[jax.ShapeDtypeStruct((4,) + a.shape, a.dtype) for a in arrs],
        in_specs=_hbm_specs(n), out_specs=_hbm_specs(n),
        scratch_shapes=[pltpu.SemaphoreType.DMA((n, 6)), pltpu.SemaphoreType.DMA((n, 6))],
        name="gather_weights")(*arrs)


def gather_weights_async(shards):
    n = len(shards)

    def body(*refs):
        _gather_body(refs[:n], refs[n:2 * n], n, refs[2 * n], refs[2 * n + 1], handshake=True)

    return pl.kernel(
        body, out_type=[jax.ShapeDtypeStruct((4,) + a.shape, a.dtype) for a in shards],
        mesh=plsc.ScalarSubcoreMesh(axis_name="seq", num_cores=1),
        scratch_types=[pltpu.SemaphoreType.DMA((n, 6)), pltpu.SemaphoreType.DMA((n, 6))],
        compiler_params=pltpu.CompilerParams(collective_id=1), name="gather_weights_async")(*shards)


def _sequencer_call(name, body, out_type, sem_shape, collective_id, args):
    return pl.kernel(
        body, out_type=out_type, mesh=plsc.ScalarSubcoreMesh(axis_name="seq", num_cores=1),
        scratch_types=[pltpu.SemaphoreType.DMA(sem_shape), pltpu.SemaphoreType.DMA(sem_shape)],
        compiler_params=pltpu.CompilerParams(collective_id=collective_id), name=name)(*args)


def _handshake(peers):
    barrier = pltpu.get_barrier_semaphore()
    for peer in peers:
        pl.semaphore_signal(barrier, inc=1, device_id=peer, device_id_type=pl.DeviceIdType.MESH)
    pl.semaphore_wait(barrier, len(peers))


def exchange_siblings(name, srcs, axes, collective_id):
    n = len(srcs)

    def body(*refs):
        ins, outs = refs[:n], refs[n:2 * n]
        send_sems, recv_sems = refs[2 * n:]
        x, y, c, sibling, chips = _mesh_places()
        _handshake([sibling])
        cps = []
        for a in range(n):
            src = ins[a]
            if axes[a] is not None:
                half = src.shape[axes[a]] // 2
                theirs = pl.ds((1 - c) * half, half)
                src = src.at[:, theirs, :] if axes[a] == 1 else src.at[:, :, theirs]
            cps.append(_remote(src, outs[a], send_sems.at[a], recv_sems.at[a], sibling))
        for cp in cps:
            cp.start()
        for cp in cps:
            cp.wait()

    def shape(g, axis):
        return g.shape if axis is None else tuple(d // 2 if k == axis else d for k, d in enumerate(g.shape))

    return _sequencer_call(name, body, [jax.ShapeDtypeStruct(shape(g, ax), g.dtype) for g, ax in zip(srcs, axes)],
                           (n,), collective_id, srcs)


def exchange_chips(name, s1s, collective_id):
    n = len(s1s)

    def body(*refs):
        ins, outs = refs[:n], refs[n:2 * n]
        send_sems, recv_sems = refs[2 * n:]
        x, y, c, sibling, chips = _mesh_places()
        _handshake([(*chip, c) for chip in chips])
        cps = []
        for a in range(n):
            for k, chip in enumerate(chips):
                cps.append(_remote(ins[a].at[2 * chip[0] + chip[1]], outs[a].at[k], send_sems.at[a, k],
                                   recv_sems.at[a, k], (*chip, c)))
        for cp in cps:
            cp.start()
        for cp in cps:
            cp.wait()

    return _sequencer_call(name, body, [jax.ShapeDtypeStruct((3,) + s.shape[1:], s.dtype) for s in s1s], (n, 3),
                           collective_id, s1s)


def allgather_small(v):
    m_per = v.shape[0]

    def body(x_ref, out_ref, send_sems, recv_sems, local_sem):
        x, y, c, sibling, chips = _mesh_places()
        me = (x, y, c)

        def rows(px, py, pc):
            return out_ref.at[pl.ds((4 * px + 2 * py + pc) * m_per, m_per), :]

        def copy(k, block, to, src=None):
            return _remote(rows(*block) if src is None else src, rows(*block), send_sems.at[k], recv_sems.at[k], to)

        mine = pltpu.make_async_copy(x_ref, rows(*me), local_sem)
        mine.start()
        first = [copy(0, me, sibling, src=x_ref)]
        first += [copy(1 + j, me, (*chip, c), src=x_ref) for j, chip in enumerate(chips)]
        for cp in first:
            cp.start()
        passed = [copy(4 + j, (*chip, c), sibling) for j, chip in enumerate(chips)]
        for j, chip in enumerate(chips):
            copy(1 + j, (*chip, c), me).wait_recv()
            passed[j].start()
        copy(0, sibling, me).wait_recv()
        for j, chip in enumerate(chips):
            copy(4 + j, (*chip, 1 - c), me).wait_recv()
        for cp in first + passed:
            cp.wait_send()
        mine.wait()

    return pl.pallas_call(
        body, out_shape=jax.ShapeDtypeStruct((8 * m_per, v.shape[1]), v.dtype),
        in_specs=[pl.BlockSpec(memory_space=pltpu.VMEM)], out_specs=pl.BlockSpec(memory_space=pltpu.VMEM),
        scratch_shapes=[pltpu.SemaphoreType.DMA((7,)), pltpu.SemaphoreType.DMA((7,)), pltpu.SemaphoreType.DMA],
        name="allgather_small")(v)


def _lower_bounds(lb_param):
    lbs = jax.nn.softmax(lb_param.astype(F32), axis=0)
    return jnp.cumsum(lbs, axis=0) - lbs[0]


def _even_fwd(x, i, W, lower, kv, slopes, T):
    O = EVEN_OFF
    g = W["norm_even"][i].reshape(1, D_MODEL)
    h, p = norm_project("mm_in_e", x, g, W["w_in_e"][i])
    kvp = jnp.pad(p[:, O["kA"]:O["kA"] + 2 * W_KV_A], ((BLOCK, BLOCK), (0, 0)))
    sink = jnp.repeat(W["sink"][i], BLOCK).reshape(N_Q_A * BLOCK, 1)
    a = attn_fwd(p, O["qA"], kvp, sink, slopes, T)
    scan_raws = [[((p, O["qB"]), W_B), ((p, O[z]), W_B), ((p, O["iB"]), W_B)] for z in ("zf", "zb")]
    scan_pars = [[lower[i][0:1]], [lower[i][1:2]]]
    o_f, o_b, ss_f, ss_b = scan_fwd("scan_fwd_h", hgrn_prep, scan_raws, scan_pars, N_HEADS_B, HEAD_DIM_B, HEAD_DIM_B, T)
    mo = mem_fwd(p, O["qM"], kv, T)
    hg = W["hgrn_norm"][i].reshape(1, W_B)
    post_ins = [("row", a, 0, W_A), ("row", o_f, 0, W_B), ("row", o_b, 0, W_B), ("row", mo, 0, W_M),
                ("row", p, O["gA"], W_A), ("row", p, O["gB"], W_B), ("row", p, O["gM"], W_M), ("full", hg)]
    x_new = mix_project("even_out", even_post_tile, T, post_ins, W["w_out_e"][i], x)
    return x_new, dict(x=x, g=g, h=h, p=p, kvp=kvp, sink=sink, scan_raws=scan_raws, scan_pars=scan_pars,
                       ss=(ss_f, ss_b), post_ins=post_ins)


def _add2(a, b):
    return a.astype(F32) + b.astype(F32)


def _assemble_even(dqA, dgA, dqB_f, dqB_b, dzf, dzb, diB_f, diB_b, dgB, dqM, dgM, dkvA):
    parts = [dqA, dgA, _add2(dqB_f, dqB_b), dzf, dzb, _add2(diB_f, diB_b), dgB, dqM, dgM, dkvA]
    return (jnp.concatenate([t.astype(BF16) for t in parts], axis=-1),)


def _even_bwd(dxo, sv, i, W, kv, slopes, T, sync):
    O = EVEN_OFF
    p = sv["p"]
    da, dof, dmo, dgA, dgB, dgM, dhg, dwo = mix_project_bwd("even_out_bwd", even_post_tile, T, sv["post_ins"],
                                                            W["w_out_e"][i], dxo, skip=(2,), narrow=(4, 5, 6))
    dqA, dkvp, dsink = attn_bwd(p, O["qA"], sv["kvp"], sv["sink"], slopes, da, T)
    dkvA = dkvp[BLOCK:-BLOCK]
    dqB_f, dzf, diB_f, dqB_b, dzb, diB_b, dlow_f, dlow_b = scan_bwd(
        "scan_bwd_h", hgrn_prep, sv["scan_raws"], sv["scan_pars"], sv["ss"], (dof, 0), N_HEADS_B, HEAD_DIM_B, HEAD_DIM_B, T)
    dqB_f = sync(dqB_f)
    row = lambda arr, w: ("row", arr, 0, w)
    dlow = jnp.concatenate([dlow_f, dlow_b], axis=0)
    dqM, dkv = mem_bwd(p, O["qM"], kv, dmo, T)
    pieces = [row(dqA, W_A), row(dgA, W_A), row(dqB_f, W_B), row(dqB_b, W_B), row(dzf, W_B), row(dzb, W_B),
              row(diB_f, W_B), row(diB_b, W_B), row(dgB, W_B), row(dqM, W_M), row(dgM, W_M), row(dkvA, 2 * W_KV_A)]
    dp, dx, dg = norm_project_bwd("mm_in_e_bwd", _assemble_even, pieces, W["w_in_e"][i], sv["x"], sv["g"], dxo)
    dwi = matmul("mm_dwi_e", sv["h"], dp, "tn")
    return dx, dict(w_in=dwi, w_out=dwo, norm=dg[0], sink=dsink.reshape(N_Q_A), low=dlow, hg=dhg[0], kv=dkv)


def _pad_gate_up(w_up):
    z = jnp.zeros((2, 128, WK_C), F32)
    z = z.at[0, 0:GATE_RANK].set(w_up[0])
    return z.at[1, GATE_RANK:2 * GATE_RANK].set(w_up[1])


def _odd_fwd(x, i, W, kv, T):
    O = ODD_OFF
    g = W["norm_odd"][i].reshape(1, D_MODEL)
    h, p = norm_project("mm_in_o", x, g, W["w_in_o"][i])
    wup = _pad_gate_up(W["w_gate_up"][i])
    one_dir = [((p, O["qC"]), WK_C), ((p, O["kC"]), WK_C), ((p, O["vC"]), WV_C), ((p, O["rr"]), 128)]
    scan_raws = [one_dir, one_dir]
    scan_pars = [[wup[d], W["b_gate"][i][d:d + 1]] for d in range(2)]
    o_f, o_b, ss_f, ss_b = scan_fwd("scan_fwd_g", gla_prep, scan_raws, scan_pars, N_HEADS_C, DK_C, DV_C, T)
    mo = mem_fwd(p, O["qM"], kv, T)
    gg = W["gla_norm"][i].reshape(1, WV_C)
    post_ins = [("row", o_f, 0, WV_C), ("row", o_b, 0, WV_C), ("row", mo, 0, W_M),
                ("row", p, O["gC"], WV_C), ("row", p, O["gM"], W_M), ("full", gg)]
    x_new = mix_project("odd_out", odd_post_tile, T, post_ins, W["w_out_o"][i], x)
    return x_new, dict(x=x, g=g, h=h, p=p, scan_raws=scan_raws, scan_pars=scan_pars, ss=(ss_f, ss_b),
                       post_ins=post_ins)


def _assemble_odd(dq0, dq1, dk0, dk1, dv0, dv1, dgC, dqM, dgM, dr0, dr1):
    parts = [_add2(dq0, dq1), _add2(dk0, dk1), _add2(dv0, dv1), dgC, dqM, dgM, _add2(dr0, dr1)]
    return (jnp.concatenate([t.astype(BF16) for t in parts], axis=-1),)


def _odd_bwd(dxo, sv, i, W, kv, T, sync):
    O = ODD_OFF
    p = sv["p"]
    dof, dmo, dgC, dgM, dgg, dwo = mix_project_bwd("odd_out_bwd", odd_post_tile, T, sv["post_ins"], W["w_out_o"][i],
                                                   dxo, skip=(1,), narrow=(3, 4))
    dqf, dkf, dvf, dr_f, dqb, dkb, dvb, dr_b, dwup_f, dbg_f, dwup_b, dbg_b = scan_bwd(
        "scan_bwd_g", gla_prep, sv["scan_raws"], sv["scan_pars"], sv["ss"], (dof, 0), N_HEADS_C, DK_C, DV_C, T)
    dqf = sync(dqf)
    row = lambda arr, w: ("row", arr, 0, w)
    dqM, dkv = mem_bwd(p, O["qM"], kv, dmo, T)
    pieces = [row(dqf, WK_C), row(dqb, WK_C), row(dkf, WK_C), row(dkb, WK_C), row(dvf, WV_C), row(dvb, WV_C),
              row(dgC, WV_C), row(dqM, W_M), row(dgM, W_M), row(dr_f, 128), row(dr_b, 128)]
    dp, dx, dg = norm_project_bwd("mm_in_o_bwd", _assemble_odd, pieces, W["w_in_o"][i], sv["x"], sv["g"], dxo)
    dwi = matmul("mm_dwi_o", sv["h"], dp, "tn")
    dw_up = jnp.stack([dwup_f[0:GATE_RANK], dwup_b[GATE_RANK:2 * GATE_RANK]])
    dbg = jnp.concatenate([dbg_f, dbg_b], axis=0)
    return dx, dict(w_in=dwi, w_out=dwo, norm=dg[0], w_up=dw_up, b_gate=dbg, gg=dgg[0], kv=dkv)


def local_step(x, mem, target, W, later=None, on_layer_grads=None, sync=lambda a: a):
    T = x.shape[0]
    slopes = jnp.repeat(2.0 ** (-8.0 * jnp.arange(1, N_Q_A + 1, dtype=F32) / N_Q_A), BLOCK).reshape(N_Q_A * BLOCK, 1)
    lower, lower_vjp = jax.vjp(_lower_bounds, W["lb_param"])
    mem_g = W["mem_norm"].reshape(1, D_MODEL)
    (mem_n,) = rows_call("mem_rms_fwd", rms_tile, N_MEM, [("row", mem, 0, D_MODEL), ("full", mem_g)], [D_MODEL], [BF16])
    kvs, saved = [], []
    for l in range(DEPTH):
        if l == 1 and later is not None:
            x, W = later(x, W)
        kvs.append(matmul("mm_kv", mem_n, W["w_kv"][l], "nn"))
        if l % 2 == 0:
            x, sv = _even_fwd(x, l // 2, W, lower, kvs[l], slopes, T)
        else:
            x, sv = _odd_fwd(x, l // 2, W, kvs[l], T)
        saved.append(sv)
    loss, dx, dgf = final_call(x, W["final_norm"].reshape(1, D_MODEL), target, T)
    per = [None] * DEPTH
    dmem_n = None
    for l in reversed(range(DEPTH)):
        if l % 2 == 0:
            dx, per[l] = _even_bwd(dx, saved[l], l // 2, W, kvs[l], slopes, T, sync)
        else:
            dx, per[l] = _odd_bwd(dx, saved[l], l // 2, W, kvs[l], T, sync)
        per[l]["w_kv"] = matmul("mm_dwkv", mem_n, per[l]["kv"], "tn")
        dmem_n = matmul("mm_dmem", per[l]["kv"], W["w_kv"][l], "nt", add=dmem_n)
        if on_layer_grads is not None:
            dx = on_layer_grads(l, dx, per[l])
    dw_kv = [per[l]["w_kv"] for l in range(DEPTH)]
    (dmem_norm,) = rows_vjp_call("mem_rms_bwd", rms_tile, N_MEM, [("row", mem, 0, D_MODEL), ("full", mem_g)],
                                 [[("row", dmem_n, 0, D_MODEL)]], skip=(0,))
    ev, od = (per[0], per[2]), (per[1], per[3])
    (d_lb,) = lower_vjp(jnp.stack([e["low"] for e in ev]))
    grads = dict(
        w_in_e=jnp.stack([e["w_in"] for e in ev]), w_in_o=jnp.stack([o["w_in"] for o in od]),
        w_out_e=jnp.stack([e["w_out"] for e in ev]), w_out_o=jnp.stack([o["w_out"] for o in od]),
        w_kv=jnp.stack(dw_kv), norm_even=jnp.stack([e["norm"] for e in ev]), sink=jnp.stack([e["sink"] for e in ev]),
        lb_param=d_lb, hgrn_norm=jnp.stack([e["hg"] for e in ev]), norm_odd=jnp.stack([o["norm"] for o in od]),
        w_gate_up=jnp.stack([o["w_up"] for o in od]), b_gate=jnp.stack([o["b_gate"] for o in od]),
        gla_norm=jnp.stack([o["gg"] for o in od]), mem_norm=dmem_norm[0], final_norm=dgf[0])
    return loss, dx, grads


SMALL_SPECS = (("lb_param", (2, 2, 128)), ("norm_odd", (2, 256)), ("w_gate_up", (2, 2, 16, 128)),
               ("b_gate", (2, 2, 128)), ("gla_norm", (2, 256)))
SMALL_ROWS = 80


def _pack_small_local(d):
    return jnp.concatenate([d[n].reshape(-1) for n, _ in SMALL_SPECS]).reshape(SMALL_ROWS, 128)


def _unpack_small_local(b):
    flat, out, o = b.reshape(-1), {}, 0
    for n, shp in SMALL_SPECS:
        sz = int(np.prod(shp))
        out[n] = flat[o:o + sz].reshape(shp)
        o += sz
    return out


def _unpack_small_full(g4):
    per = [_unpack_small_local(g4[j]) for j in range(4)]
    return {n: jnp.concatenate([per[j][n] for j in range(4)], axis=-1) for n, _ in SMALL_SPECS}


def _pack_small_blocks(full):
    blocks = []
    for j in range(4):
        blocks.append(_pack_small_local({n: full[n][..., j * shp[-1]:(j + 1) * shp[-1]] for n, shp in SMALL_SPECS}))
    return jnp.stack(blocks)


def _cols(t, order, off, widths):
    return [t[..., off[n]:off[n] + widths[n]] for n in order]


EVEN_REF_ORDER = ("qA", "kA", "vA", "gA", "qB", "zf", "zb", "iB", "gB", "qM", "gM")
ODD_REF_ORDER = ("qC", "kC", "vC", "gC", "rr", "qM", "gM")


def _layer_weights(l, g_in, g_out, g_kv):
    t = g_in.transpose(1, 0, 2).reshape(D_MODEL, -1)
    if l % 2 == 0:
        w_in = jnp.concatenate(_cols(t, EVEN_ORDER, EVEN_REF_OFF, EVEN_W), axis=-1)
    else:
        w_in = jnp.concatenate(_cols(t, ODD_ORDER, ODD_REF_OFF, ODD_W) + [jnp.zeros((D_MODEL, ODD_PAD - ODD_IN), BF16)],
                               axis=-1)
    return w_in, g_out.reshape(MIX, D_MODEL), g_kv.reshape(D_MODEL, 2 * W_M)


def _layer_grad_blocks(l, gl):
    if l % 2 == 0:
        t = jnp.concatenate(_cols(gl["w_in"], EVEN_REF_ORDER, EVEN_OFF, EVEN_W), axis=-1)
    else:
        t = jnp.concatenate(_cols(gl["w_in"], ODD_REF_ORDER, ODD_OFF, ODD_W), axis=-1)
    b_in = t.reshape(D_MODEL, 4, -1).transpose(1, 2, 0)
    return [b_in, gl["w_out"].reshape(4, MIX // 4, D_MODEL), gl["w_kv"].reshape(4, D_MODEL // 4, 2 * W_M)]


WEIGHT_NAMES = ("norm_even", "w_in_even", "sink", "lb_param", "hgrn_norm", "w_out_even", "norm_odd", "w_in_odd",
                "w_gate_up", "b_gate", "gla_norm", "w_out_odd", "mem_norm", "w_mem_kv", "final_norm")


def kernel(x, mem, norm_even, w_in_even, sink, lb_param, hgrn_norm, w_out_even, norm_odd, w_in_odd, w_gate_up, b_gate, gla_norm, w_out_odd, mem_norm, w_mem_kv, final_norm, loss_target, m_norm_even, m_w_in_even, m_sink, m_lb_param, m_hgrn_norm, m_w_out_even, m_norm_odd, m_w_in_odd, m_w_gate_up, m_b_gate, m_gla_norm, m_w_out_odd, m_mem_norm, m_w_mem_kv, m_final_norm, v_norm_even, v_w_in_even, v_sink, v_lb_param, v_hgrn_norm, v_w_out_even, v_norm_odd, v_w_in_odd, v_w_gate_up, v_b_gate, v_gla_norm, v_w_out_odd, v_mem_norm, v_w_mem_kv, v_final_norm):
    w = dict(zip(WEIGHT_NAMES, (norm_even, w_in_even, sink, lb_param, hgrn_norm, w_out_even, norm_odd, w_in_odd,
                                w_gate_up, b_gate, gla_norm, w_out_odd, mem_norm, w_mem_kv, final_norm)))
    m = dict(zip(WEIGHT_NAMES, (m_norm_even, m_w_in_even, m_sink, m_lb_param, m_hgrn_norm, m_w_out_even, m_norm_odd,
                                m_w_in_odd, m_w_gate_up, m_b_gate, m_gla_norm, m_w_out_odd, m_mem_norm, m_w_mem_kv,
                                m_final_norm)))
    v = dict(zip(WEIGHT_NAMES, (v_norm_even, v_w_in_even, v_sink, v_lb_param, v_hgrn_norm, v_w_out_even, v_norm_odd,
                                v_w_in_odd, v_w_gate_up, v_b_gate, v_gla_norm, v_w_out_odd, v_mem_norm, v_w_mem_kv,
                                v_final_norm)))
    ci = lax.axis_index("c").astype(jnp.int32).reshape(1)
    chip = (2 * lax.axis_index("x") + lax.axis_index("y")).astype(jnp.int32).reshape(1)

    shards = []
    for l in range(DEPTH):
        names = ("w_in_even", "w_out_even") if l % 2 == 0 else ("w_in_odd", "w_out_odd")
        shards.append([w[names[0]][l // 2].astype(BF16), w[names[1]][l // 2].astype(BF16), w_mem_kv[l].astype(BF16)])
    small = _pack_small_local(w)
    own = lambda g, s: lax.dynamic_update_slice(g, s[None], (chip[0], 0, 0))
    first = [own(g, s) for g, s in zip(gather_weights(shards[0], small), shards[0] + [small])]
    later_shards = shards[1] + shards[2] + shards[3]
    later_raw = gather_weights_async(later_shards)
    w0 = _layer_weights(0, *first[0:3])
    W = dict(w_in_e=[w0[0]], w_out_e=[w0[1]], w_kv=[w0[2]])
    W.update(_unpack_small_full(first[3]))
    W.update({n: w[n] for n in ("norm_even", "sink", "hgrn_norm", "mem_norm", "final_norm")})

    def later(x1, W):
        x1, raw = lax.optimization_barrier((x1, list(later_raw)))
        g = [own(a, s) for a, s in zip(raw, later_shards)]
        w1, w2, w3 = (_layer_weights(l, *g[3 * (l - 1):3 * l]) for l in (1, 2, 3))
        W = dict(W)
        W.update(w_in_e=[w0[0], w2[0]], w_in_o=[w1[0], w3[0]], w_out_e=[w0[1], w2[1]], w_out_o=[w1[1], w3[1]],
                 w_kv=[w0[2], w1[2], w2[2], w3[2]])
        return x1, W

    place = jnp.concatenate([chip, ci])

    def start(tag, blocks, wire):
        axes = [2 if b.shape[1] == ODD_IN // 4 else 1 for b in blocks]
        return dict(tag=tag, blocks=blocks, wire=wire, step=0,
                    recv=exchange_siblings(f"rs_siblings_{tag}", blocks, axes, 2))

    def advance(p):
        if p["step"] == 0:
            sums = add_sibling(p["blocks"], p["recv"], ci, p["wire"])
            p["recv3"] = exchange_chips(f"rs_chips_{p['tag']}", sums, 3)
        else:
            p["mine"] = add_chips(p["blocks"], p["recv"], p["recv3"], place)
            p["other"] = exchange_siblings(f"rs_final_{p['tag']}", p["mine"], [None] * len(p["mine"]), 4)
        p["step"] += 1

    pipes, first_layer = [], {}

    def sync(a):
        for p in pipes:
            if p["step"] < 3:
                key = ("recv", "recv3", "other")[p["step"]]
                a, arrived = lax.optimization_barrier((a, list(p[key])))
                p[key] = arrived
                if p["step"] < 2:
                    advance(p)
                else:
                    p["step"] = 3
        return a

    def on_layer_grads(l, dx, gl):
        dx = sync(dx)
        if l == 0:
            first_layer.update(gl)
        else:
            pipes.append(start(f"l{l}", _layer_grad_blocks(l, gl), [BF16] * 3))
        return dx

    loss_tile, dx, grads = local_step(x[0], mem[0], loss_target[0], W, later, on_layer_grads, sync)
    last = start("l0", _layer_grad_blocks(0, first_layer) + [_pack_small_blocks(grads)], [BF16] * 3 + [F32])
    for p in pipes + [last]:
        while p["step"] < (1 if p is last else 2):
            advance(p)
    by_layer = {int(p["tag"][1:]): p for p in pipes + [last]}
    halves = lambda layers, k: (jnp.stack([by_layer[l]["mine"][k] for l in layers]),
                                jnp.stack([by_layer[l]["other"][k] for l in layers]))
    gl, upd = {}, {}

    pack = jnp.zeros((8, D_MODEL), F32)
    pack = pack.at[0:2].set(grads["norm_even"]).at[2].set(grads["hgrn_norm"].reshape(-1))
    pack = pack.at[3].set(grads["mem_norm"]).at[4].set(grads["final_norm"])
    pack = pack.at[5, 0:16].set(grads["sink"].reshape(-1)).at[5, 16].set(loss_tile[0, 0])
    tot = sum_devices(allgather_small(pack))
    gl.update(norm_even=tot[0:2], hgrn_norm=tot[2].reshape(2, W_B), mem_norm=tot[3], final_norm=tot[4],
              sink=tot[5, 0:16].reshape(2, N_Q_A))
    loss = tot[5, 16]
    for n in ("norm_even", "hgrn_norm", "mem_norm", "final_norm", "sink"):
        upd[n] = adamw_call(w[n], gl[n], m[n], v[n])
    tr_ = lambda a: jnp.swapaxes(a, 1, 2)
    gl["w_in_odd"], *upd["w_in_odd"] = [tr_(o) for o in adamw_halves(
        tr_(w["w_in_odd"]), *halves((1, 3), 0), tr_(m["w_in_odd"]), tr_(v["w_in_odd"]), ci)]
    gl["w_out_odd"], *upd["w_out_odd"] = adamw_halves(w["w_out_odd"], *halves((1, 3), 1), m["w_out_odd"],
                                                      v["w_out_odd"], ci)
    early = [upd[n] for n in sorted(upd)] + [gl["w_in_odd"], gl["w_out_odd"]]
    last["recv3"], early = lax.optimization_barrier((list(last["recv3"]), early))
    for n, res in zip(sorted(upd), early):
        upd[n] = res
    gl["w_in_odd"], gl["w_out_odd"] = early[-2:]
    advance(last)

    big = dict(w_in_even=halves((0, 2), 0), w_out_even=halves((0, 2), 1), w_mem_kv=halves((0, 1, 2, 3), 2))
    s_mine, s_other = last["mine"][3], last["other"][3]
    g_small = jnp.where(ci[0] == 0, jnp.concatenate([s_mine, s_other]), jnp.concatenate([s_other, s_mine]))
    gl.update(_unpack_small_local(g_small))
    for n in WEIGHT_NAMES:
        if n == "w_in_even":
            gl[n], *upd[n] = [tr_(o) for o in adamw_halves(tr_(w[n]), *big[n], tr_(m[n]), tr_(v[n]), ci)]
        elif n in big:
            gl[n], *upd[n] = adamw_halves(w[n], *big[n], m[n], v[n], ci)
        elif n not in upd:
            upd[n] = adamw_call(w[n], gl[n], m[n], v[n])
    return (loss, dx[None], *[gl[n] for n in WEIGHT_NAMES], *[upd[n][0] for n in WEIGHT_NAMES],
            *[upd[n][1] for n in WEIGHT_NAMES], *[upd[n][2] for n in WEIGHT_NAMES])
```

```python
import functools

import numpy as np
import jax
import jax.numpy as jnp
from jax import lax
from jax.experimental import pallas as pl
from jax.experimental.pallas import tpu as pltpu
from jax.experimental.pallas import tpu_sc as plsc

F32 = jnp.float32
BF16 = jnp.bfloat16

D_MODEL = 1024
DEPTH = 4
N_Q_A, N_KV_A, HEAD_DIM_A = 8, 2, 64
W_A, W_KV_A = 512, 128
WINDOW = 128
BLOCK = 128
N_HEADS_B, HEAD_DIM_B, W_B = 4, 128, 512
N_HEADS_C, DK_C, DV_C, WK_C, WV_C = 4, 128, 256, 512, 1024
GATE_RANK = 16
GATE_TEMP = 16.0
N_MEM, N_HEADS_M, HEAD_DIM_M, W_M = 256, 4, 128, 512
EPS = 1e-6
MASK_VALUE = -1e30
MIN_GATE = 1e-30
EVEN_IN, ODD_IN = 4864, 4128
ODD_PAD = 4224
MIX = 1536
ADAM_LR, ADAM_B1, ADAM_B2, ADAM_EPS, ADAM_WD, ADAM_STEP = 0.001, 0.9, 0.999, 1e-08, 0.01, 10

SCAN_CHUNK = 128
SCAN_SUB = 2
SCAN_LEVELS = 7
VMEM_LIMIT = 56 * 1024 * 1024

EVEN_REF_OFF = dict(qA=0, kA=512, vA=640, gA=768, qB=1280, zf=1792, zb=2304, iB=2816, gB=3328, qM=3840, gM=4352)
EVEN_W = dict(qA=512, kA=128, vA=128, gA=512, qB=512, zf=512, zb=512, iB=512, gB=512, qM=512, gM=512)
EVEN_ORDER = ("qA", "gA", "qB", "zf", "zb", "iB", "gB", "qM", "gM", "kA", "vA")
ODD_REF_OFF = dict(qC=0, kC=512, vC=1024, gC=2048, rr=3072, qM=3104, gM=3616)
ODD_W = dict(qC=512, kC=512, vC=1024, gC=1024, rr=32, qM=512, gM=512)
ODD_ORDER = ("qC", "kC", "vC", "gC", "qM", "gM", "rr")


def _offsets(order, widths):
    off, o = {}, 0
    for n in order:
        off[n] = o
        o += widths[n]
    return off


EVEN_OFF = _offsets(EVEN_ORDER, EVEN_W)
ODD_OFF = _offsets(ODD_ORDER, ODD_W)


def _dg(a, b, ca, cb):
    return lax.dot_general(a.astype(BF16), b.astype(BF16), (((ca,), (cb,)), ((), ())),
                           preferred_element_type=F32)


def dot_nn(a, b):
    return _dg(a, b, 1, 0)


def dot_nt(a, b):
    return _dg(a, b, 1, 1)


def dot_tn(a, b):
    return _dg(a, b, 0, 0)


@jax.custom_vjp
def bdot(a, b):
    return dot_nn(a, b)


bdot.defvjp(lambda a, b: (dot_nn(a, b), (a, b)),
            lambda r, g: (dot_nt(g, r[1]), dot_tn(r[0], g)))


@jax.custom_vjp
def bdot_t(a, b):
    return dot_nt(a, b)


bdot_t.defvjp(lambda a, b: (dot_nt(a, b), (a, b)),
              lambda r, g: (dot_nn(g, r[1]), dot_tn(g, r[0])))


@jax.custom_vjp
def bdot_tn(a, b):
    return dot_tn(a, b)


bdot_tn.defvjp(lambda a, b: (dot_tn(a, b), (a, b)),
               lambda r, g: (dot_nt(r[1], g), dot_nn(r[0], g)))


def _split_mm(h, x):
    hi = x.astype(BF16)
    lo = (x - hi.astype(F32)).astype(BF16)
    return (lax.dot_general(h, hi, (((1,), (0,)), ((), ())), preferred_element_type=F32)
            + lax.dot_general(h, lo, (((1,), (0,)), ((), ())), preferred_element_type=F32))


def _sigmoid(z):
    return 1.0 / (1.0 + jnp.exp(-z))


def _silu(z):
    return z * _sigmoid(z)


def _log_sigmoid(z):
    return jnp.minimum(z, 0.0) - jnp.log(1.0 + jnp.exp(-jnp.abs(z)))


def _rms(x, g):
    return x * lax.rsqrt(jnp.mean(x * x, axis=-1, keepdims=True) + EPS) * g


def rms_tile(x, g):
    return (_rms(x, g),)


@functools.partial(jax.custom_vjp, nondiff_argnums=(1, 2))
def split(x, n, axis):
    w = x.shape[axis] // n
    return tuple(lax.slice_in_dim(x, h * w, (h + 1) * w, axis=axis) for h in range(n))


split.defvjp(lambda x, n, axis: (split(x, n, axis), None),
             lambda n, axis, _, cts: (jnp.concatenate(cts, axis=axis),))


def _group_rms(o, g, heads):
    return jnp.concatenate([_rms(oh, gh) for oh, gh in zip(split(o, heads, 1), split(g, heads, 1))], axis=-1)


def even_post_tile(a, o2f, o2b, mo, gA, gB, gM, hg):
    y = _group_rms(o2f + o2b, hg, N_HEADS_B)
    return (jnp.concatenate([a * _silu(gA), y * _silu(gB), mo * _silu(gM)], axis=-1),)


def odd_post_tile(o2f, o2b, mo, gC, gM, gg):
    y = _group_rms(o2f + o2b, gg, N_HEADS_C)
    return (jnp.concatenate([y * _silu(gC), mo * _silu(gM)], axis=-1),)


def hgrn_prep(raw, par):
    qB, z, iB = raw
    (lb,) = par
    f = lb + (1.0 - lb) * _sigmoid(z)
    return _silu(qB), (1.0 - lb) * _sigmoid(-z), iB, jnp.log(jnp.maximum(f, MIN_GATE))


def gla_prep(raw, par):
    qC, kC, vC, r128 = raw
    wup, bg = par
    return qC * (DK_C ** -0.5), kC, vC, _log_sigmoid(bdot(r128, wup) + bg) / GATE_TEMP


def mem_tile(q, k, v):
    s = bdot_t(q, k) * (HEAD_DIM_M ** -0.5)
    m = lax.stop_gradient(jnp.max(s, axis=-1, keepdims=True))
    p = jnp.exp(s - m)
    p = p / jnp.sum(p, axis=-1, keepdims=True)
    return (bdot(p, v),)


ATTN_GROUP = N_Q_A // N_KV_A


def attn_block(q, ks, vs, sink, slope, c, seq):
    rows = ATTN_GROUP * BLOCK
    i = lax.broadcasted_iota(jnp.int32, (rows, 3 * BLOCK), 0) % BLOCK
    j = lax.broadcasted_iota(jnp.int32, (rows, 3 * BLOCK), 1)
    dist = jnp.abs(i - j + BLOCK).astype(F32)
    kpos = (c - 1) * BLOCK + j
    valid = (dist <= WINDOW) & (kpos >= 0) & (kpos < seq)
    s = bdot_t(q, ks) * (HEAD_DIM_A ** -0.5)
    s = jnp.where(valid, s - slope * dist, MASK_VALUE)
    m = lax.stop_gradient(jnp.maximum(jnp.max(s, axis=-1, keepdims=True), sink))
    p = jnp.where(valid, jnp.exp(s - m), 0.0)
    denom = jnp.sum(p, axis=-1, keepdims=True) + jnp.exp(sink - m)
    return bdot(p, vs) / denom


def scan_chunk(q, k, v, e, tot, st, qm, pm):
    C = SCAN_CHUNK
    e = split(e, 2 + SCAN_LEVELS, 0)
    qe = q * jnp.exp(e[0])
    kd = k * jnp.exp(e[1])
    r = lax.broadcasted_iota(jnp.int32, (C, C), 0)
    s = lax.broadcasted_iota(jnp.int32, (C, C), 1)
    a = jnp.where(r == s, jnp.sum(q * k, axis=-1, keepdims=True), 0.0)
    for l in range(SCAN_LEVELS):
        u = jnp.where(qm[l * C:(l + 1) * C] != 0.0, q, k) * jnp.exp(e[2 + l])
        a = a + bdot_t(u, u) * pm[l * C:(l + 1) * C]
    o = bdot_t(qe, st) + bdot(a, v)
    st_new = st * jnp.exp(tot) + bdot_tn(v, kd)
    return o, st_new


def _scan_consts():
    C, L = SCAN_CHUNK, SCAN_LEVELS
    t = np.arange(C)[:, None]
    r = np.arange(C)[None, :]
    blocks = [(r <= t), (r > t)]
    qms, pms = [], []
    for l in range(1, L + 1):
        m = C >> l
        upper_t = (t % (2 * m)) >= m
        upper_r = (r % (2 * m)) >= m
        same_half = (t // m) == (r // m)
        blocks.append(same_half & np.where(upper_t, r <= t, r > t))
        qms.append(np.broadcast_to(upper_t, (C, C)))
        pms.append(((t // (2 * m)) == (r // (2 * m))) & upper_t & ~upper_r)
    hf = np.concatenate(blocks, axis=0).astype(np.float32)
    flip = lambda mat: mat.reshape(-1, C, C)[:, ::-1, ::-1].reshape(-1, C)
    qmf = np.concatenate(qms, axis=0).astype(np.float32)
    pmf = np.concatenate(pms, axis=0).astype(np.float32)
    h = np.stack([hf, flip(hf)])
    ht = np.stack([h[0].T, h[1].T])
    qm = np.stack([qmf, 1.0 - qmf])
    pm = np.stack([pmf, flip(pmf)])
    return h, ht, qm, pm


def _cparams(sem):
    return pltpu.CompilerParams(dimension_semantics=sem, vmem_limit_bytes=VMEM_LIMIT)


def _row_tile(T):
    return min(T, 512)


def _in_spec(spec, tr):
    kind = spec[0]
    if kind == "row":
        _, arr, off, w = spec
        assert off % w == 0
        return arr, pl.BlockSpec((tr, w), functools.partial(lambda i, b: (i, b), b=off // w))
    if kind == "row3":
        _, arr, d, off, w = spec
        assert off % w == 0
        return arr, pl.BlockSpec((None, tr, w), functools.partial(lambda i, d, b: (d, i, b), d=d, b=off // w))
    _, arr = spec
    return arr, pl.BlockSpec(arr.shape, functools.partial(lambda i, n: (0,) * n, n=arr.ndim))


def rows_call(name, tile_fn, T, ins, out_widths, out_dtypes=None, stacks=None):
    tr = _row_tile(T)
    n_in = len(ins)
    out_dtypes = out_dtypes or [F32] * len(out_widths)
    stacks = stacks or [(k,) for k in range(len(out_widths))]

    def body(*refs):
        vals = [r[...] for r in refs[:n_in]]
        outs = tile_fn(*vals)
        for r, members in zip(refs[n_in:], stacks):
            if len(members) == 1:
                r[...] = outs[members[0]].astype(r.dtype)
            else:
                for d, k in enumerate(members):
                    r[d] = outs[k].astype(r.dtype)

    in_specs, args = [], []
    for spec in ins:
        arr, bs = _in_spec(spec, tr)
        args.append(arr)
        in_specs.append(bs)
    out_specs, out_shape = [], []
    for w, dt, members in zip(out_widths, out_dtypes, stacks):
        n = len(members)
        if n == 1:
            out_specs.append(pl.BlockSpec((tr, w), lambda i: (i, 0)))
            out_shape.append(jax.ShapeDtypeStruct((T, w), dt))
        else:
            out_specs.append(pl.BlockSpec((n, tr, w), lambda i: (0, i, 0)))
            out_shape.append(jax.ShapeDtypeStruct((n, T, w), dt))
    return pl.pallas_call(body, out_shape=out_shape, grid=(T // tr,), in_specs=in_specs, out_specs=out_specs,
                          name=name, compiler_params=_cparams(("arbitrary",)))(*args)


def rows_vjp_call(name, tile_fn, T, ins, cts, skip=(), narrow=()):
    tr = _row_tile(T)
    n_in = len(ins)
    n_ct = [len(c) for c in cts]
    want = [k for k in range(n_in) if k not in skip]

    def body(*refs):
        i = pl.program_id(0)
        vals = [r[...] for r in refs[:n_in]]
        ct, pos = [], n_in
        for n in n_ct:
            acc = refs[pos][...]
            for r in refs[pos + 1:pos + n]:
                acc = acc + r[...]
            ct.append(acc)
            pos += n
        _, vjp = jax.vjp(tile_fn, *vals)
        grads = vjp(tuple(ct))
        for r, k in zip(refs[pos:], want):
            if ins[k][0] == "full":
                @pl.when(i == 0)
                def _():
                    r[...] = jnp.zeros_like(r)
                r[...] += grads[k]
            else:
                r[...] = grads[k].astype(r.dtype)

    in_specs, args = [], []
    for spec in list(ins) + [s for c in cts for s in c]:
        arr, bs = _in_spec(spec, tr)
        args.append(arr)
        in_specs.append(bs)
    out_specs, out_shape = [], []
    for k in want:
        if ins[k][0] == "full":
            arr = ins[k][1]
            out_specs.append(pl.BlockSpec(arr.shape, functools.partial(lambda i, n: (0,) * n, n=arr.ndim)))
            out_shape.append(jax.ShapeDtypeStruct(arr.shape, F32))
        else:
            w = ins[k][-1]
            out_specs.append(pl.BlockSpec((tr, w), lambda i: (i, 0)))
            out_shape.append(jax.ShapeDtypeStruct((T, w), BF16 if k in narrow else F32))
    return pl.pallas_call(body, out_shape=out_shape, grid=(T // tr,), in_specs=in_specs, out_specs=out_specs,
                          name=name, compiler_params=_cparams(("arbitrary",)))(*args)


def matmul(name, a, b, mode, add=None, out_dtype=F32):
    if mode == "tn":
        K, M = a.shape
        N = b.shape[1]
        tm = M if M <= 1536 else 512
        tn = N if N <= 1280 else (N // 2 if (N // 2) % 128 == 0 else N)
        tk = min(K, 512)
        grid = (M // tm, N // tn, K // tk)

        def body(a_ref, b_ref, o_ref):
            @pl.when(pl.program_id(2) == 0)
            def _():
                o_ref[...] = jnp.zeros_like(o_ref)
            o_ref[...] += dot_tn(a_ref[...], b_ref[...])

        return pl.pallas_call(
            body, out_shape=jax.ShapeDtypeStruct((M, N), F32), grid=grid,
            in_specs=[pl.BlockSpec((tk, tm), lambda i, j, k: (k, i)), pl.BlockSpec((tk, tn), lambda i, j, k: (k, j))],
            out_specs=pl.BlockSpec((tm, tn), lambda i, j, k: (i, j)), name=name,
            compiler_params=_cparams(("arbitrary", "arbitrary", "arbitrary")))(a, b)

    M, K = a.shape
    N = b.shape[1] if mode == "nn" else b.shape[0]
    tm = min(M, 512)
    tn = N if N <= 1536 else (N // 2 if (N // 2) % 128 == 0 else (N // 3 if (N // 3) % 128 == 0 else N))
    grid = (N // tn, M // tm)
    n_in = 2 + (add is not None)

    def body(*refs):
        a_ref, b_ref = refs[0], refs[1]
        o_ref = refs[n_in]
        acc = dot_nn(a_ref[...], b_ref[...]) if mode == "nn" else dot_nt(a_ref[...], b_ref[...])
        if add is not None:
            acc = acc + refs[2][...]
        o_ref[...] = acc.astype(o_ref.dtype)

    in_specs = [pl.BlockSpec((tm, K), lambda j, i: (i, 0)),
                pl.BlockSpec((K, tn), lambda j, i: (0, j)) if mode == "nn" else pl.BlockSpec((tn, K), lambda j, i: (j, 0))]
    args = [a, b]
    if add is not None:
        in_specs.append(pl.BlockSpec((tm, tn), lambda j, i: (i, j)))
        args.append(add)
    return pl.pallas_call(
        body, out_shape=jax.ShapeDtypeStruct((M, N), out_dtype), grid=grid, in_specs=in_specs,
        out_specs=pl.BlockSpec((tm, tn), lambda j, i: (i, j)), name=name,
        compiler_params=_cparams(("arbitrary", "arbitrary")))(*args)


def norm_project(name, x, g, w):
    T, D = x.shape
    N = w.shape[1]
    tm = min(T, 512)

    def body(x_ref, g_ref, w_ref, h_ref, p_ref):
        h = _rms(x_ref[...], g_ref[...]).astype(BF16)
        h_ref[...] = h
        p_ref[...] = dot_nn(h, w_ref[...])

    return pl.pallas_call(
        body, out_shape=[jax.ShapeDtypeStruct((T, D), BF16), jax.ShapeDtypeStruct((T, N), F32)], grid=(T // tm,),
        in_specs=[pl.BlockSpec((tm, D), lambda i: (i, 0)), pl.BlockSpec((1, D), lambda i: (0, 0)),
                  pl.BlockSpec((D, N), lambda i: (0, 0))],
        out_specs=[pl.BlockSpec((tm, D), lambda i: (i, 0)), pl.BlockSpec((tm, N), lambda i: (i, 0))],
        name=name, compiler_params=_cparams(("arbitrary",)))(x, g, w)


def norm_project_bwd(name, assemble, pieces, w, x, g, dy):
    T, D = x.shape
    N = w.shape[1]
    tm = min(T, 256)
    n_in = len(pieces)

    def body(*refs):
        w_ref, x_ref, g_ref, dy_ref, dp_ref, dx_ref, dg_ref = refs[n_in:]

        @pl.when(pl.program_id(0) == 0)
        def _():
            dg_ref[...] = jnp.zeros_like(dg_ref)

        (dp,) = assemble(*[r[...] for r in refs[:n_in]])
        dp_ref[...] = dp
        _, vjp = jax.vjp(_rms, x_ref[...], g_ref[...])
        dx, dg = vjp(dot_nt(dp, w_ref[...]))
        dx_ref[...] = dx + dy_ref[...]
        dg_ref[...] += dg

    in_specs, args = [], []
    for spec in pieces:
        arr, bs = _in_spec(spec, tm)
        args.append(arr)
        in_specs.append(bs)
    row = pl.BlockSpec((tm, D), lambda i: (i, 0))
    vec = pl.BlockSpec((1, D), lambda i: (0, 0))
    wide = pl.BlockSpec((tm, N), lambda i: (i, 0))
    return pl.pallas_call(
        body,
        out_shape=[jax.ShapeDtypeStruct((T, N), BF16), jax.ShapeDtypeStruct((T, D), F32), jax.ShapeDtypeStruct((1, D), F32)],
        grid=(T // tm,), in_specs=in_specs + [pl.BlockSpec((D, N), lambda i: (0, 0)), row, vec, row],
        out_specs=[wide, row, vec], name=name, compiler_params=_cparams(("arbitrary",)))(*args, w, x, g, dy)


def mix_project(name, tile_fn, T, ins, w, x):
    tr = _row_tile(T)
    n_in = len(ins)
    K, D = w.shape

    def body(*refs):
        w_ref, x_ref, y_ref = refs[n_in:]
        (mix,) = tile_fn(*[r[...] for r in refs[:n_in]])
        y_ref[...] = x_ref[...] + dot_nn(mix, w_ref[...])

    in_specs, args = [], []
    for spec in ins:
        arr, bs = _in_spec(spec, tr)
        args.append(arr)
        in_specs.append(bs)
    row = pl.BlockSpec((tr, D), lambda i: (i, 0))
    return pl.pallas_call(
        body, out_shape=jax.ShapeDtypeStruct((T, D), F32), grid=(T // tr,),
        in_specs=in_specs + [pl.BlockSpec((K, D), lambda i: (0, 0)), row], out_specs=row,
        name=name, compiler_params=_cparams(("arbitrary",)))(*args, w, x)


def mix_project_bwd(name, tile_fn, T, ins, w, dy, skip=(), narrow=()):
    tr = _row_tile(T)
    n_in = len(ins)
    K, D = w.shape
    want = [k for k in range(n_in) if k not in skip]

    def body(*refs):
        w_ref, dy_ref = refs[n_in:n_in + 2]
        outs, dw_ref = refs[n_in + 2:-1], refs[-1]
        first = pl.program_id(0) == 0
        (mix,), vjp = jax.vjp(tile_fn, *[r[...] for r in refs[:n_in]])
        d = dy_ref[...].astype(BF16)
        grads = vjp((dot_nt(d, w_ref[...]),))

        @pl.when(first)
        def _():
            dw_ref[...] = jnp.zeros_like(dw_ref)

        dw_ref[...] += dot_tn(mix, d)
        for r, k in zip(outs, want):
            if ins[k][0] == "full":
                @pl.when(first)
                def _():
                    r[...] = jnp.zeros_like(r)
                r[...] += grads[k]
            else:
                r[...] = grads[k].astype(r.dtype)

    in_specs, args = [], []
    for spec in ins:
        arr, bs = _in_spec(spec, tr)
        args.append(arr)
        in_specs.append(bs)
    out_specs, out_shape = [], []
    for k in want:
        if ins[k][0] == "full":
            arr = ins[k][1]
            out_specs.append(_full_spec(arr))
            out_shape.append(jax.ShapeDtypeStruct(arr.shape, F32))
        else:
            wd = ins[k][-1]
            out_specs.append(pl.BlockSpec((tr, wd), lambda i: (i, 0)))
            out_shape.append(jax.ShapeDtypeStruct((T, wd), BF16 if k in narrow else F32))
    wspec = pl.BlockSpec((K, D), lambda i: (0, 0))
    return pl.pallas_call(
        body, out_shape=out_shape + [jax.ShapeDtypeStruct((K, D), F32)], grid=(T // tr,),
        in_specs=in_specs + [wspec, pl.BlockSpec((tr, D), lambda i: (i, 0))], out_specs=out_specs + [wspec],
        name=name, compiler_params=_cparams(("arbitrary",)))(*args, w, dy)


def _attn_heads(n):
    G = N_Q_A // N_KV_A
    k_sl = pl.ds(n * HEAD_DIM_A, HEAD_DIM_A)
    v_sl = pl.ds(W_KV_A + n * HEAD_DIM_A, HEAD_DIM_A)
    q_sl = [pl.ds((n * G + g) * HEAD_DIM_A, HEAD_DIM_A) for g in range(G)]
    return k_sl, v_sl, q_sl, range(n * G, (n + 1) * G)


def attn_fwd(p, q_off, kvp, sink, slopes, T):
    nb = T // BLOCK
    assert q_off % W_A == 0

    def body(q_ref, kv_ref, sink_ref, slope_ref, o_ref):
        c = pl.program_id(0)
        rows = pl.ds(pl.multiple_of(c * BLOCK, BLOCK), 3 * BLOCK)
        for n in range(N_KV_A):
            k_sl, v_sl, q_sl, heads = _attn_heads(n)
            group = pl.ds(n * ATTN_GROUP * BLOCK, ATTN_GROUP * BLOCK)
            q = jnp.concatenate([q_ref[:, s] for s in q_sl], axis=0)
            o = attn_block(q, kv_ref[rows, k_sl], kv_ref[rows, v_sl], sink_ref[group, :], slope_ref[group, :], c, T)
            for g, s in enumerate(q_sl):
                o_ref[:, s] = o[g * BLOCK:(g + 1) * BLOCK]

    full = lambda a: pl.BlockSpec(a.shape, functools.partial(lambda c, nd: (0,) * nd, nd=a.ndim))
    return pl.pallas_call(
        body, out_shape=jax.ShapeDtypeStruct((T, W_A), F32), grid=(nb,),
        in_specs=[pl.BlockSpec((BLOCK, W_A), lambda c: (c, q_off // W_A)), full(kvp), full(sink), full(slopes)],
        out_specs=pl.BlockSpec((BLOCK, W_A), lambda c: (c, 0)),
        name="attn_fwd", compiler_params=_cparams(("arbitrary",)))(p, kvp, sink, slopes)


def attn_bwd(p, q_off, kvp, sink, slopes, do, T):
    nb = T // BLOCK

    def body(q_ref, kv_ref, sink_ref, slope_ref, do_ref, dq_ref, dkv_ref, dsink_ref):
        c = pl.program_id(0)

        @pl.when(c == 0)
        def _():
            dkv_ref[...] = jnp.zeros_like(dkv_ref)
            dsink_ref[...] = jnp.zeros_like(dsink_ref)

        rows = pl.ds(pl.multiple_of(c * BLOCK, BLOCK), 3 * BLOCK)
        for n in range(N_KV_A):
            k_sl, v_sl, q_sl, heads = _attn_heads(n)
            group = pl.ds(n * ATTN_GROUP * BLOCK, ATTN_GROUP * BLOCK)
            slope = slope_ref[group, :]
            q = jnp.concatenate([q_ref[:, s] for s in q_sl], axis=0)
            do = jnp.concatenate([do_ref[:, s] for s in q_sl], axis=0)
            _, vjp = jax.vjp(lambda q_, kk, vv, sk: attn_block(q_, kk, vv, sk, slope, c, T),
                             q, kv_ref[rows, k_sl], kv_ref[rows, v_sl], sink_ref[group, :])
            dq, dks, dvs, dsk = vjp(do)
            dkv_ref[rows, k_sl] += dks
            dkv_ref[rows, v_sl] += dvs
            for g, (s, h) in enumerate(zip(q_sl, heads)):
                seg = slice(g * BLOCK, (g + 1) * BLOCK)
                dq_ref[:, s] = dq[seg].astype(dq_ref.dtype)
                dsink_ref[h] += jnp.sum(dsk[seg], axis=0, keepdims=True)

    full = lambda a: pl.BlockSpec(a.shape, functools.partial(lambda c, nd: (0,) * nd, nd=a.ndim))
    qspec = pl.BlockSpec((BLOCK, W_A), lambda c: (c, 0))
    return pl.pallas_call(
        body,
        out_shape=[jax.ShapeDtypeStruct((T, W_A), BF16), jax.ShapeDtypeStruct(kvp.shape, F32),
                   jax.ShapeDtypeStruct((N_Q_A, 1, 1), F32)],
        grid=(nb,),
        in_specs=[pl.BlockSpec((BLOCK, W_A), lambda c: (c, q_off // W_A)), full(kvp), full(sink), full(slopes), qspec],
        out_specs=[qspec, full(kvp), pl.BlockSpec((N_Q_A, 1, 1), lambda c: (0, 0, 0))],
        name="attn_bwd", compiler_params=_cparams(("arbitrary",)))(p, kvp, sink, slopes, do)


def mem_fwd(p, q_off, kv, T):
    tr = min(T, 2 * _row_tile(T))
    assert q_off % W_M == 0

    def body(q_ref, kv_ref, o_ref):
        for h in range(N_HEADS_M):
            hs = pl.ds(h * HEAD_DIM_M, HEAD_DIM_M)
            (o,) = mem_tile(q_ref[:, hs], kv_ref[:, hs], kv_ref[:, pl.ds(W_M + h * HEAD_DIM_M, HEAD_DIM_M)])
            o_ref[:, hs] = o

    return pl.pallas_call(
        body, out_shape=jax.ShapeDtypeStruct((T, W_M), F32), grid=(T // tr,),
        in_specs=[pl.BlockSpec((tr, W_M), lambda i: (i, q_off // W_M)), pl.BlockSpec((N_MEM, 2 * W_M), lambda i: (0, 0))],
        out_specs=pl.BlockSpec((tr, W_M), lambda i: (i, 0)),
        name="mem_fwd", compiler_params=_cparams(("arbitrary",)))(p, kv)


def mem_bwd(p, q_off, kv, do, T):
    tr = min(T, 2 * _row_tile(T))

    def body(q_ref, kv_ref, do_ref, dq_ref, dkv_ref):
        @pl.when(pl.program_id(0) == 0)
        def _():
            dkv_ref[...] = jnp.zeros_like(dkv_ref)

        for h in range(N_HEADS_M):
            hs = pl.ds(h * HEAD_DIM_M, HEAD_DIM_M)
            vs = pl.ds(W_M + h * HEAD_DIM_M, HEAD_DIM_M)
            _, vjp = jax.vjp(mem_tile, q_ref[:, hs], kv_ref[:, hs], kv_ref[:, vs])
            dq, dk, dv = vjp((do_ref[:, hs],))
            dq_ref[:, hs] = dq.astype(dq_ref.dtype)
            dkv_ref[:, hs] += dk
            dkv_ref[:, vs] += dv

    kvspec = pl.BlockSpec((N_MEM, 2 * W_M), lambda i: (0, 0))
    return pl.pallas_call(
        body,
        out_shape=[jax.ShapeDtypeStruct((T, W_M), BF16), jax.ShapeDtypeStruct((N_MEM, 2 * W_M), F32)],
        grid=(T // tr,),
        in_specs=[pl.BlockSpec((tr, W_M), lambda i: (i, q_off // W_M)), kvspec, pl.BlockSpec((tr, W_M), lambda i: (i, 0))],
        out_specs=[pl.BlockSpec((tr, W_M), lambda i: (i, 0)), kvspec],
        name="mem_bwd", compiler_params=_cparams(("arbitrary",)))(p, kv, do)


def _scan_const_specs(dk):
    C, L = SCAN_CHUNK, SCAN_LEVELS
    return [pl.BlockSpec((2, (2 + L) * C, C), lambda n: (0, 0, 0)),
            pl.BlockSpec((2, C, (2 + L) * C), lambda n: (0, 0, 0)),
            pl.BlockSpec((2, L * C, dk), lambda n: (0, 0, 0)),
            pl.BlockSpec((2, L * C, C), lambda n: (0, 0, 0))]


def _chunk_spec(src, width, chunk_of):
    arr, sel = src
    if arr.ndim == 2:
        assert sel % width == 0
        return pl.BlockSpec((SCAN_CHUNK * SCAN_SUB, width),
                            functools.partial(lambda n, b: (chunk_of(n), b), b=sel // width))
    return pl.BlockSpec((None, SCAN_CHUNK * SCAN_SUB, width), functools.partial(lambda n, d: (d, chunk_of(n), 0), d=sel))


def _scan_const_args():
    h, ht, qm, pm = _scan_consts()
    return [jnp.asarray(h, BF16), jnp.asarray(ht, BF16), jnp.asarray(qm, F32), jnp.asarray(pm, F32)]


def _full_spec(a):
    return pl.BlockSpec(a.shape, functools.partial(lambda n, nd: (0,) * nd, nd=a.ndim))


def scan_fwd(name, prep, raws, params, heads, dk, dv, T):
    C, S = SCAN_CHUNK, SCAN_SUB
    N = T // (C * S)
    assert dk == C
    Wv = heads * dv
    orders = (lambda n: n, lambda n: N - 1 - n)
    n_raw, n_par = [len(r) for r in raws], [len(p) for p in params]

    def body(*refs):
        pos, raw_refs, par_refs = 0, [], []
        for d in range(2):
            raw_refs.append(refs[pos:pos + n_raw[d]])
            pos += n_raw[d]
        for d in range(2):
            par_refs.append(refs[pos:pos + n_par[d]])
            pos += n_par[d]
        h_ref, ht_ref, qm_ref, pm_ref = refs[pos:pos + 4]
        o_refs, ss_refs, st_ref = refs[pos + 4:pos + 6], refs[pos + 6:pos + 8], refs[pos + 8]

        @pl.when(pl.program_id(0) == 0)
        def _():
            st_ref[...] = jnp.zeros_like(st_ref)

        for d in range(2):
            consts = (qm_ref[d], pm_ref[d])
            pars = [p[...] for p in par_refs[d]]
            for sub in (range(S) if d == 0 else reversed(range(S))):
                rows = pl.ds(sub * C, C)
                q, k, v, g = prep([r[rows, :] for r in raw_refs[d]], pars)
                e = _split_mm(h_ref[d], g)
                tot = jnp.sum(g, axis=0, keepdims=True)
                for h in range(heads):
                    ks, vs = slice(h * dk, (h + 1) * dk), slice(h * dv, (h + 1) * dv)
                    st = st_ref[d, h]
                    ss_refs[d][h, sub] = st
                    o, st_new = scan_chunk(q[:, ks], k[:, ks], v[:, vs], e[:, ks], tot[:, ks], st, *consts)
                    o_refs[d][rows, vs] = o
                    st_ref[d, h] = st_new

    ss_spec = lambda order: pl.BlockSpec((heads, S, dv, dk), lambda n: (0, order(n), 0, 0))
    return pl.pallas_call(
        body,
        out_shape=[jax.ShapeDtypeStruct((T, Wv), F32)] * 2 + [jax.ShapeDtypeStruct((heads, T // C, dv, dk), F32)] * 2,
        grid=(N,),
        in_specs=[_chunk_spec(s, w, orders[d]) for d in range(2) for s, w in raws[d]]
        + [_full_spec(p) for d in range(2) for p in params[d]] + _scan_const_specs(dk),
        out_specs=[pl.BlockSpec((C * S, Wv), lambda n: (orders[0](n), 0)),
                   pl.BlockSpec((C * S, Wv), lambda n: (orders[1](n), 0)), ss_spec(orders[0]), ss_spec(orders[1])],
        scratch_shapes=[pltpu.VMEM((2, heads, dv, dk), F32)],
        name=name, compiler_params=_cparams(("arbitrary",)))(
            *[s[0] for d in range(2) for s, _ in raws[d]], *[p for d in range(2) for p in params[d]], *_scan_const_args())


def scan_bwd(name, prep, raws, params, ss, do, heads, dk, dv, T):
    C, S = SCAN_CHUNK, SCAN_SUB
    N = T // (C * S)
    Wv = heads * dv
    orders = (lambda n: N - 1 - n, lambda n: n)
    n_raw, n_par = [len(r) for r in raws], [len(p) for p in params]

    def body(*refs):
        pos, raw_refs, par_refs, draw_refs, dpar_refs = 0, [], [], [], []
        for group, counts in ((raw_refs, n_raw), (par_refs, n_par)):
            for d in range(2):
                group.append(refs[pos:pos + counts[d]])
                pos += counts[d]
        ss_refs, do_refs = refs[pos:pos + 2], refs[pos + 2:pos + 4]
        h_ref, ht_ref, qm_ref, pm_ref = refs[pos + 4:pos + 8]
        pos += 8
        for group, counts in ((draw_refs, n_raw), (dpar_refs, n_par)):
            for d in range(2):
                group.append(refs[pos:pos + counts[d]])
                pos += counts[d]
        dst_ref = refs[pos]

        @pl.when(pl.program_id(0) == 0)
        def _():
            dst_ref[...] = jnp.zeros_like(dst_ref)
            for d in range(2):
                for r in dpar_refs[d]:
                    r[...] = jnp.zeros_like(r)

        for d in range(2):
            consts = (qm_ref[d], pm_ref[d])
            pars = [p[...] for p in par_refs[d]]
            for sub in (reversed(range(S)) if d == 0 else range(S)):
                rows = pl.ds(sub * C, C)
                (q, k, v, g), prep_vjp = jax.vjp(prep, [r[rows, :] for r in raw_refs[d]], pars)
                e = _split_mm(h_ref[d], g)
                tot = jnp.sum(g, axis=0, keepdims=True)
                dqs, dks, dvs, des, dtots = [], [], [], [], []
                for h in range(heads):
                    ks, vs = slice(h * dk, (h + 1) * dk), slice(h * dv, (h + 1) * dv)
                    _, vjp = jax.vjp(lambda q_, k_, v_, e_, t_, st_: scan_chunk(q_, k_, v_, e_, t_, st_, *consts),
                                     q[:, ks], k[:, ks], v[:, vs], e[:, ks], tot[:, ks], ss_refs[d][h, sub])
                    dq, dk_, dv_, de, dtot, dst = vjp((do_refs[d][rows, vs], dst_ref[d, h]))
                    dst_ref[d, h] = dst
                    for group, val in ((dqs, dq), (dks, dk_), (dvs, dv_), (des, de), (dtots, dtot)):
                        group.append(val)
                cat = lambda parts: jnp.concatenate(parts, axis=-1)
                dg = _split_mm(ht_ref[d], cat(des)) + cat(dtots)
                draws, dpars = prep_vjp((cat(dqs), cat(dks), cat(dvs), dg))
                for r, val in zip(draw_refs[d], draws):
                    r[rows, :] = val.astype(r.dtype)
                for r, val in zip(dpar_refs[d], dpars):
                    r[...] += val

    ss_spec = lambda order: pl.BlockSpec((heads, S, dv, dk), lambda n: (0, order(n), 0, 0))
    row_out = lambda w, order: pl.BlockSpec((C * S, w), lambda n: (order(n), 0))
    return pl.pallas_call(
        body,
        out_shape=[jax.ShapeDtypeStruct((T, w), BF16) for d in range(2) for _, w in raws[d]]
        + [jax.ShapeDtypeStruct(p.shape, F32) for d in range(2) for p in params[d]],
        grid=(N,),
        in_specs=[_chunk_spec(s, w, orders[d]) for d in range(2) for s, w in raws[d]]
        + [_full_spec(p) for d in range(2) for p in params[d]]
        + [ss_spec(orders[0]), ss_spec(orders[1]), _chunk_spec(do, Wv, orders[0]), _chunk_spec(do, Wv, orders[1])]
        + _scan_const_specs(dk),
        out_specs=[row_out(w, orders[d]) for d in range(2) for _, w in raws[d]]
        + [_full_spec(p) for d in range(2) for p in params[d]],
        scratch_shapes=[pltpu.VMEM((2, heads, dv, dk), F32)],
        name=name, compiler_params=_cparams(("arbitrary",)))(
            *[s[0] for d in range(2) for s, _ in raws[d]], *[p for d in range(2) for p in params[d]],
            ss[0], ss[1], do[0], do[0], *_scan_const_args())


def final_call(x, g, target, T):
    tr = _row_tile(T)

    def tile(xv, gv, tv):
        y = _rms(xv, gv)
        err = (y - tv) ** 2
        return jnp.sum(jnp.sum(err, axis=-1, keepdims=True), axis=0, keepdims=True) * (0.5 / D_MODEL)

    def body(x_ref, g_ref, t_ref, loss_ref, dx_ref, dg_ref):
        i = pl.program_id(0)
        tv = t_ref[...]
        lv, vjp = jax.vjp(lambda a, b: tile(a, b, tv), x_ref[...], g_ref[...])
        dx, dg = vjp(jnp.ones((1, 1), F32))
        dx_ref[...] = dx

        @pl.when(i == 0)
        def _():
            loss_ref[...] = jnp.zeros_like(loss_ref)
            dg_ref[...] = jnp.zeros_like(dg_ref)

        loss_ref[...] += jnp.broadcast_to(lv, loss_ref.shape)
        dg_ref[...] += dg

    return pl.pallas_call(
        body,
        out_shape=[jax.ShapeDtypeStruct((8, 128), F32), jax.ShapeDtypeStruct((T, D_MODEL), F32),
                   jax.ShapeDtypeStruct((1, D_MODEL), F32)],
        grid=(T // tr,),
        in_specs=[pl.BlockSpec((tr, D_MODEL), lambda i: (i, 0)), pl.BlockSpec((1, D_MODEL), lambda i: (0, 0)),
                  pl.BlockSpec((tr, D_MODEL), lambda i: (i, 0))],
        out_specs=[pl.BlockSpec((8, 128), lambda i: (0, 0)), pl.BlockSpec((tr, D_MODEL), lambda i: (i, 0)),
                   pl.BlockSpec((1, D_MODEL), lambda i: (0, 0))],
        name="final_loss", compiler_params=_cparams(("arbitrary",)))(x, g, target)


def adamw_call(w, g, m, v):
    shape = w.shape
    c = shape[-1]
    r = int(np.prod(shape[:-1])) if len(shape) > 1 else 1
    tr = r if r <= 256 else 256
    assert r % tr == 0

    def body(w_ref, g_ref, m_ref, v_ref, d_ref, nm_ref, nv_ref):
        gv = g_ref[...]
        nm = ADAM_B1 * m_ref[...] + (1.0 - ADAM_B1) * gv
        nv = ADAM_B2 * v_ref[...] + (1.0 - ADAM_B2) * jnp.square(gv)
        m_hat = nm / (1.0 - ADAM_B1 ** ADAM_STEP)
        v_hat = nv / (1.0 - ADAM_B2 ** ADAM_STEP)
        d_ref[...] = -ADAM_LR * (m_hat / (jnp.sqrt(v_hat) + ADAM_EPS) + ADAM_WD * w_ref[...])
        nm_ref[...] = nm
        nv_ref[...] = nv

    spec = pl.BlockSpec((tr, c), lambda i: (i, 0))
    outs = pl.pallas_call(body, out_shape=[jax.ShapeDtypeStruct((r, c), F32)] * 3, grid=(r // tr,),
                          in_specs=[spec] * 4, out_specs=[spec] * 3, name="adamw",
                          compiler_params=_cparams(("arbitrary",)))(*(t.reshape(r, c) for t in (w, g, m, v)))
    return tuple(o.reshape(shape) for o in outs)


def adamw_halves(w, mine, other, m, v, c):
    L, R, C = w.shape
    by_cols = mine.shape[-1] != C
    if by_cols:
        tile, nbh = (R, C // 2), 1
        full_idx = lambda l, i: (l, 0, i)
    else:
        rh = R // 2
        tr = rh if rh <= 256 else rh // 2
        assert tr % 8 == 0
        tile, nbh = (tr, C), rh // tr
        full_idx = lambda l, i: (l, i, 0)

    def body(c_ref, w_ref, a_ref, b_ref, m_ref, v_ref, g_ref, d_ref, nm_ref, nv_ref):
        is_mine = (pl.program_id(1) // nbh) == c_ref[0]
        gv = jnp.where(is_mine, a_ref[...], b_ref[...])
        nm = ADAM_B1 * m_ref[...] + (1.0 - ADAM_B1) * gv
        nv = ADAM_B2 * v_ref[...] + (1.0 - ADAM_B2) * jnp.square(gv)
        m_hat = nm / (1.0 - ADAM_B1 ** ADAM_STEP)
        v_hat = nv / (1.0 - ADAM_B2 ** ADAM_STEP)
        g_ref[...] = gv
        d_ref[...] = -ADAM_LR * (m_hat / (jnp.sqrt(v_hat) + ADAM_EPS) + ADAM_WD * w_ref[...])
        nm_ref[...] = nm
        nv_ref[...] = nv

    full = pl.BlockSpec((None,) + tile, lambda l, i, c_ref: full_idx(l, i))
    half = pl.BlockSpec((None,) + tile, lambda l, i, c_ref: (l, i % nbh, 0))
    grid_spec = pltpu.PrefetchScalarGridSpec(num_scalar_prefetch=1, grid=(L, 2 * nbh),
                                             in_specs=[full, half, half, full, full], out_specs=[full] * 4)
    return pl.pallas_call(body, out_shape=[jax.ShapeDtypeStruct(w.shape, F32)] * 4, grid_spec=grid_spec,
                          name="adamw_halves", compiler_params=_cparams(("arbitrary", "arbitrary")))(c, w, mine, other, m, v)


def sum_devices(g64):
    def body(x_ref, o_ref):
        acc = x_ref[0:8, :]
        for d in range(1, 8):
            acc = acc + x_ref[8 * d:8 * d + 8, :]
        o_ref[...] = acc

    return pl.pallas_call(body, out_shape=jax.ShapeDtypeStruct((8, D_MODEL), F32), name="sum_devices")(g64)


def _half_tile(rh):
    if rh <= 512:
        return rh
    return next(rh // d for d in range(2, rh) if rh % d == 0 and (rh // d) % 16 == 0 and rh // d <= 512)


def _half_geometry(full_shape, half_shape):
    R, C = full_shape[-2:]
    if half_shape[-1] != C:
        return (R, C // 2), 1, lambda i, c: (0, c)
    tr = _half_tile(R // 2)
    nblk = (R // 2) // tr
    return (tr, C), nblk, lambda i, c: (i + c * nblk, 0)


def _work_items(counts):
    starts = [int(v) for v in np.cumsum([0] + list(counts[:-1]))]
    local = lambda a, s: jnp.clip(s - starts[a], 0, counts[a] - 1)
    return starts, int(sum(counts)), local


def add_sibling(gs, recvs, c, out_dtypes):
    n = len(gs)
    geo = [_half_geometry(g.shape, r.shape) for g, r in zip(gs, recvs)]
    counts = [4 * nblk for _, nblk, _ in geo]
    starts, total, local = _work_items(counts)

    def body(c_ref, *refs):
        s = pl.program_id(0)
        for a in range(n):
            g_ref, r_ref, o_ref = refs[a], refs[n + a], refs[2 * n + a]

            @pl.when((s >= starts[a]) & (s < starts[a] + counts[a]))
            def _():
                o_ref[...] = (g_ref[...] + r_ref[...]).astype(o_ref.dtype)

    def own_idx(s, c_ref, a):
        _, nblk, own = geo[a]
        k = local(a, s)
        return (k // nblk,) + own(k % nblk, c_ref[0])

    def half_idx(s, c_ref, a):
        k = local(a, s)
        return (k // geo[a][1], k % geo[a][1], 0)

    halves = [pl.BlockSpec((None,) + geo[a][0], functools.partial(half_idx, a=a)) for a in range(n)]
    grid_spec = pltpu.PrefetchScalarGridSpec(
        num_scalar_prefetch=1, grid=(total,),
        in_specs=[pl.BlockSpec((None,) + geo[a][0], functools.partial(own_idx, a=a)) for a in range(n)] + halves,
        out_specs=halves)
    return pl.pallas_call(body, out_shape=[jax.ShapeDtypeStruct(r.shape, dt) for r, dt in zip(recvs, out_dtypes)],
                          grid_spec=grid_spec, name="rs_add_sibling",
                          compiler_params=_cparams(("arbitrary",)))(c, *gs, *recvs)


def add_chips(gs, recvs, r3s, place):
    n = len(gs)
    geo = [_half_geometry(g.shape, r.shape) for g, r in zip(gs, recvs)]
    counts = [nblk for _, nblk, _ in geo]
    starts, total, local = _work_items(counts)

    def body(p_ref, *refs):
        s = pl.program_id(0)
        up = lambda r: r[...].astype(F32)
        for a in range(n):
            g_ref, s_ref, o_ref = refs[a], refs[n + a], refs[5 * n + a]
            a_ref, b_ref, c_ref = refs[2 * n + 3 * a:2 * n + 3 * a + 3]

            @pl.when((s >= starts[a]) & (s < starts[a] + counts[a]))
            def _():
                o_ref[...] = (((g_ref[...] + up(s_ref)) + up(a_ref)) + up(b_ref)) + up(c_ref)

    own_idx = lambda s, p_ref, a: (p_ref[0],) + geo[a][2](local(a, s), p_ref[1])
    sib_idx = lambda s, p_ref, a: (p_ref[0], local(a, s), 0)
    chip_idx = lambda s, p_ref, a, k: (k, local(a, s), 0)
    spec = lambda a, idx, **kw: pl.BlockSpec((None,) + geo[a][0], functools.partial(idx, a=a, **kw))
    grid_spec = pltpu.PrefetchScalarGridSpec(
        num_scalar_prefetch=1, grid=(total,),
        in_specs=[spec(a, own_idx) for a in range(n)] + [spec(a, sib_idx) for a in range(n)]
        + [spec(a, chip_idx, k=k) for a in range(n) for k in range(3)],
        out_specs=[pl.BlockSpec(geo[a][0], functools.partial(lambda s, p_ref, a: (local(a, s), 0), a=a))
                   for a in range(n)])
    return pl.pallas_call(body, out_shape=[jax.ShapeDtypeStruct(r.shape[1:], F32) for r in recvs],
                          grid_spec=grid_spec, name="rs_add_chips", compiler_params=_cparams(("arbitrary",)))(
                              place, *gs, *recvs, *[r for r3 in r3s for r in (r3, r3, r3)])


def _remote(src, dst, ssem, rsem, dev):
    return pltpu.make_async_remote_copy(src_ref=src, dst_ref=dst, send_sem=ssem, recv_sem=rsem,
                                        device_id=dev, device_id_type=pl.DeviceIdType.MESH)


def _mesh_places():
    x, y, c = lax.axis_index("x"), lax.axis_index("y"), lax.axis_index("c")
    chips = [(1 - x, y), (x, 1 - y), (1 - x, 1 - y)]
    return x, y, c, (x, y, 1 - c), chips


def _hbm_specs(n):
    return [pl.BlockSpec(memory_space=pltpu.HBM) for _ in range(n)]


def _gather_body(ins, outs, n_split, send_sems, recv_sems, handshake):
    x, y, c, sibling, chips = _mesh_places()
    mine = 2 * x + y
    if handshake:
        barrier = pltpu.get_barrier_semaphore()
        peers = [sibling] + [(*chip, c) for chip in chips]
        for peer in peers:
            pl.semaphore_signal(barrier, inc=1, device_id=peer, device_id_type=pl.DeviceIdType.MESH)
        pl.semaphore_wait(barrier, len(peers))

    def half(a, chip_idx, which):
        rh = ins[a].shape[0] // 2
        return outs[a].at[chip_idx, pl.ds(which * rh, rh), :]

    sent = []
    for a in range(len(ins)):
        for k, chip in enumerate(chips):
            if a < n_split:
                rh = ins[a].shape[0] // 2
                src, dst = ins[a].at[pl.ds(c * rh, rh), :], half(a, mine, c)
            else:
                src, dst = ins[a], outs[a].at[mine]
            sent.append(_remote(src, dst, send_sems.at[a, k], recv_sems.at[a, k], (*chip, c)))
    for cp in sent:
        cp.start()
    for a in range(len(ins)):
        for k, chip in enumerate(chips):
            j = 2 * chip[0] + chip[1]
            region = half(a, j, c) if a < n_split else outs[a].at[j]
            _remote(region, region, send_sems.at[a, k], recv_sems.at[a, k], (*chip, c)).wait_recv()
            if a < n_split:
                fwd = _remote(region, region, send_sems.at[a, 3 + k], recv_sems.at[a, 3 + k], sibling)
                fwd.start()
                sent.append(fwd)
    for a in range(n_split):
        for k, chip in enumerate(chips):
            region = half(a, 2 * chip[0] + chip[1], 1 - c)
            _remote(region, region, send_sems.at[a, 3 + k], recv_sems.at[a, 3 + k], sibling).wait_recv()
    for cp in sent:
        cp.wait_send()


def gather_weights(shards, small):
    arrs = list(shards) + [small]
    n = len(arrs)

    def body(*refs):
        _gather_body(refs[:n], refs[n:2 * n], n - 1, refs[2 * n], refs[2 * n + 1], handshake=False)

    return pl.pallas_call(
        body, out_shape=[jax.ShapeDtypeStruct((4,) + a.shape, a.dtype) for a in arrs],
        in_specs=_hbm_specs(n), out_specs=_hbm_specs(n),
        scratch_shapes=[pltpu.SemaphoreType.DMA((n, 6)), pltpu.SemaphoreType.DMA((n, 6))],
        name="gather_weights")(*arrs)


def gather_weights_async(shards):
    n = len(shards)

    def body(*refs):
        _gather_body(refs[:n], refs[n:2 * n], n, refs[2 * n], refs[2 * n + 1], handshake=True)

    return pl.kernel(
        body, out_type=[jax.ShapeDtypeStruct((4,) + a.shape, a.dtype) for a in shards],
        mesh=plsc.ScalarSubcoreMesh(axis_name="seq", num_cores=1),
        scratch_types=[pltpu.SemaphoreType.DMA((n, 6)), pltpu.SemaphoreType.DMA((n, 6))],
        compiler_params=pltpu.CompilerParams(collective_id=1), name="gather_weights_async")(*shards)


def _sequencer_call(name, body, out_type, sem_shape, collective_id, args):
    return pl.kernel(
        body, out_type=out_type, mesh=plsc.ScalarSubcoreMesh(axis_name="seq", num_cores=1),
        scratch_types=[pltpu.SemaphoreType.DMA(sem_shape), pltpu.SemaphoreType.DMA(sem_shape)],
        compiler_params=pltpu.CompilerParams(collective_id=collective_id), name=name)(*args)


def _handshake(peers):
    barrier = pltpu.get_barrier_semaphore()
    for peer in peers:
        pl.semaphore_signal(barrier, inc=1, device_id=peer, device_id_type=pl.DeviceIdType.MESH)
    pl.semaphore_wait(barrier, len(peers))


def exchange_siblings(name, srcs, axes, collective_id):
    n = len(srcs)

    def body(*refs):
        ins, outs = refs[:n], refs[n:2 * n]
        send_sems, recv_sems = refs[2 * n:]
        x, y, c, sibling, chips = _mesh_places()
        _handshake([sibling])
        cps = []
        for a in range(n):
            src = ins[a]
            if axes[a] is not None:
                half = src.shape[axes[a]] // 2
                theirs = pl.ds((1 - c) * half, half)
                src = src.at[:, theirs, :] if axes[a] == 1 else src.at[:, :, theirs]
            cps.append(_remote(src, outs[a], send_sems.at[a], recv_sems.at[a], sibling))
        for cp in cps:
            cp.start()
        for cp in cps:
            cp.wait()

    def shape(g, axis):
        return g.shape if axis is None else tuple(d // 2 if k == axis else d for k, d in enumerate(g.shape))

    return _sequencer_call(name, body, [jax.ShapeDtypeStruct(shape(g, ax), g.dtype) for g, ax in zip(srcs, axes)],
                           (n,), collective_id, srcs)


def exchange_chips(name, s1s, collective_id):
    n = len(s1s)

    def body(*refs):
        ins, outs = refs[:n], refs[n:2 * n]
        send_sems, recv_sems = refs[2 * n:]
        x, y, c, sibling, chips = _mesh_places()
        _handshake([(*chip, c) for chip in chips])
        cps = []
        for a in range(n):
            for k, chip in enumerate(chips):
                cps.append(_remote(ins[a].at[2 * chip[0] + chip[1]], outs[a].at[k], send_sems.at[a, k],
                                   recv_sems.at[a, k], (*chip, c)))
        for cp in cps:
            cp.start()
        for cp in cps:
            cp.wait()

    return _sequencer_call(name, body, [jax.ShapeDtypeStruct((3,) + s.shape[1:], s.dtype) for s in s1s], (n, 3),
                           collective_id, s1s)


def allgather_small(v):
    m_per = v.shape[0]

    def body(x_ref, out_ref, send_sems, recv_sems, local_sem):
        x, y, c, sibling, chips = _mesh_places()
        me = (x, y, c)

        def rows(px, py, pc):
            return out_ref.at[pl.ds((4 * px + 2 * py + pc) * m_per, m_per), :]

        def copy(k, block, to, src=None):
            return _remote(rows(*block) if src is None else src, rows(*block), send_sems.at[k], recv_sems.at[k], to)

        mine = pltpu.make_async_copy(x_ref, rows(*me), local_sem)
        mine.start()
        first = [copy(0, me, sibling, src=x_ref)]
        first += [copy(1 + j, me, (*chip, c), src=x_ref) for j, chip in enumerate(chips)]
        for cp in first:
            cp.start()
        passed = [copy(4 + j, (*chip, c), sibling) for j, chip in enumerate(chips)]
        for j, chip in enumerate(chips):
            copy(1 + j, (*chip, c), me).wait_recv()
            passed[j].start()
        copy(0, sibling, me).wait_recv()
        for j, chip in enumerate(chips):
            copy(4 + j, (*chip, 1 - c), me).wait_recv()
        for cp in first + passed:
            cp.wait_send()
        mine.wait()

    return pl.pallas_call(
        body, out_shape=jax.ShapeDtypeStruct((8 * m_per, v.shape[1]), v.dtype),
        in_specs=[pl.BlockSpec(memory_space=pltpu.VMEM)], out_specs=pl.BlockSpec(memory_space=pltpu.VMEM),
        scratch_shapes=[pltpu.SemaphoreType.DMA((7,)), pltpu.SemaphoreType.DMA((7,)), pltpu.SemaphoreType.DMA],
        name="allgather_small")(v)


def _lower_bounds(lb_param):
    lbs = jax.nn.softmax(lb_param.astype(F32), axis=0)
    return jnp.cumsum(lbs, axis=0) - lbs[0]


def _even_fwd(x, i, W, lower, kv, slopes, T):
    O = EVEN_OFF
    g = W["norm_even"][i].reshape(1, D_MODEL)
    h, p = norm_project("mm_in_e", x, g, W["w_in_e"][i])
    kvp = jnp.pad(p[:, O["kA"]:O["kA"] + 2 * W_KV_A], ((BLOCK, BLOCK), (0, 0)))
    sink = jnp.repeat(W["sink"][i], BLOCK).reshape(N_Q_A * BLOCK, 1)
    a = attn_fwd(p, O["qA"], kvp, sink, slopes, T)
    scan_raws = [[((p, O["qB"]), W_B), ((p, O[z]), W_B), ((p, O["iB"]), W_B)] for z in ("zf", "zb")]
    scan_pars = [[lower[i][0:1]], [lower[i][1:2]]]
    o_f, o_b, ss_f, ss_b = scan_fwd("scan_fwd_h", hgrn_prep, scan_raws, scan_pars, N_HEADS_B, HEAD_DIM_B, HEAD_DIM_B, T)
    mo = mem_fwd(p, O["qM"], kv, T)
    hg = W["hgrn_norm"][i].reshape(1, W_B)
    post_ins = [("row", a, 0, W_A), ("row", o_f, 0, W_B), ("row", o_b, 0, W_B), ("row", mo, 0, W_M),
                ("row", p, O["gA"], W_A), ("row", p, O["gB"], W_B), ("row", p, O["gM"], W_M), ("full", hg)]
    x_new = mix_project("even_out", even_post_tile, T, post_ins, W["w_out_e"][i], x)
    return x_new, dict(x=x, g=g, h=h, p=p, kvp=kvp, sink=sink, scan_raws=scan_raws, scan_pars=scan_pars,
                       ss=(ss_f, ss_b), post_ins=post_ins)


def _add2(a, b):
    return a.astype(F32) + b.astype(F32)


def _assemble_even(dqA, dgA, dqB_f, dqB_b, dzf, dzb, diB_f, diB_b, dgB, dqM, dgM, dkvA):
    parts = [dqA, dgA, _add2(dqB_f, dqB_b), dzf, dzb, _add2(diB_f, diB_b), dgB, dqM, dgM, dkvA]
    return (jnp.concatenate([t.astype(BF16) for t in parts], axis=-1),)


def _even_bwd(dxo, sv, i, W, kv, slopes, T, sync):
    O = EVEN_OFF
    p = sv["p"]
    da, dof, dmo, dgA, dgB, dgM, dhg, dwo = mix_project_bwd("even_out_bwd", even_post_tile, T, sv["post_ins"],
                                                            W["w_out_e"][i], dxo, skip=(2,), narrow=(4, 5, 6))
    da = sync(da)
    dqA, dkvp, dsink = attn_bwd(p, O["qA"], sv["kvp"], sv["sink"], slopes, da, T)
    dkvA = dkvp[BLOCK:-BLOCK]
    dqB_f, dzf, diB_f, dqB_b, dzb, diB_b, dlow_f, dlow_b = scan_bwd(
        "scan_bwd_h", hgrn_prep, sv["scan_raws"], sv["scan_pars"], sv["ss"], (dof, 0), N_HEADS_B, HEAD_DIM_B, HEAD_DIM_B, T)
    dqB_f = sync(dqB_f)
    row = lambda arr, w: ("row", arr, 0, w)
    dlow = jnp.concatenate([dlow_f, dlow_b], axis=0)
    dqM, dkv = mem_bwd(p, O["qM"], kv, dmo, T)
    pieces = [row(dqA, W_A), row(dgA, W_A), row(dqB_f, W_B), row(dqB_b, W_B), row(dzf, W_B), row(dzb, W_B),
              row(diB_f, W_B), row(diB_b, W_B), row(dgB, W_B), row(dqM, W_M), row(dgM, W_M), row(dkvA, 2 * W_KV_A)]
    dp, dx, dg = norm_project_bwd("mm_in_e_bwd", _assemble_even, pieces, W["w_in_e"][i], sv["x"], sv["g"], dxo)
    dwi = matmul("mm_dwi_e", sv["h"], dp, "tn")
    return dx, dict(w_in=dwi, w_out=dwo, norm=dg[0], sink=dsink.reshape(N_Q_A), low=dlow, hg=dhg[0], kv=dkv)


def _pad_gate_up(w_up):
    z = jnp.zeros((2, 128, WK_C), F32)
    z = z.at[0, 0:GATE_RANK].set(w_up[0])
    return z.at[1, GATE_RANK:2 * GATE_RANK].set(w_up[1])


def _odd_fwd(x, i, W, kv, T):
    O = ODD_OFF
    g = W["norm_odd"][i].reshape(1, D_MODEL)
    h, p = norm_project("mm_in_o", x, g, W["w_in_o"][i])
    wup = _pad_gate_up(W["w_gate_up"][i])
    one_dir = [((p, O["qC"]), WK_C), ((p, O["kC"]), WK_C), ((p, O["vC"]), WV_C), ((p, O["rr"]), 128)]
    scan_raws = [one_dir, one_dir]
    scan_pars = [[wup[d], W["b_gate"][i][d:d + 1]] for d in range(2)]
    o_f, o_b, ss_f, ss_b = scan_fwd("scan_fwd_g", gla_prep, scan_raws, scan_pars, N_HEADS_C, DK_C, DV_C, T)
    mo = mem_fwd(p, O["qM"], kv, T)
    gg = W["gla_norm"][i].reshape(1, WV_C)
    post_ins = [("row", o_f, 0, WV_C), ("row", o_b, 0, WV_C), ("row", mo, 0, W_M),
                ("row", p, O["gC"], WV_C), ("row", p, O["gM"], W_M), ("full", gg)]
    x_new = mix_project("odd_out", odd_post_tile, T, post_ins, W["w_out_o"][i], x)
    return x_new, dict(x=x, g=g, h=h, p=p, scan_raws=scan_raws, scan_pars=scan_pars, ss=(ss_f, ss_b),
                       post_ins=post_ins)


def _assemble_odd(dq0, dq1, dk0, dk1, dv0, dv1, dgC, dqM, dgM, dr0, dr1):
    parts = [_add2(dq0, dq1), _add2(dk0, dk1), _add2(dv0, dv1), dgC, dqM, dgM, _add2(dr0, dr1)]
    return (jnp.concatenate([t.astype(BF16) for t in parts], axis=-1),)


def _odd_bwd(dxo, sv, i, W, kv, T, sync):
    O = ODD_OFF
    p = sv["p"]
    dof, dmo, dgC, dgM, dgg, dwo = mix_project_bwd("odd_out_bwd", odd_post_tile, T, sv["post_ins"], W["w_out_o"][i],
                                                   dxo, skip=(1,), narrow=(3, 4))
    dof = sync(dof)
    dqf, dkf, dvf, dr_f, dqb, dkb, dvb, dr_b, dwup_f, dbg_f, dwup_b, dbg_b = scan_bwd(
        "scan_bwd_g", gla_prep, sv["scan_raws"], sv["scan_pars"], sv["ss"], (dof, 0), N_HEADS_C, DK_C, DV_C, T)
    dqf = sync(dqf)
    row = lambda arr, w: ("row", arr, 0, w)
    dqM, dkv = mem_bwd(p, O["qM"], kv, dmo, T)
    pieces = [row(dqf, WK_C), row(dqb, WK_C), row(dkf, WK_C), row(dkb, WK_C), row(dvf, WV_C), row(dvb, WV_C),
              row(dgC, WV_C), row(dqM, W_M), row(dgM, W_M), row(dr_f, 128), row(dr_b, 128)]
    dp, dx, dg = norm_project_bwd("mm_in_o_bwd", _assemble_odd, pieces, W["w_in_o"][i], sv["x"], sv["g"], dxo)
    dwi = matmul("mm_dwi_o", sv["h"], dp, "tn")
    dw_up = jnp.stack([dwup_f[0:GATE_RANK], dwup_b[GATE_RANK:2 * GATE_RANK]])
    dbg = jnp.concatenate([dbg_f, dbg_b], axis=0)
    return dx, dict(w_in=dwi, w_out=dwo, norm=dg[0], w_up=dw_up, b_gate=dbg, gg=dgg[0], kv=dkv)


def local_step(x, mem, target, W, later=None, on_layer_grads=None, sync=lambda a: a):
    T = x.shape[0]
    slopes = jnp.repeat(2.0 ** (-8.0 * jnp.arange(1, N_Q_A + 1, dtype=F32) / N_Q_A), BLOCK).reshape(N_Q_A * BLOCK, 1)
    lower, lower_vjp = jax.vjp(_lower_bounds, W["lb_param"])
    mem_g = W["mem_norm"].reshape(1, D_MODEL)
    (mem_n,) = rows_call("mem_rms_fwd", rms_tile, N_MEM, [("row", mem, 0, D_MODEL), ("full", mem_g)], [D_MODEL], [BF16])
    kvs, saved = [], []
    for l in range(DEPTH):
        if l == 1 and later is not None:
            x, W = later(x, W)
        kvs.append(matmul("mm_kv", mem_n, W["w_kv"][l], "nn"))
        if l % 2 == 0:
            x, sv = _even_fwd(x, l // 2, W, lower, kvs[l], slopes, T)
        else:
            x, sv = _odd_fwd(x, l // 2, W, kvs[l], T)
        saved.append(sv)
    loss, dx, dgf = final_call(x, W["final_norm"].reshape(1, D_MODEL), target, T)
    per = [None] * DEPTH
    dmem_n = None
    for l in reversed(range(DEPTH)):
        if l % 2 == 0:
            dx, per[l] = _even_bwd(dx, saved[l], l // 2, W, kvs[l], slopes, T, sync)
        else:
            dx, per[l] = _odd_bwd(dx, saved[l], l // 2, W, kvs[l], T, sync)
        per[l]["w_kv"] = matmul("mm_dwkv", mem_n, per[l]["kv"], "tn")
        dmem_n = matmul("mm_dmem", per[l]["kv"], W["w_kv"][l], "nt", add=dmem_n)
        if on_layer_grads is not None:
            dx = on_layer_grads(l, dx, per[l])
    dw_kv = [per[l]["w_kv"] for l in range(DEPTH)]
    (dmem_norm,) = rows_vjp_call("mem_rms_bwd", rms_tile, N_MEM, [("row", mem, 0, D_MODEL), ("full", mem_g)],
                                 [[("row", dmem_n, 0, D_MODEL)]], skip=(0,))
    ev, od = (per[0], per[2]), (per[1], per[3])
    (d_lb,) = lower_vjp(jnp.stack([e["low"] for e in ev]))
    grads = dict(
        w_in_e=jnp.stack([e["w_in"] for e in ev]), w_in_o=jnp.stack([o["w_in"] for o in od]),
        w_out_e=jnp.stack([e["w_out"] for e in ev]), w_out_o=jnp.stack([o["w_out"] for o in od]),
        w_kv=jnp.stack(dw_kv), norm_even=jnp.stack([e["norm"] for e in ev]), sink=jnp.stack([e["sink"] for e in ev]),
        lb_param=d_lb, hgrn_norm=jnp.stack([e["hg"] for e in ev]), norm_odd=jnp.stack([o["norm"] for o in od]),
        w_gate_up=jnp.stack([o["w_up"] for o in od]), b_gate=jnp.stack([o["b_gate"] for o in od]),
        gla_norm=jnp.stack([o["gg"] for o in od]), mem_norm=dmem_norm[0], final_norm=dgf[0])
    return loss, dx, grads


SMALL_SPECS = (("lb_param", (2, 2, 128)), ("norm_odd", (2, 256)), ("w_gate_up", (2, 2, 16, 128)),
               ("b_gate", (2, 2, 128)), ("gla_norm", (2, 256)))
SMALL_ROWS = 80


def _pack_small_local(d):
    return jnp.concatenate([d[n].reshape(-1) for n, _ in SMALL_SPECS]).reshape(SMALL_ROWS, 128)


def _unpack_small_local(b):
    flat, out, o = b.reshape(-1), {}, 0
    for n, shp in SMALL_SPECS:
        sz = int(np.prod(shp))
        out[n] = flat[o:o + sz].reshape(shp)
        o += sz
    return out


def _unpack_small_full(g4):
    per = [_unpack_small_local(g4[j]) for j in range(4)]
    return {n: jnp.concatenate([per[j][n] for j in range(4)], axis=-1) for n, _ in SMALL_SPECS}


def _pack_small_blocks(full):
    blocks = []
    for j in range(4):
        blocks.append(_pack_small_local({n: full[n][..., j * shp[-1]:(j + 1) * shp[-1]] for n, shp in SMALL_SPECS}))
    return jnp.stack(blocks)


def _cols(t, order, off, widths):
    return [t[..., off[n]:off[n] + widths[n]] for n in order]


EVEN_REF_ORDER = ("qA", "kA", "vA", "gA", "qB", "zf", "zb", "iB", "gB", "qM", "gM")
ODD_REF_ORDER = ("qC", "kC", "vC", "gC", "rr", "qM", "gM")


def _layer_weights(l, g_in, g_out, g_kv):
    t = g_in.transpose(1, 0, 2).reshape(D_MODEL, -1)
    if l % 2 == 0:
        w_in = jnp.concatenate(_cols(t, EVEN_ORDER, EVEN_REF_OFF, EVEN_W), axis=-1)
    else:
        w_in = jnp.concatenate(_cols(t, ODD_ORDER, ODD_REF_OFF, ODD_W) + [jnp.zeros((D_MODEL, ODD_PAD - ODD_IN), BF16)],
                               axis=-1)
    return w_in, g_out.reshape(MIX, D_MODEL), g_kv.reshape(D_MODEL, 2 * W_M)


def _layer_grad_blocks(l, gl):
    if l % 2 == 0:
        t = jnp.concatenate(_cols(gl["w_in"], EVEN_REF_ORDER, EVEN_OFF, EVEN_W), axis=-1)
    else:
        t = jnp.concatenate(_cols(gl["w_in"], ODD_REF_ORDER, ODD_OFF, ODD_W), axis=-1)
    b_in = t.reshape(D_MODEL, 4, -1).transpose(1, 2, 0)
    return [b_in, gl["w_out"].reshape(4, MIX // 4, D_MODEL), gl["w_kv"].reshape(4, D_MODEL // 4, 2 * W_M)]


WEIGHT_NAMES = ("norm_even", "w_in_even", "sink", "lb_param", "hgrn_norm", "w_out_even", "norm_odd", "w_in_odd",
                "w_gate_up", "b_gate", "gla_norm", "w_out_odd", "mem_norm", "w_mem_kv", "final_norm")


def kernel(x, mem, norm_even, w_in_even, sink, lb_param, hgrn_norm, w_out_even, norm_odd, w_in_odd, w_gate_up, b_gate, gla_norm, w_out_odd, mem_norm, w_mem_kv, final_norm, loss_target, m_norm_even, m_w_in_even, m_sink, m_lb_param, m_hgrn_norm, m_w_out_even, m_norm_odd, m_w_in_odd, m_w_gate_up, m_b_gate, m_gla_norm, m_w_out_odd, m_mem_norm, m_w_mem_kv, m_final_norm, v_norm_even, v_w_in_even, v_sink, v_lb_param, v_hgrn_norm, v_w_out_even, v_norm_odd, v_w_in_odd, v_w_gate_up, v_b_gate, v_gla_norm, v_w_out_odd, v_mem_norm, v_w_mem_kv, v_final_norm):
    w = dict(zip(WEIGHT_NAMES, (norm_even, w_in_even, sink, lb_param, hgrn_norm, w_out_even, norm_odd, w_in_odd,
                                w_gate_up, b_gate, gla_norm, w_out_odd, mem_norm, w_mem_kv, final_norm)))
    m = dict(zip(WEIGHT_NAMES, (m_norm_even, m_w_in_even, m_sink, m_lb_param, m_hgrn_norm, m_w_out_even, m_norm_odd,
                                m_w_in_odd, m_w_gate_up, m_b_gate, m_gla_norm, m_w_out_odd, m_mem_norm, m_w_mem_kv,
                                m_final_norm)))
    v = dict(zip(WEIGHT_NAMES, (v_norm_even, v_w_in_even, v_sink, v_lb_param, v_hgrn_norm, v_w_out_even, v_norm_odd,
                                v_w_in_odd, v_w_gate_up, v_b_gate, v_gla_norm, v_w_out_odd, v_mem_norm, v_w_mem_kv,
                                v_final_norm)))
    ci = lax.axis_index("c").astype(jnp.int32).reshape(1)
    chip = (2 * lax.axis_index("x") + lax.axis_index("y")).astype(jnp.int32).reshape(1)

    shards = []
    for l in range(DEPTH):
        names = ("w_in_even", "w_out_even") if l % 2 == 0 else ("w_in_odd", "w_out_odd")
        shards.append([w[names[0]][l // 2].astype(BF16), w[names[1]][l // 2].astype(BF16), w_mem_kv[l].astype(BF16)])
    small = _pack_small_local(w)
    own = lambda g, s: lax.dynamic_update_slice(g, s[None], (chip[0], 0, 0))
    first = [own(g, s) for g, s in zip(gather_weights(shards[0], small), shards[0] + [small])]
    later_shards = shards[1] + shards[2] + shards[3]
    later_raw = gather_weights_async(later_shards)
    w0 = _layer_weights(0, *first[0:3])
    W = dict(w_in_e=[w0[0]], w_out_e=[w0[1]], w_kv=[w0[2]])
    W.update(_unpack_small_full(first[3]))
    W.update({n: w[n] for n in ("norm_even", "sink", "hgrn_norm", "mem_norm", "final_norm")})

    def later(x1, W):
        x1, raw = lax.optimization_barrier((x1, list(later_raw)))
        g = [own(a, s) for a, s in zip(raw, later_shards)]
        w1, w2, w3 = (_layer_weights(l, *g[3 * (l - 1):3 * l]) for l in (1, 2, 3))
        W = dict(W)
        W.update(w_in_e=[w0[0], w2[0]], w_in_o=[w1[0], w3[0]], w_out_e=[w0[1], w2[1]], w_out_o=[w1[1], w3[1]],
                 w_kv=[w0[2], w1[2], w2[2], w3[2]])
        return x1, W

    place = jnp.concatenate([chip, ci])

    def start(tag, blocks, wire):
        axes = [2 if b.shape[1] == ODD_IN // 4 else 1 for b in blocks]
        return dict(tag=tag, blocks=blocks, wire=wire, step=0,
                    recv=exchange_siblings(f"rs_siblings_{tag}", blocks, axes, 2))

    def advance(p, a=None):
        tie = (lambda v: (a, v)) if a is None else (lambda v: lax.optimization_barrier((a, v)))
        if p["step"] == 0:
            a, sums = tie(add_sibling(p["blocks"], p["recv"], ci, p["wire"]))
            p["recv3"] = exchange_chips(f"rs_chips_{p['tag']}", sums, 3)
        else:
            a, p["mine"] = tie(add_chips(p["blocks"], p["recv"], p["recv3"], place))
            p["other"] = exchange_siblings(f"rs_final_{p['tag']}", p["mine"], [None] * len(p["mine"]), 4)
        p["step"] += 1
        return a

    pipes, first_layer = [], {}

    def sync(a):
        for p in pipes:
            if p["step"] < 3:
                key = ("recv", "recv3", "other")[p["step"]]
                a, arrived = lax.optimization_barrier((a, list(p[key])))
                p[key] = arrived
                if p["step"] < 2:
                    a = advance(p, a)
                else:
                    p["step"] = 3
        return a

    def on_layer_grads(l, dx, gl):
        dx = sync(dx)
        if l == 0:
            first_layer.update(gl)
        else:
            pipes.append(start(f"l{l}", _layer_grad_blocks(l, gl), [BF16] * 3))
        return dx

    loss_tile, dx, grads = local_step(x[0], mem[0], loss_target[0], W, later, on_layer_grads, sync)
    last = start("l0", _layer_grad_blocks(0, first_layer) + [_pack_small_blocks(grads)], [BF16] * 3 + [F32])
    for p in pipes + [last]:
        while p["step"] < (1 if p is last else 2):
            advance(p)
    by_layer = {int(p["tag"][1:]): p for p in pipes + [last]}
    halves = lambda layers, k: (jnp.stack([by_layer[l]["mine"][k] for l in layers]),
                                jnp.stack([by_layer[l]["other"][k] for l in layers]))
    gl, upd = {}, {}

    pack = jnp.zeros((8, D_MODEL), F32)
    pack = pack.at[0:2].set(grads["norm_even"]).at[2].set(grads["hgrn_norm"].reshape(-1))
    pack = pack.at[3].set(grads["mem_norm"]).at[4].set(grads["final_norm"])
    pack = pack.at[5, 0:16].set(grads["sink"].reshape(-1)).at[5, 16].set(loss_tile[0, 0])
    tot = sum_devices(allgather_small(pack))
    gl.update(norm_even=tot[0:2], hgrn_norm=tot[2].reshape(2, W_B), mem_norm=tot[3], final_norm=tot[4],
              sink=tot[5, 0:16].reshape(2, N_Q_A))
    loss = tot[5, 16]
    for n in ("norm_even", "hgrn_norm", "mem_norm", "final_norm", "sink"):
        upd[n] = adamw_call(w[n], gl[n], m[n], v[n])
    tr_ = lambda a: jnp.swapaxes(a, 1, 2)
    gl["w_in_odd"], *upd["w_in_odd"] = [tr_(o) for o in adamw_halves(
        tr_(w["w_in_odd"]), *halves((1, 3), 0), tr_(m["w_in_odd"]), tr_(v["w_in_odd"]), ci)]
    gl["w_out_odd"], *upd["w_out_odd"] = adamw_halves(w["w_out_odd"], *halves((1, 3), 1), m["w_out_odd"],
                                                      v["w_out_odd"], ci)
    early = [upd[n] for n in sorted(upd)] + [gl["w_in_odd"], gl["w_out_odd"]]
    last["recv3"], early = lax.optimization_barrier((list(last["recv3"]), early))
    for n, res in zip(sorted(upd), early):
        upd[n] = res
    gl["w_in_odd"], gl["w_out_odd"] = early[-2:]
    advance(last)

    big = dict(w_in_even=halves((0, 2), 0), w_out_even=halves((0, 2), 1), w_mem_kv=halves((0, 1, 2, 3), 2))
    s_mine, s_other = last["mine"][3], last["other"][3]
    g_small = jnp.where(ci[0] == 0, jnp.concatenate([s_mine, s_other]), jnp.concatenate([s_other, s_mine]))
    gl.update(_unpack_small_local(g_small))
    for n in WEIGHT_NAMES:
        if n == "w_in_even":
            gl[n], *upd[n] = [tr_(o) for o in adamw_halves(tr_(w[n]), *big[n], tr_(m[n]), tr_(v[n]), ci)]
        elif n in big:
            gl[n], *upd[n] = adamw_halves(w[n], *big[n], m[n], v[n], ci)
        elif n not in upd:
            upd[n] = adamw_call(w[n], gl[n], m[n], v[n])
    return (loss, dx[None], *[gl[n] for n in WEIGHT_NAMES], *[upd[n][0] for n in WEIGHT_NAMES],
            *[upd[n][1] for n in WEIGHT_NAMES], *[upd[n][2] for n in WEIGHT_NAMES])
```

```python
import functools

import numpy as np
import jax
import jax.numpy as jnp
from jax import lax
from jax.experimental import pallas as pl
from jax.experimental.pallas import tpu as pltpu
from jax.experimental.pallas import tpu_sc as plsc

F32 = jnp.float32
BF16 = jnp.bfloat16

D_MODEL = 1024
DEPTH = 4
N_Q_A, N_KV_A, HEAD_DIM_A = 8, 2, 64
W_A, W_KV_A = 512, 128
WINDOW = 128
BLOCK = 128
N_HEADS_B, HEAD_DIM_B, W_B = 4, 128, 512
N_HEADS_C, DK_C, DV_C, WK_C, WV_C = 4, 128, 256, 512, 1024
GATE_RANK = 16
GATE_TEMP = 16.0
N_MEM, N_HEADS_M, HEAD_DIM_M, W_M = 256, 4, 128, 512
EPS = 1e-6
MASK_VALUE = -1e30
MIN_GATE = 1e-30
EVEN_IN, ODD_IN = 4864, 4128
ODD_PAD = 4224
MIX = 1536
ADAM_LR, ADAM_B1, ADAM_B2, ADAM_EPS, ADAM_WD, ADAM_STEP = 0.001, 0.9, 0.999, 1e-08, 0.01, 10

SCAN_CHUNK = 128
SCAN_SUB = 2
SCAN_LEVELS = 7
VMEM_LIMIT = 56 * 1024 * 1024

EVEN_REF_OFF = dict(qA=0, kA=512, vA=640, gA=768, qB=1280, zf=1792, zb=2304, iB=2816, gB=3328, qM=3840, gM=4352)
EVEN_W = dict(qA=512, kA=128, vA=128, gA=512, qB=512, zf=512, zb=512, iB=512, gB=512, qM=512, gM=512)
EVEN_ORDER = ("qA", "gA", "qB", "zf", "zb", "iB", "gB", "qM", "gM", "kA", "vA")
ODD_REF_OFF = dict(qC=0, kC=512, vC=1024, gC=2048, rr=3072, qM=3104, gM=3616)
ODD_W = dict(qC=512, kC=512, vC=1024, gC=1024, rr=32, qM=512, gM=512)
ODD_ORDER = ("qC", "kC", "vC", "gC", "qM", "gM", "rr")


def _offsets(order, widths):
    off, o = {}, 0
    for n in order:
        off[n] = o
        o += widths[n]
    return off


EVEN_OFF = _offsets(EVEN_ORDER, EVEN_W)
ODD_OFF = _offsets(ODD_ORDER, ODD_W)


def _dg(a, b, ca, cb):
    return lax.dot_general(a.astype(BF16), b.astype(BF16), (((ca,), (cb,)), ((), ())),
                           preferred_element_type=F32)


def dot_nn(a, b):
    return _dg(a, b, 1, 0)


def dot_nt(a, b):
    return _dg(a, b, 1, 1)


def dot_tn(a, b):
    return _dg(a, b, 0, 0)


@jax.custom_vjp
def bdot(a, b):
    return dot_nn(a, b)


bdot.defvjp(lambda a, b: (dot_nn(a, b), (a, b)),
            lambda r, g: (dot_nt(g, r[1]), dot_tn(r[0], g)))


@jax.custom_vjp
def bdot_t(a, b):
    return dot_nt(a, b)


bdot_t.defvjp(lambda a, b: (dot_nt(a, b), (a, b)),
              lambda r, g: (dot_nn(g, r[1]), dot_tn(g, r[0])))


@jax.custom_vjp
def bdot_tn(a, b):
    return dot_tn(a, b)


bdot_tn.defvjp(lambda a, b: (dot_tn(a, b), (a, b)),
               lambda r, g: (dot_nt(r[1], g), dot_nn(r[0], g)))


def _split_mm(h, x):
    hi = x.astype(BF16)
    lo = (x - hi.astype(F32)).astype(BF16)
    return (lax.dot_general(h, hi, (((1,), (0,)), ((), ())), preferred_element_type=F32)
            + lax.dot_general(h, lo, (((1,), (0,)), ((), ())), preferred_element_type=F32))


def _sigmoid(z):
    return 1.0 / (1.0 + jnp.exp(-z))


def _silu(z):
    return z * _sigmoid(z)


def _log_sigmoid(z):
    return jnp.minimum(z, 0.0) - jnp.log(1.0 + jnp.exp(-jnp.abs(z)))


def _rms(x, g):
    return x * lax.rsqrt(jnp.mean(x * x, axis=-1, keepdims=True) + EPS) * g


def rms_tile(x, g):
    return (_rms(x, g),)


@functools.partial(jax.custom_vjp, nondiff_argnums=(1, 2))
def split(x, n, axis):
    w = x.shape[axis] // n
    return tuple(lax.slice_in_dim(x, h * w, (h + 1) * w, axis=axis) for h in range(n))


split.defvjp(lambda x, n, axis: (split(x, n, axis), None),
             lambda n, axis, _, cts: (jnp.concatenate(cts, axis=axis),))


def _group_rms(o, g, heads):
    return jnp.concatenate([_rms(oh, gh) for oh, gh in zip(split(o, heads, 1), split(g, heads, 1))], axis=-1)


def even_post_tile(a, o2f, o2b, mo, gA, gB, gM, hg):
    y = _group_rms(o2f + o2b, hg, N_HEADS_B)
    return (jnp.concatenate([a * _silu(gA), y * _silu(gB), mo * _silu(gM)], axis=-1),)


def odd_post_tile(o2f, o2b, mo, gC, gM, gg):
    y = _group_rms(o2f + o2b, gg, N_HEADS_C)
    return (jnp.concatenate([y * _silu(gC), mo * _silu(gM)], axis=-1),)


def hgrn_prep(raw, par):
    qB, z, iB = raw
    (lb,) = par
    f = lb + (1.0 - lb) * _sigmoid(z)
    return _silu(qB), (1.0 - lb) * _sigmoid(-z), iB, jnp.log(jnp.maximum(f, MIN_GATE))


def gla_prep(raw, par):
    qC, kC, vC, r128 = raw
    wup, bg = par
    return qC * (DK_C ** -0.5), kC, vC, _log_sigmoid(bdot(r128, wup) + bg) / GATE_TEMP


def mem_tile(q, k, v):
    s = bdot_t(q, k) * (HEAD_DIM_M ** -0.5)
    m = lax.stop_gradient(jnp.max(s, axis=-1, keepdims=True))
    p = jnp.exp(s - m)
    p = p / jnp.sum(p, axis=-1, keepdims=True)
    return (bdot(p, v),)


ATTN_GROUP = N_Q_A // N_KV_A


def attn_block(q, ks, vs, sink, slope, c, seq):
    rows = ATTN_GROUP * BLOCK
    i = lax.broadcasted_iota(jnp.int32, (rows, 3 * BLOCK), 0) % BLOCK
    j = lax.broadcasted_iota(jnp.int32, (rows, 3 * BLOCK), 1)
    dist = jnp.abs(i - j + BLOCK).astype(F32)
    kpos = (c - 1) * BLOCK + j
    valid = (dist <= WINDOW) & (kpos >= 0) & (kpos < seq)
    s = bdot_t(q, ks) * (HEAD_DIM_A ** -0.5)
    s = jnp.where(valid, s - slope * dist, MASK_VALUE)
    m = lax.stop_gradient(jnp.maximum(jnp.max(s, axis=-1, keepdims=True), sink))
    p = jnp.where(valid, jnp.exp(s - m), 0.0)
    denom = jnp.sum(p, axis=-1, keepdims=True) + jnp.exp(sink - m)
    return bdot(p, vs) / denom


def scan_chunk(q, k, v, e, tot, st, qm, pm):
    C = SCAN_CHUNK
    e = split(e, 2 + SCAN_LEVELS, 0)
    qe = q * jnp.exp(e[0])
    kd = k * jnp.exp(e[1])
    r = lax.broadcasted_iota(jnp.int32, (C, C), 0)
    s = lax.broadcasted_iota(jnp.int32, (C, C), 1)
    a = jnp.where(r == s, jnp.sum(q * k, axis=-1, keepdims=True), 0.0)
    for l in range(SCAN_LEVELS):
        u = jnp.where(qm[l * C:(l + 1) * C] != 0.0, q, k) * jnp.exp(e[2 + l])
        a = a + bdot_t(u, u) * pm[l * C:(l + 1) * C]
    o = bdot_t(qe, st) + bdot(a, v)
    st_new = st * jnp.exp(tot) + bdot_tn(v, kd)
    return o, st_new


def _scan_consts():
    C, L = SCAN_CHUNK, SCAN_LEVELS
    t = np.arange(C)[:, None]
    r = np.arange(C)[None, :]
    blocks = [(r <= t), (r > t)]
    qms, pms = [], []
    for l in range(1, L + 1):
        m = C >> l
        upper_t = (t % (2 * m)) >= m
        upper_r = (r % (2 * m)) >= m
        same_half = (t // m) == (r // m)
        blocks.append(same_half & np.where(upper_t, r <= t, r > t))
        qms.append(np.broadcast_to(upper_t, (C, C)))
        pms.append(((t // (2 * m)) == (r // (2 * m))) & upper_t & ~upper_r)
    hf = np.concatenate(blocks, axis=0).astype(np.float32)
    flip = lambda mat: mat.reshape(-1, C, C)[:, ::-1, ::-1].reshape(-1, C)
    qmf = np.concatenate(qms, axis=0).astype(np.float32)
    pmf = np.concatenate(pms, axis=0).astype(np.float32)
    h = np.stack([hf, flip(hf)])
    ht = np.stack([h[0].T, h[1].T])
    qm = np.stack([qmf, 1.0 - qmf])
    pm = np.stack([pmf, flip(pmf)])
    return h, ht, qm, pm


def _cparams(sem):
    return pltpu.CompilerParams(dimension_semantics=sem, vmem_limit_bytes=VMEM_LIMIT)


def _row_tile(T):
    return min(T, 512)


def _in_spec(spec, tr):
    kind = spec[0]
    if kind == "row":
        _, arr, off, w = spec
        assert off % w == 0
        return arr, pl.BlockSpec((tr, w), functools.partial(lambda i, b: (i, b), b=off // w))
    if kind == "row3":
        _, arr, d, off, w = spec
        assert off % w == 0
        return arr, pl.BlockSpec((None, tr, w), functools.partial(lambda i, d, b: (d, i, b), d=d, b=off // w))
    _, arr = spec
    return arr, pl.BlockSpec(arr.shape, functools.partial(lambda i, n: (0,) * n, n=arr.ndim))


def rows_call(name, tile_fn, T, ins, out_widths, out_dtypes=None, stacks=None):
    tr = _row_tile(T)
    n_in = len(ins)
    out_dtypes = out_dtypes or [F32] * len(out_widths)
    stacks = stacks or [(k,) for k in range(len(out_widths))]

    def body(*refs):
        vals = [r[...] for r in refs[:n_in]]
        outs = tile_fn(*vals)
        for r, members in zip(refs[n_in:], stacks):
            if len(members) == 1:
                r[...] = outs[members[0]].astype(r.dtype)
            else:
                for d, k in enumerate(members):
                    r[d] = outs[k].astype(r.dtype)

    in_specs, args = [], []
    for spec in ins:
        arr, bs = _in_spec(spec, tr)
        args.append(arr)
        in_specs.append(bs)
    out_specs, out_shape = [], []
    for w, dt, members in zip(out_widths, out_dtypes, stacks):
        n = len(members)
        if n == 1:
            out_specs.append(pl.BlockSpec((tr, w), lambda i: (i, 0)))
            out_shape.append(jax.ShapeDtypeStruct((T, w), dt))
        else:
            out_specs.append(pl.BlockSpec((n, tr, w), lambda i: (0, i, 0)))
            out_shape.append(jax.ShapeDtypeStruct((n, T, w), dt))
    return pl.pallas_call(body, out_shape=out_shape, grid=(T // tr,), in_specs=in_specs, out_specs=out_specs,
                          name=name, compiler_params=_cparams(("arbitrary",)))(*args)


def rows_vjp_call(name, tile_fn, T, ins, cts, skip=(), narrow=()):
    tr = _row_tile(T)
    n_in = len(ins)
    n_ct = [len(c) for c in cts]
    want = [k for k in range(n_in) if k not in skip]

    def body(*refs):
        i = pl.program_id(0)
        vals = [r[...] for r in refs[:n_in]]
        ct, pos = [], n_in
        for n in n_ct:
            acc = refs[pos][...]
            for r in refs[pos + 1:pos + n]:
                acc = acc + r[...]
            ct.append(acc)
            pos += n
        _, vjp = jax.vjp(tile_fn, *vals)
        grads = vjp(tuple(ct))
        for r, k in zip(refs[pos:], want):
            if ins[k][0] == "full":
                @pl.when(i == 0)
                def _():
                    r[...] = jnp.zeros_like(r)
                r[...] += grads[k]
            else:
                r[...] = grads[k].astype(r.dtype)

    in_specs, args = [], []
    for spec in list(ins) + [s for c in cts for s in c]:
        arr, bs = _in_spec(spec, tr)
        args.append(arr)
        in_specs.append(bs)
    out_specs, out_shape = [], []
    for k in want:
        if ins[k][0] == "full":
            arr = ins[k][1]
            out_specs.append(pl.BlockSpec(arr.shape, functools.partial(lambda i, n: (0,) * n, n=arr.ndim)))
            out_shape.append(jax.ShapeDtypeStruct(arr.shape, F32))
        else:
            w = ins[k][-1]
            out_specs.append(pl.BlockSpec((tr, w), lambda i: (i, 0)))
            out_shape.append(jax.ShapeDtypeStruct((T, w), BF16 if k in narrow else F32))
    return pl.pallas_call(body, out_shape=out_shape, grid=(T // tr,), in_specs=in_specs, out_specs=out_specs,
                          name=name, compiler_params=_cparams(("arbitrary",)))(*args)


def matmul(name, a, b, mode, add=None, out_dtype=F32):
    if mode == "tn":
        K, M = a.shape
        N = b.shape[1]
        tm = M if M <= 1536 else 512
        tn = N if N <= 1280 else (N // 2 if (N // 2) % 128 == 0 else N)
        tk = min(K, 512)
        grid = (M // tm, N // tn, K // tk)

        def body(a_ref, b_ref, o_ref):
            @pl.when(pl.program_id(2) == 0)
            def _():
                o_ref[...] = jnp.zeros_like(o_ref)
            o_ref[...] += dot_tn(a_ref[...], b_ref[...])

        return pl.pallas_call(
            body, out_shape=jax.ShapeDtypeStruct((M, N), F32), grid=grid,
            in_specs=[pl.BlockSpec((tk, tm), lambda i, j, k: (k, i)), pl.BlockSpec((tk, tn), lambda i, j, k: (k, j))],
            out_specs=pl.BlockSpec((tm, tn), lambda i, j, k: (i, j)), name=name,
            compiler_params=_cparams(("arbitrary", "arbitrary", "arbitrary")))(a, b)

    M, K = a.shape
    N = b.shape[1] if mode == "nn" else b.shape[0]
    tm = min(M, 512)
    tn = N if N <= 1536 else (N // 2 if (N // 2) % 128 == 0 else (N // 3 if (N // 3) % 128 == 0 else N))
    grid = (N // tn, M // tm)
    n_in = 2 + (add is not None)

    def body(*refs):
        a_ref, b_ref = refs[0], refs[1]
        o_ref = refs[n_in]
        acc = dot_nn(a_ref[...], b_ref[...]) if mode == "nn" else dot_nt(a_ref[...], b_ref[...])
        if add is not None:
            acc = acc + refs[2][...]
        o_ref[...] = acc.astype(o_ref.dtype)

    in_specs = [pl.BlockSpec((tm, K), lambda j, i: (i, 0)),
                pl.BlockSpec((K, tn), lambda j, i: (0, j)) if mode == "nn" else pl.BlockSpec((tn, K), lambda j, i: (j, 0))]
    args = [a, b]
    if add is not None:
        in_specs.append(pl.BlockSpec((tm, tn), lambda j, i: (i, j)))
        args.append(add)
    return pl.pallas_call(
        body, out_shape=jax.ShapeDtypeStruct((M, N), out_dtype), grid=grid, in_specs=in_specs,
        out_specs=pl.BlockSpec((tm, tn), lambda j, i: (i, j)), name=name,
        compiler_params=_cparams(("arbitrary", "arbitrary")))(*args)


def norm_project(name, x, g, w):
    T, D = x.shape
    N = w.shape[1]
    tm = min(T, 512)

    def body(x_ref, g_ref, w_ref, h_ref, p_ref):
        h = _rms(x_ref[...], g_ref[...]).astype(BF16)
        h_ref[...] = h
        p_ref[...] = dot_nn(h, w_ref[...])

    return pl.pallas_call(
        body, out_shape=[jax.ShapeDtypeStruct((T, D), BF16), jax.ShapeDtypeStruct((T, N), F32)], grid=(T // tm,),
        in_specs=[pl.BlockSpec((tm, D), lambda i: (i, 0)), pl.BlockSpec((1, D), lambda i: (0, 0)),
                  pl.BlockSpec((D, N), lambda i: (0, 0))],
        out_specs=[pl.BlockSpec((tm, D), lambda i: (i, 0)), pl.BlockSpec((tm, N), lambda i: (i, 0))],
        name=name, compiler_params=_cparams(("arbitrary",)))(x, g, w)


def norm_project_bwd(name, assemble, pieces, w, x, g, dy):
    T, D = x.shape
    N = w.shape[1]
    tm = min(T, 256)
    n_in = len(pieces)

    def body(*refs):
        w_ref, x_ref, g_ref, dy_ref, dp_ref, dx_ref, dg_ref = refs[n_in:]

        @pl.when(pl.program_id(0) == 0)
        def _():
            dg_ref[...] = jnp.zeros_like(dg_ref)

        (dp,) = assemble(*[r[...] for r in refs[:n_in]])
        dp_ref[...] = dp
        _, vjp = jax.vjp(_rms, x_ref[...], g_ref[...])
        dx, dg = vjp(dot_nt(dp, w_ref[...]))
        dx_ref[...] = dx + dy_ref[...]
        dg_ref[...] += dg

    in_specs, args = [], []
    for spec in pieces:
        arr, bs = _in_spec(spec, tm)
        args.append(arr)
        in_specs.append(bs)
    row = pl.BlockSpec((tm, D), lambda i: (i, 0))
    vec = pl.BlockSpec((1, D), lambda i: (0, 0))
    wide = pl.BlockSpec((tm, N), lambda i: (i, 0))
    return pl.pallas_call(
        body,
        out_shape=[jax.ShapeDtypeStruct((T, N), BF16), jax.ShapeDtypeStruct((T, D), F32), jax.ShapeDtypeStruct((1, D), F32)],
        grid=(T // tm,), in_specs=in_specs + [pl.BlockSpec((D, N), lambda i: (0, 0)), row, vec, row],
        out_specs=[wide, row, vec], name=name, compiler_params=_cparams(("arbitrary",)))(*args, w, x, g, dy)


def mix_project(name, tile_fn, T, ins, w, x):
    tr = _row_tile(T)
    n_in = len(ins)
    K, D = w.shape

    def body(*refs):
        w_ref, x_ref, y_ref = refs[n_in:]
        (mix,) = tile_fn(*[r[...] for r in refs[:n_in]])
        y_ref[...] = x_ref[...] + dot_nn(mix, w_ref[...])

    in_specs, args = [], []
    for spec in ins:
        arr, bs = _in_spec(spec, tr)
        args.append(arr)
        in_specs.append(bs)
    row = pl.BlockSpec((tr, D), lambda i: (i, 0))
    return pl.pallas_call(
        body, out_shape=jax.ShapeDtypeStruct((T, D), F32), grid=(T // tr,),
        in_specs=in_specs + [pl.BlockSpec((K, D), lambda i: (0, 0)), row], out_specs=row,
        name=name, compiler_params=_cparams(("arbitrary",)))(*args, w, x)


def mix_project_bwd(name, tile_fn, T, ins, w, dy, skip=(), narrow=()):
    tr = _row_tile(T)
    n_in = len(ins)
    K, D = w.shape
    want = [k for k in range(n_in) if k not in skip]

    def body(*refs):
        w_ref, dy_ref = refs[n_in:n_in + 2]
        outs, dw_ref = refs[n_in + 2:-1], refs[-1]
        first = pl.program_id(0) == 0
        (mix,), vjp = jax.vjp(tile_fn, *[r[...] for r in refs[:n_in]])
        d = dy_ref[...].astype(BF16)
        grads = vjp((dot_nt(d, w_ref[...]),))

        @pl.when(first)
        def _():
            dw_ref[...] = jnp.zeros_like(dw_ref)

        dw_ref[...] += dot_tn(mix, d)
        for r, k in zip(outs, want):
            if ins[k][0] == "full":
                @pl.when(first)
                def _():
                    r[...] = jnp.zeros_like(r)
                r[...] += grads[k]
            else:
                r[...] = grads[k].astype(r.dtype)

    in_specs, args = [], []
    for spec in ins:
        arr, bs = _in_spec(spec, tr)
        args.append(arr)
        in_specs.append(bs)
    out_specs, out_shape = [], []
    for k in want:
        if ins[k][0] == "full":
            arr = ins[k][1]
            out_specs.append(_full_spec(arr))
            out_shape.append(jax.ShapeDtypeStruct(arr.shape, F32))
        else:
            wd = ins[k][-1]
            out_specs.append(pl.BlockSpec((tr, wd), lambda i: (i, 0)))
            out_shape.append(jax.ShapeDtypeStruct((T, wd), BF16 if k in narrow else F32))
    wspec = pl.BlockSpec((K, D), lambda i: (0, 0))
    return pl.pallas_call(
        body, out_shape=out_shape + [jax.ShapeDtypeStruct((K, D), F32)], grid=(T // tr,),
        in_specs=in_specs + [wspec, pl.BlockSpec((tr, D), lambda i: (i, 0))], out_specs=out_specs + [wspec],
        name=name, compiler_params=_cparams(("arbitrary",)))(*args, w, dy)


def _attn_heads(n):
    G = N_Q_A // N_KV_A
    k_sl = pl.ds(n * HEAD_DIM_A, HEAD_DIM_A)
    v_sl = pl.ds(W_KV_A + n * HEAD_DIM_A, HEAD_DIM_A)
    q_sl = [pl.ds((n * G + g) * HEAD_DIM_A, HEAD_DIM_A) for g in range(G)]
    return k_sl, v_sl, q_sl, range(n * G, (n + 1) * G)


def attn_fwd(p, q_off, kvp, sink, slopes, T):
    nb = T // BLOCK
    assert q_off % W_A == 0

    def body(q_ref, kv_ref, sink_ref, slope_ref, o_ref):
        c = pl.program_id(0)
        rows = pl.ds(pl.multiple_of(c * BLOCK, BLOCK), 3 * BLOCK)
        for n in range(N_KV_A):
            k_sl, v_sl, q_sl, heads = _attn_heads(n)
            group = pl.ds(n * ATTN_GROUP * BLOCK, ATTN_GROUP * BLOCK)
            q = jnp.concatenate([q_ref[:, s] for s in q_sl], axis=0)
            o = attn_block(q, kv_ref[rows, k_sl], kv_ref[rows, v_sl], sink_ref[group, :], slope_ref[group, :], c, T)
            for g, s in enumerate(q_sl):
                o_ref[:, s] = o[g * BLOCK:(g + 1) * BLOCK]

    full = lambda a: pl.BlockSpec(a.shape, functools.partial(lambda c, nd: (0,) * nd, nd=a.ndim))
    return pl.pallas_call(
        body, out_shape=jax.ShapeDtypeStruct((T, W_A), F32), grid=(nb,),
        in_specs=[pl.BlockSpec((BLOCK, W_A), lambda c: (c, q_off // W_A)), full(kvp), full(sink), full(slopes)],
        out_specs=pl.BlockSpec((BLOCK, W_A), lambda c: (c, 0)),
        name="attn_fwd", compiler_params=_cparams(("arbitrary",)))(p, kvp, sink, slopes)


def attn_bwd(p, q_off, kvp, sink, slopes, do, T):
    nb = T // BLOCK

    def body(q_ref, kv_ref, sink_ref, slope_ref, do_ref, dq_ref, dkv_ref, dsink_ref):
        c = pl.program_id(0)

        @pl.when(c == 0)
        def _():
            dkv_ref[...] = jnp.zeros_like(dkv_ref)
            dsink_ref[...] = jnp.zeros_like(dsink_ref)

        rows = pl.ds(pl.multiple_of(c * BLOCK, BLOCK), 3 * BLOCK)
        for n in range(N_KV_A):
            k_sl, v_sl, q_sl, heads = _attn_heads(n)
            group = pl.ds(n * ATTN_GROUP * BLOCK, ATTN_GROUP * BLOCK)
            slope = slope_ref[group, :]
            q = jnp.concatenate([q_ref[:, s] for s in q_sl], axis=0)
            do = jnp.concatenate([do_ref[:, s] for s in q_sl], axis=0)
            _, vjp = jax.vjp(lambda q_, kk, vv, sk: attn_block(q_, kk, vv, sk, slope, c, T),
                             q, kv_ref[rows, k_sl], kv_ref[rows, v_sl], sink_ref[group, :])
            dq, dks, dvs, dsk = vjp(do)
            dkv_ref[rows, k_sl] += dks
            dkv_ref[rows, v_sl] += dvs
            for g, (s, h) in enumerate(zip(q_sl, heads)):
                seg = slice(g * BLOCK, (g + 1) * BLOCK)
                dq_ref[:, s] = dq[seg].astype(dq_ref.dtype)
                dsink_ref[h] += jnp.sum(dsk[seg], axis=0, keepdims=True)

    full = lambda a: pl.BlockSpec(a.shape, functools.partial(lambda c, nd: (0,) * nd, nd=a.ndim))
    qspec = pl.BlockSpec((BLOCK, W_A), lambda c: (c, 0))
    return pl.pallas_call(
        body,
        out_shape=[jax.ShapeDtypeStruct((T, W_A), BF16), jax.ShapeDtypeStruct(kvp.shape, F32),
                   jax.ShapeDtypeStruct((N_Q_A, 1, 1), F32)],
        grid=(nb,),
        in_specs=[pl.BlockSpec((BLOCK, W_A), lambda c: (c, q_off // W_A)), full(kvp), full(sink), full(slopes), qspec],
        out_specs=[qspec, full(kvp), pl.BlockSpec((N_Q_A, 1, 1), lambda c: (0, 0, 0))],
        name="attn_bwd", compiler_params=_cparams(("arbitrary",)))(p, kvp, sink, slopes, do)


def mem_fwd(p, q_off, kv, T):
    tr = min(T, 2 * _row_tile(T))
    assert q_off % W_M == 0

    def body(q_ref, kv_ref, o_ref):
        for h in range(N_HEADS_M):
            hs = pl.ds(h * HEAD_DIM_M, HEAD_DIM_M)
            (o,) = mem_tile(q_ref[:, hs], kv_ref[:, hs], kv_ref[:, pl.ds(W_M + h * HEAD_DIM_M, HEAD_DIM_M)])
            o_ref[:, hs] = o

    return pl.pallas_call(
        body, out_shape=jax.ShapeDtypeStruct((T, W_M), F32), grid=(T // tr,),
        in_specs=[pl.BlockSpec((tr, W_M), lambda i: (i, q_off // W_M)), pl.BlockSpec((N_MEM, 2 * W_M), lambda i: (0, 0))],
        out_specs=pl.BlockSpec((tr, W_M), lambda i: (i, 0)),
        name="mem_fwd", compiler_params=_cparams(("arbitrary",)))(p, kv)


def mem_bwd(p, q_off, kv, do, T):
    tr = min(T, 2 * _row_tile(T))

    def body(q_ref, kv_ref, do_ref, dq_ref, dkv_ref):
        @pl.when(pl.program_id(0) == 0)
        def _():
            dkv_ref[...] = jnp.zeros_like(dkv_ref)

        for h in range(N_HEADS_M):
            hs = pl.ds(h * HEAD_DIM_M, HEAD_DIM_M)
            vs = pl.ds(W_M + h * HEAD_DIM_M, HEAD_DIM_M)
            _, vjp = jax.vjp(mem_tile, q_ref[:, hs], kv_ref[:, hs], kv_ref[:, vs])
            dq, dk, dv = vjp((do_ref[:, hs],))
            dq_ref[:, hs] = dq.astype(dq_ref.dtype)
            dkv_ref[:, hs] += dk
            dkv_ref[:, vs] += dv

    kvspec = pl.BlockSpec((N_MEM, 2 * W_M), lambda i: (0, 0))
    return pl.pallas_call(
        body,
        out_shape=[jax.ShapeDtypeStruct((T, W_M), BF16), jax.ShapeDtypeStruct((N_MEM, 2 * W_M), F32)],
        grid=(T // tr,),
        in_specs=[pl.BlockSpec((tr, W_M), lambda i: (i, q_off // W_M)), kvspec, pl.BlockSpec((tr, W_M), lambda i: (i, 0))],
        out_specs=[pl.BlockSpec((tr, W_M), lambda i: (i, 0)), kvspec],
        name="mem_bwd", compiler_params=_cparams(("arbitrary",)))(p, kv, do)


def _scan_const_specs(dk):
    C, L = SCAN_CHUNK, SCAN_LEVELS
    return [pl.BlockSpec((2, (2 + L) * C, C), lambda n: (0, 0, 0)),
            pl.BlockSpec((2, C, (2 + L) * C), lambda n: (0, 0, 0)),
            pl.BlockSpec((2, L * C, dk), lambda n: (0, 0, 0)),
            pl.BlockSpec((2, L * C, C), lambda n: (0, 0, 0))]


def _chunk_spec(src, width, chunk_of):
    arr, sel = src
    if arr.ndim == 2:
        assert sel % width == 0
        return pl.BlockSpec((SCAN_CHUNK * SCAN_SUB, width),
                            functools.partial(lambda n, b: (chunk_of(n), b), b=sel // width))
    return pl.BlockSpec((None, SCAN_CHUNK * SCAN_SUB, width), functools.partial(lambda n, d: (d, chunk_of(n), 0), d=sel))


def _scan_const_args():
    h, ht, qm, pm = _scan_consts()
    return [jnp.asarray(h, BF16), jnp.asarray(ht, BF16), jnp.asarray(qm, F32), jnp.asarray(pm, F32)]


def _full_spec(a):
    return pl.BlockSpec(a.shape, functools.partial(lambda n, nd: (0,) * nd, nd=a.ndim))


def scan_fwd(name, prep, raws, params, heads, dk, dv, T):
    C, S = SCAN_CHUNK, SCAN_SUB
    N = T // (C * S)
    assert dk == C
    Wv = heads * dv
    orders = (lambda n: n, lambda n: N - 1 - n)
    n_raw, n_par = [len(r) for r in raws], [len(p) for p in params]

    def body(*refs):
        pos, raw_refs, par_refs = 0, [], []
        for d in range(2):
            raw_refs.append(refs[pos:pos + n_raw[d]])
            pos += n_raw[d]
        for d in range(2):
            par_refs.append(refs[pos:pos + n_par[d]])
            pos += n_par[d]
        h_ref, ht_ref, qm_ref, pm_ref = refs[pos:pos + 4]
        o_refs, ss_refs, st_ref = refs[pos + 4:pos + 6], refs[pos + 6:pos + 8], refs[pos + 8]

        @pl.when(pl.program_id(0) == 0)
        def _():
            st_ref[...] = jnp.zeros_like(st_ref)

        for d in range(2):
            consts = (qm_ref[d], pm_ref[d])
            pars = [p[...] for p in par_refs[d]]
            for sub in (range(S) if d == 0 else reversed(range(S))):
                rows = pl.ds(sub * C, C)
                q, k, v, g = prep([r[rows, :] for r in raw_refs[d]], pars)
                e = _split_mm(h_ref[d], g)
                tot = jnp.sum(g, axis=0, keepdims=True)
                for h in range(heads):
                    ks, vs = slice(h * dk, (h + 1) * dk), slice(h * dv, (h + 1) * dv)
                    st = st_ref[d, h]
                    ss_refs[d][h, sub] = st
                    o, st_new = scan_chunk(q[:, ks], k[:, ks], v[:, vs], e[:, ks], tot[:, ks], st, *consts)
                    o_refs[d][rows, vs] = o
                    st_ref[d, h] = st_new

    ss_spec = lambda order: pl.BlockSpec((heads, S, dv, dk), lambda n: (0, order(n), 0, 0))
    return pl.pallas_call(
        body,
        out_shape=[jax.ShapeDtypeStruct((T, Wv), F32)] * 2 + [jax.ShapeDtypeStruct((heads, T // C, dv, dk), F32)] * 2,
        grid=(N,),
        in_specs=[_chunk_spec(s, w, orders[d]) for d in range(2) for s, w in raws[d]]
        + [_full_spec(p) for d in range(2) for p in params[d]] + _scan_const_specs(dk),
        out_specs=[pl.BlockSpec((C * S, Wv), lambda n: (orders[0](n), 0)),
                   pl.BlockSpec((C * S, Wv), lambda n: (orders[1](n), 0)), ss_spec(orders[0]), ss_spec(orders[1])],
        scratch_shapes=[pltpu.VMEM((2, heads, dv, dk), F32)],
        name=name, compiler_params=_cparams(("arbitrary",)))(
            *[s[0] for d in range(2) for s, _ in raws[d]], *[p for d in range(2) for p in params[d]], *_scan_const_args())


def scan_bwd(name, prep, raws, params, ss, do, heads, dk, dv, T):
    C, S = SCAN_CHUNK, SCAN_SUB
    N = T // (C * S)
    Wv = heads * dv
    orders = (lambda n: N - 1 - n, lambda n: n)
    n_raw, n_par = [len(r) for r in raws], [len(p) for p in params]

    def body(*refs):
        pos, raw_refs, par_refs, draw_refs, dpar_refs = 0, [], [], [], []
        for group, counts in ((raw_refs, n_raw), (par_refs, n_par)):
            for d in range(2):
                group.append(refs[pos:pos + counts[d]])
                pos += counts[d]
        ss_refs, do_refs = refs[pos:pos + 2], refs[pos + 2:pos + 4]
        h_ref, ht_ref, qm_ref, pm_ref = refs[pos + 4:pos + 8]
        pos += 8
        for group, counts in ((draw_refs, n_raw), (dpar_refs, n_par)):
            for d in range(2):
                group.append(refs[pos:pos + counts[d]])
                pos += counts[d]
        dst_ref = refs[pos]

        @pl.when(pl.program_id(0) == 0)
        def _():
            dst_ref[...] = jnp.zeros_like(dst_ref)
            for d in range(2):
                for r in dpar_refs[d]:
                    r[...] = jnp.zeros_like(r)

        for d in range(2):
            consts = (qm_ref[d], pm_ref[d])
            pars = [p[...] for p in par_refs[d]]
            for sub in (reversed(range(S)) if d == 0 else range(S)):
                rows = pl.ds(sub * C, C)
                (q, k, v, g), prep_vjp = jax.vjp(prep, [r[rows, :] for r in raw_refs[d]], pars)
                e = _split_mm(h_ref[d], g)
                tot = jnp.sum(g, axis=0, keepdims=True)
                dqs, dks, dvs, des, dtots = [], [], [], [], []
                for h in range(heads):
                    ks, vs = slice(h * dk, (h + 1) * dk), slice(h * dv, (h + 1) * dv)
                    _, vjp = jax.vjp(lambda q_, k_, v_, e_, t_, st_: scan_chunk(q_, k_, v_, e_, t_, st_, *consts),
                                     q[:, ks], k[:, ks], v[:, vs], e[:, ks], tot[:, ks], ss_refs[d][h, sub])
                    dq, dk_, dv_, de, dtot, dst = vjp((do_refs[d][rows, vs], dst_ref[d, h]))
                    dst_ref[d, h] = dst
                    for group, val in ((dqs, dq), (dks, dk_), (dvs, dv_), (des, de), (dtots, dtot)):
                        group.append(val)
                cat = lambda parts: jnp.concatenate(parts, axis=-1)
                dg = _split_mm(ht_ref[d], cat(des)) + cat(dtots)
                draws, dpars = prep_vjp((cat(dqs), cat(dks), cat(dvs), dg))
                for r, val in zip(draw_refs[d], draws):
                    r[rows, :] = val.astype(r.dtype)
                for r, val in zip(dpar_refs[d], dpars):
                    r[...] += val

    ss_spec = lambda order: pl.BlockSpec((heads, S, dv, dk), lambda n: (0, order(n), 0, 0))
    row_out = lambda w, order: pl.BlockSpec((C * S, w), lambda n: (order(n), 0))
    return pl.pallas_call(
        body,
        out_shape=[jax.ShapeDtypeStruct((T, w), BF16) for d in range(2) for _, w in raws[d]]
        + [jax.ShapeDtypeStruct(p.shape, F32) for d in range(2) for p in params[d]],
        grid=(N,),
        in_specs=[_chunk_spec(s, w, orders[d]) for d in range(2) for s, w in raws[d]]
        + [_full_spec(p) for d in range(2) for p in params[d]]
        + [ss_spec(orders[0]), ss_spec(orders[1]), _chunk_spec(do, Wv, orders[0]), _chunk_spec(do, Wv, orders[1])]
        + _scan_const_specs(dk),
        out_specs=[row_out(w, orders[d]) for d in range(2) for _, w in raws[d]]
        + [_full_spec(p) for d in range(2) for p in params[d]],
        scratch_shapes=[pltpu.VMEM((2, heads, dv, dk), F32)],
        name=name, compiler_params=_cparams(("arbitrary",)))(
            *[s[0] for d in range(2) for s, _ in raws[d]], *[p for d in range(2) for p in params[d]],
            ss[0], ss[1], do[0], do[0], *_scan_const_args())


def final_call(x, g, target, T):
    tr = _row_tile(T)

    def tile(xv, gv, tv):
        y = _rms(xv, gv)
        err = (y - tv) ** 2
        return jnp.sum(jnp.sum(err, axis=-1, keepdims=True), axis=0, keepdims=True) * (0.5 / D_MODEL)

    def body(x_ref, g_ref, t_ref, loss_ref, dx_ref, dg_ref):
        i = pl.program_id(0)
        tv = t_ref[...]
        lv, vjp = jax.vjp(lambda a, b: tile(a, b, tv), x_ref[...], g_ref[...])
        dx, dg = vjp(jnp.ones((1, 1), F32))
        dx_ref[...] = dx

        @pl.when(i == 0)
        def _():
            loss_ref[...] = jnp.zeros_like(loss_ref)
            dg_ref[...] = jnp.zeros_like(dg_ref)

        loss_ref[...] += jnp.broadcast_to(lv, loss_ref.shape)
        dg_ref[...] += dg

    return pl.pallas_call(
        body,
        out_shape=[jax.ShapeDtypeStruct((8, 128), F32), jax.ShapeDtypeStruct((T, D_MODEL), F32),
                   jax.ShapeDtypeStruct((1, D_MODEL), F32)],
        grid=(T // tr,),
        in_specs=[pl.BlockSpec((tr, D_MODEL), lambda i: (i, 0)), pl.BlockSpec((1, D_MODEL), lambda i: (0, 0)),
                  pl.BlockSpec((tr, D_MODEL), lambda i: (i, 0))],
        out_specs=[pl.BlockSpec((8, 128), lambda i: (0, 0)), pl.BlockSpec((tr, D_MODEL), lambda i: (i, 0)),
                   pl.BlockSpec((1, D_MODEL), lambda i: (0, 0))],
        name="final_loss", compiler_params=_cparams(("arbitrary",)))(x, g, target)


def adamw_call(w, g, m, v):
    shape = w.shape
    c = shape[-1]
    r = int(np.prod(shape[:-1])) if len(shape) > 1 else 1
    tr = r if r <= 256 else 256
    assert r % tr == 0

    def body(w_ref, g_ref, m_ref, v_ref, d_ref, nm_ref, nv_ref):
        gv = g_ref[...]
        nm = ADAM_B1 * m_ref[...] + (1.0 - ADAM_B1) * gv
        nv = ADAM_B2 * v_ref[...] + (1.0 - ADAM_B2) * jnp.square(gv)
        m_hat = nm / (1.0 - ADAM_B1 ** ADAM_STEP)
        v_hat = nv / (1.0 - ADAM_B2 ** ADAM_STEP)
        d_ref[...] = -ADAM_LR * (m_hat / (jnp.sqrt(v_hat) + ADAM_EPS) + ADAM_WD * w_ref[...])
        nm_ref[...] = nm
        nv_ref[...] = nv

    spec = pl.BlockSpec((tr, c), lambda i: (i, 0))
    outs = pl.pallas_call(body, out_shape=[jax.ShapeDtypeStruct((r, c), F32)] * 3, grid=(r // tr,),
                          in_specs=[spec] * 4, out_specs=[spec] * 3, name="adamw",
                          compiler_params=_cparams(("arbitrary",)))(*(t.reshape(r, c) for t in (w, g, m, v)))
    return tuple(o.reshape(shape) for o in outs)


def adamw_halves(w, mine, other, m, v, c):
    L, R, C = w.shape
    by_cols = mine.shape[-1] != C
    if by_cols:
        tile, nbh = (R, C // 2), 1
        full_idx = lambda l, i: (l, 0, i)
    else:
        rh = R // 2
        tr = rh if rh <= 256 else rh // 2
        assert tr % 8 == 0
        tile, nbh = (tr, C), rh // tr
        full_idx = lambda l, i: (l, i, 0)

    def body(c_ref, w_ref, a_ref, b_ref, m_ref, v_ref, g_ref, d_ref, nm_ref, nv_ref):
        is_mine = (pl.program_id(1) // nbh) == c_ref[0]
        gv = jnp.where(is_mine, a_ref[...], b_ref[...])
        nm = ADAM_B1 * m_ref[...] + (1.0 - ADAM_B1) * gv
        nv = ADAM_B2 * v_ref[...] + (1.0 - ADAM_B2) * jnp.square(gv)
        m_hat = nm / (1.0 - ADAM_B1 ** ADAM_STEP)
        v_hat = nv / (1.0 - ADAM_B2 ** ADAM_STEP)
        g_ref[...] = gv
        d_ref[...] = -ADAM_LR * (m_hat / (jnp.sqrt(v_hat) + ADAM_EPS) + ADAM_WD * w_ref[...])
        nm_ref[...] = nm
        nv_ref[...] = nv

    full = pl.BlockSpec((None,) + tile, lambda l, i, c_ref: full_idx(l, i))
    half = pl.BlockSpec((None,) + tile, lambda l, i, c_ref: (l, i % nbh, 0))
    grid_spec = pltpu.PrefetchScalarGridSpec(num_scalar_prefetch=1, grid=(L, 2 * nbh),
                                             in_specs=[full, half, half, full, full], out_specs=[full] * 4)
    return pl.pallas_call(body, out_shape=[jax.ShapeDtypeStruct(w.shape, F32)] * 4, grid_spec=grid_spec,
                          name="adamw_halves", compiler_params=_cparams(("arbitrary", "arbitrary")))(c, w, mine, other, m, v)


def sum_devices(g64):
    def body(x_ref, o_ref):
        acc = x_ref[0:8, :]
        for d in range(1, 8):
            acc = acc + x_ref[8 * d:8 * d + 8, :]
        o_ref[...] = acc

    return pl.pallas_call(body, out_shape=jax.ShapeDtypeStruct((8, D_MODEL), F32), name="sum_devices")(g64)


def _half_tile(rh):
    if rh <= 512:
        return rh
    return next(rh // d for d in range(2, rh) if rh % d == 0 and (rh // d) % 16 == 0 and rh // d <= 512)


def _half_geometry(full_shape, half_shape):
    R, C = full_shape[-2:]
    if half_shape[-1] != C:
        return (R, C // 2), 1, lambda i, c: (0, c)
    tr = _half_tile(R // 2)
    nblk = (R // 2) // tr
    return (tr, C), nblk, lambda i, c: (i + c * nblk, 0)


def _work_items(counts):
    starts = [int(v) for v in np.cumsum([0] + list(counts[:-1]))]
    local = lambda a, s: jnp.clip(s - starts[a], 0, counts[a] - 1)
    return starts, int(sum(counts)), local


def add_sibling(gs, recvs, c, out_dtypes):
    n = len(gs)
    geo = [_half_geometry(g.shape, r.shape) for g, r in zip(gs, recvs)]
    counts = [4 * nblk for _, nblk, _ in geo]
    starts, total, local = _work_items(counts)

    def body(c_ref, *refs):
        s = pl.program_id(0)
        for a in range(n):
            g_ref, r_ref, o_ref = refs[a], refs[n + a], refs[2 * n + a]

            @pl.when((s >= starts[a]) & (s < starts[a] + counts[a]))
            def _():
                o_ref[...] = (g_ref[...] + r_ref[...]).astype(o_ref.dtype)

    def own_idx(s, c_ref, a):
        _, nblk, own = geo[a]
        k = local(a, s)
        return (k // nblk,) + own(k % nblk, c_ref[0])

    def half_idx(s, c_ref, a):
        k = local(a, s)
        return (k // geo[a][1], k % geo[a][1], 0)

    halves = [pl.BlockSpec((None,) + geo[a][0], functools.partial(half_idx, a=a)) for a in range(n)]
    grid_spec = pltpu.PrefetchScalarGridSpec(
        num_scalar_prefetch=1, grid=(total,),
        in_specs=[pl.BlockSpec((None,) + geo[a][0], functools.partial(own_idx, a=a)) for a in range(n)] + halves,
        out_specs=halves)
    return pl.pallas_call(body, out_shape=[jax.ShapeDtypeStruct(r.shape, dt) for r, dt in zip(recvs, out_dtypes)],
                          grid_spec=grid_spec, name="rs_add_sibling",
                          compiler_params=_cparams(("arbitrary",)))(c, *gs, *recvs)


def add_chips(gs, recvs, r3s, place):
    n = len(gs)
    geo = [_half_geometry(g.shape, r.shape) for g, r in zip(gs, recvs)]
    counts = [nblk for _, nblk, _ in geo]
    starts, total, local = _work_items(counts)

    def body(p_ref, *refs):
        s = pl.program_id(0)
        up = lambda r: r[...].astype(F32)
        for a in range(n):
            g_ref, s_ref, o_ref = refs[a], refs[n + a], refs[5 * n + a]
            a_ref, b_ref, c_ref = refs[2 * n + 3 * a:2 * n + 3 * a + 3]

            @pl.when((s >= starts[a]) & (s < starts[a] + counts[a]))
            def _():
                o_ref[...] = (((g_ref[...] + up(s_ref)) + up(a_ref)) + up(b_ref)) + up(c_ref)

    own_idx = lambda s, p_ref, a: (p_ref[0],) + geo[a][2](local(a, s), p_ref[1])
    sib_idx = lambda s, p_ref, a: (p_ref[0], local(a, s), 0)
    chip_idx = lambda s, p_ref, a, k: (k, local(a, s), 0)
    spec = lambda a, idx, **kw: pl.BlockSpec((None,) + geo[a][0], functools.partial(idx, a=a, **kw))
    grid_spec = pltpu.PrefetchScalarGridSpec(
        num_scalar_prefetch=1, grid=(total,),
        in_specs=[spec(a, own_idx) for a in range(n)] + [spec(a, sib_idx) for a in range(n)]
        + [spec(a, chip_idx, k=k) for a in range(n) for k in range(3)],
        out_specs=[pl.BlockSpec(geo[a][0], functools.partial(lambda s, p_ref, a: (local(a, s), 0), a=a))
                   for a in range(n)])
    return pl.pallas_call(body, out_shape=[jax.ShapeDtypeStruct(r.shape[1:], F32) for r in recvs],
                          grid_spec=grid_spec, name="rs_add_chips", compiler_params=_cparams(("arbitrary",)))(
                              place, *gs, *recvs, *[r for r3 in r3s for r in (r3, r3, r3)])


def _remote(src, dst, ssem, rsem, dev):
    return pltpu.make_async_remote_copy(src_ref=src, dst_ref=dst, send_sem=ssem, recv_sem=rsem,
                                        device_id=dev, device_id_type=pl.DeviceIdType.MESH)


def _mesh_places():
    x, y, c = lax.axis_index("x"), lax.axis_index("y"), lax.axis_index("c")
    chips = [(1 - x, y), (x, 1 - y), (1 - x, 1 - y)]
    return x, y, c, (x, y, 1 - c), chips


def _hbm_specs(n):
    return [pl.BlockSpec(memory_space=pltpu.HBM) for _ in range(n)]


def _gather_body(ins, outs, n_split, send_sems, recv_sems, handshake):
    x, y, c, sibling, chips = _mesh_places()
    mine = 2 * x + y
    if handshake:
        barrier = pltpu.get_barrier_semaphore()
        peers = [sibling] + [(*chip, c) for chip in chips]
        for peer in peers:
            pl.semaphore_signal(barrier, inc=1, device_id=peer, device_id_type=pl.DeviceIdType.MESH)
        pl.semaphore_wait(barrier, len(peers))

    def half(a, chip_idx, which):
        rh = ins[a].shape[0] // 2
        return outs[a].at[chip_idx, pl.ds(which * rh, rh), :]

    sent = []
    for a in range(len(ins)):
        for k, chip in enumerate(chips):
            if a < n_split:
                rh = ins[a].shape[0] // 2
                src, dst = ins[a].at[pl.ds(c * rh, rh), :], half(a, mine, c)
            else:
                src, dst = ins[a], outs[a].at[mine]
            sent.append(_remote(src, dst, send_sems.at[a, k], recv_sems.at[a, k], (*chip, c)))
    for cp in sent:
        cp.start()
    for a in range(len(ins)):
        for k, chip in enumerate(chips):
            j = 2 * chip[0] + chip[1]
            region = half(a, j, c) if a < n_split else outs[a].at[j]
            _remote(region, region, send_sems.at[a, k], recv_sems.at[a, k], (*chip, c)).wait_recv()
            if a < n_split:
                fwd = _remote(region, region, send_sems.at[a, 3 + k], recv_sems.at[a, 3 + k], sibling)
                fwd.start()
                sent.append(fwd)
    for a in range(n_split):
        for k, chip in enumerate(chips):
            region = half(a, 2 * chip[0] + chip[1], 1 - c)
            _remote(region, region, send_sems.at[a, 3 + k], recv_sems.at[a, 3 + k], sibling).wait_recv()
    for cp in sent:
        cp.wait_send()


def gather_weights(shards, small):
    arrs = list(shards) + [small]
    n = len(arrs)

    def body(*refs):
        _gather_body(refs[:n], refs[n:2 * n], n - 1, refs[2 * n], refs[2 * n + 1], handshake=False)

    return pl.pallas_call(
        body, out_shape=[jax.ShapeDtypeStruct((4,) + a.shape, a.dtype) for a in arrs],
        in_specs=_hbm_specs(n), out_specs=_hbm_specs(n),
        scratch_shapes=[pltpu.SemaphoreType.DMA((n, 6)), pltpu.SemaphoreType.DMA((n, 6))],
        name="gather_weights")(*arrs)


def gather_weights_async(shards):
    n = len(shards)

    def body(*refs):
        _gather_body(refs[:n], refs[n:2 * n], n, refs[2 * n], refs[2 * n + 1], handshake=True)

    return pl.kernel(
        body, out_type=[jax.ShapeDtypeStruct((4,) + a.shape, a.dtype) for a in shards],
        mesh=plsc.ScalarSubcoreMesh(axis_name="seq", num_cores=1),
        scratch_types=[pltpu.SemaphoreType.DMA((n, 6)), pltpu.SemaphoreType.DMA((n, 6))],
        compiler_params=pltpu.CompilerParams(collective_id=1), name="gather_weights_async")(*shards)


def _sequencer_call(name, body, out_type, sem_shape, collective_id, args):
    return pl.kernel(
        body, out_type=out_type, mesh=plsc.ScalarSubcoreMesh(axis_name="seq", num_cores=1),
        scratch_types=[pltpu.SemaphoreType.DMA(sem_shape), pltpu.SemaphoreType.DMA(sem_shape)],
        compiler_params=pltpu.CompilerParams(collective_id=collective_id), name=name)(*args)


def _handshake(peers):
    barrier = pltpu.get_barrier_semaphore()
    for peer in peers:
        pl.semaphore_signal(barrier, inc=1, device_id=peer, device_id_type=pl.DeviceIdType.MESH)
    pl.semaphore_wait(barrier, len(peers))


def exchange_siblings(name, srcs, axes, collective_id):
    n = len(srcs)

    def body(*refs):
        ins, outs = refs[:n], refs[n:2 * n]
        send_sems, recv_sems = refs[2 * n:]
        x, y, c, sibling, chips = _mesh_places()
        _handshake([sibling])
        cps = []
        for a in range(n):
            src = ins[a]
            if axes[a] is not None:
                half = src.shape[axes[a]] // 2
                theirs = pl.ds((1 - c) * half, half)
                src = src.at[:, theirs, :] if axes[a] == 1 else src.at[:, :, theirs]
            cps.append(_remote(src, outs[a], send_sems.at[a], recv_sems.at[a], sibling))
        for cp in cps:
            cp.start()
        for cp in cps:
            cp.wait()

    def shape(g, axis):
        return g.shape if axis is None else tuple(d // 2 if k == axis else d for k, d in enumerate(g.shape))

    return _sequencer_call(name, body, [jax.ShapeDtypeStruct(shape(g, ax), g.dtype) for g, ax in zip(srcs, axes)],
                           (n,), collective_id, srcs)


def exchange_chips(name, s1s, collective_id):
    n = len(s1s)

    def body(*refs):
        ins, outs = refs[:n], refs[n:2 * n]
        send_sems, recv_sems = refs[2 * n:]
        x, y, c, sibling, chips = _mesh_places()
        _handshake([(*chip, c) for chip in chips])
        cps = []
        for a in range(n):
            for k, chip in enumerate(chips):
                cps.append(_remote(ins[a].at[2 * chip[0] + chip[1]], outs[a].at[k], send_sems.at[a, k],
                                   recv_sems.at[a, k], (*chip, c)))
        for cp in cps:
            cp.start()
        for cp in cps:
            cp.wait()

    return _sequencer_call(name, body, [jax.ShapeDtypeStruct((3,) + s.shape[1:], s.dtype) for s in s1s], (n, 3),
                           collective_id, s1s)


def allgather_small(v):
    m_per = v.shape[0]

    def body(x_ref, out_ref, send_sems, recv_sems, local_sem):
        x, y, c, sibling, chips = _mesh_places()
        me = (x, y, c)

        def rows(px, py, pc):
            return out_ref.at[pl.ds((4 * px + 2 * py + pc) * m_per, m_per), :]

        def copy(k, block, to, src=None):
            return _remote(rows(*block) if src is None else src, rows(*block), send_sems.at[k], recv_sems.at[k], to)

        mine = pltpu.make_async_copy(x_ref, rows(*me), local_sem)
        mine.start()
        first = [copy(0, me, sibling, src=x_ref)]
        first += [copy(1 + j, me, (*chip, c), src=x_ref) for j, chip in enumerate(chips)]
        for cp in first:
            cp.start()
        passed = [copy(4 + j, (*chip, c), sibling) for j, chip in enumerate(chips)]
        for j, chip in enumerate(chips):
            copy(1 + j, (*chip, c), me).wait_recv()
            passed[j].start()
        copy(0, sibling, me).wait_recv()
        for j, chip in enumerate(chips):
            copy(4 + j, (*chip, 1 - c), me).wait_recv()
        for cp in first + passed:
            cp.wait_send()
        mine.wait()

    return pl.pallas_call(
        body, out_shape=jax.ShapeDtypeStruct((8 * m_per, v.shape[1]), v.dtype),
        in_specs=[pl.BlockSpec(memory_space=pltpu.VMEM)], out_specs=pl.BlockSpec(memory_space=pltpu.VMEM),
        scratch_shapes=[pltpu.SemaphoreType.DMA((7,)), pltpu.SemaphoreType.DMA((7,)), pltpu.SemaphoreType.DMA],
        name="allgather_small")(v)


def _lower_bounds(lb_param):
    lbs = jax.nn.softmax(lb_param.astype(F32), axis=0)
    return jnp.cumsum(lbs, axis=0) - lbs[0]


def _even_fwd(x, i, W, lower, kv, slopes, T):
    O = EVEN_OFF
    g = W["norm_even"][i].reshape(1, D_MODEL)
    h, p = norm_project("mm_in_e", x, g, W["w_in_e"][i])
    kvp = jnp.pad(p[:, O["kA"]:O["kA"] + 2 * W_KV_A], ((BLOCK, BLOCK), (0, 0)))
    sink = jnp.repeat(W["sink"][i], BLOCK).reshape(N_Q_A * BLOCK, 1)
    a = attn_fwd(p, O["qA"], kvp, sink, slopes, T)
    scan_raws = [[((p, O["qB"]), W_B), ((p, O[z]), W_B), ((p, O["iB"]), W_B)] for z in ("zf", "zb")]
    scan_pars = [[lower[i][0:1]], [lower[i][1:2]]]
    o_f, o_b, ss_f, ss_b = scan_fwd("scan_fwd_h", hgrn_prep, scan_raws, scan_pars, N_HEADS_B, HEAD_DIM_B, HEAD_DIM_B, T)
    mo = mem_fwd(p, O["qM"], kv, T)
    hg = W["hgrn_norm"][i].reshape(1, W_B)
    post_ins = [("row", a, 0, W_A), ("row", o_f, 0, W_B), ("row", o_b, 0, W_B), ("row", mo, 0, W_M),
                ("row", p, O["gA"], W_A), ("row", p, O["gB"], W_B), ("row", p, O["gM"], W_M), ("full", hg)]
    x_new = mix_project("even_out", even_post_tile, T, post_ins, W["w_out_e"][i], x)
    return x_new, dict(x=x, g=g, h=h, p=p, kvp=kvp, sink=sink, scan_raws=scan_raws, scan_pars=scan_pars,
                       ss=(ss_f, ss_b), post_ins=post_ins)


def _add2(a, b):
    return a.astype(F32) + b.astype(F32)


def _assemble_even(dqA, dgA, dqB_f, dqB_b, dzf, dzb, diB_f, diB_b, dgB, dqM, dgM, dkvA):
    parts = [dqA, dgA, _add2(dqB_f, dqB_b), dzf, dzb, _add2(diB_f, diB_b), dgB, dqM, dgM, dkvA]
    return (jnp.concatenate([t.astype(BF16) for t in parts], axis=-1),)


def _even_bwd(dxo, sv, i, W, kv, slopes, T, sync):
    O = EVEN_OFF
    p = sv["p"]
    da, dof, dmo, dgA, dgB, dgM, dhg, dwo = mix_project_bwd("even_out_bwd", even_post_tile, T, sv["post_ins"],
                                                            W["w_out_e"][i], dxo, skip=(2,), narrow=(4, 5, 6))
    da = sync(da)
    dqA, dkvp, dsink = attn_bwd(p, O["qA"], sv["kvp"], sv["sink"], slopes, da, T)
    dkvA = dkvp[BLOCK:-BLOCK]
    dqB_f, dzf, diB_f, dqB_b, dzb, diB_b, dlow_f, dlow_b = scan_bwd(
        "scan_bwd_h", hgrn_prep, sv["scan_raws"], sv["scan_pars"], sv["ss"], (dof, 0), N_HEADS_B, HEAD_DIM_B, HEAD_DIM_B, T)
    dqB_f = sync(dqB_f)
    row = lambda arr, w: ("row", arr, 0, w)
    dlow = jnp.concatenate([dlow_f, dlow_b], axis=0)
    dqM, dkv = mem_bwd(p, O["qM"], kv, dmo, T)
    pieces = [row(dqA, W_A), row(dgA, W_A), row(dqB_f, W_B), row(dqB_b, W_B), row(dzf, W_B), row(dzb, W_B),
              row(diB_f, W_B), row(diB_b, W_B), row(dgB, W_B), row(dqM, W_M), row(dgM, W_M), row(dkvA, 2 * W_KV_A)]
    dp, dx, dg = norm_project_bwd("mm_in_e_bwd", _assemble_even, pieces, W["w_in_e"][i], sv["x"], sv["g"], dxo)
    dwi = matmul("mm_dwi_e", sv["h"], dp, "tn")
    return dx, dict(w_in=dwi, w_out=dwo, norm=dg[0], sink=dsink.reshape(N_Q_A), low=dlow, hg=dhg[0], kv=dkv)


def _pad_gate_up(w_up):
    z = jnp.zeros((2, 128, WK_C), F32)
    z = z.at[0, 0:GATE_RANK].set(w_up[0])
    return z.at[1, GATE_RANK:2 * GATE_RANK].set(w_up[1])


def _odd_fwd(x, i, W, kv, T):
    O = ODD_OFF
    g = W["norm_odd"][i].reshape(1, D_MODEL)
    h, p = norm_project("mm_in_o", x, g, W["w_in_o"][i])
    wup = _pad_gate_up(W["w_gate_up"][i])
    one_dir = [((p, O["qC"]), WK_C), ((p, O["kC"]), WK_C), ((p, O["vC"]), WV_C), ((p, O["rr"]), 128)]
    scan_raws = [one_dir, one_dir]
    scan_pars = [[wup[d], W["b_gate"][i][d:d + 1]] for d in range(2)]
    o_f, o_b, ss_f, ss_b = scan_fwd("scan_fwd_g", gla_prep, scan_raws, scan_pars, N_HEADS_C, DK_C, DV_C, T)
    mo = mem_fwd(p, O["qM"], kv, T)
    gg = W["gla_norm"][i].reshape(1, WV_C)
    post_ins = [("row", o_f, 0, WV_C), ("row", o_b, 0, WV_C), ("row", mo, 0, W_M),
                ("row", p, O["gC"], WV_C), ("row", p, O["gM"], W_M), ("full", gg)]
    x_new = mix_project("odd_out", odd_post_tile, T, post_ins, W["w_out_o"][i], x)
    return x_new, dict(x=x, g=g, h=h, p=p, scan_raws=scan_raws, scan_pars=scan_pars, ss=(ss_f, ss_b),
                       post_ins=post_ins)


def _assemble_odd(dq0, dq1, dk0, dk1, dv0, dv1, dgC, dqM, dgM, dr0, dr1):
    parts = [_add2(dq0, dq1), _add2(dk0, dk1), _add2(dv0, dv1), dgC, dqM, dgM, _add2(dr0, dr1)]
    return (jnp.concatenate([t.astype(BF16) for t in parts], axis=-1),)


def _odd_bwd(dxo, sv, i, W, kv, T, sync):
    O = ODD_OFF
    p = sv["p"]
    dof, dmo, dgC, dgM, dgg, dwo = mix_project_bwd("odd_out_bwd", odd_post_tile, T, sv["post_ins"], W["w_out_o"][i],
                                                   dxo, skip=(1,), narrow=(3, 4))
    dof = sync(dof)
    dqf, dkf, dvf, dr_f, dqb, dkb, dvb, dr_b, dwup_f, dbg_f, dwup_b, dbg_b = scan_bwd(
        "scan_bwd_g", gla_prep, sv["scan_raws"], sv["scan_pars"], sv["ss"], (dof, 0), N_HEADS_C, DK_C, DV_C, T)
    dqf = sync(dqf)
    row = lambda arr, w: ("row", arr, 0, w)
    dqM, dkv = mem_bwd(p, O["qM"], kv, dmo, T)
    pieces = [row(dqf, WK_C), row(dqb, WK_C), row(dkf, WK_C), row(dkb, WK_C), row(dvf, WV_C), row(dvb, WV_C),
              row(dgC, WV_C), row(dqM, W_M), row(dgM, W_M), row(dr_f, 128), row(dr_b, 128)]
    dp, dx, dg = norm_project_bwd("mm_in_o_bwd", _assemble_odd, pieces, W["w_in_o"][i], sv["x"], sv["g"], dxo)
    dwi = matmul("mm_dwi_o", sv["h"], dp, "tn")
    dw_up = jnp.stack([dwup_f[0:GATE_RANK], dwup_b[GATE_RANK:2 * GATE_RANK]])
    dbg = jnp.concatenate([dbg_f, dbg_b], axis=0)
    return dx, dict(w_in=dwi, w_out=dwo, norm=dg[0], w_up=dw_up, b_gate=dbg, gg=dgg[0], kv=dkv)


def local_step(x, mem, target, W, later=None, on_layer_grads=None, sync=lambda a: a):
    T = x.shape[0]
    slopes = jnp.repeat(2.0 ** (-8.0 * jnp.arange(1, N_Q_A + 1, dtype=F32) / N_Q_A), BLOCK).reshape(N_Q_A * BLOCK, 1)
    lower, lower_vjp = jax.vjp(_lower_bounds, W["lb_param"])
    mem_g = W["mem_norm"].reshape(1, D_MODEL)
    (mem_n,) = rows_call("mem_rms_fwd", rms_tile, N_MEM, [("row", mem, 0, D_MODEL), ("full", mem_g)], [D_MODEL], [BF16])
    kvs, saved = [], []
    for l in range(DEPTH):
        if l == 1 and later is not None:
            x, W = later(x, W)
        kvs.append(matmul("mm_kv", mem_n, W["w_kv"][l], "nn"))
        if l % 2 == 0:
            x, sv = _even_fwd(x, l // 2, W, lower, kvs[l], slopes, T)
        else:
            x, sv = _odd_fwd(x, l // 2, W, kvs[l], T)
        saved.append(sv)
    loss, dx, dgf = final_call(x, W["final_norm"].reshape(1, D_MODEL), target, T)
    per = [None] * DEPTH
    dmem_n = None
    for l in reversed(range(DEPTH)):
        if l % 2 == 0:
            dx, per[l] = _even_bwd(dx, saved[l], l // 2, W, kvs[l], slopes, T, sync)
        else:
            dx, per[l] = _odd_bwd(dx, saved[l], l // 2, W, kvs[l], T, sync)
        per[l]["w_kv"] = matmul("mm_dwkv", mem_n, per[l]["kv"], "tn")
        dmem_n = matmul("mm_dmem", per[l]["kv"], W["w_kv"][l], "nt", add=dmem_n)
        if on_layer_grads is not None:
            dx = on_layer_grads(l, dx, per[l])
    dw_kv = [per[l]["w_kv"] for l in range(DEPTH)]
    (dmem_norm,) = rows_vjp_call("mem_rms_bwd", rms_tile, N_MEM, [("row", mem, 0, D_MODEL), ("full", mem_g)],
                                 [[("row", dmem_n, 0, D_MODEL)]], skip=(0,))
    ev, od = (per[0], per[2]), (per[1], per[3])
    (d_lb,) = lower_vjp(jnp.stack([e["low"] for e in ev]))
    grads = dict(
        w_in_e=jnp.stack([e["w_in"] for e in ev]), w_in_o=jnp.stack([o["w_in"] for o in od]),
        w_out_e=jnp.stack([e["w_out"] for e in ev]), w_out_o=jnp.stack([o["w_out"] for o in od]),
        w_kv=jnp.stack(dw_kv), norm_even=jnp.stack([e["norm"] for e in ev]), sink=jnp.stack([e["sink"] for e in ev]),
        lb_param=d_lb, hgrn_norm=jnp.stack([e["hg"] for e in ev]), norm_odd=jnp.stack([o["norm"] for o in od]),
        w_gate_up=jnp.stack([o["w_up"] for o in od]), b_gate=jnp.stack([o["b_gate"] for o in od]),
        gla_norm=jnp.stack([o["gg"] for o in od]), mem_norm=dmem_norm[0], final_norm=dgf[0])
    return loss, dx, grads


SMALL_SPECS = (("lb_param", (2, 2, 128)), ("norm_odd", (2, 256)), ("w_gate_up", (2, 2, 16, 128)),
               ("b_gate", (2, 2, 128)), ("gla_norm", (2, 256)))
SMALL_ROWS = 80


def _pack_small_local(d):
    return jnp.concatenate([d[n].reshape(-1) for n, _ in SMALL_SPECS]).reshape(SMALL_ROWS, 128)


def _unpack_small_local(b):
    flat, out, o = b.reshape(-1), {}, 0
    for n, shp in SMALL_SPECS:
        sz = int(np.prod(shp))
        out[n] = flat[o:o + sz].reshape(shp)
        o += sz
    return out


def _unpack_small_full(g4):
    per = [_unpack_small_local(g4[j]) for j in range(4)]
    return {n: jnp.concatenate([per[j][n] for j in range(4)], axis=-1) for n, _ in SMALL_SPECS}


def _pack_small_blocks(full):
    blocks = []
    for j in range(4):
        blocks.append(_pack_small_local({n: full[n][..., j * shp[-1]:(j + 1) * shp[-1]] for n, shp in SMALL_SPECS}))
    return jnp.stack(blocks)


def _cols(t, order, off, widths):
    return [t[..., off[n]:off[n] + widths[n]] for n in order]


EVEN_REF_ORDER = ("qA", "kA", "vA", "gA", "qB", "zf", "zb", "iB", "gB", "qM", "gM")
ODD_REF_ORDER = ("qC", "kC", "vC", "gC", "rr", "qM", "gM")


def _layer_weights(l, g_in, g_out, g_kv):
    t = g_in.transpose(1, 0, 2).reshape(D_MODEL, -1)
    if l % 2 == 0:
        w_in = jnp.concatenate(_cols(t, EVEN_ORDER, EVEN_REF_OFF, EVEN_W), axis=-1)
    else:
        w_in = jnp.concatenate(_cols(t, ODD_ORDER, ODD_REF_OFF, ODD_W) + [jnp.zeros((D_MODEL, ODD_PAD - ODD_IN), BF16)],
                               axis=-1)
    return w_in, g_out.reshape(MIX, D_MODEL), g_kv.reshape(D_MODEL, 2 * W_M)


def _layer_grad_blocks(l, gl):
    if l % 2 == 0:
        groups = _cols(gl["w_in"], EVEN_REF_ORDER, EVEN_OFF, EVEN_W)
    else:
        groups = _cols(gl["w_in"], ODD_REF_ORDER, ODD_OFF, ODD_W)
    b_in = jnp.concatenate([t.T for t in groups], axis=0).reshape(4, -1, D_MODEL)
    return [b_in, gl["w_out"].reshape(4, MIX // 4, D_MODEL), gl["w_kv"].reshape(4, D_MODEL // 4, 2 * W_M)]


WEIGHT_NAMES = ("norm_even", "w_in_even", "sink", "lb_param", "hgrn_norm", "w_out_even", "norm_odd", "w_in_odd",
                "w_gate_up", "b_gate", "gla_norm", "w_out_odd", "mem_norm", "w_mem_kv", "final_norm")


def kernel(x, mem, norm_even, w_in_even, sink, lb_param, hgrn_norm, w_out_even, norm_odd, w_in_odd, w_gate_up, b_gate, gla_norm, w_out_odd, mem_norm, w_mem_kv, final_norm, loss_target, m_norm_even, m_w_in_even, m_sink, m_lb_param, m_hgrn_norm, m_w_out_even, m_norm_odd, m_w_in_odd, m_w_gate_up, m_b_gate, m_gla_norm, m_w_out_odd, m_mem_norm, m_w_mem_kv, m_final_norm, v_norm_even, v_w_in_even, v_sink, v_lb_param, v_hgrn_norm, v_w_out_even, v_norm_odd, v_w_in_odd, v_w_gate_up, v_b_gate, v_gla_norm, v_w_out_odd, v_mem_norm, v_w_mem_kv, v_final_norm):
    w = dict(zip(WEIGHT_NAMES, (norm_even, w_in_even, sink, lb_param, hgrn_norm, w_out_even, norm_odd, w_in_odd,
                                w_gate_up, b_gate, gla_norm, w_out_odd, mem_norm, w_mem_kv, final_norm)))
    m = dict(zip(WEIGHT_NAMES, (m_norm_even, m_w_in_even, m_sink, m_lb_param, m_hgrn_norm, m_w_out_even, m_norm_odd,
                                m_w_in_odd, m_w_gate_up, m_b_gate, m_gla_norm, m_w_out_odd, m_mem_norm, m_w_mem_kv,
                                m_final_norm)))
    v = dict(zip(WEIGHT_NAMES, (v_norm_even, v_w_in_even, v_sink, v_lb_param, v_hgrn_norm, v_w_out_even, v_norm_odd,
                                v_w_in_odd, v_w_gate_up, v_b_gate, v_gla_norm, v_w_out_odd, v_mem_norm, v_w_mem_kv,
                                v_final_norm)))
    ci = lax.axis_index("c").astype(jnp.int32).reshape(1)
    chip = (2 * lax.axis_index("x") + lax.axis_index("y")).astype(jnp.int32).reshape(1)

    shards = []
    for l in range(DEPTH):
        names = ("w_in_even", "w_out_even") if l % 2 == 0 else ("w_in_odd", "w_out_odd")
        shards.append([w[names[0]][l // 2].astype(BF16), w[names[1]][l // 2].astype(BF16), w_mem_kv[l].astype(BF16)])
    small = _pack_small_local(w)
    own = lambda g, s: lax.dynamic_update_slice(g, s[None], (chip[0], 0, 0))
    first = [own(g, s) for g, s in zip(gather_weights(shards[0], small), shards[0] + [small])]
    later_shards = shards[1] + shards[2] + shards[3]
    later_raw = gather_weights_async(later_shards)
    w0 = _layer_weights(0, *first[0:3])
    W = dict(w_in_e=[w0[0]], w_out_e=[w0[1]], w_kv=[w0[2]])
    W.update(_unpack_small_full(first[3]))
    W.update({n: w[n] for n in ("norm_even", "sink", "hgrn_norm", "mem_norm", "final_norm")})

    def later(x1, W):
        x1, raw = lax.optimization_barrier((x1, list(later_raw)))
        g = [own(a, s) for a, s in zip(raw, later_shards)]
        w1, w2, w3 = (_layer_weights(l, *g[3 * (l - 1):3 * l]) for l in (1, 2, 3))
        W = dict(W)
        W.update(w_in_e=[w0[0], w2[0]], w_in_o=[w1[0], w3[0]], w_out_e=[w0[1], w2[1]], w_out_o=[w1[1], w3[1]],
                 w_kv=[w0[2], w1[2], w2[2], w3[2]])
        return x1, W

    place = jnp.concatenate([chip, ci])

    def start(tag, blocks, wire):
        axes = [2 if b.shape[1] == ODD_IN // 4 else 1 for b in blocks]
        return dict(tag=tag, blocks=blocks, wire=wire, step=0,
                    recv=exchange_siblings(f"rs_siblings_{tag}", blocks, axes, 2))

    def advance(p, a=None):
        tie = (lambda v: (a, v)) if a is None else (lambda v: lax.optimization_barrier((a, v)))
        if p["step"] == 0:
            a, sums = tie(add_sibling(p["blocks"], p["recv"], ci, p["wire"]))
            p["recv3"] = exchange_chips(f"rs_chips_{p['tag']}", sums, 3)
        else:
            a, p["mine"] = tie(add_chips(p["blocks"], p["recv"], p["recv3"], place))
            p["other"] = exchange_siblings(f"rs_final_{p['tag']}", p["mine"], [None] * len(p["mine"]), 4)
        p["step"] += 1
        return a

    pipes, first_layer = [], {}

    def sync(a):
        for p in pipes:
            if p["step"] < 3:
                key = ("recv", "recv3", "other")[p["step"]]
                a, arrived = lax.optimization_barrier((a, list(p[key])))
                p[key] = arrived
                if p["step"] < 2:
                    a = advance(p, a)
                else:
                    p["step"] = 3
        return a

    def on_layer_grads(l, dx, gl):
        dx = sync(dx)
        if l == 0:
            first_layer.update(gl)
        else:
            pipes.append(start(f"l{l}", _layer_grad_blocks(l, gl), [BF16] * 3))
        return dx

    loss_tile, dx, grads = local_step(x[0], mem[0], loss_target[0], W, later, on_layer_grads, sync)
    last = start("l0", _layer_grad_blocks(0, first_layer) + [_pack_small_blocks(grads)], [BF16] * 3 + [F32])
    for p in pipes + [last]:
        while p["step"] < (1 if p is last else 2):
            advance(p)
    by_layer = {int(p["tag"][1:]): p for p in pipes + [last]}
    halves = lambda layers, k: (jnp.stack([by_layer[l]["mine"][k] for l in layers]),
                                jnp.stack([by_layer[l]["other"][k] for l in layers]))
    gl, upd = {}, {}

    pack = jnp.zeros((8, D_MODEL), F32)
    pack = pack.at[0:2].set(grads["norm_even"]).at[2].set(grads["hgrn_norm"].reshape(-1))
    pack = pack.at[3].set(grads["mem_norm"]).at[4].set(grads["final_norm"])
    pack = pack.at[5, 0:16].set(grads["sink"].reshape(-1)).at[5, 16].set(loss_tile[0, 0])
    tot = sum_devices(allgather_small(pack))
    gl.update(norm_even=tot[0:2], hgrn_norm=tot[2].reshape(2, W_B), mem_norm=tot[3], final_norm=tot[4],
              sink=tot[5, 0:16].reshape(2, N_Q_A))
    loss = tot[5, 16]
    for n in ("norm_even", "hgrn_norm", "mem_norm", "final_norm", "sink"):
        upd[n] = adamw_call(w[n], gl[n], m[n], v[n])
    tr_ = lambda a: jnp.swapaxes(a, 1, 2)
    gl["w_in_odd"], *upd["w_in_odd"] = [tr_(o) for o in adamw_halves(
        tr_(w["w_in_odd"]), *halves((1, 3), 0), tr_(m["w_in_odd"]), tr_(v["w_in_odd"]), ci)]
    gl["w_out_odd"], *upd["w_out_odd"] = adamw_halves(w["w_out_odd"], *halves((1, 3), 1), m["w_out_odd"],
                                                      v["w_out_odd"], ci)
    early = [upd[n] for n in sorted(upd)] + [gl["w_in_odd"], gl["w_out_odd"]]
    last["recv3"], early = lax.optimization_barrier((list(last["recv3"]), early))
    for n, res in zip(sorted(upd), early):
        upd[n] = res
    gl["w_in_odd"], gl["w_out_odd"] = early[-2:]
    advance(last)

    big = dict(w_in_even=halves((0, 2), 0), w_out_even=halves((0, 2), 1), w_mem_kv=halves((0, 1, 2, 3), 2))
    s_mine, s_other = last["mine"][3], last["other"][3]
    g_small = jnp.where(ci[0] == 0, jnp.concatenate([s_mine, s_other]), jnp.concatenate([s_other, s_mine]))
    gl.update(_unpack_small_local(g_small))
    for n in WEIGHT_NAMES:
        if n == "w_in_even":
            gl[n], *upd[n] = [tr_(o) for o in adamw_halves(tr_(w[n]), *big[n], tr_(m[n]), tr_(v[n]), ci)]
        elif n in big:
            gl[n], *upd[n] = adamw_halves(w[n], *big[n], m[n], v[n], ci)
        elif n not in upd:
            upd[n] = adamw_call(w[n], gl[n], m[n], v[n])
    return (loss, dx[None], *[gl[n] for n in WEIGHT_NAMES], *[upd[n][0] for n in WEIGHT_NAMES],
            *[upd[n][1] for n in WEIGHT_NAMES], *[upd[n][2] for n in WEIGHT_NAMES])
```

```python
import functools

import numpy as np
import jax
import jax.numpy as jnp
from jax import lax
from jax.experimental import pallas as pl
from jax.experimental.pallas import tpu as pltpu
from jax.experimental.pallas import tpu_sc as plsc

F32 = jnp.float32
BF16 = jnp.bfloat16

D_MODEL = 1024
DEPTH = 4
N_Q_A, N_KV_A, HEAD_DIM_A = 8, 2, 64
W_A, W_KV_A = 512, 128
WINDOW = 128
BLOCK = 128
N_HEADS_B, HEAD_DIM_B, W_B = 4, 128, 512
N_HEADS_C, DK_C, DV_C, WK_C, WV_C = 4, 128, 256, 512, 1024
GATE_RANK = 16
GATE_TEMP = 16.0
N_MEM, N_HEADS_M, HEAD_DIM_M, W_M = 256, 4, 128, 512
EPS = 1e-6
MASK_VALUE = -1e30
MIN_GATE = 1e-30
EVEN_IN, ODD_IN = 4864, 4128
ODD_PAD = 4224
MIX = 1536
ADAM_LR, ADAM_B1, ADAM_B2, ADAM_EPS, ADAM_WD, ADAM_STEP = 0.001, 0.9, 0.999, 1e-08, 0.01, 10

SCAN_CHUNK = 128
SCAN_SUB_FWD = 4
SCAN_SUB_BWD = 2
SCAN_LEVELS = 7
VMEM_LIMIT = 56 * 1024 * 1024

EVEN_REF_OFF = dict(qA=0, kA=512, vA=640, gA=768, qB=1280, zf=1792, zb=2304, iB=2816, gB=3328, qM=3840, gM=4352)
EVEN_W = dict(qA=512, kA=128, vA=128, gA=512, qB=512, zf=512, zb=512, iB=512, gB=512, qM=512, gM=512)
EVEN_ORDER = ("qA", "gA", "qB", "zf", "zb", "iB", "gB", "qM", "gM", "kA", "vA")
ODD_REF_OFF = dict(qC=0, kC=512, vC=1024, gC=2048, rr=3072, qM=3104, gM=3616)
ODD_W = dict(qC=512, kC=512, vC=1024, gC=1024, rr=32, qM=512, gM=512)
ODD_ORDER = ("qC", "kC", "vC", "gC", "qM", "gM", "rr")


def _offsets(order, widths):
    off, o = {}, 0
    for n in order:
        off[n] = o
        o += widths[n]
    return off


EVEN_OFF = _offsets(EVEN_ORDER, EVEN_W)
ODD_OFF = _offsets(ODD_ORDER, ODD_W)


def _dg(a, b, ca, cb):
    return lax.dot_general(a.astype(BF16), b.astype(BF16), (((ca,), (cb,)), ((), ())),
                           preferred_element_type=F32)


def dot_nn(a, b):
    return _dg(a, b, 1, 0)


def dot_nt(a, b):
    return _dg(a, b, 1, 1)


def dot_tn(a, b):
    return _dg(a, b, 0, 0)


@jax.custom_vjp
def bdot(a, b):
    return dot_nn(a, b)


bdot.defvjp(lambda a, b: (dot_nn(a, b), (a, b)),
            lambda r, g: (dot_nt(g, r[1]), dot_tn(r[0], g)))


@jax.custom_vjp
def bdot_t(a, b):
    return dot_nt(a, b)


bdot_t.defvjp(lambda a, b: (dot_nt(a, b), (a, b)),
              lambda r, g: (dot_nn(g, r[1]), dot_tn(g, r[0])))


@jax.custom_vjp
def bdot_tn(a, b):
    return dot_tn(a, b)


bdot_tn.defvjp(lambda a, b: (dot_tn(a, b), (a, b)),
               lambda r, g: (dot_nt(r[1], g), dot_nn(r[0], g)))


def _split_mm(h, x):
    hi = x.astype(BF16)
    lo = (x - hi.astype(F32)).astype(BF16)
    return (lax.dot_general(h, hi, (((1,), (0,)), ((), ())), preferred_element_type=F32)
            + lax.dot_general(h, lo, (((1,), (0,)), ((), ())), preferred_element_type=F32))


def _sigmoid(z):
    return 1.0 / (1.0 + jnp.exp(-z))


def _silu(z):
    return z * _sigmoid(z)


def _log_sigmoid(z):
    return jnp.minimum(z, 0.0) - jnp.log(1.0 + jnp.exp(-jnp.abs(z)))


def _rms(x, g):
    return x * lax.rsqrt(jnp.mean(x * x, axis=-1, keepdims=True) + EPS) * g


def rms_tile(x, g):
    return (_rms(x, g),)


@functools.partial(jax.custom_vjp, nondiff_argnums=(1, 2))
def split(x, n, axis):
    w = x.shape[axis] // n
    return tuple(lax.slice_in_dim(x, h * w, (h + 1) * w, axis=axis) for h in range(n))


split.defvjp(lambda x, n, axis: (split(x, n, axis), None),
             lambda n, axis, _, cts: (jnp.concatenate(cts, axis=axis),))


def _group_rms(o, g, heads):
    return jnp.concatenate([_rms(oh, gh) for oh, gh in zip(split(o, heads, 1), split(g, heads, 1))], axis=-1)


def even_post_tile(a, o2f, o2b, mo, gA, gB, gM, hg):
    y = _group_rms(o2f + o2b, hg, N_HEADS_B)
    return (jnp.concatenate([a * _silu(gA), y * _silu(gB), mo * _silu(gM)], axis=-1),)


def odd_post_tile(o2f, o2b, mo, gC, gM, gg):
    y = _group_rms(o2f + o2b, gg, N_HEADS_C)
    return (jnp.concatenate([y * _silu(gC), mo * _silu(gM)], axis=-1),)


def hgrn_prep(raw, par):
    qB, z, iB = raw
    (lb,) = par
    f = lb + (1.0 - lb) * _sigmoid(z)
    return _silu(qB), (1.0 - lb) * _sigmoid(-z), iB, jnp.log(jnp.maximum(f, MIN_GATE))


def gla_prep(raw, par):
    qC, kC, vC, r128 = raw
    wup, bg = par
    return qC * (DK_C ** -0.5), kC, vC, _log_sigmoid(bdot(r128, wup) + bg) / GATE_TEMP


def mem_tile(q, k, v):
    s = bdot_t(q, k) * (HEAD_DIM_M ** -0.5)
    m = lax.stop_gradient(jnp.max(s, axis=-1, keepdims=True))
    p = jnp.exp(s - m)
    p = p / jnp.sum(p, axis=-1, keepdims=True)
    return (bdot(p, v),)


ATTN_GROUP = N_Q_A // N_KV_A


def attn_block(q, ks, vs, sink, slope, c, seq):
    rows = ATTN_GROUP * BLOCK
    i = lax.broadcasted_iota(jnp.int32, (rows, 3 * BLOCK), 0) % BLOCK
    j = lax.broadcasted_iota(jnp.int32, (rows, 3 * BLOCK), 1)
    dist = jnp.abs(i - j + BLOCK).astype(F32)
    kpos = (c - 1) * BLOCK + j
    valid = (dist <= WINDOW) & (kpos >= 0) & (kpos < seq)
    s = bdot_t(q, ks) * (HEAD_DIM_A ** -0.5)
    s = jnp.where(valid, s - slope * dist, MASK_VALUE)
    m = lax.stop_gradient(jnp.maximum(jnp.max(s, axis=-1, keepdims=True), sink))
    p = jnp.where(valid, jnp.exp(s - m), 0.0)
    denom = jnp.sum(p, axis=-1, keepdims=True) + jnp.exp(sink - m)
    return bdot(p, vs) / denom


def scan_chunk(q, k, v, e, tot, st, qm, pm):
    C = SCAN_CHUNK
    e = split(e, 2 + SCAN_LEVELS, 0)
    qe = q * jnp.exp(e[0])
    kd = k * jnp.exp(e[1])
    r = lax.broadcasted_iota(jnp.int32, (C, C), 0)
    s = lax.broadcasted_iota(jnp.int32, (C, C), 1)
    a = jnp.where(r == s, jnp.sum(q * k, axis=-1, keepdims=True), 0.0)
    for l in range(SCAN_LEVELS):
        u = jnp.where(qm[l * C:(l + 1) * C] != 0.0, q, k) * jnp.exp(e[2 + l])
        a = a + bdot_t(u, u) * pm[l * C:(l + 1) * C]
    o = bdot_t(qe, st) + bdot(a, v)
    st_new = st * jnp.exp(tot) + bdot_tn(v, kd)
    return o, st_new


def _scan_consts():
    C, L = SCAN_CHUNK, SCAN_LEVELS
    t = np.arange(C)[:, None]
    r = np.arange(C)[None, :]
    blocks = [(r <= t), (r > t)]
    qms, pms = [], []
    for l in range(1, L + 1):
        m = C >> l
        upper_t = (t % (2 * m)) >= m
        upper_r = (r % (2 * m)) >= m
        same_half = (t // m) == (r // m)
        blocks.append(same_half & np.where(upper_t, r <= t, r > t))
        qms.append(np.broadcast_to(upper_t, (C, C)))
        pms.append(((t // (2 * m)) == (r // (2 * m))) & upper_t & ~upper_r)
    hf = np.concatenate(blocks, axis=0).astype(np.float32)
    flip = lambda mat: mat.reshape(-1, C, C)[:, ::-1, ::-1].reshape(-1, C)
    qmf = np.concatenate(qms, axis=0).astype(np.float32)
    pmf = np.concatenate(pms, axis=0).astype(np.float32)
    h = np.stack([hf, flip(hf)])
    ht = np.stack([h[0].T, h[1].T])
    qm = np.stack([qmf, 1.0 - qmf])
    pm = np.stack([pmf, flip(pmf)])
    return h, ht, qm, pm


def _cparams(sem):
    return pltpu.CompilerParams(dimension_semantics=sem, vmem_limit_bytes=VMEM_LIMIT)


def _row_tile(T):
    return min(T, 512)


def _in_spec(spec, tr):
    kind = spec[0]
    if kind == "row":
        _, arr, off, w = spec
        assert off % w == 0
        return arr, pl.BlockSpec((tr, w), functools.partial(lambda i, b: (i, b), b=off // w))
    if kind == "row3":
        _, arr, d, off, w = spec
        assert off % w == 0
        return arr, pl.BlockSpec((None, tr, w), functools.partial(lambda i, d, b: (d, i, b), d=d, b=off // w))
    _, arr = spec
    return arr, pl.BlockSpec(arr.shape, functools.partial(lambda i, n: (0,) * n, n=arr.ndim))


def rows_call(name, tile_fn, T, ins, out_widths, out_dtypes=None, stacks=None):
    tr = _row_tile(T)
    n_in = len(ins)
    out_dtypes = out_dtypes or [F32] * len(out_widths)
    stacks = stacks or [(k,) for k in range(len(out_widths))]

    def body(*refs):
        vals = [r[...] for r in refs[:n_in]]
        outs = tile_fn(*vals)
        for r, members in zip(refs[n_in:], stacks):
            if len(members) == 1:
                r[...] = outs[members[0]].astype(r.dtype)
            else:
                for d, k in enumerate(members):
                    r[d] = outs[k].astype(r.dtype)

    in_specs, args = [], []
    for spec in ins:
        arr, bs = _in_spec(spec, tr)
        args.append(arr)
        in_specs.append(bs)
    out_specs, out_shape = [], []
    for w, dt, members in zip(out_widths, out_dtypes, stacks):
        n = len(members)
        if n == 1:
            out_specs.append(pl.BlockSpec((tr, w), lambda i: (i, 0)))
            out_shape.append(jax.ShapeDtypeStruct((T, w), dt))
        else:
            out_specs.append(pl.BlockSpec((n, tr, w), lambda i: (0, i, 0)))
            out_shape.append(jax.ShapeDtypeStruct((n, T, w), dt))
    return pl.pallas_call(body, out_shape=out_shape, grid=(T // tr,), in_specs=in_specs, out_specs=out_specs,
                          name=name, compiler_params=_cparams(("arbitrary",)))(*args)


def rows_vjp_call(name, tile_fn, T, ins, cts, skip=(), narrow=()):
    tr = _row_tile(T)
    n_in = len(ins)
    n_ct = [len(c) for c in cts]
    want = [k for k in range(n_in) if k not in skip]

    def body(*refs):
        i = pl.program_id(0)
        vals = [r[...] for r in refs[:n_in]]
        ct, pos = [], n_in
        for n in n_ct:
            acc = refs[pos][...]
            for r in refs[pos + 1:pos + n]:
                acc = acc + r[...]
            ct.append(acc)
            pos += n
        _, vjp = jax.vjp(tile_fn, *vals)
        grads = vjp(tuple(ct))
        for r, k in zip(refs[pos:], want):
            if ins[k][0] == "full":
                @pl.when(i == 0)
                def _():
                    r[...] = jnp.zeros_like(r)
                r[...] += grads[k]
            else:
                r[...] = grads[k].astype(r.dtype)

    in_specs, args = [], []
    for spec in list(ins) + [s for c in cts for s in c]:
        arr, bs = _in_spec(spec, tr)
        args.append(arr)
        in_specs.append(bs)
    out_specs, out_shape = [], []
    for k in want:
        if ins[k][0] == "full":
            arr = ins[k][1]
            out_specs.append(pl.BlockSpec(arr.shape, functools.partial(lambda i, n: (0,) * n, n=arr.ndim)))
            out_shape.append(jax.ShapeDtypeStruct(arr.shape, F32))
        else:
            w = ins[k][-1]
            out_specs.append(pl.BlockSpec((tr, w), lambda i: (i, 0)))
            out_shape.append(jax.ShapeDtypeStruct((T, w), BF16 if k in narrow else F32))
    return pl.pallas_call(body, out_shape=out_shape, grid=(T // tr,), in_specs=in_specs, out_specs=out_specs,
                          name=name, compiler_params=_cparams(("arbitrary",)))(*args)


def matmul(name, a, b, mode, add=None, out_dtype=F32):
    if mode == "tn":
        K, M = a.shape
        N = b.shape[1]
        tm = M if M <= 1536 else 512
        tn = N if N <= 1280 else (N // 2 if (N // 2) % 128 == 0 else N)
        tk = min(K, 512)
        grid = (M // tm, N // tn, K // tk)

        def body(a_ref, b_ref, o_ref):
            @pl.when(pl.program_id(2) == 0)
            def _():
                o_ref[...] = jnp.zeros_like(o_ref)
            o_ref[...] += dot_tn(a_ref[...], b_ref[...])

        return pl.pallas_call(
            body, out_shape=jax.ShapeDtypeStruct((M, N), F32), grid=grid,
            in_specs=[pl.BlockSpec((tk, tm), lambda i, j, k: (k, i)), pl.BlockSpec((tk, tn), lambda i, j, k: (k, j))],
            out_specs=pl.BlockSpec((tm, tn), lambda i, j, k: (i, j)), name=name,
            compiler_params=_cparams(("arbitrary", "arbitrary", "arbitrary")))(a, b)

    M, K = a.shape
    N = b.shape[1] if mode == "nn" else b.shape[0]
    tm = min(M, 512)
    tn = N if N <= 1536 else (N // 2 if (N // 2) % 128 == 0 else (N // 3 if (N // 3) % 128 == 0 else N))
    grid = (N // tn, M // tm)
    n_in = 2 + (add is not None)

    def body(*refs):
        a_ref, b_ref = refs[0], refs[1]
        o_ref = refs[n_in]
        acc = dot_nn(a_ref[...], b_ref[...]) if mode == "nn" else dot_nt(a_ref[...], b_ref[...])
        if add is not None:
            acc = acc + refs[2][...]
        o_ref[...] = acc.astype(o_ref.dtype)

    in_specs = [pl.BlockSpec((tm, K), lambda j, i: (i, 0)),
                pl.BlockSpec((K, tn), lambda j, i: (0, j)) if mode == "nn" else pl.BlockSpec((tn, K), lambda j, i: (j, 0))]
    args = [a, b]
    if add is not None:
        in_specs.append(pl.BlockSpec((tm, tn), lambda j, i: (i, j)))
        args.append(add)
    return pl.pallas_call(
        body, out_shape=jax.ShapeDtypeStruct((M, N), out_dtype), grid=grid, in_specs=in_specs,
        out_specs=pl.BlockSpec((tm, tn), lambda j, i: (i, j)), name=name,
        compiler_params=_cparams(("arbitrary", "arbitrary")))(*args)


def norm_project(name, x, g, w):
    T, D = x.shape
    N = w.shape[1]
    tm = min(T, 512)

    def body(x_ref, g_ref, w_ref, h_ref, p_ref):
        h = _rms(x_ref[...], g_ref[...]).astype(BF16)
        h_ref[...] = h
        p_ref[...] = dot_nn(h, w_ref[...])

    return pl.pallas_call(
        body, out_shape=[jax.ShapeDtypeStruct((T, D), BF16), jax.ShapeDtypeStruct((T, N), F32)], grid=(T // tm,),
        in_specs=[pl.BlockSpec((tm, D), lambda i: (i, 0)), pl.BlockSpec((1, D), lambda i: (0, 0)),
                  pl.BlockSpec((D, N), lambda i: (0, 0))],
        out_specs=[pl.BlockSpec((tm, D), lambda i: (i, 0)), pl.BlockSpec((tm, N), lambda i: (i, 0))],
        name=name, compiler_params=_cparams(("arbitrary",)))(x, g, w)


def norm_project_bwd(name, assemble, pieces, w, x, g, dy):
    T, D = x.shape
    N = w.shape[1]
    tm = min(T, 256)
    n_in = len(pieces)

    def body(*refs):
        w_ref, x_ref, g_ref, dy_ref, dp_ref, dx_ref, dg_ref = refs[n_in:]

        @pl.when(pl.program_id(0) == 0)
        def _():
            dg_ref[...] = jnp.zeros_like(dg_ref)

        (dp,) = assemble(*[r[...] for r in refs[:n_in]])
        dp_ref[...] = dp
        _, vjp = jax.vjp(_rms, x_ref[...], g_ref[...])
        dx, dg = vjp(dot_nt(dp, w_ref[...]))
        dx_ref[...] = dx + dy_ref[...]
        dg_ref[...] += dg

    in_specs, args = [], []
    for spec in pieces:
        arr, bs = _in_spec(spec, tm)
        args.append(arr)
        in_specs.append(bs)
    row = pl.BlockSpec((tm, D), lambda i: (i, 0))
    vec = pl.BlockSpec((1, D), lambda i: (0, 0))
    wide = pl.BlockSpec((tm, N), lambda i: (i, 0))
    return pl.pallas_call(
        body,
        out_shape=[jax.ShapeDtypeStruct((T, N), BF16), jax.ShapeDtypeStruct((T, D), F32), jax.ShapeDtypeStruct((1, D), F32)],
        grid=(T // tm,), in_specs=in_specs + [pl.BlockSpec((D, N), lambda i: (0, 0)), row, vec, row],
        out_specs=[wide, row, vec], name=name, compiler_params=_cparams(("arbitrary",)))(*args, w, x, g, dy)


def mix_project(name, tile_fn, T, ins, w, x):
    tr = _row_tile(T)
    n_in = len(ins)
    K, D = w.shape

    def body(*refs):
        w_ref, x_ref, y_ref = refs[n_in:]
        (mix,) = tile_fn(*[r[...] for r in refs[:n_in]])
        y_ref[...] = x_ref[...] + dot_nn(mix, w_ref[...])

    in_specs, args = [], []
    for spec in ins:
        arr, bs = _in_spec(spec, tr)
        args.append(arr)
        in_specs.append(bs)
    row = pl.BlockSpec((tr, D), lambda i: (i, 0))
    return pl.pallas_call(
        body, out_shape=jax.ShapeDtypeStruct((T, D), F32), grid=(T // tr,),
        in_specs=in_specs + [pl.BlockSpec((K, D), lambda i: (0, 0)), row], out_specs=row,
        name=name, compiler_params=_cparams(("arbitrary",)))(*args, w, x)


def mix_project_bwd(name, tile_fn, T, ins, w, dy, skip=(), narrow=()):
    tr = _row_tile(T)
    n_in = len(ins)
    K, D = w.shape
    want = [k for k in range(n_in) if k not in skip]

    def body(*refs):
        w_ref, dy_ref = refs[n_in:n_in + 2]
        outs, dw_ref = refs[n_in + 2:-1], refs[-1]
        first = pl.program_id(0) == 0
        (mix,), vjp = jax.vjp(tile_fn, *[r[...] for r in refs[:n_in]])
        d = dy_ref[...].astype(BF16)
        grads = vjp((dot_nt(d, w_ref[...]),))

        @pl.when(first)
        def _():
            dw_ref[...] = jnp.zeros_like(dw_ref)

        dw_ref[...] += dot_tn(mix, d)
        for r, k in zip(outs, want):
            if ins[k][0] == "full":
                @pl.when(first)
                def _():
                    r[...] = jnp.zeros_like(r)
                r[...] += grads[k]
            else:
                r[...] = grads[k].astype(r.dtype)

    in_specs, args = [], []
    for spec in ins:
        arr, bs = _in_spec(spec, tr)
        args.append(arr)
        in_specs.append(bs)
    out_specs, out_shape = [], []
    for k in want:
        if ins[k][0] == "full":
            arr = ins[k][1]
            out_specs.append(_full_spec(arr))
            out_shape.append(jax.ShapeDtypeStruct(arr.shape, F32))
        else:
            wd = ins[k][-1]
            out_specs.append(pl.BlockSpec((tr, wd), lambda i: (i, 0)))
            out_shape.append(jax.ShapeDtypeStruct((T, wd), BF16 if k in narrow else F32))
    wspec = pl.BlockSpec((K, D), lambda i: (0, 0))
    return pl.pallas_call(
        body, out_shape=out_shape + [jax.ShapeDtypeStruct((K, D), F32)], grid=(T // tr,),
        in_specs=in_specs + [wspec, pl.BlockSpec((tr, D), lambda i: (i, 0))], out_specs=out_specs + [wspec],
        name=name, compiler_params=_cparams(("arbitrary",)))(*args, w, dy)


def _attn_heads(n):
    G = N_Q_A // N_KV_A
    k_sl = pl.ds(n * HEAD_DIM_A, HEAD_DIM_A)
    v_sl = pl.ds(W_KV_A + n * HEAD_DIM_A, HEAD_DIM_A)
    q_sl = [pl.ds((n * G + g) * HEAD_DIM_A, HEAD_DIM_A) for g in range(G)]
    return k_sl, v_sl, q_sl, range(n * G, (n + 1) * G)


def attn_fwd(p, q_off, kvp, sink, slopes, T):
    nb = T // BLOCK
    assert q_off % W_A == 0

    def body(q_ref, kv_ref, sink_ref, slope_ref, o_ref):
        c = pl.program_id(0)
        rows = pl.ds(pl.multiple_of(c * BLOCK, BLOCK), 3 * BLOCK)
        for n in range(N_KV_A):
            k_sl, v_sl, q_sl, heads = _attn_heads(n)
            group = pl.ds(n * ATTN_GROUP * BLOCK, ATTN_GROUP * BLOCK)
            q = jnp.concatenate([q_ref[:, s] for s in q_sl], axis=0)
            o = attn_block(q, kv_ref[rows, k_sl], kv_ref[rows, v_sl], sink_ref[group, :], slope_ref[group, :], c, T)
            for g, s in enumerate(q_sl):
                o_ref[:, s] = o[g * BLOCK:(g + 1) * BLOCK]

    full = lambda a: pl.BlockSpec(a.shape, functools.partial(lambda c, nd: (0,) * nd, nd=a.ndim))
    return pl.pallas_call(
        body, out_shape=jax.ShapeDtypeStruct((T, W_A), F32), grid=(nb,),
        in_specs=[pl.BlockSpec((BLOCK, W_A), lambda c: (c, q_off // W_A)), full(kvp), full(sink), full(slopes)],
        out_specs=pl.BlockSpec((BLOCK, W_A), lambda c: (c, 0)),
        name="attn_fwd", compiler_params=_cparams(("arbitrary",)))(p, kvp, sink, slopes)


def attn_bwd(p, q_off, kvp, sink, slopes, do, T):
    nb = T // BLOCK

    def body(q_ref, kv_ref, sink_ref, slope_ref, do_ref, dq_ref, dkv_ref, dsink_ref):
        c = pl.program_id(0)

        @pl.when(c == 0)
        def _():
            dkv_ref[...] = jnp.zeros_like(dkv_ref)
            dsink_ref[...] = jnp.zeros_like(dsink_ref)

        rows = pl.ds(pl.multiple_of(c * BLOCK, BLOCK), 3 * BLOCK)
        for n in range(N_KV_A):
            k_sl, v_sl, q_sl, heads = _attn_heads(n)
            group = pl.ds(n * ATTN_GROUP * BLOCK, ATTN_GROUP * BLOCK)
            slope = slope_ref[group, :]
            q = jnp.concatenate([q_ref[:, s] for s in q_sl], axis=0)
            do = jnp.concatenate([do_ref[:, s] for s in q_sl], axis=0)
            _, vjp = jax.vjp(lambda q_, kk, vv, sk: attn_block(q_, kk, vv, sk, slope, c, T),
                             q, kv_ref[rows, k_sl], kv_ref[rows, v_sl], sink_ref[group, :])
            dq, dks, dvs, dsk = vjp(do)
            dkv_ref[rows, k_sl] += dks
            dkv_ref[rows, v_sl] += dvs
            for g, (s, h) in enumerate(zip(q_sl, heads)):
                seg = slice(g * BLOCK, (g + 1) * BLOCK)
                dq_ref[:, s] = dq[seg].astype(dq_ref.dtype)
                dsink_ref[h] += jnp.sum(dsk[seg], axis=0, keepdims=True)

    full = lambda a: pl.BlockSpec(a.shape, functools.partial(lambda c, nd: (0,) * nd, nd=a.ndim))
    qspec = pl.BlockSpec((BLOCK, W_A), lambda c: (c, 0))
    return pl.pallas_call(
        body,
        out_shape=[jax.ShapeDtypeStruct((T, W_A), BF16), jax.ShapeDtypeStruct(kvp.shape, F32),
                   jax.ShapeDtypeStruct((N_Q_A, 1, 1), F32)],
        grid=(nb,),
        in_specs=[pl.BlockSpec((BLOCK, W_A), lambda c: (c, q_off // W_A)), full(kvp), full(sink), full(slopes), qspec],
        out_specs=[qspec, full(kvp), pl.BlockSpec((N_Q_A, 1, 1), lambda c: (0, 0, 0))],
        name="attn_bwd", compiler_params=_cparams(("arbitrary",)))(p, kvp, sink, slopes, do)


def mem_fwd(p, q_off, kv, T):
    tr = min(T, 2 * _row_tile(T))
    assert q_off % W_M == 0

    def body(q_ref, kv_ref, o_ref):
        for h in range(N_HEADS_M):
            hs = pl.ds(h * HEAD_DIM_M, HEAD_DIM_M)
            (o,) = mem_tile(q_ref[:, hs], kv_ref[:, hs], kv_ref[:, pl.ds(W_M + h * HEAD_DIM_M, HEAD_DIM_M)])
            o_ref[:, hs] = o

    return pl.pallas_call(
        body, out_shape=jax.ShapeDtypeStruct((T, W_M), F32), grid=(T // tr,),
        in_specs=[pl.BlockSpec((tr, W_M), lambda i: (i, q_off // W_M)), pl.BlockSpec((N_MEM, 2 * W_M), lambda i: (0, 0))],
        out_specs=pl.BlockSpec((tr, W_M), lambda i: (i, 0)),
        name="mem_fwd", compiler_params=_cparams(("arbitrary",)))(p, kv)


def mem_bwd(p, q_off, kv, do, T):
    tr = min(T, 2 * _row_tile(T))

    def body(q_ref, kv_ref, do_ref, dq_ref, dkv_ref):
        @pl.when(pl.program_id(0) == 0)
        def _():
            dkv_ref[...] = jnp.zeros_like(dkv_ref)

        for h in range(N_HEADS_M):
            hs = pl.ds(h * HEAD_DIM_M, HEAD_DIM_M)
            vs = pl.ds(W_M + h * HEAD_DIM_M, HEAD_DIM_M)
            _, vjp = jax.vjp(mem_tile, q_ref[:, hs], kv_ref[:, hs], kv_ref[:, vs])
            dq, dk, dv = vjp((do_ref[:, hs],))
            dq_ref[:, hs] = dq.astype(dq_ref.dtype)
            dkv_ref[:, hs] += dk
            dkv_ref[:, vs] += dv

    kvspec = pl.BlockSpec((N_MEM, 2 * W_M), lambda i: (0, 0))
    return pl.pallas_call(
        body,
        out_shape=[jax.ShapeDtypeStruct((T, W_M), BF16), jax.ShapeDtypeStruct((N_MEM, 2 * W_M), F32)],
        grid=(T // tr,),
        in_specs=[pl.BlockSpec((tr, W_M), lambda i: (i, q_off // W_M)), kvspec, pl.BlockSpec((tr, W_M), lambda i: (i, 0))],
        out_specs=[pl.BlockSpec((tr, W_M), lambda i: (i, 0)), kvspec],
        name="mem_bwd", compiler_params=_cparams(("arbitrary",)))(p, kv, do)


def _scan_const_specs(dk):
    C, L = SCAN_CHUNK, SCAN_LEVELS
    return [pl.BlockSpec((2, (2 + L) * C, C), lambda n: (0, 0, 0)),
            pl.BlockSpec((2, C, (2 + L) * C), lambda n: (0, 0, 0)),
            pl.BlockSpec((2, L * C, dk), lambda n: (0, 0, 0)),
            pl.BlockSpec((2, L * C, C), lambda n: (0, 0, 0))]


def _chunk_spec(src, width, chunk_of, sub):
    arr, sel = src
    if arr.ndim == 2:
        assert sel % width == 0
        return pl.BlockSpec((SCAN_CHUNK * sub, width), functools.partial(lambda n, b: (chunk_of(n), b), b=sel // width))
    return pl.BlockSpec((None, SCAN_CHUNK * sub, width), functools.partial(lambda n, d: (d, chunk_of(n), 0), d=sel))


def _scan_const_args():
    h, ht, qm, pm = _scan_consts()
    return [jnp.asarray(h, BF16), jnp.asarray(ht, BF16), jnp.asarray(qm, F32), jnp.asarray(pm, F32)]


def _full_spec(a):
    return pl.BlockSpec(a.shape, functools.partial(lambda n, nd: (0,) * nd, nd=a.ndim))


def scan_fwd(name, prep, raws, params, heads, dk, dv, T):
    C, S = SCAN_CHUNK, min(SCAN_SUB_FWD, T // SCAN_CHUNK)
    N = T // (C * S)
    assert dk == C
    Wv = heads * dv
    orders = (lambda n: n, lambda n: N - 1 - n)
    n_raw, n_par = [len(r) for r in raws], [len(p) for p in params]

    def body(*refs):
        pos, raw_refs, par_refs = 0, [], []
        for d in range(2):
            raw_refs.append(refs[pos:pos + n_raw[d]])
            pos += n_raw[d]
        for d in range(2):
            par_refs.append(refs[pos:pos + n_par[d]])
            pos += n_par[d]
        h_ref, ht_ref, qm_ref, pm_ref = refs[pos:pos + 4]
        o_refs, ss_refs, st_ref = refs[pos + 4:pos + 6], refs[pos + 6:pos + 8], refs[pos + 8]

        @pl.when(pl.program_id(0) == 0)
        def _():
            st_ref[...] = jnp.zeros_like(st_ref)

        for d in range(2):
            consts = (qm_ref[d], pm_ref[d])
            pars = [p[...] for p in par_refs[d]]
            for sub in (range(S) if d == 0 else reversed(range(S))):
                rows = pl.ds(sub * C, C)
                q, k, v, g = prep([r[rows, :] for r in raw_refs[d]], pars)
                e = _split_mm(h_ref[d], g)
                tot = jnp.sum(g, axis=0, keepdims=True)
                for h in range(heads):
                    ks, vs = slice(h * dk, (h + 1) * dk), slice(h * dv, (h + 1) * dv)
                    st = st_ref[d, h]
                    ss_refs[d][h, sub] = st
                    o, st_new = scan_chunk(q[:, ks], k[:, ks], v[:, vs], e[:, ks], tot[:, ks], st, *consts)
                    o_refs[d][rows, vs] = o
                    st_ref[d, h] = st_new

    ss_spec = lambda order: pl.BlockSpec((heads, S, dv, dk), lambda n: (0, order(n), 0, 0))
    return pl.pallas_call(
        body,
        out_shape=[jax.ShapeDtypeStruct((T, Wv), F32)] * 2 + [jax.ShapeDtypeStruct((heads, T // C, dv, dk), F32)] * 2,
        grid=(N,),
        in_specs=[_chunk_spec(s, w, orders[d], S) for d in range(2) for s, w in raws[d]]
        + [_full_spec(p) for d in range(2) for p in params[d]] + _scan_const_specs(dk),
        out_specs=[pl.BlockSpec((C * S, Wv), lambda n: (orders[0](n), 0)),
                   pl.BlockSpec((C * S, Wv), lambda n: (orders[1](n), 0)), ss_spec(orders[0]), ss_spec(orders[1])],
        scratch_shapes=[pltpu.VMEM((2, heads, dv, dk), F32)],
        name=name, compiler_params=_cparams(("arbitrary",)))(
            *[s[0] for d in range(2) for s, _ in raws[d]], *[p for d in range(2) for p in params[d]], *_scan_const_args())


def scan_bwd(name, prep, raws, params, ss, do, heads, dk, dv, T):
    C, S = SCAN_CHUNK, min(SCAN_SUB_BWD, T // SCAN_CHUNK)
    N = T // (C * S)
    Wv = heads * dv
    orders = (lambda n: N - 1 - n, lambda n: n)
    n_raw, n_par = [len(r) for r in raws], [len(p) for p in params]

    def body(*refs):
        pos, raw_refs, par_refs, draw_refs, dpar_refs = 0, [], [], [], []
        for group, counts in ((raw_refs, n_raw), (par_refs, n_par)):
            for d in range(2):
                group.append(refs[pos:pos + counts[d]])
                pos += counts[d]
        ss_refs, do_refs = refs[pos:pos + 2], refs[pos + 2:pos + 4]
        h_ref, ht_ref, qm_ref, pm_ref = refs[pos + 4:pos + 8]
        pos += 8
        for group, counts in ((draw_refs, n_raw), (dpar_refs, n_par)):
            for d in range(2):
                group.append(refs[pos:pos + counts[d]])
                pos += counts[d]
        dst_ref = refs[pos]

        @pl.when(pl.program_id(0) == 0)
        def _():
            dst_ref[...] = jnp.zeros_like(dst_ref)
            for d in range(2):
                for r in dpar_refs[d]:
                    r[...] = jnp.zeros_like(r)

        for d in range(2):
            consts = (qm_ref[d], pm_ref[d])
            pars = [p[...] for p in par_refs[d]]
            for sub in (reversed(range(S)) if d == 0 else range(S)):
                rows = pl.ds(sub * C, C)
                (q, k, v, g), prep_vjp = jax.vjp(prep, [r[rows, :] for r in raw_refs[d]], pars)
                e = _split_mm(h_ref[d], g)
                tot = jnp.sum(g, axis=0, keepdims=True)
                dqs, dks, dvs, des, dtots = [], [], [], [], []
                for h in range(heads):
                    ks, vs = slice(h * dk, (h + 1) * dk), slice(h * dv, (h + 1) * dv)
                    _, vjp = jax.vjp(lambda q_, k_, v_, e_, t_, st_: scan_chunk(q_, k_, v_, e_, t_, st_, *consts),
                                     q[:, ks], k[:, ks], v[:, vs], e[:, ks], tot[:, ks], ss_refs[d][h, sub])
                    dq, dk_, dv_, de, dtot, dst = vjp((do_refs[d][rows, vs], dst_ref[d, h]))
                    dst_ref[d, h] = dst
                    for group, val in ((dqs, dq), (dks, dk_), (dvs, dv_), (des, de), (dtots, dtot)):
                        group.append(val)
                cat = lambda parts: jnp.concatenate(parts, axis=-1)
                dg = _split_mm(ht_ref[d], cat(des)) + cat(dtots)
                draws, dpars = prep_vjp((cat(dqs), cat(dks), cat(dvs), dg))
                for r, val in zip(draw_refs[d], draws):
                    r[rows, :] = val.astype(r.dtype)
                for r, val in zip(dpar_refs[d], dpars):
                    r[...] += val

    ss_spec = lambda order: pl.BlockSpec((heads, S, dv, dk), lambda n: (0, order(n), 0, 0))
    row_out = lambda w, order: pl.BlockSpec((C * S, w), lambda n: (order(n), 0))
    return pl.pallas_call(
        body,
        out_shape=[jax.ShapeDtypeStruct((T, w), BF16) for d in range(2) for _, w in raws[d]]
        + [jax.ShapeDtypeStruct(p.shape, F32) for d in range(2) for p in params[d]],
        grid=(N,),
        in_specs=[_chunk_spec(s, w, orders[d], S) for d in range(2) for s, w in raws[d]]
        + [_full_spec(p) for d in range(2) for p in params[d]]
        + [ss_spec(orders[0]), ss_spec(orders[1]), _chunk_spec(do, Wv, orders[0], S), _chunk_spec(do, Wv, orders[1], S)]
        + _scan_const_specs(dk),
        out_specs=[row_out(w, orders[d]) for d in range(2) for _, w in raws[d]]
        + [_full_spec(p) for d in range(2) for p in params[d]],
        scratch_shapes=[pltpu.VMEM((2, heads, dv, dk), F32)],
        name=name, compiler_params=_cparams(("arbitrary",)))(
            *[s[0] for d in range(2) for s, _ in raws[d]], *[p for d in range(2) for p in params[d]],
            ss[0], ss[1], do[0], do[0], *_scan_const_args())


def final_call(x, g, target, T):
    tr = _row_tile(T)

    def tile(xv, gv, tv):
        y = _rms(xv, gv)
        err = (y - tv) ** 2
        return jnp.sum(jnp.sum(err, axis=-1, keepdims=True), axis=0, keepdims=True) * (0.5 / D_MODEL)

    def body(x_ref, g_ref, t_ref, loss_ref, dx_ref, dg_ref):
        i = pl.program_id(0)
        tv = t_ref[...]
        lv, vjp = jax.vjp(lambda a, b: tile(a, b, tv), x_ref[...], g_ref[...])
        dx, dg = vjp(jnp.ones((1, 1), F32))
        dx_ref[...] = dx

        @pl.when(i == 0)
        def _():
            loss_ref[...] = jnp.zeros_like(loss_ref)
            dg_ref[...] = jnp.zeros_like(dg_ref)

        loss_ref[...] += jnp.broadcast_to(lv, loss_ref.shape)
        dg_ref[...] += dg

    return pl.pallas_call(
        body,
        out_shape=[jax.ShapeDtypeStruct((8, 128), F32), jax.ShapeDtypeStruct((T, D_MODEL), F32),
                   jax.ShapeDtypeStruct((1, D_MODEL), F32)],
        grid=(T // tr,),
        in_specs=[pl.BlockSpec((tr, D_MODEL), lambda i: (i, 0)), pl.BlockSpec((1, D_MODEL), lambda i: (0, 0)),
                  pl.BlockSpec((tr, D_MODEL), lambda i: (i, 0))],
        out_specs=[pl.BlockSpec((8, 128), lambda i: (0, 0)), pl.BlockSpec((tr, D_MODEL), lambda i: (i, 0)),
                   pl.BlockSpec((1, D_MODEL), lambda i: (0, 0))],
        name="final_loss", compiler_params=_cparams(("arbitrary",)))(x, g, target)


def adamw_call(w, g, m, v):
    shape = w.shape
    c = shape[-1]
    r = int(np.prod(shape[:-1])) if len(shape) > 1 else 1
    tr = r if r <= 256 else 256
    assert r % tr == 0

    def body(w_ref, g_ref, m_ref, v_ref, d_ref, nm_ref, nv_ref):
        gv = g_ref[...]
        nm = ADAM_B1 * m_ref[...] + (1.0 - ADAM_B1) * gv
        nv = ADAM_B2 * v_ref[...] + (1.0 - ADAM_B2) * jnp.square(gv)
        m_hat = nm / (1.0 - ADAM_B1 ** ADAM_STEP)
        v_hat = nv / (1.0 - ADAM_B2 ** ADAM_STEP)
        d_ref[...] = -ADAM_LR * (m_hat / (jnp.sqrt(v_hat) + ADAM_EPS) + ADAM_WD * w_ref[...])
        nm_ref[...] = nm
        nv_ref[...] = nv

    spec = pl.BlockSpec((tr, c), lambda i: (i, 0))
    outs = pl.pallas_call(body, out_shape=[jax.ShapeDtypeStruct((r, c), F32)] * 3, grid=(r // tr,),
                          in_specs=[spec] * 4, out_specs=[spec] * 3, name="adamw",
                          compiler_params=_cparams(("arbitrary",)))(*(t.reshape(r, c) for t in (w, g, m, v)))
    return tuple(o.reshape(shape) for o in outs)


def adamw_halves(w, mine, other, m, v, c):
    L, R, C = w.shape
    by_cols = mine.shape[-1] != C
    if by_cols:
        tile, nbh = (R, C // 2), 1
        full_idx = lambda l, i: (l, 0, i)
    else:
        rh = R // 2
        tr = rh if rh <= 256 else rh // 2
        assert tr % 8 == 0
        tile, nbh = (tr, C), rh // tr
        full_idx = lambda l, i: (l, i, 0)

    def body(c_ref, w_ref, a_ref, b_ref, m_ref, v_ref, g_ref, d_ref, nm_ref, nv_ref):
        is_mine = (pl.program_id(1) // nbh) == c_ref[0]
        gv = jnp.where(is_mine, a_ref[...], b_ref[...])
        nm = ADAM_B1 * m_ref[...] + (1.0 - ADAM_B1) * gv
        nv = ADAM_B2 * v_ref[...] + (1.0 - ADAM_B2) * jnp.square(gv)
        m_hat = nm / (1.0 - ADAM_B1 ** ADAM_STEP)
        v_hat = nv / (1.0 - ADAM_B2 ** ADAM_STEP)
        g_ref[...] = gv
        d_ref[...] = -ADAM_LR * (m_hat / (jnp.sqrt(v_hat) + ADAM_EPS) + ADAM_WD * w_ref[...])
        nm_ref[...] = nm
        nv_ref[...] = nv

    full = pl.BlockSpec((None,) + tile, lambda l, i, c_ref: full_idx(l, i))
    half = pl.BlockSpec((None,) + tile, lambda l, i, c_ref: (l, i % nbh, 0))
    grid_spec = pltpu.PrefetchScalarGridSpec(num_scalar_prefetch=1, grid=(L, 2 * nbh),
                                             in_specs=[full, half, half, full, full], out_specs=[full] * 4)
    return pl.pallas_call(body, out_shape=[jax.ShapeDtypeStruct(w.shape, F32)] * 4, grid_spec=grid_spec,
                          name="adamw_halves", compiler_params=_cparams(("arbitrary", "arbitrary")))(c, w, mine, other, m, v)


def sum_devices(g64):
    def body(x_ref, o_ref):
        acc = x_ref[0:8, :]
        for d in range(1, 8):
            acc = acc + x_ref[8 * d:8 * d + 8, :]
        o_ref[...] = acc

    return pl.pallas_call(body, out_shape=jax.ShapeDtypeStruct((8, D_MODEL), F32), name="sum_devices")(g64)


def _half_tile(rh):
    if rh <= 512:
        return rh
    return next(rh // d for d in range(2, rh) if rh % d == 0 and (rh // d) % 16 == 0 and rh // d <= 512)


def _half_geometry(full_shape, half_shape):
    R, C = full_shape[-2:]
    if half_shape[-1] != C:
        return (R, C // 2), 1, lambda i, c: (0, c)
    tr = _half_tile(R // 2)
    nblk = (R // 2) // tr
    return (tr, C), nblk, lambda i, c: (i + c * nblk, 0)


def _work_items(counts):
    starts = [int(v) for v in np.cumsum([0] + list(counts[:-1]))]
    local = lambda a, s: jnp.clip(s - starts[a], 0, counts[a] - 1)
    return starts, int(sum(counts)), local


def add_sibling(gs, recvs, c, out_dtypes):
    n = len(gs)
    geo = [_half_geometry(g.shape, r.shape) for g, r in zip(gs, recvs)]
    counts = [4 * nblk for _, nblk, _ in geo]
    starts, total, local = _work_items(counts)

    def body(c_ref, *refs):
        s = pl.program_id(0)
        for a in range(n):
            g_ref, r_ref, o_ref = refs[a], refs[n + a], refs[2 * n + a]

            @pl.when((s >= starts[a]) & (s < starts[a] + counts[a]))
            def _():
                o_ref[...] = (g_ref[...] + r_ref[...]).astype(o_ref.dtype)

    def own_idx(s, c_ref, a):
        _, nblk, own = geo[a]
        k = local(a, s)
        return (k // nblk,) + own(k % nblk, c_ref[0])

    def half_idx(s, c_ref, a):
        k = local(a, s)
        return (k // geo[a][1], k % geo[a][1], 0)

    halves = [pl.BlockSpec((None,) + geo[a][0], functools.partial(half_idx, a=a)) for a in range(n)]
    grid_spec = pltpu.PrefetchScalarGridSpec(
        num_scalar_prefetch=1, grid=(total,),
        in_specs=[pl.BlockSpec((None,) + geo[a][0], functools.partial(own_idx, a=a)) for a in range(n)] + halves,
        out_specs=halves)
    return pl.pallas_call(body, out_shape=[jax.ShapeDtypeStruct(r.shape, dt) for r, dt in zip(recvs, out_dtypes)],
                          grid_spec=grid_spec, name="rs_add_sibling",
                          compiler_params=_cparams(("arbitrary",)))(c, *gs, *recvs)


def add_chips(gs, recvs, r3s, place):
    n = len(gs)
    geo = [_half_geometry(g.shape, r.shape) for g, r in zip(gs, recvs)]
    counts = [nblk for _, nblk, _ in geo]
    starts, total, local = _work_items(counts)

    def body(p_ref, *refs):
        s = pl.program_id(0)
        up = lambda r: r[...].astype(F32)
        for a in range(n):
            g_ref, s_ref, o_ref = refs[a], refs[n + a], refs[5 * n + a]
            a_ref, b_ref, c_ref = refs[2 * n + 3 * a:2 * n + 3 * a + 3]

            @pl.when((s >= starts[a]) & (s < starts[a] + counts[a]))
            def _():
                o_ref[...] = (((g_ref[...] + up(s_ref)) + up(a_ref)) + up(b_ref)) + up(c_ref)

    own_idx = lambda s, p_ref, a: (p_ref[0],) + geo[a][2](local(a, s), p_ref[1])
    sib_idx = lambda s, p_ref, a: (p_ref[0], local(a, s), 0)
    chip_idx = lambda s, p_ref, a, k: (k, local(a, s), 0)
    spec = lambda a, idx, **kw: pl.BlockSpec((None,) + geo[a][0], functools.partial(idx, a=a, **kw))
    grid_spec = pltpu.PrefetchScalarGridSpec(
        num_scalar_prefetch=1, grid=(total,),
        in_specs=[spec(a, own_idx) for a in range(n)] + [spec(a, sib_idx) for a in range(n)]
        + [spec(a, chip_idx, k=k) for a in range(n) for k in range(3)],
        out_specs=[pl.BlockSpec(geo[a][0], functools.partial(lambda s, p_ref, a: (local(a, s), 0), a=a))
                   for a in range(n)])
    return pl.pallas_call(body, out_shape=[jax.ShapeDtypeStruct(r.shape[1:], F32) for r in recvs],
                          grid_spec=grid_spec, name="rs_add_chips", compiler_params=_cparams(("arbitrary",)))(
                              place, *gs, *recvs, *[r for r3 in r3s for r in (r3, r3, r3)])


def _remote(src, dst, ssem, rsem, dev):
    return pltpu.make_async_remote_copy(src_ref=src, dst_ref=dst, send_sem=ssem, recv_sem=rsem,
                                        device_id=dev, device_id_type=pl.DeviceIdType.MESH)


def _mesh_places():
    x, y, c = lax.axis_index("x"), lax.axis_index("y"), lax.axis_index("c")
    chips = [(1 - x, y), (x, 1 - y), (1 - x, 1 - y)]
    return x, y, c, (x, y, 1 - c), chips


def _hbm_specs(n):
    return [pl.BlockSpec(memory_space=pltpu.HBM) for _ in range(n)]


def _gather_body(ins, outs, n_split, send_sems, recv_sems, handshake):
    x, y, c, sibling, chips = _mesh_places()
    mine = 2 * x + y
    if handshake:
        barrier = pltpu.get_barrier_semaphore()
        peers = [sibling] + [(*chip, c) for chip in chips]
        for peer in peers:
            pl.semaphore_signal(barrier, inc=1, device_id=peer, device_id_type=pl.DeviceIdType.MESH)
        pl.semaphore_wait(barrier, len(peers))

    def half(a, chip_idx, which):
        rh = ins[a].shape[0] // 2
        return outs[a].at[chip_idx, pl.ds(which * rh, rh), :]

    sent = []
    for a in range(len(ins)):
        for k, chip in enumerate(chips):
            if a < n_split:
                rh = ins[a].shape[0] // 2
                src, dst = ins[a].at[pl.ds(c * rh, rh), :], half(a, mine, c)
            else:
                src, dst = ins[a], outs[a].at[mine]
            sent.append(_remote(src, dst, send_sems.at[a, k], recv_sems.at[a, k], (*chip, c)))
    for cp in sent:
        cp.start()
    for a in range(len(ins)):
        for k, chip in enumerate(chips):
            j = 2 * chip[0] + chip[1]
            region = half(a, j, c) if a < n_split else outs[a].at[j]
            _remote(region, region, send_sems.at[a, k], recv_sems.at[a, k], (*chip, c)).wait_recv()
            if a < n_split:
                fwd = _remote(region, region, send_sems.at[a, 3 + k], recv_sems.at[a, 3 + k], sibling)
                fwd.start()
                sent.append(fwd)
    for a in range(n_split):
        for k, chip in enumerate(chips):
            region = half(a, 2 * chip[0] + chip[1], 1 - c)
            _remote(region, region, send_sems.at[a, 3 + k], recv_sems.at[a, 3 + k], sibling).wait_recv()
    for cp in sent:
        cp.wait_send()


def gather_weights(shards, small):
    arrs = list(shards) + [small]
    n = len(arrs)

    def body(*refs):
        _gather_body(refs[:n], refs[n:2 * n], n - 1, refs[2 * n], refs[2 * n + 1], handshake=False)

    return pl.pallas_call(
        body, out_shape=[jax.ShapeDtypeStruct((4,) + a.shape, a.dtype) for a in arrs],
        in_specs=_hbm_specs(n), out_specs=_hbm_specs(n),
        scratch_shapes=[pltpu.SemaphoreType.DMA((n, 6)), pltpu.SemaphoreType.DMA((n, 6))],
        name="gather_weights")(*arrs)


def gather_weights_async(shards):
    n = len(shards)

    def body(*refs):
        _gather_body(refs[:n], refs[n:2 * n], n, refs[2 * n], refs[2 * n + 1], handshake=True)

    return pl.kernel(
        body, out_type=[jax.ShapeDtypeStruct((4,) + a.shape, a.dtype) for a in shards],
        mesh=plsc.ScalarSubcoreMesh(axis_name="seq", num_cores=1),
        scratch_types=[pltpu.SemaphoreType.DMA((n, 6)), pltpu.SemaphoreType.DMA((n, 6))],
        compiler_params=pltpu.CompilerParams(collective_id=1), name="gather_weights_async")(*shards)


def _sequencer_call(name, body, out_type, sem_shape, collective_id, args):
    return pl.kernel(
        body, out_type=out_type, mesh=plsc.ScalarSubcoreMesh(axis_name="seq", num_cores=1),
        scratch_types=[pltpu.SemaphoreType.DMA(sem_shape), pltpu.SemaphoreType.DMA(sem_shape)],
        compiler_params=pltpu.CompilerParams(collective_id=collective_id), name=name)(*args)


def _handshake(peers):
    barrier = pltpu.get_barrier_semaphore()
    for peer in peers:
        pl.semaphore_signal(barrier, inc=1, device_id=peer, device_id_type=pl.DeviceIdType.MESH)
    pl.semaphore_wait(barrier, len(peers))


def exchange_siblings(name, srcs, axes, collective_id):
    n = len(srcs)

    def body(*refs):
        ins, outs = refs[:n], refs[n:2 * n]
        send_sems, recv_sems = refs[2 * n:]
        x, y, c, sibling, chips = _mesh_places()
        _handshake([sibling])
        cps = []
        for a in range(n):
            src = ins[a]
            if axes[a] is not None:
                half = src.shape[axes[a]] // 2
                theirs = pl.ds((1 - c) * half, half)
                src = src.at[:, theirs, :] if axes[a] == 1 else src.at[:, :, theirs]
            cps.append(_remote(src, outs[a], send_sems.at[a], recv_sems.at[a], sibling))
        for cp in cps:
            cp.start()
        for cp in cps:
            cp.wait()

    def shape(g, axis):
        return g.shape if axis is None else tuple(d // 2 if k == axis else d for k, d in enumerate(g.shape))

    return _sequencer_call(name, body, [jax.ShapeDtypeStruct(shape(g, ax), g.dtype) for g, ax in zip(srcs, axes)],
                           (n,), collective_id, srcs)


def exchange_chips(name, s1s, collective_id):
    n = len(s1s)

    def body(*refs):
        ins, outs = refs[:n], refs[n:2 * n]
        send_sems, recv_sems = refs[2 * n:]
        x, y, c, sibling, chips = _mesh_places()
        _handshake([(*chip, c) for chip in chips])
        cps = []
        for a in range(n):
            for k, chip in enumerate(chips):
                cps.append(_remote(ins[a].at[2 * chip[0] + chip[1]], outs[a].at[k], send_sems.at[a, k],
                                   recv_sems.at[a, k], (*chip, c)))
        for cp in cps:
            cp.start()
        for cp in cps:
            cp.wait()

    return _sequencer_call(name, body, [jax.ShapeDtypeStruct((3,) + s.shape[1:], s.dtype) for s in s1s], (n, 3),
                           collective_id, s1s)


def allgather_small(v):
    m_per = v.shape[0]

    def body(x_ref, out_ref, send_sems, recv_sems, local_sem):
        x, y, c, sibling, chips = _mesh_places()
        me = (x, y, c)

        def rows(px, py, pc):
            return out_ref.at[pl.ds((4 * px + 2 * py + pc) * m_per, m_per), :]

        def copy(k, block, to, src=None):
            return _remote(rows(*block) if src is None else src, rows(*block), send_sems.at[k], recv_sems.at[k], to)

        mine = pltpu.make_async_copy(x_ref, rows(*me), local_sem)
        mine.start()
        first = [copy(0, me, sibling, src=x_ref)]
        first += [copy(1 + j, me, (*chip, c), src=x_ref) for j, chip in enumerate(chips)]
        for cp in first:
            cp.start()
        passed = [copy(4 + j, (*chip, c), sibling) for j, chip in enumerate(chips)]
        for j, chip in enumerate(chips):
            copy(1 + j, (*chip, c), me).wait_recv()
            passed[j].start()
        copy(0, sibling, me).wait_recv()
        for j, chip in enumerate(chips):
            copy(4 + j, (*chip, 1 - c), me).wait_recv()
        for cp in first + passed:
            cp.wait_send()
        mine.wait()

    return pl.pallas_call(
        body, out_shape=jax.ShapeDtypeStruct((8 * m_per, v.shape[1]), v.dtype),
        in_specs=[pl.BlockSpec(memory_space=pltpu.VMEM)], out_specs=pl.BlockSpec(memory_space=pltpu.VMEM),
        scratch_shapes=[pltpu.SemaphoreType.DMA((7,)), pltpu.SemaphoreType.DMA((7,)), pltpu.SemaphoreType.DMA],
        name="allgather_small")(v)


def _lower_bounds(lb_param):
    lbs = jax.nn.softmax(lb_param.astype(F32), axis=0)
    return jnp.cumsum(lbs, axis=0) - lbs[0]


def _even_fwd(x, i, W, lower, kv, slopes, T):
    O = EVEN_OFF
    g = W["norm_even"][i].reshape(1, D_MODEL)
    h, p = norm_project("mm_in_e", x, g, W["w_in_e"][i])
    kvp = jnp.pad(p[:, O["kA"]:O["kA"] + 2 * W_KV_A], ((BLOCK, BLOCK), (0, 0)))
    sink = jnp.repeat(W["sink"][i], BLOCK).reshape(N_Q_A * BLOCK, 1)
    a = attn_fwd(p, O["qA"], kvp, sink, slopes, T)
    scan_raws = [[((p, O["qB"]), W_B), ((p, O[z]), W_B), ((p, O["iB"]), W_B)] for z in ("zf", "zb")]
    scan_pars = [[lower[i][0:1]], [lower[i][1:2]]]
    o_f, o_b, ss_f, ss_b = scan_fwd("scan_fwd_h", hgrn_prep, scan_raws, scan_pars, N_HEADS_B, HEAD_DIM_B, HEAD_DIM_B, T)
    mo = mem_fwd(p, O["qM"], kv, T)
    hg = W["hgrn_norm"][i].reshape(1, W_B)
    post_ins = [("row", a, 0, W_A), ("row", o_f, 0, W_B), ("row", o_b, 0, W_B), ("row", mo, 0, W_M),
                ("row", p, O["gA"], W_A), ("row", p, O["gB"], W_B), ("row", p, O["gM"], W_M), ("full", hg)]
    x_new = mix_project("even_out", even_post_tile, T, post_ins, W["w_out_e"][i], x)
    return x_new, dict(x=x, g=g, h=h, p=p, kvp=kvp, sink=sink, scan_raws=scan_raws, scan_pars=scan_pars,
                       ss=(ss_f, ss_b), post_ins=post_ins)


def _add2(a, b):
    return a.astype(F32) + b.astype(F32)


def _assemble_even(dqA, dgA, dqB_f, dqB_b, dzf, dzb, diB_f, diB_b, dgB, dqM, dgM, dkvA):
    parts = [dqA, dgA, _add2(dqB_f, dqB_b), dzf, dzb, _add2(diB_f, diB_b), dgB, dqM, dgM, dkvA]
    return (jnp.concatenate([t.astype(BF16) for t in parts], axis=-1),)


def _even_bwd(dxo, sv, i, W, kv, slopes, T, sync):
    O = EVEN_OFF
    p = sv["p"]
    da, dof, dmo, dgA, dgB, dgM, dhg, dwo = mix_project_bwd("even_out_bwd", even_post_tile, T, sv["post_ins"],
                                                            W["w_out_e"][i], dxo, skip=(2,), narrow=(4, 5, 6))
    da = sync(da)
    dqA, dkvp, dsink = attn_bwd(p, O["qA"], sv["kvp"], sv["sink"], slopes, da, T)
    dkvA = dkvp[BLOCK:-BLOCK]
    dqB_f, dzf, diB_f, dqB_b, dzb, diB_b, dlow_f, dlow_b = scan_bwd(
        "scan_bwd_h", hgrn_prep, sv["scan_raws"], sv["scan_pars"], sv["ss"], (dof, 0), N_HEADS_B, HEAD_DIM_B, HEAD_DIM_B, T)
    dqB_f = sync(dqB_f)
    row = lambda arr, w: ("row", arr, 0, w)
    dlow = jnp.concatenate([dlow_f, dlow_b], axis=0)
    dqM, dkv = mem_bwd(p, O["qM"], kv, dmo, T)
    pieces = [row(dqA, W_A), row(dgA, W_A), row(dqB_f, W_B), row(dqB_b, W_B), row(dzf, W_B), row(dzb, W_B),
              row(diB_f, W_B), row(diB_b, W_B), row(dgB, W_B), row(dqM, W_M), row(dgM, W_M), row(dkvA, 2 * W_KV_A)]
    dp, dx, dg = norm_project_bwd("mm_in_e_bwd", _assemble_even, pieces, W["w_in_e"][i], sv["x"], sv["g"], dxo)
    dwi = matmul("mm_dwi_e", sv["h"], dp, "tn")
    return dx, dict(w_in=dwi, w_out=dwo, norm=dg[0], sink=dsink.reshape(N_Q_A), low=dlow, hg=dhg[0], kv=dkv)


def _pad_gate_up(w_up):
    z = jnp.zeros((2, 128, WK_C), F32)
    z = z.at[0, 0:GATE_RANK].set(w_up[0])
    return z.at[1, GATE_RANK:2 * GATE_RANK].set(w_up[1])


def _odd_fwd(x, i, W, kv, T):
    O = ODD_OFF
    g = W["norm_odd"][i].reshape(1, D_MODEL)
    h, p = norm_project("mm_in_o", x, g, W["w_in_o"][i])
    wup = _pad_gate_up(W["w_gate_up"][i])
    one_dir = [((p, O["qC"]), WK_C), ((p, O["kC"]), WK_C), ((p, O["vC"]), WV_C), ((p, O["rr"]), 128)]
    scan_raws = [one_dir, one_dir]
    scan_pars = [[wup[d], W["b_gate"][i][d:d + 1]] for d in range(2)]
    o_f, o_b, ss_f, ss_b = scan_fwd("scan_fwd_g", gla_prep, scan_raws, scan_pars, N_HEADS_C, DK_C, DV_C, T)
    mo = mem_fwd(p, O["qM"], kv, T)
    gg = W["gla_norm"][i].reshape(1, WV_C)
    post_ins = [("row", o_f, 0, WV_C), ("row", o_b, 0, WV_C), ("row", mo, 0, W_M),
                ("row", p, O["gC"], WV_C), ("row", p, O["gM"], W_M), ("full", gg)]
    x_new = mix_project("odd_out", odd_post_tile, T, post_ins, W["w_out_o"][i], x)
    return x_new, dict(x=x, g=g, h=h, p=p, scan_raws=scan_raws, scan_pars=scan_pars, ss=(ss_f, ss_b),
                       post_ins=post_ins)


def _assemble_odd(dq0, dq1, dk0, dk1, dv0, dv1, dgC, dqM, dgM, dr0, dr1):
    parts = [_add2(dq0, dq1), _add2(dk0, dk1), _add2(dv0, dv1), dgC, dqM, dgM, _add2(dr0, dr1)]
    return (jnp.concatenate([t.astype(BF16) for t in parts], axis=-1),)


def _odd_bwd(dxo, sv, i, W, kv, T, sync):
    O = ODD_OFF
    p = sv["p"]
    dof, dmo, dgC, dgM, dgg, dwo = mix_project_bwd("odd_out_bwd", odd_post_tile, T, sv["post_ins"], W["w_out_o"][i],
                                                   dxo, skip=(1,), narrow=(3, 4))
    dof = sync(dof)
    dqf, dkf, dvf, dr_f, dqb, dkb, dvb, dr_b, dwup_f, dbg_f, dwup_b, dbg_b = scan_bwd(
        "scan_bwd_g", gla_prep, sv["scan_raws"], sv["scan_pars"], sv["ss"], (dof, 0), N_HEADS_C, DK_C, DV_C, T)
    dqf = sync(dqf)
    row = lambda arr, w: ("row", arr, 0, w)
    dqM, dkv = mem_bwd(p, O["qM"], kv, dmo, T)
    pieces = [row(dqf, WK_C), row(dqb, WK_C), row(dkf, WK_C), row(dkb, WK_C), row(dvf, WV_C), row(dvb, WV_C),
              row(dgC, WV_C), row(dqM, W_M), row(dgM, W_M), row(dr_f, 128), row(dr_b, 128)]
    dp, dx, dg = norm_project_bwd("mm_in_o_bwd", _assemble_odd, pieces, W["w_in_o"][i], sv["x"], sv["g"], dxo)
    dwi = matmul("mm_dwi_o", sv["h"], dp, "tn")
    dw_up = jnp.stack([dwup_f[0:GATE_RANK], dwup_b[GATE_RANK:2 * GATE_RANK]])
    dbg = jnp.concatenate([dbg_f, dbg_b], axis=0)
    return dx, dict(w_in=dwi, w_out=dwo, norm=dg[0], w_up=dw_up, b_gate=dbg, gg=dgg[0], kv=dkv)


def local_step(x, mem, target, W, later=None, on_layer_grads=None, sync=lambda a: a):
    T = x.shape[0]
    slopes = jnp.repeat(2.0 ** (-8.0 * jnp.arange(1, N_Q_A + 1, dtype=F32) / N_Q_A), BLOCK).reshape(N_Q_A * BLOCK, 1)
    lower, lower_vjp = jax.vjp(_lower_bounds, W["lb_param"])
    mem_g = W["mem_norm"].reshape(1, D_MODEL)
    (mem_n,) = rows_call("mem_rms_fwd", rms_tile, N_MEM, [("row", mem, 0, D_MODEL), ("full", mem_g)], [D_MODEL], [BF16])
    kvs, saved = [], []
    for l in range(DEPTH):
        if l == 1 and later is not None:
            x, W = later(x, W)
        kvs.append(matmul("mm_kv", mem_n, W["w_kv"][l], "nn"))
        if l % 2 == 0:
            x, sv = _even_fwd(x, l // 2, W, lower, kvs[l], slopes, T)
        else:
            x, sv = _odd_fwd(x, l // 2, W, kvs[l], T)
        saved.append(sv)
    loss, dx, dgf = final_call(x, W["final_norm"].reshape(1, D_MODEL), target, T)
    per = [None] * DEPTH
    dmem_n = None
    for l in reversed(range(DEPTH)):
        if l % 2 == 0:
            dx, per[l] = _even_bwd(dx, saved[l], l // 2, W, kvs[l], slopes, T, sync)
        else:
            dx, per[l] = _odd_bwd(dx, saved[l], l // 2, W, kvs[l], T, sync)
        per[l]["w_kv"] = matmul("mm_dwkv", mem_n, per[l]["kv"], "tn")
        dmem_n = matmul("mm_dmem", per[l]["kv"], W["w_kv"][l], "nt", add=dmem_n)
        if on_layer_grads is not None:
            dx = on_layer_grads(l, dx, per[l])
    dw_kv = [per[l]["w_kv"] for l in range(DEPTH)]
    (dmem_norm,) = rows_vjp_call("mem_rms_bwd", rms_tile, N_MEM, [("row", mem, 0, D_MODEL), ("full", mem_g)],
                                 [[("row", dmem_n, 0, D_MODEL)]], skip=(0,))
    ev, od = (per[0], per[2]), (per[1], per[3])
    (d_lb,) = lower_vjp(jnp.stack([e["low"] for e in ev]))
    grads = dict(
        w_in_e=jnp.stack([e["w_in"] for e in ev]), w_in_o=jnp.stack([o["w_in"] for o in od]),
        w_out_e=jnp.stack([e["w_out"] for e in ev]), w_out_o=jnp.stack([o["w_out"] for o in od]),
        w_kv=jnp.stack(dw_kv), norm_even=jnp.stack([e["norm"] for e in ev]), sink=jnp.stack([e["sink"] for e in ev]),
        lb_param=d_lb, hgrn_norm=jnp.stack([e["hg"] for e in ev]), norm_odd=jnp.stack([o["norm"] for o in od]),
        w_gate_up=jnp.stack([o["w_up"] for o in od]), b_gate=jnp.stack([o["b_gate"] for o in od]),
        gla_norm=jnp.stack([o["gg"] for o in od]), mem_norm=dmem_norm[0], final_norm=dgf[0])
    return loss, dx, grads


SMALL_SPECS = (("lb_param", (2, 2, 128)), ("norm_odd", (2, 256)), ("w_gate_up", (2, 2, 16, 128)),
               ("b_gate", (2, 2, 128)), ("gla_norm", (2, 256)))
SMALL_ROWS = 80


def _pack_small_local(d):
    return jnp.concatenate([d[n].reshape(-1) for n, _ in SMALL_SPECS]).reshape(SMALL_ROWS, 128)


def _unpack_small_local(b):
    flat, out, o = b.reshape(-1), {}, 0
    for n, shp in SMALL_SPECS:
        sz = int(np.prod(shp))
        out[n] = flat[o:o + sz].reshape(shp)
        o += sz
    return out


def _unpack_small_full(g4):
    per = [_unpack_small_local(g4[j]) for j in range(4)]
    return {n: jnp.concatenate([per[j][n] for j in range(4)], axis=-1) for n, _ in SMALL_SPECS}


def _pack_small_blocks(full):
    blocks = []
    for j in range(4):
        blocks.append(_pack_small_local({n: full[n][..., j * shp[-1]:(j + 1) * shp[-1]] for n, shp in SMALL_SPECS}))
    return jnp.stack(blocks)


def _cols(t, order, off, widths):
    return [t[..., off[n]:off[n] + widths[n]] for n in order]


EVEN_REF_ORDER = ("qA", "kA", "vA", "gA", "qB", "zf", "zb", "iB", "gB", "qM", "gM")
ODD_REF_ORDER = ("qC", "kC", "vC", "gC", "rr", "qM", "gM")


def _layer_weights(l, g_in, g_out, g_kv):
    t = g_in.transpose(1, 0, 2).reshape(D_MODEL, -1)
    if l % 2 == 0:
        w_in = jnp.concatenate(_cols(t, EVEN_ORDER, EVEN_REF_OFF, EVEN_W), axis=-1)
    else:
        w_in = jnp.concatenate(_cols(t, ODD_ORDER, ODD_REF_OFF, ODD_W) + [jnp.zeros((D_MODEL, ODD_PAD - ODD_IN), BF16)],
                               axis=-1)
    return w_in, g_out.reshape(MIX, D_MODEL), g_kv.reshape(D_MODEL, 2 * W_M)


def _layer_grad_blocks(l, gl):
    if l % 2 == 0:
        t = jnp.concatenate(_cols(gl["w_in"], EVEN_REF_ORDER, EVEN_OFF, EVEN_W), axis=-1)
    else:
        t = jnp.concatenate(_cols(gl["w_in"], ODD_REF_ORDER, ODD_OFF, ODD_W), axis=-1)
    b_in = t.reshape(D_MODEL, 4, -1).transpose(1, 2, 0)
    return [b_in, gl["w_out"].reshape(4, MIX // 4, D_MODEL), gl["w_kv"].reshape(4, D_MODEL // 4, 2 * W_M)]


WEIGHT_NAMES = ("norm_even", "w_in_even", "sink", "lb_param", "hgrn_norm", "w_out_even", "norm_odd", "w_in_odd",
                "w_gate_up", "b_gate", "gla_norm", "w_out_odd", "mem_norm", "w_mem_kv", "final_norm")


def kernel(x, mem, norm_even, w_in_even, sink, lb_param, hgrn_norm, w_out_even, norm_odd, w_in_odd, w_gate_up, b_gate, gla_norm, w_out_odd, mem_norm, w_mem_kv, final_norm, loss_target, m_norm_even, m_w_in_even, m_sink, m_lb_param, m_hgrn_norm, m_w_out_even, m_norm_odd, m_w_in_odd, m_w_gate_up, m_b_gate, m_gla_norm, m_w_out_odd, m_mem_norm, m_w_mem_kv, m_final_norm, v_norm_even, v_w_in_even, v_sink, v_lb_param, v_hgrn_norm, v_w_out_even, v_norm_odd, v_w_in_odd, v_w_gate_up, v_b_gate, v_gla_norm, v_w_out_odd, v_mem_norm, v_w_mem_kv, v_final_norm):
    w = dict(zip(WEIGHT_NAMES, (norm_even, w_in_even, sink, lb_param, hgrn_norm, w_out_even, norm_odd, w_in_odd,
                                w_gate_up, b_gate, gla_norm, w_out_odd, mem_norm, w_mem_kv, final_norm)))
    m = dict(zip(WEIGHT_NAMES, (m_norm_even, m_w_in_even, m_sink, m_lb_param, m_hgrn_norm, m_w_out_even, m_norm_odd,
                                m_w_in_odd, m_w_gate_up, m_b_gate, m_gla_norm, m_w_out_odd, m_mem_norm, m_w_mem_kv,
                                m_final_norm)))
    v = dict(zip(WEIGHT_NAMES, (v_norm_even, v_w_in_even, v_sink, v_lb_param, v_hgrn_norm, v_w_out_even, v_norm_odd,
                                v_w_in_odd, v_w_gate_up, v_b_gate, v_gla_norm, v_w_out_odd, v_mem_norm, v_w_mem_kv,
                                v_final_norm)))
    ci = lax.axis_index("c").astype(jnp.int32).reshape(1)
    chip = (2 * lax.axis_index("x") + lax.axis_index("y")).astype(jnp.int32).reshape(1)

    shards = []
    for l in range(DEPTH):
        names = ("w_in_even", "w_out_even") if l % 2 == 0 else ("w_in_odd", "w_out_odd")
        shards.append([w[names[0]][l // 2].astype(BF16), w[names[1]][l // 2].astype(BF16), w_mem_kv[l].astype(BF16)])
    small = _pack_small_local(w)
    own = lambda g, s: lax.dynamic_update_slice(g, s[None], (chip[0], 0, 0))
    first = [own(g, s) for g, s in zip(gather_weights(shards[0], small), shards[0] + [small])]
    later_shards = shards[1] + shards[2] + shards[3]
    later_raw = gather_weights_async(later_shards)
    w0 = _layer_weights(0, *first[0:3])
    W = dict(w_in_e=[w0[0]], w_out_e=[w0[1]], w_kv=[w0[2]])
    W.update(_unpack_small_full(first[3]))
    W.update({n: w[n] for n in ("norm_even", "sink", "hgrn_norm", "mem_norm", "final_norm")})

    def later(x1, W):
        x1, raw = lax.optimization_barrier((x1, list(later_raw)))
        g = [own(a, s) for a, s in zip(raw, later_shards)]
        w1, w2, w3 = (_layer_weights(l, *g[3 * (l - 1):3 * l]) for l in (1, 2, 3))
        W = dict(W)
        W.update(w_in_e=[w0[0], w2[0]], w_in_o=[w1[0], w3[0]], w_out_e=[w0[1], w2[1]], w_out_o=[w1[1], w3[1]],
                 w_kv=[w0[2], w1[2], w2[2], w3[2]])
        return x1, W

    place = jnp.concatenate([chip, ci])

    def start(tag, blocks, wire):
        axes = [2 if b.shape[1] == ODD_IN // 4 else 1 for b in blocks]
        return dict(tag=tag, blocks=blocks, wire=wire, step=0,
                    recv=exchange_siblings(f"rs_siblings_{tag}", blocks, axes, 2))

    def advance(p, a=None):
        tie = (lambda v: (a, v)) if a is None else (lambda v: lax.optimization_barrier((a, v)))
        if p["step"] == 0:
            a, sums = tie(add_sibling(p["blocks"], p["recv"], ci, p["wire"]))
            p["recv3"] = exchange_chips(f"rs_chips_{p['tag']}", sums, 3)
        else:
            a, p["mine"] = tie(add_chips(p["blocks"], p["recv"], p["recv3"], place))
            p["other"] = exchange_siblings(f"rs_final_{p['tag']}", p["mine"], [None] * len(p["mine"]), 4)
        p["step"] += 1
        return a

    pipes, first_layer = [], {}

    def sync(a):
        for p in pipes:
            if p["step"] < 3:
                key = ("recv", "recv3", "other")[p["step"]]
                a, arrived = lax.optimization_barrier((a, list(p[key])))
                p[key] = arrived
                if p["step"] < 2:
                    a = advance(p, a)
                else:
                    p["step"] = 3
        return a

    def on_layer_grads(l, dx, gl):
        dx = sync(dx)
        if l == 0:
            first_layer.update(gl)
        else:
            pipes.append(start(f"l{l}", _layer_grad_blocks(l, gl), [BF16] * 3))
        return dx

    loss_tile, dx, grads = local_step(x[0], mem[0], loss_target[0], W, later, on_layer_grads, sync)
    last = start("l0", _layer_grad_blocks(0, first_layer) + [_pack_small_blocks(grads)], [BF16] * 3 + [F32])
    for p in pipes + [last]:
        while p["step"] < (1 if p is last else 2):
            advance(p)
    by_layer = {int(p["tag"][1:]): p for p in pipes + [last]}
    halves = lambda layers, k: (jnp.stack([by_layer[l]["mine"][k] for l in layers]),
                                jnp.stack([by_layer[l]["other"][k] for l in layers]))
    gl, upd = {}, {}

    pack = jnp.zeros((8, D_MODEL), F32)
    pack = pack.at[0:2].set(grads["norm_even"]).at[2].set(grads["hgrn_norm"].reshape(-1))
    pack = pack.at[3].set(grads["mem_norm"]).at[4].set(grads["final_norm"])
    pack = pack.at[5, 0:16].set(grads["sink"].reshape(-1)).at[5, 16].set(loss_tile[0, 0])
    tot = sum_devices(allgather_small(pack))
    gl.update(norm_even=tot[0:2], hgrn_norm=tot[2].reshape(2, W_B), mem_norm=tot[3], final_norm=tot[4],
              sink=tot[5, 0:16].reshape(2, N_Q_A))
    loss = tot[5, 16]
    for n in ("norm_even", "hgrn_norm", "mem_norm", "final_norm", "sink"):
        upd[n] = adamw_call(w[n], gl[n], m[n], v[n])
    tr_ = lambda a: jnp.swapaxes(a, 1, 2)
    gl["w_in_odd"], *upd["w_in_odd"] = [tr_(o) for o in adamw_halves(
        tr_(w["w_in_odd"]), *halves((1, 3), 0), tr_(m["w_in_odd"]), tr_(v["w_in_odd"]), ci)]
    gl["w_out_odd"], *upd["w_out_odd"] = adamw_halves(w["w_out_odd"], *halves((1, 3), 1), m["w_out_odd"],
                                                      v["w_out_odd"], ci)
    early = [upd[n] for n in sorted(upd)] + [gl["w_in_odd"], gl["w_out_odd"]]
    last["recv3"], early = lax.optimization_barrier((list(last["recv3"]), early))
    for n, res in zip(sorted(upd), early):
        upd[n] = res
    gl["w_in_odd"], gl["w_out_odd"] = early[-2:]
    advance(last)

    big = dict(w_in_even=halves((0, 2), 0), w_out_even=halves((0, 2), 1), w_mem_kv=halves((0, 1, 2, 3), 2))
    s_mine, s_other = last["mine"][3], last["other"][3]
    g_small = jnp.where(ci[0] == 0, jnp.concatenate([s_mine, s_other]), jnp.concatenate([s_other, s_mine]))
    gl.update(_unpack_small_local(g_small))
    for n in WEIGHT_NAMES:
        if n == "w_in_even":
            gl[n], *upd[n] = [tr_(o) for o in adamw_halves(tr_(w[n]), *big[n], tr_(m[n]), tr_(v[n]), ci)]
        elif n in big:
            gl[n], *upd[n] = adamw_halves(w[n], *big[n], m[n], v[n], ci)
        elif n not in upd:
            upd[n] = adamw_call(w[n], gl[n], m[n], v[n])
    return (loss, dx[None], *[gl[n] for n in WEIGHT_NAMES], *[upd[n][0] for n in WEIGHT_NAMES],
            *[upd[n][1] for n in WEIGHT_NAMES], *[upd[n][2] for n in WEIGHT_NAMES])
```

```python
import functools

import numpy as np
import jax
import jax.numpy as jnp
from jax import lax
from jax.experimental import pallas as pl
from jax.experimental.pallas import tpu as pltpu
from jax.experimental.pallas import tpu_sc as plsc

F32 = jnp.float32
BF16 = jnp.bfloat16

D_MODEL = 1024
DEPTH = 4
N_Q_A, N_KV_A, HEAD_DIM_A = 8, 2, 64
W_A, W_KV_A = 512, 128
WINDOW = 128
BLOCK = 128
N_HEADS_B, HEAD_DIM_B, W_B = 4, 128, 512
N_HEADS_C, DK_C, DV_C, WK_C, WV_C = 4, 128, 256, 512, 1024
GATE_RANK = 16
GATE_TEMP = 16.0
N_MEM, N_HEADS_M, HEAD_DIM_M, W_M = 256, 4, 128, 512
EPS = 1e-6
MASK_VALUE = -1e30
MIN_GATE = 1e-30
EVEN_IN, ODD_IN = 4864, 4128
ODD_PAD = 4224
MIX = 1536
ADAM_LR, ADAM_B1, ADAM_B2, ADAM_EPS, ADAM_WD, ADAM_STEP = 0.001, 0.9, 0.999, 1e-08, 0.01, 10

SCAN_CHUNK = 128
SCAN_SUB = 2
SCAN_LEVELS = 7
VMEM_LIMIT = 56 * 1024 * 1024

EVEN_REF_OFF = dict(qA=0, kA=512, vA=640, gA=768, qB=1280, zf=1792, zb=2304, iB=2816, gB=3328, qM=3840, gM=4352)
EVEN_W = dict(qA=512, kA=128, vA=128, gA=512, qB=512, zf=512, zb=512, iB=512, gB=512, qM=512, gM=512)
EVEN_ORDER = ("qA", "gA", "qB", "zf", "zb", "iB", "gB", "qM", "gM", "kA", "vA")
ODD_REF_OFF = dict(qC=0, kC=512, vC=1024, gC=2048, rr=3072, qM=3104, gM=3616)
ODD_W = dict(qC=512, kC=512, vC=1024, gC=1024, rr=32, qM=512, gM=512)
ODD_ORDER = ("qC", "kC", "vC", "gC", "qM", "gM", "rr")


def _offsets(order, widths):
    off, o = {}, 0
    for n in order:
        off[n] = o
        o += widths[n]
    return off


EVEN_OFF = _offsets(EVEN_ORDER, EVEN_W)
ODD_OFF = _offsets(ODD_ORDER, ODD_W)


def _dg(a, b, ca, cb):
    return lax.dot_general(a.astype(BF16), b.astype(BF16), (((ca,), (cb,)), ((), ())),
                           preferred_element_type=F32)


def dot_nn(a, b):
    return _dg(a, b, 1, 0)


def dot_nt(a, b):
    return _dg(a, b, 1, 1)


def dot_tn(a, b):
    return _dg(a, b, 0, 0)


@jax.custom_vjp
def bdot(a, b):
    return dot_nn(a, b)


bdot.defvjp(lambda a, b: (dot_nn(a, b), (a, b)),
            lambda r, g: (dot_nt(g, r[1]), dot_tn(r[0], g)))


@jax.custom_vjp
def bdot_t(a, b):
    return dot_nt(a, b)


bdot_t.defvjp(lambda a, b: (dot_nt(a, b), (a, b)),
              lambda r, g: (dot_nn(g, r[1]), dot_tn(g, r[0])))


@jax.custom_vjp
def bdot_tn(a, b):
    return dot_tn(a, b)


bdot_tn.defvjp(lambda a, b: (dot_tn(a, b), (a, b)),
               lambda r, g: (dot_nt(r[1], g), dot_nn(r[0], g)))


def _split_mm(h, x):
    hi = x.astype(BF16)
    lo = (x - hi.astype(F32)).astype(BF16)
    return (lax.dot_general(h, hi, (((1,), (0,)), ((), ())), preferred_element_type=F32)
            + lax.dot_general(h, lo, (((1,), (0,)), ((), ())), preferred_element_type=F32))


def _sigmoid(z):
    return 1.0 / (1.0 + jnp.exp(-z))


def _silu(z):
    return z * _sigmoid(z)


def _log_sigmoid(z):
    return jnp.minimum(z, 0.0) - jnp.log(1.0 + jnp.exp(-jnp.abs(z)))


def _rms(x, g):
    return x * lax.rsqrt(jnp.mean(x * x, axis=-1, keepdims=True) + EPS) * g


def rms_tile(x, g):
    return (_rms(x, g),)


@functools.partial(jax.custom_vjp, nondiff_argnums=(1, 2))
def split(x, n, axis):
    w = x.shape[axis] // n
    return tuple(lax.slice_in_dim(x, h * w, (h + 1) * w, axis=axis) for h in range(n))


split.defvjp(lambda x, n, axis: (split(x, n, axis), None),
             lambda n, axis, _, cts: (jnp.concatenate(cts, axis=axis),))


def _group_rms(o, g, heads):
    return jnp.concatenate([_rms(oh, gh) for oh, gh in zip(split(o, heads, 1), split(g, heads, 1))], axis=-1)


def even_post_tile(a, o2f, o2b, mo, gA, gB, gM, hg):
    y = _group_rms(o2f + o2b, hg, N_HEADS_B)
    return (jnp.concatenate([a * _silu(gA), y * _silu(gB), mo * _silu(gM)], axis=-1),)


def odd_post_tile(o2f, o2b, mo, gC, gM, gg):
    y = _group_rms(o2f + o2b, gg, N_HEADS_C)
    return (jnp.concatenate([y * _silu(gC), mo * _silu(gM)], axis=-1),)


def hgrn_prep(raw, par):
    qB, z, iB = raw
    (lb,) = par
    f = lb + (1.0 - lb) * _sigmoid(z)
    return _silu(qB), (1.0 - lb) * _sigmoid(-z), iB, jnp.log(jnp.maximum(f, MIN_GATE))


def gla_prep(raw, par):
    qC, kC, vC, r128 = raw
    wup, bg = par
    return qC * (DK_C ** -0.5), kC, vC, _log_sigmoid(bdot(r128, wup) + bg) / GATE_TEMP


def mem_tile(q, k, v):
    s = bdot_t(q, k) * (HEAD_DIM_M ** -0.5)
    m = lax.stop_gradient(jnp.max(s, axis=-1, keepdims=True))
    p = jnp.exp(s - m)
    p = p / jnp.sum(p, axis=-1, keepdims=True)
    return (bdot(p, v),)


ATTN_GROUP = N_Q_A // N_KV_A


def attn_block(q, ks, vs, sink, slope, c, seq):
    rows = q.shape[0]
    i = lax.broadcasted_iota(jnp.int32, (rows, 3 * BLOCK), 0) % BLOCK
    j = lax.broadcasted_iota(jnp.int32, (rows, 3 * BLOCK), 1)
    dist = jnp.abs(i - j + BLOCK).astype(F32)
    kpos = (c - 1) * BLOCK + j
    valid = (dist <= WINDOW) & (kpos >= 0) & (kpos < seq)
    s = bdot_t(q, ks) * (HEAD_DIM_A ** -0.5)
    s = jnp.where(valid, s - slope * dist, MASK_VALUE)
    m = lax.stop_gradient(jnp.maximum(jnp.max(s, axis=-1, keepdims=True), sink))
    p = jnp.where(valid, jnp.exp(s - m), 0.0)
    denom = jnp.sum(p, axis=-1, keepdims=True) + jnp.exp(sink - m)
    return bdot(p, vs) / denom


def scan_chunk(q, k, v, e, tot, st, qm, pm):
    C = SCAN_CHUNK
    e = split(e, 2 + SCAN_LEVELS, 0)
    qe = q * jnp.exp(e[0])
    kd = k * jnp.exp(e[1])
    r = lax.broadcasted_iota(jnp.int32, (C, C), 0)
    s = lax.broadcasted_iota(jnp.int32, (C, C), 1)
    a = jnp.where(r == s, jnp.sum(q * k, axis=-1, keepdims=True), 0.0)
    for l in range(SCAN_LEVELS):
        u = jnp.where(qm[l * C:(l + 1) * C] != 0.0, q, k) * jnp.exp(e[2 + l])
        a = a + bdot_t(u, u) * pm[l * C:(l + 1) * C]
    o = bdot_t(qe, st) + bdot(a, v)
    st_new = st * jnp.exp(tot) + bdot_tn(v, kd)
    return o, st_new


def _scan_consts():
    C, L = SCAN_CHUNK, SCAN_LEVELS
    t = np.arange(C)[:, None]
    r = np.arange(C)[None, :]
    blocks = [(r <= t), (r > t)]
    qms, pms = [], []
    for l in range(1, L + 1):
        m = C >> l
        upper_t = (t % (2 * m)) >= m
        upper_r = (r % (2 * m)) >= m
        same_half = (t // m) == (r // m)
        blocks.append(same_half & np.where(upper_t, r <= t, r > t))
        qms.append(np.broadcast_to(upper_t, (C, C)))
        pms.append(((t // (2 * m)) == (r // (2 * m))) & upper_t & ~upper_r)
    hf = np.concatenate(blocks, axis=0).astype(np.float32)
    flip = lambda mat: mat.reshape(-1, C, C)[:, ::-1, ::-1].reshape(-1, C)
    qmf = np.concatenate(qms, axis=0).astype(np.float32)
    pmf = np.concatenate(pms, axis=0).astype(np.float32)
    h = np.stack([hf, flip(hf)])
    ht = np.stack([h[0].T, h[1].T])
    qm = np.stack([qmf, 1.0 - qmf])
    pm = np.stack([pmf, flip(pmf)])
    return h, ht, qm, pm


def _cparams(sem):
    return pltpu.CompilerParams(dimension_semantics=sem, vmem_limit_bytes=VMEM_LIMIT)


def _row_tile(T):
    return min(T, 512)


def _in_spec(spec, tr):
    kind = spec[0]
    if kind == "row":
        _, arr, off, w = spec
        assert off % w == 0
        return arr, pl.BlockSpec((tr, w), functools.partial(lambda i, b: (i, b), b=off // w))
    if kind == "row3":
        _, arr, d, off, w = spec
        assert off % w == 0
        return arr, pl.BlockSpec((None, tr, w), functools.partial(lambda i, d, b: (d, i, b), d=d, b=off // w))
    _, arr = spec
    return arr, pl.BlockSpec(arr.shape, functools.partial(lambda i, n: (0,) * n, n=arr.ndim))


def rows_call(name, tile_fn, T, ins, out_widths, out_dtypes=None, stacks=None):
    tr = _row_tile(T)
    n_in = len(ins)
    out_dtypes = out_dtypes or [F32] * len(out_widths)
    stacks = stacks or [(k,) for k in range(len(out_widths))]

    def body(*refs):
        vals = [r[...] for r in refs[:n_in]]
        outs = tile_fn(*vals)
        for r, members in zip(refs[n_in:], stacks):
            if len(members) == 1:
                r[...] = outs[members[0]].astype(r.dtype)
            else:
                for d, k in enumerate(members):
                    r[d] = outs[k].astype(r.dtype)

    in_specs, args = [], []
    for spec in ins:
        arr, bs = _in_spec(spec, tr)
        args.append(arr)
        in_specs.append(bs)
    out_specs, out_shape = [], []
    for w, dt, members in zip(out_widths, out_dtypes, stacks):
        n = len(members)
        if n == 1:
            out_specs.append(pl.BlockSpec((tr, w), lambda i: (i, 0)))
            out_shape.append(jax.ShapeDtypeStruct((T, w), dt))
        else:
            out_specs.append(pl.BlockSpec((n, tr, w), lambda i: (0, i, 0)))
            out_shape.append(jax.ShapeDtypeStruct((n, T, w), dt))
    return pl.pallas_call(body, out_shape=out_shape, grid=(T // tr,), in_specs=in_specs, out_specs=out_specs,
                          name=name, compiler_params=_cparams(("arbitrary",)))(*args)


def rows_vjp_call(name, tile_fn, T, ins, cts, skip=(), narrow=()):
    tr = _row_tile(T)
    n_in = len(ins)
    n_ct = [len(c) for c in cts]
    want = [k for k in range(n_in) if k not in skip]

    def body(*refs):
        i = pl.program_id(0)
        vals = [r[...] for r in refs[:n_in]]
        ct, pos = [], n_in
        for n in n_ct:
            acc = refs[pos][...]
            for r in refs[pos + 1:pos + n]:
                acc = acc + r[...]
            ct.append(acc)
            pos += n
        _, vjp = jax.vjp(tile_fn, *vals)
        grads = vjp(tuple(ct))
        for r, k in zip(refs[pos:], want):
            if ins[k][0] == "full":
                @pl.when(i == 0)
                def _():
                    r[...] = jnp.zeros_like(r)
                r[...] += grads[k]
            else:
                r[...] = grads[k].astype(r.dtype)

    in_specs, args = [], []
    for spec in list(ins) + [s for c in cts for s in c]:
        arr, bs = _in_spec(spec, tr)
        args.append(arr)
        in_specs.append(bs)
    out_specs, out_shape = [], []
    for k in want:
        if ins[k][0] == "full":
            arr = ins[k][1]
            out_specs.append(pl.BlockSpec(arr.shape, functools.partial(lambda i, n: (0,) * n, n=arr.ndim)))
            out_shape.append(jax.ShapeDtypeStruct(arr.shape, F32))
        else:
            w = ins[k][-1]
            out_specs.append(pl.BlockSpec((tr, w), lambda i: (i, 0)))
            out_shape.append(jax.ShapeDtypeStruct((T, w), BF16 if k in narrow else F32))
    return pl.pallas_call(body, out_shape=out_shape, grid=(T // tr,), in_specs=in_specs, out_specs=out_specs,
                          name=name, compiler_params=_cparams(("arbitrary",)))(*args)


def matmul(name, a, b, mode, add=None, out_dtype=F32):
    if mode == "tn":
        K, M = a.shape
        N = b.shape[1]
        tm = M if M <= 1536 else 512
        tn = N if N <= 1280 else (N // 2 if (N // 2) % 128 == 0 else N)
        tk = min(K, 512)
        grid = (M // tm, N // tn, K // tk)

        def body(a_ref, b_ref, o_ref):
            @pl.when(pl.program_id(2) == 0)
            def _():
                o_ref[...] = jnp.zeros_like(o_ref)
            o_ref[...] += dot_tn(a_ref[...], b_ref[...])

        return pl.pallas_call(
            body, out_shape=jax.ShapeDtypeStruct((M, N), F32), grid=grid,
            in_specs=[pl.BlockSpec((tk, tm), lambda i, j, k: (k, i)), pl.BlockSpec((tk, tn), lambda i, j, k: (k, j))],
            out_specs=pl.BlockSpec((tm, tn), lambda i, j, k: (i, j)), name=name,
            compiler_params=_cparams(("arbitrary", "arbitrary", "arbitrary")))(a, b)

    M, K = a.shape
    N = b.shape[1] if mode == "nn" else b.shape[0]
    tm = min(M, 512)
    tn = N if N <= 1536 else (N // 2 if (N // 2) % 128 == 0 else (N // 3 if (N // 3) % 128 == 0 else N))
    grid = (N // tn, M // tm)
    n_in = 2 + (add is not None)

    def body(*refs):
        a_ref, b_ref = refs[0], refs[1]
        o_ref = refs[n_in]
        acc = dot_nn(a_ref[...], b_ref[...]) if mode == "nn" else dot_nt(a_ref[...], b_ref[...])
        if add is not None:
            acc = acc + refs[2][...]
        o_ref[...] = acc.astype(o_ref.dtype)

    in_specs = [pl.BlockSpec((tm, K), lambda j, i: (i, 0)),
                pl.BlockSpec((K, tn), lambda j, i: (0, j)) if mode == "nn" else pl.BlockSpec((tn, K), lambda j, i: (j, 0))]
    args = [a, b]
    if add is not None:
        in_specs.append(pl.BlockSpec((tm, tn), lambda j, i: (i, j)))
        args.append(add)
    return pl.pallas_call(
        body, out_shape=jax.ShapeDtypeStruct((M, N), out_dtype), grid=grid, in_specs=in_specs,
        out_specs=pl.BlockSpec((tm, tn), lambda j, i: (i, j)), name=name,
        compiler_params=_cparams(("arbitrary", "arbitrary")))(*args)


def norm_project(name, x, g, w):
    T, D = x.shape
    N = w.shape[1]
    tm = min(T, 512)

    def body(x_ref, g_ref, w_ref, h_ref, p_ref):
        h = _rms(x_ref[...], g_ref[...]).astype(BF16)
        h_ref[...] = h
        p_ref[...] = dot_nn(h, w_ref[...])

    return pl.pallas_call(
        body, out_shape=[jax.ShapeDtypeStruct((T, D), BF16), jax.ShapeDtypeStruct((T, N), F32)], grid=(T // tm,),
        in_specs=[pl.BlockSpec((tm, D), lambda i: (i, 0)), pl.BlockSpec((1, D), lambda i: (0, 0)),
                  pl.BlockSpec((D, N), lambda i: (0, 0))],
        out_specs=[pl.BlockSpec((tm, D), lambda i: (i, 0)), pl.BlockSpec((tm, N), lambda i: (i, 0))],
        name=name, compiler_params=_cparams(("arbitrary",)))(x, g, w)


def norm_project_bwd(name, assemble, pieces, w, x, g, dy):
    T, D = x.shape
    N = w.shape[1]
    tm = min(T, 256)
    n_in = len(pieces)

    def body(*refs):
        w_ref, x_ref, g_ref, dy_ref, dp_ref, dx_ref, dg_ref = refs[n_in:]

        @pl.when(pl.program_id(0) == 0)
        def _():
            dg_ref[...] = jnp.zeros_like(dg_ref)

        (dp,) = assemble(*[r[...] for r in refs[:n_in]])
        dp_ref[...] = dp
        _, vjp = jax.vjp(_rms, x_ref[...], g_ref[...])
        dx, dg = vjp(dot_nt(dp, w_ref[...]))
        dx_ref[...] = dx + dy_ref[...]
        dg_ref[...] += dg

    in_specs, args = [], []
    for spec in pieces:
        arr, bs = _in_spec(spec, tm)
        args.append(arr)
        in_specs.append(bs)
    row = pl.BlockSpec((tm, D), lambda i: (i, 0))
    vec = pl.BlockSpec((1, D), lambda i: (0, 0))
    wide = pl.BlockSpec((tm, N), lambda i: (i, 0))
    return pl.pallas_call(
        body,
        out_shape=[jax.ShapeDtypeStruct((T, N), BF16), jax.ShapeDtypeStruct((T, D), F32), jax.ShapeDtypeStruct((1, D), F32)],
        grid=(T // tm,), in_specs=in_specs + [pl.BlockSpec((D, N), lambda i: (0, 0)), row, vec, row],
        out_specs=[wide, row, vec], name=name, compiler_params=_cparams(("arbitrary",)))(*args, w, x, g, dy)


def mix_project(name, tile_fn, T, ins, w, x):
    tr = _row_tile(T)
    n_in = len(ins)
    K, D = w.shape

    def body(*refs):
        w_ref, x_ref, y_ref = refs[n_in:]
        (mix,) = tile_fn(*[r[...] for r in refs[:n_in]])
        y_ref[...] = x_ref[...] + dot_nn(mix, w_ref[...])

    in_specs, args = [], []
    for spec in ins:
        arr, bs = _in_spec(spec, tr)
        args.append(arr)
        in_specs.append(bs)
    row = pl.BlockSpec((tr, D), lambda i: (i, 0))
    return pl.pallas_call(
        body, out_shape=jax.ShapeDtypeStruct((T, D), F32), grid=(T // tr,),
        in_specs=in_specs + [pl.BlockSpec((K, D), lambda i: (0, 0)), row], out_specs=row,
        name=name, compiler_params=_cparams(("arbitrary",)))(*args, w, x)


def mix_project_bwd(name, tile_fn, T, ins, w, dy, skip=(), narrow=()):
    tr = _row_tile(T)
    n_in = len(ins)
    K, D = w.shape
    want = [k for k in range(n_in) if k not in skip]

    def body(*refs):
        w_ref, dy_ref = refs[n_in:n_in + 2]
        outs, dw_ref = refs[n_in + 2:-1], refs[-1]
        first = pl.program_id(0) == 0
        (mix,), vjp = jax.vjp(tile_fn, *[r[...] for r in refs[:n_in]])
        d = dy_ref[...].astype(BF16)
        grads = vjp((dot_nt(d, w_ref[...]),))

        @pl.when(first)
        def _():
            dw_ref[...] = jnp.zeros_like(dw_ref)

        dw_ref[...] += dot_tn(mix, d)
        for r, k in zip(outs, want):
            if ins[k][0] == "full":
                @pl.when(first)
                def _():
                    r[...] = jnp.zeros_like(r)
                r[...] += grads[k]
            else:
                r[...] = grads[k].astype(r.dtype)

    in_specs, args = [], []
    for spec in ins:
        arr, bs = _in_spec(spec, tr)
        args.append(arr)
        in_specs.append(bs)
    out_specs, out_shape = [], []
    for k in want:
        if ins[k][0] == "full":
            arr = ins[k][1]
            out_specs.append(_full_spec(arr))
            out_shape.append(jax.ShapeDtypeStruct(arr.shape, F32))
        else:
            wd = ins[k][-1]
            out_specs.append(pl.BlockSpec((tr, wd), lambda i: (i, 0)))
            out_shape.append(jax.ShapeDtypeStruct((T, wd), BF16 if k in narrow else F32))
    wspec = pl.BlockSpec((K, D), lambda i: (0, 0))
    return pl.pallas_call(
        body, out_shape=out_shape + [jax.ShapeDtypeStruct((K, D), F32)], grid=(T // tr,),
        in_specs=in_specs + [wspec, pl.BlockSpec((tr, D), lambda i: (i, 0))], out_specs=out_specs + [wspec],
        name=name, compiler_params=_cparams(("arbitrary",)))(*args, w, dy)


def _attn_heads(n):
    G = N_Q_A // N_KV_A
    k_sl = pl.ds(n * HEAD_DIM_A, HEAD_DIM_A)
    v_sl = pl.ds(W_KV_A + n * HEAD_DIM_A, HEAD_DIM_A)
    q_sl = [pl.ds((n * G + g) * HEAD_DIM_A, HEAD_DIM_A) for g in range(G)]
    return k_sl, v_sl, q_sl, range(n * G, (n + 1) * G)


def attn_fwd(p, q_off, kvp, sink, slopes, T):
    nb = T // BLOCK
    assert q_off % W_A == 0

    def body(q_ref, kv_ref, sink_ref, slope_ref, o_ref):
        c = pl.program_id(0)
        rows = pl.ds(pl.multiple_of(c * BLOCK, BLOCK), 3 * BLOCK)
        for n in range(N_KV_A):
            k_sl, v_sl, q_sl, heads = _attn_heads(n)
            ks, vs = kv_ref[rows, k_sl], kv_ref[rows, v_sl]
            for s, h in zip(q_sl, heads):
                mine = pl.ds(h * BLOCK, BLOCK)
                o_ref[:, s] = attn_block(q_ref[:, s], ks, vs, sink_ref[mine, :], slope_ref[mine, :], c, T)

    full = lambda a: pl.BlockSpec(a.shape, functools.partial(lambda c, nd: (0,) * nd, nd=a.ndim))
    return pl.pallas_call(
        body, out_shape=jax.ShapeDtypeStruct((T, W_A), F32), grid=(nb,),
        in_specs=[pl.BlockSpec((BLOCK, W_A), lambda c: (c, q_off // W_A)), full(kvp), full(sink), full(slopes)],
        out_specs=pl.BlockSpec((BLOCK, W_A), lambda c: (c, 0)),
        name="attn_fwd", compiler_params=_cparams(("arbitrary",)))(p, kvp, sink, slopes)


def attn_bwd(p, q_off, kvp, sink, slopes, do, T):
    nb = T // BLOCK

    def body(q_ref, kv_ref, sink_ref, slope_ref, do_ref, dq_ref, dkv_ref, dsink_ref):
        c = pl.program_id(0)

        @pl.when(c == 0)
        def _():
            dkv_ref[...] = jnp.zeros_like(dkv_ref)
            dsink_ref[...] = jnp.zeros_like(dsink_ref)

        rows = pl.ds(pl.multiple_of(c * BLOCK, BLOCK), 3 * BLOCK)
        for n in range(N_KV_A):
            k_sl, v_sl, q_sl, heads = _attn_heads(n)
            group = pl.ds(n * ATTN_GROUP * BLOCK, ATTN_GROUP * BLOCK)
            slope = slope_ref[group, :]
            q = jnp.concatenate([q_ref[:, s] for s in q_sl], axis=0)
            do = jnp.concatenate([do_ref[:, s] for s in q_sl], axis=0)
            _, vjp = jax.vjp(lambda q_, kk, vv, sk: attn_block(q_, kk, vv, sk, slope, c, T),
                             q, kv_ref[rows, k_sl], kv_ref[rows, v_sl], sink_ref[group, :])
            dq, dks, dvs, dsk = vjp(do)
            dkv_ref[rows, k_sl] += dks
            dkv_ref[rows, v_sl] += dvs
            for g, (s, h) in enumerate(zip(q_sl, heads)):
                seg = slice(g * BLOCK, (g + 1) * BLOCK)
                dq_ref[:, s] = dq[seg].astype(dq_ref.dtype)
                dsink_ref[h] += jnp.sum(dsk[seg], axis=0, keepdims=True)

    full = lambda a: pl.BlockSpec(a.shape, functools.partial(lambda c, nd: (0,) * nd, nd=a.ndim))
    qspec = pl.BlockSpec((BLOCK, W_A), lambda c: (c, 0))
    return pl.pallas_call(
        body,
        out_shape=[jax.ShapeDtypeStruct((T, W_A), BF16), jax.ShapeDtypeStruct(kvp.shape, F32),
                   jax.ShapeDtypeStruct((N_Q_A, 1, 1), F32)],
        grid=(nb,),
        in_specs=[pl.BlockSpec((BLOCK, W_A), lambda c: (c, q_off // W_A)), full(kvp), full(sink), full(slopes), qspec],
        out_specs=[qspec, full(kvp), pl.BlockSpec((N_Q_A, 1, 1), lambda c: (0, 0, 0))],
        name="attn_bwd", compiler_params=_cparams(("arbitrary",)))(p, kvp, sink, slopes, do)


def mem_fwd(p, q_off, kv, T):
    tr = min(T, 2 * _row_tile(T))
    assert q_off % W_M == 0

    def body(q_ref, kv_ref, o_ref):
        for h in range(N_HEADS_M):
            hs = pl.ds(h * HEAD_DIM_M, HEAD_DIM_M)
            (o,) = mem_tile(q_ref[:, hs], kv_ref[:, hs], kv_ref[:, pl.ds(W_M + h * HEAD_DIM_M, HEAD_DIM_M)])
            o_ref[:, hs] = o

    return pl.pallas_call(
        body, out_shape=jax.ShapeDtypeStruct((T, W_M), F32), grid=(T // tr,),
        in_specs=[pl.BlockSpec((tr, W_M), lambda i: (i, q_off // W_M)), pl.BlockSpec((N_MEM, 2 * W_M), lambda i: (0, 0))],
        out_specs=pl.BlockSpec((tr, W_M), lambda i: (i, 0)),
        name="mem_fwd", compiler_params=_cparams(("arbitrary",)))(p, kv)


def mem_bwd(p, q_off, kv, do, T):
    tr = min(T, 2 * _row_tile(T))

    def body(q_ref, kv_ref, do_ref, dq_ref, dkv_ref):
        @pl.when(pl.program_id(0) == 0)
        def _():
            dkv_ref[...] = jnp.zeros_like(dkv_ref)

        for h in range(N_HEADS_M):
            hs = pl.ds(h * HEAD_DIM_M, HEAD_DIM_M)
            vs = pl.ds(W_M + h * HEAD_DIM_M, HEAD_DIM_M)
            _, vjp = jax.vjp(mem_tile, q_ref[:, hs], kv_ref[:, hs], kv_ref[:, vs])
            dq, dk, dv = vjp((do_ref[:, hs],))
            dq_ref[:, hs] = dq.astype(dq_ref.dtype)
            dkv_ref[:, hs] += dk
            dkv_ref[:, vs] += dv

    kvspec = pl.BlockSpec((N_MEM, 2 * W_M), lambda i: (0, 0))
    return pl.pallas_call(
        body,
        out_shape=[jax.ShapeDtypeStruct((T, W_M), BF16), jax.ShapeDtypeStruct((N_MEM, 2 * W_M), F32)],
        grid=(T // tr,),
        in_specs=[pl.BlockSpec((tr, W_M), lambda i: (i, q_off // W_M)), kvspec, pl.BlockSpec((tr, W_M), lambda i: (i, 0))],
        out_specs=[pl.BlockSpec((tr, W_M), lambda i: (i, 0)), kvspec],
        name="mem_bwd", compiler_params=_cparams(("arbitrary",)))(p, kv, do)


def _scan_const_specs(dk):
    C, L = SCAN_CHUNK, SCAN_LEVELS
    return [pl.BlockSpec((2, (2 + L) * C, C), lambda n: (0, 0, 0)),
            pl.BlockSpec((2, C, (2 + L) * C), lambda n: (0, 0, 0)),
            pl.BlockSpec((2, L * C, dk), lambda n: (0, 0, 0)),
            pl.BlockSpec((2, L * C, C), lambda n: (0, 0, 0))]


def _chunk_spec(src, width, chunk_of):
    arr, sel = src
    if arr.ndim == 2:
        assert sel % width == 0
        return pl.BlockSpec((SCAN_CHUNK * SCAN_SUB, width),
                            functools.partial(lambda n, b: (chunk_of(n), b), b=sel // width))
    return pl.BlockSpec((None, SCAN_CHUNK * SCAN_SUB, width), functools.partial(lambda n, d: (d, chunk_of(n), 0), d=sel))


def _scan_const_args():
    h, ht, qm, pm = _scan_consts()
    return [jnp.asarray(h, BF16), jnp.asarray(ht, BF16), jnp.asarray(qm, F32), jnp.asarray(pm, F32)]


def _full_spec(a):
    return pl.BlockSpec(a.shape, functools.partial(lambda n, nd: (0,) * nd, nd=a.ndim))


def scan_fwd(name, prep, raws, params, heads, dk, dv, T):
    C, S = SCAN_CHUNK, SCAN_SUB
    N = T // (C * S)
    assert dk == C
    Wv = heads * dv
    orders = (lambda n: n, lambda n: N - 1 - n)
    n_raw, n_par = [len(r) for r in raws], [len(p) for p in params]

    def body(*refs):
        pos, raw_refs, par_refs = 0, [], []
        for d in range(2):
            raw_refs.append(refs[pos:pos + n_raw[d]])
            pos += n_raw[d]
        for d in range(2):
            par_refs.append(refs[pos:pos + n_par[d]])
            pos += n_par[d]
        h_ref, ht_ref, qm_ref, pm_ref = refs[pos:pos + 4]
        o_refs, ss_refs, st_ref = refs[pos + 4:pos + 6], refs[pos + 6:pos + 8], refs[pos + 8]

        @pl.when(pl.program_id(0) == 0)
        def _():
            st_ref[...] = jnp.zeros_like(st_ref)

        for d in range(2):
            consts = (qm_ref[d], pm_ref[d])
            pars = [p[...] for p in par_refs[d]]
            for sub in (range(S) if d == 0 else reversed(range(S))):
                rows = pl.ds(sub * C, C)
                q, k, v, g = prep([r[rows, :] for r in raw_refs[d]], pars)
                e = _split_mm(h_ref[d], g)
                tot = jnp.sum(g, axis=0, keepdims=True)
                for h in range(heads):
                    ks, vs = slice(h * dk, (h + 1) * dk), slice(h * dv, (h + 1) * dv)
                    st = st_ref[d, h]
                    ss_refs[d][h, sub] = st
                    o, st_new = scan_chunk(q[:, ks], k[:, ks], v[:, vs], e[:, ks], tot[:, ks], st, *consts)
                    o_refs[d][rows, vs] = o
                    st_ref[d, h] = st_new

    ss_spec = lambda order: pl.BlockSpec((heads, S, dv, dk), lambda n: (0, order(n), 0, 0))
    return pl.pallas_call(
        body,
        out_shape=[jax.ShapeDtypeStruct((T, Wv), F32)] * 2 + [jax.ShapeDtypeStruct((heads, T // C, dv, dk), F32)] * 2,
        grid=(N,),
        in_specs=[_chunk_spec(s, w, orders[d]) for d in range(2) for s, w in raws[d]]
        + [_full_spec(p) for d in range(2) for p in params[d]] + _scan_const_specs(dk),
        out_specs=[pl.BlockSpec((C * S, Wv), lambda n: (orders[0](n), 0)),
                   pl.BlockSpec((C * S, Wv), lambda n: (orders[1](n), 0)), ss_spec(orders[0]), ss_spec(orders[1])],
        scratch_shapes=[pltpu.VMEM((2, heads, dv, dk), F32)],
        name=name, compiler_params=_cparams(("arbitrary",)))(
            *[s[0] for d in range(2) for s, _ in raws[d]], *[p for d in range(2) for p in params[d]], *_scan_const_args())


def scan_bwd(name, prep, raws, params, ss, do, heads, dk, dv, T):
    C, S = SCAN_CHUNK, SCAN_SUB
    N = T // (C * S)
    Wv = heads * dv
    orders = (lambda n: N - 1 - n, lambda n: n)
    n_raw, n_par = [len(r) for r in raws], [len(p) for p in params]

    def body(*refs):
        pos, raw_refs, par_refs, draw_refs, dpar_refs = 0, [], [], [], []
        for group, counts in ((raw_refs, n_raw), (par_refs, n_par)):
            for d in range(2):
                group.append(refs[pos:pos + counts[d]])
                pos += counts[d]
        ss_refs, do_refs = refs[pos:pos + 2], refs[pos + 2:pos + 4]
        h_ref, ht_ref, qm_ref, pm_ref = refs[pos + 4:pos + 8]
        pos += 8
        for group, counts in ((draw_refs, n_raw), (dpar_refs, n_par)):
            for d in range(2):
                group.append(refs[pos:pos + counts[d]])
                pos += counts[d]
        dst_ref = refs[pos]

        @pl.when(pl.program_id(0) == 0)
        def _():
            dst_ref[...] = jnp.zeros_like(dst_ref)
            for d in range(2):
                for r in dpar_refs[d]:
                    r[...] = jnp.zeros_like(r)

        for d in range(2):
            consts = (qm_ref[d], pm_ref[d])
            pars = [p[...] for p in par_refs[d]]
            for sub in (reversed(range(S)) if d == 0 else range(S)):
                rows = pl.ds(sub * C, C)
                (q, k, v, g), prep_vjp = jax.vjp(prep, [r[rows, :] for r in raw_refs[d]], pars)
                e = _split_mm(h_ref[d], g)
                tot = jnp.sum(g, axis=0, keepdims=True)
                dqs, dks, dvs, des, dtots = [], [], [], [], []
                for h in range(heads):
                    ks, vs = slice(h * dk, (h + 1) * dk), slice(h * dv, (h + 1) * dv)
                    _, vjp = jax.vjp(lambda q_, k_, v_, e_, t_, st_: scan_chunk(q_, k_, v_, e_, t_, st_, *consts),
                                     q[:, ks], k[:, ks], v[:, vs], e[:, ks], tot[:, ks], ss_refs[d][h, sub])
                    dq, dk_, dv_, de, dtot, dst = vjp((do_refs[d][rows, vs], dst_ref[d, h]))
                    dst_ref[d, h] = dst
                    for group, val in ((dqs, dq), (dks, dk_), (dvs, dv_), (des, de), (dtots, dtot)):
                        group.append(val)
                cat = lambda parts: jnp.concatenate(parts, axis=-1)
                dg = _split_mm(ht_ref[d], cat(des)) + cat(dtots)
                draws, dpars = prep_vjp((cat(dqs), cat(dks), cat(dvs), dg))
                for r, val in zip(draw_refs[d], draws):
                    r[rows, :] = val.astype(r.dtype)
                for r, val in zip(dpar_refs[d], dpars):
                    r[...] += val

    ss_spec = lambda order: pl.BlockSpec((heads, S, dv, dk), lambda n: (0, order(n), 0, 0))
    row_out = lambda w, order: pl.BlockSpec((C * S, w), lambda n: (order(n), 0))
    return pl.pallas_call(
        body,
        out_shape=[jax.ShapeDtypeStruct((T, w), BF16) for d in range(2) for _, w in raws[d]]
        + [jax.ShapeDtypeStruct(p.shape, F32) for d in range(2) for p in params[d]],
        grid=(N,),
        in_specs=[_chunk_spec(s, w, orders[d]) for d in range(2) for s, w in raws[d]]
        + [_full_spec(p) for d in range(2) for p in params[d]]
        + [ss_spec(orders[0]), ss_spec(orders[1]), _chunk_spec(do, Wv, orders[0]), _chunk_spec(do, Wv, orders[1])]
        + _scan_const_specs(dk),
        out_specs=[row_out(w, orders[d]) for d in range(2) for _, w in raws[d]]
        + [_full_spec(p) for d in range(2) for p in params[d]],
        scratch_shapes=[pltpu.VMEM((2, heads, dv, dk), F32)],
        name=name, compiler_params=_cparams(("arbitrary",)))(
            *[s[0] for d in range(2) for s, _ in raws[d]], *[p for d in range(2) for p in params[d]],
            ss[0], ss[1], do[0], do[0], *_scan_const_args())


def final_call(x, g, target, T):
    tr = _row_tile(T)

    def tile(xv, gv, tv):
        y = _rms(xv, gv)
        err = (y - tv) ** 2
        return jnp.sum(jnp.sum(err, axis=-1, keepdims=True), axis=0, keepdims=True) * (0.5 / D_MODEL)

    def body(x_ref, g_ref, t_ref, loss_ref, dx_ref, dg_ref):
        i = pl.program_id(0)
        tv = t_ref[...]
        lv, vjp = jax.vjp(lambda a, b: tile(a, b, tv), x_ref[...], g_ref[...])
        dx, dg = vjp(jnp.ones((1, 1), F32))
        dx_ref[...] = dx

        @pl.when(i == 0)
        def _():
            loss_ref[...] = jnp.zeros_like(loss_ref)
            dg_ref[...] = jnp.zeros_like(dg_ref)

        loss_ref[...] += jnp.broadcast_to(lv, loss_ref.shape)
        dg_ref[...] += dg

    return pl.pallas_call(
        body,
        out_shape=[jax.ShapeDtypeStruct((8, 128), F32), jax.ShapeDtypeStruct((T, D_MODEL), F32),
                   jax.ShapeDtypeStruct((1, D_MODEL), F32)],
        grid=(T // tr,),
        in_specs=[pl.BlockSpec((tr, D_MODEL), lambda i: (i, 0)), pl.BlockSpec((1, D_MODEL), lambda i: (0, 0)),
                  pl.BlockSpec((tr, D_MODEL), lambda i: (i, 0))],
        out_specs=[pl.BlockSpec((8, 128), lambda i: (0, 0)), pl.BlockSpec((tr, D_MODEL), lambda i: (i, 0)),
                   pl.BlockSpec((1, D_MODEL), lambda i: (0, 0))],
        name="final_loss", compiler_params=_cparams(("arbitrary",)))(x, g, target)


def adamw_call(w, g, m, v):
    shape = w.shape
    c = shape[-1]
    r = int(np.prod(shape[:-1])) if len(shape) > 1 else 1
    tr = r if r <= 256 else 256
    assert r % tr == 0

    def body(w_ref, g_ref, m_ref, v_ref, d_ref, nm_ref, nv_ref):
        gv = g_ref[...]
        nm = ADAM_B1 * m_ref[...] + (1.0 - ADAM_B1) * gv
        nv = ADAM_B2 * v_ref[...] + (1.0 - ADAM_B2) * jnp.square(gv)
        m_hat = nm / (1.0 - ADAM_B1 ** ADAM_STEP)
        v_hat = nv / (1.0 - ADAM_B2 ** ADAM_STEP)
        d_ref[...] = -ADAM_LR * (m_hat / (jnp.sqrt(v_hat) + ADAM_EPS) + ADAM_WD * w_ref[...])
        nm_ref[...] = nm
        nv_ref[...] = nv

    spec = pl.BlockSpec((tr, c), lambda i: (i, 0))
    outs = pl.pallas_call(body, out_shape=[jax.ShapeDtypeStruct((r, c), F32)] * 3, grid=(r // tr,),
                          in_specs=[spec] * 4, out_specs=[spec] * 3, name="adamw",
                          compiler_params=_cparams(("arbitrary",)))(*(t.reshape(r, c) for t in (w, g, m, v)))
    return tuple(o.reshape(shape) for o in outs)


def adamw_halves(w, mine, other, m, v, c):
    L, R, C = w.shape
    by_cols = mine.shape[-1] != C
    if by_cols:
        tile, nbh = (R, C // 2), 1
        full_idx = lambda l, i: (l, 0, i)
    else:
        rh = R // 2
        tr = rh if rh <= 256 else rh // 2
        assert tr % 8 == 0
        tile, nbh = (tr, C), rh // tr
        full_idx = lambda l, i: (l, i, 0)

    def body(c_ref, w_ref, a_ref, b_ref, m_ref, v_ref, g_ref, d_ref, nm_ref, nv_ref):
        is_mine = (pl.program_id(1) // nbh) == c_ref[0]
        gv = jnp.where(is_mine, a_ref[...], b_ref[...])
        nm = ADAM_B1 * m_ref[...] + (1.0 - ADAM_B1) * gv
        nv = ADAM_B2 * v_ref[...] + (1.0 - ADAM_B2) * jnp.square(gv)
        m_hat = nm / (1.0 - ADAM_B1 ** ADAM_STEP)
        v_hat = nv / (1.0 - ADAM_B2 ** ADAM_STEP)
        g_ref[...] = gv
        d_ref[...] = -ADAM_LR * (m_hat / (jnp.sqrt(v_hat) + ADAM_EPS) + ADAM_WD * w_ref[...])
        nm_ref[...] = nm
        nv_ref[...] = nv

    full = pl.BlockSpec((None,) + tile, lambda l, i, c_ref: full_idx(l, i))
    half = pl.BlockSpec((None,) + tile, lambda l, i, c_ref: (l, i % nbh, 0))
    grid_spec = pltpu.PrefetchScalarGridSpec(num_scalar_prefetch=1, grid=(L, 2 * nbh),
                                             in_specs=[full, half, half, full, full], out_specs=[full] * 4)
    return pl.pallas_call(body, out_shape=[jax.ShapeDtypeStruct(w.shape, F32)] * 4, grid_spec=grid_spec,
                          name="adamw_halves", compiler_params=_cparams(("arbitrary", "arbitrary")))(c, w, mine, other, m, v)


def sum_devices(g64):
    def body(x_ref, o_ref):
        acc = x_ref[0:8, :]
        for d in range(1, 8):
            acc = acc + x_ref[8 * d:8 * d + 8, :]
        o_ref[...] = acc

    return pl.pallas_call(body, out_shape=jax.ShapeDtypeStruct((8, D_MODEL), F32), name="sum_devices")(g64)


def _half_tile(rh):
    if rh <= 512:
        return rh
    return next(rh // d for d in range(2, rh) if rh % d == 0 and (rh // d) % 16 == 0 and rh // d <= 512)


def _half_geometry(full_shape, half_shape):
    R, C = full_shape[-2:]
    if half_shape[-1] != C:
        return (R, C // 2), 1, lambda i, c: (0, c)
    tr = _half_tile(R // 2)
    nblk = (R // 2) // tr
    return (tr, C), nblk, lambda i, c: (i + c * nblk, 0)


def _work_items(counts):
    starts = [int(v) for v in np.cumsum([0] + list(counts[:-1]))]
    local = lambda a, s: jnp.clip(s - starts[a], 0, counts[a] - 1)
    return starts, int(sum(counts)), local


def add_sibling(gs, recvs, c, out_dtypes):
    n = len(gs)
    geo = [_half_geometry(g.shape, r.shape) for g, r in zip(gs, recvs)]
    counts = [4 * nblk for _, nblk, _ in geo]
    starts, total, local = _work_items(counts)

    def body(c_ref, *refs):
        s = pl.program_id(0)
        for a in range(n):
            g_ref, r_ref, o_ref = refs[a], refs[n + a], refs[2 * n + a]

            @pl.when((s >= starts[a]) & (s < starts[a] + counts[a]))
            def _():
                o_ref[...] = (g_ref[...] + r_ref[...]).astype(o_ref.dtype)

    def own_idx(s, c_ref, a):
        _, nblk, own = geo[a]
        k = local(a, s)
        return (k // nblk,) + own(k % nblk, c_ref[0])

    def half_idx(s, c_ref, a):
        k = local(a, s)
        return (k // geo[a][1], k % geo[a][1], 0)

    halves = [pl.BlockSpec((None,) + geo[a][0], functools.partial(half_idx, a=a)) for a in range(n)]
    grid_spec = pltpu.PrefetchScalarGridSpec(
        num_scalar_prefetch=1, grid=(total,),
        in_specs=[pl.BlockSpec((None,) + geo[a][0], functools.partial(own_idx, a=a)) for a in range(n)] + halves,
        out_specs=halves)
    return pl.pallas_call(body, out_shape=[jax.ShapeDtypeStruct(r.shape, dt) for r, dt in zip(recvs, out_dtypes)],
                          grid_spec=grid_spec, name="rs_add_sibling",
                          compiler_params=_cparams(("arbitrary",)))(c, *gs, *recvs)


def add_chips(gs, recvs, r3s, place):
    n = len(gs)
    geo = [_half_geometry(g.shape, r.shape) for g, r in zip(gs, recvs)]
    counts = [nblk for _, nblk, _ in geo]
    starts, total, local = _work_items(counts)

    def body(p_ref, *refs):
        s = pl.program_id(0)
        up = lambda r: r[...].astype(F32)
        for a in range(n):
            g_ref, s_ref, o_ref = refs[a], refs[n + a], refs[5 * n + a]
            a_ref, b_ref, c_ref = refs[2 * n + 3 * a:2 * n + 3 * a + 3]

            @pl.when((s >= starts[a]) & (s < starts[a] + counts[a]))
            def _():
                o_ref[...] = (((g_ref[...] + up(s_ref)) + up(a_ref)) + up(b_ref)) + up(c_ref)

    own_idx = lambda s, p_ref, a: (p_ref[0],) + geo[a][2](local(a, s), p_ref[1])
    sib_idx = lambda s, p_ref, a: (p_ref[0], local(a, s), 0)
    chip_idx = lambda s, p_ref, a, k: (k, local(a, s), 0)
    spec = lambda a, idx, **kw: pl.BlockSpec((None,) + geo[a][0], functools.partial(idx, a=a, **kw))
    grid_spec = pltpu.PrefetchScalarGridSpec(
        num_scalar_prefetch=1, grid=(total,),
        in_specs=[spec(a, own_idx) for a in range(n)] + [spec(a, sib_idx) for a in range(n)]
        + [spec(a, chip_idx, k=k) for a in range(n) for k in range(3)],
        out_specs=[pl.BlockSpec(geo[a][0], functools.partial(lambda s, p_ref, a: (local(a, s), 0), a=a))
                   for a in range(n)])
    return pl.pallas_call(body, out_shape=[jax.ShapeDtypeStruct(r.shape[1:], F32) for r in recvs],
                          grid_spec=grid_spec, name="rs_add_chips", compiler_params=_cparams(("arbitrary",)))(
                              place, *gs, *recvs, *[r for r3 in r3s for r in (r3, r3, r3)])


def _remote(src, dst, ssem, rsem, dev):
    return pltpu.make_async_remote_copy(src_ref=src, dst_ref=dst, send_sem=ssem, recv_sem=rsem,
                                        device_id=dev, device_id_type=pl.DeviceIdType.MESH)


def _mesh_places():
    x, y, c = lax.axis_index("x"), lax.axis_index("y"), lax.axis_index("c")
    chips = [(1 - x, y), (x, 1 - y), (1 - x, 1 - y)]
    return x, y, c, (x, y, 1 - c), chips


def _hbm_specs(n):
    return [pl.BlockSpec(memory_space=pltpu.HBM) for _ in range(n)]


def _gather_body(ins, outs, n_split, send_sems, recv_sems, handshake):
    x, y, c, sibling, chips = _mesh_places()
    mine = 2 * x + y
    if handshake:
        barrier = pltpu.get_barrier_semaphore()
        peers = [sibling] + [(*chip, c) for chip in chips]
        for peer in peers:
            pl.semaphore_signal(barrier, inc=1, device_id=peer, device_id_type=pl.DeviceIdType.MESH)
        pl.semaphore_wait(barrier, len(peers))

    def half(a, chip_idx, which):
        rh = ins[a].shape[0] // 2
        return outs[a].at[chip_idx, pl.ds(which * rh, rh), :]

    sent = []
    for a in range(len(ins)):
        for k, chip in enumerate(chips):
            if a < n_split:
                rh = ins[a].shape[0] // 2
                src, dst = ins[a].at[pl.ds(c * rh, rh), :], half(a, mine, c)
            else:
                src, dst = ins[a], outs[a].at[mine]
            sent.append(_remote(src, dst, send_sems.at[a, k], recv_sems.at[a, k], (*chip, c)))
    for cp in sent:
        cp.start()
    for a in range(len(ins)):
        for k, chip in enumerate(chips):
            j = 2 * chip[0] + chip[1]
            region = half(a, j, c) if a < n_split else outs[a].at[j]
            _remote(region, region, send_sems.at[a, k], recv_sems.at[a, k], (*chip, c)).wait_recv()
            if a < n_split:
                fwd = _remote(region, region, send_sems.at[a, 3 + k], recv_sems.at[a, 3 + k], sibling)
                fwd.start()
                sent.append(fwd)
    for a in range(n_split):
        for k, chip in enumerate(chips):
            region = half(a, 2 * chip[0] + chip[1], 1 - c)
            _remote(region, region, send_sems.at[a, 3 + k], recv_sems.at[a, 3 + k], sibling).wait_recv()
    for cp in sent:
        cp.wait_send()


def gather_weights(shards, small):
    arrs = list(shards) + [small]
    n = len(arrs)

    def body(*refs):
        _gather_body(refs[:n], refs[n:2 * n], n - 1, refs[2 * n], refs[2 * n + 1], handshake=False)

    return pl.pallas_call(
        body, out_shape=[jax.ShapeDtypeStruct((4,) + a.shape, a.dtype) for a in arrs],
        in_specs=_hbm_specs(n), out_specs=_hbm_specs(n),
        scratch_shapes=[pltpu.SemaphoreType.DMA((n, 6)), pltpu.SemaphoreType.DMA((n, 6))],
        name="gather_weights")(*arrs)


def gather_weights_async(shards):
    n = len(shards)

    def body(*refs):
        _gather_body(refs[:n], refs[n:2 * n], n, refs[2 * n], refs[2 * n + 1], handshake=True)

    return pl.kernel(
        body, out_type=[jax.ShapeDtypeStruct((4,) + a.shape, a.dtype) for a in shards],
        mesh=plsc.ScalarSubcoreMesh(axis_name="seq", num_cores=1),
        scratch_types=[pltpu.SemaphoreType.DMA((n, 6)), pltpu.SemaphoreType.DMA((n, 6))],
        compiler_params=pltpu.CompilerParams(collective_id=1), name="gather_weights_async")(*shards)


def _sequencer_call(name, body, out_type, sem_shape, collective_id, args):
    return pl.kernel(
        body, out_type=out_type, mesh=plsc.ScalarSubcoreMesh(axis_name="seq", num_cores=1),
        scratch_types=[pltpu.SemaphoreType.DMA(sem_shape), pltpu.SemaphoreType.DMA(sem_shape)],
        compiler_params=pltpu.CompilerParams(collective_id=collective_id), name=name)(*args)


def _handshake(peers):
    barrier = pltpu.get_barrier_semaphore()
    for peer in peers:
        pl.semaphore_signal(barrier, inc=1, device_id=peer, device_id_type=pl.DeviceIdType.MESH)
    pl.semaphore_wait(barrier, len(peers))


def exchange_siblings(name, srcs, axes, collective_id):
    n = len(srcs)

    def body(*refs):
        ins, outs = refs[:n], refs[n:2 * n]
        send_sems, recv_sems = refs[2 * n:]
        x, y, c, sibling, chips = _mesh_places()
        _handshake([sibling])
        cps = []
        for a in range(n):
            src = ins[a]
            if axes[a] is not None:
                half = src.shape[axes[a]] // 2
                theirs = pl.ds((1 - c) * half, half)
                src = src.at[:, theirs, :] if axes[a] == 1 else src.at[:, :, theirs]
            cps.append(_remote(src, outs[a], send_sems.at[a], recv_sems.at[a], sibling))
        for cp in cps:
            cp.start()
        for cp in cps:
            cp.wait()

    def shape(g, axis):
        return g.shape if axis is None else tuple(d // 2 if k == axis else d for k, d in enumerate(g.shape))

    return _sequencer_call(name, body, [jax.ShapeDtypeStruct(shape(g, ax), g.dtype) for g, ax in zip(srcs, axes)],
                           (n,), collective_id, srcs)


def exchange_chips(name, s1s, collective_id):
    n = len(s1s)

    def body(*refs):
        ins, outs = refs[:n], refs[n:2 * n]
        send_sems, recv_sems = refs[2 * n:]
        x, y, c, sibling, chips = _mesh_places()
        _handshake([(*chip, c) for chip in chips])
        cps = []
        for a in range(n):
            for k, chip in enumerate(chips):
                cps.append(_remote(ins[a].at[2 * chip[0] + chip[1]], outs[a].at[k], send_sems.at[a, k],
                                   recv_sems.at[a, k], (*chip, c)))
        for cp in cps:
            cp.start()
        for cp in cps:
            cp.wait()

    return _sequencer_call(name, body, [jax.ShapeDtypeStruct((3,) + s.shape[1:], s.dtype) for s in s1s], (n, 3),
                           collective_id, s1s)


def allgather_small(v):
    m_per = v.shape[0]

    def body(x_ref, out_ref, send_sems, recv_sems, local_sem):
        x, y, c, sibling, chips = _mesh_places()
        me = (x, y, c)

        def rows(px, py, pc):
            return out_ref.at[pl.ds((4 * px + 2 * py + pc) * m_per, m_per), :]

        def copy(k, block, to, src=None):
            return _remote(rows(*block) if src is None else src, rows(*block), send_sems.at[k], recv_sems.at[k], to)

        mine = pltpu.make_async_copy(x_ref, rows(*me), local_sem)
        mine.start()
        first = [copy(0, me, sibling, src=x_ref)]
        first += [copy(1 + j, me, (*chip, c), src=x_ref) for j, chip in enumerate(chips)]
        for cp in first:
            cp.start()
        passed = [copy(4 + j, (*chip, c), sibling) for j, chip in enumerate(chips)]
        for j, chip in enumerate(chips):
            copy(1 + j, (*chip, c), me).wait_recv()
            passed[j].start()
        copy(0, sibling, me).wait_recv()
        for j, chip in enumerate(chips):
            copy(4 + j, (*chip, 1 - c), me).wait_recv()
        for cp in first + passed:
            cp.wait_send()
        mine.wait()

    return pl.pallas_call(
        body, out_shape=jax.ShapeDtypeStruct((8 * m_per, v.shape[1]), v.dtype),
        in_specs=[pl.BlockSpec(memory_space=pltpu.VMEM)], out_specs=pl.BlockSpec(memory_space=pltpu.VMEM),
        scratch_shapes=[pltpu.SemaphoreType.DMA((7,)), pltpu.SemaphoreType.DMA((7,)), pltpu.SemaphoreType.DMA],
        name="allgather_small")(v)


def _lower_bounds(lb_param):
    lbs = jax.nn.softmax(lb_param.astype(F32), axis=0)
    return jnp.cumsum(lbs, axis=0) - lbs[0]


def _even_fwd(x, i, W, lower, kv, slopes, T):
    O = EVEN_OFF
    g = W["norm_even"][i].reshape(1, D_MODEL)
    h, p = norm_project("mm_in_e", x, g, W["w_in_e"][i])
    kvp = jnp.pad(p[:, O["kA"]:O["kA"] + 2 * W_KV_A], ((BLOCK, BLOCK), (0, 0)))
    sink = jnp.repeat(W["sink"][i], BLOCK).reshape(N_Q_A * BLOCK, 1)
    a = attn_fwd(p, O["qA"], kvp, sink, slopes, T)
    scan_raws = [[((p, O["qB"]), W_B), ((p, O[z]), W_B), ((p, O["iB"]), W_B)] for z in ("zf", "zb")]
    scan_pars = [[lower[i][0:1]], [lower[i][1:2]]]
    o_f, o_b, ss_f, ss_b = scan_fwd("scan_fwd_h", hgrn_prep, scan_raws, scan_pars, N_HEADS_B, HEAD_DIM_B, HEAD_DIM_B, T)
    mo = mem_fwd(p, O["qM"], kv, T)
    hg = W["hgrn_norm"][i].reshape(1, W_B)
    post_ins = [("row", a, 0, W_A), ("row", o_f, 0, W_B), ("row", o_b, 0, W_B), ("row", mo, 0, W_M),
                ("row", p, O["gA"], W_A), ("row", p, O["gB"], W_B), ("row", p, O["gM"], W_M), ("full", hg)]
    x_new = mix_project("even_out", even_post_tile, T, post_ins, W["w_out_e"][i], x)
    return x_new, dict(x=x, g=g, h=h, p=p, kvp=kvp, sink=sink, scan_raws=scan_raws, scan_pars=scan_pars,
                       ss=(ss_f, ss_b), post_ins=post_ins)


def _add2(a, b):
    return a.astype(F32) + b.astype(F32)


def _assemble_even(dqA, dgA, dqB_f, dqB_b, dzf, dzb, diB_f, diB_b, dgB, dqM, dgM, dkvA):
    parts = [dqA, dgA, _add2(dqB_f, dqB_b), dzf, dzb, _add2(diB_f, diB_b), dgB, dqM, dgM, dkvA]
    return (jnp.concatenate([t.astype(BF16) for t in parts], axis=-1),)


def _even_bwd(dxo, sv, i, W, kv, slopes, T, sync):
    O = EVEN_OFF
    p = sv["p"]
    da, dof, dmo, dgA, dgB, dgM, dhg, dwo = mix_project_bwd("even_out_bwd", even_post_tile, T, sv["post_ins"],
                                                            W["w_out_e"][i], dxo, skip=(2,), narrow=(4, 5, 6))
    da = sync(da)
    dqA, dkvp, dsink = attn_bwd(p, O["qA"], sv["kvp"], sv["sink"], slopes, da, T)
    dkvA = dkvp[BLOCK:-BLOCK]
    dqB_f, dzf, diB_f, dqB_b, dzb, diB_b, dlow_f, dlow_b = scan_bwd(
        "scan_bwd_h", hgrn_prep, sv["scan_raws"], sv["scan_pars"], sv["ss"], (dof, 0), N_HEADS_B, HEAD_DIM_B, HEAD_DIM_B, T)
    dqB_f = sync(dqB_f)
    row = lambda arr, w: ("row", arr, 0, w)
    dlow = jnp.concatenate([dlow_f, dlow_b], axis=0)
    dqM, dkv = mem_bwd(p, O["qM"], kv, dmo, T)
    pieces = [row(dqA, W_A), row(dgA, W_A), row(dqB_f, W_B), row(dqB_b, W_B), row(dzf, W_B), row(dzb, W_B),
              row(diB_f, W_B), row(diB_b, W_B), row(dgB, W_B), row(dqM, W_M), row(dgM, W_M), row(dkvA, 2 * W_KV_A)]
    dp, dx, dg = norm_project_bwd("mm_in_e_bwd", _assemble_even, pieces, W["w_in_e"][i], sv["x"], sv["g"], dxo)
    dwi = matmul("mm_dwi_e", sv["h"], dp, "tn")
    return dx, dict(w_in=dwi, w_out=dwo, norm=dg[0], sink=dsink.reshape(N_Q_A), low=dlow, hg=dhg[0], kv=dkv)


def _pad_gate_up(w_up):
    z = jnp.zeros((2, 128, WK_C), F32)
    z = z.at[0, 0:GATE_RANK].set(w_up[0])
    return z.at[1, GATE_RANK:2 * GATE_RANK].set(w_up[1])


def _odd_fwd(x, i, W, kv, T):
    O = ODD_OFF
    g = W["norm_odd"][i].reshape(1, D_MODEL)
    h, p = norm_project("mm_in_o", x, g, W["w_in_o"][i])
    wup = _pad_gate_up(W["w_gate_up"][i])
    one_dir = [((p, O["qC"]), WK_C), ((p, O["kC"]), WK_C), ((p, O["vC"]), WV_C), ((p, O["rr"]), 128)]
    scan_raws = [one_dir, one_dir]
    scan_pars = [[wup[d], W["b_gate"][i][d:d + 1]] for d in range(2)]
    o_f, o_b, ss_f, ss_b = scan_fwd("scan_fwd_g", gla_prep, scan_raws, scan_pars, N_HEADS_C, DK_C, DV_C, T)
    mo = mem_fwd(p, O["qM"], kv, T)
    gg = W["gla_norm"][i].reshape(1, WV_C)
    post_ins = [("row", o_f, 0, WV_C), ("row", o_b, 0, WV_C), ("row", mo, 0, W_M),
                ("row", p, O["gC"], WV_C), ("row", p, O["gM"], W_M), ("full", gg)]
    x_new = mix_project("odd_out", odd_post_tile, T, post_ins, W["w_out_o"][i], x)
    return x_new, dict(x=x, g=g, h=h, p=p, scan_raws=scan_raws, scan_pars=scan_pars, ss=(ss_f, ss_b),
                       post_ins=post_ins)


def _assemble_odd(dq0, dq1, dk0, dk1, dv0, dv1, dgC, dqM, dgM, dr0, dr1):
    parts = [_add2(dq0, dq1), _add2(dk0, dk1), _add2(dv0, dv1), dgC, dqM, dgM, _add2(dr0, dr1)]
    return (jnp.concatenate([t.astype(BF16) for t in parts], axis=-1),)


def _odd_bwd(dxo, sv, i, W, kv, T, sync):
    O = ODD_OFF
    p = sv["p"]
    dof, dmo, dgC, dgM, dgg, dwo = mix_project_bwd("odd_out_bwd", odd_post_tile, T, sv["post_ins"], W["w_out_o"][i],
                                                   dxo, skip=(1,), narrow=(3, 4))
    dof = sync(dof)
    dqf, dkf, dvf, dr_f, dqb, dkb, dvb, dr_b, dwup_f, dbg_f, dwup_b, dbg_b = scan_bwd(
        "scan_bwd_g", gla_prep, sv["scan_raws"], sv["scan_pars"], sv["ss"], (dof, 0), N_HEADS_C, DK_C, DV_C, T)
    dqf = sync(dqf)
    row = lambda arr, w: ("row", arr, 0, w)
    dqM, dkv = mem_bwd(p, O["qM"], kv, dmo, T)
    pieces = [row(dqf, WK_C), row(dqb, WK_C), row(dkf, WK_C), row(dkb, WK_C), row(dvf, WV_C), row(dvb, WV_C),
              row(dgC, WV_C), row(dqM, W_M), row(dgM, W_M), row(dr_f, 128), row(dr_b, 128)]
    dp, dx, dg = norm_project_bwd("mm_in_o_bwd", _assemble_odd, pieces, W["w_in_o"][i], sv["x"], sv["g"], dxo)
    dwi = matmul("mm_dwi_o", sv["h"], dp, "tn")
    dw_up = jnp.stack([dwup_f[0:GATE_RANK], dwup_b[GATE_RANK:2 * GATE_RANK]])
    dbg = jnp.concatenate([dbg_f, dbg_b], axis=0)
    return dx, dict(w_in=dwi, w_out=dwo, norm=dg[0], w_up=dw_up, b_gate=dbg, gg=dgg[0], kv=dkv)


def local_step(x, mem, target, W, later=None, on_layer_grads=None, sync=lambda a: a):
    T = x.shape[0]
    slopes = jnp.repeat(2.0 ** (-8.0 * jnp.arange(1, N_Q_A + 1, dtype=F32) / N_Q_A), BLOCK).reshape(N_Q_A * BLOCK, 1)
    lower, lower_vjp = jax.vjp(_lower_bounds, W["lb_param"])
    mem_g = W["mem_norm"].reshape(1, D_MODEL)
    (mem_n,) = rows_call("mem_rms_fwd", rms_tile, N_MEM, [("row", mem, 0, D_MODEL), ("full", mem_g)], [D_MODEL], [BF16])
    kvs, saved = [], []
    for l in range(DEPTH):
        if l == 1 and later is not None:
            x, W = later(x, W)
        kvs.append(matmul("mm_kv", mem_n, W["w_kv"][l], "nn"))
        if l % 2 == 0:
            x, sv = _even_fwd(x, l // 2, W, lower, kvs[l], slopes, T)
        else:
            x, sv = _odd_fwd(x, l // 2, W, kvs[l], T)
        saved.append(sv)
    loss, dx, dgf = final_call(x, W["final_norm"].reshape(1, D_MODEL), target, T)
    per = [None] * DEPTH
    dmem_n = None
    for l in reversed(range(DEPTH)):
        if l % 2 == 0:
            dx, per[l] = _even_bwd(dx, saved[l], l // 2, W, kvs[l], slopes, T, sync)
        else:
            dx, per[l] = _odd_bwd(dx, saved[l], l // 2, W, kvs[l], T, sync)
        per[l]["w_kv"] = matmul("mm_dwkv", mem_n, per[l]["kv"], "tn")
        dmem_n = matmul("mm_dmem", per[l]["kv"], W["w_kv"][l], "nt", add=dmem_n)
        if on_layer_grads is not None:
            dx = on_layer_grads(l, dx, per[l])
    dw_kv = [per[l]["w_kv"] for l in range(DEPTH)]
    (dmem_norm,) = rows_vjp_call("mem_rms_bwd", rms_tile, N_MEM, [("row", mem, 0, D_MODEL), ("full", mem_g)],
                                 [[("row", dmem_n, 0, D_MODEL)]], skip=(0,))
    ev, od = (per[0], per[2]), (per[1], per[3])
    (d_lb,) = lower_vjp(jnp.stack([e["low"] for e in ev]))
    grads = dict(
        w_in_e=jnp.stack([e["w_in"] for e in ev]), w_in_o=jnp.stack([o["w_in"] for o in od]),
        w_out_e=jnp.stack([e["w_out"] for e in ev]), w_out_o=jnp.stack([o["w_out"] for o in od]),
        w_kv=jnp.stack(dw_kv), norm_even=jnp.stack([e["norm"] for e in ev]), sink=jnp.stack([e["sink"] for e in ev]),
        lb_param=d_lb, hgrn_norm=jnp.stack([e["hg"] for e in ev]), norm_odd=jnp.stack([o["norm"] for o in od]),
        w_gate_up=jnp.stack([o["w_up"] for o in od]), b_gate=jnp.stack([o["b_gate"] for o in od]),
        gla_norm=jnp.stack([o["gg"] for o in od]), mem_norm=dmem_norm[0], final_norm=dgf[0])
    return loss, dx, grads


SMALL_SPECS = (("lb_param", (2, 2, 128)), ("norm_odd", (2, 256)), ("w_gate_up", (2, 2, 16, 128)),
               ("b_gate", (2, 2, 128)), ("gla_norm", (2, 256)))
SMALL_ROWS = 80


def _pack_small_local(d):
    return jnp.concatenate([d[n].reshape(-1) for n, _ in SMALL_SPECS]).reshape(SMALL_ROWS, 128)


def _unpack_small_local(b):
    flat, out, o = b.reshape(-1), {}, 0
    for n, shp in SMALL_SPECS:
        sz = int(np.prod(shp))
        out[n] = flat[o:o + sz].reshape(shp)
        o += sz
    return out


def _unpack_small_full(g4):
    per = [_unpack_small_local(g4[j]) for j in range(4)]
    return {n: jnp.concatenate([per[j][n] for j in range(4)], axis=-1) for n, _ in SMALL_SPECS}


def _pack_small_blocks(full):
    blocks = []
    for j in range(4):
        blocks.append(_pack_small_local({n: full[n][..., j * shp[-1]:(j + 1) * shp[-1]] for n, shp in SMALL_SPECS}))
    return jnp.stack(blocks)


def _cols(t, order, off, widths):
    return [t[..., off[n]:off[n] + widths[n]] for n in order]


EVEN_REF_ORDER = ("qA", "kA", "vA", "gA", "qB", "zf", "zb", "iB", "gB", "qM", "gM")
ODD_REF_ORDER = ("qC", "kC", "vC", "gC", "rr", "qM", "gM")


def _layer_weights(l, g_in, g_out, g_kv):
    t = g_in.transpose(1, 0, 2).reshape(D_MODEL, -1)
    if l % 2 == 0:
        w_in = jnp.concatenate(_cols(t, EVEN_ORDER, EVEN_REF_OFF, EVEN_W), axis=-1)
    else:
        w_in = jnp.concatenate(_cols(t, ODD_ORDER, ODD_REF_OFF, ODD_W) + [jnp.zeros((D_MODEL, ODD_PAD - ODD_IN), BF16)],
                               axis=-1)
    return w_in, g_out.reshape(MIX, D_MODEL), g_kv.reshape(D_MODEL, 2 * W_M)


def _layer_grad_blocks(l, gl):
    if l % 2 == 0:
        t = jnp.concatenate(_cols(gl["w_in"], EVEN_REF_ORDER, EVEN_OFF, EVEN_W), axis=-1)
    else:
        t = jnp.concatenate(_cols(gl["w_in"], ODD_REF_ORDER, ODD_OFF, ODD_W), axis=-1)
    b_in = t.reshape(D_MODEL, 4, -1).transpose(1, 2, 0)
    return [b_in, gl["w_out"].reshape(4, MIX // 4, D_MODEL), gl["w_kv"].reshape(4, D_MODEL // 4, 2 * W_M)]


WEIGHT_NAMES = ("norm_even", "w_in_even", "sink", "lb_param", "hgrn_norm", "w_out_even", "norm_odd", "w_in_odd",
                "w_gate_up", "b_gate", "gla_norm", "w_out_odd", "mem_norm", "w_mem_kv", "final_norm")


def kernel(x, mem, norm_even, w_in_even, sink, lb_param, hgrn_norm, w_out_even, norm_odd, w_in_odd, w_gate_up, b_gate, gla_norm, w_out_odd, mem_norm, w_mem_kv, final_norm, loss_target, m_norm_even, m_w_in_even, m_sink, m_lb_param, m_hgrn_norm, m_w_out_even, m_norm_odd, m_w_in_odd, m_w_gate_up, m_b_gate, m_gla_norm, m_w_out_odd, m_mem_norm, m_w_mem_kv, m_final_norm, v_norm_even, v_w_in_even, v_sink, v_lb_param, v_hgrn_norm, v_w_out_even, v_norm_odd, v_w_in_odd, v_w_gate_up, v_b_gate, v_gla_norm, v_w_out_odd, v_mem_norm, v_w_mem_kv, v_final_norm):
    w = dict(zip(WEIGHT_NAMES, (norm_even, w_in_even, sink, lb_param, hgrn_norm, w_out_even, norm_odd, w_in_odd,
                                w_gate_up, b_gate, gla_norm, w_out_odd, mem_norm, w_mem_kv, final_norm)))
    m = dict(zip(WEIGHT_NAMES, (m_norm_even, m_w_in_even, m_sink, m_lb_param, m_hgrn_norm, m_w_out_even, m_norm_odd,
                                m_w_in_odd, m_w_gate_up, m_b_gate, m_gla_norm, m_w_out_odd, m_mem_norm, m_w_mem_kv,
                                m_final_norm)))
    v = dict(zip(WEIGHT_NAMES, (v_norm_even, v_w_in_even, v_sink, v_lb_param, v_hgrn_norm, v_w_out_even, v_norm_odd,
                                v_w_in_odd, v_w_gate_up, v_b_gate, v_gla_norm, v_w_out_odd, v_mem_norm, v_w_mem_kv,
                                v_final_norm)))
    ci = lax.axis_index("c").astype(jnp.int32).reshape(1)
    chip = (2 * lax.axis_index("x") + lax.axis_index("y")).astype(jnp.int32).reshape(1)

    shards = []
    for l in range(DEPTH):
        names = ("w_in_even", "w_out_even") if l % 2 == 0 else ("w_in_odd", "w_out_odd")
        shards.append([w[names[0]][l // 2].astype(BF16), w[names[1]][l // 2].astype(BF16), w_mem_kv[l].astype(BF16)])
    small = _pack_small_local(w)
    own = lambda g, s: lax.dynamic_update_slice(g, s[None], (chip[0], 0, 0))
    first = [own(g, s) for g, s in zip(gather_weights(shards[0], small), shards[0] + [small])]
    later_shards = shards[1] + shards[2] + shards[3]
    later_raw = gather_weights_async(later_shards)
    w0 = _layer_weights(0, *first[0:3])
    W = dict(w_in_e=[w0[0]], w_out_e=[w0[1]], w_kv=[w0[2]])
    W.update(_unpack_small_full(first[3]))
    W.update({n: w[n] for n in ("norm_even", "sink", "hgrn_norm", "mem_norm", "final_norm")})

    def later(x1, W):
        x1, raw = lax.optimization_barrier((x1, list(later_raw)))
        g = [own(a, s) for a, s in zip(raw, later_shards)]
        w1, w2, w3 = (_layer_weights(l, *g[3 * (l - 1):3 * l]) for l in (1, 2, 3))
        W = dict(W)
        W.update(w_in_e=[w0[0], w2[0]], w_in_o=[w1[0], w3[0]], w_out_e=[w0[1], w2[1]], w_out_o=[w1[1], w3[1]],
                 w_kv=[w0[2], w1[2], w2[2], w3[2]])
        return x1, W

    place = jnp.concatenate([chip, ci])

    def start(tag, blocks, wire):
        axes = [2 if b.shape[1] == ODD_IN // 4 else 1 for b in blocks]
        return dict(tag=tag, blocks=blocks, wire=wire, step=0,
                    recv=exchange_siblings(f"rs_siblings_{tag}", blocks, axes, 2))

    def advance(p, a=None):
        tie = (lambda v: (a, v)) if a is None else (lambda v: lax.optimization_barrier((a, v)))
        if p["step"] == 0:
            a, sums = tie(add_sibling(p["blocks"], p["recv"], ci, p["wire"]))
            p["recv3"] = exchange_chips(f"rs_chips_{p['tag']}", sums, 3)
        else:
            a, p["mine"] = tie(add_chips(p["blocks"], p["recv"], p["recv3"], place))
            p["other"] = exchange_siblings(f"rs_final_{p['tag']}", p["mine"], [None] * len(p["mine"]), 4)
        p["step"] += 1
        return a

    pipes, first_layer = [], {}

    def sync(a):
        for p in pipes:
            if p["step"] < 3:
                key = ("recv", "recv3", "other")[p["step"]]
                a, arrived = lax.optimization_barrier((a, list(p[key])))
                p[key] = arrived
                if p["step"] < 2:
                    a = advance(p, a)
                else:
                    p["step"] = 3
        return a

    def on_layer_grads(l, dx, gl):
        dx = sync(dx)
        if l == 0:
            first_layer.update(gl)
        else:
            pipes.append(start(f"l{l}", _layer_grad_blocks(l, gl), [BF16] * 3))
        return dx

    loss_tile, dx, grads = local_step(x[0], mem[0], loss_target[0], W, later, on_layer_grads, sync)
    last = start("l0", _layer_grad_blocks(0, first_layer) + [_pack_small_blocks(grads)], [BF16] * 3 + [F32])
    for p in pipes + [last]:
        while p["step"] < (1 if p is last else 2):
            advance(p)
    by_layer = {int(p["tag"][1:]): p for p in pipes + [last]}
    halves = lambda layers, k: (jnp.stack([by_layer[l]["mine"][k] for l in layers]),
                                jnp.stack([by_layer[l]["other"][k] for l in layers]))
    gl, upd = {}, {}

    pack = jnp.zeros((8, D_MODEL), F32)
    pack = pack.at[0:2].set(grads["norm_even"]).at[2].set(grads["hgrn_norm"].reshape(-1))
    pack = pack.at[3].set(grads["mem_norm"]).at[4].set(grads["final_norm"])
    pack = pack.at[5, 0:16].set(grads["sink"].reshape(-1)).at[5, 16].set(loss_tile[0, 0])
    tot = sum_devices(allgather_small(pack))
    gl.update(norm_even=tot[0:2], hgrn_norm=tot[2].reshape(2, W_B), mem_norm=tot[3], final_norm=tot[4],
              sink=tot[5, 0:16].reshape(2, N_Q_A))
    loss = tot[5, 16]
    for n in ("norm_even", "hgrn_norm", "mem_norm", "final_norm", "sink"):
        upd[n] = adamw_call(w[n], gl[n], m[n], v[n])
    tr_ = lambda a: jnp.swapaxes(a, 1, 2)
    gl["w_in_odd"], *upd["w_in_odd"] = [tr_(o) for o in adamw_halves(
        tr_(w["w_in_odd"]), *halves((1, 3), 0), tr_(m["w_in_odd"]), tr_(v["w_in_odd"]), ci)]
    gl["w_out_odd"], *upd["w_out_odd"] = adamw_halves(w["w_out_odd"], *halves((1, 3), 1), m["w_out_odd"],
                                                      v["w_out_odd"], ci)
    early = [upd[n] for n in sorted(upd)] + [gl["w_in_odd"], gl["w_out_odd"]]
    last["recv3"], early = lax.optimization_barrier((list(last["recv3"]), early))
    for n, res in zip(sorted(upd), early):
        upd[n] = res
    gl["w_in_odd"], gl["w_out_odd"] = early[-2:]
    advance(last)

    big = dict(w_in_even=halves((0, 2), 0), w_out_even=halves((0, 2), 1), w_mem_kv=halves((0, 1, 2, 3), 2))
    s_mine, s_other = last["mine"][3], last["other"][3]
    g_small = jnp.where(ci[0] == 0, jnp.concatenate([s_mine, s_other]), jnp.concatenate([s_other, s_mine]))
    gl.update(_unpack_small_local(g_small))
    for n in WEIGHT_NAMES:
        if n == "w_in_even":
            gl[n], *upd[n] = [tr_(o) for o in adamw_halves(tr_(w[n]), *big[n], tr_(m[n]), tr_(v[n]), ci)]
        elif n in big:
            gl[n], *upd[n] = adamw_halves(w[n], *big[n], m[n], v[n], ci)
        elif n not in upd:
            upd[n] = adamw_call(w[n], gl[n], m[n], v[n])
    return (loss, dx[None], *[gl[n] for n in WEIGHT_NAMES], *[upd[n][0] for n in WEIGHT_NAMES],
            *[upd[n][1] for n in WEIGHT_NAMES], *[upd[n][2] for n in WEIGHT_NAMES])
```

```python
import functools

import numpy as np
import jax
import jax.numpy as jnp
from jax import lax
from jax.experimental import pallas as pl
from jax.experimental.pallas import tpu as pltpu
from jax.experimental.pallas import tpu_sc as plsc

F32 = jnp.float32
BF16 = jnp.bfloat16

D_MODEL = 1024
DEPTH = 4
N_Q_A, N_KV_A, HEAD_DIM_A = 8, 2, 64
W_A, W_KV_A = 512, 128
WINDOW = 128
BLOCK = 128
N_HEADS_B, HEAD_DIM_B, W_B = 4, 128, 512
N_HEADS_C, DK_C, DV_C, WK_C, WV_C = 4, 128, 256, 512, 1024
GATE_RANK = 16
GATE_TEMP = 16.0
N_MEM, N_HEADS_M, HEAD_DIM_M, W_M = 256, 4, 128, 512
EPS = 1e-6
MASK_VALUE = -1e30
MIN_GATE = 1e-30
EVEN_IN, ODD_IN = 4864, 4128
ODD_PAD = 4224
MIX = 1536
ADAM_LR, ADAM_B1, ADAM_B2, ADAM_EPS, ADAM_WD, ADAM_STEP = 0.001, 0.9, 0.999, 1e-08, 0.01, 10

SCAN_CHUNK = 128
SCAN_SUB = 2
SCAN_LEVELS = 7
VMEM_LIMIT = 56 * 1024 * 1024

EVEN_REF_OFF = dict(qA=0, kA=512, vA=640, gA=768, qB=1280, zf=1792, zb=2304, iB=2816, gB=3328, qM=3840, gM=4352)
EVEN_W = dict(qA=512, kA=128, vA=128, gA=512, qB=512, zf=512, zb=512, iB=512, gB=512, qM=512, gM=512)
EVEN_ORDER = ("qA", "gA", "qB", "zf", "zb", "iB", "gB", "qM", "gM", "kA", "vA")
ODD_REF_OFF = dict(qC=0, kC=512, vC=1024, gC=2048, rr=3072, qM=3104, gM=3616)
ODD_W = dict(qC=512, kC=512, vC=1024, gC=1024, rr=32, qM=512, gM=512)
ODD_ORDER = ("qC", "kC", "vC", "gC", "qM", "gM", "rr")


def _offsets(order, widths):
    off, o = {}, 0
    for n in order:
        off[n] = o
        o += widths[n]
    return off


EVEN_OFF = _offsets(EVEN_ORDER, EVEN_W)
ODD_OFF = _offsets(ODD_ORDER, ODD_W)


def _dg(a, b, ca, cb):
    return lax.dot_general(a.astype(BF16), b.astype(BF16), (((ca,), (cb,)), ((), ())),
                           preferred_element_type=F32)


def dot_nn(a, b):
    return _dg(a, b, 1, 0)


def dot_nt(a, b):
    return _dg(a, b, 1, 1)


def dot_tn(a, b):
    return _dg(a, b, 0, 0)


@jax.custom_vjp
def bdot(a, b):
    return dot_nn(a, b)


bdot.defvjp(lambda a, b: (dot_nn(a, b), (a, b)),
            lambda r, g: (dot_nt(g, r[1]), dot_tn(r[0], g)))


@jax.custom_vjp
def bdot_t(a, b):
    return dot_nt(a, b)


bdot_t.defvjp(lambda a, b: (dot_nt(a, b), (a, b)),
              lambda r, g: (dot_nn(g, r[1]), dot_tn(g, r[0])))


@jax.custom_vjp
def bdot_tn(a, b):
    return dot_tn(a, b)


bdot_tn.defvjp(lambda a, b: (dot_tn(a, b), (a, b)),
               lambda r, g: (dot_nt(r[1], g), dot_nn(r[0], g)))


def _split_mm(h, x):
    hi = x.astype(BF16)
    lo = (x - hi.astype(F32)).astype(BF16)
    return (lax.dot_general(h, hi, (((1,), (0,)), ((), ())), preferred_element_type=F32)
            + lax.dot_general(h, lo, (((1,), (0,)), ((), ())), preferred_element_type=F32))


def _sigmoid(z):
    return 1.0 / (1.0 + jnp.exp(-z))


def _silu(z):
    return z * _sigmoid(z)


def _log_sigmoid(z):
    return jnp.minimum(z, 0.0) - jnp.log(1.0 + jnp.exp(-jnp.abs(z)))


def _rms(x, g):
    return x * lax.rsqrt(jnp.mean(x * x, axis=-1, keepdims=True) + EPS) * g


def rms_tile(x, g):
    return (_rms(x, g),)


@functools.partial(jax.custom_vjp, nondiff_argnums=(1, 2))
def split(x, n, axis):
    w = x.shape[axis] // n
    return tuple(lax.slice_in_dim(x, h * w, (h + 1) * w, axis=axis) for h in range(n))


split.defvjp(lambda x, n, axis: (split(x, n, axis), None),
             lambda n, axis, _, cts: (jnp.concatenate(cts, axis=axis),))


def _group_rms(o, g, heads):
    return jnp.concatenate([_rms(oh, gh) for oh, gh in zip(split(o, heads, 1), split(g, heads, 1))], axis=-1)


def even_post_tile(a, o2f, o2b, mo, gA, gB, gM, hg):
    y = _group_rms(o2f + o2b, hg, N_HEADS_B)
    return (jnp.concatenate([a * _silu(gA), y * _silu(gB), mo * _silu(gM)], axis=-1),)


def odd_post_tile(o2f, o2b, mo, gC, gM, gg):
    y = _group_rms(o2f + o2b, gg, N_HEADS_C)
    return (jnp.concatenate([y * _silu(gC), mo * _silu(gM)], axis=-1),)


def hgrn_prep(raw, par):
    qB, z, iB = raw
    (lb,) = par
    f = lb + (1.0 - lb) * _sigmoid(z)
    return _silu(qB), (1.0 - lb) * _sigmoid(-z), iB, jnp.log(jnp.maximum(f, MIN_GATE))


def gla_prep(raw, par):
    qC, kC, vC, r128 = raw
    wup, bg = par
    return qC * (DK_C ** -0.5), kC, vC, _log_sigmoid(bdot(r128, wup) + bg) / GATE_TEMP


def mem_tile(q, k, v):
    s = bdot_t(q, k) * (HEAD_DIM_M ** -0.5)
    m = lax.stop_gradient(jnp.max(s, axis=-1, keepdims=True))
    p = jnp.exp(s - m)
    p = p / jnp.sum(p, axis=-1, keepdims=True)
    return (bdot(p, v),)


ATTN_GROUP = N_Q_A // N_KV_A


def attn_block(q, ks, vs, sink, slope, c, seq):
    rows = ATTN_GROUP * BLOCK
    i = lax.broadcasted_iota(jnp.int32, (rows, 3 * BLOCK), 0) % BLOCK
    j = lax.broadcasted_iota(jnp.int32, (rows, 3 * BLOCK), 1)
    dist = jnp.abs(i - j + BLOCK).astype(F32)
    kpos = (c - 1) * BLOCK + j
    valid = (dist <= WINDOW) & (kpos >= 0) & (kpos < seq)
    s = bdot_t(q, ks) * (HEAD_DIM_A ** -0.5)
    s = jnp.where(valid, s - slope * dist, MASK_VALUE)
    m = lax.stop_gradient(jnp.maximum(jnp.max(s, axis=-1, keepdims=True), sink))
    p = jnp.where(valid, jnp.exp(s - m), 0.0)
    denom = jnp.sum(p, axis=-1, keepdims=True) + jnp.exp(sink - m)
    return bdot(p, vs) / denom


def scan_chunk(q, k, v, e, tot, st, qm, pm):
    C = SCAN_CHUNK
    e = split(e, 2 + SCAN_LEVELS, 0)
    qe = q * jnp.exp(e[0])
    kd = k * jnp.exp(e[1])
    r = lax.broadcasted_iota(jnp.int32, (C, C), 0)
    s = lax.broadcasted_iota(jnp.int32, (C, C), 1)
    a = jnp.where(r == s, jnp.sum(q * k, axis=-1, keepdims=True), 0.0)
    for l in range(SCAN_LEVELS):
        u = jnp.where(qm[l * C:(l + 1) * C] != 0.0, q, k) * jnp.exp(e[2 + l])
        a = a + bdot_t(u, u) * pm[l * C:(l + 1) * C]
    o = bdot_t(qe, st) + bdot(a, v)
    st_new = st * jnp.exp(tot) + bdot_tn(v, kd)
    return o, st_new


def _scan_consts():
    C, L = SCAN_CHUNK, SCAN_LEVELS
    t = np.arange(C)[:, None]
    r = np.arange(C)[None, :]
    blocks = [(r <= t), (r > t)]
    qms, pms = [], []
    for l in range(1, L + 1):
        m = C >> l
        upper_t = (t % (2 * m)) >= m
        upper_r = (r % (2 * m)) >= m
        same_half = (t // m) == (r // m)
        blocks.append(same_half & np.where(upper_t, r <= t, r > t))
        qms.append(np.broadcast_to(upper_t, (C, C)))
        pms.append(((t // (2 * m)) == (r // (2 * m))) & upper_t & ~upper_r)
    hf = np.concatenate(blocks, axis=0).astype(np.float32)
    flip = lambda mat: mat.reshape(-1, C, C)[:, ::-1, ::-1].reshape(-1, C)
    qmf = np.concatenate(qms, axis=0).astype(np.float32)
    pmf = np.concatenate(pms, axis=0).astype(np.float32)
    h = np.stack([hf, flip(hf)])
    ht = np.stack([h[0].T, h[1].T])
    qm = np.stack([qmf, 1.0 - qmf])
    pm = np.stack([pmf, flip(pmf)])
    return h, ht, qm, pm


def _cparams(sem):
    return pltpu.CompilerParams(dimension_semantics=sem, vmem_limit_bytes=VMEM_LIMIT)


def _row_tile(T):
    return min(T, 512)


def _in_spec(spec, tr):
    kind = spec[0]
    if kind == "row":
        _, arr, off, w = spec
        assert off % w == 0
        return arr, pl.BlockSpec((tr, w), functools.partial(lambda i, b: (i, b), b=off // w))
    if kind == "row3":
        _, arr, d, off, w = spec
        assert off % w == 0
        return arr, pl.BlockSpec((None, tr, w), functools.partial(lambda i, d, b: (d, i, b), d=d, b=off // w))
    _, arr = spec
    return arr, pl.BlockSpec(arr.shape, functools.partial(lambda i, n: (0,) * n, n=arr.ndim))


def rows_call(name, tile_fn, T, ins, out_widths, out_dtypes=None, stacks=None):
    tr = _row_tile(T)
    n_in = len(ins)
    out_dtypes = out_dtypes or [F32] * len(out_widths)
    stacks = stacks or [(k,) for k in range(len(out_widths))]

    def body(*refs):
        vals = [r[...] for r in refs[:n_in]]
        outs = tile_fn(*vals)
        for r, members in zip(refs[n_in:], stacks):
            if len(members) == 1:
                r[...] = outs[members[0]].astype(r.dtype)
            else:
                for d, k in enumerate(members):
                    r[d] = outs[k].astype(r.dtype)

    in_specs, args = [], []
    for spec in ins:
        arr, bs = _in_spec(spec, tr)
        args.append(arr)
        in_specs.append(bs)
    out_specs, out_shape = [], []
    for w, dt, members in zip(out_widths, out_dtypes, stacks):
        n = len(members)
        if n == 1:
            out_specs.append(pl.BlockSpec((tr, w), lambda i: (i, 0)))
            out_shape.append(jax.ShapeDtypeStruct((T, w), dt))
        else:
            out_specs.append(pl.BlockSpec((n, tr, w), lambda i: (0, i, 0)))
            out_shape.append(jax.ShapeDtypeStruct((n, T, w), dt))
    return pl.pallas_call(body, out_shape=out_shape, grid=(T // tr,), in_specs=in_specs, out_specs=out_specs,
                          name=name, compiler_params=_cparams(("arbitrary",)))(*args)


def rows_vjp_call(name, tile_fn, T, ins, cts, skip=(), narrow=()):
    tr = _row_tile(T)
    n_in = len(ins)
    n_ct = [len(c) for c in cts]
    want = [k for k in range(n_in) if k not in skip]

    def body(*refs):
        i = pl.program_id(0)
        vals = [r[...] for r in refs[:n_in]]
        ct, pos = [], n_in
        for n in n_ct:
            acc = refs[pos][...]
            for r in refs[pos + 1:pos + n]:
                acc = acc + r[...]
            ct.append(acc)
            pos += n
        _, vjp = jax.vjp(tile_fn, *vals)
        grads = vjp(tuple(ct))
        for r, k in zip(refs[pos:], want):
            if ins[k][0] == "full":
                @pl.when(i == 0)
                def _():
                    r[...] = jnp.zeros_like(r)
                r[...] += grads[k]
            else:
                r[...] = grads[k].astype(r.dtype)

    in_specs, args = [], []
    for spec in list(ins) + [s for c in cts for s in c]:
        arr, bs = _in_spec(spec, tr)
        args.append(arr)
        in_specs.append(bs)
    out_specs, out_shape = [], []
    for k in want:
        if ins[k][0] == "full":
            arr = ins[k][1]
            out_specs.append(pl.BlockSpec(arr.shape, functools.partial(lambda i, n: (0,) * n, n=arr.ndim)))
            out_shape.append(jax.ShapeDtypeStruct(arr.shape, F32))
        else:
            w = ins[k][-1]
            out_specs.append(pl.BlockSpec((tr, w), lambda i: (i, 0)))
            out_shape.append(jax.ShapeDtypeStruct((T, w), BF16 if k in narrow else F32))
    return pl.pallas_call(body, out_shape=out_shape, grid=(T // tr,), in_specs=in_specs, out_specs=out_specs,
                          name=name, compiler_params=_cparams(("arbitrary",)))(*args)


def matmul(name, a, b, mode, add=None, out_dtype=F32):
    if mode == "tn":
        K, M = a.shape
        N = b.shape[1]
        tm = M if M <= 1536 else 512
        tn = N if N <= 1280 else (N // 2 if (N // 2) % 128 == 0 else N)
        tk = min(K, 512)
        grid = (M // tm, N // tn, K // tk)

        def body(a_ref, b_ref, o_ref):
            @pl.when(pl.program_id(2) == 0)
            def _():
                o_ref[...] = jnp.zeros_like(o_ref)
            o_ref[...] += dot_tn(a_ref[...], b_ref[...])

        return pl.pallas_call(
            body, out_shape=jax.ShapeDtypeStruct((M, N), F32), grid=grid,
            in_specs=[pl.BlockSpec((tk, tm), lambda i, j, k: (k, i)), pl.BlockSpec((tk, tn), lambda i, j, k: (k, j))],
            out_specs=pl.BlockSpec((tm, tn), lambda i, j, k: (i, j)), name=name,
            compiler_params=_cparams(("arbitrary", "arbitrary", "arbitrary")))(a, b)

    M, K = a.shape
    N = b.shape[1] if mode == "nn" else b.shape[0]
    tm = min(M, 512)
    tn = N if N <= 1536 else (N // 2 if (N // 2) % 128 == 0 else (N // 3 if (N // 3) % 128 == 0 else N))
    grid = (N // tn, M // tm)
    n_in = 2 + (add is not None)

    def body(*refs):
        a_ref, b_ref = refs[0], refs[1]
        o_ref = refs[n_in]
        acc = dot_nn(a_ref[...], b_ref[...]) if mode == "nn" else dot_nt(a_ref[...], b_ref[...])
        if add is not None:
            acc = acc + refs[2][...]
        o_ref[...] = acc.astype(o_ref.dtype)

    in_specs = [pl.BlockSpec((tm, K), lambda j, i: (i, 0)),
                pl.BlockSpec((K, tn), lambda j, i: (0, j)) if mode == "nn" else pl.BlockSpec((tn, K), lambda j, i: (j, 0))]
    args = [a, b]
    if add is not None:
        in_specs.append(pl.BlockSpec((tm, tn), lambda j, i: (i, j)))
        args.append(add)
    return pl.pallas_call(
        body, out_shape=jax.ShapeDtypeStruct((M, N), out_dtype), grid=grid, in_specs=in_specs,
        out_specs=pl.BlockSpec((tm, tn), lambda j, i: (i, j)), name=name,
        compiler_params=_cparams(("arbitrary", "arbitrary")))(*args)


def norm_project(name, x, g, w):
    T, D = x.shape
    N = w.shape[1]
    tm = min(T, 512)

    def body(x_ref, g_ref, w_ref, h_ref, p_ref):
        h = _rms(x_ref[...], g_ref[...]).astype(BF16)
        h_ref[...] = h
        p_ref[...] = dot_nn(h, w_ref[...])

    return pl.pallas_call(
        body, out_shape=[jax.ShapeDtypeStruct((T, D), BF16), jax.ShapeDtypeStruct((T, N), F32)], grid=(T // tm,),
        in_specs=[pl.BlockSpec((tm, D), lambda i: (i, 0)), pl.BlockSpec((1, D), lambda i: (0, 0)),
                  pl.BlockSpec((D, N), lambda i: (0, 0))],
        out_specs=[pl.BlockSpec((tm, D), lambda i: (i, 0)), pl.BlockSpec((tm, N), lambda i: (i, 0))],
        name=name, compiler_params=_cparams(("arbitrary",)))(x, g, w)


def norm_project_bwd(name, assemble, pieces, w, x, g, dy):
    T, D = x.shape
    N = w.shape[1]
    tm = min(T, 256)
    n_in = len(pieces)

    def body(*refs):
        w_ref, x_ref, g_ref, dy_ref, dp_ref, dx_ref, dg_ref = refs[n_in:]

        @pl.when(pl.program_id(0) == 0)
        def _():
            dg_ref[...] = jnp.zeros_like(dg_ref)

        (dp,) = assemble(*[r[...] for r in refs[:n_in]])
        dp_ref[...] = dp
        _, vjp = jax.vjp(_rms, x_ref[...], g_ref[...])
        dx, dg = vjp(dot_nt(dp, w_ref[...]))
        dx_ref[...] = dx + dy_ref[...]
        dg_ref[...] += dg

    in_specs, args = [], []
    for spec in pieces:
        arr, bs = _in_spec(spec, tm)
        args.append(arr)
        in_specs.append(bs)
    row = pl.BlockSpec((tm, D), lambda i: (i, 0))
    vec = pl.BlockSpec((1, D), lambda i: (0, 0))
    wide = pl.BlockSpec((tm, N), lambda i: (i, 0))
    return pl.pallas_call(
        body,
        out_shape=[jax.ShapeDtypeStruct((T, N), BF16), jax.ShapeDtypeStruct((T, D), F32), jax.ShapeDtypeStruct((1, D), F32)],
        grid=(T // tm,), in_specs=in_specs + [pl.BlockSpec((D, N), lambda i: (0, 0)), row, vec, row],
        out_specs=[wide, row, vec], name=name, compiler_params=_cparams(("arbitrary",)))(*args, w, x, g, dy)


def mix_project(name, tile_fn, T, ins, w, x):
    tr = _row_tile(T)
    n_in = len(ins)
    K, D = w.shape

    def body(*refs):
        w_ref, x_ref, y_ref = refs[n_in:]
        (mix,) = tile_fn(*[r[...] for r in refs[:n_in]])
        y_ref[...] = x_ref[...] + dot_nn(mix, w_ref[...])

    in_specs, args = [], []
    for spec in ins:
        arr, bs = _in_spec(spec, tr)
        args.append(arr)
        in_specs.append(bs)
    row = pl.BlockSpec((tr, D), lambda i: (i, 0))
    return pl.pallas_call(
        body, out_shape=jax.ShapeDtypeStruct((T, D), F32), grid=(T // tr,),
        in_specs=in_specs + [pl.BlockSpec((K, D), lambda i: (0, 0)), row], out_specs=row,
        name=name, compiler_params=_cparams(("arbitrary",)))(*args, w, x)


def mix_project_bwd(name, tile_fn, T, ins, w, dy, skip=(), narrow=()):
    tr = _row_tile(T)
    n_in = len(ins)
    K, D = w.shape
    want = [k for k in range(n_in) if k not in skip]

    def body(*refs):
        w_ref, dy_ref = refs[n_in:n_in + 2]
        outs, dw_ref = refs[n_in + 2:-1], refs[-1]
        first = pl.program_id(0) == 0
        (mix,), vjp = jax.vjp(tile_fn, *[r[...] for r in refs[:n_in]])
        d = dy_ref[...].astype(BF16)
        grads = vjp((dot_nt(d, w_ref[...]),))

        @pl.when(first)
        def _():
            dw_ref[...] = jnp.zeros_like(dw_ref)

        dw_ref[...] += dot_tn(mix, d)
        for r, k in zip(outs, want):
            if ins[k][0] == "full":
                @pl.when(first)
                def _():
                    r[...] = jnp.zeros_like(r)
                r[...] += grads[k]
            else:
                r[...] = grads[k].astype(r.dtype)

    in_specs, args = [], []
    for spec in ins:
        arr, bs = _in_spec(spec, tr)
        args.append(arr)
        in_specs.append(bs)
    out_specs, out_shape = [], []
    for k in want:
        if ins[k][0] == "full":
            arr = ins[k][1]
            out_specs.append(_full_spec(arr))
            out_shape.append(jax.ShapeDtypeStruct(arr.shape, F32))
        else:
            wd = ins[k][-1]
            out_specs.append(pl.BlockSpec((tr, wd), lambda i: (i, 0)))
            out_shape.append(jax.ShapeDtypeStruct((T, wd), BF16 if k in narrow else F32))
    wspec = pl.BlockSpec((K, D), lambda i: (0, 0))
    return pl.pallas_call(
        body, out_shape=out_shape + [jax.ShapeDtypeStruct((K, D), F32)], grid=(T // tr,),
        in_specs=in_specs + [wspec, pl.BlockSpec((tr, D), lambda i: (i, 0))], out_specs=out_specs + [wspec],
        name=name, compiler_params=_cparams(("arbitrary",)))(*args, w, dy)


def _attn_heads(n):
    G = N_Q_A // N_KV_A
    k_sl = pl.ds(n * HEAD_DIM_A, HEAD_DIM_A)
    v_sl = pl.ds(W_KV_A + n * HEAD_DIM_A, HEAD_DIM_A)
    q_sl = [pl.ds((n * G + g) * HEAD_DIM_A, HEAD_DIM_A) for g in range(G)]
    return k_sl, v_sl, q_sl, range(n * G, (n + 1) * G)


def attn_fwd(p, q_off, kvp, sink, slopes, T):
    nb = T // BLOCK
    assert q_off % W_A == 0

    def body(q_ref, kv_ref, sink_ref, slope_ref, o_ref):
        c = pl.program_id(0)
        rows = pl.ds(pl.multiple_of(c * BLOCK, BLOCK), 3 * BLOCK)
        for n in range(N_KV_A):
            k_sl, v_sl, q_sl, heads = _attn_heads(n)
            group = pl.ds(n * ATTN_GROUP * BLOCK, ATTN_GROUP * BLOCK)
            q = jnp.concatenate([q_ref[:, s] for s in q_sl], axis=0)
            o = attn_block(q, kv_ref[rows, k_sl], kv_ref[rows, v_sl], sink_ref[group, :], slope_ref[group, :], c, T)
            for g, s in enumerate(q_sl):
                o_ref[:, s] = o[g * BLOCK:(g + 1) * BLOCK]

    full = lambda a: pl.BlockSpec(a.shape, functools.partial(lambda c, nd: (0,) * nd, nd=a.ndim))
    return pl.pallas_call(
        body, out_shape=jax.ShapeDtypeStruct((T, W_A), F32), grid=(nb,),
        in_specs=[pl.BlockSpec((BLOCK, W_A), lambda c: (c, q_off // W_A)), full(kvp), full(sink), full(slopes)],
        out_specs=pl.BlockSpec((BLOCK, W_A), lambda c: (c, 0)),
        name="attn_fwd", compiler_params=_cparams(("arbitrary",)))(p, kvp, sink, slopes)


def attn_bwd(p, q_off, kvp, sink, slopes, do, T):
    nb = T // BLOCK

    def body(q_ref, kv_ref, sink_ref, slope_ref, do_ref, dq_ref, dkv_ref, dsink_ref):
        c = pl.program_id(0)

        @pl.when(c == 0)
        def _():
            dkv_ref[...] = jnp.zeros_like(dkv_ref)
            dsink_ref[...] = jnp.zeros_like(dsink_ref)

        rows = pl.ds(pl.multiple_of(c * BLOCK, BLOCK), 3 * BLOCK)
        for n in range(N_KV_A):
            k_sl, v_sl, q_sl, heads = _attn_heads(n)
            group = pl.ds(n * ATTN_GROUP * BLOCK, ATTN_GROUP * BLOCK)
            slope = slope_ref[group, :]
            q = jnp.concatenate([q_ref[:, s] for s in q_sl], axis=0)
            do = jnp.concatenate([do_ref[:, s] for s in q_sl], axis=0)
            _, vjp = jax.vjp(lambda q_, kk, vv, sk: attn_block(q_, kk, vv, sk, slope, c, T),
                             q, kv_ref[rows, k_sl], kv_ref[rows, v_sl], sink_ref[group, :])
            dq, dks, dvs, dsk = vjp(do)
            dkv_ref[rows, k_sl] += dks
            dkv_ref[rows, v_sl] += dvs
            for g, (s, h) in enumerate(zip(q_sl, heads)):
                seg = slice(g * BLOCK, (g + 1) * BLOCK)
                dq_ref[:, s] = dq[seg].astype(dq_ref.dtype)
                dsink_ref[h] += jnp.sum(dsk[seg], axis=0, keepdims=True)

    full = lambda a: pl.BlockSpec(a.shape, functools.partial(lambda c, nd: (0,) * nd, nd=a.ndim))
    qspec = pl.BlockSpec((BLOCK, W_A), lambda c: (c, 0))
    return pl.pallas_call(
        body,
        out_shape=[jax.ShapeDtypeStruct((T, W_A), BF16), jax.ShapeDtypeStruct(kvp.shape, F32),
                   jax.ShapeDtypeStruct((N_Q_A, 1, 1), F32)],
        grid=(nb,),
        in_specs=[pl.BlockSpec((BLOCK, W_A), lambda c: (c, q_off // W_A)), full(kvp), full(sink), full(slopes), qspec],
        out_specs=[qspec, full(kvp), pl.BlockSpec((N_Q_A, 1, 1), lambda c: (0, 0, 0))],
        name="attn_bwd", compiler_params=_cparams(("arbitrary",)))(p, kvp, sink, slopes, do)


def mem_fwd(p, q_off, kv, T):
    tr = min(T, 2 * _row_tile(T))
    assert q_off % W_M == 0

    def body(q_ref, kv_ref, o_ref):
        for h in range(N_HEADS_M):
            hs = pl.ds(h * HEAD_DIM_M, HEAD_DIM_M)
            (o,) = mem_tile(q_ref[:, hs], kv_ref[:, hs], kv_ref[:, pl.ds(W_M + h * HEAD_DIM_M, HEAD_DIM_M)])
            o_ref[:, hs] = o

    return pl.pallas_call(
        body, out_shape=jax.ShapeDtypeStruct((T, W_M), F32), grid=(T // tr,),
        in_specs=[pl.BlockSpec((tr, W_M), lambda i: (i, q_off // W_M)), pl.BlockSpec((N_MEM, 2 * W_M), lambda i: (0, 0))],
        out_specs=pl.BlockSpec((tr, W_M), lambda i: (i, 0)),
        name="mem_fwd", compiler_params=_cparams(("arbitrary",)))(p, kv)


def mem_bwd(p, q_off, kv, do, T):
    tr = min(T, 2 * _row_tile(T))

    def body(q_ref, kv_ref, do_ref, dq_ref, dkv_ref):
        @pl.when(pl.program_id(0) == 0)
        def _():
            dkv_ref[...] = jnp.zeros_like(dkv_ref)

        for h in range(N_HEADS_M):
            hs = pl.ds(h * HEAD_DIM_M, HEAD_DIM_M)
            vs = pl.ds(W_M + h * HEAD_DIM_M, HEAD_DIM_M)
            _, vjp = jax.vjp(mem_tile, q_ref[:, hs], kv_ref[:, hs], kv_ref[:, vs])
            dq, dk, dv = vjp((do_ref[:, hs],))
            dq_ref[:, hs] = dq.astype(dq_ref.dtype)
            dkv_ref[:, hs] += dk
            dkv_ref[:, vs] += dv

    kvspec = pl.BlockSpec((N_MEM, 2 * W_M), lambda i: (0, 0))
    return pl.pallas_call(
        body,
        out_shape=[jax.ShapeDtypeStruct((T, W_M), BF16), jax.ShapeDtypeStruct((N_MEM, 2 * W_M), F32)],
        grid=(T // tr,),
        in_specs=[pl.BlockSpec((tr, W_M), lambda i: (i, q_off // W_M)), kvspec, pl.BlockSpec((tr, W_M), lambda i: (i, 0))],
        out_specs=[pl.BlockSpec((tr, W_M), lambda i: (i, 0)), kvspec],
        name="mem_bwd", compiler_params=_cparams(("arbitrary",)))(p, kv, do)


def _scan_const_specs(dk):
    C, L = SCAN_CHUNK, SCAN_LEVELS
    return [pl.BlockSpec((2, (2 + L) * C, C), lambda n: (0, 0, 0)),
            pl.BlockSpec((2, C, (2 + L) * C), lambda n: (0, 0, 0)),
            pl.BlockSpec((2, L * C, dk), lambda n: (0, 0, 0)),
            pl.BlockSpec((2, L * C, C), lambda n: (0, 0, 0))]


def _chunk_spec(src, width, chunk_of):
    arr, sel = src
    if arr.ndim == 2:
        assert sel % width == 0
        return pl.BlockSpec((SCAN_CHUNK * SCAN_SUB, width),
                            functools.partial(lambda n, b: (chunk_of(n), b), b=sel // width))
    return pl.BlockSpec((None, SCAN_CHUNK * SCAN_SUB, width), functools.partial(lambda n, d: (d, chunk_of(n), 0), d=sel))


def _scan_const_args():
    h, ht, qm, pm = _scan_consts()
    return [jnp.asarray(h, BF16), jnp.asarray(ht, BF16), jnp.asarray(qm, F32), jnp.asarray(pm, F32)]


def _full_spec(a):
    return pl.BlockSpec(a.shape, functools.partial(lambda n, nd: (0,) * nd, nd=a.ndim))


def scan_fwd(name, prep, raws, params, heads, dk, dv, T):
    C, S = SCAN_CHUNK, SCAN_SUB
    N = T // (C * S)
    assert dk == C
    Wv = heads * dv
    orders = (lambda n: n, lambda n: N - 1 - n)
    n_raw, n_par = [len(r) for r in raws], [len(p) for p in params]

    def body(*refs):
        pos, raw_refs, par_refs = 0, [], []
        for d in range(2):
            raw_refs.append(refs[pos:pos + n_raw[d]])
            pos += n_raw[d]
        for d in range(2):
            par_refs.append(refs[pos:pos + n_par[d]])
            pos += n_par[d]
        h_ref, ht_ref, qm_ref, pm_ref = refs[pos:pos + 4]
        o_refs, ss_refs, st_ref = refs[pos + 4:pos + 6], refs[pos + 6:pos + 8], refs[pos + 8]

        @pl.when(pl.program_id(0) == 0)
        def _():
            st_ref[...] = jnp.zeros_like(st_ref)

        for d in range(2):
            consts = (qm_ref[d], pm_ref[d])
            pars = [p[...] for p in par_refs[d]]
            for sub in (range(S) if d == 0 else reversed(range(S))):
                rows = pl.ds(sub * C, C)
                q, k, v, g = prep([r[rows, :] for r in raw_refs[d]], pars)
                e = _split_mm(h_ref[d], g)
                tot = jnp.sum(g, axis=0, keepdims=True)
                for h in range(heads):
                    ks, vs = slice(h * dk, (h + 1) * dk), slice(h * dv, (h + 1) * dv)
                    st = st_ref[d, h]
                    ss_refs[d][h, sub] = st
                    o, st_new = scan_chunk(q[:, ks], k[:, ks], v[:, vs], e[:, ks], tot[:, ks], st, *consts)
                    o_refs[d][rows, vs] = o
                    st_ref[d, h] = st_new

    ss_spec = lambda order: pl.BlockSpec((heads, S, dv, dk), lambda n: (0, order(n), 0, 0))
    return pl.pallas_call(
        body,
        out_shape=[jax.ShapeDtypeStruct((T, Wv), F32)] * 2 + [jax.ShapeDtypeStruct((heads, T // C, dv, dk), F32)] * 2,
        grid=(N,),
        in_specs=[_chunk_spec(s, w, orders[d]) for d in range(2) for s, w in raws[d]]
        + [_full_spec(p) for d in range(2) for p in params[d]] + _scan_const_specs(dk),
        out_specs=[pl.BlockSpec((C * S, Wv), lambda n: (orders[0](n), 0)),
                   pl.BlockSpec((C * S, Wv), lambda n: (orders[1](n), 0)), ss_spec(orders[0]), ss_spec(orders[1])],
        scratch_shapes=[pltpu.VMEM((2, heads, dv, dk), F32)],
        name=name, compiler_params=_cparams(("arbitrary",)))(
            *[s[0] for d in range(2) for s, _ in raws[d]], *[p for d in range(2) for p in params[d]], *_scan_const_args())


def scan_bwd(name, prep, raws, params, ss, do, heads, dk, dv, T):
    C, S = SCAN_CHUNK, SCAN_SUB
    N = T // (C * S)
    Wv = heads * dv
    orders = (lambda n: N - 1 - n, lambda n: n)
    n_raw, n_par = [len(r) for r in raws], [len(p) for p in params]

    def body(*refs):
        pos, raw_refs, par_refs, draw_refs, dpar_refs = 0, [], [], [], []
        for group, counts in ((raw_refs, n_raw), (par_refs, n_par)):
            for d in range(2):
                group.append(refs[pos:pos + counts[d]])
                pos += counts[d]
        ss_refs, do_refs = refs[pos:pos + 2], refs[pos + 2:pos + 4]
        h_ref, ht_ref, qm_ref, pm_ref = refs[pos + 4:pos + 8]
        pos += 8
        for group, counts in ((draw_refs, n_raw), (dpar_refs, n_par)):
            for d in range(2):
                group.append(refs[pos:pos + counts[d]])
                pos += counts[d]
        dst_ref = refs[pos]

        @pl.when(pl.program_id(0) == 0)
        def _():
            dst_ref[...] = jnp.zeros_like(dst_ref)
            for d in range(2):
                for r in dpar_refs[d]:
                    r[...] = jnp.zeros_like(r)

        for d in range(2):
            consts = (qm_ref[d], pm_ref[d])
            pars = [p[...] for p in par_refs[d]]
            for sub in (reversed(range(S)) if d == 0 else range(S)):
                rows = pl.ds(sub * C, C)
                (q, k, v, g), prep_vjp = jax.vjp(prep, [r[rows, :] for r in raw_refs[d]], pars)
                e = _split_mm(h_ref[d], g)
                tot = jnp.sum(g, axis=0, keepdims=True)
                dqs, dks, dvs, des, dtots = [], [], [], [], []
                for h in range(heads):
                    ks, vs = slice(h * dk, (h + 1) * dk), slice(h * dv, (h + 1) * dv)
                    _, vjp = jax.vjp(lambda q_, k_, v_, e_, t_, st_: scan_chunk(q_, k_, v_, e_, t_, st_, *consts),
                                     q[:, ks], k[:, ks], v[:, vs], e[:, ks], tot[:, ks], ss_refs[d][h, sub])
                    dq, dk_, dv_, de, dtot, dst = vjp((do_refs[d][rows, vs], dst_ref[d, h]))
                    dst_ref[d, h] = dst
                    for group, val in ((dqs, dq), (dks, dk_), (dvs, dv_), (des, de), (dtots, dtot)):
                        group.append(val)
                cat = lambda parts: jnp.concatenate(parts, axis=-1)
                dg = _split_mm(ht_ref[d], cat(des)) + cat(dtots)
                draws, dpars = prep_vjp((cat(dqs), cat(dks), cat(dvs), dg))
                for r, val in zip(draw_refs[d], draws):
                    r[rows, :] = val.astype(r.dtype)
                for r, val in zip(dpar_refs[d], dpars):
                    r[...] += val

    ss_spec = lambda order: pl.BlockSpec((heads, S, dv, dk), lambda n: (0, order(n), 0, 0))
    row_out = lambda w, order: pl.BlockSpec((C * S, w), lambda n: (order(n), 0))
    return pl.pallas_call(
        body,
        out_shape=[jax.ShapeDtypeStruct((T, w), BF16) for d in range(2) for _, w in raws[d]]
        + [jax.ShapeDtypeStruct(p.shape, F32) for d in range(2) for p in params[d]],
        grid=(N,),
        in_specs=[_chunk_spec(s, w, orders[d]) for d in range(2) for s, w in raws[d]]
        + [_full_spec(p) for d in range(2) for p in params[d]]
        + [ss_spec(orders[0]), ss_spec(orders[1]), _chunk_spec(do, Wv, orders[0]), _chunk_spec(do, Wv, orders[1])]
        + _scan_const_specs(dk),
        out_specs=[row_out(w, orders[d]) for d in range(2) for _, w in raws[d]]
        + [_full_spec(p) for d in range(2) for p in params[d]],
        scratch_shapes=[pltpu.VMEM((2, heads, dv, dk), F32)],
        name=name, compiler_params=_cparams(("arbitrary",)))(
            *[s[0] for d in range(2) for s, _ in raws[d]], *[p for d in range(2) for p in params[d]],
            ss[0], ss[1], do[0], do[0], *_scan_const_args())


def final_call(x, g, target, T):
    tr = _row_tile(T)

    def tile(xv, gv, tv):
        y = _rms(xv, gv)
        err = (y - tv) ** 2
        return jnp.sum(jnp.sum(err, axis=-1, keepdims=True), axis=0, keepdims=True) * (0.5 / D_MODEL)

    def body(x_ref, g_ref, t_ref, loss_ref, dx_ref, dg_ref):
        i = pl.program_id(0)
        tv = t_ref[...]
        lv, vjp = jax.vjp(lambda a, b: tile(a, b, tv), x_ref[...], g_ref[...])
        dx, dg = vjp(jnp.ones((1, 1), F32))
        dx_ref[...] = dx

        @pl.when(i == 0)
        def _():
            loss_ref[...] = jnp.zeros_like(loss_ref)
            dg_ref[...] = jnp.zeros_like(dg_ref)

        loss_ref[...] += jnp.broadcast_to(lv, loss_ref.shape)
        dg_ref[...] += dg

    return pl.pallas_call(
        body,
        out_shape=[jax.ShapeDtypeStruct((8, 128), F32), jax.ShapeDtypeStruct((T, D_MODEL), F32),
                   jax.ShapeDtypeStruct((1, D_MODEL), F32)],
        grid=(T // tr,),
        in_specs=[pl.BlockSpec((tr, D_MODEL), lambda i: (i, 0)), pl.BlockSpec((1, D_MODEL), lambda i: (0, 0)),
                  pl.BlockSpec((tr, D_MODEL), lambda i: (i, 0))],
        out_specs=[pl.BlockSpec((8, 128), lambda i: (0, 0)), pl.BlockSpec((tr, D_MODEL), lambda i: (i, 0)),
                   pl.BlockSpec((1, D_MODEL), lambda i: (0, 0))],
        name="final_loss", compiler_params=_cparams(("arbitrary",)))(x, g, target)


def adamw_call(w, g, m, v):
    shape = w.shape
    c = shape[-1]
    r = int(np.prod(shape[:-1])) if len(shape) > 1 else 1
    tr = r if r <= 256 else 256
    assert r % tr == 0

    def body(w_ref, g_ref, m_ref, v_ref, d_ref, nm_ref, nv_ref):
        gv = g_ref[...]
        nm = ADAM_B1 * m_ref[...] + (1.0 - ADAM_B1) * gv
        nv = ADAM_B2 * v_ref[...] + (1.0 - ADAM_B2) * jnp.square(gv)
        m_hat = nm / (1.0 - ADAM_B1 ** ADAM_STEP)
        v_hat = nv / (1.0 - ADAM_B2 ** ADAM_STEP)
        d_ref[...] = -ADAM_LR * (m_hat / (jnp.sqrt(v_hat) + ADAM_EPS) + ADAM_WD * w_ref[...])
        nm_ref[...] = nm
        nv_ref[...] = nv

    spec = pl.BlockSpec((tr, c), lambda i: (i, 0))
    outs = pl.pallas_call(body, out_shape=[jax.ShapeDtypeStruct((r, c), F32)] * 3, grid=(r // tr,),
                          in_specs=[spec] * 4, out_specs=[spec] * 3, name="adamw",
                          compiler_params=_cparams(("arbitrary",)))(*(t.reshape(r, c) for t in (w, g, m, v)))
    return tuple(o.reshape(shape) for o in outs)


def adamw_halves(w, mine, other, m, v, c):
    L, R, C = w.shape
    by_cols = mine.shape[-1] != C
    if by_cols:
        tile, nbh = (R, C // 2), 1
        full_idx = lambda l, i: (l, 0, i)
    else:
        rh = R // 2
        tr = rh if rh <= 256 else rh // 2
        assert tr % 8 == 0
        tile, nbh = (tr, C), rh // tr
        full_idx = lambda l, i: (l, i, 0)

    def body(c_ref, w_ref, a_ref, b_ref, m_ref, v_ref, g_ref, d_ref, nm_ref, nv_ref):
        is_mine = (pl.program_id(1) // nbh) == c_ref[0]
        gv = jnp.where(is_mine, a_ref[...], b_ref[...])
        nm = ADAM_B1 * m_ref[...] + (1.0 - ADAM_B1) * gv
        nv = ADAM_B2 * v_ref[...] + (1.0 - ADAM_B2) * jnp.square(gv)
        m_hat = nm / (1.0 - ADAM_B1 ** ADAM_STEP)
        v_hat = nv / (1.0 - ADAM_B2 ** ADAM_STEP)
        g_ref[...] = gv
        d_ref[...] = -ADAM_LR * (m_hat / (jnp.sqrt(v_hat) + ADAM_EPS) + ADAM_WD * w_ref[...])
        nm_ref[...] = nm
        nv_ref[...] = nv

    full = pl.BlockSpec((None,) + tile, lambda l, i, c_ref: full_idx(l, i))
    half = pl.BlockSpec((None,) + tile, lambda l, i, c_ref: (l, i % nbh, 0))
    grid_spec = pltpu.PrefetchScalarGridSpec(num_scalar_prefetch=1, grid=(L, 2 * nbh),
                                             in_specs=[full, half, half, full, full], out_specs=[full] * 4)
    return pl.pallas_call(body, out_shape=[jax.ShapeDtypeStruct(w.shape, F32)] * 4, grid_spec=grid_spec,
                          name="adamw_halves", compiler_params=_cparams(("arbitrary", "arbitrary")))(c, w, mine, other, m, v)


def sum_devices(g64):
    def body(x_ref, o_ref):
        acc = x_ref[0:8, :]
        for d in range(1, 8):
            acc = acc + x_ref[8 * d:8 * d + 8, :]
        o_ref[...] = acc

    return pl.pallas_call(body, out_shape=jax.ShapeDtypeStruct((8, D_MODEL), F32), name="sum_devices")(g64)


def _half_tile(rh):
    if rh <= 512:
        return rh
    return next(rh // d for d in range(2, rh) if rh % d == 0 and (rh // d) % 16 == 0 and rh // d <= 512)


def _half_geometry(full_shape, half_shape):
    R, C = full_shape[-2:]
    if half_shape[-1] != C:
        return (R, C // 2), 1, lambda i, c: (0, c)
    tr = _half_tile(R // 2)
    nblk = (R // 2) // tr
    return (tr, C), nblk, lambda i, c: (i + c * nblk, 0)


def _work_items(counts):
    starts = [int(v) for v in np.cumsum([0] + list(counts[:-1]))]
    local = lambda a, s: jnp.clip(s - starts[a], 0, counts[a] - 1)
    return starts, int(sum(counts)), local


def add_sibling(gs, recvs, c, out_dtypes):
    n = len(gs)
    geo = [_half_geometry(g.shape, r.shape) for g, r in zip(gs, recvs)]
    counts = [4 * nblk for _, nblk, _ in geo]
    starts, total, local = _work_items(counts)

    def body(c_ref, *refs):
        s = pl.program_id(0)
        for a in range(n):
            g_ref, r_ref, o_ref = refs[a], refs[n + a], refs[2 * n + a]

            @pl.when((s >= starts[a]) & (s < starts[a] + counts[a]))
            def _():
                o_ref[...] = (g_ref[...] + r_ref[...]).astype(o_ref.dtype)

    def own_idx(s, c_ref, a):
        _, nblk, own = geo[a]
        k = local(a, s)
        return (k // nblk,) + own(k % nblk, c_ref[0])

    def half_idx(s, c_ref, a):
        k = local(a, s)
        return (k // geo[a][1], k % geo[a][1], 0)

    halves = [pl.BlockSpec((None,) + geo[a][0], functools.partial(half_idx, a=a)) for a in range(n)]
    grid_spec = pltpu.PrefetchScalarGridSpec(
        num_scalar_prefetch=1, grid=(total,),
        in_specs=[pl.BlockSpec((None,) + geo[a][0], functools.partial(own_idx, a=a)) for a in range(n)] + halves,
        out_specs=halves)
    return pl.pallas_call(body, out_shape=[jax.ShapeDtypeStruct(r.shape, dt) for r, dt in zip(recvs, out_dtypes)],
                          grid_spec=grid_spec, name="rs_add_sibling",
                          compiler_params=_cparams(("arbitrary",)))(c, *gs, *recvs)


def add_chips(gs, recvs, r3s, place):
    n = len(gs)
    geo = [_half_geometry(g.shape, r.shape) for g, r in zip(gs, recvs)]
    counts = [nblk for _, nblk, _ in geo]
    starts, total, local = _work_items(counts)

    def body(p_ref, *refs):
        s = pl.program_id(0)
        up = lambda r: r[...].astype(F32)
        for a in range(n):
            g_ref, s_ref, o_ref = refs[a], refs[n + a], refs[5 * n + a]
            a_ref, b_ref, c_ref = refs[2 * n + 3 * a:2 * n + 3 * a + 3]

            @pl.when((s >= starts[a]) & (s < starts[a] + counts[a]))
            def _():
                o_ref[...] = (((g_ref[...] + up(s_ref)) + up(a_ref)) + up(b_ref)) + up(c_ref)

    own_idx = lambda s, p_ref, a: (p_ref[0],) + geo[a][2](local(a, s), p_ref[1])
    sib_idx = lambda s, p_ref, a: (p_ref[0], local(a, s), 0)
    chip_idx = lambda s, p_ref, a, k: (k, local(a, s), 0)
    spec = lambda a, idx, **kw: pl.BlockSpec((None,) + geo[a][0], functools.partial(idx, a=a, **kw))
    grid_spec = pltpu.PrefetchScalarGridSpec(
        num_scalar_prefetch=1, grid=(total,),
        in_specs=[spec(a, own_idx) for a in range(n)] + [spec(a, sib_idx) for a in range(n)]
        + [spec(a, chip_idx, k=k) for a in range(n) for k in range(3)],
        out_specs=[pl.BlockSpec(geo[a][0], functools.partial(lambda s, p_ref, a: (local(a, s), 0), a=a))
                   for a in range(n)])
    return pl.pallas_call(body, out_shape=[jax.ShapeDtypeStruct(r.shape[1:], F32) for r in recvs],
                          grid_spec=grid_spec, name="rs_add_chips", compiler_params=_cparams(("arbitrary",)))(
                              place, *gs, *recvs, *[r for r3 in r3s for r in (r3, r3, r3)])


def _remote(src, dst, ssem, rsem, dev):
    return pltpu.make_async_remote_copy(src_ref=src, dst_ref=dst, send_sem=ssem, recv_sem=rsem,
                                        device_id=dev, device_id_type=pl.DeviceIdType.MESH)


def _mesh_places():
    x, y, c = lax.axis_index("x"), lax.axis_index("y"), lax.axis_index("c")
    chips = [(1 - x, y), (x, 1 - y), (1 - x, 1 - y)]
    return x, y, c, (x, y, 1 - c), chips


def _hbm_specs(n):
    return [pl.BlockSpec(memory_space=pltpu.HBM) for _ in range(n)]


def _gather_body(ins, outs, n_split, send_sems, recv_sems, handshake):
    x, y, c, sibling, chips = _mesh_places()
    mine = 2 * x + y
    if handshake:
        barrier = pltpu.get_barrier_semaphore()
        peers = [sibling] + [(*chip, c) for chip in chips]
        for peer in peers:
            pl.semaphore_signal(barrier, inc=1, device_id=peer, device_id_type=pl.DeviceIdType.MESH)
        pl.semaphore_wait(barrier, len(peers))

    def half(a, chip_idx, which):
        rh = ins[a].shape[0] // 2
        return outs[a].at[chip_idx, pl.ds(which * rh, rh), :]

    sent = []
    for a in range(len(ins)):
        for k, chip in enumerate(chips):
            if a < n_split:
                rh = ins[a].shape[0] // 2
                src, dst = ins[a].at[pl.ds(c * rh, rh), :], half(a, mine, c)
            else:
                src, dst = ins[a], outs[a].at[mine]
            sent.append(_remote(src, dst, send_sems.at[a, k], recv_sems.at[a, k], (*chip, c)))
    for cp in sent:
        cp.start()
    for a in range(len(ins)):
        for k, chip in enumerate(chips):
            j = 2 * chip[0] + chip[1]
            region = half(a, j, c) if a < n_split else outs[a].at[j]
            _remote(region, region, send_sems.at[a, k], recv_sems.at[a, k], (*chip, c)).wait_recv()
            if a < n_split:
                fwd = _remote(region, region, send_sems.at[a, 3 + k], recv_sems.at[a, 3 + k], sibling)
                fwd.start()
                sent.append(fwd)
    for a in range(n_split):
        for k, chip in enumerate(chips):
            region = half(a, 2 * chip[0] + chip[1], 1 - c)
            _remote(region, region, send_sems.at[a, 3 + k], recv_sems.at[a, 3 + k], sibling).wait_recv()
    for cp in sent:
        cp.wait_send()


def gather_weights(shards, small):
    arrs = list(shards) + [small]
    n = len(arrs)

    def body(*refs):
        _gather_body(refs[:n], refs[n:2 * n], n - 1, refs[2 * n], refs[2 * n + 1], handshake=False)

    return pl.pallas_call(
        body, out_shape=[jax.ShapeDtypeStruct((4,) + a.shape, a.dtype) for a in arrs],
        in_specs=_hbm_specs(n), out_specs=_hbm_specs(n),
        scratch_shapes=[pltpu.SemaphoreType.DMA((n, 6)), pltpu.SemaphoreType.DMA((n, 6))],
        name="gather_weights")(*arrs)


def gather_weights_async(shards):
    n = len(shards)

    def body(*refs):
        _gather_body(refs[:n], refs[n:2 * n], n, refs[2 * n], refs[2 * n + 1], handshake=True)

    return pl.kernel(
        body, out_type=[jax.ShapeDtypeStruct((4,) + a.shape, a.dtype) for a in shards],
        mesh=plsc.ScalarSubcoreMesh(axis_name="seq", num_cores=1),
        scratch_types=[pltpu.SemaphoreType.DMA((n, 6)), pltpu.SemaphoreType.DMA((n, 6))],
        compiler_params=pltpu.CompilerParams(collective_id=1), name="gather_weights_async")(*shards)


def _sequencer_call(name, body, out_type, sem_shape, collective_id, args):
    return pl.kernel(
        body, out_type=out_type, mesh=plsc.ScalarSubcoreMesh(axis_name="seq", num_cores=1),
        scratch_types=[pltpu.SemaphoreType.DMA(sem_shape), pltpu.SemaphoreType.DMA(sem_shape)],
        compiler_params=pltpu.CompilerParams(collective_id=collective_id), name=name)(*args)


def _handshake(peers):
    barrier = pltpu.get_barrier_semaphore()
    for peer in peers:
        pl.semaphore_signal(barrier, inc=1, device_id=peer, device_id_type=pl.DeviceIdType.MESH)
    pl.semaphore_wait(barrier, len(peers))


def exchange_siblings(name, srcs, axes, collective_id):
    n = len(srcs)

    def body(*refs):
        ins, outs = refs[:n], refs[n:2 * n]
        send_sems, recv_sems = refs[2 * n:]
        x, y, c, sibling, chips = _mesh_places()
        _handshake([sibling])
        cps = []
        for a in range(n):
            src = ins[a]
            if axes[a] is not None:
                half = src.shape[axes[a]] // 2
                theirs = pl.ds((1 - c) * half, half)
                src = src.at[:, theirs, :] if axes[a] == 1 else src.at[:, :, theirs]
            cps.append(_remote(src, outs[a], send_sems.at[a], recv_sems.at[a], sibling))
        for cp in cps:
            cp.start()
        for cp in cps:
            cp.wait()

    def shape(g, axis):
        return g.shape if axis is None else tuple(d // 2 if k == axis else d for k, d in enumerate(g.shape))

    return _sequencer_call(name, body, [jax.ShapeDtypeStruct(shape(g, ax), g.dtype) for g, ax in zip(srcs, axes)],
                           (n,), collective_id, srcs)


def exchange_chips(name, s1s, collective_id):
    n = len(s1s)

    def body(*refs):
        ins, outs = refs[:n], refs[n:2 * n]
        send_sems, recv_sems = refs[2 * n:]
        x, y, c, sibling, chips = _mesh_places()
        _handshake([(*chip, c) for chip in chips])
        cps = []
        for a in range(n):
            for k, chip in enumerate(chips):
                cps.append(_remote(ins[a].at[2 * chip[0] + chip[1]], outs[a].at[k], send_sems.at[a, k],
                                   recv_sems.at[a, k], (*chip, c)))
        for cp in cps:
            cp.start()
        for cp in cps:
            cp.wait()

    return _sequencer_call(name, body, [jax.ShapeDtypeStruct((3,) + s.shape[1:], s.dtype) for s in s1s], (n, 3),
                           collective_id, s1s)


def allgather_small(v):
    m_per = v.shape[0]

    def body(x_ref, out_ref, send_sems, recv_sems, local_sem):
        x, y, c, sibling, chips = _mesh_places()
        me = (x, y, c)

        def rows(px, py, pc):
            return out_ref.at[pl.ds((4 * px + 2 * py + pc) * m_per, m_per), :]

        def copy(k, block, to, src=None):
            return _remote(rows(*block) if src is None else src, rows(*block), send_sems.at[k], recv_sems.at[k], to)

        mine = pltpu.make_async_copy(x_ref, rows(*me), local_sem)
        mine.start()
        first = [copy(0, me, sibling, src=x_ref)]
        first += [copy(1 + j, me, (*chip, c), src=x_ref) for j, chip in enumerate(chips)]
        for cp in first:
            cp.start()
        passed = [copy(4 + j, (*chip, c), sibling) for j, chip in enumerate(chips)]
        for j, chip in enumerate(chips):
            copy(1 + j, (*chip, c), me).wait_recv()
            passed[j].start()
        copy(0, sibling, me).wait_recv()
        for j, chip in enumerate(chips):
            copy(4 + j, (*chip, 1 - c), me).wait_recv()
        for cp in first + passed:
            cp.wait_send()
        mine.wait()

    return pl.pallas_call(
        body, out_shape=jax.ShapeDtypeStruct((8 * m_per, v.shape[1]), v.dtype),
        in_specs=[pl.BlockSpec(memory_space=pltpu.VMEM)], out_specs=pl.BlockSpec(memory_space=pltpu.VMEM),
        scratch_shapes=[pltpu.SemaphoreType.DMA((7,)), pltpu.SemaphoreType.DMA((7,)), pltpu.SemaphoreType.DMA],
        name="allgather_small")(v)


def _lower_bounds(lb_param):
    lbs = jax.nn.softmax(lb_param.astype(F32), axis=0)
    return jnp.cumsum(lbs, axis=0) - lbs[0]


def _even_fwd(x, i, W, lower, kv, slopes, T):
    O = EVEN_OFF
    g = W["norm_even"][i].reshape(1, D_MODEL)
    h, p = norm_project("mm_in_e", x, g, W["w_in_e"][i])
    kvp = jnp.pad(p[:, O["kA"]:O["kA"] + 2 * W_KV_A], ((BLOCK, BLOCK), (0, 0)))
    sink = jnp.repeat(W["sink"][i], BLOCK).reshape(N_Q_A * BLOCK, 1)
    a = attn_fwd(p, O["qA"], kvp, sink, slopes, T)
    scan_raws = [[((p, O["qB"]), W_B), ((p, O[z]), W_B), ((p, O["iB"]), W_B)] for z in ("zf", "zb")]
    scan_pars = [[lower[i][0:1]], [lower[i][1:2]]]
    o_f, o_b, ss_f, ss_b = scan_fwd("scan_fwd_h", hgrn_prep, scan_raws, scan_pars, N_HEADS_B, HEAD_DIM_B, HEAD_DIM_B, T)
    mo = mem_fwd(p, O["qM"], kv, T)
    hg = W["hgrn_norm"][i].reshape(1, W_B)
    post_ins = [("row", a, 0, W_A), ("row", o_f, 0, W_B), ("row", o_b, 0, W_B), ("row", mo, 0, W_M),
                ("row", p, O["gA"], W_A), ("row", p, O["gB"], W_B), ("row", p, O["gM"], W_M), ("full", hg)]
    x_new = mix_project("even_out", even_post_tile, T, post_ins, W["w_out_e"][i], x)
    return x_new, dict(x=x, g=g, h=h, p=p, kvp=kvp, sink=sink, scan_raws=scan_raws, scan_pars=scan_pars,
                       ss=(ss_f, ss_b), post_ins=post_ins)


def _add2(a, b):
    return a.astype(F32) + b.astype(F32)


def _assemble_even(dqA, dgA, dqB_f, dqB_b, dzf, dzb, diB_f, diB_b, dgB, dqM, dgM, dkvA):
    parts = [dqA, dgA, _add2(dqB_f, dqB_b), dzf, dzb, _add2(diB_f, diB_b), dgB, dqM, dgM, dkvA]
    return (jnp.concatenate([t.astype(BF16) for t in parts], axis=-1),)


def _even_bwd(dxo, sv, i, W, kv, slopes, T, sync, early):
    O = EVEN_OFF
    p = sv["p"]
    da, dof, dmo, dgA, dgB, dgM, dhg, dwo = mix_project_bwd("even_out_bwd", even_post_tile, T, sv["post_ins"],
                                                            W["w_out_e"][i], dxo, skip=(2,), narrow=(4, 5, 6))
    dqM, dkv = mem_bwd(p, O["qM"], kv, dmo, T)
    da = early(da, dwo, dkv)
    dqA, dkvp, dsink = attn_bwd(p, O["qA"], sv["kvp"], sv["sink"], slopes, da, T)
    dkvA = dkvp[BLOCK:-BLOCK]
    dqB_f, dzf, diB_f, dqB_b, dzb, diB_b, dlow_f, dlow_b = scan_bwd(
        "scan_bwd_h", hgrn_prep, sv["scan_raws"], sv["scan_pars"], sv["ss"], (dof, 0), N_HEADS_B, HEAD_DIM_B, HEAD_DIM_B, T)
    dqB_f = sync(dqB_f)
    row = lambda arr, w: ("row", arr, 0, w)
    dlow = jnp.concatenate([dlow_f, dlow_b], axis=0)
    pieces = [row(dqA, W_A), row(dgA, W_A), row(dqB_f, W_B), row(dqB_b, W_B), row(dzf, W_B), row(dzb, W_B),
              row(diB_f, W_B), row(diB_b, W_B), row(dgB, W_B), row(dqM, W_M), row(dgM, W_M), row(dkvA, 2 * W_KV_A)]
    dp, dx, dg = norm_project_bwd("mm_in_e_bwd", _assemble_even, pieces, W["w_in_e"][i], sv["x"], sv["g"], dxo)
    dwi = matmul("mm_dwi_e", sv["h"], dp, "tn")
    return dx, dict(w_in=dwi, w_out=dwo, norm=dg[0], sink=dsink.reshape(N_Q_A), low=dlow, hg=dhg[0], kv=dkv)


def _pad_gate_up(w_up):
    z = jnp.zeros((2, 128, WK_C), F32)
    z = z.at[0, 0:GATE_RANK].set(w_up[0])
    return z.at[1, GATE_RANK:2 * GATE_RANK].set(w_up[1])


def _odd_fwd(x, i, W, kv, T):
    O = ODD_OFF
    g = W["norm_odd"][i].reshape(1, D_MODEL)
    h, p = norm_project("mm_in_o", x, g, W["w_in_o"][i])
    wup = _pad_gate_up(W["w_gate_up"][i])
    one_dir = [((p, O["qC"]), WK_C), ((p, O["kC"]), WK_C), ((p, O["vC"]), WV_C), ((p, O["rr"]), 128)]
    scan_raws = [one_dir, one_dir]
    scan_pars = [[wup[d], W["b_gate"][i][d:d + 1]] for d in range(2)]
    o_f, o_b, ss_f, ss_b = scan_fwd("scan_fwd_g", gla_prep, scan_raws, scan_pars, N_HEADS_C, DK_C, DV_C, T)
    mo = mem_fwd(p, O["qM"], kv, T)
    gg = W["gla_norm"][i].reshape(1, WV_C)
    post_ins = [("row", o_f, 0, WV_C), ("row", o_b, 0, WV_C), ("row", mo, 0, W_M),
                ("row", p, O["gC"], WV_C), ("row", p, O["gM"], W_M), ("full", gg)]
    x_new = mix_project("odd_out", odd_post_tile, T, post_ins, W["w_out_o"][i], x)
    return x_new, dict(x=x, g=g, h=h, p=p, scan_raws=scan_raws, scan_pars=scan_pars, ss=(ss_f, ss_b),
                       post_ins=post_ins)


def _assemble_odd(dq0, dq1, dk0, dk1, dv0, dv1, dgC, dqM, dgM, dr0, dr1):
    parts = [_add2(dq0, dq1), _add2(dk0, dk1), _add2(dv0, dv1), dgC, dqM, dgM, _add2(dr0, dr1)]
    return (jnp.concatenate([t.astype(BF16) for t in parts], axis=-1),)


def _odd_bwd(dxo, sv, i, W, kv, T, sync, early):
    O = ODD_OFF
    p = sv["p"]
    dof, dmo, dgC, dgM, dgg, dwo = mix_project_bwd("odd_out_bwd", odd_post_tile, T, sv["post_ins"], W["w_out_o"][i],
                                                   dxo, skip=(1,), narrow=(3, 4))
    dqM, dkv = mem_bwd(p, O["qM"], kv, dmo, T)
    dof = early(dof, dwo, dkv)
    dqf, dkf, dvf, dr_f, dqb, dkb, dvb, dr_b, dwup_f, dbg_f, dwup_b, dbg_b = scan_bwd(
        "scan_bwd_g", gla_prep, sv["scan_raws"], sv["scan_pars"], sv["ss"], (dof, 0), N_HEADS_C, DK_C, DV_C, T)
    dqf = sync(dqf)
    row = lambda arr, w: ("row", arr, 0, w)
    pieces = [row(dqf, WK_C), row(dqb, WK_C), row(dkf, WK_C), row(dkb, WK_C), row(dvf, WV_C), row(dvb, WV_C),
              row(dgC, WV_C), row(dqM, W_M), row(dgM, W_M), row(dr_f, 128), row(dr_b, 128)]
    dp, dx, dg = norm_project_bwd("mm_in_o_bwd", _assemble_odd, pieces, W["w_in_o"][i], sv["x"], sv["g"], dxo)
    dwi = matmul("mm_dwi_o", sv["h"], dp, "tn")
    dw_up = jnp.stack([dwup_f[0:GATE_RANK], dwup_b[GATE_RANK:2 * GATE_RANK]])
    dbg = jnp.concatenate([dbg_f, dbg_b], axis=0)
    return dx, dict(w_in=dwi, w_out=dwo, norm=dg[0], w_up=dw_up, b_gate=dbg, gg=dgg[0], kv=dkv)


def local_step(x, mem, target, W, later=None, on_layer_grads=None, sync=lambda a: a, on_early_grads=None):
    T = x.shape[0]
    slopes = jnp.repeat(2.0 ** (-8.0 * jnp.arange(1, N_Q_A + 1, dtype=F32) / N_Q_A), BLOCK).reshape(N_Q_A * BLOCK, 1)
    lower, lower_vjp = jax.vjp(_lower_bounds, W["lb_param"])
    mem_g = W["mem_norm"].reshape(1, D_MODEL)
    (mem_n,) = rows_call("mem_rms_fwd", rms_tile, N_MEM, [("row", mem, 0, D_MODEL), ("full", mem_g)], [D_MODEL], [BF16])
    kvs, saved = [], []
    for l in range(DEPTH):
        if l == 1 and later is not None:
            x, W = later(x, W)
        kvs.append(matmul("mm_kv", mem_n, W["w_kv"][l], "nn"))
        if l % 2 == 0:
            x, sv = _even_fwd(x, l // 2, W, lower, kvs[l], slopes, T)
        else:
            x, sv = _odd_fwd(x, l // 2, W, kvs[l], T)
        saved.append(sv)
    loss, dx, dgf = final_call(x, W["final_norm"].reshape(1, D_MODEL), target, T)
    per = [None] * DEPTH
    dmem_n = None
    early_kv = {}

    def early_of(l):
        def early(a, dwo, dkv):
            early_kv[l] = matmul("mm_dwkv", mem_n, dkv, "tn")
            return sync(a) if on_early_grads is None else on_early_grads(l, a, dwo, early_kv[l])
        return early

    for l in reversed(range(DEPTH)):
        if l % 2 == 0:
            dx, per[l] = _even_bwd(dx, saved[l], l // 2, W, kvs[l], slopes, T, sync, early_of(l))
        else:
            dx, per[l] = _odd_bwd(dx, saved[l], l // 2, W, kvs[l], T, sync, early_of(l))
        per[l]["w_kv"] = early_kv[l]
        dmem_n = matmul("mm_dmem", per[l]["kv"], W["w_kv"][l], "nt", add=dmem_n)
        if on_layer_grads is not None:
            dx = on_layer_grads(l, dx, per[l])
    dw_kv = [per[l]["w_kv"] for l in range(DEPTH)]
    (dmem_norm,) = rows_vjp_call("mem_rms_bwd", rms_tile, N_MEM, [("row", mem, 0, D_MODEL), ("full", mem_g)],
                                 [[("row", dmem_n, 0, D_MODEL)]], skip=(0,))
    ev, od = (per[0], per[2]), (per[1], per[3])
    (d_lb,) = lower_vjp(jnp.stack([e["low"] for e in ev]))
    grads = dict(
        w_in_e=jnp.stack([e["w_in"] for e in ev]), w_in_o=jnp.stack([o["w_in"] for o in od]),
        w_out_e=jnp.stack([e["w_out"] for e in ev]), w_out_o=jnp.stack([o["w_out"] for o in od]),
        w_kv=jnp.stack(dw_kv), norm_even=jnp.stack([e["norm"] for e in ev]), sink=jnp.stack([e["sink"] for e in ev]),
        lb_param=d_lb, hgrn_norm=jnp.stack([e["hg"] for e in ev]), norm_odd=jnp.stack([o["norm"] for o in od]),
        w_gate_up=jnp.stack([o["w_up"] for o in od]), b_gate=jnp.stack([o["b_gate"] for o in od]),
        gla_norm=jnp.stack([o["gg"] for o in od]), mem_norm=dmem_norm[0], final_norm=dgf[0])
    return loss, dx, grads


SMALL_SPECS = (("lb_param", (2, 2, 128)), ("norm_odd", (2, 256)), ("w_gate_up", (2, 2, 16, 128)),
               ("b_gate", (2, 2, 128)), ("gla_norm", (2, 256)))
SMALL_ROWS = 80


def _pack_small_local(d):
    return jnp.concatenate([d[n].reshape(-1) for n, _ in SMALL_SPECS]).reshape(SMALL_ROWS, 128)


def _unpack_small_local(b):
    flat, out, o = b.reshape(-1), {}, 0
    for n, shp in SMALL_SPECS:
        sz = int(np.prod(shp))
        out[n] = flat[o:o + sz].reshape(shp)
        o += sz
    return out


def _unpack_small_full(g4):
    per = [_unpack_small_local(g4[j]) for j in range(4)]
    return {n: jnp.concatenate([per[j][n] for j in range(4)], axis=-1) for n, _ in SMALL_SPECS}


def _pack_small_blocks(full):
    blocks = []
    for j in range(4):
        blocks.append(_pack_small_local({n: full[n][..., j * shp[-1]:(j + 1) * shp[-1]] for n, shp in SMALL_SPECS}))
    return jnp.stack(blocks)


def _cols(t, order, off, widths):
    return [t[..., off[n]:off[n] + widths[n]] for n in order]


EVEN_REF_ORDER = ("qA", "kA", "vA", "gA", "qB", "zf", "zb", "iB", "gB", "qM", "gM")
ODD_REF_ORDER = ("qC", "kC", "vC", "gC", "rr", "qM", "gM")


def _layer_weights(l, g_in, g_out, g_kv):
    t = g_in.transpose(1, 0, 2).reshape(D_MODEL, -1)
    if l % 2 == 0:
        w_in = jnp.concatenate(_cols(t, EVEN_ORDER, EVEN_REF_OFF, EVEN_W), axis=-1)
    else:
        w_in = jnp.concatenate(_cols(t, ODD_ORDER, ODD_REF_OFF, ODD_W) + [jnp.zeros((D_MODEL, ODD_PAD - ODD_IN), BF16)],
                               axis=-1)
    return w_in, g_out.reshape(MIX, D_MODEL), g_kv.reshape(D_MODEL, 2 * W_M)


def _layer_grad_blocks(l, gl):
    if l % 2 == 0:
        t = jnp.concatenate(_cols(gl["w_in"], EVEN_REF_ORDER, EVEN_OFF, EVEN_W), axis=-1)
    else:
        t = jnp.concatenate(_cols(gl["w_in"], ODD_REF_ORDER, ODD_OFF, ODD_W), axis=-1)
    b_in = t.reshape(D_MODEL, 4, -1).transpose(1, 2, 0)
    return [b_in, gl["w_out"].reshape(4, MIX // 4, D_MODEL), gl["w_kv"].reshape(4, D_MODEL // 4, 2 * W_M)]


WEIGHT_NAMES = ("norm_even", "w_in_even", "sink", "lb_param", "hgrn_norm", "w_out_even", "norm_odd", "w_in_odd",
                "w_gate_up", "b_gate", "gla_norm", "w_out_odd", "mem_norm", "w_mem_kv", "final_norm")


def kernel(x, mem, norm_even, w_in_even, sink, lb_param, hgrn_norm, w_out_even, norm_odd, w_in_odd, w_gate_up, b_gate, gla_norm, w_out_odd, mem_norm, w_mem_kv, final_norm, loss_target, m_norm_even, m_w_in_even, m_sink, m_lb_param, m_hgrn_norm, m_w_out_even, m_norm_odd, m_w_in_odd, m_w_gate_up, m_b_gate, m_gla_norm, m_w_out_odd, m_mem_norm, m_w_mem_kv, m_final_norm, v_norm_even, v_w_in_even, v_sink, v_lb_param, v_hgrn_norm, v_w_out_even, v_norm_odd, v_w_in_odd, v_w_gate_up, v_b_gate, v_gla_norm, v_w_out_odd, v_mem_norm, v_w_mem_kv, v_final_norm):
    w = dict(zip(WEIGHT_NAMES, (norm_even, w_in_even, sink, lb_param, hgrn_norm, w_out_even, norm_odd, w_in_odd,
                                w_gate_up, b_gate, gla_norm, w_out_odd, mem_norm, w_mem_kv, final_norm)))
    m = dict(zip(WEIGHT_NAMES, (m_norm_even, m_w_in_even, m_sink, m_lb_param, m_hgrn_norm, m_w_out_even, m_norm_odd,
                                m_w_in_odd, m_w_gate_up, m_b_gate, m_gla_norm, m_w_out_odd, m_mem_norm, m_w_mem_kv,
                                m_final_norm)))
    v = dict(zip(WEIGHT_NAMES, (v_norm_even, v_w_in_even, v_sink, v_lb_param, v_hgrn_norm, v_w_out_even, v_norm_odd,
                                v_w_in_odd, v_w_gate_up, v_b_gate, v_gla_norm, v_w_out_odd, v_mem_norm, v_w_mem_kv,
                                v_final_norm)))
    ci = lax.axis_index("c").astype(jnp.int32).reshape(1)
    chip = (2 * lax.axis_index("x") + lax.axis_index("y")).astype(jnp.int32).reshape(1)

    shards = []
    for l in range(DEPTH):
        names = ("w_in_even", "w_out_even") if l % 2 == 0 else ("w_in_odd", "w_out_odd")
        shards.append([w[names[0]][l // 2].astype(BF16), w[names[1]][l // 2].astype(BF16), w_mem_kv[l].astype(BF16)])
    small = _pack_small_local(w)
    own = lambda g, s: lax.dynamic_update_slice(g, s[None], (chip[0], 0, 0))
    first = [own(g, s) for g, s in zip(gather_weights(shards[0], small), shards[0] + [small])]
    later_shards = shards[1] + shards[2] + shards[3]
    later_raw = gather_weights_async(later_shards)
    w0 = _layer_weights(0, *first[0:3])
    W = dict(w_in_e=[w0[0]], w_out_e=[w0[1]], w_kv=[w0[2]])
    W.update(_unpack_small_full(first[3]))
    W.update({n: w[n] for n in ("norm_even", "sink", "hgrn_norm", "mem_norm", "final_norm")})

    def later(x1, W):
        x1, raw = lax.optimization_barrier((x1, list(later_raw)))
        g = [own(a, s) for a, s in zip(raw, later_shards)]
        w1, w2, w3 = (_layer_weights(l, *g[3 * (l - 1):3 * l]) for l in (1, 2, 3))
        W = dict(W)
        W.update(w_in_e=[w0[0], w2[0]], w_in_o=[w1[0], w3[0]], w_out_e=[w0[1], w2[1]], w_out_o=[w1[1], w3[1]],
                 w_kv=[w0[2], w1[2], w2[2], w3[2]])
        return x1, W

    place = jnp.concatenate([chip, ci])

    def start(tag, blocks, wire):
        axes = [2 if b.shape[1] == ODD_IN // 4 else 1 for b in blocks]
        return dict(tag=tag, blocks=blocks, wire=wire, step=0,
                    recv=exchange_siblings(f"rs_siblings_{tag}", blocks, axes, 2))

    def advance(p, a=None):
        tie = (lambda v: (a, v)) if a is None else (lambda v: lax.optimization_barrier((a, v)))
        if p["step"] == 0:
            a, sums = tie(add_sibling(p["blocks"], p["recv"], ci, p["wire"]))
            p["recv3"] = exchange_chips(f"rs_chips_{p['tag']}", sums, 3)
        else:
            a, p["mine"] = tie(add_chips(p["blocks"], p["recv"], p["recv3"], place))
            p["other"] = exchange_siblings(f"rs_final_{p['tag']}", p["mine"], [None] * len(p["mine"]), 4)
        p["step"] += 1
        return a

    pipes, first_layer = [], {}

    def sync(a):
        for p in pipes:
            if p["step"] < 3:
                key = ("recv", "recv3", "other")[p["step"]]
                a, arrived = lax.optimization_barrier((a, list(p[key])))
                p[key] = arrived
                if p["step"] < 2:
                    a = advance(p, a)
                else:
                    p["step"] = 3
        return a

    def on_early_grads(l, a, dwo, dwkv):
        a = sync(a)
        blocks = [dwo.reshape(4, MIX // 4, D_MODEL), dwkv.reshape(4, D_MODEL // 4, 2 * W_M)]
        a, blocks = lax.optimization_barrier((a, blocks))
        pipes.append(start(f"l{l}a", blocks, [BF16] * 2))
        return a

    def on_layer_grads(l, dx, gl):
        dx = sync(dx)
        b_in = _layer_grad_blocks(l, gl)[0]
        if l == 0:
            first_layer["b_in"] = b_in
        else:
            pipes.append(start(f"l{l}b", [b_in], [BF16]))
        return dx

    loss_tile, dx, grads = local_step(x[0], mem[0], loss_target[0], W, later, on_layer_grads, sync, on_early_grads)
    last = start("l0b", [first_layer["b_in"], _pack_small_blocks(grads)], [BF16, F32])
    for p in pipes + [last]:
        while p["step"] < (1 if p is last else 2):
            advance(p)
    by_tag = {p["tag"]: p for p in pipes + [last]}
    halves = lambda layers, part, k: (jnp.stack([by_tag[f"l{l}{part}"]["mine"][k] for l in layers]),
                                      jnp.stack([by_tag[f"l{l}{part}"]["other"][k] for l in layers]))
    gl, upd = {}, {}

    pack = jnp.zeros((8, D_MODEL), F32)
    pack = pack.at[0:2].set(grads["norm_even"]).at[2].set(grads["hgrn_norm"].reshape(-1))
    pack = pack.at[3].set(grads["mem_norm"]).at[4].set(grads["final_norm"])
    pack = pack.at[5, 0:16].set(grads["sink"].reshape(-1)).at[5, 16].set(loss_tile[0, 0])
    tot = sum_devices(allgather_small(pack))
    gl.update(norm_even=tot[0:2], hgrn_norm=tot[2].reshape(2, W_B), mem_norm=tot[3], final_norm=tot[4],
              sink=tot[5, 0:16].reshape(2, N_Q_A))
    loss = tot[5, 16]
    for n in ("norm_even", "hgrn_norm", "mem_norm", "final_norm", "sink"):
        upd[n] = adamw_call(w[n], gl[n], m[n], v[n])
    tr_ = lambda a: jnp.swapaxes(a, 1, 2)
    gl["w_in_odd"], *upd["w_in_odd"] = [tr_(o) for o in adamw_halves(
        tr_(w["w_in_odd"]), *halves((1, 3), "b", 0), tr_(m["w_in_odd"]), tr_(v["w_in_odd"]), ci)]
    gl["w_out_odd"], *upd["w_out_odd"] = adamw_halves(w["w_out_odd"], *halves((1, 3), "a", 0), m["w_out_odd"],
                                                      v["w_out_odd"], ci)
    early = [upd[n] for n in sorted(upd)] + [gl["w_in_odd"], gl["w_out_odd"]]
    last["recv3"], early = lax.optimization_barrier((list(last["recv3"]), early))
    for n, res in zip(sorted(upd), early):
        upd[n] = res
    gl["w_in_odd"], gl["w_out_odd"] = early[-2:]
    advance(last)

    big = dict(w_in_even=halves((0, 2), "b", 0), w_out_even=halves((0, 2), "a", 0),
               w_mem_kv=halves((0, 1, 2, 3), "a", 1))
    s_mine, s_other = last["mine"][1], last["other"][1]
    g_small = jnp.where(ci[0] == 0, jnp.concatenate([s_mine, s_other]), jnp.concatenate([s_other, s_mine]))
    gl.update(_unpack_small_local(g_small))
    for n in WEIGHT_NAMES:
        if n == "w_in_even":
            gl[n], *upd[n] = [tr_(o) for o in adamw_halves(tr_(w[n]), *big[n], tr_(m[n]), tr_(v[n]), ci)]
        elif n in big:
            gl[n], *upd[n] = adamw_halves(w[n], *big[n], m[n], v[n], ci)
        elif n not in upd:
            upd[n] = adamw_call(w[n], gl[n], m[n], v[n])
    return (loss, dx[None], *[gl[n] for n in WEIGHT_NAMES], *[upd[n][0] for n in WEIGHT_NAMES],
            *[upd[n][1] for n in WEIGHT_NAMES], *[upd[n][2] for n in WEIGHT_NAMES])
```
